```python
import jax, jax.numpy as jnp
from jax import lax
import numpy as np

D_MODEL = 1024
BATCH = 8
SEQ = 2048
DEPTH = 1

HEAD_DIM = 64
FOX_HEADS = 8
DIL_GROUPS = ((128, 1), (512, 4), (2048, 16))
DIL_HEADS_PER_GROUP = 4
DIL_HEADS = DIL_HEADS_PER_GROUP * len(DIL_GROUPS)
D_FF = 4 * D_MODEL
ROPE_THETA = 500000.0
ROPE_DIM = HEAD_DIM // 4
Q_BLOCK = 128
EPS = 1e-6
NEG_INF = -1e30
FOX_W = FOX_HEADS * HEAD_DIM
DIL_W = DIL_HEADS * HEAD_DIM
DIL_OUT_W = DIL_HEADS_PER_GROUP * HEAD_DIM
IN_SPLIT_SIZES = (FOX_W, FOX_W, FOX_W, FOX_HEADS, DIL_W, DIL_W, DIL_W, D_MODEL, D_MODEL)
D_IN = sum(IN_SPLIT_SIZES)

kernel_name = 'hybrid_fox_dilated_gated_block'


def _rms_norm(x, g):
    xf = x.astype(jnp.float32)
    y = xf * lax.rsqrt(jnp.mean(xf * xf, axis=-1, keepdims=True) + EPS)
    return (y * g.astype(jnp.float32)).astype(x.dtype)


def _partial_rope(x, positions):
    half = ROPE_DIM // 2
    inv_freq = jnp.power(jnp.float32(ROPE_THETA), -jnp.arange(half, dtype=jnp.float32) * 2.0 / ROPE_DIM)
    ang = positions.astype(jnp.float32)[:, None] * inv_freq[None, :]
    cos = jnp.cos(ang)[None, :, None, :]
    sin = jnp.sin(ang)[None, :, None, :]
    xf = x.astype(jnp.float32)
    x1 = xf[..., :half]
    x2 = xf[..., half:ROPE_DIM]
    out = jnp.concatenate([x1 * cos - x2 * sin, x2 * cos + x1 * sin, xf[..., ROPE_DIM:]], axis=-1)
    return out.astype(x.dtype)


def _fox_attention(q, k, v, log_f):
    B, S, H, D = q.shape
    nb = S // Q_BLOCK
    scale = D ** -0.5
    F = jnp.cumsum(log_f, axis=1)
    F_k = jnp.transpose(F, (0, 2, 1))[:, :, None, :]
    k_pos = jnp.arange(S)
    q_blocks = jnp.swapaxes(q.reshape(B, nb, Q_BLOCK, H, D), 0, 1)
    F_blocks = jnp.swapaxes(F.reshape(B, nb, Q_BLOCK, H), 0, 1)

    def one_block(args):
        q_blk, F_blk, i = args
        s = jnp.einsum('bqhd,bkhd->bhqk', q_blk, k, preferred_element_type=jnp.float32) * scale
        s = s + jnp.transpose(F_blk, (0, 2, 1))[..., None] - F_k
        q_pos = i * Q_BLOCK + jnp.arange(Q_BLOCK)
        mask = k_pos[None, :] <= q_pos[:, None]
        s = jnp.where(mask[None, None], s, NEG_INF)
        p = jax.nn.softmax(s, axis=-1)
        return jnp.einsum('bhqk,bkhd->bqhd', p.astype(v.dtype), v)

    out = lax.map(one_block, (q_blocks, F_blocks, jnp.arange(nb)))
    return jnp.swapaxes(out, 0, 1).reshape(B, S, H, D)


def _dilated_group(q, k, v, window, dilation):
    B, S, H, D = q.shape
    L = S // dilation
    W = window // dilation
    blk = W
    Lp = -(-L // blk) * blk
    nb = Lp // blk
    N = B * dilation
    scale = D ** -0.5

    def to_sub(t):
        t = jnp.transpose(t.reshape(B, L, dilation, H, D), (0, 2, 1, 3, 4)).reshape(N, L, H, D)
        t = jnp.pad(t, ((0, 0), (0, Lp - L), (0, 0), (0, 0)))
        return t.reshape(N, nb, blk, H, D)

    def with_prev(t):
        prev = jnp.pad(t[:, :-1], ((0, 0), (1, 0), (0, 0), (0, 0), (0, 0)))
        return jnp.concatenate([prev, t], axis=2)

    qs = to_sub(q)
    k2 = with_prev(to_sub(k))
    v2 = with_prev(to_sub(v))
    s = jnp.einsum('nbqhd,nbkhd->nbhqk', qs, k2, preferred_element_type=jnp.float32) * scale
    a = jnp.arange(blk)
    kk = jnp.arange(2 * blk)
    bidx = jnp.arange(nb)
    diff = blk + a[:, None] - kk[None, :]
    key_idx = bidx[:, None] * blk - blk + kk[None, :]
    mask = ((diff >= 0) & (diff <= W))[None, :, :] & (key_idx >= 0)[:, None, :]
    s = jnp.where(mask[None, :, None], s, NEG_INF)
    m = jnp.max(s, axis=-1, keepdims=True)
    p = jnp.exp(s - m)
    denom = jnp.sum(p, axis=-1, keepdims=True)
    lse = (m + jnp.log(denom))[..., 0]
    out = jnp.einsum('nbhqk,nbkhd->nbqhd', (p / denom).astype(v.dtype), v2)
    out = out.reshape(N, Lp, H, D)[:, :L]
    out = jnp.transpose(out.reshape(B, dilation, L, H, D), (0, 2, 1, 3, 4)).reshape(B, S, H, D)
    lse = jnp.transpose(lse, (0, 1, 3, 2)).reshape(N, Lp, H)[:, :L]
    lse = jnp.transpose(lse.reshape(B, dilation, L, H), (0, 2, 1, 3)).reshape(B, S, H)
    return out, lse


def _dilated_mixture(q, k, v):
    outs, lses = [], []
    for g, (window, dilation) in enumerate(DIL_GROUPS):
        sl = slice(g * DIL_HEADS_PER_GROUP, (g + 1) * DIL_HEADS_PER_GROUP)
        o, lse = _dilated_group(q[:, :, sl], k[:, :, sl], v[:, :, sl], window, dilation)
        outs.append(o)
        lses.append(lse)
    outs = jnp.stack(outs, axis=0)
    alpha = jax.nn.softmax(jnp.stack(lses, axis=0), axis=0)
    return jnp.sum(alpha[..., None].astype(outs.dtype) * outs, axis=0)


def _fwd_setup_inputs(seed: int = 0) -> dict:
    key = jax.random.key(seed)
    ks = jax.random.split(key, 12)
    f32 = jnp.float32

    def nrm(k, shape, fan_in):
        return jax.random.normal(k, shape, f32) * (fan_in ** -0.5)

    return {
        'x': jax.random.normal(ks[0], (BATCH, SEQ, D_MODEL), f32),
        'norm_attn_g': 1.0 + 0.02 * jax.random.normal(ks[1], (DEPTH, D_MODEL), f32),
        'w_in': nrm(ks[2], (DEPTH, D_MODEL, D_IN), D_MODEL),
        'b_forget': 2.0 + 0.5 * jax.random.normal(ks[3], (DEPTH, FOX_HEADS), f32),
        'w_branch_a': nrm(ks[4], (DEPTH, FOX_W, D_MODEL), FOX_W),
        'w_branch_b': nrm(ks[5], (DEPTH, DIL_OUT_W, D_MODEL), DIL_OUT_W),
        'w_out': nrm(ks[6], (DEPTH, D_MODEL, D_MODEL), D_MODEL),
        'norm_mlp_g': 1.0 + 0.02 * jax.random.normal(ks[7], (DEPTH, D_MODEL), f32),
        'w_up': nrm(ks[8], (DEPTH, D_MODEL, D_FF), D_MODEL),
        'w_down': nrm(ks[9], (DEPTH, D_FF, D_MODEL), D_FF),
        'norm_final_g': 1.0 + 0.02 * jax.random.normal(ks[10], (D_MODEL,), f32),
    }


def _fwd_reference(x, norm_attn_g, w_in, b_forget, w_branch_a, w_branch_b, w_out,
              norm_mlp_g, w_up, w_down, norm_final_g):
    B, S, _ = x.shape
    positions = jnp.arange(S)
    split_points = [int(v) for v in np.cumsum(IN_SPLIT_SIZES)[:-1]]
    for l in range(DEPTH):
        h = _rms_norm(x, norm_attn_g[l])
        proj = h @ w_in[l]
        qa, ka, va, fa, qb, kb, vb, ga, gb = jnp.split(proj, split_points, axis=-1)
        log_f = jax.nn.log_sigmoid(fa.astype(jnp.float32) + b_forget[l].astype(jnp.float32))
        oa = _fox_attention(qa.reshape(B, S, FOX_HEADS, HEAD_DIM),
                            ka.reshape(B, S, FOX_HEADS, HEAD_DIM),
                            va.reshape(B, S, FOX_HEADS, HEAD_DIM), log_f)
        qb = _partial_rope(qb.reshape(B, S, DIL_HEADS, HEAD_DIM), positions)
        kb = _partial_rope(kb.reshape(B, S, DIL_HEADS, HEAD_DIM), positions)
        ob = _dilated_mixture(qb, kb, vb.reshape(B, S, DIL_HEADS, HEAD_DIM))
        ya = oa.reshape(B, S, FOX_W) @ w_branch_a[l]
        yb = ob.reshape(B, S, DIL_OUT_W) @ w_branch_b[l]
        mixed = jax.nn.sigmoid(ga) * ya + jax.nn.sigmoid(gb) * yb
        x = x + mixed @ w_out[l]
        h = _rms_norm(x, norm_mlp_g[l])
        x = x + jnp.square(jax.nn.relu(h @ w_up[l])) @ w_down[l]
    return _rms_norm(x, norm_final_g)


import jax as _jax
import jax.numpy as _jnp

TWIN_FORMAT = 'train_step'
FWD_PARAMS = ['x', 'norm_attn_g', 'w_in', 'b_forget', 'w_branch_a', 'w_branch_b', 'w_out', 'norm_mlp_g', 'w_up', 'w_down', 'norm_final_g']
TWIN_WEIGHTS = ['norm_attn_g', 'w_in', 'b_forget', 'w_branch_a', 'w_branch_b', 'w_out', 'norm_mlp_g', 'w_up', 'w_down', 'norm_final_g']
TWIN_DIFF_INPUT = 'x'
TWIN_INPUTS = ['x', 'norm_attn_g', 'w_in', 'b_forget', 'w_branch_a', 'w_branch_b', 'w_out', 'norm_mlp_g', 'w_up', 'w_down', 'norm_final_g', 'loss_target', 'm_norm_attn_g', 'm_w_in', 'm_b_forget', 'm_w_branch_a', 'm_w_branch_b', 'm_w_out', 'm_norm_mlp_g', 'm_w_up', 'm_w_down', 'm_norm_final_g', 'v_norm_attn_g', 'v_w_in', 'v_b_forget', 'v_w_branch_a', 'v_w_branch_b', 'v_w_out', 'v_norm_mlp_g', 'v_w_up', 'v_w_down', 'v_norm_final_g']
TWIN_OUTPUTS = ['loss', 'grad_x', 'grad_norm_attn_g', 'grad_w_in', 'grad_b_forget', 'grad_w_branch_a', 'grad_w_branch_b', 'grad_w_out', 'grad_norm_mlp_g', 'grad_w_up', 'grad_w_down', 'grad_norm_final_g', 'delta_norm_attn_g', 'delta_w_in', 'delta_b_forget', 'delta_w_branch_a', 'delta_w_branch_b', 'delta_w_out', 'delta_norm_mlp_g', 'delta_w_up', 'delta_w_down', 'delta_norm_final_g', 'new_m_norm_attn_g', 'new_m_w_in', 'new_m_b_forget', 'new_m_w_branch_a', 'new_m_w_branch_b', 'new_m_w_out', 'new_m_norm_mlp_g', 'new_m_w_up', 'new_m_w_down', 'new_m_norm_final_g', 'new_v_norm_attn_g', 'new_v_w_in', 'new_v_b_forget', 'new_v_w_branch_a', 'new_v_w_branch_b', 'new_v_w_out', 'new_v_norm_mlp_g', 'new_v_w_up', 'new_v_w_down', 'new_v_norm_final_g']
TWIN_LEAF_KINDS = {'loss': 'loss', 'grad_x': 'grad_x', 'grad_norm_attn_g': 'grad_w', 'grad_w_in': 'grad_w', 'grad_b_forget': 'grad_w', 'grad_w_branch_a': 'grad_w', 'grad_w_branch_b': 'grad_w', 'grad_w_out': 'grad_w', 'grad_norm_mlp_g': 'grad_w', 'grad_w_up': 'grad_w', 'grad_w_down': 'grad_w', 'grad_norm_final_g': 'grad_w', 'delta_norm_attn_g': 'delta_w', 'delta_w_in': 'delta_w', 'delta_b_forget': 'delta_w', 'delta_w_branch_a': 'delta_w', 'delta_w_branch_b': 'delta_w', 'delta_w_out': 'delta_w', 'delta_norm_mlp_g': 'delta_w', 'delta_w_up': 'delta_w', 'delta_w_down': 'delta_w', 'delta_norm_final_g': 'delta_w', 'new_m_norm_attn_g': 'new_m', 'new_m_w_in': 'new_m', 'new_m_b_forget': 'new_m', 'new_m_w_branch_a': 'new_m', 'new_m_w_branch_b': 'new_m', 'new_m_w_out': 'new_m', 'new_m_norm_mlp_g': 'new_m', 'new_m_w_up': 'new_m', 'new_m_w_down': 'new_m', 'new_m_norm_final_g': 'new_m', 'new_v_norm_attn_g': 'new_v', 'new_v_w_in': 'new_v', 'new_v_b_forget': 'new_v', 'new_v_w_branch_a': 'new_v', 'new_v_w_branch_b': 'new_v', 'new_v_w_out': 'new_v', 'new_v_norm_mlp_g': 'new_v', 'new_v_w_up': 'new_v', 'new_v_w_down': 'new_v', 'new_v_norm_final_g': 'new_v'}


def _forward(args):
    return _fwd_reference(*[args[k] for k in FWD_PARAMS])


def _output_shape():
    out = _jax.eval_shape(lambda: _forward(_fwd_setup_inputs(0)))
    return out.shape, out.dtype

N_MICROBATCH = 1
ADAM_LR = 0.001
ADAM_B1 = 0.9
ADAM_B2 = 0.999
ADAM_EPS = 1e-08
ADAM_WD = 0.01
ADAM_STEP = 10
PER_EXAMPLE_BATCH_AXIS = {'x': 0, 'loss_target': 0}
SHARED_INPUTS = []
_WEIGHT_DTYPES = {'norm_attn_g': _jnp.float32, 'w_in': _jnp.float32, 'b_forget': _jnp.float32, 'w_branch_a': _jnp.float32, 'w_branch_b': _jnp.float32, 'w_out': _jnp.float32, 'norm_mlp_g': _jnp.float32, 'w_up': _jnp.float32, 'w_down': _jnp.float32, 'norm_final_g': _jnp.float32}
MOMENT_SCALE = {'norm_attn_g': 5.103325e-02, 'w_in': 2.115668e-02, 'b_forget': 1.947436e-01, 'w_branch_a': 3.048684e-02, 'w_branch_b': 1.321632e-02, 'w_out': 3.309099e-02, 'norm_mlp_g': 1.129006e-01, 'w_up': 5.555321e-02, 'w_down': 1.043178e-01, 'norm_final_g': 1.615414e+01}


def _to_microbatches(a, axis):
    t = _jnp.moveaxis(a, axis, 0)
    t = t.reshape((N_MICROBATCH, t.shape[0] // N_MICROBATCH) + t.shape[1:])
    return _jnp.moveaxis(t, 1, axis + 1)


def setup_inputs(seed: int = 0) -> dict:
    inp = _fwd_setup_inputs(seed)
    key = _jax.random.fold_in(_jax.random.key(seed), 7919)
    shape, _ = _output_shape()
    out = dict(inp)
    out["loss_target"] = _jax.random.normal(_jax.random.fold_in(key, 0), shape, _jnp.float32)
    for i, name in enumerate(TWIN_WEIGHTS):
        w = inp[name].astype(_jnp.float32)
        if MOMENT_SCALE is None:
            s = _jnp.sqrt(_jnp.mean(_jnp.square(w)) + 1e-30)
        else:
            s = MOMENT_SCALE[name]
        km, kv = _jax.random.split(_jax.random.fold_in(key, i + 1))
        out[name] = w
        out["m_" + name] = s * _jax.random.normal(km, w.shape, _jnp.float32)
        out["v_" + name] = (s * s) * _jax.random.uniform(kv, w.shape, _jnp.float32, 0.5, 1.5)
    if N_MICROBATCH > 1:
        for name, axis in PER_EXAMPLE_BATCH_AXIS.items():
            out[name] = _to_microbatches(out[name], axis)
    return {'x': out['x'], 'norm_attn_g': out['norm_attn_g'], 'w_in': out['w_in'], 'b_forget': out['b_forget'], 'w_branch_a': out['w_branch_a'], 'w_branch_b': out['w_branch_b'], 'w_out': out['w_out'], 'norm_mlp_g': out['norm_mlp_g'], 'w_up': out['w_up'], 'w_down': out['w_down'], 'norm_final_g': out['norm_final_g'], 'loss_target': out['loss_target'], 'm_norm_attn_g': out['m_norm_attn_g'], 'm_w_in': out['m_w_in'], 'm_b_forget': out['m_b_forget'], 'm_w_branch_a': out['m_w_branch_a'], 'm_w_branch_b': out['m_w_branch_b'], 'm_w_out': out['m_w_out'], 'm_norm_mlp_g': out['m_norm_mlp_g'], 'm_w_up': out['m_w_up'], 'm_w_down': out['m_w_down'], 'm_norm_final_g': out['m_norm_final_g'], 'v_norm_attn_g': out['v_norm_attn_g'], 'v_w_in': out['v_w_in'], 'v_b_forget': out['v_b_forget'], 'v_w_branch_a': out['v_w_branch_a'], 'v_w_branch_b': out['v_w_branch_b'], 'v_w_out': out['v_w_out'], 'v_norm_mlp_g': out['v_norm_mlp_g'], 'v_w_up': out['v_w_up'], 'v_w_down': out['v_w_down'], 'v_norm_final_g': out['v_norm_final_g']}


def _loss(weights, diff, rest, loss_target):
    with _jax.named_scope("forward"):
        args = {**rest, TWIN_DIFF_INPUT: diff, **{k: w.astype(_WEIGHT_DTYPES[k]) for k, w in weights.items()}}
        y = _forward(args)
    with _jax.named_scope("loss_head"):
        err = _jnp.square(y.astype(_jnp.float32) - loss_target)
        return 0.5 * _jnp.sum(_jnp.mean(err, axis=-1)) if err.ndim else 0.5 * err


def _adamw(w, g, m, v):
    m = ADAM_B1 * m + (1.0 - ADAM_B1) * g
    v = ADAM_B2 * v + (1.0 - ADAM_B2) * _jnp.square(g)
    m_hat = m / (1.0 - ADAM_B1 ** ADAM_STEP)
    v_hat = v / (1.0 - ADAM_B2 ** ADAM_STEP)
    delta = -ADAM_LR * (m_hat / (_jnp.sqrt(v_hat) + ADAM_EPS) + ADAM_WD * w)
    return delta, m, v


def reference(x, norm_attn_g, w_in, b_forget, w_branch_a, w_branch_b, w_out, norm_mlp_g, w_up, w_down, norm_final_g, loss_target, m_norm_attn_g, m_w_in, m_b_forget, m_w_branch_a, m_w_branch_b, m_w_out, m_norm_mlp_g, m_w_up, m_w_down, m_norm_final_g, v_norm_attn_g, v_w_in, v_b_forget, v_w_branch_a, v_w_branch_b, v_w_out, v_norm_mlp_g, v_w_up, v_w_down, v_norm_final_g):
    given = dict(x=x, norm_attn_g=norm_attn_g, w_in=w_in, b_forget=b_forget, w_branch_a=w_branch_a, w_branch_b=w_branch_b, w_out=w_out, norm_mlp_g=norm_mlp_g, w_up=w_up, w_down=w_down, norm_final_g=norm_final_g, loss_target=loss_target, m_norm_attn_g=m_norm_attn_g, m_w_in=m_w_in, m_b_forget=m_b_forget, m_w_branch_a=m_w_branch_a, m_w_branch_b=m_w_branch_b, m_w_out=m_w_out, m_norm_mlp_g=m_norm_mlp_g, m_w_up=m_w_up, m_w_down=m_w_down, m_norm_final_g=m_norm_final_g, v_norm_attn_g=v_norm_attn_g, v_w_in=v_w_in, v_b_forget=v_b_forget, v_w_branch_a=v_w_branch_a, v_w_branch_b=v_w_branch_b, v_w_out=v_w_out, v_norm_mlp_g=v_norm_mlp_g, v_w_up=v_w_up, v_w_down=v_w_down, v_norm_final_g=v_norm_final_g)
    weights = {n: given[n] for n in TWIN_WEIGHTS}
    shared = {n: given[n] for n in SHARED_INPUTS}
    per_example = {n: given[n] for n in ['x']}
    grad_fn = _jax.value_and_grad(_loss, argnums=(0, 1))

    def one_microbatch(ex, loss_target):
        ex = dict(ex)
        diff = ex.pop(TWIN_DIFF_INPUT)
        return grad_fn(weights, diff, {**shared, **ex}, loss_target)

    if N_MICROBATCH == 1:
        loss, (grad_w, grad_x) = one_microbatch(per_example, given["loss_target"])
    else:
        def body(carry, xs):
            loss_sum, grad_sum = carry
            l_k, (gw_k, gx_k) = one_microbatch(xs[0], xs[1])
            with _jax.named_scope("update"):
                return (loss_sum + l_k, _jax.tree.map(_jnp.add, grad_sum, gw_k)), gx_k

        init = (_jnp.zeros((), _jnp.float32), _jax.tree.map(_jnp.zeros_like, weights))
        (loss, grad_w), grad_x = _jax.lax.scan(body, init, (per_example, given["loss_target"]))
    with _jax.named_scope("update"):
        delta_w, new_m, new_v = {}, {}, {}
        for n in TWIN_WEIGHTS:
            delta_w[n], new_m[n], new_v[n] = _adamw(weights[n], grad_w[n], given["m_" + n], given["v_" + n])
    return (loss, grad_x, *[grad_w[n] for n in TWIN_WEIGHTS], *[delta_w[n] for n in TWIN_WEIGHTS],
            *[new_m[n] for n in TWIN_WEIGHTS], *[new_v[n] for n in TWIN_WEIGHTS])
```

```python
import functools

import jax
import jax.numpy as jnp
import numpy as np
from jax import lax
from jax.experimental import pallas as pl
from jax.experimental.pallas import tpu as pltpu

f32 = jnp.float32
bf16 = jnp.bfloat16

S = 2048
D = 1024
DFF = 4096
HD = 64
FOXW = 512
DILW = 768
DILOUT = 256
DIL = (1, 4, 16)
BAND = 128
EPS = 1e-6
NEG = -1e30
ROPE_THETA = 500000.0
NCHIP = 4

C_QKVA, C_QB, C_KB, C_VB, C_G, C_FA, WP = 0, 1536, 2304, 3072, 3840, 5888, 6016
D_IN = 5896
SHARD_IN = 1474

ADAM_LR, ADAM_B1, ADAM_B2, ADAM_EPS, ADAM_WD, ADAM_STEP = 0.001, 0.9, 0.999, 1e-08, 0.01, 10

VMEM_LIMIT = 56 * 1024 * 1024
TQ = 256


def _params(sem=None):
    return pltpu.CompilerParams(dimension_semantics=sem, vmem_limit_bytes=VMEM_LIMIT)


def _dot(a, b):
    return jnp.dot(a, b, preferred_element_type=f32)


def _dot_nt(a, b):
    return lax.dot_general(a, b, (((1,), (1,)), ((), ())), preferred_element_type=f32)


def _dot_tn(a, b):
    return lax.dot_general(a, b, (((0,), (0,)), ((), ())), preferred_element_type=f32)


def _split3(x):
    hi = x.astype(bf16)
    r1 = x - hi.astype(f32)
    mid = r1.astype(bf16)
    lo = (r1 - mid.astype(f32)).astype(bf16)
    return hi, mid, lo


def _rope_tables():
    half = 8
    inv_freq = jnp.power(jnp.float32(ROPE_THETA), -jnp.arange(half, dtype=f32) * 2.0 / 16)
    ang = jnp.arange(S).astype(f32)[:, None] * inv_freq[None, :]
    cos, sin = jnp.cos(ang), jnp.sin(ang)
    one = jnp.ones((S, HD - 16), f32)
    zero = jnp.zeros((S, HD - 16), f32)
    z8 = jnp.zeros((S, 8), f32)
    c = jnp.concatenate([cos, cos, one], axis=1)
    s1 = jnp.concatenate([-sin, z8, zero], axis=1)
    s2 = jnp.concatenate([z8, sin, zero], axis=1)
    return tuple(jnp.concatenate([t, t], axis=1) for t in (c, s1, s2))


def _mm(a, b, mode, out_dtype, tm, tn, name, stack_cols=False):
    if mode == "nn":
        (M, K), (_, N) = a.shape, b.shape
        a_spec = pl.BlockSpec((tm, K), lambda i, j: (i, 0))
        b_spec = pl.BlockSpec((K, tn), lambda i, j: (0, j))
        dot = _dot
    elif mode == "nt":
        (M, K), (N, _) = a.shape, b.shape
        a_spec = pl.BlockSpec((tm, K), lambda i, j: (i, 0))
        b_spec = pl.BlockSpec((tn, K), lambda i, j: (j, 0))
        dot = _dot_nt
    else:
        (K, M), (_, N) = a.shape, b.shape
        a_spec = pl.BlockSpec((K, tm), lambda i, j: (0, i))
        b_spec = pl.BlockSpec((K, tn), lambda i, j: (0, j))
        dot = _dot_tn

    def body(a_ref, b_ref, o_ref):
        o_ref[...] = dot(a_ref[...], b_ref[...]).astype(out_dtype)

    if stack_cols:
        assert tm == M
        out_spec = pl.BlockSpec((None, tm, tn), lambda i, j: (j, 0, 0))
        out_shape = jax.ShapeDtypeStruct((N // tn, M, tn), out_dtype)
    else:
        out_spec = pl.BlockSpec((tm, tn), lambda i, j: (i, j))
        out_shape = jax.ShapeDtypeStruct((M, N), out_dtype)
    return pl.pallas_call(
        body, name=name, grid=(M // tm, N // tn), in_specs=[a_spec, b_spec],
        out_specs=out_spec, out_shape=out_shape,
        compiler_params=_params(("parallel", "parallel")),
    )(a, b)


def _norm_inproj(x, g1, wp, rope):
    tm = 256
    c_t, s1_t, s2_t = rope

    def body(x_ref, g_ref, w_ref, c_ref, s1_ref, s2_ref, h_ref, qkva_ref, qkvb_ref, gates_ref, fa_ref):
        xb = x_ref[...]
        r = lax.rsqrt(jnp.mean(xb * xb, axis=-1, keepdims=True) + EPS)
        h = ((xb * r) * g_ref[...]).astype(bf16)
        h_ref[...] = h
        qkva_ref[...] = _dot(h, w_ref[:, C_QKVA:C_QB]).astype(bf16)
        c, s1, s2 = c_ref[...], s1_ref[...], s2_ref[...]
        for sec, lo in enumerate((C_QB, C_KB)):
            pb = _dot(h, w_ref[:, lo:lo + DILW])
            for ch in range(DILW // 128):
                pc = pb[:, ch * 128:(ch + 1) * 128]
                roped = pc * c + pltpu.roll(pc, 120, 1) * s1 + pltpu.roll(pc, 8, 1) * s2
                qkvb_ref[:, sec * DILW + ch * 128: sec * DILW + (ch + 1) * 128] = roped
        qkvb_ref[:, 2 * DILW:3 * DILW] = _dot(h, w_ref[:, C_VB:C_G])
        gates_ref[...] = _dot(h, w_ref[:, C_G:C_FA])
        fa_ref[...] = _dot(h, w_ref[:, C_FA:WP])

    row = lambda w: pl.BlockSpec((tm, w), lambda i: (i, 0))
    return pl.pallas_call(
        body, name="norm_inproj", grid=(S // tm,),
        in_specs=[row(D), pl.BlockSpec((1, D), lambda i: (0, 0)), pl.BlockSpec((D, WP), lambda i: (0, 0)),
                  row(128), row(128), row(128)],
        out_specs=[row(D), row(3 * FOXW), row(3 * DILW), row(2 * D), row(128)],
        out_shape=[jax.ShapeDtypeStruct((S, D), bf16), jax.ShapeDtypeStruct((S, 3 * FOXW), bf16),
                   jax.ShapeDtypeStruct((S, 3 * DILW), f32), jax.ShapeDtypeStruct((S, 2 * D), f32),
                   jax.ShapeDtypeStruct((S, 128), f32)],
        compiler_params=_params(("parallel",)),
    )(x, g1, wp, c_t, s1_t, s2_t)


def _forget_cumsum(fa, bpad):
    nb = S // TQ

    def body(fa_ref, b_ref, F_ref, ftb_ref):
        rr = lax.broadcasted_iota(jnp.int32, (TQ, TQ), 0)
        cc = lax.broadcasted_iota(jnp.int32, (TQ, TQ), 1)
        tri = (rr >= cc).astype(bf16)
        lane = lax.broadcasted_iota(jnp.int32, (1, 128), 1)
        carry = jnp.zeros((1, 128), f32)
        for b in range(nb):
            z = fa_ref[b * TQ:(b + 1) * TQ, :] + b_ref[...]
            lf = jnp.minimum(z, 0.0) - jnp.log(1.0 + jnp.exp(-jnp.abs(z)))
            lf = jnp.where(lane < 8, lf, 0.0)
            hi, mid, lo = _split3(lf)
            fb = (_dot(tri, hi) + _dot(tri, mid)) + _dot(tri, lo) + carry
            F_ref[b * TQ:(b + 1) * TQ, :] = fb
            ftb_ref[b] = fb.T[0:8, :]
            carry = fb[TQ - 1:TQ, :]

    return pl.pallas_call(
        body, name="forget_cumsum",
        out_shape=[jax.ShapeDtypeStruct((S, 128), f32), jax.ShapeDtypeStruct((nb, 8, TQ), f32)],
        compiler_params=_params(),
    )(fa, bpad)


def _head_masks():
    lane = lax.broadcasted_iota(jnp.int32, (1, 128), 1)
    return lane, (lane < HD, lane >= HD)


def _fox_fwd(qkva, F, ftb):
    nb = S // TQ

    def body(q_ref, k_ref, v_ref, F_ref, ftb_ref, o_ref, lse_ref):
        p = pl.program_id(0)
        lane, hm = _head_masks()
        sub8 = lax.broadcasted_iota(jnp.int32, (8, 1), 0)
        rowi = lax.broadcasted_iota(jnp.int32, (TQ, 1), 0)
        coli = lax.broadcasted_iota(jnp.int32, (1, TQ), 1)

        def qblock(i, c):
            r0 = pl.multiple_of(i * TQ, TQ)
            q = q_ref[pl.ds(r0, TQ), :].astype(f32) * 0.125
            qs = [jnp.where(hm[hh], q, 0.0).astype(bf16) for hh in (0, 1)]
            Fb = F_ref[pl.ds(r0, TQ), :]
            fc = [jnp.sum(jnp.where(lane == 2 * p + hh, Fb, 0.0), axis=1, keepdims=True) for hh in (0, 1)]

            def kvblock(j, carry):
                c0 = pl.multiple_of(j * TQ, TQ)
                k = k_ref[pl.ds(c0, TQ), :]
                v = v_ref[pl.ds(c0, TQ), :]
                frow = ftb_ref[j]
                causal = (c0 + coli) <= (r0 + rowi)
                new = []
                for hh in (0, 1):
                    m, l, a = carry[3 * hh:3 * hh + 3]
                    fr = jnp.sum(jnp.where(sub8 == 2 * p + hh, frow, 0.0), axis=0, keepdims=True)
                    s = _dot_nt(qs[hh], k) + (fc[hh] - fr)
                    s = jnp.where(causal, s, NEG)
                    mn = jnp.maximum(m, jnp.max(s, axis=1, keepdims=True))
                    al = jnp.exp(m - mn)
                    pr = jnp.exp(s - mn)
                    l = al * l + jnp.sum(pr, axis=1, keepdims=True)
                    a = al * a + _dot(pr.astype(bf16), v)
                    new += [mn, l, a]
                return tuple(new)

            init = (jnp.full((TQ, 1), NEG, f32), jnp.zeros((TQ, 1), f32), jnp.zeros((TQ, 128), f32)) * 2
            m0, l0, a0, m1, l1, a1 = lax.fori_loop(0, i + 1, kvblock, init)
            o = jnp.where(hm[0], a0 / l0, a1 / l1)
            lse = jnp.where(hm[0], m0 + jnp.log(l0), m1 + jnp.log(l1))
            o_ref[pl.ds(r0, TQ), :] = o.astype(bf16)
            lse_ref[pl.ds(r0, TQ), :] = lse
            return c

        lax.fori_loop(0, nb, qblock, 0)

    blk = lambda off: pl.BlockSpec((S, 128), lambda p: (0, off + p))
    return pl.pallas_call(
        body, name="fox_fwd", grid=(4,),
        in_specs=[blk(0), blk(4), blk(8), pl.BlockSpec((S, 128), lambda p: (0, 0)),
                  pl.BlockSpec((nb, 8, TQ), lambda p: (0, 0, 0))],
        out_specs=[blk(0), blk(0)],
        out_shape=[jax.ShapeDtypeStruct((S, FOXW), bf16), jax.ShapeDtypeStruct((S, FOXW), f32)],
        compiler_params=_params(("parallel",)),
    )(qkva, qkva, qkva, F, ftb)


def _permute_in(dst, src, r):
    L = S // r
    for rho in range(r):
        dst[rho * L:(rho + 1) * L, :] = src[pl.ds(rho, L, stride=r), :]


def _permute_out(dst, src, r):
    L = S // r
    for rho in range(r):
        dst[pl.ds(rho, L, stride=r), :] = src[rho * L:(rho + 1) * L, :]


def _band_geometry(bb, nbl):
    r0 = pl.multiple_of(bb * BAND, BAND)
    k0 = pl.multiple_of(jnp.maximum(bb - 1, 0) * BAND, BAND)
    sub0 = (bb - lax.rem(bb, nbl)) * BAND
    qi = r0 + lax.broadcasted_iota(jnp.int32, (BAND, 1), 0)
    ki = k0 + lax.broadcasted_iota(jnp.int32, (1, 2 * BAND), 1)
    diff = qi - ki
    valid = (diff >= 0) & (diff <= BAND) & (ki >= sub0)
    return r0, k0, valid


def _dil_in_specs():
    specs = []
    for role in range(3):
        for g in range(3):
            specs.append(pl.BlockSpec((S, 128), functools.partial(lambda p, o: (0, o + p), o=role * 6 + g * 2)))
    return specs


def _dil_fwd(qkvb):
    def body(*refs):
        q_refs, k_refs, v_refs = refs[0:3], refs[3:6], refs[6:9]
        ob_ref, lse_ref = refs[9:11]
        qp, kp, vp, op, lp = refs[11:16]
        on = refs[16:19]
        ln = refs[19:22]
        _, hm = _head_masks()
        for g, r in enumerate(DIL):
            nbl = S // r // BAND
            if r == 1:
                qs_, ks_, vs_, od, ld = q_refs[g], k_refs[g], v_refs[g], on[g], ln[g]
            else:
                _permute_in(qp, q_refs[g], r)
                _permute_in(kp, k_refs[g], r)
                _permute_in(vp, v_refs[g], r)
                qs_, ks_, vs_, od, ld = qp, kp, vp, op, lp

            def blk(bb, c, qs_=qs_, ks_=ks_, vs_=vs_, od=od, ld=ld, nbl=nbl):
                r0, k0, valid = _band_geometry(bb, nbl)
                q = qs_[pl.ds(r0, BAND), :] * 0.125
                kw = ks_[pl.ds(k0, 2 * BAND), :].astype(bf16)
                vw = vs_[pl.ds(k0, 2 * BAND), :]
                o = jnp.zeros((BAND, 128), f32)
                lse = jnp.zeros((BAND, 128), f32)
                for hh in (0, 1):
                    qh = jnp.where(hm[hh], q, 0.0).astype(bf16)
                    s = jnp.where(valid, _dot_nt(qh, kw), NEG)
                    m = jnp.max(s, axis=1, keepdims=True)
                    pr = jnp.exp(s - m)
                    l = jnp.sum(pr, axis=1, keepdims=True)
                    vm = jnp.where(hm[hh], vw, 0.0).astype(bf16)
                    o = o + _dot((pr / l).astype(bf16), vm)
                    lse = jnp.where(hm[hh], m + jnp.log(l), lse)
                od[pl.ds(r0, BAND), :] = o
                ld[pl.ds(r0, BAND), :] = lse
                return c

            lax.fori_loop(0, S // BAND, blk, 0)
            if r != 1:
                _permute_out(on[g], op, r)
                _permute_out(ln[g], lp, r)

        def combine(i, c):
            r0 = pl.multiple_of(i * TQ, TQ)
            ls = [ln[g][pl.ds(r0, TQ), :] for g in range(3)]
            mx = jnp.maximum(jnp.maximum(ls[0], ls[1]), ls[2])
            es = [jnp.exp(l - mx) for l in ls]
            tot = (es[0] + es[1]) + es[2]
            acc = (es[0] / tot) * on[0][pl.ds(r0, TQ), :]
            acc = acc + (es[1] / tot) * on[1][pl.ds(r0, TQ), :]
            acc = acc + (es[2] / tot) * on[2][pl.ds(r0, TQ), :]
            ob_ref[pl.ds(r0, TQ), :] = acc.astype(bf16)
            lse_ref[pl.ds(r0, TQ), :] = mx + jnp.log(tot)
            return c

        lax.fori_loop(0, S // TQ, combine, 0)

    out_blk = pl.BlockSpec((S, 128), lambda p: (0, p))
    return pl.pallas_call(
        body, name="dil_fwd", grid=(2,),
        in_specs=_dil_in_specs(), out_specs=[out_blk, out_blk],
        out_shape=[jax.ShapeDtypeStruct((S, DILOUT), bf16), jax.ShapeDtypeStruct((S, DILOUT), f32)],
        scratch_shapes=[pltpu.VMEM((S, 128), f32)] * 11,
        compiler_params=_params(("parallel",)),
    )(*([qkvb] * 9))


def _branch_mix(oa, ob, was, wbs, gates):
    tm = 512

    def body(oa_ref, ob_ref, wa_ref, wb_ref, g_ref, ya_ref, yb_ref, mix_ref):
        oa_b, ob_b = oa_ref[...], ob_ref[...]
        for q in range(NCHIP):
            cols = slice(q * 256, (q + 1) * 256)
            ya = _dot(oa_b, wa_ref[q])
            yb = _dot(ob_b, wb_ref[q])
            ya_ref[:, cols] = ya
            yb_ref[:, cols] = yb
            ga = g_ref[:, q * 256:(q + 1) * 256]
            gb = g_ref[:, D + q * 256:D + (q + 1) * 256]
            mix_ref[:, cols] = (jax.nn.sigmoid(ga) * ya + jax.nn.sigmoid(gb) * yb).astype(bf16)

    row = lambda w: pl.BlockSpec((tm, w), lambda i: (i, 0))
    full3 = lambda a: pl.BlockSpec(a.shape, lambda i: (0, 0, 0))
    return pl.pallas_call(
        body, name="branch_mix", grid=(S // tm,),
        in_specs=[row(FOXW), row(DILOUT), full3(was), full3(wbs), row(2 * D)],
        out_specs=[row(D), row(D), row(D)],
        out_shape=[jax.ShapeDtypeStruct((S, D), f32), jax.ShapeDtypeStruct((S, D), f32),
                   jax.ShapeDtypeStruct((S, D), bf16)],
        compiler_params=_params(("parallel",)),
    )(oa, ob, was, wbs, gates)


def _outproj_norm(mixed, wout, x, g2):
    tm = 512

    def body(m_ref, w_ref, x_ref, g_ref, x2_ref, h2_ref):
        x2 = x_ref[...] + _dot(m_ref[...], w_ref[...])
        x2_ref[...] = x2
        r = lax.rsqrt(jnp.mean(x2 * x2, axis=-1, keepdims=True) + EPS)
        h2_ref[...] = ((x2 * r) * g_ref[...]).astype(bf16)

    row = pl.BlockSpec((tm, D), lambda i: (i, 0))
    return pl.pallas_call(
        body, name="outproj_norm", grid=(S // tm,),
        in_specs=[row, pl.BlockSpec((D, D), lambda i: (0, 0)), row, pl.BlockSpec((1, D), lambda i: (0, 0))],
        out_specs=[row, row],
        out_shape=[jax.ShapeDtypeStruct((S, D), f32), jax.ShapeDtypeStruct((S, D), bf16)],
        compiler_params=_params(("parallel",)),
    )(mixed, wout, x, g2)


def _mlp_up(h2, wups):
    tm = 512

    def body(h_ref, w_ref, u_ref, a_ref):
        u = _dot(h_ref[...], w_ref[...])
        u_ref[...] = u
        ru = jnp.maximum(u, 0.0)
        a_ref[...] = (ru * ru).astype(bf16)

    out = pl.BlockSpec((tm, D), lambda q, i: (i, q))
    return pl.pallas_call(
        body, name="mlp_up", grid=(NCHIP, S // tm),
        in_specs=[pl.BlockSpec((tm, D), lambda q, i: (i, 0)), pl.BlockSpec((None, D, D), lambda q, i: (q, 0, 0))],
        out_specs=[out, out],
        out_shape=[jax.ShapeDtypeStruct((S, DFF), f32), jax.ShapeDtypeStruct((S, DFF), bf16)],
        compiler_params=_params(("parallel", "parallel")),
    )(h2, wups)


def _mlp_down_loss(a, wdown, x2, g3, tgt):
    tm = 256

    def body(a_ref, w_ref, x2_ref, g_ref, t_ref, dx_ref, dxb_ref, dg_ref, loss_ref):
        i = pl.program_id(0)
        x3 = x2_ref[...] + _dot(a_ref[...], w_ref[...])
        r = lax.rsqrt(jnp.mean(x3 * x3, axis=-1, keepdims=True) + EPS)
        xh = x3 * r
        g = g_ref[...]
        e = xh * g - t_ref[...]
        part = 0.5 * jnp.sum(jnp.mean(e * e, axis=-1, keepdims=True), axis=0, keepdims=True)
        dy = e * (1.0 / D)
        gdy = dy * g
        dx = r * (gdy - xh * jnp.mean(gdy * xh, axis=-1, keepdims=True))
        dx_ref[...] = dx
        dxb_ref[...] = dx.astype(bf16)

        @pl.when(i == 0)
        def _():
            dg_ref[...] = jnp.zeros_like(dg_ref)
            loss_ref[...] = jnp.zeros_like(loss_ref)

        dg_ref[...] += jnp.sum(dy * xh, axis=0, keepdims=True)
        loss_ref[...] += jnp.broadcast_to(part, (1, 128))

    row = pl.BlockSpec((tm, D), lambda i: (i, 0))
    vec = pl.BlockSpec((1, D), lambda i: (0, 0))
    return pl.pallas_call(
        body, name="mlp_down_loss", grid=(S // tm,),
        in_specs=[pl.BlockSpec((tm, DFF), lambda i: (i, 0)), pl.BlockSpec((DFF, D), lambda i: (0, 0)), row, vec, row],
        out_specs=[row, row, vec, pl.BlockSpec((1, 128), lambda i: (0, 0))],
        out_shape=[jax.ShapeDtypeStruct((S, D), f32), jax.ShapeDtypeStruct((S, D), bf16),
                   jax.ShapeDtypeStruct((1, D), f32), jax.ShapeDtypeStruct((1, 128), f32)],
        compiler_params=_params(("arbitrary",)),
    )(a, wdown, x2, g3, tgt)


def _mlp_down_bwd(dx3b, wdown, u):
    tm = 256

    def body(d_ref, w_ref, u_ref, du_ref):
        d = d_ref[...]
        for q in range(NCHIP):
            cols = slice(q * D, (q + 1) * D)
            da = _dot_nt(d, w_ref[cols, :])
            du_ref[:, cols] = (da * (2.0 * jnp.maximum(u_ref[:, cols], 0.0))).astype(bf16)

    return pl.pallas_call(
        body, name="mlp_down_bwd", grid=(S // tm,),
        in_specs=[pl.BlockSpec((tm, D), lambda i: (i, 0)), pl.BlockSpec((DFF, D), lambda i: (0, 0)),
                  pl.BlockSpec((tm, DFF), lambda i: (i, 0))],
        out_specs=pl.BlockSpec((tm, DFF), lambda i: (i, 0)),
        out_shape=jax.ShapeDtypeStruct((S, DFF), bf16),
        compiler_params=_params(("parallel",)),
    )(dx3b, wdown, u)


def _mlp_up_bwd(du, wups, x2, dx3, g2):
    tm = 256

    def body(du_ref, w_ref, x2_ref, dx3_ref, g_ref, dx2_ref, dx2b_ref, dg_ref):
        i = pl.program_id(0)
        dh = jnp.zeros((tm, D), f32)
        for q in range(NCHIP):
            dh = dh + _dot_nt(du_ref[:, q * D:(q + 1) * D], w_ref[q])
        x2 = x2_ref[...]
        r = lax.rsqrt(jnp.mean(x2 * x2, axis=-1, keepdims=True) + EPS)
        xh = x2 * r
        gdh = dh * g_ref[...]
        dx2 = dx3_ref[...] + r * (gdh - xh * jnp.mean(gdh * xh, axis=-1, keepdims=True))
        dx2_ref[...] = dx2
        dx2b_ref[...] = dx2.astype(bf16)

        @pl.when(i == 0)
        def _():
            dg_ref[...] = jnp.zeros_like(dg_ref)

        dg_ref[...] += jnp.sum(dh * xh, axis=0, keepdims=True)

    row = pl.BlockSpec((tm, D), lambda i: (i, 0))
    vec = pl.BlockSpec((1, D), lambda i: (0, 0))
    return pl.pallas_call(
        body, name="mlp_up_bwd", grid=(S // tm,),
        in_specs=[pl.BlockSpec((tm, DFF), lambda i: (i, 0)), pl.BlockSpec((NCHIP, D, D), lambda i: (0, 0, 0)),
                  row, row, vec],
        out_specs=[row, row, vec],
        out_shape=[jax.ShapeDtypeStruct((S, D), f32), jax.ShapeDtypeStruct((S, D), bf16),
                   jax.ShapeDtypeStruct((1, D), f32)],
        compiler_params=_params(("arbitrary",)),
    )(du, wups, x2, dx3, g2)


def _gate_bwd(dx2b, wout, gates, ya, yb):
    tm = 256

    def body(d_ref, w_ref, g_ref, ya_ref, yb_ref, dya_ref, dyb_ref, dg_ref):
        dm = _dot_nt(d_ref[...], w_ref[...])
        sa = jax.nn.sigmoid(g_ref[:, 0:D])
        sb = jax.nn.sigmoid(g_ref[:, D:2 * D])
        dya_ref[...] = (dm * sa).astype(bf16)
        dyb_ref[...] = (dm * sb).astype(bf16)
        dg_ref[:, 0:D] = (dm * ya_ref[...] * (sa * (1.0 - sa))).astype(bf16)
        dg_ref[:, D:2 * D] = (dm * yb_ref[...] * (sb * (1.0 - sb))).astype(bf16)

    row = lambda w: pl.BlockSpec((tm, w), lambda i: (i, 0))
    return pl.pallas_call(
        body, name="gate_bwd", grid=(S // tm,),
        in_specs=[row(D), pl.BlockSpec((D, D), lambda i: (0, 0)), row(2 * D), row(D), row(D)],
        out_specs=[row(D), row(D), row(2 * D)],
        out_shape=[jax.ShapeDtypeStruct((S, D), bf16), jax.ShapeDtypeStruct((S, D), bf16),
                   jax.ShapeDtypeStruct((S, 2 * D), bf16)],
        compiler_params=_params(("parallel",)),
    )(dx2b, wout, gates, ya, yb)


def _branch_bwd(dya, dyb, was, wbs):
    tm = 512

    def body(dya_ref, dyb_ref, wa_ref, wb_ref, doa_ref, dob_ref):
        doa = jnp.zeros((tm, FOXW), f32)
        dob = jnp.zeros((tm, DILOUT), f32)
        for q in range(NCHIP):
            cols = slice(q * 256, (q + 1) * 256)
            doa = doa + _dot_nt(dya_ref[:, cols], wa_ref[q])
            dob = dob + _dot_nt(dyb_ref[:, cols], wb_ref[q])
        doa_ref[...] = doa.astype(bf16)
        dob_ref[...] = dob

    row = lambda w: pl.BlockSpec((tm, w), lambda i: (i, 0))
    full3 = lambda a: pl.BlockSpec(a.shape, lambda i: (0, 0, 0))
    return pl.pallas_call(
        body, name="branch_bwd", grid=(S // tm,),
        in_specs=[row(D), row(D), full3(was), full3(wbs)],
        out_specs=[row(FOXW), row(DILOUT)],
        out_shape=[jax.ShapeDtypeStruct((S, FOXW), bf16), jax.ShapeDtypeStruct((S, DILOUT), f32)],
        compiler_params=_params(("parallel",)),
    )(dya, dyb, was, wbs)


def _branch_wgrad(oa, ob, dya, dyb):
    def body(oa_ref, ob_ref, dya_ref, dyb_ref, dwa_ref, dwb_ref):
        dwa_ref[...] = _dot_tn(oa_ref[...], dya_ref[...])
        dwb_ref[...] = _dot_tn(ob_ref[...], dyb_ref[...])

    full = lambda w: pl.BlockSpec((S, w), lambda q: (0, 0))
    colq = pl.BlockSpec((S, 256), lambda q: (0, q))
    return pl.pallas_call(
        body, name="branch_wgrad", grid=(NCHIP,),
        in_specs=[full(FOXW), full(DILOUT), colq, colq],
        out_specs=[pl.BlockSpec((None, FOXW, 256), lambda q: (q, 0, 0)),
                   pl.BlockSpec((None, DILOUT, 256), lambda q: (q, 0, 0))],
        out_shape=[jax.ShapeDtypeStruct((NCHIP, FOXW, 256), f32), jax.ShapeDtypeStruct((NCHIP, DILOUT, 256), f32)],
        compiler_params=_params(("parallel",)),
    )(oa, ob, dya, dyb)


def _fox_bwd(qkva, doa, oa, lse, F, ftb):
    nb = S // TQ

    def body(q_ref, k_ref, v_ref, do_ref, o_ref, lse_ref, F_ref, ftb_ref, dq_ref, dk_ref, dv_ref, dft_ref, dfq_ref,
             dq_scr):
        p = pl.program_id(0)
        lane, hm = _head_masks()
        sub8 = lax.broadcasted_iota(jnp.int32, (8, 1), 0)
        rowi = lax.broadcasted_iota(jnp.int32, (TQ, 1), 0)
        coli = lax.broadcasted_iota(jnp.int32, (1, TQ), 1)
        dq_scr[...] = jnp.zeros_like(dq_scr)
        dfq_ref[...] = jnp.zeros_like(dfq_ref)

        def kvblock(j, c):
            c0 = pl.multiple_of(j * TQ, TQ)
            k = k_ref[pl.ds(c0, TQ), :]
            v = v_ref[pl.ds(c0, TQ), :]
            kf = k.astype(f32)
            km = [jnp.where(hm[hh], kf, 0.0).astype(bf16) for hh in (0, 1)]
            frow = ftb_ref[j]
            fr = [jnp.sum(jnp.where(sub8 == 2 * p + hh, frow, 0.0), axis=0, keepdims=True) for hh in (0, 1)]

            def qblock(i, carry):
                dk, dv, df0, df1 = carry
                df = [df0, df1]
                r0 = pl.multiple_of(i * TQ, TQ)
                q = q_ref[pl.ds(r0, TQ), :].astype(f32) * 0.125
                do = do_ref[pl.ds(r0, TQ), :].astype(f32)
                prod = do * o_ref[pl.ds(r0, TQ), :].astype(f32)
                lseb = lse_ref[pl.ds(r0, TQ), :]
                Fb = F_ref[pl.ds(r0, TQ), :]
                causal = (c0 + coli) <= (r0 + rowi)
                dqacc = jnp.zeros((TQ, 128), f32)
                rowsum = jnp.zeros((TQ, 128), f32)
                for hh in (0, 1):
                    qh = jnp.where(hm[hh], q, 0.0).astype(bf16)
                    doh = jnp.where(hm[hh], do, 0.0).astype(bf16)
                    delta = jnp.sum(jnp.where(hm[hh], prod, 0.0), axis=1, keepdims=True)
                    fc = jnp.sum(jnp.where(lane == 2 * p + hh, Fb, 0.0), axis=1, keepdims=True)
                    s = _dot_nt(qh, k) + (fc - fr[hh])
                    pr = jnp.where(causal, jnp.exp(s - lseb[:, hh * HD:hh * HD + 1]), 0.0)
                    dp = _dot_nt(doh, v)
                    ds = pr * (dp - delta)
                    dsb = ds.astype(bf16)
                    dv = dv + _dot_tn(pr.astype(bf16), doh)
                    dk = dk + _dot_tn(dsb, qh)
                    dqacc = dqacc + _dot(dsb, km[hh])
                    df[hh] = df[hh] - jnp.sum(ds, axis=0, keepdims=True)
                    rowsum = jnp.where(hm[hh], jnp.sum(ds, axis=1, keepdims=True), rowsum)
                dq_scr[pl.ds(r0, TQ), :] += dqacc * 0.125
                dfq_ref[pl.ds(r0, TQ), :] += rowsum
                return dk, dv, df[0], df[1]

            z = jnp.zeros((TQ, 128), f32)
            zr = jnp.zeros((1, TQ), f32)
            dk, dv, df0, df1 = lax.fori_loop(j, nb, qblock, (z, z, zr, zr))
            dk_ref[pl.ds(c0, TQ), :] = dk.astype(bf16)
            dv_ref[pl.ds(c0, TQ), :] = dv.astype(bf16)
            dft_ref[j] = jnp.where(sub8 == 0, df0, jnp.where(sub8 == 1, df1, 0.0))
            return c

        lax.fori_loop(0, nb, kvblock, 0)
        dq_ref[...] = dq_scr[...].astype(bf16)

    blk = lambda off: pl.BlockSpec((S, 128), lambda p: (0, off + p))
    return pl.pallas_call(
        body, name="fox_bwd", grid=(4,),
        in_specs=[blk(0), blk(4), blk(8), blk(0), blk(0), blk(0), pl.BlockSpec((S, 128), lambda p: (0, 0)),
                  pl.BlockSpec((nb, 8, TQ), lambda p: (0, 0, 0))],
        out_specs=[blk(0), blk(0), blk(0), pl.BlockSpec((None, nb, 8, TQ), lambda p: (p, 0, 0, 0)), blk(0)],
        out_shape=[jax.ShapeDtypeStruct((S, FOXW), bf16)] * 3 + [jax.ShapeDtypeStruct((4, nb, 8, TQ), f32),
                                                                 jax.ShapeDtypeStruct((S, FOXW), f32)],
        scratch_shapes=[pltpu.VMEM((S, 128), f32)],
        compiler_params=_params(("parallel",)),
    )(qkva, qkva, qkva, doa, oa, lse, F, ftb)


def _forget_bwd(dft, dfq, fa, bpad):
    nb = S // TQ

    def body(dft_ref, dfq_ref, fa_ref, b_ref, dfa_ref, db_ref, rows):
        rr = lax.broadcasted_iota(jnp.int32, (TQ, TQ), 0)
        cc = lax.broadcasted_iota(jnp.int32, (TQ, TQ), 1)
        upper = (cc >= rr).astype(bf16)
        ones = jnp.ones((8, TQ), bf16)
        lane = lax.broadcasted_iota(jnp.int32, (1, 128), 1)
        carry = jnp.zeros((1, 128), f32)
        db = jnp.zeros((1, 128), f32)
        rows[...] = jnp.zeros_like(rows)
        for b in reversed(range(nb)):
            for p in range(4):
                rows[2 * p:2 * p + 2, :] = dft_ref[p, b, 0:2, :]
            cols = jnp.zeros((TQ, 128), f32)
            for h in range(8):
                c0 = (h // 2) * 128 + (h % 2) * HD
                cols = jnp.where(lane == h, dfq_ref[b * TQ:(b + 1) * TQ, c0:c0 + 1], cols)
            dlf = carry
            tot = jnp.zeros((8, 128), f32)
            for part in _split3(rows[...]):
                dlf = dlf + _dot_nt(upper, part)
                tot = tot + _dot_nt(ones, part)
            for part in _split3(cols):
                dlf = dlf + _dot(upper, part)
            carry = carry + tot[0:1, :] + jnp.sum(cols, axis=0, keepdims=True)
            z = fa_ref[b * TQ:(b + 1) * TQ, :] + b_ref[...]
            dz = jnp.where(lane < 8, dlf * jax.nn.sigmoid(-z), 0.0)
            dfa_ref[b * TQ:(b + 1) * TQ, :] = dz.astype(bf16)
            db = db + jnp.sum(dz, axis=0, keepdims=True)
        db_ref[...] = db

    return pl.pallas_call(
        body, name="forget_bwd",
        out_shape=[jax.ShapeDtypeStruct((S, 128), bf16), jax.ShapeDtypeStruct((1, 128), f32)],
        scratch_shapes=[pltpu.VMEM((128, TQ), f32)],
        compiler_params=_params(),
    )(dft, dfq, fa, bpad)


def _dil_bwd(qkvb, dob, ob, lseb, rope):
    c_t, s1_t, s2_t = rope

    def body(*refs):
        q_refs, k_refs, v_refs = refs[0:3], refs[3:6], refs[6:9]
        dob_ref, ob_ref, lse_ref, c_ref, s1_ref, s2_ref = refs[9:15]
        dq_out, dk_out, dv_out = refs[15:18], refs[18:21], refs[21:24]
        qp, kp, vp, dop, lp, dlp, dln, dqp, dkp, dvp, nat = refs[24:35]
        _, hm = _head_masks()

        def delta_rows(i, c):
            r0 = pl.multiple_of(i * TQ, TQ)
            prod = dob_ref[pl.ds(r0, TQ), :] * ob_ref[pl.ds(r0, TQ), :].astype(f32)
            d0 = jnp.sum(jnp.where(hm[0], prod, 0.0), axis=1, keepdims=True)
            d1 = jnp.sum(jnp.where(hm[1], prod, 0.0), axis=1, keepdims=True)
            dln[pl.ds(r0, TQ), :] = jnp.where(hm[0], d0, d1)
            return c

        lax.fori_loop(0, S // TQ, delta_rows, 0)

        for g, r in enumerate(DIL):
            nbl = S // r // BAND
            if r == 1:
                srcs = (q_refs[g], k_refs[g], v_refs[g], dob_ref, lse_ref, dln)
            else:
                for dst, src in ((qp, q_refs[g]), (kp, k_refs[g]), (vp, v_refs[g]), (dop, dob_ref),
                                 (lp, lse_ref), (dlp, dln)):
                    _permute_in(dst, src, r)
                srcs = (qp, kp, vp, dop, lp, dlp)
            dkp[...] = jnp.zeros_like(dkp)
            dvp[...] = jnp.zeros_like(dvp)

            def blk(bb, c, srcs=srcs, nbl=nbl):
                qs_, ks_, vs_, dos_, ls_, dls_ = srcs
                r0, k0, valid = _band_geometry(bb, nbl)
                q = qs_[pl.ds(r0, BAND), :] * 0.125
                kwf = ks_[pl.ds(k0, 2 * BAND), :]
                kw = kwf.astype(bf16)
                vw = vs_[pl.ds(k0, 2 * BAND), :].astype(bf16)
                do = dos_[pl.ds(r0, BAND), :]
                lse = ls_[pl.ds(r0, BAND), :]
                dlt = dls_[pl.ds(r0, BAND), :]
                dq = jnp.zeros((BAND, 128), f32)
                dk = jnp.zeros((2 * BAND, 128), f32)
                dv = jnp.zeros((2 * BAND, 128), f32)
                for hh in (0, 1):
                    qh = jnp.where(hm[hh], q, 0.0).astype(bf16)
                    doh = jnp.where(hm[hh], do, 0.0).astype(bf16)
                    kh = jnp.where(hm[hh], kwf, 0.0).astype(bf16)
                    s = _dot_nt(qh, kw)
                    pr = jnp.where(valid, jnp.exp(s - lse[:, hh * HD:hh * HD + 1]), 0.0)
                    dp = _dot_nt(doh, vw)
                    ds = pr * (dp - dlt[:, hh * HD:hh * HD + 1])
                    dsb = ds.astype(bf16)
                    dv = dv + _dot_tn(pr.astype(bf16), doh)
                    dk = dk + _dot_tn(dsb, qh)
                    dq = dq + _dot(dsb, kh)
                dqp[pl.ds(r0, BAND), :] = dq * 0.125
                dkp[pl.ds(k0, 2 * BAND), :] += dk
                dvp[pl.ds(k0, 2 * BAND), :] += dv
                return c

            lax.fori_loop(0, S // BAND, blk, 0)

            for acc, out, roped in ((dqp, dq_out[g], True), (dkp, dk_out[g], True), (dvp, dv_out[g], False)):
                if r == 1:
                    src = acc
                else:
                    _permute_out(nat, acc, r)
                    src = nat

                def emit(i, c, src=src, out=out, roped=roped):
                    r0 = pl.multiple_of(i * TQ, TQ)
                    d = src[pl.ds(r0, TQ), :]
                    if roped:
                        d = (d * c_ref[pl.ds(r0, TQ), :] + pltpu.roll(d * s1_ref[pl.ds(r0, TQ), :], 8, 1)
                             + pltpu.roll(d * s2_ref[pl.ds(r0, TQ), :], 120, 1))
                    out[pl.ds(r0, TQ), :] = d.astype(bf16)
                    return c

                lax.fori_loop(0, S // TQ, emit, 0)

    pair = pl.BlockSpec((S, 128), lambda p: (0, p))
    tab = pl.BlockSpec((S, 128), lambda p: (0, 0))
    return pl.pallas_call(
        body, name="dil_bwd", grid=(2,),
        in_specs=_dil_in_specs() + [pair, pair, pair, tab, tab, tab],
        out_specs=[pair] * 9,
        out_shape=[jax.ShapeDtypeStruct((S, DILOUT), bf16)] * 9,
        scratch_shapes=[pltpu.VMEM((S, 128), f32)] * 11,
        compiler_params=_params(("parallel",)),
    )(*([qkvb] * 9), dob, ob, lseb, c_t, s1_t, s2_t)


def _inproj_bwd(dproj, wp, x, dx2, g1):
    tm = 256

    def body(d_ref, w_ref, x_ref, dx2_ref, g_ref, dx_ref, dg_ref):
        i = pl.program_id(0)
        dh = _dot_nt(d_ref[...], w_ref[...])
        xb = x_ref[...]
        r = lax.rsqrt(jnp.mean(xb * xb, axis=-1, keepdims=True) + EPS)
        xh = xb * r
        gdh = dh * g_ref[...]
        dx_ref[...] = dx2_ref[...] + r * (gdh - xh * jnp.mean(gdh * xh, axis=-1, keepdims=True))

        @pl.when(i == 0)
        def _():
            dg_ref[...] = jnp.zeros_like(dg_ref)

        dg_ref[...] += jnp.sum(dh * xh, axis=0, keepdims=True)

    row = pl.BlockSpec((tm, D), lambda i: (i, 0))
    vec = pl.BlockSpec((1, D), lambda i: (0, 0))
    return pl.pallas_call(
        body, name="inproj_bwd", grid=(S // tm,),
        in_specs=[pl.BlockSpec((tm, WP), lambda i: (i, 0)), pl.BlockSpec((D, WP), lambda i: (0, 0)), row, row, vec],
        out_specs=[row, vec],
        out_shape=[jax.ShapeDtypeStruct((S, D), f32), jax.ShapeDtypeStruct((1, D), f32)],
        compiler_params=_params(("arbitrary",)),
    )(dproj, wp, x, dx2, g1)


def _pad_win(w_full):
    return jnp.concatenate([w_full[:, 0:1536], w_full[:, 1544:D_IN], w_full[:, 1536:1544],
                            jnp.zeros((D, WP - D_IN), w_full.dtype)], axis=1)


def _unpad_win(dwp):
    return jnp.concatenate([dwp[:, 0:1536], dwp[:, C_FA:C_FA + 8], dwp[:, 1536:C_FA]], axis=1)


def _local_step(x, tgt, g1, bf, g2, g3, wp, was, wbs, wout, wups, wdown):
    rope = _rope_tables()
    bpad = jnp.pad(bf, ((0, 0), (0, 120)))
    h1, qkva, qkvb, gates, fa = _norm_inproj(x, g1, wp, rope)
    F, ftb = _forget_cumsum(fa, bpad)
    oa, lsea = _fox_fwd(qkva, F, ftb)
    ob, lseb = _dil_fwd(qkvb)
    ya, yb, mixed = _branch_mix(oa, ob, was, wbs, gates)
    x2, h2 = _outproj_norm(mixed, wout, x, g2)
    u, a = _mlp_up(h2, wups)
    dx3, dx3b, dg3, loss = _mlp_down_loss(a, wdown, x2, g3.reshape(1, D), tgt)

    du = _mlp_down_bwd(dx3b, wdown, u)
    dwdown = _mm(a, dx3b, "tn", f32, 1024, D, "wgrad_down")
    dwup = _mm(h2, du, "tn", f32, D, 1024, "wgrad_up", stack_cols=True)
    dx2, dx2b, dg2 = _mlp_up_bwd(du, wups, x2, dx3, g2)
    dya, dyb, dgates = _gate_bwd(dx2b, wout, gates, ya, yb)
    dwout = _mm(mixed, dx2b, "tn", f32, D, D, "wgrad_out")
    doa, dob = _branch_bwd(dya, dyb, was, wbs)
    dwas, dwbs = _branch_wgrad(oa, ob, dya, dyb)
    dqa, dka, dva, dft, dfq = _fox_bwd(qkva, doa, oa, lsea, F, ftb)
    dfa, dbf = _forget_bwd(dft, dfq, fa, bpad)
    dd = _dil_bwd(qkvb, dob, ob, lseb, rope)
    dq_g, dk_g, dv_g = dd[0:3], dd[3:6], dd[6:9]
    dproj = jnp.concatenate([dqa, dka, dva, *dq_g, *dk_g, *dv_g, dgates, dfa], axis=1)
    dwp = _mm(h1, dproj, "tn", f32, D, 128, "wgrad_in")
    gx, dg1 = _inproj_bwd(dproj, wp, x, dx2, g1)
    grads = dict(win=_unpad_win(dwp), was=dwas, wbs=dwbs, wout=dwout, wup=dwup, wdown=dwdown,
                 g1=dg1, bf=dbf[:, 0:8], g2=dg2, g3=dg3)
    return loss[0, 0], gx, grads


HBM = pl.BlockSpec(memory_space=pltpu.HBM)
SMALL_ROWS = 8


def _place():
    x, y, c = lax.axis_index("x"), lax.axis_index("y"), lax.axis_index("c")
    chips = [(1 - x, y), (x, 1 - y), (1 - x, 1 - y)]
    return x, y, c, chips


def _rcopy(src, dst, ssem, rsem, dev):
    return pltpu.make_async_remote_copy(src_ref=src, dst_ref=dst, send_sem=ssem, recv_sem=rsem,
                                        device_id=dev, device_id_type=pl.DeviceIdType.MESH)


def _half(nrows, which):
    return pl.ds(which * (nrows // 2), nrows // 2)


def _allgather(shards):
    n = len(shards)

    def body(*refs):
        src, out = refs[:n], refs[n:2 * n]
        ssem, rsem, fssem, frsem, lsem = refs[2 * n:]
        x, y, c, chips = _place()
        me_q = 2 * x + y
        sib = (x, y, 1 - c)
        local = [pltpu.make_async_copy(src[a], out[a].at[me_q], lsem.at[a]) for a in range(n)]
        for cp in local:
            cp.start()
        sends = []
        for a in range(n):
            rows = _half(shards[a].shape[0], c)
            for j, (cx, cy) in enumerate(chips):
                cp = _rcopy(src[a].at[rows], out[a].at[me_q, rows], ssem.at[a * 3 + j], rsem.at[a * 3 + j],
                            (cx, cy, c))
                cp.start()
                sends.append(cp)
        for a in range(n):
            rows = _half(shards[a].shape[0], c)
            for j, (cx, cy) in enumerate(chips):
                landed = out[a].at[2 * cx + cy, rows]
                _rcopy(landed, landed, ssem.at[a * 3 + j], rsem.at[a * 3 + j], (cx, cy, c)).wait_recv()
                cp = _rcopy(landed, landed, fssem.at[a * 3 + j], frsem.at[a * 3 + j], sib)
                cp.start()
                sends.append(cp)
        for a in range(n):
            rows = _half(shards[a].shape[0], 1 - c)
            for j, (cx, cy) in enumerate(chips):
                other = out[a].at[2 * cx + cy, rows]
                _rcopy(other, other, fssem.at[a * 3 + j], frsem.at[a * 3 + j], sib).wait_recv()
        for cp in sends:
            cp.wait_send()
        for cp in local:
            cp.wait()

    return pl.pallas_call(
        body, name="weights_allgather",
        in_specs=[HBM] * n, out_specs=[HBM] * n,
        out_shape=[jax.ShapeDtypeStruct((NCHIP,) + s.shape, s.dtype) for s in shards],
        scratch_shapes=[pltpu.SemaphoreType.DMA((3 * n,))] * 4 + [pltpu.SemaphoreType.DMA((n,))],
        compiler_params=pltpu.CompilerParams(has_side_effects=True),
    )(*shards)


def _pair_exchange(gs):
    n = len(gs)

    def body(*refs):
        g, t = refs[:n], refs[n:2 * n]
        ssem, rsem = refs[2 * n:]
        x, y, c, _ = _place()
        sib = (x, y, 1 - c)
        cps = []
        for a in range(n):
            rows = _half(gs[a].shape[1], 1 - c)
            cp = _rcopy(g[a].at[:, rows, :], t[a], ssem.at[a], rsem.at[a], sib)
            cp.start()
            cps.append(cp)
        for cp in cps:
            cp.wait_recv()
        for cp in cps:
            cp.wait_send()

    return pl.pallas_call(
        body, name="grad_pair_exchange",
        in_specs=[HBM] * n, out_specs=[HBM] * n,
        out_shape=[jax.ShapeDtypeStruct((NCHIP, g.shape[1] // 2, g.shape[2]), f32) for g in gs],
        scratch_shapes=[pltpu.SemaphoreType.DMA((n,))] * 2,
        compiler_params=pltpu.CompilerParams(has_side_effects=True),
    )(*gs)


def _row_tile(h):
    return min(h, 256)


def _pair_add(g, t, c_arr, name):
    _, R, C = g.shape
    h = R // 2
    tr = _row_tile(h)
    nblk = h // tr

    def body(c_ref, g_ref, t_ref, p32_ref, p16_ref):
        s = g_ref[...] + t_ref[...]
        p32_ref[...] = s
        p16_ref[...] = s.astype(bf16)

    blk = pl.BlockSpec((None, tr, C), lambda q, i, c_ref: (q, i, 0))
    return pl.pallas_call(
        body, name=name,
        grid_spec=pltpu.PrefetchScalarGridSpec(
            num_scalar_prefetch=1, grid=(NCHIP, nblk),
            in_specs=[pl.BlockSpec((None, tr, C), lambda q, i, c_ref: (q, c_ref[0] * nblk + i, 0)), blk],
            out_specs=[blk, blk]),
        out_shape=[jax.ShapeDtypeStruct((NCHIP, h, C), f32), jax.ShapeDtypeStruct((NCHIP, h, C), bf16)],
        compiler_params=_params(("parallel", "parallel")),
    )(c_arr, g, t)


def _shard_exchange(p16s, small):
    n = len(p16s)

    def body(*refs):
        p, small_ref = refs[:n], refs[n]
        r, sm = refs[n + 1:2 * n + 1], refs[2 * n + 1]
        ssem, rsem, sssem, srsem, lsem = refs[2 * n + 2:]
        x, y, c, chips = _place()
        me = 4 * x + 2 * y + c
        local = pltpu.make_async_copy(small_ref, sm.at[me], lsem)
        local.start()
        cps = []
        for a in range(n):
            for j, (cx, cy) in enumerate(chips):
                cp = _rcopy(p[a].at[2 * cx + cy], r[a].at[j], ssem.at[a * 3 + j], rsem.at[a * 3 + j], (cx, cy, c))
                cp.start()
                cps.append(cp)
        for k in range(1, 8):
            px = (1 - x) if k & 4 else x
            py = (1 - y) if k & 2 else y
            pc = (1 - c) if k & 1 else c
            cp = _rcopy(small_ref, sm.at[me], sssem.at[k - 1], srsem.at[k - 1], (px, py, pc))
            cp.start()
            cps.append(cp)
        for a in range(n):
            for j in range(3):
                _rcopy(r[a].at[j], r[a].at[j], ssem.at[a * 3 + j], rsem.at[a * 3 + j], (x, y, c)).wait_recv()
        for k in range(1, 8):
            px = (1 - x) if k & 4 else x
            py = (1 - y) if k & 2 else y
            pc = (1 - c) if k & 1 else c
            slot = sm.at[4 * px + 2 * py + pc]
            _rcopy(slot, slot, sssem.at[k - 1], srsem.at[k - 1], (px, py, pc)).wait_recv()
        for cp in cps:
            cp.wait_send()
        local.wait()

    return pl.pallas_call(
        body, name="grad_shard_exchange",
        in_specs=[HBM] * (n + 1), out_specs=[HBM] * (n + 1),
        out_shape=[jax.ShapeDtypeStruct((3,) + p.shape[1:], bf16) for p in p16s]
        + [jax.ShapeDtypeStruct((8, SMALL_ROWS, D), f32)],
        scratch_shapes=[pltpu.SemaphoreType.DMA((3 * n,))] * 2 + [pltpu.SemaphoreType.DMA((7,))] * 2
        + [pltpu.SemaphoreType.DMA(())],
        compiler_params=pltpu.CompilerParams(has_side_effects=True),
    )(*p16s, small)


def _shard_sum(p32, r, q_arr, name):
    _, h, C = p32.shape
    tr = _row_tile(h)

    def body(q_ref, p_ref, r_ref, o_ref):
        s = p_ref[...]
        for j in range(3):
            s = s + r_ref[j].astype(f32)
        o_ref[...] = s

    return pl.pallas_call(
        body, name=name,
        grid_spec=pltpu.PrefetchScalarGridSpec(
            num_scalar_prefetch=1, grid=(h // tr,),
            in_specs=[pl.BlockSpec((None, tr, C), lambda i, q_ref: (q_ref[0], i, 0)),
                      pl.BlockSpec((3, tr, C), lambda i, q_ref: (0, i, 0))],
            out_specs=pl.BlockSpec((tr, C), lambda i, q_ref: (i, 0))),
        out_shape=jax.ShapeDtypeStruct((h, C), f32),
        compiler_params=_params(("parallel",)),
    )(q_arr, p32, r)


def _half_swap(halves):
    n = len(halves)

    def body(*refs):
        hv, full = refs[:n], refs[n:2 * n]
        ssem, rsem, lsem = refs[2 * n:]
        x, y, c, _ = _place()
        sib = (x, y, 1 - c)
        cps, local = [], []
        for a in range(n):
            mine = full[a].at[_half(2 * halves[a].shape[0], c)]
            lc = pltpu.make_async_copy(hv[a], mine, lsem.at[a])
            lc.start()
            local.append(lc)
            cp = _rcopy(hv[a], mine, ssem.at[a], rsem.at[a], sib)
            cp.start()
            cps.append(cp)
        for a in range(n):
            other = full[a].at[_half(2 * halves[a].shape[0], 1 - c)]
            _rcopy(other, other, ssem.at[a], rsem.at[a], sib).wait_recv()
        for cp in cps:
            cp.wait_send()
        for lc in local:
            lc.wait()

    return pl.pallas_call(
        body, name="grad_half_swap",
        in_specs=[HBM] * n, out_specs=[HBM] * n,
        out_shape=[jax.ShapeDtypeStruct((2 * hv.shape[0], hv.shape[1]), f32) for hv in halves],
        scratch_shapes=[pltpu.SemaphoreType.DMA((n,))] * 3,
        compiler_params=pltpu.CompilerParams(has_side_effects=True),
    )(*halves)


def _small_sum(sm):
    def body(sm_ref, o_ref):
        s = sm_ref[0]
        for d in range(1, 8):
            s = s + sm_ref[d]
        o_ref[...] = s

    return pl.pallas_call(body, name="small_grad_sum", out_shape=jax.ShapeDtypeStruct((SMALL_ROWS, D), f32))(sm)


def _adamw(w, g, m, v, name):
    R, C = w.shape
    tr = min(R, 256)

    def body(w_ref, g_ref, m_ref, v_ref, d_ref, nm_ref, nv_ref):
        g_ = g_ref[...]
        m_ = ADAM_B1 * m_ref[...] + (1.0 - ADAM_B1) * g_
        v_ = ADAM_B2 * v_ref[...] + (1.0 - ADAM_B2) * (g_ * g_)
        m_hat = m_ / (1.0 - ADAM_B1 ** ADAM_STEP)
        v_hat = v_ / (1.0 - ADAM_B2 ** ADAM_STEP)
        d_ref[...] = -ADAM_LR * (m_hat / (jnp.sqrt(v_hat) + ADAM_EPS) + ADAM_WD * w_ref[...])
        nm_ref[...] = m_
        nv_ref[...] = v_

    blk = pl.BlockSpec((tr, C), lambda i: (i, 0))
    return pl.pallas_call(
        body, name=name, grid=(R // tr,), in_specs=[blk] * 4, out_specs=[blk] * 3,
        out_shape=[jax.ShapeDtypeStruct((R, C), f32)] * 3,
        compiler_params=_params(("parallel",)),
    )(w, g, m, v)


def _stack_cols(w, n):
    K_, N = w.shape
    return jnp.stack([w[:, q * (N // n):(q + 1) * (N // n)] for q in range(n)], axis=0)


def kernel(x, norm_attn_g, w_in, b_forget, w_branch_a, w_branch_b, w_out, norm_mlp_g, w_up, w_down, norm_final_g, loss_target, m_norm_attn_g, m_w_in, m_b_forget, m_w_branch_a, m_w_branch_b, m_w_out, m_norm_mlp_g, m_w_up, m_w_down, m_norm_final_g, v_norm_attn_g, v_w_in, v_b_forget, v_w_branch_a, v_w_branch_b, v_w_out, v_norm_mlp_g, v_w_up, v_w_down, v_norm_final_g):
    xi, yi, ci = lax.axis_index("x"), lax.axis_index("y"), lax.axis_index("c")
    c_arr = jnp.reshape(ci, (1,)).astype(jnp.int32)
    q_arr = jnp.reshape(2 * xi + yi, (1,)).astype(jnp.int32)

    big = [w_in[0], w_branch_a[0], w_branch_b[0], w_out[0], w_up[0], w_down[0]]
    names = ["w_in", "w_branch_a", "w_branch_b", "w_out", "w_up", "w_down"]
    wins, was, wbs, wouts, wups, wdowns = _allgather([w.astype(bf16) for w in big])
    wp = _pad_win(jnp.concatenate([wins[q] for q in range(NCHIP)], axis=1))
    wout = wouts.reshape(D, D)
    wdown = wdowns.reshape(DFF, D)

    loss_part, gx, g = _local_step(x[0], loss_target[0], norm_attn_g, b_forget, norm_mlp_g, norm_final_g,
                                   wp, was, wbs, wout, wups, wdown)
    loss = lax.psum(loss_part, ("x", "y", "c"))

    gs = [_stack_cols(g["win"], NCHIP), g["was"], g["wbs"], g["wout"].reshape(NCHIP, D // NCHIP, D), g["wup"],
          g["wdown"].reshape(NCHIP, DFF // NCHIP, D)]
    ts = _pair_exchange(gs)
    p32s, p16s = zip(*[_pair_add(gs[a], ts[a], c_arr, "pair_add_" + names[a]) for a in range(6)])
    small = jnp.concatenate([g["g1"], g["g2"], g["g3"], jnp.pad(g["bf"], ((0, 0), (0, D - 8))),
                             jnp.zeros((SMALL_ROWS - 4, D), f32)], axis=0)
    *rs, sm = _shard_exchange(list(p16s), small)
    halves = [_shard_sum(p32s[a], rs[a], q_arr, "shard_sum_" + names[a]) for a in range(6)]
    gfull = _half_swap(halves)
    gsmall = _small_sum(sm)

    ms = [m_w_in[0], m_w_branch_a[0], m_w_branch_b[0], m_w_out[0], m_w_up[0], m_w_down[0]]
    vs = [v_w_in[0], v_w_branch_a[0], v_w_branch_b[0], v_w_out[0], v_w_up[0], v_w_down[0]]
    upd = {names[a]: _adamw(big[a], gfull[a], ms[a], vs[a], "adamw_" + names[a]) for a in range(6)}
    grad = {names[a]: gfull[a] for a in range(6)}

    grad["norm_attn_g"], grad["norm_mlp_g"] = gsmall[0:1], gsmall[1:2]
    grad["norm_final_g"], grad["b_forget"] = gsmall[2:3], gsmall[3:4, 0:8]
    upd["norm_attn_g"] = _adamw(norm_attn_g, grad["norm_attn_g"], m_norm_attn_g, v_norm_attn_g, "adamw_g1")
    upd["norm_mlp_g"] = _adamw(norm_mlp_g, grad["norm_mlp_g"], m_norm_mlp_g, v_norm_mlp_g, "adamw_g2")
    upd["norm_final_g"] = _adamw(norm_final_g.reshape(1, D), grad["norm_final_g"], m_norm_final_g.reshape(1, D),
                                 v_norm_final_g.reshape(1, D), "adamw_g3")
    upd["b_forget"] = _adamw(b_forget, grad["b_forget"], m_b_forget, v_b_forget, "adamw_bf")

    order = ["norm_attn_g", "w_in", "b_forget", "w_branch_a", "w_branch_b", "w_out", "norm_mlp_g", "w_up", "w_down",
             "norm_final_g"]
    shapes = dict(norm_attn_g=norm_attn_g.shape, w_in=w_in.shape, b_forget=b_forget.shape,
                  w_branch_a=w_branch_a.shape, w_branch_b=w_branch_b.shape, w_out=w_out.shape,
                  norm_mlp_g=norm_mlp_g.shape, w_up=w_up.shape, w_down=w_down.shape, norm_final_g=norm_final_g.shape)
    outs = [loss, gx.reshape(x.shape)]
    outs += [grad[nm].reshape(shapes[nm]) for nm in order]
    for k in range(3):
        outs += [upd[nm][k].reshape(shapes[nm]) for nm in order]
    return tuple(outs)
```

```python
import functools

import jax
import jax.numpy as jnp
import numpy as np
from jax import lax
from jax.experimental import pallas as pl
from jax.experimental.pallas import tpu as pltpu

f32 = jnp.float32
bf16 = jnp.bfloat16

S = 2048
D = 1024
DFF = 4096
HD = 64
FOXW = 512
DILW = 768
DILOUT = 256
DIL = (1, 4, 16)
BAND = 128
EPS = 1e-6
NEG = -1e30
ROPE_THETA = 500000.0
NCHIP = 4

C_QKVA, C_QB, C_KB, C_VB, C_G, C_FA, WP = 0, 1536, 2304, 3072, 3840, 5888, 6016
D_IN = 5896
SHARD_IN = 1474

ADAM_LR, ADAM_B1, ADAM_B2, ADAM_EPS, ADAM_WD, ADAM_STEP = 0.001, 0.9, 0.999, 1e-08, 0.01, 10

VMEM_LIMIT = 56 * 1024 * 1024
TQ = 256


def _params(sem=None):
    return pltpu.CompilerParams(dimension_semantics=sem, vmem_limit_bytes=VMEM_LIMIT)


def _dot(a, b):
    return jnp.dot(a, b, preferred_element_type=f32)


def _dot_nt(a, b):
    return lax.dot_general(a, b, (((1,), (1,)), ((), ())), preferred_element_type=f32)


def _dot_tn(a, b):
    return lax.dot_general(a, b, (((0,), (0,)), ((), ())), preferred_element_type=f32)


def _split3(x):
    hi = x.astype(bf16)
    r1 = x - hi.astype(f32)
    mid = r1.astype(bf16)
    lo = (r1 - mid.astype(f32)).astype(bf16)
    return hi, mid, lo


def _rope_tables():
    half = 8
    inv_freq = jnp.power(jnp.float32(ROPE_THETA), -jnp.arange(half, dtype=f32) * 2.0 / 16)
    ang = jnp.arange(S).astype(f32)[:, None] * inv_freq[None, :]
    cos, sin = jnp.cos(ang), jnp.sin(ang)
    one = jnp.ones((S, HD - 16), f32)
    zero = jnp.zeros((S, HD - 16), f32)
    z8 = jnp.zeros((S, 8), f32)
    c = jnp.concatenate([cos, cos, one], axis=1)
    s1 = jnp.concatenate([-sin, z8, zero], axis=1)
    s2 = jnp.concatenate([z8, sin, zero], axis=1)
    return tuple(jnp.concatenate([t, t], axis=1) for t in (c, s1, s2))


def _mm(a, b, mode, out_dtype, tm, tn, name, stack_cols=False):
    if mode == "nn":
        (M, K), (_, N) = a.shape, b.shape
        a_spec = pl.BlockSpec((tm, K), lambda i, j: (i, 0))
        b_spec = pl.BlockSpec((K, tn), lambda i, j: (0, j))
        dot = _dot
    elif mode == "nt":
        (M, K), (N, _) = a.shape, b.shape
        a_spec = pl.BlockSpec((tm, K), lambda i, j: (i, 0))
        b_spec = pl.BlockSpec((tn, K), lambda i, j: (j, 0))
        dot = _dot_nt
    else:
        (K, M), (_, N) = a.shape, b.shape
        a_spec = pl.BlockSpec((K, tm), lambda i, j: (0, i))
        b_spec = pl.BlockSpec((K, tn), lambda i, j: (0, j))
        dot = _dot_tn

    def body(a_ref, b_ref, o_ref):
        o_ref[...] = dot(a_ref[...], b_ref[...]).astype(out_dtype)

    if stack_cols:
        assert tm == M
        out_spec = pl.BlockSpec((None, tm, tn), lambda i, j: (j, 0, 0))
        out_shape = jax.ShapeDtypeStruct((N // tn, M, tn), out_dtype)
    else:
        out_spec = pl.BlockSpec((tm, tn), lambda i, j: (i, j))
        out_shape = jax.ShapeDtypeStruct((M, N), out_dtype)
    return pl.pallas_call(
        body, name=name, grid=(M // tm, N // tn), in_specs=[a_spec, b_spec],
        out_specs=out_spec, out_shape=out_shape,
        compiler_params=_params(("parallel", "parallel")),
    )(a, b)


def _norm_inproj(x, g1, wp, rope):
    tm = 256
    c_t, s1_t, s2_t = rope

    def body(x_ref, g_ref, w_ref, c_ref, s1_ref, s2_ref, h_ref, qkva_ref, qkvb_ref, gates_ref, fa_ref):
        xb = x_ref[...]
        r = lax.rsqrt(jnp.mean(xb * xb, axis=-1, keepdims=True) + EPS)
        h = ((xb * r) * g_ref[...]).astype(bf16)
        h_ref[...] = h
        qkva_ref[...] = _dot(h, w_ref[:, C_QKVA:C_QB]).astype(bf16)
        c, s1, s2 = c_ref[...], s1_ref[...], s2_ref[...]
        for sec, lo in enumerate((C_QB, C_KB)):
            pb = _dot(h, w_ref[:, lo:lo + DILW])
            for ch in range(DILW // 128):
                pc = pb[:, ch * 128:(ch + 1) * 128]
                roped = pc * c + pltpu.roll(pc, 120, 1) * s1 + pltpu.roll(pc, 8, 1) * s2
                qkvb_ref[:, sec * DILW + ch * 128: sec * DILW + (ch + 1) * 128] = roped
        qkvb_ref[:, 2 * DILW:3 * DILW] = _dot(h, w_ref[:, C_VB:C_G])
        gates_ref[...] = _dot(h, w_ref[:, C_G:C_FA])
        fa_ref[...] = _dot(h, w_ref[:, C_FA:WP])

    row = lambda w: pl.BlockSpec((tm, w), lambda i: (i, 0))
    return pl.pallas_call(
        body, name="norm_inproj", grid=(S // tm,),
        in_specs=[row(D), pl.BlockSpec((1, D), lambda i: (0, 0)), pl.BlockSpec((D, WP), lambda i: (0, 0)),
                  row(128), row(128), row(128)],
        out_specs=[row(D), row(3 * FOXW), row(3 * DILW), row(2 * D), row(128)],
        out_shape=[jax.ShapeDtypeStruct((S, D), bf16), jax.ShapeDtypeStruct((S, 3 * FOXW), bf16),
                   jax.ShapeDtypeStruct((S, 3 * DILW), f32), jax.ShapeDtypeStruct((S, 2 * D), f32),
                   jax.ShapeDtypeStruct((S, 128), f32)],
        compiler_params=_params(("parallel",)),
    )(x, g1, wp, c_t, s1_t, s2_t)


def _forget_cumsum(fa, bpad):
    nb = S // TQ

    def body(fa_ref, b_ref, F_ref, ftb_ref):
        rr = lax.broadcasted_iota(jnp.int32, (TQ, TQ), 0)
        cc = lax.broadcasted_iota(jnp.int32, (TQ, TQ), 1)
        tri = (rr >= cc).astype(bf16)
        lane = lax.broadcasted_iota(jnp.int32, (1, 128), 1)
        carry = jnp.zeros((1, 128), f32)
        for b in range(nb):
            z = fa_ref[b * TQ:(b + 1) * TQ, :] + b_ref[...]
            lf = jnp.minimum(z, 0.0) - jnp.log(1.0 + jnp.exp(-jnp.abs(z)))
            lf = jnp.where(lane < 8, lf, 0.0)
            hi, mid, lo = _split3(lf)
            fb = (_dot(tri, hi) + _dot(tri, mid)) + _dot(tri, lo) + carry
            F_ref[b * TQ:(b + 1) * TQ, :] = fb
            ftb_ref[b] = fb.T[0:8, :]
            carry = fb[TQ - 1:TQ, :]

    return pl.pallas_call(
        body, name="forget_cumsum",
        out_shape=[jax.ShapeDtypeStruct((S, 128), f32), jax.ShapeDtypeStruct((nb, 8, TQ), f32)],
        compiler_params=_params(),
    )(fa, bpad)


def _head_masks():
    lane = lax.broadcasted_iota(jnp.int32, (1, 128), 1)
    return lane, (lane < HD, lane >= HD)


def _fox_fwd(qkva, F, ftb):
    nb = S // TQ

    def body(q_ref, k_ref, v_ref, F_ref, ftb_ref, o_ref, lse_ref):
        p = pl.program_id(0)
        lane, hm = _head_masks()
        sub8 = lax.broadcasted_iota(jnp.int32, (8, 1), 0)
        rowi = lax.broadcasted_iota(jnp.int32, (TQ, 1), 0)
        coli = lax.broadcasted_iota(jnp.int32, (1, TQ), 1)

        def qblock(i, c):
            r0 = pl.multiple_of(i * TQ, TQ)
            q = q_ref[pl.ds(r0, TQ), :].astype(f32) * 0.125
            qs = [jnp.where(hm[hh], q, 0.0).astype(bf16) for hh in (0, 1)]
            Fb = F_ref[pl.ds(r0, TQ), :]
            fc = [jnp.sum(jnp.where(lane == 2 * p + hh, Fb, 0.0), axis=1, keepdims=True) for hh in (0, 1)]

            def kvblock(j, carry):
                c0 = pl.multiple_of(j * TQ, TQ)
                k = k_ref[pl.ds(c0, TQ), :]
                v = v_ref[pl.ds(c0, TQ), :]
                frow = ftb_ref[j]
                causal = (c0 + coli) <= (r0 + rowi)
                new = []
                for hh in (0, 1):
                    m, l, a = carry[3 * hh:3 * hh + 3]
                    fr = jnp.sum(jnp.where(sub8 == 2 * p + hh, frow, 0.0), axis=0, keepdims=True)
                    s = _dot_nt(qs[hh], k) + (fc[hh] - fr)
                    s = jnp.where(causal, s, NEG)
                    mn = jnp.maximum(m, jnp.max(s, axis=1, keepdims=True))
                    al = jnp.exp(m - mn)
                    pr = jnp.exp(s - mn)
                    l = al * l + jnp.sum(pr, axis=1, keepdims=True)
                    a = al * a + _dot(pr.astype(bf16), v)
                    new += [mn, l, a]
                return tuple(new)

            init = (jnp.full((TQ, 1), NEG, f32), jnp.zeros((TQ, 1), f32), jnp.zeros((TQ, 128), f32)) * 2
            m0, l0, a0, m1, l1, a1 = lax.fori_loop(0, i + 1, kvblock, init)
            o = jnp.where(hm[0], a0 / l0, a1 / l1)
            lse = jnp.where(hm[0], m0 + jnp.log(l0), m1 + jnp.log(l1))
            o_ref[pl.ds(r0, TQ), :] = o.astype(bf16)
            lse_ref[pl.ds(r0, TQ), :] = lse
            return c

        lax.fori_loop(0, nb, qblock, 0)

    blk = lambda off: pl.BlockSpec((S, 128), lambda p: (0, off + p))
    return pl.pallas_call(
        body, name="fox_fwd", grid=(4,),
        in_specs=[blk(0), blk(4), blk(8), pl.BlockSpec((S, 128), lambda p: (0, 0)),
                  pl.BlockSpec((nb, 8, TQ), lambda p: (0, 0, 0))],
        out_specs=[blk(0), blk(0)],
        out_shape=[jax.ShapeDtypeStruct((S, FOXW), bf16), jax.ShapeDtypeStruct((S, FOXW), f32)],
        compiler_params=_params(("parallel",)),
    )(qkva, qkva, qkva, F, ftb)


def _permute_in(dst, src, r):
    L = S // r
    for rho in range(r):
        dst[rho * L:(rho + 1) * L, :] = src[pl.ds(rho, L, stride=r), :]


def _permute_out(dst, src, r):
    L = S // r
    for rho in range(r):
        dst[pl.ds(rho, L, stride=r), :] = src[rho * L:(rho + 1) * L, :]


def _band_geometry(bb, nbl):
    r0 = pl.multiple_of(bb * BAND, BAND)
    k0 = pl.multiple_of(jnp.maximum(bb - 1, 0) * BAND, BAND)
    sub0 = (bb - lax.rem(bb, nbl)) * BAND
    qi = r0 + lax.broadcasted_iota(jnp.int32, (BAND, 1), 0)
    ki = k0 + lax.broadcasted_iota(jnp.int32, (1, 2 * BAND), 1)
    diff = qi - ki
    valid = (diff >= 0) & (diff <= BAND) & (ki >= sub0)
    return r0, k0, valid


def _dil_in_specs():
    specs = []
    for role in range(3):
        for g in range(3):
            specs.append(pl.BlockSpec((S, 128), functools.partial(lambda p, o: (0, o + p), o=role * 6 + g * 2)))
    return specs


def _dil_fwd(qkvb):
    def body(*refs):
        q_refs, k_refs, v_refs = refs[0:3], refs[3:6], refs[6:9]
        ob_ref, lse_ref = refs[9:11]
        qp, kp, vp, op, lp = refs[11:16]
        on = refs[16:19]
        ln = refs[19:22]
        _, hm = _head_masks()
        for g, r in enumerate(DIL):
            nbl = S // r // BAND
            if r == 1:
                qs_, ks_, vs_, od, ld = q_refs[g], k_refs[g], v_refs[g], on[g], ln[g]
            else:
                _permute_in(qp, q_refs[g], r)
                _permute_in(kp, k_refs[g], r)
                _permute_in(vp, v_refs[g], r)
                qs_, ks_, vs_, od, ld = qp, kp, vp, op, lp

            def blk(bb, c, qs_=qs_, ks_=ks_, vs_=vs_, od=od, ld=ld, nbl=nbl):
                r0, k0, valid = _band_geometry(bb, nbl)
                q = qs_[pl.ds(r0, BAND), :] * 0.125
                kw = ks_[pl.ds(k0, 2 * BAND), :].astype(bf16)
                vw = vs_[pl.ds(k0, 2 * BAND), :]
                o = jnp.zeros((BAND, 128), f32)
                lse = jnp.zeros((BAND, 128), f32)
                for hh in (0, 1):
                    qh = jnp.where(hm[hh], q, 0.0).astype(bf16)
                    s = jnp.where(valid, _dot_nt(qh, kw), NEG)
                    m = jnp.max(s, axis=1, keepdims=True)
                    pr = jnp.exp(s - m)
                    l = jnp.sum(pr, axis=1, keepdims=True)
                    vm = jnp.where(hm[hh], vw, 0.0).astype(bf16)
                    o = o + _dot((pr / l).astype(bf16), vm)
                    lse = jnp.where(hm[hh], m + jnp.log(l), lse)
                od[pl.ds(r0, BAND), :] = o
                ld[pl.ds(r0, BAND), :] = lse
                return c

            lax.fori_loop(0, S // BAND, blk, 0)
            if r != 1:
                _permute_out(on[g], op, r)
                _permute_out(ln[g], lp, r)

        def combine(i, c):
            r0 = pl.multiple_of(i * TQ, TQ)
            ls = [ln[g][pl.ds(r0, TQ), :] for g in range(3)]
            mx = jnp.maximum(jnp.maximum(ls[0], ls[1]), ls[2])
            es = [jnp.exp(l - mx) for l in ls]
            tot = (es[0] + es[1]) + es[2]
            acc = (es[0] / tot) * on[0][pl.ds(r0, TQ), :]
            acc = acc + (es[1] / tot) * on[1][pl.ds(r0, TQ), :]
            acc = acc + (es[2] / tot) * on[2][pl.ds(r0, TQ), :]
            ob_ref[pl.ds(r0, TQ), :] = acc.astype(bf16)
            lse_ref[pl.ds(r0, TQ), :] = mx + jnp.log(tot)
            return c

        lax.fori_loop(0, S // TQ, combine, 0)

    out_blk = pl.BlockSpec((S, 128), lambda p: (0, p))
    return pl.pallas_call(
        body, name="dil_fwd", grid=(2,),
        in_specs=_dil_in_specs(), out_specs=[out_blk, out_blk],
        out_shape=[jax.ShapeDtypeStruct((S, DILOUT), bf16), jax.ShapeDtypeStruct((S, DILOUT), f32)],
        scratch_shapes=[pltpu.VMEM((S, 128), f32)] * 11,
        compiler_params=_params(("parallel",)),
    )(*([qkvb] * 9))


def _branch_mix(oa, ob, was, wbs, gates):
    tm = 512

    def body(oa_ref, ob_ref, wa_ref, wb_ref, g_ref, ya_ref, yb_ref, mix_ref):
        oa_b, ob_b = oa_ref[...], ob_ref[...]
        for q in range(NCHIP):
            cols = slice(q * 256, (q + 1) * 256)
            ya = _dot(oa_b, wa_ref[q])
            yb = _dot(ob_b, wb_ref[q])
            ya_ref[:, cols] = ya
            yb_ref[:, cols] = yb
            ga = g_ref[:, q * 256:(q + 1) * 256]
            gb = g_ref[:, D + q * 256:D + (q + 1) * 256]
            mix_ref[:, cols] = (jax.nn.sigmoid(ga) * ya + jax.nn.sigmoid(gb) * yb).astype(bf16)

    row = lambda w: pl.BlockSpec((tm, w), lambda i: (i, 0))
    full3 = lambda a: pl.BlockSpec(a.shape, lambda i: (0, 0, 0))
    return pl.pallas_call(
        body, name="branch_mix", grid=(S // tm,),
        in_specs=[row(FOXW), row(DILOUT), full3(was), full3(wbs), row(2 * D)],
        out_specs=[row(D), row(D), row(D)],
        out_shape=[jax.ShapeDtypeStruct((S, D), f32), jax.ShapeDtypeStruct((S, D), f32),
                   jax.ShapeDtypeStruct((S, D), bf16)],
        compiler_params=_params(("parallel",)),
    )(oa, ob, was, wbs, gates)


def _outproj_norm(mixed, wout, x, g2):
    tm = 512

    def body(m_ref, w_ref, x_ref, g_ref, x2_ref, h2_ref):
        x2 = x_ref[...] + _dot(m_ref[...], w_ref[...])
        x2_ref[...] = x2
        r = lax.rsqrt(jnp.mean(x2 * x2, axis=-1, keepdims=True) + EPS)
        h2_ref[...] = ((x2 * r) * g_ref[...]).astype(bf16)

    row = pl.BlockSpec((tm, D), lambda i: (i, 0))
    return pl.pallas_call(
        body, name="outproj_norm", grid=(S // tm,),
        in_specs=[row, pl.BlockSpec((D, D), lambda i: (0, 0)), row, pl.BlockSpec((1, D), lambda i: (0, 0))],
        out_specs=[row, row],
        out_shape=[jax.ShapeDtypeStruct((S, D), f32), jax.ShapeDtypeStruct((S, D), bf16)],
        compiler_params=_params(("parallel",)),
    )(mixed, wout, x, g2)


def _mlp_up(h2, wups):
    tm = 512

    def body(h_ref, w_ref, u_ref, a_ref):
        u = _dot(h_ref[...], w_ref[...])
        u_ref[...] = u
        ru = jnp.maximum(u, 0.0)
        a_ref[...] = (ru * ru).astype(bf16)

    out = pl.BlockSpec((tm, D), lambda q, i: (i, q))
    return pl.pallas_call(
        body, name="mlp_up", grid=(NCHIP, S // tm),
        in_specs=[pl.BlockSpec((tm, D), lambda q, i: (i, 0)), pl.BlockSpec((None, D, D), lambda q, i: (q, 0, 0))],
        out_specs=[out, out],
        out_shape=[jax.ShapeDtypeStruct((S, DFF), f32), jax.ShapeDtypeStruct((S, DFF), bf16)],
        compiler_params=_params(("parallel", "parallel")),
    )(h2, wups)


def _mlp_down_loss(a, wdown, x2, g3, tgt):
    tm = 256

    def body(a_ref, w_ref, x2_ref, g_ref, t_ref, dx_ref, dxb_ref, dg_ref, loss_ref):
        i = pl.program_id(0)
        x3 = x2_ref[...] + _dot(a_ref[...], w_ref[...])
        r = lax.rsqrt(jnp.mean(x3 * x3, axis=-1, keepdims=True) + EPS)
        xh = x3 * r
        g = g_ref[...]
        e = xh * g - t_ref[...]
        part = 0.5 * jnp.sum(jnp.mean(e * e, axis=-1, keepdims=True), axis=0, keepdims=True)
        dy = e * (1.0 / D)
        gdy = dy * g
        dx = r * (gdy - xh * jnp.mean(gdy * xh, axis=-1, keepdims=True))
        dx_ref[...] = dx
        dxb_ref[...] = dx.astype(bf16)

        @pl.when(i == 0)
        def _():
            dg_ref[...] = jnp.zeros_like(dg_ref)
            loss_ref[...] = jnp.zeros_like(loss_ref)

        dg_ref[...] += jnp.sum(dy * xh, axis=0, keepdims=True)
        loss_ref[...] += jnp.broadcast_to(part, (1, 128))

    row = pl.BlockSpec((tm, D), lambda i: (i, 0))
    vec = pl.BlockSpec((1, D), lambda i: (0, 0))
    return pl.pallas_call(
        body, name="mlp_down_loss", grid=(S // tm,),
        in_specs=[pl.BlockSpec((tm, DFF), lambda i: (i, 0)), pl.BlockSpec((DFF, D), lambda i: (0, 0)), row, vec, row],
        out_specs=[row, row, vec, pl.BlockSpec((1, 128), lambda i: (0, 0))],
        out_shape=[jax.ShapeDtypeStruct((S, D), f32), jax.ShapeDtypeStruct((S, D), bf16),
                   jax.ShapeDtypeStruct((1, D), f32), jax.ShapeDtypeStruct((1, 128), f32)],
        compiler_params=_params(("arbitrary",)),
    )(a, wdown, x2, g3, tgt)


def _mlp_down_bwd(dx3b, wdown, u):
    tm = 256

    def body(d_ref, w_ref, u_ref, du_ref):
        d = d_ref[...]
        for q in range(NCHIP):
            cols = slice(q * D, (q + 1) * D)
            da = _dot_nt(d, w_ref[cols, :])
            du_ref[:, cols] = (da * (2.0 * jnp.maximum(u_ref[:, cols], 0.0))).astype(bf16)

    return pl.pallas_call(
        body, name="mlp_down_bwd", grid=(S // tm,),
        in_specs=[pl.BlockSpec((tm, D), lambda i: (i, 0)), pl.BlockSpec((DFF, D), lambda i: (0, 0)),
                  pl.BlockSpec((tm, DFF), lambda i: (i, 0))],
        out_specs=pl.BlockSpec((tm, DFF), lambda i: (i, 0)),
        out_shape=jax.ShapeDtypeStruct((S, DFF), bf16),
        compiler_params=_params(("parallel",)),
    )(dx3b, wdown, u)


def _mlp_up_bwd(du, wups, x2, dx3, g2):
    tm = 256

    def body(du_ref, w_ref, x2_ref, dx3_ref, g_ref, dx2_ref, dx2b_ref, dg_ref):
        i = pl.program_id(0)
        dh = jnp.zeros((tm, D), f32)
        for q in range(NCHIP):
            dh = dh + _dot_nt(du_ref[:, q * D:(q + 1) * D], w_ref[q])
        x2 = x2_ref[...]
        r = lax.rsqrt(jnp.mean(x2 * x2, axis=-1, keepdims=True) + EPS)
        xh = x2 * r
        gdh = dh * g_ref[...]
        dx2 = dx3_ref[...] + r * (gdh - xh * jnp.mean(gdh * xh, axis=-1, keepdims=True))
        dx2_ref[...] = dx2
        dx2b_ref[...] = dx2.astype(bf16)

        @pl.when(i == 0)
        def _():
            dg_ref[...] = jnp.zeros_like(dg_ref)

        dg_ref[...] += jnp.sum(dh * xh, axis=0, keepdims=True)

    row = pl.BlockSpec((tm, D), lambda i: (i, 0))
    vec = pl.BlockSpec((1, D), lambda i: (0, 0))
    return pl.pallas_call(
        body, name="mlp_up_bwd", grid=(S // tm,),
        in_specs=[pl.BlockSpec((tm, DFF), lambda i: (i, 0)), pl.BlockSpec((NCHIP, D, D), lambda i: (0, 0, 0)),
                  row, row, vec],
        out_specs=[row, row, vec],
        out_shape=[jax.ShapeDtypeStruct((S, D), f32), jax.ShapeDtypeStruct((S, D), bf16),
                   jax.ShapeDtypeStruct((1, D), f32)],
        compiler_params=_params(("arbitrary",)),
    )(du, wups, x2, dx3, g2)


def _gate_bwd(dx2b, wout, gates, ya, yb):
    tm = 256

    def body(d_ref, w_ref, g_ref, ya_ref, yb_ref, dya_ref, dyb_ref, dg_ref):
        dm = _dot_nt(d_ref[...], w_ref[...])
        sa = jax.nn.sigmoid(g_ref[:, 0:D])
        sb = jax.nn.sigmoid(g_ref[:, D:2 * D])
        dya_ref[...] = (dm * sa).astype(bf16)
        dyb_ref[...] = (dm * sb).astype(bf16)
        dg_ref[:, 0:D] = (dm * ya_ref[...] * (sa * (1.0 - sa))).astype(bf16)
        dg_ref[:, D:2 * D] = (dm * yb_ref[...] * (sb * (1.0 - sb))).astype(bf16)

    row = lambda w: pl.BlockSpec((tm, w), lambda i: (i, 0))
    return pl.pallas_call(
        body, name="gate_bwd", grid=(S // tm,),
        in_specs=[row(D), pl.BlockSpec((D, D), lambda i: (0, 0)), row(2 * D), row(D), row(D)],
        out_specs=[row(D), row(D), row(2 * D)],
        out_shape=[jax.ShapeDtypeStruct((S, D), bf16), jax.ShapeDtypeStruct((S, D), bf16),
                   jax.ShapeDtypeStruct((S, 2 * D), bf16)],
        compiler_params=_params(("parallel",)),
    )(dx2b, wout, gates, ya, yb)


def _branch_bwd(dya, dyb, was, wbs):
    tm = 512

    def body(dya_ref, dyb_ref, wa_ref, wb_ref, doa_ref, dob_ref):
        doa = jnp.zeros((tm, FOXW), f32)
        dob = jnp.zeros((tm, DILOUT), f32)
        for q in range(NCHIP):
            cols = slice(q * 256, (q + 1) * 256)
            doa = doa + _dot_nt(dya_ref[:, cols], wa_ref[q])
            dob = dob + _dot_nt(dyb_ref[:, cols], wb_ref[q])
        doa_ref[...] = doa.astype(bf16)
        dob_ref[...] = dob

    row = lambda w: pl.BlockSpec((tm, w), lambda i: (i, 0))
    full3 = lambda a: pl.BlockSpec(a.shape, lambda i: (0, 0, 0))
    return pl.pallas_call(
        body, name="branch_bwd", grid=(S // tm,),
        in_specs=[row(D), row(D), full3(was), full3(wbs)],
        out_specs=[row(FOXW), row(DILOUT)],
        out_shape=[jax.ShapeDtypeStruct((S, FOXW), bf16), jax.ShapeDtypeStruct((S, DILOUT), f32)],
        compiler_params=_params(("parallel",)),
    )(dya, dyb, was, wbs)


def _branch_wgrad(oa, ob, dya, dyb):
    def body(oa_ref, ob_ref, dya_ref, dyb_ref, dwa_ref, dwb_ref):
        dwa_ref[...] = _dot_tn(oa_ref[...], dya_ref[...])
        dwb_ref[...] = _dot_tn(ob_ref[...], dyb_ref[...])

    full = lambda w: pl.BlockSpec((S, w), lambda q: (0, 0))
    colq = pl.BlockSpec((S, 256), lambda q: (0, q))
    return pl.pallas_call(
        body, name="branch_wgrad", grid=(NCHIP,),
        in_specs=[full(FOXW), full(DILOUT), colq, colq],
        out_specs=[pl.BlockSpec((None, FOXW, 256), lambda q: (q, 0, 0)),
                   pl.BlockSpec((None, DILOUT, 256), lambda q: (q, 0, 0))],
        out_shape=[jax.ShapeDtypeStruct((NCHIP, FOXW, 256), f32), jax.ShapeDtypeStruct((NCHIP, DILOUT, 256), f32)],
        compiler_params=_params(("parallel",)),
    )(oa, ob, dya, dyb)


def _fox_bwd(qkva, doa, oa, lse, F, ftb):
    nb = S // TQ

    def body(q_ref, k_ref, v_ref, do_ref, o_ref, lse_ref, F_ref, ftb_ref, dq_ref, dk_ref, dv_ref, dft_ref, dfq_ref,
             dq_scr):
        p = pl.program_id(0)
        lane, hm = _head_masks()
        sub8 = lax.broadcasted_iota(jnp.int32, (8, 1), 0)
        rowi = lax.broadcasted_iota(jnp.int32, (TQ, 1), 0)
        coli = lax.broadcasted_iota(jnp.int32, (1, TQ), 1)
        dq_scr[...] = jnp.zeros_like(dq_scr)
        dfq_ref[...] = jnp.zeros_like(dfq_ref)

        def kvblock(j, c):
            c0 = pl.multiple_of(j * TQ, TQ)
            k = k_ref[pl.ds(c0, TQ), :]
            v = v_ref[pl.ds(c0, TQ), :]
            kf = k.astype(f32)
            km = [jnp.where(hm[hh], kf, 0.0).astype(bf16) for hh in (0, 1)]
            frow = ftb_ref[j]
            fr = [jnp.sum(jnp.where(sub8 == 2 * p + hh, frow, 0.0), axis=0, keepdims=True) for hh in (0, 1)]

            def qblock(i, carry):
                dk, dv, df0, df1 = carry
                df = [df0, df1]
                r0 = pl.multiple_of(i * TQ, TQ)
                q = q_ref[pl.ds(r0, TQ), :].astype(f32) * 0.125
                do = do_ref[pl.ds(r0, TQ), :].astype(f32)
                prod = do * o_ref[pl.ds(r0, TQ), :].astype(f32)
                lseb = lse_ref[pl.ds(r0, TQ), :]
                Fb = F_ref[pl.ds(r0, TQ), :]
                causal = (c0 + coli) <= (r0 + rowi)
                dqacc = jnp.zeros((TQ, 128), f32)
                rowsum = jnp.zeros((TQ, 128), f32)
                for hh in (0, 1):
                    qh = jnp.where(hm[hh], q, 0.0).astype(bf16)
                    doh = jnp.where(hm[hh], do, 0.0).astype(bf16)
                    delta = jnp.sum(jnp.where(hm[hh], prod, 0.0), axis=1, keepdims=True)
                    fc = jnp.sum(jnp.where(lane == 2 * p + hh, Fb, 0.0), axis=1, keepdims=True)
                    s = _dot_nt(qh, k) + (fc - fr[hh])
                    pr = jnp.where(causal, jnp.exp(s - lseb[:, hh * HD:hh * HD + 1]), 0.0)
                    dp = _dot_nt(doh, v)
                    ds = pr * (dp - delta)
                    dsb = ds.astype(bf16)
                    dv = dv + _dot_tn(pr.astype(bf16), doh)
                    dk = dk + _dot_tn(dsb, qh)
                    dqacc = dqacc + _dot(dsb, km[hh])
                    df[hh] = df[hh] - jnp.sum(ds, axis=0, keepdims=True)
                    rowsum = jnp.where(hm[hh], jnp.sum(ds, axis=1, keepdims=True), rowsum)
                dq_scr[pl.ds(r0, TQ), :] += dqacc * 0.125
                dfq_ref[pl.ds(r0, TQ), :] += rowsum
                return dk, dv, df[0], df[1]

            z = jnp.zeros((TQ, 128), f32)
            zr = jnp.zeros((1, TQ), f32)
            dk, dv, df0, df1 = lax.fori_loop(j, nb, qblock, (z, z, zr, zr))
            dk_ref[pl.ds(c0, TQ), :] = dk.astype(bf16)
            dv_ref[pl.ds(c0, TQ), :] = dv.astype(bf16)
            dft_ref[j] = jnp.where(sub8 == 0, df0, jnp.where(sub8 == 1, df1, 0.0))
            return c

        lax.fori_loop(0, nb, kvblock, 0)
        dq_ref[...] = dq_scr[...].astype(bf16)

    blk = lambda off: pl.BlockSpec((S, 128), lambda p: (0, off + p))
    return pl.pallas_call(
        body, name="fox_bwd", grid=(4,),
        in_specs=[blk(0), blk(4), blk(8), blk(0), blk(0), blk(0), pl.BlockSpec((S, 128), lambda p: (0, 0)),
                  pl.BlockSpec((nb, 8, TQ), lambda p: (0, 0, 0))],
        out_specs=[blk(0), blk(0), blk(0), pl.BlockSpec((None, nb, 8, TQ), lambda p: (p, 0, 0, 0)), blk(0)],
        out_shape=[jax.ShapeDtypeStruct((S, FOXW), bf16)] * 3 + [jax.ShapeDtypeStruct((4, nb, 8, TQ), f32),
                                                                 jax.ShapeDtypeStruct((S, FOXW), f32)],
        scratch_shapes=[pltpu.VMEM((S, 128), f32)],
        compiler_params=_params(("parallel",)),
    )(qkva, qkva, qkva, doa, oa, lse, F, ftb)


def _forget_bwd(dft, dfq, fa, bpad):
    nb = S // TQ

    def body(dft_ref, dfq_ref, fa_ref, b_ref, dfa_ref, db_ref, rows):
        rr = lax.broadcasted_iota(jnp.int32, (TQ, TQ), 0)
        cc = lax.broadcasted_iota(jnp.int32, (TQ, TQ), 1)
        upper = (cc >= rr).astype(bf16)
        ones = jnp.ones((8, TQ), bf16)
        lane = lax.broadcasted_iota(jnp.int32, (1, 128), 1)
        carry = jnp.zeros((1, 128), f32)
        db = jnp.zeros((1, 128), f32)
        rows[...] = jnp.zeros_like(rows)
        for b in reversed(range(nb)):
            for p in range(4):
                rows[2 * p:2 * p + 2, :] = dft_ref[p, b, 0:2, :]
            cols = jnp.zeros((TQ, 128), f32)
            for h in range(8):
                c0 = (h // 2) * 128 + (h % 2) * HD
                cols = jnp.where(lane == h, dfq_ref[b * TQ:(b + 1) * TQ, c0:c0 + 1], cols)
            dlf = carry
            tot = jnp.zeros((8, 128), f32)
            for part in _split3(rows[...]):
                dlf = dlf + _dot_nt(upper, part)
                tot = tot + _dot_nt(ones, part)
            for part in _split3(cols):
                dlf = dlf + _dot(upper, part)
            carry = carry + tot[0:1, :] + jnp.sum(cols, axis=0, keepdims=True)
            z = fa_ref[b * TQ:(b + 1) * TQ, :] + b_ref[...]
            dz = jnp.where(lane < 8, dlf * jax.nn.sigmoid(-z), 0.0)
            dfa_ref[b * TQ:(b + 1) * TQ, :] = dz.astype(bf16)
            db = db + jnp.sum(dz, axis=0, keepdims=True)
        db_ref[...] = db

    return pl.pallas_call(
        body, name="forget_bwd",
        out_shape=[jax.ShapeDtypeStruct((S, 128), bf16), jax.ShapeDtypeStruct((1, 128), f32)],
        scratch_shapes=[pltpu.VMEM((128, TQ), f32)],
        compiler_params=_params(),
    )(dft, dfq, fa, bpad)


def _dil_bwd(qkvb, dob, ob, lseb, rope):
    c_t, s1_t, s2_t = rope

    def body(*refs):
        q_refs, k_refs, v_refs = refs[0:3], refs[3:6], refs[6:9]
        dob_ref, ob_ref, lse_ref, c_ref, s1_ref, s2_ref = refs[9:15]
        dq_out, dk_out, dv_out = refs[15:18], refs[18:21], refs[21:24]
        qp, kp, vp, dop, lp, dlp, dln, dqp, dkp, dvp, nat = refs[24:35]
        _, hm = _head_masks()

        def delta_rows(i, c):
            r0 = pl.multiple_of(i * TQ, TQ)
            prod = dob_ref[pl.ds(r0, TQ), :] * ob_ref[pl.ds(r0, TQ), :].astype(f32)
            d0 = jnp.sum(jnp.where(hm[0], prod, 0.0), axis=1, keepdims=True)
            d1 = jnp.sum(jnp.where(hm[1], prod, 0.0), axis=1, keepdims=True)
            dln[pl.ds(r0, TQ), :] = jnp.where(hm[0], d0, d1)
            return c

        lax.fori_loop(0, S // TQ, delta_rows, 0)

        for g, r in enumerate(DIL):
            nbl = S // r // BAND
            if r == 1:
                srcs = (q_refs[g], k_refs[g], v_refs[g], dob_ref, lse_ref, dln)
            else:
                for dst, src in ((qp, q_refs[g]), (kp, k_refs[g]), (vp, v_refs[g]), (dop, dob_ref),
                                 (lp, lse_ref), (dlp, dln)):
                    _permute_in(dst, src, r)
                srcs = (qp, kp, vp, dop, lp, dlp)
            dkp[...] = jnp.zeros_like(dkp)
            dvp[...] = jnp.zeros_like(dvp)

            def blk(bb, c, srcs=srcs, nbl=nbl):
                qs_, ks_, vs_, dos_, ls_, dls_ = srcs
                r0, k0, valid = _band_geometry(bb, nbl)
                q = qs_[pl.ds(r0, BAND), :] * 0.125
                kwf = ks_[pl.ds(k0, 2 * BAND), :]
                kw = kwf.astype(bf16)
                vw = vs_[pl.ds(k0, 2 * BAND), :].astype(bf16)
                do = dos_[pl.ds(r0, BAND), :]
                lse = ls_[pl.ds(r0, BAND), :]
                dlt = dls_[pl.ds(r0, BAND), :]
                dq = jnp.zeros((BAND, 128), f32)
                dk = jnp.zeros((2 * BAND, 128), f32)
                dv = jnp.zeros((2 * BAND, 128), f32)
                for hh in (0, 1):
                    qh = jnp.where(hm[hh], q, 0.0).astype(bf16)
                    doh = jnp.where(hm[hh], do, 0.0).astype(bf16)
                    kh = jnp.where(hm[hh], kwf, 0.0).astype(bf16)
                    s = _dot_nt(qh, kw)
                    pr = jnp.where(valid, jnp.exp(s - lse[:, hh * HD:hh * HD + 1]), 0.0)
                    dp = _dot_nt(doh, vw)
                    ds = pr * (dp - dlt[:, hh * HD:hh * HD + 1])
                    dsb = ds.astype(bf16)
                    dv = dv + _dot_tn(pr.astype(bf16), doh)
                    dk = dk + _dot_tn(dsb, qh)
                    dq = dq + _dot(dsb, kh)
                dqp[pl.ds(r0, BAND), :] = dq * 0.125
                dkp[pl.ds(k0, 2 * BAND), :] += dk
                dvp[pl.ds(k0, 2 * BAND), :] += dv
                return c

            lax.fori_loop(0, S // BAND, blk, 0)

            for acc, out, roped in ((dqp, dq_out[g], True), (dkp, dk_out[g], True), (dvp, dv_out[g], False)):
                if r == 1:
                    src = acc
                else:
                    _permute_out(nat, acc, r)
                    src = nat

                def emit(i, c, src=src, out=out, roped=roped):
                    r0 = pl.multiple_of(i * TQ, TQ)
                    d = src[pl.ds(r0, TQ), :]
                    if roped:
                        d = (d * c_ref[pl.ds(r0, TQ), :] + pltpu.roll(d * s1_ref[pl.ds(r0, TQ), :], 8, 1)
                             + pltpu.roll(d * s2_ref[pl.ds(r0, TQ), :], 120, 1))
                    out[pl.ds(r0, TQ), :] = d.astype(bf16)
                    return c

                lax.fori_loop(0, S // TQ, emit, 0)

    pair = pl.BlockSpec((S, 128), lambda p: (0, p))
    tab = pl.BlockSpec((S, 128), lambda p: (0, 0))
    return pl.pallas_call(
        body, name="dil_bwd", grid=(2,),
        in_specs=_dil_in_specs() + [pair, pair, pair, tab, tab, tab],
        out_specs=[pair] * 9,
        out_shape=[jax.ShapeDtypeStruct((S, DILOUT), bf16)] * 9,
        scratch_shapes=[pltpu.VMEM((S, 128), f32)] * 11,
        compiler_params=_params(("parallel",)),
    )(*([qkvb] * 9), dob, ob, lseb, c_t, s1_t, s2_t)


def _inproj_bwd(dproj, wp, x, dx2, g1):
    tm = 256

    def body(d_ref, w_ref, x_ref, dx2_ref, g_ref, dx_ref, dg_ref):
        i = pl.program_id(0)
        dh = _dot_nt(d_ref[...], w_ref[...])
        xb = x_ref[...]
        r = lax.rsqrt(jnp.mean(xb * xb, axis=-1, keepdims=True) + EPS)
        xh = xb * r
        gdh = dh * g_ref[...]
        dx_ref[...] = dx2_ref[...] + r * (gdh - xh * jnp.mean(gdh * xh, axis=-1, keepdims=True))

        @pl.when(i == 0)
        def _():
            dg_ref[...] = jnp.zeros_like(dg_ref)

        dg_ref[...] += jnp.sum(dh * xh, axis=0, keepdims=True)

    row = pl.BlockSpec((tm, D), lambda i: (i, 0))
    vec = pl.BlockSpec((1, D), lambda i: (0, 0))
    return pl.pallas_call(
        body, name="inproj_bwd", grid=(S // tm,),
        in_specs=[pl.BlockSpec((tm, WP), lambda i: (i, 0)), pl.BlockSpec((D, WP), lambda i: (0, 0)), row, row, vec],
        out_specs=[row, vec],
        out_shape=[jax.ShapeDtypeStruct((S, D), f32), jax.ShapeDtypeStruct((1, D), f32)],
        compiler_params=_params(("arbitrary",)),
    )(dproj, wp, x, dx2, g1)


def _pad_win(w_full):
    return jnp.concatenate([w_full[:, 0:1536], w_full[:, 1544:D_IN], w_full[:, 1536:1544],
                            jnp.zeros((D, WP - D_IN), w_full.dtype)], axis=1)


def _unpad_win(dwp):
    return jnp.concatenate([dwp[:, 0:1536], dwp[:, C_FA:C_FA + 8], dwp[:, 1536:C_FA]], axis=1)


HBM = pl.BlockSpec(memory_space=pltpu.HBM)
SEM = pl.BlockSpec(memory_space=pltpu.SEMAPHORE)
ANY = pl.BlockSpec(memory_space=pl.ANY)
SMALL_ROWS = 8


def _comm_call(name, body, bufs, sems_in=(), new_sems=(), after=None):
    nb, ns, nn = len(bufs), len(sems_in), len(new_sems)
    extra = [] if after is None else [after]

    def kern(*refs):
        off = nb + ns + len(extra)
        body(refs[:nb], refs[nb:nb + ns], refs[off:off + nn])

    res = pl.pallas_call(
        kern, name=name,
        in_specs=[HBM] * nb + [SEM] * ns + [ANY] * len(extra),
        out_specs=[SEM] * nn + [HBM] * nb,
        out_shape=[pltpu.SemaphoreType.DMA((k,)) for k in new_sems] + [pltpu.HBM(b.shape, b.dtype) for b in bufs],
        input_output_aliases={i: nn + i for i in range(nb)},
        compiler_params=pltpu.CompilerParams(has_side_effects=pltpu.SideEffectType.DATAFLOW_SIDE_EFFECTING),
    )(*[pltpu.with_memory_space_constraint(b, pltpu.HBM) for b in bufs], *sems_in, *extra)
    return list(res[:nn]), list(res[nn:])


def _place():
    x, y, c = lax.axis_index("x"), lax.axis_index("y"), lax.axis_index("c")
    chips = [(1 - x, y), (x, 1 - y), (1 - x, 1 - y)]
    return x, y, c, chips


def _rcopy(src, dst, ssem, rsem, dev):
    return pltpu.make_async_remote_copy(src_ref=src, dst_ref=dst, send_sem=ssem, recv_sem=rsem,
                                        device_id=dev, device_id_type=pl.DeviceIdType.MESH)


def _half(nrows, which):
    return pl.ds(which * (nrows // 2), nrows // 2)


def _ici_copies(stack, group_sizes, ssems, rsems):
    x, y, c, chips = _place()
    me_q = 2 * x + y
    sends, recvs = [], []
    a = 0
    for grp, size in enumerate(group_sizes):
        for k in range(size):
            rows = _half(stack[a].shape[1], c)
            for j, (cx, cy) in enumerate(chips):
                mine = stack[a].at[me_q, rows]
                sends.append(_rcopy(mine, mine, ssems[grp].at[k * 3 + j], rsems[grp].at[k * 3 + j], (cx, cy, c)))
                theirs = stack[a].at[2 * cx + cy, rows]
                recvs.append(_rcopy(theirs, theirs, ssems[grp].at[k * 3 + j], rsems[grp].at[k * 3 + j],
                                    (cx, cy, c)))
            a += 1
    return sends, recvs


def _allgather_start(stacks, group_sizes):
    def body(bufs, _, new):
        sends, _r = _ici_copies(bufs, group_sizes, new[0::2], new[1::2])
        for cp in sends:
            cp.start()

    sizes = []
    for size in group_sizes:
        sizes += [3 * size, 3 * size]
    sems, stacks = _comm_call("allgather_start", body, stacks, new_sems=sizes)
    return [(sems[2 * g], sems[2 * g + 1]) for g in range(len(group_sizes))], stacks


def _forward_copies(stack, ssem, rsem):
    x, y, c, chips = _place()
    sib = (x, y, 1 - c)
    sends, recvs = [], []
    for a in range(len(stack)):
        for j, (cx, cy) in enumerate(chips):
            landed = stack[a].at[2 * cx + cy, _half(stack[a].shape[1], c)]
            sends.append(_rcopy(landed, landed, ssem.at[a * 3 + j], rsem.at[a * 3 + j], sib))
            other = stack[a].at[2 * cx + cy, _half(stack[a].shape[1], 1 - c)]
            recvs.append(_rcopy(other, other, ssem.at[a * 3 + j], rsem.at[a * 3 + j], sib))
    return sends, recvs


def _allgather_forward(name, stacks, sems, after=None):
    n = len(stacks)

    def body(bufs, taken, new):
        sends, recvs = _ici_copies(bufs, [n], [taken[0]], [taken[1]])
        for cp in sends:
            cp.wait_send()
        for cp in recvs:
            cp.wait_recv()
        fwd, _r = _forward_copies(bufs, new[0], new[1])
        for cp in fwd:
            cp.start()

    return _comm_call(name, body, stacks, sems_in=sems, new_sems=(3 * n, 3 * n), after=after)


def _allgather_finish(name, stacks, sems, after=None):
    def body(bufs, taken, _):
        sends, recvs = _forward_copies(bufs, taken[0], taken[1])
        for cp in sends:
            cp.wait_send()
        for cp in recvs:
            cp.wait_recv()

    return _comm_call(name, body, stacks, sems_in=sems, after=after)[1]


def _pair_copies(g, t, ssem, rsem):
    x, y, c, _ = _place()
    return [_rcopy(g[a].at[:, _half(g[a].shape[1], 1 - c), :], t[a], ssem.at[a], rsem.at[a], (x, y, 1 - c))
            for a in range(len(g))]


def _pair_start(name, gs):
    n = len(gs)
    ts = [lax.empty((NCHIP, g.shape[1] // 2, g.shape[2]), f32) for g in gs]

    def body(bufs, _, new):
        for cp in _pair_copies(bufs[:n], bufs[n:], new[0], new[1]):
            cp.start()

    sems, bufs = _comm_call(name, body, list(gs) + ts, new_sems=(n, n))
    return sems, bufs


def _pair_wait(name, bufs, sems, after):
    n = len(bufs) // 2

    def body(refs, taken, _):
        for cp in _pair_copies(refs[:n], refs[n:], taken[0], taken[1]):
            cp.wait_send()
            cp.wait_recv()

    bufs = _comm_call(name, body, bufs, sems_in=sems, after=after)[1]
    return bufs[:n], bufs[n:]


def _row_tile(h):
    return min(h, 256)


def _pair_add(g, t, c_arr, name):
    _, R, C = g.shape
    h = R // 2
    tr = _row_tile(h)
    nblk = h // tr

    def body(c_ref, g_ref, t_ref, p32_ref, p16_ref):
        s = g_ref[...] + t_ref[...]
        p32_ref[...] = s
        p16_ref[...] = s.astype(bf16)

    blk = pl.BlockSpec((None, tr, C), lambda q, i, c_ref: (q, i, 0))
    return pl.pallas_call(
        body, name=name,
        grid_spec=pltpu.PrefetchScalarGridSpec(
            num_scalar_prefetch=1, grid=(NCHIP, nblk),
            in_specs=[pl.BlockSpec((None, tr, C), lambda q, i, c_ref: (q, c_ref[0] * nblk + i, 0)), blk],
            out_specs=[blk, blk]),
        out_shape=[jax.ShapeDtypeStruct((NCHIP, h, C), f32), jax.ShapeDtypeStruct((NCHIP, h, C), bf16)],
        compiler_params=_params(("parallel", "parallel")),
    )(c_arr, g, t)


def _shard_copies(p, r, sm, ssem, rsem):
    x, y, c, chips = _place()
    n = len(p)
    sends, recvs = [], []
    for a in range(n):
        for j, (cx, cy) in enumerate(chips):
            k = a * 3 + j
            sends.append(_rcopy(p[a].at[2 * cx + cy], r[a].at[j], ssem.at[k], rsem.at[k], (cx, cy, c)))
            recvs.append(_rcopy(r[a].at[j], r[a].at[j], ssem.at[k], rsem.at[k], (cx, cy, c)))
    if sm is not None:
        mine = sm.at[4 * x + 2 * y + c]
        for i in range(1, 8):
            px = (1 - x) if i & 4 else x
            py = (1 - y) if i & 2 else y
            pc = (1 - c) if i & 1 else c
            k = 3 * n + i - 1
            sends.append(_rcopy(mine, mine, ssem.at[k], rsem.at[k], (px, py, pc)))
            slot = sm.at[4 * px + 2 * py + pc]
            recvs.append(_rcopy(slot, slot, ssem.at[k], rsem.at[k], (px, py, pc)))
    return sends, recvs


def _shard_start(name, p16s, sm=None):
    n = len(p16s)
    rs = [lax.empty((3,) + p.shape[1:], bf16) for p in p16s]
    extra = [] if sm is None else [sm]
    nsem = 3 * n + (7 if sm is not None else 0)

    def body(bufs, _, new):
        sends, _r = _shard_copies(bufs[:n], bufs[n:2 * n], bufs[2 * n] if extra else None, new[0], new[1])
        for cp in sends:
            cp.start()

    return _comm_call(name, body, list(p16s) + rs + extra, new_sems=(nsem, nsem))


def _shard_wait(name, bufs, sems, n, after):
    has_sm = len(bufs) > 2 * n

    def body(refs, taken, _):
        sends, recvs = _shard_copies(refs[:n], refs[n:2 * n], refs[2 * n] if has_sm else None, taken[0], taken[1])
        for cp in sends:
            cp.wait_send()
        for cp in recvs:
            cp.wait_recv()

    bufs = _comm_call(name, body, bufs, sems_in=sems, after=after)[1]
    return bufs[n:2 * n], (bufs[2 * n] if has_sm else None)


def _shard_sum(p32, r, q_arr, c_arr, name):
    _, h, C = p32.shape
    tr = _row_tile(h)
    nblk = h // tr

    def body(q_ref, c_ref, p_ref, r_ref, o_ref):
        s = p_ref[...]
        for j in range(3):
            s = s + r_ref[j].astype(f32)
        o_ref[...] = s

    return pl.pallas_call(
        body, name=name,
        grid_spec=pltpu.PrefetchScalarGridSpec(
            num_scalar_prefetch=2, grid=(nblk,),
            in_specs=[pl.BlockSpec((None, tr, C), lambda i, q_ref, c_ref: (q_ref[0], i, 0)),
                      pl.BlockSpec((3, tr, C), lambda i, q_ref, c_ref: (0, i, 0))],
            out_specs=pl.BlockSpec((tr, C), lambda i, q_ref, c_ref: (c_ref[0] * nblk + i, 0))),
        out_shape=jax.ShapeDtypeStruct((2 * h, C), f32),
        compiler_params=_params(("parallel",)),
    )(q_arr, c_arr, p32, r)


def _swap_copies(full, ssem, rsem):
    x, y, c, _ = _place()
    sends, recvs = [], []
    for a in range(len(full)):
        mine = full[a].at[_half(full[a].shape[0], c)]
        sends.append(_rcopy(mine, mine, ssem.at[a], rsem.at[a], (x, y, 1 - c)))
        other = full[a].at[_half(full[a].shape[0], 1 - c)]
        recvs.append(_rcopy(other, other, ssem.at[a], rsem.at[a], (x, y, 1 - c)))
    return sends, recvs


def _swap_start(name, fulls):
    n = len(fulls)

    def body(bufs, _, new):
        for cp in _swap_copies(bufs, new[0], new[1])[0]:
            cp.start()

    return _comm_call(name, body, list(fulls), new_sems=(n, n))


def _swap_wait(name, fulls, sems, after):
    def body(refs, taken, _):
        sends, recvs = _swap_copies(refs, taken[0], taken[1])
        for cp in sends:
            cp.wait_send()
        for cp in recvs:
            cp.wait_recv()

    return _comm_call(name, body, fulls, sems_in=sems, after=after)[1]


def _small_sum(sm):
    def body(sm_ref, o_ref):
        s = sm_ref[0]
        for d in range(1, 8):
            s = s + sm_ref[d]
        o_ref[...] = s

    return pl.pallas_call(body, name="small_grad_sum", out_shape=jax.ShapeDtypeStruct((SMALL_ROWS, D), f32))(sm)


def _adamw(w, g, m, v, name):
    R, C = w.shape
    tr = min(R, 256)

    def body(w_ref, g_ref, m_ref, v_ref, d_ref, nm_ref, nv_ref):
        g_ = g_ref[...]
        m_ = ADAM_B1 * m_ref[...] + (1.0 - ADAM_B1) * g_
        v_ = ADAM_B2 * v_ref[...] + (1.0 - ADAM_B2) * (g_ * g_)
        m_hat = m_ / (1.0 - ADAM_B1 ** ADAM_STEP)
        v_hat = v_ / (1.0 - ADAM_B2 ** ADAM_STEP)
        d_ref[...] = -ADAM_LR * (m_hat / (jnp.sqrt(v_hat) + ADAM_EPS) + ADAM_WD * w_ref[...])
        nm_ref[...] = m_
        nv_ref[...] = v_

    blk = pl.BlockSpec((tr, C), lambda i: (i, 0))
    return pl.pallas_call(
        body, name=name, grid=(R // tr,), in_specs=[blk] * 4, out_specs=[blk] * 3,
        out_shape=[jax.ShapeDtypeStruct((R, C), f32)] * 3,
        compiler_params=_params(("parallel",)),
    )(w, g, m, v)


def _stack_cols(w, n):
    K_, N = w.shape
    return jnp.stack([w[:, q * (N // n):(q + 1) * (N // n)] for q in range(n)], axis=0)


def kernel(x, norm_attn_g, w_in, b_forget, w_branch_a, w_branch_b, w_out, norm_mlp_g, w_up, w_down, norm_final_g, loss_target, m_norm_attn_g, m_w_in, m_b_forget, m_w_branch_a, m_w_branch_b, m_w_out, m_norm_mlp_g, m_w_up, m_w_down, m_norm_final_g, v_norm_attn_g, v_w_in, v_b_forget, v_w_branch_a, v_w_branch_b, v_w_out, v_norm_mlp_g, v_w_up, v_w_down, v_norm_final_g):
    xi, yi, ci = lax.axis_index("x"), lax.axis_index("y"), lax.axis_index("c")
    c_arr = jnp.reshape(ci, (1,)).astype(jnp.int32)
    q_arr = jnp.reshape(2 * xi + yi, (1,)).astype(jnp.int32)
    x_, tgt = x[0], loss_target[0]

    names = ["w_in", "w_branch_a", "w_branch_b", "w_out", "w_up", "w_down"]
    big = dict(zip(names, [w_in[0], w_branch_a[0], w_branch_b[0], w_out[0], w_up[0], w_down[0]]))
    ms = dict(zip(names, [m_w_in[0], m_w_branch_a[0], m_w_branch_b[0], m_w_out[0], m_w_up[0], m_w_down[0]]))
    vs = dict(zip(names, [v_w_in[0], v_w_branch_a[0], v_w_branch_b[0], v_w_out[0], v_w_up[0], v_w_down[0]]))
    grad, upd = {}, {}

    stacks = [lax.dynamic_update_slice(lax.empty((NCHIP,) + w.shape, bf16), w.astype(bf16)[None], (q_arr[0], 0, 0))
              for w in big.values()]
    (sem_in, sem_rest), stacks = _allgather_start(stacks, [1, 5])
    sem_f, win_s = _allgather_forward("allgather_forward_in", stacks[0:1], sem_in)
    (win_s,) = _allgather_finish("allgather_finish_in", win_s, sem_f)
    wp = _pad_win(jnp.concatenate([win_s[q] for q in range(NCHIP)], axis=1))

    rope = _rope_tables()
    bpad = jnp.pad(b_forget, ((0, 0), (0, 120)))
    h1, qkva, qkvb, gates, fa = _norm_inproj(x_, norm_attn_g, wp, rope)
    F, ftb = _forget_cumsum(fa, bpad)
    oa, lsea = _fox_fwd(qkva, F, ftb)
    sem_f, rest = _allgather_forward("allgather_forward_rest", stacks[1:], sem_rest, after=oa)
    ob, lseb = _dil_fwd(qkvb)
    was, wbs, wouts, wups, wdowns = _allgather_finish("allgather_finish_rest", rest, sem_f, after=ob)
    wout = wouts.reshape(D, D)
    wdown = wdowns.reshape(DFF, D)
    ya, yb, mixed = _branch_mix(oa, ob, was, wbs, gates)
    x2, h2 = _outproj_norm(mixed, wout, x_, norm_mlp_g)
    u, a = _mlp_up(h2, wups)
    dx3, dx3b, dg3, loss_part = _mlp_down_loss(a, wdown, x2, norm_final_g.reshape(1, D), tgt)
    loss = lax.psum(loss_part[0, 0], ("x", "y", "c"))

    def reduce_to_pairs(tag, group, bufs, sems, after):
        gs, ts = _pair_wait("pair_wait_" + tag, bufs, sems, after)
        p32s, p16s = zip(*[_pair_add(gs[i], ts[i], c_arr, "pair_add_" + nm) for i, nm in enumerate(group)])
        return p32s, p16s

    def reduce_to_shard(tag, group, p32s, bufs, sems, after):
        rs, sm = _shard_wait("shard_wait_" + tag, bufs, sems, len(group), after)
        fulls = [_shard_sum(p32s[i], rs[i], q_arr, c_arr, "shard_sum_" + nm) for i, nm in enumerate(group)]
        return _swap_start("swap_start_" + tag, fulls), sm

    def finish(tag, group, fulls, sems, after):
        fulls = _swap_wait("swap_wait_" + tag, fulls, sems, after)
        for nm, gfull in zip(group, fulls):
            grad[nm] = gfull
            upd[nm] = _adamw(big[nm], gfull, ms[nm], vs[nm], "adamw_" + nm)

    grp_a, grp_b, grp_c = ["w_down", "w_up"], ["w_out", "w_branch_a", "w_branch_b"], ["w_in"]
    du = _mlp_down_bwd(dx3b, wdown, u)
    dwdown = _mm(a, dx3b, "tn", f32, 1024, D, "wgrad_down")
    dwup = _mm(h2, du, "tn", f32, D, 1024, "wgrad_up", stack_cols=True)
    pair_a = _pair_start("pair_start_a", [dwdown.reshape(NCHIP, DFF // NCHIP, D), dwup])
    dx2, dx2b, dg2 = _mlp_up_bwd(du, wups, x2, dx3, norm_mlp_g)
    p32_a, p16_a = reduce_to_pairs("a", grp_a, pair_a[1], pair_a[0], dx2b)
    shard_a = _shard_start("shard_start_a", p16_a)
    dya, dyb, dgates = _gate_bwd(dx2b, wout, gates, ya, yb)
    dwout = _mm(mixed, dx2b, "tn", f32, D, D, "wgrad_out")
    doa, dob = _branch_bwd(dya, dyb, was, wbs)
    dwas, dwbs = _branch_wgrad(oa, ob, dya, dyb)
    pair_b = _pair_start("pair_start_b", [dwout.reshape(NCHIP, D // NCHIP, D), dwas, dwbs])
    dqa, dka, dva, dft, dfq = _fox_bwd(qkva, doa, oa, lsea, F, ftb)
    p32_b, p16_b = reduce_to_pairs("b", grp_b, pair_b[1], pair_b[0], dqa)
    (swap_a, fulls_a), _ = reduce_to_shard("a", grp_a, p32_a, shard_a[1], shard_a[0], dqa)
    shard_b = _shard_start("shard_start_b", p16_b)
    dfa, dbf = _forget_bwd(dft, dfq, fa, bpad)
    dd = _dil_bwd(qkvb, dob, ob, lseb, rope)
    (swap_b, fulls_b), _ = reduce_to_shard("b", grp_b, p32_b, shard_b[1], shard_b[0], dd[0])
    finish("a", grp_a, fulls_a, swap_a, dd[0])
    dproj = jnp.concatenate([dqa, dka, dva, *dd, dgates, dfa], axis=1)
    dwp = _mm(h1, dproj, "tn", f32, D, 128, "wgrad_in")
    pair_c = _pair_start("pair_start_c", [_stack_cols(_unpad_win(dwp), NCHIP)])
    gx, dg1 = _inproj_bwd(dproj, wp, x_, dx2, norm_attn_g)
    p32_c, p16_c = reduce_to_pairs("c", grp_c, pair_c[1], pair_c[0], gx)
    small = jnp.concatenate([dg1, dg2, dg3, jnp.pad(dbf[:, 0:8], ((0, 0), (0, D - 8))),
                             jnp.zeros((SMALL_ROWS - 4, D), f32)], axis=0)
    sm = lax.dynamic_update_slice(lax.empty((8, SMALL_ROWS, D), f32), small[None],
                                  (4 * xi + 2 * yi + ci, 0, 0))
    shard_c = _shard_start("shard_start_c", p16_c, sm)
    finish("b", grp_b, fulls_b, swap_b, p16_c[0])
    (swap_c, fulls_c), sm = reduce_to_shard("c", grp_c, p32_c, shard_c[1], shard_c[0], upd["w_out"][0])
    gsmall = _small_sum(sm)
    finish("c", grp_c, fulls_c, swap_c, gsmall)

    grad["norm_attn_g"], grad["norm_mlp_g"] = gsmall[0:1], gsmall[1:2]
    grad["norm_final_g"], grad["b_forget"] = gsmall[2:3], gsmall[3:4, 0:8]
    upd["norm_attn_g"] = _adamw(norm_attn_g, grad["norm_attn_g"], m_norm_attn_g, v_norm_attn_g, "adamw_g1")
    upd["norm_mlp_g"] = _adamw(norm_mlp_g, grad["norm_mlp_g"], m_norm_mlp_g, v_norm_mlp_g, "adamw_g2")
    upd["norm_final_g"] = _adamw(norm_final_g.reshape(1, D), grad["norm_final_g"], m_norm_final_g.reshape(1, D),
                                 v_norm_final_g.reshape(1, D), "adamw_g3")
    upd["b_forget"] = _adamw(b_forget, grad["b_forget"], m_b_forget, v_b_forget, "adamw_bf")

    order = ["norm_attn_g", "w_in", "b_forget", "w_branch_a", "w_branch_b", "w_out", "norm_mlp_g", "w_up", "w_down",
             "norm_final_g"]
    shapes = dict(norm_attn_g=norm_attn_g.shape, w_in=w_in.shape, b_forget=b_forget.shape,
                  w_branch_a=w_branch_a.shape, w_branch_b=w_branch_b.shape, w_out=w_out.shape,
                  norm_mlp_g=norm_mlp_g.shape, w_up=w_up.shape, w_down=w_down.shape, norm_final_g=norm_final_g.shape)
    outs = [loss, gx.reshape(x.shape)]
    outs += [grad[nm].reshape(shapes[nm]) for nm in order]
    for k in range(3):
        outs += [upd[nm][k].reshape(shapes[nm]) for nm in order]
    return tuple(outs)
```

```python
import functools

import jax
import jax.numpy as jnp
import numpy as np
from jax import lax
from jax.experimental import pallas as pl
from jax.experimental.pallas import tpu as pltpu

f32 = jnp.float32
bf16 = jnp.bfloat16

S = 2048
D = 1024
DFF = 4096
HD = 64
FOXW = 512
DILW = 768
DILOUT = 256
DIL = (1, 4, 16)
BAND = 128
EPS = 1e-6
NEG = -1e30
ROPE_THETA = 500000.0
NCHIP = 4

C_QKVA, C_QB, C_KB, C_VB, C_G, C_FA, WP = 0, 1536, 2304, 3072, 3840, 5888, 6016
D_IN = 5896
SHARD_IN = 1474

ADAM_LR, ADAM_B1, ADAM_B2, ADAM_EPS, ADAM_WD, ADAM_STEP = 0.001, 0.9, 0.999, 1e-08, 0.01, 10

VMEM_LIMIT = 56 * 1024 * 1024
TQ = 256


def _params(sem=None):
    return pltpu.CompilerParams(dimension_semantics=sem, vmem_limit_bytes=VMEM_LIMIT)


def _dot(a, b):
    return jnp.dot(a, b, preferred_element_type=f32)


def _dot_nt(a, b):
    return lax.dot_general(a, b, (((1,), (1,)), ((), ())), preferred_element_type=f32)


def _dot_tn(a, b):
    return lax.dot_general(a, b, (((0,), (0,)), ((), ())), preferred_element_type=f32)


def _split3(x):
    hi = x.astype(bf16)
    r1 = x - hi.astype(f32)
    mid = r1.astype(bf16)
    lo = (r1 - mid.astype(f32)).astype(bf16)
    return hi, mid, lo


def _rope_tables():
    half = 8
    inv_freq = jnp.power(jnp.float32(ROPE_THETA), -jnp.arange(half, dtype=f32) * 2.0 / 16)
    ang = jnp.arange(S).astype(f32)[:, None] * inv_freq[None, :]
    cos, sin = jnp.cos(ang), jnp.sin(ang)
    one = jnp.ones((S, HD - 16), f32)
    zero = jnp.zeros((S, HD - 16), f32)
    z8 = jnp.zeros((S, 8), f32)
    c = jnp.concatenate([cos, cos, one], axis=1)
    s1 = jnp.concatenate([-sin, z8, zero], axis=1)
    s2 = jnp.concatenate([z8, sin, zero], axis=1)
    return tuple(jnp.concatenate([t, t], axis=1) for t in (c, s1, s2))


def _mm(a, b, mode, out_dtype, tm, tn, name, stack_cols=False):
    if mode == "nn":
        (M, K), (_, N) = a.shape, b.shape
        a_spec = pl.BlockSpec((tm, K), lambda i, j: (i, 0))
        b_spec = pl.BlockSpec((K, tn), lambda i, j: (0, j))
        dot = _dot
    elif mode == "nt":
        (M, K), (N, _) = a.shape, b.shape
        a_spec = pl.BlockSpec((tm, K), lambda i, j: (i, 0))
        b_spec = pl.BlockSpec((tn, K), lambda i, j: (j, 0))
        dot = _dot_nt
    else:
        (K, M), (_, N) = a.shape, b.shape
        a_spec = pl.BlockSpec((K, tm), lambda i, j: (0, i))
        b_spec = pl.BlockSpec((K, tn), lambda i, j: (0, j))
        dot = _dot_tn

    def body(a_ref, b_ref, o_ref):
        o_ref[...] = dot(a_ref[...], b_ref[...]).astype(out_dtype)

    if stack_cols:
        assert tm == M
        out_spec = pl.BlockSpec((None, tm, tn), lambda i, j: (j, 0, 0))
        out_shape = jax.ShapeDtypeStruct((N // tn, M, tn), out_dtype)
    else:
        out_spec = pl.BlockSpec((tm, tn), lambda i, j: (i, j))
        out_shape = jax.ShapeDtypeStruct((M, N), out_dtype)
    return pl.pallas_call(
        body, name=name, grid=(M // tm, N // tn), in_specs=[a_spec, b_spec],
        out_specs=out_spec, out_shape=out_shape,
        compiler_params=_params(("parallel", "parallel")),
    )(a, b)


def _norm_inproj(x, g1, wp, rope):
    tm = 256
    c_t, s1_t, s2_t = rope

    def body(x_ref, g_ref, w_ref, c_ref, s1_ref, s2_ref, h_ref, qkva_ref, qkvb_ref, gates_ref, fa_ref):
        xb = x_ref[...]
        r = lax.rsqrt(jnp.mean(xb * xb, axis=-1, keepdims=True) + EPS)
        h = ((xb * r) * g_ref[...]).astype(bf16)
        h_ref[...] = h
        qkva_ref[...] = _dot(h, w_ref[:, C_QKVA:C_QB]).astype(bf16)
        c, s1, s2 = c_ref[...], s1_ref[...], s2_ref[...]
        for sec, lo in enumerate((C_QB, C_KB)):
            pb = _dot(h, w_ref[:, lo:lo + DILW])
            for ch in range(DILW // 128):
                pc = pb[:, ch * 128:(ch + 1) * 128]
                roped = pc * c + pltpu.roll(pc, 120, 1) * s1 + pltpu.roll(pc, 8, 1) * s2
                qkvb_ref[:, sec * DILW + ch * 128: sec * DILW + (ch + 1) * 128] = roped
        qkvb_ref[:, 2 * DILW:3 * DILW] = _dot(h, w_ref[:, C_VB:C_G])
        gates_ref[...] = _dot(h, w_ref[:, C_G:C_FA])
        fa_ref[...] = _dot(h, w_ref[:, C_FA:WP])

    row = lambda w: pl.BlockSpec((tm, w), lambda i: (i, 0))
    return pl.pallas_call(
        body, name="norm_inproj", grid=(S // tm,),
        in_specs=[row(D), pl.BlockSpec((1, D), lambda i: (0, 0)), pl.BlockSpec((D, WP), lambda i: (0, 0)),
                  row(128), row(128), row(128)],
        out_specs=[row(D), row(3 * FOXW), row(3 * DILW), row(2 * D), row(128)],
        out_shape=[jax.ShapeDtypeStruct((S, D), bf16), jax.ShapeDtypeStruct((S, 3 * FOXW), bf16),
                   jax.ShapeDtypeStruct((S, 3 * DILW), f32), jax.ShapeDtypeStruct((S, 2 * D), f32),
                   jax.ShapeDtypeStruct((S, 128), f32)],
        compiler_params=_params(("parallel",)),
    )(x, g1, wp, c_t, s1_t, s2_t)


def _forget_cumsum(fa, bpad):
    nb = S // TQ

    def body(fa_ref, b_ref, F_ref, ftb_ref):
        rr = lax.broadcasted_iota(jnp.int32, (TQ, TQ), 0)
        cc = lax.broadcasted_iota(jnp.int32, (TQ, TQ), 1)
        tri = (rr >= cc).astype(bf16)
        lane = lax.broadcasted_iota(jnp.int32, (1, 128), 1)
        carry = jnp.zeros((1, 128), f32)
        for b in range(nb):
            z = fa_ref[b * TQ:(b + 1) * TQ, :] + b_ref[...]
            lf = jnp.minimum(z, 0.0) - jnp.log(1.0 + jnp.exp(-jnp.abs(z)))
            lf = jnp.where(lane < 8, lf, 0.0)
            hi, mid, lo = _split3(lf)
            fb = (_dot(tri, hi) + _dot(tri, mid)) + _dot(tri, lo) + carry
            F_ref[b * TQ:(b + 1) * TQ, :] = fb
            ftb_ref[b] = fb.T[0:8, :]
            carry = fb[TQ - 1:TQ, :]

    return pl.pallas_call(
        body, name="forget_cumsum",
        out_shape=[jax.ShapeDtypeStruct((S, 128), f32), jax.ShapeDtypeStruct((nb, 8, TQ), f32)],
        compiler_params=_params(),
    )(fa, bpad)


def _head_masks():
    lane = lax.broadcasted_iota(jnp.int32, (1, 128), 1)
    return lane, (lane < HD, lane >= HD)


def _fox_fwd(qkva, F, ftb):
    nb = S // TQ

    def body(q_ref, k_ref, v_ref, F_ref, ftb_ref, o_ref, lse_ref):
        p = pl.program_id(0)
        lane, hm = _head_masks()
        sub8 = lax.broadcasted_iota(jnp.int32, (8, 1), 0)
        rowi = lax.broadcasted_iota(jnp.int32, (TQ, 1), 0)
        coli = lax.broadcasted_iota(jnp.int32, (1, TQ), 1)

        def qblock(i, c):
            r0 = pl.multiple_of(i * TQ, TQ)
            q = q_ref[pl.ds(r0, TQ), :].astype(f32) * 0.125
            qs = [jnp.where(hm[hh], q, 0.0).astype(bf16) for hh in (0, 1)]
            Fb = F_ref[pl.ds(r0, TQ), :]
            fc = [jnp.sum(jnp.where(lane == 2 * p + hh, Fb, 0.0), axis=1, keepdims=True) for hh in (0, 1)]

            def kvblock(j, carry):
                c0 = pl.multiple_of(j * TQ, TQ)
                k = k_ref[pl.ds(c0, TQ), :]
                v = v_ref[pl.ds(c0, TQ), :]
                frow = ftb_ref[j]
                causal = (c0 + coli) <= (r0 + rowi)
                new = []
                for hh in (0, 1):
                    m, l, a = carry[3 * hh:3 * hh + 3]
                    fr = jnp.sum(jnp.where(sub8 == 2 * p + hh, frow, 0.0), axis=0, keepdims=True)
                    s = _dot_nt(qs[hh], k) + (fc[hh] - fr)
                    s = jnp.where(causal, s, NEG)
                    mn = jnp.maximum(m, jnp.max(s, axis=1, keepdims=True))
                    al = jnp.exp(m - mn)
                    pr = jnp.exp(s - mn)
                    l = al * l + jnp.sum(pr, axis=1, keepdims=True)
                    a = al * a + _dot(pr.astype(bf16), v)
                    new += [mn, l, a]
                return tuple(new)

            init = (jnp.full((TQ, 1), NEG, f32), jnp.zeros((TQ, 1), f32), jnp.zeros((TQ, 128), f32)) * 2
            m0, l0, a0, m1, l1, a1 = lax.fori_loop(0, i + 1, kvblock, init)
            o = jnp.where(hm[0], a0 / l0, a1 / l1)
            lse = jnp.where(hm[0], m0 + jnp.log(l0), m1 + jnp.log(l1))
            o_ref[pl.ds(r0, TQ), :] = o.astype(bf16)
            lse_ref[pl.ds(r0, TQ), :] = lse
            return c

        lax.fori_loop(0, nb, qblock, 0)

    blk = lambda off: pl.BlockSpec((S, 128), lambda p: (0, off + p))
    return pl.pallas_call(
        body, name="fox_fwd", grid=(4,),
        in_specs=[blk(0), blk(4), blk(8), pl.BlockSpec((S, 128), lambda p: (0, 0)),
                  pl.BlockSpec((nb, 8, TQ), lambda p: (0, 0, 0))],
        out_specs=[blk(0), blk(0)],
        out_shape=[jax.ShapeDtypeStruct((S, FOXW), bf16), jax.ShapeDtypeStruct((S, FOXW), f32)],
        compiler_params=_params(("parallel",)),
    )(qkva, qkva, qkva, F, ftb)


def _permute_in(dst, src, r):
    L = S // r
    for rho in range(r):
        dst[rho * L:(rho + 1) * L, :] = src[pl.ds(rho, L, stride=r), :]


def _permute_out(dst, src, r):
    L = S // r
    for rho in range(r):
        dst[pl.ds(rho, L, stride=r), :] = src[rho * L:(rho + 1) * L, :]


def _band_geometry(bb, nbl):
    r0 = pl.multiple_of(bb * BAND, BAND)
    k0 = pl.multiple_of(jnp.maximum(bb - 1, 0) * BAND, BAND)
    sub0 = (bb - lax.rem(bb, nbl)) * BAND
    qi = r0 + lax.broadcasted_iota(jnp.int32, (BAND, 1), 0)
    ki = k0 + lax.broadcasted_iota(jnp.int32, (1, 2 * BAND), 1)
    diff = qi - ki
    valid = (diff >= 0) & (diff <= BAND) & (ki >= sub0)
    return r0, k0, valid


def _dil_in_specs():
    specs = []
    for role in range(3):
        for g in range(3):
            specs.append(pl.BlockSpec((S, 128), functools.partial(lambda p, o: (0, o + p), o=role * 6 + g * 2)))
    return specs


def _dil_fwd(qkvb):
    def body(*refs):
        q_refs, k_refs, v_refs = refs[0:3], refs[3:6], refs[6:9]
        ob_ref, lse_ref = refs[9:11]
        qp, kp, vp, op, lp = refs[11:16]
        on = refs[16:19]
        ln = refs[19:22]
        _, hm = _head_masks()
        for g, r in enumerate(DIL):
            nbl = S // r // BAND
            if r == 1:
                qs_, ks_, vs_, od, ld = q_refs[g], k_refs[g], v_refs[g], on[g], ln[g]
            else:
                _permute_in(qp, q_refs[g], r)
                _permute_in(kp, k_refs[g], r)
                _permute_in(vp, v_refs[g], r)
                qs_, ks_, vs_, od, ld = qp, kp, vp, op, lp

            def blk(bb, c, qs_=qs_, ks_=ks_, vs_=vs_, od=od, ld=ld, nbl=nbl):
                r0, k0, valid = _band_geometry(bb, nbl)
                q = qs_[pl.ds(r0, BAND), :] * 0.125
                kw = ks_[pl.ds(k0, 2 * BAND), :].astype(bf16)
                vw = vs_[pl.ds(k0, 2 * BAND), :]
                o = jnp.zeros((BAND, 128), f32)
                lse = jnp.zeros((BAND, 128), f32)
                for hh in (0, 1):
                    qh = jnp.where(hm[hh], q, 0.0).astype(bf16)
                    s = jnp.where(valid, _dot_nt(qh, kw), NEG)
                    m = jnp.max(s, axis=1, keepdims=True)
                    pr = jnp.exp(s - m)
                    l = jnp.sum(pr, axis=1, keepdims=True)
                    vm = jnp.where(hm[hh], vw, 0.0).astype(bf16)
                    o = o + _dot((pr / l).astype(bf16), vm)
                    lse = jnp.where(hm[hh], m + jnp.log(l), lse)
                od[pl.ds(r0, BAND), :] = o
                ld[pl.ds(r0, BAND), :] = lse
                return c

            lax.fori_loop(0, S // BAND, blk, 0)
            if r != 1:
                _permute_out(on[g], op, r)
                _permute_out(ln[g], lp, r)

        def combine(i, c):
            r0 = pl.multiple_of(i * TQ, TQ)
            ls = [ln[g][pl.ds(r0, TQ), :] for g in range(3)]
            mx = jnp.maximum(jnp.maximum(ls[0], ls[1]), ls[2])
            es = [jnp.exp(l - mx) for l in ls]
            tot = (es[0] + es[1]) + es[2]
            acc = (es[0] / tot) * on[0][pl.ds(r0, TQ), :]
            acc = acc + (es[1] / tot) * on[1][pl.ds(r0, TQ), :]
            acc = acc + (es[2] / tot) * on[2][pl.ds(r0, TQ), :]
            ob_ref[pl.ds(r0, TQ), :] = acc.astype(bf16)
            lse_ref[pl.ds(r0, TQ), :] = mx + jnp.log(tot)
            return c

        lax.fori_loop(0, S // TQ, combine, 0)

    out_blk = pl.BlockSpec((S, 128), lambda p: (0, p))
    return pl.pallas_call(
        body, name="dil_fwd", grid=(2,),
        in_specs=_dil_in_specs(), out_specs=[out_blk, out_blk],
        out_shape=[jax.ShapeDtypeStruct((S, DILOUT), bf16), jax.ShapeDtypeStruct((S, DILOUT), f32)],
        scratch_shapes=[pltpu.VMEM((S, 128), f32)] * 11,
        compiler_params=_params(("parallel",)),
    )(*([qkvb] * 9))


def _branch_mix(oa, ob, was, wbs, gates):
    tm = 512

    def body(oa_ref, ob_ref, wa_ref, wb_ref, g_ref, ya_ref, yb_ref, mix_ref):
        oa_b, ob_b = oa_ref[...], ob_ref[...]
        for q in range(NCHIP):
            cols = slice(q * 256, (q + 1) * 256)
            ya = _dot(oa_b, wa_ref[q])
            yb = _dot(ob_b, wb_ref[q])
            ya_ref[:, cols] = ya
            yb_ref[:, cols] = yb
            ga = g_ref[:, q * 256:(q + 1) * 256]
            gb = g_ref[:, D + q * 256:D + (q + 1) * 256]
            mix_ref[:, cols] = (jax.nn.sigmoid(ga) * ya + jax.nn.sigmoid(gb) * yb).astype(bf16)

    row = lambda w: pl.BlockSpec((tm, w), lambda i: (i, 0))
    full3 = lambda a: pl.BlockSpec(a.shape, lambda i: (0, 0, 0))
    return pl.pallas_call(
        body, name="branch_mix", grid=(S // tm,),
        in_specs=[row(FOXW), row(DILOUT), full3(was), full3(wbs), row(2 * D)],
        out_specs=[row(D), row(D), row(D)],
        out_shape=[jax.ShapeDtypeStruct((S, D), f32), jax.ShapeDtypeStruct((S, D), f32),
                   jax.ShapeDtypeStruct((S, D), bf16)],
        compiler_params=_params(("parallel",)),
    )(oa, ob, was, wbs, gates)


def _outproj_norm(mixed, wout, x, g2):
    tm = 512

    def body(m_ref, w_ref, x_ref, g_ref, x2_ref, h2_ref):
        x2 = x_ref[...] + _dot(m_ref[...], w_ref[...])
        x2_ref[...] = x2
        r = lax.rsqrt(jnp.mean(x2 * x2, axis=-1, keepdims=True) + EPS)
        h2_ref[...] = ((x2 * r) * g_ref[...]).astype(bf16)

    row = pl.BlockSpec((tm, D), lambda i: (i, 0))
    return pl.pallas_call(
        body, name="outproj_norm", grid=(S // tm,),
        in_specs=[row, pl.BlockSpec((D, D), lambda i: (0, 0)), row, pl.BlockSpec((1, D), lambda i: (0, 0))],
        out_specs=[row, row],
        out_shape=[jax.ShapeDtypeStruct((S, D), f32), jax.ShapeDtypeStruct((S, D), bf16)],
        compiler_params=_params(("parallel",)),
    )(mixed, wout, x, g2)


def _mlp_up(h2, wups):
    tm = 512

    def body(h_ref, w_ref, u_ref, a_ref):
        u = _dot(h_ref[...], w_ref[...])
        u_ref[...] = u
        ru = jnp.maximum(u, 0.0)
        a_ref[...] = (ru * ru).astype(bf16)

    out = pl.BlockSpec((tm, D), lambda q, i: (i, q))
    return pl.pallas_call(
        body, name="mlp_up", grid=(NCHIP, S // tm),
        in_specs=[pl.BlockSpec((tm, D), lambda q, i: (i, 0)), pl.BlockSpec((None, D, D), lambda q, i: (q, 0, 0))],
        out_specs=[out, out],
        out_shape=[jax.ShapeDtypeStruct((S, DFF), f32), jax.ShapeDtypeStruct((S, DFF), bf16)],
        compiler_params=_params(("parallel", "parallel")),
    )(h2, wups)


def _mlp_down_loss(a, wdown, x2, g3, tgt):
    tm = 256

    def body(a_ref, w_ref, x2_ref, g_ref, t_ref, dx_ref, dxb_ref, dg_ref, loss_ref):
        i = pl.program_id(0)
        x3 = x2_ref[...] + _dot(a_ref[...], w_ref[...])
        r = lax.rsqrt(jnp.mean(x3 * x3, axis=-1, keepdims=True) + EPS)
        xh = x3 * r
        g = g_ref[...]
        e = xh * g - t_ref[...]
        part = 0.5 * jnp.sum(jnp.mean(e * e, axis=-1, keepdims=True), axis=0, keepdims=True)
        dy = e * (1.0 / D)
        gdy = dy * g
        dx = r * (gdy - xh * jnp.mean(gdy * xh, axis=-1, keepdims=True))
        dx_ref[...] = dx
        dxb_ref[...] = dx.astype(bf16)

        @pl.when(i == 0)
        def _():
            dg_ref[...] = jnp.zeros_like(dg_ref)
            loss_ref[...] = jnp.zeros_like(loss_ref)

        dg_ref[...] += jnp.sum(dy * xh, axis=0, keepdims=True)
        loss_ref[...] += jnp.broadcast_to(part, (1, 128))

    row = pl.BlockSpec((tm, D), lambda i: (i, 0))
    vec = pl.BlockSpec((1, D), lambda i: (0, 0))
    return pl.pallas_call(
        body, name="mlp_down_loss", grid=(S // tm,),
        in_specs=[pl.BlockSpec((tm, DFF), lambda i: (i, 0)), pl.BlockSpec((DFF, D), lambda i: (0, 0)), row, vec, row],
        out_specs=[row, row, vec, pl.BlockSpec((1, 128), lambda i: (0, 0))],
        out_shape=[jax.ShapeDtypeStruct((S, D), f32), jax.ShapeDtypeStruct((S, D), bf16),
                   jax.ShapeDtypeStruct((1, D), f32), jax.ShapeDtypeStruct((1, 128), f32)],
        compiler_params=_params(("arbitrary",)),
    )(a, wdown, x2, g3, tgt)


def _mlp_down_bwd(dx3b, wdown, u):
    tm = 256

    def body(d_ref, w_ref, u_ref, du_ref):
        d = d_ref[...]
        for q in range(NCHIP):
            cols = slice(q * D, (q + 1) * D)
            da = _dot_nt(d, w_ref[cols, :])
            du_ref[:, cols] = (da * (2.0 * jnp.maximum(u_ref[:, cols], 0.0))).astype(bf16)

    return pl.pallas_call(
        body, name="mlp_down_bwd", grid=(S // tm,),
        in_specs=[pl.BlockSpec((tm, D), lambda i: (i, 0)), pl.BlockSpec((DFF, D), lambda i: (0, 0)),
                  pl.BlockSpec((tm, DFF), lambda i: (i, 0))],
        out_specs=pl.BlockSpec((tm, DFF), lambda i: (i, 0)),
        out_shape=jax.ShapeDtypeStruct((S, DFF), bf16),
        compiler_params=_params(("parallel",)),
    )(dx3b, wdown, u)


def _mlp_up_bwd(du, wups, x2, dx3, g2):
    tm = 256

    def body(du_ref, w_ref, x2_ref, dx3_ref, g_ref, dx2_ref, dx2b_ref, dg_ref):
        i = pl.program_id(0)
        dh = jnp.zeros((tm, D), f32)
        for q in range(NCHIP):
            dh = dh + _dot_nt(du_ref[:, q * D:(q + 1) * D], w_ref[q])
        x2 = x2_ref[...]
        r = lax.rsqrt(jnp.mean(x2 * x2, axis=-1, keepdims=True) + EPS)
        xh = x2 * r
        gdh = dh * g_ref[...]
        dx2 = dx3_ref[...] + r * (gdh - xh * jnp.mean(gdh * xh, axis=-1, keepdims=True))
        dx2_ref[...] = dx2
        dx2b_ref[...] = dx2.astype(bf16)

        @pl.when(i == 0)
        def _():
            dg_ref[...] = jnp.zeros_like(dg_ref)

        dg_ref[...] += jnp.sum(dh * xh, axis=0, keepdims=True)

    row = pl.BlockSpec((tm, D), lambda i: (i, 0))
    vec = pl.BlockSpec((1, D), lambda i: (0, 0))
    return pl.pallas_call(
        body, name="mlp_up_bwd", grid=(S // tm,),
        in_specs=[pl.BlockSpec((tm, DFF), lambda i: (i, 0)), pl.BlockSpec((NCHIP, D, D), lambda i: (0, 0, 0)),
                  row, row, vec],
        out_specs=[row, row, vec],
        out_shape=[jax.ShapeDtypeStruct((S, D), f32), jax.ShapeDtypeStruct((S, D), bf16),
                   jax.ShapeDtypeStruct((1, D), f32)],
        compiler_params=_params(("arbitrary",)),
    )(du, wups, x2, dx3, g2)


def _gate_bwd(dx2b, wout, gates, ya, yb):
    tm = 256

    def body(d_ref, w_ref, g_ref, ya_ref, yb_ref, dya_ref, dyb_ref, dg_ref):
        dm = _dot_nt(d_ref[...], w_ref[...])
        sa = jax.nn.sigmoid(g_ref[:, 0:D])
        sb = jax.nn.sigmoid(g_ref[:, D:2 * D])
        dya_ref[...] = (dm * sa).astype(bf16)
        dyb_ref[...] = (dm * sb).astype(bf16)
        dg_ref[:, 0:D] = (dm * ya_ref[...] * (sa * (1.0 - sa))).astype(bf16)
        dg_ref[:, D:2 * D] = (dm * yb_ref[...] * (sb * (1.0 - sb))).astype(bf16)

    row = lambda w: pl.BlockSpec((tm, w), lambda i: (i, 0))
    return pl.pallas_call(
        body, name="gate_bwd", grid=(S // tm,),
        in_specs=[row(D), pl.BlockSpec((D, D), lambda i: (0, 0)), row(2 * D), row(D), row(D)],
        out_specs=[row(D), row(D), row(2 * D)],
        out_shape=[jax.ShapeDtypeStruct((S, D), bf16), jax.ShapeDtypeStruct((S, D), bf16),
                   jax.ShapeDtypeStruct((S, 2 * D), bf16)],
        compiler_params=_params(("parallel",)),
    )(dx2b, wout, gates, ya, yb)


def _branch_bwd(dya, dyb, was, wbs):
    tm = 512

    def body(dya_ref, dyb_ref, wa_ref, wb_ref, doa_ref, dob_ref):
        doa = jnp.zeros((tm, FOXW), f32)
        dob = jnp.zeros((tm, DILOUT), f32)
        for q in range(NCHIP):
            cols = slice(q * 256, (q + 1) * 256)
            doa = doa + _dot_nt(dya_ref[:, cols], wa_ref[q])
            dob = dob + _dot_nt(dyb_ref[:, cols], wb_ref[q])
        doa_ref[...] = doa.astype(bf16)
        dob_ref[...] = dob

    row = lambda w: pl.BlockSpec((tm, w), lambda i: (i, 0))
    full3 = lambda a: pl.BlockSpec(a.shape, lambda i: (0, 0, 0))
    return pl.pallas_call(
        body, name="branch_bwd", grid=(S // tm,),
        in_specs=[row(D), row(D), full3(was), full3(wbs)],
        out_specs=[row(FOXW), row(DILOUT)],
        out_shape=[jax.ShapeDtypeStruct((S, FOXW), bf16), jax.ShapeDtypeStruct((S, DILOUT), f32)],
        compiler_params=_params(("parallel",)),
    )(dya, dyb, was, wbs)


def _branch_wgrad(oa, ob, dya, dyb):
    def body(oa_ref, ob_ref, dya_ref, dyb_ref, dwa_ref, dwb_ref):
        dwa_ref[...] = _dot_tn(oa_ref[...], dya_ref[...])
        dwb_ref[...] = _dot_tn(ob_ref[...], dyb_ref[...])

    full = lambda w: pl.BlockSpec((S, w), lambda q: (0, 0))
    colq = pl.BlockSpec((S, 256), lambda q: (0, q))
    return pl.pallas_call(
        body, name="branch_wgrad", grid=(NCHIP,),
        in_specs=[full(FOXW), full(DILOUT), colq, colq],
        out_specs=[pl.BlockSpec((None, FOXW, 256), lambda q: (q, 0, 0)),
                   pl.BlockSpec((None, DILOUT, 256), lambda q: (q, 0, 0))],
        out_shape=[jax.ShapeDtypeStruct((NCHIP, FOXW, 256), f32), jax.ShapeDtypeStruct((NCHIP, DILOUT, 256), f32)],
        compiler_params=_params(("parallel",)),
    )(oa, ob, dya, dyb)


def _fox_bwd(qkva, doa, oa, lse, F, ftb):
    nb = S // TQ

    def body(q_ref, k_ref, v_ref, do_ref, o_ref, lse_ref, F_ref, ftb_ref, dq_ref, dk_ref, dv_ref, dft_ref, dfq_ref,
             dq_scr):
        p = pl.program_id(0)
        lane, hm = _head_masks()
        sub8 = lax.broadcasted_iota(jnp.int32, (8, 1), 0)
        rowi = lax.broadcasted_iota(jnp.int32, (TQ, 1), 0)
        coli = lax.broadcasted_iota(jnp.int32, (1, TQ), 1)
        dq_scr[...] = jnp.zeros_like(dq_scr)
        dfq_ref[...] = jnp.zeros_like(dfq_ref)

        def kvblock(j, c):
            c0 = pl.multiple_of(j * TQ, TQ)
            k = k_ref[pl.ds(c0, TQ), :]
            v = v_ref[pl.ds(c0, TQ), :]
            kf = k.astype(f32)
            km = [jnp.where(hm[hh], kf, 0.0).astype(bf16) for hh in (0, 1)]
            frow = ftb_ref[j]
            fr = [jnp.sum(jnp.where(sub8 == 2 * p + hh, frow, 0.0), axis=0, keepdims=True) for hh in (0, 1)]

            def qblock(i, carry):
                dk, dv, df0, df1 = carry
                df = [df0, df1]
                r0 = pl.multiple_of(i * TQ, TQ)
                q = q_ref[pl.ds(r0, TQ), :].astype(f32) * 0.125
                do = do_ref[pl.ds(r0, TQ), :].astype(f32)
                prod = do * o_ref[pl.ds(r0, TQ), :].astype(f32)
                lseb = lse_ref[pl.ds(r0, TQ), :]
                Fb = F_ref[pl.ds(r0, TQ), :]
                causal = (c0 + coli) <= (r0 + rowi)
                dqacc = jnp.zeros((TQ, 128), f32)
                rowsum = jnp.zeros((TQ, 128), f32)
                for hh in (0, 1):
                    qh = jnp.where(hm[hh], q, 0.0).astype(bf16)
                    doh = jnp.where(hm[hh], do, 0.0).astype(bf16)
                    delta = jnp.sum(jnp.where(hm[hh], prod, 0.0), axis=1, keepdims=True)
                    fc = jnp.sum(jnp.where(lane == 2 * p + hh, Fb, 0.0), axis=1, keepdims=True)
                    s = _dot_nt(qh, k) + (fc - fr[hh])
                    pr = jnp.where(causal, jnp.exp(s - lseb[:, hh * HD:hh * HD + 1]), 0.0)
                    dp = _dot_nt(doh, v)
                    ds = pr * (dp - delta)
                    dsb = ds.astype(bf16)
                    dv = dv + _dot_tn(pr.astype(bf16), doh)
                    dk = dk + _dot_tn(dsb, qh)
                    dqacc = dqacc + _dot(dsb, km[hh])
                    df[hh] = df[hh] - jnp.sum(ds, axis=0, keepdims=True)
                    rowsum = jnp.where(hm[hh], jnp.sum(ds, axis=1, keepdims=True), rowsum)
                dq_scr[pl.ds(r0, TQ), :] += dqacc * 0.125
                dfq_ref[pl.ds(r0, TQ), :] += rowsum
                return dk, dv, df[0], df[1]

            z = jnp.zeros((TQ, 128), f32)
            zr = jnp.zeros((1, TQ), f32)
            dk, dv, df0, df1 = lax.fori_loop(j, nb, qblock, (z, z, zr, zr))
            dk_ref[pl.ds(c0, TQ), :] = dk.astype(bf16)
            dv_ref[pl.ds(c0, TQ), :] = dv.astype(bf16)
            dft_ref[j] = jnp.where(sub8 == 0, df0, jnp.where(sub8 == 1, df1, 0.0))
            return c

        lax.fori_loop(0, nb, kvblock, 0)
        dq_ref[...] = dq_scr[...].astype(bf16)

    blk = lambda off: pl.BlockSpec((S, 128), lambda p: (0, off + p))
    return pl.pallas_call(
        body, name="fox_bwd", grid=(4,),
        in_specs=[blk(0), blk(4), blk(8), blk(0), blk(0), blk(0), pl.BlockSpec((S, 128), lambda p: (0, 0)),
                  pl.BlockSpec((nb, 8, TQ), lambda p: (0, 0, 0))],
        out_specs=[blk(0), blk(0), blk(0), pl.BlockSpec((None, nb, 8, TQ), lambda p: (p, 0, 0, 0)), blk(0)],
        out_shape=[jax.ShapeDtypeStruct((S, FOXW), bf16)] * 3 + [jax.ShapeDtypeStruct((4, nb, 8, TQ), f32),
                                                                 jax.ShapeDtypeStruct((S, FOXW), f32)],
        scratch_shapes=[pltpu.VMEM((S, 128), f32)],
        compiler_params=_params(("parallel",)),
    )(qkva, qkva, qkva, doa, oa, lse, F, ftb)


def _forget_bwd(dft, dfq, fa, bpad):
    nb = S // TQ

    def body(dft_ref, dfq_ref, fa_ref, b_ref, dfa_ref, db_ref, rows):
        rr = lax.broadcasted_iota(jnp.int32, (TQ, TQ), 0)
        cc = lax.broadcasted_iota(jnp.int32, (TQ, TQ), 1)
        upper = (cc >= rr).astype(bf16)
        ones = jnp.ones((8, TQ), bf16)
        lane = lax.broadcasted_iota(jnp.int32, (1, 128), 1)
        carry = jnp.zeros((1, 128), f32)
        db = jnp.zeros((1, 128), f32)
        rows[...] = jnp.zeros_like(rows)
        for b in reversed(range(nb)):
            for p in range(4):
                rows[2 * p:2 * p + 2, :] = dft_ref[p, b, 0:2, :]
            cols = jnp.zeros((TQ, 128), f32)
            for h in range(8):
                c0 = (h // 2) * 128 + (h % 2) * HD
                cols = jnp.where(lane == h, dfq_ref[b * TQ:(b + 1) * TQ, c0:c0 + 1], cols)
            dlf = carry
            tot = jnp.zeros((8, 128), f32)
            for part in _split3(rows[...]):
                dlf = dlf + _dot_nt(upper, part)
                tot = tot + _dot_nt(ones, part)
            for part in _split3(cols):
                dlf = dlf + _dot(upper, part)
            carry = carry + tot[0:1, :] + jnp.sum(cols, axis=0, keepdims=True)
            z = fa_ref[b * TQ:(b + 1) * TQ, :] + b_ref[...]
            dz = jnp.where(lane < 8, dlf * jax.nn.sigmoid(-z), 0.0)
            dfa_ref[b * TQ:(b + 1) * TQ, :] = dz.astype(bf16)
            db = db + jnp.sum(dz, axis=0, keepdims=True)
        db_ref[...] = db

    return pl.pallas_call(
        body, name="forget_bwd",
        out_shape=[jax.ShapeDtypeStruct((S, 128), bf16), jax.ShapeDtypeStruct((1, 128), f32)],
        scratch_shapes=[pltpu.VMEM((128, TQ), f32)],
        compiler_params=_params(),
    )(dft, dfq, fa, bpad)


def _dil_bwd(qkvb, dob, ob, lseb, rope):
    c_t, s1_t, s2_t = rope

    def body(*refs):
        q_refs, k_refs, v_refs = refs[0:3], refs[3:6], refs[6:9]
        dob_ref, ob_ref, lse_ref, c_ref, s1_ref, s2_ref = refs[9:15]
        dq_out, dk_out, dv_out = refs[15:18], refs[18:21], refs[21:24]
        qp, kp, vp, dop, lp, dlp, dln, dqp, dkp, dvp, nat = refs[24:35]
        _, hm = _head_masks()

        def delta_rows(i, c):
            r0 = pl.multiple_of(i * TQ, TQ)
            prod = dob_ref[pl.ds(r0, TQ), :] * ob_ref[pl.ds(r0, TQ), :].astype(f32)
            d0 = jnp.sum(jnp.where(hm[0], prod, 0.0), axis=1, keepdims=True)
            d1 = jnp.sum(jnp.where(hm[1], prod, 0.0), axis=1, keepdims=True)
            dln[pl.ds(r0, TQ), :] = jnp.where(hm[0], d0, d1)
            return c

        lax.fori_loop(0, S // TQ, delta_rows, 0)

        for g, r in enumerate(DIL):
            nbl = S // r // BAND
            if r == 1:
                srcs = (q_refs[g], k_refs[g], v_refs[g], dob_ref, lse_ref, dln)
            else:
                for dst, src in ((qp, q_refs[g]), (kp, k_refs[g]), (vp, v_refs[g]), (dop, dob_ref),
                                 (lp, lse_ref), (dlp, dln)):
                    _permute_in(dst, src, r)
                srcs = (qp, kp, vp, dop, lp, dlp)
            dkp[...] = jnp.zeros_like(dkp)
            dvp[...] = jnp.zeros_like(dvp)

            def blk(bb, c, srcs=srcs, nbl=nbl):
                qs_, ks_, vs_, dos_, ls_, dls_ = srcs
                r0, k0, valid = _band_geometry(bb, nbl)
                q = qs_[pl.ds(r0, BAND), :] * 0.125
                kwf = ks_[pl.ds(k0, 2 * BAND), :]
                kw = kwf.astype(bf16)
                vw = vs_[pl.ds(k0, 2 * BAND), :].astype(bf16)
                do = dos_[pl.ds(r0, BAND), :]
                lse = ls_[pl.ds(r0, BAND), :]
                dlt = dls_[pl.ds(r0, BAND), :]
                dq = jnp.zeros((BAND, 128), f32)
                dk = jnp.zeros((2 * BAND, 128), f32)
                dv = jnp.zeros((2 * BAND, 128), f32)
                for hh in (0, 1):
                    qh = jnp.where(hm[hh], q, 0.0).astype(bf16)
                    doh = jnp.where(hm[hh], do, 0.0).astype(bf16)
                    kh = jnp.where(hm[hh], kwf, 0.0).astype(bf16)
                    s = _dot_nt(qh, kw)
                    pr = jnp.where(valid, jnp.exp(s - lse[:, hh * HD:hh * HD + 1]), 0.0)
                    dp = _dot_nt(doh, vw)
                    ds = pr * (dp - dlt[:, hh * HD:hh * HD + 1])
                    dsb = ds.astype(bf16)
                    dv = dv + _dot_tn(pr.astype(bf16), doh)
                    dk = dk + _dot_tn(dsb, qh)
                    dq = dq + _dot(dsb, kh)
                dqp[pl.ds(r0, BAND), :] = dq * 0.125
                dkp[pl.ds(k0, 2 * BAND), :] += dk
                dvp[pl.ds(k0, 2 * BAND), :] += dv
                return c

            lax.fori_loop(0, S // BAND, blk, 0)

            for acc, out, roped in ((dqp, dq_out[g], True), (dkp, dk_out[g], True), (dvp, dv_out[g], False)):
                if r == 1:
                    src = acc
                else:
                    _permute_out(nat, acc, r)
                    src = nat

                def emit(i, c, src=src, out=out, roped=roped):
                    r0 = pl.multiple_of(i * TQ, TQ)
                    d = src[pl.ds(r0, TQ), :]
                    if roped:
                        d = (d * c_ref[pl.ds(r0, TQ), :] + pltpu.roll(d * s1_ref[pl.ds(r0, TQ), :], 8, 1)
                             + pltpu.roll(d * s2_ref[pl.ds(r0, TQ), :], 120, 1))
                    out[pl.ds(r0, TQ), :] = d.astype(bf16)
                    return c

                lax.fori_loop(0, S // TQ, emit, 0)

    pair = pl.BlockSpec((S, 128), lambda p: (0, p))
    tab = pl.BlockSpec((S, 128), lambda p: (0, 0))
    return pl.pallas_call(
        body, name="dil_bwd", grid=(2,),
        in_specs=_dil_in_specs() + [pair, pair, pair, tab, tab, tab],
        out_specs=[pair] * 9,
        out_shape=[jax.ShapeDtypeStruct((S, DILOUT), bf16)] * 9,
        scratch_shapes=[pltpu.VMEM((S, 128), f32)] * 11,
        compiler_params=_params(("parallel",)),
    )(*([qkvb] * 9), dob, ob, lseb, c_t, s1_t, s2_t)


def _inproj_bwd(dproj, wp, x, dx2, g1):
    tm = 256

    def body(d_ref, w_ref, x_ref, dx2_ref, g_ref, dx_ref, dg_ref):
        i = pl.program_id(0)
        dh = _dot_nt(d_ref[...], w_ref[...])
        xb = x_ref[...]
        r = lax.rsqrt(jnp.mean(xb * xb, axis=-1, keepdims=True) + EPS)
        xh = xb * r
        gdh = dh * g_ref[...]
        dx_ref[...] = dx2_ref[...] + r * (gdh - xh * jnp.mean(gdh * xh, axis=-1, keepdims=True))

        @pl.when(i == 0)
        def _():
            dg_ref[...] = jnp.zeros_like(dg_ref)

        dg_ref[...] += jnp.sum(dh * xh, axis=0, keepdims=True)

    row = pl.BlockSpec((tm, D), lambda i: (i, 0))
    vec = pl.BlockSpec((1, D), lambda i: (0, 0))
    return pl.pallas_call(
        body, name="inproj_bwd", grid=(S // tm,),
        in_specs=[pl.BlockSpec((tm, WP), lambda i: (i, 0)), pl.BlockSpec((D, WP), lambda i: (0, 0)), row, row, vec],
        out_specs=[row, vec],
        out_shape=[jax.ShapeDtypeStruct((S, D), f32), jax.ShapeDtypeStruct((1, D), f32)],
        compiler_params=_params(("arbitrary",)),
    )(dproj, wp, x, dx2, g1)


def _pad_win(w_full):
    return jnp.concatenate([w_full[:, 0:1536], w_full[:, 1544:D_IN], w_full[:, 1536:1544],
                            jnp.zeros((D, WP - D_IN), w_full.dtype)], axis=1)


def _unpad_win(dwp):
    return jnp.concatenate([dwp[:, 0:1536], dwp[:, C_FA:C_FA + 8], dwp[:, 1536:C_FA]], axis=1)


HBM = pl.BlockSpec(memory_space=pltpu.HBM)
SEM = pl.BlockSpec(memory_space=pltpu.SEMAPHORE)
ANY = pl.BlockSpec(memory_space=pl.ANY)
SMALL_ROWS = 8


class _Order:
    def __init__(self):
        self.tok = None

    def after(self, v):
        return v if self.tok is None else lax.optimization_barrier((self.tok, v))[1]

    def mark(self, v):
        self.tok = v

    def run(self, fn, first, *rest, **kw):
        out = fn(self.after(first), *rest, **kw)
        self.mark(out[0] if isinstance(out, (tuple, list)) else out)
        return out


def _comm_call(name, body, bufs, order, sems_in=(), new_sems=()):
    nb, ns, nn = len(bufs), len(sems_in), len(new_sems)
    extra = [] if order.tok is None else [order.tok]

    def kern(*refs):
        off = nb + ns + len(extra)
        body(refs[:nb], refs[nb:nb + ns], refs[off:off + nn])
        refs[-1][...] = jnp.zeros((8, 128), f32)

    res = pl.pallas_call(
        kern, name=name,
        in_specs=[HBM] * nb + [SEM] * ns + [ANY] * len(extra),
        out_specs=[SEM] * nn + [HBM] * nb + [pl.BlockSpec(memory_space=pltpu.VMEM)],
        out_shape=[pltpu.SemaphoreType.DMA((k,)) for k in new_sems] + [pltpu.HBM(b.shape, b.dtype) for b in bufs]
        + [jax.ShapeDtypeStruct((8, 128), f32)],
        input_output_aliases={i: nn + i for i in range(nb)},
        compiler_params=pltpu.CompilerParams(has_side_effects=pltpu.SideEffectType.DATAFLOW_SIDE_EFFECTING),
    )(*[pltpu.with_memory_space_constraint(b, pltpu.HBM) for b in bufs], *sems_in, *extra)
    order.mark(res[-1])
    return list(res[:nn]), list(res[nn:nn + nb])


def _place():
    x, y, c = lax.axis_index("x"), lax.axis_index("y"), lax.axis_index("c")
    chips = [(1 - x, y), (x, 1 - y), (1 - x, 1 - y)]
    return x, y, c, chips


def _rcopy(src, dst, ssem, rsem, dev):
    return pltpu.make_async_remote_copy(src_ref=src, dst_ref=dst, send_sem=ssem, recv_sem=rsem,
                                        device_id=dev, device_id_type=pl.DeviceIdType.MESH)


def _half(nrows, which):
    return pl.ds(which * (nrows // 2), nrows // 2)


def _ici_copies(stack, group_sizes, ssems, rsems):
    x, y, c, chips = _place()
    me_q = 2 * x + y
    sends, recvs = [], []
    a = 0
    for grp, size in enumerate(group_sizes):
        for k in range(size):
            rows = _half(stack[a].shape[1], c)
            for j, (cx, cy) in enumerate(chips):
                mine = stack[a].at[me_q, rows]
                sends.append(_rcopy(mine, mine, ssems[grp].at[k * 3 + j], rsems[grp].at[k * 3 + j], (cx, cy, c)))
                theirs = stack[a].at[2 * cx + cy, rows]
                recvs.append(_rcopy(theirs, theirs, ssems[grp].at[k * 3 + j], rsems[grp].at[k * 3 + j],
                                    (cx, cy, c)))
            a += 1
    return sends, recvs


def _allgather_start(stacks, group_sizes, order):
    def body(bufs, _, new):
        sends, _r = _ici_copies(bufs, group_sizes, new[0::2], new[1::2])
        for cp in sends:
            cp.start()

    sizes = []
    for size in group_sizes:
        sizes += [3 * size, 3 * size]
    sems, stacks = _comm_call("allgather_start", body, stacks, order, new_sems=sizes)
    return [(sems[2 * g], sems[2 * g + 1]) for g in range(len(group_sizes))], stacks


def _forward_copies(stack, ssem, rsem):
    x, y, c, chips = _place()
    sib = (x, y, 1 - c)
    sends, recvs = [], []
    for a in range(len(stack)):
        for j, (cx, cy) in enumerate(chips):
            landed = stack[a].at[2 * cx + cy, _half(stack[a].shape[1], c)]
            sends.append(_rcopy(landed, landed, ssem.at[a * 3 + j], rsem.at[a * 3 + j], sib))
            other = stack[a].at[2 * cx + cy, _half(stack[a].shape[1], 1 - c)]
            recvs.append(_rcopy(other, other, ssem.at[a * 3 + j], rsem.at[a * 3 + j], sib))
    return sends, recvs


def _allgather_forward(name, stacks, sems, order):
    n = len(stacks)

    def body(bufs, taken, new):
        sends, recvs = _ici_copies(bufs, [n], [taken[0]], [taken[1]])
        for cp in sends:
            cp.wait_send()
        for cp in recvs:
            cp.wait_recv()
        fwd, _r = _forward_copies(bufs, new[0], new[1])
        for cp in fwd:
            cp.start()

    return _comm_call(name, body, stacks, order, sems_in=sems, new_sems=(3 * n, 3 * n))


def _allgather_finish(name, stacks, sems, order):
    def body(bufs, taken, _):
        sends, recvs = _forward_copies(bufs, taken[0], taken[1])
        for cp in sends:
            cp.wait_send()
        for cp in recvs:
            cp.wait_recv()

    return _comm_call(name, body, stacks, order, sems_in=sems)[1]


def _pair_copies(g, t, ssem, rsem):
    x, y, c, _ = _place()
    return [_rcopy(g[a].at[:, _half(g[a].shape[1], 1 - c), :], t[a], ssem.at[a], rsem.at[a], (x, y, 1 - c))
            for a in range(len(g))]


def _pair_start(name, gs, order):
    n = len(gs)
    ts = [lax.empty((NCHIP, g.shape[1] // 2, g.shape[2]), f32) for g in gs]

    def body(bufs, _, new):
        for cp in _pair_copies(bufs[:n], bufs[n:], new[0], new[1]):
            cp.start()

    sems, bufs = _comm_call(name, body, list(gs) + ts, order, new_sems=(n, n))
    return sems, bufs


def _pair_wait(name, bufs, sems, order):
    n = len(bufs) // 2

    def body(refs, taken, _):
        for cp in _pair_copies(refs[:n], refs[n:], taken[0], taken[1]):
            cp.wait_send()
            cp.wait_recv()

    bufs = _comm_call(name, body, bufs, order, sems_in=sems)[1]
    return bufs[:n], bufs[n:]


def _row_tile(h):
    return min(h, 256)


def _pair_add(g, t, c_arr, name):
    _, R, C = g.shape
    h = R // 2
    tr = _row_tile(h)
    nblk = h // tr

    def body(c_ref, g_ref, t_ref, p32_ref, p16_ref):
        s = g_ref[...] + t_ref[...]
        p32_ref[...] = s
        p16_ref[...] = s.astype(bf16)

    blk = pl.BlockSpec((None, tr, C), lambda q, i, c_ref: (q, i, 0))
    return pl.pallas_call(
        body, name=name,
        grid_spec=pltpu.PrefetchScalarGridSpec(
            num_scalar_prefetch=1, grid=(NCHIP, nblk),
            in_specs=[pl.BlockSpec((None, tr, C), lambda q, i, c_ref: (q, c_ref[0] * nblk + i, 0)), blk],
            out_specs=[blk, blk]),
        out_shape=[jax.ShapeDtypeStruct((NCHIP, h, C), f32), jax.ShapeDtypeStruct((NCHIP, h, C), bf16)],
        compiler_params=_params(("parallel", "parallel")),
    )(c_arr, g, t)


def _shard_copies(p, r, sm, ssem, rsem):
    x, y, c, chips = _place()
    n = len(p)
    sends, recvs = [], []
    for a in range(n):
        for j, (cx, cy) in enumerate(chips):
            k = a * 3 + j
            sends.append(_rcopy(p[a].at[2 * cx + cy], r[a].at[j], ssem.at[k], rsem.at[k], (cx, cy, c)))
            recvs.append(_rcopy(r[a].at[j], r[a].at[j], ssem.at[k], rsem.at[k], (cx, cy, c)))
    if sm is not None:
        mine = sm.at[4 * x + 2 * y + c]
        for i in range(1, 8):
            px = (1 - x) if i & 4 else x
            py = (1 - y) if i & 2 else y
            pc = (1 - c) if i & 1 else c
            k = 3 * n + i - 1
            sends.append(_rcopy(mine, mine, ssem.at[k], rsem.at[k], (px, py, pc)))
            slot = sm.at[4 * px + 2 * py + pc]
            recvs.append(_rcopy(slot, slot, ssem.at[k], rsem.at[k], (px, py, pc)))
    return sends, recvs


def _shard_start(name, p16s, order, sm=None):
    n = len(p16s)
    rs = [lax.empty((3,) + p.shape[1:], bf16) for p in p16s]
    extra = [] if sm is None else [sm]
    nsem = 3 * n + (7 if sm is not None else 0)

    def body(bufs, _, new):
        sends, _r = _shard_copies(bufs[:n], bufs[n:2 * n], bufs[2 * n] if extra else None, new[0], new[1])
        for cp in sends:
            cp.start()

    return _comm_call(name, body, list(p16s) + rs + extra, order, new_sems=(nsem, nsem))


def _shard_wait(name, bufs, sems, n, order):
    has_sm = len(bufs) > 2 * n

    def body(refs, taken, _):
        sends, recvs = _shard_copies(refs[:n], refs[n:2 * n], refs[2 * n] if has_sm else None, taken[0], taken[1])
        for cp in sends:
            cp.wait_send()
        for cp in recvs:
            cp.wait_recv()

    bufs = _comm_call(name, body, bufs, order, sems_in=sems)[1]
    return bufs[n:2 * n], (bufs[2 * n] if has_sm else None)


def _shard_sum(p32, r, q_arr, c_arr, name):
    _, h, C = p32.shape
    tr = _row_tile(h)
    nblk = h // tr

    def body(q_ref, c_ref, p_ref, r_ref, o_ref):
        s = p_ref[...]
        for j in range(3):
            s = s + r_ref[j].astype(f32)
        o_ref[...] = s

    return pl.pallas_call(
        body, name=name,
        grid_spec=pltpu.PrefetchScalarGridSpec(
            num_scalar_prefetch=2, grid=(nblk,),
            in_specs=[pl.BlockSpec((None, tr, C), lambda i, q_ref, c_ref: (q_ref[0], i, 0)),
                      pl.BlockSpec((3, tr, C), lambda i, q_ref, c_ref: (0, i, 0))],
            out_specs=pl.BlockSpec((tr, C), lambda i, q_ref, c_ref: (c_ref[0] * nblk + i, 0))),
        out_shape=jax.ShapeDtypeStruct((2 * h, C), f32),
        compiler_params=_params(("parallel",)),
    )(q_arr, c_arr, p32, r)


def _swap_copies(full, ssem, rsem):
    x, y, c, _ = _place()
    sends, recvs = [], []
    for a in range(len(full)):
        mine = full[a].at[_half(full[a].shape[0], c)]
        sends.append(_rcopy(mine, mine, ssem.at[a], rsem.at[a], (x, y, 1 - c)))
        other = full[a].at[_half(full[a].shape[0], 1 - c)]
        recvs.append(_rcopy(other, other, ssem.at[a], rsem.at[a], (x, y, 1 - c)))
    return sends, recvs


def _swap_start(name, fulls, order):
    n = len(fulls)

    def body(bufs, _, new):
        for cp in _swap_copies(bufs, new[0], new[1])[0]:
            cp.start()

    return _comm_call(name, body, list(fulls), order, new_sems=(n, n))


def _swap_wait(name, fulls, sems, order):
    def body(refs, taken, _):
        sends, recvs = _swap_copies(refs, taken[0], taken[1])
        for cp in sends:
            cp.wait_send()
        for cp in recvs:
            cp.wait_recv()

    return _comm_call(name, body, fulls, order, sems_in=sems)[1]


def _small_sum(sm):
    def body(sm_ref, o_ref):
        s = sm_ref[0]
        for d in range(1, 8):
            s = s + sm_ref[d]
        o_ref[...] = s

    return pl.pallas_call(body, name="small_grad_sum", out_shape=jax.ShapeDtypeStruct((SMALL_ROWS, D), f32))(sm)


def _adamw(w, g, m, v, name):
    R, C = w.shape
    tr = min(R, 256)

    def body(w_ref, g_ref, m_ref, v_ref, d_ref, nm_ref, nv_ref):
        g_ = g_ref[...]
        m_ = ADAM_B1 * m_ref[...] + (1.0 - ADAM_B1) * g_
        v_ = ADAM_B2 * v_ref[...] + (1.0 - ADAM_B2) * (g_ * g_)
        m_hat = m_ / (1.0 - ADAM_B1 ** ADAM_STEP)
        v_hat = v_ / (1.0 - ADAM_B2 ** ADAM_STEP)
        d_ref[...] = -ADAM_LR * (m_hat / (jnp.sqrt(v_hat) + ADAM_EPS) + ADAM_WD * w_ref[...])
        nm_ref[...] = m_
        nv_ref[...] = v_

    blk = pl.BlockSpec((tr, C), lambda i: (i, 0))
    return pl.pallas_call(
        body, name=name, grid=(R // tr,), in_specs=[blk] * 4, out_specs=[blk] * 3,
        out_shape=[jax.ShapeDtypeStruct((R, C), f32)] * 3,
        compiler_params=_params(("parallel",)),
    )(w, g, m, v)


def _stack_cols(w, n):
    K_, N = w.shape
    return jnp.stack([w[:, q * (N // n):(q + 1) * (N // n)] for q in range(n)], axis=0)


def kernel(x, norm_attn_g, w_in, b_forget, w_branch_a, w_branch_b, w_out, norm_mlp_g, w_up, w_down, norm_final_g, loss_target, m_norm_attn_g, m_w_in, m_b_forget, m_w_branch_a, m_w_branch_b, m_w_out, m_norm_mlp_g, m_w_up, m_w_down, m_norm_final_g, v_norm_attn_g, v_w_in, v_b_forget, v_w_branch_a, v_w_branch_b, v_w_out, v_norm_mlp_g, v_w_up, v_w_down, v_norm_final_g):
    xi, yi, ci = lax.axis_index("x"), lax.axis_index("y"), lax.axis_index("c")
    c_arr = jnp.reshape(ci, (1,)).astype(jnp.int32)
    q_arr = jnp.reshape(2 * xi + yi, (1,)).astype(jnp.int32)
    x_, tgt = x[0], loss_target[0]

    names = ["w_in", "w_branch_a", "w_branch_b", "w_out", "w_up", "w_down"]
    big = dict(zip(names, [w_in[0], w_branch_a[0], w_branch_b[0], w_out[0], w_up[0], w_down[0]]))
    ms = dict(zip(names, [m_w_in[0], m_w_branch_a[0], m_w_branch_b[0], m_w_out[0], m_w_up[0], m_w_down[0]]))
    vs = dict(zip(names, [v_w_in[0], v_w_branch_a[0], v_w_branch_b[0], v_w_out[0], v_w_up[0], v_w_down[0]]))
    grad, upd = {}, {}

    stacks = [lax.dynamic_update_slice(lax.empty((NCHIP,) + w.shape, bf16), w.astype(bf16)[None], (q_arr[0], 0, 0))
              for w in big.values()]
    order = _Order()
    run = order.run
    (sem_in, sem_rest), stacks = _allgather_start(stacks, [1, 5], order)
    sem_f, win_s = _allgather_forward("allgather_forward_in", stacks[0:1], sem_in, order)
    (win_s,) = _allgather_finish("allgather_finish_in", win_s, sem_f, order)
    wp = _pad_win(jnp.concatenate([win_s[q] for q in range(NCHIP)], axis=1))

    rope = _rope_tables()
    bpad = jnp.pad(b_forget, ((0, 0), (0, 120)))
    h1, qkva, qkvb, gates, fa = run(_norm_inproj, x_, norm_attn_g, wp, rope)
    F, ftb = run(_forget_cumsum, fa, bpad)
    oa, lsea = run(_fox_fwd, qkva, F, ftb)
    sem_f, rest = _allgather_forward("allgather_forward_rest", stacks[1:], sem_rest, order)
    ob, lseb = run(_dil_fwd, qkvb)
    was, wbs, wouts, wups, wdowns = _allgather_finish("allgather_finish_rest", rest, sem_f, order)
    wout = wouts.reshape(D, D)
    wdown = wdowns.reshape(DFF, D)
    ya, yb, mixed = run(_branch_mix, oa, ob, was, wbs, gates)
    x2, h2 = run(_outproj_norm, mixed, wout, x_, norm_mlp_g)
    u, a = run(_mlp_up, h2, wups)
    dx3, dx3b, dg3, loss_part = run(_mlp_down_loss, a, wdown, x2, norm_final_g.reshape(1, D), tgt)
    loss = lax.psum(loss_part[0, 0], ("x", "y", "c"))

    def reduce_to_pairs(tag, group, bufs, sems):
        gs, ts = _pair_wait("pair_wait_" + tag, bufs, sems, order)
        return zip(*[run(_pair_add, gs[i], ts[i], c_arr, "pair_add_" + nm) for i, nm in enumerate(group)])

    def reduce_to_shard(tag, group, p32s, bufs, sems):
        rs, sm_all = _shard_wait("shard_wait_" + tag, bufs, sems, len(group), order)
        fulls = [run(_shard_sum, p32s[i], rs[i], q_arr, c_arr, "shard_sum_" + nm) for i, nm in enumerate(group)]
        return _swap_start("swap_start_" + tag, fulls, order), sm_all

    def finish(tag, group, fulls, sems):
        fulls = _swap_wait("swap_wait_" + tag, fulls, sems, order)
        for nm, gfull in zip(group, fulls):
            grad[nm] = gfull
            upd[nm] = run(_adamw, big[nm], gfull, ms[nm], vs[nm], "adamw_" + nm)

    grp_a, grp_b, grp_c = ["w_down", "w_up"], ["w_out", "w_branch_a", "w_branch_b"], ["w_in"]
    du = run(_mlp_down_bwd, dx3b, wdown, u)
    dwdown = run(_mm, a, dx3b, "tn", f32, 1024, D, "wgrad_down")
    dwup = run(_mm, h2, du, "tn", f32, D, 1024, "wgrad_up", stack_cols=True)
    sem_pa, buf_pa = _pair_start("pair_start_a", [dwdown.reshape(NCHIP, DFF // NCHIP, D), dwup], order)
    dx2, dx2b, dg2 = run(_mlp_up_bwd, du, wups, x2, dx3, norm_mlp_g)
    p32_a, p16_a = reduce_to_pairs("a", grp_a, buf_pa, sem_pa)
    sem_sa, buf_sa = _shard_start("shard_start_a", p16_a, order)
    dya, dyb, dgates = run(_gate_bwd, dx2b, wout, gates, ya, yb)
    dwout = run(_mm, mixed, dx2b, "tn", f32, D, D, "wgrad_out")
    doa, dob = run(_branch_bwd, dya, dyb, was, wbs)
    dwas, dwbs = run(_branch_wgrad, oa, ob, dya, dyb)
    sem_pb, buf_pb = _pair_start("pair_start_b", [dwout.reshape(NCHIP, D // NCHIP, D), dwas, dwbs], order)
    dqa, dka, dva, dft, dfq = run(_fox_bwd, qkva, doa, oa, lsea, F, ftb)
    p32_b, p16_b = reduce_to_pairs("b", grp_b, buf_pb, sem_pb)
    (sem_wa, fulls_a), _ = reduce_to_shard("a", grp_a, p32_a, buf_sa, sem_sa)
    sem_sb, buf_sb = _shard_start("shard_start_b", p16_b, order)
    dfa, dbf = run(_forget_bwd, dft, dfq, fa, bpad)
    dd = run(_dil_bwd, qkvb, dob, ob, lseb, rope)
    (sem_wb, fulls_b), _ = reduce_to_shard("b", grp_b, p32_b, buf_sb, sem_sb)
    finish("a", grp_a, fulls_a, sem_wa)
    dproj = jnp.concatenate([dqa, dka, dva, *dd, dgates, dfa], axis=1)
    dwp = run(_mm, h1, dproj, "tn", f32, D, 128, "wgrad_in")
    sem_pc, buf_pc = _pair_start("pair_start_c", [_stack_cols(_unpad_win(dwp), NCHIP)], order)
    gx, dg1 = run(_inproj_bwd, dproj, wp, x_, dx2, norm_attn_g)
    p32_c, p16_c = reduce_to_pairs("c", grp_c, buf_pc, sem_pc)
    small = jnp.concatenate([dg1, dg2, dg3, jnp.pad(dbf[:, 0:8], ((0, 0), (0, D - 8))),
                             jnp.zeros((SMALL_ROWS - 4, D), f32)], axis=0)
    sm = lax.dynamic_update_slice(lax.empty((8, SMALL_ROWS, D), f32), small[None],
                                  (4 * xi + 2 * yi + ci, 0, 0))
    sem_sc, buf_sc = _shard_start("shard_start_c", p16_c, order, sm)
    finish("b", grp_b, fulls_b, sem_wb)
    (sem_wc, fulls_c), sm = reduce_to_shard("c", grp_c, p32_c, buf_sc, sem_sc)
    gsmall = run(_small_sum, sm)

    grad["norm_attn_g"], grad["norm_mlp_g"] = gsmall[0:1], gsmall[1:2]
    grad["norm_final_g"], grad["b_forget"] = gsmall[2:3], gsmall[3:4, 0:8]
    upd["norm_attn_g"] = run(_adamw, norm_attn_g, grad["norm_attn_g"], m_norm_attn_g, v_norm_attn_g, "adamw_g1")
    upd["norm_mlp_g"] = run(_adamw, norm_mlp_g, grad["norm_mlp_g"], m_norm_mlp_g, v_norm_mlp_g, "adamw_g2")
    upd["norm_final_g"] = run(_adamw, norm_final_g.reshape(1, D), grad["norm_final_g"],
                              m_norm_final_g.reshape(1, D), v_norm_final_g.reshape(1, D), "adamw_g3")
    upd["b_forget"] = run(_adamw, b_forget, grad["b_forget"], m_b_forget, v_b_forget, "adamw_bf")
    finish("c", grp_c, fulls_c, sem_wc)

    order = ["norm_attn_g", "w_in", "b_forget", "w_branch_a", "w_branch_b", "w_out", "norm_mlp_g", "w_up", "w_down",
             "norm_final_g"]
    shapes = dict(norm_attn_g=norm_attn_g.shape, w_in=w_in.shape, b_forget=b_forget.shape,
                  w_branch_a=w_branch_a.shape, w_branch_b=w_branch_b.shape, w_out=w_out.shape,
                  norm_mlp_g=norm_mlp_g.shape, w_up=w_up.shape, w_down=w_down.shape, norm_final_g=norm_final_g.shape)
    outs = [loss, gx.reshape(x.shape)]
    outs += [grad[nm].reshape(shapes[nm]) for nm in order]
    for k in range(3):
        outs += [upd[nm][k].reshape(shapes[nm]) for nm in order]
    return tuple(outs)
```

```python
import functools

import jax
import jax.numpy as jnp
import numpy as np
from jax import lax
from jax.experimental import pallas as pl
from jax.experimental.pallas import tpu as pltpu

f32 = jnp.float32
bf16 = jnp.bfloat16

S = 2048
D = 1024
DFF = 4096
HD = 64
FOXW = 512
DILW = 768
DILOUT = 256
DIL = (1, 4, 16)
BAND = 128
EPS = 1e-6
NEG = -1e30
ROPE_THETA = 500000.0
NCHIP = 4

C_QKVA, C_QB, C_KB, C_VB, C_G, C_FA, WP = 0, 1536, 2304, 3072, 3840, 5888, 6016
D_IN = 5896
SHARD_IN = 1474

ADAM_LR, ADAM_B1, ADAM_B2, ADAM_EPS, ADAM_WD, ADAM_STEP = 0.001, 0.9, 0.999, 1e-08, 0.01, 10

VMEM_LIMIT = 56 * 1024 * 1024
TQ = 256


def _params(sem=None):
    return pltpu.CompilerParams(dimension_semantics=sem, vmem_limit_bytes=VMEM_LIMIT)


class _Order:
    def __init__(self):
        self.tok = None

    def mark(self, v):
        self.tok = v


def _call(order, body, args, in_specs=None, **kw):
    args = list(args)
    n_in = len(args)
    if in_specs is None:
        in_specs = [pl.BlockSpec(memory_space=pltpu.VMEM)] * n_in
    kern = body
    if order.tok is not None:
        in_specs = list(in_specs) + [pl.BlockSpec(memory_space=pl.ANY)]
        args.append(order.tok)

        def kern(*refs):
            body(*refs[:n_in], *refs[n_in + 1:])

    out = pl.pallas_call(kern, in_specs=in_specs, **kw)(*args)
    order.mark(out[0] if isinstance(out, (tuple, list)) else out)
    return out


def _call_indexed(order, body, scalars, args, grid, in_specs, out_specs, **kw):
    args, in_specs = list(args), list(in_specs)
    n_front = len(scalars) + len(args)
    kern = body
    if order.tok is not None:
        in_specs.append(pl.BlockSpec(memory_space=pl.ANY))
        args.append(order.tok)

        def kern(*refs):
            body(*refs[:n_front], *refs[n_front + 1:])

    out = pl.pallas_call(
        kern, grid_spec=pltpu.PrefetchScalarGridSpec(num_scalar_prefetch=len(scalars), grid=grid, in_specs=in_specs,
                                                     out_specs=out_specs), **kw)(*scalars, *args)
    order.mark(out[0] if isinstance(out, (tuple, list)) else out)
    return out


def _dot(a, b):
    return jnp.dot(a, b, preferred_element_type=f32)


def _dot_nt(a, b):
    return lax.dot_general(a, b, (((1,), (1,)), ((), ())), preferred_element_type=f32)


def _dot_tn(a, b):
    return lax.dot_general(a, b, (((0,), (0,)), ((), ())), preferred_element_type=f32)


def _split3(x):
    hi = x.astype(bf16)
    r1 = x - hi.astype(f32)
    mid = r1.astype(bf16)
    lo = (r1 - mid.astype(f32)).astype(bf16)
    return hi, mid, lo


def _rope_tables():
    half = 8
    inv_freq = jnp.power(jnp.float32(ROPE_THETA), -jnp.arange(half, dtype=f32) * 2.0 / 16)
    ang = jnp.arange(S).astype(f32)[:, None] * inv_freq[None, :]
    cos, sin = jnp.cos(ang), jnp.sin(ang)
    one = jnp.ones((S, HD - 16), f32)
    zero = jnp.zeros((S, HD - 16), f32)
    z8 = jnp.zeros((S, 8), f32)
    c = jnp.concatenate([cos, cos, one], axis=1)
    s1 = jnp.concatenate([-sin, z8, zero], axis=1)
    s2 = jnp.concatenate([z8, sin, zero], axis=1)
    return tuple(jnp.concatenate([t, t], axis=1) for t in (c, s1, s2))


def _mm(order, a, b, mode, out_dtype, tm, tn, name, stack_cols=False):
    if mode == "nn":
        (M, K), (_, N) = a.shape, b.shape
        a_spec = pl.BlockSpec((tm, K), lambda i, j: (i, 0))
        b_spec = pl.BlockSpec((K, tn), lambda i, j: (0, j))
        dot = _dot
    elif mode == "nt":
        (M, K), (N, _) = a.shape, b.shape
        a_spec = pl.BlockSpec((tm, K), lambda i, j: (i, 0))
        b_spec = pl.BlockSpec((tn, K), lambda i, j: (j, 0))
        dot = _dot_nt
    else:
        (K, M), (_, N) = a.shape, b.shape
        a_spec = pl.BlockSpec((K, tm), lambda i, j: (0, i))
        b_spec = pl.BlockSpec((K, tn), lambda i, j: (0, j))
        dot = _dot_tn

    def body(a_ref, b_ref, o_ref):
        o_ref[...] = dot(a_ref[...], b_ref[...]).astype(out_dtype)

    if stack_cols:
        assert tm == M
        out_spec = pl.BlockSpec((None, tm, tn), lambda i, j: (j, 0, 0))
        out_shape = jax.ShapeDtypeStruct((N // tn, M, tn), out_dtype)
    else:
        out_spec = pl.BlockSpec((tm, tn), lambda i, j: (i, j))
        out_shape = jax.ShapeDtypeStruct((M, N), out_dtype)
    return _call(
        order, body, (a, b), name=name, grid=(M // tm, N // tn), in_specs=[a_spec, b_spec],
        out_specs=out_spec, out_shape=out_shape,
        compiler_params=_params(("parallel", "parallel")),
    )


def _norm_inproj(order, x, g1, wp, rope):
    tm = 256
    c_t, s1_t, s2_t = rope

    def body(x_ref, g_ref, w_ref, c_ref, s1_ref, s2_ref, h_ref, qkva_ref, qkvb_ref, gates_ref, fa_ref):
        xb = x_ref[...]
        r = lax.rsqrt(jnp.mean(xb * xb, axis=-1, keepdims=True) + EPS)
        h = ((xb * r) * g_ref[...]).astype(bf16)
        h_ref[...] = h
        qkva_ref[...] = _dot(h, w_ref[:, C_QKVA:C_QB]).astype(bf16)
        c, s1, s2 = c_ref[...], s1_ref[...], s2_ref[...]
        for sec, lo in enumerate((C_QB, C_KB)):
            pb = _dot(h, w_ref[:, lo:lo + DILW])
            for ch in range(DILW // 128):
                pc = pb[:, ch * 128:(ch + 1) * 128]
                roped = pc * c + pltpu.roll(pc, 120, 1) * s1 + pltpu.roll(pc, 8, 1) * s2
                qkvb_ref[:, sec * DILW + ch * 128: sec * DILW + (ch + 1) * 128] = roped
        qkvb_ref[:, 2 * DILW:3 * DILW] = _dot(h, w_ref[:, C_VB:C_G])
        gates_ref[...] = _dot(h, w_ref[:, C_G:C_FA])
        fa_ref[...] = _dot(h, w_ref[:, C_FA:WP])

    row = lambda w: pl.BlockSpec((tm, w), lambda i: (i, 0))
    return _call(
        order, body, (x, g1, wp, c_t, s1_t, s2_t), name="norm_inproj", grid=(S // tm,),
        in_specs=[row(D), pl.BlockSpec((1, D), lambda i: (0, 0)), pl.BlockSpec((D, WP), lambda i: (0, 0)),
                  row(128), row(128), row(128)],
        out_specs=[row(D), row(3 * FOXW), row(3 * DILW), row(2 * D), row(128)],
        out_shape=[jax.ShapeDtypeStruct((S, D), bf16), jax.ShapeDtypeStruct((S, 3 * FOXW), bf16),
                   jax.ShapeDtypeStruct((S, 3 * DILW), f32), jax.ShapeDtypeStruct((S, 2 * D), f32),
                   jax.ShapeDtypeStruct((S, 128), f32)],
        compiler_params=_params(("parallel",)),
    )


def _forget_cumsum(order, fa, bpad):
    nb = S // TQ

    def body(fa_ref, b_ref, F_ref, ftb_ref):
        rr = lax.broadcasted_iota(jnp.int32, (TQ, TQ), 0)
        cc = lax.broadcasted_iota(jnp.int32, (TQ, TQ), 1)
        tri = (rr >= cc).astype(bf16)
        lane = lax.broadcasted_iota(jnp.int32, (1, 128), 1)
        carry = jnp.zeros((1, 128), f32)
        for b in range(nb):
            z = fa_ref[b * TQ:(b + 1) * TQ, :] + b_ref[...]
            lf = jnp.minimum(z, 0.0) - jnp.log(1.0 + jnp.exp(-jnp.abs(z)))
            lf = jnp.where(lane < 8, lf, 0.0)
            hi, mid, lo = _split3(lf)
            fb = (_dot(tri, hi) + _dot(tri, mid)) + _dot(tri, lo) + carry
            F_ref[b * TQ:(b + 1) * TQ, :] = fb
            ftb_ref[b] = fb.T[0:8, :]
            carry = fb[TQ - 1:TQ, :]

    return _call(
        order, body, (fa, bpad), name="forget_cumsum",
        out_shape=[jax.ShapeDtypeStruct((S, 128), f32), jax.ShapeDtypeStruct((nb, 8, TQ), f32)],
        compiler_params=_params(),
    )


def _head_masks():
    lane = lax.broadcasted_iota(jnp.int32, (1, 128), 1)
    return lane, (lane < HD, lane >= HD)


def _fox_fwd(order, qkva, F, ftb):
    nb = S // TQ

    def body(q_ref, k_ref, v_ref, F_ref, ftb_ref, o_ref, lse_ref):
        p = pl.program_id(0)
        lane, hm = _head_masks()
        sub8 = lax.broadcasted_iota(jnp.int32, (8, 1), 0)
        rowi = lax.broadcasted_iota(jnp.int32, (TQ, 1), 0)
        coli = lax.broadcasted_iota(jnp.int32, (1, TQ), 1)

        def qblock(i, c):
            r0 = pl.multiple_of(i * TQ, TQ)
            q = q_ref[pl.ds(r0, TQ), :].astype(f32) * 0.125
            qs = [jnp.where(hm[hh], q, 0.0).astype(bf16) for hh in (0, 1)]
            Fb = F_ref[pl.ds(r0, TQ), :]
            fc = [jnp.sum(jnp.where(lane == 2 * p + hh, Fb, 0.0), axis=1, keepdims=True) for hh in (0, 1)]

            def kvblock(j, carry):
                c0 = pl.multiple_of(j * TQ, TQ)
                k = k_ref[pl.ds(c0, TQ), :]
                v = v_ref[pl.ds(c0, TQ), :]
                frow = ftb_ref[j]
                causal = (c0 + coli) <= (r0 + rowi)
                new = []
                for hh in (0, 1):
                    m, l, a = carry[3 * hh:3 * hh + 3]
                    fr = jnp.sum(jnp.where(sub8 == 2 * p + hh, frow, 0.0), axis=0, keepdims=True)
                    s = _dot_nt(qs[hh], k) + (fc[hh] - fr)
                    s = jnp.where(causal, s, NEG)
                    mn = jnp.maximum(m, jnp.max(s, axis=1, keepdims=True))
                    al = jnp.exp(m - mn)
                    pr = jnp.exp(s - mn)
                    l = al * l + jnp.sum(pr, axis=1, keepdims=True)
                    a = al * a + _dot(pr.astype(bf16), v)
                    new += [mn, l, a]
                return tuple(new)

            init = (jnp.full((TQ, 1), NEG, f32), jnp.zeros((TQ, 1), f32), jnp.zeros((TQ, 128), f32)) * 2
            m0, l0, a0, m1, l1, a1 = lax.fori_loop(0, i + 1, kvblock, init)
            o = jnp.where(hm[0], a0 / l0, a1 / l1)
            lse = jnp.where(hm[0], m0 + jnp.log(l0), m1 + jnp.log(l1))
            o_ref[pl.ds(r0, TQ), :] = o.astype(bf16)
            lse_ref[pl.ds(r0, TQ), :] = lse
            return c

        lax.fori_loop(0, nb, qblock, 0)

    blk = lambda off: pl.BlockSpec((S, 128), lambda p: (0, off + p))
    return _call(
        order, body, (qkva, qkva, qkva, F, ftb), name="fox_fwd", grid=(4,),
        in_specs=[blk(0), blk(4), blk(8), pl.BlockSpec((S, 128), lambda p: (0, 0)),
                  pl.BlockSpec((nb, 8, TQ), lambda p: (0, 0, 0))],
        out_specs=[blk(0), blk(0)],
        out_shape=[jax.ShapeDtypeStruct((S, FOXW), bf16), jax.ShapeDtypeStruct((S, FOXW), f32)],
        compiler_params=_params(("parallel",)),
    )


def _permute_in(dst, src, r):
    L = S // r
    for rho in range(r):
        dst[rho * L:(rho + 1) * L, :] = src[pl.ds(rho, L, stride=r), :]


def _permute_out(dst, src, r):
    L = S // r
    for rho in range(r):
        dst[pl.ds(rho, L, stride=r), :] = src[rho * L:(rho + 1) * L, :]


def _band_geometry(bb, nbl):
    r0 = pl.multiple_of(bb * BAND, BAND)
    k0 = pl.multiple_of(jnp.maximum(bb - 1, 0) * BAND, BAND)
    sub0 = (bb - lax.rem(bb, nbl)) * BAND
    qi = r0 + lax.broadcasted_iota(jnp.int32, (BAND, 1), 0)
    ki = k0 + lax.broadcasted_iota(jnp.int32, (1, 2 * BAND), 1)
    diff = qi - ki
    valid = (diff >= 0) & (diff <= BAND) & (ki >= sub0)
    return r0, k0, valid


def _dil_in_specs():
    specs = []
    for role in range(3):
        for g in range(3):
            specs.append(pl.BlockSpec((S, 128), functools.partial(lambda p, o: (0, o + p), o=role * 6 + g * 2)))
    return specs


def _dil_fwd(order, qkvb):
    def body(*refs):
        q_refs, k_refs, v_refs = refs[0:3], refs[3:6], refs[6:9]
        ob_ref, lse_ref = refs[9:11]
        qp, kp, vp, op, lp = refs[11:16]
        on = refs[16:19]
        ln = refs[19:22]
        _, hm = _head_masks()
        for g, r in enumerate(DIL):
            nbl = S // r // BAND
            if r == 1:
                qs_, ks_, vs_, od, ld = q_refs[g], k_refs[g], v_refs[g], on[g], ln[g]
            else:
                _permute_in(qp, q_refs[g], r)
                _permute_in(kp, k_refs[g], r)
                _permute_in(vp, v_refs[g], r)
                qs_, ks_, vs_, od, ld = qp, kp, vp, op, lp

            def blk(bb, c, qs_=qs_, ks_=ks_, vs_=vs_, od=od, ld=ld, nbl=nbl):
                r0, k0, valid = _band_geometry(bb, nbl)
                q = qs_[pl.ds(r0, BAND), :] * 0.125
                kw = ks_[pl.ds(k0, 2 * BAND), :].astype(bf16)
                vw = vs_[pl.ds(k0, 2 * BAND), :]
                o = jnp.zeros((BAND, 128), f32)
                lse = jnp.zeros((BAND, 128), f32)
                for hh in (0, 1):
                    qh = jnp.where(hm[hh], q, 0.0).astype(bf16)
                    s = jnp.where(valid, _dot_nt(qh, kw), NEG)
                    m = jnp.max(s, axis=1, keepdims=True)
                    pr = jnp.exp(s - m)
                    l = jnp.sum(pr, axis=1, keepdims=True)
                    vm = jnp.where(hm[hh], vw, 0.0).astype(bf16)
                    o = o + _dot((pr / l).astype(bf16), vm)
                    lse = jnp.where(hm[hh], m + jnp.log(l), lse)
                od[pl.ds(r0, BAND), :] = o
                ld[pl.ds(r0, BAND), :] = lse
                return c

            lax.fori_loop(0, S // BAND, blk, 0)
            if r != 1:
                _permute_out(on[g], op, r)
                _permute_out(ln[g], lp, r)

        def combine(i, c):
            r0 = pl.multiple_of(i * TQ, TQ)
            ls = [ln[g][pl.ds(r0, TQ), :] for g in range(3)]
            mx = jnp.maximum(jnp.maximum(ls[0], ls[1]), ls[2])
            es = [jnp.exp(l - mx) for l in ls]
            tot = (es[0] + es[1]) + es[2]
            acc = (es[0] / tot) * on[0][pl.ds(r0, TQ), :]
            acc = acc + (es[1] / tot) * on[1][pl.ds(r0, TQ), :]
            acc = acc + (es[2] / tot) * on[2][pl.ds(r0, TQ), :]
            ob_ref[pl.ds(r0, TQ), :] = acc.astype(bf16)
            lse_ref[pl.ds(r0, TQ), :] = mx + jnp.log(tot)
            return c

        lax.fori_loop(0, S // TQ, combine, 0)

    out_blk = pl.BlockSpec((S, 128), lambda p: (0, p))
    return _call(
        order, body, [qkvb] * 9, name="dil_fwd", grid=(2,),
        in_specs=_dil_in_specs(), out_specs=[out_blk, out_blk],
        out_shape=[jax.ShapeDtypeStruct((S, DILOUT), bf16), jax.ShapeDtypeStruct((S, DILOUT), f32)],
        scratch_shapes=[pltpu.VMEM((S, 128), f32)] * 11,
        compiler_params=_params(("parallel",)),
    )


def _branch_mix(order, oa, ob, was, wbs, gates):
    tm = 512

    def body(oa_ref, ob_ref, wa_ref, wb_ref, g_ref, ya_ref, yb_ref, mix_ref):
        oa_b, ob_b = oa_ref[...], ob_ref[...]
        for q in range(NCHIP):
            cols = slice(q * 256, (q + 1) * 256)
            ya = _dot(oa_b, wa_ref[q])
            yb = _dot(ob_b, wb_ref[q])
            ya_ref[:, cols] = ya
            yb_ref[:, cols] = yb
            ga = g_ref[:, q * 256:(q + 1) * 256]
            gb = g_ref[:, D + q * 256:D + (q + 1) * 256]
            mix_ref[:, cols] = (jax.nn.sigmoid(ga) * ya + jax.nn.sigmoid(gb) * yb).astype(bf16)

    row = lambda w: pl.BlockSpec((tm, w), lambda i: (i, 0))
    full3 = lambda a: pl.BlockSpec(a.shape, lambda i: (0, 0, 0))
    return _call(
        order, body, (oa, ob, was, wbs, gates), name="branch_mix", grid=(S // tm,),
        in_specs=[row(FOXW), row(DILOUT), full3(was), full3(wbs), row(2 * D)],
        out_specs=[row(D), row(D), row(D)],
        out_shape=[jax.ShapeDtypeStruct((S, D), f32), jax.ShapeDtypeStruct((S, D), f32),
                   jax.ShapeDtypeStruct((S, D), bf16)],
        compiler_params=_params(("parallel",)),
    )


def _outproj_norm(order, mixed, wout, x, g2):
    tm = 512

    def body(m_ref, w_ref, x_ref, g_ref, x2_ref, h2_ref):
        x2 = x_ref[...] + _dot(m_ref[...], w_ref[...])
        x2_ref[...] = x2
        r = lax.rsqrt(jnp.mean(x2 * x2, axis=-1, keepdims=True) + EPS)
        h2_ref[...] = ((x2 * r) * g_ref[...]).astype(bf16)

    row = pl.BlockSpec((tm, D), lambda i: (i, 0))
    return _call(
        order, body, (mixed, wout, x, g2), name="outproj_norm", grid=(S // tm,),
        in_specs=[row, pl.BlockSpec((D, D), lambda i: (0, 0)), row, pl.BlockSpec((1, D), lambda i: (0, 0))],
        out_specs=[row, row],
        out_shape=[jax.ShapeDtypeStruct((S, D), f32), jax.ShapeDtypeStruct((S, D), bf16)],
        compiler_params=_params(("parallel",)),
    )


def _mlp_up(order, h2, wups):
    tm = 512

    def body(h_ref, w_ref, u_ref, a_ref):
        u = _dot(h_ref[...], w_ref[...])
        u_ref[...] = u
        ru = jnp.maximum(u, 0.0)
        a_ref[...] = (ru * ru).astype(bf16)

    out = pl.BlockSpec((tm, D), lambda q, i: (i, q))
    return _call(
        order, body, (h2, wups), name="mlp_up", grid=(NCHIP, S // tm),
        in_specs=[pl.BlockSpec((tm, D), lambda q, i: (i, 0)), pl.BlockSpec((None, D, D), lambda q, i: (q, 0, 0))],
        out_specs=[out, out],
        out_shape=[jax.ShapeDtypeStruct((S, DFF), f32), jax.ShapeDtypeStruct((S, DFF), bf16)],
        compiler_params=_params(("parallel", "parallel")),
    )


def _mlp_down_loss(order, a, wdown, x2, g3, tgt):
    tm = 256

    def body(a_ref, w_ref, x2_ref, g_ref, t_ref, dx_ref, dxb_ref, dg_ref, loss_ref):
        i = pl.program_id(0)
        x3 = x2_ref[...] + _dot(a_ref[...], w_ref[...])
        r = lax.rsqrt(jnp.mean(x3 * x3, axis=-1, keepdims=True) + EPS)
        xh = x3 * r
        g = g_ref[...]
        e = xh * g - t_ref[...]
        part = 0.5 * jnp.sum(jnp.mean(e * e, axis=-1, keepdims=True), axis=0, keepdims=True)
        dy = e * (1.0 / D)
        gdy = dy * g
        dx = r * (gdy - xh * jnp.mean(gdy * xh, axis=-1, keepdims=True))
        dx_ref[...] = dx
        dxb_ref[...] = dx.astype(bf16)

        @pl.when(i == 0)
        def _():
            dg_ref[...] = jnp.zeros_like(dg_ref)
            loss_ref[...] = jnp.zeros_like(loss_ref)

        dg_ref[...] += jnp.sum(dy * xh, axis=0, keepdims=True)
        loss_ref[...] += jnp.broadcast_to(part, (1, 128))

    row = pl.BlockSpec((tm, D), lambda i: (i, 0))
    vec = pl.BlockSpec((1, D), lambda i: (0, 0))
    return _call(
        order, body, (a, wdown, x2, g3, tgt), name="mlp_down_loss", grid=(S // tm,),
        in_specs=[pl.BlockSpec((tm, DFF), lambda i: (i, 0)), pl.BlockSpec((DFF, D), lambda i: (0, 0)), row, vec, row],
        out_specs=[row, row, vec, pl.BlockSpec((1, 128), lambda i: (0, 0))],
        out_shape=[jax.ShapeDtypeStruct((S, D), f32), jax.ShapeDtypeStruct((S, D), bf16),
                   jax.ShapeDtypeStruct((1, D), f32), jax.ShapeDtypeStruct((1, 128), f32)],
        compiler_params=_params(("arbitrary",)),
    )


def _mlp_down_bwd(order, dx3b, wdown, u):
    tm = 256

    def body(d_ref, w_ref, u_ref, du_ref):
        d = d_ref[...]
        for q in range(NCHIP):
            cols = slice(q * D, (q + 1) * D)
            da = _dot_nt(d, w_ref[cols, :])
            du_ref[:, cols] = (da * (2.0 * jnp.maximum(u_ref[:, cols], 0.0))).astype(bf16)

    return _call(
        order, body, (dx3b, wdown, u), name="mlp_down_bwd", grid=(S // tm,),
        in_specs=[pl.BlockSpec((tm, D), lambda i: (i, 0)), pl.BlockSpec((DFF, D), lambda i: (0, 0)),
                  pl.BlockSpec((tm, DFF), lambda i: (i, 0))],
        out_specs=pl.BlockSpec((tm, DFF), lambda i: (i, 0)),
        out_shape=jax.ShapeDtypeStruct((S, DFF), bf16),
        compiler_params=_params(("parallel",)),
    )


def _mlp_up_bwd(order, du, wups, x2, dx3, g2):
    tm = 256

    def body(du_ref, w_ref, x2_ref, dx3_ref, g_ref, dx2_ref, dx2b_ref, dg_ref):
        i = pl.program_id(0)
        dh = jnp.zeros((tm, D), f32)
        for q in range(NCHIP):
            dh = dh + _dot_nt(du_ref[:, q * D:(q + 1) * D], w_ref[q])
        x2 = x2_ref[...]
        r = lax.rsqrt(jnp.mean(x2 * x2, axis=-1, keepdims=True) + EPS)
        xh = x2 * r
        gdh = dh * g_ref[...]
        dx2 = dx3_ref[...] + r * (gdh - xh * jnp.mean(gdh * xh, axis=-1, keepdims=True))
        dx2_ref[...] = dx2
        dx2b_ref[...] = dx2.astype(bf16)

        @pl.when(i == 0)
        def _():
            dg_ref[...] = jnp.zeros_like(dg_ref)

        dg_ref[...] += jnp.sum(dh * xh, axis=0, keepdims=True)

    row = pl.BlockSpec((tm, D), lambda i: (i, 0))
    vec = pl.BlockSpec((1, D), lambda i: (0, 0))
    return _call(
        order, body, (du, wups, x2, dx3, g2), name="mlp_up_bwd", grid=(S // tm,),
        in_specs=[pl.BlockSpec((tm, DFF), lambda i: (i, 0)), pl.BlockSpec((NCHIP, D, D), lambda i: (0, 0, 0)),
                  row, row, vec],
        out_specs=[row, row, vec],
        out_shape=[jax.ShapeDtypeStruct((S, D), f32), jax.ShapeDtypeStruct((S, D), bf16),
                   jax.ShapeDtypeStruct((1, D), f32)],
        compiler_params=_params(("arbitrary",)),
    )


def _gate_bwd(order, dx2b, wout, gates, ya, yb):
    tm = 256

    def body(d_ref, w_ref, g_ref, ya_ref, yb_ref, dya_ref, dyb_ref, dg_ref):
        dm = _dot_nt(d_ref[...], w_ref[...])
        sa = jax.nn.sigmoid(g_ref[:, 0:D])
        sb = jax.nn.sigmoid(g_ref[:, D:2 * D])
        dya_ref[...] = (dm * sa).astype(bf16)
        dyb_ref[...] = (dm * sb).astype(bf16)
        dg_ref[:, 0:D] = (dm * ya_ref[...] * (sa * (1.0 - sa))).astype(bf16)
        dg_ref[:, D:2 * D] = (dm * yb_ref[...] * (sb * (1.0 - sb))).astype(bf16)

    row = lambda w: pl.BlockSpec((tm, w), lambda i: (i, 0))
    return _call(
        order, body, (dx2b, wout, gates, ya, yb), name="gate_bwd", grid=(S // tm,),
        in_specs=[row(D), pl.BlockSpec((D, D), lambda i: (0, 0)), row(2 * D), row(D), row(D)],
        out_specs=[row(D), row(D), row(2 * D)],
        out_shape=[jax.ShapeDtypeStruct((S, D), bf16), jax.ShapeDtypeStruct((S, D), bf16),
                   jax.ShapeDtypeStruct((S, 2 * D), bf16)],
        compiler_params=_params(("parallel",)),
    )


def _branch_bwd(order, dya, dyb, was, wbs):
    tm = 512

    def body(dya_ref, dyb_ref, wa_ref, wb_ref, doa_ref, dob_ref):
        doa = jnp.zeros((tm, FOXW), f32)
        dob = jnp.zeros((tm, DILOUT), f32)
        for q in range(NCHIP):
            cols = slice(q * 256, (q + 1) * 256)
            doa = doa + _dot_nt(dya_ref[:, cols], wa_ref[q])
            dob = dob + _dot_nt(dyb_ref[:, cols], wb_ref[q])
        doa_ref[...] = doa.astype(bf16)
        dob_ref[...] = dob

    row = lambda w: pl.BlockSpec((tm, w), lambda i: (i, 0))
    full3 = lambda a: pl.BlockSpec(a.shape, lambda i: (0, 0, 0))
    return _call(
        order, body, (dya, dyb, was, wbs), name="branch_bwd", grid=(S // tm,),
        in_specs=[row(D), row(D), full3(was), full3(wbs)],
        out_specs=[row(FOXW), row(DILOUT)],
        out_shape=[jax.ShapeDtypeStruct((S, FOXW), bf16), jax.ShapeDtypeStruct((S, DILOUT), f32)],
        compiler_params=_params(("parallel",)),
    )


def _branch_wgrad(order, oa, ob, dya, dyb):
    def body(oa_ref, ob_ref, dya_ref, dyb_ref, dwa_ref, dwb_ref):
        dwa_ref[...] = _dot_tn(oa_ref[...], dya_ref[...])
        dwb_ref[...] = _dot_tn(ob_ref[...], dyb_ref[...])

    full = lambda w: pl.BlockSpec((S, w), lambda q: (0, 0))
    colq = pl.BlockSpec((S, 256), lambda q: (0, q))
    return _call(
        order, body, (oa, ob, dya, dyb), name="branch_wgrad", grid=(NCHIP,),
        in_specs=[full(FOXW), full(DILOUT), colq, colq],
        out_specs=[pl.BlockSpec((None, FOXW, 256), lambda q: (q, 0, 0)),
                   pl.BlockSpec((None, DILOUT, 256), lambda q: (q, 0, 0))],
        out_shape=[jax.ShapeDtypeStruct((NCHIP, FOXW, 256), f32), jax.ShapeDtypeStruct((NCHIP, DILOUT, 256), f32)],
        compiler_params=_params(("parallel",)),
    )


def _fox_bwd(order, qkva, doa, oa, lse, F, ftb):
    nb = S // TQ

    def body(q_ref, k_ref, v_ref, do_ref, o_ref, lse_ref, F_ref, ftb_ref, dq_ref, dk_ref, dv_ref, dft_ref, dfq_ref,
             dq_scr):
        p = pl.program_id(0)
        lane, hm = _head_masks()
        sub8 = lax.broadcasted_iota(jnp.int32, (8, 1), 0)
        rowi = lax.broadcasted_iota(jnp.int32, (TQ, 1), 0)
        coli = lax.broadcasted_iota(jnp.int32, (1, TQ), 1)
        dq_scr[...] = jnp.zeros_like(dq_scr)
        dfq_ref[...] = jnp.zeros_like(dfq_ref)

        def kvblock(j, c):
            c0 = pl.multiple_of(j * TQ, TQ)
            k = k_ref[pl.ds(c0, TQ), :]
            v = v_ref[pl.ds(c0, TQ), :]
            kf = k.astype(f32)
            km = [jnp.where(hm[hh], kf, 0.0).astype(bf16) for hh in (0, 1)]
            frow = ftb_ref[j]
            fr = [jnp.sum(jnp.where(sub8 == 2 * p + hh, frow, 0.0), axis=0, keepdims=True) for hh in (0, 1)]

            def qblock(i, carry):
                dk, dv, df0, df1 = carry
                df = [df0, df1]
                r0 = pl.multiple_of(i * TQ, TQ)
                q = q_ref[pl.ds(r0, TQ), :].astype(f32) * 0.125
                do = do_ref[pl.ds(r0, TQ), :].astype(f32)
                prod = do * o_ref[pl.ds(r0, TQ), :].astype(f32)
                lseb = lse_ref[pl.ds(r0, TQ), :]
                Fb = F_ref[pl.ds(r0, TQ), :]
                causal = (c0 + coli) <= (r0 + rowi)
                dqacc = jnp.zeros((TQ, 128), f32)
                rowsum = jnp.zeros((TQ, 128), f32)
                for hh in (0, 1):
                    qh = jnp.where(hm[hh], q, 0.0).astype(bf16)
                    doh = jnp.where(hm[hh], do, 0.0).astype(bf16)
                    delta = jnp.sum(jnp.where(hm[hh], prod, 0.0), axis=1, keepdims=True)
                    fc = jnp.sum(jnp.where(lane == 2 * p + hh, Fb, 0.0), axis=1, keepdims=True)
                    s = _dot_nt(qh, k) + (fc - fr[hh])
                    pr = jnp.where(causal, jnp.exp(s - lseb[:, hh * HD:hh * HD + 1]), 0.0)
                    dp = _dot_nt(doh, v)
                    ds = pr * (dp - delta)
                    dsb = ds.astype(bf16)
                    dv = dv + _dot_tn(pr.astype(bf16), doh)
                    dk = dk + _dot_tn(dsb, qh)
                    dqacc = dqacc + _dot(dsb, km[hh])
                    df[hh] = df[hh] - jnp.sum(ds, axis=0, keepdims=True)
                    rowsum = jnp.where(hm[hh], jnp.sum(ds, axis=1, keepdims=True), rowsum)
                dq_scr[pl.ds(r0, TQ), :] += dqacc * 0.125
                dfq_ref[pl.ds(r0, TQ), :] += rowsum
                return dk, dv, df[0], df[1]

            z = jnp.zeros((TQ, 128), f32)
            zr = jnp.zeros((1, TQ), f32)
            dk, dv, df0, df1 = lax.fori_loop(j, nb, qblock, (z, z, zr, zr))
            dk_ref[pl.ds(c0, TQ), :] = dk.astype(bf16)
            dv_ref[pl.ds(c0, TQ), :] = dv.astype(bf16)
            dft_ref[j] = jnp.where(sub8 == 0, df0, jnp.where(sub8 == 1, df1, 0.0))
            return c

        lax.fori_loop(0, nb, kvblock, 0)
        dq_ref[...] = dq_scr[...].astype(bf16)

    blk = lambda off: pl.BlockSpec((S, 128), lambda p: (0, off + p))
    return _call(
        order, body, (qkva, qkva, qkva, doa, oa, lse, F, ftb), name="fox_bwd", grid=(4,),
        in_specs=[blk(0), blk(4), blk(8), blk(0), blk(0), blk(0), pl.BlockSpec((S, 128), lambda p: (0, 0)),
                  pl.BlockSpec((nb, 8, TQ), lambda p: (0, 0, 0))],
        out_specs=[blk(0), blk(0), blk(0), pl.BlockSpec((None, nb, 8, TQ), lambda p: (p, 0, 0, 0)), blk(0)],
        out_shape=[jax.ShapeDtypeStruct((S, FOXW), bf16)] * 3 + [jax.ShapeDtypeStruct((4, nb, 8, TQ), f32),
                                                                 jax.ShapeDtypeStruct((S, FOXW), f32)],
        scratch_shapes=[pltpu.VMEM((S, 128), f32)],
        compiler_params=_params(("parallel",)),
    )


def _forget_bwd(order, dft, dfq, fa, bpad):
    nb = S // TQ

    def body(dft_ref, dfq_ref, fa_ref, b_ref, dfa_ref, db_ref, rows):
        rr = lax.broadcasted_iota(jnp.int32, (TQ, TQ), 0)
        cc = lax.broadcasted_iota(jnp.int32, (TQ, TQ), 1)
        upper = (cc >= rr).astype(bf16)
        ones = jnp.ones((8, TQ), bf16)
        lane = lax.broadcasted_iota(jnp.int32, (1, 128), 1)
        carry = jnp.zeros((1, 128), f32)
        db = jnp.zeros((1, 128), f32)
        rows[...] = jnp.zeros_like(rows)
        for b in reversed(range(nb)):
            for p in range(4):
                rows[2 * p:2 * p + 2, :] = dft_ref[p, b, 0:2, :]
            cols = jnp.zeros((TQ, 128), f32)
            for h in range(8):
                c0 = (h // 2) * 128 + (h % 2) * HD
                cols = jnp.where(lane == h, dfq_ref[b * TQ:(b + 1) * TQ, c0:c0 + 1], cols)
            dlf = carry
            tot = jnp.zeros((8, 128), f32)
            for part in _split3(rows[...]):
                dlf = dlf + _dot_nt(upper, part)
                tot = tot + _dot_nt(ones, part)
            for part in _split3(cols):
                dlf = dlf + _dot(upper, part)
            carry = carry + tot[0:1, :] + jnp.sum(cols, axis=0, keepdims=True)
            z = fa_ref[b * TQ:(b + 1) * TQ, :] + b_ref[...]
            dz = jnp.where(lane < 8, dlf * jax.nn.sigmoid(-z), 0.0)
            dfa_ref[b * TQ:(b + 1) * TQ, :] = dz.astype(bf16)
            db = db + jnp.sum(dz, axis=0, keepdims=True)
        db_ref[...] = db

    return _call(
        order, body, (dft, dfq, fa, bpad), name="forget_bwd",
        out_shape=[jax.ShapeDtypeStruct((S, 128), bf16), jax.ShapeDtypeStruct((1, 128), f32)],
        scratch_shapes=[pltpu.VMEM((128, TQ), f32)],
        compiler_params=_params(),
    )


def _dil_bwd(order, qkvb, dob, ob, lseb, rope):
    c_t, s1_t, s2_t = rope

    def body(*refs):
        q_refs, k_refs, v_refs = refs[0:3], refs[3:6], refs[6:9]
        dob_ref, ob_ref, lse_ref, c_ref, s1_ref, s2_ref = refs[9:15]
        dq_out, dk_out, dv_out = refs[15:18], refs[18:21], refs[21:24]
        qp, kp, vp, dop, lp, dlp, dln, dqp, dkp, dvp, nat = refs[24:35]
        _, hm = _head_masks()

        def delta_rows(i, c):
            r0 = pl.multiple_of(i * TQ, TQ)
            prod = dob_ref[pl.ds(r0, TQ), :] * ob_ref[pl.ds(r0, TQ), :].astype(f32)
            d0 = jnp.sum(jnp.where(hm[0], prod, 0.0), axis=1, keepdims=True)
            d1 = jnp.sum(jnp.where(hm[1], prod, 0.0), axis=1, keepdims=True)
            dln[pl.ds(r0, TQ), :] = jnp.where(hm[0], d0, d1)
            return c

        lax.fori_loop(0, S // TQ, delta_rows, 0)

        for g, r in enumerate(DIL):
            nbl = S // r // BAND
            if r == 1:
                srcs = (q_refs[g], k_refs[g], v_refs[g], dob_ref, lse_ref, dln)
            else:
                for dst, src in ((qp, q_refs[g]), (kp, k_refs[g]), (vp, v_refs[g]), (dop, dob_ref),
                                 (lp, lse_ref), (dlp, dln)):
                    _permute_in(dst, src, r)
                srcs = (qp, kp, vp, dop, lp, dlp)
            dkp[...] = jnp.zeros_like(dkp)
            dvp[...] = jnp.zeros_like(dvp)

            def blk(bb, c, srcs=srcs, nbl=nbl):
                qs_, ks_, vs_, dos_, ls_, dls_ = srcs
                r0, k0, valid = _band_geometry(bb, nbl)
                q = qs_[pl.ds(r0, BAND), :] * 0.125
                kwf = ks_[pl.ds(k0, 2 * BAND), :]
                kw = kwf.astype(bf16)
                vw = vs_[pl.ds(k0, 2 * BAND), :].astype(bf16)
                do = dos_[pl.ds(r0, BAND), :]
                lse = ls_[pl.ds(r0, BAND), :]
                dlt = dls_[pl.ds(r0, BAND), :]
                dq = jnp.zeros((BAND, 128), f32)
                dk = jnp.zeros((2 * BAND, 128), f32)
                dv = jnp.zeros((2 * BAND, 128), f32)
                for hh in (0, 1):
                    qh = jnp.where(hm[hh], q, 0.0).astype(bf16)
                    doh = jnp.where(hm[hh], do, 0.0).astype(bf16)
                    kh = jnp.where(hm[hh], kwf, 0.0).astype(bf16)
                    s = _dot_nt(qh, kw)
                    pr = jnp.where(valid, jnp.exp(s - lse[:, hh * HD:hh * HD + 1]), 0.0)
                    dp = _dot_nt(doh, vw)
                    ds = pr * (dp - dlt[:, hh * HD:hh * HD + 1])
                    dsb = ds.astype(bf16)
                    dv = dv + _dot_tn(pr.astype(bf16), doh)
                    dk = dk + _dot_tn(dsb, qh)
                    dq = dq + _dot(dsb, kh)
                dqp[pl.ds(r0, BAND), :] = dq * 0.125
                dkp[pl.ds(k0, 2 * BAND), :] += dk
                dvp[pl.ds(k0, 2 * BAND), :] += dv
                return c

            lax.fori_loop(0, S // BAND, blk, 0)

            for acc, out, roped in ((dqp, dq_out[g], True), (dkp, dk_out[g], True), (dvp, dv_out[g], False)):
                if r == 1:
                    src = acc
                else:
                    _permute_out(nat, acc, r)
                    src = nat

                def emit(i, c, src=src, out=out, roped=roped):
                    r0 = pl.multiple_of(i * TQ, TQ)
                    d = src[pl.ds(r0, TQ), :]
                    if roped:
                        d = (d * c_ref[pl.ds(r0, TQ), :] + pltpu.roll(d * s1_ref[pl.ds(r0, TQ), :], 8, 1)
                             + pltpu.roll(d * s2_ref[pl.ds(r0, TQ), :], 120, 1))
                    out[pl.ds(r0, TQ), :] = d.astype(bf16)
                    return c

                lax.fori_loop(0, S // TQ, emit, 0)

    pair = pl.BlockSpec((S, 128), lambda p: (0, p))
    tab = pl.BlockSpec((S, 128), lambda p: (0, 0))
    return _call(
        order, body, [qkvb] * 9 + [dob, ob, lseb, c_t, s1_t, s2_t], name="dil_bwd", grid=(2,),
        in_specs=_dil_in_specs() + [pair, pair, pair, tab, tab, tab],
        out_specs=[pair] * 9,
        out_shape=[jax.ShapeDtypeStruct((S, DILOUT), bf16)] * 9,
        scratch_shapes=[pltpu.VMEM((S, 128), f32)] * 11,
        compiler_params=_params(("parallel",)),
    )


def _inproj_bwd(order, dproj, wp, x, dx2, g1):
    tm = 256

    def body(d_ref, w_ref, x_ref, dx2_ref, g_ref, dx_ref, dg_ref):
        i = pl.program_id(0)
        dh = _dot_nt(d_ref[...], w_ref[...])
        xb = x_ref[...]
        r = lax.rsqrt(jnp.mean(xb * xb, axis=-1, keepdims=True) + EPS)
        xh = xb * r
        gdh = dh * g_ref[...]
        dx_ref[...] = dx2_ref[...] + r * (gdh - xh * jnp.mean(gdh * xh, axis=-1, keepdims=True))

        @pl.when(i == 0)
        def _():
            dg_ref[...] = jnp.zeros_like(dg_ref)

        dg_ref[...] += jnp.sum(dh * xh, axis=0, keepdims=True)

    row = pl.BlockSpec((tm, D), lambda i: (i, 0))
    vec = pl.BlockSpec((1, D), lambda i: (0, 0))
    return _call(
        order, body, (dproj, wp, x, dx2, g1), name="inproj_bwd", grid=(S // tm,),
        in_specs=[pl.BlockSpec((tm, WP), lambda i: (i, 0)), pl.BlockSpec((D, WP), lambda i: (0, 0)), row, row, vec],
        out_specs=[row, vec],
        out_shape=[jax.ShapeDtypeStruct((S, D), f32), jax.ShapeDtypeStruct((1, D), f32)],
        compiler_params=_params(("arbitrary",)),
    )


def _pad_win(w_full):
    return jnp.concatenate([w_full[:, 0:1536], w_full[:, 1544:D_IN], w_full[:, 1536:1544],
                            jnp.zeros((D, WP - D_IN), w_full.dtype)], axis=1)


def _unpad_win(dwp):
    return jnp.concatenate([dwp[:, 0:1536], dwp[:, C_FA:C_FA + 8], dwp[:, 1536:C_FA]], axis=1)


HBM = pl.BlockSpec(memory_space=pltpu.HBM)
SEM = pl.BlockSpec(memory_space=pltpu.SEMAPHORE)
ANY = pl.BlockSpec(memory_space=pl.ANY)
SMALL_ROWS = 8


def _comm_call(name, body, bufs, order, sems_in=(), new_sems=()):
    nb, ns, nn = len(bufs), len(sems_in), len(new_sems)
    extra = [] if order.tok is None else [order.tok]

    def kern(*refs):
        off = nb + ns + len(extra)
        body(refs[:nb], refs[nb:nb + ns], refs[off:off + nn])
        refs[-1][...] = jnp.zeros((8, 128), f32)

    res = pl.pallas_call(
        kern, name=name,
        in_specs=[HBM] * nb + [SEM] * ns + [ANY] * len(extra),
        out_specs=[SEM] * nn + [HBM] * nb + [pl.BlockSpec(memory_space=pltpu.VMEM)],
        out_shape=[pltpu.SemaphoreType.DMA((k,)) for k in new_sems] + [pltpu.HBM(b.shape, b.dtype) for b in bufs]
        + [jax.ShapeDtypeStruct((8, 128), f32)],
        input_output_aliases={i: nn + i for i in range(nb)},
        compiler_params=pltpu.CompilerParams(has_side_effects=pltpu.SideEffectType.DATAFLOW_SIDE_EFFECTING),
    )(*[pltpu.with_memory_space_constraint(b, pltpu.HBM) for b in bufs], *sems_in, *extra)
    order.mark(res[-1])
    return list(res[:nn]), list(res[nn:nn + nb])


def _place():
    x, y, c = lax.axis_index("x"), lax.axis_index("y"), lax.axis_index("c")
    chips = [(1 - x, y), (x, 1 - y), (1 - x, 1 - y)]
    return x, y, c, chips


def _rcopy(src, dst, ssem, rsem, dev):
    return pltpu.make_async_remote_copy(src_ref=src, dst_ref=dst, send_sem=ssem, recv_sem=rsem,
                                        device_id=dev, device_id_type=pl.DeviceIdType.MESH)


def _half(nrows, which):
    return pl.ds(which * (nrows // 2), nrows // 2)


def _ici_copies(stack, group_sizes, ssems, rsems):
    x, y, c, chips = _place()
    me_q = 2 * x + y
    sends, recvs = [], []
    a = 0
    for grp, size in enumerate(group_sizes):
        for k in range(size):
            rows = _half(stack[a].shape[1], c)
            for j, (cx, cy) in enumerate(chips):
                mine = stack[a].at[me_q, rows]
                sends.append(_rcopy(mine, mine, ssems[grp].at[k * 3 + j], rsems[grp].at[k * 3 + j], (cx, cy, c)))
                theirs = stack[a].at[2 * cx + cy, rows]
                recvs.append(_rcopy(theirs, theirs, ssems[grp].at[k * 3 + j], rsems[grp].at[k * 3 + j],
                                    (cx, cy, c)))
            a += 1
    return sends, recvs


def _allgather_start(stacks, group_sizes, order):
    def body(bufs, _, new):
        sends, _r = _ici_copies(bufs, group_sizes, new[0::2], new[1::2])
        for cp in sends:
            cp.start()

    sizes = []
    for size in group_sizes:
        sizes += [3 * size, 3 * size]
    sems, stacks = _comm_call("allgather_start", body, stacks, order, new_sems=sizes)
    return [(sems[2 * g], sems[2 * g + 1]) for g in range(len(group_sizes))], stacks


def _forward_copies(stack, ssem, rsem):
    x, y, c, chips = _place()
    sib = (x, y, 1 - c)
    sends, recvs = [], []
    for a in range(len(stack)):
        for j, (cx, cy) in enumerate(chips):
            landed = stack[a].at[2 * cx + cy, _half(stack[a].shape[1], c)]
            sends.append(_rcopy(landed, landed, ssem.at[a * 3 + j], rsem.at[a * 3 + j], sib))
            other = stack[a].at[2 * cx + cy, _half(stack[a].shape[1], 1 - c)]
            recvs.append(_rcopy(other, other, ssem.at[a * 3 + j], rsem.at[a * 3 + j], sib))
    return sends, recvs


def _allgather_forward(name, stacks, sems, order):
    n = len(stacks)

    def body(bufs, taken, new):
        sends, recvs = _ici_copies(bufs, [n], [taken[0]], [taken[1]])
        for cp in sends:
            cp.wait_send()
        for cp in recvs:
            cp.wait_recv()
        fwd, _r = _forward_copies(bufs, new[0], new[1])
        for cp in fwd:
            cp.start()

    return _comm_call(name, body, stacks, order, sems_in=sems, new_sems=(3 * n, 3 * n))


def _allgather_finish(name, stacks, sems, order):
    def body(bufs, taken, _):
        sends, recvs = _forward_copies(bufs, taken[0], taken[1])
        for cp in sends:
            cp.wait_send()
        for cp in recvs:
            cp.wait_recv()

    return _comm_call(name, body, stacks, order, sems_in=sems)[1]


def _pair_copies(g, t, ssem, rsem):
    x, y, c, _ = _place()
    return [_rcopy(g[a].at[:, _half(g[a].shape[1], 1 - c), :], t[a], ssem.at[a], rsem.at[a], (x, y, 1 - c))
            for a in range(len(g))]


def _pair_start(name, gs, order):
    n = len(gs)
    ts = [lax.empty((NCHIP, g.shape[1] // 2, g.shape[2]), f32) for g in gs]

    def body(bufs, _, new):
        for cp in _pair_copies(bufs[:n], bufs[n:], new[0], new[1]):
            cp.start()

    sems, bufs = _comm_call(name, body, list(gs) + ts, order, new_sems=(n, n))
    return sems, bufs


def _pair_wait(name, bufs, sems, order):
    n = len(bufs) // 2

    def body(refs, taken, _):
        for cp in _pair_copies(refs[:n], refs[n:], taken[0], taken[1]):
            cp.wait_send()
            cp.wait_recv()

    bufs = _comm_call(name, body, bufs, order, sems_in=sems)[1]
    return bufs[:n], bufs[n:]


def _row_tile(h):
    return min(h, 256)


def _pair_add(order, g, t, c_arr, name):
    _, R, C = g.shape
    h = R // 2
    tr = _row_tile(h)
    nblk = h // tr

    def body(c_ref, g_ref, t_ref, p32_ref, p16_ref):
        s = g_ref[...] + t_ref[...]
        p32_ref[...] = s
        p16_ref[...] = s.astype(bf16)

    blk = pl.BlockSpec((None, tr, C), lambda q, i, c_ref: (q, i, 0))
    return _call_indexed(
        order, body, (c_arr,), (g, t), (NCHIP, nblk),
        [pl.BlockSpec((None, tr, C), lambda q, i, c_ref: (q, c_ref[0] * nblk + i, 0)), blk], [blk, blk],
        name=name,
        out_shape=[jax.ShapeDtypeStruct((NCHIP, h, C), f32), jax.ShapeDtypeStruct((NCHIP, h, C), bf16)],
        compiler_params=_params(("parallel", "parallel")),
    )


def _shard_copies(p, r, sm, ssem, rsem):
    x, y, c, chips = _place()
    n = len(p)
    sends, recvs = [], []
    for a in range(n):
        for j, (cx, cy) in enumerate(chips):
            k = a * 3 + j
            sends.append(_rcopy(p[a].at[2 * cx + cy], r[a].at[j], ssem.at[k], rsem.at[k], (cx, cy, c)))
            recvs.append(_rcopy(r[a].at[j], r[a].at[j], ssem.at[k], rsem.at[k], (cx, cy, c)))
    if sm is not None:
        mine = sm.at[4 * x + 2 * y + c]
        for i in range(1, 8):
            px = (1 - x) if i & 4 else x
            py = (1 - y) if i & 2 else y
            pc = (1 - c) if i & 1 else c
            k = 3 * n + i - 1
            sends.append(_rcopy(mine, mine, ssem.at[k], rsem.at[k], (px, py, pc)))
            slot = sm.at[4 * px + 2 * py + pc]
            recvs.append(_rcopy(slot, slot, ssem.at[k], rsem.at[k], (px, py, pc)))
    return sends, recvs


def _shard_start(name, p16s, order, sm=None):
    n = len(p16s)
    rs = [lax.empty((3,) + p.shape[1:], bf16) for p in p16s]
    extra = [] if sm is None else [sm]
    nsem = 3 * n + (7 if sm is not None else 0)

    def body(bufs, _, new):
        sends, _r = _shard_copies(bufs[:n], bufs[n:2 * n], bufs[2 * n] if extra else None, new[0], new[1])
        for cp in sends:
            cp.start()

    return _comm_call(name, body, list(p16s) + rs + extra, order, new_sems=(nsem, nsem))


def _shard_wait(name, bufs, sems, n, order):
    has_sm = len(bufs) > 2 * n

    def body(refs, taken, _):
        sends, recvs = _shard_copies(refs[:n], refs[n:2 * n], refs[2 * n] if has_sm else None, taken[0], taken[1])
        for cp in sends:
            cp.wait_send()
        for cp in recvs:
            cp.wait_recv()

    bufs = _comm_call(name, body, bufs, order, sems_in=sems)[1]
    return bufs[n:2 * n], (bufs[2 * n] if has_sm else None)


def _shard_sum(order, p32, r, q_arr, c_arr, name):
    _, h, C = p32.shape
    tr = _row_tile(h)
    nblk = h // tr

    def body(q_ref, c_ref, p_ref, r_ref, o_ref):
        s = p_ref[...]
        for j in range(3):
            s = s + r_ref[j].astype(f32)
        o_ref[...] = s

    return _call_indexed(
        order, body, (q_arr, c_arr), (p32, r), (nblk,),
        [pl.BlockSpec((None, tr, C), lambda i, q_ref, c_ref: (q_ref[0], i, 0)),
         pl.BlockSpec((3, tr, C), lambda i, q_ref, c_ref: (0, i, 0))],
        pl.BlockSpec((tr, C), lambda i, q_ref, c_ref: (c_ref[0] * nblk + i, 0)),
        name=name, out_shape=jax.ShapeDtypeStruct((2 * h, C), f32),
        compiler_params=_params(("parallel",)),
    )


def _swap_copies(full, ssem, rsem):
    x, y, c, _ = _place()
    sends, recvs = [], []
    for a in range(len(full)):
        mine = full[a].at[_half(full[a].shape[0], c)]
        sends.append(_rcopy(mine, mine, ssem.at[a], rsem.at[a], (x, y, 1 - c)))
        other = full[a].at[_half(full[a].shape[0], 1 - c)]
        recvs.append(_rcopy(other, other, ssem.at[a], rsem.at[a], (x, y, 1 - c)))
    return sends, recvs


def _swap_start(name, fulls, order):
    n = len(fulls)

    def body(bufs, _, new):
        for cp in _swap_copies(bufs, new[0], new[1])[0]:
            cp.start()

    return _comm_call(name, body, list(fulls), order, new_sems=(n, n))


def _swap_wait(name, fulls, sems, order):
    def body(refs, taken, _):
        sends, recvs = _swap_copies(refs, taken[0], taken[1])
        for cp in sends:
            cp.wait_send()
        for cp in recvs:
            cp.wait_recv()

    return _comm_call(name, body, fulls, order, sems_in=sems)[1]


def _small_sum(order, sm):
    def body(sm_ref, o_ref):
        s = sm_ref[0]
        for d in range(1, 8):
            s = s + sm_ref[d]
        o_ref[...] = s

    return _call(order, body, (sm,), name="small_grad_sum", out_shape=jax.ShapeDtypeStruct((SMALL_ROWS, D), f32))


def _adamw(order, w, g, m, v, name):
    R, C = w.shape
    tr = min(R, 256)

    def body(w_ref, g_ref, m_ref, v_ref, d_ref, nm_ref, nv_ref):
        g_ = g_ref[...]
        m_ = ADAM_B1 * m_ref[...] + (1.0 - ADAM_B1) * g_
        v_ = ADAM_B2 * v_ref[...] + (1.0 - ADAM_B2) * (g_ * g_)
        m_hat = m_ / (1.0 - ADAM_B1 ** ADAM_STEP)
        v_hat = v_ / (1.0 - ADAM_B2 ** ADAM_STEP)
        d_ref[...] = -ADAM_LR * (m_hat / (jnp.sqrt(v_hat) + ADAM_EPS) + ADAM_WD * w_ref[...])
        nm_ref[...] = m_
        nv_ref[...] = v_

    blk = pl.BlockSpec((tr, C), lambda i: (i, 0))
    return _call(
        order, body, (w, g, m, v), name=name, grid=(R // tr,), in_specs=[blk] * 4, out_specs=[blk] * 3,
        out_shape=[jax.ShapeDtypeStruct((R, C), f32)] * 3,
        compiler_params=_params(("parallel",)),
    )


def _stack_cols(w, n):
    K_, N = w.shape
    return jnp.stack([w[:, q * (N // n):(q + 1) * (N // n)] for q in range(n)], axis=0)


def kernel(x, norm_attn_g, w_in, b_forget, w_branch_a, w_branch_b, w_out, norm_mlp_g, w_up, w_down, norm_final_g, loss_target, m_norm_attn_g, m_w_in, m_b_forget, m_w_branch_a, m_w_branch_b, m_w_out, m_norm_mlp_g, m_w_up, m_w_down, m_norm_final_g, v_norm_attn_g, v_w_in, v_b_forget, v_w_branch_a, v_w_branch_b, v_w_out, v_norm_mlp_g, v_w_up, v_w_down, v_norm_final_g):
    xi, yi, ci = lax.axis_index("x"), lax.axis_index("y"), lax.axis_index("c")
    c_arr = jnp.reshape(ci, (1,)).astype(jnp.int32)
    q_arr = jnp.reshape(2 * xi + yi, (1,)).astype(jnp.int32)
    x_, tgt = x[0], loss_target[0]

    names = ["w_in", "w_branch_a", "w_branch_b", "w_out", "w_up", "w_down"]
    big = dict(zip(names, [w_in[0], w_branch_a[0], w_branch_b[0], w_out[0], w_up[0], w_down[0]]))
    ms = dict(zip(names, [m_w_in[0], m_w_branch_a[0], m_w_branch_b[0], m_w_out[0], m_w_up[0], m_w_down[0]]))
    vs = dict(zip(names, [v_w_in[0], v_w_branch_a[0], v_w_branch_b[0], v_w_out[0], v_w_up[0], v_w_down[0]]))
    grad, upd = {}, {}

    stacks = [lax.dynamic_update_slice(lax.empty((NCHIP,) + w.shape, bf16), w.astype(bf16)[None], (q_arr[0], 0, 0))
              for w in big.values()]
    order = _Order()

    def run(fn, *args, **kw):
        return fn(order, *args, **kw)

    (sem_in, sem_rest), stacks = _allgather_start(stacks, [1, 5], order)
    sem_f, win_s = _allgather_forward("allgather_forward_in", stacks[0:1], sem_in, order)
    (win_s,) = _allgather_finish("allgather_finish_in", win_s, sem_f, order)
    wp = _pad_win(jnp.concatenate([win_s[q] for q in range(NCHIP)], axis=1))

    rope = _rope_tables()
    bpad = jnp.pad(b_forget, ((0, 0), (0, 120)))
    h1, qkva, qkvb, gates, fa = run(_norm_inproj, x_, norm_attn_g, wp, rope)
    F, ftb = run(_forget_cumsum, fa, bpad)
    oa, lsea = run(_fox_fwd, qkva, F, ftb)
    sem_f, rest = _allgather_forward("allgather_forward_rest", stacks[1:], sem_rest, order)
    ob, lseb = run(_dil_fwd, qkvb)
    was, wbs, wouts, wups, wdowns = _allgather_finish("allgather_finish_rest", rest, sem_f, order)
    wout = wouts.reshape(D, D)
    wdown = wdowns.reshape(DFF, D)
    ya, yb, mixed = run(_branch_mix, oa, ob, was, wbs, gates)
    x2, h2 = run(_outproj_norm, mixed, wout, x_, norm_mlp_g)
    u, a = run(_mlp_up, h2, wups)
    dx3, dx3b, dg3, loss_part = run(_mlp_down_loss, a, wdown, x2, norm_final_g.reshape(1, D), tgt)
    loss = lax.psum(loss_part[0, 0], ("x", "y", "c"))

    def reduce_to_pairs(tag, group, bufs, sems):
        gs, ts = _pair_wait("pair_wait_" + tag, bufs, sems, order)
        return zip(*[run(_pair_add, gs[i], ts[i], c_arr, "pair_add_" + nm) for i, nm in enumerate(group)])

    def reduce_to_shard(tag, group, p32s, bufs, sems):
        rs, sm_all = _shard_wait("shard_wait_" + tag, bufs, sems, len(group), order)
        fulls = [run(_shard_sum, p32s[i], rs[i], q_arr, c_arr, "shard_sum_" + nm) for i, nm in enumerate(group)]
        return _swap_start("swap_start_" + tag, fulls, order), sm_all

    def finish(tag, group, fulls, sems):
        fulls = _swap_wait("swap_wait_" + tag, fulls, sems, order)
        for nm, gfull in zip(group, fulls):
            grad[nm] = gfull
            upd[nm] = run(_adamw, big[nm], gfull, ms[nm], vs[nm], "adamw_" + nm)

    grp_a, grp_b, grp_c = ["w_down", "w_up"], ["w_out", "w_branch_a", "w_branch_b"], ["w_in"]
    du = run(_mlp_down_bwd, dx3b, wdown, u)
    dwdown = run(_mm, a, dx3b, "tn", f32, 1024, D, "wgrad_down")
    dwup = run(_mm, h2, du, "tn", f32, D, 1024, "wgrad_up", stack_cols=True)
    sem_pa, buf_pa = _pair_start("pair_start_a", [dwdown.reshape(NCHIP, DFF // NCHIP, D), dwup], order)
    dx2, dx2b, dg2 = run(_mlp_up_bwd, du, wups, x2, dx3, norm_mlp_g)
    p32_a, p16_a = reduce_to_pairs("a", grp_a, buf_pa, sem_pa)
    sem_sa, buf_sa = _shard_start("shard_start_a", p16_a, order)
    dya, dyb, dgates = run(_gate_bwd, dx2b, wout, gates, ya, yb)
    dwout = run(_mm, mixed, dx2b, "tn", f32, D, D, "wgrad_out")
    doa, dob = run(_branch_bwd, dya, dyb, was, wbs)
    dwas, dwbs = run(_branch_wgrad, oa, ob, dya, dyb)
    sem_pb, buf_pb = _pair_start("pair_start_b", [dwout.reshape(NCHIP, D // NCHIP, D), dwas, dwbs], order)
    dqa, dka, dva, dft, dfq = run(_fox_bwd, qkva, doa, oa, lsea, F, ftb)
    p32_b, p16_b = reduce_to_pairs("b", grp_b, buf_pb, sem_pb)
    (sem_wa, fulls_a), _ = reduce_to_shard("a", grp_a, p32_a, buf_sa, sem_sa)
    sem_sb, buf_sb = _shard_start("shard_start_b", p16_b, order)
    dfa, dbf = run(_forget_bwd, dft, dfq, fa, bpad)
    dd = run(_dil_bwd, qkvb, dob, ob, lseb, rope)
    (sem_wb, fulls_b), _ = reduce_to_shard("b", grp_b, p32_b, buf_sb, sem_sb)
    finish("a", grp_a, fulls_a, sem_wa)
    dproj = jnp.concatenate([dqa, dka, dva, *dd, dgates, dfa], axis=1)
    dwp = run(_mm, h1, dproj, "tn", f32, D, 128, "wgrad_in")
    sem_pc, buf_pc = _pair_start("pair_start_c", [_stack_cols(_unpad_win(dwp), NCHIP)], order)
    gx, dg1 = run(_inproj_bwd, dproj, wp, x_, dx2, norm_attn_g)
    p32_c, p16_c = reduce_to_pairs("c", grp_c, buf_pc, sem_pc)
    small = jnp.concatenate([dg1, dg2, dg3, jnp.pad(dbf[:, 0:8], ((0, 0), (0, D - 8))),
                             jnp.zeros((SMALL_ROWS - 4, D), f32)], axis=0)
    sm = lax.dynamic_update_slice(lax.empty((8, SMALL_ROWS, D), f32), small[None],
                                  (4 * xi + 2 * yi + ci, 0, 0))
    sem_sc, buf_sc = _shard_start("shard_start_c", p16_c, order, sm)
    finish("b", grp_b, fulls_b, sem_wb)
    (sem_wc, fulls_c), sm = reduce_to_shard("c", grp_c, p32_c, buf_sc, sem_sc)
    gsmall = run(_small_sum, sm)

    grad["norm_attn_g"], grad["norm_mlp_g"] = gsmall[0:1], gsmall[1:2]
    grad["norm_final_g"], grad["b_forget"] = gsmall[2:3], gsmall[3:4, 0:8]
    upd["norm_attn_g"] = run(_adamw, norm_attn_g, grad["norm_attn_g"], m_norm_attn_g, v_norm_attn_g, "adamw_g1")
    upd["norm_mlp_g"] = run(_adamw, norm_mlp_g, grad["norm_mlp_g"], m_norm_mlp_g, v_norm_mlp_g, "adamw_g2")
    upd["norm_final_g"] = run(_adamw, norm_final_g.reshape(1, D), grad["norm_final_g"],
                              m_norm_final_g.reshape(1, D), v_norm_final_g.reshape(1, D), "adamw_g3")
    upd["b_forget"] = run(_adamw, b_forget, grad["b_forget"], m_b_forget, v_b_forget, "adamw_bf")
    finish("c", grp_c, fulls_c, sem_wc)

    order = ["norm_attn_g", "w_in", "b_forget", "w_branch_a", "w_branch_b", "w_out", "norm_mlp_g", "w_up", "w_down",
             "norm_final_g"]
    shapes = dict(norm_attn_g=norm_attn_g.shape, w_in=w_in.shape, b_forget=b_forget.shape,
                  w_branch_a=w_branch_a.shape, w_branch_b=w_branch_b.shape, w_out=w_out.shape,
                  norm_mlp_g=norm_mlp_g.shape, w_up=w_up.shape, w_down=w_down.shape, norm_final_g=norm_final_g.shape)
    outs = [loss, gx.reshape(x.shape)]
    outs += [grad[nm].reshape(shapes[nm]) for nm in order]
    for k in range(3):
        outs += [upd[nm][k].reshape(shapes[nm]) for nm in order]
    return tuple(outs)
```

```python
import functools

import jax
import jax.numpy as jnp
import numpy as np
from jax import lax
from jax.experimental import pallas as pl
from jax.experimental.pallas import tpu as pltpu

f32 = jnp.float32
bf16 = jnp.bfloat16

S = 2048
D = 1024
DFF = 4096
HD = 64
FOXW = 512
DILW = 768
DILOUT = 256
DIL = (1, 4, 16)
BAND = 128
EPS = 1e-6
NEG = -1e30
ROPE_THETA = 500000.0
NCHIP = 4

C_QKVA, C_QB, C_KB, C_VB, C_G, C_FA, WP = 0, 1536, 2304, 3072, 3840, 5888, 6016
D_IN = 5896
SHARD_IN = 1474

ADAM_LR, ADAM_B1, ADAM_B2, ADAM_EPS, ADAM_WD, ADAM_STEP = 0.001, 0.9, 0.999, 1e-08, 0.01, 10

VMEM_LIMIT = 56 * 1024 * 1024
TQ = 256


def _params(sem=None):
    return pltpu.CompilerParams(dimension_semantics=sem, vmem_limit_bytes=VMEM_LIMIT)


class _Order:
    def __init__(self):
        self.tok = None

    def mark(self, v):
        self.tok = v


def _call(order, body, args, in_specs=None, **kw):
    args = list(args)
    n_in = len(args)
    if in_specs is None:
        in_specs = [pl.BlockSpec(memory_space=pltpu.VMEM)] * n_in
    kern = body
    if order.tok is not None:
        in_specs = list(in_specs) + [pl.BlockSpec(memory_space=pl.ANY)]
        args.append(order.tok)

        def kern(*refs):
            body(*refs[:n_in], *refs[n_in + 1:])

    out = pl.pallas_call(kern, in_specs=in_specs, **kw)(*args)
    order.mark(out[0] if isinstance(out, (tuple, list)) else out)
    return out


def _call_indexed(order, body, scalars, args, grid, in_specs, out_specs, **kw):
    args, in_specs = list(args), list(in_specs)
    n_front = len(scalars) + len(args)
    kern = body
    if order.tok is not None:
        in_specs.append(pl.BlockSpec(memory_space=pl.ANY))
        args.append(order.tok)

        def kern(*refs):
            body(*refs[:n_front], *refs[n_front + 1:])

    out = pl.pallas_call(
        kern, grid_spec=pltpu.PrefetchScalarGridSpec(num_scalar_prefetch=len(scalars), grid=grid, in_specs=in_specs,
                                                     out_specs=out_specs), **kw)(*scalars, *args)
    order.mark(out[0] if isinstance(out, (tuple, list)) else out)
    return out


def _dot(a, b):
    return jnp.dot(a, b, preferred_element_type=f32)


def _dot_nt(a, b):
    return lax.dot_general(a, b, (((1,), (1,)), ((), ())), preferred_element_type=f32)


def _dot_tn(a, b):
    return lax.dot_general(a, b, (((0,), (0,)), ((), ())), preferred_element_type=f32)


def _split3(x):
    hi = x.astype(bf16)
    r1 = x - hi.astype(f32)
    mid = r1.astype(bf16)
    lo = (r1 - mid.astype(f32)).astype(bf16)
    return hi, mid, lo


def _rope_tables():
    half = 8
    inv_freq = jnp.power(jnp.float32(ROPE_THETA), -jnp.arange(half, dtype=f32) * 2.0 / 16)
    ang = jnp.arange(S).astype(f32)[:, None] * inv_freq[None, :]
    cos, sin = jnp.cos(ang), jnp.sin(ang)
    one = jnp.ones((S, HD - 16), f32)
    zero = jnp.zeros((S, HD - 16), f32)
    z8 = jnp.zeros((S, 8), f32)
    c = jnp.concatenate([cos, cos, one], axis=1)
    s1 = jnp.concatenate([-sin, z8, zero], axis=1)
    s2 = jnp.concatenate([z8, sin, zero], axis=1)
    return tuple(jnp.concatenate([t, t], axis=1) for t in (c, s1, s2))


def _mm(order, a, b, mode, out_dtype, tm, tn, name, stack_cols=False):
    if mode == "nn":
        (M, K), (_, N) = a.shape, b.shape
        a_spec = pl.BlockSpec((tm, K), lambda i, j: (i, 0))
        b_spec = pl.BlockSpec((K, tn), lambda i, j: (0, j))
        dot = _dot
    elif mode == "nt":
        (M, K), (N, _) = a.shape, b.shape
        a_spec = pl.BlockSpec((tm, K), lambda i, j: (i, 0))
        b_spec = pl.BlockSpec((tn, K), lambda i, j: (j, 0))
        dot = _dot_nt
    else:
        (K, M), (_, N) = a.shape, b.shape
        a_spec = pl.BlockSpec((K, tm), lambda i, j: (0, i))
        b_spec = pl.BlockSpec((K, tn), lambda i, j: (0, j))
        dot = _dot_tn

    def body(a_ref, b_ref, o_ref):
        o_ref[...] = dot(a_ref[...], b_ref[...]).astype(out_dtype)

    if stack_cols:
        assert tm == M
        out_spec = pl.BlockSpec((None, tm, tn), lambda i, j: (j, 0, 0))
        out_shape = jax.ShapeDtypeStruct((N // tn, M, tn), out_dtype)
    else:
        out_spec = pl.BlockSpec((tm, tn), lambda i, j: (i, j))
        out_shape = jax.ShapeDtypeStruct((M, N), out_dtype)
    return _call(
        order, body, (a, b), name=name, grid=(M // tm, N // tn), in_specs=[a_spec, b_spec],
        out_specs=out_spec, out_shape=out_shape,
        compiler_params=_params(("parallel", "parallel")),
    )


def _norm_inproj(order, x, g1, wp, rope):
    tm = 256
    c_t, s1_t, s2_t = rope

    def body(x_ref, g_ref, w_ref, c_ref, s1_ref, s2_ref, h_ref, qkva_ref, qkvb_ref, gates_ref, fa_ref):
        xb = x_ref[...]
        r = lax.rsqrt(jnp.mean(xb * xb, axis=-1, keepdims=True) + EPS)
        h = ((xb * r) * g_ref[...]).astype(bf16)
        h_ref[...] = h
        qkva_ref[...] = _dot(h, w_ref[:, C_QKVA:C_QB]).astype(bf16)
        c, s1, s2 = c_ref[...], s1_ref[...], s2_ref[...]
        for sec, lo in enumerate((C_QB, C_KB)):
            pb = _dot(h, w_ref[:, lo:lo + DILW])
            for ch in range(DILW // 128):
                pc = pb[:, ch * 128:(ch + 1) * 128]
                roped = pc * c + pltpu.roll(pc, 120, 1) * s1 + pltpu.roll(pc, 8, 1) * s2
                qkvb_ref[:, sec * DILW + ch * 128: sec * DILW + (ch + 1) * 128] = roped
        qkvb_ref[:, 2 * DILW:3 * DILW] = _dot(h, w_ref[:, C_VB:C_G])
        gates_ref[...] = _dot(h, w_ref[:, C_G:C_FA])
        fa_ref[...] = _dot(h, w_ref[:, C_FA:WP])

    row = lambda w: pl.BlockSpec((tm, w), lambda i: (i, 0))
    return _call(
        order, body, (x, g1, wp, c_t, s1_t, s2_t), name="norm_inproj", grid=(S // tm,),
        in_specs=[row(D), pl.BlockSpec((1, D), lambda i: (0, 0)), pl.BlockSpec((D, WP), lambda i: (0, 0)),
                  row(128), row(128), row(128)],
        out_specs=[row(D), row(3 * FOXW), row(3 * DILW), row(2 * D), row(128)],
        out_shape=[jax.ShapeDtypeStruct((S, D), bf16), jax.ShapeDtypeStruct((S, 3 * FOXW), bf16),
                   jax.ShapeDtypeStruct((S, 3 * DILW), f32), jax.ShapeDtypeStruct((S, 2 * D), f32),
                   jax.ShapeDtypeStruct((S, 128), f32)],
        compiler_params=_params(("parallel",)),
    )


def _forget_cumsum(order, fa, bpad):
    nb = S // TQ

    def body(fa_ref, b_ref, F_ref, ftb_ref):
        rr = lax.broadcasted_iota(jnp.int32, (TQ, TQ), 0)
        cc = lax.broadcasted_iota(jnp.int32, (TQ, TQ), 1)
        tri = (rr >= cc).astype(bf16)
        lane = lax.broadcasted_iota(jnp.int32, (1, 128), 1)
        carry = jnp.zeros((1, 128), f32)
        for b in range(nb):
            z = fa_ref[b * TQ:(b + 1) * TQ, :] + b_ref[...]
            lf = jnp.minimum(z, 0.0) - jnp.log(1.0 + jnp.exp(-jnp.abs(z)))
            lf = jnp.where(lane < 8, lf, 0.0)
            hi, mid, lo = _split3(lf)
            fb = (_dot(tri, hi) + _dot(tri, mid)) + _dot(tri, lo) + carry
            F_ref[b * TQ:(b + 1) * TQ, :] = fb
            ftb_ref[b] = fb.T[0:8, :]
            carry = fb[TQ - 1:TQ, :]

    return _call(
        order, body, (fa, bpad), name="forget_cumsum",
        out_shape=[jax.ShapeDtypeStruct((S, 128), f32), jax.ShapeDtypeStruct((nb, 8, TQ), f32)],
        compiler_params=_params(),
    )


def _head_masks():
    lane = lax.broadcasted_iota(jnp.int32, (1, 128), 1)
    return lane, (lane < HD, lane >= HD)


def _fox_fwd(order, qkva, F, ftb):
    nb = S // TQ

    def body(q_ref, k_ref, v_ref, F_ref, ftb_ref, o_ref, lse_ref):
        p = pl.program_id(0)
        lane, hm = _head_masks()
        sub8 = lax.broadcasted_iota(jnp.int32, (8, 1), 0)
        rowi = lax.broadcasted_iota(jnp.int32, (TQ, 1), 0)
        coli = lax.broadcasted_iota(jnp.int32, (1, TQ), 1)

        def qblock(i, c):
            r0 = pl.multiple_of(i * TQ, TQ)
            q = q_ref[pl.ds(r0, TQ), :].astype(f32) * 0.125
            qs = [jnp.where(hm[hh], q, 0.0).astype(bf16) for hh in (0, 1)]
            Fb = F_ref[pl.ds(r0, TQ), :]
            fc = [jnp.sum(jnp.where(lane == 2 * p + hh, Fb, 0.0), axis=1, keepdims=True) for hh in (0, 1)]

            def kvblock(j, carry):
                c0 = pl.multiple_of(j * TQ, TQ)
                k = k_ref[pl.ds(c0, TQ), :]
                v = v_ref[pl.ds(c0, TQ), :]
                frow = ftb_ref[j]
                causal = (c0 + coli) <= (r0 + rowi)
                new = []
                for hh in (0, 1):
                    m, l, a = carry[3 * hh:3 * hh + 3]
                    fr = jnp.sum(jnp.where(sub8 == 2 * p + hh, frow, 0.0), axis=0, keepdims=True)
                    s = _dot_nt(qs[hh], k) + (fc[hh] - fr)
                    s = jnp.where(causal, s, NEG)
                    mn = jnp.maximum(m, jnp.max(s, axis=1, keepdims=True))
                    al = jnp.exp(m - mn)
                    pr = jnp.exp(s - mn)
                    l = al * l + jnp.sum(pr, axis=1, keepdims=True)
                    a = al * a + _dot(pr.astype(bf16), v)
                    new += [mn, l, a]
                return tuple(new)

            init = (jnp.full((TQ, 1), NEG, f32), jnp.zeros((TQ, 1), f32), jnp.zeros((TQ, 128), f32)) * 2
            m0, l0, a0, m1, l1, a1 = lax.fori_loop(0, i + 1, kvblock, init)
            o = jnp.where(hm[0], a0 / l0, a1 / l1)
            lse = jnp.where(hm[0], m0 + jnp.log(l0), m1 + jnp.log(l1))
            o_ref[pl.ds(r0, TQ), :] = o.astype(bf16)
            lse_ref[pl.ds(r0, TQ), :] = lse
            return c

        lax.fori_loop(0, nb, qblock, 0)

    blk = lambda off: pl.BlockSpec((S, 128), lambda p: (0, off + p))
    return _call(
        order, body, (qkva, qkva, qkva, F, ftb), name="fox_fwd", grid=(4,),
        in_specs=[blk(0), blk(4), blk(8), pl.BlockSpec((S, 128), lambda p: (0, 0)),
                  pl.BlockSpec((nb, 8, TQ), lambda p: (0, 0, 0))],
        out_specs=[blk(0), blk(0)],
        out_shape=[jax.ShapeDtypeStruct((S, FOXW), bf16), jax.ShapeDtypeStruct((S, FOXW), f32)],
        compiler_params=_params(("parallel",)),
    )


def _permute_in(dst, src, r):
    L = S // r
    for rho in range(r):
        dst[rho * L:(rho + 1) * L, :] = src[pl.ds(rho, L, stride=r), :]


def _permute_out(dst, src, r):
    L = S // r
    for rho in range(r):
        dst[pl.ds(rho, L, stride=r), :] = src[rho * L:(rho + 1) * L, :]


def _band_geometry(bb, nbl):
    r0 = pl.multiple_of(bb * BAND, BAND)
    k0 = pl.multiple_of(jnp.maximum(bb - 1, 0) * BAND, BAND)
    sub0 = (bb - lax.rem(bb, nbl)) * BAND
    qi = r0 + lax.broadcasted_iota(jnp.int32, (BAND, 1), 0)
    ki = k0 + lax.broadcasted_iota(jnp.int32, (1, 2 * BAND), 1)
    diff = qi - ki
    valid = (diff >= 0) & (diff <= BAND) & (ki >= sub0)
    return r0, k0, valid


def _dil_in_specs():
    specs = []
    for role in range(3):
        for g in range(3):
            specs.append(pl.BlockSpec((S, 128), functools.partial(lambda p, o: (0, o + p), o=role * 6 + g * 2)))
    return specs


def _dil_fwd(order, qkvb):
    def body(*refs):
        q_refs, k_refs, v_refs = refs[0:3], refs[3:6], refs[6:9]
        ob_ref, lse_ref = refs[9:11]
        qp, kp, vp, op, lp = refs[11:16]
        on = refs[16:19]
        ln = refs[19:22]
        _, hm = _head_masks()
        for g, r in enumerate(DIL):
            nbl = S // r // BAND
            if r == 1:
                qs_, ks_, vs_, od, ld = q_refs[g], k_refs[g], v_refs[g], on[g], ln[g]
            else:
                _permute_in(qp, q_refs[g], r)
                _permute_in(kp, k_refs[g], r)
                _permute_in(vp, v_refs[g], r)
                qs_, ks_, vs_, od, ld = qp, kp, vp, op, lp

            def blk(bb, c, qs_=qs_, ks_=ks_, vs_=vs_, od=od, ld=ld, nbl=nbl):
                r0, k0, valid = _band_geometry(bb, nbl)
                q = qs_[pl.ds(r0, BAND), :] * 0.125
                kw = ks_[pl.ds(k0, 2 * BAND), :].astype(bf16)
                vw = vs_[pl.ds(k0, 2 * BAND), :]
                o = jnp.zeros((BAND, 128), f32)
                lse = jnp.zeros((BAND, 128), f32)
                for hh in (0, 1):
                    qh = jnp.where(hm[hh], q, 0.0).astype(bf16)
                    s = jnp.where(valid, _dot_nt(qh, kw), NEG)
                    m = jnp.max(s, axis=1, keepdims=True)
                    pr = jnp.exp(s - m)
                    l = jnp.sum(pr, axis=1, keepdims=True)
                    vm = jnp.where(hm[hh], vw, 0.0).astype(bf16)
                    o = o + _dot((pr / l).astype(bf16), vm)
                    lse = jnp.where(hm[hh], m + jnp.log(l), lse)
                od[pl.ds(r0, BAND), :] = o
                ld[pl.ds(r0, BAND), :] = lse
                return c

            lax.fori_loop(0, S // BAND, blk, 0)
            if r != 1:
                _permute_out(on[g], op, r)
                _permute_out(ln[g], lp, r)

        def combine(i, c):
            r0 = pl.multiple_of(i * TQ, TQ)
            ls = [ln[g][pl.ds(r0, TQ), :] for g in range(3)]
            mx = jnp.maximum(jnp.maximum(ls[0], ls[1]), ls[2])
            es = [jnp.exp(l - mx) for l in ls]
            tot = (es[0] + es[1]) + es[2]
            acc = (es[0] / tot) * on[0][pl.ds(r0, TQ), :]
            acc = acc + (es[1] / tot) * on[1][pl.ds(r0, TQ), :]
            acc = acc + (es[2] / tot) * on[2][pl.ds(r0, TQ), :]
            ob_ref[pl.ds(r0, TQ), :] = acc.astype(bf16)
            lse_ref[pl.ds(r0, TQ), :] = mx + jnp.log(tot)
            return c

        lax.fori_loop(0, S // TQ, combine, 0)

    out_blk = pl.BlockSpec((S, 128), lambda p: (0, p))
    return _call(
        order, body, [qkvb] * 9, name="dil_fwd", grid=(2,),
        in_specs=_dil_in_specs(), out_specs=[out_blk, out_blk],
        out_shape=[jax.ShapeDtypeStruct((S, DILOUT), bf16), jax.ShapeDtypeStruct((S, DILOUT), f32)],
        scratch_shapes=[pltpu.VMEM((S, 128), f32)] * 11,
        compiler_params=_params(("parallel",)),
    )


def _branch_mix(order, oa, ob, was, wbs, gates):
    tm = 512

    def body(oa_ref, ob_ref, wa_ref, wb_ref, g_ref, ya_ref, yb_ref, mix_ref):
        oa_b, ob_b = oa_ref[...], ob_ref[...]
        for q in range(NCHIP):
            cols = slice(q * 256, (q + 1) * 256)
            ya = _dot(oa_b, wa_ref[q])
            yb = _dot(ob_b, wb_ref[q])
            ya_ref[:, cols] = ya
            yb_ref[:, cols] = yb
            ga = g_ref[:, q * 256:(q + 1) * 256]
            gb = g_ref[:, D + q * 256:D + (q + 1) * 256]
            mix_ref[:, cols] = (jax.nn.sigmoid(ga) * ya + jax.nn.sigmoid(gb) * yb).astype(bf16)

    row = lambda w: pl.BlockSpec((tm, w), lambda i: (i, 0))
    full3 = lambda a: pl.BlockSpec(a.shape, lambda i: (0, 0, 0))
    return _call(
        order, body, (oa, ob, was, wbs, gates), name="branch_mix", grid=(S // tm,),
        in_specs=[row(FOXW), row(DILOUT), full3(was), full3(wbs), row(2 * D)],
        out_specs=[row(D), row(D), row(D)],
        out_shape=[jax.ShapeDtypeStruct((S, D), f32), jax.ShapeDtypeStruct((S, D), f32),
                   jax.ShapeDtypeStruct((S, D), bf16)],
        compiler_params=_params(("parallel",)),
    )


def _outproj_norm(order, mixed, wout, x, g2):
    tm = 512

    def body(m_ref, w_ref, x_ref, g_ref, x2_ref, h2_ref):
        x2 = x_ref[...] + _dot(m_ref[...], w_ref[...])
        x2_ref[...] = x2
        r = lax.rsqrt(jnp.mean(x2 * x2, axis=-1, keepdims=True) + EPS)
        h2_ref[...] = ((x2 * r) * g_ref[...]).astype(bf16)

    row = pl.BlockSpec((tm, D), lambda i: (i, 0))
    return _call(
        order, body, (mixed, wout, x, g2), name="outproj_norm", grid=(S // tm,),
        in_specs=[row, pl.BlockSpec((D, D), lambda i: (0, 0)), row, pl.BlockSpec((1, D), lambda i: (0, 0))],
        out_specs=[row, row],
        out_shape=[jax.ShapeDtypeStruct((S, D), f32), jax.ShapeDtypeStruct((S, D), bf16)],
        compiler_params=_params(("parallel",)),
    )


def _mlp_up(order, h2, wups):
    tm = 512

    def body(h_ref, w_ref, u_ref, a_ref):
        u = _dot(h_ref[...], w_ref[...])
        u_ref[...] = u
        ru = jnp.maximum(u, 0.0)
        a_ref[...] = (ru * ru).astype(bf16)

    out = pl.BlockSpec((tm, D), lambda q, i: (i, q))
    return _call(
        order, body, (h2, wups), name="mlp_up", grid=(NCHIP, S // tm),
        in_specs=[pl.BlockSpec((tm, D), lambda q, i: (i, 0)), pl.BlockSpec((None, D, D), lambda q, i: (q, 0, 0))],
        out_specs=[out, out],
        out_shape=[jax.ShapeDtypeStruct((S, DFF), f32), jax.ShapeDtypeStruct((S, DFF), bf16)],
        compiler_params=_params(("parallel", "parallel")),
    )


def _mlp_down_loss(order, a, wdown, x2, g3, tgt):
    tm = 256

    def body(a_ref, w_ref, x2_ref, g_ref, t_ref, dx_ref, dxb_ref, dg_ref, loss_ref):
        i = pl.program_id(0)
        x3 = x2_ref[...] + _dot(a_ref[...], w_ref[...])
        r = lax.rsqrt(jnp.mean(x3 * x3, axis=-1, keepdims=True) + EPS)
        xh = x3 * r
        g = g_ref[...]
        e = xh * g - t_ref[...]
        part = 0.5 * jnp.sum(jnp.mean(e * e, axis=-1, keepdims=True), axis=0, keepdims=True)
        dy = e * (1.0 / D)
        gdy = dy * g
        dx = r * (gdy - xh * jnp.mean(gdy * xh, axis=-1, keepdims=True))
        dx_ref[...] = dx
        dxb_ref[...] = dx.astype(bf16)

        @pl.when(i == 0)
        def _():
            dg_ref[...] = jnp.zeros_like(dg_ref)
            loss_ref[...] = jnp.zeros_like(loss_ref)

        dg_ref[...] += jnp.sum(dy * xh, axis=0, keepdims=True)
        loss_ref[...] += jnp.broadcast_to(part, (1, 128))

    row = pl.BlockSpec((tm, D), lambda i: (i, 0))
    vec = pl.BlockSpec((1, D), lambda i: (0, 0))
    return _call(
        order, body, (a, wdown, x2, g3, tgt), name="mlp_down_loss", grid=(S // tm,),
        in_specs=[pl.BlockSpec((tm, DFF), lambda i: (i, 0)), pl.BlockSpec((DFF, D), lambda i: (0, 0)), row, vec, row],
        out_specs=[row, row, vec, pl.BlockSpec((1, 128), lambda i: (0, 0))],
        out_shape=[jax.ShapeDtypeStruct((S, D), f32), jax.ShapeDtypeStruct((S, D), bf16),
                   jax.ShapeDtypeStruct((1, D), f32), jax.ShapeDtypeStruct((1, 128), f32)],
        compiler_params=_params(("arbitrary",)),
    )


def _mlp_down_bwd(order, dx3b, wdown, u):
    tm = 256

    def body(d_ref, w_ref, u_ref, du_ref):
        d = d_ref[...]
        for q in range(NCHIP):
            cols = slice(q * D, (q + 1) * D)
            da = _dot_nt(d, w_ref[cols, :])
            du_ref[:, cols] = (da * (2.0 * jnp.maximum(u_ref[:, cols], 0.0))).astype(bf16)

    return _call(
        order, body, (dx3b, wdown, u), name="mlp_down_bwd", grid=(S // tm,),
        in_specs=[pl.BlockSpec((tm, D), lambda i: (i, 0)), pl.BlockSpec((DFF, D), lambda i: (0, 0)),
                  pl.BlockSpec((tm, DFF), lambda i: (i, 0))],
        out_specs=pl.BlockSpec((tm, DFF), lambda i: (i, 0)),
        out_shape=jax.ShapeDtypeStruct((S, DFF), bf16),
        compiler_params=_params(("parallel",)),
    )


def _mlp_up_bwd(order, du, wups, x2, dx3, g2):
    tm = 256

    def body(du_ref, w_ref, x2_ref, dx3_ref, g_ref, dx2_ref, dx2b_ref, dg_ref):
        i = pl.program_id(0)
        dh = jnp.zeros((tm, D), f32)
        for q in range(NCHIP):
            dh = dh + _dot_nt(du_ref[:, q * D:(q + 1) * D], w_ref[q])
        x2 = x2_ref[...]
        r = lax.rsqrt(jnp.mean(x2 * x2, axis=-1, keepdims=True) + EPS)
        xh = x2 * r
        gdh = dh * g_ref[...]
        dx2 = dx3_ref[...] + r * (gdh - xh * jnp.mean(gdh * xh, axis=-1, keepdims=True))
        dx2_ref[...] = dx2
        dx2b_ref[...] = dx2.astype(bf16)

        @pl.when(i == 0)
        def _():
            dg_ref[...] = jnp.zeros_like(dg_ref)

        dg_ref[...] += jnp.sum(dh * xh, axis=0, keepdims=True)

    row = pl.BlockSpec((tm, D), lambda i: (i, 0))
    vec = pl.BlockSpec((1, D), lambda i: (0, 0))
    return _call(
        order, body, (du, wups, x2, dx3, g2), name="mlp_up_bwd", grid=(S // tm,),
        in_specs=[pl.BlockSpec((tm, DFF), lambda i: (i, 0)), pl.BlockSpec((NCHIP, D, D), lambda i: (0, 0, 0)),
                  row, row, vec],
        out_specs=[row, row, vec],
        out_shape=[jax.ShapeDtypeStruct((S, D), f32), jax.ShapeDtypeStruct((S, D), bf16),
                   jax.ShapeDtypeStruct((1, D), f32)],
        compiler_params=_params(("arbitrary",)),
    )


def _gate_bwd(order, dx2b, wout, gates, ya, yb):
    tm = 256

    def body(d_ref, w_ref, g_ref, ya_ref, yb_ref, dya_ref, dyb_ref, dg_ref):
        dm = _dot_nt(d_ref[...], w_ref[...])
        sa = jax.nn.sigmoid(g_ref[:, 0:D])
        sb = jax.nn.sigmoid(g_ref[:, D:2 * D])
        dya_ref[...] = (dm * sa).astype(bf16)
        dyb_ref[...] = (dm * sb).astype(bf16)
        dg_ref[:, 0:D] = (dm * ya_ref[...] * (sa * (1.0 - sa))).astype(bf16)
        dg_ref[:, D:2 * D] = (dm * yb_ref[...] * (sb * (1.0 - sb))).astype(bf16)

    row = lambda w: pl.BlockSpec((tm, w), lambda i: (i, 0))
    return _call(
        order, body, (dx2b, wout, gates, ya, yb), name="gate_bwd", grid=(S // tm,),
        in_specs=[row(D), pl.BlockSpec((D, D), lambda i: (0, 0)), row(2 * D), row(D), row(D)],
        out_specs=[row(D), row(D), row(2 * D)],
        out_shape=[jax.ShapeDtypeStruct((S, D), bf16), jax.ShapeDtypeStruct((S, D), bf16),
                   jax.ShapeDtypeStruct((S, 2 * D), bf16)],
        compiler_params=_params(("parallel",)),
    )


def _branch_bwd(order, dya, dyb, was, wbs):
    tm = 512

    def body(dya_ref, dyb_ref, wa_ref, wb_ref, doa_ref, dob_ref):
        doa = jnp.zeros((tm, FOXW), f32)
        dob = jnp.zeros((tm, DILOUT), f32)
        for q in range(NCHIP):
            cols = slice(q * 256, (q + 1) * 256)
            doa = doa + _dot_nt(dya_ref[:, cols], wa_ref[q])
            dob = dob + _dot_nt(dyb_ref[:, cols], wb_ref[q])
        doa_ref[...] = doa.astype(bf16)
        dob_ref[...] = dob

    row = lambda w: pl.BlockSpec((tm, w), lambda i: (i, 0))
    full3 = lambda a: pl.BlockSpec(a.shape, lambda i: (0, 0, 0))
    return _call(
        order, body, (dya, dyb, was, wbs), name="branch_bwd", grid=(S // tm,),
        in_specs=[row(D), row(D), full3(was), full3(wbs)],
        out_specs=[row(FOXW), row(DILOUT)],
        out_shape=[jax.ShapeDtypeStruct((S, FOXW), bf16), jax.ShapeDtypeStruct((S, DILOUT), f32)],
        compiler_params=_params(("parallel",)),
    )


def _branch_wgrad(order, oa, ob, dya, dyb):
    def body(oa_ref, ob_ref, dya_ref, dyb_ref, dwa_ref, dwb_ref):
        dwa_ref[...] = _dot_tn(oa_ref[...], dya_ref[...])
        dwb_ref[...] = _dot_tn(ob_ref[...], dyb_ref[...])

    full = lambda w: pl.BlockSpec((S, w), lambda q: (0, 0))
    colq = pl.BlockSpec((S, 256), lambda q: (0, q))
    return _call(
        order, body, (oa, ob, dya, dyb), name="branch_wgrad", grid=(NCHIP,),
        in_specs=[full(FOXW), full(DILOUT), colq, colq],
        out_specs=[pl.BlockSpec((None, FOXW, 256), lambda q: (q, 0, 0)),
                   pl.BlockSpec((None, DILOUT, 256), lambda q: (q, 0, 0))],
        out_shape=[jax.ShapeDtypeStruct((NCHIP, FOXW, 256), f32), jax.ShapeDtypeStruct((NCHIP, DILOUT, 256), f32)],
        compiler_params=_params(("parallel",)),
    )


def _fox_bwd(order, qkva, doa, oa, lse, F, ftb):
    nb = S // TQ

    def body(q_ref, k_ref, v_ref, do_ref, o_ref, lse_ref, F_ref, ftb_ref, dq_ref, dk_ref, dv_ref, dft_ref, dfq_ref,
             dq_scr):
        p = pl.program_id(0)
        lane, hm = _head_masks()
        sub8 = lax.broadcasted_iota(jnp.int32, (8, 1), 0)
        rowi = lax.broadcasted_iota(jnp.int32, (TQ, 1), 0)
        coli = lax.broadcasted_iota(jnp.int32, (1, TQ), 1)
        dq_scr[...] = jnp.zeros_like(dq_scr)
        dfq_ref[...] = jnp.zeros_like(dfq_ref)

        def kvblock(j, c):
            c0 = pl.multiple_of(j * TQ, TQ)
            k = k_ref[pl.ds(c0, TQ), :]
            v = v_ref[pl.ds(c0, TQ), :]
            kf = k.astype(f32)
            km = [jnp.where(hm[hh], kf, 0.0).astype(bf16) for hh in (0, 1)]
            frow = ftb_ref[j]
            fr = [jnp.sum(jnp.where(sub8 == 2 * p + hh, frow, 0.0), axis=0, keepdims=True) for hh in (0, 1)]

            def qblock(i, carry):
                dk, dv, df0, df1 = carry
                df = [df0, df1]
                r0 = pl.multiple_of(i * TQ, TQ)
                q = q_ref[pl.ds(r0, TQ), :].astype(f32) * 0.125
                do = do_ref[pl.ds(r0, TQ), :].astype(f32)
                prod = do * o_ref[pl.ds(r0, TQ), :].astype(f32)
                lseb = lse_ref[pl.ds(r0, TQ), :]
                Fb = F_ref[pl.ds(r0, TQ), :]
                causal = (c0 + coli) <= (r0 + rowi)
                dqacc = jnp.zeros((TQ, 128), f32)
                rowsum = jnp.zeros((TQ, 128), f32)
                for hh in (0, 1):
                    qh = jnp.where(hm[hh], q, 0.0).astype(bf16)
                    doh = jnp.where(hm[hh], do, 0.0).astype(bf16)
                    delta = jnp.sum(jnp.where(hm[hh], prod, 0.0), axis=1, keepdims=True)
                    fc = jnp.sum(jnp.where(lane == 2 * p + hh, Fb, 0.0), axis=1, keepdims=True)
                    s = _dot_nt(qh, k) + (fc - fr[hh])
                    pr = jnp.where(causal, jnp.exp(s - lseb[:, hh * HD:hh * HD + 1]), 0.0)
                    dp = _dot_nt(doh, v)
                    ds = pr * (dp - delta)
                    dsb = ds.astype(bf16)
                    dv = dv + _dot_tn(pr.astype(bf16), doh)
                    dk = dk + _dot_tn(dsb, qh)
                    dqacc = dqacc + _dot(dsb, km[hh])
                    df[hh] = df[hh] - jnp.sum(ds, axis=0, keepdims=True)
                    rowsum = jnp.where(hm[hh], jnp.sum(ds, axis=1, keepdims=True), rowsum)
                dq_scr[pl.ds(r0, TQ), :] += dqacc * 0.125
                dfq_ref[pl.ds(r0, TQ), :] += rowsum
                return dk, dv, df[0], df[1]

            z = jnp.zeros((TQ, 128), f32)
            zr = jnp.zeros((1, TQ), f32)
            dk, dv, df0, df1 = lax.fori_loop(j, nb, qblock, (z, z, zr, zr))
            dk_ref[pl.ds(c0, TQ), :] = dk.astype(bf16)
            dv_ref[pl.ds(c0, TQ), :] = dv.astype(bf16)
            dft_ref[j] = jnp.where(sub8 == 0, df0, jnp.where(sub8 == 1, df1, 0.0))
            return c

        lax.fori_loop(0, nb, kvblock, 0)
        dq_ref[...] = dq_scr[...].astype(bf16)

    blk = lambda off: pl.BlockSpec((S, 128), lambda p: (0, off + p))
    return _call(
        order, body, (qkva, qkva, qkva, doa, oa, lse, F, ftb), name="fox_bwd", grid=(4,),
        in_specs=[blk(0), blk(4), blk(8), blk(0), blk(0), blk(0), pl.BlockSpec((S, 128), lambda p: (0, 0)),
                  pl.BlockSpec((nb, 8, TQ), lambda p: (0, 0, 0))],
        out_specs=[blk(0), blk(0), blk(0), pl.BlockSpec((None, nb, 8, TQ), lambda p: (p, 0, 0, 0)), blk(0)],
        out_shape=[jax.ShapeDtypeStruct((S, FOXW), bf16)] * 3 + [jax.ShapeDtypeStruct((4, nb, 8, TQ), f32),
                                                                 jax.ShapeDtypeStruct((S, FOXW), f32)],
        scratch_shapes=[pltpu.VMEM((S, 128), f32)],
        compiler_params=_params(("parallel",)),
    )


def _forget_bwd(order, dft, dfq, fa, bpad):
    nb = S // TQ

    def body(dft_ref, dfq_ref, fa_ref, b_ref, dfa_ref, db_ref, rows):
        rr = lax.broadcasted_iota(jnp.int32, (TQ, TQ), 0)
        cc = lax.broadcasted_iota(jnp.int32, (TQ, TQ), 1)
        upper = (cc >= rr).astype(bf16)
        ones = jnp.ones((8, TQ), bf16)
        lane = lax.broadcasted_iota(jnp.int32, (1, 128), 1)
        carry = jnp.zeros((1, 128), f32)
        db = jnp.zeros((1, 128), f32)
        rows[...] = jnp.zeros_like(rows)
        for b in reversed(range(nb)):
            for p in range(4):
                rows[2 * p:2 * p + 2, :] = dft_ref[p, b, 0:2, :]
            cols = jnp.zeros((TQ, 128), f32)
            for h in range(8):
                c0 = (h // 2) * 128 + (h % 2) * HD
                cols = jnp.where(lane == h, dfq_ref[b * TQ:(b + 1) * TQ, c0:c0 + 1], cols)
            dlf = carry
            tot = jnp.zeros((8, 128), f32)
            for part in _split3(rows[...]):
                dlf = dlf + _dot_nt(upper, part)
                tot = tot + _dot_nt(ones, part)
            for part in _split3(cols):
                dlf = dlf + _dot(upper, part)
            carry = carry + tot[0:1, :] + jnp.sum(cols, axis=0, keepdims=True)
            z = fa_ref[b * TQ:(b + 1) * TQ, :] + b_ref[...]
            dz = jnp.where(lane < 8, dlf * jax.nn.sigmoid(-z), 0.0)
            dfa_ref[b * TQ:(b + 1) * TQ, :] = dz.astype(bf16)
            db = db + jnp.sum(dz, axis=0, keepdims=True)
        db_ref[...] = db

    return _call(
        order, body, (dft, dfq, fa, bpad), name="forget_bwd",
        out_shape=[jax.ShapeDtypeStruct((S, 128), bf16), jax.ShapeDtypeStruct((1, 128), f32)],
        scratch_shapes=[pltpu.VMEM((128, TQ), f32)],
        compiler_params=_params(),
    )


def _dil_bwd(order, qkvb, dob, ob, lseb, rope):
    c_t, s1_t, s2_t = rope

    def body(*refs):
        q_refs, k_refs, v_refs = refs[0:3], refs[3:6], refs[6:9]
        dob_ref, ob_ref, lse_ref, c_ref, s1_ref, s2_ref = refs[9:15]
        dq_out, dk_out, dv_out = refs[15:18], refs[18:21], refs[21:24]
        qp, kp, vp, dop, lp, dlp, dln, dqp, dkp, dvp, nat = refs[24:35]
        _, hm = _head_masks()

        def delta_rows(i, c):
            r0 = pl.multiple_of(i * TQ, TQ)
            prod = dob_ref[pl.ds(r0, TQ), :] * ob_ref[pl.ds(r0, TQ), :].astype(f32)
            d0 = jnp.sum(jnp.where(hm[0], prod, 0.0), axis=1, keepdims=True)
            d1 = jnp.sum(jnp.where(hm[1], prod, 0.0), axis=1, keepdims=True)
            dln[pl.ds(r0, TQ), :] = jnp.where(hm[0], d0, d1)
            return c

        lax.fori_loop(0, S // TQ, delta_rows, 0)

        for g, r in enumerate(DIL):
            nbl = S // r // BAND
            if r == 1:
                srcs = (q_refs[g], k_refs[g], v_refs[g], dob_ref, lse_ref, dln)
            else:
                for dst, src in ((qp, q_refs[g]), (kp, k_refs[g]), (vp, v_refs[g]), (dop, dob_ref),
                                 (lp, lse_ref), (dlp, dln)):
                    _permute_in(dst, src, r)
                srcs = (qp, kp, vp, dop, lp, dlp)
            dkp[...] = jnp.zeros_like(dkp)
            dvp[...] = jnp.zeros_like(dvp)

            def blk(bb, c, srcs=srcs, nbl=nbl):
                qs_, ks_, vs_, dos_, ls_, dls_ = srcs
                r0, k0, valid = _band_geometry(bb, nbl)
                q = qs_[pl.ds(r0, BAND), :] * 0.125
                kwf = ks_[pl.ds(k0, 2 * BAND), :]
                kw = kwf.astype(bf16)
                vw = vs_[pl.ds(k0, 2 * BAND), :].astype(bf16)
                do = dos_[pl.ds(r0, BAND), :]
                lse = ls_[pl.ds(r0, BAND), :]
                dlt = dls_[pl.ds(r0, BAND), :]
                dq = jnp.zeros((BAND, 128), f32)
                dk = jnp.zeros((2 * BAND, 128), f32)
                dv = jnp.zeros((2 * BAND, 128), f32)
                for hh in (0, 1):
                    qh = jnp.where(hm[hh], q, 0.0).astype(bf16)
                    doh = jnp.where(hm[hh], do, 0.0).astype(bf16)
                    kh = jnp.where(hm[hh], kwf, 0.0).astype(bf16)
                    s = _dot_nt(qh, kw)
                    pr = jnp.where(valid, jnp.exp(s - lse[:, hh * HD:hh * HD + 1]), 0.0)
                    dp = _dot_nt(doh, vw)
                    ds = pr * (dp - dlt[:, hh * HD:hh * HD + 1])
                    dsb = ds.astype(bf16)
                    dv = dv + _dot_tn(pr.astype(bf16), doh)
                    dk = dk + _dot_tn(dsb, qh)
                    dq = dq + _dot(dsb, kh)
                dqp[pl.ds(r0, BAND), :] = dq * 0.125
                dkp[pl.ds(k0, 2 * BAND), :] += dk
                dvp[pl.ds(k0, 2 * BAND), :] += dv
                return c

            lax.fori_loop(0, S // BAND, blk, 0)

            for acc, out, roped in ((dqp, dq_out[g], True), (dkp, dk_out[g], True), (dvp, dv_out[g], False)):
                if r == 1:
                    src = acc
                else:
                    _permute_out(nat, acc, r)
                    src = nat

                def emit(i, c, src=src, out=out, roped=roped):
                    r0 = pl.multiple_of(i * TQ, TQ)
                    d = src[pl.ds(r0, TQ), :]
                    if roped:
                        d = (d * c_ref[pl.ds(r0, TQ), :] + pltpu.roll(d * s1_ref[pl.ds(r0, TQ), :], 8, 1)
                             + pltpu.roll(d * s2_ref[pl.ds(r0, TQ), :], 120, 1))
                    out[pl.ds(r0, TQ), :] = d.astype(bf16)
                    return c

                lax.fori_loop(0, S // TQ, emit, 0)

    pair = pl.BlockSpec((S, 128), lambda p: (0, p))
    tab = pl.BlockSpec((S, 128), lambda p: (0, 0))
    return _call(
        order, body, [qkvb] * 9 + [dob, ob, lseb, c_t, s1_t, s2_t], name="dil_bwd", grid=(2,),
        in_specs=_dil_in_specs() + [pair, pair, pair, tab, tab, tab],
        out_specs=[pair] * 9,
        out_shape=[jax.ShapeDtypeStruct((S, DILOUT), bf16)] * 9,
        scratch_shapes=[pltpu.VMEM((S, 128), f32)] * 11,
        compiler_params=_params(("parallel",)),
    )


def _inproj_bwd(order, dproj, wp, x, dx2, g1):
    tm = 256

    def body(d_ref, w_ref, x_ref, dx2_ref, g_ref, dx_ref, dg_ref):
        i = pl.program_id(0)
        dh = _dot_nt(d_ref[...], w_ref[...])
        xb = x_ref[...]
        r = lax.rsqrt(jnp.mean(xb * xb, axis=-1, keepdims=True) + EPS)
        xh = xb * r
        gdh = dh * g_ref[...]
        dx_ref[...] = dx2_ref[...] + r * (gdh - xh * jnp.mean(gdh * xh, axis=-1, keepdims=True))

        @pl.when(i == 0)
        def _():
            dg_ref[...] = jnp.zeros_like(dg_ref)

        dg_ref[...] += jnp.sum(dh * xh, axis=0, keepdims=True)

    row = pl.BlockSpec((tm, D), lambda i: (i, 0))
    vec = pl.BlockSpec((1, D), lambda i: (0, 0))
    return _call(
        order, body, (dproj, wp, x, dx2, g1), name="inproj_bwd", grid=(S // tm,),
        in_specs=[pl.BlockSpec((tm, WP), lambda i: (i, 0)), pl.BlockSpec((D, WP), lambda i: (0, 0)), row, row, vec],
        out_specs=[row, vec],
        out_shape=[jax.ShapeDtypeStruct((S, D), f32), jax.ShapeDtypeStruct((1, D), f32)],
        compiler_params=_params(("arbitrary",)),
    )


def _pad_win(w_full):
    return jnp.concatenate([w_full[:, 0:1536], w_full[:, 1544:D_IN], w_full[:, 1536:1544],
                            jnp.zeros((D, WP - D_IN), w_full.dtype)], axis=1)


def _unpad_win(dwp):
    return jnp.concatenate([dwp[:, 0:1536], dwp[:, C_FA:C_FA + 8], dwp[:, 1536:C_FA]], axis=1)


HBM = pl.BlockSpec(memory_space=pltpu.HBM)
SEM = pl.BlockSpec(memory_space=pltpu.SEMAPHORE)
ANY = pl.BlockSpec(memory_space=pl.ANY)
SMALL_ROWS = 8


def _comm_call(name, body, bufs, order, sems_in=(), new_sems=()):
    nb, ns, nn = len(bufs), len(sems_in), len(new_sems)
    extra = [] if order.tok is None or any(order.tok is b for b in bufs) else [order.tok]

    def kern(*refs):
        off = nb + ns + len(extra)
        body(refs[:nb], refs[nb:nb + ns], refs[off:off + nn])
        refs[-1][...] = jnp.zeros((8, 128), f32)

    res = pl.pallas_call(
        kern, name=name,
        in_specs=[HBM] * nb + [SEM] * ns + [ANY] * len(extra),
        out_specs=[SEM] * nn + [HBM] * nb + [pl.BlockSpec(memory_space=pltpu.VMEM)],
        out_shape=[pltpu.SemaphoreType.DMA((k,)) for k in new_sems] + [pltpu.HBM(b.shape, b.dtype) for b in bufs]
        + [jax.ShapeDtypeStruct((8, 128), f32)],
        input_output_aliases={i: nn + i for i in range(nb)},
        compiler_params=pltpu.CompilerParams(has_side_effects=pltpu.SideEffectType.DATAFLOW_SIDE_EFFECTING),
    )(*[pltpu.with_memory_space_constraint(b, pltpu.HBM) for b in bufs], *sems_in, *extra)
    order.mark(res[-1])
    return list(res[:nn]), list(res[nn:nn + nb])


def _place():
    x, y, c = lax.axis_index("x"), lax.axis_index("y"), lax.axis_index("c")
    chips = [(1 - x, y), (x, 1 - y), (1 - x, 1 - y)]
    return x, y, c, chips


def _rcopy(src, dst, ssem, rsem, dev):
    return pltpu.make_async_remote_copy(src_ref=src, dst_ref=dst, send_sem=ssem, recv_sem=rsem,
                                        device_id=dev, device_id_type=pl.DeviceIdType.MESH)


def _half(nrows, which):
    return pl.ds(which * (nrows // 2), nrows // 2)


def _ici_copies(stack, group_sizes, ssems, rsems):
    x, y, c, chips = _place()
    me_q = 2 * x + y
    sends, recvs = [], []
    a = 0
    for grp, size in enumerate(group_sizes):
        for k in range(size):
            rows = _half(stack[a].shape[1], c)
            for j, (cx, cy) in enumerate(chips):
                mine = stack[a].at[me_q, rows]
                sends.append(_rcopy(mine, mine, ssems[grp].at[k * 3 + j], rsems[grp].at[k * 3 + j], (cx, cy, c)))
                theirs = stack[a].at[2 * cx + cy, rows]
                recvs.append(_rcopy(theirs, theirs, ssems[grp].at[k * 3 + j], rsems[grp].at[k * 3 + j],
                                    (cx, cy, c)))
            a += 1
    return sends, recvs


def _allgather_start(stacks, group_sizes, order):
    def body(bufs, _, new):
        sends, _r = _ici_copies(bufs, group_sizes, new[0::2], new[1::2])
        for cp in sends:
            cp.start()

    sizes = []
    for size in group_sizes:
        sizes += [3 * size, 3 * size]
    sems, stacks = _comm_call("allgather_start", body, stacks, order, new_sems=sizes)
    return [(sems[2 * g], sems[2 * g + 1]) for g in range(len(group_sizes))], stacks


def _forward_copies(stack, ssem, rsem):
    x, y, c, chips = _place()
    sib = (x, y, 1 - c)
    sends, recvs = [], []
    for a in range(len(stack)):
        for j, (cx, cy) in enumerate(chips):
            landed = stack[a].at[2 * cx + cy, _half(stack[a].shape[1], c)]
            sends.append(_rcopy(landed, landed, ssem.at[a * 3 + j], rsem.at[a * 3 + j], sib))
            other = stack[a].at[2 * cx + cy, _half(stack[a].shape[1], 1 - c)]
            recvs.append(_rcopy(other, other, ssem.at[a * 3 + j], rsem.at[a * 3 + j], sib))
    return sends, recvs


def _allgather_forward(name, stacks, sems, order):
    n = len(stacks)

    def body(bufs, taken, new):
        sends, recvs = _ici_copies(bufs, [n], [taken[0]], [taken[1]])
        for cp in sends:
            cp.wait_send()
        for cp in recvs:
            cp.wait_recv()
        fwd, _r = _forward_copies(bufs, new[0], new[1])
        for cp in fwd:
            cp.start()

    return _comm_call(name, body, stacks, order, sems_in=sems, new_sems=(3 * n, 3 * n))


def _allgather_finish(name, stacks, sems, order):
    def body(bufs, taken, _):
        sends, recvs = _forward_copies(bufs, taken[0], taken[1])
        for cp in sends:
            cp.wait_send()
        for cp in recvs:
            cp.wait_recv()

    return _comm_call(name, body, stacks, order, sems_in=sems)[1]


def _pair_copies(g, t, ssem, rsem):
    x, y, c, _ = _place()
    return [_rcopy(g[a].at[:, _half(g[a].shape[1], 1 - c), :], t[a], ssem.at[a], rsem.at[a], (x, y, 1 - c))
            for a in range(len(g))]


def _pair_start(name, gs, order):
    n = len(gs)
    ts = [lax.empty((NCHIP, g.shape[1] // 2, g.shape[2]), f32) for g in gs]

    def body(bufs, _, new):
        for cp in _pair_copies(bufs[:n], bufs[n:], new[0], new[1]):
            cp.start()

    sems, bufs = _comm_call(name, body, list(gs) + ts, order, new_sems=(n, n))
    return sems, bufs


def _pair_wait(name, bufs, sems, order):
    n = len(bufs) // 2

    def body(refs, taken, _):
        for cp in _pair_copies(refs[:n], refs[n:], taken[0], taken[1]):
            cp.wait_send()
            cp.wait_recv()

    bufs = _comm_call(name, body, bufs, order, sems_in=sems)[1]
    return bufs[:n], bufs[n:]


def _row_tile(h):
    return min(h, 256)


def _pair_add(order, g, t, c_arr, name):
    _, R, C = g.shape
    h = R // 2
    tr = _row_tile(h)
    nblk = h // tr

    def body(c_ref, g_ref, t_ref, p32_ref, p16_ref):
        s = g_ref[...] + t_ref[...]
        p32_ref[...] = s
        p16_ref[...] = s.astype(bf16)

    blk = pl.BlockSpec((None, tr, C), lambda q, i, c_ref: (q, i, 0))
    return _call_indexed(
        order, body, (c_arr,), (g, t), (NCHIP, nblk),
        [pl.BlockSpec((None, tr, C), lambda q, i, c_ref: (q, c_ref[0] * nblk + i, 0)), blk], [blk, blk],
        name=name,
        out_shape=[jax.ShapeDtypeStruct((NCHIP, h, C), f32), jax.ShapeDtypeStruct((NCHIP, h, C), bf16)],
        compiler_params=_params(("parallel", "parallel")),
    )


def _shard_copies(p, r, sm, ssem, rsem):
    x, y, c, chips = _place()
    n = len(p)
    sends, recvs = [], []
    for a in range(n):
        for j, (cx, cy) in enumerate(chips):
            k = a * 3 + j
            sends.append(_rcopy(p[a].at[2 * cx + cy], r[a].at[j], ssem.at[k], rsem.at[k], (cx, cy, c)))
            recvs.append(_rcopy(r[a].at[j], r[a].at[j], ssem.at[k], rsem.at[k], (cx, cy, c)))
    if sm is not None:
        mine = sm.at[4 * x + 2 * y + c]
        for i in range(1, 8):
            px = (1 - x) if i & 4 else x
            py = (1 - y) if i & 2 else y
            pc = (1 - c) if i & 1 else c
            k = 3 * n + i - 1
            sends.append(_rcopy(mine, mine, ssem.at[k], rsem.at[k], (px, py, pc)))
            slot = sm.at[4 * px + 2 * py + pc]
            recvs.append(_rcopy(slot, slot, ssem.at[k], rsem.at[k], (px, py, pc)))
    return sends, recvs


def _shard_start(name, p16s, order, sm=None):
    n = len(p16s)
    rs = [lax.empty((3,) + p.shape[1:], bf16) for p in p16s]
    extra = [] if sm is None else [sm]
    nsem = 3 * n + (7 if sm is not None else 0)

    def body(bufs, _, new):
        sends, _r = _shard_copies(bufs[:n], bufs[n:2 * n], bufs[2 * n] if extra else None, new[0], new[1])
        for cp in sends:
            cp.start()

    return _comm_call(name, body, list(p16s) + rs + extra, order, new_sems=(nsem, nsem))


def _shard_wait(name, bufs, sems, n, order):
    has_sm = len(bufs) > 2 * n

    def body(refs, taken, _):
        sends, recvs = _shard_copies(refs[:n], refs[n:2 * n], refs[2 * n] if has_sm else None, taken[0], taken[1])
        for cp in sends:
            cp.wait_send()
        for cp in recvs:
            cp.wait_recv()

    bufs = _comm_call(name, body, bufs, order, sems_in=sems)[1]
    return bufs[n:2 * n], (bufs[2 * n] if has_sm else None)


def _shard_sum(order, p32, r, q_arr, c_arr, name):
    _, h, C = p32.shape
    tr = _row_tile(h)
    nblk = h // tr

    def body(q_ref, c_ref, p_ref, r_ref, o_ref):
        s = p_ref[...]
        for j in range(3):
            s = s + r_ref[j].astype(f32)
        o_ref[...] = s

    return _call_indexed(
        order, body, (q_arr, c_arr), (p32, r), (nblk,),
        [pl.BlockSpec((None, tr, C), lambda i, q_ref, c_ref: (q_ref[0], i, 0)),
         pl.BlockSpec((3, tr, C), lambda i, q_ref, c_ref: (0, i, 0))],
        pl.BlockSpec((tr, C), lambda i, q_ref, c_ref: (c_ref[0] * nblk + i, 0)),
        name=name, out_shape=jax.ShapeDtypeStruct((2 * h, C), f32),
        compiler_params=_params(("parallel",)),
    )


def _swap_copies(full, ssem, rsem):
    x, y, c, _ = _place()
    sends, recvs = [], []
    for a in range(len(full)):
        mine = full[a].at[_half(full[a].shape[0], c)]
        sends.append(_rcopy(mine, mine, ssem.at[a], rsem.at[a], (x, y, 1 - c)))
        other = full[a].at[_half(full[a].shape[0], 1 - c)]
        recvs.append(_rcopy(other, other, ssem.at[a], rsem.at[a], (x, y, 1 - c)))
    return sends, recvs


def _swap_start(name, fulls, order):
    n = len(fulls)

    def body(bufs, _, new):
        for cp in _swap_copies(bufs, new[0], new[1])[0]:
            cp.start()

    return _comm_call(name, body, list(fulls), order, new_sems=(n, n))


def _swap_wait(name, fulls, sems, order):
    def body(refs, taken, _):
        sends, recvs = _swap_copies(refs, taken[0], taken[1])
        for cp in sends:
            cp.wait_send()
        for cp in recvs:
            cp.wait_recv()

    return _comm_call(name, body, fulls, order, sems_in=sems)[1]


def _small_sum(order, sm):
    def body(sm_ref, o_ref):
        s = sm_ref[0]
        for d in range(1, 8):
            s = s + sm_ref[d]
        o_ref[...] = s

    return _call(order, body, (sm,), name="small_grad_sum", out_shape=jax.ShapeDtypeStruct((SMALL_ROWS, D), f32))


def _adamw(order, w, g, m, v, name):
    R, C = w.shape
    tr = min(R, 256)

    def body(w_ref, g_ref, m_ref, v_ref, d_ref, nm_ref, nv_ref):
        g_ = g_ref[...]
        m_ = ADAM_B1 * m_ref[...] + (1.0 - ADAM_B1) * g_
        v_ = ADAM_B2 * v_ref[...] + (1.0 - ADAM_B2) * (g_ * g_)
        m_hat = m_ / (1.0 - ADAM_B1 ** ADAM_STEP)
        v_hat = v_ / (1.0 - ADAM_B2 ** ADAM_STEP)
        d_ref[...] = -ADAM_LR * (m_hat / (jnp.sqrt(v_hat) + ADAM_EPS) + ADAM_WD * w_ref[...])
        nm_ref[...] = m_
        nv_ref[...] = v_

    blk = pl.BlockSpec((tr, C), lambda i: (i, 0))
    return _call(
        order, body, (w, g, m, v), name=name, grid=(R // tr,), in_specs=[blk] * 4, out_specs=[blk] * 3,
        out_shape=[jax.ShapeDtypeStruct((R, C), f32)] * 3,
        compiler_params=_params(("parallel",)),
    )


def _stack_cols(w, n):
    K_, N = w.shape
    return jnp.stack([w[:, q * (N // n):(q + 1) * (N // n)] for q in range(n)], axis=0)


def kernel(x, norm_attn_g, w_in, b_forget, w_branch_a, w_branch_b, w_out, norm_mlp_g, w_up, w_down, norm_final_g, loss_target, m_norm_attn_g, m_w_in, m_b_forget, m_w_branch_a, m_w_branch_b, m_w_out, m_norm_mlp_g, m_w_up, m_w_down, m_norm_final_g, v_norm_attn_g, v_w_in, v_b_forget, v_w_branch_a, v_w_branch_b, v_w_out, v_norm_mlp_g, v_w_up, v_w_down, v_norm_final_g):
    xi, yi, ci = lax.axis_index("x"), lax.axis_index("y"), lax.axis_index("c")
    c_arr = jnp.reshape(ci, (1,)).astype(jnp.int32)
    q_arr = jnp.reshape(2 * xi + yi, (1,)).astype(jnp.int32)
    x_, tgt = x[0], loss_target[0]

    names = ["w_in", "w_branch_a", "w_branch_b", "w_out", "w_up", "w_down"]
    big = dict(zip(names, [w_in[0], w_branch_a[0], w_branch_b[0], w_out[0], w_up[0], w_down[0]]))
    ms = dict(zip(names, [m_w_in[0], m_w_branch_a[0], m_w_branch_b[0], m_w_out[0], m_w_up[0], m_w_down[0]]))
    vs = dict(zip(names, [v_w_in[0], v_w_branch_a[0], v_w_branch_b[0], v_w_out[0], v_w_up[0], v_w_down[0]]))
    grad, upd = {}, {}

    stacks = [lax.dynamic_update_slice(lax.empty((NCHIP,) + w.shape, bf16), w.astype(bf16)[None], (q_arr[0], 0, 0))
              for w in big.values()]
    order = _Order()

    def run(fn, *args, **kw):
        return fn(order, *args, **kw)

    (sem_in, sem_rest), stacks = _allgather_start(stacks, [1, 5], order)
    sem_f, win_s = _allgather_forward("allgather_forward_in", stacks[0:1], sem_in, order)
    (win_s,) = _allgather_finish("allgather_finish_in", win_s, sem_f, order)
    wp = _pad_win(jnp.concatenate([win_s[q] for q in range(NCHIP)], axis=1))

    rope = _rope_tables()
    bpad = jnp.pad(b_forget, ((0, 0), (0, 120)))
    h1, qkva, qkvb, gates, fa = run(_norm_inproj, x_, norm_attn_g, wp, rope)
    F, ftb = run(_forget_cumsum, fa, bpad)
    oa, lsea = run(_fox_fwd, qkva, F, ftb)
    sem_f, rest = _allgather_forward("allgather_forward_rest", stacks[1:], sem_rest, order)
    ob, lseb = run(_dil_fwd, qkvb)
    was, wbs, wouts, wups, wdowns = _allgather_finish("allgather_finish_rest", rest, sem_f, order)
    wout = wouts.reshape(D, D)
    wdown = wdowns.reshape(DFF, D)
    ya, yb, mixed = run(_branch_mix, oa, ob, was, wbs, gates)
    x2, h2 = run(_outproj_norm, mixed, wout, x_, norm_mlp_g)
    u, a = run(_mlp_up, h2, wups)
    dx3, dx3b, dg3, loss_part = run(_mlp_down_loss, a, wdown, x2, norm_final_g.reshape(1, D), tgt)
    loss = lax.psum(loss_part[0, 0], ("x", "y", "c"))

    def reduce_to_pairs(tag, group, bufs, sems):
        gs, ts = _pair_wait("pair_wait_" + tag, bufs, sems, order)
        return zip(*[run(_pair_add, gs[i], ts[i], c_arr, "pair_add_" + nm) for i, nm in enumerate(group)])

    def reduce_to_shard(tag, group, p32s, bufs, sems):
        rs, sm_all = _shard_wait("shard_wait_" + tag, bufs, sems, len(group), order)
        fulls = [run(_shard_sum, p32s[i], rs[i], q_arr, c_arr, "shard_sum_" + nm) for i, nm in enumerate(group)]
        return _swap_start("swap_start_" + tag, fulls, order), sm_all

    def finish(tag, group, fulls, sems):
        fulls = _swap_wait("swap_wait_" + tag, fulls, sems, order)
        for nm, gfull in zip(group, fulls):
            grad[nm] = gfull
            upd[nm] = run(_adamw, big[nm], gfull, ms[nm], vs[nm], "adamw_" + nm)

    grp_a, grp_b, grp_c = ["w_down", "w_up"], ["w_out", "w_branch_a", "w_branch_b"], ["w_in"]
    du = run(_mlp_down_bwd, dx3b, wdown, u)
    dwdown = run(_mm, a, dx3b, "tn", f32, 1024, D, "wgrad_down")
    dwup = run(_mm, h2, du, "tn", f32, D, 1024, "wgrad_up", stack_cols=True)
    sem_pa, buf_pa = _pair_start("pair_start_a", [dwdown.reshape(NCHIP, DFF // NCHIP, D), dwup], order)
    dx2, dx2b, dg2 = run(_mlp_up_bwd, du, wups, x2, dx3, norm_mlp_g)
    p32_a, p16_a = reduce_to_pairs("a", grp_a, buf_pa, sem_pa)
    sem_sa, buf_sa = _shard_start("shard_start_a", p16_a, order)
    dya, dyb, dgates = run(_gate_bwd, dx2b, wout, gates, ya, yb)
    dwout = run(_mm, mixed, dx2b, "tn", f32, D, D, "wgrad_out")
    doa, dob = run(_branch_bwd, dya, dyb, was, wbs)
    dwas, dwbs = run(_branch_wgrad, oa, ob, dya, dyb)
    sem_pb, buf_pb = _pair_start("pair_start_b", [dwout.reshape(NCHIP, D // NCHIP, D), dwas, dwbs], order)
    dqa, dka, dva, dft, dfq = run(_fox_bwd, qkva, doa, oa, lsea, F, ftb)
    p32_b, p16_b = reduce_to_pairs("b", grp_b, buf_pb, sem_pb)
    (sem_wa, fulls_a), _ = reduce_to_shard("a", grp_a, p32_a, buf_sa, sem_sa)
    sem_sb, buf_sb = _shard_start("shard_start_b", p16_b, order)
    dfa, dbf = run(_forget_bwd, dft, dfq, fa, bpad)
    dd = run(_dil_bwd, qkvb, dob, ob, lseb, rope)
    (sem_wb, fulls_b), _ = reduce_to_shard("b", grp_b, p32_b, buf_sb, sem_sb)
    finish("a", grp_a, fulls_a, sem_wa)
    dproj = jnp.concatenate([dqa, dka, dva, *dd, dgates, dfa], axis=1)
    dwp = run(_mm, h1, dproj, "tn", f32, D, 128, "wgrad_in")
    sem_pc, buf_pc = _pair_start("pair_start_c", [_stack_cols(_unpad_win(dwp), NCHIP)], order)
    finish("b", grp_b, fulls_b, sem_wb)
    p32_c, p16_c = reduce_to_pairs("c", grp_c, buf_pc, sem_pc)
    sem_sc, buf_sc = _shard_start("shard_start_c", p16_c, order)
    gx, dg1 = run(_inproj_bwd, dproj, wp, x_, dx2, norm_attn_g)
    small = jnp.concatenate([dg1, dg2, dg3, jnp.pad(dbf[:, 0:8], ((0, 0), (0, D - 8))),
                             jnp.zeros((SMALL_ROWS - 4, D), f32)], axis=0)
    sm = lax.dynamic_update_slice(lax.empty((8, SMALL_ROWS, D), f32), small[None],
                                  (4 * xi + 2 * yi + ci, 0, 0))
    sem_sm, buf_sm = _shard_start("small_start", [], order, sm)
    (sem_wc, fulls_c), _ = reduce_to_shard("c", grp_c, p32_c, buf_sc, sem_sc)
    _, sm = _shard_wait("small_wait", buf_sm, sem_sm, 0, order)
    gsmall = run(_small_sum, sm)

    grad["norm_attn_g"], grad["norm_mlp_g"] = gsmall[0:1], gsmall[1:2]
    grad["norm_final_g"], grad["b_forget"] = gsmall[2:3], gsmall[3:4, 0:8]
    upd["norm_attn_g"] = run(_adamw, norm_attn_g, grad["norm_attn_g"], m_norm_attn_g, v_norm_attn_g, "adamw_g1")
    upd["norm_mlp_g"] = run(_adamw, norm_mlp_g, grad["norm_mlp_g"], m_norm_mlp_g, v_norm_mlp_g, "adamw_g2")
    upd["norm_final_g"] = run(_adamw, norm_final_g.reshape(1, D), grad["norm_final_g"],
                              m_norm_final_g.reshape(1, D), v_norm_final_g.reshape(1, D), "adamw_g3")
    upd["b_forget"] = run(_adamw, b_forget, grad["b_forget"], m_b_forget, v_b_forget, "adamw_bf")
    finish("c", grp_c, fulls_c, sem_wc)

    order = ["norm_attn_g", "w_in", "b_forget", "w_branch_a", "w_branch_b", "w_out", "norm_mlp_g", "w_up", "w_down",
             "norm_final_g"]
    shapes = dict(norm_attn_g=norm_attn_g.shape, w_in=w_in.shape, b_forget=b_forget.shape,
                  w_branch_a=w_branch_a.shape, w_branch_b=w_branch_b.shape, w_out=w_out.shape,
                  norm_mlp_g=norm_mlp_g.shape, w_up=w_up.shape, w_down=w_down.shape, norm_final_g=norm_final_g.shape)
    outs = [loss, gx.reshape(x.shape)]
    outs += [grad[nm].reshape(shapes[nm]) for nm in order]
    for k in range(3):
        outs += [upd[nm][k].reshape(shapes[nm]) for nm in order]
    return tuple(outs)
```

```python
import jax
import jax.numpy as jnp
from jax import lax
from jax.experimental import pallas as pl
from jax.experimental.pallas import tpu as pltpu

f32 = jnp.float32
bf16 = jnp.bfloat16

S = 2048
D = 1024
DFF = 4096
HD = 64
FOXW = 512
DILOUT = 256
DIL = (1, 4, 16)
BAND = 128
EPS = 1e-6
NEG = -1e30
ROPE_THETA = 500000.0
NCHIP = 4
TQ = 256

ADAM_LR, ADAM_B1, ADAM_B2, ADAM_EPS, ADAM_WD, ADAM_STEP = 0.001, 0.9, 0.999, 1e-08, 0.01, 10
VMEM_LIMIT = 56 * 1024 * 1024

UNIT = 64
NP = 6144
F_DIL, F_FOX, F_FA, F_G = 0, 2304, 3840, 4096
DIL_BLK, FOX_BLK = 1152, 384
WIN_UNITS, WIN_ROWS = 24, 1536
WIN_UNIT0 = (0, 23, 45, 68)
OWN_ROW0 = (0, 2, 60, 62)
SHARD_IN = 1474
FA_ROWS = 32


def _compact_to_internal():
    c2i = {}
    for p in range(2):
        for role in range(3):
            for g in range(3):
                for hh in range(2):
                    c2i[24 + 12 * role + 4 * g + 2 * p + hh] = 18 * p + 6 * role + 2 * g + hh
    for p in range(4):
        for role in range(3):
            for hh in range(2):
                c2i[8 * role + 2 * p + hh] = F_FOX // UNIT + 6 * p + 2 * role + hh
    for j in range(32):
        c2i[60 + j] = F_G // UNIT + j
    return c2i


C2I = _compact_to_internal()
OVERLAP_UNITS = (23, 45, 46, 68)


def _params(sem=None):
    return pltpu.CompilerParams(dimension_semantics=sem, vmem_limit_bytes=VMEM_LIMIT)


class _Order:
    def __init__(self):
        self.tok = None

    def mark(self, v):
        self.tok = v

    def token_for(self, args):
        return [] if self.tok is None or any(self.tok is a for a in args) else [self.tok]


def _call(order, body, args, in_specs=None, **kw):
    args = list(args)
    n_in = len(args)
    if in_specs is None:
        in_specs = [pl.BlockSpec(memory_space=pltpu.VMEM)] * n_in
    kern = body
    extra = order.token_for(args)
    if extra:
        in_specs = list(in_specs) + [pl.BlockSpec(memory_space=pl.ANY)]

        def kern(*refs):
            body(*refs[:n_in], *refs[n_in + 1:])

    out = pl.pallas_call(kern, in_specs=in_specs, **kw)(*args, *extra)
    order.mark(out[0] if isinstance(out, (tuple, list)) else out)
    return out


def _call_indexed(order, body, scalars, args, grid, in_specs, out_specs, **kw):
    args, in_specs = list(args), list(in_specs)
    n_front = len(scalars) + len(args)
    kern = body
    extra = order.token_for(args)
    if extra:
        in_specs.append(pl.BlockSpec(memory_space=pl.ANY))

        def kern(*refs):
            body(*refs[:n_front], *refs[n_front + 1:])

    out = pl.pallas_call(
        kern, grid_spec=pltpu.PrefetchScalarGridSpec(num_scalar_prefetch=len(scalars), grid=grid, in_specs=in_specs,
                                                     out_specs=out_specs), **kw)(*scalars, *args, *extra)
    order.mark(out[0] if isinstance(out, (tuple, list)) else out)
    return out


def _dot(a, b):
    return jnp.dot(a, b, preferred_element_type=f32)


def _dot_nt(a, b):
    return lax.dot_general(a, b, (((1,), (1,)), ((), ())), preferred_element_type=f32)


def _dot_tn(a, b):
    return lax.dot_general(a, b, (((0,), (0,)), ((), ())), preferred_element_type=f32)


def _split3(x):
    hi = x.astype(bf16)
    r1 = x - hi.astype(f32)
    mid = r1.astype(bf16)
    lo = (r1 - mid.astype(f32)).astype(bf16)
    return hi, mid, lo


def _rope_tables():
    half = 8
    inv_freq = jnp.power(jnp.float32(ROPE_THETA), -jnp.arange(half, dtype=f32) * 2.0 / 16)
    ang = jnp.arange(S).astype(f32)[:, None] * inv_freq[None, :]
    cos, sin = jnp.cos(ang), jnp.sin(ang)
    one = jnp.ones((S, HD - 16), f32)
    zero = jnp.zeros((S, HD - 16), f32)
    z8 = jnp.zeros((S, 8), f32)
    c = jnp.concatenate([cos, cos, one], axis=1)
    s1 = jnp.concatenate([-sin, z8, zero], axis=1)
    s2 = jnp.concatenate([z8, sin, zero], axis=1)
    return tuple(jnp.concatenate([t, t], axis=1) for t in (c, s1, s2))


def _mm(order, a, b, mode, out_dtype, tm, tn, name, stack_cols=False):
    if mode == "nn":
        (M, K), (_, N) = a.shape, b.shape
        a_spec = pl.BlockSpec((tm, K), lambda i, j: (i, 0))
        b_spec = pl.BlockSpec((K, tn), lambda i, j: (0, j))
        dot = _dot
    elif mode == "nt":
        (M, K), (N, _) = a.shape, b.shape
        a_spec = pl.BlockSpec((tm, K), lambda i, j: (i, 0))
        b_spec = pl.BlockSpec((tn, K), lambda i, j: (j, 0))
        dot = _dot_nt
    else:
        (K, M), (_, N) = a.shape, b.shape
        a_spec = pl.BlockSpec((K, tm), lambda i, j: (0, i))
        b_spec = pl.BlockSpec((K, tn), lambda i, j: (0, j))
        dot = _dot_tn

    def body(a_ref, b_ref, o_ref):
        o_ref[...] = dot(a_ref[...], b_ref[...]).astype(out_dtype)

    if stack_cols:
        assert tm == M
        out_spec = pl.BlockSpec((None, tm, tn), lambda i, j: (j, 0, 0))
        out_shape = jax.ShapeDtypeStruct((N // tn, M, tn), out_dtype)
    else:
        out_spec = pl.BlockSpec((tm, tn), lambda i, j: (i, j))
        out_shape = jax.ShapeDtypeStruct((M, N), out_dtype)
    return _call(
        order, body, (a, b), name=name, grid=(M // tm, N // tn), in_specs=[a_spec, b_spec],
        out_specs=out_spec, out_shape=out_shape,
        compiler_params=_params(("parallel", "parallel")),
    )


def _assemble_win(order, wins, fas):
    def body(win_ref, fa_ref, o_ref):
        q = pl.program_id(0)

        @pl.when(q == 0)
        def _():
            o_ref[...] = jnp.zeros_like(o_ref)

        for k in range(NCHIP):
            @pl.when(q == k)
            def _(k=k):
                for j in range(WIN_UNITS):
                    cu = WIN_UNIT0[k] + j
                    dst = pl.ds(C2I[cu] * UNIT, UNIT)
                    if cu in OVERLAP_UNITS:
                        o_ref[dst, :] += win_ref[j * UNIT:(j + 1) * UNIT, :]
                    else:
                        o_ref[dst, :] = win_ref[j * UNIT:(j + 1) * UNIT, :]
                if k == 1:
                    o_ref[F_FA:F_FA + FA_ROWS, :] = fa_ref[...]

    return _call(
        order, body, (wins, fas), name="assemble_w_in", grid=(NCHIP,),
        in_specs=[pl.BlockSpec((None, WIN_ROWS, D), lambda q: (q, 0, 0)),
                  pl.BlockSpec((None, FA_ROWS, D), lambda q: (1, 0, 0))],
        out_specs=pl.BlockSpec((NP, D), lambda q: (0, 0)),
        out_shape=jax.ShapeDtypeStruct((NP, D), bf16),
        compiler_params=_params(("arbitrary",)),
    )


def _norm_inproj(order, x, g1, wt, rope):
    tm = 256
    c_t, s1_t, s2_t = rope

    def body(x_ref, g_ref, w_ref, c_ref, s1_ref, s2_ref, h_ref, qkvb_ref, qkva_ref, gates_ref, fa_ref):
        xb = x_ref[...]
        r = lax.rsqrt(jnp.mean(xb * xb, axis=-1, keepdims=True) + EPS)
        h = ((xb * r) * g_ref[...]).astype(bf16)
        h_ref[...] = h
        c, s1, s2 = c_ref[...], s1_ref[...], s2_ref[...]
        for p in range(2):
            pb = _dot_nt(h, w_ref[F_DIL + p * DIL_BLK:F_DIL + (p + 1) * DIL_BLK, :])
            for ch in range(DIL_BLK // 128):
                pc = pb[:, ch * 128:(ch + 1) * 128]
                if ch < 6:
                    pc = pc * c + pltpu.roll(pc, 120, 1) * s1 + pltpu.roll(pc, 8, 1) * s2
                qkvb_ref[:, p * DIL_BLK + ch * 128:p * DIL_BLK + (ch + 1) * 128] = pc
        qkva_ref[...] = _dot_nt(h, w_ref[F_FOX:F_FA, :]).astype(bf16)
        fa_ref[...] = _dot_nt(h, w_ref[F_FA:F_FA + 128, :])
        gates_ref[...] = _dot_nt(h, w_ref[F_G:NP, :])

    row = lambda w: pl.BlockSpec((tm, w), lambda i: (i, 0))
    return _call(
        order, body, (x, g1, wt, c_t, s1_t, s2_t), name="norm_inproj", grid=(S // tm,),
        in_specs=[row(D), pl.BlockSpec((1, D), lambda i: (0, 0)), pl.BlockSpec((NP, D), lambda i: (0, 0)),
                  row(128), row(128), row(128)],
        out_specs=[row(D), row(2 * DIL_BLK), row(4 * FOX_BLK), row(2 * D), row(128)],
        out_shape=[jax.ShapeDtypeStruct((S, D), bf16), jax.ShapeDtypeStruct((S, 2 * DIL_BLK), f32),
                   jax.ShapeDtypeStruct((S, 4 * FOX_BLK), bf16), jax.ShapeDtypeStruct((S, 2 * D), f32),
                   jax.ShapeDtypeStruct((S, 128), f32)],
        compiler_params=_params(("parallel",)),
    )


def _forget_cumsum(order, fa, bpad):
    nb = S // TQ

    def body(fa_ref, b_ref, F_ref, ftb_ref):
        rr = lax.broadcasted_iota(jnp.int32, (TQ, TQ), 0)
        cc = lax.broadcasted_iota(jnp.int32, (TQ, TQ), 1)
        tri = (rr >= cc).astype(bf16)
        lane = lax.broadcasted_iota(jnp.int32, (1, 128), 1)
        carry = jnp.zeros((1, 128), f32)
        for b in range(nb):
            z = fa_ref[b * TQ:(b + 1) * TQ, :] + b_ref[...]
            lf = jnp.minimum(z, 0.0) - jnp.log(1.0 + jnp.exp(-jnp.abs(z)))
            lf = jnp.where(lane < 8, lf, 0.0)
            hi, mid, lo = _split3(lf)
            fb = (_dot(tri, hi) + _dot(tri, mid)) + _dot(tri, lo) + carry
            F_ref[b * TQ:(b + 1) * TQ, :] = fb
            ftb_ref[b] = fb.T[0:8, :]
            carry = fb[TQ - 1:TQ, :]

    return _call(
        order, body, (fa, bpad), name="forget_cumsum",
        out_shape=[jax.ShapeDtypeStruct((S, 128), f32), jax.ShapeDtypeStruct((nb, 8, TQ), f32)],
        compiler_params=_params(),
    )


def _head_masks():
    lane = lax.broadcasted_iota(jnp.int32, (1, 128), 1)
    return lane, (lane < HD, lane >= HD)


def _fox_fwd(order, qkva, F, ftb):
    nb = S // TQ

    def body(qkv_ref, F_ref, ftb_ref, o_ref, lse_ref):
        p = pl.program_id(0)
        lane, hm = _head_masks()
        sub8 = lax.broadcasted_iota(jnp.int32, (8, 1), 0)
        rowi = lax.broadcasted_iota(jnp.int32, (TQ, 1), 0)
        coli = lax.broadcasted_iota(jnp.int32, (1, TQ), 1)

        def qblock(i, c):
            r0 = pl.multiple_of(i * TQ, TQ)
            q = qkv_ref[pl.ds(r0, TQ), 0:128].astype(f32) * 0.125
            qs = [jnp.where(hm[hh], q, 0.0).astype(bf16) for hh in (0, 1)]
            Fb = F_ref[pl.ds(r0, TQ), :]
            fc = [jnp.sum(jnp.where(lane == 2 * p + hh, Fb, 0.0), axis=1, keepdims=True) for hh in (0, 1)]

            def kvblock(j, carry):
                c0 = pl.multiple_of(j * TQ, TQ)
                k = qkv_ref[pl.ds(c0, TQ), 128:256]
                v = qkv_ref[pl.ds(c0, TQ), 256:384]
                frow = ftb_ref[j]
                causal = (c0 + coli) <= (r0 + rowi)
                new = []
                for hh in (0, 1):
                    m, l, a = carry[3 * hh:3 * hh + 3]
                    fr = jnp.sum(jnp.where(sub8 == 2 * p + hh, frow, 0.0), axis=0, keepdims=True)
                    s = _dot_nt(qs[hh], k) + (fc[hh] - fr)
                    s = jnp.where(causal, s, NEG)
                    mn = jnp.maximum(m, jnp.max(s, axis=1, keepdims=True))
                    al = jnp.exp(m - mn)
                    pr = jnp.exp(s - mn)
                    l = al * l + jnp.sum(pr, axis=1, keepdims=True)
                    a = al * a + _dot(pr.astype(bf16), v)
                    new += [mn, l, a]
                return tuple(new)

            init = (jnp.full((TQ, 1), NEG, f32), jnp.zeros((TQ, 1), f32), jnp.zeros((TQ, 128), f32)) * 2
            m0, l0, a0, m1, l1, a1 = lax.fori_loop(0, i + 1, kvblock, init)
            o = jnp.where(hm[0], a0 / l0, a1 / l1)
            lse = jnp.where(hm[0], m0 + jnp.log(l0), m1 + jnp.log(l1))
            o_ref[pl.ds(r0, TQ), :] = o.astype(bf16)
            lse_ref[pl.ds(r0, TQ), :] = lse
            return c

        lax.fori_loop(0, nb, qblock, 0)

    pair = pl.BlockSpec((S, 128), lambda p: (0, p))
    return _call(
        order, body, (qkva, F, ftb), name="fox_fwd", grid=(4,),
        in_specs=[pl.BlockSpec((S, FOX_BLK), lambda p: (0, p)), pl.BlockSpec((S, 128), lambda p: (0, 0)),
                  pl.BlockSpec((nb, 8, TQ), lambda p: (0, 0, 0))],
        out_specs=[pair, pair],
        out_shape=[jax.ShapeDtypeStruct((S, FOXW), bf16), jax.ShapeDtypeStruct((S, FOXW), f32)],
        compiler_params=_params(("parallel",)),
    )


def _permute_in(dst, src, r):
    L = S // r
    for rho in range(r):
        dst[rho * L:(rho + 1) * L, :] = src[pl.ds(rho, L, stride=r), :]


def _permute_out(dst, src, r):
    L = S // r
    for rho in range(r):
        dst[pl.ds(rho, L, stride=r), :] = src[rho * L:(rho + 1) * L, :]


def _band_geometry(bb, nbl):
    r0 = pl.multiple_of(bb * BAND, BAND)
    k0 = pl.multiple_of(jnp.maximum(bb - 1, 0) * BAND, BAND)
    sub0 = (bb - lax.rem(bb, nbl)) * BAND
    qi = r0 + lax.broadcasted_iota(jnp.int32, (BAND, 1), 0)
    ki = k0 + lax.broadcasted_iota(jnp.int32, (1, 2 * BAND), 1)
    diff = qi - ki
    valid = (diff >= 0) & (diff <= BAND) & (ki >= sub0)
    return r0, k0, valid


def _dil_views(ref):
    return [[ref.at[:, pl.ds((3 * role + g) * 128, 128)] for g in range(3)] for role in range(3)]


def _dil_in_specs():
    return [pl.BlockSpec((S, 128), lambda p, k=k: (0, 9 * p + k)) for k in range(9)]


def _dil_fwd(order, qkvb):
    def body(*refs):
        q_refs, k_refs, v_refs = refs[0:3], refs[3:6], refs[6:9]
        ob_ref, lse_ref, qp, kp, vp, op, lp = refs[9:16]
        on, ln = refs[16:19], refs[19:22]
        _, hm = _head_masks()
        for g, r in enumerate(DIL):
            nbl = S // r // BAND
            if r == 1:
                qs_, ks_, vs_, od, ld = q_refs[g], k_refs[g], v_refs[g], on[g], ln[g]
            else:
                _permute_in(qp, q_refs[g], r)
                _permute_in(kp, k_refs[g], r)
                _permute_in(vp, v_refs[g], r)
                qs_, ks_, vs_, od, ld = qp, kp, vp, op, lp

            def blk(bb, c, qs_=qs_, ks_=ks_, vs_=vs_, od=od, ld=ld, nbl=nbl):
                r0, k0, valid = _band_geometry(bb, nbl)
                q = qs_[pl.ds(r0, BAND), :] * 0.125
                kw = ks_[pl.ds(k0, 2 * BAND), :].astype(bf16)
                vw = vs_[pl.ds(k0, 2 * BAND), :]
                o = jnp.zeros((BAND, 128), f32)
                lse = jnp.zeros((BAND, 128), f32)
                for hh in (0, 1):
                    qh = jnp.where(hm[hh], q, 0.0).astype(bf16)
                    s = jnp.where(valid, _dot_nt(qh, kw), NEG)
                    m = jnp.max(s, axis=1, keepdims=True)
                    pr = jnp.exp(s - m)
                    l = jnp.sum(pr, axis=1, keepdims=True)
                    vm = jnp.where(hm[hh], vw, 0.0).astype(bf16)
                    o = o + _dot((pr / l).astype(bf16), vm)
                    lse = jnp.where(hm[hh], m + jnp.log(l), lse)
                od[pl.ds(r0, BAND), :] = o
                ld[pl.ds(r0, BAND), :] = lse
                return c

            lax.fori_loop(0, S // BAND, blk, 0)
            if r != 1:
                _permute_out(on[g], op, r)
                _permute_out(ln[g], lp, r)

        def combine(i, c):
            r0 = pl.multiple_of(i * TQ, TQ)
            ls = [ln[g][pl.ds(r0, TQ), :] for g in range(3)]
            mx = jnp.maximum(jnp.maximum(ls[0], ls[1]), ls[2])
            es = [jnp.exp(l - mx) for l in ls]
            tot = (es[0] + es[1]) + es[2]
            acc = (es[0] / tot) * on[0][pl.ds(r0, TQ), :]
            acc = acc + (es[1] / tot) * on[1][pl.ds(r0, TQ), :]
            acc = acc + (es[2] / tot) * on[2][pl.ds(r0, TQ), :]
            ob_ref[pl.ds(r0, TQ), :] = acc.astype(bf16)
            lse_ref[pl.ds(r0, TQ), :] = mx + jnp.log(tot)
            return c

        lax.fori_loop(0, S // TQ, combine, 0)

    out_blk = pl.BlockSpec((S, 128), lambda p: (0, p))
    return _call(
        order, body, [qkvb] * 9, name="dil_fwd", grid=(2,),
        in_specs=_dil_in_specs(), out_specs=[out_blk, out_blk],
        out_shape=[jax.ShapeDtypeStruct((S, DILOUT), bf16), jax.ShapeDtypeStruct((S, DILOUT), f32)],
        scratch_shapes=[pltpu.VMEM((S, 128), f32)] * 11,
        compiler_params=_params(("parallel",)),
    )


def _branch_mix(order, oa, ob, was, wbs, gates):
    tm = 512

    def body(oa_ref, ob_ref, wa_ref, wb_ref, g_ref, ya_ref, yb_ref, mix_ref):
        oa_b, ob_b = oa_ref[...], ob_ref[...]
        for q in range(NCHIP):
            cols = slice(q * 256, (q + 1) * 256)
            ya = _dot(oa_b, wa_ref[q])
            yb = _dot(ob_b, wb_ref[q])
            ya_ref[:, cols] = ya
            yb_ref[:, cols] = yb
            ga = g_ref[:, q * 256:(q + 1) * 256]
            gb = g_ref[:, D + q * 256:D + (q + 1) * 256]
            mix_ref[:, cols] = (jax.nn.sigmoid(ga) * ya + jax.nn.sigmoid(gb) * yb).astype(bf16)

    row = lambda w: pl.BlockSpec((tm, w), lambda i: (i, 0))
    full3 = lambda a: pl.BlockSpec(a.shape, lambda i: (0, 0, 0))
    return _call(
        order, body, (oa, ob, was, wbs, gates), name="branch_mix", grid=(S // tm,),
        in_specs=[row(FOXW), row(DILOUT), full3(was), full3(wbs), row(2 * D)],
        out_specs=[row(D), row(D), row(D)],
        out_shape=[jax.ShapeDtypeStruct((S, D), f32), jax.ShapeDtypeStruct((S, D), f32),
                   jax.ShapeDtypeStruct((S, D), bf16)],
        compiler_params=_params(("parallel",)),
    )


def _outproj_norm(order, mixed, wout, x, g2):
    tm = 512

    def body(m_ref, w_ref, x_ref, g_ref, x2_ref, h2_ref):
        x2 = x_ref[...] + _dot(m_ref[...], w_ref[...])
        x2_ref[...] = x2
        r = lax.rsqrt(jnp.mean(x2 * x2, axis=-1, keepdims=True) + EPS)
        h2_ref[...] = ((x2 * r) * g_ref[...]).astype(bf16)

    row = pl.BlockSpec((tm, D), lambda i: (i, 0))
    return _call(
        order, body, (mixed, wout, x, g2), name="outproj_norm", grid=(S // tm,),
        in_specs=[row, pl.BlockSpec((D, D), lambda i: (0, 0)), row, pl.BlockSpec((1, D), lambda i: (0, 0))],
        out_specs=[row, row],
        out_shape=[jax.ShapeDtypeStruct((S, D), f32), jax.ShapeDtypeStruct((S, D), bf16)],
        compiler_params=_params(("parallel",)),
    )


def _mlp_up(order, h2, wups):
    tm = 512

    def body(h_ref, w_ref, u_ref, a_ref):
        u = _dot(h_ref[...], w_ref[...])
        u_ref[...] = u
        ru = jnp.maximum(u, 0.0)
        a_ref[...] = (ru * ru).astype(bf16)

    out = pl.BlockSpec((tm, D), lambda q, i: (i, q))
    return _call(
        order, body, (h2, wups), name="mlp_up", grid=(NCHIP, S // tm),
        in_specs=[pl.BlockSpec((tm, D), lambda q, i: (i, 0)), pl.BlockSpec((None, D, D), lambda q, i: (q, 0, 0))],
        out_specs=[out, out],
        out_shape=[jax.ShapeDtypeStruct((S, DFF), f32), jax.ShapeDtypeStruct((S, DFF), bf16)],
        compiler_params=_params(("parallel", "parallel")),
    )


def _mlp_down_loss(order, a, wdown, x2, g3, tgt):
    tm = 256

    def body(a_ref, w_ref, x2_ref, g_ref, t_ref, dx_ref, dxb_ref, dg_ref, loss_ref):
        i = pl.program_id(0)
        x3 = x2_ref[...] + _dot(a_ref[...], w_ref[...])
        r = lax.rsqrt(jnp.mean(x3 * x3, axis=-1, keepdims=True) + EPS)
        xh = x3 * r
        g = g_ref[...]
        e = xh * g - t_ref[...]
        part = 0.5 * jnp.sum(jnp.mean(e * e, axis=-1, keepdims=True), axis=0, keepdims=True)
        dy = e * (1.0 / D)
        gdy = dy * g
        dx = r * (gdy - xh * jnp.mean(gdy * xh, axis=-1, keepdims=True))
        dx_ref[...] = dx
        dxb_ref[...] = dx.astype(bf16)

        @pl.when(i == 0)
        def _():
            dg_ref[...] = jnp.zeros_like(dg_ref)
            loss_ref[...] = jnp.zeros_like(loss_ref)

        dg_ref[...] += jnp.sum(dy * xh, axis=0, keepdims=True)
        loss_ref[...] += jnp.broadcast_to(part, (1, 128))

    row = pl.BlockSpec((tm, D), lambda i: (i, 0))
    vec = pl.BlockSpec((1, D), lambda i: (0, 0))
    return _call(
        order, body, (a, wdown, x2, g3, tgt), name="mlp_down_loss", grid=(S // tm,),
        in_specs=[pl.BlockSpec((tm, DFF), lambda i: (i, 0)), pl.BlockSpec((DFF, D), lambda i: (0, 0)), row, vec, row],
        out_specs=[row, row, vec, pl.BlockSpec((1, 128), lambda i: (0, 0))],
        out_shape=[jax.ShapeDtypeStruct((S, D), f32), jax.ShapeDtypeStruct((S, D), bf16),
                   jax.ShapeDtypeStruct((1, D), f32), jax.ShapeDtypeStruct((1, 128), f32)],
        compiler_params=_params(("arbitrary",)),
    )


def _mlp_down_bwd(order, dx3b, wdown, u):
    tm = 256

    def body(d_ref, w_ref, u_ref, du_ref):
        d = d_ref[...]
        for q in range(NCHIP):
            cols = slice(q * D, (q + 1) * D)
            da = _dot_nt(d, w_ref[cols, :])
            du_ref[:, cols] = (da * (2.0 * jnp.maximum(u_ref[:, cols], 0.0))).astype(bf16)

    return _call(
        order, body, (dx3b, wdown, u), name="mlp_down_bwd", grid=(S // tm,),
        in_specs=[pl.BlockSpec((tm, D), lambda i: (i, 0)), pl.BlockSpec((DFF, D), lambda i: (0, 0)),
                  pl.BlockSpec((tm, DFF), lambda i: (i, 0))],
        out_specs=pl.BlockSpec((tm, DFF), lambda i: (i, 0)),
        out_shape=jax.ShapeDtypeStruct((S, DFF), bf16),
        compiler_params=_params(("parallel",)),
    )


def _mlp_up_bwd(order, du, wups, x2, dx3, g2):
    tm = 256

    def body(du_ref, w_ref, x2_ref, dx3_ref, g_ref, dx2_ref, dx2b_ref, dg_ref):
        i = pl.program_id(0)
        dh = jnp.zeros((tm, D), f32)
        for q in range(NCHIP):
            dh = dh + _dot_nt(du_ref[:, q * D:(q + 1) * D], w_ref[q])
        x2 = x2_ref[...]
        r = lax.rsqrt(jnp.mean(x2 * x2, axis=-1, keepdims=True) + EPS)
        xh = x2 * r
        gdh = dh * g_ref[...]
        dx2 = dx3_ref[...] + r * (gdh - xh * jnp.mean(gdh * xh, axis=-1, keepdims=True))
        dx2_ref[...] = dx2
        dx2b_ref[...] = dx2.astype(bf16)

        @pl.when(i == 0)
        def _():
            dg_ref[...] = jnp.zeros_like(dg_ref)

        dg_ref[...] += jnp.sum(dh * xh, axis=0, keepdims=True)

    row = pl.BlockSpec((tm, D), lambda i: (i, 0))
    vec = pl.BlockSpec((1, D), lambda i: (0, 0))
    return _call(
        order, body, (du, wups, x2, dx3, g2), name="mlp_up_bwd", grid=(S // tm,),
        in_specs=[pl.BlockSpec((tm, DFF), lambda i: (i, 0)), pl.BlockSpec((NCHIP, D, D), lambda i: (0, 0, 0)),
                  row, row, vec],
        out_specs=[row, row, vec],
        out_shape=[jax.ShapeDtypeStruct((S, D), f32), jax.ShapeDtypeStruct((S, D), bf16),
                   jax.ShapeDtypeStruct((1, D), f32)],
        compiler_params=_params(("arbitrary",)),
    )


def _gate_bwd(order, dx2b, wout, gates, ya, yb):
    tm = 256

    def body(d_ref, w_ref, g_ref, ya_ref, yb_ref, dya_ref, dyb_ref, dproj_ref):
        dm = _dot_nt(d_ref[...], w_ref[...])
        sa = jax.nn.sigmoid(g_ref[:, 0:D])
        sb = jax.nn.sigmoid(g_ref[:, D:2 * D])
        dya_ref[...] = (dm * sa).astype(bf16)
        dyb_ref[...] = (dm * sb).astype(bf16)
        dproj_ref[:, 0:D] = (dm * ya_ref[...] * (sa * (1.0 - sa))).astype(bf16)
        dproj_ref[:, D:2 * D] = (dm * yb_ref[...] * (sb * (1.0 - sb))).astype(bf16)

    row = lambda w: pl.BlockSpec((tm, w), lambda i: (i, 0))
    return _call(
        order, body, (dx2b, wout, gates, ya, yb), name="gate_bwd", grid=(S // tm,),
        in_specs=[row(D), pl.BlockSpec((D, D), lambda i: (0, 0)), row(2 * D), row(D), row(D)],
        out_specs=[row(D), row(D), pl.BlockSpec((tm, 2 * D), lambda i: (i, F_G // (2 * D)))],
        out_shape=[jax.ShapeDtypeStruct((S, D), bf16), jax.ShapeDtypeStruct((S, D), bf16),
                   jax.ShapeDtypeStruct((S, NP), bf16)],
        compiler_params=_params(("parallel",)),
    )


def _branch_bwd(order, dya, dyb, was, wbs):
    tm = 512

    def body(dya_ref, dyb_ref, wa_ref, wb_ref, doa_ref, dob_ref):
        doa = jnp.zeros((tm, FOXW), f32)
        dob = jnp.zeros((tm, DILOUT), f32)
        for q in range(NCHIP):
            cols = slice(q * 256, (q + 1) * 256)
            doa = doa + _dot_nt(dya_ref[:, cols], wa_ref[q])
            dob = dob + _dot_nt(dyb_ref[:, cols], wb_ref[q])
        doa_ref[...] = doa.astype(bf16)
        dob_ref[...] = dob

    row = lambda w: pl.BlockSpec((tm, w), lambda i: (i, 0))
    full3 = lambda a: pl.BlockSpec(a.shape, lambda i: (0, 0, 0))
    return _call(
        order, body, (dya, dyb, was, wbs), name="branch_bwd", grid=(S // tm,),
        in_specs=[row(D), row(D), full3(was), full3(wbs)],
        out_specs=[row(FOXW), row(DILOUT)],
        out_shape=[jax.ShapeDtypeStruct((S, FOXW), bf16), jax.ShapeDtypeStruct((S, DILOUT), f32)],
        compiler_params=_params(("parallel",)),
    )


def _branch_wgrad(order, oa, ob, dya, dyb):
    def body(oa_ref, ob_ref, dya_ref, dyb_ref, dwa_ref, dwb_ref):
        dwa_ref[...] = _dot_tn(oa_ref[...], dya_ref[...])
        dwb_ref[...] = _dot_tn(ob_ref[...], dyb_ref[...])

    full = lambda w: pl.BlockSpec((S, w), lambda q: (0, 0))
    colq = pl.BlockSpec((S, 256), lambda q: (0, q))
    return _call(
        order, body, (oa, ob, dya, dyb), name="branch_wgrad", grid=(NCHIP,),
        in_specs=[full(FOXW), full(DILOUT), colq, colq],
        out_specs=[pl.BlockSpec((None, FOXW, 256), lambda q: (q, 0, 0)),
                   pl.BlockSpec((None, DILOUT, 256), lambda q: (q, 0, 0))],
        out_shape=[jax.ShapeDtypeStruct((NCHIP, FOXW, 256), f32), jax.ShapeDtypeStruct((NCHIP, DILOUT, 256), f32)],
        compiler_params=_params(("parallel",)),
    )


def _fox_bwd(order, qkva, doa, oa, lse, F, ftb, dproj):
    nb = S // TQ

    def body(qkv_ref, do_ref, o_ref, lse_ref, F_ref, ftb_ref, _dproj_in, dft_ref, dfq_ref, dqkv_ref, dq_scr):
        p = pl.program_id(0)
        lane, hm = _head_masks()
        sub8 = lax.broadcasted_iota(jnp.int32, (8, 1), 0)
        rowi = lax.broadcasted_iota(jnp.int32, (TQ, 1), 0)
        coli = lax.broadcasted_iota(jnp.int32, (1, TQ), 1)
        dq_scr[...] = jnp.zeros_like(dq_scr)
        dfq_ref[...] = jnp.zeros_like(dfq_ref)

        def kvblock(j, c):
            c0 = pl.multiple_of(j * TQ, TQ)
            k = qkv_ref[pl.ds(c0, TQ), 128:256]
            v = qkv_ref[pl.ds(c0, TQ), 256:384]
            kf = k.astype(f32)
            km = [jnp.where(hm[hh], kf, 0.0).astype(bf16) for hh in (0, 1)]
            frow = ftb_ref[j]
            fr = [jnp.sum(jnp.where(sub8 == 2 * p + hh, frow, 0.0), axis=0, keepdims=True) for hh in (0, 1)]

            def qblock(i, carry):
                dk, dv, df0, df1 = carry
                df = [df0, df1]
                r0 = pl.multiple_of(i * TQ, TQ)
                q = qkv_ref[pl.ds(r0, TQ), 0:128].astype(f32) * 0.125
                do = do_ref[pl.ds(r0, TQ), :].astype(f32)
                prod = do * o_ref[pl.ds(r0, TQ), :].astype(f32)
                lseb = lse_ref[pl.ds(r0, TQ), :]
                Fb = F_ref[pl.ds(r0, TQ), :]
                causal = (c0 + coli) <= (r0 + rowi)
                dqacc = jnp.zeros((TQ, 128), f32)
                rowsum = jnp.zeros((TQ, 128), f32)
                for hh in (0, 1):
                    qh = jnp.where(hm[hh], q, 0.0).astype(bf16)
                    doh = jnp.where(hm[hh], do, 0.0).astype(bf16)
                    delta = jnp.sum(jnp.where(hm[hh], prod, 0.0), axis=1, keepdims=True)
                    fc = jnp.sum(jnp.where(lane == 2 * p + hh, Fb, 0.0), axis=1, keepdims=True)
                    s = _dot_nt(qh, k) + (fc - fr[hh])
                    pr = jnp.where(causal, jnp.exp(s - lseb[:, hh * HD:hh * HD + 1]), 0.0)
                    dp = _dot_nt(doh, v)
                    ds = pr * (dp - delta)
                    dsb = ds.astype(bf16)
                    dv = dv + _dot_tn(pr.astype(bf16), doh)
                    dk = dk + _dot_tn(dsb, qh)
                    dqacc = dqacc + _dot(dsb, km[hh])
                    df[hh] = df[hh] - jnp.sum(ds, axis=0, keepdims=True)
                    rowsum = jnp.where(hm[hh], jnp.sum(ds, axis=1, keepdims=True), rowsum)
                dq_scr[pl.ds(r0, TQ), :] += dqacc * 0.125
                dfq_ref[pl.ds(r0, TQ), :] += rowsum
                return dk, dv, df[0], df[1]

            z = jnp.zeros((TQ, 128), f32)
            zr = jnp.zeros((1, TQ), f32)
            dk, dv, df0, df1 = lax.fori_loop(j, nb, qblock, (z, z, zr, zr))
            dqkv_ref[pl.ds(c0, TQ), 128:256] = dk.astype(bf16)
            dqkv_ref[pl.ds(c0, TQ), 256:384] = dv.astype(bf16)
            dft_ref[j] = jnp.where(sub8 == 0, df0, jnp.where(sub8 == 1, df1, 0.0))
            return c

        lax.fori_loop(0, nb, kvblock, 0)
        dqkv_ref[:, 0:128] = dq_scr[...].astype(bf16)

    pair = pl.BlockSpec((S, 128), lambda p: (0, p))
    return _call(
        order, body, (qkva, doa, oa, lse, F, ftb, dproj), name="fox_bwd", grid=(4,),
        in_specs=[pl.BlockSpec((S, FOX_BLK), lambda p: (0, p)), pair, pair, pair,
                  pl.BlockSpec((S, 128), lambda p: (0, 0)), pl.BlockSpec((nb, 8, TQ), lambda p: (0, 0, 0)),
                  pl.BlockSpec(memory_space=pl.ANY)],
        out_specs=[pl.BlockSpec((None, nb, 8, TQ), lambda p: (p, 0, 0, 0)), pair,
                   pl.BlockSpec((S, FOX_BLK), lambda p: (0, F_FOX // FOX_BLK + p))],
        out_shape=[jax.ShapeDtypeStruct((4, nb, 8, TQ), f32), jax.ShapeDtypeStruct((S, FOXW), f32),
                   jax.ShapeDtypeStruct((S, NP), bf16)],
        input_output_aliases={6: 2},
        scratch_shapes=[pltpu.VMEM((S, 128), f32)],
        compiler_params=_params(("parallel",)),
    )


def _forget_bwd(order, dft, dfq, fa, bpad, dproj):
    nb = S // TQ

    def body(dft_ref, dfq_ref, fa_ref, b_ref, _dproj_in, db_ref, dfa_ref, rows):
        rr = lax.broadcasted_iota(jnp.int32, (TQ, TQ), 0)
        cc = lax.broadcasted_iota(jnp.int32, (TQ, TQ), 1)
        upper = (cc >= rr).astype(bf16)
        ones = jnp.ones((8, TQ), bf16)
        lane = lax.broadcasted_iota(jnp.int32, (1, 128), 1)
        carry = jnp.zeros((1, 128), f32)
        db = jnp.zeros((1, 128), f32)
        rows[...] = jnp.zeros_like(rows)
        for b in reversed(range(nb)):
            for p in range(4):
                rows[2 * p:2 * p + 2, :] = dft_ref[p, b, 0:2, :]
            cols = jnp.zeros((TQ, 128), f32)
            for h in range(8):
                c0 = (h // 2) * 128 + (h % 2) * HD
                cols = jnp.where(lane == h, dfq_ref[b * TQ:(b + 1) * TQ, c0:c0 + 1], cols)
            dlf = carry
            tot = jnp.zeros((8, 128), f32)
            for part in _split3(rows[...]):
                dlf = dlf + _dot_nt(upper, part)
                tot = tot + _dot_nt(ones, part)
            for part in _split3(cols):
                dlf = dlf + _dot(upper, part)
            carry = carry + tot[0:1, :] + jnp.sum(cols, axis=0, keepdims=True)
            z = fa_ref[b * TQ:(b + 1) * TQ, :] + b_ref[...]
            dz = jnp.where(lane < 8, dlf * jax.nn.sigmoid(-z), 0.0)
            dfa_ref[b * TQ:(b + 1) * TQ, 0:128] = dz.astype(bf16)
            dfa_ref[b * TQ:(b + 1) * TQ, 128:256] = jnp.zeros((TQ, 128), bf16)
            db = db + jnp.sum(dz, axis=0, keepdims=True)
        db_ref[...] = db

    whole = lambda a: pl.BlockSpec(a.shape, lambda i: (0,) * a.ndim)
    return _call(
        order, body, (dft, dfq, fa, bpad, dproj), name="forget_bwd", grid=(1,),
        in_specs=[whole(dft), whole(dfq), whole(fa), whole(bpad), pl.BlockSpec(memory_space=pl.ANY)],
        out_specs=[pl.BlockSpec((1, 128), lambda i: (0, 0)), pl.BlockSpec((S, 256), lambda i: (0, F_FA // 256))],
        out_shape=[jax.ShapeDtypeStruct((1, 128), f32), jax.ShapeDtypeStruct((S, NP), bf16)],
        input_output_aliases={4: 1},
        scratch_shapes=[pltpu.VMEM((128, TQ), f32)],
        compiler_params=_params(("arbitrary",)),
    )


def _dil_bwd(order, qkvb, dob, ob, lseb, rope, dproj):
    c_t, s1_t, s2_t = rope

    def body(*refs):
        q_refs, k_refs, v_refs = refs[0:3], refs[3:6], refs[6:9]
        dob_ref, ob_ref, lse_ref, c_ref, s1_ref, s2_ref, _dproj_in, dqkv_ref = refs[9:17]
        qp, kp, vp, dop, lp, dlp, dln, dqp, dkp, dvp, nat = refs[17:28]
        dq_out, dk_out, dv_out = _dil_views(dqkv_ref)
        _, hm = _head_masks()

        def delta_rows(i, c):
            r0 = pl.multiple_of(i * TQ, TQ)
            prod = dob_ref[pl.ds(r0, TQ), :] * ob_ref[pl.ds(r0, TQ), :].astype(f32)
            d0 = jnp.sum(jnp.where(hm[0], prod, 0.0), axis=1, keepdims=True)
            d1 = jnp.sum(jnp.where(hm[1], prod, 0.0), axis=1, keepdims=True)
            dln[pl.ds(r0, TQ), :] = jnp.where(hm[0], d0, d1)
            return c

        lax.fori_loop(0, S // TQ, delta_rows, 0)

        for g, r in enumerate(DIL):
            nbl = S // r // BAND
            if r == 1:
                srcs = (q_refs[g], k_refs[g], v_refs[g], dob_ref, lse_ref, dln)
            else:
                for dst, src in ((qp, q_refs[g]), (kp, k_refs[g]), (vp, v_refs[g]), (dop, dob_ref),
                                 (lp, lse_ref), (dlp, dln)):
                    _permute_in(dst, src, r)
                srcs = (qp, kp, vp, dop, lp, dlp)
            dkp[...] = jnp.zeros_like(dkp)
            dvp[...] = jnp.zeros_like(dvp)

            def blk(bb, c, srcs=srcs, nbl=nbl):
                qs_, ks_, vs_, dos_, ls_, dls_ = srcs
                r0, k0, valid = _band_geometry(bb, nbl)
                q = qs_[pl.ds(r0, BAND), :] * 0.125
                kwf = ks_[pl.ds(k0, 2 * BAND), :]
                kw = kwf.astype(bf16)
                vw = vs_[pl.ds(k0, 2 * BAND), :].astype(bf16)
                do = dos_[pl.ds(r0, BAND), :]
                lse = ls_[pl.ds(r0, BAND), :]
                dlt = dls_[pl.ds(r0, BAND), :]
                dq = jnp.zeros((BAND, 128), f32)
                dk = jnp.zeros((2 * BAND, 128), f32)
                dv = jnp.zeros((2 * BAND, 128), f32)
                for hh in (0, 1):
                    qh = jnp.where(hm[hh], q, 0.0).astype(bf16)
                    doh = jnp.where(hm[hh], do, 0.0).astype(bf16)
                    kh = jnp.where(hm[hh], kwf, 0.0).astype(bf16)
                    s = _dot_nt(qh, kw)
                    pr = jnp.where(valid, jnp.exp(s - lse[:, hh * HD:hh * HD + 1]), 0.0)
                    dp = _dot_nt(doh, vw)
                    ds = pr * (dp - dlt[:, hh * HD:hh * HD + 1])
                    dsb = ds.astype(bf16)
                    dv = dv + _dot_tn(pr.astype(bf16), doh)
                    dk = dk + _dot_tn(dsb, qh)
                    dq = dq + _dot(dsb, kh)
                dqp[pl.ds(r0, BAND), :] = dq * 0.125
                dkp[pl.ds(k0, 2 * BAND), :] += dk
                dvp[pl.ds(k0, 2 * BAND), :] += dv
                return c

            lax.fori_loop(0, S // BAND, blk, 0)

            for acc, out, roped in ((dqp, dq_out[g], True), (dkp, dk_out[g], True), (dvp, dv_out[g], False)):
                if r == 1:
                    src = acc
                else:
                    _permute_out(nat, acc, r)
                    src = nat

                def emit(i, c, src=src, out=out, roped=roped):
                    r0 = pl.multiple_of(i * TQ, TQ)
                    d = src[pl.ds(r0, TQ), :]
                    if roped:
                        d = (d * c_ref[pl.ds(r0, TQ), :] + pltpu.roll(d * s1_ref[pl.ds(r0, TQ), :], 8, 1)
                             + pltpu.roll(d * s2_ref[pl.ds(r0, TQ), :], 120, 1))
                    out[pl.ds(r0, TQ), :] = d.astype(bf16)
                    return c

                lax.fori_loop(0, S // TQ, emit, 0)

    pair = pl.BlockSpec((S, 128), lambda p: (0, p))
    tab = pl.BlockSpec((S, 128), lambda p: (0, 0))
    blk_spec = pl.BlockSpec((S, DIL_BLK), lambda p: (0, p))
    return _call(
        order, body, [qkvb] * 9 + [dob, ob, lseb, c_t, s1_t, s2_t, dproj], name="dil_bwd", grid=(2,),
        in_specs=_dil_in_specs() + [pair, pair, pair, tab, tab, tab, pl.BlockSpec(memory_space=pl.ANY)],
        out_specs=blk_spec,
        out_shape=jax.ShapeDtypeStruct((S, NP), bf16),
        input_output_aliases={15: 0},
        scratch_shapes=[pltpu.VMEM((S, 128), f32)] * 11,
        compiler_params=_params(("parallel",)),
    )


def _inproj_bwd(order, dproj, wt, x, dx2, g1):
    tm = 256

    def body(d_ref, w_ref, x_ref, dx2_ref, g_ref, dx_ref, dg_ref):
        i = pl.program_id(0)
        dh = _dot(d_ref[...], w_ref[...])
        xb = x_ref[...]
        r = lax.rsqrt(jnp.mean(xb * xb, axis=-1, keepdims=True) + EPS)
        xh = xb * r
        gdh = dh * g_ref[...]
        dx_ref[...] = dx2_ref[...] + r * (gdh - xh * jnp.mean(gdh * xh, axis=-1, keepdims=True))

        @pl.when(i == 0)
        def _():
            dg_ref[...] = jnp.zeros_like(dg_ref)

        dg_ref[...] += jnp.sum(dh * xh, axis=0, keepdims=True)

    row = pl.BlockSpec((tm, D), lambda i: (i, 0))
    vec = pl.BlockSpec((1, D), lambda i: (0, 0))
    return _call(
        order, body, (dproj, wt, x, dx2, g1), name="inproj_bwd", grid=(S // tm,),
        in_specs=[pl.BlockSpec((tm, NP), lambda i: (i, 0)), pl.BlockSpec((NP, D), lambda i: (0, 0)), row, row, vec],
        out_specs=[row, vec],
        out_shape=[jax.ShapeDtypeStruct((S, D), f32), jax.ShapeDtypeStruct((1, D), f32)],
        compiler_params=_params(("arbitrary",)),
    )


HBM = pl.BlockSpec(memory_space=pltpu.HBM)
SEM = pl.BlockSpec(memory_space=pltpu.SEMAPHORE)
SMALL_ROWS = 8


def _comm_call(name, body, bufs, order, sems_in=(), new_sems=()):
    nb, ns, nn = len(bufs), len(sems_in), len(new_sems)
    extra = order.token_for(bufs)

    def kern(*refs):
        off = nb + ns + len(extra)
        body(refs[:nb], refs[nb:nb + ns], refs[off:off + nn])
        refs[-1][...] = jnp.zeros((8, 128), f32)

    res = pl.pallas_call(
        kern, name=name,
        in_specs=[HBM] * nb + [SEM] * ns + [pl.BlockSpec(memory_space=pl.ANY)] * len(extra),
        out_specs=[SEM] * nn + [HBM] * nb + [pl.BlockSpec(memory_space=pltpu.VMEM)],
        out_shape=[pltpu.SemaphoreType.DMA((k,)) for k in new_sems] + [pltpu.HBM(b.shape, b.dtype) for b in bufs]
        + [jax.ShapeDtypeStruct((8, 128), f32)],
        input_output_aliases={i: nn + i for i in range(nb)},
        compiler_params=pltpu.CompilerParams(has_side_effects=pltpu.SideEffectType.DATAFLOW_SIDE_EFFECTING),
    )(*[pltpu.with_memory_space_constraint(b, pltpu.HBM) for b in bufs], *sems_in, *extra)
    order.mark(res[-1])
    return list(res[:nn]), list(res[nn:nn + nb])


def _place():
    x, y, c = lax.axis_index("x"), lax.axis_index("y"), lax.axis_index("c")
    chips = [(1 - x, y), (x, 1 - y), (1 - x, 1 - y)]
    return x, y, c, chips


def _rcopy(src, dst, ssem, rsem, dev):
    return pltpu.make_async_remote_copy(src_ref=src, dst_ref=dst, send_sem=ssem, recv_sem=rsem,
                                        device_id=dev, device_id_type=pl.DeviceIdType.MESH)


def _half(nrows, which):
    return pl.ds(which * (nrows // 2), nrows // 2)


def _ici_copies(stack, group_sizes, ssems, rsems):
    x, y, c, chips = _place()
    me_q = 2 * x + y
    sends, recvs = [], []
    a = 0
    for grp, size in enumerate(group_sizes):
        for k in range(size):
            rows = _half(stack[a].shape[1], c)
            for j, (cx, cy) in enumerate(chips):
                mine = stack[a].at[me_q, rows]
                sends.append(_rcopy(mine, mine, ssems[grp].at[k * 3 + j], rsems[grp].at[k * 3 + j], (cx, cy, c)))
                theirs = stack[a].at[2 * cx + cy, rows]
                recvs.append(_rcopy(theirs, theirs, ssems[grp].at[k * 3 + j], rsems[grp].at[k * 3 + j],
                                    (cx, cy, c)))
            a += 1
    return sends, recvs


def _allgather_start(stacks, group_sizes, order):
    def body(bufs, _, new):
        sends, _r = _ici_copies(bufs, group_sizes, new[0::2], new[1::2])
        for cp in sends:
            cp.start()

    sizes = []
    for size in group_sizes:
        sizes += [3 * size, 3 * size]
    sems, stacks = _comm_call("allgather_start", body, stacks, order, new_sems=sizes)
    return [(sems[2 * g], sems[2 * g + 1]) for g in range(len(group_sizes))], stacks


def _forward_copies(stack, ssem, rsem):
    x, y, c, chips = _place()
    sib = (x, y, 1 - c)
    sends, recvs = [], []
    for a in range(len(stack)):
        for j, (cx, cy) in enumerate(chips):
            landed = stack[a].at[2 * cx + cy, _half(stack[a].shape[1], c)]
            sends.append(_rcopy(landed, landed, ssem.at[a * 3 + j], rsem.at[a * 3 + j], sib))
            other = stack[a].at[2 * cx + cy, _half(stack[a].shape[1], 1 - c)]
            recvs.append(_rcopy(other, other, ssem.at[a * 3 + j], rsem.at[a * 3 + j], sib))
    return sends, recvs


def _allgather_forward(name, stacks, sems, order):
    n = len(stacks)

    def body(bufs, taken, new):
        sends, recvs = _ici_copies(bufs, [n], [taken[0]], [taken[1]])
        for cp in sends:
            cp.wait_send()
        for cp in recvs:
            cp.wait_recv()
        fwd, _r = _forward_copies(bufs, new[0], new[1])
        for cp in fwd:
            cp.start()

    return _comm_call(name, body, stacks, order, sems_in=sems, new_sems=(3 * n, 3 * n))


def _allgather_finish(name, stacks, sems, order):
    def body(bufs, taken, _):
        sends, recvs = _forward_copies(bufs, taken[0], taken[1])
        for cp in sends:
            cp.wait_send()
        for cp in recvs:
            cp.wait_recv()

    return _comm_call(name, body, stacks, order, sems_in=sems)[1]


def _window_unit(q, j):
    return C2I[WIN_UNIT0[q] + j]


def _pair_copies(g, t, ssem, rsem, gathered):
    x, y, c, _ = _place()
    sib = (x, y, 1 - c)
    cps, whole = [], []
    for a in range(len(g)):
        if a == 0 and gathered:
            for q in range(NCHIP):
                for j in range(WIN_UNITS // 2):
                    u = jnp.where(c == 0, _window_unit(q, WIN_UNITS // 2 + j), _window_unit(q, j))
                    src = g[0].at[pl.ds(pl.multiple_of(u * UNIT, UNIT), UNIT), :]
                    cps.append(_rcopy(src, t[0].at[q, pl.ds(j * UNIT, UNIT), :], ssem.at[0], rsem.at[0], sib))
            whole.append(_rcopy(t[0], t[0], ssem.at[0], rsem.at[0], sib))
        else:
            cp = _rcopy(g[a].at[:, _half(g[a].shape[1], 1 - c), :], t[a], ssem.at[a], rsem.at[a], sib)
            cps.append(cp)
            whole.append(cp)
    return cps, whole


def _pair_start(name, gs, order, gathered=False):
    n = len(gs)
    ts = [lax.empty((NCHIP, WIN_ROWS // 2, D) if (a == 0 and gathered) else (NCHIP, g.shape[1] // 2, g.shape[2]), f32)
          for a, g in enumerate(gs)]

    def body(bufs, _, new):
        for cp in _pair_copies(bufs[:n], bufs[n:], new[0], new[1], gathered)[0]:
            cp.start()

    return _comm_call(name, body, list(gs) + ts, order, new_sems=(n, n))


def _pair_wait(name, bufs, sems, order, gathered=False):
    n = len(bufs) // 2

    def body(refs, taken, _):
        for cp in _pair_copies(refs[:n], refs[n:], taken[0], taken[1], gathered)[1]:
            cp.wait_send()
            cp.wait_recv()

    bufs = _comm_call(name, body, bufs, order, sems_in=sems)[1]
    return bufs[:n], bufs[n:]


def _row_tile(h):
    return min(h, 256)


def _pair_add(order, g, t, c_arr, name):
    _, R, C = g.shape
    h = R // 2
    tr = _row_tile(h)
    nblk = h // tr

    def body(c_ref, g_ref, t_ref, p32_ref, p16_ref):
        s = g_ref[...] + t_ref[...]
        p32_ref[...] = s
        p16_ref[...] = s.astype(bf16)

    blk = pl.BlockSpec((None, tr, C), lambda q, i, c_ref: (q, i, 0))
    return _call_indexed(
        order, body, (c_arr,), (g, t), (NCHIP, nblk),
        [pl.BlockSpec((None, tr, C), lambda q, i, c_ref: (q, c_ref[0] * nblk + i, 0)), blk], [blk, blk],
        name=name,
        out_shape=[jax.ShapeDtypeStruct((NCHIP, h, C), f32), jax.ShapeDtypeStruct((NCHIP, h, C), bf16)],
        compiler_params=_params(("parallel", "parallel")),
    )


def _pair_add_gathered(order, dwt, t, c_arr, name):
    half_units = WIN_UNITS // 2
    table = jnp.asarray([_window_unit(q, j) for q in range(NCHIP) for j in range(WIN_UNITS)], jnp.int32)

    def body(tab_ref, c_ref, g_ref, t_ref, p32_ref, p16_ref):
        s = g_ref[...] + t_ref[...]
        p32_ref[...] = s
        p16_ref[...] = s.astype(bf16)

    blk = pl.BlockSpec((None, UNIT, D), lambda q, j, tab_ref, c_ref: (q, j, 0))
    return _call_indexed(
        order, body, (table, c_arr), (dwt, t), (NCHIP, half_units),
        [pl.BlockSpec((UNIT, D), lambda q, j, tab_ref, c_ref: (tab_ref[q * WIN_UNITS + c_ref[0] * half_units + j], 0)),
         blk], [blk, blk],
        name=name,
        out_shape=[jax.ShapeDtypeStruct((NCHIP, WIN_ROWS // 2, D), f32),
                   jax.ShapeDtypeStruct((NCHIP, WIN_ROWS // 2, D), bf16)],
        compiler_params=_params(("parallel", "parallel")),
    )


def _shard_copies(p, r, sm, ssem, rsem):
    x, y, c, chips = _place()
    n = len(p)
    sends, recvs = [], []
    for a in range(n):
        for j, (cx, cy) in enumerate(chips):
            k = a * 3 + j
            sends.append(_rcopy(p[a].at[2 * cx + cy], r[a].at[j], ssem.at[k], rsem.at[k], (cx, cy, c)))
            recvs.append(_rcopy(r[a].at[j], r[a].at[j], ssem.at[k], rsem.at[k], (cx, cy, c)))
    if sm is not None:
        mine = sm.at[4 * x + 2 * y + c]
        for i in range(1, 8):
            px = (1 - x) if i & 4 else x
            py = (1 - y) if i & 2 else y
            pc = (1 - c) if i & 1 else c
            k = 3 * n + i - 1
            sends.append(_rcopy(mine, mine, ssem.at[k], rsem.at[k], (px, py, pc)))
            slot = sm.at[4 * px + 2 * py + pc]
            recvs.append(_rcopy(slot, slot, ssem.at[k], rsem.at[k], (px, py, pc)))
    return sends, recvs


def _shard_start(name, p16s, order, sm=None):
    n = len(p16s)
    rs = [lax.empty((3,) + p.shape[1:], bf16) for p in p16s]
    extra = [] if sm is None else [sm]
    nsem = 3 * n + (7 if sm is not None else 0)

    def body(bufs, _, new):
        sends, _r = _shard_copies(bufs[:n], bufs[n:2 * n], bufs[2 * n] if extra else None, new[0], new[1])
        for cp in sends:
            cp.start()

    return _comm_call(name, body, list(p16s) + rs + extra, order, new_sems=(nsem, nsem))


def _shard_wait(name, bufs, sems, n, order):
    has_sm = len(bufs) > 2 * n

    def body(refs, taken, _):
        sends, recvs = _shard_copies(refs[:n], refs[n:2 * n], refs[2 * n] if has_sm else None, taken[0], taken[1])
        for cp in sends:
            cp.wait_send()
        for cp in recvs:
            cp.wait_recv()

    bufs = _comm_call(name, body, bufs, order, sems_in=sems)[1]
    return bufs[n:2 * n], (bufs[2 * n] if has_sm else None)


def _shard_sum(order, p32, r, q_arr, c_arr, name):
    _, h, C = p32.shape
    tr = _row_tile(h)
    nblk = h // tr

    def body(q_ref, c_ref, p_ref, r_ref, o_ref):
        s = p_ref[...]
        for j in range(3):
            s = s + r_ref[j].astype(f32)
        o_ref[...] = s

    return _call_indexed(
        order, body, (q_arr, c_arr), (p32, r), (nblk,),
        [pl.BlockSpec((None, tr, C), lambda i, q_ref, c_ref: (q_ref[0], i, 0)),
         pl.BlockSpec((3, tr, C), lambda i, q_ref, c_ref: (0, i, 0))],
        pl.BlockSpec((tr, C), lambda i, q_ref, c_ref: (c_ref[0] * nblk + i, 0)),
        name=name, out_shape=jax.ShapeDtypeStruct((2 * h, C), f32),
        compiler_params=_params(("parallel",)),
    )


def _swap_copies(full, ssem, rsem):
    x, y, c, _ = _place()
    sends, recvs = [], []
    for a in range(len(full)):
        mine = full[a].at[_half(full[a].shape[0], c)]
        sends.append(_rcopy(mine, mine, ssem.at[a], rsem.at[a], (x, y, 1 - c)))
        other = full[a].at[_half(full[a].shape[0], 1 - c)]
        recvs.append(_rcopy(other, other, ssem.at[a], rsem.at[a], (x, y, 1 - c)))
    return sends, recvs


def _swap_start(name, fulls, order):
    n = len(fulls)

    def body(bufs, _, new):
        for cp in _swap_copies(bufs, new[0], new[1])[0]:
            cp.start()

    return _comm_call(name, body, list(fulls), order, new_sems=(n, n))


def _swap_wait(name, fulls, sems, order):
    def body(refs, taken, _):
        sends, recvs = _swap_copies(refs, taken[0], taken[1])
        for cp in sends:
            cp.wait_send()
        for cp in recvs:
            cp.wait_recv()

    return _comm_call(name, body, fulls, order, sems_in=sems)[1]


def _small_sum(order, sm):
    def body(sm_ref, o_ref):
        s = sm_ref[0]
        for d in range(1, 8):
            s = s + sm_ref[d]
        o_ref[...] = s

    return _call(order, body, (sm,), name="small_grad_sum", out_shape=jax.ShapeDtypeStruct((SMALL_ROWS, D), f32))


def _adamw(order, w, g, m, v, name):
    R, C = w.shape
    tr = R if R <= 256 else next(t for t in (256, 1072) if R % t == 0)

    def body(w_ref, g_ref, m_ref, v_ref, d_ref, nm_ref, nv_ref):
        g_ = g_ref[...]
        m_ = ADAM_B1 * m_ref[...] + (1.0 - ADAM_B1) * g_
        v_ = ADAM_B2 * v_ref[...] + (1.0 - ADAM_B2) * (g_ * g_)
        m_hat = m_ / (1.0 - ADAM_B1 ** ADAM_STEP)
        v_hat = v_ / (1.0 - ADAM_B2 ** ADAM_STEP)
        d_ref[...] = -ADAM_LR * (m_hat / (jnp.sqrt(v_hat) + ADAM_EPS) + ADAM_WD * w_ref[...])
        nm_ref[...] = m_
        nv_ref[...] = v_

    assert R % tr == 0
    blk = pl.BlockSpec((tr, C), lambda i: (i, 0))
    return _call(
        order, body, (w, g, m, v), name=name, grid=(R // tr,), in_specs=[blk] * 4, out_specs=[blk] * 3,
        out_shape=[jax.ShapeDtypeStruct((R, C), f32)] * 3,
        compiler_params=_params(("parallel",)),
    )


def _feature_major(w):
    return jnp.transpose(w, (2, 0, 1)).reshape(SHARD_IN, D)


def _flat(wt):
    return wt.reshape(SHARD_IN * 8, 128)


def _unflat(a):
    return jnp.transpose(a.reshape(SHARD_IN, 1, D), (1, 2, 0))


def _window_of(wt, q):
    wb = wt.astype(bf16)
    off = jnp.asarray(OWN_ROW0, jnp.int32)[q]
    plain = lax.dynamic_update_slice(jnp.zeros((WIN_ROWS, D), bf16), wb, (off, 0))
    lo = jnp.pad(wb[0:62], ((2, WIN_ROWS - 64), (0, 0)))
    hi = jnp.pad(wb[70:SHARD_IN], ((64, WIN_ROWS - 64 - (SHARD_IN - 70)), (0, 0)))
    win = jnp.where(q == 1, lo + hi, plain)
    fa = jnp.pad(wb[62:70], ((0, FA_ROWS - 8), (0, 0)))
    return win, fa


def _own_rows(gwin, gfa, q):
    off = jnp.asarray(OWN_ROW0, jnp.int32)[q]
    plain = lax.dynamic_slice(gwin, (off, 0), (SHARD_IN, D))
    chip1 = (jnp.pad(gwin[2:64], ((0, SHARD_IN - 62), (0, 0))) + jnp.pad(gfa[0:8], ((62, SHARD_IN - 70), (0, 0)))
             + jnp.pad(gwin[64:64 + SHARD_IN - 70], ((70, 0), (0, 0))))
    return jnp.where(q == 1, chip1, plain)


def kernel(x, norm_attn_g, w_in, b_forget, w_branch_a, w_branch_b, w_out, norm_mlp_g, w_up, w_down, norm_final_g, loss_target, m_norm_attn_g, m_w_in, m_b_forget, m_w_branch_a, m_w_branch_b, m_w_out, m_norm_mlp_g, m_w_up, m_w_down, m_norm_final_g, v_norm_attn_g, v_w_in, v_b_forget, v_w_branch_a, v_w_branch_b, v_w_out, v_norm_mlp_g, v_w_up, v_w_down, v_norm_final_g):
    xi, yi, ci = lax.axis_index("x"), lax.axis_index("y"), lax.axis_index("c")
    q_me = 2 * xi + yi
    c_arr = jnp.reshape(ci, (1,)).astype(jnp.int32)
    q_arr = jnp.reshape(q_me, (1,)).astype(jnp.int32)
    x_, tgt = x[0], loss_target[0]

    names = ["w_branch_a", "w_branch_b", "w_out", "w_up", "w_down"]
    big = dict(zip(names, [w_branch_a[0], w_branch_b[0], w_out[0], w_up[0], w_down[0]]))
    ms = dict(zip(names, [m_w_branch_a[0], m_w_branch_b[0], m_w_out[0], m_w_up[0], m_w_down[0]]))
    vs = dict(zip(names, [v_w_branch_a[0], v_w_branch_b[0], v_w_out[0], v_w_up[0], v_w_down[0]]))
    grad, upd = {}, {}
    order = _Order()

    def run(fn, *args, **kw):
        return fn(order, *args, **kw)

    def own_slot(a):
        return lax.dynamic_update_slice(lax.empty((NCHIP,) + a.shape, a.dtype), a[None], (q_me, 0, 0))

    wt_own = _feature_major(w_in)
    win, fa_blk = _window_of(wt_own, q_me)
    stacks = [own_slot(win), own_slot(fa_blk)] + [own_slot(w.astype(bf16)) for w in big.values()]
    (sem_in, sem_rest), stacks = _allgather_start(stacks, [2, 5], order)
    sem_f, in_s = _allgather_forward("allgather_forward_in", stacks[0:2], sem_in, order)
    wins, fas = _allgather_finish("allgather_finish_in", in_s, sem_f, order)
    wt = run(_assemble_win, wins, fas)

    rope = _rope_tables()
    bpad = jnp.pad(b_forget, ((0, 0), (0, 120)))
    h1, qkvb, qkva, gates, fa = run(_norm_inproj, x_, norm_attn_g, wt, rope)
    F, ftb = run(_forget_cumsum, fa, bpad)
    oa, lsea = run(_fox_fwd, qkva, F, ftb)
    sem_f, rest = _allgather_forward("allgather_forward_rest", stacks[2:], sem_rest, order)
    ob, lseb = run(_dil_fwd, qkvb)
    was, wbs, wouts, wups, wdowns = _allgather_finish("allgather_finish_rest", rest, sem_f, order)
    wout = wouts.reshape(D, D)
    wdown = wdowns.reshape(DFF, D)
    ya, yb, mixed = run(_branch_mix, oa, ob, was, wbs, gates)
    x2, h2 = run(_outproj_norm, mixed, wout, x_, norm_mlp_g)
    u, a = run(_mlp_up, h2, wups)
    dx3, dx3b, dg3, loss_part = run(_mlp_down_loss, a, wdown, x2, norm_final_g.reshape(1, D), tgt)
    loss = lax.psum(loss_part[0, 0], ("x", "y", "c"))

    def reduce_to_shard(tag, group, p32s, bufs, sems):
        rs, _ = _shard_wait("shard_wait_" + tag, bufs, sems, len(group), order)
        fulls = [run(_shard_sum, p32s[i], rs[i], q_arr, c_arr, "shard_sum_" + nm) for i, nm in enumerate(group)]
        return _swap_start("swap_start_" + tag, fulls, order)

    def pair_sums(tag, group, bufs, sems):
        gs, ts = _pair_wait("pair_wait_" + tag, bufs, sems, order)
        return zip(*[run(_pair_add, gs[i], ts[i], c_arr, "pair_add_" + nm) for i, nm in enumerate(group)])

    def finish(tag, group, fulls, sems):
        fulls = _swap_wait("swap_wait_" + tag, fulls, sems, order)
        for nm, gfull in zip(group, fulls):
            grad[nm] = gfull
            upd[nm] = run(_adamw, big[nm], gfull, ms[nm], vs[nm], "adamw_" + nm)

    grp_a, grp_b, grp_c = ["w_down", "w_up"], ["w_out", "w_branch_a", "w_branch_b"], ["w_in", "w_in_fa"]
    du = run(_mlp_down_bwd, dx3b, wdown, u)
    dwdown = run(_mm, a, dx3b, "tn", f32, 1024, D, "wgrad_down")
    dwup = run(_mm, h2, du, "tn", f32, D, 1024, "wgrad_up", stack_cols=True)
    sem_pa, buf_pa = _pair_start("pair_start_a", [dwdown.reshape(NCHIP, DFF // NCHIP, D), dwup], order)
    dx2, dx2b, dg2 = run(_mlp_up_bwd, du, wups, x2, dx3, norm_mlp_g)
    p32_a, p16_a = pair_sums("a", grp_a, buf_pa, sem_pa)
    sem_sa, buf_sa = _shard_start("shard_start_a", p16_a, order)
    dya, dyb, dproj = run(_gate_bwd, dx2b, wout, gates, ya, yb)
    dwout = run(_mm, mixed, dx2b, "tn", f32, D, D, "wgrad_out")
    doa, dob = run(_branch_bwd, dya, dyb, was, wbs)
    dwas, dwbs = run(_branch_wgrad, oa, ob, dya, dyb)
    sem_pb, buf_pb = _pair_start("pair_start_b", [dwout.reshape(NCHIP, D // NCHIP, D), dwas, dwbs], order)
    dft, dfq, dproj = run(_fox_bwd, qkva, doa, oa, lsea, F, ftb, dproj)
    p32_b, p16_b = pair_sums("b", grp_b, buf_pb, sem_pb)
    sem_wa, fulls_a = reduce_to_shard("a", grp_a, p32_a, buf_sa, sem_sa)
    sem_sb, buf_sb = _shard_start("shard_start_b", p16_b, order)
    dbf, dproj = run(_forget_bwd, dft, dfq, fa, bpad, dproj)
    dproj = run(_dil_bwd, qkvb, dob, ob, lseb, rope, dproj)
    sem_wb, fulls_b = reduce_to_shard("b", grp_b, p32_b, buf_sb, sem_sb)
    finish("a", grp_a, fulls_a, sem_wa)
    dwt = run(_mm, dproj, h1, "tn", f32, 512, D, "wgrad_in")
    dwfa = jnp.broadcast_to(dwt[F_FA:F_FA + FA_ROWS][None], (NCHIP, FA_ROWS, D))
    sem_pc, buf_pc = _pair_start("pair_start_c", [dwt, dwfa], order, gathered=True)
    finish("b", grp_b, fulls_b, sem_wb)
    (dwt_c, dwfa_c), (t_in, t_fa) = _pair_wait("pair_wait_c", buf_pc, sem_pc, order, gathered=True)
    p32_in, p16_in = run(_pair_add_gathered, dwt_c, t_in, c_arr, "pair_add_w_in")
    p32_fa, p16_fa = run(_pair_add, dwfa_c, t_fa, c_arr, "pair_add_w_in_fa")
    sem_sc, buf_sc = _shard_start("shard_start_c", [p16_in, p16_fa], order)
    gx, dg1 = run(_inproj_bwd, dproj, wt, x_, dx2, norm_attn_g)
    small = jnp.concatenate([dg1, dg2, dg3, jnp.pad(dbf[:, 0:8], ((0, 0), (0, D - 8))),
                             jnp.zeros((SMALL_ROWS - 4, D), f32)], axis=0)
    sm = lax.dynamic_update_slice(lax.empty((8, SMALL_ROWS, D), f32), small[None],
                                  (4 * xi + 2 * yi + ci, 0, 0))
    sem_sm, buf_sm = _shard_start("small_start", [], order, sm)
    sem_wc, fulls_c = reduce_to_shard("c", grp_c, [p32_in, p32_fa], buf_sc, sem_sc)
    _, sm = _shard_wait("small_wait", buf_sm, sem_sm, 0, order)
    gsmall = run(_small_sum, sm)

    grad["norm_attn_g"], grad["norm_mlp_g"] = gsmall[0:1], gsmall[1:2]
    grad["norm_final_g"], grad["b_forget"] = gsmall[2:3], gsmall[3:4, 0:8]
    upd["norm_attn_g"] = run(_adamw, norm_attn_g, grad["norm_attn_g"], m_norm_attn_g, v_norm_attn_g, "adamw_g1")
    upd["norm_mlp_g"] = run(_adamw, norm_mlp_g, grad["norm_mlp_g"], m_norm_mlp_g, v_norm_mlp_g, "adamw_g2")
    upd["norm_final_g"] = run(_adamw, norm_final_g.reshape(1, D), grad["norm_final_g"],
                              m_norm_final_g.reshape(1, D), v_norm_final_g.reshape(1, D), "adamw_g3")
    upd["b_forget"] = run(_adamw, b_forget, grad["b_forget"], m_b_forget, v_b_forget, "adamw_bf")

    gwin, gfa = _swap_wait("swap_wait_c", fulls_c, sem_wc, order)
    g_in = _flat(_own_rows(gwin, gfa, q_me))
    upd_in = run(_adamw, _flat(wt_own), g_in, _flat(_feature_major(m_w_in)), _flat(_feature_major(v_w_in)),
                 "adamw_w_in")
    grad["w_in"] = _unflat(g_in)
    upd["w_in"] = [_unflat(t) for t in upd_in]

    order_out = ["norm_attn_g", "w_in", "b_forget", "w_branch_a", "w_branch_b", "w_out", "norm_mlp_g", "w_up",
                 "w_down", "norm_final_g"]
    shapes = dict(norm_attn_g=norm_attn_g.shape, w_in=w_in.shape, b_forget=b_forget.shape,
                  w_branch_a=w_branch_a.shape, w_branch_b=w_branch_b.shape, w_out=w_out.shape,
                  norm_mlp_g=norm_mlp_g.shape, w_up=w_up.shape, w_down=w_down.shape, norm_final_g=norm_final_g.shape)
    outs = [loss, gx.reshape(x.shape)]
    outs += [grad[nm].reshape(shapes[nm]) for nm in order_out]
    for k in range(3):
        outs += [upd[nm][k].reshape(shapes[nm]) for nm in order_out]
    return tuple(outs)
```

```python
import jax
import jax.numpy as jnp
from jax import lax
from jax.experimental import pallas as pl
from jax.experimental.pallas import tpu as pltpu

f32 = jnp.float32
bf16 = jnp.bfloat16

S = 2048
D = 1024
DFF = 4096
HD = 64
FOXW = 512
DILOUT = 256
DIL = (1, 4, 16)
BAND = 128
EPS = 1e-6
NEG = -1e30
ROPE_THETA = 500000.0
NCHIP = 4
TQ = 256

ADAM_LR, ADAM_B1, ADAM_B2, ADAM_EPS, ADAM_WD, ADAM_STEP = 0.001, 0.9, 0.999, 1e-08, 0.01, 10
VMEM_LIMIT = 56 * 1024 * 1024

UNIT = 64
NP = 6144
F_DIL, F_FOX, F_FA, F_G = 0, 2304, 3840, 4096
DIL_BLK, FOX_BLK = 1152, 384
WIN_UNITS, WIN_ROWS = 24, 1536
WIN_UNIT0 = (0, 23, 45, 68)
OWN_ROW0 = (0, 2, 60, 62)
SHARD_IN = 1474
FA_ROWS = 32


def _compact_to_internal():
    c2i = {}
    for p in range(2):
        for role in range(3):
            for g in range(3):
                for hh in range(2):
                    c2i[24 + 12 * role + 4 * g + 2 * p + hh] = 18 * p + 6 * role + 2 * g + hh
    for p in range(4):
        for role in range(3):
            for hh in range(2):
                c2i[8 * role + 2 * p + hh] = F_FOX // UNIT + 6 * p + 2 * role + hh
    for j in range(32):
        c2i[60 + j] = F_G // UNIT + j
    return c2i


C2I = _compact_to_internal()
OVERLAP_UNITS = (23, 45, 46, 68)


def _params(sem=None):
    return pltpu.CompilerParams(dimension_semantics=sem, vmem_limit_bytes=VMEM_LIMIT)


class _Order:
    def __init__(self):
        self.tok = None

    def mark(self, v):
        self.tok = v

    def token_for(self, args):
        return [] if self.tok is None or any(self.tok is a for a in args) else [self.tok]


def _call(order, body, args, in_specs=None, **kw):
    args = list(args)
    n_in = len(args)
    if in_specs is None:
        in_specs = [pl.BlockSpec(memory_space=pltpu.VMEM)] * n_in
    kern = body
    extra = order.token_for(args)
    if extra:
        in_specs = list(in_specs) + [pl.BlockSpec(memory_space=pl.ANY)]

        def kern(*refs):
            body(*refs[:n_in], *refs[n_in + 1:])

    out = pl.pallas_call(kern, in_specs=in_specs, **kw)(*args, *extra)
    order.mark(out[0] if isinstance(out, (tuple, list)) else out)
    return out


def _call_indexed(order, body, scalars, args, grid, in_specs, out_specs, **kw):
    args, in_specs = list(args), list(in_specs)
    n_front = len(scalars) + len(args)
    kern = body
    extra = order.token_for(args)
    if extra:
        in_specs.append(pl.BlockSpec(memory_space=pl.ANY))

        def kern(*refs):
            body(*refs[:n_front], *refs[n_front + 1:])

    out = pl.pallas_call(
        kern, grid_spec=pltpu.PrefetchScalarGridSpec(num_scalar_prefetch=len(scalars), grid=grid, in_specs=in_specs,
                                                     out_specs=out_specs), **kw)(*scalars, *args, *extra)
    order.mark(out[0] if isinstance(out, (tuple, list)) else out)
    return out


def _dot(a, b):
    return jnp.dot(a, b, preferred_element_type=f32)


def _dot_nt(a, b):
    return lax.dot_general(a, b, (((1,), (1,)), ((), ())), preferred_element_type=f32)


def _dot_tn(a, b):
    return lax.dot_general(a, b, (((0,), (0,)), ((), ())), preferred_element_type=f32)


def _split3(x):
    hi = x.astype(bf16)
    r1 = x - hi.astype(f32)
    mid = r1.astype(bf16)
    lo = (r1 - mid.astype(f32)).astype(bf16)
    return hi, mid, lo


def _rope_tables():
    half = 8
    inv_freq = jnp.power(jnp.float32(ROPE_THETA), -jnp.arange(half, dtype=f32) * 2.0 / 16)
    ang = jnp.arange(S).astype(f32)[:, None] * inv_freq[None, :]
    cos, sin = jnp.cos(ang), jnp.sin(ang)
    one = jnp.ones((S, HD - 16), f32)
    zero = jnp.zeros((S, HD - 16), f32)
    z8 = jnp.zeros((S, 8), f32)
    c = jnp.concatenate([cos, cos, one], axis=1)
    s1 = jnp.concatenate([-sin, z8, zero], axis=1)
    s2 = jnp.concatenate([z8, sin, zero], axis=1)
    return tuple(jnp.concatenate([t, t], axis=1) for t in (c, s1, s2))


def _mm(order, a, b, mode, out_dtype, tm, tn, name, stack_cols=False):
    if mode == "nn":
        (M, K), (_, N) = a.shape, b.shape
        a_spec = pl.BlockSpec((tm, K), lambda i, j: (i, 0))
        b_spec = pl.BlockSpec((K, tn), lambda i, j: (0, j))
        dot = _dot
    elif mode == "nt":
        (M, K), (N, _) = a.shape, b.shape
        a_spec = pl.BlockSpec((tm, K), lambda i, j: (i, 0))
        b_spec = pl.BlockSpec((tn, K), lambda i, j: (j, 0))
        dot = _dot_nt
    else:
        (K, M), (_, N) = a.shape, b.shape
        a_spec = pl.BlockSpec((K, tm), lambda i, j: (0, i))
        b_spec = pl.BlockSpec((K, tn), lambda i, j: (0, j))
        dot = _dot_tn

    def body(a_ref, b_ref, o_ref):
        o_ref[...] = dot(a_ref[...], b_ref[...]).astype(out_dtype)

    if stack_cols:
        assert tm == M
        out_spec = pl.BlockSpec((None, tm, tn), lambda i, j: (j, 0, 0))
        out_shape = jax.ShapeDtypeStruct((N // tn, M, tn), out_dtype)
    else:
        out_spec = pl.BlockSpec((tm, tn), lambda i, j: (i, j))
        out_shape = jax.ShapeDtypeStruct((M, N), out_dtype)
    return _call(
        order, body, (a, b), name=name, grid=(M // tm, N // tn), in_specs=[a_spec, b_spec],
        out_specs=out_spec, out_shape=out_shape,
        compiler_params=_params(("parallel", "parallel")),
    )


def _assemble_win(order, wins, fas):
    def body(win_ref, fa_ref, o_ref):
        q = pl.program_id(0)

        @pl.when(q == 0)
        def _():
            o_ref[...] = jnp.zeros_like(o_ref)

        for k in range(NCHIP):
            @pl.when(q == k)
            def _(k=k):
                for j in range(WIN_UNITS):
                    cu = WIN_UNIT0[k] + j
                    dst = pl.ds(C2I[cu] * UNIT, UNIT)
                    if cu in OVERLAP_UNITS:
                        o_ref[dst, :] += win_ref[j * UNIT:(j + 1) * UNIT, :]
                    else:
                        o_ref[dst, :] = win_ref[j * UNIT:(j + 1) * UNIT, :]
                if k == 1:
                    o_ref[F_FA:F_FA + FA_ROWS, :] = fa_ref[...]

    return _call(
        order, body, (wins, fas), name="assemble_w_in", grid=(NCHIP,),
        in_specs=[pl.BlockSpec((None, WIN_ROWS, D), lambda q: (q, 0, 0)),
                  pl.BlockSpec((None, FA_ROWS, D), lambda q: (1, 0, 0))],
        out_specs=pl.BlockSpec((NP, D), lambda q: (0, 0)),
        out_shape=jax.ShapeDtypeStruct((NP, D), bf16),
        compiler_params=_params(("arbitrary",)),
    )


def _norm_inproj(order, x, g1, wt, rope):
    tm = 256
    c_t, s1_t, s2_t = rope

    def body(x_ref, g_ref, w_ref, c_ref, s1_ref, s2_ref, h_ref, qkvb_ref, qkva_ref, gates_ref, fa_ref):
        xb = x_ref[...]
        r = lax.rsqrt(jnp.mean(xb * xb, axis=-1, keepdims=True) + EPS)
        h = ((xb * r) * g_ref[...]).astype(bf16)
        h_ref[...] = h
        c, s1, s2 = c_ref[...], s1_ref[...], s2_ref[...]
        for p in range(2):
            pb = _dot_nt(h, w_ref[F_DIL + p * DIL_BLK:F_DIL + (p + 1) * DIL_BLK, :])
            for ch in range(DIL_BLK // 128):
                pc = pb[:, ch * 128:(ch + 1) * 128]
                if ch < 6:
                    pc = pc * c + pltpu.roll(pc, 120, 1) * s1 + pltpu.roll(pc, 8, 1) * s2
                qkvb_ref[:, p * DIL_BLK + ch * 128:p * DIL_BLK + (ch + 1) * 128] = pc
        qkva_ref[...] = _dot_nt(h, w_ref[F_FOX:F_FA, :]).astype(bf16)
        fa_ref[...] = _dot_nt(h, w_ref[F_FA:F_FA + 128, :])
        gates_ref[...] = _dot_nt(h, w_ref[F_G:NP, :])

    row = lambda w: pl.BlockSpec((tm, w), lambda i: (i, 0))
    return _call(
        order, body, (x, g1, wt, c_t, s1_t, s2_t), name="norm_inproj", grid=(S // tm,),
        in_specs=[row(D), pl.BlockSpec((1, D), lambda i: (0, 0)), pl.BlockSpec((NP, D), lambda i: (0, 0)),
                  row(128), row(128), row(128)],
        out_specs=[row(D), row(2 * DIL_BLK), row(4 * FOX_BLK), row(2 * D), row(128)],
        out_shape=[jax.ShapeDtypeStruct((S, D), bf16), jax.ShapeDtypeStruct((S, 2 * DIL_BLK), f32),
                   jax.ShapeDtypeStruct((S, 4 * FOX_BLK), bf16), jax.ShapeDtypeStruct((S, 2 * D), f32),
                   jax.ShapeDtypeStruct((S, 128), f32)],
        compiler_params=_params(("parallel",)),
    )


def _forget_cumsum(order, fa, bpad):
    nb = S // TQ

    def body(fa_ref, b_ref, F_ref):
        rr = lax.broadcasted_iota(jnp.int32, (TQ, TQ), 0)
        cc = lax.broadcasted_iota(jnp.int32, (TQ, TQ), 1)
        tri = (rr >= cc).astype(bf16)
        lane = lax.broadcasted_iota(jnp.int32, (1, 128), 1)
        carry = jnp.zeros((1, 128), f32)
        for b in range(nb):
            z = fa_ref[b * TQ:(b + 1) * TQ, :] + b_ref[...]
            lf = jnp.minimum(z, 0.0) - jnp.log(1.0 + jnp.exp(-jnp.abs(z)))
            lf = jnp.where(lane < 8, lf, 0.0)
            hi, mid, lo = _split3(lf)
            fb = (_dot(tri, hi) + _dot(tri, mid)) + _dot(tri, lo) + carry
            F_ref[b * TQ:(b + 1) * TQ, :] = fb
            carry = fb[TQ - 1:TQ, :]

    return _call(
        order, body, (fa, bpad), name="forget_cumsum",
        out_shape=jax.ShapeDtypeStruct((S, 128), f32),
        compiler_params=_params(),
    )


def _head_masks():
    lane = lax.broadcasted_iota(jnp.int32, (1, 128), 1)
    return lane, (lane < HD, lane >= HD)


L_FT, L_ONE, L_LSE = 0, 3, 6
FOX_TQ, FOX_TK = 256, 512


def _set_lanes(x, lane, first, cols):
    for n, col in enumerate(cols):
        x = jnp.where(lane == first + n, col, x)
    return x


def _f32_parts(col):
    return [t.astype(f32) for t in _split3(col)]


def _fox_operands(qkv_ref, F_ref, lse_ref, qa, ka, p, rows):
    lane, hm = _head_masks()
    q = qkv_ref[rows, 0:128].astype(f32) * 0.125
    k = qkv_ref[rows, 128:256].astype(f32)
    Fb = F_ref[rows, :]
    for hh in (0, 1):
        free = (1 - hh) * HD
        fparts = _f32_parts(jnp.sum(jnp.where(lane == 2 * p + hh, Fb, 0.0), axis=1, keepdims=True))
        qcols = fparts + [1.0] * 3
        kcols = [1.0] * 3 + [-t for t in fparts]
        if lse_ref is not None:
            qcols += [-t for t in _f32_parts(lse_ref[rows, hh * HD:hh * HD + 1])]
            kcols += [1.0] * 3
        qa[hh, rows, :] = _set_lanes(jnp.where(hm[hh], q, 0.0), lane, free, qcols).astype(bf16)
        ka[hh, rows, :] = _set_lanes(k, lane, free, kcols).astype(bf16)


def _fox_fwd(order, qkva, F):
    tq, tk = FOX_TQ, FOX_TK

    def body(qkv_ref, F_ref, o_ref, lse_ref, qa, ka, vt):
        p = pl.program_id(0)
        keyi = lax.broadcasted_iota(jnp.int32, (tk, 1), 0)
        qryi = lax.broadcasted_iota(jnp.int32, (1, tq), 1)
        sub = lax.broadcasted_iota(jnp.int32, (128, 1), 0)

        def prep(i, c):
            rows = pl.ds(pl.multiple_of(i * tk, tk), tk)
            _fox_operands(qkv_ref, F_ref, None, qa, ka, p, rows)
            vt[i] = qkv_ref[rows, 256:384].astype(f32).T.astype(bf16)
            return c

        lax.fori_loop(0, S // tk, prep, 0)

        def qblock(i, c):
            r0 = pl.multiple_of(i * tq, tq)
            qh = [qa[hh, pl.ds(r0, tq), :] for hh in (0, 1)]

            def kv(jb, carry, masked):
                keys = pl.ds(pl.multiple_of(jb * tk, tk), tk)
                sts = [_dot_nt(ka[hh, keys, :], qh[hh]) for hh in (0, 1)]
                new = []
                for hh in (0, 1):
                    m, l, a = carry[3 * hh:3 * hh + 3]
                    st = sts[hh]
                    if masked:
                        st = jnp.where(jb * tk + keyi <= r0 + qryi, st, NEG)
                    mn = jnp.maximum(m, jnp.max(st, axis=0, keepdims=True))
                    al = jnp.exp(m - mn)
                    pt = jnp.exp(st - mn)
                    l = al * l + jnp.sum(pt, axis=0, keepdims=True)
                    a = al * a + _dot(vt[jb, hh * HD:(hh + 1) * HD, :], pt.astype(bf16))
                    new += [mn, l, a]
                return tuple(new)

            init = (jnp.full((1, tq), NEG, f32), jnp.zeros((1, tq), f32), jnp.zeros((HD, tq), f32)) * 2
            last = (r0 + tq - 1) // tk
            carry = lax.fori_loop(0, last, lambda j, cr: kv(j, cr, False), init)
            m0, l0, a0, m1, l1, a1 = kv(last, carry, True)
            ot = jnp.concatenate([a0 / l0, a1 / l1], axis=0)
            lt = jnp.where(sub < HD, m0 + jnp.log(l0), m1 + jnp.log(l1))
            o_ref[pl.ds(r0, tq), :] = ot.T.astype(bf16)
            lse_ref[pl.ds(r0, tq), :] = lt.T
            return c

        lax.fori_loop(0, S // tq, qblock, 0)

    pair = pl.BlockSpec((S, 128), lambda p: (0, p))
    return _call(
        order, body, (qkva, F), name="fox_fwd", grid=(4,),
        in_specs=[pl.BlockSpec((S, FOX_BLK), lambda p: (0, p)), pl.BlockSpec((S, 128), lambda p: (0, 0))],
        out_specs=[pair, pair],
        out_shape=[jax.ShapeDtypeStruct((S, FOXW), bf16), jax.ShapeDtypeStruct((S, FOXW), f32)],
        scratch_shapes=[pltpu.VMEM((2, S, 128), bf16)] * 2 + [pltpu.VMEM((S // tk, 128, tk), bf16)],
        compiler_params=_params(("parallel",)),
    )


def _permute_in(dst, src, r):
    L = S // r
    for rho in range(r):
        dst[rho * L:(rho + 1) * L, :] = src[pl.ds(rho, L, stride=r), :]


def _permute_out(dst, src, r):
    L = S // r
    for rho in range(r):
        dst[pl.ds(rho, L, stride=r), :] = src[rho * L:(rho + 1) * L, :]


def _band_geometry(bb, nbl):
    r0 = pl.multiple_of(bb * BAND, BAND)
    k0 = pl.multiple_of(jnp.maximum(bb - 1, 0) * BAND, BAND)
    sub0 = (bb - lax.rem(bb, nbl)) * BAND
    qi = r0 + lax.broadcasted_iota(jnp.int32, (BAND, 1), 0)
    ki = k0 + lax.broadcasted_iota(jnp.int32, (1, 2 * BAND), 1)
    diff = qi - ki
    valid = (diff >= 0) & (diff <= BAND) & (ki >= sub0)
    return r0, k0, valid


def _dil_views(ref):
    return [[ref.at[:, pl.ds((3 * role + g) * 128, 128)] for g in range(3)] for role in range(3)]


def _dil_in_specs():
    return [pl.BlockSpec((S, 128), lambda p, k=k: (0, 9 * p + k)) for k in range(9)]


def _dil_fwd(order, qkvb):
    def body(*refs):
        q_refs, k_refs, v_refs = refs[0:3], refs[3:6], refs[6:9]
        ob_ref, lse_ref, qp, kp, vp, op, lp = refs[9:16]
        on, ln = refs[16:19], refs[19:22]
        _, hm = _head_masks()
        for g, r in enumerate(DIL):
            nbl = S // r // BAND
            if r == 1:
                qs_, ks_, vs_, od, ld = q_refs[g], k_refs[g], v_refs[g], on[g], ln[g]
            else:
                _permute_in(qp, q_refs[g], r)
                _permute_in(kp, k_refs[g], r)
                _permute_in(vp, v_refs[g], r)
                qs_, ks_, vs_, od, ld = qp, kp, vp, op, lp

            def blk(bb, c, qs_=qs_, ks_=ks_, vs_=vs_, od=od, ld=ld, nbl=nbl):
                r0, k0, valid = _band_geometry(bb, nbl)
                q = qs_[pl.ds(r0, BAND), :] * 0.125
                kw = ks_[pl.ds(k0, 2 * BAND), :].astype(bf16)
                vw = vs_[pl.ds(k0, 2 * BAND), :]
                o = jnp.zeros((BAND, 128), f32)
                lse = jnp.zeros((BAND, 128), f32)
                for hh in (0, 1):
                    qh = jnp.where(hm[hh], q, 0.0).astype(bf16)
                    s = jnp.where(valid, _dot_nt(qh, kw), NEG)
                    m = jnp.max(s, axis=1, keepdims=True)
                    pr = jnp.exp(s - m)
                    l = jnp.sum(pr, axis=1, keepdims=True)
                    vm = jnp.where(hm[hh], vw, 0.0).astype(bf16)
                    o = o + _dot((pr / l).astype(bf16), vm)
                    lse = jnp.where(hm[hh], m + jnp.log(l), lse)
                od[pl.ds(r0, BAND), :] = o
                ld[pl.ds(r0, BAND), :] = lse
                return c

            lax.fori_loop(0, S // BAND, blk, 0)
            if r != 1:
                _permute_out(on[g], op, r)
                _permute_out(ln[g], lp, r)

        def combine(i, c):
            r0 = pl.multiple_of(i * TQ, TQ)
            ls = [ln[g][pl.ds(r0, TQ), :] for g in range(3)]
            mx = jnp.maximum(jnp.maximum(ls[0], ls[1]), ls[2])
            es = [jnp.exp(l - mx) for l in ls]
            tot = (es[0] + es[1]) + es[2]
            acc = (es[0] / tot) * on[0][pl.ds(r0, TQ), :]
            acc = acc + (es[1] / tot) * on[1][pl.ds(r0, TQ), :]
            acc = acc + (es[2] / tot) * on[2][pl.ds(r0, TQ), :]
            ob_ref[pl.ds(r0, TQ), :] = acc.astype(bf16)
            lse_ref[pl.ds(r0, TQ), :] = mx + jnp.log(tot)
            return c

        lax.fori_loop(0, S // TQ, combine, 0)

    out_blk = pl.BlockSpec((S, 128), lambda p: (0, p))
    return _call(
        order, body, [qkvb] * 9, name="dil_fwd", grid=(2,),
        in_specs=_dil_in_specs(), out_specs=[out_blk, out_blk],
        out_shape=[jax.ShapeDtypeStruct((S, DILOUT), bf16), jax.ShapeDtypeStruct((S, DILOUT), f32)],
        scratch_shapes=[pltpu.VMEM((S, 128), f32)] * 11,
        compiler_params=_params(("parallel",)),
    )


def _branch_mix(order, oa, ob, was, wbs, gates):
    tm = 512

    def body(oa_ref, ob_ref, wa_ref, wb_ref, g_ref, ya_ref, yb_ref, mix_ref):
        oa_b, ob_b = oa_ref[...], ob_ref[...]
        for q in range(NCHIP):
            cols = slice(q * 256, (q + 1) * 256)
            ya = _dot(oa_b, wa_ref[q])
            yb = _dot(ob_b, wb_ref[q])
            ya_ref[:, cols] = ya
            yb_ref[:, cols] = yb
            ga = g_ref[:, q * 256:(q + 1) * 256]
            gb = g_ref[:, D + q * 256:D + (q + 1) * 256]
            mix_ref[:, cols] = (jax.nn.sigmoid(ga) * ya + jax.nn.sigmoid(gb) * yb).astype(bf16)

    row = lambda w: pl.BlockSpec((tm, w), lambda i: (i, 0))
    full3 = lambda a: pl.BlockSpec(a.shape, lambda i: (0, 0, 0))
    return _call(
        order, body, (oa, ob, was, wbs, gates), name="branch_mix", grid=(S // tm,),
        in_specs=[row(FOXW), row(DILOUT), full3(was), full3(wbs), row(2 * D)],
        out_specs=[row(D), row(D), row(D)],
        out_shape=[jax.ShapeDtypeStruct((S, D), f32), jax.ShapeDtypeStruct((S, D), f32),
                   jax.ShapeDtypeStruct((S, D), bf16)],
        compiler_params=_params(("parallel",)),
    )


def _outproj_norm(order, mixed, wout, x, g2):
    tm = 512

    def body(m_ref, w_ref, x_ref, g_ref, x2_ref, h2_ref):
        x2 = x_ref[...] + _dot(m_ref[...], w_ref[...])
        x2_ref[...] = x2
        r = lax.rsqrt(jnp.mean(x2 * x2, axis=-1, keepdims=True) + EPS)
        h2_ref[...] = ((x2 * r) * g_ref[...]).astype(bf16)

    row = pl.BlockSpec((tm, D), lambda i: (i, 0))
    return _call(
        order, body, (mixed, wout, x, g2), name="outproj_norm", grid=(S // tm,),
        in_specs=[row, pl.BlockSpec((D, D), lambda i: (0, 0)), row, pl.BlockSpec((1, D), lambda i: (0, 0))],
        out_specs=[row, row],
        out_shape=[jax.ShapeDtypeStruct((S, D), f32), jax.ShapeDtypeStruct((S, D), bf16)],
        compiler_params=_params(("parallel",)),
    )


def _mlp_up(order, h2, wups):
    tm = 512

    def body(h_ref, w_ref, u_ref, a_ref):
        u = _dot(h_ref[...], w_ref[...])
        u_ref[...] = u
        ru = jnp.maximum(u, 0.0)
        a_ref[...] = (ru * ru).astype(bf16)

    out = pl.BlockSpec((tm, D), lambda q, i: (i, q))
    return _call(
        order, body, (h2, wups), name="mlp_up", grid=(NCHIP, S // tm),
        in_specs=[pl.BlockSpec((tm, D), lambda q, i: (i, 0)), pl.BlockSpec((None, D, D), lambda q, i: (q, 0, 0))],
        out_specs=[out, out],
        out_shape=[jax.ShapeDtypeStruct((S, DFF), f32), jax.ShapeDtypeStruct((S, DFF), bf16)],
        compiler_params=_params(("parallel", "parallel")),
    )


def _mlp_down_loss(order, a, wdown, x2, g3, tgt):
    tm = 256

    def body(a_ref, w_ref, x2_ref, g_ref, t_ref, dx_ref, dxb_ref, dg_ref, loss_ref):
        i = pl.program_id(0)
        x3 = x2_ref[...] + _dot(a_ref[...], w_ref[...])
        r = lax.rsqrt(jnp.mean(x3 * x3, axis=-1, keepdims=True) + EPS)
        xh = x3 * r
        g = g_ref[...]
        e = xh * g - t_ref[...]
        part = 0.5 * jnp.sum(jnp.mean(e * e, axis=-1, keepdims=True), axis=0, keepdims=True)
        dy = e * (1.0 / D)
        gdy = dy * g
        dx = r * (gdy - xh * jnp.mean(gdy * xh, axis=-1, keepdims=True))
        dx_ref[...] = dx
        dxb_ref[...] = dx.astype(bf16)

        @pl.when(i == 0)
        def _():
            dg_ref[...] = jnp.zeros_like(dg_ref)
            loss_ref[...] = jnp.zeros_like(loss_ref)

        dg_ref[...] += jnp.sum(dy * xh, axis=0, keepdims=True)
        loss_ref[...] += jnp.broadcast_to(part, (1, 128))

    row = pl.BlockSpec((tm, D), lambda i: (i, 0))
    vec = pl.BlockSpec((1, D), lambda i: (0, 0))
    return _call(
        order, body, (a, wdown, x2, g3, tgt), name="mlp_down_loss", grid=(S // tm,),
        in_specs=[pl.BlockSpec((tm, DFF), lambda i: (i, 0)), pl.BlockSpec((DFF, D), lambda i: (0, 0)), row, vec, row],
        out_specs=[row, row, vec, pl.BlockSpec((1, 128), lambda i: (0, 0))],
        out_shape=[jax.ShapeDtypeStruct((S, D), f32), jax.ShapeDtypeStruct((S, D), bf16),
                   jax.ShapeDtypeStruct((1, D), f32), jax.ShapeDtypeStruct((1, 128), f32)],
        compiler_params=_params(("arbitrary",)),
    )


def _mlp_down_bwd(order, dx3b, wdown, u):
    tm = 256

    def body(d_ref, w_ref, u_ref, du_ref):
        d = d_ref[...]
        for q in range(NCHIP):
            cols = slice(q * D, (q + 1) * D)
            da = _dot_nt(d, w_ref[cols, :])
            du_ref[:, cols] = (da * (2.0 * jnp.maximum(u_ref[:, cols], 0.0))).astype(bf16)

    return _call(
        order, body, (dx3b, wdown, u), name="mlp_down_bwd", grid=(S // tm,),
        in_specs=[pl.BlockSpec((tm, D), lambda i: (i, 0)), pl.BlockSpec((DFF, D), lambda i: (0, 0)),
                  pl.BlockSpec((tm, DFF), lambda i: (i, 0))],
        out_specs=pl.BlockSpec((tm, DFF), lambda i: (i, 0)),
        out_shape=jax.ShapeDtypeStruct((S, DFF), bf16),
        compiler_params=_params(("parallel",)),
    )


def _mlp_up_bwd(order, du, wups, x2, dx3, g2):
    tm = 256

    def body(du_ref, w_ref, x2_ref, dx3_ref, g_ref, dx2_ref, dx2b_ref, dg_ref):
        i = pl.program_id(0)
        dh = jnp.zeros((tm, D), f32)
        for q in range(NCHIP):
            dh = dh + _dot_nt(du_ref[:, q * D:(q + 1) * D], w_ref[q])
        x2 = x2_ref[...]
        r = lax.rsqrt(jnp.mean(x2 * x2, axis=-1, keepdims=True) + EPS)
        xh = x2 * r
        gdh = dh * g_ref[...]
        dx2 = dx3_ref[...] + r * (gdh - xh * jnp.mean(gdh * xh, axis=-1, keepdims=True))
        dx2_ref[...] = dx2
        dx2b_ref[...] = dx2.astype(bf16)

        @pl.when(i == 0)
        def _():
            dg_ref[...] = jnp.zeros_like(dg_ref)

        dg_ref[...] += jnp.sum(dh * xh, axis=0, keepdims=True)

    row = pl.BlockSpec((tm, D), lambda i: (i, 0))
    vec = pl.BlockSpec((1, D), lambda i: (0, 0))
    return _call(
        order, body, (du, wups, x2, dx3, g2), name="mlp_up_bwd", grid=(S // tm,),
        in_specs=[pl.BlockSpec((tm, DFF), lambda i: (i, 0)), pl.BlockSpec((NCHIP, D, D), lambda i: (0, 0, 0)),
                  row, row, vec],
        out_specs=[row, row, vec],
        out_shape=[jax.ShapeDtypeStruct((S, D), f32), jax.ShapeDtypeStruct((S, D), bf16),
                   jax.ShapeDtypeStruct((1, D), f32)],
        compiler_params=_params(("arbitrary",)),
    )


def _gate_bwd(order, dx2b, wout, gates, ya, yb):
    tm = 256

    def body(d_ref, w_ref, g_ref, ya_ref, yb_ref, dya_ref, dyb_ref, dproj_ref):
        dm = _dot_nt(d_ref[...], w_ref[...])
        sa = jax.nn.sigmoid(g_ref[:, 0:D])
        sb = jax.nn.sigmoid(g_ref[:, D:2 * D])
        dya_ref[...] = (dm * sa).astype(bf16)
        dyb_ref[...] = (dm * sb).astype(bf16)
        dproj_ref[:, 0:D] = (dm * ya_ref[...] * (sa * (1.0 - sa))).astype(bf16)
        dproj_ref[:, D:2 * D] = (dm * yb_ref[...] * (sb * (1.0 - sb))).astype(bf16)

    row = lambda w: pl.BlockSpec((tm, w), lambda i: (i, 0))
    return _call(
        order, body, (dx2b, wout, gates, ya, yb), name="gate_bwd", grid=(S // tm,),
        in_specs=[row(D), pl.BlockSpec((D, D), lambda i: (0, 0)), row(2 * D), row(D), row(D)],
        out_specs=[row(D), row(D), pl.BlockSpec((tm, 2 * D), lambda i: (i, F_G // (2 * D)))],
        out_shape=[jax.ShapeDtypeStruct((S, D), bf16), jax.ShapeDtypeStruct((S, D), bf16),
                   jax.ShapeDtypeStruct((S, NP), bf16)],
        compiler_params=_params(("parallel",)),
    )


def _branch_bwd(order, dya, dyb, was, wbs):
    tm = 512

    def body(dya_ref, dyb_ref, wa_ref, wb_ref, doa_ref, dob_ref):
        doa = jnp.zeros((tm, FOXW), f32)
        dob = jnp.zeros((tm, DILOUT), f32)
        for q in range(NCHIP):
            cols = slice(q * 256, (q + 1) * 256)
            doa = doa + _dot_nt(dya_ref[:, cols], wa_ref[q])
            dob = dob + _dot_nt(dyb_ref[:, cols], wb_ref[q])
        doa_ref[...] = doa.astype(bf16)
        dob_ref[...] = dob

    row = lambda w: pl.BlockSpec((tm, w), lambda i: (i, 0))
    full3 = lambda a: pl.BlockSpec(a.shape, lambda i: (0, 0, 0))
    return _call(
        order, body, (dya, dyb, was, wbs), name="branch_bwd", grid=(S // tm,),
        in_specs=[row(D), row(D), full3(was), full3(wbs)],
        out_specs=[row(FOXW), row(DILOUT)],
        out_shape=[jax.ShapeDtypeStruct((S, FOXW), bf16), jax.ShapeDtypeStruct((S, DILOUT), f32)],
        compiler_params=_params(("parallel",)),
    )


def _branch_wgrad(order, oa, ob, dya, dyb):
    def body(oa_ref, ob_ref, dya_ref, dyb_ref, dwa_ref, dwb_ref):
        dwa_ref[...] = _dot_tn(oa_ref[...], dya_ref[...])
        dwb_ref[...] = _dot_tn(ob_ref[...], dyb_ref[...])

    full = lambda w: pl.BlockSpec((S, w), lambda q: (0, 0))
    colq = pl.BlockSpec((S, 256), lambda q: (0, q))
    return _call(
        order, body, (oa, ob, dya, dyb), name="branch_wgrad", grid=(NCHIP,),
        in_specs=[full(FOXW), full(DILOUT), colq, colq],
        out_specs=[pl.BlockSpec((None, FOXW, 256), lambda q: (q, 0, 0)),
                   pl.BlockSpec((None, DILOUT, 256), lambda q: (q, 0, 0))],
        out_shape=[jax.ShapeDtypeStruct((NCHIP, FOXW, 256), f32), jax.ShapeDtypeStruct((NCHIP, DILOUT, 256), f32)],
        compiler_params=_params(("parallel",)),
    )


def _fox_bwd(order, qkva, doa, oa, lse, F, dproj):
    tq, tk = FOX_TQ, FOX_TK

    def body(qkv_ref, do_ref, o_ref, lse_ref, F_ref, _dproj_in, dF_ref, dqkv_ref, qa, ka, da, va, kat,
             dk_scr, dv_scr, dqt_scr):
        p = pl.program_id(0)
        lane, hm = _head_masks()
        keyi = lax.broadcasted_iota(jnp.int32, (tk, 1), 0)
        qryi = lax.broadcasted_iota(jnp.int32, (1, tq), 1)

        def prep(i, c):
            rows = pl.ds(pl.multiple_of(i * tk, tk), tk)
            _fox_operands(qkv_ref, F_ref, lse_ref, qa, ka, p, rows)
            do = do_ref[rows, :].astype(f32)
            prod = do * o_ref[rows, :].astype(f32)
            v = qkv_ref[rows, 256:384].astype(f32)
            for hh in (0, 1):
                free = (1 - hh) * HD
                delta = jnp.sum(jnp.where(hm[hh], prod, 0.0), axis=1, keepdims=True)
                da[hh, rows, :] = _set_lanes(jnp.where(hm[hh], do, 0.0), lane, free,
                                             [-t for t in _f32_parts(delta)]).astype(bf16)
                va[hh, rows, :] = _set_lanes(v, lane, free, [1.0] * 3).astype(bf16)
                kat[hh, i] = ka[hh, rows, :].astype(f32).T.astype(bf16)
                dk_scr[hh, rows, :] = jnp.zeros((tk, 128), f32)
                dv_scr[hh, rows, :] = jnp.zeros((tk, 128), f32)
            return c

        lax.fori_loop(0, S // tk, prep, 0)

        def qblock(i, c):
            r0 = pl.multiple_of(i * tq, tq)
            qrows = pl.ds(r0, tq)
            qh = [qa[hh, qrows, :] for hh in (0, 1)]
            dh = [da[hh, qrows, :] for hh in (0, 1)]
            dqt_scr[...] = jnp.zeros_like(dqt_scr)

            def kv(jb, c2, masked):
                keys = pl.ds(pl.multiple_of(jb * tk, tk), tk)
                sts = [_dot_nt(ka[hh, keys, :], qh[hh]) for hh in (0, 1)]
                dps = [_dot_nt(va[hh, keys, :], dh[hh]) for hh in (0, 1)]
                for hh in (0, 1):
                    pt = jnp.exp(sts[hh])
                    if masked:
                        pt = jnp.where(jb * tk + keyi <= r0 + qryi, pt, 0.0)
                    dsb = (pt * dps[hh]).astype(bf16)
                    dv_scr[hh, keys, :] += _dot(pt.astype(bf16), dh[hh])
                    dk_scr[hh, keys, :] += _dot(dsb, qh[hh])
                    dqt_scr[hh] += _dot(kat[hh, jb], dsb)
                return c2

            last = (r0 + tq - 1) // tk
            lax.fori_loop(0, last, lambda j, c2: kv(j, c2, False), 0)
            kv(last, 0, True)
            dq0, dq1 = dqt_scr[0].T, dqt_scr[1].T
            dqkv_ref[qrows, 0:128] = (jnp.where(hm[0], dq0, dq1) * 0.125).astype(bf16)
            dF_ref[qrows, :] = jnp.where(lane == 0, dq0[:, HD:HD + 1], jnp.where(lane == 1, dq1[:, 0:1], 0.0))
            return c

        lax.fori_loop(0, S // tq, qblock, 0)

        def finish(i, c):
            rows = pl.ds(pl.multiple_of(i * tq, tq), tq)
            dk0, dk1 = dk_scr[0, rows, :], dk_scr[1, rows, :]
            dqkv_ref[rows, 128:256] = jnp.where(hm[0], dk0, dk1).astype(bf16)
            dqkv_ref[rows, 256:384] = jnp.where(hm[0], dv_scr[0, rows, :], dv_scr[1, rows, :]).astype(bf16)
            cs = jnp.where(lane == 0, dk0[:, HD + L_ONE:HD + L_ONE + 1],
                           jnp.where(lane == 1, dk1[:, L_ONE:L_ONE + 1], 0.0))
            dF_ref[rows, :] = dF_ref[rows, :] - cs
            return c

        lax.fori_loop(0, S // tq, finish, 0)

    pair = pl.BlockSpec((S, 128), lambda p: (0, p))
    return _call(
        order, body, (qkva, doa, oa, lse, F, dproj), name="fox_bwd", grid=(4,),
        in_specs=[pl.BlockSpec((S, FOX_BLK), lambda p: (0, p)), pair, pair, pair,
                  pl.BlockSpec((S, 128), lambda p: (0, 0)), pl.BlockSpec(memory_space=pl.ANY)],
        out_specs=[pair, pl.BlockSpec((S, FOX_BLK), lambda p: (0, F_FOX // FOX_BLK + p))],
        out_shape=[jax.ShapeDtypeStruct((S, FOXW), f32), jax.ShapeDtypeStruct((S, NP), bf16)],
        input_output_aliases={5: 1},
        scratch_shapes=[pltpu.VMEM((2, S, 128), bf16)] * 4 + [pltpu.VMEM((2, S // tk, 128, tk), bf16)]
        + [pltpu.VMEM((2, S, 128), f32)] * 2 + [pltpu.VMEM((2, 128, tq), f32)],
        compiler_params=_params(("parallel",)),
    )


def _forget_bwd(order, dF, fa, bpad, dproj):
    nb = S // TQ

    def body(dF_ref, fa_ref, b_ref, _dproj_in, db_ref, dfa_ref):
        rr = lax.broadcasted_iota(jnp.int32, (TQ, TQ), 0)
        cc = lax.broadcasted_iota(jnp.int32, (TQ, TQ), 1)
        upper = (cc >= rr).astype(bf16)
        lane = lax.broadcasted_iota(jnp.int32, (1, 128), 1)
        carry = jnp.zeros((1, 128), f32)
        db = jnp.zeros((1, 128), f32)
        for b in reversed(range(nb)):
            cols = jnp.zeros((TQ, 128), f32)
            for h in range(8):
                c0 = (h // 2) * 128 + h % 2
                cols = jnp.where(lane == h, dF_ref[b * TQ:(b + 1) * TQ, c0:c0 + 1], cols)
            dlf = carry
            for part in _split3(cols):
                dlf = dlf + _dot(upper, part)
            carry = carry + jnp.sum(cols, axis=0, keepdims=True)
            z = fa_ref[b * TQ:(b + 1) * TQ, :] + b_ref[...]
            dz = jnp.where(lane < 8, dlf * jax.nn.sigmoid(-z), 0.0)
            dfa_ref[b * TQ:(b + 1) * TQ, 0:128] = dz.astype(bf16)
            dfa_ref[b * TQ:(b + 1) * TQ, 128:256] = jnp.zeros((TQ, 128), bf16)
            db = db + jnp.sum(dz, axis=0, keepdims=True)
        db_ref[...] = db

    whole = lambda a: pl.BlockSpec(a.shape, lambda i: (0,) * a.ndim)
    return _call(
        order, body, (dF, fa, bpad, dproj), name="forget_bwd", grid=(1,),
        in_specs=[whole(dF), whole(fa), whole(bpad), pl.BlockSpec(memory_space=pl.ANY)],
        out_specs=[pl.BlockSpec((1, 128), lambda i: (0, 0)), pl.BlockSpec((S, 256), lambda i: (0, F_FA // 256))],
        out_shape=[jax.ShapeDtypeStruct((1, 128), f32), jax.ShapeDtypeStruct((S, NP), bf16)],
        input_output_aliases={3: 1},
        compiler_params=_params(("arbitrary",)),
    )


def _dil_bwd(order, qkvb, dob, ob, lseb, rope, dproj):
    c_t, s1_t, s2_t = rope

    def body(*refs):
        q_refs, k_refs, v_refs = refs[0:3], refs[3:6], refs[6:9]
        dob_ref, ob_ref, lse_ref, c_ref, s1_ref, s2_ref, _dproj_in, dqkv_ref = refs[9:17]
        qp, kp, vp, dop, lp, dlp, dln, dqp, dkp, dvp, nat = refs[17:28]
        dq_out, dk_out, dv_out = _dil_views(dqkv_ref)
        _, hm = _head_masks()

        def delta_rows(i, c):
            r0 = pl.multiple_of(i * TQ, TQ)
            prod = dob_ref[pl.ds(r0, TQ), :] * ob_ref[pl.ds(r0, TQ), :].astype(f32)
            d0 = jnp.sum(jnp.where(hm[0], prod, 0.0), axis=1, keepdims=True)
            d1 = jnp.sum(jnp.where(hm[1], prod, 0.0), axis=1, keepdims=True)
            dln[pl.ds(r0, TQ), :] = jnp.where(hm[0], d0, d1)
            return c

        lax.fori_loop(0, S // TQ, delta_rows, 0)

        for g, r in enumerate(DIL):
            nbl = S // r // BAND
            if r == 1:
                srcs = (q_refs[g], k_refs[g], v_refs[g], dob_ref, lse_ref, dln)
            else:
                for dst, src in ((qp, q_refs[g]), (kp, k_refs[g]), (vp, v_refs[g]), (dop, dob_ref),
                                 (lp, lse_ref), (dlp, dln)):
                    _permute_in(dst, src, r)
                srcs = (qp, kp, vp, dop, lp, dlp)
            dkp[...] = jnp.zeros_like(dkp)
            dvp[...] = jnp.zeros_like(dvp)

            def blk(bb, c, srcs=srcs, nbl=nbl):
                qs_, ks_, vs_, dos_, ls_, dls_ = srcs
                r0, k0, valid = _band_geometry(bb, nbl)
                q = qs_[pl.ds(r0, BAND), :] * 0.125
                kwf = ks_[pl.ds(k0, 2 * BAND), :]
                kw = kwf.astype(bf16)
                vw = vs_[pl.ds(k0, 2 * BAND), :].astype(bf16)
                do = dos_[pl.ds(r0, BAND), :]
                lse = ls_[pl.ds(r0, BAND), :]
                dlt = dls_[pl.ds(r0, BAND), :]
                dq = jnp.zeros((BAND, 128), f32)
                dk = jnp.zeros((2 * BAND, 128), f32)
                dv = jnp.zeros((2 * BAND, 128), f32)
                for hh in (0, 1):
                    qh = jnp.where(hm[hh], q, 0.0).astype(bf16)
                    doh = jnp.where(hm[hh], do, 0.0).astype(bf16)
                    kh = jnp.where(hm[hh], kwf, 0.0).astype(bf16)
                    s = _dot_nt(qh, kw)
                    pr = jnp.where(valid, jnp.exp(s - lse[:, hh * HD:hh * HD + 1]), 0.0)
                    dp = _dot_nt(doh, vw)
                    ds = pr * (dp - dlt[:, hh * HD:hh * HD + 1])
                    dsb = ds.astype(bf16)
                    dv = dv + _dot_tn(pr.astype(bf16), doh)
                    dk = dk + _dot_tn(dsb, qh)
                    dq = dq + _dot(dsb, kh)
                dqp[pl.ds(r0, BAND), :] = dq * 0.125
                dkp[pl.ds(k0, 2 * BAND), :] += dk
                dvp[pl.ds(k0, 2 * BAND), :] += dv
                return c

            lax.fori_loop(0, S // BAND, blk, 0)

            for acc, out, roped in ((dqp, dq_out[g], True), (dkp, dk_out[g], True), (dvp, dv_out[g], False)):
                if r == 1:
                    src = acc
                else:
                    _permute_out(nat, acc, r)
                    src = nat

                def emit(i, c, src=src, out=out, roped=roped):
                    r0 = pl.multiple_of(i * TQ, TQ)
                    d = src[pl.ds(r0, TQ), :]
                    if roped:
                        d = (d * c_ref[pl.ds(r0, TQ), :] + pltpu.roll(d * s1_ref[pl.ds(r0, TQ), :], 8, 1)
                             + pltpu.roll(d * s2_ref[pl.ds(r0, TQ), :], 120, 1))
                    out[pl.ds(r0, TQ), :] = d.astype(bf16)
                    return c

                lax.fori_loop(0, S // TQ, emit, 0)

    pair = pl.BlockSpec((S, 128), lambda p: (0, p))
    tab = pl.BlockSpec((S, 128), lambda p: (0, 0))
    blk_spec = pl.BlockSpec((S, DIL_BLK), lambda p: (0, p))
    return _call(
        order, body, [qkvb] * 9 + [dob, ob, lseb, c_t, s1_t, s2_t, dproj], name="dil_bwd", grid=(2,),
        in_specs=_dil_in_specs() + [pair, pair, pair, tab, tab, tab, pl.BlockSpec(memory_space=pl.ANY)],
        out_specs=blk_spec,
        out_shape=jax.ShapeDtypeStruct((S, NP), bf16),
        input_output_aliases={15: 0},
        scratch_shapes=[pltpu.VMEM((S, 128), f32)] * 11,
        compiler_params=_params(("parallel",)),
    )


def _inproj_bwd(order, dproj, wt, x, dx2, g1):
    tm = 256

    def body(d_ref, w_ref, x_ref, dx2_ref, g_ref, dx_ref, dg_ref):
        i = pl.program_id(0)
        dh = _dot(d_ref[...], w_ref[...])
        xb = x_ref[...]
        r = lax.rsqrt(jnp.mean(xb * xb, axis=-1, keepdims=True) + EPS)
        xh = xb * r
        gdh = dh * g_ref[...]
        dx_ref[...] = dx2_ref[...] + r * (gdh - xh * jnp.mean(gdh * xh, axis=-1, keepdims=True))

        @pl.when(i == 0)
        def _():
            dg_ref[...] = jnp.zeros_like(dg_ref)

        dg_ref[...] += jnp.sum(dh * xh, axis=0, keepdims=True)

    row = pl.BlockSpec((tm, D), lambda i: (i, 0))
    vec = pl.BlockSpec((1, D), lambda i: (0, 0))
    return _call(
        order, body, (dproj, wt, x, dx2, g1), name="inproj_bwd", grid=(S // tm,),
        in_specs=[pl.BlockSpec((tm, NP), lambda i: (i, 0)), pl.BlockSpec((NP, D), lambda i: (0, 0)), row, row, vec],
        out_specs=[row, vec],
        out_shape=[jax.ShapeDtypeStruct((S, D), f32), jax.ShapeDtypeStruct((1, D), f32)],
        compiler_params=_params(("arbitrary",)),
    )


HBM = pl.BlockSpec(memory_space=pltpu.HBM)
SEM = pl.BlockSpec(memory_space=pltpu.SEMAPHORE)
SMALL_ROWS = 8


def _comm_call(name, body, bufs, order, sems_in=(), new_sems=()):
    nb, ns, nn = len(bufs), len(sems_in), len(new_sems)
    extra = order.token_for(bufs)

    def kern(*refs):
        off = nb + ns + len(extra)
        body(refs[:nb], refs[nb:nb + ns], refs[off:off + nn])
        refs[-1][...] = jnp.zeros((8, 128), f32)

    res = pl.pallas_call(
        kern, name=name,
        in_specs=[HBM] * nb + [SEM] * ns + [pl.BlockSpec(memory_space=pl.ANY)] * len(extra),
        out_specs=[SEM] * nn + [HBM] * nb + [pl.BlockSpec(memory_space=pltpu.VMEM)],
        out_shape=[pltpu.SemaphoreType.DMA((k,)) for k in new_sems] + [pltpu.HBM(b.shape, b.dtype) for b in bufs]
        + [jax.ShapeDtypeStruct((8, 128), f32)],
        input_output_aliases={i: nn + i for i in range(nb)},
        compiler_params=pltpu.CompilerParams(has_side_effects=pltpu.SideEffectType.DATAFLOW_SIDE_EFFECTING),
    )(*[pltpu.with_memory_space_constraint(b, pltpu.HBM) for b in bufs], *sems_in, *extra)
    order.mark(res[-1])
    return list(res[:nn]), list(res[nn:nn + nb])


def _place():
    x, y, c = lax.axis_index("x"), lax.axis_index("y"), lax.axis_index("c")
    chips = [(1 - x, y), (x, 1 - y), (1 - x, 1 - y)]
    return x, y, c, chips


def _rcopy(src, dst, ssem, rsem, dev):
    return pltpu.make_async_remote_copy(src_ref=src, dst_ref=dst, send_sem=ssem, recv_sem=rsem,
                                        device_id=dev, device_id_type=pl.DeviceIdType.MESH)


def _half(nrows, which):
    return pl.ds(which * (nrows // 2), nrows // 2)


def _ici_copies(stack, group_sizes, ssems, rsems):
    x, y, c, chips = _place()
    me_q = 2 * x + y
    sends, recvs = [], []
    a = 0
    for grp, size in enumerate(group_sizes):
        for k in range(size):
            rows = _half(stack[a].shape[1], c)
            for j, (cx, cy) in enumerate(chips):
                mine = stack[a].at[me_q, rows]
                sends.append(_rcopy(mine, mine, ssems[grp].at[k * 3 + j], rsems[grp].at[k * 3 + j], (cx, cy, c)))
                theirs = stack[a].at[2 * cx + cy, rows]
                recvs.append(_rcopy(theirs, theirs, ssems[grp].at[k * 3 + j], rsems[grp].at[k * 3 + j],
                                    (cx, cy, c)))
            a += 1
    return sends, recvs


def _allgather_start(stacks, group_sizes, order):
    def body(bufs, _, new):
        sends, _r = _ici_copies(bufs, group_sizes, new[0::2], new[1::2])
        for cp in sends:
            cp.start()

    sizes = []
    for size in group_sizes:
        sizes += [3 * size, 3 * size]
    sems, stacks = _comm_call("allgather_start", body, stacks, order, new_sems=sizes)
    return [(sems[2 * g], sems[2 * g + 1]) for g in range(len(group_sizes))], stacks


def _forward_copies(stack, ssem, rsem):
    x, y, c, chips = _place()
    sib = (x, y, 1 - c)
    sends, recvs = [], []
    for a in range(len(stack)):
        for j, (cx, cy) in enumerate(chips):
            landed = stack[a].at[2 * cx + cy, _half(stack[a].shape[1], c)]
            sends.append(_rcopy(landed, landed, ssem.at[a * 3 + j], rsem.at[a * 3 + j], sib))
            other = stack[a].at[2 * cx + cy, _half(stack[a].shape[1], 1 - c)]
            recvs.append(_rcopy(other, other, ssem.at[a * 3 + j], rsem.at[a * 3 + j], sib))
    return sends, recvs


def _allgather_forward(name, stacks, sems, order):
    n = len(stacks)

    def body(bufs, taken, new):
        sends, recvs = _ici_copies(bufs, [n], [taken[0]], [taken[1]])
        for cp in sends:
            cp.wait_send()
        for cp in recvs:
            cp.wait_recv()
        fwd, _r = _forward_copies(bufs, new[0], new[1])
        for cp in fwd:
            cp.start()

    return _comm_call(name, body, stacks, order, sems_in=sems, new_sems=(3 * n, 3 * n))


def _allgather_finish(name, stacks, sems, order):
    def body(bufs, taken, _):
        sends, recvs = _forward_copies(bufs, taken[0], taken[1])
        for cp in sends:
            cp.wait_send()
        for cp in recvs:
            cp.wait_recv()

    return _comm_call(name, body, stacks, order, sems_in=sems)[1]


def _window_unit(q, j):
    return C2I[WIN_UNIT0[q] + j]


def _pair_copies(g, t, ssem, rsem, gathered):
    x, y, c, _ = _place()
    sib = (x, y, 1 - c)
    cps, whole = [], []
    for a in range(len(g)):
        if a == 0 and gathered:
            for q in range(NCHIP):
                for j in range(WIN_UNITS // 2):
                    u = jnp.where(c == 0, _window_unit(q, WIN_UNITS // 2 + j), _window_unit(q, j))
                    src = g[0].at[pl.ds(pl.multiple_of(u * UNIT, UNIT), UNIT), :]
                    cps.append(_rcopy(src, t[0].at[q, pl.ds(j * UNIT, UNIT), :], ssem.at[0], rsem.at[0], sib))
            whole.append(_rcopy(t[0], t[0], ssem.at[0], rsem.at[0], sib))
        else:
            cp = _rcopy(g[a].at[:, _half(g[a].shape[1], 1 - c), :], t[a], ssem.at[a], rsem.at[a], sib)
            cps.append(cp)
            whole.append(cp)
    return cps, whole


def _pair_start(name, gs, order, gathered=False):
    n = len(gs)
    ts = [lax.empty((NCHIP, WIN_ROWS // 2, D) if (a == 0 and gathered) else (NCHIP, g.shape[1] // 2, g.shape[2]), f32)
          for a, g in enumerate(gs)]

    def body(bufs, _, new):
        for cp in _pair_copies(bufs[:n], bufs[n:], new[0], new[1], gathered)[0]:
            cp.start()

    return _comm_call(name, body, list(gs) + ts, order, new_sems=(n, n))


def _pair_wait(name, bufs, sems, order, gathered=False):
    n = len(bufs) // 2

    def body(refs, taken, _):
        for cp in _pair_copies(refs[:n], refs[n:], taken[0], taken[1], gathered)[1]:
            cp.wait_send()
            cp.wait_recv()

    bufs = _comm_call(name, body, bufs, order, sems_in=sems)[1]
    return bufs[:n], bufs[n:]


def _row_tile(h):
    return min(h, 256)


def _pair_add(order, g, t, c_arr, name):
    _, R, C = g.shape
    h = R // 2
    tr = _row_tile(h)
    nblk = h // tr

    def body(c_ref, g_ref, t_ref, p32_ref, p16_ref):
        s = g_ref[...] + t_ref[...]
        p32_ref[...] = s
        p16_ref[...] = s.astype(bf16)

    blk = pl.BlockSpec((None, tr, C), lambda q, i, c_ref: (q, i, 0))
    return _call_indexed(
        order, body, (c_arr,), (g, t), (NCHIP, nblk),
        [pl.BlockSpec((None, tr, C), lambda q, i, c_ref: (q, c_ref[0] * nblk + i, 0)), blk], [blk, blk],
        name=name,
        out_shape=[jax.ShapeDtypeStruct((NCHIP, h, C), f32), jax.ShapeDtypeStruct((NCHIP, h, C), bf16)],
        compiler_params=_params(("parallel", "parallel")),
    )


def _pair_add_gathered(order, dwt, t, c_arr, name):
    half_units = WIN_UNITS // 2
    table = jnp.asarray([_window_unit(q, j) for q in range(NCHIP) for j in range(WIN_UNITS)], jnp.int32)

    def body(tab_ref, c_ref, g_ref, t_ref, p32_ref, p16_ref):
        s = g_ref[...] + t_ref[...]
        p32_ref[...] = s
        p16_ref[...] = s.astype(bf16)

    blk = pl.BlockSpec((None, UNIT, D), lambda q, j, tab_ref, c_ref: (q, j, 0))
    return _call_indexed(
        order, body, (table, c_arr), (dwt, t), (NCHIP, half_units),
        [pl.BlockSpec((UNIT, D), lambda q, j, tab_ref, c_ref: (tab_ref[q * WIN_UNITS + c_ref[0] * half_units + j], 0)),
         blk], [blk, blk],
        name=name,
        out_shape=[jax.ShapeDtypeStruct((NCHIP, WIN_ROWS // 2, D), f32),
                   jax.ShapeDtypeStruct((NCHIP, WIN_ROWS // 2, D), bf16)],
        compiler_params=_params(("parallel", "parallel")),
    )


def _shard_copies(p, r, sm, ssem, rsem):
    x, y, c, chips = _place()
    n = len(p)
    sends, recvs = [], []
    for a in range(n):
        for j, (cx, cy) in enumerate(chips):
            k = a * 3 + j
            sends.append(_rcopy(p[a].at[2 * cx + cy], r[a].at[j], ssem.at[k], rsem.at[k], (cx, cy, c)))
            recvs.append(_rcopy(r[a].at[j], r[a].at[j], ssem.at[k], rsem.at[k], (cx, cy, c)))
    if sm is not None:
        mine = sm.at[4 * x + 2 * y + c]
        for i in range(1, 8):
            px = (1 - x) if i & 4 else x
            py = (1 - y) if i & 2 else y
            pc = (1 - c) if i & 1 else c
            k = 3 * n + i - 1
            sends.append(_rcopy(mine, mine, ssem.at[k], rsem.at[k], (px, py, pc)))
            slot = sm.at[4 * px + 2 * py + pc]
            recvs.append(_rcopy(slot, slot, ssem.at[k], rsem.at[k], (px, py, pc)))
    return sends, recvs


def _shard_start(name, p16s, order, sm=None):
    n = len(p16s)
    rs = [lax.empty((3,) + p.shape[1:], bf16) for p in p16s]
    extra = [] if sm is None else [sm]
    nsem = 3 * n + (7 if sm is not None else 0)

    def body(bufs, _, new):
        sends, _r = _shard_copies(bufs[:n], bufs[n:2 * n], bufs[2 * n] if extra else None, new[0], new[1])
        for cp in sends:
            cp.start()

    return _comm_call(name, body, list(p16s) + rs + extra, order, new_sems=(nsem, nsem))


def _shard_wait(name, bufs, sems, n, order):
    has_sm = len(bufs) > 2 * n

    def body(refs, taken, _):
        sends, recvs = _shard_copies(refs[:n], refs[n:2 * n], refs[2 * n] if has_sm else None, taken[0], taken[1])
        for cp in sends:
            cp.wait_send()
        for cp in recvs:
            cp.wait_recv()

    bufs = _comm_call(name, body, bufs, order, sems_in=sems)[1]
    return bufs[n:2 * n], (bufs[2 * n] if has_sm else None)


def _shard_sum(order, p32, r, q_arr, c_arr, name):
    _, h, C = p32.shape
    tr = _row_tile(h)
    nblk = h // tr

    def body(q_ref, c_ref, p_ref, r_ref, o_ref):
        s = p_ref[...]
        for j in range(3):
            s = s + r_ref[j].astype(f32)
        o_ref[...] = s

    return _call_indexed(
        order, body, (q_arr, c_arr), (p32, r), (nblk,),
        [pl.BlockSpec((None, tr, C), lambda i, q_ref, c_ref: (q_ref[0], i, 0)),
         pl.BlockSpec((3, tr, C), lambda i, q_ref, c_ref: (0, i, 0))],
        pl.BlockSpec((tr, C), lambda i, q_ref, c_ref: (c_ref[0] * nblk + i, 0)),
        name=name, out_shape=jax.ShapeDtypeStruct((2 * h, C), f32),
        compiler_params=_params(("parallel",)),
    )


def _swap_copies(full, ssem, rsem):
    x, y, c, _ = _place()
    sends, recvs = [], []
    for a in range(len(full)):
        mine = full[a].at[_half(full[a].shape[0], c)]
        sends.append(_rcopy(mine, mine, ssem.at[a], rsem.at[a], (x, y, 1 - c)))
        other = full[a].at[_half(full[a].shape[0], 1 - c)]
        recvs.append(_rcopy(other, other, ssem.at[a], rsem.at[a], (x, y, 1 - c)))
    return sends, recvs


def _swap_start(name, fulls, order):
    n = len(fulls)

    def body(bufs, _, new):
        for cp in _swap_copies(bufs, new[0], new[1])[0]:
            cp.start()

    return _comm_call(name, body, list(fulls), order, new_sems=(n, n))


def _swap_wait(name, fulls, sems, order):
    def body(refs, taken, _):
        sends, recvs = _swap_copies(refs, taken[0], taken[1])
        for cp in sends:
            cp.wait_send()
        for cp in recvs:
            cp.wait_recv()

    return _comm_call(name, body, fulls, order, sems_in=sems)[1]


def _small_sum(order, sm):
    def body(sm_ref, o_ref):
        s = sm_ref[0]
        for d in range(1, 8):
            s = s + sm_ref[d]
        o_ref[...] = s

    return _call(order, body, (sm,), name="small_grad_sum", out_shape=jax.ShapeDtypeStruct((SMALL_ROWS, D), f32))


def _adamw(order, w, g, m, v, name):
    R, C = w.shape
    tr = R if R <= 256 else next(t for t in (256, 1072) if R % t == 0)

    def body(w_ref, g_ref, m_ref, v_ref, d_ref, nm_ref, nv_ref):
        g_ = g_ref[...]
        m_ = ADAM_B1 * m_ref[...] + (1.0 - ADAM_B1) * g_
        v_ = ADAM_B2 * v_ref[...] + (1.0 - ADAM_B2) * (g_ * g_)
        m_hat = m_ / (1.0 - ADAM_B1 ** ADAM_STEP)
        v_hat = v_ / (1.0 - ADAM_B2 ** ADAM_STEP)
        d_ref[...] = -ADAM_LR * (m_hat / (jnp.sqrt(v_hat) + ADAM_EPS) + ADAM_WD * w_ref[...])
        nm_ref[...] = m_
        nv_ref[...] = v_

    assert R % tr == 0
    blk = pl.BlockSpec((tr, C), lambda i: (i, 0))
    return _call(
        order, body, (w, g, m, v), name=name, grid=(R // tr,), in_specs=[blk] * 4, out_specs=[blk] * 3,
        out_shape=[jax.ShapeDtypeStruct((R, C), f32)] * 3,
        compiler_params=_params(("parallel",)),
    )


def _feature_major(w):
    return jnp.transpose(w, (2, 0, 1)).reshape(SHARD_IN, D)


def _flat(wt):
    return wt.reshape(SHARD_IN * 8, 128)


def _unflat(a):
    return jnp.transpose(a.reshape(SHARD_IN, 1, D), (1, 2, 0))


def _window_of(wt, q):
    wb = wt.astype(bf16)
    off = jnp.asarray(OWN_ROW0, jnp.int32)[q]
    plain = lax.dynamic_update_slice(jnp.zeros((WIN_ROWS, D), bf16), wb, (off, 0))
    lo = jnp.pad(wb[0:62], ((2, WIN_ROWS - 64), (0, 0)))
    hi = jnp.pad(wb[70:SHARD_IN], ((64, WIN_ROWS - 64 - (SHARD_IN - 70)), (0, 0)))
    win = jnp.where(q == 1, lo + hi, plain)
    fa = jnp.pad(wb[62:70], ((0, FA_ROWS - 8), (0, 0)))
    return win, fa


def _own_rows(gwin, gfa, q):
    off = jnp.asarray(OWN_ROW0, jnp.int32)[q]
    plain = lax.dynamic_slice(gwin, (off, 0), (SHARD_IN, D))
    chip1 = (jnp.pad(gwin[2:64], ((0, SHARD_IN - 62), (0, 0))) + jnp.pad(gfa[0:8], ((62, SHARD_IN - 70), (0, 0)))
             + jnp.pad(gwin[64:64 + SHARD_IN - 70], ((70, 0), (0, 0))))
    return jnp.where(q == 1, chip1, plain)


def kernel(x, norm_attn_g, w_in, b_forget, w_branch_a, w_branch_b, w_out, norm_mlp_g, w_up, w_down, norm_final_g, loss_target, m_norm_attn_g, m_w_in, m_b_forget, m_w_branch_a, m_w_branch_b, m_w_out, m_norm_mlp_g, m_w_up, m_w_down, m_norm_final_g, v_norm_attn_g, v_w_in, v_b_forget, v_w_branch_a, v_w_branch_b, v_w_out, v_norm_mlp_g, v_w_up, v_w_down, v_norm_final_g):
    xi, yi, ci = lax.axis_index("x"), lax.axis_index("y"), lax.axis_index("c")
    q_me = 2 * xi + yi
    c_arr = jnp.reshape(ci, (1,)).astype(jnp.int32)
    q_arr = jnp.reshape(q_me, (1,)).astype(jnp.int32)
    x_, tgt = x[0], loss_target[0]

    names = ["w_branch_a", "w_branch_b", "w_out", "w_up", "w_down"]
    big = dict(zip(names, [w_branch_a[0], w_branch_b[0], w_out[0], w_up[0], w_down[0]]))
    ms = dict(zip(names, [m_w_branch_a[0], m_w_branch_b[0], m_w_out[0], m_w_up[0], m_w_down[0]]))
    vs = dict(zip(names, [v_w_branch_a[0], v_w_branch_b[0], v_w_out[0], v_w_up[0], v_w_down[0]]))
    grad, upd = {}, {}
    order = _Order()

    def run(fn, *args, **kw):
        return fn(order, *args, **kw)

    def own_slot(a):
        return lax.dynamic_update_slice(lax.empty((NCHIP,) + a.shape, a.dtype), a[None], (q_me, 0, 0))

    wt_own = _feature_major(w_in)
    win, fa_blk = _window_of(wt_own, q_me)
    stacks = [own_slot(win), own_slot(fa_blk)] + [own_slot(w.astype(bf16)) for w in big.values()]
    (sem_in, sem_rest), stacks = _allgather_start(stacks, [2, 5], order)
    sem_f, in_s = _allgather_forward("allgather_forward_in", stacks[0:2], sem_in, order)
    wins, fas = _allgather_finish("allgather_finish_in", in_s, sem_f, order)
    wt = run(_assemble_win, wins, fas)

    rope = _rope_tables()
    bpad = jnp.pad(b_forget, ((0, 0), (0, 120)))
    h1, qkvb, qkva, gates, fa = run(_norm_inproj, x_, norm_attn_g, wt, rope)
    F = run(_forget_cumsum, fa, bpad)
    oa, lsea = run(_fox_fwd, qkva, F)
    sem_f, rest = _allgather_forward("allgather_forward_rest", stacks[2:], sem_rest, order)
    ob, lseb = run(_dil_fwd, qkvb)
    was, wbs, wouts, wups, wdowns = _allgather_finish("allgather_finish_rest", rest, sem_f, order)
    wout = wouts.reshape(D, D)
    wdown = wdowns.reshape(DFF, D)
    ya, yb, mixed = run(_branch_mix, oa, ob, was, wbs, gates)
    x2, h2 = run(_outproj_norm, mixed, wout, x_, norm_mlp_g)
    u, a = run(_mlp_up, h2, wups)
    dx3, dx3b, dg3, loss_part = run(_mlp_down_loss, a, wdown, x2, norm_final_g.reshape(1, D), tgt)
    loss = lax.psum(loss_part[0, 0], ("x", "y", "c"))

    def reduce_to_shard(tag, group, p32s, bufs, sems):
        rs, _ = _shard_wait("shard_wait_" + tag, bufs, sems, len(group), order)
        fulls = [run(_shard_sum, p32s[i], rs[i], q_arr, c_arr, "shard_sum_" + nm) for i, nm in enumerate(group)]
        return _swap_start("swap_start_" + tag, fulls, order)

    def pair_sums(tag, group, bufs, sems):
        gs, ts = _pair_wait("pair_wait_" + tag, bufs, sems, order)
        return zip(*[run(_pair_add, gs[i], ts[i], c_arr, "pair_add_" + nm) for i, nm in enumerate(group)])

    def finish(tag, group, fulls, sems):
        fulls = _swap_wait("swap_wait_" + tag, fulls, sems, order)
        for nm, gfull in zip(group, fulls):
            grad[nm] = gfull
            upd[nm] = run(_adamw, big[nm], gfull, ms[nm], vs[nm], "adamw_" + nm)

    grp_a, grp_b, grp_c = ["w_down", "w_up"], ["w_out", "w_branch_a", "w_branch_b"], ["w_in", "w_in_fa"]
    du = run(_mlp_down_bwd, dx3b, wdown, u)
    dwdown = run(_mm, a, dx3b, "tn", f32, 1024, D, "wgrad_down")
    dwup = run(_mm, h2, du, "tn", f32, D, 1024, "wgrad_up", stack_cols=True)
    sem_pa, buf_pa = _pair_start("pair_start_a", [dwdown.reshape(NCHIP, DFF // NCHIP, D), dwup], order)
    dx2, dx2b, dg2 = run(_mlp_up_bwd, du, wups, x2, dx3, norm_mlp_g)
    p32_a, p16_a = pair_sums("a", grp_a, buf_pa, sem_pa)
    sem_sa, buf_sa = _shard_start("shard_start_a", p16_a, order)
    dya, dyb, dproj = run(_gate_bwd, dx2b, wout, gates, ya, yb)
    dwout = run(_mm, mixed, dx2b, "tn", f32, D, D, "wgrad_out")
    doa, dob = run(_branch_bwd, dya, dyb, was, wbs)
    dwas, dwbs = run(_branch_wgrad, oa, ob, dya, dyb)
    sem_pb, buf_pb = _pair_start("pair_start_b", [dwout.reshape(NCHIP, D // NCHIP, D), dwas, dwbs], order)
    dF, dproj = run(_fox_bwd, qkva, doa, oa, lsea, F, dproj)
    p32_b, p16_b = pair_sums("b", grp_b, buf_pb, sem_pb)
    sem_wa, fulls_a = reduce_to_shard("a", grp_a, p32_a, buf_sa, sem_sa)
    sem_sb, buf_sb = _shard_start("shard_start_b", p16_b, order)
    dbf, dproj = run(_forget_bwd, dF, fa, bpad, dproj)
    dproj = run(_dil_bwd, qkvb, dob, ob, lseb, rope, dproj)
    sem_wb, fulls_b = reduce_to_shard("b", grp_b, p32_b, buf_sb, sem_sb)
    finish("a", grp_a, fulls_a, sem_wa)
    dwt = run(_mm, dproj, h1, "tn", f32, 512, D, "wgrad_in")
    dwfa = jnp.broadcast_to(dwt[F_FA:F_FA + FA_ROWS][None], (NCHIP, FA_ROWS, D))
    sem_pc, buf_pc = _pair_start("pair_start_c", [dwt, dwfa], order, gathered=True)
    finish("b", grp_b, fulls_b, sem_wb)
    (dwt_c, dwfa_c), (t_in, t_fa) = _pair_wait("pair_wait_c", buf_pc, sem_pc, order, gathered=True)
    p32_in, p16_in = run(_pair_add_gathered, dwt_c, t_in, c_arr, "pair_add_w_in")
    p32_fa, p16_fa = run(_pair_add, dwfa_c, t_fa, c_arr, "pair_add_w_in_fa")
    sem_sc, buf_sc = _shard_start("shard_start_c", [p16_in, p16_fa], order)
    gx, dg1 = run(_inproj_bwd, dproj, wt, x_, dx2, norm_attn_g)
    small = jnp.concatenate([dg1, dg2, dg3, jnp.pad(dbf[:, 0:8], ((0, 0), (0, D - 8))),
                             jnp.zeros((SMALL_ROWS - 4, D), f32)], axis=0)
    sm = lax.dynamic_update_slice(lax.empty((8, SMALL_ROWS, D), f32), small[None],
                                  (4 * xi + 2 * yi + ci, 0, 0))
    sem_sm, buf_sm = _shard_start("small_start", [], order, sm)
    sem_wc, fulls_c = reduce_to_shard("c", grp_c, [p32_in, p32_fa], buf_sc, sem_sc)
    _, sm = _shard_wait("small_wait", buf_sm, sem_sm, 0, order)
    gsmall = run(_small_sum, sm)

    grad["norm_attn_g"], grad["norm_mlp_g"] = gsmall[0:1], gsmall[1:2]
    grad["norm_final_g"], grad["b_forget"] = gsmall[2:3], gsmall[3:4, 0:8]
    upd["norm_attn_g"] = run(_adamw, norm_attn_g, grad["norm_attn_g"], m_norm_attn_g, v_norm_attn_g, "adamw_g1")
    upd["norm_mlp_g"] = run(_adamw, norm_mlp_g, grad["norm_mlp_g"], m_norm_mlp_g, v_norm_mlp_g, "adamw_g2")
    upd["norm_final_g"] = run(_adamw, norm_final_g.reshape(1, D), grad["norm_final_g"],
                              m_norm_final_g.reshape(1, D), v_norm_final_g.reshape(1, D), "adamw_g3")
    upd["b_forget"] = run(_adamw, b_forget, grad["b_forget"], m_b_forget, v_b_forget, "adamw_bf")

    gwin, gfa = _swap_wait("swap_wait_c", fulls_c, sem_wc, order)
    g_in = _flat(_own_rows(gwin, gfa, q_me))
    upd_in = run(_adamw, _flat(wt_own), g_in, _flat(_feature_major(m_w_in)), _flat(_feature_major(v_w_in)),
                 "adamw_w_in")
    grad["w_in"] = _unflat(g_in)
    upd["w_in"] = [_unflat(t) for t in upd_in]

    order_out = ["norm_attn_g", "w_in", "b_forget", "w_branch_a", "w_branch_b", "w_out", "norm_mlp_g", "w_up",
                 "w_down", "norm_final_g"]
    shapes = dict(norm_attn_g=norm_attn_g.shape, w_in=w_in.shape, b_forget=b_forget.shape,
                  w_branch_a=w_branch_a.shape, w_branch_b=w_branch_b.shape, w_out=w_out.shape,
                  norm_mlp_g=norm_mlp_g.shape, w_up=w_up.shape, w_down=w_down.shape, norm_final_g=norm_final_g.shape)
    outs = [loss, gx.reshape(x.shape)]
    outs += [grad[nm].reshape(shapes[nm]) for nm in order_out]
    for k in range(3):
        outs += [upd[nm][k].reshape(shapes[nm]) for nm in order_out]
    return tuple(outs)
```

```python
import jax
import jax.numpy as jnp
from jax import lax
from jax.experimental import pallas as pl
from jax.experimental.pallas import tpu as pltpu

f32 = jnp.float32
bf16 = jnp.bfloat16

S = 2048
D = 1024
DFF = 4096
HD = 64
FOXW = 512
DILOUT = 256
DIL = (1, 4, 16)
BAND = 128
EPS = 1e-6
NEG = -1e30
ROPE_THETA = 500000.0
NCHIP = 4
TQ = 256

ADAM_LR, ADAM_B1, ADAM_B2, ADAM_EPS, ADAM_WD, ADAM_STEP = 0.001, 0.9, 0.999, 1e-08, 0.01, 10
VMEM_LIMIT = 56 * 1024 * 1024

UNIT = 64
NP = 6144
F_DIL, F_FOX, F_FA, F_G = 0, 2304, 3840, 4096
DIL_BLK, FOX_BLK = 1152, 384
WIN_UNITS, WIN_ROWS = 24, 1536
WIN_UNIT0 = (0, 23, 45, 68)
OWN_ROW0 = (0, 2, 60, 62)
SHARD_IN = 1474
FA_ROWS = 32


def _compact_to_internal():
    c2i = {}
    for p in range(2):
        for role in range(3):
            for g in range(3):
                for hh in range(2):
                    c2i[24 + 12 * role + 4 * g + 2 * p + hh] = 18 * p + 6 * role + 2 * g + hh
    for p in range(4):
        for role in range(3):
            for hh in range(2):
                c2i[8 * role + 2 * p + hh] = F_FOX // UNIT + 6 * p + 2 * role + hh
    for j in range(32):
        c2i[60 + j] = F_G // UNIT + j
    return c2i


C2I = _compact_to_internal()
OVERLAP_UNITS = (23, 45, 46, 68)


def _params(sem=None):
    return pltpu.CompilerParams(dimension_semantics=sem, vmem_limit_bytes=VMEM_LIMIT)


class _Order:
    def __init__(self):
        self.tok = None

    def mark(self, v):
        self.tok = v

    def token_for(self, args):
        return [] if self.tok is None or any(self.tok is a for a in args) else [self.tok]


def _call(order, body, args, in_specs=None, **kw):
    args = list(args)
    n_in = len(args)
    if in_specs is None:
        in_specs = [pl.BlockSpec(memory_space=pltpu.VMEM)] * n_in
    kern = body
    extra = order.token_for(args)
    if extra:
        in_specs = list(in_specs) + [pl.BlockSpec(memory_space=pl.ANY)]

        def kern(*refs):
            body(*refs[:n_in], *refs[n_in + 1:])

    out = pl.pallas_call(kern, in_specs=in_specs, **kw)(*args, *extra)
    order.mark(out[0] if isinstance(out, (tuple, list)) else out)
    return out


def _call_indexed(order, body, scalars, args, grid, in_specs, out_specs, **kw):
    args, in_specs = list(args), list(in_specs)
    n_front = len(scalars) + len(args)
    kern = body
    extra = order.token_for(args)
    if extra:
        in_specs.append(pl.BlockSpec(memory_space=pl.ANY))

        def kern(*refs):
            body(*refs[:n_front], *refs[n_front + 1:])

    out = pl.pallas_call(
        kern, grid_spec=pltpu.PrefetchScalarGridSpec(num_scalar_prefetch=len(scalars), grid=grid, in_specs=in_specs,
                                                     out_specs=out_specs), **kw)(*scalars, *args, *extra)
    order.mark(out[0] if isinstance(out, (tuple, list)) else out)
    return out


def _dot(a, b):
    return jnp.dot(a, b, preferred_element_type=f32)


def _dot_nt(a, b):
    return lax.dot_general(a, b, (((1,), (1,)), ((), ())), preferred_element_type=f32)


def _dot_tn(a, b):
    return lax.dot_general(a, b, (((0,), (0,)), ((), ())), preferred_element_type=f32)


def _split3(x):
    hi = x.astype(bf16)
    r1 = x - hi.astype(f32)
    mid = r1.astype(bf16)
    lo = (r1 - mid.astype(f32)).astype(bf16)
    return hi, mid, lo


def _rope_tables():
    half = 8
    inv_freq = jnp.power(jnp.float32(ROPE_THETA), -jnp.arange(half, dtype=f32) * 2.0 / 16)
    ang = jnp.arange(S).astype(f32)[:, None] * inv_freq[None, :]
    cos, sin = jnp.cos(ang), jnp.sin(ang)
    one = jnp.ones((S, HD - 16), f32)
    zero = jnp.zeros((S, HD - 16), f32)
    z8 = jnp.zeros((S, 8), f32)
    c = jnp.concatenate([cos, cos, one], axis=1)
    s1 = jnp.concatenate([-sin, z8, zero], axis=1)
    s2 = jnp.concatenate([z8, sin, zero], axis=1)
    return tuple(jnp.concatenate([t, t], axis=1) for t in (c, s1, s2))


def _mm(order, a, b, mode, out_dtype, tm, tn, name, stack_cols=False):
    if mode == "nn":
        (M, K), (_, N) = a.shape, b.shape
        a_spec = pl.BlockSpec((tm, K), lambda i, j: (i, 0))
        b_spec = pl.BlockSpec((K, tn), lambda i, j: (0, j))
        dot = _dot
    elif mode == "nt":
        (M, K), (N, _) = a.shape, b.shape
        a_spec = pl.BlockSpec((tm, K), lambda i, j: (i, 0))
        b_spec = pl.BlockSpec((tn, K), lambda i, j: (j, 0))
        dot = _dot_nt
    else:
        (K, M), (_, N) = a.shape, b.shape
        a_spec = pl.BlockSpec((K, tm), lambda i, j: (0, i))
        b_spec = pl.BlockSpec((K, tn), lambda i, j: (0, j))
        dot = _dot_tn

    def body(a_ref, b_ref, o_ref):
        o_ref[...] = dot(a_ref[...], b_ref[...]).astype(out_dtype)

    if stack_cols:
        assert tm == M
        out_spec = pl.BlockSpec((None, tm, tn), lambda i, j: (j, 0, 0))
        out_shape = jax.ShapeDtypeStruct((N // tn, M, tn), out_dtype)
    else:
        out_spec = pl.BlockSpec((tm, tn), lambda i, j: (i, j))
        out_shape = jax.ShapeDtypeStruct((M, N), out_dtype)
    return _call(
        order, body, (a, b), name=name, grid=(M // tm, N // tn), in_specs=[a_spec, b_spec],
        out_specs=out_spec, out_shape=out_shape,
        compiler_params=_params(("parallel", "parallel")),
    )


def _assemble_win(order, wins, fas):
    def body(win_ref, fa_ref, o_ref):
        q = pl.program_id(0)

        @pl.when(q == 0)
        def _():
            o_ref[...] = jnp.zeros_like(o_ref)

        for k in range(NCHIP):
            @pl.when(q == k)
            def _(k=k):
                for j in range(WIN_UNITS):
                    cu = WIN_UNIT0[k] + j
                    dst = pl.ds(C2I[cu] * UNIT, UNIT)
                    if cu in OVERLAP_UNITS:
                        o_ref[dst, :] += win_ref[j * UNIT:(j + 1) * UNIT, :]
                    else:
                        o_ref[dst, :] = win_ref[j * UNIT:(j + 1) * UNIT, :]
                if k == 1:
                    o_ref[F_FA:F_FA + FA_ROWS, :] = fa_ref[...]

    return _call(
        order, body, (wins, fas), name="assemble_w_in", grid=(NCHIP,),
        in_specs=[pl.BlockSpec((None, WIN_ROWS, D), lambda q: (q, 0, 0)),
                  pl.BlockSpec((None, FA_ROWS, D), lambda q: (1, 0, 0))],
        out_specs=pl.BlockSpec((NP, D), lambda q: (0, 0)),
        out_shape=jax.ShapeDtypeStruct((NP, D), bf16),
        compiler_params=_params(("arbitrary",)),
    )


def _norm_inproj(order, x, g1, wt, rope):
    tm = 256
    c_t, s1_t, s2_t = rope

    def body(x_ref, g_ref, w_ref, c_ref, s1_ref, s2_ref, h_ref, qkvb_ref, qkva_ref, gates_ref, fa_ref):
        xb = x_ref[...]
        r = lax.rsqrt(jnp.mean(xb * xb, axis=-1, keepdims=True) + EPS)
        h = ((xb * r) * g_ref[...]).astype(bf16)
        h_ref[...] = h
        c, s1, s2 = c_ref[...], s1_ref[...], s2_ref[...]
        for p in range(2):
            pb = _dot_nt(h, w_ref[F_DIL + p * DIL_BLK:F_DIL + (p + 1) * DIL_BLK, :])
            for ch in range(DIL_BLK // 128):
                pc = pb[:, ch * 128:(ch + 1) * 128]
                if ch < 6:
                    pc = pc * c + pltpu.roll(pc, 120, 1) * s1 + pltpu.roll(pc, 8, 1) * s2
                qkvb_ref[:, p * DIL_BLK + ch * 128:p * DIL_BLK + (ch + 1) * 128] = pc
        qkva_ref[...] = _dot_nt(h, w_ref[F_FOX:F_FA, :]).astype(bf16)
        fa_ref[...] = _dot_nt(h, w_ref[F_FA:F_FA + 128, :])
        gates_ref[...] = _dot_nt(h, w_ref[F_G:NP, :])

    row = lambda w: pl.BlockSpec((tm, w), lambda i: (i, 0))
    return _call(
        order, body, (x, g1, wt, c_t, s1_t, s2_t), name="norm_inproj", grid=(S // tm,),
        in_specs=[row(D), pl.BlockSpec((1, D), lambda i: (0, 0)), pl.BlockSpec((NP, D), lambda i: (0, 0)),
                  row(128), row(128), row(128)],
        out_specs=[row(D), row(2 * DIL_BLK), row(4 * FOX_BLK), row(2 * D), row(128)],
        out_shape=[jax.ShapeDtypeStruct((S, D), bf16), jax.ShapeDtypeStruct((S, 2 * DIL_BLK), f32),
                   jax.ShapeDtypeStruct((S, 4 * FOX_BLK), bf16), jax.ShapeDtypeStruct((S, 2 * D), f32),
                   jax.ShapeDtypeStruct((S, 128), f32)],
        compiler_params=_params(("parallel",)),
    )


def _forget_cumsum(order, fa, bpad):
    nb = S // TQ

    def body(fa_ref, b_ref, F_ref):
        rr = lax.broadcasted_iota(jnp.int32, (TQ, TQ), 0)
        cc = lax.broadcasted_iota(jnp.int32, (TQ, TQ), 1)
        tri = (rr >= cc).astype(bf16)
        lane = lax.broadcasted_iota(jnp.int32, (1, 128), 1)
        carry = jnp.zeros((1, 128), f32)
        for b in range(nb):
            z = fa_ref[b * TQ:(b + 1) * TQ, :] + b_ref[...]
            lf = jnp.minimum(z, 0.0) - jnp.log(1.0 + jnp.exp(-jnp.abs(z)))
            lf = jnp.where(lane < 8, lf, 0.0)
            hi, mid, lo = _split3(lf)
            fb = (_dot(tri, hi) + _dot(tri, mid)) + _dot(tri, lo) + carry
            F_ref[b * TQ:(b + 1) * TQ, :] = fb
            carry = fb[TQ - 1:TQ, :]

    return _call(
        order, body, (fa, bpad), name="forget_cumsum",
        out_shape=jax.ShapeDtypeStruct((S, 128), f32),
        compiler_params=_params(),
    )


def _head_masks():
    lane = lax.broadcasted_iota(jnp.int32, (1, 128), 1)
    return lane, (lane < HD, lane >= HD)


L_FT, L_ONE, L_LSE = 0, 3, 6
FOX_TQ, FOX_TK = 256, 512


def _set_lanes(x, lane, first, cols):
    for n, col in enumerate(cols):
        x = jnp.where(lane == first + n, col, x)
    return x


def _f32_parts(col):
    return [t.astype(f32) for t in _split3(col)]


def _fox_operands(qkv_ref, F_ref, lse_ref, qa, ka, p, rows):
    lane, hm = _head_masks()
    q = qkv_ref[rows, 0:128].astype(f32) * 0.125
    k = qkv_ref[rows, 128:256].astype(f32)
    Fb = F_ref[rows, :]
    for hh in (0, 1):
        free = (1 - hh) * HD
        fparts = _f32_parts(jnp.sum(jnp.where(lane == 2 * p + hh, Fb, 0.0), axis=1, keepdims=True))
        qcols = fparts + [1.0] * 3
        kcols = [1.0] * 3 + [-t for t in fparts]
        if lse_ref is not None:
            qcols += [-t for t in _f32_parts(lse_ref[rows, hh * HD:hh * HD + 1])]
            kcols += [1.0] * 3
        qa[hh, rows, :] = _set_lanes(jnp.where(hm[hh], q, 0.0), lane, free, qcols).astype(bf16)
        ka[hh, rows, :] = _set_lanes(k, lane, free, kcols).astype(bf16)


def _fox_fwd(order, qkva, F):
    tq, tk = FOX_TQ, FOX_TK

    def body(qkv_ref, F_ref, o_ref, lse_ref, qa, ka, vt):
        p = pl.program_id(0)
        keyi = lax.broadcasted_iota(jnp.int32, (tk, 1), 0)
        qryi = lax.broadcasted_iota(jnp.int32, (1, tq), 1)
        sub = lax.broadcasted_iota(jnp.int32, (128, 1), 0)

        def prep(i, c):
            rows = pl.ds(pl.multiple_of(i * tk, tk), tk)
            _fox_operands(qkv_ref, F_ref, None, qa, ka, p, rows)
            vt[i] = qkv_ref[rows, 256:384].astype(f32).T.astype(bf16)
            return c

        lax.fori_loop(0, S // tk, prep, 0)

        def qblock(i, c):
            r0 = pl.multiple_of(i * tq, tq)
            qh = [qa[hh, pl.ds(r0, tq), :] for hh in (0, 1)]

            def kv(jb, carry, masked):
                keys = pl.ds(pl.multiple_of(jb * tk, tk), tk)
                sts = [_dot_nt(ka[hh, keys, :], qh[hh]) for hh in (0, 1)]
                new = []
                for hh in (0, 1):
                    m, l, a = carry[3 * hh:3 * hh + 3]
                    st = sts[hh]
                    if masked:
                        st = jnp.where(jb * tk + keyi <= r0 + qryi, st, NEG)
                    mn = jnp.maximum(m, jnp.max(st, axis=0, keepdims=True))
                    al = jnp.exp(m - mn)
                    pt = jnp.exp(st - mn)
                    l = al * l + jnp.sum(pt, axis=0, keepdims=True)
                    a = al * a + _dot(vt[jb, hh * HD:(hh + 1) * HD, :], pt.astype(bf16))
                    new += [mn, l, a]
                return tuple(new)

            init = (jnp.full((1, tq), NEG, f32), jnp.zeros((1, tq), f32), jnp.zeros((HD, tq), f32)) * 2
            last = (r0 + tq - 1) // tk
            carry = lax.fori_loop(0, last, lambda j, cr: kv(j, cr, False), init)
            m0, l0, a0, m1, l1, a1 = kv(last, carry, True)
            ot = jnp.concatenate([a0 / l0, a1 / l1], axis=0)
            lt = jnp.where(sub < HD, m0 + jnp.log(l0), m1 + jnp.log(l1))
            o_ref[pl.ds(r0, tq), :] = ot.T.astype(bf16)
            lse_ref[pl.ds(r0, tq), :] = lt.T
            return c

        lax.fori_loop(0, S // tq, qblock, 0)

    pair = pl.BlockSpec((S, 128), lambda p: (0, p))
    return _call(
        order, body, (qkva, F), name="fox_fwd", grid=(4,),
        in_specs=[pl.BlockSpec((S, FOX_BLK), lambda p: (0, p)), pl.BlockSpec((S, 128), lambda p: (0, 0))],
        out_specs=[pair, pair],
        out_shape=[jax.ShapeDtypeStruct((S, FOXW), bf16), jax.ShapeDtypeStruct((S, FOXW), f32)],
        scratch_shapes=[pltpu.VMEM((2, S, 128), bf16)] * 2 + [pltpu.VMEM((S // tk, 128, tk), bf16)],
        compiler_params=_params(("parallel",)),
    )


def _permute_in(dst, src, r):
    L = S // r
    for rho in range(r):
        dst[rho * L:(rho + 1) * L, :] = src[pl.ds(rho, L, stride=r), :]


def _permute_out(dst, src, r):
    L = S // r
    for rho in range(r):
        dst[pl.ds(rho, L, stride=r), :] = src[rho * L:(rho + 1) * L, :]


def _band_geometry(bb, nbl):
    r0 = pl.multiple_of(bb * BAND, BAND)
    k0 = pl.multiple_of(jnp.maximum(bb - 1, 0) * BAND, BAND)
    sub0 = (bb - lax.rem(bb, nbl)) * BAND
    qi = r0 + lax.broadcasted_iota(jnp.int32, (BAND, 1), 0)
    ki = k0 + lax.broadcasted_iota(jnp.int32, (1, 2 * BAND), 1)
    diff = qi - ki
    valid = (diff >= 0) & (diff <= BAND) & (ki >= sub0)
    return r0, k0, valid


def _dil_views(ref):
    return [[ref.at[:, pl.ds((3 * role + g) * 128, 128)] for g in range(3)] for role in range(3)]


DIL_UNROLL = 4


def _dil_in_specs():
    return [pl.BlockSpec((S, 128), lambda p, k=k: (0, 9 * p + k)) for k in range(9)]


def _dil_fwd(order, qkvb):
    def body(*refs):
        q_refs, k_refs, v_refs = refs[0:3], refs[3:6], refs[6:9]
        ob_ref, lse_ref, qp, kp, vp, op, lp = refs[9:16]
        on, ln = refs[16:19], refs[19:22]
        _, hm = _head_masks()
        for g, r in enumerate(DIL):
            nbl = S // r // BAND
            if r == 1:
                qs_, ks_, vs_, od, ld = q_refs[g], k_refs[g], v_refs[g], on[g], ln[g]
            else:
                _permute_in(qp, q_refs[g], r)
                _permute_in(kp, k_refs[g], r)
                _permute_in(vp, v_refs[g], r)
                qs_, ks_, vs_, od, ld = qp, kp, vp, op, lp

            def blk(t, c, qs_=qs_, ks_=ks_, vs_=vs_, od=od, ld=ld, nbl=nbl):
                work = []
                for u in range(DIL_UNROLL):
                    r0, k0, valid = _band_geometry(DIL_UNROLL * t + u, nbl)
                    q = qs_[pl.ds(r0, BAND), :] * 0.125
                    kw = ks_[pl.ds(k0, 2 * BAND), :].astype(bf16)
                    vw = vs_[pl.ds(k0, 2 * BAND), :]
                    for hh in (0, 1):
                        qh = jnp.where(hm[hh], q, 0.0).astype(bf16)
                        work.append((u, hh, r0, valid, vw, _dot_nt(qh, kw)))
                o = [jnp.zeros((BAND, 128), f32)] * DIL_UNROLL
                lse = [jnp.zeros((BAND, 128), f32)] * DIL_UNROLL
                for u, hh, r0, valid, vw, s in work:
                    s = jnp.where(valid, s, NEG)
                    m = jnp.max(s, axis=1, keepdims=True)
                    pr = jnp.exp(s - m)
                    l = jnp.sum(pr, axis=1, keepdims=True)
                    vm = jnp.where(hm[hh], vw, 0.0).astype(bf16)
                    o[u] = o[u] + _dot((pr / l).astype(bf16), vm)
                    lse[u] = jnp.where(hm[hh], m + jnp.log(l), lse[u])
                    if hh == 1:
                        od[pl.ds(r0, BAND), :] = o[u]
                        ld[pl.ds(r0, BAND), :] = lse[u]
                return c

            lax.fori_loop(0, S // BAND // DIL_UNROLL, blk, 0)
            if r != 1:
                _permute_out(on[g], op, r)
                _permute_out(ln[g], lp, r)

        def combine(i, c):
            r0 = pl.multiple_of(i * TQ, TQ)
            ls = [ln[g][pl.ds(r0, TQ), :] for g in range(3)]
            mx = jnp.maximum(jnp.maximum(ls[0], ls[1]), ls[2])
            es = [jnp.exp(l - mx) for l in ls]
            tot = (es[0] + es[1]) + es[2]
            acc = (es[0] / tot) * on[0][pl.ds(r0, TQ), :]
            acc = acc + (es[1] / tot) * on[1][pl.ds(r0, TQ), :]
            acc = acc + (es[2] / tot) * on[2][pl.ds(r0, TQ), :]
            ob_ref[pl.ds(r0, TQ), :] = acc.astype(bf16)
            lse_ref[pl.ds(r0, TQ), :] = mx + jnp.log(tot)
            return c

        lax.fori_loop(0, S // TQ, combine, 0)

    out_blk = pl.BlockSpec((S, 128), lambda p: (0, p))
    return _call(
        order, body, [qkvb] * 9, name="dil_fwd", grid=(2,),
        in_specs=_dil_in_specs(), out_specs=[out_blk, out_blk],
        out_shape=[jax.ShapeDtypeStruct((S, DILOUT), bf16), jax.ShapeDtypeStruct((S, DILOUT), f32)],
        scratch_shapes=[pltpu.VMEM((S, 128), f32)] * 11,
        compiler_params=_params(("parallel",)),
    )


def _branch_mix(order, oa, ob, was, wbs, gates):
    tm = 512

    def body(oa_ref, ob_ref, wa_ref, wb_ref, g_ref, ya_ref, yb_ref, mix_ref):
        oa_b, ob_b = oa_ref[...], ob_ref[...]
        for q in range(NCHIP):
            cols = slice(q * 256, (q + 1) * 256)
            ya = _dot(oa_b, wa_ref[q])
            yb = _dot(ob_b, wb_ref[q])
            ya_ref[:, cols] = ya
            yb_ref[:, cols] = yb
            ga = g_ref[:, q * 256:(q + 1) * 256]
            gb = g_ref[:, D + q * 256:D + (q + 1) * 256]
            mix_ref[:, cols] = (jax.nn.sigmoid(ga) * ya + jax.nn.sigmoid(gb) * yb).astype(bf16)

    row = lambda w: pl.BlockSpec((tm, w), lambda i: (i, 0))
    full3 = lambda a: pl.BlockSpec(a.shape, lambda i: (0, 0, 0))
    return _call(
        order, body, (oa, ob, was, wbs, gates), name="branch_mix", grid=(S // tm,),
        in_specs=[row(FOXW), row(DILOUT), full3(was), full3(wbs), row(2 * D)],
        out_specs=[row(D), row(D), row(D)],
        out_shape=[jax.ShapeDtypeStruct((S, D), f32), jax.ShapeDtypeStruct((S, D), f32),
                   jax.ShapeDtypeStruct((S, D), bf16)],
        compiler_params=_params(("parallel",)),
    )


def _outproj_norm(order, mixed, wout, x, g2):
    tm = 512

    def body(m_ref, w_ref, x_ref, g_ref, x2_ref, h2_ref):
        x2 = x_ref[...] + _dot(m_ref[...], w_ref[...])
        x2_ref[...] = x2
        r = lax.rsqrt(jnp.mean(x2 * x2, axis=-1, keepdims=True) + EPS)
        h2_ref[...] = ((x2 * r) * g_ref[...]).astype(bf16)

    row = pl.BlockSpec((tm, D), lambda i: (i, 0))
    return _call(
        order, body, (mixed, wout, x, g2), name="outproj_norm", grid=(S // tm,),
        in_specs=[row, pl.BlockSpec((D, D), lambda i: (0, 0)), row, pl.BlockSpec((1, D), lambda i: (0, 0))],
        out_specs=[row, row],
        out_shape=[jax.ShapeDtypeStruct((S, D), f32), jax.ShapeDtypeStruct((S, D), bf16)],
        compiler_params=_params(("parallel",)),
    )


def _mlp_up(order, h2, wups):
    tm = 512

    def body(h_ref, w_ref, u_ref, a_ref):
        u = _dot(h_ref[...], w_ref[...])
        u_ref[...] = u
        ru = jnp.maximum(u, 0.0)
        a_ref[...] = (ru * ru).astype(bf16)

    out = pl.BlockSpec((tm, D), lambda q, i: (i, q))
    return _call(
        order, body, (h2, wups), name="mlp_up", grid=(NCHIP, S // tm),
        in_specs=[pl.BlockSpec((tm, D), lambda q, i: (i, 0)), pl.BlockSpec((None, D, D), lambda q, i: (q, 0, 0))],
        out_specs=[out, out],
        out_shape=[jax.ShapeDtypeStruct((S, DFF), f32), jax.ShapeDtypeStruct((S, DFF), bf16)],
        compiler_params=_params(("parallel", "parallel")),
    )


def _mlp_down_loss(order, a, wdown, x2, g3, tgt):
    tm = 256

    def body(a_ref, w_ref, x2_ref, g_ref, t_ref, dx_ref, dxb_ref, dg_ref, loss_ref):
        i = pl.program_id(0)
        x3 = x2_ref[...] + _dot(a_ref[...], w_ref[...])
        r = lax.rsqrt(jnp.mean(x3 * x3, axis=-1, keepdims=True) + EPS)
        xh = x3 * r
        g = g_ref[...]
        e = xh * g - t_ref[...]
        part = 0.5 * jnp.sum(jnp.mean(e * e, axis=-1, keepdims=True), axis=0, keepdims=True)
        dy = e * (1.0 / D)
        gdy = dy * g
        dx = r * (gdy - xh * jnp.mean(gdy * xh, axis=-1, keepdims=True))
        dx_ref[...] = dx
        dxb_ref[...] = dx.astype(bf16)

        @pl.when(i == 0)
        def _():
            dg_ref[...] = jnp.zeros_like(dg_ref)
            loss_ref[...] = jnp.zeros_like(loss_ref)

        dg_ref[...] += jnp.sum(dy * xh, axis=0, keepdims=True)
        loss_ref[...] += jnp.broadcast_to(part, (1, 128))

    row = pl.BlockSpec((tm, D), lambda i: (i, 0))
    vec = pl.BlockSpec((1, D), lambda i: (0, 0))
    return _call(
        order, body, (a, wdown, x2, g3, tgt), name="mlp_down_loss", grid=(S // tm,),
        in_specs=[pl.BlockSpec((tm, DFF), lambda i: (i, 0)), pl.BlockSpec((DFF, D), lambda i: (0, 0)), row, vec, row],
        out_specs=[row, row, vec, pl.BlockSpec((1, 128), lambda i: (0, 0))],
        out_shape=[jax.ShapeDtypeStruct((S, D), f32), jax.ShapeDtypeStruct((S, D), bf16),
                   jax.ShapeDtypeStruct((1, D), f32), jax.ShapeDtypeStruct((1, 128), f32)],
        compiler_params=_params(("arbitrary",)),
    )


def _mlp_down_bwd(order, dx3b, wdown, u):
    tm = 256

    def body(d_ref, w_ref, u_ref, du_ref):
        d = d_ref[...]
        for q in range(NCHIP):
            cols = slice(q * D, (q + 1) * D)
            da = _dot_nt(d, w_ref[cols, :])
            du_ref[:, cols] = (da * (2.0 * jnp.maximum(u_ref[:, cols], 0.0))).astype(bf16)

    return _call(
        order, body, (dx3b, wdown, u), name="mlp_down_bwd", grid=(S // tm,),
        in_specs=[pl.BlockSpec((tm, D), lambda i: (i, 0)), pl.BlockSpec((DFF, D), lambda i: (0, 0)),
                  pl.BlockSpec((tm, DFF), lambda i: (i, 0))],
        out_specs=pl.BlockSpec((tm, DFF), lambda i: (i, 0)),
        out_shape=jax.ShapeDtypeStruct((S, DFF), bf16),
        compiler_params=_params(("parallel",)),
    )


def _mlp_up_bwd(order, du, wups, x2, dx3, g2):
    tm = 256

    def body(du_ref, w_ref, x2_ref, dx3_ref, g_ref, dx2_ref, dx2b_ref, dg_ref):
        i = pl.program_id(0)
        dh = jnp.zeros((tm, D), f32)
        for q in range(NCHIP):
            dh = dh + _dot_nt(du_ref[:, q * D:(q + 1) * D], w_ref[q])
        x2 = x2_ref[...]
        r = lax.rsqrt(jnp.mean(x2 * x2, axis=-1, keepdims=True) + EPS)
        xh = x2 * r
        gdh = dh * g_ref[...]
        dx2 = dx3_ref[...] + r * (gdh - xh * jnp.mean(gdh * xh, axis=-1, keepdims=True))
        dx2_ref[...] = dx2
        dx2b_ref[...] = dx2.astype(bf16)

        @pl.when(i == 0)
        def _():
            dg_ref[...] = jnp.zeros_like(dg_ref)

        dg_ref[...] += jnp.sum(dh * xh, axis=0, keepdims=True)

    row = pl.BlockSpec((tm, D), lambda i: (i, 0))
    vec = pl.BlockSpec((1, D), lambda i: (0, 0))
    return _call(
        order, body, (du, wups, x2, dx3, g2), name="mlp_up_bwd", grid=(S // tm,),
        in_specs=[pl.BlockSpec((tm, DFF), lambda i: (i, 0)), pl.BlockSpec((NCHIP, D, D), lambda i: (0, 0, 0)),
                  row, row, vec],
        out_specs=[row, row, vec],
        out_shape=[jax.ShapeDtypeStruct((S, D), f32), jax.ShapeDtypeStruct((S, D), bf16),
                   jax.ShapeDtypeStruct((1, D), f32)],
        compiler_params=_params(("arbitrary",)),
    )


def _gate_bwd(order, dx2b, wout, gates, ya, yb):
    tm = 256

    def body(d_ref, w_ref, g_ref, ya_ref, yb_ref, dya_ref, dyb_ref, dproj_ref):
        dm = _dot_nt(d_ref[...], w_ref[...])
        sa = jax.nn.sigmoid(g_ref[:, 0:D])
        sb = jax.nn.sigmoid(g_ref[:, D:2 * D])
        dya_ref[...] = (dm * sa).astype(bf16)
        dyb_ref[...] = (dm * sb).astype(bf16)
        dproj_ref[:, 0:D] = (dm * ya_ref[...] * (sa * (1.0 - sa))).astype(bf16)
        dproj_ref[:, D:2 * D] = (dm * yb_ref[...] * (sb * (1.0 - sb))).astype(bf16)

    row = lambda w: pl.BlockSpec((tm, w), lambda i: (i, 0))
    return _call(
        order, body, (dx2b, wout, gates, ya, yb), name="gate_bwd", grid=(S // tm,),
        in_specs=[row(D), pl.BlockSpec((D, D), lambda i: (0, 0)), row(2 * D), row(D), row(D)],
        out_specs=[row(D), row(D), pl.BlockSpec((tm, 2 * D), lambda i: (i, F_G // (2 * D)))],
        out_shape=[jax.ShapeDtypeStruct((S, D), bf16), jax.ShapeDtypeStruct((S, D), bf16),
                   jax.ShapeDtypeStruct((S, NP), bf16)],
        compiler_params=_params(("parallel",)),
    )


def _branch_bwd(order, dya, dyb, was, wbs):
    tm = 512

    def body(dya_ref, dyb_ref, wa_ref, wb_ref, doa_ref, dob_ref):
        doa = jnp.zeros((tm, FOXW), f32)
        dob = jnp.zeros((tm, DILOUT), f32)
        for q in range(NCHIP):
            cols = slice(q * 256, (q + 1) * 256)
            doa = doa + _dot_nt(dya_ref[:, cols], wa_ref[q])
            dob = dob + _dot_nt(dyb_ref[:, cols], wb_ref[q])
        doa_ref[...] = doa.astype(bf16)
        dob_ref[...] = dob

    row = lambda w: pl.BlockSpec((tm, w), lambda i: (i, 0))
    full3 = lambda a: pl.BlockSpec(a.shape, lambda i: (0, 0, 0))
    return _call(
        order, body, (dya, dyb, was, wbs), name="branch_bwd", grid=(S // tm,),
        in_specs=[row(D), row(D), full3(was), full3(wbs)],
        out_specs=[row(FOXW), row(DILOUT)],
        out_shape=[jax.ShapeDtypeStruct((S, FOXW), bf16), jax.ShapeDtypeStruct((S, DILOUT), f32)],
        compiler_params=_params(("parallel",)),
    )


def _branch_wgrad(order, oa, ob, dya, dyb):
    def body(oa_ref, ob_ref, dya_ref, dyb_ref, dwa_ref, dwb_ref):
        dwa_ref[...] = _dot_tn(oa_ref[...], dya_ref[...])
        dwb_ref[...] = _dot_tn(ob_ref[...], dyb_ref[...])

    full = lambda w: pl.BlockSpec((S, w), lambda q: (0, 0))
    colq = pl.BlockSpec((S, 256), lambda q: (0, q))
    return _call(
        order, body, (oa, ob, dya, dyb), name="branch_wgrad", grid=(NCHIP,),
        in_specs=[full(FOXW), full(DILOUT), colq, colq],
        out_specs=[pl.BlockSpec((None, FOXW, 256), lambda q: (q, 0, 0)),
                   pl.BlockSpec((None, DILOUT, 256), lambda q: (q, 0, 0))],
        out_shape=[jax.ShapeDtypeStruct((NCHIP, FOXW, 256), f32), jax.ShapeDtypeStruct((NCHIP, DILOUT, 256), f32)],
        compiler_params=_params(("parallel",)),
    )


def _fox_bwd(order, qkva, doa, oa, lse, F, dproj):
    tq, tk = FOX_TQ, FOX_TK

    def body(qkv_ref, do_ref, o_ref, lse_ref, F_ref, _dproj_in, dF_ref, dqkv_ref, qa, ka, da, va, kat,
             dk_scr, dv_scr, dqt_scr):
        p = pl.program_id(0)
        lane, hm = _head_masks()
        keyi = lax.broadcasted_iota(jnp.int32, (tk, 1), 0)
        qryi = lax.broadcasted_iota(jnp.int32, (1, tq), 1)

        def prep(i, c):
            rows = pl.ds(pl.multiple_of(i * tk, tk), tk)
            _fox_operands(qkv_ref, F_ref, lse_ref, qa, ka, p, rows)
            do = do_ref[rows, :].astype(f32)
            prod = do * o_ref[rows, :].astype(f32)
            v = qkv_ref[rows, 256:384].astype(f32)
            for hh in (0, 1):
                free = (1 - hh) * HD
                delta = jnp.sum(jnp.where(hm[hh], prod, 0.0), axis=1, keepdims=True)
                da[hh, rows, :] = _set_lanes(jnp.where(hm[hh], do, 0.0), lane, free,
                                             [-t for t in _f32_parts(delta)]).astype(bf16)
                va[hh, rows, :] = _set_lanes(v, lane, free, [1.0] * 3).astype(bf16)
                kat[hh, i] = ka[hh, rows, :].astype(f32).T.astype(bf16)
                dk_scr[hh, rows, :] = jnp.zeros((tk, 128), f32)
                dv_scr[hh, rows, :] = jnp.zeros((tk, 128), f32)
            return c

        lax.fori_loop(0, S // tk, prep, 0)

        def qblock(i, c):
            r0 = pl.multiple_of(i * tq, tq)
            qrows = pl.ds(r0, tq)
            qh = [qa[hh, qrows, :] for hh in (0, 1)]
            dh = [da[hh, qrows, :] for hh in (0, 1)]
            dqt_scr[...] = jnp.zeros_like(dqt_scr)

            def kv(jb, c2, masked):
                keys = pl.ds(pl.multiple_of(jb * tk, tk), tk)
                sts = [_dot_nt(ka[hh, keys, :], qh[hh]) for hh in (0, 1)]
                dps = [_dot_nt(va[hh, keys, :], dh[hh]) for hh in (0, 1)]
                for hh in (0, 1):
                    pt = jnp.exp(sts[hh])
                    if masked:
                        pt = jnp.where(jb * tk + keyi <= r0 + qryi, pt, 0.0)
                    dsb = (pt * dps[hh]).astype(bf16)
                    dv_scr[hh, keys, :] += _dot(pt.astype(bf16), dh[hh])
                    dk_scr[hh, keys, :] += _dot(dsb, qh[hh])
                    dqt_scr[hh] += _dot(kat[hh, jb], dsb)
                return c2

            last = (r0 + tq - 1) // tk
            lax.fori_loop(0, last, lambda j, c2: kv(j, c2, False), 0)
            kv(last, 0, True)
            dq0, dq1 = dqt_scr[0].T, dqt_scr[1].T
            dqkv_ref[qrows, 0:128] = (jnp.where(hm[0], dq0, dq1) * 0.125).astype(bf16)
            dF_ref[qrows, :] = jnp.where(lane == 0, dq0[:, HD:HD + 1], jnp.where(lane == 1, dq1[:, 0:1], 0.0))
            return c

        lax.fori_loop(0, S // tq, qblock, 0)

        def finish(i, c):
            rows = pl.ds(pl.multiple_of(i * tq, tq), tq)
            dk0, dk1 = dk_scr[0, rows, :], dk_scr[1, rows, :]
            dqkv_ref[rows, 128:256] = jnp.where(hm[0], dk0, dk1).astype(bf16)
            dqkv_ref[rows, 256:384] = jnp.where(hm[0], dv_scr[0, rows, :], dv_scr[1, rows, :]).astype(bf16)
            cs = jnp.where(lane == 0, dk0[:, HD + L_ONE:HD + L_ONE + 1],
                           jnp.where(lane == 1, dk1[:, L_ONE:L_ONE + 1], 0.0))
            dF_ref[rows, :] = dF_ref[rows, :] - cs
            return c

        lax.fori_loop(0, S // tq, finish, 0)

    pair = pl.BlockSpec((S, 128), lambda p: (0, p))
    return _call(
        order, body, (qkva, doa, oa, lse, F, dproj), name="fox_bwd", grid=(4,),
        in_specs=[pl.BlockSpec((S, FOX_BLK), lambda p: (0, p)), pair, pair, pair,
                  pl.BlockSpec((S, 128), lambda p: (0, 0)), pl.BlockSpec(memory_space=pl.ANY)],
        out_specs=[pair, pl.BlockSpec((S, FOX_BLK), lambda p: (0, F_FOX // FOX_BLK + p))],
        out_shape=[jax.ShapeDtypeStruct((S, FOXW), f32), jax.ShapeDtypeStruct((S, NP), bf16)],
        input_output_aliases={5: 1},
        scratch_shapes=[pltpu.VMEM((2, S, 128), bf16)] * 4 + [pltpu.VMEM((2, S // tk, 128, tk), bf16)]
        + [pltpu.VMEM((2, S, 128), f32)] * 2 + [pltpu.VMEM((2, 128, tq), f32)],
        compiler_params=_params(("parallel",)),
    )


def _forget_bwd(order, dF, fa, bpad, dproj):
    nb = S // TQ

    def body(dF_ref, fa_ref, b_ref, _dproj_in, db_ref, dfa_ref):
        rr = lax.broadcasted_iota(jnp.int32, (TQ, TQ), 0)
        cc = lax.broadcasted_iota(jnp.int32, (TQ, TQ), 1)
        upper = (cc >= rr).astype(bf16)
        lane = lax.broadcasted_iota(jnp.int32, (1, 128), 1)
        carry = jnp.zeros((1, 128), f32)
        db = jnp.zeros((1, 128), f32)
        for b in reversed(range(nb)):
            cols = jnp.zeros((TQ, 128), f32)
            for h in range(8):
                c0 = (h // 2) * 128 + h % 2
                cols = jnp.where(lane == h, dF_ref[b * TQ:(b + 1) * TQ, c0:c0 + 1], cols)
            dlf = carry
            for part in _split3(cols):
                dlf = dlf + _dot(upper, part)
            carry = carry + jnp.sum(cols, axis=0, keepdims=True)
            z = fa_ref[b * TQ:(b + 1) * TQ, :] + b_ref[...]
            dz = jnp.where(lane < 8, dlf * jax.nn.sigmoid(-z), 0.0)
            dfa_ref[b * TQ:(b + 1) * TQ, 0:128] = dz.astype(bf16)
            dfa_ref[b * TQ:(b + 1) * TQ, 128:256] = jnp.zeros((TQ, 128), bf16)
            db = db + jnp.sum(dz, axis=0, keepdims=True)
        db_ref[...] = db

    whole = lambda a: pl.BlockSpec(a.shape, lambda i: (0,) * a.ndim)
    return _call(
        order, body, (dF, fa, bpad, dproj), name="forget_bwd", grid=(1,),
        in_specs=[whole(dF), whole(fa), whole(bpad), pl.BlockSpec(memory_space=pl.ANY)],
        out_specs=[pl.BlockSpec((1, 128), lambda i: (0, 0)), pl.BlockSpec((S, 256), lambda i: (0, F_FA // 256))],
        out_shape=[jax.ShapeDtypeStruct((1, 128), f32), jax.ShapeDtypeStruct((S, NP), bf16)],
        input_output_aliases={3: 1},
        compiler_params=_params(("arbitrary",)),
    )


def _dil_bwd(order, qkvb, dob, ob, lseb, rope, dproj):
    c_t, s1_t, s2_t = rope

    def body(*refs):
        q_refs, k_refs, v_refs = refs[0:3], refs[3:6], refs[6:9]
        dob_ref, ob_ref, lse_ref, c_ref, s1_ref, s2_ref, _dproj_in, dqkv_ref = refs[9:17]
        qp, kp, vp, dop, lp, dlp, dln, dqp, dkp, dvp, nat = refs[17:28]
        dq_out, dk_out, dv_out = _dil_views(dqkv_ref)
        _, hm = _head_masks()

        def delta_rows(i, c):
            r0 = pl.multiple_of(i * TQ, TQ)
            prod = dob_ref[pl.ds(r0, TQ), :] * ob_ref[pl.ds(r0, TQ), :].astype(f32)
            d0 = jnp.sum(jnp.where(hm[0], prod, 0.0), axis=1, keepdims=True)
            d1 = jnp.sum(jnp.where(hm[1], prod, 0.0), axis=1, keepdims=True)
            dln[pl.ds(r0, TQ), :] = jnp.where(hm[0], d0, d1)
            return c

        lax.fori_loop(0, S // TQ, delta_rows, 0)

        for g, r in enumerate(DIL):
            nbl = S // r // BAND
            if r == 1:
                srcs = (q_refs[g], k_refs[g], v_refs[g], dob_ref, lse_ref, dln)
            else:
                for dst, src in ((qp, q_refs[g]), (kp, k_refs[g]), (vp, v_refs[g]), (dop, dob_ref),
                                 (lp, lse_ref), (dlp, dln)):
                    _permute_in(dst, src, r)
                srcs = (qp, kp, vp, dop, lp, dlp)
            dkp[...] = jnp.zeros_like(dkp)
            dvp[...] = jnp.zeros_like(dvp)

            def blk(t, c, srcs=srcs, nbl=nbl):
                qs_, ks_, vs_, dos_, ls_, dls_ = srcs
                work = []
                for u in range(DIL_UNROLL):
                    r0, k0, valid = _band_geometry(DIL_UNROLL * t + u, nbl)
                    q = qs_[pl.ds(r0, BAND), :] * 0.125
                    kwf = ks_[pl.ds(k0, 2 * BAND), :]
                    kw = kwf.astype(bf16)
                    vw = vs_[pl.ds(k0, 2 * BAND), :].astype(bf16)
                    do = dos_[pl.ds(r0, BAND), :]
                    lse = ls_[pl.ds(r0, BAND), :]
                    dlt = dls_[pl.ds(r0, BAND), :]
                    for hh in (0, 1):
                        qh = jnp.where(hm[hh], q, 0.0).astype(bf16)
                        doh = jnp.where(hm[hh], do, 0.0).astype(bf16)
                        kh = jnp.where(hm[hh], kwf, 0.0).astype(bf16)
                        work.append((u, hh, r0, k0, valid, qh, doh, kh, lse[:, hh * HD:hh * HD + 1],
                                     dlt[:, hh * HD:hh * HD + 1], _dot_nt(qh, kw), _dot_nt(doh, vw)))
                for u, hh, r0, k0, valid, qh, doh, kh, lse_h, dlt_h, s, dp in work:
                    if hh == 0:
                        dq = jnp.zeros((BAND, 128), f32)
                        dk = jnp.zeros((2 * BAND, 128), f32)
                        dv = jnp.zeros((2 * BAND, 128), f32)
                    pr = jnp.where(valid, jnp.exp(s - lse_h), 0.0)
                    dsb = (pr * (dp - dlt_h)).astype(bf16)
                    dv = dv + _dot_tn(pr.astype(bf16), doh)
                    dk = dk + _dot_tn(dsb, qh)
                    dq = dq + _dot(dsb, kh)
                    if hh == 1:
                        dqp[pl.ds(r0, BAND), :] = dq * 0.125
                        dkp[pl.ds(k0, 2 * BAND), :] += dk
                        dvp[pl.ds(k0, 2 * BAND), :] += dv
                return c

            lax.fori_loop(0, S // BAND // DIL_UNROLL, blk, 0)

            for acc, out, roped in ((dqp, dq_out[g], True), (dkp, dk_out[g], True), (dvp, dv_out[g], False)):
                if r == 1:
                    src = acc
                else:
                    _permute_out(nat, acc, r)
                    src = nat

                def emit(i, c, src=src, out=out, roped=roped):
                    r0 = pl.multiple_of(i * TQ, TQ)
                    d = src[pl.ds(r0, TQ), :]
                    if roped:
                        d = (d * c_ref[pl.ds(r0, TQ), :] + pltpu.roll(d * s1_ref[pl.ds(r0, TQ), :], 8, 1)
                             + pltpu.roll(d * s2_ref[pl.ds(r0, TQ), :], 120, 1))
                    out[pl.ds(r0, TQ), :] = d.astype(bf16)
                    return c

                lax.fori_loop(0, S // TQ, emit, 0)

    pair = pl.BlockSpec((S, 128), lambda p: (0, p))
    tab = pl.BlockSpec((S, 128), lambda p: (0, 0))
    blk_spec = pl.BlockSpec((S, DIL_BLK), lambda p: (0, p))
    return _call(
        order, body, [qkvb] * 9 + [dob, ob, lseb, c_t, s1_t, s2_t, dproj], name="dil_bwd", grid=(2,),
        in_specs=_dil_in_specs() + [pair, pair, pair, tab, tab, tab, pl.BlockSpec(memory_space=pl.ANY)],
        out_specs=blk_spec,
        out_shape=jax.ShapeDtypeStruct((S, NP), bf16),
        input_output_aliases={15: 0},
        scratch_shapes=[pltpu.VMEM((S, 128), f32)] * 11,
        compiler_params=_params(("parallel",)),
    )


def _inproj_bwd(order, dproj, wt, x, dx2, g1):
    tm = 256

    def body(d_ref, w_ref, x_ref, dx2_ref, g_ref, dx_ref, dg_ref):
        i = pl.program_id(0)
        dh = _dot(d_ref[...], w_ref[...])
        xb = x_ref[...]
        r = lax.rsqrt(jnp.mean(xb * xb, axis=-1, keepdims=True) + EPS)
        xh = xb * r
        gdh = dh * g_ref[...]
        dx_ref[...] = dx2_ref[...] + r * (gdh - xh * jnp.mean(gdh * xh, axis=-1, keepdims=True))

        @pl.when(i == 0)
        def _():
            dg_ref[...] = jnp.zeros_like(dg_ref)

        dg_ref[...] += jnp.sum(dh * xh, axis=0, keepdims=True)

    row = pl.BlockSpec((tm, D), lambda i: (i, 0))
    vec = pl.BlockSpec((1, D), lambda i: (0, 0))
    return _call(
        order, body, (dproj, wt, x, dx2, g1), name="inproj_bwd", grid=(S // tm,),
        in_specs=[pl.BlockSpec((tm, NP), lambda i: (i, 0)), pl.BlockSpec((NP, D), lambda i: (0, 0)), row, row, vec],
        out_specs=[row, vec],
        out_shape=[jax.ShapeDtypeStruct((S, D), f32), jax.ShapeDtypeStruct((1, D), f32)],
        compiler_params=_params(("arbitrary",)),
    )


HBM = pl.BlockSpec(memory_space=pltpu.HBM)
SEM = pl.BlockSpec(memory_space=pltpu.SEMAPHORE)
SMALL_ROWS = 8


def _comm_call(name, body, bufs, order, sems_in=(), new_sems=()):
    nb, ns, nn = len(bufs), len(sems_in), len(new_sems)
    extra = order.token_for(bufs)

    def kern(*refs):
        off = nb + ns + len(extra)
        body(refs[:nb], refs[nb:nb + ns], refs[off:off + nn])
        refs[-1][...] = jnp.zeros((8, 128), f32)

    res = pl.pallas_call(
        kern, name=name,
        in_specs=[HBM] * nb + [SEM] * ns + [pl.BlockSpec(memory_space=pl.ANY)] * len(extra),
        out_specs=[SEM] * nn + [HBM] * nb + [pl.BlockSpec(memory_space=pltpu.VMEM)],
        out_shape=[pltpu.SemaphoreType.DMA((k,)) for k in new_sems] + [pltpu.HBM(b.shape, b.dtype) for b in bufs]
        + [jax.ShapeDtypeStruct((8, 128), f32)],
        input_output_aliases={i: nn + i for i in range(nb)},
        compiler_params=pltpu.CompilerParams(has_side_effects=pltpu.SideEffectType.DATAFLOW_SIDE_EFFECTING),
    )(*[pltpu.with_memory_space_constraint(b, pltpu.HBM) for b in bufs], *sems_in, *extra)
    order.mark(res[-1])
    return list(res[:nn]), list(res[nn:nn + nb])


def _place():
    x, y, c = lax.axis_index("x"), lax.axis_index("y"), lax.axis_index("c")
    chips = [(1 - x, y), (x, 1 - y), (1 - x, 1 - y)]
    return x, y, c, chips


def _rcopy(src, dst, ssem, rsem, dev):
    return pltpu.make_async_remote_copy(src_ref=src, dst_ref=dst, send_sem=ssem, recv_sem=rsem,
                                        device_id=dev, device_id_type=pl.DeviceIdType.MESH)


def _half(nrows, which):
    return pl.ds(which * (nrows // 2), nrows // 2)


def _ici_copies(stack, group_sizes, ssems, rsems):
    x, y, c, chips = _place()
    me_q = 2 * x + y
    sends, recvs = [], []
    a = 0
    for grp, size in enumerate(group_sizes):
        for k in range(size):
            rows = _half(stack[a].shape[1], c)
            for j, (cx, cy) in enumerate(chips):
                mine = stack[a].at[me_q, rows]
                sends.append(_rcopy(mine, mine, ssems[grp].at[k * 3 + j], rsems[grp].at[k * 3 + j], (cx, cy, c)))
                theirs = stack[a].at[2 * cx + cy, rows]
                recvs.append(_rcopy(theirs, theirs, ssems[grp].at[k * 3 + j], rsems[grp].at[k * 3 + j],
                                    (cx, cy, c)))
            a += 1
    return sends, recvs


def _allgather_start(stacks, group_sizes, order):
    def body(bufs, _, new):
        sends, _r = _ici_copies(bufs, group_sizes, new[0::2], new[1::2])
        for cp in sends:
            cp.start()

    sizes = []
    for size in group_sizes:
        sizes += [3 * size, 3 * size]
    sems, stacks = _comm_call("allgather_start", body, stacks, order, new_sems=sizes)
    return [(sems[2 * g], sems[2 * g + 1]) for g in range(len(group_sizes))], stacks


def _forward_copies(stack, ssem, rsem):
    x, y, c, chips = _place()
    sib = (x, y, 1 - c)
    sends, recvs = [], []
    for a in range(len(stack)):
        for j, (cx, cy) in enumerate(chips):
            landed = stack[a].at[2 * cx + cy, _half(stack[a].shape[1], c)]
            sends.append(_rcopy(landed, landed, ssem.at[a * 3 + j], rsem.at[a * 3 + j], sib))
            other = stack[a].at[2 * cx + cy, _half(stack[a].shape[1], 1 - c)]
            recvs.append(_rcopy(other, other, ssem.at[a * 3 + j], rsem.at[a * 3 + j], sib))
    return sends, recvs


def _allgather_forward(name, stacks, sems, order):
    n = len(stacks)

    def body(bufs, taken, new):
        sends, recvs = _ici_copies(bufs, [n], [taken[0]], [taken[1]])
        for cp in sends:
            cp.wait_send()
        for cp in recvs:
            cp.wait_recv()
        fwd, _r = _forward_copies(bufs, new[0], new[1])
        for cp in fwd:
            cp.start()

    return _comm_call(name, body, stacks, order, sems_in=sems, new_sems=(3 * n, 3 * n))


def _allgather_finish(name, stacks, sems, order):
    def body(bufs, taken, _):
        sends, recvs = _forward_copies(bufs, taken[0], taken[1])
        for cp in sends:
            cp.wait_send()
        for cp in recvs:
            cp.wait_recv()

    return _comm_call(name, body, stacks, order, sems_in=sems)[1]


def _window_unit(q, j):
    return C2I[WIN_UNIT0[q] + j]


def _pair_copies(g, t, ssem, rsem, gathered):
    x, y, c, _ = _place()
    sib = (x, y, 1 - c)
    cps, whole = [], []
    for a in range(len(g)):
        if a == 0 and gathered:
            for q in range(NCHIP):
                for j in range(WIN_UNITS // 2):
                    u = jnp.where(c == 0, _window_unit(q, WIN_UNITS // 2 + j), _window_unit(q, j))
                    src = g[0].at[pl.ds(pl.multiple_of(u * UNIT, UNIT), UNIT), :]
                    cps.append(_rcopy(src, t[0].at[q, pl.ds(j * UNIT, UNIT), :], ssem.at[0], rsem.at[0], sib))
            whole.append(_rcopy(t[0], t[0], ssem.at[0], rsem.at[0], sib))
        else:
            cp = _rcopy(g[a].at[:, _half(g[a].shape[1], 1 - c), :], t[a], ssem.at[a], rsem.at[a], sib)
            cps.append(cp)
            whole.append(cp)
    return cps, whole


def _pair_start(name, gs, order, gathered=False):
    n = len(gs)
    ts = [lax.empty((NCHIP, WIN_ROWS // 2, D) if (a == 0 and gathered) else (NCHIP, g.shape[1] // 2, g.shape[2]), f32)
          for a, g in enumerate(gs)]

    def body(bufs, _, new):
        for cp in _pair_copies(bufs[:n], bufs[n:], new[0], new[1], gathered)[0]:
            cp.start()

    return _comm_call(name, body, list(gs) + ts, order, new_sems=(n, n))


def _pair_wait(name, bufs, sems, order, gathered=False):
    n = len(bufs) // 2

    def body(refs, taken, _):
        for cp in _pair_copies(refs[:n], refs[n:], taken[0], taken[1], gathered)[1]:
            cp.wait_send()
            cp.wait_recv()

    bufs = _comm_call(name, body, bufs, order, sems_in=sems)[1]
    return bufs[:n], bufs[n:]


def _row_tile(h):
    return min(h, 256)


def _pair_add(order, g, t, c_arr, name):
    _, R, C = g.shape
    h = R // 2
    tr = _row_tile(h)
    nblk = h // tr

    def body(c_ref, g_ref, t_ref, p32_ref, p16_ref):
        s = g_ref[...] + t_ref[...]
        p32_ref[...] = s
        p16_ref[...] = s.astype(bf16)

    blk = pl.BlockSpec((None, tr, C), lambda q, i, c_ref: (q, i, 0))
    return _call_indexed(
        order, body, (c_arr,), (g, t), (NCHIP, nblk),
        [pl.BlockSpec((None, tr, C), lambda q, i, c_ref: (q, c_ref[0] * nblk + i, 0)), blk], [blk, blk],
        name=name,
        out_shape=[jax.ShapeDtypeStruct((NCHIP, h, C), f32), jax.ShapeDtypeStruct((NCHIP, h, C), bf16)],
        compiler_params=_params(("parallel", "parallel")),
    )


def _pair_add_gathered(order, dwt, t, c_arr, name):
    half_units = WIN_UNITS // 2
    table = jnp.asarray([_window_unit(q, j) for q in range(NCHIP) for j in range(WIN_UNITS)], jnp.int32)

    def body(tab_ref, c_ref, g_ref, t_ref, p32_ref, p16_ref):
        s = g_ref[...] + t_ref[...]
        p32_ref[...] = s
        p16_ref[...] = s.astype(bf16)

    blk = pl.BlockSpec((None, UNIT, D), lambda q, j, tab_ref, c_ref: (q, j, 0))
    return _call_indexed(
        order, body, (table, c_arr), (dwt, t), (NCHIP, half_units),
        [pl.BlockSpec((UNIT, D), lambda q, j, tab_ref, c_ref: (tab_ref[q * WIN_UNITS + c_ref[0] * half_units + j], 0)),
         blk], [blk, blk],
        name=name,
        out_shape=[jax.ShapeDtypeStruct((NCHIP, WIN_ROWS // 2, D), f32),
                   jax.ShapeDtypeStruct((NCHIP, WIN_ROWS // 2, D), bf16)],
        compiler_params=_params(("parallel", "parallel")),
    )


def _shard_copies(p, r, sm, ssem, rsem):
    x, y, c, chips = _place()
    n = len(p)
    sends, recvs = [], []
    for a in range(n):
        for j, (cx, cy) in enumerate(chips):
            k = a * 3 + j
            sends.append(_rcopy(p[a].at[2 * cx + cy], r[a].at[j], ssem.at[k], rsem.at[k], (cx, cy, c)))
            recvs.append(_rcopy(r[a].at[j], r[a].at[j], ssem.at[k], rsem.at[k], (cx, cy, c)))
    if sm is not None:
        mine = sm.at[4 * x + 2 * y + c]
        for i in range(1, 8):
            px = (1 - x) if i & 4 else x
            py = (1 - y) if i & 2 else y
            pc = (1 - c) if i & 1 else c
            k = 3 * n + i - 1
            sends.append(_rcopy(mine, mine, ssem.at[k], rsem.at[k], (px, py, pc)))
            slot = sm.at[4 * px + 2 * py + pc]
            recvs.append(_rcopy(slot, slot, ssem.at[k], rsem.at[k], (px, py, pc)))
    return sends, recvs


def _shard_start(name, p16s, order, sm=None):
    n = len(p16s)
    rs = [lax.empty((3,) + p.shape[1:], bf16) for p in p16s]
    extra = [] if sm is None else [sm]
    nsem = 3 * n + (7 if sm is not None else 0)

    def body(bufs, _, new):
        sends, _r = _shard_copies(bufs[:n], bufs[n:2 * n], bufs[2 * n] if extra else None, new[0], new[1])
        for cp in sends:
            cp.start()

    return _comm_call(name, body, list(p16s) + rs + extra, order, new_sems=(nsem, nsem))


def _shard_wait(name, bufs, sems, n, order):
    has_sm = len(bufs) > 2 * n

    def body(refs, taken, _):
        sends, recvs = _shard_copies(refs[:n], refs[n:2 * n], refs[2 * n] if has_sm else None, taken[0], taken[1])
        for cp in sends:
            cp.wait_send()
        for cp in recvs:
            cp.wait_recv()

    bufs = _comm_call(name, body, bufs, order, sems_in=sems)[1]
    return bufs[n:2 * n], (bufs[2 * n] if has_sm else None)


def _shard_sum(order, p32, r, q_arr, c_arr, name):
    _, h, C = p32.shape
    tr = _row_tile(h)
    nblk = h // tr

    def body(q_ref, c_ref, p_ref, r_ref, o_ref):
        s = p_ref[...]
        for j in range(3):
            s = s + r_ref[j].astype(f32)
        o_ref[...] = s

    return _call_indexed(
        order, body, (q_arr, c_arr), (p32, r), (nblk,),
        [pl.BlockSpec((None, tr, C), lambda i, q_ref, c_ref: (q_ref[0], i, 0)),
         pl.BlockSpec((3, tr, C), lambda i, q_ref, c_ref: (0, i, 0))],
        pl.BlockSpec((tr, C), lambda i, q_ref, c_ref: (c_ref[0] * nblk + i, 0)),
        name=name, out_shape=jax.ShapeDtypeStruct((2 * h, C), f32),
        compiler_params=_params(("parallel",)),
    )


def _swap_copies(full, ssem, rsem):
    x, y, c, _ = _place()
    sends, recvs = [], []
    for a in range(len(full)):
        mine = full[a].at[_half(full[a].shape[0], c)]
        sends.append(_rcopy(mine, mine, ssem.at[a], rsem.at[a], (x, y, 1 - c)))
        other = full[a].at[_half(full[a].shape[0], 1 - c)]
        recvs.append(_rcopy(other, other, ssem.at[a], rsem.at[a], (x, y, 1 - c)))
    return sends, recvs


def _swap_start(name, fulls, order):
    n = len(fulls)

    def body(bufs, _, new):
        for cp in _swap_copies(bufs, new[0], new[1])[0]:
            cp.start()

    return _comm_call(name, body, list(fulls), order, new_sems=(n, n))


def _swap_wait(name, fulls, sems, order):
    def body(refs, taken, _):
        sends, recvs = _swap_copies(refs, taken[0], taken[1])
        for cp in sends:
            cp.wait_send()
        for cp in recvs:
            cp.wait_recv()

    return _comm_call(name, body, fulls, order, sems_in=sems)[1]


def _small_sum(order, sm):
    def body(sm_ref, o_ref):
        s = sm_ref[0]
        for d in range(1, 8):
            s = s + sm_ref[d]
        o_ref[...] = s

    return _call(order, body, (sm,), name="small_grad_sum", out_shape=jax.ShapeDtypeStruct((SMALL_ROWS, D), f32))


def _adamw(order, w, g, m, v, name):
    R, C = w.shape
    tr = R if R <= 256 else next(t for t in (256, 1072) if R % t == 0)

    def body(w_ref, g_ref, m_ref, v_ref, d_ref, nm_ref, nv_ref):
        g_ = g_ref[...]
        m_ = ADAM_B1 * m_ref[...] + (1.0 - ADAM_B1) * g_
        v_ = ADAM_B2 * v_ref[...] + (1.0 - ADAM_B2) * (g_ * g_)
        m_hat = m_ / (1.0 - ADAM_B1 ** ADAM_STEP)
        v_hat = v_ / (1.0 - ADAM_B2 ** ADAM_STEP)
        d_ref[...] = -ADAM_LR * (m_hat / (jnp.sqrt(v_hat) + ADAM_EPS) + ADAM_WD * w_ref[...])
        nm_ref[...] = m_
        nv_ref[...] = v_

    assert R % tr == 0
    blk = pl.BlockSpec((tr, C), lambda i: (i, 0))
    return _call(
        order, body, (w, g, m, v), name=name, grid=(R // tr,), in_specs=[blk] * 4, out_specs=[blk] * 3,
        out_shape=[jax.ShapeDtypeStruct((R, C), f32)] * 3,
        compiler_params=_params(("parallel",)),
    )


def _feature_major(w):
    return jnp.transpose(w, (2, 0, 1)).reshape(SHARD_IN, D)


def _flat(wt):
    return wt.reshape(SHARD_IN * 8, 128)


def _unflat(a):
    return jnp.transpose(a.reshape(SHARD_IN, 1, D), (1, 2, 0))


def _window_of(wt, q):
    wb = wt.astype(bf16)
    off = jnp.asarray(OWN_ROW0, jnp.int32)[q]
    plain = lax.dynamic_update_slice(jnp.zeros((WIN_ROWS, D), bf16), wb, (off, 0))
    lo = jnp.pad(wb[0:62], ((2, WIN_ROWS - 64), (0, 0)))
    hi = jnp.pad(wb[70:SHARD_IN], ((64, WIN_ROWS - 64 - (SHARD_IN - 70)), (0, 0)))
    win = jnp.where(q == 1, lo + hi, plain)
    fa = jnp.pad(wb[62:70], ((0, FA_ROWS - 8), (0, 0)))
    return win, fa


def _own_rows(gwin, gfa, q):
    off = jnp.asarray(OWN_ROW0, jnp.int32)[q]
    plain = lax.dynamic_slice(gwin, (off, 0), (SHARD_IN, D))
    chip1 = (jnp.pad(gwin[2:64], ((0, SHARD_IN - 62), (0, 0))) + jnp.pad(gfa[0:8], ((62, SHARD_IN - 70), (0, 0)))
             + jnp.pad(gwin[64:64 + SHARD_IN - 70], ((70, 0), (0, 0))))
    return jnp.where(q == 1, chip1, plain)


def kernel(x, norm_attn_g, w_in, b_forget, w_branch_a, w_branch_b, w_out, norm_mlp_g, w_up, w_down, norm_final_g, loss_target, m_norm_attn_g, m_w_in, m_b_forget, m_w_branch_a, m_w_branch_b, m_w_out, m_norm_mlp_g, m_w_up, m_w_down, m_norm_final_g, v_norm_attn_g, v_w_in, v_b_forget, v_w_branch_a, v_w_branch_b, v_w_out, v_norm_mlp_g, v_w_up, v_w_down, v_norm_final_g):
    xi, yi, ci = lax.axis_index("x"), lax.axis_index("y"), lax.axis_index("c")
    q_me = 2 * xi + yi
    c_arr = jnp.reshape(ci, (1,)).astype(jnp.int32)
    q_arr = jnp.reshape(q_me, (1,)).astype(jnp.int32)
    x_, tgt = x[0], loss_target[0]

    names = ["w_branch_a", "w_branch_b", "w_out", "w_up", "w_down"]
    big = dict(zip(names, [w_branch_a[0], w_branch_b[0], w_out[0], w_up[0], w_down[0]]))
    ms = dict(zip(names, [m_w_branch_a[0], m_w_branch_b[0], m_w_out[0], m_w_up[0], m_w_down[0]]))
    vs = dict(zip(names, [v_w_branch_a[0], v_w_branch_b[0], v_w_out[0], v_w_up[0], v_w_down[0]]))
    grad, upd = {}, {}
    order = _Order()

    def run(fn, *args, **kw):
        return fn(order, *args, **kw)

    def own_slot(a):
        return lax.dynamic_update_slice(lax.empty((NCHIP,) + a.shape, a.dtype), a[None], (q_me, 0, 0))

    wt_own = _feature_major(w_in)
    win, fa_blk = _window_of(wt_own, q_me)
    stacks = [own_slot(win), own_slot(fa_blk)] + [own_slot(w.astype(bf16)) for w in big.values()]
    (sem_in, sem_rest), stacks = _allgather_start(stacks, [2, 5], order)
    sem_f, in_s = _allgather_forward("allgather_forward_in", stacks[0:2], sem_in, order)
    wins, fas = _allgather_finish("allgather_finish_in", in_s, sem_f, order)
    wt = run(_assemble_win, wins, fas)

    rope = _rope_tables()
    bpad = jnp.pad(b_forget, ((0, 0), (0, 120)))
    h1, qkvb, qkva, gates, fa = run(_norm_inproj, x_, norm_attn_g, wt, rope)
    F = run(_forget_cumsum, fa, bpad)
    oa, lsea = run(_fox_fwd, qkva, F)
    sem_f, rest = _allgather_forward("allgather_forward_rest", stacks[2:], sem_rest, order)
    ob, lseb = run(_dil_fwd, qkvb)
    was, wbs, wouts, wups, wdowns = _allgather_finish("allgather_finish_rest", rest, sem_f, order)
    wout = wouts.reshape(D, D)
    wdown = wdowns.reshape(DFF, D)
    ya, yb, mixed = run(_branch_mix, oa, ob, was, wbs, gates)
    x2, h2 = run(_outproj_norm, mixed, wout, x_, norm_mlp_g)
    u, a = run(_mlp_up, h2, wups)
    dx3, dx3b, dg3, loss_part = run(_mlp_down_loss, a, wdown, x2, norm_final_g.reshape(1, D), tgt)
    loss = lax.psum(loss_part[0, 0], ("x", "y", "c"))

    def reduce_to_shard(tag, group, p32s, bufs, sems):
        rs, _ = _shard_wait("shard_wait_" + tag, bufs, sems, len(group), order)
        fulls = [run(_shard_sum, p32s[i], rs[i], q_arr, c_arr, "shard_sum_" + nm) for i, nm in enumerate(group)]
        return _swap_start("swap_start_" + tag, fulls, order)

    def pair_sums(tag, group, bufs, sems):
        gs, ts = _pair_wait("pair_wait_" + tag, bufs, sems, order)
        return zip(*[run(_pair_add, gs[i], ts[i], c_arr, "pair_add_" + nm) for i, nm in enumerate(group)])

    def finish(tag, group, fulls, sems):
        fulls = _swap_wait("swap_wait_" + tag, fulls, sems, order)
        for nm, gfull in zip(group, fulls):
            grad[nm] = gfull
            upd[nm] = run(_adamw, big[nm], gfull, ms[nm], vs[nm], "adamw_" + nm)

    grp_a, grp_b, grp_c = ["w_down", "w_up"], ["w_out", "w_branch_a", "w_branch_b"], ["w_in", "w_in_fa"]
    du = run(_mlp_down_bwd, dx3b, wdown, u)
    dwdown = run(_mm, a, dx3b, "tn", f32, 1024, D, "wgrad_down")
    dwup = run(_mm, h2, du, "tn", f32, D, 1024, "wgrad_up", stack_cols=True)
    sem_pa, buf_pa = _pair_start("pair_start_a", [dwdown.reshape(NCHIP, DFF // NCHIP, D), dwup], order)
    dx2, dx2b, dg2 = run(_mlp_up_bwd, du, wups, x2, dx3, norm_mlp_g)
    p32_a, p16_a = pair_sums("a", grp_a, buf_pa, sem_pa)
    sem_sa, buf_sa = _shard_start("shard_start_a", p16_a, order)
    dya, dyb, dproj = run(_gate_bwd, dx2b, wout, gates, ya, yb)
    dwout = run(_mm, mixed, dx2b, "tn", f32, D, D, "wgrad_out")
    doa, dob = run(_branch_bwd, dya, dyb, was, wbs)
    dwas, dwbs = run(_branch_wgrad, oa, ob, dya, dyb)
    sem_pb, buf_pb = _pair_start("pair_start_b", [dwout.reshape(NCHIP, D // NCHIP, D), dwas, dwbs], order)
    dF, dproj = run(_fox_bwd, qkva, doa, oa, lsea, F, dproj)
    p32_b, p16_b = pair_sums("b", grp_b, buf_pb, sem_pb)
    sem_wa, fulls_a = reduce_to_shard("a", grp_a, p32_a, buf_sa, sem_sa)
    sem_sb, buf_sb = _shard_start("shard_start_b", p16_b, order)
    dbf, dproj = run(_forget_bwd, dF, fa, bpad, dproj)
    dproj = run(_dil_bwd, qkvb, dob, ob, lseb, rope, dproj)
    sem_wb, fulls_b = reduce_to_shard("b", grp_b, p32_b, buf_sb, sem_sb)
    finish("a", grp_a, fulls_a, sem_wa)
    dwt = run(_mm, dproj, h1, "tn", f32, 512, D, "wgrad_in")
    dwfa = jnp.broadcast_to(dwt[F_FA:F_FA + FA_ROWS][None], (NCHIP, FA_ROWS, D))
    sem_pc, buf_pc = _pair_start("pair_start_c", [dwt, dwfa], order, gathered=True)
    finish("b", grp_b, fulls_b, sem_wb)
    (dwt_c, dwfa_c), (t_in, t_fa) = _pair_wait("pair_wait_c", buf_pc, sem_pc, order, gathered=True)
    p32_in, p16_in = run(_pair_add_gathered, dwt_c, t_in, c_arr, "pair_add_w_in")
    p32_fa, p16_fa = run(_pair_add, dwfa_c, t_fa, c_arr, "pair_add_w_in_fa")
    sem_sc, buf_sc = _shard_start("shard_start_c", [p16_in, p16_fa], order)
    gx, dg1 = run(_inproj_bwd, dproj, wt, x_, dx2, norm_attn_g)
    small = jnp.concatenate([dg1, dg2, dg3, jnp.pad(dbf[:, 0:8], ((0, 0), (0, D - 8))),
                             jnp.zeros((SMALL_ROWS - 4, D), f32)], axis=0)
    sm = lax.dynamic_update_slice(lax.empty((8, SMALL_ROWS, D), f32), small[None],
                                  (4 * xi + 2 * yi + ci, 0, 0))
    sem_sm, buf_sm = _shard_start("small_start", [], order, sm)
    sem_wc, fulls_c = reduce_to_shard("c", grp_c, [p32_in, p32_fa], buf_sc, sem_sc)
    _, sm = _shard_wait("small_wait", buf_sm, sem_sm, 0, order)
    gsmall = run(_small_sum, sm)

    grad["norm_attn_g"], grad["norm_mlp_g"] = gsmall[0:1], gsmall[1:2]
    grad["norm_final_g"], grad["b_forget"] = gsmall[2:3], gsmall[3:4, 0:8]
    upd["norm_attn_g"] = run(_adamw, norm_attn_g, grad["norm_attn_g"], m_norm_attn_g, v_norm_attn_g, "adamw_g1")
    upd["norm_mlp_g"] = run(_adamw, norm_mlp_g, grad["norm_mlp_g"], m_norm_mlp_g, v_norm_mlp_g, "adamw_g2")
    upd["norm_final_g"] = run(_adamw, norm_final_g.reshape(1, D), grad["norm_final_g"],
                              m_norm_final_g.reshape(1, D), v_norm_final_g.reshape(1, D), "adamw_g3")
    upd["b_forget"] = run(_adamw, b_forget, grad["b_forget"], m_b_forget, v_b_forget, "adamw_bf")

    gwin, gfa = _swap_wait("swap_wait_c", fulls_c, sem_wc, order)
    g_in = _flat(_own_rows(gwin, gfa, q_me))
    upd_in = run(_adamw, _flat(wt_own), g_in, _flat(_feature_major(m_w_in)), _flat(_feature_major(v_w_in)),
                 "adamw_w_in")
    grad["w_in"] = _unflat(g_in)
    upd["w_in"] = [_unflat(t) for t in upd_in]

    order_out = ["norm_attn_g", "w_in", "b_forget", "w_branch_a", "w_branch_b", "w_out", "norm_mlp_g", "w_up",
                 "w_down", "norm_final_g"]
    shapes = dict(norm_attn_g=norm_attn_g.shape, w_in=w_in.shape, b_forget=b_forget.shape,
                  w_branch_a=w_branch_a.shape, w_branch_b=w_branch_b.shape, w_out=w_out.shape,
                  norm_mlp_g=norm_mlp_g.shape, w_up=w_up.shape, w_down=w_down.shape, norm_final_g=norm_final_g.shape)
    outs = [loss, gx.reshape(x.shape)]
    outs += [grad[nm].reshape(shapes[nm]) for nm in order_out]
    for k in range(3):
        outs += [upd[nm][k].reshape(shapes[nm]) for nm in order_out]
    return tuple(outs)
```

```python
import jax
import jax.numpy as jnp
from jax import lax
from jax.experimental import pallas as pl
from jax.experimental.pallas import tpu as pltpu

f32 = jnp.float32
bf16 = jnp.bfloat16

S = 2048
D = 1024
DFF = 4096
HD = 64
FOXW = 512
DILOUT = 256
DIL = (1, 4, 16)
BAND = 128
EPS = 1e-6
NEG = -1e30
ROPE_THETA = 500000.0
NCHIP = 4
TQ = 256

ADAM_LR, ADAM_B1, ADAM_B2, ADAM_EPS, ADAM_WD, ADAM_STEP = 0.001, 0.9, 0.999, 1e-08, 0.01, 10
VMEM_LIMIT = 56 * 1024 * 1024

UNIT = 64
NP = 6144
F_DIL, F_FOX, F_FA, F_G = 0, 2304, 3840, 4096
DIL_BLK, FOX_BLK = 1152, 384
WIN_UNITS, WIN_ROWS = 24, 1536
WIN_UNIT0 = (0, 23, 45, 68)
OWN_ROW0 = (0, 2, 60, 62)
SHARD_IN = 1474
FA_ROWS = 32


def _compact_to_internal():
    c2i = {}
    for p in range(2):
        for role in range(3):
            for g in range(3):
                for hh in range(2):
                    c2i[24 + 12 * role + 4 * g + 2 * p + hh] = 18 * p + 6 * role + 2 * g + hh
    for p in range(4):
        for role in range(3):
            for hh in range(2):
                c2i[8 * role + 2 * p + hh] = F_FOX // UNIT + 6 * p + 2 * role + hh
    for j in range(32):
        c2i[60 + j] = F_G // UNIT + j
    return c2i


C2I = _compact_to_internal()
OVERLAP_UNITS = (23, 45, 46, 68)


def _params(sem=None):
    return pltpu.CompilerParams(dimension_semantics=sem, vmem_limit_bytes=VMEM_LIMIT)


class _Order:
    def __init__(self):
        self.tok = None

    def mark(self, v):
        self.tok = v

    def token_for(self, args):
        return [] if self.tok is None or any(self.tok is a for a in args) else [self.tok]


def _call(order, body, args, in_specs=None, **kw):
    args = list(args)
    n_in = len(args)
    if in_specs is None:
        in_specs = [pl.BlockSpec(memory_space=pltpu.VMEM)] * n_in
    kern = body
    extra = order.token_for(args)
    if extra:
        in_specs = list(in_specs) + [pl.BlockSpec(memory_space=pl.ANY)]

        def kern(*refs):
            body(*refs[:n_in], *refs[n_in + 1:])

    out = pl.pallas_call(kern, in_specs=in_specs, **kw)(*args, *extra)
    order.mark(out[0] if isinstance(out, (tuple, list)) else out)
    return out


def _call_indexed(order, body, scalars, args, grid, in_specs, out_specs, scratch_shapes=(), **kw):
    args, in_specs = list(args), list(in_specs)
    n_front = len(scalars) + len(args)
    kern = body
    extra = order.token_for(args)
    if extra:
        in_specs.append(pl.BlockSpec(memory_space=pl.ANY))

        def kern(*refs):
            body(*refs[:n_front], *refs[n_front + 1:])

    out = pl.pallas_call(
        kern, grid_spec=pltpu.PrefetchScalarGridSpec(num_scalar_prefetch=len(scalars), grid=grid, in_specs=in_specs,
                                                     out_specs=out_specs, scratch_shapes=scratch_shapes),
        **kw)(*scalars, *args, *extra)
    order.mark(out[0] if isinstance(out, (tuple, list)) else out)
    return out


def _dot(a, b):
    return jnp.dot(a, b, preferred_element_type=f32)


def _dot_nt(a, b):
    return lax.dot_general(a, b, (((1,), (1,)), ((), ())), preferred_element_type=f32)


def _dot_tn(a, b):
    return lax.dot_general(a, b, (((0,), (0,)), ((), ())), preferred_element_type=f32)


def _split3(x):
    hi = x.astype(bf16)
    r1 = x - hi.astype(f32)
    mid = r1.astype(bf16)
    lo = (r1 - mid.astype(f32)).astype(bf16)
    return hi, mid, lo


def _rope_tables():
    half = 8
    inv_freq = jnp.power(jnp.float32(ROPE_THETA), -jnp.arange(half, dtype=f32) * 2.0 / 16)
    ang = jnp.arange(S).astype(f32)[:, None] * inv_freq[None, :]
    cos, sin = jnp.cos(ang), jnp.sin(ang)
    one = jnp.ones((S, HD - 16), f32)
    zero = jnp.zeros((S, HD - 16), f32)
    z8 = jnp.zeros((S, 8), f32)
    c = jnp.concatenate([cos, cos, one], axis=1)
    s1 = jnp.concatenate([-sin, z8, zero], axis=1)
    s2 = jnp.concatenate([z8, sin, zero], axis=1)
    return tuple(jnp.concatenate([t, t], axis=1) for t in (c, s1, s2))


def _mm(order, a, b, mode, out_dtype, tm, tn, name, stack_cols=False):
    if mode == "nn":
        (M, K), (_, N) = a.shape, b.shape
        a_spec = pl.BlockSpec((tm, K), lambda i, j: (i, 0))
        b_spec = pl.BlockSpec((K, tn), lambda i, j: (0, j))
        dot = _dot
    elif mode == "nt":
        (M, K), (N, _) = a.shape, b.shape
        a_spec = pl.BlockSpec((tm, K), lambda i, j: (i, 0))
        b_spec = pl.BlockSpec((tn, K), lambda i, j: (j, 0))
        dot = _dot_nt
    else:
        (K, M), (_, N) = a.shape, b.shape
        a_spec = pl.BlockSpec((K, tm), lambda i, j: (0, i))
        b_spec = pl.BlockSpec((K, tn), lambda i, j: (0, j))
        dot = _dot_tn

    def body(a_ref, b_ref, o_ref):
        o_ref[...] = dot(a_ref[...], b_ref[...]).astype(out_dtype)

    if stack_cols:
        assert tm == M
        out_spec = pl.BlockSpec((None, tm, tn), lambda i, j: (j, 0, 0))
        out_shape = jax.ShapeDtypeStruct((N // tn, M, tn), out_dtype)
    else:
        out_spec = pl.BlockSpec((tm, tn), lambda i, j: (i, j))
        out_shape = jax.ShapeDtypeStruct((M, N), out_dtype)
    return _call(
        order, body, (a, b), name=name, grid=(M // tm, N // tn), in_specs=[a_spec, b_spec],
        out_specs=out_spec, out_shape=out_shape,
        compiler_params=_params(("parallel", "parallel")),
    )


def _assemble_win(order, wins, fas):
    def body(win_ref, fa_ref, o_ref):
        q = pl.program_id(0)

        @pl.when(q == 0)
        def _():
            o_ref[...] = jnp.zeros_like(o_ref)

        for k in range(NCHIP):
            @pl.when(q == k)
            def _(k=k):
                for j in range(WIN_UNITS):
                    cu = WIN_UNIT0[k] + j
                    dst = pl.ds(C2I[cu] * UNIT, UNIT)
                    if cu in OVERLAP_UNITS:
                        o_ref[dst, :] += win_ref[j * UNIT:(j + 1) * UNIT, :]
                    else:
                        o_ref[dst, :] = win_ref[j * UNIT:(j + 1) * UNIT, :]
                if k == 1:
                    o_ref[F_FA:F_FA + FA_ROWS, :] = fa_ref[...]

    return _call(
        order, body, (wins, fas), name="assemble_w_in", grid=(NCHIP,),
        in_specs=[pl.BlockSpec((None, WIN_ROWS, D), lambda q: (q, 0, 0)),
                  pl.BlockSpec((None, FA_ROWS, D), lambda q: (1, 0, 0))],
        out_specs=pl.BlockSpec((NP, D), lambda q: (0, 0)),
        out_shape=jax.ShapeDtypeStruct((NP, D), bf16),
        compiler_params=_params(("arbitrary",)),
    )


def _norm_inproj(order, x, g1, wt, rope):
    tm = 256
    c_t, s1_t, s2_t = rope

    def body(x_ref, g_ref, w_ref, c_ref, s1_ref, s2_ref, h_ref, qkvb_ref, qkva_ref, gates_ref, fa_ref):
        xb = x_ref[...]
        r = lax.rsqrt(jnp.mean(xb * xb, axis=-1, keepdims=True) + EPS)
        h = ((xb * r) * g_ref[...]).astype(bf16)
        h_ref[...] = h
        c, s1, s2 = c_ref[...], s1_ref[...], s2_ref[...]
        for p in range(2):
            pb = _dot_nt(h, w_ref[F_DIL + p * DIL_BLK:F_DIL + (p + 1) * DIL_BLK, :])
            for ch in range(DIL_BLK // 128):
                pc = pb[:, ch * 128:(ch + 1) * 128]
                if ch < 6:
                    pc = pc * c + pltpu.roll(pc, 120, 1) * s1 + pltpu.roll(pc, 8, 1) * s2
                qkvb_ref[:, p * DIL_BLK + ch * 128:p * DIL_BLK + (ch + 1) * 128] = pc
        qkva_ref[...] = _dot_nt(h, w_ref[F_FOX:F_FA, :]).astype(bf16)
        fa_ref[...] = _dot_nt(h, w_ref[F_FA:F_FA + 128, :])
        gates_ref[...] = _dot_nt(h, w_ref[F_G:NP, :])

    row = lambda w: pl.BlockSpec((tm, w), lambda i: (i, 0))
    return _call(
        order, body, (x, g1, wt, c_t, s1_t, s2_t), name="norm_inproj", grid=(S // tm,),
        in_specs=[row(D), pl.BlockSpec((1, D), lambda i: (0, 0)), pl.BlockSpec((NP, D), lambda i: (0, 0)),
                  row(128), row(128), row(128)],
        out_specs=[row(D), row(2 * DIL_BLK), row(4 * FOX_BLK), row(2 * D), row(128)],
        out_shape=[jax.ShapeDtypeStruct((S, D), bf16), jax.ShapeDtypeStruct((S, 2 * DIL_BLK), f32),
                   jax.ShapeDtypeStruct((S, 4 * FOX_BLK), bf16), jax.ShapeDtypeStruct((S, 2 * D), f32),
                   jax.ShapeDtypeStruct((S, 128), f32)],
        compiler_params=_params(("parallel",)),
    )


def _forget_cumsum(order, fa, bpad):
    nb = S // TQ

    def body(fa_ref, b_ref, F_ref):
        rr = lax.broadcasted_iota(jnp.int32, (TQ, TQ), 0)
        cc = lax.broadcasted_iota(jnp.int32, (TQ, TQ), 1)
        tri = (rr >= cc).astype(bf16)
        lane = lax.broadcasted_iota(jnp.int32, (1, 128), 1)
        carry = jnp.zeros((1, 128), f32)
        for b in range(nb):
            z = fa_ref[b * TQ:(b + 1) * TQ, :] + b_ref[...]
            lf = jnp.minimum(z, 0.0) - jnp.log(1.0 + jnp.exp(-jnp.abs(z)))
            lf = jnp.where(lane < 8, lf, 0.0)
            hi, mid, lo = _split3(lf)
            fb = (_dot(tri, hi) + _dot(tri, mid)) + _dot(tri, lo) + carry
            F_ref[b * TQ:(b + 1) * TQ, :] = fb
            carry = fb[TQ - 1:TQ, :]

    return _call(
        order, body, (fa, bpad), name="forget_cumsum",
        out_shape=jax.ShapeDtypeStruct((S, 128), f32),
        compiler_params=_params(),
    )


def _head_masks():
    lane = lax.broadcasted_iota(jnp.int32, (1, 128), 1)
    return lane, (lane < HD, lane >= HD)


L_FT, L_ONE, L_LSE = 0, 3, 6
FOX_TQ, FOX_TK = 256, 512


def _set_lanes(x, lane, first, cols):
    for n, col in enumerate(cols):
        x = jnp.where(lane == first + n, col, x)
    return x


def _f32_parts(col):
    return [t.astype(f32) for t in _split3(col)]


def _fox_operands(qkv_ref, F_ref, lse_ref, qa, ka, p, rows):
    lane, hm = _head_masks()
    q = qkv_ref[rows, 0:128].astype(f32) * 0.125
    k = qkv_ref[rows, 128:256].astype(f32)
    Fb = F_ref[rows, :]
    for hh in (0, 1):
        free = (1 - hh) * HD
        fparts = _f32_parts(jnp.sum(jnp.where(lane == 2 * p + hh, Fb, 0.0), axis=1, keepdims=True))
        qcols = fparts + [1.0] * 3
        kcols = [1.0] * 3 + [-t for t in fparts]
        if lse_ref is not None:
            qcols += [-t for t in _f32_parts(lse_ref[rows, hh * HD:hh * HD + 1])]
            kcols += [1.0] * 3
        qa[hh, rows, :] = _set_lanes(jnp.where(hm[hh], q, 0.0), lane, free, qcols).astype(bf16)
        ka[hh, rows, :] = _set_lanes(k, lane, free, kcols).astype(bf16)


def _fox_fwd(order, qkva, F):
    tq, tk = FOX_TQ, FOX_TK

    def body(qkv_ref, F_ref, o_ref, lse_ref, qa, ka, vt):
        p = pl.program_id(0)
        keyi = lax.broadcasted_iota(jnp.int32, (tk, 1), 0)
        qryi = lax.broadcasted_iota(jnp.int32, (1, tq), 1)
        sub = lax.broadcasted_iota(jnp.int32, (128, 1), 0)

        def prep(i, c):
            rows = pl.ds(pl.multiple_of(i * tk, tk), tk)
            _fox_operands(qkv_ref, F_ref, None, qa, ka, p, rows)
            vt[i] = qkv_ref[rows, 256:384].astype(f32).T.astype(bf16)
            return c

        lax.fori_loop(0, S // tk, prep, 0)

        def qblock(i, c):
            r0 = pl.multiple_of(i * tq, tq)
            qh = [qa[hh, pl.ds(r0, tq), :] for hh in (0, 1)]

            def kv(jb, carry, masked):
                keys = pl.ds(pl.multiple_of(jb * tk, tk), tk)
                sts = [_dot_nt(ka[hh, keys, :], qh[hh]) for hh in (0, 1)]
                new = []
                for hh in (0, 1):
                    m, l, a = carry[3 * hh:3 * hh + 3]
                    st = sts[hh]
                    if masked:
                        st = jnp.where(jb * tk + keyi <= r0 + qryi, st, NEG)
                    mn = jnp.maximum(m, jnp.max(st, axis=0, keepdims=True))
                    al = jnp.exp(m - mn)
                    pt = jnp.exp(st - mn)
                    l = al * l + jnp.sum(pt, axis=0, keepdims=True)
                    a = al * a + _dot(vt[jb, hh * HD:(hh + 1) * HD, :], pt.astype(bf16))
                    new += [mn, l, a]
                return tuple(new)

            init = (jnp.full((1, tq), NEG, f32), jnp.zeros((1, tq), f32), jnp.zeros((HD, tq), f32)) * 2
            last = (r0 + tq - 1) // tk
            carry = lax.fori_loop(0, last, lambda j, cr: kv(j, cr, False), init)
            m0, l0, a0, m1, l1, a1 = kv(last, carry, True)
            ot = jnp.concatenate([a0 / l0, a1 / l1], axis=0)
            lt = jnp.where(sub < HD, m0 + jnp.log(l0), m1 + jnp.log(l1))
            o_ref[pl.ds(r0, tq), :] = ot.T.astype(bf16)
            lse_ref[pl.ds(r0, tq), :] = lt.T
            return c

        lax.fori_loop(0, S // tq, qblock, 0)

    pair = pl.BlockSpec((S, 128), lambda p: (0, p))
    return _call(
        order, body, (qkva, F), name="fox_fwd", grid=(4,),
        in_specs=[pl.BlockSpec((S, FOX_BLK), lambda p: (0, p)), pl.BlockSpec((S, 128), lambda p: (0, 0))],
        out_specs=[pair, pair],
        out_shape=[jax.ShapeDtypeStruct((S, FOXW), bf16), jax.ShapeDtypeStruct((S, FOXW), f32)],
        scratch_shapes=[pltpu.VMEM((2, S, 128), bf16)] * 2 + [pltpu.VMEM((S // tk, 128, tk), bf16)],
        compiler_params=_params(("parallel",)),
    )


def _permute_in(dst, src, r):
    L = S // r
    for rho in range(r):
        dst[rho * L:(rho + 1) * L, :] = src[pl.ds(rho, L, stride=r), :]


def _permute_out(dst, src, r):
    L = S // r
    for rho in range(r):
        dst[pl.ds(rho, L, stride=r), :] = src[rho * L:(rho + 1) * L, :]


def _band_geometry(bb, nbl):
    r0 = pl.multiple_of(bb * BAND, BAND)
    k0 = pl.multiple_of(jnp.maximum(bb - 1, 0) * BAND, BAND)
    sub0 = (bb - lax.rem(bb, nbl)) * BAND
    qi = r0 + lax.broadcasted_iota(jnp.int32, (BAND, 1), 0)
    ki = k0 + lax.broadcasted_iota(jnp.int32, (1, 2 * BAND), 1)
    diff = qi - ki
    valid = (diff >= 0) & (diff <= BAND) & (ki >= sub0)
    return r0, k0, valid


def _dil_views(ref):
    return [[ref.at[:, pl.ds((3 * role + g) * 128, 128)] for g in range(3)] for role in range(3)]


DIL_UNROLL = 4


def _dil_in_specs():
    return [pl.BlockSpec((S, 128), lambda p, k=k: (0, 9 * p + k)) for k in range(9)]


def _dil_fwd(order, qkvb):
    def body(*refs):
        q_refs, k_refs, v_refs = refs[0:3], refs[3:6], refs[6:9]
        ob_ref, lse_ref, qp, kp, vp, op, lp = refs[9:16]
        on, ln = refs[16:19], refs[19:22]
        _, hm = _head_masks()
        for g, r in enumerate(DIL):
            nbl = S // r // BAND
            if r == 1:
                qs_, ks_, vs_, od, ld = q_refs[g], k_refs[g], v_refs[g], on[g], ln[g]
            else:
                _permute_in(qp, q_refs[g], r)
                _permute_in(kp, k_refs[g], r)
                _permute_in(vp, v_refs[g], r)
                qs_, ks_, vs_, od, ld = qp, kp, vp, op, lp

            def blk(t, c, qs_=qs_, ks_=ks_, vs_=vs_, od=od, ld=ld, nbl=nbl):
                work = []
                for u in range(DIL_UNROLL):
                    r0, k0, valid = _band_geometry(DIL_UNROLL * t + u, nbl)
                    q = qs_[pl.ds(r0, BAND), :] * 0.125
                    kw = ks_[pl.ds(k0, 2 * BAND), :].astype(bf16)
                    vw = vs_[pl.ds(k0, 2 * BAND), :]
                    for hh in (0, 1):
                        qh = jnp.where(hm[hh], q, 0.0).astype(bf16)
                        work.append((u, hh, r0, valid, vw, _dot_nt(qh, kw)))
                o = [jnp.zeros((BAND, 128), f32)] * DIL_UNROLL
                lse = [jnp.zeros((BAND, 128), f32)] * DIL_UNROLL
                for u, hh, r0, valid, vw, s in work:
                    s = jnp.where(valid, s, NEG)
                    m = jnp.max(s, axis=1, keepdims=True)
                    pr = jnp.exp(s - m)
                    l = jnp.sum(pr, axis=1, keepdims=True)
                    vm = jnp.where(hm[hh], vw, 0.0).astype(bf16)
                    o[u] = o[u] + _dot((pr / l).astype(bf16), vm)
                    lse[u] = jnp.where(hm[hh], m + jnp.log(l), lse[u])
                    if hh == 1:
                        od[pl.ds(r0, BAND), :] = o[u]
                        ld[pl.ds(r0, BAND), :] = lse[u]
                return c

            lax.fori_loop(0, S // BAND // DIL_UNROLL, blk, 0)
            if r != 1:
                _permute_out(on[g], op, r)
                _permute_out(ln[g], lp, r)

        def combine(i, c):
            r0 = pl.multiple_of(i * TQ, TQ)
            ls = [ln[g][pl.ds(r0, TQ), :] for g in range(3)]
            mx = jnp.maximum(jnp.maximum(ls[0], ls[1]), ls[2])
            es = [jnp.exp(l - mx) for l in ls]
            tot = (es[0] + es[1]) + es[2]
            acc = (es[0] / tot) * on[0][pl.ds(r0, TQ), :]
            acc = acc + (es[1] / tot) * on[1][pl.ds(r0, TQ), :]
            acc = acc + (es[2] / tot) * on[2][pl.ds(r0, TQ), :]
            ob_ref[pl.ds(r0, TQ), :] = acc.astype(bf16)
            lse_ref[pl.ds(r0, TQ), :] = mx + jnp.log(tot)
            return c

        lax.fori_loop(0, S // TQ, combine, 0)

    out_blk = pl.BlockSpec((S, 128), lambda p: (0, p))
    return _call(
        order, body, [qkvb] * 9, name="dil_fwd", grid=(2,),
        in_specs=_dil_in_specs(), out_specs=[out_blk, out_blk],
        out_shape=[jax.ShapeDtypeStruct((S, DILOUT), bf16), jax.ShapeDtypeStruct((S, DILOUT), f32)],
        scratch_shapes=[pltpu.VMEM((S, 128), f32)] * 11,
        compiler_params=_params(("parallel",)),
    )


def _branch_mix(order, oa, ob, was, wbs, gates):
    tm = 512

    def body(oa_ref, ob_ref, wa_ref, wb_ref, g_ref, ya_ref, yb_ref, mix_ref):
        oa_b, ob_b = oa_ref[...], ob_ref[...]
        for q in range(NCHIP):
            cols = slice(q * 256, (q + 1) * 256)
            ya = _dot(oa_b, wa_ref[q])
            yb = _dot(ob_b, wb_ref[q])
            ya_ref[:, cols] = ya
            yb_ref[:, cols] = yb
            ga = g_ref[:, q * 256:(q + 1) * 256]
            gb = g_ref[:, D + q * 256:D + (q + 1) * 256]
            mix_ref[:, cols] = (jax.nn.sigmoid(ga) * ya + jax.nn.sigmoid(gb) * yb).astype(bf16)

    row = lambda w: pl.BlockSpec((tm, w), lambda i: (i, 0))
    full3 = lambda a: pl.BlockSpec(a.shape, lambda i: (0, 0, 0))
    return _call(
        order, body, (oa, ob, was, wbs, gates), name="branch_mix", grid=(S // tm,),
        in_specs=[row(FOXW), row(DILOUT), full3(was), full3(wbs), row(2 * D)],
        out_specs=[row(D), row(D), row(D)],
        out_shape=[jax.ShapeDtypeStruct((S, D), f32), jax.ShapeDtypeStruct((S, D), f32),
                   jax.ShapeDtypeStruct((S, D), bf16)],
        compiler_params=_params(("parallel",)),
    )


def _outproj_norm(order, mixed, wout, x, g2):
    tm = 512

    def body(m_ref, w_ref, x_ref, g_ref, x2_ref, h2_ref):
        x2 = x_ref[...] + _dot(m_ref[...], w_ref[...])
        x2_ref[...] = x2
        r = lax.rsqrt(jnp.mean(x2 * x2, axis=-1, keepdims=True) + EPS)
        h2_ref[...] = ((x2 * r) * g_ref[...]).astype(bf16)

    row = pl.BlockSpec((tm, D), lambda i: (i, 0))
    return _call(
        order, body, (mixed, wout, x, g2), name="outproj_norm", grid=(S // tm,),
        in_specs=[row, pl.BlockSpec((D, D), lambda i: (0, 0)), row, pl.BlockSpec((1, D), lambda i: (0, 0))],
        out_specs=[row, row],
        out_shape=[jax.ShapeDtypeStruct((S, D), f32), jax.ShapeDtypeStruct((S, D), bf16)],
        compiler_params=_params(("parallel",)),
    )


def _mlp_up(order, h2, wups):
    tm = 512

    def body(h_ref, w_ref, u_ref, a_ref):
        u = _dot(h_ref[...], w_ref[...])
        u_ref[...] = u
        ru = jnp.maximum(u, 0.0)
        a_ref[...] = (ru * ru).astype(bf16)

    out = pl.BlockSpec((tm, D), lambda q, i: (i, q))
    return _call(
        order, body, (h2, wups), name="mlp_up", grid=(NCHIP, S // tm),
        in_specs=[pl.BlockSpec((tm, D), lambda q, i: (i, 0)), pl.BlockSpec((None, D, D), lambda q, i: (q, 0, 0))],
        out_specs=[out, out],
        out_shape=[jax.ShapeDtypeStruct((S, DFF), f32), jax.ShapeDtypeStruct((S, DFF), bf16)],
        compiler_params=_params(("parallel", "parallel")),
    )


def _mlp_down_loss(order, a, wdown, x2, g3, tgt):
    tm = 256

    def body(a_ref, w_ref, x2_ref, g_ref, t_ref, dx_ref, dxb_ref, dg_ref, loss_ref):
        i = pl.program_id(0)
        x3 = x2_ref[...] + _dot(a_ref[...], w_ref[...])
        r = lax.rsqrt(jnp.mean(x3 * x3, axis=-1, keepdims=True) + EPS)
        xh = x3 * r
        g = g_ref[...]
        e = xh * g - t_ref[...]
        part = 0.5 * jnp.sum(jnp.mean(e * e, axis=-1, keepdims=True), axis=0, keepdims=True)
        dy = e * (1.0 / D)
        gdy = dy * g
        dx = r * (gdy - xh * jnp.mean(gdy * xh, axis=-1, keepdims=True))
        dx_ref[...] = dx
        dxb_ref[...] = dx.astype(bf16)

        @pl.when(i == 0)
        def _():
            dg_ref[...] = jnp.zeros_like(dg_ref)
            loss_ref[...] = jnp.zeros_like(loss_ref)

        dg_ref[...] += jnp.sum(dy * xh, axis=0, keepdims=True)
        loss_ref[...] += jnp.broadcast_to(part, (1, 128))

    row = pl.BlockSpec((tm, D), lambda i: (i, 0))
    vec = pl.BlockSpec((1, D), lambda i: (0, 0))
    return _call(
        order, body, (a, wdown, x2, g3, tgt), name="mlp_down_loss", grid=(S // tm,),
        in_specs=[pl.BlockSpec((tm, DFF), lambda i: (i, 0)), pl.BlockSpec((DFF, D), lambda i: (0, 0)), row, vec, row],
        out_specs=[row, row, vec, pl.BlockSpec((1, 128), lambda i: (0, 0))],
        out_shape=[jax.ShapeDtypeStruct((S, D), f32), jax.ShapeDtypeStruct((S, D), bf16),
                   jax.ShapeDtypeStruct((1, D), f32), jax.ShapeDtypeStruct((1, 128), f32)],
        compiler_params=_params(("arbitrary",)),
    )


def _mlp_down_bwd(order, dx3b, wdown, u):
    tm = 256

    def body(d_ref, w_ref, u_ref, du_ref):
        d = d_ref[...]
        for q in range(NCHIP):
            cols = slice(q * D, (q + 1) * D)
            da = _dot_nt(d, w_ref[cols, :])
            du_ref[:, cols] = (da * (2.0 * jnp.maximum(u_ref[:, cols], 0.0))).astype(bf16)

    return _call(
        order, body, (dx3b, wdown, u), name="mlp_down_bwd", grid=(S // tm,),
        in_specs=[pl.BlockSpec((tm, D), lambda i: (i, 0)), pl.BlockSpec((DFF, D), lambda i: (0, 0)),
                  pl.BlockSpec((tm, DFF), lambda i: (i, 0))],
        out_specs=pl.BlockSpec((tm, DFF), lambda i: (i, 0)),
        out_shape=jax.ShapeDtypeStruct((S, DFF), bf16),
        compiler_params=_params(("parallel",)),
    )


def _mlp_up_bwd(order, du, wups, x2, dx3, g2):
    tm = 256

    def body(du_ref, w_ref, x2_ref, dx3_ref, g_ref, dx2_ref, dx2b_ref, dg_ref):
        i = pl.program_id(0)
        dh = jnp.zeros((tm, D), f32)
        for q in range(NCHIP):
            dh = dh + _dot_nt(du_ref[:, q * D:(q + 1) * D], w_ref[q])
        x2 = x2_ref[...]
        r = lax.rsqrt(jnp.mean(x2 * x2, axis=-1, keepdims=True) + EPS)
        xh = x2 * r
        gdh = dh * g_ref[...]
        dx2 = dx3_ref[...] + r * (gdh - xh * jnp.mean(gdh * xh, axis=-1, keepdims=True))
        dx2_ref[...] = dx2
        dx2b_ref[...] = dx2.astype(bf16)

        @pl.when(i == 0)
        def _():
            dg_ref[...] = jnp.zeros_like(dg_ref)

        dg_ref[...] += jnp.sum(dh * xh, axis=0, keepdims=True)

    row = pl.BlockSpec((tm, D), lambda i: (i, 0))
    vec = pl.BlockSpec((1, D), lambda i: (0, 0))
    return _call(
        order, body, (du, wups, x2, dx3, g2), name="mlp_up_bwd", grid=(S // tm,),
        in_specs=[pl.BlockSpec((tm, DFF), lambda i: (i, 0)), pl.BlockSpec((NCHIP, D, D), lambda i: (0, 0, 0)),
                  row, row, vec],
        out_specs=[row, row, vec],
        out_shape=[jax.ShapeDtypeStruct((S, D), f32), jax.ShapeDtypeStruct((S, D), bf16),
                   jax.ShapeDtypeStruct((1, D), f32)],
        compiler_params=_params(("arbitrary",)),
    )


def _gate_bwd(order, dx2b, wout, gates, ya, yb):
    tm = 256

    def body(d_ref, w_ref, g_ref, ya_ref, yb_ref, dya_ref, dyb_ref, dproj_ref):
        dm = _dot_nt(d_ref[...], w_ref[...])
        sa = jax.nn.sigmoid(g_ref[:, 0:D])
        sb = jax.nn.sigmoid(g_ref[:, D:2 * D])
        dya_ref[...] = (dm * sa).astype(bf16)
        dyb_ref[...] = (dm * sb).astype(bf16)
        dproj_ref[:, 0:D] = (dm * ya_ref[...] * (sa * (1.0 - sa))).astype(bf16)
        dproj_ref[:, D:2 * D] = (dm * yb_ref[...] * (sb * (1.0 - sb))).astype(bf16)

    row = lambda w: pl.BlockSpec((tm, w), lambda i: (i, 0))
    return _call(
        order, body, (dx2b, wout, gates, ya, yb), name="gate_bwd", grid=(S // tm,),
        in_specs=[row(D), pl.BlockSpec((D, D), lambda i: (0, 0)), row(2 * D), row(D), row(D)],
        out_specs=[row(D), row(D), pl.BlockSpec((tm, 2 * D), lambda i: (i, F_G // (2 * D)))],
        out_shape=[jax.ShapeDtypeStruct((S, D), bf16), jax.ShapeDtypeStruct((S, D), bf16),
                   jax.ShapeDtypeStruct((S, NP), bf16)],
        compiler_params=_params(("parallel",)),
    )


def _branch_bwd(order, dya, dyb, was, wbs):
    tm = 512

    def body(dya_ref, dyb_ref, wa_ref, wb_ref, doa_ref, dob_ref):
        doa = jnp.zeros((tm, FOXW), f32)
        dob = jnp.zeros((tm, DILOUT), f32)
        for q in range(NCHIP):
            cols = slice(q * 256, (q + 1) * 256)
            doa = doa + _dot_nt(dya_ref[:, cols], wa_ref[q])
            dob = dob + _dot_nt(dyb_ref[:, cols], wb_ref[q])
        doa_ref[...] = doa.astype(bf16)
        dob_ref[...] = dob

    row = lambda w: pl.BlockSpec((tm, w), lambda i: (i, 0))
    full3 = lambda a: pl.BlockSpec(a.shape, lambda i: (0, 0, 0))
    return _call(
        order, body, (dya, dyb, was, wbs), name="branch_bwd", grid=(S // tm,),
        in_specs=[row(D), row(D), full3(was), full3(wbs)],
        out_specs=[row(FOXW), row(DILOUT)],
        out_shape=[jax.ShapeDtypeStruct((S, FOXW), bf16), jax.ShapeDtypeStruct((S, DILOUT), f32)],
        compiler_params=_params(("parallel",)),
    )


def _branch_wgrad(order, oa, ob, dya, dyb):
    def body(oa_ref, ob_ref, dya_ref, dyb_ref, dwa_ref, dwb_ref):
        dwa_ref[...] = _dot_tn(oa_ref[...], dya_ref[...])
        dwb_ref[...] = _dot_tn(ob_ref[...], dyb_ref[...])

    full = lambda w: pl.BlockSpec((S, w), lambda q: (0, 0))
    colq = pl.BlockSpec((S, 256), lambda q: (0, q))
    return _call(
        order, body, (oa, ob, dya, dyb), name="branch_wgrad", grid=(NCHIP,),
        in_specs=[full(FOXW), full(DILOUT), colq, colq],
        out_specs=[pl.BlockSpec((None, FOXW, 256), lambda q: (q, 0, 0)),
                   pl.BlockSpec((None, DILOUT, 256), lambda q: (q, 0, 0))],
        out_shape=[jax.ShapeDtypeStruct((NCHIP, FOXW, 256), f32), jax.ShapeDtypeStruct((NCHIP, DILOUT, 256), f32)],
        compiler_params=_params(("parallel",)),
    )


def _fox_bwd(order, qkva, doa, oa, lse, F, dproj):
    tq, tk = FOX_TQ, FOX_TK

    def body(qkv_ref, do_ref, o_ref, lse_ref, F_ref, _dproj_in, dF_ref, dqkv_ref, qa, ka, da, va, kat,
             dk_scr, dv_scr, dqt_scr):
        p = pl.program_id(0)
        lane, hm = _head_masks()
        keyi = lax.broadcasted_iota(jnp.int32, (tk, 1), 0)
        qryi = lax.broadcasted_iota(jnp.int32, (1, tq), 1)

        def prep(i, c):
            rows = pl.ds(pl.multiple_of(i * tk, tk), tk)
            _fox_operands(qkv_ref, F_ref, lse_ref, qa, ka, p, rows)
            do = do_ref[rows, :].astype(f32)
            prod = do * o_ref[rows, :].astype(f32)
            v = qkv_ref[rows, 256:384].astype(f32)
            for hh in (0, 1):
                free = (1 - hh) * HD
                delta = jnp.sum(jnp.where(hm[hh], prod, 0.0), axis=1, keepdims=True)
                da[hh, rows, :] = _set_lanes(jnp.where(hm[hh], do, 0.0), lane, free,
                                             [-t for t in _f32_parts(delta)]).astype(bf16)
                va[hh, rows, :] = _set_lanes(v, lane, free, [1.0] * 3).astype(bf16)
                kat[hh, i] = ka[hh, rows, :].astype(f32).T.astype(bf16)
                dk_scr[hh, rows, :] = jnp.zeros((tk, 128), f32)
                dv_scr[hh, rows, :] = jnp.zeros((tk, 128), f32)
            return c

        lax.fori_loop(0, S // tk, prep, 0)

        def qblock(i, c):
            r0 = pl.multiple_of(i * tq, tq)
            qrows = pl.ds(r0, tq)
            qh = [qa[hh, qrows, :] for hh in (0, 1)]
            dh = [da[hh, qrows, :] for hh in (0, 1)]
            dqt_scr[...] = jnp.zeros_like(dqt_scr)

            def kv(jb, c2, masked):
                keys = pl.ds(pl.multiple_of(jb * tk, tk), tk)
                sts = [_dot_nt(ka[hh, keys, :], qh[hh]) for hh in (0, 1)]
                dps = [_dot_nt(va[hh, keys, :], dh[hh]) for hh in (0, 1)]
                for hh in (0, 1):
                    pt = jnp.exp(sts[hh])
                    if masked:
                        pt = jnp.where(jb * tk + keyi <= r0 + qryi, pt, 0.0)
                    dsb = (pt * dps[hh]).astype(bf16)
                    dv_scr[hh, keys, :] += _dot(pt.astype(bf16), dh[hh])
                    dk_scr[hh, keys, :] += _dot(dsb, qh[hh])
                    dqt_scr[hh] += _dot(kat[hh, jb], dsb)
                return c2

            last = (r0 + tq - 1) // tk
            lax.fori_loop(0, last, lambda j, c2: kv(j, c2, False), 0)
            kv(last, 0, True)
            dq0, dq1 = dqt_scr[0].T, dqt_scr[1].T
            dqkv_ref[qrows, 0:128] = (jnp.where(hm[0], dq0, dq1) * 0.125).astype(bf16)
            dF_ref[qrows, :] = jnp.where(lane == 0, dq0[:, HD:HD + 1], jnp.where(lane == 1, dq1[:, 0:1], 0.0))
            return c

        lax.fori_loop(0, S // tq, qblock, 0)

        def finish(i, c):
            rows = pl.ds(pl.multiple_of(i * tq, tq), tq)
            dk0, dk1 = dk_scr[0, rows, :], dk_scr[1, rows, :]
            dqkv_ref[rows, 128:256] = jnp.where(hm[0], dk0, dk1).astype(bf16)
            dqkv_ref[rows, 256:384] = jnp.where(hm[0], dv_scr[0, rows, :], dv_scr[1, rows, :]).astype(bf16)
            cs = jnp.where(lane == 0, dk0[:, HD + L_ONE:HD + L_ONE + 1],
                           jnp.where(lane == 1, dk1[:, L_ONE:L_ONE + 1], 0.0))
            dF_ref[rows, :] = dF_ref[rows, :] - cs
            return c

        lax.fori_loop(0, S // tq, finish, 0)

    pair = pl.BlockSpec((S, 128), lambda p: (0, p))
    return _call(
        order, body, (qkva, doa, oa, lse, F, dproj), name="fox_bwd", grid=(4,),
        in_specs=[pl.BlockSpec((S, FOX_BLK), lambda p: (0, p)), pair, pair, pair,
                  pl.BlockSpec((S, 128), lambda p: (0, 0)), pl.BlockSpec(memory_space=pl.ANY)],
        out_specs=[pair, pl.BlockSpec((S, FOX_BLK), lambda p: (0, F_FOX // FOX_BLK + p))],
        out_shape=[jax.ShapeDtypeStruct((S, FOXW), f32), jax.ShapeDtypeStruct((S, NP), bf16)],
        input_output_aliases={5: 1},
        scratch_shapes=[pltpu.VMEM((2, S, 128), bf16)] * 4 + [pltpu.VMEM((2, S // tk, 128, tk), bf16)]
        + [pltpu.VMEM((2, S, 128), f32)] * 2 + [pltpu.VMEM((2, 128, tq), f32)],
        compiler_params=_params(("parallel",)),
    )


def _forget_bwd(order, dF, fa, bpad, dproj):
    nb = S // TQ

    def body(dF_ref, fa_ref, b_ref, _dproj_in, db_ref, dfa_ref):
        rr = lax.broadcasted_iota(jnp.int32, (TQ, TQ), 0)
        cc = lax.broadcasted_iota(jnp.int32, (TQ, TQ), 1)
        upper = (cc >= rr).astype(bf16)
        lane = lax.broadcasted_iota(jnp.int32, (1, 128), 1)
        carry = jnp.zeros((1, 128), f32)
        db = jnp.zeros((1, 128), f32)
        for b in reversed(range(nb)):
            cols = jnp.zeros((TQ, 128), f32)
            for h in range(8):
                c0 = (h // 2) * 128 + h % 2
                cols = jnp.where(lane == h, dF_ref[b * TQ:(b + 1) * TQ, c0:c0 + 1], cols)
            dlf = carry
            for part in _split3(cols):
                dlf = dlf + _dot(upper, part)
            carry = carry + jnp.sum(cols, axis=0, keepdims=True)
            z = fa_ref[b * TQ:(b + 1) * TQ, :] + b_ref[...]
            dz = jnp.where(lane < 8, dlf * jax.nn.sigmoid(-z), 0.0)
            dfa_ref[b * TQ:(b + 1) * TQ, 0:128] = dz.astype(bf16)
            dfa_ref[b * TQ:(b + 1) * TQ, 128:256] = jnp.zeros((TQ, 128), bf16)
            db = db + jnp.sum(dz, axis=0, keepdims=True)
        db_ref[...] = db

    whole = lambda a: pl.BlockSpec(a.shape, lambda i: (0,) * a.ndim)
    return _call(
        order, body, (dF, fa, bpad, dproj), name="forget_bwd", grid=(1,),
        in_specs=[whole(dF), whole(fa), whole(bpad), pl.BlockSpec(memory_space=pl.ANY)],
        out_specs=[pl.BlockSpec((1, 128), lambda i: (0, 0)), pl.BlockSpec((S, 256), lambda i: (0, F_FA // 256))],
        out_shape=[jax.ShapeDtypeStruct((1, 128), f32), jax.ShapeDtypeStruct((S, NP), bf16)],
        input_output_aliases={3: 1},
        compiler_params=_params(("arbitrary",)),
    )


def _dil_bwd(order, qkvb, dob, ob, lseb, rope, dproj):
    c_t, s1_t, s2_t = rope

    def body(*refs):
        q_refs, k_refs, v_refs = refs[0:3], refs[3:6], refs[6:9]
        dob_ref, ob_ref, lse_ref, c_ref, s1_ref, s2_ref, _dproj_in, dqkv_ref = refs[9:17]
        qp, kp, vp, dop, lp, dlp, dln, dqp, dkp, dvp, nat = refs[17:28]
        dq_out, dk_out, dv_out = _dil_views(dqkv_ref)
        _, hm = _head_masks()

        def delta_rows(i, c):
            r0 = pl.multiple_of(i * TQ, TQ)
            prod = dob_ref[pl.ds(r0, TQ), :] * ob_ref[pl.ds(r0, TQ), :].astype(f32)
            d0 = jnp.sum(jnp.where(hm[0], prod, 0.0), axis=1, keepdims=True)
            d1 = jnp.sum(jnp.where(hm[1], prod, 0.0), axis=1, keepdims=True)
            dln[pl.ds(r0, TQ), :] = jnp.where(hm[0], d0, d1)
            return c

        lax.fori_loop(0, S // TQ, delta_rows, 0)

        for g, r in enumerate(DIL):
            nbl = S // r // BAND
            if r == 1:
                srcs = (q_refs[g], k_refs[g], v_refs[g], dob_ref, lse_ref, dln)
            else:
                for dst, src in ((qp, q_refs[g]), (kp, k_refs[g]), (vp, v_refs[g]), (dop, dob_ref),
                                 (lp, lse_ref), (dlp, dln)):
                    _permute_in(dst, src, r)
                srcs = (qp, kp, vp, dop, lp, dlp)
            dkp[...] = jnp.zeros_like(dkp)
            dvp[...] = jnp.zeros_like(dvp)

            def blk(t, c, srcs=srcs, nbl=nbl):
                qs_, ks_, vs_, dos_, ls_, dls_ = srcs
                work = []
                for u in range(DIL_UNROLL):
                    r0, k0, valid = _band_geometry(DIL_UNROLL * t + u, nbl)
                    q = qs_[pl.ds(r0, BAND), :] * 0.125
                    kwf = ks_[pl.ds(k0, 2 * BAND), :]
                    kw = kwf.astype(bf16)
                    vw = vs_[pl.ds(k0, 2 * BAND), :].astype(bf16)
                    do = dos_[pl.ds(r0, BAND), :]
                    lse = ls_[pl.ds(r0, BAND), :]
                    dlt = dls_[pl.ds(r0, BAND), :]
                    for hh in (0, 1):
                        qh = jnp.where(hm[hh], q, 0.0).astype(bf16)
                        doh = jnp.where(hm[hh], do, 0.0).astype(bf16)
                        kh = jnp.where(hm[hh], kwf, 0.0).astype(bf16)
                        work.append((u, hh, r0, k0, valid, qh, doh, kh, lse[:, hh * HD:hh * HD + 1],
                                     dlt[:, hh * HD:hh * HD + 1], _dot_nt(qh, kw), _dot_nt(doh, vw)))
                for u, hh, r0, k0, valid, qh, doh, kh, lse_h, dlt_h, s, dp in work:
                    if hh == 0:
                        dq = jnp.zeros((BAND, 128), f32)
                        dk = jnp.zeros((2 * BAND, 128), f32)
                        dv = jnp.zeros((2 * BAND, 128), f32)
                    pr = jnp.where(valid, jnp.exp(s - lse_h), 0.0)
                    dsb = (pr * (dp - dlt_h)).astype(bf16)
                    dv = dv + _dot_tn(pr.astype(bf16), doh)
                    dk = dk + _dot_tn(dsb, qh)
                    dq = dq + _dot(dsb, kh)
                    if hh == 1:
                        dqp[pl.ds(r0, BAND), :] = dq * 0.125
                        dkp[pl.ds(k0, 2 * BAND), :] += dk
                        dvp[pl.ds(k0, 2 * BAND), :] += dv
                return c

            lax.fori_loop(0, S // BAND // DIL_UNROLL, blk, 0)

            for acc, out, roped in ((dqp, dq_out[g], True), (dkp, dk_out[g], True), (dvp, dv_out[g], False)):
                if r == 1:
                    src = acc
                else:
                    _permute_out(nat, acc, r)
                    src = nat

                def emit(i, c, src=src, out=out, roped=roped):
                    r0 = pl.multiple_of(i * TQ, TQ)
                    d = src[pl.ds(r0, TQ), :]
                    if roped:
                        d = (d * c_ref[pl.ds(r0, TQ), :] + pltpu.roll(d * s1_ref[pl.ds(r0, TQ), :], 8, 1)
                             + pltpu.roll(d * s2_ref[pl.ds(r0, TQ), :], 120, 1))
                    out[pl.ds(r0, TQ), :] = d.astype(bf16)
                    return c

                lax.fori_loop(0, S // TQ, emit, 0)

    pair = pl.BlockSpec((S, 128), lambda p: (0, p))
    tab = pl.BlockSpec((S, 128), lambda p: (0, 0))
    blk_spec = pl.BlockSpec((S, DIL_BLK), lambda p: (0, p))
    return _call(
        order, body, [qkvb] * 9 + [dob, ob, lseb, c_t, s1_t, s2_t, dproj], name="dil_bwd", grid=(2,),
        in_specs=_dil_in_specs() + [pair, pair, pair, tab, tab, tab, pl.BlockSpec(memory_space=pl.ANY)],
        out_specs=blk_spec,
        out_shape=jax.ShapeDtypeStruct((S, NP), bf16),
        input_output_aliases={15: 0},
        scratch_shapes=[pltpu.VMEM((S, 128), f32)] * 11,
        compiler_params=_params(("parallel",)),
    )


def _inproj_bwd(order, dproj, wt, x, dx2, g1):
    tm = 256

    def body(d_ref, w_ref, x_ref, dx2_ref, g_ref, dx_ref, dg_ref):
        i = pl.program_id(0)
        dh = _dot(d_ref[...], w_ref[...])
        xb = x_ref[...]
        r = lax.rsqrt(jnp.mean(xb * xb, axis=-1, keepdims=True) + EPS)
        xh = xb * r
        gdh = dh * g_ref[...]
        dx_ref[...] = dx2_ref[...] + r * (gdh - xh * jnp.mean(gdh * xh, axis=-1, keepdims=True))

        @pl.when(i == 0)
        def _():
            dg_ref[...] = jnp.zeros_like(dg_ref)

        dg_ref[...] += jnp.sum(dh * xh, axis=0, keepdims=True)

    row = pl.BlockSpec((tm, D), lambda i: (i, 0))
    vec = pl.BlockSpec((1, D), lambda i: (0, 0))
    return _call(
        order, body, (dproj, wt, x, dx2, g1), name="inproj_bwd", grid=(S // tm,),
        in_specs=[pl.BlockSpec((tm, NP), lambda i: (i, 0)), pl.BlockSpec((NP, D), lambda i: (0, 0)), row, row, vec],
        out_specs=[row, vec],
        out_shape=[jax.ShapeDtypeStruct((S, D), f32), jax.ShapeDtypeStruct((1, D), f32)],
        compiler_params=_params(("arbitrary",)),
    )


HBM = pl.BlockSpec(memory_space=pltpu.HBM)
SEM = pl.BlockSpec(memory_space=pltpu.SEMAPHORE)
SMALL_ROWS = 8


def _comm_call(name, body, bufs, order, sems_in=(), new_sems=()):
    nb, ns, nn = len(bufs), len(sems_in), len(new_sems)
    extra = order.token_for(bufs)

    def kern(*refs):
        off = nb + ns + len(extra)
        body(refs[:nb], refs[nb:nb + ns], refs[off:off + nn])
        refs[-1][...] = jnp.zeros((8, 128), f32)

    res = pl.pallas_call(
        kern, name=name,
        in_specs=[HBM] * nb + [SEM] * ns + [pl.BlockSpec(memory_space=pl.ANY)] * len(extra),
        out_specs=[SEM] * nn + [HBM] * nb + [pl.BlockSpec(memory_space=pltpu.VMEM)],
        out_shape=[pltpu.SemaphoreType.DMA((k,)) for k in new_sems] + [pltpu.HBM(b.shape, b.dtype) for b in bufs]
        + [jax.ShapeDtypeStruct((8, 128), f32)],
        input_output_aliases={i: nn + i for i in range(nb)},
        compiler_params=pltpu.CompilerParams(has_side_effects=pltpu.SideEffectType.DATAFLOW_SIDE_EFFECTING),
    )(*[pltpu.with_memory_space_constraint(b, pltpu.HBM) for b in bufs], *sems_in, *extra)
    order.mark(res[-1])
    return list(res[:nn]), list(res[nn:nn + nb])


def _place():
    x, y, c = lax.axis_index("x"), lax.axis_index("y"), lax.axis_index("c")
    chips = [(1 - x, y), (x, 1 - y), (1 - x, 1 - y)]
    return x, y, c, chips


def _rcopy(src, dst, ssem, rsem, dev):
    return pltpu.make_async_remote_copy(src_ref=src, dst_ref=dst, send_sem=ssem, recv_sem=rsem,
                                        device_id=dev, device_id_type=pl.DeviceIdType.MESH)


def _half(nrows, which):
    return pl.ds(which * (nrows // 2), nrows // 2)


def _ici_copies(stack, group_sizes, ssems, rsems):
    x, y, c, chips = _place()
    me_q = 2 * x + y
    sends, recvs = [], []
    a = 0
    for grp, size in enumerate(group_sizes):
        for k in range(size):
            rows = _half(stack[a].shape[1], c)
            for j, (cx, cy) in enumerate(chips):
                mine = stack[a].at[me_q, rows]
                sends.append(_rcopy(mine, mine, ssems[grp].at[k * 3 + j], rsems[grp].at[k * 3 + j], (cx, cy, c)))
                theirs = stack[a].at[2 * cx + cy, rows]
                recvs.append(_rcopy(theirs, theirs, ssems[grp].at[k * 3 + j], rsems[grp].at[k * 3 + j],
                                    (cx, cy, c)))
            a += 1
    return sends, recvs


def _allgather_start(stacks, group_sizes, order):
    def body(bufs, _, new):
        sends, _r = _ici_copies(bufs, group_sizes, new[0::2], new[1::2])
        for cp in sends:
            cp.start()

    sizes = []
    for size in group_sizes:
        sizes += [3 * size, 3 * size]
    sems, stacks = _comm_call("allgather_start", body, stacks, order, new_sems=sizes)
    return [(sems[2 * g], sems[2 * g + 1]) for g in range(len(group_sizes))], stacks


def _forward_copies(stack, ssem, rsem):
    x, y, c, chips = _place()
    sib = (x, y, 1 - c)
    sends, recvs = [], []
    for a in range(len(stack)):
        for j, (cx, cy) in enumerate(chips):
            landed = stack[a].at[2 * cx + cy, _half(stack[a].shape[1], c)]
            sends.append(_rcopy(landed, landed, ssem.at[a * 3 + j], rsem.at[a * 3 + j], sib))
            other = stack[a].at[2 * cx + cy, _half(stack[a].shape[1], 1 - c)]
            recvs.append(_rcopy(other, other, ssem.at[a * 3 + j], rsem.at[a * 3 + j], sib))
    return sends, recvs


def _allgather_forward(name, stacks, sems, order):
    n = len(stacks)

    def body(bufs, taken, new):
        sends, recvs = _ici_copies(bufs, [n], [taken[0]], [taken[1]])
        for cp in sends:
            cp.wait_send()
        for cp in recvs:
            cp.wait_recv()
        fwd, _r = _forward_copies(bufs, new[0], new[1])
        for cp in fwd:
            cp.start()

    return _comm_call(name, body, stacks, order, sems_in=sems, new_sems=(3 * n, 3 * n))


def _allgather_finish(name, stacks, sems, order):
    def body(bufs, taken, _):
        sends, recvs = _forward_copies(bufs, taken[0], taken[1])
        for cp in sends:
            cp.wait_send()
        for cp in recvs:
            cp.wait_recv()

    return _comm_call(name, body, stacks, order, sems_in=sems)[1]


def _window_unit(q, j):
    return C2I[WIN_UNIT0[q] + j]


def _pair_copies(g, t, ssem, rsem, gathered):
    x, y, c, _ = _place()
    sib = (x, y, 1 - c)
    cps, whole = [], []
    for a in range(len(g)):
        if a == 0 and gathered:
            for q in range(NCHIP):
                for j in range(WIN_UNITS // 2):
                    u = jnp.where(c == 0, _window_unit(q, WIN_UNITS // 2 + j), _window_unit(q, j))
                    src = g[0].at[pl.ds(pl.multiple_of(u * UNIT, UNIT), UNIT), :]
                    cps.append(_rcopy(src, t[0].at[q, pl.ds(j * UNIT, UNIT), :], ssem.at[0], rsem.at[0], sib))
            whole.append(_rcopy(t[0], t[0], ssem.at[0], rsem.at[0], sib))
        else:
            cp = _rcopy(g[a].at[:, _half(g[a].shape[1], 1 - c), :], t[a], ssem.at[a], rsem.at[a], sib)
            cps.append(cp)
            whole.append(cp)
    return cps, whole


def _comm_multi(name, parts, order):
    def body(buf_refs, taken, new):
        ib = it = inew = 0
        for pbody, pbufs, psems, pnew, _ in parts:
            pbody(buf_refs[ib:ib + len(pbufs)], taken[it:it + len(psems)], new[inew:inew + len(pnew)])
            ib, it, inew = ib + len(pbufs), it + len(psems), inew + len(pnew)

    sems, bufs = _comm_call(name, body, [b for p in parts for b in p[1]], order,
                            sems_in=[s for p in parts for s in p[2]], new_sems=[k for p in parts for k in p[3]])
    out, ib, inew = [], 0, 0
    for _, pbufs, _, pnew, unpack in parts:
        out.append(unpack(sems[inew:inew + len(pnew)], bufs[ib:ib + len(pbufs)]))
        ib, inew = ib + len(pbufs), inew + len(pnew)
    return out


def _pair_start_part(gs, gathered=False):
    n = len(gs)
    ts = [lax.empty((NCHIP, WIN_ROWS // 2, D) if (a == 0 and gathered) else (NCHIP, g.shape[1] // 2, g.shape[2]), f32)
          for a, g in enumerate(gs)]

    def body(bufs, _, new):
        for cp in _pair_copies(bufs[:n], bufs[n:], new[0], new[1], gathered)[0]:
            cp.start()

    return body, list(gs) + ts, (), (n, n), lambda sems, bufs: (sems, bufs)


def _pair_wait_part(bufs, sems, gathered=False):
    n = len(bufs) // 2

    def body(refs, taken, _):
        for cp in _pair_copies(refs[:n], refs[n:], taken[0], taken[1], gathered)[1]:
            cp.wait_send()
            cp.wait_recv()

    return body, list(bufs), list(sems), (), lambda _, out: (out[:n], out[n:])


def _row_tile(h):
    return min(h, 256)


def _pair_add(order, g, t, c_arr, name):
    _, R, C = g.shape
    h = R // 2
    tr = _row_tile(h)
    nblk = h // tr

    def body(c_ref, g_ref, t_ref, p32_ref, p16_ref):
        s = g_ref[...] + t_ref[...]
        p32_ref[...] = s
        p16_ref[...] = s.astype(bf16)

    blk = pl.BlockSpec((None, tr, C), lambda q, i, c_ref: (q, i, 0))
    return _call_indexed(
        order, body, (c_arr,), (g, t), (NCHIP, nblk),
        [pl.BlockSpec((None, tr, C), lambda q, i, c_ref: (q, c_ref[0] * nblk + i, 0)), blk], [blk, blk],
        name=name,
        out_shape=[jax.ShapeDtypeStruct((NCHIP, h, C), f32), jax.ShapeDtypeStruct((NCHIP, h, C), bf16)],
        compiler_params=_params(("parallel", "parallel")),
    )


def _pair_add_gathered(order, dwt, t, c_arr, name):
    half_units, half_rows = WIN_UNITS // 2, WIN_ROWS // 2
    table = jnp.asarray([_window_unit(q, j) for q in range(NCHIP) for j in range(WIN_UNITS)], jnp.int32)

    def body(tab_ref, c_ref, g_hbm, t_ref, p32_ref, p16_ref, buf, sem):
        q = pl.program_id(0)

        def gather(w, slot):
            cps = []
            for j in range(half_units):
                u = tab_ref[w * WIN_UNITS + c_ref[0] * half_units + j]
                cps.append(pltpu.make_async_copy(g_hbm.at[pl.ds(pl.multiple_of(u * UNIT, UNIT), UNIT), :],
                                                 buf.at[slot, pl.ds(j * UNIT, UNIT), :], sem.at[slot]))
            return cps

        @pl.when(q == 0)
        def _():
            for cp in gather(0, 0):
                cp.start()

        @pl.when(q + 1 < NCHIP)
        def _():
            for cp in gather(q + 1, (q + 1) % 2):
                cp.start()

        slot = q % 2
        pltpu.make_async_copy(buf.at[slot], buf.at[slot], sem.at[slot]).wait()
        s = buf[slot] + t_ref[...]
        p32_ref[...] = s
        p16_ref[...] = s.astype(bf16)

    blk = pl.BlockSpec((None, half_rows, D), lambda q, tab_ref, c_ref: (q, 0, 0))
    return _call_indexed(
        order, body, (table, c_arr), (dwt, t), (NCHIP,),
        [pl.BlockSpec(memory_space=pl.ANY), blk], [blk, blk],
        scratch_shapes=[pltpu.VMEM((2, half_rows, D), f32), pltpu.SemaphoreType.DMA((2,))],
        name=name,
        out_shape=[jax.ShapeDtypeStruct((NCHIP, half_rows, D), f32),
                   jax.ShapeDtypeStruct((NCHIP, half_rows, D), bf16)],
        compiler_params=_params(("arbitrary",)),
    )


def _shard_copies(p, r, sm, ssem, rsem):
    x, y, c, chips = _place()
    n = len(p)
    sends, recvs = [], []
    for a in range(n):
        for j, (cx, cy) in enumerate(chips):
            k = a * 3 + j
            sends.append(_rcopy(p[a].at[2 * cx + cy], r[a].at[j], ssem.at[k], rsem.at[k], (cx, cy, c)))
            recvs.append(_rcopy(r[a].at[j], r[a].at[j], ssem.at[k], rsem.at[k], (cx, cy, c)))
    if sm is not None:
        mine = sm.at[4 * x + 2 * y + c]
        for i in range(1, 8):
            px = (1 - x) if i & 4 else x
            py = (1 - y) if i & 2 else y
            pc = (1 - c) if i & 1 else c
            k = 3 * n + i - 1
            sends.append(_rcopy(mine, mine, ssem.at[k], rsem.at[k], (px, py, pc)))
            slot = sm.at[4 * px + 2 * py + pc]
            recvs.append(_rcopy(slot, slot, ssem.at[k], rsem.at[k], (px, py, pc)))
    return sends, recvs


def _shard_start_part(p16s, sm=None):
    n = len(p16s)
    rs = [lax.empty((3,) + p.shape[1:], bf16) for p in p16s]
    extra = [] if sm is None else [sm]
    nsem = 3 * n + (7 if sm is not None else 0)

    def body(bufs, _, new):
        sends, _r = _shard_copies(bufs[:n], bufs[n:2 * n], bufs[2 * n] if extra else None, new[0], new[1])
        for cp in sends:
            cp.start()

    return body, list(p16s) + rs + extra, (), (nsem, nsem), lambda sems, bufs: (sems, bufs)


def _shard_wait_part(bufs, sems, n):
    has_sm = len(bufs) > 2 * n

    def body(refs, taken, _):
        sends, recvs = _shard_copies(refs[:n], refs[n:2 * n], refs[2 * n] if has_sm else None, taken[0], taken[1])
        for cp in sends:
            cp.wait_send()
        for cp in recvs:
            cp.wait_recv()

    return body, list(bufs), list(sems), (), lambda _, out: (out[n:2 * n], (out[2 * n] if has_sm else None))


def _shard_sum(order, p32, r, q_arr, c_arr, name):
    _, h, C = p32.shape
    tr = _row_tile(h)
    nblk = h // tr

    def body(q_ref, c_ref, p_ref, r_ref, o_ref):
        s = p_ref[...]
        for j in range(3):
            s = s + r_ref[j].astype(f32)
        o_ref[...] = s

    return _call_indexed(
        order, body, (q_arr, c_arr), (p32, r), (nblk,),
        [pl.BlockSpec((None, tr, C), lambda i, q_ref, c_ref: (q_ref[0], i, 0)),
         pl.BlockSpec((3, tr, C), lambda i, q_ref, c_ref: (0, i, 0))],
        pl.BlockSpec((tr, C), lambda i, q_ref, c_ref: (c_ref[0] * nblk + i, 0)),
        name=name, out_shape=jax.ShapeDtypeStruct((2 * h, C), f32),
        compiler_params=_params(("parallel",)),
    )


def _swap_copies(full, ssem, rsem):
    x, y, c, _ = _place()
    sends, recvs = [], []
    for a in range(len(full)):
        mine = full[a].at[_half(full[a].shape[0], c)]
        sends.append(_rcopy(mine, mine, ssem.at[a], rsem.at[a], (x, y, 1 - c)))
        other = full[a].at[_half(full[a].shape[0], 1 - c)]
        recvs.append(_rcopy(other, other, ssem.at[a], rsem.at[a], (x, y, 1 - c)))
    return sends, recvs


def _swap_start_part(fulls):
    n = len(fulls)

    def body(bufs, _, new):
        for cp in _swap_copies(bufs, new[0], new[1])[0]:
            cp.start()

    return body, list(fulls), (), (n, n), lambda sems, bufs: (sems, bufs)


def _swap_wait_part(fulls, sems):
    def body(refs, taken, _):
        sends, recvs = _swap_copies(refs, taken[0], taken[1])
        for cp in sends:
            cp.wait_send()
        for cp in recvs:
            cp.wait_recv()

    return body, list(fulls), list(sems), (), lambda _, out: out


def _small_sum(order, sm):
    def body(sm_ref, o_ref):
        s = sm_ref[0]
        for d in range(1, 8):
            s = s + sm_ref[d]
        o_ref[...] = s

    return _call(order, body, (sm,), name="small_grad_sum", out_shape=jax.ShapeDtypeStruct((SMALL_ROWS, D), f32))


def _adamw(order, w, g, m, v, name):
    R, C = w.shape
    if R <= 256 or R % 256 == 0:
        tr, tc = min(R, 256), C
    else:
        tr, tc = R, 128

    def body(w_ref, g_ref, m_ref, v_ref, d_ref, nm_ref, nv_ref):
        g_ = g_ref[...]
        m_ = ADAM_B1 * m_ref[...] + (1.0 - ADAM_B1) * g_
        v_ = ADAM_B2 * v_ref[...] + (1.0 - ADAM_B2) * (g_ * g_)
        m_hat = m_ / (1.0 - ADAM_B1 ** ADAM_STEP)
        v_hat = v_ / (1.0 - ADAM_B2 ** ADAM_STEP)
        d_ref[...] = -ADAM_LR * (m_hat / (jnp.sqrt(v_hat) + ADAM_EPS) + ADAM_WD * w_ref[...])
        nm_ref[...] = m_
        nv_ref[...] = v_

    blk = pl.BlockSpec((tr, tc), lambda i, j: (i, j))
    return _call(
        order, body, (w, g, m, v), name=name, grid=(R // tr, C // tc), in_specs=[blk] * 4, out_specs=[blk] * 3,
        out_shape=[jax.ShapeDtypeStruct((R, C), f32)] * 3,
        compiler_params=_params(("parallel", "parallel")),
    )


def _feature_major(w):
    return jnp.transpose(w, (2, 0, 1)).reshape(SHARD_IN, D)


def _unfeature_major(a):
    return jnp.transpose(a.reshape(SHARD_IN, 1, D), (1, 2, 0))


def _window_of(wt, q):
    def plain(k):
        return lambda w: jnp.pad(w, ((OWN_ROW0[k], WIN_ROWS - OWN_ROW0[k] - SHARD_IN), (0, 0))).astype(bf16)

    def chip1(w):
        lo = jnp.pad(w[0:62], ((2, WIN_ROWS - 64), (0, 0)))
        hi = jnp.pad(w[70:SHARD_IN], ((64, WIN_ROWS - 64 - (SHARD_IN - 70)), (0, 0)))
        return (lo + hi).astype(bf16)

    win = lax.switch(q, [plain(0), chip1, plain(2), plain(3)], wt)
    fa = jnp.pad(wt[62:70], ((0, FA_ROWS - 8), (0, 0))).astype(bf16)
    return win, fa


def _own_rows(gwin, gfa, q):
    def plain(k):
        return lambda gw, gf: gw[OWN_ROW0[k]:OWN_ROW0[k] + SHARD_IN]

    def chip1(gw, gf):
        return (jnp.pad(gw[2:64], ((0, SHARD_IN - 62), (0, 0))) + jnp.pad(gf[0:8], ((62, SHARD_IN - 70), (0, 0)))
                + jnp.pad(gw[64:64 + SHARD_IN - 70], ((70, 0), (0, 0))))

    return lax.switch(q, [plain(0), chip1, plain(2), plain(3)], gwin, gfa)


def kernel(x, norm_attn_g, w_in, b_forget, w_branch_a, w_branch_b, w_out, norm_mlp_g, w_up, w_down, norm_final_g, loss_target, m_norm_attn_g, m_w_in, m_b_forget, m_w_branch_a, m_w_branch_b, m_w_out, m_norm_mlp_g, m_w_up, m_w_down, m_norm_final_g, v_norm_attn_g, v_w_in, v_b_forget, v_w_branch_a, v_w_branch_b, v_w_out, v_norm_mlp_g, v_w_up, v_w_down, v_norm_final_g):
    xi, yi, ci = lax.axis_index("x"), lax.axis_index("y"), lax.axis_index("c")
    q_me = 2 * xi + yi
    c_arr = jnp.reshape(ci, (1,)).astype(jnp.int32)
    q_arr = jnp.reshape(q_me, (1,)).astype(jnp.int32)
    x_, tgt = x[0], loss_target[0]

    names = ["w_branch_a", "w_branch_b", "w_out", "w_up", "w_down"]
    big = dict(zip(names, [w_branch_a[0], w_branch_b[0], w_out[0], w_up[0], w_down[0]]))
    ms = dict(zip(names, [m_w_branch_a[0], m_w_branch_b[0], m_w_out[0], m_w_up[0], m_w_down[0]]))
    vs = dict(zip(names, [v_w_branch_a[0], v_w_branch_b[0], v_w_out[0], v_w_up[0], v_w_down[0]]))
    grad, upd = {}, {}
    order = _Order()

    def run(fn, *args, **kw):
        return fn(order, *args, **kw)

    def own_slot(a):
        return lax.dynamic_update_slice(lax.empty((NCHIP,) + a.shape, a.dtype), a[None], (q_me, 0, 0))

    wt_own = _feature_major(w_in)
    win, fa_blk = _window_of(wt_own, q_me)
    stacks = [own_slot(win), own_slot(fa_blk)] + [own_slot(w.astype(bf16)) for w in big.values()]
    (sem_in, sem_rest), stacks = _allgather_start(stacks, [2, 5], order)
    sem_f, in_s = _allgather_forward("allgather_forward_in", stacks[0:2], sem_in, order)
    wins, fas = _allgather_finish("allgather_finish_in", in_s, sem_f, order)
    wt = run(_assemble_win, wins, fas)

    rope = _rope_tables()
    bpad = jnp.pad(b_forget, ((0, 0), (0, 120)))
    h1, qkvb, qkva, gates, fa = run(_norm_inproj, x_, norm_attn_g, wt, rope)
    F = run(_forget_cumsum, fa, bpad)
    oa, lsea = run(_fox_fwd, qkva, F)
    sem_f, rest = _allgather_forward("allgather_forward_rest", stacks[2:], sem_rest, order)
    ob, lseb = run(_dil_fwd, qkvb)
    was, wbs, wouts, wups, wdowns = _allgather_finish("allgather_finish_rest", rest, sem_f, order)
    wout = wouts.reshape(D, D)
    wdown = wdowns.reshape(DFF, D)
    ya, yb, mixed = run(_branch_mix, oa, ob, was, wbs, gates)
    x2, h2 = run(_outproj_norm, mixed, wout, x_, norm_mlp_g)
    u, a = run(_mlp_up, h2, wups)
    dx3, dx3b, dg3, loss_part = run(_mlp_down_loss, a, wdown, x2, norm_final_g.reshape(1, D), tgt)
    loss = lax.psum(loss_part[0, 0], ("x", "y", "c"))

    def comm(name, *parts):
        return _comm_multi(name, list(parts), order)

    def pair_adds(group, gs, ts):
        return zip(*[run(_pair_add, gs[i], ts[i], c_arr, "pair_add_" + nm) for i, nm in enumerate(group)])

    def shard_sums(group, p32s, rs):
        return [run(_shard_sum, p32s[i], rs[i], q_arr, c_arr, "shard_sum_" + nm) for i, nm in enumerate(group)]

    def adamw_group(group, fulls):
        for nm, gfull in zip(group, fulls):
            grad[nm] = gfull
            upd[nm] = run(_adamw, big[nm], gfull, ms[nm], vs[nm], "adamw_" + nm)

    grp_a, grp_b, grp_c = ["w_down", "w_up"], ["w_out", "w_branch_a", "w_branch_b"], ["w_in", "w_in_fa"]
    du = run(_mlp_down_bwd, dx3b, wdown, u)
    dwdown = run(_mm, a, dx3b, "tn", f32, 1024, D, "wgrad_down")
    dwup = run(_mm, h2, du, "tn", f32, D, 1024, "wgrad_up", stack_cols=True)
    ((sem_pa, buf_pa),) = comm("pair_start_a", _pair_start_part([dwdown.reshape(NCHIP, DFF // NCHIP, D), dwup]))
    dx2, dx2b, dg2 = run(_mlp_up_bwd, du, wups, x2, dx3, norm_mlp_g)
    ((gs, ts),) = comm("pair_wait_a", _pair_wait_part(buf_pa, sem_pa))
    p32_a, p16_a = pair_adds(grp_a, gs, ts)
    ((sem_sa, buf_sa),) = comm("shard_start_a", _shard_start_part(p16_a))
    dya, dyb, dproj = run(_gate_bwd, dx2b, wout, gates, ya, yb)
    dwout = run(_mm, mixed, dx2b, "tn", f32, D, D, "wgrad_out")
    doa, dob = run(_branch_bwd, dya, dyb, was, wbs)
    dwas, dwbs = run(_branch_wgrad, oa, ob, dya, dyb)
    ((sem_pb, buf_pb),) = comm("pair_start_b", _pair_start_part([dwout.reshape(NCHIP, D // NCHIP, D), dwas, dwbs]))
    dF, dproj = run(_fox_bwd, qkva, doa, oa, lsea, F, dproj)
    (gs, ts), (rs_a, _) = comm("pair_wait_b_shard_wait_a", _pair_wait_part(buf_pb, sem_pb),
                               _shard_wait_part(buf_sa, sem_sa, len(grp_a)))
    p32_b, p16_b = pair_adds(grp_b, gs, ts)
    fulls_a = shard_sums(grp_a, p32_a, rs_a)
    (sem_wa, fulls_a), (sem_sb, buf_sb) = comm("swap_start_a_shard_start_b", _swap_start_part(fulls_a),
                                               _shard_start_part(p16_b))
    dbf, dproj = run(_forget_bwd, dF, fa, bpad, dproj)
    dproj = run(_dil_bwd, qkvb, dob, ob, lseb, rope, dproj)
    (rs_b, _), fulls_a = comm("shard_wait_b_swap_wait_a", _shard_wait_part(buf_sb, sem_sb, len(grp_b)),
                              _swap_wait_part(fulls_a, sem_wa))
    fulls_b = shard_sums(grp_b, p32_b, rs_b)
    ((sem_wb, fulls_b),) = comm("swap_start_b", _swap_start_part(fulls_b))
    adamw_group(grp_a, fulls_a)
    dwt = run(_mm, dproj, h1, "tn", f32, 512, D, "wgrad_in")
    dwfa = jnp.broadcast_to(dwt[F_FA:F_FA + FA_ROWS][None], (NCHIP, FA_ROWS, D))
    (sem_pc, buf_pc), fulls_b = comm("pair_start_c_swap_wait_b", _pair_start_part([dwt, dwfa], gathered=True),
                                     _swap_wait_part(fulls_b, sem_wb))
    adamw_group(grp_b, fulls_b)
    (((dwt_c, dwfa_c), (t_in, t_fa)),) = comm("pair_wait_c", _pair_wait_part(buf_pc, sem_pc, gathered=True))
    p32_in, p16_in = run(_pair_add_gathered, dwt_c, t_in, c_arr, "pair_add_w_in")
    p32_fa, p16_fa = run(_pair_add, dwfa_c, t_fa, c_arr, "pair_add_w_in_fa")
    ((sem_sc, buf_sc),) = comm("shard_start_c", _shard_start_part([p16_in, p16_fa]))
    gx, dg1 = run(_inproj_bwd, dproj, wt, x_, dx2, norm_attn_g)
    small = jnp.concatenate([dg1, dg2, dg3, jnp.pad(dbf[:, 0:8], ((0, 0), (0, D - 8))),
                             jnp.zeros((SMALL_ROWS - 4, D), f32)], axis=0)
    sm = lax.dynamic_update_slice(lax.empty((8, SMALL_ROWS, D), f32), small[None],
                                  (4 * xi + 2 * yi + ci, 0, 0))
    (sem_sm, buf_sm), (rs_c, _) = comm("small_start_shard_wait_c", _shard_start_part([], sm),
                                       _shard_wait_part(buf_sc, sem_sc, len(grp_c)))
    fulls_c = shard_sums(grp_c, [p32_in, p32_fa], rs_c)
    (sem_wc, fulls_c), (_, sm) = comm("swap_start_c_small_wait", _swap_start_part(fulls_c),
                                      _shard_wait_part(buf_sm, sem_sm, 0))
    gsmall = run(_small_sum, sm)

    grad["norm_attn_g"], grad["norm_mlp_g"] = gsmall[0:1], gsmall[1:2]
    grad["norm_final_g"], grad["b_forget"] = gsmall[2:3], gsmall[3:4, 0:8]
    upd["norm_attn_g"] = run(_adamw, norm_attn_g, grad["norm_attn_g"], m_norm_attn_g, v_norm_attn_g, "adamw_g1")
    upd["norm_mlp_g"] = run(_adamw, norm_mlp_g, grad["norm_mlp_g"], m_norm_mlp_g, v_norm_mlp_g, "adamw_g2")
    upd["norm_final_g"] = run(_adamw, norm_final_g.reshape(1, D), grad["norm_final_g"],
                              m_norm_final_g.reshape(1, D), v_norm_final_g.reshape(1, D), "adamw_g3")
    upd["b_forget"] = run(_adamw, b_forget, grad["b_forget"], m_b_forget, v_b_forget, "adamw_bf")

    ((gwin, gfa),) = comm("swap_wait_c", _swap_wait_part(fulls_c, sem_wc))
    g_in = _own_rows(gwin, gfa, q_me)
    upd_in = run(_adamw, wt_own, g_in, _feature_major(m_w_in), _feature_major(v_w_in), "adamw_w_in")
    grad["w_in"] = _unfeature_major(g_in)
    upd["w_in"] = [_unfeature_major(t) for t in upd_in]

    order_out = ["norm_attn_g", "w_in", "b_forget", "w_branch_a", "w_branch_b", "w_out", "norm_mlp_g", "w_up",
                 "w_down", "norm_final_g"]
    shapes = dict(norm_attn_g=norm_attn_g.shape, w_in=w_in.shape, b_forget=b_forget.shape,
                  w_branch_a=w_branch_a.shape, w_branch_b=w_branch_b.shape, w_out=w_out.shape,
                  norm_mlp_g=norm_mlp_g.shape, w_up=w_up.shape, w_down=w_down.shape, norm_final_g=norm_final_g.shape)
    outs = [loss, gx.reshape(x.shape)]
    outs += [grad[nm].reshape(shapes[nm]) for nm in order_out]
    for k in range(3):
        outs += [upd[nm][k].reshape(shapes[nm]) for nm in order_out]
    return tuple(outs)
```

```python
import jax
import jax.numpy as jnp
from jax import lax
from jax.experimental import pallas as pl
from jax.experimental.pallas import tpu as pltpu

f32 = jnp.float32
bf16 = jnp.bfloat16

S = 2048
D = 1024
DFF = 4096
HD = 64
FOXW = 512
DILOUT = 256
DIL = (1, 4, 16)
BAND = 128
EPS = 1e-6
NEG = -1e30
ROPE_THETA = 500000.0
NCHIP = 4
TQ = 256

ADAM_LR, ADAM_B1, ADAM_B2, ADAM_EPS, ADAM_WD, ADAM_STEP = 0.001, 0.9, 0.999, 1e-08, 0.01, 10
VMEM_LIMIT = 56 * 1024 * 1024

UNIT = 64
NP = 6144
F_DIL, F_FOX, F_FA, F_G = 0, 2304, 3840, 4096
DIL_BLK, FOX_BLK = 1152, 384
WIN_UNITS, WIN_ROWS = 24, 1536
WIN_UNIT0 = (0, 23, 45, 68)
OWN_ROW0 = (0, 2, 60, 62)
SHARD_IN = 1474
FA_ROWS = 32


def _compact_to_internal():
    c2i = {}
    for p in range(2):
        for role in range(3):
            for g in range(3):
                for hh in range(2):
                    c2i[24 + 12 * role + 4 * g + 2 * p + hh] = 18 * p + 6 * role + 2 * g + hh
    for p in range(4):
        for role in range(3):
            for hh in range(2):
                c2i[8 * role + 2 * p + hh] = F_FOX // UNIT + 6 * p + 2 * role + hh
    for j in range(32):
        c2i[60 + j] = F_G // UNIT + j
    return c2i


C2I = _compact_to_internal()
OVERLAP_UNITS = (23, 45, 46, 68)


def _params(sem=None):
    return pltpu.CompilerParams(dimension_semantics=sem, vmem_limit_bytes=VMEM_LIMIT)


class _Order:
    def __init__(self):
        self.tok = None

    def mark(self, v):
        self.tok = v

    def token_for(self, args):
        return [] if self.tok is None or any(self.tok is a for a in args) else [self.tok]


def _call(order, body, args, in_specs=None, **kw):
    args = list(args)
    n_in = len(args)
    if in_specs is None:
        in_specs = [pl.BlockSpec(memory_space=pltpu.VMEM)] * n_in
    kern = body
    extra = order.token_for(args)
    if extra:
        in_specs = list(in_specs) + [pl.BlockSpec(memory_space=pl.ANY)]

        def kern(*refs):
            body(*refs[:n_in], *refs[n_in + 1:])

    out = pl.pallas_call(kern, in_specs=in_specs, **kw)(*args, *extra)
    order.mark(out[0] if isinstance(out, (tuple, list)) else out)
    return out


def _call_indexed(order, body, scalars, args, grid, in_specs, out_specs, scratch_shapes=(), **kw):
    args, in_specs = list(args), list(in_specs)
    n_front = len(scalars) + len(args)
    kern = body
    extra = order.token_for(args)
    if extra:
        in_specs.append(pl.BlockSpec(memory_space=pl.ANY))

        def kern(*refs):
            body(*refs[:n_front], *refs[n_front + 1:])

    out = pl.pallas_call(
        kern, grid_spec=pltpu.PrefetchScalarGridSpec(num_scalar_prefetch=len(scalars), grid=grid, in_specs=in_specs,
                                                     out_specs=out_specs, scratch_shapes=scratch_shapes),
        **kw)(*scalars, *args, *extra)
    order.mark(out[0] if isinstance(out, (tuple, list)) else out)
    return out


def _dot(a, b):
    return jnp.dot(a, b, preferred_element_type=f32)


def _dot_nt(a, b):
    return lax.dot_general(a, b, (((1,), (1,)), ((), ())), preferred_element_type=f32)


def _dot_tn(a, b):
    return lax.dot_general(a, b, (((0,), (0,)), ((), ())), preferred_element_type=f32)


def _split3(x):
    hi = x.astype(bf16)
    r1 = x - hi.astype(f32)
    mid = r1.astype(bf16)
    lo = (r1 - mid.astype(f32)).astype(bf16)
    return hi, mid, lo


def _rope_tables():
    half = 8
    inv_freq = jnp.power(jnp.float32(ROPE_THETA), -jnp.arange(half, dtype=f32) * 2.0 / 16)
    ang = jnp.arange(S).astype(f32)[:, None] * inv_freq[None, :]
    cos, sin = jnp.cos(ang), jnp.sin(ang)
    one = jnp.ones((S, HD - 16), f32)
    zero = jnp.zeros((S, HD - 16), f32)
    z8 = jnp.zeros((S, 8), f32)
    c = jnp.concatenate([cos, cos, one], axis=1)
    s1 = jnp.concatenate([-sin, z8, zero], axis=1)
    s2 = jnp.concatenate([z8, sin, zero], axis=1)
    return tuple(jnp.concatenate([t, t], axis=1) for t in (c, s1, s2))


def _mm(order, a, b, mode, out_dtype, tm, tn, name, stack_cols=False):
    if mode == "nn":
        (M, K), (_, N) = a.shape, b.shape
        a_spec = pl.BlockSpec((tm, K), lambda i, j: (i, 0))
        b_spec = pl.BlockSpec((K, tn), lambda i, j: (0, j))
        dot = _dot
    elif mode == "nt":
        (M, K), (N, _) = a.shape, b.shape
        a_spec = pl.BlockSpec((tm, K), lambda i, j: (i, 0))
        b_spec = pl.BlockSpec((tn, K), lambda i, j: (j, 0))
        dot = _dot_nt
    else:
        (K, M), (_, N) = a.shape, b.shape
        a_spec = pl.BlockSpec((K, tm), lambda i, j: (0, i))
        b_spec = pl.BlockSpec((K, tn), lambda i, j: (0, j))
        dot = _dot_tn

    def body(a_ref, b_ref, o_ref):
        o_ref[...] = dot(a_ref[...], b_ref[...]).astype(out_dtype)

    if stack_cols:
        assert tm == M
        out_spec = pl.BlockSpec((None, tm, tn), lambda i, j: (j, 0, 0))
        out_shape = jax.ShapeDtypeStruct((N // tn, M, tn), out_dtype)
    else:
        out_spec = pl.BlockSpec((tm, tn), lambda i, j: (i, j))
        out_shape = jax.ShapeDtypeStruct((M, N), out_dtype)
    return _call(
        order, body, (a, b), name=name, grid=(M // tm, N // tn), in_specs=[a_spec, b_spec],
        out_specs=out_spec, out_shape=out_shape,
        compiler_params=_params(("parallel", "parallel")),
    )


def _assemble_win(order, wins, fas):
    def body(win_ref, fa_ref, o_ref):
        q = pl.program_id(0)

        @pl.when(q == 0)
        def _():
            o_ref[...] = jnp.zeros_like(o_ref)

        for k in range(NCHIP):
            @pl.when(q == k)
            def _(k=k):
                for j in range(WIN_UNITS):
                    cu = WIN_UNIT0[k] + j
                    dst = pl.ds(C2I[cu] * UNIT, UNIT)
                    if cu in OVERLAP_UNITS:
                        o_ref[dst, :] += win_ref[j * UNIT:(j + 1) * UNIT, :]
                    else:
                        o_ref[dst, :] = win_ref[j * UNIT:(j + 1) * UNIT, :]
                if k == 1:
                    o_ref[F_FA:F_FA + FA_ROWS, :] = fa_ref[...]

    return _call(
        order, body, (wins, fas), name="assemble_w_in", grid=(NCHIP,),
        in_specs=[pl.BlockSpec((None, WIN_ROWS, D), lambda q: (q, 0, 0)),
                  pl.BlockSpec((None, FA_ROWS, D), lambda q: (1, 0, 0))],
        out_specs=pl.BlockSpec((NP, D), lambda q: (0, 0)),
        out_shape=jax.ShapeDtypeStruct((NP, D), bf16),
        compiler_params=_params(("arbitrary",)),
    )


def _norm_inproj(order, x, g1, wt, rope):
    tm = 256
    c_t, s1_t, s2_t = rope

    def body(x_ref, g_ref, w_ref, c_ref, s1_ref, s2_ref, h_ref, qkvb_ref, qkva_ref, gates_ref, fa_ref):
        xb = x_ref[...]
        r = lax.rsqrt(jnp.mean(xb * xb, axis=-1, keepdims=True) + EPS)
        h = ((xb * r) * g_ref[...]).astype(bf16)
        h_ref[...] = h
        c, s1, s2 = c_ref[...], s1_ref[...], s2_ref[...]
        for p in range(2):
            pb = _dot_nt(h, w_ref[F_DIL + p * DIL_BLK:F_DIL + (p + 1) * DIL_BLK, :])
            for ch in range(DIL_BLK // 128):
                pc = pb[:, ch * 128:(ch + 1) * 128]
                if ch < 6:
                    pc = pc * c + pltpu.roll(pc, 120, 1) * s1 + pltpu.roll(pc, 8, 1) * s2
                qkvb_ref[:, p * DIL_BLK + ch * 128:p * DIL_BLK + (ch + 1) * 128] = pc
        qkva_ref[...] = _dot_nt(h, w_ref[F_FOX:F_FA, :]).astype(bf16)
        fa_ref[...] = _dot_nt(h, w_ref[F_FA:F_FA + 128, :])
        gates_ref[...] = _dot_nt(h, w_ref[F_G:NP, :]).astype(bf16)

    row = lambda w: pl.BlockSpec((tm, w), lambda i: (i, 0))
    return _call(
        order, body, (x, g1, wt, c_t, s1_t, s2_t), name="norm_inproj", grid=(S // tm,),
        in_specs=[row(D), pl.BlockSpec((1, D), lambda i: (0, 0)), pl.BlockSpec((NP, D), lambda i: (0, 0)),
                  row(128), row(128), row(128)],
        out_specs=[row(D), row(2 * DIL_BLK), row(4 * FOX_BLK), row(2 * D), row(128)],
        out_shape=[jax.ShapeDtypeStruct((S, D), bf16), jax.ShapeDtypeStruct((S, 2 * DIL_BLK), f32),
                   jax.ShapeDtypeStruct((S, 4 * FOX_BLK), bf16), jax.ShapeDtypeStruct((S, 2 * D), bf16),
                   jax.ShapeDtypeStruct((S, 128), f32)],
        compiler_params=_params(("parallel",)),
    )


def _forget_cumsum(order, fa, bpad):
    nb = S // TQ

    def body(fa_ref, b_ref, F_ref):
        rr = lax.broadcasted_iota(jnp.int32, (TQ, TQ), 0)
        cc = lax.broadcasted_iota(jnp.int32, (TQ, TQ), 1)
        tri = (rr >= cc).astype(bf16)
        lane = lax.broadcasted_iota(jnp.int32, (1, 128), 1)
        carry = jnp.zeros((1, 128), f32)
        for b in range(nb):
            z = fa_ref[b * TQ:(b + 1) * TQ, :] + b_ref[...]
            lf = jnp.minimum(z, 0.0) - jnp.log(1.0 + jnp.exp(-jnp.abs(z)))
            lf = jnp.where(lane < 8, lf, 0.0)
            hi, mid, lo = _split3(lf)
            fb = (_dot(tri, hi) + _dot(tri, mid)) + _dot(tri, lo) + carry
            F_ref[b * TQ:(b + 1) * TQ, :] = fb
            carry = fb[TQ - 1:TQ, :]

    return _call(
        order, body, (fa, bpad), name="forget_cumsum",
        out_shape=jax.ShapeDtypeStruct((S, 128), f32),
        compiler_params=_params(),
    )


def _head_masks():
    lane = lax.broadcasted_iota(jnp.int32, (1, 128), 1)
    return lane, (lane < HD, lane >= HD)


L_FT, L_ONE, L_LSE = 0, 3, 6
FOX_TQ, FOX_TK = 256, 512


def _set_lanes(x, lane, first, cols):
    for n, col in enumerate(cols):
        x = jnp.where(lane == first + n, col, x)
    return x


def _f32_parts(col):
    return [t.astype(f32) for t in _split3(col)]


def _fox_operands(qkv_ref, F_ref, lse_ref, qa, ka, p, rows):
    lane, hm = _head_masks()
    q = qkv_ref[rows, 0:128].astype(f32) * 0.125
    k = qkv_ref[rows, 128:256].astype(f32)
    Fb = F_ref[rows, :]
    for hh in (0, 1):
        free = (1 - hh) * HD
        fparts = _f32_parts(jnp.sum(jnp.where(lane == 2 * p + hh, Fb, 0.0), axis=1, keepdims=True))
        qcols = fparts + [1.0] * 3
        kcols = [1.0] * 3 + [-t for t in fparts]
        if lse_ref is not None:
            qcols += [-t for t in _f32_parts(lse_ref[rows, hh * HD:hh * HD + 1])]
            kcols += [1.0] * 3
        qa[hh, rows, :] = _set_lanes(jnp.where(hm[hh], q, 0.0), lane, free, qcols).astype(bf16)
        ka[hh, rows, :] = _set_lanes(k, lane, free, kcols).astype(bf16)


def _fox_fwd(order, qkva, F):
    tq, tk = FOX_TQ, FOX_TK

    def body(qkv_ref, F_ref, o_ref, lse_ref, qa, ka, vt):
        p = pl.program_id(0)
        keyi = lax.broadcasted_iota(jnp.int32, (tk, 1), 0)
        qryi = lax.broadcasted_iota(jnp.int32, (1, tq), 1)
        sub = lax.broadcasted_iota(jnp.int32, (128, 1), 0)

        def prep(i, c):
            rows = pl.ds(pl.multiple_of(i * tk, tk), tk)
            _fox_operands(qkv_ref, F_ref, None, qa, ka, p, rows)
            vt[i] = qkv_ref[rows, 256:384].astype(f32).T.astype(bf16)
            return c

        lax.fori_loop(0, S // tk, prep, 0)

        def qblock(i, c):
            r0 = pl.multiple_of(i * tq, tq)
            qh = [qa[hh, pl.ds(r0, tq), :] for hh in (0, 1)]

            def kv(jb, carry, masked):
                keys = pl.ds(pl.multiple_of(jb * tk, tk), tk)
                sts = [_dot_nt(ka[hh, keys, :], qh[hh]) for hh in (0, 1)]
                new = []
                for hh in (0, 1):
                    m, l, a = carry[3 * hh:3 * hh + 3]
                    st = sts[hh]
                    if masked:
                        st = jnp.where(jb * tk + keyi <= r0 + qryi, st, NEG)
                    mn = jnp.maximum(m, jnp.max(st, axis=0, keepdims=True))
                    al = jnp.exp(m - mn)
                    pt = jnp.exp(st - mn)
                    l = al * l + jnp.sum(pt, axis=0, keepdims=True)
                    a = al * a + _dot(vt[jb, hh * HD:(hh + 1) * HD, :], pt.astype(bf16))
                    new += [mn, l, a]
                return tuple(new)

            init = (jnp.full((1, tq), NEG, f32), jnp.zeros((1, tq), f32), jnp.zeros((HD, tq), f32)) * 2
            last = (r0 + tq - 1) // tk
            carry = lax.fori_loop(0, last, lambda j, cr: kv(j, cr, False), init)
            m0, l0, a0, m1, l1, a1 = kv(last, carry, True)
            ot = jnp.concatenate([a0 / l0, a1 / l1], axis=0)
            lt = jnp.where(sub < HD, m0 + jnp.log(l0), m1 + jnp.log(l1))
            o_ref[pl.ds(r0, tq), :] = ot.T.astype(bf16)
            lse_ref[pl.ds(r0, tq), :] = lt.T
            return c

        lax.fori_loop(0, S // tq, qblock, 0)

    pair = pl.BlockSpec((S, 128), lambda p: (0, p))
    return _call(
        order, body, (qkva, F), name="fox_fwd", grid=(4,),
        in_specs=[pl.BlockSpec((S, FOX_BLK), lambda p: (0, p)), pl.BlockSpec((S, 128), lambda p: (0, 0))],
        out_specs=[pair, pair],
        out_shape=[jax.ShapeDtypeStruct((S, FOXW), bf16), jax.ShapeDtypeStruct((S, FOXW), f32)],
        scratch_shapes=[pltpu.VMEM((2, S, 128), bf16)] * 2 + [pltpu.VMEM((S // tk, 128, tk), bf16)],
        compiler_params=_params(("parallel",)),
    )


def _permute_in(dst, src, r):
    L = S // r
    for rho in range(r):
        dst[rho * L:(rho + 1) * L, :] = src[pl.ds(rho, L, stride=r), :]


def _permute_out(dst, src, r):
    L = S // r
    for rho in range(r):
        dst[pl.ds(rho, L, stride=r), :] = src[rho * L:(rho + 1) * L, :]


def _band_geometry(bb, nbl):
    r0 = pl.multiple_of(bb * BAND, BAND)
    k0 = pl.multiple_of(jnp.maximum(bb - 1, 0) * BAND, BAND)
    sub0 = (bb - lax.rem(bb, nbl)) * BAND
    qi = r0 + lax.broadcasted_iota(jnp.int32, (BAND, 1), 0)
    ki = k0 + lax.broadcasted_iota(jnp.int32, (1, 2 * BAND), 1)
    diff = qi - ki
    valid = (diff >= 0) & (diff <= BAND) & (ki >= sub0)
    return r0, k0, valid


def _dil_views(ref):
    return [[ref.at[:, pl.ds((3 * role + g) * 128, 128)] for g in range(3)] for role in range(3)]


DIL_UNROLL = 4


def _dil_in_specs():
    return [pl.BlockSpec((S, 128), lambda p, k=k: (0, 9 * p + k)) for k in range(9)]


def _dil_fwd(order, qkvb):
    def body(*refs):
        q_refs, k_refs, v_refs = refs[0:3], refs[3:6], refs[6:9]
        ob_ref, lse_ref, qp, kp, vp, op, lp = refs[9:16]
        on, ln = refs[16:19], refs[19:22]
        _, hm = _head_masks()
        for g, r in enumerate(DIL):
            nbl = S // r // BAND
            if r == 1:
                qs_, ks_, vs_, od, ld = q_refs[g], k_refs[g], v_refs[g], on[g], ln[g]
            else:
                _permute_in(qp, q_refs[g], r)
                _permute_in(kp, k_refs[g], r)
                _permute_in(vp, v_refs[g], r)
                qs_, ks_, vs_, od, ld = qp, kp, vp, op, lp

            def blk(t, c, qs_=qs_, ks_=ks_, vs_=vs_, od=od, ld=ld, nbl=nbl):
                work = []
                for u in range(DIL_UNROLL):
                    r0, k0, valid = _band_geometry(DIL_UNROLL * t + u, nbl)
                    q = qs_[pl.ds(r0, BAND), :] * 0.125
                    kw = ks_[pl.ds(k0, 2 * BAND), :].astype(bf16)
                    vw = vs_[pl.ds(k0, 2 * BAND), :]
                    for hh in (0, 1):
                        qh = jnp.where(hm[hh], q, 0.0).astype(bf16)
                        work.append((u, hh, r0, valid, vw, _dot_nt(qh, kw)))
                o = [jnp.zeros((BAND, 128), f32)] * DIL_UNROLL
                lse = [jnp.zeros((BAND, 128), f32)] * DIL_UNROLL
                for u, hh, r0, valid, vw, s in work:
                    s = jnp.where(valid, s, NEG)
                    m = jnp.max(s, axis=1, keepdims=True)
                    pr = jnp.exp(s - m)
                    l = jnp.sum(pr, axis=1, keepdims=True)
                    vm = jnp.where(hm[hh], vw, 0.0).astype(bf16)
                    o[u] = o[u] + _dot((pr / l).astype(bf16), vm)
                    lse[u] = jnp.where(hm[hh], m + jnp.log(l), lse[u])
                    if hh == 1:
                        od[pl.ds(r0, BAND), :] = o[u]
                        ld[pl.ds(r0, BAND), :] = lse[u]
                return c

            lax.fori_loop(0, S // BAND // DIL_UNROLL, blk, 0)
            if r != 1:
                _permute_out(on[g], op, r)
                _permute_out(ln[g], lp, r)

        def combine(i, c):
            r0 = pl.multiple_of(i * TQ, TQ)
            ls = [ln[g][pl.ds(r0, TQ), :] for g in range(3)]
            mx = jnp.maximum(jnp.maximum(ls[0], ls[1]), ls[2])
            es = [jnp.exp(l - mx) for l in ls]
            tot = (es[0] + es[1]) + es[2]
            acc = (es[0] / tot) * on[0][pl.ds(r0, TQ), :]
            acc = acc + (es[1] / tot) * on[1][pl.ds(r0, TQ), :]
            acc = acc + (es[2] / tot) * on[2][pl.ds(r0, TQ), :]
            ob_ref[pl.ds(r0, TQ), :] = acc.astype(bf16)
            lse_ref[pl.ds(r0, TQ), :] = mx + jnp.log(tot)
            return c

        lax.fori_loop(0, S // TQ, combine, 0)

    out_blk = pl.BlockSpec((S, 128), lambda p: (0, p))
    return _call(
        order, body, [qkvb] * 9, name="dil_fwd", grid=(2,),
        in_specs=_dil_in_specs(), out_specs=[out_blk, out_blk],
        out_shape=[jax.ShapeDtypeStruct((S, DILOUT), bf16), jax.ShapeDtypeStruct((S, DILOUT), f32)],
        scratch_shapes=[pltpu.VMEM((S, 128), f32)] * 11,
        compiler_params=_params(("parallel",)),
    )


def _branch_mix(order, oa, ob, was, wbs, gates):
    tm = 512

    def body(oa_ref, ob_ref, wa_ref, wb_ref, g_ref, ya_ref, yb_ref, mix_ref):
        oa_b, ob_b = oa_ref[...], ob_ref[...]
        for q in range(NCHIP):
            cols = slice(q * 256, (q + 1) * 256)
            ya = _dot(oa_b, wa_ref[q])
            yb = _dot(ob_b, wb_ref[q])
            ya_ref[:, cols] = ya.astype(bf16)
            yb_ref[:, cols] = yb.astype(bf16)
            ga = g_ref[:, q * 256:(q + 1) * 256].astype(f32)
            gb = g_ref[:, D + q * 256:D + (q + 1) * 256].astype(f32)
            mix_ref[:, cols] = (jax.nn.sigmoid(ga) * ya + jax.nn.sigmoid(gb) * yb).astype(bf16)

    row = lambda w: pl.BlockSpec((tm, w), lambda i: (i, 0))
    full3 = lambda a: pl.BlockSpec(a.shape, lambda i: (0, 0, 0))
    return _call(
        order, body, (oa, ob, was, wbs, gates), name="branch_mix", grid=(S // tm,),
        in_specs=[row(FOXW), row(DILOUT), full3(was), full3(wbs), row(2 * D)],
        out_specs=[row(D), row(D), row(D)],
        out_shape=[jax.ShapeDtypeStruct((S, D), bf16), jax.ShapeDtypeStruct((S, D), bf16),
                   jax.ShapeDtypeStruct((S, D), bf16)],
        compiler_params=_params(("parallel",)),
    )


def _outproj_norm(order, mixed, wout, x, g2):
    tm = 512

    def body(m_ref, w_ref, x_ref, g_ref, x2_ref, h2_ref):
        x2 = x_ref[...] + _dot(m_ref[...], w_ref[...])
        x2_ref[...] = x2
        r = lax.rsqrt(jnp.mean(x2 * x2, axis=-1, keepdims=True) + EPS)
        h2_ref[...] = ((x2 * r) * g_ref[...]).astype(bf16)

    row = pl.BlockSpec((tm, D), lambda i: (i, 0))
    return _call(
        order, body, (mixed, wout, x, g2), name="outproj_norm", grid=(S // tm,),
        in_specs=[row, pl.BlockSpec((D, D), lambda i: (0, 0)), row, pl.BlockSpec((1, D), lambda i: (0, 0))],
        out_specs=[row, row],
        out_shape=[jax.ShapeDtypeStruct((S, D), f32), jax.ShapeDtypeStruct((S, D), bf16)],
        compiler_params=_params(("parallel",)),
    )


def _mlp_up(order, h2, wups):
    tm = 512

    def body(h_ref, w_ref, ru_ref, a_ref):
        ru = jnp.maximum(_dot(h_ref[...], w_ref[...]), 0.0)
        ru_ref[...] = ru.astype(bf16)
        a_ref[...] = (ru * ru).astype(bf16)

    out = pl.BlockSpec((tm, D), lambda q, i: (i, q))
    return _call(
        order, body, (h2, wups), name="mlp_up", grid=(NCHIP, S // tm),
        in_specs=[pl.BlockSpec((tm, D), lambda q, i: (i, 0)), pl.BlockSpec((None, D, D), lambda q, i: (q, 0, 0))],
        out_specs=[out, out],
        out_shape=[jax.ShapeDtypeStruct((S, DFF), bf16), jax.ShapeDtypeStruct((S, DFF), bf16)],
        compiler_params=_params(("parallel", "parallel")),
    )


def _mlp_down_loss(order, a, wdown, x2, g3, tgt):
    tm = 256

    def body(a_ref, w_ref, x2_ref, g_ref, t_ref, dx_ref, dxb_ref, dg_ref, loss_ref):
        i = pl.program_id(0)
        x3 = x2_ref[...] + _dot(a_ref[...], w_ref[...])
        r = lax.rsqrt(jnp.mean(x3 * x3, axis=-1, keepdims=True) + EPS)
        xh = x3 * r
        g = g_ref[...]
        e = xh * g - t_ref[...]
        part = 0.5 * jnp.sum(jnp.mean(e * e, axis=-1, keepdims=True), axis=0, keepdims=True)
        dy = e * (1.0 / D)
        gdy = dy * g
        dx = r * (gdy - xh * jnp.mean(gdy * xh, axis=-1, keepdims=True))
        dx_ref[...] = dx
        dxb_ref[...] = dx.astype(bf16)

        @pl.when(i == 0)
        def _():
            dg_ref[...] = jnp.zeros_like(dg_ref)
            loss_ref[...] = jnp.zeros_like(loss_ref)

        dg_ref[...] += jnp.sum(dy * xh, axis=0, keepdims=True)
        loss_ref[...] += jnp.broadcast_to(part, (1, 128))

    row = pl.BlockSpec((tm, D), lambda i: (i, 0))
    vec = pl.BlockSpec((1, D), lambda i: (0, 0))
    return _call(
        order, body, (a, wdown, x2, g3, tgt), name="mlp_down_loss", grid=(S // tm,),
        in_specs=[pl.BlockSpec((tm, DFF), lambda i: (i, 0)), pl.BlockSpec((DFF, D), lambda i: (0, 0)), row, vec, row],
        out_specs=[row, row, vec, pl.BlockSpec((1, 128), lambda i: (0, 0))],
        out_shape=[jax.ShapeDtypeStruct((S, D), f32), jax.ShapeDtypeStruct((S, D), bf16),
                   jax.ShapeDtypeStruct((1, D), f32), jax.ShapeDtypeStruct((1, 128), f32)],
        compiler_params=_params(("arbitrary",)),
    )


def _mlp_down_bwd(order, dx3b, wdown, u):
    tm = 256

    def body(d_ref, w_ref, u_ref, du_ref):
        d = d_ref[...]
        for q in range(NCHIP):
            cols = slice(q * D, (q + 1) * D)
            da = _dot_nt(d, w_ref[cols, :])
            du_ref[:, cols] = (da * (2.0 * u_ref[:, cols].astype(f32))).astype(bf16)

    return _call(
        order, body, (dx3b, wdown, u), name="mlp_down_bwd", grid=(S // tm,),
        in_specs=[pl.BlockSpec((tm, D), lambda i: (i, 0)), pl.BlockSpec((DFF, D), lambda i: (0, 0)),
                  pl.BlockSpec((tm, DFF), lambda i: (i, 0))],
        out_specs=pl.BlockSpec((tm, DFF), lambda i: (i, 0)),
        out_shape=jax.ShapeDtypeStruct((S, DFF), bf16),
        compiler_params=_params(("parallel",)),
    )


def _mlp_up_bwd(order, du, wups, x2, dx3, g2):
    tm = 256

    def body(du_ref, w_ref, x2_ref, dx3_ref, g_ref, dx2_ref, dx2b_ref, dg_ref):
        i = pl.program_id(0)
        dh = jnp.zeros((tm, D), f32)
        for q in range(NCHIP):
            dh = dh + _dot_nt(du_ref[:, q * D:(q + 1) * D], w_ref[q])
        x2 = x2_ref[...]
        r = lax.rsqrt(jnp.mean(x2 * x2, axis=-1, keepdims=True) + EPS)
        xh = x2 * r
        gdh = dh * g_ref[...]
        dx2 = dx3_ref[...] + r * (gdh - xh * jnp.mean(gdh * xh, axis=-1, keepdims=True))
        dx2_ref[...] = dx2
        dx2b_ref[...] = dx2.astype(bf16)

        @pl.when(i == 0)
        def _():
            dg_ref[...] = jnp.zeros_like(dg_ref)

        dg_ref[...] += jnp.sum(dh * xh, axis=0, keepdims=True)

    row = pl.BlockSpec((tm, D), lambda i: (i, 0))
    vec = pl.BlockSpec((1, D), lambda i: (0, 0))
    return _call(
        order, body, (du, wups, x2, dx3, g2), name="mlp_up_bwd", grid=(S // tm,),
        in_specs=[pl.BlockSpec((tm, DFF), lambda i: (i, 0)), pl.BlockSpec((NCHIP, D, D), lambda i: (0, 0, 0)),
                  row, row, vec],
        out_specs=[row, row, vec],
        out_shape=[jax.ShapeDtypeStruct((S, D), f32), jax.ShapeDtypeStruct((S, D), bf16),
                   jax.ShapeDtypeStruct((1, D), f32)],
        compiler_params=_params(("arbitrary",)),
    )


def _gate_bwd(order, dx2b, wout, gates, ya, yb):
    tm = 256

    def body(d_ref, w_ref, g_ref, ya_ref, yb_ref, dya_ref, dyb_ref, dproj_ref):
        dm = _dot_nt(d_ref[...], w_ref[...])
        sa = jax.nn.sigmoid(g_ref[:, 0:D].astype(f32))
        sb = jax.nn.sigmoid(g_ref[:, D:2 * D].astype(f32))
        dya_ref[...] = (dm * sa).astype(bf16)
        dyb_ref[...] = (dm * sb).astype(bf16)
        dproj_ref[:, 0:D] = (dm * ya_ref[...].astype(f32) * (sa * (1.0 - sa))).astype(bf16)
        dproj_ref[:, D:2 * D] = (dm * yb_ref[...].astype(f32) * (sb * (1.0 - sb))).astype(bf16)

    row = lambda w: pl.BlockSpec((tm, w), lambda i: (i, 0))
    return _call(
        order, body, (dx2b, wout, gates, ya, yb), name="gate_bwd", grid=(S // tm,),
        in_specs=[row(D), pl.BlockSpec((D, D), lambda i: (0, 0)), row(2 * D), row(D), row(D)],
        out_specs=[row(D), row(D), pl.BlockSpec((tm, 2 * D), lambda i: (i, F_G // (2 * D)))],
        out_shape=[jax.ShapeDtypeStruct((S, D), bf16), jax.ShapeDtypeStruct((S, D), bf16),
                   jax.ShapeDtypeStruct((S, NP), bf16)],
        compiler_params=_params(("parallel",)),
    )


def _branch_bwd(order, dya, dyb, was, wbs):
    tm = 512

    def body(dya_ref, dyb_ref, wa_ref, wb_ref, doa_ref, dob_ref):
        doa = jnp.zeros((tm, FOXW), f32)
        dob = jnp.zeros((tm, DILOUT), f32)
        for q in range(NCHIP):
            cols = slice(q * 256, (q + 1) * 256)
            doa = doa + _dot_nt(dya_ref[:, cols], wa_ref[q])
            dob = dob + _dot_nt(dyb_ref[:, cols], wb_ref[q])
        doa_ref[...] = doa.astype(bf16)
        dob_ref[...] = dob

    row = lambda w: pl.BlockSpec((tm, w), lambda i: (i, 0))
    full3 = lambda a: pl.BlockSpec(a.shape, lambda i: (0, 0, 0))
    return _call(
        order, body, (dya, dyb, was, wbs), name="branch_bwd", grid=(S // tm,),
        in_specs=[row(D), row(D), full3(was), full3(wbs)],
        out_specs=[row(FOXW), row(DILOUT)],
        out_shape=[jax.ShapeDtypeStruct((S, FOXW), bf16), jax.ShapeDtypeStruct((S, DILOUT), f32)],
        compiler_params=_params(("parallel",)),
    )


def _branch_wgrad(order, oa, ob, dya, dyb):
    def body(oa_ref, ob_ref, dya_ref, dyb_ref, dwa_ref, dwb_ref):
        dwa_ref[...] = _dot_tn(oa_ref[...], dya_ref[...])
        dwb_ref[...] = _dot_tn(ob_ref[...], dyb_ref[...])

    full = lambda w: pl.BlockSpec((S, w), lambda q: (0, 0))
    colq = pl.BlockSpec((S, 256), lambda q: (0, q))
    return _call(
        order, body, (oa, ob, dya, dyb), name="branch_wgrad", grid=(NCHIP,),
        in_specs=[full(FOXW), full(DILOUT), colq, colq],
        out_specs=[pl.BlockSpec((None, FOXW, 256), lambda q: (q, 0, 0)),
                   pl.BlockSpec((None, DILOUT, 256), lambda q: (q, 0, 0))],
        out_shape=[jax.ShapeDtypeStruct((NCHIP, FOXW, 256), f32), jax.ShapeDtypeStruct((NCHIP, DILOUT, 256), f32)],
        compiler_params=_params(("parallel",)),
    )


def _fox_bwd(order, qkva, doa, oa, lse, F, dproj):
    tq, tk = FOX_TQ, FOX_TK

    def body(qkv_ref, do_ref, o_ref, lse_ref, F_ref, _dproj_in, dF_ref, dqkv_ref, qa, ka, da, va, kat,
             dk_scr, dv_scr, dqt_scr):
        p = pl.program_id(0)
        lane, hm = _head_masks()
        keyi = lax.broadcasted_iota(jnp.int32, (tk, 1), 0)
        qryi = lax.broadcasted_iota(jnp.int32, (1, tq), 1)

        def prep(i, c):
            rows = pl.ds(pl.multiple_of(i * tk, tk), tk)
            _fox_operands(qkv_ref, F_ref, lse_ref, qa, ka, p, rows)
            do = do_ref[rows, :].astype(f32)
            prod = do * o_ref[rows, :].astype(f32)
            v = qkv_ref[rows, 256:384].astype(f32)
            for hh in (0, 1):
                free = (1 - hh) * HD
                delta = jnp.sum(jnp.where(hm[hh], prod, 0.0), axis=1, keepdims=True)
                da[hh, rows, :] = _set_lanes(jnp.where(hm[hh], do, 0.0), lane, free,
                                             [-t for t in _f32_parts(delta)]).astype(bf16)
                va[hh, rows, :] = _set_lanes(v, lane, free, [1.0] * 3).astype(bf16)
                kat[hh, i] = ka[hh, rows, :].astype(f32).T.astype(bf16)
                dk_scr[hh, rows, :] = jnp.zeros((tk, 128), f32)
                dv_scr[hh, rows, :] = jnp.zeros((tk, 128), f32)
            return c

        lax.fori_loop(0, S // tk, prep, 0)

        def qblock(i, c):
            r0 = pl.multiple_of(i * tq, tq)
            qrows = pl.ds(r0, tq)
            qh = [qa[hh, qrows, :] for hh in (0, 1)]
            dh = [da[hh, qrows, :] for hh in (0, 1)]
            dqt_scr[...] = jnp.zeros_like(dqt_scr)

            def kv(jb, c2, masked):
                keys = pl.ds(pl.multiple_of(jb * tk, tk), tk)
                sts = [_dot_nt(ka[hh, keys, :], qh[hh]) for hh in (0, 1)]
                dps = [_dot_nt(va[hh, keys, :], dh[hh]) for hh in (0, 1)]
                for hh in (0, 1):
                    pt = jnp.exp(sts[hh])
                    if masked:
                        pt = jnp.where(jb * tk + keyi <= r0 + qryi, pt, 0.0)
                    dsb = (pt * dps[hh]).astype(bf16)
                    dv_scr[hh, keys, :] += _dot(pt.astype(bf16), dh[hh])
                    dk_scr[hh, keys, :] += _dot(dsb, qh[hh])
                    dqt_scr[hh] += _dot(kat[hh, jb], dsb)
                return c2

            last = (r0 + tq - 1) // tk
            lax.fori_loop(0, last, lambda j, c2: kv(j, c2, False), 0)
            kv(last, 0, True)
            dq0, dq1 = dqt_scr[0].T, dqt_scr[1].T
            dqkv_ref[qrows, 0:128] = (jnp.where(hm[0], dq0, dq1) * 0.125).astype(bf16)
            dF_ref[qrows, :] = jnp.where(lane == 0, dq0[:, HD:HD + 1], jnp.where(lane == 1, dq1[:, 0:1], 0.0))
            return c

        lax.fori_loop(0, S // tq, qblock, 0)

        def finish(i, c):
            rows = pl.ds(pl.multiple_of(i * tq, tq), tq)
            dk0, dk1 = dk_scr[0, rows, :], dk_scr[1, rows, :]
            dqkv_ref[rows, 128:256] = jnp.where(hm[0], dk0, dk1).astype(bf16)
            dqkv_ref[rows, 256:384] = jnp.where(hm[0], dv_scr[0, rows, :], dv_scr[1, rows, :]).astype(bf16)
            cs = jnp.where(lane == 0, dk0[:, HD + L_ONE:HD + L_ONE + 1],
                           jnp.where(lane == 1, dk1[:, L_ONE:L_ONE + 1], 0.0))
            dF_ref[rows, :] = dF_ref[rows, :] - cs
            return c

        lax.fori_loop(0, S // tq, finish, 0)

    pair = pl.BlockSpec((S, 128), lambda p: (0, p))
    return _call(
        order, body, (qkva, doa, oa, lse, F, dproj), name="fox_bwd", grid=(4,),
        in_specs=[pl.BlockSpec((S, FOX_BLK), lambda p: (0, p)), pair, pair, pair,
                  pl.BlockSpec((S, 128), lambda p: (0, 0)), pl.BlockSpec(memory_space=pl.ANY)],
        out_specs=[pair, pl.BlockSpec((S, FOX_BLK), lambda p: (0, F_FOX // FOX_BLK + p))],
        out_shape=[jax.ShapeDtypeStruct((S, FOXW), f32), jax.ShapeDtypeStruct((S, NP), bf16)],
        input_output_aliases={5: 1},
        scratch_shapes=[pltpu.VMEM((2, S, 128), bf16)] * 4 + [pltpu.VMEM((2, S // tk, 128, tk), bf16)]
        + [pltpu.VMEM((2, S, 128), f32)] * 2 + [pltpu.VMEM((2, 128, tq), f32)],
        compiler_params=_params(("parallel",)),
    )


def _forget_bwd(order, dF, fa, bpad, dproj):
    nb = S // TQ

    def body(dF_ref, fa_ref, b_ref, _dproj_in, db_ref, dfa_ref):
        rr = lax.broadcasted_iota(jnp.int32, (TQ, TQ), 0)
        cc = lax.broadcasted_iota(jnp.int32, (TQ, TQ), 1)
        upper = (cc >= rr).astype(bf16)
        lane = lax.broadcasted_iota(jnp.int32, (1, 128), 1)
        carry = jnp.zeros((1, 128), f32)
        db = jnp.zeros((1, 128), f32)
        for b in reversed(range(nb)):
            cols = jnp.zeros((TQ, 128), f32)
            for h in range(8):
                c0 = (h // 2) * 128 + h % 2
                cols = jnp.where(lane == h, dF_ref[b * TQ:(b + 1) * TQ, c0:c0 + 1], cols)
            dlf = carry
            for part in _split3(cols):
                dlf = dlf + _dot(upper, part)
            carry = carry + jnp.sum(cols, axis=0, keepdims=True)
            z = fa_ref[b * TQ:(b + 1) * TQ, :] + b_ref[...]
            dz = jnp.where(lane < 8, dlf * jax.nn.sigmoid(-z), 0.0)
            dfa_ref[b * TQ:(b + 1) * TQ, 0:128] = dz.astype(bf16)
            dfa_ref[b * TQ:(b + 1) * TQ, 128:256] = jnp.zeros((TQ, 128), bf16)
            db = db + jnp.sum(dz, axis=0, keepdims=True)
        db_ref[...] = db

    whole = lambda a: pl.BlockSpec(a.shape, lambda i: (0,) * a.ndim)
    return _call(
        order, body, (dF, fa, bpad, dproj), name="forget_bwd", grid=(1,),
        in_specs=[whole(dF), whole(fa), whole(bpad), pl.BlockSpec(memory_space=pl.ANY)],
        out_specs=[pl.BlockSpec((1, 128), lambda i: (0, 0)), pl.BlockSpec((S, 256), lambda i: (0, F_FA // 256))],
        out_shape=[jax.ShapeDtypeStruct((1, 128), f32), jax.ShapeDtypeStruct((S, NP), bf16)],
        input_output_aliases={3: 1},
        compiler_params=_params(("arbitrary",)),
    )


def _dil_bwd(order, qkvb, dob, ob, lseb, rope, dproj):
    c_t, s1_t, s2_t = rope

    def body(*refs):
        q_refs, k_refs, v_refs = refs[0:3], refs[3:6], refs[6:9]
        dob_ref, ob_ref, lse_ref, c_ref, s1_ref, s2_ref, _dproj_in, dqkv_ref = refs[9:17]
        qp, kp, vp, dop, lp, dlp, dln, dqp, dkp, dvp, nat = refs[17:28]
        dq_out, dk_out, dv_out = _dil_views(dqkv_ref)
        _, hm = _head_masks()

        def delta_rows(i, c):
            r0 = pl.multiple_of(i * TQ, TQ)
            prod = dob_ref[pl.ds(r0, TQ), :] * ob_ref[pl.ds(r0, TQ), :].astype(f32)
            d0 = jnp.sum(jnp.where(hm[0], prod, 0.0), axis=1, keepdims=True)
            d1 = jnp.sum(jnp.where(hm[1], prod, 0.0), axis=1, keepdims=True)
            dln[pl.ds(r0, TQ), :] = jnp.where(hm[0], d0, d1)
            return c

        lax.fori_loop(0, S // TQ, delta_rows, 0)

        for g, r in enumerate(DIL):
            nbl = S // r // BAND
            if r == 1:
                srcs = (q_refs[g], k_refs[g], v_refs[g], dob_ref, lse_ref, dln)
            else:
                for dst, src in ((qp, q_refs[g]), (kp, k_refs[g]), (vp, v_refs[g]), (dop, dob_ref),
                                 (lp, lse_ref), (dlp, dln)):
                    _permute_in(dst, src, r)
                srcs = (qp, kp, vp, dop, lp, dlp)
            dkp[...] = jnp.zeros_like(dkp)
            dvp[...] = jnp.zeros_like(dvp)

            def blk(t, c, srcs=srcs, nbl=nbl):
                qs_, ks_, vs_, dos_, ls_, dls_ = srcs
                work = []
                for u in range(DIL_UNROLL):
                    r0, k0, valid = _band_geometry(DIL_UNROLL * t + u, nbl)
                    q = qs_[pl.ds(r0, BAND), :] * 0.125
                    kwf = ks_[pl.ds(k0, 2 * BAND), :]
                    kw = kwf.astype(bf16)
                    vw = vs_[pl.ds(k0, 2 * BAND), :].astype(bf16)
                    do = dos_[pl.ds(r0, BAND), :]
                    lse = ls_[pl.ds(r0, BAND), :]
                    dlt = dls_[pl.ds(r0, BAND), :]
                    for hh in (0, 1):
                        qh = jnp.where(hm[hh], q, 0.0).astype(bf16)
                        doh = jnp.where(hm[hh], do, 0.0).astype(bf16)
                        kh = jnp.where(hm[hh], kwf, 0.0).astype(bf16)
                        work.append((u, hh, r0, k0, valid, qh, doh, kh, lse[:, hh * HD:hh * HD + 1],
                                     dlt[:, hh * HD:hh * HD + 1], _dot_nt(qh, kw), _dot_nt(doh, vw)))
                for u, hh, r0, k0, valid, qh, doh, kh, lse_h, dlt_h, s, dp in work:
                    if hh == 0:
                        dq = jnp.zeros((BAND, 128), f32)
                        dk = jnp.zeros((2 * BAND, 128), f32)
                        dv = jnp.zeros((2 * BAND, 128), f32)
                    pr = jnp.where(valid, jnp.exp(s - lse_h), 0.0)
                    dsb = (pr * (dp - dlt_h)).astype(bf16)
                    dv = dv + _dot_tn(pr.astype(bf16), doh)
                    dk = dk + _dot_tn(dsb, qh)
                    dq = dq + _dot(dsb, kh)
                    if hh == 1:
                        dqp[pl.ds(r0, BAND), :] = dq * 0.125
                        dkp[pl.ds(k0, 2 * BAND), :] += dk
                        dvp[pl.ds(k0, 2 * BAND), :] += dv
                return c

            lax.fori_loop(0, S // BAND // DIL_UNROLL, blk, 0)

            for acc, out, roped in ((dqp, dq_out[g], True), (dkp, dk_out[g], True), (dvp, dv_out[g], False)):
                if r == 1:
                    src = acc
                else:
                    _permute_out(nat, acc, r)
                    src = nat

                def emit(i, c, src=src, out=out, roped=roped):
                    r0 = pl.multiple_of(i * TQ, TQ)
                    d = src[pl.ds(r0, TQ), :]
                    if roped:
                        d = (d * c_ref[pl.ds(r0, TQ), :] + pltpu.roll(d * s1_ref[pl.ds(r0, TQ), :], 8, 1)
                             + pltpu.roll(d * s2_ref[pl.ds(r0, TQ), :], 120, 1))
                    out[pl.ds(r0, TQ), :] = d.astype(bf16)
                    return c

                lax.fori_loop(0, S // TQ, emit, 0)

    pair = pl.BlockSpec((S, 128), lambda p: (0, p))
    tab = pl.BlockSpec((S, 128), lambda p: (0, 0))
    blk_spec = pl.BlockSpec((S, DIL_BLK), lambda p: (0, p))
    return _call(
        order, body, [qkvb] * 9 + [dob, ob, lseb, c_t, s1_t, s2_t, dproj], name="dil_bwd", grid=(2,),
        in_specs=_dil_in_specs() + [pair, pair, pair, tab, tab, tab, pl.BlockSpec(memory_space=pl.ANY)],
        out_specs=blk_spec,
        out_shape=jax.ShapeDtypeStruct((S, NP), bf16),
        input_output_aliases={15: 0},
        scratch_shapes=[pltpu.VMEM((S, 128), f32)] * 11,
        compiler_params=_params(("parallel",)),
    )


def _inproj_bwd(order, dproj, wt, x, dx2, g1):
    tm = 256

    def body(d_ref, w_ref, x_ref, dx2_ref, g_ref, dx_ref, dg_ref):
        i = pl.program_id(0)
        dh = _dot(d_ref[...], w_ref[...])
        xb = x_ref[...]
        r = lax.rsqrt(jnp.mean(xb * xb, axis=-1, keepdims=True) + EPS)
        xh = xb * r
        gdh = dh * g_ref[...]
        dx_ref[...] = dx2_ref[...] + r * (gdh - xh * jnp.mean(gdh * xh, axis=-1, keepdims=True))

        @pl.when(i == 0)
        def _():
            dg_ref[...] = jnp.zeros_like(dg_ref)

        dg_ref[...] += jnp.sum(dh * xh, axis=0, keepdims=True)

    row = pl.BlockSpec((tm, D), lambda i: (i, 0))
    vec = pl.BlockSpec((1, D), lambda i: (0, 0))
    return _call(
        order, body, (dproj, wt, x, dx2, g1), name="inproj_bwd", grid=(S // tm,),
        in_specs=[pl.BlockSpec((tm, NP), lambda i: (i, 0)), pl.BlockSpec((NP, D), lambda i: (0, 0)), row, row, vec],
        out_specs=[row, vec],
        out_shape=[jax.ShapeDtypeStruct((S, D), f32), jax.ShapeDtypeStruct((1, D), f32)],
        compiler_params=_params(("arbitrary",)),
    )


HBM = pl.BlockSpec(memory_space=pltpu.HBM)
SEM = pl.BlockSpec(memory_space=pltpu.SEMAPHORE)
SMALL_ROWS = 8


def _comm_call(name, body, bufs, order, sems_in=(), new_sems=()):
    nb, ns, nn = len(bufs), len(sems_in), len(new_sems)
    extra = order.token_for(bufs)

    def kern(*refs):
        off = nb + ns + len(extra)
        body(refs[:nb], refs[nb:nb + ns], refs[off:off + nn])
        refs[-1][...] = jnp.zeros((8, 128), f32)

    res = pl.pallas_call(
        kern, name=name,
        in_specs=[HBM] * nb + [SEM] * ns + [pl.BlockSpec(memory_space=pl.ANY)] * len(extra),
        out_specs=[SEM] * nn + [HBM] * nb + [pl.BlockSpec(memory_space=pltpu.VMEM)],
        out_shape=[pltpu.SemaphoreType.DMA((k,)) for k in new_sems] + [pltpu.HBM(b.shape, b.dtype) for b in bufs]
        + [jax.ShapeDtypeStruct((8, 128), f32)],
        input_output_aliases={i: nn + i for i in range(nb)},
        compiler_params=pltpu.CompilerParams(has_side_effects=pltpu.SideEffectType.DATAFLOW_SIDE_EFFECTING),
    )(*[pltpu.with_memory_space_constraint(b, pltpu.HBM) for b in bufs], *sems_in, *extra)
    order.mark(res[-1])
    return list(res[:nn]), list(res[nn:nn + nb])


def _place():
    x, y, c = lax.axis_index("x"), lax.axis_index("y"), lax.axis_index("c")
    chips = [(1 - x, y), (x, 1 - y), (1 - x, 1 - y)]
    return x, y, c, chips


def _rcopy(src, dst, ssem, rsem, dev):
    return pltpu.make_async_remote_copy(src_ref=src, dst_ref=dst, send_sem=ssem, recv_sem=rsem,
                                        device_id=dev, device_id_type=pl.DeviceIdType.MESH)


def _half(nrows, which):
    return pl.ds(which * (nrows // 2), nrows // 2)


def _ici_copies(stack, group_sizes, ssems, rsems):
    x, y, c, chips = _place()
    me_q = 2 * x + y
    sends, recvs = [], []
    a = 0
    for grp, size in enumerate(group_sizes):
        for k in range(size):
            rows = _half(stack[a].shape[1], c)
            for j, (cx, cy) in enumerate(chips):
                mine = stack[a].at[me_q, rows]
                sends.append(_rcopy(mine, mine, ssems[grp].at[k * 3 + j], rsems[grp].at[k * 3 + j], (cx, cy, c)))
                theirs = stack[a].at[2 * cx + cy, rows]
                recvs.append(_rcopy(theirs, theirs, ssems[grp].at[k * 3 + j], rsems[grp].at[k * 3 + j],
                                    (cx, cy, c)))
            a += 1
    return sends, recvs


def _allgather_start(name, stacks, order):
    n = len(stacks)

    def body(bufs, _, new):
        sends, _r = _ici_copies(bufs, [n], [new[0]], [new[1]])
        for cp in sends:
            cp.start()

    return _comm_call(name, body, stacks, order, new_sems=(3 * n, 3 * n))


def _forward_copies(stack, ssem, rsem):
    x, y, c, chips = _place()
    sib = (x, y, 1 - c)
    sends, recvs = [], []
    for a in range(len(stack)):
        for j, (cx, cy) in enumerate(chips):
            landed = stack[a].at[2 * cx + cy, _half(stack[a].shape[1], c)]
            sends.append(_rcopy(landed, landed, ssem.at[a * 3 + j], rsem.at[a * 3 + j], sib))
            other = stack[a].at[2 * cx + cy, _half(stack[a].shape[1], 1 - c)]
            recvs.append(_rcopy(other, other, ssem.at[a * 3 + j], rsem.at[a * 3 + j], sib))
    return sends, recvs


def _allgather_forward(name, stacks, sems, order):
    n = len(stacks)

    def body(bufs, taken, new):
        sends, recvs = _ici_copies(bufs, [n], [taken[0]], [taken[1]])
        for cp in sends:
            cp.wait_send()
        for cp in recvs:
            cp.wait_recv()
        fwd, _r = _forward_copies(bufs, new[0], new[1])
        for cp in fwd:
            cp.start()

    return _comm_call(name, body, stacks, order, sems_in=sems, new_sems=(3 * n, 3 * n))


def _allgather_finish(name, stacks, sems, order):
    def body(bufs, taken, _):
        sends, recvs = _forward_copies(bufs, taken[0], taken[1])
        for cp in sends:
            cp.wait_send()
        for cp in recvs:
            cp.wait_recv()

    return _comm_call(name, body, stacks, order, sems_in=sems)[1]


def _window_unit(q, j):
    return C2I[WIN_UNIT0[q] + j]


def _pair_copies(g, t, ssem, rsem, gathered):
    x, y, c, _ = _place()
    sib = (x, y, 1 - c)
    cps, whole = [], []
    for a in range(len(g)):
        if a == 0 and gathered:
            for q in range(NCHIP):
                for j in range(WIN_UNITS // 2):
                    u = jnp.where(c == 0, _window_unit(q, WIN_UNITS // 2 + j), _window_unit(q, j))
                    src = g[0].at[pl.ds(pl.multiple_of(u * UNIT, UNIT), UNIT), :]
                    cps.append(_rcopy(src, t[0].at[q, pl.ds(j * UNIT, UNIT), :], ssem.at[0], rsem.at[0], sib))
            whole.append(_rcopy(t[0], t[0], ssem.at[0], rsem.at[0], sib))
        else:
            cp = _rcopy(g[a].at[:, _half(g[a].shape[1], 1 - c), :], t[a], ssem.at[a], rsem.at[a], sib)
            cps.append(cp)
            whole.append(cp)
    return cps, whole


def _comm_multi(name, parts, order):
    def body(buf_refs, taken, new):
        ib = it = inew = 0
        for pbody, pbufs, psems, pnew, _ in parts:
            pbody(buf_refs[ib:ib + len(pbufs)], taken[it:it + len(psems)], new[inew:inew + len(pnew)])
            ib, it, inew = ib + len(pbufs), it + len(psems), inew + len(pnew)

    sems, bufs = _comm_call(name, body, [b for p in parts for b in p[1]], order,
                            sems_in=[s for p in parts for s in p[2]], new_sems=[k for p in parts for k in p[3]])
    out, ib, inew = [], 0, 0
    for _, pbufs, _, pnew, unpack in parts:
        out.append(unpack(sems[inew:inew + len(pnew)], bufs[ib:ib + len(pbufs)]))
        ib, inew = ib + len(pbufs), inew + len(pnew)
    return out


def _pair_start_part(gs, gathered=False):
    n = len(gs)
    ts = [lax.empty((NCHIP, WIN_ROWS // 2, D) if (a == 0 and gathered) else (NCHIP, g.shape[1] // 2, g.shape[2]), f32)
          for a, g in enumerate(gs)]

    def body(bufs, _, new):
        for cp in _pair_copies(bufs[:n], bufs[n:], new[0], new[1], gathered)[0]:
            cp.start()

    return body, list(gs) + ts, (), (n, n), lambda sems, bufs: (sems, bufs)


def _pair_wait_part(bufs, sems, gathered=False):
    n = len(bufs) // 2

    def body(refs, taken, _):
        for cp in _pair_copies(refs[:n], refs[n:], taken[0], taken[1], gathered)[1]:
            cp.wait_send()
            cp.wait_recv()

    return body, list(bufs), list(sems), (), lambda _, out: (out[:n], out[n:])


def _row_tile(h):
    return min(h, 256)


def _pair_add(order, g, t, c_arr, name):
    _, R, C = g.shape
    h = R // 2
    tr = _row_tile(h)
    nblk = h // tr

    def body(c_ref, g_ref, t_ref, p32_ref, p16_ref):
        s = g_ref[...] + t_ref[...]
        p32_ref[...] = s
        p16_ref[...] = s.astype(bf16)

    blk = pl.BlockSpec((None, tr, C), lambda q, i, c_ref: (q, i, 0))
    return _call_indexed(
        order, body, (c_arr,), (g, t), (NCHIP, nblk),
        [pl.BlockSpec((None, tr, C), lambda q, i, c_ref: (q, c_ref[0] * nblk + i, 0)), blk], [blk, blk],
        name=name,
        out_shape=[jax.ShapeDtypeStruct((NCHIP, h, C), f32), jax.ShapeDtypeStruct((NCHIP, h, C), bf16)],
        compiler_params=_params(("parallel", "parallel")),
    )


def _pair_add_gathered(order, dwt, t, c_arr, name):
    half_units, half_rows = WIN_UNITS // 2, WIN_ROWS // 2
    table = jnp.asarray([_window_unit(q, j) for q in range(NCHIP) for j in range(WIN_UNITS)], jnp.int32)

    def body(tab_ref, c_ref, g_hbm, t_ref, p32_ref, p16_ref, buf, sem):
        q = pl.program_id(0)

        def gather(w, slot):
            cps = []
            for j in range(half_units):
                u = tab_ref[w * WIN_UNITS + c_ref[0] * half_units + j]
                cps.append(pltpu.make_async_copy(g_hbm.at[pl.ds(pl.multiple_of(u * UNIT, UNIT), UNIT), :],
                                                 buf.at[slot, pl.ds(j * UNIT, UNIT), :], sem.at[slot]))
            return cps

        @pl.when(q == 0)
        def _():
            for cp in gather(0, 0):
                cp.start()

        @pl.when(q + 1 < NCHIP)
        def _():
            for cp in gather(q + 1, (q + 1) % 2):
                cp.start()

        slot = q % 2
        pltpu.make_async_copy(buf.at[slot], buf.at[slot], sem.at[slot]).wait()
        s = buf[slot] + t_ref[...]
        p32_ref[...] = s
        p16_ref[...] = s.astype(bf16)

    blk = pl.BlockSpec((None, half_rows, D), lambda q, tab_ref, c_ref: (q, 0, 0))
    return _call_indexed(
        order, body, (table, c_arr), (dwt, t), (NCHIP,),
        [pl.BlockSpec(memory_space=pl.ANY), blk], [blk, blk],
        scratch_shapes=[pltpu.VMEM((2, half_rows, D), f32), pltpu.SemaphoreType.DMA((2,))],
        name=name,
        out_shape=[jax.ShapeDtypeStruct((NCHIP, half_rows, D), f32),
                   jax.ShapeDtypeStruct((NCHIP, half_rows, D), bf16)],
        compiler_params=_params(("arbitrary",)),
    )


def _shard_copies(p, r, sm, ssem, rsem):
    x, y, c, chips = _place()
    n = len(p)
    sends, recvs = [], []
    for a in range(n):
        for j, (cx, cy) in enumerate(chips):
            k = a * 3 + j
            sends.append(_rcopy(p[a].at[2 * cx + cy], r[a].at[j], ssem.at[k], rsem.at[k], (cx, cy, c)))
            recvs.append(_rcopy(r[a].at[j], r[a].at[j], ssem.at[k], rsem.at[k], (cx, cy, c)))
    if sm is not None:
        mine = sm.at[4 * x + 2 * y + c]
        for i in range(1, 8):
            px = (1 - x) if i & 4 else x
            py = (1 - y) if i & 2 else y
            pc = (1 - c) if i & 1 else c
            k = 3 * n + i - 1
            sends.append(_rcopy(mine, mine, ssem.at[k], rsem.at[k], (px, py, pc)))
            slot = sm.at[4 * px + 2 * py + pc]
            recvs.append(_rcopy(slot, slot, ssem.at[k], rsem.at[k], (px, py, pc)))
    return sends, recvs


def _shard_start_part(p16s, sm=None):
    n = len(p16s)
    rs = [lax.empty((3,) + p.shape[1:], bf16) for p in p16s]
    extra = [] if sm is None else [sm]
    nsem = 3 * n + (7 if sm is not None else 0)

    def body(bufs, _, new):
        sends, _r = _shard_copies(bufs[:n], bufs[n:2 * n], bufs[2 * n] if extra else None, new[0], new[1])
        for cp in sends:
            cp.start()

    return body, list(p16s) + rs + extra, (), (nsem, nsem), lambda sems, bufs: (sems, bufs)


def _shard_wait_part(bufs, sems, n):
    has_sm = len(bufs) > 2 * n

    def body(refs, taken, _):
        sends, recvs = _shard_copies(refs[:n], refs[n:2 * n], refs[2 * n] if has_sm else None, taken[0], taken[1])
        for cp in sends:
            cp.wait_send()
        for cp in recvs:
            cp.wait_recv()

    return body, list(bufs), list(sems), (), lambda _, out: (out[n:2 * n], (out[2 * n] if has_sm else None))


def _shard_sum(order, p32, r, q_arr, c_arr, name):
    _, h, C = p32.shape
    tr = _row_tile(h)
    nblk = h // tr

    def body(q_ref, c_ref, p_ref, r_ref, o_ref):
        s = p_ref[...]
        for j in range(3):
            s = s + r_ref[j].astype(f32)
        o_ref[...] = s

    return _call_indexed(
        order, body, (q_arr, c_arr), (p32, r), (nblk,),
        [pl.BlockSpec((None, tr, C), lambda i, q_ref, c_ref: (q_ref[0], i, 0)),
         pl.BlockSpec((3, tr, C), lambda i, q_ref, c_ref: (0, i, 0))],
        pl.BlockSpec((tr, C), lambda i, q_ref, c_ref: (c_ref[0] * nblk + i, 0)),
        name=name, out_shape=jax.ShapeDtypeStruct((2 * h, C), f32),
        compiler_params=_params(("parallel",)),
    )


def _swap_copies(full, ssem, rsem):
    x, y, c, _ = _place()
    sends, recvs = [], []
    for a in range(len(full)):
        mine = full[a].at[_half(full[a].shape[0], c)]
        sends.append(_rcopy(mine, mine, ssem.at[a], rsem.at[a], (x, y, 1 - c)))
        other = full[a].at[_half(full[a].shape[0], 1 - c)]
        recvs.append(_rcopy(other, other, ssem.at[a], rsem.at[a], (x, y, 1 - c)))
    return sends, recvs


def _swap_start_part(fulls):
    n = len(fulls)

    def body(bufs, _, new):
        for cp in _swap_copies(bufs, new[0], new[1])[0]:
            cp.start()

    return body, list(fulls), (), (n, n), lambda sems, bufs: (sems, bufs)


def _swap_wait_part(fulls, sems):
    def body(refs, taken, _):
        sends, recvs = _swap_copies(refs, taken[0], taken[1])
        for cp in sends:
            cp.wait_send()
        for cp in recvs:
            cp.wait_recv()

    return body, list(fulls), list(sems), (), lambda _, out: out


def _small_sum(order, sm):
    def body(sm_ref, o_ref):
        s = sm_ref[0]
        for d in range(1, 8):
            s = s + sm_ref[d]
        o_ref[...] = s

    return _call(order, body, (sm,), name="small_grad_sum", out_shape=jax.ShapeDtypeStruct((SMALL_ROWS, D), f32))


def _adamw(order, w, g, m, v, name):
    R, C = w.shape
    if R <= 256 or R % 256 == 0:
        tr, tc = min(R, 256), C
    else:
        tr, tc = R, 128

    def body(w_ref, g_ref, m_ref, v_ref, d_ref, nm_ref, nv_ref):
        g_ = g_ref[...]
        m_ = ADAM_B1 * m_ref[...] + (1.0 - ADAM_B1) * g_
        v_ = ADAM_B2 * v_ref[...] + (1.0 - ADAM_B2) * (g_ * g_)
        m_hat = m_ / (1.0 - ADAM_B1 ** ADAM_STEP)
        v_hat = v_ / (1.0 - ADAM_B2 ** ADAM_STEP)
        d_ref[...] = -ADAM_LR * (m_hat / (jnp.sqrt(v_hat) + ADAM_EPS) + ADAM_WD * w_ref[...])
        nm_ref[...] = m_
        nv_ref[...] = v_

    blk = pl.BlockSpec((tr, tc), lambda i, j: (i, j))
    return _call(
        order, body, (w, g, m, v), name=name, grid=(R // tr, C // tc), in_specs=[blk] * 4, out_specs=[blk] * 3,
        out_shape=[jax.ShapeDtypeStruct((R, C), f32)] * 3,
        compiler_params=_params(("parallel", "parallel")),
    )


def _feature_major(w):
    return jnp.transpose(w, (2, 0, 1)).reshape(SHARD_IN, D)


def _unfeature_major(a):
    return jnp.transpose(a.reshape(SHARD_IN, 1, D), (1, 2, 0))


def _window_of(wt, q):
    def plain(k):
        return lambda w: jnp.pad(w, ((OWN_ROW0[k], WIN_ROWS - OWN_ROW0[k] - SHARD_IN), (0, 0))).astype(bf16)

    def chip1(w):
        lo = jnp.pad(w[0:62], ((2, WIN_ROWS - 64), (0, 0)))
        hi = jnp.pad(w[70:SHARD_IN], ((64, WIN_ROWS - 64 - (SHARD_IN - 70)), (0, 0)))
        return (lo + hi).astype(bf16)

    win = lax.switch(q, [plain(0), chip1, plain(2), plain(3)], wt)
    fa = jnp.pad(wt[62:70], ((0, FA_ROWS - 8), (0, 0))).astype(bf16)
    return win, fa


def _own_rows(gwin, gfa, q):
    def plain(k):
        return lambda gw, gf: gw[OWN_ROW0[k]:OWN_ROW0[k] + SHARD_IN]

    def chip1(gw, gf):
        return (jnp.pad(gw[2:64], ((0, SHARD_IN - 62), (0, 0))) + jnp.pad(gf[0:8], ((62, SHARD_IN - 70), (0, 0)))
                + jnp.pad(gw[64:64 + SHARD_IN - 70], ((70, 0), (0, 0))))

    return lax.switch(q, [plain(0), chip1, plain(2), plain(3)], gwin, gfa)


def kernel(x, norm_attn_g, w_in, b_forget, w_branch_a, w_branch_b, w_out, norm_mlp_g, w_up, w_down, norm_final_g, loss_target, m_norm_attn_g, m_w_in, m_b_forget, m_w_branch_a, m_w_branch_b, m_w_out, m_norm_mlp_g, m_w_up, m_w_down, m_norm_final_g, v_norm_attn_g, v_w_in, v_b_forget, v_w_branch_a, v_w_branch_b, v_w_out, v_norm_mlp_g, v_w_up, v_w_down, v_norm_final_g):
    xi, yi, ci = lax.axis_index("x"), lax.axis_index("y"), lax.axis_index("c")
    q_me = 2 * xi + yi
    c_arr = jnp.reshape(ci, (1,)).astype(jnp.int32)
    q_arr = jnp.reshape(q_me, (1,)).astype(jnp.int32)
    x_, tgt = x[0], loss_target[0]

    names = ["w_branch_a", "w_branch_b", "w_out", "w_up", "w_down"]
    big = dict(zip(names, [w_branch_a[0], w_branch_b[0], w_out[0], w_up[0], w_down[0]]))
    ms = dict(zip(names, [m_w_branch_a[0], m_w_branch_b[0], m_w_out[0], m_w_up[0], m_w_down[0]]))
    vs = dict(zip(names, [v_w_branch_a[0], v_w_branch_b[0], v_w_out[0], v_w_up[0], v_w_down[0]]))
    grad, upd = {}, {}
    order = _Order()

    def run(fn, *args, **kw):
        return fn(order, *args, **kw)

    def own_slot(a):
        return lax.dynamic_update_slice(lax.empty((NCHIP,) + a.shape, a.dtype), a[None], (q_me, 0, 0))

    wt_own = _feature_major(w_in)
    win, fa_blk = _window_of(wt_own, q_me)
    sem_in, in_s = _allgather_start("allgather_start_in", [own_slot(win), own_slot(fa_blk)], order)
    sem_rest, rest = _allgather_start("allgather_start_rest", [own_slot(w.astype(bf16)) for w in big.values()], order)
    sem_f, in_s = _allgather_forward("allgather_forward_in", in_s, sem_in, order)
    wins, fas = _allgather_finish("allgather_finish_in", in_s, sem_f, order)
    wt = run(_assemble_win, wins, fas)

    rope = _rope_tables()
    bpad = jnp.pad(b_forget, ((0, 0), (0, 120)))
    h1, qkvb, qkva, gates, fa = run(_norm_inproj, x_, norm_attn_g, wt, rope)
    F = run(_forget_cumsum, fa, bpad)
    oa, lsea = run(_fox_fwd, qkva, F)
    sem_f, rest = _allgather_forward("allgather_forward_rest", rest, sem_rest, order)
    ob, lseb = run(_dil_fwd, qkvb)
    was, wbs, wouts, wups, wdowns = _allgather_finish("allgather_finish_rest", rest, sem_f, order)
    wout = wouts.reshape(D, D)
    wdown = wdowns.reshape(DFF, D)
    ya, yb, mixed = run(_branch_mix, oa, ob, was, wbs, gates)
    x2, h2 = run(_outproj_norm, mixed, wout, x_, norm_mlp_g)
    u, a = run(_mlp_up, h2, wups)
    dx3, dx3b, dg3, loss_part = run(_mlp_down_loss, a, wdown, x2, norm_final_g.reshape(1, D), tgt)

    def comm(name, *parts):
        return _comm_multi(name, list(parts), order)

    def pair_adds(group, gs, ts):
        return zip(*[run(_pair_add, gs[i], ts[i], c_arr, "pair_add_" + nm) for i, nm in enumerate(group)])

    def shard_sums(group, p32s, rs):
        return [run(_shard_sum, p32s[i], rs[i], q_arr, c_arr, "shard_sum_" + nm) for i, nm in enumerate(group)]

    def adamw_group(group, fulls):
        for nm, gfull in zip(group, fulls):
            grad[nm] = gfull
            upd[nm] = run(_adamw, big[nm], gfull, ms[nm], vs[nm], "adamw_" + nm)

    grp_a, grp_b, grp_c = ["w_down", "w_up"], ["w_out", "w_branch_a", "w_branch_b"], ["w_in", "w_in_fa"]
    du = run(_mlp_down_bwd, dx3b, wdown, u)
    dwdown = run(_mm, a, dx3b, "tn", f32, 1024, D, "wgrad_down")
    dwup = run(_mm, h2, du, "tn", f32, D, 1024, "wgrad_up", stack_cols=True)
    ((sem_pa, buf_pa),) = comm("pair_start_a", _pair_start_part([dwdown.reshape(NCHIP, DFF // NCHIP, D), dwup]))
    dx2, dx2b, dg2 = run(_mlp_up_bwd, du, wups, x2, dx3, norm_mlp_g)
    ((gs, ts),) = comm("pair_wait_a", _pair_wait_part(buf_pa, sem_pa))
    p32_a, p16_a = pair_adds(grp_a, gs, ts)
    ((sem_sa, buf_sa),) = comm("shard_start_a", _shard_start_part(p16_a))
    dya, dyb, dproj = run(_gate_bwd, dx2b, wout, gates, ya, yb)
    dwout = run(_mm, mixed, dx2b, "tn", f32, D, D, "wgrad_out")
    doa, dob = run(_branch_bwd, dya, dyb, was, wbs)
    dwas, dwbs = run(_branch_wgrad, oa, ob, dya, dyb)
    ((sem_pb, buf_pb),) = comm("pair_start_b", _pair_start_part([dwout.reshape(NCHIP, D // NCHIP, D), dwas, dwbs]))
    dF, dproj = run(_fox_bwd, qkva, doa, oa, lsea, F, dproj)
    (gs, ts), (rs_a, _) = comm("pair_wait_b_shard_wait_a", _pair_wait_part(buf_pb, sem_pb),
                               _shard_wait_part(buf_sa, sem_sa, len(grp_a)))
    p32_b, p16_b = pair_adds(grp_b, gs, ts)
    fulls_a = shard_sums(grp_a, p32_a, rs_a)
    (sem_wa, fulls_a), (sem_sb, buf_sb) = comm("swap_start_a_shard_start_b", _swap_start_part(fulls_a),
                                               _shard_start_part(p16_b))
    dbf, dproj = run(_forget_bwd, dF, fa, bpad, dproj)
    dproj = run(_dil_bwd, qkvb, dob, ob, lseb, rope, dproj)
    (rs_b, _), fulls_a = comm("shard_wait_b_swap_wait_a", _shard_wait_part(buf_sb, sem_sb, len(grp_b)),
                              _swap_wait_part(fulls_a, sem_wa))
    fulls_b = shard_sums(grp_b, p32_b, rs_b)
    ((sem_wb, fulls_b),) = comm("swap_start_b", _swap_start_part(fulls_b))
    dwt = run(_mm, dproj, h1, "tn", f32, 512, D, "wgrad_in")
    dwfa = jnp.broadcast_to(dwt[F_FA:F_FA + FA_ROWS][None], (NCHIP, FA_ROWS, D))
    (sem_pc, buf_pc), fulls_b = comm("pair_start_c_swap_wait_b", _pair_start_part([dwt, dwfa], gathered=True),
                                     _swap_wait_part(fulls_b, sem_wb))
    adamw_group(grp_b, fulls_b)
    (((dwt_c, dwfa_c), (t_in, t_fa)),) = comm("pair_wait_c", _pair_wait_part(buf_pc, sem_pc, gathered=True))
    p32_in, p16_in = run(_pair_add_gathered, dwt_c, t_in, c_arr, "pair_add_w_in")
    p32_fa, p16_fa = run(_pair_add, dwfa_c, t_fa, c_arr, "pair_add_w_in_fa")
    ((sem_sc, buf_sc),) = comm("shard_start_c", _shard_start_part([p16_in, p16_fa]))
    gx, dg1 = run(_inproj_bwd, dproj, wt, x_, dx2, norm_attn_g)
    adamw_group(grp_a, fulls_a)
    small = jnp.concatenate([dg1, dg2, dg3, jnp.pad(dbf[:, 0:8], ((0, 0), (0, D - 8))),
                             jnp.pad(loss_part, ((0, 0), (0, D - 128))),
                             jnp.zeros((SMALL_ROWS - 5, D), f32)], axis=0)
    sm = lax.dynamic_update_slice(lax.empty((8, SMALL_ROWS, D), f32), small[None],
                                  (4 * xi + 2 * yi + ci, 0, 0))
    (sem_sm, buf_sm), (rs_c, _) = comm("small_start_shard_wait_c", _shard_start_part([], sm),
                                       _shard_wait_part(buf_sc, sem_sc, len(grp_c)))
    fulls_c = shard_sums(grp_c, [p32_in, p32_fa], rs_c)
    (sem_wc, fulls_c), (_, sm) = comm("swap_start_c_small_wait", _swap_start_part(fulls_c),
                                      _shard_wait_part(buf_sm, sem_sm, 0))
    gsmall = run(_small_sum, sm)
    loss = gsmall[4, 0]

    grad["norm_attn_g"], grad["norm_mlp_g"] = gsmall[0:1], gsmall[1:2]
    grad["norm_final_g"], grad["b_forget"] = gsmall[2:3], gsmall[3:4, 0:8]
    upd["norm_attn_g"] = run(_adamw, norm_attn_g, grad["norm_attn_g"], m_norm_attn_g, v_norm_attn_g, "adamw_g1")
    upd["norm_mlp_g"] = run(_adamw, norm_mlp_g, grad["norm_mlp_g"], m_norm_mlp_g, v_norm_mlp_g, "adamw_g2")
    upd["norm_final_g"] = run(_adamw, norm_final_g.reshape(1, D), grad["norm_final_g"],
                              m_norm_final_g.reshape(1, D), v_norm_final_g.reshape(1, D), "adamw_g3")
    upd["b_forget"] = run(_adamw, b_forget, grad["b_forget"], m_b_forget, v_b_forget, "adamw_bf")

    ((gwin, gfa),) = comm("swap_wait_c", _swap_wait_part(fulls_c, sem_wc))
    g_in = _own_rows(gwin, gfa, q_me)
    upd_in = run(_adamw, wt_own, g_in, _feature_major(m_w_in), _feature_major(v_w_in), "adamw_w_in")
    grad["w_in"] = _unfeature_major(g_in)
    upd["w_in"] = [_unfeature_major(t) for t in upd_in]

    order_out = ["norm_attn_g", "w_in", "b_forget", "w_branch_a", "w_branch_b", "w_out", "norm_mlp_g", "w_up",
                 "w_down", "norm_final_g"]
    shapes = dict(norm_attn_g=norm_attn_g.shape, w_in=w_in.shape, b_forget=b_forget.shape,
                  w_branch_a=w_branch_a.shape, w_branch_b=w_branch_b.shape, w_out=w_out.shape,
                  norm_mlp_g=norm_mlp_g.shape, w_up=w_up.shape, w_down=w_down.shape, norm_final_g=norm_final_g.shape)
    outs = [loss, gx.reshape(x.shape)]
    outs += [grad[nm].reshape(shapes[nm]) for nm in order_out]
    for k in range(3):
        outs += [upd[nm][k].reshape(shapes[nm]) for nm in order_out]
    return tuple(outs)
```

```python
import jax
import jax.numpy as jnp
from jax import lax
from jax.experimental import pallas as pl
from jax.experimental.pallas import tpu as pltpu

f32 = jnp.float32
bf16 = jnp.bfloat16

S = 2048
D = 1024
DFF = 4096
HD = 64
FOXW = 512
DILOUT = 256
DIL = (1, 4, 16)
BAND = 128
EPS = 1e-6
NEG = -1e30
ROPE_THETA = 500000.0
NCHIP = 4
TQ = 256

ADAM_LR, ADAM_B1, ADAM_B2, ADAM_EPS, ADAM_WD, ADAM_STEP = 0.001, 0.9, 0.999, 1e-08, 0.01, 10
VMEM_LIMIT = 56 * 1024 * 1024

UNIT = 64
NP = 6144
F_DIL, F_FOX, F_FA, F_G = 0, 2304, 3840, 4096
DIL_BLK, FOX_BLK = 1152, 384
WIN_UNITS, WIN_ROWS = 24, 1536
WIN_UNIT0 = (0, 23, 45, 68)
OWN_ROW0 = (0, 2, 60, 62)
SHARD_IN = 1474
FA_ROWS = 32


def _compact_to_internal():
    c2i = {}
    for p in range(2):
        for role in range(3):
            for g in range(3):
                for hh in range(2):
                    c2i[24 + 12 * role + 4 * g + 2 * p + hh] = 18 * p + 6 * role + 2 * g + hh
    for p in range(4):
        for role in range(3):
            for hh in range(2):
                c2i[8 * role + 2 * p + hh] = F_FOX // UNIT + 6 * p + 2 * role + hh
    for j in range(32):
        c2i[60 + j] = F_G // UNIT + j
    return c2i


C2I = _compact_to_internal()
OVERLAP_UNITS = (23, 45, 46, 68)


def _params(sem=None):
    return pltpu.CompilerParams(dimension_semantics=sem, vmem_limit_bytes=VMEM_LIMIT)


class _Order:
    def __init__(self):
        self.tok = None

    def mark(self, v):
        self.tok = v

    def token_for(self, args):
        return [] if self.tok is None or any(self.tok is a for a in args) else [self.tok]


def _call(order, body, args, in_specs=None, **kw):
    args = list(args)
    n_in = len(args)
    if in_specs is None:
        in_specs = [pl.BlockSpec(memory_space=pltpu.VMEM)] * n_in
    kern = body
    extra = order.token_for(args)
    if extra:
        in_specs = list(in_specs) + [pl.BlockSpec(memory_space=pl.ANY)]

        def kern(*refs):
            body(*refs[:n_in], *refs[n_in + 1:])

    out = pl.pallas_call(kern, in_specs=in_specs, **kw)(*args, *extra)
    order.mark(out[0] if isinstance(out, (tuple, list)) else out)
    return out


def _call_indexed(order, body, scalars, args, grid, in_specs, out_specs, scratch_shapes=(), **kw):
    args, in_specs = list(args), list(in_specs)
    n_front = len(scalars) + len(args)
    kern = body
    extra = order.token_for(args)
    if extra:
        in_specs.append(pl.BlockSpec(memory_space=pl.ANY))

        def kern(*refs):
            body(*refs[:n_front], *refs[n_front + 1:])

    out = pl.pallas_call(
        kern, grid_spec=pltpu.PrefetchScalarGridSpec(num_scalar_prefetch=len(scalars), grid=grid, in_specs=in_specs,
                                                     out_specs=out_specs, scratch_shapes=scratch_shapes),
        **kw)(*scalars, *args, *extra)
    order.mark(out[0] if isinstance(out, (tuple, list)) else out)
    return out


def _dot(a, b):
    return jnp.dot(a, b, preferred_element_type=f32)


def _dot_nt(a, b):
    return lax.dot_general(a, b, (((1,), (1,)), ((), ())), preferred_element_type=f32)


def _dot_tn(a, b):
    return lax.dot_general(a, b, (((0,), (0,)), ((), ())), preferred_element_type=f32)


def _split3(x):
    hi = x.astype(bf16)
    r1 = x - hi.astype(f32)
    mid = r1.astype(bf16)
    lo = (r1 - mid.astype(f32)).astype(bf16)
    return hi, mid, lo


def _rope_tables():
    half = 8
    inv_freq = jnp.power(jnp.float32(ROPE_THETA), -jnp.arange(half, dtype=f32) * 2.0 / 16)
    ang = jnp.arange(S).astype(f32)[:, None] * inv_freq[None, :]
    cos, sin = jnp.cos(ang), jnp.sin(ang)
    one = jnp.ones((S, HD - 16), f32)
    zero = jnp.zeros((S, HD - 16), f32)
    z8 = jnp.zeros((S, 8), f32)
    c = jnp.concatenate([cos, cos, one], axis=1)
    s1 = jnp.concatenate([-sin, z8, zero], axis=1)
    s2 = jnp.concatenate([z8, sin, zero], axis=1)
    return tuple(jnp.concatenate([t, t], axis=1) for t in (c, s1, s2))


def _mm(order, a, b, mode, out_dtype, tm, tn, name, stack_cols=False):
    if mode == "nn":
        (M, K), (_, N) = a.shape, b.shape
        a_spec = pl.BlockSpec((tm, K), lambda i, j: (i, 0))
        b_spec = pl.BlockSpec((K, tn), lambda i, j: (0, j))
        dot = _dot
    elif mode == "nt":
        (M, K), (N, _) = a.shape, b.shape
        a_spec = pl.BlockSpec((tm, K), lambda i, j: (i, 0))
        b_spec = pl.BlockSpec((tn, K), lambda i, j: (j, 0))
        dot = _dot_nt
    else:
        (K, M), (_, N) = a.shape, b.shape
        a_spec = pl.BlockSpec((K, tm), lambda i, j: (0, i))
        b_spec = pl.BlockSpec((K, tn), lambda i, j: (0, j))
        dot = _dot_tn

    def body(a_ref, b_ref, o_ref):
        o_ref[...] = dot(a_ref[...], b_ref[...]).astype(out_dtype)

    if stack_cols:
        assert tm == M
        out_spec = pl.BlockSpec((None, tm, tn), lambda i, j: (j, 0, 0))
        out_shape = jax.ShapeDtypeStruct((N // tn, M, tn), out_dtype)
    else:
        out_spec = pl.BlockSpec((tm, tn), lambda i, j: (i, j))
        out_shape = jax.ShapeDtypeStruct((M, N), out_dtype)
    return _call(
        order, body, (a, b), name=name, grid=(M // tm, N // tn), in_specs=[a_spec, b_spec],
        out_specs=out_spec, out_shape=out_shape,
        compiler_params=_params(("parallel", "parallel")),
    )


def _assemble_win(order, wins, fas):
    def body(win_ref, fa_ref, o_ref):
        q = pl.program_id(0)

        @pl.when(q == 0)
        def _():
            o_ref[...] = jnp.zeros_like(o_ref)

        for k in range(NCHIP):
            @pl.when(q == k)
            def _(k=k):
                for j in range(WIN_UNITS):
                    cu = WIN_UNIT0[k] + j
                    dst = pl.ds(C2I[cu] * UNIT, UNIT)
                    if cu in OVERLAP_UNITS:
                        o_ref[dst, :] += win_ref[j * UNIT:(j + 1) * UNIT, :]
                    else:
                        o_ref[dst, :] = win_ref[j * UNIT:(j + 1) * UNIT, :]
                if k == 1:
                    o_ref[F_FA:F_FA + FA_ROWS, :] = fa_ref[...]

    return _call(
        order, body, (wins, fas), name="assemble_w_in", grid=(NCHIP,),
        in_specs=[pl.BlockSpec((None, WIN_ROWS, D), lambda q: (q, 0, 0)),
                  pl.BlockSpec((None, FA_ROWS, D), lambda q: (1, 0, 0))],
        out_specs=pl.BlockSpec((NP, D), lambda q: (0, 0)),
        out_shape=jax.ShapeDtypeStruct((NP, D), bf16),
        compiler_params=_params(("arbitrary",)),
    )


def _norm_inproj(order, x, g1, wt, rope):
    tm = 256
    c_t, s1_t, s2_t = rope

    def body(x_ref, g_ref, w_ref, c_ref, s1_ref, s2_ref, h_ref, qkvb_ref, qkva_ref, gates_ref, fa_ref):
        xb = x_ref[...]
        r = lax.rsqrt(jnp.mean(xb * xb, axis=-1, keepdims=True) + EPS)
        h = ((xb * r) * g_ref[...]).astype(bf16)
        h_ref[...] = h
        c, s1, s2 = c_ref[...], s1_ref[...], s2_ref[...]
        for p in range(2):
            pb = _dot_nt(h, w_ref[F_DIL + p * DIL_BLK:F_DIL + (p + 1) * DIL_BLK, :])
            for ch in range(DIL_BLK // 128):
                pc = pb[:, ch * 128:(ch + 1) * 128]
                if ch < 6:
                    pc = pc * c + pltpu.roll(pc, 120, 1) * s1 + pltpu.roll(pc, 8, 1) * s2
                qkvb_ref[:, p * DIL_BLK + ch * 128:p * DIL_BLK + (ch + 1) * 128] = pc
        qkva_ref[...] = _dot_nt(h, w_ref[F_FOX:F_FA, :]).astype(bf16)
        fa_ref[...] = _dot_nt(h, w_ref[F_FA:F_FA + 128, :])
        gates_ref[...] = _dot_nt(h, w_ref[F_G:NP, :]).astype(bf16)

    row = lambda w: pl.BlockSpec((tm, w), lambda i: (i, 0))
    return _call(
        order, body, (x, g1, wt, c_t, s1_t, s2_t), name="norm_inproj", grid=(S // tm,),
        in_specs=[row(D), pl.BlockSpec((1, D), lambda i: (0, 0)), pl.BlockSpec((NP, D), lambda i: (0, 0)),
                  row(128), row(128), row(128)],
        out_specs=[row(D), row(2 * DIL_BLK), row(4 * FOX_BLK), row(2 * D), row(128)],
        out_shape=[jax.ShapeDtypeStruct((S, D), bf16), jax.ShapeDtypeStruct((S, 2 * DIL_BLK), f32),
                   jax.ShapeDtypeStruct((S, 4 * FOX_BLK), bf16), jax.ShapeDtypeStruct((S, 2 * D), bf16),
                   jax.ShapeDtypeStruct((S, 128), f32)],
        compiler_params=_params(("parallel",)),
    )


def _forget_cumsum(order, fa, bpad):
    nb = S // TQ

    def body(fa_ref, b_ref, F_ref):
        rr = lax.broadcasted_iota(jnp.int32, (TQ, TQ), 0)
        cc = lax.broadcasted_iota(jnp.int32, (TQ, TQ), 1)
        tri = (rr >= cc).astype(bf16)
        lane = lax.broadcasted_iota(jnp.int32, (1, 128), 1)
        carry = jnp.zeros((1, 128), f32)
        for b in range(nb):
            z = fa_ref[b * TQ:(b + 1) * TQ, :] + b_ref[...]
            lf = jnp.minimum(z, 0.0) - jnp.log(1.0 + jnp.exp(-jnp.abs(z)))
            lf = jnp.where(lane < 8, lf, 0.0)
            hi, mid, lo = _split3(lf)
            fb = (_dot(tri, hi) + _dot(tri, mid)) + _dot(tri, lo) + carry
            F_ref[b * TQ:(b + 1) * TQ, :] = fb
            carry = fb[TQ - 1:TQ, :]

    return _call(
        order, body, (fa, bpad), name="forget_cumsum",
        out_shape=jax.ShapeDtypeStruct((S, 128), f32),
        compiler_params=_params(),
    )


def _head_masks():
    lane = lax.broadcasted_iota(jnp.int32, (1, 128), 1)
    return lane, (lane < HD, lane >= HD)


L_FT, L_ONE, L_LSE = 0, 3, 6
FOX_TQ, FOX_TK = 256, 512


def _set_lanes(x, lane, first, cols):
    for n, col in enumerate(cols):
        x = jnp.where(lane == first + n, col, x)
    return x


def _f32_parts(col):
    return [t.astype(f32) for t in _split3(col)]


def _fox_operands(qkv_ref, F_ref, lse_ref, qa, ka, p, rows):
    lane, hm = _head_masks()
    q = qkv_ref[rows, 0:128].astype(f32) * 0.125
    k = qkv_ref[rows, 128:256].astype(f32)
    Fb = F_ref[rows, :]
    for hh in (0, 1):
        free = (1 - hh) * HD
        fparts = _f32_parts(jnp.sum(jnp.where(lane == 2 * p + hh, Fb, 0.0), axis=1, keepdims=True))
        qcols = fparts + [1.0] * 3
        kcols = [1.0] * 3 + [-t for t in fparts]
        if lse_ref is not None:
            qcols += [-t for t in _f32_parts(lse_ref[rows, hh * HD:hh * HD + 1])]
            kcols += [1.0] * 3
        qa[hh, rows, :] = _set_lanes(jnp.where(hm[hh], q, 0.0), lane, free, qcols).astype(bf16)
        ka[hh, rows, :] = _set_lanes(k, lane, free, kcols).astype(bf16)


def _fox_fwd(order, qkva, F):
    tq, tk = FOX_TQ, FOX_TK

    def body(qkv_ref, F_ref, o_ref, lse_ref, qa, ka, vt):
        p = pl.program_id(0)
        keyi = lax.broadcasted_iota(jnp.int32, (tk, 1), 0)
        qryi = lax.broadcasted_iota(jnp.int32, (1, tq), 1)
        sub = lax.broadcasted_iota(jnp.int32, (128, 1), 0)

        def prep(i, c):
            rows = pl.ds(pl.multiple_of(i * tk, tk), tk)
            _fox_operands(qkv_ref, F_ref, None, qa, ka, p, rows)
            vt[i] = qkv_ref[rows, 256:384].astype(f32).T.astype(bf16)
            return c

        lax.fori_loop(0, S // tk, prep, 0)

        def qblock(i, c):
            r0 = pl.multiple_of(i * tq, tq)
            qh = [qa[hh, pl.ds(r0, tq), :] for hh in (0, 1)]

            def kv(jb, carry, masked):
                keys = pl.ds(pl.multiple_of(jb * tk, tk), tk)
                sts = [_dot_nt(ka[hh, keys, :], qh[hh]) for hh in (0, 1)]
                new = []
                for hh in (0, 1):
                    m, l, a = carry[3 * hh:3 * hh + 3]
                    st = sts[hh]
                    if masked:
                        st = jnp.where(jb * tk + keyi <= r0 + qryi, st, NEG)
                    mn = jnp.maximum(m, jnp.max(st, axis=0, keepdims=True))
                    al = jnp.exp(m - mn)
                    pt = jnp.exp(st - mn)
                    l = al * l + jnp.sum(pt, axis=0, keepdims=True)
                    a = al * a + _dot(vt[jb, hh * HD:(hh + 1) * HD, :], pt.astype(bf16))
                    new += [mn, l, a]
                return tuple(new)

            init = (jnp.full((1, tq), NEG, f32), jnp.zeros((1, tq), f32), jnp.zeros((HD, tq), f32)) * 2
            last = (r0 + tq - 1) // tk
            carry = lax.fori_loop(0, last, lambda j, cr: kv(j, cr, False), init)
            m0, l0, a0, m1, l1, a1 = kv(last, carry, True)
            ot = jnp.concatenate([a0 / l0, a1 / l1], axis=0)
            lt = jnp.where(sub < HD, m0 + jnp.log(l0), m1 + jnp.log(l1))
            o_ref[pl.ds(r0, tq), :] = ot.T.astype(bf16)
            lse_ref[pl.ds(r0, tq), :] = lt.T
            return c

        lax.fori_loop(0, S // tq, qblock, 0)

    pair = pl.BlockSpec((S, 128), lambda p: (0, p))
    return _call(
        order, body, (qkva, F), name="fox_fwd", grid=(4,),
        in_specs=[pl.BlockSpec((S, FOX_BLK), lambda p: (0, p)), pl.BlockSpec((S, 128), lambda p: (0, 0))],
        out_specs=[pair, pair],
        out_shape=[jax.ShapeDtypeStruct((S, FOXW), bf16), jax.ShapeDtypeStruct((S, FOXW), f32)],
        scratch_shapes=[pltpu.VMEM((2, S, 128), bf16)] * 2 + [pltpu.VMEM((S // tk, 128, tk), bf16)],
        compiler_params=_params(("parallel",)),
    )


def _permute_in(dst, src, r):
    L = S // r
    for rho in range(r):
        dst[rho * L:(rho + 1) * L, :] = src[pl.ds(rho, L, stride=r), :]


def _permute_out(dst, src, r):
    L = S // r
    for rho in range(r):
        dst[pl.ds(rho, L, stride=r), :] = src[rho * L:(rho + 1) * L, :]


def _band_geometry(bb, nbl):
    r0 = pl.multiple_of(bb * BAND, BAND)
    k0 = pl.multiple_of(jnp.maximum(bb - 1, 0) * BAND, BAND)
    sub0 = (bb - lax.rem(bb, nbl)) * BAND
    qi = r0 + lax.broadcasted_iota(jnp.int32, (BAND, 1), 0)
    ki = k0 + lax.broadcasted_iota(jnp.int32, (1, 2 * BAND), 1)
    diff = qi - ki
    valid = (diff >= 0) & (diff <= BAND) & (ki >= sub0)
    return r0, k0, valid


def _dil_views(ref):
    return [[ref.at[:, pl.ds((3 * role + g) * 128, 128)] for g in range(3)] for role in range(3)]


DIL_UNROLL = 4


def _dil_in_specs():
    return [pl.BlockSpec((S, 128), lambda p, k=k: (0, 9 * p + k)) for k in range(9)]


def _dil_fwd(order, qkvb):
    def body(*refs):
        q_refs, k_refs, v_refs = refs[0:3], refs[3:6], refs[6:9]
        ob_ref, lse_ref, qp, kp, vp, op, lp = refs[9:16]
        on, ln = refs[16:19], refs[19:22]
        _, hm = _head_masks()
        for g, r in enumerate(DIL):
            nbl = S // r // BAND
            if r == 1:
                qs_, ks_, vs_, od, ld = q_refs[g], k_refs[g], v_refs[g], on[g], ln[g]
            else:
                _permute_in(qp, q_refs[g], r)
                _permute_in(kp, k_refs[g], r)
                _permute_in(vp, v_refs[g], r)
                qs_, ks_, vs_, od, ld = qp, kp, vp, op, lp

            def blk(t, c, qs_=qs_, ks_=ks_, vs_=vs_, od=od, ld=ld, nbl=nbl):
                work = []
                for u in range(DIL_UNROLL):
                    r0, k0, valid = _band_geometry(DIL_UNROLL * t + u, nbl)
                    q = qs_[pl.ds(r0, BAND), :] * 0.125
                    kw = ks_[pl.ds(k0, 2 * BAND), :].astype(bf16)
                    vw = vs_[pl.ds(k0, 2 * BAND), :]
                    for hh in (0, 1):
                        qh = jnp.where(hm[hh], q, 0.0).astype(bf16)
                        work.append((u, hh, r0, valid, vw, _dot_nt(qh, kw)))
                o = [jnp.zeros((BAND, 128), f32)] * DIL_UNROLL
                lse = [jnp.zeros((BAND, 128), f32)] * DIL_UNROLL
                for u, hh, r0, valid, vw, s in work:
                    s = jnp.where(valid, s, NEG)
                    m = jnp.max(s, axis=1, keepdims=True)
                    pr = jnp.exp(s - m)
                    l = jnp.sum(pr, axis=1, keepdims=True)
                    vm = jnp.where(hm[hh], vw, 0.0).astype(bf16)
                    o[u] = o[u] + _dot((pr / l).astype(bf16), vm)
                    lse[u] = jnp.where(hm[hh], m + jnp.log(l), lse[u])
                    if hh == 1:
                        od[pl.ds(r0, BAND), :] = o[u]
                        ld[pl.ds(r0, BAND), :] = lse[u]
                return c

            lax.fori_loop(0, S // BAND // DIL_UNROLL, blk, 0)
            if r != 1:
                _permute_out(on[g], op, r)
                _permute_out(ln[g], lp, r)

        def combine(i, c):
            r0 = pl.multiple_of(i * TQ, TQ)
            ls = [ln[g][pl.ds(r0, TQ), :] for g in range(3)]
            mx = jnp.maximum(jnp.maximum(ls[0], ls[1]), ls[2])
            es = [jnp.exp(l - mx) for l in ls]
            tot = (es[0] + es[1]) + es[2]
            acc = (es[0] / tot) * on[0][pl.ds(r0, TQ), :]
            acc = acc + (es[1] / tot) * on[1][pl.ds(r0, TQ), :]
            acc = acc + (es[2] / tot) * on[2][pl.ds(r0, TQ), :]
            ob_ref[pl.ds(r0, TQ), :] = acc.astype(bf16)
            lse_ref[pl.ds(r0, TQ), :] = mx + jnp.log(tot)
            return c

        lax.fori_loop(0, S // TQ, combine, 0)

    out_blk = pl.BlockSpec((S, 128), lambda p: (0, p))
    return _call(
        order, body, [qkvb] * 9, name="dil_fwd", grid=(2,),
        in_specs=_dil_in_specs(), out_specs=[out_blk, out_blk],
        out_shape=[jax.ShapeDtypeStruct((S, DILOUT), bf16), jax.ShapeDtypeStruct((S, DILOUT), f32)],
        scratch_shapes=[pltpu.VMEM((S, 128), f32)] * 11,
        compiler_params=_params(("parallel",)),
    )


def _branch_mix(order, oa, ob, was, wbs, gates):
    tm = 512

    def body(oa_ref, ob_ref, wa_ref, wb_ref, g_ref, ya_ref, yb_ref, mix_ref):
        oa_b, ob_b = oa_ref[...], ob_ref[...]
        for q in range(NCHIP):
            cols = slice(q * 256, (q + 1) * 256)
            ya = _dot(oa_b, wa_ref[q])
            yb = _dot(ob_b, wb_ref[q])
            ya_ref[:, cols] = ya.astype(bf16)
            yb_ref[:, cols] = yb.astype(bf16)
            ga = g_ref[:, q * 256:(q + 1) * 256].astype(f32)
            gb = g_ref[:, D + q * 256:D + (q + 1) * 256].astype(f32)
            mix_ref[:, cols] = (jax.nn.sigmoid(ga) * ya + jax.nn.sigmoid(gb) * yb).astype(bf16)

    row = lambda w: pl.BlockSpec((tm, w), lambda i: (i, 0))
    full3 = lambda a: pl.BlockSpec(a.shape, lambda i: (0, 0, 0))
    return _call(
        order, body, (oa, ob, was, wbs, gates), name="branch_mix", grid=(S // tm,),
        in_specs=[row(FOXW), row(DILOUT), full3(was), full3(wbs), row(2 * D)],
        out_specs=[row(D), row(D), row(D)],
        out_shape=[jax.ShapeDtypeStruct((S, D), bf16), jax.ShapeDtypeStruct((S, D), bf16),
                   jax.ShapeDtypeStruct((S, D), bf16)],
        compiler_params=_params(("parallel",)),
    )


def _outproj_norm(order, mixed, wout, x, g2):
    tm = 512

    def body(m_ref, w_ref, x_ref, g_ref, x2_ref, h2_ref):
        x2 = x_ref[...] + _dot(m_ref[...], w_ref[...])
        x2_ref[...] = x2
        r = lax.rsqrt(jnp.mean(x2 * x2, axis=-1, keepdims=True) + EPS)
        h2_ref[...] = ((x2 * r) * g_ref[...]).astype(bf16)

    row = pl.BlockSpec((tm, D), lambda i: (i, 0))
    return _call(
        order, body, (mixed, wout, x, g2), name="outproj_norm", grid=(S // tm,),
        in_specs=[row, pl.BlockSpec((D, D), lambda i: (0, 0)), row, pl.BlockSpec((1, D), lambda i: (0, 0))],
        out_specs=[row, row],
        out_shape=[jax.ShapeDtypeStruct((S, D), f32), jax.ShapeDtypeStruct((S, D), bf16)],
        compiler_params=_params(("parallel",)),
    )


def _mlp_up(order, h2, wups):
    tm = 1024

    def body(h_ref, w_ref, ru_ref, a_ref):
        ru = jnp.maximum(_dot(h_ref[...], w_ref[...]), 0.0)
        ru_ref[...] = ru.astype(bf16)
        a_ref[...] = (ru * ru).astype(bf16)

    out = pl.BlockSpec((tm, D), lambda q, i: (i, q))
    return _call(
        order, body, (h2, wups), name="mlp_up", grid=(NCHIP, S // tm),
        in_specs=[pl.BlockSpec((tm, D), lambda q, i: (i, 0)), pl.BlockSpec((None, D, D), lambda q, i: (q, 0, 0))],
        out_specs=[out, out],
        out_shape=[jax.ShapeDtypeStruct((S, DFF), bf16), jax.ShapeDtypeStruct((S, DFF), bf16)],
        compiler_params=_params(("parallel", "parallel")),
    )


def _mlp_down_loss(order, a, wdown, x2, g3, tgt):
    tm = 512

    def body(a_ref, w_ref, x2_ref, g_ref, t_ref, dx_ref, dxb_ref, dg_ref, loss_ref):
        i = pl.program_id(0)
        x3 = x2_ref[...] + _dot(a_ref[...], w_ref[...])
        r = lax.rsqrt(jnp.mean(x3 * x3, axis=-1, keepdims=True) + EPS)
        xh = x3 * r
        g = g_ref[...]
        e = xh * g - t_ref[...]
        part = 0.5 * jnp.sum(jnp.mean(e * e, axis=-1, keepdims=True), axis=0, keepdims=True)
        dy = e * (1.0 / D)
        gdy = dy * g
        dx = r * (gdy - xh * jnp.mean(gdy * xh, axis=-1, keepdims=True))
        dx_ref[...] = dx
        dxb_ref[...] = dx.astype(bf16)

        @pl.when(i == 0)
        def _():
            dg_ref[...] = jnp.zeros_like(dg_ref)
            loss_ref[...] = jnp.zeros_like(loss_ref)

        dg_ref[...] += jnp.sum(dy * xh, axis=0, keepdims=True)
        loss_ref[...] += jnp.broadcast_to(part, (1, 128))

    row = pl.BlockSpec((tm, D), lambda i: (i, 0))
    vec = pl.BlockSpec((1, D), lambda i: (0, 0))
    return _call(
        order, body, (a, wdown, x2, g3, tgt), name="mlp_down_loss", grid=(S // tm,),
        in_specs=[pl.BlockSpec((tm, DFF), lambda i: (i, 0)), pl.BlockSpec((DFF, D), lambda i: (0, 0)), row, vec, row],
        out_specs=[row, row, vec, pl.BlockSpec((1, 128), lambda i: (0, 0))],
        out_shape=[jax.ShapeDtypeStruct((S, D), f32), jax.ShapeDtypeStruct((S, D), bf16),
                   jax.ShapeDtypeStruct((1, D), f32), jax.ShapeDtypeStruct((1, 128), f32)],
        compiler_params=_params(("arbitrary",)),
    )


def _mlp_down_bwd(order, dx3b, wdown, u):
    tm = 512

    def body(d_ref, w_ref, u_ref, du_ref):
        d = d_ref[...]
        for q in range(NCHIP):
            cols = slice(q * D, (q + 1) * D)
            da = _dot_nt(d, w_ref[cols, :])
            du_ref[:, cols] = (da * (2.0 * u_ref[:, cols].astype(f32))).astype(bf16)

    return _call(
        order, body, (dx3b, wdown, u), name="mlp_down_bwd", grid=(S // tm,),
        in_specs=[pl.BlockSpec((tm, D), lambda i: (i, 0)), pl.BlockSpec((DFF, D), lambda i: (0, 0)),
                  pl.BlockSpec((tm, DFF), lambda i: (i, 0))],
        out_specs=pl.BlockSpec((tm, DFF), lambda i: (i, 0)),
        out_shape=jax.ShapeDtypeStruct((S, DFF), bf16),
        compiler_params=_params(("parallel",)),
    )


def _mlp_up_bwd(order, du, wups, x2, dx3, g2):
    tm = 512

    def body(du_ref, w_ref, x2_ref, dx3_ref, g_ref, dx2_ref, dx2b_ref, dg_ref):
        i = pl.program_id(0)
        dh = jnp.zeros((tm, D), f32)
        for q in range(NCHIP):
            dh = dh + _dot_nt(du_ref[:, q * D:(q + 1) * D], w_ref[q])
        x2 = x2_ref[...]
        r = lax.rsqrt(jnp.mean(x2 * x2, axis=-1, keepdims=True) + EPS)
        xh = x2 * r
        gdh = dh * g_ref[...]
        dx2 = dx3_ref[...] + r * (gdh - xh * jnp.mean(gdh * xh, axis=-1, keepdims=True))
        dx2_ref[...] = dx2
        dx2b_ref[...] = dx2.astype(bf16)

        @pl.when(i == 0)
        def _():
            dg_ref[...] = jnp.zeros_like(dg_ref)

        dg_ref[...] += jnp.sum(dh * xh, axis=0, keepdims=True)

    row = pl.BlockSpec((tm, D), lambda i: (i, 0))
    vec = pl.BlockSpec((1, D), lambda i: (0, 0))
    return _call(
        order, body, (du, wups, x2, dx3, g2), name="mlp_up_bwd", grid=(S // tm,),
        in_specs=[pl.BlockSpec((tm, DFF), lambda i: (i, 0)), pl.BlockSpec((NCHIP, D, D), lambda i: (0, 0, 0)),
                  row, row, vec],
        out_specs=[row, row, vec],
        out_shape=[jax.ShapeDtypeStruct((S, D), f32), jax.ShapeDtypeStruct((S, D), bf16),
                   jax.ShapeDtypeStruct((1, D), f32)],
        compiler_params=_params(("arbitrary",)),
    )


def _gate_bwd(order, dx2b, wout, gates, ya, yb):
    tm = 512

    def body(d_ref, w_ref, g_ref, ya_ref, yb_ref, dya_ref, dyb_ref, dproj_ref):
        dm = _dot_nt(d_ref[...], w_ref[...])
        sa = jax.nn.sigmoid(g_ref[:, 0:D].astype(f32))
        sb = jax.nn.sigmoid(g_ref[:, D:2 * D].astype(f32))
        dya_ref[...] = (dm * sa).astype(bf16)
        dyb_ref[...] = (dm * sb).astype(bf16)
        dproj_ref[:, 0:D] = (dm * ya_ref[...].astype(f32) * (sa * (1.0 - sa))).astype(bf16)
        dproj_ref[:, D:2 * D] = (dm * yb_ref[...].astype(f32) * (sb * (1.0 - sb))).astype(bf16)

    row = lambda w: pl.BlockSpec((tm, w), lambda i: (i, 0))
    return _call(
        order, body, (dx2b, wout, gates, ya, yb), name="gate_bwd", grid=(S // tm,),
        in_specs=[row(D), pl.BlockSpec((D, D), lambda i: (0, 0)), row(2 * D), row(D), row(D)],
        out_specs=[row(D), row(D), pl.BlockSpec((tm, 2 * D), lambda i: (i, F_G // (2 * D)))],
        out_shape=[jax.ShapeDtypeStruct((S, D), bf16), jax.ShapeDtypeStruct((S, D), bf16),
                   jax.ShapeDtypeStruct((S, NP), bf16)],
        compiler_params=_params(("parallel",)),
    )


def _branch_bwd(order, dya, dyb, was, wbs):
    tm = 512

    def body(dya_ref, dyb_ref, wa_ref, wb_ref, doa_ref, dob_ref):
        doa = jnp.zeros((tm, FOXW), f32)
        dob = jnp.zeros((tm, DILOUT), f32)
        for q in range(NCHIP):
            cols = slice(q * 256, (q + 1) * 256)
            doa = doa + _dot_nt(dya_ref[:, cols], wa_ref[q])
            dob = dob + _dot_nt(dyb_ref[:, cols], wb_ref[q])
        doa_ref[...] = doa.astype(bf16)
        dob_ref[...] = dob

    row = lambda w: pl.BlockSpec((tm, w), lambda i: (i, 0))
    full3 = lambda a: pl.BlockSpec(a.shape, lambda i: (0, 0, 0))
    return _call(
        order, body, (dya, dyb, was, wbs), name="branch_bwd", grid=(S // tm,),
        in_specs=[row(D), row(D), full3(was), full3(wbs)],
        out_specs=[row(FOXW), row(DILOUT)],
        out_shape=[jax.ShapeDtypeStruct((S, FOXW), bf16), jax.ShapeDtypeStruct((S, DILOUT), f32)],
        compiler_params=_params(("parallel",)),
    )


def _branch_wgrad(order, oa, ob, dya, dyb):
    def body(oa_ref, ob_ref, dya_ref, dyb_ref, dwa_ref, dwb_ref):
        dwa_ref[...] = _dot_tn(oa_ref[...], dya_ref[...])
        dwb_ref[...] = _dot_tn(ob_ref[...], dyb_ref[...])

    full = lambda w: pl.BlockSpec((S, w), lambda q: (0, 0))
    colq = pl.BlockSpec((S, 256), lambda q: (0, q))
    return _call(
        order, body, (oa, ob, dya, dyb), name="branch_wgrad", grid=(NCHIP,),
        in_specs=[full(FOXW), full(DILOUT), colq, colq],
        out_specs=[pl.BlockSpec((None, FOXW, 256), lambda q: (q, 0, 0)),
                   pl.BlockSpec((None, DILOUT, 256), lambda q: (q, 0, 0))],
        out_shape=[jax.ShapeDtypeStruct((NCHIP, FOXW, 256), f32), jax.ShapeDtypeStruct((NCHIP, DILOUT, 256), f32)],
        compiler_params=_params(("parallel",)),
    )


def _fox_bwd(order, qkva, doa, oa, lse, F, dproj):
    tq, tk = FOX_TQ, FOX_TK

    def body(qkv_ref, do_ref, o_ref, lse_ref, F_ref, _dproj_in, dF_ref, dqkv_ref, qa, ka, da, va, kat,
             dk_scr, dv_scr, dqt_scr):
        p = pl.program_id(0)
        lane, hm = _head_masks()
        keyi = lax.broadcasted_iota(jnp.int32, (tk, 1), 0)
        qryi = lax.broadcasted_iota(jnp.int32, (1, tq), 1)

        def prep(i, c):
            rows = pl.ds(pl.multiple_of(i * tk, tk), tk)
            _fox_operands(qkv_ref, F_ref, lse_ref, qa, ka, p, rows)
            do = do_ref[rows, :].astype(f32)
            prod = do * o_ref[rows, :].astype(f32)
            v = qkv_ref[rows, 256:384].astype(f32)
            for hh in (0, 1):
                free = (1 - hh) * HD
                delta = jnp.sum(jnp.where(hm[hh], prod, 0.0), axis=1, keepdims=True)
                da[hh, rows, :] = _set_lanes(jnp.where(hm[hh], do, 0.0), lane, free,
                                             [-t for t in _f32_parts(delta)]).astype(bf16)
                va[hh, rows, :] = _set_lanes(v, lane, free, [1.0] * 3).astype(bf16)
                kat[hh, i] = ka[hh, rows, :].astype(f32).T.astype(bf16)
                dk_scr[hh, rows, :] = jnp.zeros((tk, 128), f32)
                dv_scr[hh, rows, :] = jnp.zeros((tk, 128), f32)
            return c

        lax.fori_loop(0, S // tk, prep, 0)

        def qblock(i, c):
            r0 = pl.multiple_of(i * tq, tq)
            qrows = pl.ds(r0, tq)
            qh = [qa[hh, qrows, :] for hh in (0, 1)]
            dh = [da[hh, qrows, :] for hh in (0, 1)]
            dqt_scr[...] = jnp.zeros_like(dqt_scr)

            def kv(jb, c2, masked):
                keys = pl.ds(pl.multiple_of(jb * tk, tk), tk)
                sts = [_dot_nt(ka[hh, keys, :], qh[hh]) for hh in (0, 1)]
                dps = [_dot_nt(va[hh, keys, :], dh[hh]) for hh in (0, 1)]
                for hh in (0, 1):
                    pt = jnp.exp(sts[hh])
                    if masked:
                        pt = jnp.where(jb * tk + keyi <= r0 + qryi, pt, 0.0)
                    dsb = (pt * dps[hh]).astype(bf16)
                    dv_scr[hh, keys, :] += _dot(pt.astype(bf16), dh[hh])
                    dk_scr[hh, keys, :] += _dot(dsb, qh[hh])
                    dqt_scr[hh] += _dot(kat[hh, jb], dsb)
                return c2

            last = (r0 + tq - 1) // tk
            lax.fori_loop(0, last, lambda j, c2: kv(j, c2, False), 0)
            kv(last, 0, True)
            dq0, dq1 = dqt_scr[0].T, dqt_scr[1].T
            dqkv_ref[qrows, 0:128] = (jnp.where(hm[0], dq0, dq1) * 0.125).astype(bf16)
            dF_ref[qrows, :] = jnp.where(lane == 0, dq0[:, HD:HD + 1], jnp.where(lane == 1, dq1[:, 0:1], 0.0))
            return c

        lax.fori_loop(0, S // tq, qblock, 0)

        def finish(i, c):
            rows = pl.ds(pl.multiple_of(i * tq, tq), tq)
            dk0, dk1 = dk_scr[0, rows, :], dk_scr[1, rows, :]
            dqkv_ref[rows, 128:256] = jnp.where(hm[0], dk0, dk1).astype(bf16)
            dqkv_ref[rows, 256:384] = jnp.where(hm[0], dv_scr[0, rows, :], dv_scr[1, rows, :]).astype(bf16)
            cs = jnp.where(lane == 0, dk0[:, HD + L_ONE:HD + L_ONE + 1],
                           jnp.where(lane == 1, dk1[:, L_ONE:L_ONE + 1], 0.0))
            dF_ref[rows, :] = dF_ref[rows, :] - cs
            return c

        lax.fori_loop(0, S // tq, finish, 0)

    pair = pl.BlockSpec((S, 128), lambda p: (0, p))
    return _call(
        order, body, (qkva, doa, oa, lse, F, dproj), name="fox_bwd", grid=(4,),
        in_specs=[pl.BlockSpec((S, FOX_BLK), lambda p: (0, p)), pair, pair, pair,
                  pl.BlockSpec((S, 128), lambda p: (0, 0)), pl.BlockSpec(memory_space=pl.ANY)],
        out_specs=[pair, pl.BlockSpec((S, FOX_BLK), lambda p: (0, F_FOX // FOX_BLK + p))],
        out_shape=[jax.ShapeDtypeStruct((S, FOXW), f32), jax.ShapeDtypeStruct((S, NP), bf16)],
        input_output_aliases={5: 1},
        scratch_shapes=[pltpu.VMEM((2, S, 128), bf16)] * 4 + [pltpu.VMEM((2, S // tk, 128, tk), bf16)]
        + [pltpu.VMEM((2, S, 128), f32)] * 2 + [pltpu.VMEM((2, 128, tq), f32)],
        compiler_params=_params(("parallel",)),
    )


def _forget_bwd(order, dF, fa, bpad, dproj):
    nb = S // TQ

    def body(dF_ref, fa_ref, b_ref, _dproj_in, db_ref, dfa_ref):
        rr = lax.broadcasted_iota(jnp.int32, (TQ, TQ), 0)
        cc = lax.broadcasted_iota(jnp.int32, (TQ, TQ), 1)
        upper = (cc >= rr).astype(bf16)
        lane = lax.broadcasted_iota(jnp.int32, (1, 128), 1)
        carry = jnp.zeros((1, 128), f32)
        db = jnp.zeros((1, 128), f32)
        for b in reversed(range(nb)):
            cols = jnp.zeros((TQ, 128), f32)
            for h in range(8):
                c0 = (h // 2) * 128 + h % 2
                cols = jnp.where(lane == h, dF_ref[b * TQ:(b + 1) * TQ, c0:c0 + 1], cols)
            dlf = carry
            for part in _split3(cols):
                dlf = dlf + _dot(upper, part)
            carry = carry + jnp.sum(cols, axis=0, keepdims=True)
            z = fa_ref[b * TQ:(b + 1) * TQ, :] + b_ref[...]
            dz = jnp.where(lane < 8, dlf * jax.nn.sigmoid(-z), 0.0)
            dfa_ref[b * TQ:(b + 1) * TQ, 0:128] = dz.astype(bf16)
            dfa_ref[b * TQ:(b + 1) * TQ, 128:256] = jnp.zeros((TQ, 128), bf16)
            db = db + jnp.sum(dz, axis=0, keepdims=True)
        db_ref[...] = db

    whole = lambda a: pl.BlockSpec(a.shape, lambda i: (0,) * a.ndim)
    return _call(
        order, body, (dF, fa, bpad, dproj), name="forget_bwd", grid=(1,),
        in_specs=[whole(dF), whole(fa), whole(bpad), pl.BlockSpec(memory_space=pl.ANY)],
        out_specs=[pl.BlockSpec((1, 128), lambda i: (0, 0)), pl.BlockSpec((S, 256), lambda i: (0, F_FA // 256))],
        out_shape=[jax.ShapeDtypeStruct((1, 128), f32), jax.ShapeDtypeStruct((S, NP), bf16)],
        input_output_aliases={3: 1},
        compiler_params=_params(("arbitrary",)),
    )


def _dil_bwd(order, qkvb, dob, ob, lseb, rope, dproj):
    c_t, s1_t, s2_t = rope

    def body(*refs):
        q_refs, k_refs, v_refs = refs[0:3], refs[3:6], refs[6:9]
        dob_ref, ob_ref, lse_ref, c_ref, s1_ref, s2_ref, _dproj_in, dqkv_ref = refs[9:17]
        qp, kp, vp, dop, lp, dlp, dln, dqp, dkp, dvp, nat = refs[17:28]
        dq_out, dk_out, dv_out = _dil_views(dqkv_ref)
        _, hm = _head_masks()

        def delta_rows(i, c):
            r0 = pl.multiple_of(i * TQ, TQ)
            prod = dob_ref[pl.ds(r0, TQ), :] * ob_ref[pl.ds(r0, TQ), :].astype(f32)
            d0 = jnp.sum(jnp.where(hm[0], prod, 0.0), axis=1, keepdims=True)
            d1 = jnp.sum(jnp.where(hm[1], prod, 0.0), axis=1, keepdims=True)
            dln[pl.ds(r0, TQ), :] = jnp.where(hm[0], d0, d1)
            return c

        lax.fori_loop(0, S // TQ, delta_rows, 0)

        for g, r in enumerate(DIL):
            nbl = S // r // BAND
            if r == 1:
                srcs = (q_refs[g], k_refs[g], v_refs[g], dob_ref, lse_ref, dln)
            else:
                for dst, src in ((qp, q_refs[g]), (kp, k_refs[g]), (vp, v_refs[g]), (dop, dob_ref),
                                 (lp, lse_ref), (dlp, dln)):
                    _permute_in(dst, src, r)
                srcs = (qp, kp, vp, dop, lp, dlp)
            dkp[...] = jnp.zeros_like(dkp)
            dvp[...] = jnp.zeros_like(dvp)

            def blk(t, c, srcs=srcs, nbl=nbl):
                qs_, ks_, vs_, dos_, ls_, dls_ = srcs
                work = []
                for u in range(DIL_UNROLL):
                    r0, k0, valid = _band_geometry(DIL_UNROLL * t + u, nbl)
                    q = qs_[pl.ds(r0, BAND), :] * 0.125
                    kwf = ks_[pl.ds(k0, 2 * BAND), :]
                    kw = kwf.astype(bf16)
                    vw = vs_[pl.ds(k0, 2 * BAND), :].astype(bf16)
                    do = dos_[pl.ds(r0, BAND), :]
                    lse = ls_[pl.ds(r0, BAND), :]
                    dlt = dls_[pl.ds(r0, BAND), :]
                    for hh in (0, 1):
                        qh = jnp.where(hm[hh], q, 0.0).astype(bf16)
                        doh = jnp.where(hm[hh], do, 0.0).astype(bf16)
                        kh = jnp.where(hm[hh], kwf, 0.0).astype(bf16)
                        work.append((u, hh, r0, k0, valid, qh, doh, kh, lse[:, hh * HD:hh * HD + 1],
                                     dlt[:, hh * HD:hh * HD + 1], _dot_nt(qh, kw), _dot_nt(doh, vw)))
                for u, hh, r0, k0, valid, qh, doh, kh, lse_h, dlt_h, s, dp in work:
                    if hh == 0:
                        dq = jnp.zeros((BAND, 128), f32)
                        dk = jnp.zeros((2 * BAND, 128), f32)
                        dv = jnp.zeros((2 * BAND, 128), f32)
                    pr = jnp.where(valid, jnp.exp(s - lse_h), 0.0)
                    dsb = (pr * (dp - dlt_h)).astype(bf16)
                    dv = dv + _dot_tn(pr.astype(bf16), doh)
                    dk = dk + _dot_tn(dsb, qh)
                    dq = dq + _dot(dsb, kh)
                    if hh == 1:
                        dqp[pl.ds(r0, BAND), :] = dq * 0.125
                        dkp[pl.ds(k0, 2 * BAND), :] += dk
                        dvp[pl.ds(k0, 2 * BAND), :] += dv
                return c

            lax.fori_loop(0, S // BAND // DIL_UNROLL, blk, 0)

            for acc, out, roped in ((dqp, dq_out[g], True), (dkp, dk_out[g], True), (dvp, dv_out[g], False)):
                if r == 1:
                    src = acc
                else:
                    _permute_out(nat, acc, r)
                    src = nat

                def emit(i, c, src=src, out=out, roped=roped):
                    r0 = pl.multiple_of(i * TQ, TQ)
                    d = src[pl.ds(r0, TQ), :]
                    if roped:
                        d = (d * c_ref[pl.ds(r0, TQ), :] + pltpu.roll(d * s1_ref[pl.ds(r0, TQ), :], 8, 1)
                             + pltpu.roll(d * s2_ref[pl.ds(r0, TQ), :], 120, 1))
                    out[pl.ds(r0, TQ), :] = d.astype(bf16)
                    return c

                lax.fori_loop(0, S // TQ, emit, 0)

    pair = pl.BlockSpec((S, 128), lambda p: (0, p))
    tab = pl.BlockSpec((S, 128), lambda p: (0, 0))
    blk_spec = pl.BlockSpec((S, DIL_BLK), lambda p: (0, p))
    return _call(
        order, body, [qkvb] * 9 + [dob, ob, lseb, c_t, s1_t, s2_t, dproj], name="dil_bwd", grid=(2,),
        in_specs=_dil_in_specs() + [pair, pair, pair, tab, tab, tab, pl.BlockSpec(memory_space=pl.ANY)],
        out_specs=blk_spec,
        out_shape=jax.ShapeDtypeStruct((S, NP), bf16),
        input_output_aliases={15: 0},
        scratch_shapes=[pltpu.VMEM((S, 128), f32)] * 11,
        compiler_params=_params(("parallel",)),
    )


def _inproj_bwd(order, dproj, wt, x, dx2, g1):
    tm = 256

    def body(d_ref, w_ref, x_ref, dx2_ref, g_ref, dx_ref, dg_ref):
        i = pl.program_id(0)
        dh = _dot(d_ref[...], w_ref[...])
        xb = x_ref[...]
        r = lax.rsqrt(jnp.mean(xb * xb, axis=-1, keepdims=True) + EPS)
        xh = xb * r
        gdh = dh * g_ref[...]
        dx_ref[...] = dx2_ref[...] + r * (gdh - xh * jnp.mean(gdh * xh, axis=-1, keepdims=True))

        @pl.when(i == 0)
        def _():
            dg_ref[...] = jnp.zeros_like(dg_ref)

        dg_ref[...] += jnp.sum(dh * xh, axis=0, keepdims=True)

    row = pl.BlockSpec((tm, D), lambda i: (i, 0))
    vec = pl.BlockSpec((1, D), lambda i: (0, 0))
    return _call(
        order, body, (dproj, wt, x, dx2, g1), name="inproj_bwd", grid=(S // tm,),
        in_specs=[pl.BlockSpec((tm, NP), lambda i: (i, 0)), pl.BlockSpec((NP, D), lambda i: (0, 0)), row, row, vec],
        out_specs=[row, vec],
        out_shape=[jax.ShapeDtypeStruct((S, D), f32), jax.ShapeDtypeStruct((1, D), f32)],
        compiler_params=_params(("arbitrary",)),
    )


HBM = pl.BlockSpec(memory_space=pltpu.HBM)
SEM = pl.BlockSpec(memory_space=pltpu.SEMAPHORE)
SMALL_ROWS = 8


def _comm_call(name, body, bufs, order, sems_in=(), new_sems=()):
    nb, ns, nn = len(bufs), len(sems_in), len(new_sems)
    extra = order.token_for(bufs)

    def kern(*refs):
        off = nb + ns + len(extra)
        body(refs[:nb], refs[nb:nb + ns], refs[off:off + nn])
        refs[-1][...] = jnp.zeros((8, 128), f32)

    res = pl.pallas_call(
        kern, name=name,
        in_specs=[HBM] * nb + [SEM] * ns + [pl.BlockSpec(memory_space=pl.ANY)] * len(extra),
        out_specs=[SEM] * nn + [HBM] * nb + [pl.BlockSpec(memory_space=pltpu.VMEM)],
        out_shape=[pltpu.SemaphoreType.DMA((k,)) for k in new_sems] + [pltpu.HBM(b.shape, b.dtype) for b in bufs]
        + [jax.ShapeDtypeStruct((8, 128), f32)],
        input_output_aliases={i: nn + i for i in range(nb)},
        compiler_params=pltpu.CompilerParams(has_side_effects=pltpu.SideEffectType.DATAFLOW_SIDE_EFFECTING),
    )(*[pltpu.with_memory_space_constraint(b, pltpu.HBM) for b in bufs], *sems_in, *extra)
    order.mark(res[-1])
    return list(res[:nn]), list(res[nn:nn + nb])


def _place():
    x, y, c = lax.axis_index("x"), lax.axis_index("y"), lax.axis_index("c")
    chips = [(1 - x, y), (x, 1 - y), (1 - x, 1 - y)]
    return x, y, c, chips


def _rcopy(src, dst, ssem, rsem, dev):
    return pltpu.make_async_remote_copy(src_ref=src, dst_ref=dst, send_sem=ssem, recv_sem=rsem,
                                        device_id=dev, device_id_type=pl.DeviceIdType.MESH)


def _half(nrows, which):
    return pl.ds(which * (nrows // 2), nrows // 2)


def _ici_copies(stack, group_sizes, ssems, rsems):
    x, y, c, chips = _place()
    me_q = 2 * x + y
    sends, recvs = [], []
    a = 0
    for grp, size in enumerate(group_sizes):
        for k in range(size):
            rows = _half(stack[a].shape[1], c)
            for j, (cx, cy) in enumerate(chips):
                mine = stack[a].at[me_q, rows]
                sends.append(_rcopy(mine, mine, ssems[grp].at[k * 3 + j], rsems[grp].at[k * 3 + j], (cx, cy, c)))
                theirs = stack[a].at[2 * cx + cy, rows]
                recvs.append(_rcopy(theirs, theirs, ssems[grp].at[k * 3 + j], rsems[grp].at[k * 3 + j],
                                    (cx, cy, c)))
            a += 1
    return sends, recvs


def _allgather_start(name, stacks, order):
    n = len(stacks)

    def body(bufs, _, new):
        sends, _r = _ici_copies(bufs, [n], [new[0]], [new[1]])
        for cp in sends:
            cp.start()

    return _comm_call(name, body, stacks, order, new_sems=(3 * n, 3 * n))


def _forward_copies(stack, ssem, rsem):
    x, y, c, chips = _place()
    sib = (x, y, 1 - c)
    sends, recvs = [], []
    for a in range(len(stack)):
        for j, (cx, cy) in enumerate(chips):
            landed = stack[a].at[2 * cx + cy, _half(stack[a].shape[1], c)]
            sends.append(_rcopy(landed, landed, ssem.at[a * 3 + j], rsem.at[a * 3 + j], sib))
            other = stack[a].at[2 * cx + cy, _half(stack[a].shape[1], 1 - c)]
            recvs.append(_rcopy(other, other, ssem.at[a * 3 + j], rsem.at[a * 3 + j], sib))
    return sends, recvs


def _allgather_forward(name, stacks, sems, order):
    n = len(stacks)

    def body(bufs, taken, new):
        sends, recvs = _ici_copies(bufs, [n], [taken[0]], [taken[1]])
        for cp in sends:
            cp.wait_send()
        for cp in recvs:
            cp.wait_recv()
        fwd, _r = _forward_copies(bufs, new[0], new[1])
        for cp in fwd:
            cp.start()

    return _comm_call(name, body, stacks, order, sems_in=sems, new_sems=(3 * n, 3 * n))


def _allgather_finish(name, stacks, sems, order):
    def body(bufs, taken, _):
        sends, recvs = _forward_copies(bufs, taken[0], taken[1])
        for cp in sends:
            cp.wait_send()
        for cp in recvs:
            cp.wait_recv()

    return _comm_call(name, body, stacks, order, sems_in=sems)[1]


def _window_unit(q, j):
    return C2I[WIN_UNIT0[q] + j]


def _pair_copies(g, t, ssem, rsem, gathered):
    x, y, c, _ = _place()
    sib = (x, y, 1 - c)
    cps, whole = [], []
    for a in range(len(g)):
        if a == 0 and gathered:
            for q in range(NCHIP):
                for j in range(WIN_UNITS // 2):
                    u = jnp.where(c == 0, _window_unit(q, WIN_UNITS // 2 + j), _window_unit(q, j))
                    src = g[0].at[pl.ds(pl.multiple_of(u * UNIT, UNIT), UNIT), :]
                    cps.append(_rcopy(src, t[0].at[q, pl.ds(j * UNIT, UNIT), :], ssem.at[0], rsem.at[0], sib))
            whole.append(_rcopy(t[0], t[0], ssem.at[0], rsem.at[0], sib))
        else:
            cp = _rcopy(g[a].at[:, _half(g[a].shape[1], 1 - c), :], t[a], ssem.at[a], rsem.at[a], sib)
            cps.append(cp)
            whole.append(cp)
    return cps, whole


def _comm_multi(name, parts, order):
    def body(buf_refs, taken, new):
        ib = it = inew = 0
        for pbody, pbufs, psems, pnew, _ in parts:
            pbody(buf_refs[ib:ib + len(pbufs)], taken[it:it + len(psems)], new[inew:inew + len(pnew)])
            ib, it, inew = ib + len(pbufs), it + len(psems), inew + len(pnew)

    sems, bufs = _comm_call(name, body, [b for p in parts for b in p[1]], order,
                            sems_in=[s for p in parts for s in p[2]], new_sems=[k for p in parts for k in p[3]])
    out, ib, inew = [], 0, 0
    for _, pbufs, _, pnew, unpack in parts:
        out.append(unpack(sems[inew:inew + len(pnew)], bufs[ib:ib + len(pbufs)]))
        ib, inew = ib + len(pbufs), inew + len(pnew)
    return out


def _pair_start_part(gs, gathered=False):
    n = len(gs)
    ts = [lax.empty((NCHIP, WIN_ROWS // 2, D) if (a == 0 and gathered) else (NCHIP, g.shape[1] // 2, g.shape[2]), f32)
          for a, g in enumerate(gs)]

    def body(bufs, _, new):
        for cp in _pair_copies(bufs[:n], bufs[n:], new[0], new[1], gathered)[0]:
            cp.start()

    return body, list(gs) + ts, (), (n, n), lambda sems, bufs: (sems, bufs)


def _pair_wait_part(bufs, sems, gathered=False):
    n = len(bufs) // 2

    def body(refs, taken, _):
        for cp in _pair_copies(refs[:n], refs[n:], taken[0], taken[1], gathered)[1]:
            cp.wait_send()
            cp.wait_recv()

    return body, list(bufs), list(sems), (), lambda _, out: (out[:n], out[n:])


def _row_tile(h):
    return min(h, 256)


def _pair_add(order, g, t, c_arr, name):
    _, R, C = g.shape
    h = R // 2
    tr = _row_tile(h)
    nblk = h // tr

    def body(c_ref, g_ref, t_ref, p32_ref, p16_ref):
        s = g_ref[...] + t_ref[...]
        p32_ref[...] = s
        p16_ref[...] = s.astype(bf16)

    blk = pl.BlockSpec((None, tr, C), lambda q, i, c_ref: (q, i, 0))
    return _call_indexed(
        order, body, (c_arr,), (g, t), (NCHIP, nblk),
        [pl.BlockSpec((None, tr, C), lambda q, i, c_ref: (q, c_ref[0] * nblk + i, 0)), blk], [blk, blk],
        name=name,
        out_shape=[jax.ShapeDtypeStruct((NCHIP, h, C), f32), jax.ShapeDtypeStruct((NCHIP, h, C), bf16)],
        compiler_params=_params(("parallel", "parallel")),
    )


def _pair_add_gathered(order, dwt, t, c_arr, name):
    half_units, half_rows = WIN_UNITS // 2, WIN_ROWS // 2
    table = jnp.asarray([_window_unit(q, j) for q in range(NCHIP) for j in range(WIN_UNITS)], jnp.int32)

    def body(tab_ref, c_ref, g_hbm, t_ref, p32_ref, p16_ref, buf, sem):
        q = pl.program_id(0)

        def gather(w, slot):
            cps = []
            for j in range(half_units):
                u = tab_ref[w * WIN_UNITS + c_ref[0] * half_units + j]
                cps.append(pltpu.make_async_copy(g_hbm.at[pl.ds(pl.multiple_of(u * UNIT, UNIT), UNIT), :],
                                                 buf.at[slot, pl.ds(j * UNIT, UNIT), :], sem.at[slot]))
            return cps

        @pl.when(q == 0)
        def _():
            for cp in gather(0, 0):
                cp.start()

        @pl.when(q + 1 < NCHIP)
        def _():
            for cp in gather(q + 1, (q + 1) % 2):
                cp.start()

        slot = q % 2
        pltpu.make_async_copy(buf.at[slot], buf.at[slot], sem.at[slot]).wait()
        s = buf[slot] + t_ref[...]
        p32_ref[...] = s
        p16_ref[...] = s.astype(bf16)

    blk = pl.BlockSpec((None, half_rows, D), lambda q, tab_ref, c_ref: (q, 0, 0))
    return _call_indexed(
        order, body, (table, c_arr), (dwt, t), (NCHIP,),
        [pl.BlockSpec(memory_space=pl.ANY), blk], [blk, blk],
        scratch_shapes=[pltpu.VMEM((2, half_rows, D), f32), pltpu.SemaphoreType.DMA((2,))],
        name=name,
        out_shape=[jax.ShapeDtypeStruct((NCHIP, half_rows, D), f32),
                   jax.ShapeDtypeStruct((NCHIP, half_rows, D), bf16)],
        compiler_params=_params(("arbitrary",)),
    )


def _shard_copies(p, r, sm, ssem, rsem):
    x, y, c, chips = _place()
    n = len(p)
    sends, recvs = [], []
    for a in range(n):
        for j, (cx, cy) in enumerate(chips):
            k = a * 3 + j
            sends.append(_rcopy(p[a].at[2 * cx + cy], r[a].at[j], ssem.at[k], rsem.at[k], (cx, cy, c)))
            recvs.append(_rcopy(r[a].at[j], r[a].at[j], ssem.at[k], rsem.at[k], (cx, cy, c)))
    if sm is not None:
        mine = sm.at[4 * x + 2 * y + c]
        for i in range(1, 8):
            px = (1 - x) if i & 4 else x
            py = (1 - y) if i & 2 else y
            pc = (1 - c) if i & 1 else c
            k = 3 * n + i - 1
            sends.append(_rcopy(mine, mine, ssem.at[k], rsem.at[k], (px, py, pc)))
            slot = sm.at[4 * px + 2 * py + pc]
            recvs.append(_rcopy(slot, slot, ssem.at[k], rsem.at[k], (px, py, pc)))
    return sends, recvs


def _shard_start_part(p16s, sm=None):
    n = len(p16s)
    rs = [lax.empty((3,) + p.shape[1:], bf16) for p in p16s]
    extra = [] if sm is None else [sm]
    nsem = 3 * n + (7 if sm is not None else 0)

    def body(bufs, _, new):
        sends, _r = _shard_copies(bufs[:n], bufs[n:2 * n], bufs[2 * n] if extra else None, new[0], new[1])
        for cp in sends:
            cp.start()

    return body, list(p16s) + rs + extra, (), (nsem, nsem), lambda sems, bufs: (sems, bufs)


def _shard_wait_part(bufs, sems, n):
    has_sm = len(bufs) > 2 * n

    def body(refs, taken, _):
        sends, recvs = _shard_copies(refs[:n], refs[n:2 * n], refs[2 * n] if has_sm else None, taken[0], taken[1])
        for cp in sends:
            cp.wait_send()
        for cp in recvs:
            cp.wait_recv()

    return body, list(bufs), list(sems), (), lambda _, out: (out[n:2 * n], (out[2 * n] if has_sm else None))


def _shard_sum(order, p32, r, q_arr, c_arr, name):
    _, h, C = p32.shape
    tr = _row_tile(h)
    nblk = h // tr

    def body(q_ref, c_ref, p_ref, r_ref, o_ref):
        s = p_ref[...]
        for j in range(3):
            s = s + r_ref[j].astype(f32)
        o_ref[...] = s

    return _call_indexed(
        order, body, (q_arr, c_arr), (p32, r), (nblk,),
        [pl.BlockSpec((None, tr, C), lambda i, q_ref, c_ref: (q_ref[0], i, 0)),
         pl.BlockSpec((3, tr, C), lambda i, q_ref, c_ref: (0, i, 0))],
        pl.BlockSpec((tr, C), lambda i, q_ref, c_ref: (c_ref[0] * nblk + i, 0)),
        name=name, out_shape=jax.ShapeDtypeStruct((2 * h, C), f32),
        compiler_params=_params(("parallel",)),
    )


def _swap_copies(full, ssem, rsem):
    x, y, c, _ = _place()
    sends, recvs = [], []
    for a in range(len(full)):
        mine = full[a].at[_half(full[a].shape[0], c)]
        sends.append(_rcopy(mine, mine, ssem.at[a], rsem.at[a], (x, y, 1 - c)))
        other = full[a].at[_half(full[a].shape[0], 1 - c)]
        recvs.append(_rcopy(other, other, ssem.at[a], rsem.at[a], (x, y, 1 - c)))
    return sends, recvs


def _swap_start_part(fulls):
    n = len(fulls)

    def body(bufs, _, new):
        for cp in _swap_copies(bufs, new[0], new[1])[0]:
            cp.start()

    return body, list(fulls), (), (n, n), lambda sems, bufs: (sems, bufs)


def _swap_wait_part(fulls, sems):
    def body(refs, taken, _):
        sends, recvs = _swap_copies(refs, taken[0], taken[1])
        for cp in sends:
            cp.wait_send()
        for cp in recvs:
            cp.wait_recv()

    return body, list(fulls), list(sems), (), lambda _, out: out


def _small_sum(order, sm):
    def body(sm_ref, o_ref):
        s = sm_ref[0]
        for d in range(1, 8):
            s = s + sm_ref[d]
        o_ref[...] = s

    return _call(order, body, (sm,), name="small_grad_sum", out_shape=jax.ShapeDtypeStruct((SMALL_ROWS, D), f32))


def _adamw_math(w, g, m, v):
    m = ADAM_B1 * m + (1.0 - ADAM_B1) * g
    v = ADAM_B2 * v + (1.0 - ADAM_B2) * (g * g)
    m_hat = m / (1.0 - ADAM_B1 ** ADAM_STEP)
    v_hat = v / (1.0 - ADAM_B2 ** ADAM_STEP)
    return -ADAM_LR * (m_hat / (jnp.sqrt(v_hat) + ADAM_EPS) + ADAM_WD * w), m, v


def _adamw_small(order, ws, gs, ms, vs, name):
    n = len(ws)

    def body(*refs):
        for i in range(n):
            res = _adamw_math(*[refs[k * n + i][...] for k in range(4)])
            for k in range(3):
                refs[4 * n + 3 * i + k][...] = res[k]

    out = _call(order, body, list(ws) + list(gs) + list(ms) + list(vs), name=name,
                out_shape=[jax.ShapeDtypeStruct(w.shape, f32) for w in ws for _ in range(3)])
    return [out[3 * i:3 * i + 3] for i in range(n)]


def _adamw(order, w, g, m, v, name):
    R, C = w.shape
    if R <= 256 or R % 256 == 0:
        tr, tc = min(R, 256), C
    else:
        tr, tc = R, 128

    def body(w_ref, g_ref, m_ref, v_ref, d_ref, nm_ref, nv_ref):
        d_ref[...], nm_ref[...], nv_ref[...] = _adamw_math(w_ref[...], g_ref[...], m_ref[...], v_ref[...])

    blk = pl.BlockSpec((tr, tc), lambda i, j: (i, j))
    return _call(
        order, body, (w, g, m, v), name=name, grid=(R // tr, C // tc), in_specs=[blk] * 4, out_specs=[blk] * 3,
        out_shape=[jax.ShapeDtypeStruct((R, C), f32)] * 3,
        compiler_params=_params(("parallel", "parallel")),
    )


def _feature_major(w):
    return jnp.transpose(w, (2, 0, 1)).reshape(SHARD_IN, D)


def _unfeature_major(a):
    return jnp.transpose(a.reshape(SHARD_IN, 1, D), (1, 2, 0))


def _window_of(wt, q):
    def plain(k):
        return lambda w: jnp.pad(w, ((OWN_ROW0[k], WIN_ROWS - OWN_ROW0[k] - SHARD_IN), (0, 0))).astype(bf16)

    def chip1(w):
        lo = jnp.pad(w[0:62], ((2, WIN_ROWS - 64), (0, 0)))
        hi = jnp.pad(w[70:SHARD_IN], ((64, WIN_ROWS - 64 - (SHARD_IN - 70)), (0, 0)))
        return (lo + hi).astype(bf16)

    win = lax.switch(q, [plain(0), chip1, plain(2), plain(3)], wt)
    fa = jnp.pad(wt[62:70], ((0, FA_ROWS - 8), (0, 0))).astype(bf16)
    return win, fa


def _own_rows(gwin, gfa, q):
    def plain(k):
        return lambda gw, gf: gw[OWN_ROW0[k]:OWN_ROW0[k] + SHARD_IN]

    def chip1(gw, gf):
        return (jnp.pad(gw[2:64], ((0, SHARD_IN - 62), (0, 0))) + jnp.pad(gf[0:8], ((62, SHARD_IN - 70), (0, 0)))
                + jnp.pad(gw[64:64 + SHARD_IN - 70], ((70, 0), (0, 0))))

    return lax.switch(q, [plain(0), chip1, plain(2), plain(3)], gwin, gfa)


def kernel(x, norm_attn_g, w_in, b_forget, w_branch_a, w_branch_b, w_out, norm_mlp_g, w_up, w_down, norm_final_g, loss_target, m_norm_attn_g, m_w_in, m_b_forget, m_w_branch_a, m_w_branch_b, m_w_out, m_norm_mlp_g, m_w_up, m_w_down, m_norm_final_g, v_norm_attn_g, v_w_in, v_b_forget, v_w_branch_a, v_w_branch_b, v_w_out, v_norm_mlp_g, v_w_up, v_w_down, v_norm_final_g):
    xi, yi, ci = lax.axis_index("x"), lax.axis_index("y"), lax.axis_index("c")
    q_me = 2 * xi + yi
    c_arr = jnp.reshape(ci, (1,)).astype(jnp.int32)
    q_arr = jnp.reshape(q_me, (1,)).astype(jnp.int32)
    x_, tgt = x[0], loss_target[0]

    names = ["w_branch_a", "w_branch_b", "w_out", "w_up", "w_down"]
    big = dict(zip(names, [w_branch_a[0], w_branch_b[0], w_out[0], w_up[0], w_down[0]]))
    ms = dict(zip(names, [m_w_branch_a[0], m_w_branch_b[0], m_w_out[0], m_w_up[0], m_w_down[0]]))
    vs = dict(zip(names, [v_w_branch_a[0], v_w_branch_b[0], v_w_out[0], v_w_up[0], v_w_down[0]]))
    grad, upd = {}, {}
    order = _Order()

    def run(fn, *args, **kw):
        return fn(order, *args, **kw)

    def own_slot(a):
        return lax.dynamic_update_slice(lax.empty((NCHIP,) + a.shape, a.dtype), a[None], (q_me, 0, 0))

    wt_own = _feature_major(w_in)
    win, fa_blk = _window_of(wt_own, q_me)
    sem_in, in_s = _allgather_start("allgather_start_in", [own_slot(win), own_slot(fa_blk)], order)
    sem_rest, rest = _allgather_start("allgather_start_rest", [own_slot(w.astype(bf16)) for w in big.values()], order)
    sem_f, in_s = _allgather_forward("allgather_forward_in", in_s, sem_in, order)
    wins, fas = _allgather_finish("allgather_finish_in", in_s, sem_f, order)
    wt = run(_assemble_win, wins, fas)

    rope = _rope_tables()
    bpad = jnp.pad(b_forget, ((0, 0), (0, 120)))
    h1, qkvb, qkva, gates, fa = run(_norm_inproj, x_, norm_attn_g, wt, rope)
    F = run(_forget_cumsum, fa, bpad)
    oa, lsea = run(_fox_fwd, qkva, F)
    sem_f, rest = _allgather_forward("allgather_forward_rest", rest, sem_rest, order)
    ob, lseb = run(_dil_fwd, qkvb)
    was, wbs, wouts, wups, wdowns = _allgather_finish("allgather_finish_rest", rest, sem_f, order)
    wout = wouts.reshape(D, D)
    wdown = wdowns.reshape(DFF, D)
    ya, yb, mixed = run(_branch_mix, oa, ob, was, wbs, gates)
    x2, h2 = run(_outproj_norm, mixed, wout, x_, norm_mlp_g)
    u, a = run(_mlp_up, h2, wups)
    dx3, dx3b, dg3, loss_part = run(_mlp_down_loss, a, wdown, x2, norm_final_g.reshape(1, D), tgt)

    def comm(name, *parts):
        return _comm_multi(name, list(parts), order)

    def pair_adds(group, gs, ts):
        return zip(*[run(_pair_add, gs[i], ts[i], c_arr, "pair_add_" + nm) for i, nm in enumerate(group)])

    def shard_sums(group, p32s, rs):
        return [run(_shard_sum, p32s[i], rs[i], q_arr, c_arr, "shard_sum_" + nm) for i, nm in enumerate(group)]

    def adamw_group(group, fulls):
        for nm, gfull in zip(group, fulls):
            grad[nm] = gfull
            upd[nm] = run(_adamw, big[nm], gfull, ms[nm], vs[nm], "adamw_" + nm)

    grp_a, grp_b, grp_c = ["w_down", "w_up"], ["w_out", "w_branch_a", "w_branch_b"], ["w_in", "w_in_fa"]
    du = run(_mlp_down_bwd, dx3b, wdown, u)
    dwdown = run(_mm, a, dx3b, "tn", f32, 1024, D, "wgrad_down")
    dwup = run(_mm, h2, du, "tn", f32, D, 1024, "wgrad_up", stack_cols=True)
    ((sem_pa, buf_pa),) = comm("pair_start_a", _pair_start_part([dwdown.reshape(NCHIP, DFF // NCHIP, D), dwup]))
    dx2, dx2b, dg2 = run(_mlp_up_bwd, du, wups, x2, dx3, norm_mlp_g)
    ((gs, ts),) = comm("pair_wait_a", _pair_wait_part(buf_pa, sem_pa))
    p32_a, p16_a = pair_adds(grp_a, gs, ts)
    ((sem_sa, buf_sa),) = comm("shard_start_a", _shard_start_part(p16_a))
    dya, dyb, dproj = run(_gate_bwd, dx2b, wout, gates, ya, yb)
    dwout = run(_mm, mixed, dx2b, "tn", f32, D, D, "wgrad_out")
    doa, dob = run(_branch_bwd, dya, dyb, was, wbs)
    dwas, dwbs = run(_branch_wgrad, oa, ob, dya, dyb)
    ((sem_pb, buf_pb),) = comm("pair_start_b", _pair_start_part([dwout.reshape(NCHIP, D // NCHIP, D), dwas, dwbs]))
    dF, dproj = run(_fox_bwd, qkva, doa, oa, lsea, F, dproj)
    (gs, ts), (rs_a, _) = comm("pair_wait_b_shard_wait_a", _pair_wait_part(buf_pb, sem_pb),
                               _shard_wait_part(buf_sa, sem_sa, len(grp_a)))
    p32_b, p16_b = pair_adds(grp_b, gs, ts)
    fulls_a = shard_sums(grp_a, p32_a, rs_a)
    (sem_wa, fulls_a), (sem_sb, buf_sb) = comm("swap_start_a_shard_start_b", _swap_start_part(fulls_a),
                                               _shard_start_part(p16_b))
    dbf, dproj = run(_forget_bwd, dF, fa, bpad, dproj)
    dproj = run(_dil_bwd, qkvb, dob, ob, lseb, rope, dproj)
    (rs_b, _), fulls_a = comm("shard_wait_b_swap_wait_a", _shard_wait_part(buf_sb, sem_sb, len(grp_b)),
                              _swap_wait_part(fulls_a, sem_wa))
    fulls_b = shard_sums(grp_b, p32_b, rs_b)
    ((sem_wb, fulls_b),) = comm("swap_start_b", _swap_start_part(fulls_b))
    dwt = run(_mm, dproj, h1, "tn", f32, 512, D, "wgrad_in")
    dwfa = jnp.broadcast_to(dwt[F_FA:F_FA + FA_ROWS][None], (NCHIP, FA_ROWS, D))
    (sem_pc, buf_pc), fulls_b = comm("pair_start_c_swap_wait_b", _pair_start_part([dwt, dwfa], gathered=True),
                                     _swap_wait_part(fulls_b, sem_wb))
    adamw_group(grp_b, fulls_b)
    (((dwt_c, dwfa_c), (t_in, t_fa)),) = comm("pair_wait_c", _pair_wait_part(buf_pc, sem_pc, gathered=True))
    p32_in, p16_in = run(_pair_add_gathered, dwt_c, t_in, c_arr, "pair_add_w_in")
    p32_fa, p16_fa = run(_pair_add, dwfa_c, t_fa, c_arr, "pair_add_w_in_fa")
    ((sem_sc, buf_sc),) = comm("shard_start_c", _shard_start_part([p16_in, p16_fa]))
    gx, dg1 = run(_inproj_bwd, dproj, wt, x_, dx2, norm_attn_g)
    adamw_group(grp_a, fulls_a)
    small = jnp.concatenate([dg1, dg2, dg3, jnp.pad(dbf[:, 0:8], ((0, 0), (0, D - 8))),
                             jnp.pad(loss_part, ((0, 0), (0, D - 128))),
                             jnp.zeros((SMALL_ROWS - 5, D), f32)], axis=0)
    sm = lax.dynamic_update_slice(lax.empty((8, SMALL_ROWS, D), f32), small[None],
                                  (4 * xi + 2 * yi + ci, 0, 0))
    (sem_sm, buf_sm), (rs_c, _) = comm("small_start_shard_wait_c", _shard_start_part([], sm),
                                       _shard_wait_part(buf_sc, sem_sc, len(grp_c)))
    fulls_c = shard_sums(grp_c, [p32_in, p32_fa], rs_c)
    (sem_wc, fulls_c), (_, sm) = comm("swap_start_c_small_wait", _swap_start_part(fulls_c),
                                      _shard_wait_part(buf_sm, sem_sm, 0))
    gsmall = run(_small_sum, sm)
    loss = gsmall[4, 0]

    grad["norm_attn_g"], grad["norm_mlp_g"] = gsmall[0:1], gsmall[1:2]
    grad["norm_final_g"], grad["b_forget"] = gsmall[2:3], gsmall[3:4, 0:8]
    smalls = ["norm_attn_g", "norm_mlp_g", "norm_final_g", "b_forget"]
    res = run(_adamw_small, [norm_attn_g, norm_mlp_g, norm_final_g.reshape(1, D), b_forget],
              [grad[nm] for nm in smalls],
              [m_norm_attn_g, m_norm_mlp_g, m_norm_final_g.reshape(1, D), m_b_forget],
              [v_norm_attn_g, v_norm_mlp_g, v_norm_final_g.reshape(1, D), v_b_forget], "adamw_small")
    upd.update(zip(smalls, res))

    ((gwin, gfa),) = comm("swap_wait_c", _swap_wait_part(fulls_c, sem_wc))
    g_in = _own_rows(gwin, gfa, q_me)
    upd_in = run(_adamw, wt_own, g_in, _feature_major(m_w_in), _feature_major(v_w_in), "adamw_w_in")
    grad["w_in"] = _unfeature_major(g_in)
    upd["w_in"] = [_unfeature_major(t) for t in upd_in]

    order_out = ["norm_attn_g", "w_in", "b_forget", "w_branch_a", "w_branch_b", "w_out", "norm_mlp_g", "w_up",
                 "w_down", "norm_final_g"]
    shapes = dict(norm_attn_g=norm_attn_g.shape, w_in=w_in.shape, b_forget=b_forget.shape,
                  w_branch_a=w_branch_a.shape, w_branch_b=w_branch_b.shape, w_out=w_out.shape,
                  norm_mlp_g=norm_mlp_g.shape, w_up=w_up.shape, w_down=w_down.shape, norm_final_g=norm_final_g.shape)
    outs = [loss, gx.reshape(x.shape)]
    outs += [grad[nm].reshape(shapes[nm]) for nm in order_out]
    for k in range(3):
        outs += [upd[nm][k].reshape(shapes[nm]) for nm in order_out]
    return tuple(outs)
```

```python
import jax
import jax.numpy as jnp
from jax import lax
from jax.experimental import pallas as pl
from jax.experimental.pallas import tpu as pltpu

f32 = jnp.float32
bf16 = jnp.bfloat16

S = 2048
D = 1024
DFF = 4096
HD = 64
FOXW = 512
DILOUT = 256
DIL = (1, 4, 16)
BAND = 128
EPS = 1e-6
NEG = -1e30
ROPE_THETA = 500000.0
NCHIP = 4
TQ = 256

ADAM_LR, ADAM_B1, ADAM_B2, ADAM_EPS, ADAM_WD, ADAM_STEP = 0.001, 0.9, 0.999, 1e-08, 0.01, 10
VMEM_LIMIT = 56 * 1024 * 1024

UNIT = 64
NP = 6144
F_DIL, F_FOX, F_FA, F_G = 0, 2304, 3840, 4096
DIL_BLK, FOX_BLK = 1152, 384
WIN_UNITS, WIN_ROWS = 24, 1536
WIN_UNIT0 = (0, 23, 45, 68)
OWN_ROW0 = (0, 2, 60, 62)
SHARD_IN = 1474
FA_ROWS = 32


def _compact_to_internal():
    c2i = {}
    for p in range(2):
        for role in range(3):
            for g in range(3):
                for hh in range(2):
                    c2i[24 + 12 * role + 4 * g + 2 * p + hh] = 18 * p + 6 * role + 2 * g + hh
    for p in range(4):
        for role in range(3):
            for hh in range(2):
                c2i[8 * role + 2 * p + hh] = F_FOX // UNIT + 6 * p + 2 * role + hh
    for j in range(32):
        c2i[60 + j] = F_G // UNIT + j
    return c2i


C2I = _compact_to_internal()
OVERLAP_UNITS = (23, 45, 46, 68)


def _params(sem=None):
    return pltpu.CompilerParams(dimension_semantics=sem, vmem_limit_bytes=VMEM_LIMIT)


class _Order:
    def __init__(self):
        self.tok = None

    def mark(self, v):
        self.tok = v

    def token_for(self, args):
        return [] if self.tok is None or any(self.tok is a for a in args) else [self.tok]


def _call(order, body, args, in_specs=None, **kw):
    args = list(args)
    n_in = len(args)
    if in_specs is None:
        in_specs = [pl.BlockSpec(memory_space=pltpu.VMEM)] * n_in
    kern = body
    extra = order.token_for(args)
    if extra:
        in_specs = list(in_specs) + [pl.BlockSpec(memory_space=pl.ANY)]

        def kern(*refs):
            body(*refs[:n_in], *refs[n_in + 1:])

    out = pl.pallas_call(kern, in_specs=in_specs, **kw)(*args, *extra)
    order.mark(out[0] if isinstance(out, (tuple, list)) else out)
    return out


def _call_indexed(order, body, scalars, args, grid, in_specs, out_specs, scratch_shapes=(), **kw):
    args, in_specs = list(args), list(in_specs)
    n_front = len(scalars) + len(args)
    kern = body
    extra = order.token_for(args)
    if extra:
        in_specs.append(pl.BlockSpec(memory_space=pl.ANY))

        def kern(*refs):
            body(*refs[:n_front], *refs[n_front + 1:])

    out = pl.pallas_call(
        kern, grid_spec=pltpu.PrefetchScalarGridSpec(num_scalar_prefetch=len(scalars), grid=grid, in_specs=in_specs,
                                                     out_specs=out_specs, scratch_shapes=scratch_shapes),
        **kw)(*scalars, *args, *extra)
    order.mark(out[0] if isinstance(out, (tuple, list)) else out)
    return out


def _dot(a, b):
    return jnp.dot(a, b, preferred_element_type=f32)


def _dot_nt(a, b):
    return lax.dot_general(a, b, (((1,), (1,)), ((), ())), preferred_element_type=f32)


def _dot_tn(a, b):
    return lax.dot_general(a, b, (((0,), (0,)), ((), ())), preferred_element_type=f32)


def _split3(x):
    hi = x.astype(bf16)
    r1 = x - hi.astype(f32)
    mid = r1.astype(bf16)
    lo = (r1 - mid.astype(f32)).astype(bf16)
    return hi, mid, lo


def _rope_tables():
    half = 8
    inv_freq = jnp.power(jnp.float32(ROPE_THETA), -jnp.arange(half, dtype=f32) * 2.0 / 16)
    ang = jnp.arange(S).astype(f32)[:, None] * inv_freq[None, :]
    cos, sin = jnp.cos(ang), jnp.sin(ang)
    one = jnp.ones((S, HD - 16), f32)
    zero = jnp.zeros((S, HD - 16), f32)
    z8 = jnp.zeros((S, 8), f32)
    c = jnp.concatenate([cos, cos, one], axis=1)
    s1 = jnp.concatenate([-sin, z8, zero], axis=1)
    s2 = jnp.concatenate([z8, sin, zero], axis=1)
    return tuple(jnp.concatenate([t, t], axis=1) for t in (c, s1, s2))


def _mm(order, a, b, mode, out_dtype, tm, tn, name, stack_cols=False):
    if mode == "nn":
        (M, K), (_, N) = a.shape, b.shape
        a_spec = pl.BlockSpec((tm, K), lambda i, j: (i, 0))
        b_spec = pl.BlockSpec((K, tn), lambda i, j: (0, j))
        dot = _dot
    elif mode == "nt":
        (M, K), (N, _) = a.shape, b.shape
        a_spec = pl.BlockSpec((tm, K), lambda i, j: (i, 0))
        b_spec = pl.BlockSpec((tn, K), lambda i, j: (j, 0))
        dot = _dot_nt
    else:
        (K, M), (_, N) = a.shape, b.shape
        a_spec = pl.BlockSpec((K, tm), lambda i, j: (0, i))
        b_spec = pl.BlockSpec((K, tn), lambda i, j: (0, j))
        dot = _dot_tn

    def body(a_ref, b_ref, o_ref):
        o_ref[...] = dot(a_ref[...], b_ref[...]).astype(out_dtype)

    if stack_cols:
        assert tm == M
        out_spec = pl.BlockSpec((None, tm, tn), lambda i, j: (j, 0, 0))
        out_shape = jax.ShapeDtypeStruct((N // tn, M, tn), out_dtype)
    else:
        out_spec = pl.BlockSpec((tm, tn), lambda i, j: (i, j))
        out_shape = jax.ShapeDtypeStruct((M, N), out_dtype)
    return _call(
        order, body, (a, b), name=name, grid=(M // tm, N // tn), in_specs=[a_spec, b_spec],
        out_specs=out_spec, out_shape=out_shape,
        compiler_params=_params(("parallel", "parallel")),
    )


def _assemble_win(order, wins, fas):
    def body(win_ref, fa_ref, o_ref):
        q = pl.program_id(0)

        @pl.when(q == 0)
        def _():
            o_ref[...] = jnp.zeros_like(o_ref)

        for k in range(NCHIP):
            @pl.when(q == k)
            def _(k=k):
                for j in range(WIN_UNITS):
                    cu = WIN_UNIT0[k] + j
                    dst = pl.ds(C2I[cu] * UNIT, UNIT)
                    if cu in OVERLAP_UNITS:
                        o_ref[dst, :] += win_ref[j * UNIT:(j + 1) * UNIT, :]
                    else:
                        o_ref[dst, :] = win_ref[j * UNIT:(j + 1) * UNIT, :]
                if k == 1:
                    o_ref[F_FA:F_FA + FA_ROWS, :] = fa_ref[...]

    return _call(
        order, body, (wins, fas), name="assemble_w_in", grid=(NCHIP,),
        in_specs=[pl.BlockSpec((None, WIN_ROWS, D), lambda q: (q, 0, 0)),
                  pl.BlockSpec((None, FA_ROWS, D), lambda q: (1, 0, 0))],
        out_specs=pl.BlockSpec((NP, D), lambda q: (0, 0)),
        out_shape=jax.ShapeDtypeStruct((NP, D), bf16),
        compiler_params=_params(("arbitrary",)),
    )


def _norm_inproj(order, x, g1, wt, rope):
    tm = 256
    c_t, s1_t, s2_t = rope

    def body(x_ref, g_ref, w_ref, c_ref, s1_ref, s2_ref, h_ref, qkvb_ref, qkva_ref, gates_ref, fa_ref):
        xb = x_ref[...]
        r = lax.rsqrt(jnp.mean(xb * xb, axis=-1, keepdims=True) + EPS)
        h = ((xb * r) * g_ref[...]).astype(bf16)
        h_ref[...] = h
        c, s1, s2 = c_ref[...], s1_ref[...], s2_ref[...]
        for p in range(2):
            pb = _dot_nt(h, w_ref[F_DIL + p * DIL_BLK:F_DIL + (p + 1) * DIL_BLK, :])
            for ch in range(DIL_BLK // 128):
                pc = pb[:, ch * 128:(ch + 1) * 128]
                if ch < 6:
                    pc = pc * c + pltpu.roll(pc, 120, 1) * s1 + pltpu.roll(pc, 8, 1) * s2
                qkvb_ref[:, p * DIL_BLK + ch * 128:p * DIL_BLK + (ch + 1) * 128] = pc
        qkva_ref[...] = _dot_nt(h, w_ref[F_FOX:F_FA, :]).astype(bf16)
        fa_ref[...] = _dot_nt(h, w_ref[F_FA:F_FA + 128, :])
        gates_ref[...] = _dot_nt(h, w_ref[F_G:NP, :]).astype(bf16)

    row = lambda w: pl.BlockSpec((tm, w), lambda i: (i, 0))
    return _call(
        order, body, (x, g1, wt, c_t, s1_t, s2_t), name="norm_inproj", grid=(S // tm,),
        in_specs=[row(D), pl.BlockSpec((1, D), lambda i: (0, 0)), pl.BlockSpec((NP, D), lambda i: (0, 0)),
                  row(128), row(128), row(128)],
        out_specs=[row(D), row(2 * DIL_BLK), row(4 * FOX_BLK), row(2 * D), row(128)],
        out_shape=[jax.ShapeDtypeStruct((S, D), bf16), jax.ShapeDtypeStruct((S, 2 * DIL_BLK), f32),
                   jax.ShapeDtypeStruct((S, 4 * FOX_BLK), bf16), jax.ShapeDtypeStruct((S, 2 * D), bf16),
                   jax.ShapeDtypeStruct((S, 128), f32)],
        compiler_params=_params(("parallel",)),
    )


def _forget_cumsum(order, fa, bpad):
    nb = S // TQ

    def body(fa_ref, b_ref, F_ref):
        rr = lax.broadcasted_iota(jnp.int32, (TQ, TQ), 0)
        cc = lax.broadcasted_iota(jnp.int32, (TQ, TQ), 1)
        tri = (rr >= cc).astype(bf16)
        lane = lax.broadcasted_iota(jnp.int32, (1, 128), 1)
        carry = jnp.zeros((1, 128), f32)
        for b in range(nb):
            z = fa_ref[b * TQ:(b + 1) * TQ, :] + b_ref[...]
            lf = jnp.minimum(z, 0.0) - jnp.log(1.0 + jnp.exp(-jnp.abs(z)))
            lf = jnp.where(lane < 8, lf, 0.0)
            hi, mid, lo = _split3(lf)
            fb = (_dot(tri, hi) + _dot(tri, mid)) + _dot(tri, lo) + carry
            F_ref[b * TQ:(b + 1) * TQ, :] = fb
            carry = fb[TQ - 1:TQ, :]

    return _call(
        order, body, (fa, bpad), name="forget_cumsum",
        out_shape=jax.ShapeDtypeStruct((S, 128), f32),
        compiler_params=_params(),
    )


def _head_masks():
    lane = lax.broadcasted_iota(jnp.int32, (1, 128), 1)
    return lane, (lane < HD, lane >= HD)


L_FT, L_ONE = 0, 3
FOX_TQ, FOX_TK = 256, 512


def _set_lanes(x, lane, first, cols):
    for n, col in enumerate(cols):
        x = jnp.where(lane == first + n, col, x)
    return x


def _f32_parts(col):
    return [t.astype(f32) for t in _split3(col)]


def _fox_operands(qkv_ref, F_ref, lse_ref, qa, ka, p, rows):
    lane, hm = _head_masks()
    q = qkv_ref[rows, 0:128].astype(f32) * 0.125
    k = qkv_ref[rows, 128:256].astype(f32)
    Fb = F_ref[rows, :]
    for hh in (0, 1):
        free = (1 - hh) * HD
        fcol = jnp.sum(jnp.where(lane == 2 * p + hh, Fb, 0.0), axis=1, keepdims=True)
        qterm = fcol if lse_ref is None else fcol - lse_ref[rows, hh * HD:hh * HD + 1]
        qcols = _f32_parts(qterm) + [1.0] * 3
        kcols = [1.0] * 3 + [-t for t in _f32_parts(fcol)]
        qa[hh, rows, :] = _set_lanes(jnp.where(hm[hh], q, 0.0), lane, free, qcols).astype(bf16)
        ka[hh, rows, :] = _set_lanes(k, lane, free, kcols).astype(bf16)


def _fox_fwd(order, qkva, F):
    tq, tk = FOX_TQ, FOX_TK

    def body(qkv_ref, F_ref, o_ref, lse_ref, qa, ka, vt):
        p = pl.program_id(0)
        keyi = lax.broadcasted_iota(jnp.int32, (tk, 1), 0)
        qryi = lax.broadcasted_iota(jnp.int32, (1, tq), 1)
        sub = lax.broadcasted_iota(jnp.int32, (128, 1), 0)

        def prep(i, c):
            rows = pl.ds(pl.multiple_of(i * tk, tk), tk)
            _fox_operands(qkv_ref, F_ref, None, qa, ka, p, rows)
            vt[i] = qkv_ref[rows, 256:384].astype(f32).T.astype(bf16)
            return c

        lax.fori_loop(0, S // tk, prep, 0)

        def qblock(i, c):
            r0 = pl.multiple_of(i * tq, tq)
            qh = [qa[hh, pl.ds(r0, tq), :] for hh in (0, 1)]

            def kv(jb, carry, masked):
                keys = pl.ds(pl.multiple_of(jb * tk, tk), tk)
                sts = [_dot_nt(ka[hh, keys, :], qh[hh]) for hh in (0, 1)]
                new = []
                for hh in (0, 1):
                    m, l, a = carry[3 * hh:3 * hh + 3]
                    st = sts[hh]
                    if masked:
                        st = jnp.where(jb * tk + keyi <= r0 + qryi, st, NEG)
                    mn = jnp.maximum(m, jnp.max(st, axis=0, keepdims=True))
                    al = jnp.exp(m - mn)
                    pt = jnp.exp(st - mn)
                    l = al * l + jnp.sum(pt, axis=0, keepdims=True)
                    a = al * a + _dot(vt[jb, hh * HD:(hh + 1) * HD, :], pt.astype(bf16))
                    new += [mn, l, a]
                return tuple(new)

            init = (jnp.full((1, tq), NEG, f32), jnp.zeros((1, tq), f32), jnp.zeros((HD, tq), f32)) * 2
            last = (r0 + tq - 1) // tk
            carry = lax.fori_loop(0, last, lambda j, cr: kv(j, cr, False), init)
            m0, l0, a0, m1, l1, a1 = kv(last, carry, True)
            ot = jnp.concatenate([a0 / l0, a1 / l1], axis=0)
            lt = jnp.where(sub < HD, m0 + jnp.log(l0), m1 + jnp.log(l1))
            o_ref[pl.ds(r0, tq), :] = ot.T.astype(bf16)
            lse_ref[pl.ds(r0, tq), :] = lt.T
            return c

        lax.fori_loop(0, S // tq, qblock, 0)

    pair = pl.BlockSpec((S, 128), lambda p: (0, p))
    return _call(
        order, body, (qkva, F), name="fox_fwd", grid=(4,),
        in_specs=[pl.BlockSpec((S, FOX_BLK), lambda p: (0, p)), pl.BlockSpec((S, 128), lambda p: (0, 0))],
        out_specs=[pair, pair],
        out_shape=[jax.ShapeDtypeStruct((S, FOXW), bf16), jax.ShapeDtypeStruct((S, FOXW), f32)],
        scratch_shapes=[pltpu.VMEM((2, S, 128), bf16)] * 2 + [pltpu.VMEM((S // tk, 128, tk), bf16)],
        compiler_params=_params(("parallel",)),
    )


def _permute_in(dst, src, r):
    L = S // r
    for rho in range(r):
        dst[rho * L:(rho + 1) * L, :] = src[pl.ds(rho, L, stride=r), :]


def _permute_out(dst, src, r):
    L = S // r
    for rho in range(r):
        dst[pl.ds(rho, L, stride=r), :] = src[rho * L:(rho + 1) * L, :]


def _band_geometry(bb, nbl):
    r0 = pl.multiple_of(bb * BAND, BAND)
    k0 = pl.multiple_of(jnp.maximum(bb - 1, 0) * BAND, BAND)
    sub0 = (bb - lax.rem(bb, nbl)) * BAND
    qi = r0 + lax.broadcasted_iota(jnp.int32, (BAND, 1), 0)
    ki = k0 + lax.broadcasted_iota(jnp.int32, (1, 2 * BAND), 1)
    diff = qi - ki
    valid = (diff >= 0) & (diff <= BAND) & (ki >= sub0)
    return r0, k0, valid


def _dil_views(ref):
    return [[ref.at[:, pl.ds((3 * role + g) * 128, 128)] for g in range(3)] for role in range(3)]


DIL_UNROLL = 4


def _dil_in_specs():
    return [pl.BlockSpec((S, 128), lambda p, k=k: (0, 9 * p + k)) for k in range(9)]


def _dil_fwd(order, qkvb):
    def body(*refs):
        q_refs, k_refs, v_refs = refs[0:3], refs[3:6], refs[6:9]
        ob_ref, lse_ref, qp, kp, vp, op, lp = refs[9:16]
        on, ln = refs[16:19], refs[19:22]
        _, hm = _head_masks()
        for g, r in enumerate(DIL):
            nbl = S // r // BAND
            if r == 1:
                qs_, ks_, vs_, od, ld = q_refs[g], k_refs[g], v_refs[g], on[g], ln[g]
            else:
                _permute_in(qp, q_refs[g], r)
                _permute_in(kp, k_refs[g], r)
                _permute_in(vp, v_refs[g], r)
                qs_, ks_, vs_, od, ld = qp, kp, vp, op, lp

            def blk(t, c, qs_=qs_, ks_=ks_, vs_=vs_, od=od, ld=ld, nbl=nbl):
                work = []
                for u in range(DIL_UNROLL):
                    r0, k0, valid = _band_geometry(DIL_UNROLL * t + u, nbl)
                    q = qs_[pl.ds(r0, BAND), :] * 0.125
                    kw = ks_[pl.ds(k0, 2 * BAND), :].astype(bf16)
                    vw = vs_[pl.ds(k0, 2 * BAND), :]
                    for hh in (0, 1):
                        qh = jnp.where(hm[hh], q, 0.0).astype(bf16)
                        work.append((u, hh, r0, valid, vw, _dot_nt(qh, kw)))
                o = [jnp.zeros((BAND, 128), f32)] * DIL_UNROLL
                lse = [jnp.zeros((BAND, 128), f32)] * DIL_UNROLL
                for u, hh, r0, valid, vw, s in work:
                    s = jnp.where(valid, s, NEG)
                    m = jnp.max(s, axis=1, keepdims=True)
                    pr = jnp.exp(s - m)
                    l = jnp.sum(pr, axis=1, keepdims=True)
                    vm = jnp.where(hm[hh], vw, 0.0).astype(bf16)
                    o[u] = o[u] + _dot((pr / l).astype(bf16), vm)
                    lse[u] = jnp.where(hm[hh], m + jnp.log(l), lse[u])
                    if hh == 1:
                        od[pl.ds(r0, BAND), :] = o[u]
                        ld[pl.ds(r0, BAND), :] = lse[u]
                return c

            lax.fori_loop(0, S // BAND // DIL_UNROLL, blk, 0)
            if r != 1:
                _permute_out(on[g], op, r)
                _permute_out(ln[g], lp, r)

        def combine(i, c):
            r0 = pl.multiple_of(i * TQ, TQ)
            ls = [ln[g][pl.ds(r0, TQ), :] for g in range(3)]
            mx = jnp.maximum(jnp.maximum(ls[0], ls[1]), ls[2])
            es = [jnp.exp(l - mx) for l in ls]
            tot = (es[0] + es[1]) + es[2]
            acc = (es[0] / tot) * on[0][pl.ds(r0, TQ), :]
            acc = acc + (es[1] / tot) * on[1][pl.ds(r0, TQ), :]
            acc = acc + (es[2] / tot) * on[2][pl.ds(r0, TQ), :]
            ob_ref[pl.ds(r0, TQ), :] = acc.astype(bf16)
            lse_ref[pl.ds(r0, TQ), :] = mx + jnp.log(tot)
            return c

        lax.fori_loop(0, S // TQ, combine, 0)

    out_blk = pl.BlockSpec((S, 128), lambda p: (0, p))
    return _call(
        order, body, [qkvb] * 9, name="dil_fwd", grid=(2,),
        in_specs=_dil_in_specs(), out_specs=[out_blk, out_blk],
        out_shape=[jax.ShapeDtypeStruct((S, DILOUT), bf16), jax.ShapeDtypeStruct((S, DILOUT), f32)],
        scratch_shapes=[pltpu.VMEM((S, 128), f32)] * 11,
        compiler_params=_params(("parallel",)),
    )


def _branch_mix(order, oa, ob, was, wbs, gates):
    tm = 512

    def body(oa_ref, ob_ref, wa_ref, wb_ref, g_ref, ya_ref, yb_ref, mix_ref):
        oa_b, ob_b = oa_ref[...], ob_ref[...]
        for q in range(NCHIP):
            cols = slice(q * 256, (q + 1) * 256)
            ya = _dot(oa_b, wa_ref[q])
            yb = _dot(ob_b, wb_ref[q])
            ya_ref[:, cols] = ya.astype(bf16)
            yb_ref[:, cols] = yb.astype(bf16)
            ga = g_ref[:, q * 256:(q + 1) * 256].astype(f32)
            gb = g_ref[:, D + q * 256:D + (q + 1) * 256].astype(f32)
            mix_ref[:, cols] = (jax.nn.sigmoid(ga) * ya + jax.nn.sigmoid(gb) * yb).astype(bf16)

    row = lambda w: pl.BlockSpec((tm, w), lambda i: (i, 0))
    full3 = lambda a: pl.BlockSpec(a.shape, lambda i: (0, 0, 0))
    return _call(
        order, body, (oa, ob, was, wbs, gates), name="branch_mix", grid=(S // tm,),
        in_specs=[row(FOXW), row(DILOUT), full3(was), full3(wbs), row(2 * D)],
        out_specs=[row(D), row(D), row(D)],
        out_shape=[jax.ShapeDtypeStruct((S, D), bf16), jax.ShapeDtypeStruct((S, D), bf16),
                   jax.ShapeDtypeStruct((S, D), bf16)],
        compiler_params=_params(("parallel",)),
    )


def _outproj_norm(order, mixed, wout, x, g2):
    tm = 512

    def body(m_ref, w_ref, x_ref, g_ref, x2_ref, h2_ref):
        x2 = x_ref[...] + _dot(m_ref[...], w_ref[...])
        x2_ref[...] = x2
        r = lax.rsqrt(jnp.mean(x2 * x2, axis=-1, keepdims=True) + EPS)
        h2_ref[...] = ((x2 * r) * g_ref[...]).astype(bf16)

    row = pl.BlockSpec((tm, D), lambda i: (i, 0))
    return _call(
        order, body, (mixed, wout, x, g2), name="outproj_norm", grid=(S // tm,),
        in_specs=[row, pl.BlockSpec((D, D), lambda i: (0, 0)), row, pl.BlockSpec((1, D), lambda i: (0, 0))],
        out_specs=[row, row],
        out_shape=[jax.ShapeDtypeStruct((S, D), f32), jax.ShapeDtypeStruct((S, D), bf16)],
        compiler_params=_params(("parallel",)),
    )


def _mlp_up(order, h2, wups):
    tm = 1024

    def body(h_ref, w_ref, ru_ref, a_ref):
        ru = jnp.maximum(_dot(h_ref[...], w_ref[...]), 0.0)
        ru_ref[...] = ru.astype(bf16)
        a_ref[...] = (ru * ru).astype(bf16)

    out = pl.BlockSpec((tm, D), lambda q, i: (i, q))
    return _call(
        order, body, (h2, wups), name="mlp_up", grid=(NCHIP, S // tm),
        in_specs=[pl.BlockSpec((tm, D), lambda q, i: (i, 0)), pl.BlockSpec((None, D, D), lambda q, i: (q, 0, 0))],
        out_specs=[out, out],
        out_shape=[jax.ShapeDtypeStruct((S, DFF), bf16), jax.ShapeDtypeStruct((S, DFF), bf16)],
        compiler_params=_params(("parallel", "parallel")),
    )


def _mlp_down_loss(order, a, wdown, x2, g3, tgt):
    tm = 512

    def body(a_ref, w_ref, x2_ref, g_ref, t_ref, dx_ref, dxb_ref, dg_ref, loss_ref):
        i = pl.program_id(0)
        x3 = x2_ref[...] + _dot(a_ref[...], w_ref[...])
        r = lax.rsqrt(jnp.mean(x3 * x3, axis=-1, keepdims=True) + EPS)
        xh = x3 * r
        g = g_ref[...]
        e = xh * g - t_ref[...]
        part = 0.5 * jnp.sum(jnp.mean(e * e, axis=-1, keepdims=True), axis=0, keepdims=True)
        dy = e * (1.0 / D)
        gdy = dy * g
        dx = r * (gdy - xh * jnp.mean(gdy * xh, axis=-1, keepdims=True))
        dx_ref[...] = dx
        dxb_ref[...] = dx.astype(bf16)

        @pl.when(i == 0)
        def _():
            dg_ref[...] = jnp.zeros_like(dg_ref)
            loss_ref[...] = jnp.zeros_like(loss_ref)

        dg_ref[...] += jnp.sum(dy * xh, axis=0, keepdims=True)
        loss_ref[...] += jnp.broadcast_to(part, (1, 128))

    row = pl.BlockSpec((tm, D), lambda i: (i, 0))
    vec = pl.BlockSpec((1, D), lambda i: (0, 0))
    return _call(
        order, body, (a, wdown, x2, g3, tgt), name="mlp_down_loss", grid=(S // tm,),
        in_specs=[pl.BlockSpec((tm, DFF), lambda i: (i, 0)), pl.BlockSpec((DFF, D), lambda i: (0, 0)), row, vec, row],
        out_specs=[row, row, vec, pl.BlockSpec((1, 128), lambda i: (0, 0))],
        out_shape=[jax.ShapeDtypeStruct((S, D), f32), jax.ShapeDtypeStruct((S, D), bf16),
                   jax.ShapeDtypeStruct((1, D), f32), jax.ShapeDtypeStruct((1, 128), f32)],
        compiler_params=_params(("arbitrary",)),
    )


def _mlp_down_bwd(order, dx3b, wdown, u):
    tm = 512

    def body(d_ref, w_ref, u_ref, du_ref):
        d = d_ref[...]
        for q in range(NCHIP):
            cols = slice(q * D, (q + 1) * D)
            da = _dot_nt(d, w_ref[cols, :])
            du_ref[:, cols] = (da * (2.0 * u_ref[:, cols].astype(f32))).astype(bf16)

    return _call(
        order, body, (dx3b, wdown, u), name="mlp_down_bwd", grid=(S // tm,),
        in_specs=[pl.BlockSpec((tm, D), lambda i: (i, 0)), pl.BlockSpec((DFF, D), lambda i: (0, 0)),
                  pl.BlockSpec((tm, DFF), lambda i: (i, 0))],
        out_specs=pl.BlockSpec((tm, DFF), lambda i: (i, 0)),
        out_shape=jax.ShapeDtypeStruct((S, DFF), bf16),
        compiler_params=_params(("parallel",)),
    )


def _mlp_up_bwd(order, du, wups, x2, dx3, g2):
    tm = 512

    def body(du_ref, w_ref, x2_ref, dx3_ref, g_ref, dx2_ref, dx2b_ref, dg_ref):
        i = pl.program_id(0)
        dh = jnp.zeros((tm, D), f32)
        for q in range(NCHIP):
            dh = dh + _dot_nt(du_ref[:, q * D:(q + 1) * D], w_ref[q])
        x2 = x2_ref[...]
        r = lax.rsqrt(jnp.mean(x2 * x2, axis=-1, keepdims=True) + EPS)
        xh = x2 * r
        gdh = dh * g_ref[...]
        dx2 = dx3_ref[...] + r * (gdh - xh * jnp.mean(gdh * xh, axis=-1, keepdims=True))
        dx2_ref[...] = dx2
        dx2b_ref[...] = dx2.astype(bf16)

        @pl.when(i == 0)
        def _():
            dg_ref[...] = jnp.zeros_like(dg_ref)

        dg_ref[...] += jnp.sum(dh * xh, axis=0, keepdims=True)

    row = pl.BlockSpec((tm, D), lambda i: (i, 0))
    vec = pl.BlockSpec((1, D), lambda i: (0, 0))
    return _call(
        order, body, (du, wups, x2, dx3, g2), name="mlp_up_bwd", grid=(S // tm,),
        in_specs=[pl.BlockSpec((tm, DFF), lambda i: (i, 0)), pl.BlockSpec((NCHIP, D, D), lambda i: (0, 0, 0)),
                  row, row, vec],
        out_specs=[row, row, vec],
        out_shape=[jax.ShapeDtypeStruct((S, D), f32), jax.ShapeDtypeStruct((S, D), bf16),
                   jax.ShapeDtypeStruct((1, D), f32)],
        compiler_params=_params(("arbitrary",)),
    )


def _gate_bwd(order, dx2b, wout, gates, ya, yb):
    tm = 512

    def body(d_ref, w_ref, g_ref, ya_ref, yb_ref, dya_ref, dyb_ref, dproj_ref):
        dm = _dot_nt(d_ref[...], w_ref[...])
        sa = jax.nn.sigmoid(g_ref[:, 0:D].astype(f32))
        sb = jax.nn.sigmoid(g_ref[:, D:2 * D].astype(f32))
        dya_ref[...] = (dm * sa).astype(bf16)
        dyb_ref[...] = (dm * sb).astype(bf16)
        dproj_ref[:, 0:D] = (dm * ya_ref[...].astype(f32) * (sa * (1.0 - sa))).astype(bf16)
        dproj_ref[:, D:2 * D] = (dm * yb_ref[...].astype(f32) * (sb * (1.0 - sb))).astype(bf16)

    row = lambda w: pl.BlockSpec((tm, w), lambda i: (i, 0))
    return _call(
        order, body, (dx2b, wout, gates, ya, yb), name="gate_bwd", grid=(S // tm,),
        in_specs=[row(D), pl.BlockSpec((D, D), lambda i: (0, 0)), row(2 * D), row(D), row(D)],
        out_specs=[row(D), row(D), pl.BlockSpec((tm, 2 * D), lambda i: (i, F_G // (2 * D)))],
        out_shape=[jax.ShapeDtypeStruct((S, D), bf16), jax.ShapeDtypeStruct((S, D), bf16),
                   jax.ShapeDtypeStruct((S, NP), bf16)],
        compiler_params=_params(("parallel",)),
    )


def _branch_bwd(order, dya, dyb, was, wbs):
    tm = 512

    def body(dya_ref, dyb_ref, wa_ref, wb_ref, doa_ref, dob_ref):
        doa = jnp.zeros((tm, FOXW), f32)
        dob = jnp.zeros((tm, DILOUT), f32)
        for q in range(NCHIP):
            cols = slice(q * 256, (q + 1) * 256)
            doa = doa + _dot_nt(dya_ref[:, cols], wa_ref[q])
            dob = dob + _dot_nt(dyb_ref[:, cols], wb_ref[q])
        doa_ref[...] = doa.astype(bf16)
        dob_ref[...] = dob

    row = lambda w: pl.BlockSpec((tm, w), lambda i: (i, 0))
    full3 = lambda a: pl.BlockSpec(a.shape, lambda i: (0, 0, 0))
    return _call(
        order, body, (dya, dyb, was, wbs), name="branch_bwd", grid=(S // tm,),
        in_specs=[row(D), row(D), full3(was), full3(wbs)],
        out_specs=[row(FOXW), row(DILOUT)],
        out_shape=[jax.ShapeDtypeStruct((S, FOXW), bf16), jax.ShapeDtypeStruct((S, DILOUT), f32)],
        compiler_params=_params(("parallel",)),
    )


def _branch_wgrad(order, oa, ob, dya, dyb):
    def body(oa_ref, ob_ref, dya_ref, dyb_ref, dwa_ref, dwb_ref):
        dwa_ref[...] = _dot_tn(oa_ref[...], dya_ref[...])
        dwb_ref[...] = _dot_tn(ob_ref[...], dyb_ref[...])

    full = lambda w: pl.BlockSpec((S, w), lambda q: (0, 0))
    colq = pl.BlockSpec((S, 256), lambda q: (0, q))
    return _call(
        order, body, (oa, ob, dya, dyb), name="branch_wgrad", grid=(NCHIP,),
        in_specs=[full(FOXW), full(DILOUT), colq, colq],
        out_specs=[pl.BlockSpec((None, FOXW, 256), lambda q: (q, 0, 0)),
                   pl.BlockSpec((None, DILOUT, 256), lambda q: (q, 0, 0))],
        out_shape=[jax.ShapeDtypeStruct((NCHIP, FOXW, 256), f32), jax.ShapeDtypeStruct((NCHIP, DILOUT, 256), f32)],
        compiler_params=_params(("parallel",)),
    )


def _fox_bwd(order, qkva, doa, oa, lse, F, dproj):
    tq, tk = FOX_TQ, FOX_TK

    def body(qkv_ref, do_ref, o_ref, lse_ref, F_ref, _dproj_in, dF_ref, dqkv_ref, qa, ka, da, va, kat,
             dk_scr, dv_scr, dqt_scr):
        p = pl.program_id(0)
        lane, hm = _head_masks()
        keyi = lax.broadcasted_iota(jnp.int32, (tk, 1), 0)
        qryi = lax.broadcasted_iota(jnp.int32, (1, tq), 1)

        def prep(i, c):
            rows = pl.ds(pl.multiple_of(i * tk, tk), tk)
            _fox_operands(qkv_ref, F_ref, lse_ref, qa, ka, p, rows)
            do = do_ref[rows, :].astype(f32)
            prod = do * o_ref[rows, :].astype(f32)
            v = qkv_ref[rows, 256:384].astype(f32)
            for hh in (0, 1):
                free = (1 - hh) * HD
                delta = jnp.sum(jnp.where(hm[hh], prod, 0.0), axis=1, keepdims=True)
                da[hh, rows, :] = _set_lanes(jnp.where(hm[hh], do, 0.0), lane, free,
                                             [-t for t in _f32_parts(delta)]).astype(bf16)
                va[hh, rows, :] = _set_lanes(v, lane, free, [1.0] * 3).astype(bf16)
                kat[hh, i] = ka[hh, rows, :].astype(f32).T.astype(bf16)
                dk_scr[hh, rows, :] = jnp.zeros((tk, 128), f32)
                dv_scr[hh, rows, :] = jnp.zeros((tk, 128), f32)
            return c

        lax.fori_loop(0, S // tk, prep, 0)

        def qblock(i, c):
            r0 = pl.multiple_of(i * tq, tq)
            qrows = pl.ds(r0, tq)
            qh = [qa[hh, qrows, :] for hh in (0, 1)]
            dh = [da[hh, qrows, :] for hh in (0, 1)]
            dqt_scr[...] = jnp.zeros_like(dqt_scr)

            def kv(jb, c2, masked):
                keys = pl.ds(pl.multiple_of(jb * tk, tk), tk)
                sts = [_dot_nt(ka[hh, keys, :], qh[hh]) for hh in (0, 1)]
                dps = [_dot_nt(va[hh, keys, :], dh[hh]) for hh in (0, 1)]
                for hh in (0, 1):
                    pt = jnp.exp(sts[hh])
                    if masked:
                        pt = jnp.where(jb * tk + keyi <= r0 + qryi, pt, 0.0)
                    dsb = (pt * dps[hh]).astype(bf16)
                    dv_scr[hh, keys, :] += _dot(pt.astype(bf16), dh[hh])
                    dk_scr[hh, keys, :] += _dot(dsb, qh[hh])
                    dqt_scr[hh] += _dot(kat[hh, jb], dsb)
                return c2

            last = (r0 + tq - 1) // tk
            lax.fori_loop(0, last, lambda j, c2: kv(j, c2, False), 0)
            kv(last, 0, True)
            dq0, dq1 = dqt_scr[0].T, dqt_scr[1].T
            dqkv_ref[qrows, 0:128] = (jnp.where(hm[0], dq0, dq1) * 0.125).astype(bf16)
            dF_ref[qrows, :] = jnp.where(lane == 0, dq0[:, HD:HD + 1], jnp.where(lane == 1, dq1[:, 0:1], 0.0))
            return c

        lax.fori_loop(0, S // tq, qblock, 0)

        def finish(i, c):
            rows = pl.ds(pl.multiple_of(i * tq, tq), tq)
            dk0, dk1 = dk_scr[0, rows, :], dk_scr[1, rows, :]
            dqkv_ref[rows, 128:256] = jnp.where(hm[0], dk0, dk1).astype(bf16)
            dqkv_ref[rows, 256:384] = jnp.where(hm[0], dv_scr[0, rows, :], dv_scr[1, rows, :]).astype(bf16)
            cs = jnp.where(lane == 0, dk0[:, HD + L_ONE:HD + L_ONE + 1],
                           jnp.where(lane == 1, dk1[:, L_ONE:L_ONE + 1], 0.0))
            dF_ref[rows, :] = dF_ref[rows, :] - cs
            return c

        lax.fori_loop(0, S // tq, finish, 0)

    pair = pl.BlockSpec((S, 128), lambda p: (0, p))
    return _call(
        order, body, (qkva, doa, oa, lse, F, dproj), name="fox_bwd", grid=(4,),
        in_specs=[pl.BlockSpec((S, FOX_BLK), lambda p: (0, p)), pair, pair, pair,
                  pl.BlockSpec((S, 128), lambda p: (0, 0)), pl.BlockSpec(memory_space=pl.ANY)],
        out_specs=[pair, pl.BlockSpec((S, FOX_BLK), lambda p: (0, F_FOX // FOX_BLK + p))],
        out_shape=[jax.ShapeDtypeStruct((S, FOXW), f32), jax.ShapeDtypeStruct((S, NP), bf16)],
        input_output_aliases={5: 1},
        scratch_shapes=[pltpu.VMEM((2, S, 128), bf16)] * 4 + [pltpu.VMEM((2, S // tk, 128, tk), bf16)]
        + [pltpu.VMEM((2, S, 128), f32)] * 2 + [pltpu.VMEM((2, 128, tq), f32)],
        compiler_params=_params(("parallel",)),
    )


def _forget_bwd(order, dF, fa, bpad, dproj):
    nb = S // TQ

    def body(dF_ref, fa_ref, b_ref, _dproj_in, db_ref, dfa_ref):
        rr = lax.broadcasted_iota(jnp.int32, (TQ, TQ), 0)
        cc = lax.broadcasted_iota(jnp.int32, (TQ, TQ), 1)
        upper = (cc >= rr).astype(bf16)
        lane = lax.broadcasted_iota(jnp.int32, (1, 128), 1)
        carry = jnp.zeros((1, 128), f32)
        db = jnp.zeros((1, 128), f32)
        for b in reversed(range(nb)):
            cols = jnp.zeros((TQ, 128), f32)
            for h in range(8):
                c0 = (h // 2) * 128 + h % 2
                cols = jnp.where(lane == h, dF_ref[b * TQ:(b + 1) * TQ, c0:c0 + 1], cols)
            dlf = carry
            for part in _split3(cols):
                dlf = dlf + _dot(upper, part)
            carry = carry + jnp.sum(cols, axis=0, keepdims=True)
            z = fa_ref[b * TQ:(b + 1) * TQ, :] + b_ref[...]
            dz = jnp.where(lane < 8, dlf * jax.nn.sigmoid(-z), 0.0)
            dfa_ref[b * TQ:(b + 1) * TQ, 0:128] = dz.astype(bf16)
            dfa_ref[b * TQ:(b + 1) * TQ, 128:256] = jnp.zeros((TQ, 128), bf16)
            db = db + jnp.sum(dz, axis=0, keepdims=True)
        db_ref[...] = db

    whole = lambda a: pl.BlockSpec(a.shape, lambda i: (0,) * a.ndim)
    return _call(
        order, body, (dF, fa, bpad, dproj), name="forget_bwd", grid=(1,),
        in_specs=[whole(dF), whole(fa), whole(bpad), pl.BlockSpec(memory_space=pl.ANY)],
        out_specs=[pl.BlockSpec((1, 128), lambda i: (0, 0)), pl.BlockSpec((S, 256), lambda i: (0, F_FA // 256))],
        out_shape=[jax.ShapeDtypeStruct((1, 128), f32), jax.ShapeDtypeStruct((S, NP), bf16)],
        input_output_aliases={3: 1},
        compiler_params=_params(("arbitrary",)),
    )


def _dil_bwd(order, qkvb, dob, ob, lseb, rope, dproj):
    c_t, s1_t, s2_t = rope

    def body(*refs):
        q_refs, k_refs, v_refs = refs[0:3], refs[3:6], refs[6:9]
        dob_ref, ob_ref, lse_ref, c_ref, s1_ref, s2_ref, _dproj_in, dqkv_ref = refs[9:17]
        qp, kp, vp, dop, lp, dlp, dln, dqp, dkp, dvp, nat = refs[17:28]
        dq_out, dk_out, dv_out = _dil_views(dqkv_ref)
        _, hm = _head_masks()

        def delta_rows(i, c):
            r0 = pl.multiple_of(i * TQ, TQ)
            prod = dob_ref[pl.ds(r0, TQ), :] * ob_ref[pl.ds(r0, TQ), :].astype(f32)
            d0 = jnp.sum(jnp.where(hm[0], prod, 0.0), axis=1, keepdims=True)
            d1 = jnp.sum(jnp.where(hm[1], prod, 0.0), axis=1, keepdims=True)
            dln[pl.ds(r0, TQ), :] = jnp.where(hm[0], d0, d1)
            return c

        lax.fori_loop(0, S // TQ, delta_rows, 0)

        for g, r in enumerate(DIL):
            nbl = S // r // BAND
            if r == 1:
                srcs = (q_refs[g], k_refs[g], v_refs[g], dob_ref, lse_ref, dln)
            else:
                for dst, src in ((qp, q_refs[g]), (kp, k_refs[g]), (vp, v_refs[g]), (dop, dob_ref),
                                 (lp, lse_ref), (dlp, dln)):
                    _permute_in(dst, src, r)
                srcs = (qp, kp, vp, dop, lp, dlp)
            dkp[...] = jnp.zeros_like(dkp)
            dvp[...] = jnp.zeros_like(dvp)

            def blk(t, c, srcs=srcs, nbl=nbl):
                qs_, ks_, vs_, dos_, ls_, dls_ = srcs
                work = []
                for u in range(DIL_UNROLL):
                    r0, k0, valid = _band_geometry(DIL_UNROLL * t + u, nbl)
                    q = qs_[pl.ds(r0, BAND), :] * 0.125
                    kwf = ks_[pl.ds(k0, 2 * BAND), :]
                    kw = kwf.astype(bf16)
                    vw = vs_[pl.ds(k0, 2 * BAND), :].astype(bf16)
                    do = dos_[pl.ds(r0, BAND), :]
                    lse = ls_[pl.ds(r0, BAND), :]
                    dlt = dls_[pl.ds(r0, BAND), :]
                    for hh in (0, 1):
                        qh = jnp.where(hm[hh], q, 0.0).astype(bf16)
                        doh = jnp.where(hm[hh], do, 0.0).astype(bf16)
                        kh = jnp.where(hm[hh], kwf, 0.0).astype(bf16)
                        work.append((u, hh, r0, k0, valid, qh, doh, kh, lse[:, hh * HD:hh * HD + 1],
                                     dlt[:, hh * HD:hh * HD + 1], _dot_nt(qh, kw), _dot_nt(doh, vw)))
                for u, hh, r0, k0, valid, qh, doh, kh, lse_h, dlt_h, s, dp in work:
                    if hh == 0:
                        dq = jnp.zeros((BAND, 128), f32)
                        dk = jnp.zeros((2 * BAND, 128), f32)
                        dv = jnp.zeros((2 * BAND, 128), f32)
                    pr = jnp.where(valid, jnp.exp(s - lse_h), 0.0)
                    dsb = (pr * (dp - dlt_h)).astype(bf16)
                    dv = dv + _dot_tn(pr.astype(bf16), doh)
                    dk = dk + _dot_tn(dsb, qh)
                    dq = dq + _dot(dsb, kh)
                    if hh == 1:
                        dqp[pl.ds(r0, BAND), :] = dq * 0.125
                        dkp[pl.ds(k0, 2 * BAND), :] += dk
                        dvp[pl.ds(k0, 2 * BAND), :] += dv
                return c

            lax.fori_loop(0, S // BAND // DIL_UNROLL, blk, 0)

            for acc, out, roped in ((dqp, dq_out[g], True), (dkp, dk_out[g], True), (dvp, dv_out[g], False)):
                if r == 1:
                    src = acc
                else:
                    _permute_out(nat, acc, r)
                    src = nat

                def emit(i, c, src=src, out=out, roped=roped):
                    r0 = pl.multiple_of(i * TQ, TQ)
                    d = src[pl.ds(r0, TQ), :]
                    if roped:
                        d = (d * c_ref[pl.ds(r0, TQ), :] + pltpu.roll(d * s1_ref[pl.ds(r0, TQ), :], 8, 1)
                             + pltpu.roll(d * s2_ref[pl.ds(r0, TQ), :], 120, 1))
                    out[pl.ds(r0, TQ), :] = d.astype(bf16)
                    return c

                lax.fori_loop(0, S // TQ, emit, 0)

    pair = pl.BlockSpec((S, 128), lambda p: (0, p))
    tab = pl.BlockSpec((S, 128), lambda p: (0, 0))
    blk_spec = pl.BlockSpec((S, DIL_BLK), lambda p: (0, p))
    return _call(
        order, body, [qkvb] * 9 + [dob, ob, lseb, c_t, s1_t, s2_t, dproj], name="dil_bwd", grid=(2,),
        in_specs=_dil_in_specs() + [pair, pair, pair, tab, tab, tab, pl.BlockSpec(memory_space=pl.ANY)],
        out_specs=blk_spec,
        out_shape=jax.ShapeDtypeStruct((S, NP), bf16),
        input_output_aliases={15: 0},
        scratch_shapes=[pltpu.VMEM((S, 128), f32)] * 11,
        compiler_params=_params(("parallel",)),
    )


def _inproj_bwd(order, dproj, wt, x, dx2, g1):
    tm = 256

    def body(d_ref, w_ref, x_ref, dx2_ref, g_ref, dx_ref, dg_ref):
        i = pl.program_id(0)
        dh = _dot(d_ref[...], w_ref[...])
        xb = x_ref[...]
        r = lax.rsqrt(jnp.mean(xb * xb, axis=-1, keepdims=True) + EPS)
        xh = xb * r
        gdh = dh * g_ref[...]
        dx_ref[...] = dx2_ref[...] + r * (gdh - xh * jnp.mean(gdh * xh, axis=-1, keepdims=True))

        @pl.when(i == 0)
        def _():
            dg_ref[...] = jnp.zeros_like(dg_ref)

        dg_ref[...] += jnp.sum(dh * xh, axis=0, keepdims=True)

    row = pl.BlockSpec((tm, D), lambda i: (i, 0))
    vec = pl.BlockSpec((1, D), lambda i: (0, 0))
    return _call(
        order, body, (dproj, wt, x, dx2, g1), name="inproj_bwd", grid=(S // tm,),
        in_specs=[pl.BlockSpec((tm, NP), lambda i: (i, 0)), pl.BlockSpec((NP, D), lambda i: (0, 0)), row, row, vec],
        out_specs=[row, vec],
        out_shape=[jax.ShapeDtypeStruct((S, D), f32), jax.ShapeDtypeStruct((1, D), f32)],
        compiler_params=_params(("arbitrary",)),
    )


HBM = pl.BlockSpec(memory_space=pltpu.HBM)
SEM = pl.BlockSpec(memory_space=pltpu.SEMAPHORE)
SMALL_ROWS = 8


def _comm_call(name, body, bufs, order, sems_in=(), new_sems=(), behind=()):
    nb, ns, nn = len(bufs), len(sems_in), len(new_sems)
    extra = order.token_for(bufs) + list(behind)

    def kern(*refs):
        off = nb + ns + len(extra)
        body(refs[:nb], refs[nb:nb + ns], refs[off:off + nn])
        refs[-1][...] = jnp.zeros((8, 128), f32)

    res = pl.pallas_call(
        kern, name=name,
        in_specs=[HBM] * nb + [SEM] * ns + [pl.BlockSpec(memory_space=pl.ANY)] * len(extra),
        out_specs=[SEM] * nn + [HBM] * nb + [pl.BlockSpec(memory_space=pltpu.VMEM)],
        out_shape=[pltpu.SemaphoreType.DMA((k,)) for k in new_sems] + [pltpu.HBM(b.shape, b.dtype) for b in bufs]
        + [jax.ShapeDtypeStruct((8, 128), f32)],
        input_output_aliases={i: nn + i for i in range(nb)},
        compiler_params=pltpu.CompilerParams(has_side_effects=pltpu.SideEffectType.DATAFLOW_SIDE_EFFECTING),
    )(*[pltpu.with_memory_space_constraint(b, pltpu.HBM) for b in bufs], *sems_in, *extra)
    order.mark(res[-1])
    return list(res[:nn]), list(res[nn:nn + nb])


def _place():
    x, y, c = lax.axis_index("x"), lax.axis_index("y"), lax.axis_index("c")
    chips = [(1 - x, y), (x, 1 - y), (1 - x, 1 - y)]
    return x, y, c, chips


def _rcopy(src, dst, ssem, rsem, dev):
    return pltpu.make_async_remote_copy(src_ref=src, dst_ref=dst, send_sem=ssem, recv_sem=rsem,
                                        device_id=dev, device_id_type=pl.DeviceIdType.MESH)


def _half(nrows, which):
    return pl.ds(which * (nrows // 2), nrows // 2)


def _ici_copies(stack, group_sizes, ssems, rsems):
    x, y, c, chips = _place()
    me_q = 2 * x + y
    sends, recvs = [], []
    a = 0
    for grp, size in enumerate(group_sizes):
        for k in range(size):
            rows = _half(stack[a].shape[1], c)
            for j, (cx, cy) in enumerate(chips):
                mine = stack[a].at[me_q, rows]
                sends.append(_rcopy(mine, mine, ssems[grp].at[k * 3 + j], rsems[grp].at[k * 3 + j], (cx, cy, c)))
                theirs = stack[a].at[2 * cx + cy, rows]
                recvs.append(_rcopy(theirs, theirs, ssems[grp].at[k * 3 + j], rsems[grp].at[k * 3 + j],
                                    (cx, cy, c)))
            a += 1
    return sends, recvs


def _allgather_start(name, stacks, order):
    n = len(stacks)

    def body(bufs, _, new):
        sends, _r = _ici_copies(bufs, [n], [new[0]], [new[1]])
        for cp in sends:
            cp.start()

    return _comm_call(name, body, stacks, order, new_sems=(3 * n, 3 * n))


def _forward_copies(stack, ssem, rsem):
    x, y, c, chips = _place()
    sib = (x, y, 1 - c)
    sends, recvs = [], []
    for a in range(len(stack)):
        for j, (cx, cy) in enumerate(chips):
            landed = stack[a].at[2 * cx + cy, _half(stack[a].shape[1], c)]
            sends.append(_rcopy(landed, landed, ssem.at[a * 3 + j], rsem.at[a * 3 + j], sib))
            other = stack[a].at[2 * cx + cy, _half(stack[a].shape[1], 1 - c)]
            recvs.append(_rcopy(other, other, ssem.at[a * 3 + j], rsem.at[a * 3 + j], sib))
    return sends, recvs


def _allgather_forward(name, stacks, sems, order, behind=()):
    n = len(stacks)

    def body(bufs, taken, new):
        sends, recvs = _ici_copies(bufs, [n], [taken[0]], [taken[1]])
        for cp in sends:
            cp.wait_send()
        for cp in recvs:
            cp.wait_recv()
        fwd, _r = _forward_copies(bufs, new[0], new[1])
        for cp in fwd:
            cp.start()

    return _comm_call(name, body, stacks, order, sems_in=sems, new_sems=(3 * n, 3 * n), behind=behind)


def _allgather_finish(name, stacks, sems, order):
    def body(bufs, taken, _):
        sends, recvs = _forward_copies(bufs, taken[0], taken[1])
        for cp in sends:
            cp.wait_send()
        for cp in recvs:
            cp.wait_recv()

    return _comm_call(name, body, stacks, order, sems_in=sems)[1]


def _window_unit(q, j):
    return C2I[WIN_UNIT0[q] + j]


def _pair_copies(g, t, ssem, rsem, gathered):
    x, y, c, _ = _place()
    sib = (x, y, 1 - c)
    cps, whole = [], []
    for a in range(len(g)):
        if a == 0 and gathered:
            for q in range(NCHIP):
                for j in range(WIN_UNITS // 2):
                    u = jnp.where(c == 0, _window_unit(q, WIN_UNITS // 2 + j), _window_unit(q, j))
                    src = g[0].at[pl.ds(pl.multiple_of(u * UNIT, UNIT), UNIT), :]
                    cps.append(_rcopy(src, t[0].at[q, pl.ds(j * UNIT, UNIT), :], ssem.at[0], rsem.at[0], sib))
            whole.append(_rcopy(t[0], t[0], ssem.at[0], rsem.at[0], sib))
        else:
            cp = _rcopy(g[a].at[:, _half(g[a].shape[1], 1 - c), :], t[a], ssem.at[a], rsem.at[a], sib)
            cps.append(cp)
            whole.append(cp)
    return cps, whole


def _comm_multi(name, parts, order):
    def body(buf_refs, taken, new):
        ib = it = inew = 0
        for pbody, pbufs, psems, pnew, _ in parts:
            pbody(buf_refs[ib:ib + len(pbufs)], taken[it:it + len(psems)], new[inew:inew + len(pnew)])
            ib, it, inew = ib + len(pbufs), it + len(psems), inew + len(pnew)

    sems, bufs = _comm_call(name, body, [b for p in parts for b in p[1]], order,
                            sems_in=[s for p in parts for s in p[2]], new_sems=[k for p in parts for k in p[3]])
    out, ib, inew = [], 0, 0
    for _, pbufs, _, pnew, unpack in parts:
        out.append(unpack(sems[inew:inew + len(pnew)], bufs[ib:ib + len(pbufs)]))
        ib, inew = ib + len(pbufs), inew + len(pnew)
    return out


def _pair_start_part(gs, gathered=False):
    n = len(gs)
    ts = [lax.empty((NCHIP, WIN_ROWS // 2, D) if (a == 0 and gathered) else (NCHIP, g.shape[1] // 2, g.shape[2]), f32)
          for a, g in enumerate(gs)]

    def body(bufs, _, new):
        for cp in _pair_copies(bufs[:n], bufs[n:], new[0], new[1], gathered)[0]:
            cp.start()

    return body, list(gs) + ts, (), (n, n), lambda sems, bufs: (sems, bufs)


def _pair_wait_part(bufs, sems, gathered=False):
    n = len(bufs) // 2

    def body(refs, taken, _):
        for cp in _pair_copies(refs[:n], refs[n:], taken[0], taken[1], gathered)[1]:
            cp.wait_send()
            cp.wait_recv()

    return body, list(bufs), list(sems), (), lambda _, out: (out[:n], out[n:])


def _row_tile(h):
    return min(h, 256)


def _pair_add(order, g, t, c_arr, name):
    _, R, C = g.shape
    h = R // 2
    tr = _row_tile(h)
    nblk = h // tr

    def body(c_ref, g_ref, t_ref, p32_ref, p16_ref):
        s = g_ref[...] + t_ref[...]
        p32_ref[...] = s
        p16_ref[...] = s.astype(bf16)

    blk = pl.BlockSpec((None, tr, C), lambda q, i, c_ref: (q, i, 0))
    return _call_indexed(
        order, body, (c_arr,), (g, t), (NCHIP, nblk),
        [pl.BlockSpec((None, tr, C), lambda q, i, c_ref: (q, c_ref[0] * nblk + i, 0)), blk], [blk, blk],
        name=name,
        out_shape=[jax.ShapeDtypeStruct((NCHIP, h, C), f32), jax.ShapeDtypeStruct((NCHIP, h, C), bf16)],
        compiler_params=_params(("parallel", "parallel")),
    )


def _pair_add_gathered(order, dwt, t, c_arr, name):
    half_units, half_rows = WIN_UNITS // 2, WIN_ROWS // 2
    table = jnp.asarray([_window_unit(q, j) for q in range(NCHIP) for j in range(WIN_UNITS)], jnp.int32)

    def body(tab_ref, c_ref, g_hbm, t_ref, p32_ref, p16_ref, buf, sem):
        q = pl.program_id(0)

        def gather(w, slot):
            cps = []
            for j in range(half_units):
                u = tab_ref[w * WIN_UNITS + c_ref[0] * half_units + j]
                cps.append(pltpu.make_async_copy(g_hbm.at[pl.ds(pl.multiple_of(u * UNIT, UNIT), UNIT), :],
                                                 buf.at[slot, pl.ds(j * UNIT, UNIT), :], sem.at[slot]))
            return cps

        @pl.when(q == 0)
        def _():
            for cp in gather(0, 0):
                cp.start()

        @pl.when(q + 1 < NCHIP)
        def _():
            for cp in gather(q + 1, (q + 1) % 2):
                cp.start()

        slot = q % 2
        pltpu.make_async_copy(buf.at[slot], buf.at[slot], sem.at[slot]).wait()
        s = buf[slot] + t_ref[...]
        p32_ref[...] = s
        p16_ref[...] = s.astype(bf16)

    blk = pl.BlockSpec((None, half_rows, D), lambda q, tab_ref, c_ref: (q, 0, 0))
    return _call_indexed(
        order, body, (table, c_arr), (dwt, t), (NCHIP,),
        [pl.BlockSpec(memory_space=pl.ANY), blk], [blk, blk],
        scratch_shapes=[pltpu.VMEM((2, half_rows, D), f32), pltpu.SemaphoreType.DMA((2,))],
        name=name,
        out_shape=[jax.ShapeDtypeStruct((NCHIP, half_rows, D), f32),
                   jax.ShapeDtypeStruct((NCHIP, half_rows, D), bf16)],
        compiler_params=_params(("arbitrary",)),
    )


def _shard_copies(p, r, sm, ssem, rsem):
    x, y, c, chips = _place()
    n = len(p)
    sends, recvs = [], []
    for a in range(n):
        for j, (cx, cy) in enumerate(chips):
            k = a * 3 + j
            sends.append(_rcopy(p[a].at[2 * cx + cy], r[a].at[j], ssem.at[k], rsem.at[k], (cx, cy, c)))
            recvs.append(_rcopy(r[a].at[j], r[a].at[j], ssem.at[k], rsem.at[k], (cx, cy, c)))
    if sm is not None:
        mine = sm.at[4 * x + 2 * y + c]
        for i in range(1, 8):
            px = (1 - x) if i & 4 else x
            py = (1 - y) if i & 2 else y
            pc = (1 - c) if i & 1 else c
            k = 3 * n + i - 1
            sends.append(_rcopy(mine, mine, ssem.at[k], rsem.at[k], (px, py, pc)))
            slot = sm.at[4 * px + 2 * py + pc]
            recvs.append(_rcopy(slot, slot, ssem.at[k], rsem.at[k], (px, py, pc)))
    return sends, recvs


def _shard_start_part(p16s, sm=None):
    n = len(p16s)
    rs = [lax.empty((3,) + p.shape[1:], bf16) for p in p16s]
    extra = [] if sm is None else [sm]
    nsem = 3 * n + (7 if sm is not None else 0)

    def body(bufs, _, new):
        sends, _r = _shard_copies(bufs[:n], bufs[n:2 * n], bufs[2 * n] if extra else None, new[0], new[1])
        for cp in sends:
            cp.start()

    return body, list(p16s) + rs + extra, (), (nsem, nsem), lambda sems, bufs: (sems, bufs)


def _shard_wait_part(bufs, sems, n):
    has_sm = len(bufs) > 2 * n

    def body(refs, taken, _):
        sends, recvs = _shard_copies(refs[:n], refs[n:2 * n], refs[2 * n] if has_sm else None, taken[0], taken[1])
        for cp in sends:
            cp.wait_send()
        for cp in recvs:
            cp.wait_recv()

    return body, list(bufs), list(sems), (), lambda _, out: (out[n:2 * n], (out[2 * n] if has_sm else None))


def _shard_sum(order, p32, r, q_arr, c_arr, name):
    _, h, C = p32.shape
    tr = _row_tile(h)
    nblk = h // tr

    def body(q_ref, c_ref, p_ref, r_ref, o_ref):
        s = p_ref[...]
        for j in range(3):
            s = s + r_ref[j].astype(f32)
        o_ref[...] = s

    return _call_indexed(
        order, body, (q_arr, c_arr), (p32, r), (nblk,),
        [pl.BlockSpec((None, tr, C), lambda i, q_ref, c_ref: (q_ref[0], i, 0)),
         pl.BlockSpec((3, tr, C), lambda i, q_ref, c_ref: (0, i, 0))],
        pl.BlockSpec((tr, C), lambda i, q_ref, c_ref: (c_ref[0] * nblk + i, 0)),
        name=name, out_shape=jax.ShapeDtypeStruct((2 * h, C), f32),
        compiler_params=_params(("parallel",)),
    )


def _swap_copies(full, ssem, rsem):
    x, y, c, _ = _place()
    sends, recvs = [], []
    for a in range(len(full)):
        mine = full[a].at[_half(full[a].shape[0], c)]
        sends.append(_rcopy(mine, mine, ssem.at[a], rsem.at[a], (x, y, 1 - c)))
        other = full[a].at[_half(full[a].shape[0], 1 - c)]
        recvs.append(_rcopy(other, other, ssem.at[a], rsem.at[a], (x, y, 1 - c)))
    return sends, recvs


def _swap_start_part(fulls):
    n = len(fulls)

    def body(bufs, _, new):
        for cp in _swap_copies(bufs, new[0], new[1])[0]:
            cp.start()

    return body, list(fulls), (), (n, n), lambda sems, bufs: (sems, bufs)


def _swap_wait_part(fulls, sems):
    def body(refs, taken, _):
        sends, recvs = _swap_copies(refs, taken[0], taken[1])
        for cp in sends:
            cp.wait_send()
        for cp in recvs:
            cp.wait_recv()

    return body, list(fulls), list(sems), (), lambda _, out: out


def _small_sum(order, sm):
    def body(sm_ref, o_ref):
        s = sm_ref[0]
        for d in range(1, 8):
            s = s + sm_ref[d]
        o_ref[...] = s

    return _call(order, body, (sm,), name="small_grad_sum", out_shape=jax.ShapeDtypeStruct((SMALL_ROWS, D), f32))


def _adamw_math(w, g, m, v):
    m = ADAM_B1 * m + (1.0 - ADAM_B1) * g
    v = ADAM_B2 * v + (1.0 - ADAM_B2) * (g * g)
    m_hat = m / (1.0 - ADAM_B1 ** ADAM_STEP)
    v_hat = v / (1.0 - ADAM_B2 ** ADAM_STEP)
    return -ADAM_LR * (m_hat / (jnp.sqrt(v_hat) + ADAM_EPS) + ADAM_WD * w), m, v


def _adamw_small(order, ws, gs, ms, vs, name):
    n = len(ws)

    def body(*refs):
        for i in range(n):
            res = _adamw_math(*[refs[k * n + i][...] for k in range(4)])
            for k in range(3):
                refs[4 * n + 3 * i + k][...] = res[k]

    out = _call(order, body, list(ws) + list(gs) + list(ms) + list(vs), name=name,
                out_shape=[jax.ShapeDtypeStruct(w.shape, f32) for w in ws for _ in range(3)])
    return [out[3 * i:3 * i + 3] for i in range(n)]


def _adamw(order, w, g, m, v, name):
    R, C = w.shape
    if R <= 256 or R % 256 == 0:
        tr, tc = min(R, 256), C
    else:
        tr, tc = R, 128

    def body(w_ref, g_ref, m_ref, v_ref, d_ref, nm_ref, nv_ref):
        d_ref[...], nm_ref[...], nv_ref[...] = _adamw_math(w_ref[...], g_ref[...], m_ref[...], v_ref[...])

    blk = pl.BlockSpec((tr, tc), lambda i, j: (i, j))
    return _call(
        order, body, (w, g, m, v), name=name, grid=(R // tr, C // tc), in_specs=[blk] * 4, out_specs=[blk] * 3,
        out_shape=[jax.ShapeDtypeStruct((R, C), f32)] * 3,
        compiler_params=_params(("parallel", "parallel")),
    )


def _feature_major(w):
    return jnp.transpose(w, (2, 0, 1)).reshape(SHARD_IN, D)


def _unfeature_major(a):
    return jnp.transpose(a.reshape(SHARD_IN, 1, D), (1, 2, 0))


def _window_of(wt, q):
    def plain(k):
        return lambda w: jnp.pad(w, ((OWN_ROW0[k], WIN_ROWS - OWN_ROW0[k] - SHARD_IN), (0, 0))).astype(bf16)

    def chip1(w):
        lo = jnp.pad(w[0:62], ((2, WIN_ROWS - 64), (0, 0)))
        hi = jnp.pad(w[70:SHARD_IN], ((64, WIN_ROWS - 64 - (SHARD_IN - 70)), (0, 0)))
        return (lo + hi).astype(bf16)

    win = lax.switch(q, [plain(0), chip1, plain(2), plain(3)], wt)
    fa = jnp.pad(wt[62:70], ((0, FA_ROWS - 8), (0, 0))).astype(bf16)
    return win, fa


def _own_rows(gwin, gfa, q):
    def plain(k):
        return lambda gw, gf: gw[OWN_ROW0[k]:OWN_ROW0[k] + SHARD_IN]

    def chip1(gw, gf):
        return (jnp.pad(gw[2:64], ((0, SHARD_IN - 62), (0, 0))) + jnp.pad(gf[0:8], ((62, SHARD_IN - 70), (0, 0)))
                + jnp.pad(gw[64:64 + SHARD_IN - 70], ((70, 0), (0, 0))))

    return lax.switch(q, [plain(0), chip1, plain(2), plain(3)], gwin, gfa)


def kernel(x, norm_attn_g, w_in, b_forget, w_branch_a, w_branch_b, w_out, norm_mlp_g, w_up, w_down, norm_final_g, loss_target, m_norm_attn_g, m_w_in, m_b_forget, m_w_branch_a, m_w_branch_b, m_w_out, m_norm_mlp_g, m_w_up, m_w_down, m_norm_final_g, v_norm_attn_g, v_w_in, v_b_forget, v_w_branch_a, v_w_branch_b, v_w_out, v_norm_mlp_g, v_w_up, v_w_down, v_norm_final_g):
    xi, yi, ci = lax.axis_index("x"), lax.axis_index("y"), lax.axis_index("c")
    q_me = 2 * xi + yi
    c_arr = jnp.reshape(ci, (1,)).astype(jnp.int32)
    q_arr = jnp.reshape(q_me, (1,)).astype(jnp.int32)
    x_, tgt = x[0], loss_target[0]

    names = ["w_branch_a", "w_branch_b", "w_out", "w_up", "w_down"]
    big = dict(zip(names, [w_branch_a[0], w_branch_b[0], w_out[0], w_up[0], w_down[0]]))
    ms = dict(zip(names, [m_w_branch_a[0], m_w_branch_b[0], m_w_out[0], m_w_up[0], m_w_down[0]]))
    vs = dict(zip(names, [v_w_branch_a[0], v_w_branch_b[0], v_w_out[0], v_w_up[0], v_w_down[0]]))
    grad, upd = {}, {}
    order = _Order()

    def run(fn, *args, **kw):
        return fn(order, *args, **kw)

    def own_slot(a):
        return lax.dynamic_update_slice(lax.empty((NCHIP,) + a.shape, a.dtype), a[None], (q_me, 0, 0))

    wt_own = _feature_major(w_in)
    win, fa_blk = _window_of(wt_own, q_me)
    sem_in, in_s = _allgather_start("allgather_start_in", [own_slot(win), own_slot(fa_blk)], order)
    sem_rest, rest = _allgather_start("allgather_start_rest", [own_slot(w.astype(bf16)) for w in big.values()], order)
    rope = _rope_tables()
    mt_own, vt_own = _feature_major(m_w_in), _feature_major(v_w_in)
    sem_f, in_s = _allgather_forward("allgather_forward_in", in_s, sem_in, order,
                                     behind=[wt_own, mt_own, vt_own, *rope])
    wins, fas = _allgather_finish("allgather_finish_in", in_s, sem_f, order)
    wt = run(_assemble_win, wins, fas)

    bpad = jnp.pad(b_forget, ((0, 0), (0, 120)))
    h1, qkvb, qkva, gates, fa = run(_norm_inproj, x_, norm_attn_g, wt, rope)
    F = run(_forget_cumsum, fa, bpad)
    oa, lsea = run(_fox_fwd, qkva, F)
    sem_f, rest = _allgather_forward("allgather_forward_rest", rest, sem_rest, order)
    ob, lseb = run(_dil_fwd, qkvb)
    was, wbs, wouts, wups, wdowns = _allgather_finish("allgather_finish_rest", rest, sem_f, order)
    wout = wouts.reshape(D, D)
    wdown = wdowns.reshape(DFF, D)
    ya, yb, mixed = run(_branch_mix, oa, ob, was, wbs, gates)
    x2, h2 = run(_outproj_norm, mixed, wout, x_, norm_mlp_g)
    u, a = run(_mlp_up, h2, wups)
    dx3, dx3b, dg3, loss_part = run(_mlp_down_loss, a, wdown, x2, norm_final_g.reshape(1, D), tgt)

    def comm(name, *parts):
        return _comm_multi(name, list(parts), order)

    def pair_adds(group, gs, ts):
        return zip(*[run(_pair_add, gs[i], ts[i], c_arr, "pair_add_" + nm) for i, nm in enumerate(group)])

    def shard_sums(group, p32s, rs):
        return [run(_shard_sum, p32s[i], rs[i], q_arr, c_arr, "shard_sum_" + nm) for i, nm in enumerate(group)]

    def adamw_group(group, fulls):
        for nm, gfull in zip(group, fulls):
            grad[nm] = gfull
            upd[nm] = run(_adamw, big[nm], gfull, ms[nm], vs[nm], "adamw_" + nm)

    grp_a, grp_b, grp_c = ["w_down", "w_up"], ["w_out", "w_branch_a", "w_branch_b"], ["w_in", "w_in_fa"]
    du = run(_mlp_down_bwd, dx3b, wdown, u)
    dwdown = run(_mm, a, dx3b, "tn", f32, 1024, D, "wgrad_down")
    dwup = run(_mm, h2, du, "tn", f32, D, 1024, "wgrad_up", stack_cols=True)
    ((sem_pa, buf_pa),) = comm("pair_start_a", _pair_start_part([dwdown.reshape(NCHIP, DFF // NCHIP, D), dwup]))
    dx2, dx2b, dg2 = run(_mlp_up_bwd, du, wups, x2, dx3, norm_mlp_g)
    ((gs, ts),) = comm("pair_wait_a", _pair_wait_part(buf_pa, sem_pa))
    p32_a, p16_a = pair_adds(grp_a, gs, ts)
    ((sem_sa, buf_sa),) = comm("shard_start_a", _shard_start_part(p16_a))
    dya, dyb, dproj = run(_gate_bwd, dx2b, wout, gates, ya, yb)
    dwout = run(_mm, mixed, dx2b, "tn", f32, D, D, "wgrad_out")
    doa, dob = run(_branch_bwd, dya, dyb, was, wbs)
    dwas, dwbs = run(_branch_wgrad, oa, ob, dya, dyb)
    ((sem_pb, buf_pb),) = comm("pair_start_b", _pair_start_part([dwout.reshape(NCHIP, D // NCHIP, D), dwas, dwbs]))
    dF, dproj = run(_fox_bwd, qkva, doa, oa, lsea, F, dproj)
    (gs, ts), (rs_a, _) = comm("pair_wait_b_shard_wait_a", _pair_wait_part(buf_pb, sem_pb),
                               _shard_wait_part(buf_sa, sem_sa, len(grp_a)))
    p32_b, p16_b = pair_adds(grp_b, gs, ts)
    fulls_a = shard_sums(grp_a, p32_a, rs_a)
    (sem_wa, fulls_a), (sem_sb, buf_sb) = comm("swap_start_a_shard_start_b", _swap_start_part(fulls_a),
                                               _shard_start_part(p16_b))
    dbf, dproj = run(_forget_bwd, dF, fa, bpad, dproj)
    dproj = run(_dil_bwd, qkvb, dob, ob, lseb, rope, dproj)
    (rs_b, _), fulls_a = comm("shard_wait_b_swap_wait_a", _shard_wait_part(buf_sb, sem_sb, len(grp_b)),
                              _swap_wait_part(fulls_a, sem_wa))
    fulls_b = shard_sums(grp_b, p32_b, rs_b)
    ((sem_wb, fulls_b),) = comm("swap_start_b", _swap_start_part(fulls_b))
    dwt = run(_mm, dproj, h1, "tn", f32, 512, D, "wgrad_in")
    dwfa = jnp.broadcast_to(dwt[F_FA:F_FA + FA_ROWS][None], (NCHIP, FA_ROWS, D))
    (sem_pc, buf_pc), fulls_b = comm("pair_start_c_swap_wait_b", _pair_start_part([dwt, dwfa], gathered=True),
                                     _swap_wait_part(fulls_b, sem_wb))
    adamw_group(grp_b, fulls_b)
    (((dwt_c, dwfa_c), (t_in, t_fa)),) = comm("pair_wait_c", _pair_wait_part(buf_pc, sem_pc, gathered=True))
    p32_in, p16_in = run(_pair_add_gathered, dwt_c, t_in, c_arr, "pair_add_w_in")
    p32_fa, p16_fa = run(_pair_add, dwfa_c, t_fa, c_arr, "pair_add_w_in_fa")
    ((sem_sc, buf_sc),) = comm("shard_start_c", _shard_start_part([p16_in, p16_fa]))
    gx, dg1 = run(_inproj_bwd, dproj, wt, x_, dx2, norm_attn_g)
    adamw_group(grp_a, fulls_a)
    small = jnp.concatenate([dg1, dg2, dg3, jnp.pad(dbf[:, 0:8], ((0, 0), (0, D - 8))),
                             jnp.pad(loss_part, ((0, 0), (0, D - 128))),
                             jnp.zeros((SMALL_ROWS - 5, D), f32)], axis=0)
    sm = lax.dynamic_update_slice(lax.empty((8, SMALL_ROWS, D), f32), small[None],
                                  (4 * xi + 2 * yi + ci, 0, 0))
    (sem_sm, buf_sm), (rs_c, _) = comm("small_start_shard_wait_c", _shard_start_part([], sm),
                                       _shard_wait_part(buf_sc, sem_sc, len(grp_c)))
    fulls_c = shard_sums(grp_c, [p32_in, p32_fa], rs_c)
    (sem_wc, fulls_c), (_, sm) = comm("swap_start_c_small_wait", _swap_start_part(fulls_c),
                                      _shard_wait_part(buf_sm, sem_sm, 0))
    gsmall = run(_small_sum, sm)
    loss = gsmall[4, 0]

    grad["norm_attn_g"], grad["norm_mlp_g"] = gsmall[0:1], gsmall[1:2]
    grad["norm_final_g"], grad["b_forget"] = gsmall[2:3], gsmall[3:4, 0:8]
    smalls = ["norm_attn_g", "norm_mlp_g", "norm_final_g", "b_forget"]
    res = run(_adamw_small, [norm_attn_g, norm_mlp_g, norm_final_g.reshape(1, D), b_forget],
              [grad[nm] for nm in smalls],
              [m_norm_attn_g, m_norm_mlp_g, m_norm_final_g.reshape(1, D), m_b_forget],
              [v_norm_attn_g, v_norm_mlp_g, v_norm_final_g.reshape(1, D), v_b_forget], "adamw_small")
    upd.update(zip(smalls, res))

    ((gwin, gfa),) = comm("swap_wait_c", _swap_wait_part(fulls_c, sem_wc))
    g_in = _own_rows(gwin, gfa, q_me)
    upd_in = run(_adamw, wt_own, g_in, mt_own, vt_own, "adamw_w_in")
    grad["w_in"] = _unfeature_major(g_in)
    upd["w_in"] = [_unfeature_major(t) for t in upd_in]

    order_out = ["norm_attn_g", "w_in", "b_forget", "w_branch_a", "w_branch_b", "w_out", "norm_mlp_g", "w_up",
                 "w_down", "norm_final_g"]
    shapes = dict(norm_attn_g=norm_attn_g.shape, w_in=w_in.shape, b_forget=b_forget.shape,
                  w_branch_a=w_branch_a.shape, w_branch_b=w_branch_b.shape, w_out=w_out.shape,
                  norm_mlp_g=norm_mlp_g.shape, w_up=w_up.shape, w_down=w_down.shape, norm_final_g=norm_final_g.shape)
    outs = [loss, gx.reshape(x.shape)]
    outs += [grad[nm].reshape(shapes[nm]) for nm in order_out]
    for k in range(3):
        outs += [upd[nm][k].reshape(shapes[nm]) for nm in order_out]
    return tuple(outs)
```

```python
import jax
import jax.numpy as jnp
from jax import lax
from jax.experimental import pallas as pl
from jax.experimental.pallas import tpu as pltpu

f32 = jnp.float32
bf16 = jnp.bfloat16

S = 2048
D = 1024
DFF = 4096
HD = 64
FOXW = 512
DILOUT = 256
DIL = (1, 4, 16)
BAND = 128
EPS = 1e-6
NEG = -1e30
ROPE_THETA = 500000.0
NCHIP = 4
TQ = 256

ADAM_LR, ADAM_B1, ADAM_B2, ADAM_EPS, ADAM_WD, ADAM_STEP = 0.001, 0.9, 0.999, 1e-08, 0.01, 10
VMEM_LIMIT = 56 * 1024 * 1024

UNIT = 64
NP = 6144
F_DIL, F_FOX, F_FA, F_G = 0, 2304, 3840, 4096
DIL_BLK, FOX_BLK = 1152, 384
WIN_UNITS, WIN_ROWS = 24, 1536
WIN_UNIT0 = (0, 23, 45, 68)
OWN_ROW0 = (0, 2, 60, 62)
SHARD_IN = 1474
FA_ROWS = 32


def _compact_to_internal():
    c2i = {}
    for p in range(2):
        for role in range(3):
            for g in range(3):
                for hh in range(2):
                    c2i[24 + 12 * role + 4 * g + 2 * p + hh] = 18 * p + 6 * role + 2 * g + hh
    for p in range(4):
        for role in range(3):
            for hh in range(2):
                c2i[8 * role + 2 * p + hh] = F_FOX // UNIT + 6 * p + 2 * role + hh
    for j in range(32):
        c2i[60 + j] = F_G // UNIT + j
    return c2i


C2I = _compact_to_internal()
OVERLAP_UNITS = (23, 45, 46, 68)


def _params(sem=None):
    return pltpu.CompilerParams(dimension_semantics=sem, vmem_limit_bytes=VMEM_LIMIT)


class _Order:
    def __init__(self):
        self.tok = None

    def mark(self, v):
        self.tok = v

    def token_for(self, args):
        return [] if self.tok is None or any(self.tok is a for a in args) else [self.tok]


def _call(order, body, args, in_specs=None, **kw):
    args = list(args)
    n_in = len(args)
    if in_specs is None:
        in_specs = [pl.BlockSpec(memory_space=pltpu.VMEM)] * n_in
    kern = body
    extra = order.token_for(args)
    if extra:
        in_specs = list(in_specs) + [pl.BlockSpec(memory_space=pl.ANY)]

        def kern(*refs):
            body(*refs[:n_in], *refs[n_in + 1:])

    out = pl.pallas_call(kern, in_specs=in_specs, **kw)(*args, *extra)
    order.mark(out[0] if isinstance(out, (tuple, list)) else out)
    return out


def _call_indexed(order, body, scalars, args, grid, in_specs, out_specs, scratch_shapes=(), **kw):
    args, in_specs = list(args), list(in_specs)
    n_front = len(scalars) + len(args)
    kern = body
    extra = order.token_for(args)
    if extra:
        in_specs.append(pl.BlockSpec(memory_space=pl.ANY))

        def kern(*refs):
            body(*refs[:n_front], *refs[n_front + 1:])

    out = pl.pallas_call(
        kern, grid_spec=pltpu.PrefetchScalarGridSpec(num_scalar_prefetch=len(scalars), grid=grid, in_specs=in_specs,
                                                     out_specs=out_specs, scratch_shapes=scratch_shapes),
        **kw)(*scalars, *args, *extra)
    order.mark(out[0] if isinstance(out, (tuple, list)) else out)
    return out


def _dot(a, b):
    return jnp.dot(a, b, preferred_element_type=f32)


def _dot_nt(a, b):
    return lax.dot_general(a, b, (((1,), (1,)), ((), ())), preferred_element_type=f32)


def _dot_tn(a, b):
    return lax.dot_general(a, b, (((0,), (0,)), ((), ())), preferred_element_type=f32)


def _split3(x):
    hi = x.astype(bf16)
    r1 = x - hi.astype(f32)
    mid = r1.astype(bf16)
    lo = (r1 - mid.astype(f32)).astype(bf16)
    return hi, mid, lo


def _rope_tables():
    half = 8
    inv_freq = jnp.power(jnp.float32(ROPE_THETA), -jnp.arange(half, dtype=f32) * 2.0 / 16)
    ang = jnp.arange(S).astype(f32)[:, None] * inv_freq[None, :]
    cos, sin = jnp.cos(ang), jnp.sin(ang)
    one = jnp.ones((S, HD - 16), f32)
    zero = jnp.zeros((S, HD - 16), f32)
    z8 = jnp.zeros((S, 8), f32)
    c = jnp.concatenate([cos, cos, one], axis=1)
    s1 = jnp.concatenate([-sin, z8, zero], axis=1)
    s2 = jnp.concatenate([z8, sin, zero], axis=1)
    return tuple(jnp.concatenate([t, t], axis=1) for t in (c, s1, s2))


def _mm(order, a, b, mode, out_dtype, tm, tn, name, stack_cols=False):
    if mode == "nn":
        (M, K), (_, N) = a.shape, b.shape
        a_spec = pl.BlockSpec((tm, K), lambda i, j: (i, 0))
        b_spec = pl.BlockSpec((K, tn), lambda i, j: (0, j))
        dot = _dot
    elif mode == "nt":
        (M, K), (N, _) = a.shape, b.shape
        a_spec = pl.BlockSpec((tm, K), lambda i, j: (i, 0))
        b_spec = pl.BlockSpec((tn, K), lambda i, j: (j, 0))
        dot = _dot_nt
    else:
        (K, M), (_, N) = a.shape, b.shape
        a_spec = pl.BlockSpec((K, tm), lambda i, j: (0, i))
        b_spec = pl.BlockSpec((K, tn), lambda i, j: (0, j))
        dot = _dot_tn

    def body(a_ref, b_ref, o_ref):
        o_ref[...] = dot(a_ref[...], b_ref[...]).astype(out_dtype)

    if stack_cols:
        assert tm == M
        out_spec = pl.BlockSpec((None, tm, tn), lambda i, j: (j, 0, 0))
        out_shape = jax.ShapeDtypeStruct((N // tn, M, tn), out_dtype)
    else:
        out_spec = pl.BlockSpec((tm, tn), lambda i, j: (i, j))
        out_shape = jax.ShapeDtypeStruct((M, N), out_dtype)
    return _call(
        order, body, (a, b), name=name, grid=(M // tm, N // tn), in_specs=[a_spec, b_spec],
        out_specs=out_spec, out_shape=out_shape,
        compiler_params=_params(("parallel", "parallel")),
    )


def _assemble_win(order, wins, fas):
    def body(win_ref, fa_ref, o_ref):
        q = pl.program_id(0)

        @pl.when(q == 0)
        def _():
            o_ref[...] = jnp.zeros_like(o_ref)

        for k in range(NCHIP):
            @pl.when(q == k)
            def _(k=k):
                for j in range(WIN_UNITS):
                    cu = WIN_UNIT0[k] + j
                    dst = pl.ds(C2I[cu] * UNIT, UNIT)
                    if cu in OVERLAP_UNITS:
                        o_ref[dst, :] += win_ref[j * UNIT:(j + 1) * UNIT, :]
                    else:
                        o_ref[dst, :] = win_ref[j * UNIT:(j + 1) * UNIT, :]
                if k == 1:
                    o_ref[F_FA:F_FA + FA_ROWS, :] = fa_ref[...]

    return _call(
        order, body, (wins, fas), name="assemble_w_in", grid=(NCHIP,),
        in_specs=[pl.BlockSpec((None, WIN_ROWS, D), lambda q: (q, 0, 0)),
                  pl.BlockSpec((None, FA_ROWS, D), lambda q: (1, 0, 0))],
        out_specs=pl.BlockSpec((NP, D), lambda q: (0, 0)),
        out_shape=jax.ShapeDtypeStruct((NP, D), bf16),
        compiler_params=_params(("arbitrary",)),
    )


def _norm_inproj(order, x, g1, wt, rope):
    tm = 256
    c_t, s1_t, s2_t = rope

    def body(x_ref, g_ref, w_ref, c_ref, s1_ref, s2_ref, h_ref, qkvb_ref, qkva_ref, gates_ref, fa_ref):
        xb = x_ref[...]
        r = lax.rsqrt(jnp.mean(xb * xb, axis=-1, keepdims=True) + EPS)
        h = ((xb * r) * g_ref[...]).astype(bf16)
        h_ref[...] = h
        c, s1, s2 = c_ref[...], s1_ref[...], s2_ref[...]
        for p in range(2):
            pb = _dot_nt(h, w_ref[F_DIL + p * DIL_BLK:F_DIL + (p + 1) * DIL_BLK, :])
            for ch in range(DIL_BLK // 128):
                pc = pb[:, ch * 128:(ch + 1) * 128]
                if ch < 6:
                    pc = pc * c + pltpu.roll(pc, 120, 1) * s1 + pltpu.roll(pc, 8, 1) * s2
                qkvb_ref[:, p * DIL_BLK + ch * 128:p * DIL_BLK + (ch + 1) * 128] = pc
        qkva_ref[...] = _dot_nt(h, w_ref[F_FOX:F_FA, :]).astype(bf16)
        fa_ref[...] = _dot_nt(h, w_ref[F_FA:F_FA + 128, :])
        gates_ref[...] = _dot_nt(h, w_ref[F_G:NP, :]).astype(bf16)

    row = lambda w: pl.BlockSpec((tm, w), lambda i: (i, 0))
    return _call(
        order, body, (x, g1, wt, c_t, s1_t, s2_t), name="norm_inproj", grid=(S // tm,),
        in_specs=[row(D), pl.BlockSpec((1, D), lambda i: (0, 0)), pl.BlockSpec((NP, D), lambda i: (0, 0)),
                  row(128), row(128), row(128)],
        out_specs=[row(D), row(2 * DIL_BLK), row(4 * FOX_BLK), row(2 * D), row(128)],
        out_shape=[jax.ShapeDtypeStruct((S, D), bf16), jax.ShapeDtypeStruct((S, 2 * DIL_BLK), f32),
                   jax.ShapeDtypeStruct((S, 4 * FOX_BLK), bf16), jax.ShapeDtypeStruct((S, 2 * D), bf16),
                   jax.ShapeDtypeStruct((S, 128), f32)],
        compiler_params=_params(("parallel",)),
    )


def _forget_cumsum(order, fa, bpad):
    nb = S // TQ

    def body(fa_ref, b_ref, F_ref):
        rr = lax.broadcasted_iota(jnp.int32, (TQ, TQ), 0)
        cc = lax.broadcasted_iota(jnp.int32, (TQ, TQ), 1)
        tri = (rr >= cc).astype(bf16)
        lane = lax.broadcasted_iota(jnp.int32, (1, 128), 1)
        carry = jnp.zeros((1, 128), f32)
        for b in range(nb):
            z = fa_ref[b * TQ:(b + 1) * TQ, :] + b_ref[...]
            lf = jnp.minimum(z, 0.0) - jnp.log(1.0 + jnp.exp(-jnp.abs(z)))
            lf = jnp.where(lane < 8, lf, 0.0)
            hi, mid, lo = _split3(lf)
            fb = (_dot(tri, hi) + _dot(tri, mid)) + _dot(tri, lo) + carry
            F_ref[b * TQ:(b + 1) * TQ, :] = fb
            carry = fb[TQ - 1:TQ, :]

    return _call(
        order, body, (fa, bpad), name="forget_cumsum",
        out_shape=jax.ShapeDtypeStruct((S, 128), f32),
        compiler_params=_params(),
    )


def _head_masks():
    lane = lax.broadcasted_iota(jnp.int32, (1, 128), 1)
    return lane, (lane < HD, lane >= HD)


L_FT, L_ONE = 0, 3
FOX_TQ, FOX_TK = 256, 512


def _set_lanes(x, lane, first, cols):
    for n, col in enumerate(cols):
        x = jnp.where(lane == first + n, col, x)
    return x


def _f32_parts(col):
    return [t.astype(f32) for t in _split3(col)]


def _fox_operands(qkv_ref, F_ref, lse_ref, qa, ka, p, rows):
    lane, hm = _head_masks()
    q = qkv_ref[rows, 0:128].astype(f32) * 0.125
    k = qkv_ref[rows, 128:256].astype(f32)
    Fb = F_ref[rows, :]
    for hh in (0, 1):
        free = (1 - hh) * HD
        fcol = jnp.sum(jnp.where(lane == 2 * p + hh, Fb, 0.0), axis=1, keepdims=True)
        qterm = fcol if lse_ref is None else fcol - lse_ref[rows, hh * HD:hh * HD + 1]
        qcols = _f32_parts(qterm) + [1.0] * 3
        kcols = [1.0] * 3 + [-t for t in _f32_parts(fcol)]
        qa[hh, rows, :] = _set_lanes(jnp.where(hm[hh], q, 0.0), lane, free, qcols).astype(bf16)
        ka[hh, rows, :] = _set_lanes(k, lane, free, kcols).astype(bf16)


def _fox_fwd(order, qkva, F):
    tq, tk = FOX_TQ, FOX_TK

    def body(qkv_ref, F_ref, o_ref, lse_ref, qa, ka, vt):
        p = pl.program_id(0)
        keyi = lax.broadcasted_iota(jnp.int32, (tk, 1), 0)
        qryi = lax.broadcasted_iota(jnp.int32, (1, tq), 1)
        sub = lax.broadcasted_iota(jnp.int32, (128, 1), 0)

        def prep(i, c):
            rows = pl.ds(pl.multiple_of(i * tk, tk), tk)
            _fox_operands(qkv_ref, F_ref, None, qa, ka, p, rows)
            vt[i] = qkv_ref[rows, 256:384].astype(f32).T.astype(bf16)
            return c

        lax.fori_loop(0, S // tk, prep, 0)

        def qblock(i, first_half):
            r0 = pl.multiple_of(i * tq, tq)
            qh = [qa[hh, pl.ds(r0, tq), :] for hh in (0, 1)]

            def kv(jb, carry, masked, width):
                keys = pl.ds(pl.multiple_of(jb * tk, tk), width)
                sts = [_dot_nt(ka[hh, keys, :], qh[hh]) for hh in (0, 1)]
                new = []
                for hh in (0, 1):
                    m, l, a = carry[3 * hh:3 * hh + 3]
                    st = sts[hh]
                    if masked:
                        st = jnp.where(jb * tk + keyi[0:width] <= r0 + qryi, st, NEG)
                    mn = jnp.maximum(m, jnp.max(st, axis=0, keepdims=True))
                    al = jnp.exp(m - mn)
                    pt = jnp.exp(st - mn)
                    l = al * l + jnp.sum(pt, axis=0, keepdims=True)
                    a = al * a + _dot(vt[jb, hh * HD:(hh + 1) * HD, 0:width], pt.astype(bf16))
                    new += [mn, l, a]
                return tuple(new)

            init = (jnp.full((1, tq), NEG, f32), jnp.zeros((1, tq), f32), jnp.zeros((HD, tq), f32)) * 2
            last = (r0 + tq - 1) // tk
            carry = lax.fori_loop(0, last, lambda j, cr: kv(j, cr, False, tk), init)
            m0, l0, a0, m1, l1, a1 = kv(last, carry, True, tk // 2 if first_half else tk)
            ot = jnp.concatenate([a0 / l0, a1 / l1], axis=0)
            lt = jnp.where(sub < HD, m0 + jnp.log(l0), m1 + jnp.log(l1))
            o_ref[pl.ds(r0, tq), :] = ot.T.astype(bf16)
            lse_ref[pl.ds(r0, tq), :] = lt.T

        def qpair(t, c):
            qblock(2 * t, True)
            qblock(2 * t + 1, False)
            return c

        assert tk == 2 * tq
        lax.fori_loop(0, S // tk, qpair, 0)

    pair = pl.BlockSpec((S, 128), lambda p: (0, p))
    return _call(
        order, body, (qkva, F), name="fox_fwd", grid=(4,),
        in_specs=[pl.BlockSpec((S, FOX_BLK), lambda p: (0, p)), pl.BlockSpec((S, 128), lambda p: (0, 0))],
        out_specs=[pair, pair],
        out_shape=[jax.ShapeDtypeStruct((S, FOXW), bf16), jax.ShapeDtypeStruct((S, FOXW), f32)],
        scratch_shapes=[pltpu.VMEM((2, S, 128), bf16)] * 2 + [pltpu.VMEM((S // tk, 128, tk), bf16)],
        compiler_params=_params(("parallel",)),
    )


def _permute_in(dst, src, r):
    L = S // r
    for rho in range(r):
        dst[rho * L:(rho + 1) * L, :] = src[pl.ds(rho, L, stride=r), :]


def _permute_out(dst, src, r):
    L = S // r
    for rho in range(r):
        dst[pl.ds(rho, L, stride=r), :] = src[rho * L:(rho + 1) * L, :]


def _band_geometry(bb, nbl):
    r0 = pl.multiple_of(bb * BAND, BAND)
    k0 = pl.multiple_of(jnp.maximum(bb - 1, 0) * BAND, BAND)
    sub0 = (bb - lax.rem(bb, nbl)) * BAND
    qi = r0 + lax.broadcasted_iota(jnp.int32, (BAND, 1), 0)
    ki = k0 + lax.broadcasted_iota(jnp.int32, (1, 2 * BAND), 1)
    diff = qi - ki
    valid = (diff >= 0) & (diff <= BAND) & (ki >= sub0)
    return r0, k0, valid


def _dil_views(ref):
    return [[ref.at[:, pl.ds((3 * role + g) * 128, 128)] for g in range(3)] for role in range(3)]


DIL_UNROLL = 4


def _dil_in_specs():
    return [pl.BlockSpec((S, 128), lambda p, k=k: (0, 9 * p + k)) for k in range(9)]


def _dil_fwd(order, qkvb):
    def body(*refs):
        q_refs, k_refs, v_refs = refs[0:3], refs[3:6], refs[6:9]
        ob_ref, lse_ref, qp, kp, vp, op, lp = refs[9:16]
        on, ln = refs[16:19], refs[19:22]
        _, hm = _head_masks()
        for g, r in enumerate(DIL):
            nbl = S // r // BAND
            if r == 1:
                qs_, ks_, vs_, od, ld = q_refs[g], k_refs[g], v_refs[g], on[g], ln[g]
            else:
                _permute_in(qp, q_refs[g], r)
                _permute_in(kp, k_refs[g], r)
                _permute_in(vp, v_refs[g], r)
                qs_, ks_, vs_, od, ld = qp, kp, vp, op, lp

            def blk(t, c, qs_=qs_, ks_=ks_, vs_=vs_, od=od, ld=ld, nbl=nbl):
                work = []
                for u in range(DIL_UNROLL):
                    r0, k0, valid = _band_geometry(DIL_UNROLL * t + u, nbl)
                    q = qs_[pl.ds(r0, BAND), :] * 0.125
                    kw = ks_[pl.ds(k0, 2 * BAND), :].astype(bf16)
                    vw = vs_[pl.ds(k0, 2 * BAND), :]
                    for hh in (0, 1):
                        qh = jnp.where(hm[hh], q, 0.0).astype(bf16)
                        work.append((u, hh, r0, valid, vw, _dot_nt(qh, kw)))
                o = [jnp.zeros((BAND, 128), f32)] * DIL_UNROLL
                lse = [jnp.zeros((BAND, 128), f32)] * DIL_UNROLL
                for u, hh, r0, valid, vw, s in work:
                    s = jnp.where(valid, s, NEG)
                    m = jnp.max(s, axis=1, keepdims=True)
                    pr = jnp.exp(s - m)
                    l = jnp.sum(pr, axis=1, keepdims=True)
                    vm = jnp.where(hm[hh], vw, 0.0).astype(bf16)
                    o[u] = o[u] + _dot((pr / l).astype(bf16), vm)
                    lse[u] = jnp.where(hm[hh], m + jnp.log(l), lse[u])
                    if hh == 1:
                        od[pl.ds(r0, BAND), :] = o[u]
                        ld[pl.ds(r0, BAND), :] = lse[u]
                return c

            lax.fori_loop(0, S // BAND // DIL_UNROLL, blk, 0)
            if r != 1:
                _permute_out(on[g], op, r)
                _permute_out(ln[g], lp, r)

        def combine(i, c):
            r0 = pl.multiple_of(i * TQ, TQ)
            ls = [ln[g][pl.ds(r0, TQ), :] for g in range(3)]
            mx = jnp.maximum(jnp.maximum(ls[0], ls[1]), ls[2])
            es = [jnp.exp(l - mx) for l in ls]
            tot = (es[0] + es[1]) + es[2]
            acc = (es[0] / tot) * on[0][pl.ds(r0, TQ), :]
            acc = acc + (es[1] / tot) * on[1][pl.ds(r0, TQ), :]
            acc = acc + (es[2] / tot) * on[2][pl.ds(r0, TQ), :]
            ob_ref[pl.ds(r0, TQ), :] = acc.astype(bf16)
            lse_ref[pl.ds(r0, TQ), :] = mx + jnp.log(tot)
            return c

        lax.fori_loop(0, S // TQ, combine, 0)

    out_blk = pl.BlockSpec((S, 128), lambda p: (0, p))
    return _call(
        order, body, [qkvb] * 9, name="dil_fwd", grid=(2,),
        in_specs=_dil_in_specs(), out_specs=[out_blk, out_blk],
        out_shape=[jax.ShapeDtypeStruct((S, DILOUT), bf16), jax.ShapeDtypeStruct((S, DILOUT), f32)],
        scratch_shapes=[pltpu.VMEM((S, 128), f32)] * 11,
        compiler_params=_params(("parallel",)),
    )


def _branch_mix(order, oa, ob, was, wbs, gates):
    tm = 512

    def body(oa_ref, ob_ref, wa_ref, wb_ref, g_ref, ya_ref, yb_ref, mix_ref):
        oa_b, ob_b = oa_ref[...], ob_ref[...]
        for q in range(NCHIP):
            cols = slice(q * 256, (q + 1) * 256)
            ya = _dot(oa_b, wa_ref[q])
            yb = _dot(ob_b, wb_ref[q])
            ya_ref[:, cols] = ya.astype(bf16)
            yb_ref[:, cols] = yb.astype(bf16)
            ga = g_ref[:, q * 256:(q + 1) * 256].astype(f32)
            gb = g_ref[:, D + q * 256:D + (q + 1) * 256].astype(f32)
            mix_ref[:, cols] = (jax.nn.sigmoid(ga) * ya + jax.nn.sigmoid(gb) * yb).astype(bf16)

    row = lambda w: pl.BlockSpec((tm, w), lambda i: (i, 0))
    full3 = lambda a: pl.BlockSpec(a.shape, lambda i: (0, 0, 0))
    return _call(
        order, body, (oa, ob, was, wbs, gates), name="branch_mix", grid=(S // tm,),
        in_specs=[row(FOXW), row(DILOUT), full3(was), full3(wbs), row(2 * D)],
        out_specs=[row(D), row(D), row(D)],
        out_shape=[jax.ShapeDtypeStruct((S, D), bf16), jax.ShapeDtypeStruct((S, D), bf16),
                   jax.ShapeDtypeStruct((S, D), bf16)],
        compiler_params=_params(("parallel",)),
    )


def _outproj_norm(order, mixed, wout, x, g2):
    tm = 512

    def body(m_ref, w_ref, x_ref, g_ref, x2_ref, h2_ref):
        x2 = x_ref[...] + _dot(m_ref[...], w_ref[...])
        x2_ref[...] = x2
        r = lax.rsqrt(jnp.mean(x2 * x2, axis=-1, keepdims=True) + EPS)
        h2_ref[...] = ((x2 * r) * g_ref[...]).astype(bf16)

    row = pl.BlockSpec((tm, D), lambda i: (i, 0))
    return _call(
        order, body, (mixed, wout, x, g2), name="outproj_norm", grid=(S // tm,),
        in_specs=[row, pl.BlockSpec((D, D), lambda i: (0, 0)), row, pl.BlockSpec((1, D), lambda i: (0, 0))],
        out_specs=[row, row],
        out_shape=[jax.ShapeDtypeStruct((S, D), f32), jax.ShapeDtypeStruct((S, D), bf16)],
        compiler_params=_params(("parallel",)),
    )


def _mlp_up(order, h2, wups):
    tm = 1024

    def body(h_ref, w_ref, ru_ref, a_ref):
        ru = jnp.maximum(_dot(h_ref[...], w_ref[...]), 0.0)
        ru_ref[...] = ru.astype(bf16)
        a_ref[...] = (ru * ru).astype(bf16)

    out = pl.BlockSpec((tm, D), lambda q, i: (i, q))
    return _call(
        order, body, (h2, wups), name="mlp_up", grid=(NCHIP, S // tm),
        in_specs=[pl.BlockSpec((tm, D), lambda q, i: (i, 0)), pl.BlockSpec((None, D, D), lambda q, i: (q, 0, 0))],
        out_specs=[out, out],
        out_shape=[jax.ShapeDtypeStruct((S, DFF), bf16), jax.ShapeDtypeStruct((S, DFF), bf16)],
        compiler_params=_params(("parallel", "parallel")),
    )


def _mlp_down_loss(order, a, wdown, x2, g3, tgt):
    tm = 512

    def body(a_ref, w_ref, x2_ref, g_ref, t_ref, dx_ref, dxb_ref, dg_ref, loss_ref):
        i = pl.program_id(0)
        x3 = x2_ref[...] + _dot(a_ref[...], w_ref[...])
        r = lax.rsqrt(jnp.mean(x3 * x3, axis=-1, keepdims=True) + EPS)
        xh = x3 * r
        g = g_ref[...]
        e = xh * g - t_ref[...]
        part = 0.5 * jnp.sum(jnp.mean(e * e, axis=-1, keepdims=True), axis=0, keepdims=True)
        dy = e * (1.0 / D)
        gdy = dy * g
        dx = r * (gdy - xh * jnp.mean(gdy * xh, axis=-1, keepdims=True))
        dx_ref[...] = dx
        dxb_ref[...] = dx.astype(bf16)

        @pl.when(i == 0)
        def _():
            dg_ref[...] = jnp.zeros_like(dg_ref)
            loss_ref[...] = jnp.zeros_like(loss_ref)

        dg_ref[...] += jnp.sum(dy * xh, axis=0, keepdims=True)
        loss_ref[...] += jnp.broadcast_to(part, (1, 128))

    row = pl.BlockSpec((tm, D), lambda i: (i, 0))
    vec = pl.BlockSpec((1, D), lambda i: (0, 0))
    return _call(
        order, body, (a, wdown, x2, g3, tgt), name="mlp_down_loss", grid=(S // tm,),
        in_specs=[pl.BlockSpec((tm, DFF), lambda i: (i, 0)), pl.BlockSpec((DFF, D), lambda i: (0, 0)), row, vec, row],
        out_specs=[row, row, vec, pl.BlockSpec((1, 128), lambda i: (0, 0))],
        out_shape=[jax.ShapeDtypeStruct((S, D), f32), jax.ShapeDtypeStruct((S, D), bf16),
                   jax.ShapeDtypeStruct((1, D), f32), jax.ShapeDtypeStruct((1, 128), f32)],
        compiler_params=_params(("arbitrary",)),
    )


def _mlp_down_bwd(order, dx3b, wdown, u):
    tm = 512

    def body(d_ref, w_ref, u_ref, du_ref):
        d = d_ref[...]
        for q in range(NCHIP):
            cols = slice(q * D, (q + 1) * D)
            da = _dot_nt(d, w_ref[cols, :])
            du_ref[:, cols] = (da * (2.0 * u_ref[:, cols].astype(f32))).astype(bf16)

    return _call(
        order, body, (dx3b, wdown, u), name="mlp_down_bwd", grid=(S // tm,),
        in_specs=[pl.BlockSpec((tm, D), lambda i: (i, 0)), pl.BlockSpec((DFF, D), lambda i: (0, 0)),
                  pl.BlockSpec((tm, DFF), lambda i: (i, 0))],
        out_specs=pl.BlockSpec((tm, DFF), lambda i: (i, 0)),
        out_shape=jax.ShapeDtypeStruct((S, DFF), bf16),
        compiler_params=_params(("parallel",)),
    )


def _mlp_up_bwd(order, du, wups, x2, dx3, g2):
    tm = 512

    def body(du_ref, w_ref, x2_ref, dx3_ref, g_ref, dx2_ref, dx2b_ref, dg_ref):
        i = pl.program_id(0)
        dh = jnp.zeros((tm, D), f32)
        for q in range(NCHIP):
            dh = dh + _dot_nt(du_ref[:, q * D:(q + 1) * D], w_ref[q])
        x2 = x2_ref[...]
        r = lax.rsqrt(jnp.mean(x2 * x2, axis=-1, keepdims=True) + EPS)
        xh = x2 * r
        gdh = dh * g_ref[...]
        dx2 = dx3_ref[...] + r * (gdh - xh * jnp.mean(gdh * xh, axis=-1, keepdims=True))
        dx2_ref[...] = dx2
        dx2b_ref[...] = dx2.astype(bf16)

        @pl.when(i == 0)
        def _():
            dg_ref[...] = jnp.zeros_like(dg_ref)

        dg_ref[...] += jnp.sum(dh * xh, axis=0, keepdims=True)

    row = pl.BlockSpec((tm, D), lambda i: (i, 0))
    vec = pl.BlockSpec((1, D), lambda i: (0, 0))
    return _call(
        order, body, (du, wups, x2, dx3, g2), name="mlp_up_bwd", grid=(S // tm,),
        in_specs=[pl.BlockSpec((tm, DFF), lambda i: (i, 0)), pl.BlockSpec((NCHIP, D, D), lambda i: (0, 0, 0)),
                  row, row, vec],
        out_specs=[row, row, vec],
        out_shape=[jax.ShapeDtypeStruct((S, D), f32), jax.ShapeDtypeStruct((S, D), bf16),
                   jax.ShapeDtypeStruct((1, D), f32)],
        compiler_params=_params(("arbitrary",)),
    )


def _gate_bwd(order, dx2b, wout, gates, ya, yb):
    tm = 512

    def body(d_ref, w_ref, g_ref, ya_ref, yb_ref, dya_ref, dyb_ref, dproj_ref):
        dm = _dot_nt(d_ref[...], w_ref[...])
        sa = jax.nn.sigmoid(g_ref[:, 0:D].astype(f32))
        sb = jax.nn.sigmoid(g_ref[:, D:2 * D].astype(f32))
        dya_ref[...] = (dm * sa).astype(bf16)
        dyb_ref[...] = (dm * sb).astype(bf16)
        dproj_ref[:, 0:D] = (dm * ya_ref[...].astype(f32) * (sa * (1.0 - sa))).astype(bf16)
        dproj_ref[:, D:2 * D] = (dm * yb_ref[...].astype(f32) * (sb * (1.0 - sb))).astype(bf16)

    row = lambda w: pl.BlockSpec((tm, w), lambda i: (i, 0))
    return _call(
        order, body, (dx2b, wout, gates, ya, yb), name="gate_bwd", grid=(S // tm,),
        in_specs=[row(D), pl.BlockSpec((D, D), lambda i: (0, 0)), row(2 * D), row(D), row(D)],
        out_specs=[row(D), row(D), pl.BlockSpec((tm, 2 * D), lambda i: (i, F_G // (2 * D)))],
        out_shape=[jax.ShapeDtypeStruct((S, D), bf16), jax.ShapeDtypeStruct((S, D), bf16),
                   jax.ShapeDtypeStruct((S, NP), bf16)],
        compiler_params=_params(("parallel",)),
    )


def _branch_bwd(order, dya, dyb, was, wbs):
    tm = 512

    def body(dya_ref, dyb_ref, wa_ref, wb_ref, doa_ref, dob_ref):
        doa = jnp.zeros((tm, FOXW), f32)
        dob = jnp.zeros((tm, DILOUT), f32)
        for q in range(NCHIP):
            cols = slice(q * 256, (q + 1) * 256)
            doa = doa + _dot_nt(dya_ref[:, cols], wa_ref[q])
            dob = dob + _dot_nt(dyb_ref[:, cols], wb_ref[q])
        doa_ref[...] = doa.astype(bf16)
        dob_ref[...] = dob

    row = lambda w: pl.BlockSpec((tm, w), lambda i: (i, 0))
    full3 = lambda a: pl.BlockSpec(a.shape, lambda i: (0, 0, 0))
    return _call(
        order, body, (dya, dyb, was, wbs), name="branch_bwd", grid=(S // tm,),
        in_specs=[row(D), row(D), full3(was), full3(wbs)],
        out_specs=[row(FOXW), row(DILOUT)],
        out_shape=[jax.ShapeDtypeStruct((S, FOXW), bf16), jax.ShapeDtypeStruct((S, DILOUT), f32)],
        compiler_params=_params(("parallel",)),
    )


def _branch_wgrad(order, oa, ob, dya, dyb):
    def body(oa_ref, ob_ref, dya_ref, dyb_ref, dwa_ref, dwb_ref):
        dwa_ref[...] = _dot_tn(oa_ref[...], dya_ref[...])
        dwb_ref[...] = _dot_tn(ob_ref[...], dyb_ref[...])

    full = lambda w: pl.BlockSpec((S, w), lambda q: (0, 0))
    colq = pl.BlockSpec((S, 256), lambda q: (0, q))
    return _call(
        order, body, (oa, ob, dya, dyb), name="branch_wgrad", grid=(NCHIP,),
        in_specs=[full(FOXW), full(DILOUT), colq, colq],
        out_specs=[pl.BlockSpec((None, FOXW, 256), lambda q: (q, 0, 0)),
                   pl.BlockSpec((None, DILOUT, 256), lambda q: (q, 0, 0))],
        out_shape=[jax.ShapeDtypeStruct((NCHIP, FOXW, 256), f32), jax.ShapeDtypeStruct((NCHIP, DILOUT, 256), f32)],
        compiler_params=_params(("parallel",)),
    )


def _fox_bwd(order, qkva, doa, oa, lse, F, dproj):
    tq, tk = FOX_TQ, FOX_TK

    def body(qkv_ref, do_ref, o_ref, lse_ref, F_ref, _dproj_in, dF_ref, dqkv_ref, qa, ka, da, va, kat,
             dk_scr, dv_scr, dqt_scr):
        p = pl.program_id(0)
        lane, hm = _head_masks()
        keyi = lax.broadcasted_iota(jnp.int32, (tk, 1), 0)
        qryi = lax.broadcasted_iota(jnp.int32, (1, tq), 1)

        def prep(i, c):
            rows = pl.ds(pl.multiple_of(i * tk, tk), tk)
            _fox_operands(qkv_ref, F_ref, lse_ref, qa, ka, p, rows)
            do = do_ref[rows, :].astype(f32)
            prod = do * o_ref[rows, :].astype(f32)
            v = qkv_ref[rows, 256:384].astype(f32)
            for hh in (0, 1):
                free = (1 - hh) * HD
                delta = jnp.sum(jnp.where(hm[hh], prod, 0.0), axis=1, keepdims=True)
                da[hh, rows, :] = _set_lanes(jnp.where(hm[hh], do, 0.0), lane, free,
                                             [-t for t in _f32_parts(delta)]).astype(bf16)
                va[hh, rows, :] = _set_lanes(v, lane, free, [1.0] * 3).astype(bf16)
                kat[hh, i] = ka[hh, rows, :].astype(f32).T.astype(bf16)
                dk_scr[hh, rows, :] = jnp.zeros((tk, 128), f32)
                dv_scr[hh, rows, :] = jnp.zeros((tk, 128), f32)
            return c

        lax.fori_loop(0, S // tk, prep, 0)

        def qblock(i, first_half):
            r0 = pl.multiple_of(i * tq, tq)
            qrows = pl.ds(r0, tq)
            qh = [qa[hh, qrows, :] for hh in (0, 1)]
            dh = [da[hh, qrows, :] for hh in (0, 1)]
            dqt_scr[...] = jnp.zeros_like(dqt_scr)

            def kv(jb, c2, masked, width):
                keys = pl.ds(pl.multiple_of(jb * tk, tk), width)
                sts = [_dot_nt(ka[hh, keys, :], qh[hh]) for hh in (0, 1)]
                dps = [_dot_nt(va[hh, keys, :], dh[hh]) for hh in (0, 1)]
                for hh in (0, 1):
                    pt = jnp.exp(sts[hh])
                    if masked:
                        pt = jnp.where(jb * tk + keyi[0:width] <= r0 + qryi, pt, 0.0)
                    dsb = (pt * dps[hh]).astype(bf16)
                    dv_scr[hh, keys, :] += _dot(pt.astype(bf16), dh[hh])
                    dk_scr[hh, keys, :] += _dot(dsb, qh[hh])
                    dqt_scr[hh] += _dot(kat[hh, jb, :, 0:width], dsb)
                return c2

            last = (r0 + tq - 1) // tk
            lax.fori_loop(0, last, lambda j, c2: kv(j, c2, False, tk), 0)
            kv(last, 0, True, tk // 2 if first_half else tk)
            dq0, dq1 = dqt_scr[0].T, dqt_scr[1].T
            dqkv_ref[qrows, 0:128] = (jnp.where(hm[0], dq0, dq1) * 0.125).astype(bf16)
            dF_ref[qrows, :] = jnp.where(lane == 0, dq0[:, HD:HD + 1], jnp.where(lane == 1, dq1[:, 0:1], 0.0))

        def qpair(t, c):
            qblock(2 * t, True)
            qblock(2 * t + 1, False)
            return c

        assert tk == 2 * tq
        lax.fori_loop(0, S // tk, qpair, 0)

        def finish(i, c):
            rows = pl.ds(pl.multiple_of(i * tq, tq), tq)
            dk0, dk1 = dk_scr[0, rows, :], dk_scr[1, rows, :]
            dqkv_ref[rows, 128:256] = jnp.where(hm[0], dk0, dk1).astype(bf16)
            dqkv_ref[rows, 256:384] = jnp.where(hm[0], dv_scr[0, rows, :], dv_scr[1, rows, :]).astype(bf16)
            cs = jnp.where(lane == 0, dk0[:, HD + L_ONE:HD + L_ONE + 1],
                           jnp.where(lane == 1, dk1[:, L_ONE:L_ONE + 1], 0.0))
            dF_ref[rows, :] = dF_ref[rows, :] - cs
            return c

        lax.fori_loop(0, S // tq, finish, 0)

    pair = pl.BlockSpec((S, 128), lambda p: (0, p))
    return _call(
        order, body, (qkva, doa, oa, lse, F, dproj), name="fox_bwd", grid=(4,),
        in_specs=[pl.BlockSpec((S, FOX_BLK), lambda p: (0, p)), pair, pair, pair,
                  pl.BlockSpec((S, 128), lambda p: (0, 0)), pl.BlockSpec(memory_space=pl.ANY)],
        out_specs=[pair, pl.BlockSpec((S, FOX_BLK), lambda p: (0, F_FOX // FOX_BLK + p))],
        out_shape=[jax.ShapeDtypeStruct((S, FOXW), f32), jax.ShapeDtypeStruct((S, NP), bf16)],
        input_output_aliases={5: 1},
        scratch_shapes=[pltpu.VMEM((2, S, 128), bf16)] * 4 + [pltpu.VMEM((2, S // tk, 128, tk), bf16)]
        + [pltpu.VMEM((2, S, 128), f32)] * 2 + [pltpu.VMEM((2, 128, tq), f32)],
        compiler_params=_params(("parallel",)),
    )


def _forget_bwd(order, dF, fa, bpad, dproj):
    nb = S // TQ

    def body(dF_ref, fa_ref, b_ref, _dproj_in, db_ref, dfa_ref):
        rr = lax.broadcasted_iota(jnp.int32, (TQ, TQ), 0)
        cc = lax.broadcasted_iota(jnp.int32, (TQ, TQ), 1)
        upper = (cc >= rr).astype(bf16)
        lane = lax.broadcasted_iota(jnp.int32, (1, 128), 1)
        carry = jnp.zeros((1, 128), f32)
        db = jnp.zeros((1, 128), f32)
        for b in reversed(range(nb)):
            cols = jnp.zeros((TQ, 128), f32)
            for h in range(8):
                c0 = (h // 2) * 128 + h % 2
                cols = jnp.where(lane == h, dF_ref[b * TQ:(b + 1) * TQ, c0:c0 + 1], cols)
            dlf = carry
            for part in _split3(cols):
                dlf = dlf + _dot(upper, part)
            carry = carry + jnp.sum(cols, axis=0, keepdims=True)
            z = fa_ref[b * TQ:(b + 1) * TQ, :] + b_ref[...]
            dz = jnp.where(lane < 8, dlf * jax.nn.sigmoid(-z), 0.0)
            dfa_ref[b * TQ:(b + 1) * TQ, 0:128] = dz.astype(bf16)
            dfa_ref[b * TQ:(b + 1) * TQ, 128:256] = jnp.zeros((TQ, 128), bf16)
            db = db + jnp.sum(dz, axis=0, keepdims=True)
        db_ref[...] = db

    whole = lambda a: pl.BlockSpec(a.shape, lambda i: (0,) * a.ndim)
    return _call(
        order, body, (dF, fa, bpad, dproj), name="forget_bwd", grid=(1,),
        in_specs=[whole(dF), whole(fa), whole(bpad), pl.BlockSpec(memory_space=pl.ANY)],
        out_specs=[pl.BlockSpec((1, 128), lambda i: (0, 0)), pl.BlockSpec((S, 256), lambda i: (0, F_FA // 256))],
        out_shape=[jax.ShapeDtypeStruct((1, 128), f32), jax.ShapeDtypeStruct((S, NP), bf16)],
        input_output_aliases={3: 1},
        compiler_params=_params(("arbitrary",)),
    )


def _dil_bwd(order, qkvb, dob, ob, lseb, rope, dproj):
    c_t, s1_t, s2_t = rope

    def body(*refs):
        q_refs, k_refs, v_refs = refs[0:3], refs[3:6], refs[6:9]
        dob_ref, ob_ref, lse_ref, c_ref, s1_ref, s2_ref, _dproj_in, dqkv_ref = refs[9:17]
        qp, kp, vp, dop, lp, dlp, dln, dqp, dkp, dvp, nat = refs[17:28]
        dq_out, dk_out, dv_out = _dil_views(dqkv_ref)
        _, hm = _head_masks()

        def delta_rows(i, c):
            r0 = pl.multiple_of(i * TQ, TQ)
            prod = dob_ref[pl.ds(r0, TQ), :] * ob_ref[pl.ds(r0, TQ), :].astype(f32)
            d0 = jnp.sum(jnp.where(hm[0], prod, 0.0), axis=1, keepdims=True)
            d1 = jnp.sum(jnp.where(hm[1], prod, 0.0), axis=1, keepdims=True)
            dln[pl.ds(r0, TQ), :] = jnp.where(hm[0], d0, d1)
            return c

        lax.fori_loop(0, S // TQ, delta_rows, 0)

        for g, r in enumerate(DIL):
            nbl = S // r // BAND
            if r == 1:
                srcs = (q_refs[g], k_refs[g], v_refs[g], dob_ref, lse_ref, dln)
            else:
                for dst, src in ((qp, q_refs[g]), (kp, k_refs[g]), (vp, v_refs[g]), (dop, dob_ref),
                                 (lp, lse_ref), (dlp, dln)):
                    _permute_in(dst, src, r)
                srcs = (qp, kp, vp, dop, lp, dlp)
            dkp[...] = jnp.zeros_like(dkp)
            dvp[...] = jnp.zeros_like(dvp)

            def blk(t, c, srcs=srcs, nbl=nbl):
                qs_, ks_, vs_, dos_, ls_, dls_ = srcs
                work = []
                for u in range(DIL_UNROLL):
                    r0, k0, valid = _band_geometry(DIL_UNROLL * t + u, nbl)
                    q = qs_[pl.ds(r0, BAND), :] * 0.125
                    kwf = ks_[pl.ds(k0, 2 * BAND), :]
                    kw = kwf.astype(bf16)
                    vw = vs_[pl.ds(k0, 2 * BAND), :].astype(bf16)
                    do = dos_[pl.ds(r0, BAND), :]
                    lse = ls_[pl.ds(r0, BAND), :]
                    dlt = dls_[pl.ds(r0, BAND), :]
                    for hh in (0, 1):
                        qh = jnp.where(hm[hh], q, 0.0).astype(bf16)
                        doh = jnp.where(hm[hh], do, 0.0).astype(bf16)
                        kh = jnp.where(hm[hh], kwf, 0.0).astype(bf16)
                        work.append((u, hh, r0, k0, valid, qh, doh, kh, lse[:, hh * HD:hh * HD + 1],
                                     dlt[:, hh * HD:hh * HD + 1], _dot_nt(qh, kw), _dot_nt(doh, vw)))
                for u, hh, r0, k0, valid, qh, doh, kh, lse_h, dlt_h, s, dp in work:
                    if hh == 0:
                        dq = jnp.zeros((BAND, 128), f32)
                        dk = jnp.zeros((2 * BAND, 128), f32)
                        dv = jnp.zeros((2 * BAND, 128), f32)
                    pr = jnp.where(valid, jnp.exp(s - lse_h), 0.0)
                    dsb = (pr * (dp - dlt_h)).astype(bf16)
                    dv = dv + _dot_tn(pr.astype(bf16), doh)
                    dk = dk + _dot_tn(dsb, qh)
                    dq = dq + _dot(dsb, kh)
                    if hh == 1:
                        dqp[pl.ds(r0, BAND), :] = dq * 0.125
                        dkp[pl.ds(k0, 2 * BAND), :] += dk
                        dvp[pl.ds(k0, 2 * BAND), :] += dv
                return c

            lax.fori_loop(0, S // BAND // DIL_UNROLL, blk, 0)

            for acc, out, roped in ((dqp, dq_out[g], True), (dkp, dk_out[g], True), (dvp, dv_out[g], False)):
                if r == 1:
                    src = acc
                else:
                    _permute_out(nat, acc, r)
                    src = nat

                def emit(i, c, src=src, out=out, roped=roped):
                    r0 = pl.multiple_of(i * TQ, TQ)
                    d = src[pl.ds(r0, TQ), :]
                    if roped:
                        d = (d * c_ref[pl.ds(r0, TQ), :] + pltpu.roll(d * s1_ref[pl.ds(r0, TQ), :], 8, 1)
                             + pltpu.roll(d * s2_ref[pl.ds(r0, TQ), :], 120, 1))
                    out[pl.ds(r0, TQ), :] = d.astype(bf16)
                    return c

                lax.fori_loop(0, S // TQ, emit, 0)

    pair = pl.BlockSpec((S, 128), lambda p: (0, p))
    tab = pl.BlockSpec((S, 128), lambda p: (0, 0))
    blk_spec = pl.BlockSpec((S, DIL_BLK), lambda p: (0, p))
    return _call(
        order, body, [qkvb] * 9 + [dob, ob, lseb, c_t, s1_t, s2_t, dproj], name="dil_bwd", grid=(2,),
        in_specs=_dil_in_specs() + [pair, pair, pair, tab, tab, tab, pl.BlockSpec(memory_space=pl.ANY)],
        out_specs=blk_spec,
        out_shape=jax.ShapeDtypeStruct((S, NP), bf16),
        input_output_aliases={15: 0},
        scratch_shapes=[pltpu.VMEM((S, 128), f32)] * 11,
        compiler_params=_params(("parallel",)),
    )


def _inproj_bwd(order, dproj, wt, x, dx2, g1):
    tm = 256

    def body(d_ref, w_ref, x_ref, dx2_ref, g_ref, dx_ref, dg_ref):
        i = pl.program_id(0)
        dh = _dot(d_ref[...], w_ref[...])
        xb = x_ref[...]
        r = lax.rsqrt(jnp.mean(xb * xb, axis=-1, keepdims=True) + EPS)
        xh = xb * r
        gdh = dh * g_ref[...]
        dx_ref[...] = dx2_ref[...] + r * (gdh - xh * jnp.mean(gdh * xh, axis=-1, keepdims=True))

        @pl.when(i == 0)
        def _():
            dg_ref[...] = jnp.zeros_like(dg_ref)

        dg_ref[...] += jnp.sum(dh * xh, axis=0, keepdims=True)

    row = pl.BlockSpec((tm, D), lambda i: (i, 0))
    vec = pl.BlockSpec((1, D), lambda i: (0, 0))
    return _call(
        order, body, (dproj, wt, x, dx2, g1), name="inproj_bwd", grid=(S // tm,),
        in_specs=[pl.BlockSpec((tm, NP), lambda i: (i, 0)), pl.BlockSpec((NP, D), lambda i: (0, 0)), row, row, vec],
        out_specs=[row, vec],
        out_shape=[jax.ShapeDtypeStruct((S, D), f32), jax.ShapeDtypeStruct((1, D), f32)],
        compiler_params=_params(("arbitrary",)),
    )


HBM = pl.BlockSpec(memory_space=pltpu.HBM)
SEM = pl.BlockSpec(memory_space=pltpu.SEMAPHORE)
SMALL_ROWS = 8


def _comm_call(name, body, bufs, order, sems_in=(), new_sems=(), behind=()):
    nb, ns, nn = len(bufs), len(sems_in), len(new_sems)
    extra = order.token_for(bufs) + list(behind)

    def kern(*refs):
        off = nb + ns + len(extra)
        body(refs[:nb], refs[nb:nb + ns], refs[off:off + nn])
        refs[-1][...] = jnp.zeros((8, 128), f32)

    res = pl.pallas_call(
        kern, name=name,
        in_specs=[HBM] * nb + [SEM] * ns + [pl.BlockSpec(memory_space=pl.ANY)] * len(extra),
        out_specs=[SEM] * nn + [HBM] * nb + [pl.BlockSpec(memory_space=pltpu.VMEM)],
        out_shape=[pltpu.SemaphoreType.DMA((k,)) for k in new_sems] + [pltpu.HBM(b.shape, b.dtype) for b in bufs]
        + [jax.ShapeDtypeStruct((8, 128), f32)],
        input_output_aliases={i: nn + i for i in range(nb)},
        compiler_params=pltpu.CompilerParams(has_side_effects=pltpu.SideEffectType.DATAFLOW_SIDE_EFFECTING),
    )(*[pltpu.with_memory_space_constraint(b, pltpu.HBM) for b in bufs], *sems_in, *extra)
    order.mark(res[-1])
    return list(res[:nn]), list(res[nn:nn + nb])


def _place():
    x, y, c = lax.axis_index("x"), lax.axis_index("y"), lax.axis_index("c")
    chips = [(1 - x, y), (x, 1 - y), (1 - x, 1 - y)]
    return x, y, c, chips


def _rcopy(src, dst, ssem, rsem, dev):
    return pltpu.make_async_remote_copy(src_ref=src, dst_ref=dst, send_sem=ssem, recv_sem=rsem,
                                        device_id=dev, device_id_type=pl.DeviceIdType.MESH)


def _half(nrows, which):
    return pl.ds(which * (nrows // 2), nrows // 2)


def _ici_copies(stack, group_sizes, ssems, rsems):
    x, y, c, chips = _place()
    me_q = 2 * x + y
    sends, recvs = [], []
    a = 0
    for grp, size in enumerate(group_sizes):
        for k in range(size):
            rows = _half(stack[a].shape[1], c)
            for j, (cx, cy) in enumerate(chips):
                mine = stack[a].at[me_q, rows]
                sends.append(_rcopy(mine, mine, ssems[grp].at[k * 3 + j], rsems[grp].at[k * 3 + j], (cx, cy, c)))
                theirs = stack[a].at[2 * cx + cy, rows]
                recvs.append(_rcopy(theirs, theirs, ssems[grp].at[k * 3 + j], rsems[grp].at[k * 3 + j],
                                    (cx, cy, c)))
            a += 1
    return sends, recvs


def _allgather_start(name, stacks, order):
    n = len(stacks)

    def body(bufs, _, new):
        sends, _r = _ici_copies(bufs, [n], [new[0]], [new[1]])
        for cp in sends:
            cp.start()

    return _comm_call(name, body, stacks, order, new_sems=(3 * n, 3 * n))


def _forward_copies(stack, ssem, rsem):
    x, y, c, chips = _place()
    sib = (x, y, 1 - c)
    sends, recvs = [], []
    for a in range(len(stack)):
        for j, (cx, cy) in enumerate(chips):
            landed = stack[a].at[2 * cx + cy, _half(stack[a].shape[1], c)]
            sends.append(_rcopy(landed, landed, ssem.at[a * 3 + j], rsem.at[a * 3 + j], sib))
            other = stack[a].at[2 * cx + cy, _half(stack[a].shape[1], 1 - c)]
            recvs.append(_rcopy(other, other, ssem.at[a * 3 + j], rsem.at[a * 3 + j], sib))
    return sends, recvs


def _allgather_forward(name, stacks, sems, order, behind=()):
    n = len(stacks)

    def body(bufs, taken, new):
        sends, recvs = _ici_copies(bufs, [n], [taken[0]], [taken[1]])
        fwd, _r = _forward_copies(bufs, new[0], new[1])
        for arrived, onward in zip(recvs, fwd):
            arrived.wait_recv()
            onward.start()
        for cp in sends:
            cp.wait_send()

    return _comm_call(name, body, stacks, order, sems_in=sems, new_sems=(3 * n, 3 * n), behind=behind)


def _allgather_finish(name, stacks, sems, order):
    def body(bufs, taken, _):
        sends, recvs = _forward_copies(bufs, taken[0], taken[1])
        for cp in sends:
            cp.wait_send()
        for cp in recvs:
            cp.wait_recv()

    return _comm_call(name, body, stacks, order, sems_in=sems)[1]


def _window_unit(q, j):
    return C2I[WIN_UNIT0[q] + j]


def _pair_copies(g, t, ssem, rsem, gathered):
    x, y, c, _ = _place()
    sib = (x, y, 1 - c)
    cps, whole = [], []
    for a in range(len(g)):
        if a == 0 and gathered:
            for q in range(NCHIP):
                for j in range(WIN_UNITS // 2):
                    u = jnp.where(c == 0, _window_unit(q, WIN_UNITS // 2 + j), _window_unit(q, j))
                    src = g[0].at[pl.ds(pl.multiple_of(u * UNIT, UNIT), UNIT), :]
                    cps.append(_rcopy(src, t[0].at[q, pl.ds(j * UNIT, UNIT), :], ssem.at[0], rsem.at[0], sib))
            whole.append(_rcopy(t[0], t[0], ssem.at[0], rsem.at[0], sib))
        else:
            cp = _rcopy(g[a].at[:, _half(g[a].shape[1], 1 - c), :], t[a], ssem.at[a], rsem.at[a], sib)
            cps.append(cp)
            whole.append(cp)
    return cps, whole


def _comm_multi(name, parts, order):
    def body(buf_refs, taken, new):
        ib = it = inew = 0
        for pbody, pbufs, psems, pnew, _ in parts:
            pbody(buf_refs[ib:ib + len(pbufs)], taken[it:it + len(psems)], new[inew:inew + len(pnew)])
            ib, it, inew = ib + len(pbufs), it + len(psems), inew + len(pnew)

    sems, bufs = _comm_call(name, body, [b for p in parts for b in p[1]], order,
                            sems_in=[s for p in parts for s in p[2]], new_sems=[k for p in parts for k in p[3]])
    out, ib, inew = [], 0, 0
    for _, pbufs, _, pnew, unpack in parts:
        out.append(unpack(sems[inew:inew + len(pnew)], bufs[ib:ib + len(pbufs)]))
        ib, inew = ib + len(pbufs), inew + len(pnew)
    return out


def _pair_start_part(gs, gathered=False):
    n = len(gs)
    ts = [lax.empty((NCHIP, WIN_ROWS // 2, D) if (a == 0 and gathered) else (NCHIP, g.shape[1] // 2, g.shape[2]), f32)
          for a, g in enumerate(gs)]

    def body(bufs, _, new):
        for cp in _pair_copies(bufs[:n], bufs[n:], new[0], new[1], gathered)[0]:
            cp.start()

    return body, list(gs) + ts, (), (n, n), lambda sems, bufs: (sems, bufs)


def _pair_wait_part(bufs, sems, gathered=False):
    n = len(bufs) // 2

    def body(refs, taken, _):
        for cp in _pair_copies(refs[:n], refs[n:], taken[0], taken[1], gathered)[1]:
            cp.wait_send()
            cp.wait_recv()

    return body, list(bufs), list(sems), (), lambda _, out: (out[:n], out[n:])


def _row_tile(h):
    return min(h, 256)


def _pair_add(order, g, t, c_arr, name):
    _, R, C = g.shape
    h = R // 2
    tr = _row_tile(h)
    nblk = h // tr

    def body(c_ref, g_ref, t_ref, p32_ref, p16_ref):
        s = g_ref[...] + t_ref[...]
        p32_ref[...] = s
        p16_ref[...] = s.astype(bf16)

    blk = pl.BlockSpec((None, tr, C), lambda q, i, c_ref: (q, i, 0))
    return _call_indexed(
        order, body, (c_arr,), (g, t), (NCHIP, nblk),
        [pl.BlockSpec((None, tr, C), lambda q, i, c_ref: (q, c_ref[0] * nblk + i, 0)), blk], [blk, blk],
        name=name,
        out_shape=[jax.ShapeDtypeStruct((NCHIP, h, C), f32), jax.ShapeDtypeStruct((NCHIP, h, C), bf16)],
        compiler_params=_params(("parallel", "parallel")),
    )


def _pair_add_gathered(order, dwt, t, c_arr, name):
    half_units, half_rows = WIN_UNITS // 2, WIN_ROWS // 2
    table = jnp.asarray([_window_unit(q, j) for q in range(NCHIP) for j in range(WIN_UNITS)], jnp.int32)

    def body(tab_ref, c_ref, g_hbm, t_ref, p32_ref, p16_ref, buf, sem):
        q = pl.program_id(0)

        def gather(w, slot):
            cps = []
            for j in range(half_units):
                u = tab_ref[w * WIN_UNITS + c_ref[0] * half_units + j]
                cps.append(pltpu.make_async_copy(g_hbm.at[pl.ds(pl.multiple_of(u * UNIT, UNIT), UNIT), :],
                                                 buf.at[slot, pl.ds(j * UNIT, UNIT), :], sem.at[slot]))
            return cps

        @pl.when(q == 0)
        def _():
            for cp in gather(0, 0):
                cp.start()

        @pl.when(q + 1 < NCHIP)
        def _():
            for cp in gather(q + 1, (q + 1) % 2):
                cp.start()

        slot = q % 2
        pltpu.make_async_copy(buf.at[slot], buf.at[slot], sem.at[slot]).wait()
        s = buf[slot] + t_ref[...]
        p32_ref[...] = s
        p16_ref[...] = s.astype(bf16)

    blk = pl.BlockSpec((None, half_rows, D), lambda q, tab_ref, c_ref: (q, 0, 0))
    return _call_indexed(
        order, body, (table, c_arr), (dwt, t), (NCHIP,),
        [pl.BlockSpec(memory_space=pl.ANY), blk], [blk, blk],
        scratch_shapes=[pltpu.VMEM((2, half_rows, D), f32), pltpu.SemaphoreType.DMA((2,))],
        name=name,
        out_shape=[jax.ShapeDtypeStruct((NCHIP, half_rows, D), f32),
                   jax.ShapeDtypeStruct((NCHIP, half_rows, D), bf16)],
        compiler_params=_params(("arbitrary",)),
    )


def _shard_copies(p, r, sm, ssem, rsem):
    x, y, c, chips = _place()
    n = len(p)
    sends, recvs = [], []
    for a in range(n):
        for j, (cx, cy) in enumerate(chips):
            k = a * 3 + j
            sends.append(_rcopy(p[a].at[2 * cx + cy], r[a].at[j], ssem.at[k], rsem.at[k], (cx, cy, c)))
            recvs.append(_rcopy(r[a].at[j], r[a].at[j], ssem.at[k], rsem.at[k], (cx, cy, c)))
    if sm is not None:
        mine = sm.at[4 * x + 2 * y + c]
        for i in range(1, 8):
            px = (1 - x) if i & 4 else x
            py = (1 - y) if i & 2 else y
            pc = (1 - c) if i & 1 else c
            k = 3 * n + i - 1
            sends.append(_rcopy(mine, mine, ssem.at[k], rsem.at[k], (px, py, pc)))
            slot = sm.at[4 * px + 2 * py + pc]
            recvs.append(_rcopy(slot, slot, ssem.at[k], rsem.at[k], (px, py, pc)))
    return sends, recvs


def _shard_start_part(p16s, sm=None):
    n = len(p16s)
    rs = [lax.empty((3,) + p.shape[1:], bf16) for p in p16s]
    extra = [] if sm is None else [sm]
    nsem = 3 * n + (7 if sm is not None else 0)

    def body(bufs, _, new):
        sends, _r = _shard_copies(bufs[:n], bufs[n:2 * n], bufs[2 * n] if extra else None, new[0], new[1])
        for cp in sends:
            cp.start()

    return body, list(p16s) + rs + extra, (), (nsem, nsem), lambda sems, bufs: (sems, bufs)


def _shard_wait_part(bufs, sems, n):
    has_sm = len(bufs) > 2 * n

    def body(refs, taken, _):
        sends, recvs = _shard_copies(refs[:n], refs[n:2 * n], refs[2 * n] if has_sm else None, taken[0], taken[1])
        for cp in sends:
            cp.wait_send()
        for cp in recvs:
            cp.wait_recv()

    return body, list(bufs), list(sems), (), lambda _, out: (out[n:2 * n], (out[2 * n] if has_sm else None))


def _shard_sum(order, p32, r, q_arr, c_arr, name):
    _, h, C = p32.shape
    tr = _row_tile(h)
    nblk = h // tr

    def body(q_ref, c_ref, p_ref, r_ref, o_ref):
        s = p_ref[...]
        for j in range(3):
            s = s + r_ref[j].astype(f32)
        o_ref[...] = s

    return _call_indexed(
        order, body, (q_arr, c_arr), (p32, r), (nblk,),
        [pl.BlockSpec((None, tr, C), lambda i, q_ref, c_ref: (q_ref[0], i, 0)),
         pl.BlockSpec((3, tr, C), lambda i, q_ref, c_ref: (0, i, 0))],
        pl.BlockSpec((tr, C), lambda i, q_ref, c_ref: (c_ref[0] * nblk + i, 0)),
        name=name, out_shape=jax.ShapeDtypeStruct((2 * h, C), f32),
        compiler_params=_params(("parallel",)),
    )


def _swap_copies(full, ssem, rsem):
    x, y, c, _ = _place()
    sends, recvs = [], []
    for a in range(len(full)):
        mine = full[a].at[_half(full[a].shape[0], c)]
        sends.append(_rcopy(mine, mine, ssem.at[a], rsem.at[a], (x, y, 1 - c)))
        other = full[a].at[_half(full[a].shape[0], 1 - c)]
        recvs.append(_rcopy(other, other, ssem.at[a], rsem.at[a], (x, y, 1 - c)))
    return sends, recvs


def _swap_start_part(fulls):
    n = len(fulls)

    def body(bufs, _, new):
        for cp in _swap_copies(bufs, new[0], new[1])[0]:
            cp.start()

    return body, list(fulls), (), (n, n), lambda sems, bufs: (sems, bufs)


def _swap_wait_part(fulls, sems):
    def body(refs, taken, _):
        sends, recvs = _swap_copies(refs, taken[0], taken[1])
        for cp in sends:
            cp.wait_send()
        for cp in recvs:
            cp.wait_recv()

    return body, list(fulls), list(sems), (), lambda _, out: out


def _small_sum(order, sm):
    def body(sm_ref, o_ref):
        s = sm_ref[0]
        for d in range(1, 8):
            s = s + sm_ref[d]
        o_ref[...] = s

    return _call(order, body, (sm,), name="small_grad_sum", out_shape=jax.ShapeDtypeStruct((SMALL_ROWS, D), f32))


def _adamw_math(w, g, m, v):
    m = ADAM_B1 * m + (1.0 - ADAM_B1) * g
    v = ADAM_B2 * v + (1.0 - ADAM_B2) * (g * g)
    m_hat = m / (1.0 - ADAM_B1 ** ADAM_STEP)
    v_hat = v / (1.0 - ADAM_B2 ** ADAM_STEP)
    return -ADAM_LR * (m_hat / (jnp.sqrt(v_hat) + ADAM_EPS) + ADAM_WD * w), m, v


def _adamw_small(order, ws, gs, ms, vs, name):
    n = len(ws)

    def body(*refs):
        for i in range(n):
            res = _adamw_math(*[refs[k * n + i][...] for k in range(4)])
            for k in range(3):
                refs[4 * n + 3 * i + k][...] = res[k]

    out = _call(order, body, list(ws) + list(gs) + list(ms) + list(vs), name=name,
                out_shape=[jax.ShapeDtypeStruct(w.shape, f32) for w in ws for _ in range(3)])
    return [out[3 * i:3 * i + 3] for i in range(n)]


def _adamw(order, w, g, m, v, name):
    R, C = w.shape
    if R <= 256 or R % 256 == 0:
        tr, tc = min(R, 256), C
    else:
        tr, tc = R, 128

    def body(w_ref, g_ref, m_ref, v_ref, d_ref, nm_ref, nv_ref):
        d_ref[...], nm_ref[...], nv_ref[...] = _adamw_math(w_ref[...], g_ref[...], m_ref[...], v_ref[...])

    blk = pl.BlockSpec((tr, tc), lambda i, j: (i, j))
    return _call(
        order, body, (w, g, m, v), name=name, grid=(R // tr, C // tc), in_specs=[blk] * 4, out_specs=[blk] * 3,
        out_shape=[jax.ShapeDtypeStruct((R, C), f32)] * 3,
        compiler_params=_params(("parallel", "parallel")),
    )


def _feature_major(w):
    return jnp.transpose(w, (2, 0, 1)).reshape(SHARD_IN, D)


def _unfeature_major(a):
    return jnp.transpose(a.reshape(SHARD_IN, 1, D), (1, 2, 0))


def _window_of(wt, q):
    def plain(k):
        return lambda w: jnp.pad(w, ((OWN_ROW0[k], WIN_ROWS - OWN_ROW0[k] - SHARD_IN), (0, 0))).astype(bf16)

    def chip1(w):
        lo = jnp.pad(w[0:62], ((2, WIN_ROWS - 64), (0, 0)))
        hi = jnp.pad(w[70:SHARD_IN], ((64, WIN_ROWS - 64 - (SHARD_IN - 70)), (0, 0)))
        return (lo + hi).astype(bf16)

    win = lax.switch(q, [plain(0), chip1, plain(2), plain(3)], wt)
    fa = jnp.pad(wt[62:70], ((0, FA_ROWS - 8), (0, 0))).astype(bf16)
    return win, fa


def _own_rows(gwin, gfa, q):
    def plain(k):
        return lambda gw, gf: gw[OWN_ROW0[k]:OWN_ROW0[k] + SHARD_IN]

    def chip1(gw, gf):
        return (jnp.pad(gw[2:64], ((0, SHARD_IN - 62), (0, 0))) + jnp.pad(gf[0:8], ((62, SHARD_IN - 70), (0, 0)))
                + jnp.pad(gw[64:64 + SHARD_IN - 70], ((70, 0), (0, 0))))

    return lax.switch(q, [plain(0), chip1, plain(2), plain(3)], gwin, gfa)


def kernel(x, norm_attn_g, w_in, b_forget, w_branch_a, w_branch_b, w_out, norm_mlp_g, w_up, w_down, norm_final_g, loss_target, m_norm_attn_g, m_w_in, m_b_forget, m_w_branch_a, m_w_branch_b, m_w_out, m_norm_mlp_g, m_w_up, m_w_down, m_norm_final_g, v_norm_attn_g, v_w_in, v_b_forget, v_w_branch_a, v_w_branch_b, v_w_out, v_norm_mlp_g, v_w_up, v_w_down, v_norm_final_g):
    xi, yi, ci = lax.axis_index("x"), lax.axis_index("y"), lax.axis_index("c")
    q_me = 2 * xi + yi
    c_arr = jnp.reshape(ci, (1,)).astype(jnp.int32)
    q_arr = jnp.reshape(q_me, (1,)).astype(jnp.int32)
    x_, tgt = x[0], loss_target[0]

    names = ["w_branch_a", "w_branch_b", "w_out", "w_up", "w_down"]
    big = dict(zip(names, [w_branch_a[0], w_branch_b[0], w_out[0], w_up[0], w_down[0]]))
    ms = dict(zip(names, [m_w_branch_a[0], m_w_branch_b[0], m_w_out[0], m_w_up[0], m_w_down[0]]))
    vs = dict(zip(names, [v_w_branch_a[0], v_w_branch_b[0], v_w_out[0], v_w_up[0], v_w_down[0]]))
    grad, upd = {}, {}
    order = _Order()

    def run(fn, *args, **kw):
        return fn(order, *args, **kw)

    def own_slot(a):
        return lax.dynamic_update_slice(lax.empty((NCHIP,) + a.shape, a.dtype), a[None], (q_me, 0, 0))

    wt_own = _feature_major(w_in)
    win, fa_blk = _window_of(wt_own, q_me)
    sem_in, in_s = _allgather_start("allgather_start_in", [own_slot(win), own_slot(fa_blk)], order)
    sem_rest, rest = _allgather_start("allgather_start_rest", [own_slot(w.astype(bf16)) for w in big.values()], order)
    rope = _rope_tables()
    mt_own, vt_own = _feature_major(m_w_in), _feature_major(v_w_in)
    sem_f, in_s = _allgather_forward("allgather_forward_in", in_s, sem_in, order,
                                     behind=[wt_own, mt_own, vt_own, *rope])
    wins, fas = _allgather_finish("allgather_finish_in", in_s, sem_f, order)
    wt = run(_assemble_win, wins, fas)

    bpad = jnp.pad(b_forget, ((0, 0), (0, 120)))
    h1, qkvb, qkva, gates, fa = run(_norm_inproj, x_, norm_attn_g, wt, rope)
    F = run(_forget_cumsum, fa, bpad)
    oa, lsea = run(_fox_fwd, qkva, F)
    sem_f, rest = _allgather_forward("allgather_forward_rest", rest, sem_rest, order)
    ob, lseb = run(_dil_fwd, qkvb)
    was, wbs, wouts, wups, wdowns = _allgather_finish("allgather_finish_rest", rest, sem_f, order)
    wout = wouts.reshape(D, D)
    wdown = wdowns.reshape(DFF, D)
    ya, yb, mixed = run(_branch_mix, oa, ob, was, wbs, gates)
    x2, h2 = run(_outproj_norm, mixed, wout, x_, norm_mlp_g)
    u, a = run(_mlp_up, h2, wups)
    dx3, dx3b, dg3, loss_part = run(_mlp_down_loss, a, wdown, x2, norm_final_g.reshape(1, D), tgt)

    def comm(name, *parts):
        return _comm_multi(name, list(parts), order)

    def pair_adds(group, gs, ts):
        return zip(*[run(_pair_add, gs[i], ts[i], c_arr, "pair_add_" + nm) for i, nm in enumerate(group)])

    def shard_sums(group, p32s, rs):
        return [run(_shard_sum, p32s[i], rs[i], q_arr, c_arr, "shard_sum_" + nm) for i, nm in enumerate(group)]

    def adamw_group(group, fulls):
        for nm, gfull in zip(group, fulls):
            grad[nm] = gfull
            upd[nm] = run(_adamw, big[nm], gfull, ms[nm], vs[nm], "adamw_" + nm)

    grp_a, grp_b, grp_c = ["w_down", "w_up"], ["w_out", "w_branch_a", "w_branch_b"], ["w_in", "w_in_fa"]
    du = run(_mlp_down_bwd, dx3b, wdown, u)
    dwdown = run(_mm, a, dx3b, "tn", f32, 1024, D, "wgrad_down")
    dwup = run(_mm, h2, du, "tn", f32, D, 1024, "wgrad_up", stack_cols=True)
    ((sem_pa, buf_pa),) = comm("pair_start_a", _pair_start_part([dwdown.reshape(NCHIP, DFF // NCHIP, D), dwup]))
    dx2, dx2b, dg2 = run(_mlp_up_bwd, du, wups, x2, dx3, norm_mlp_g)
    ((gs, ts),) = comm("pair_wait_a", _pair_wait_part(buf_pa, sem_pa))
    p32_a, p16_a = pair_adds(grp_a, gs, ts)
    ((sem_sa, buf_sa),) = comm("shard_start_a", _shard_start_part(p16_a))
    dya, dyb, dproj = run(_gate_bwd, dx2b, wout, gates, ya, yb)
    dwout = run(_mm, mixed, dx2b, "tn", f32, D, D, "wgrad_out")
    doa, dob = run(_branch_bwd, dya, dyb, was, wbs)
    dwas, dwbs = run(_branch_wgrad, oa, ob, dya, dyb)
    ((sem_pb, buf_pb),) = comm("pair_start_b", _pair_start_part([dwout.reshape(NCHIP, D // NCHIP, D), dwas, dwbs]))
    dF, dproj = run(_fox_bwd, qkva, doa, oa, lsea, F, dproj)
    (gs, ts), (rs_a, _) = comm("pair_wait_b_shard_wait_a", _pair_wait_part(buf_pb, sem_pb),
                               _shard_wait_part(buf_sa, sem_sa, len(grp_a)))
    p32_b, p16_b = pair_adds(grp_b, gs, ts)
    fulls_a = shard_sums(grp_a, p32_a, rs_a)
    (sem_wa, fulls_a), (sem_sb, buf_sb) = comm("swap_start_a_shard_start_b", _swap_start_part(fulls_a),
                                               _shard_start_part(p16_b))
    dbf, dproj = run(_forget_bwd, dF, fa, bpad, dproj)
    dproj = run(_dil_bwd, qkvb, dob, ob, lseb, rope, dproj)
    (rs_b, _), fulls_a = comm("shard_wait_b_swap_wait_a", _shard_wait_part(buf_sb, sem_sb, len(grp_b)),
                              _swap_wait_part(fulls_a, sem_wa))
    fulls_b = shard_sums(grp_b, p32_b, rs_b)
    ((sem_wb, fulls_b),) = comm("swap_start_b", _swap_start_part(fulls_b))
    dwt = run(_mm, dproj, h1, "tn", f32, 512, D, "wgrad_in")
    dwfa = jnp.broadcast_to(dwt[F_FA:F_FA + FA_ROWS][None], (NCHIP, FA_ROWS, D))
    (sem_pc, buf_pc), fulls_b = comm("pair_start_c_swap_wait_b", _pair_start_part([dwt, dwfa], gathered=True),
                                     _swap_wait_part(fulls_b, sem_wb))
    adamw_group(grp_b, fulls_b)
    (((dwt_c, dwfa_c), (t_in, t_fa)),) = comm("pair_wait_c", _pair_wait_part(buf_pc, sem_pc, gathered=True))
    p32_in, p16_in = run(_pair_add_gathered, dwt_c, t_in, c_arr, "pair_add_w_in")
    p32_fa, p16_fa = run(_pair_add, dwfa_c, t_fa, c_arr, "pair_add_w_in_fa")
    ((sem_sc, buf_sc),) = comm("shard_start_c", _shard_start_part([p16_in, p16_fa]))
    gx, dg1 = run(_inproj_bwd, dproj, wt, x_, dx2, norm_attn_g)
    adamw_group(grp_a, fulls_a)
    small = jnp.concatenate([dg1, dg2, dg3, jnp.pad(dbf[:, 0:8], ((0, 0), (0, D - 8))),
                             jnp.pad(loss_part, ((0, 0), (0, D - 128))),
                             jnp.zeros((SMALL_ROWS - 5, D), f32)], axis=0)
    sm = lax.dynamic_update_slice(lax.empty((8, SMALL_ROWS, D), f32), small[None],
                                  (4 * xi + 2 * yi + ci, 0, 0))
    (sem_sm, buf_sm), (rs_c, _) = comm("small_start_shard_wait_c", _shard_start_part([], sm),
                                       _shard_wait_part(buf_sc, sem_sc, len(grp_c)))
    fulls_c = shard_sums(grp_c, [p32_in, p32_fa], rs_c)
    (sem_wc, fulls_c), (_, sm) = comm("swap_start_c_small_wait", _swap_start_part(fulls_c),
                                      _shard_wait_part(buf_sm, sem_sm, 0))
    gsmall = run(_small_sum, sm)
    loss = gsmall[4, 0]

    grad["norm_attn_g"], grad["norm_mlp_g"] = gsmall[0:1], gsmall[1:2]
    grad["norm_final_g"], grad["b_forget"] = gsmall[2:3], gsmall[3:4, 0:8]
    smalls = ["norm_attn_g", "norm_mlp_g", "norm_final_g", "b_forget"]
    res = run(_adamw_small, [norm_attn_g, norm_mlp_g, norm_final_g.reshape(1, D), b_forget],
              [grad[nm] for nm in smalls],
              [m_norm_attn_g, m_norm_mlp_g, m_norm_final_g.reshape(1, D), m_b_forget],
              [v_norm_attn_g, v_norm_mlp_g, v_norm_final_g.reshape(1, D), v_b_forget], "adamw_small")
    upd.update(zip(smalls, res))

    ((gwin, gfa),) = comm("swap_wait_c", _swap_wait_part(fulls_c, sem_wc))
    g_in = _own_rows(gwin, gfa, q_me)
    upd_in = run(_adamw, wt_own, g_in, mt_own, vt_own, "adamw_w_in")
    grad["w_in"] = _unfeature_major(g_in)
    upd["w_in"] = [_unfeature_major(t) for t in upd_in]

    order_out = ["norm_attn_g", "w_in", "b_forget", "w_branch_a", "w_branch_b", "w_out", "norm_mlp_g", "w_up",
                 "w_down", "norm_final_g"]
    shapes = dict(norm_attn_g=norm_attn_g.shape, w_in=w_in.shape, b_forget=b_forget.shape,
                  w_branch_a=w_branch_a.shape, w_branch_b=w_branch_b.shape, w_out=w_out.shape,
                  norm_mlp_g=norm_mlp_g.shape, w_up=w_up.shape, w_down=w_down.shape, norm_final_g=norm_final_g.shape)
    outs = [loss, gx.reshape(x.shape)]
    outs += [grad[nm].reshape(shapes[nm]) for nm in order_out]
    for k in range(3):
        outs += [upd[nm][k].reshape(shapes[nm]) for nm in order_out]
    return tuple(outs)
```

```python
import jax
import jax.numpy as jnp
from jax import lax
from jax.experimental import pallas as pl
from jax.experimental.pallas import tpu as pltpu

f32 = jnp.float32
bf16 = jnp.bfloat16

S = 2048
D = 1024
DFF = 4096
HD = 64
FOXW = 512
DILOUT = 256
DIL = (1, 4, 16)
BAND = 128
EPS = 1e-6
NEG = -1e30
ROPE_THETA = 500000.0
NCHIP = 4
TQ = 256

ADAM_LR, ADAM_B1, ADAM_B2, ADAM_EPS, ADAM_WD, ADAM_STEP = 0.001, 0.9, 0.999, 1e-08, 0.01, 10
VMEM_LIMIT = 56 * 1024 * 1024

UNIT = 64
NP = 6144
F_DIL, F_FOX, F_FA, F_G = 0, 2304, 3840, 4096
DIL_BLK, FOX_BLK = 1152, 384
WIN_UNITS, WIN_ROWS = 24, 1536
WIN_UNIT0 = (0, 23, 45, 68)
OWN_ROW0 = (0, 2, 60, 62)
SHARD_IN = 1474
FA_ROWS = 32


def _compact_to_internal():
    c2i = {}
    for p in range(2):
        for role in range(3):
            for g in range(3):
                for hh in range(2):
                    c2i[24 + 12 * role + 4 * g + 2 * p + hh] = 18 * p + 6 * role + 2 * g + hh
    for p in range(4):
        for role in range(3):
            for hh in range(2):
                c2i[8 * role + 2 * p + hh] = F_FOX // UNIT + 6 * p + 2 * role + hh
    for j in range(32):
        c2i[60 + j] = F_G // UNIT + j
    return c2i


C2I = _compact_to_internal()
OVERLAP_UNITS = (23, 45, 46, 68)


def _params(sem=None):
    return pltpu.CompilerParams(dimension_semantics=sem, vmem_limit_bytes=VMEM_LIMIT)


class _Order:
    def __init__(self):
        self.tok = None

    def mark(self, v):
        self.tok = v

    def token_for(self, args):
        return [] if self.tok is None or any(self.tok is a for a in args) else [self.tok]


def _call(order, body, args, in_specs=None, **kw):
    args = list(args)
    n_in = len(args)
    if in_specs is None:
        in_specs = [pl.BlockSpec(memory_space=pltpu.VMEM)] * n_in
    kern = body
    extra = order.token_for(args)
    if extra:
        in_specs = list(in_specs) + [pl.BlockSpec(memory_space=pl.ANY)]

        def kern(*refs):
            body(*refs[:n_in], *refs[n_in + 1:])

    out = pl.pallas_call(kern, in_specs=in_specs, **kw)(*args, *extra)
    order.mark(out[0] if isinstance(out, (tuple, list)) else out)
    return out


def _call_indexed(order, body, scalars, args, grid, in_specs, out_specs, scratch_shapes=(), **kw):
    args, in_specs = list(args), list(in_specs)
    n_front = len(scalars) + len(args)
    kern = body
    extra = order.token_for(args)
    if extra:
        in_specs.append(pl.BlockSpec(memory_space=pl.ANY))

        def kern(*refs):
            body(*refs[:n_front], *refs[n_front + 1:])

    out = pl.pallas_call(
        kern, grid_spec=pltpu.PrefetchScalarGridSpec(num_scalar_prefetch=len(scalars), grid=grid, in_specs=in_specs,
                                                     out_specs=out_specs, scratch_shapes=scratch_shapes),
        **kw)(*scalars, *args, *extra)
    order.mark(out[0] if isinstance(out, (tuple, list)) else out)
    return out


def _dot(a, b):
    return jnp.dot(a, b, preferred_element_type=f32)


def _dot_nt(a, b):
    return lax.dot_general(a, b, (((1,), (1,)), ((), ())), preferred_element_type=f32)


def _dot_tn(a, b):
    return lax.dot_general(a, b, (((0,), (0,)), ((), ())), preferred_element_type=f32)


def _split3(x):
    hi = x.astype(bf16)
    r1 = x - hi.astype(f32)
    mid = r1.astype(bf16)
    lo = (r1 - mid.astype(f32)).astype(bf16)
    return hi, mid, lo


def _rope_tables():
    half = 8
    inv_freq = jnp.power(jnp.float32(ROPE_THETA), -jnp.arange(half, dtype=f32) * 2.0 / 16)
    ang = jnp.arange(S).astype(f32)[:, None] * inv_freq[None, :]
    cos, sin = jnp.cos(ang), jnp.sin(ang)
    one = jnp.ones((S, HD - 16), f32)
    zero = jnp.zeros((S, HD - 16), f32)
    z8 = jnp.zeros((S, 8), f32)
    c = jnp.concatenate([cos, cos, one], axis=1)
    s1 = jnp.concatenate([-sin, z8, zero], axis=1)
    s2 = jnp.concatenate([z8, sin, zero], axis=1)
    return tuple(jnp.concatenate([t, t], axis=1) for t in (c, s1, s2))


def _mm(order, a, b, mode, out_dtype, tm, tn, name, stack_cols=False):
    if mode == "nn":
        (M, K), (_, N) = a.shape, b.shape
        a_spec = pl.BlockSpec((tm, K), lambda i, j: (i, 0))
        b_spec = pl.BlockSpec((K, tn), lambda i, j: (0, j))
        dot = _dot
    elif mode == "nt":
        (M, K), (N, _) = a.shape, b.shape
        a_spec = pl.BlockSpec((tm, K), lambda i, j: (i, 0))
        b_spec = pl.BlockSpec((tn, K), lambda i, j: (j, 0))
        dot = _dot_nt
    else:
        (K, M), (_, N) = a.shape, b.shape
        a_spec = pl.BlockSpec((K, tm), lambda i, j: (0, i))
        b_spec = pl.BlockSpec((K, tn), lambda i, j: (0, j))
        dot = _dot_tn

    def body(a_ref, b_ref, o_ref):
        o_ref[...] = dot(a_ref[...], b_ref[...]).astype(out_dtype)

    if stack_cols:
        assert tm == M
        out_spec = pl.BlockSpec((None, tm, tn), lambda i, j: (j, 0, 0))
        out_shape = jax.ShapeDtypeStruct((N // tn, M, tn), out_dtype)
    else:
        out_spec = pl.BlockSpec((tm, tn), lambda i, j: (i, j))
        out_shape = jax.ShapeDtypeStruct((M, N), out_dtype)
    return _call(
        order, body, (a, b), name=name, grid=(M // tm, N // tn), in_specs=[a_spec, b_spec],
        out_specs=out_spec, out_shape=out_shape,
        compiler_params=_params(("parallel", "parallel")),
    )


def _assemble_win(order, wins, fas):
    def body(win_ref, fa_ref, o_ref):
        q = pl.program_id(0)

        @pl.when(q == 0)
        def _():
            o_ref[...] = jnp.zeros_like(o_ref)

        for k in range(NCHIP):
            @pl.when(q == k)
            def _(k=k):
                for j in range(WIN_UNITS):
                    cu = WIN_UNIT0[k] + j
                    dst = pl.ds(C2I[cu] * UNIT, UNIT)
                    if cu in OVERLAP_UNITS:
                        o_ref[dst, :] += win_ref[j * UNIT:(j + 1) * UNIT, :]
                    else:
                        o_ref[dst, :] = win_ref[j * UNIT:(j + 1) * UNIT, :]
                if k == 1:
                    o_ref[F_FA:F_FA + FA_ROWS, :] = fa_ref[...]

    return _call(
        order, body, (wins, fas), name="assemble_w_in", grid=(NCHIP,),
        in_specs=[pl.BlockSpec((None, WIN_ROWS, D), lambda q: (q, 0, 0)),
                  pl.BlockSpec((None, FA_ROWS, D), lambda q: (1, 0, 0))],
        out_specs=pl.BlockSpec((NP, D), lambda q: (0, 0)),
        out_shape=jax.ShapeDtypeStruct((NP, D), bf16),
        compiler_params=_params(("arbitrary",)),
    )


def _norm_inproj(order, x, g1, wt, rope):
    tm = 256
    c_t, s1_t, s2_t = rope

    def body(x_ref, g_ref, w_ref, c_ref, s1_ref, s2_ref, h_ref, qkvb_ref, qkva_ref, gates_ref, fa_ref):
        xb = x_ref[...]
        r = lax.rsqrt(jnp.mean(xb * xb, axis=-1, keepdims=True) + EPS)
        h = ((xb * r) * g_ref[...]).astype(bf16)
        h_ref[...] = h
        c, s1, s2 = c_ref[...], s1_ref[...], s2_ref[...]
        for p in range(2):
            pb = _dot_nt(h, w_ref[F_DIL + p * DIL_BLK:F_DIL + (p + 1) * DIL_BLK, :])
            for ch in range(DIL_BLK // 128):
                pc = pb[:, ch * 128:(ch + 1) * 128]
                if ch < 6:
                    pc = pc * c + pltpu.roll(pc, 120, 1) * s1 + pltpu.roll(pc, 8, 1) * s2
                qkvb_ref[:, p * DIL_BLK + ch * 128:p * DIL_BLK + (ch + 1) * 128] = pc
        qkva_ref[...] = _dot_nt(h, w_ref[F_FOX:F_FA, :]).astype(bf16)
        fa_ref[...] = _dot_nt(h, w_ref[F_FA:F_FA + 128, :])
        gates_ref[...] = _dot_nt(h, w_ref[F_G:NP, :]).astype(bf16)

    row = lambda w: pl.BlockSpec((tm, w), lambda i: (i, 0))
    return _call(
        order, body, (x, g1, wt, c_t, s1_t, s2_t), name="norm_inproj", grid=(S // tm,),
        in_specs=[row(D), pl.BlockSpec((1, D), lambda i: (0, 0)), pl.BlockSpec((NP, D), lambda i: (0, 0)),
                  row(128), row(128), row(128)],
        out_specs=[row(D), row(2 * DIL_BLK), row(4 * FOX_BLK), row(2 * D), row(128)],
        out_shape=[jax.ShapeDtypeStruct((S, D), bf16), jax.ShapeDtypeStruct((S, 2 * DIL_BLK), f32),
                   jax.ShapeDtypeStruct((S, 4 * FOX_BLK), bf16), jax.ShapeDtypeStruct((S, 2 * D), bf16),
                   jax.ShapeDtypeStruct((S, 128), f32)],
        compiler_params=_params(("parallel",)),
    )


def _forget_cumsum(order, fa, bpad):
    nb = S // TQ

    def body(fa_ref, b_ref, F_ref):
        rr = lax.broadcasted_iota(jnp.int32, (TQ, TQ), 0)
        cc = lax.broadcasted_iota(jnp.int32, (TQ, TQ), 1)
        tri = (rr >= cc).astype(bf16)
        lane = lax.broadcasted_iota(jnp.int32, (1, 128), 1)
        carry = jnp.zeros((1, 128), f32)
        for b in range(nb):
            z = fa_ref[b * TQ:(b + 1) * TQ, :] + b_ref[...]
            lf = jnp.minimum(z, 0.0) - jnp.log(1.0 + jnp.exp(-jnp.abs(z)))
            lf = jnp.where(lane < 8, lf, 0.0)
            hi, mid, lo = _split3(lf)
            fb = (_dot(tri, hi) + _dot(tri, mid)) + _dot(tri, lo) + carry
            F_ref[b * TQ:(b + 1) * TQ, :] = fb
            carry = fb[TQ - 1:TQ, :]

    return _call(
        order, body, (fa, bpad), name="forget_cumsum",
        out_shape=jax.ShapeDtypeStruct((S, 128), f32),
        compiler_params=_params(),
    )


def _head_masks():
    lane = lax.broadcasted_iota(jnp.int32, (1, 128), 1)
    return lane, (lane < HD, lane >= HD)


L_FT, L_ONE = 0, 3
FOX_TQ, FOX_TK = 256, 512


def _set_lanes(x, lane, first, cols):
    for n, col in enumerate(cols):
        x = jnp.where(lane == first + n, col, x)
    return x


def _f32_parts(col):
    return [t.astype(f32) for t in _split3(col)]


def _fox_operands(qkv_ref, F_ref, lse_ref, qa, ka, p, rows):
    lane, hm = _head_masks()
    q = qkv_ref[rows, 0:128].astype(f32) * 0.125
    k = qkv_ref[rows, 128:256].astype(f32)
    Fb = F_ref[rows, :]
    for hh in (0, 1):
        free = (1 - hh) * HD
        fcol = jnp.sum(jnp.where(lane == 2 * p + hh, Fb, 0.0), axis=1, keepdims=True)
        qterm = fcol if lse_ref is None else fcol - lse_ref[rows, hh * HD:hh * HD + 1]
        qcols = _f32_parts(qterm) + [1.0] * 3
        kcols = [1.0] * 3 + [-t for t in _f32_parts(fcol)]
        qa[hh, rows, :] = _set_lanes(jnp.where(hm[hh], q, 0.0), lane, free, qcols).astype(bf16)
        ka[hh, rows, :] = _set_lanes(k, lane, free, kcols).astype(bf16)


def _fox_fwd(order, qkva, F):
    tq, tk = FOX_TQ, FOX_TK

    def body(qkv_ref, F_ref, o_ref, lse_ref, qa, ka, vt):
        p = pl.program_id(0)
        keyi = lax.broadcasted_iota(jnp.int32, (tk, 1), 0)
        qryi = lax.broadcasted_iota(jnp.int32, (1, tq), 1)
        sub = lax.broadcasted_iota(jnp.int32, (128, 1), 0)

        def prep(i, c):
            rows = pl.ds(pl.multiple_of(i * tk, tk), tk)
            _fox_operands(qkv_ref, F_ref, None, qa, ka, p, rows)
            vt[i] = qkv_ref[rows, 256:384].astype(f32).T.astype(bf16)
            return c

        lax.fori_loop(0, S // tk, prep, 0)

        def qblock(i, first_half):
            r0 = pl.multiple_of(i * tq, tq)
            qh = [qa[hh, pl.ds(r0, tq), :] for hh in (0, 1)]

            def kv(jb, carry, masked, width):
                keys = pl.ds(pl.multiple_of(jb * tk, tk), width)
                sts = [_dot_nt(ka[hh, keys, :], qh[hh]) for hh in (0, 1)]
                new = []
                for hh in (0, 1):
                    m, l, a = carry[3 * hh:3 * hh + 3]
                    st = sts[hh]
                    if masked:
                        st = jnp.where(jb * tk + keyi[0:width] <= r0 + qryi, st, NEG)
                    mn = jnp.maximum(m, jnp.max(st, axis=0, keepdims=True))
                    al = jnp.exp(m - mn)
                    pt = jnp.exp(st - mn)
                    l = al * l + jnp.sum(pt, axis=0, keepdims=True)
                    a = al * a + _dot(vt[jb, hh * HD:(hh + 1) * HD, 0:width], pt.astype(bf16))
                    new += [mn, l, a]
                return tuple(new)

            init = (jnp.full((1, tq), NEG, f32), jnp.zeros((1, tq), f32), jnp.zeros((HD, tq), f32)) * 2
            last = (r0 + tq - 1) // tk
            carry = lax.fori_loop(0, last, lambda j, cr: kv(j, cr, False, tk), init)
            m0, l0, a0, m1, l1, a1 = kv(last, carry, True, tk // 2 if first_half else tk)
            ot = jnp.concatenate([a0 / l0, a1 / l1], axis=0)
            lt = jnp.where(sub < HD, m0 + jnp.log(l0), m1 + jnp.log(l1))
            o_ref[pl.ds(r0, tq), :] = ot.T.astype(bf16)
            lse_ref[pl.ds(r0, tq), :] = lt.T

        def qpair(t, c):
            qblock(2 * t, True)
            qblock(2 * t + 1, False)
            return c

        assert tk == 2 * tq
        lax.fori_loop(0, S // tk, qpair, 0)

    pair = pl.BlockSpec((S, 128), lambda p: (0, p))
    return _call(
        order, body, (qkva, F), name="fox_fwd", grid=(4,),
        in_specs=[pl.BlockSpec((S, FOX_BLK), lambda p: (0, p)), pl.BlockSpec((S, 128), lambda p: (0, 0))],
        out_specs=[pair, pair],
        out_shape=[jax.ShapeDtypeStruct((S, FOXW), bf16), jax.ShapeDtypeStruct((S, FOXW), f32)],
        scratch_shapes=[pltpu.VMEM((2, S, 128), bf16)] * 2 + [pltpu.VMEM((S // tk, 128, tk), bf16)],
        compiler_params=_params(("parallel",)),
    )


def _permute_in(dst, src, r):
    L = S // r
    for rho in range(r):
        dst[rho * L:(rho + 1) * L, :] = src[pl.ds(rho, L, stride=r), :]


def _permute_out(dst, src, r):
    L = S // r
    for rho in range(r):
        dst[pl.ds(rho, L, stride=r), :] = src[rho * L:(rho + 1) * L, :]


def _band_geometry(bb, nbl):
    r0 = pl.multiple_of(bb * BAND, BAND)
    k0 = pl.multiple_of(jnp.maximum(bb - 1, 0) * BAND, BAND)
    sub0 = (bb - lax.rem(bb, nbl)) * BAND
    qi = r0 + lax.broadcasted_iota(jnp.int32, (BAND, 1), 0)
    ki = k0 + lax.broadcasted_iota(jnp.int32, (1, 2 * BAND), 1)
    diff = qi - ki
    valid = (diff >= 0) & (diff <= BAND) & (ki >= sub0)
    return r0, k0, valid


def _dil_views(ref):
    return [[ref.at[:, pl.ds((3 * role + g) * 128, 128)] for g in range(3)] for role in range(3)]


DIL_UNROLL = 4


def _dil_in_specs():
    return [pl.BlockSpec((S, 128), lambda p, k=k: (0, 9 * p + k)) for k in range(9)]


def _dil_fwd(order, qkvb):
    def body(*refs):
        q_refs, k_refs, v_refs = refs[0:3], refs[3:6], refs[6:9]
        ob_ref, lse_ref, qp, kp, vp, op, lp = refs[9:16]
        on, ln = refs[16:19], refs[19:22]
        _, hm = _head_masks()
        for g, r in enumerate(DIL):
            nbl = S // r // BAND
            if r == 1:
                qs_, ks_, vs_, od, ld = q_refs[g], k_refs[g], v_refs[g], on[g], ln[g]
            else:
                _permute_in(qp, q_refs[g], r)
                _permute_in(kp, k_refs[g], r)
                _permute_in(vp, v_refs[g], r)
                qs_, ks_, vs_, od, ld = qp, kp, vp, op, lp

            def blk(t, c, qs_=qs_, ks_=ks_, vs_=vs_, od=od, ld=ld, nbl=nbl):
                work = []
                for u in range(DIL_UNROLL):
                    r0, k0, valid = _band_geometry(DIL_UNROLL * t + u, nbl)
                    q = qs_[pl.ds(r0, BAND), :] * 0.125
                    kw = ks_[pl.ds(k0, 2 * BAND), :].astype(bf16)
                    vw = vs_[pl.ds(k0, 2 * BAND), :]
                    for hh in (0, 1):
                        qh = jnp.where(hm[hh], q, 0.0).astype(bf16)
                        work.append((u, hh, r0, valid, vw, _dot_nt(qh, kw)))
                o = [jnp.zeros((BAND, 128), f32)] * DIL_UNROLL
                lse = [jnp.zeros((BAND, 128), f32)] * DIL_UNROLL
                for u, hh, r0, valid, vw, s in work:
                    s = jnp.where(valid, s, NEG)
                    m = jnp.max(s, axis=1, keepdims=True)
                    pr = jnp.exp(s - m)
                    l = jnp.sum(pr, axis=1, keepdims=True)
                    vm = jnp.where(hm[hh], vw, 0.0).astype(bf16)
                    o[u] = o[u] + _dot((pr / l).astype(bf16), vm)
                    lse[u] = jnp.where(hm[hh], m + jnp.log(l), lse[u])
                    if hh == 1:
                        od[pl.ds(r0, BAND), :] = o[u]
                        ld[pl.ds(r0, BAND), :] = lse[u]
                return c

            lax.fori_loop(0, S // BAND // DIL_UNROLL, blk, 0)
            if r != 1:
                _permute_out(on[g], op, r)
                _permute_out(ln[g], lp, r)

        def combine(i, c):
            r0 = pl.multiple_of(i * TQ, TQ)
            ls = [ln[g][pl.ds(r0, TQ), :] for g in range(3)]
            mx = jnp.maximum(jnp.maximum(ls[0], ls[1]), ls[2])
            es = [jnp.exp(l - mx) for l in ls]
            tot = (es[0] + es[1]) + es[2]
            acc = (es[0] / tot) * on[0][pl.ds(r0, TQ), :]
            acc = acc + (es[1] / tot) * on[1][pl.ds(r0, TQ), :]
            acc = acc + (es[2] / tot) * on[2][pl.ds(r0, TQ), :]
            ob_ref[pl.ds(r0, TQ), :] = acc.astype(bf16)
            lse_ref[pl.ds(r0, TQ), :] = mx + jnp.log(tot)
            return c

        lax.fori_loop(0, S // TQ, combine, 0)

    out_blk = pl.BlockSpec((S, 128), lambda p: (0, p))
    return _call(
        order, body, [qkvb] * 9, name="dil_fwd", grid=(2,),
        in_specs=_dil_in_specs(), out_specs=[out_blk, out_blk],
        out_shape=[jax.ShapeDtypeStruct((S, DILOUT), bf16), jax.ShapeDtypeStruct((S, DILOUT), f32)],
        scratch_shapes=[pltpu.VMEM((S, 128), f32)] * 11,
        compiler_params=_params(("parallel",)),
    )


def _branch_mix(order, oa, ob, was, wbs, gates):
    tm = 512

    def body(oa_ref, ob_ref, wa_ref, wb_ref, g_ref, ya_ref, yb_ref, mix_ref):
        oa_b, ob_b = oa_ref[...], ob_ref[...]
        for q in range(NCHIP):
            cols = slice(q * 256, (q + 1) * 256)
            ya = _dot(oa_b, wa_ref[q])
            yb = _dot(ob_b, wb_ref[q])
            ya_ref[:, cols] = ya.astype(bf16)
            yb_ref[:, cols] = yb.astype(bf16)
            ga = g_ref[:, q * 256:(q + 1) * 256].astype(f32)
            gb = g_ref[:, D + q * 256:D + (q + 1) * 256].astype(f32)
            mix_ref[:, cols] = (jax.nn.sigmoid(ga) * ya + jax.nn.sigmoid(gb) * yb).astype(bf16)

    row = lambda w: pl.BlockSpec((tm, w), lambda i: (i, 0))
    full3 = lambda a: pl.BlockSpec(a.shape, lambda i: (0, 0, 0))
    return _call(
        order, body, (oa, ob, was, wbs, gates), name="branch_mix", grid=(S // tm,),
        in_specs=[row(FOXW), row(DILOUT), full3(was), full3(wbs), row(2 * D)],
        out_specs=[row(D), row(D), row(D)],
        out_shape=[jax.ShapeDtypeStruct((S, D), bf16), jax.ShapeDtypeStruct((S, D), bf16),
                   jax.ShapeDtypeStruct((S, D), bf16)],
        compiler_params=_params(("parallel",)),
    )


def _outproj_norm(order, mixed, wout, x, g2):
    tm = 512

    def body(m_ref, w_ref, x_ref, g_ref, x2_ref, h2_ref):
        x2 = x_ref[...] + _dot(m_ref[...], w_ref[...])
        x2_ref[...] = x2
        r = lax.rsqrt(jnp.mean(x2 * x2, axis=-1, keepdims=True) + EPS)
        h2_ref[...] = ((x2 * r) * g_ref[...]).astype(bf16)

    row = pl.BlockSpec((tm, D), lambda i: (i, 0))
    return _call(
        order, body, (mixed, wout, x, g2), name="outproj_norm", grid=(S // tm,),
        in_specs=[row, pl.BlockSpec((D, D), lambda i: (0, 0)), row, pl.BlockSpec((1, D), lambda i: (0, 0))],
        out_specs=[row, row],
        out_shape=[jax.ShapeDtypeStruct((S, D), f32), jax.ShapeDtypeStruct((S, D), bf16)],
        compiler_params=_params(("parallel",)),
    )


def _mlp_up(order, h2, wups):
    tm = 1024

    def body(h_ref, w_ref, ru_ref, a_ref):
        ru = jnp.maximum(_dot(h_ref[...], w_ref[...]), 0.0)
        ru_ref[...] = ru.astype(bf16)
        a_ref[...] = (ru * ru).astype(bf16)

    out = pl.BlockSpec((tm, D), lambda q, i: (i, q))
    return _call(
        order, body, (h2, wups), name="mlp_up", grid=(NCHIP, S // tm),
        in_specs=[pl.BlockSpec((tm, D), lambda q, i: (i, 0)), pl.BlockSpec((None, D, D), lambda q, i: (q, 0, 0))],
        out_specs=[out, out],
        out_shape=[jax.ShapeDtypeStruct((S, DFF), bf16), jax.ShapeDtypeStruct((S, DFF), bf16)],
        compiler_params=_params(("parallel", "parallel")),
    )


def _mlp_down_loss(order, a, wdown, x2, g3, tgt):
    tm = 512

    def body(a_ref, w_ref, x2_ref, g_ref, t_ref, dx_ref, dxb_ref, dg_ref, loss_ref, acc):
        i, k = pl.program_id(0), pl.program_id(1)

        @pl.when(k == 0)
        def _():
            acc[...] = x2_ref[...]

        acc[...] += _dot(a_ref[...], w_ref[...])

        @pl.when((i == 0) & (k == 0))
        def _():
            dg_ref[...] = jnp.zeros_like(dg_ref)
            loss_ref[...] = jnp.zeros_like(loss_ref)

        @pl.when(k == NCHIP - 1)
        def _():
            x3 = acc[...]
            r = lax.rsqrt(jnp.mean(x3 * x3, axis=-1, keepdims=True) + EPS)
            xh = x3 * r
            g = g_ref[...]
            e = xh * g - t_ref[...]
            part = 0.5 * jnp.sum(jnp.mean(e * e, axis=-1, keepdims=True), axis=0, keepdims=True)
            dy = e * (1.0 / D)
            gdy = dy * g
            dx = r * (gdy - xh * jnp.mean(gdy * xh, axis=-1, keepdims=True))
            dx_ref[...] = dx
            dxb_ref[...] = dx.astype(bf16)
            dg_ref[...] += jnp.sum(dy * xh, axis=0, keepdims=True)
            loss_ref[...] += jnp.broadcast_to(part, (1, 128))

    row = pl.BlockSpec((tm, D), lambda i, k: (i, 0))
    vec = pl.BlockSpec((1, D), lambda i, k: (0, 0))
    return _call(
        order, body, (a, wdown, x2, g3, tgt), name="mlp_down_loss", grid=(S // tm, NCHIP),
        in_specs=[pl.BlockSpec((tm, D), lambda i, k: (i, k)), pl.BlockSpec((D, D), lambda i, k: (k, 0)), row, vec, row],
        out_specs=[row, row, vec, pl.BlockSpec((1, 128), lambda i, k: (0, 0))],
        out_shape=[jax.ShapeDtypeStruct((S, D), f32), jax.ShapeDtypeStruct((S, D), bf16),
                   jax.ShapeDtypeStruct((1, D), f32), jax.ShapeDtypeStruct((1, 128), f32)],
        scratch_shapes=[pltpu.VMEM((tm, D), f32)],
        compiler_params=_params(("arbitrary", "arbitrary")),
    )


def _mlp_down_bwd(order, dx3b, wdown, u):
    tm = 512

    def body(d_ref, w_ref, u_ref, du_ref):
        da = _dot_nt(d_ref[...], w_ref[...])
        du_ref[...] = (da * (2.0 * u_ref[...].astype(f32))).astype(bf16)

    return _call(
        order, body, (dx3b, wdown, u), name="mlp_down_bwd", grid=(S // tm, NCHIP),
        in_specs=[pl.BlockSpec((tm, D), lambda i, q: (i, 0)), pl.BlockSpec((D, D), lambda i, q: (q, 0)),
                  pl.BlockSpec((tm, D), lambda i, q: (i, q))],
        out_specs=pl.BlockSpec((tm, D), lambda i, q: (i, q)),
        out_shape=jax.ShapeDtypeStruct((S, DFF), bf16),
        compiler_params=_params(("parallel", "parallel")),
    )


def _mlp_up_bwd(order, du, wups, x2, dx3, g2):
    tm = 512

    def body(du_ref, w_ref, x2_ref, dx3_ref, g_ref, dx2_ref, dx2b_ref, dg_ref, acc):
        i, q = pl.program_id(0), pl.program_id(1)

        @pl.when(q == 0)
        def _():
            acc[...] = jnp.zeros_like(acc)

        acc[...] += _dot_nt(du_ref[...], w_ref[...])

        @pl.when((i == 0) & (q == 0))
        def _():
            dg_ref[...] = jnp.zeros_like(dg_ref)

        @pl.when(q == NCHIP - 1)
        def _():
            dh = acc[...]
            x2 = x2_ref[...]
            r = lax.rsqrt(jnp.mean(x2 * x2, axis=-1, keepdims=True) + EPS)
            xh = x2 * r
            gdh = dh * g_ref[...]
            dx2 = dx3_ref[...] + r * (gdh - xh * jnp.mean(gdh * xh, axis=-1, keepdims=True))
            dx2_ref[...] = dx2
            dx2b_ref[...] = dx2.astype(bf16)
            dg_ref[...] += jnp.sum(dh * xh, axis=0, keepdims=True)

    row = pl.BlockSpec((tm, D), lambda i, q: (i, 0))
    vec = pl.BlockSpec((1, D), lambda i, q: (0, 0))
    return _call(
        order, body, (du, wups, x2, dx3, g2), name="mlp_up_bwd", grid=(S // tm, NCHIP),
        in_specs=[pl.BlockSpec((tm, D), lambda i, q: (i, q)), pl.BlockSpec((None, D, D), lambda i, q: (q, 0, 0)),
                  row, row, vec],
        out_specs=[row, row, vec],
        out_shape=[jax.ShapeDtypeStruct((S, D), f32), jax.ShapeDtypeStruct((S, D), bf16),
                   jax.ShapeDtypeStruct((1, D), f32)],
        scratch_shapes=[pltpu.VMEM((tm, D), f32)],
        compiler_params=_params(("arbitrary", "arbitrary")),
    )


def _gate_bwd(order, dx2b, wout, gates, ya, yb):
    tm = 512

    def body(d_ref, w_ref, g_ref, ya_ref, yb_ref, dya_ref, dyb_ref, dproj_ref):
        dm = _dot_nt(d_ref[...], w_ref[...])
        sa = jax.nn.sigmoid(g_ref[:, 0:D].astype(f32))
        sb = jax.nn.sigmoid(g_ref[:, D:2 * D].astype(f32))
        dya_ref[...] = (dm * sa).astype(bf16)
        dyb_ref[...] = (dm * sb).astype(bf16)
        dproj_ref[:, 0:D] = (dm * ya_ref[...].astype(f32) * (sa * (1.0 - sa))).astype(bf16)
        dproj_ref[:, D:2 * D] = (dm * yb_ref[...].astype(f32) * (sb * (1.0 - sb))).astype(bf16)

    row = lambda w: pl.BlockSpec((tm, w), lambda i: (i, 0))
    return _call(
        order, body, (dx2b, wout, gates, ya, yb), name="gate_bwd", grid=(S // tm,),
        in_specs=[row(D), pl.BlockSpec((D, D), lambda i: (0, 0)), row(2 * D), row(D), row(D)],
        out_specs=[row(D), row(D), pl.BlockSpec((tm, 2 * D), lambda i: (i, F_G // (2 * D)))],
        out_shape=[jax.ShapeDtypeStruct((S, D), bf16), jax.ShapeDtypeStruct((S, D), bf16),
                   jax.ShapeDtypeStruct((S, NP), bf16)],
        compiler_params=_params(("parallel",)),
    )


def _branch_bwd(order, dya, dyb, was, wbs):
    tm = 512

    def body(dya_ref, dyb_ref, wa_ref, wb_ref, doa_ref, dob_ref):
        doa = jnp.zeros((tm, FOXW), f32)
        dob = jnp.zeros((tm, DILOUT), f32)
        for q in range(NCHIP):
            cols = slice(q * 256, (q + 1) * 256)
            doa = doa + _dot_nt(dya_ref[:, cols], wa_ref[q])
            dob = dob + _dot_nt(dyb_ref[:, cols], wb_ref[q])
        doa_ref[...] = doa.astype(bf16)
        dob_ref[...] = dob

    row = lambda w: pl.BlockSpec((tm, w), lambda i: (i, 0))
    full3 = lambda a: pl.BlockSpec(a.shape, lambda i: (0, 0, 0))
    return _call(
        order, body, (dya, dyb, was, wbs), name="branch_bwd", grid=(S // tm,),
        in_specs=[row(D), row(D), full3(was), full3(wbs)],
        out_specs=[row(FOXW), row(DILOUT)],
        out_shape=[jax.ShapeDtypeStruct((S, FOXW), bf16), jax.ShapeDtypeStruct((S, DILOUT), f32)],
        compiler_params=_params(("parallel",)),
    )


def _branch_wgrad(order, oa, ob, dya, dyb):
    def body(oa_ref, ob_ref, dya_ref, dyb_ref, dwa_ref, dwb_ref):
        dwa_ref[...] = _dot_tn(oa_ref[...], dya_ref[...])
        dwb_ref[...] = _dot_tn(ob_ref[...], dyb_ref[...])

    full = lambda w: pl.BlockSpec((S, w), lambda q: (0, 0))
    colq = pl.BlockSpec((S, 256), lambda q: (0, q))
    return _call(
        order, body, (oa, ob, dya, dyb), name="branch_wgrad", grid=(NCHIP,),
        in_specs=[full(FOXW), full(DILOUT), colq, colq],
        out_specs=[pl.BlockSpec((None, FOXW, 256), lambda q: (q, 0, 0)),
                   pl.BlockSpec((None, DILOUT, 256), lambda q: (q, 0, 0))],
        out_shape=[jax.ShapeDtypeStruct((NCHIP, FOXW, 256), f32), jax.ShapeDtypeStruct((NCHIP, DILOUT, 256), f32)],
        compiler_params=_params(("parallel",)),
    )


def _fox_bwd(order, qkva, doa, oa, lse, F, dproj):
    tq, tk = FOX_TQ, FOX_TK

    def body(qkv_ref, do_ref, o_ref, lse_ref, F_ref, _dproj_in, dF_ref, dqkv_ref, qa, ka, da, va, kat,
             dk_scr, dv_scr, dqt_scr):
        p = pl.program_id(0)
        lane, hm = _head_masks()
        keyi = lax.broadcasted_iota(jnp.int32, (tk, 1), 0)
        qryi = lax.broadcasted_iota(jnp.int32, (1, tq), 1)

        def prep(i, c):
            rows = pl.ds(pl.multiple_of(i * tk, tk), tk)
            _fox_operands(qkv_ref, F_ref, lse_ref, qa, ka, p, rows)
            do = do_ref[rows, :].astype(f32)
            prod = do * o_ref[rows, :].astype(f32)
            v = qkv_ref[rows, 256:384].astype(f32)
            for hh in (0, 1):
                free = (1 - hh) * HD
                delta = jnp.sum(jnp.where(hm[hh], prod, 0.0), axis=1, keepdims=True)
                da[hh, rows, :] = _set_lanes(jnp.where(hm[hh], do, 0.0), lane, free,
                                             [-t for t in _f32_parts(delta)]).astype(bf16)
                va[hh, rows, :] = _set_lanes(v, lane, free, [1.0] * 3).astype(bf16)
                kat[hh, i] = ka[hh, rows, :].astype(f32).T.astype(bf16)
                dk_scr[hh, rows, :] = jnp.zeros((tk, 128), f32)
                dv_scr[hh, rows, :] = jnp.zeros((tk, 128), f32)
            return c

        lax.fori_loop(0, S // tk, prep, 0)

        def qblock(i, first_half):
            r0 = pl.multiple_of(i * tq, tq)
            qrows = pl.ds(r0, tq)
            qh = [qa[hh, qrows, :] for hh in (0, 1)]
            dh = [da[hh, qrows, :] for hh in (0, 1)]
            dqt_scr[...] = jnp.zeros_like(dqt_scr)

            def kv(jb, c2, masked, width):
                keys = pl.ds(pl.multiple_of(jb * tk, tk), width)
                sts = [_dot_nt(ka[hh, keys, :], qh[hh]) for hh in (0, 1)]
                dps = [_dot_nt(va[hh, keys, :], dh[hh]) for hh in (0, 1)]
                for hh in (0, 1):
                    pt = jnp.exp(sts[hh])
                    if masked:
                        pt = jnp.where(jb * tk + keyi[0:width] <= r0 + qryi, pt, 0.0)
                    dsb = (pt * dps[hh]).astype(bf16)
                    dv_scr[hh, keys, :] += _dot(pt.astype(bf16), dh[hh])
                    dk_scr[hh, keys, :] += _dot(dsb, qh[hh])
                    dqt_scr[hh] += _dot(kat[hh, jb, :, 0:width], dsb)
                return c2

            last = (r0 + tq - 1) // tk
            lax.fori_loop(0, last, lambda j, c2: kv(j, c2, False, tk), 0)
            kv(last, 0, True, tk // 2 if first_half else tk)
            dq0, dq1 = dqt_scr[0].T, dqt_scr[1].T
            dqkv_ref[qrows, 0:128] = (jnp.where(hm[0], dq0, dq1) * 0.125).astype(bf16)
            dF_ref[qrows, :] = jnp.where(lane == 0, dq0[:, HD:HD + 1], jnp.where(lane == 1, dq1[:, 0:1], 0.0))

        def qpair(t, c):
            qblock(2 * t, True)
            qblock(2 * t + 1, False)
            return c

        assert tk == 2 * tq
        lax.fori_loop(0, S // tk, qpair, 0)

        def finish(i, c):
            rows = pl.ds(pl.multiple_of(i * tq, tq), tq)
            dk0, dk1 = dk_scr[0, rows, :], dk_scr[1, rows, :]
            dqkv_ref[rows, 128:256] = jnp.where(hm[0], dk0, dk1).astype(bf16)
            dqkv_ref[rows, 256:384] = jnp.where(hm[0], dv_scr[0, rows, :], dv_scr[1, rows, :]).astype(bf16)
            cs = jnp.where(lane == 0, dk0[:, HD + L_ONE:HD + L_ONE + 1],
                           jnp.where(lane == 1, dk1[:, L_ONE:L_ONE + 1], 0.0))
            dF_ref[rows, :] = dF_ref[rows, :] - cs
            return c

        lax.fori_loop(0, S // tq, finish, 0)

    pair = pl.BlockSpec((S, 128), lambda p: (0, p))
    return _call(
        order, body, (qkva, doa, oa, lse, F, dproj), name="fox_bwd", grid=(4,),
        in_specs=[pl.BlockSpec((S, FOX_BLK), lambda p: (0, p)), pair, pair, pair,
                  pl.BlockSpec((S, 128), lambda p: (0, 0)), pl.BlockSpec(memory_space=pl.ANY)],
        out_specs=[pair, pl.BlockSpec((S, FOX_BLK), lambda p: (0, F_FOX // FOX_BLK + p))],
        out_shape=[jax.ShapeDtypeStruct((S, FOXW), f32), jax.ShapeDtypeStruct((S, NP), bf16)],
        input_output_aliases={5: 1},
        scratch_shapes=[pltpu.VMEM((2, S, 128), bf16)] * 4 + [pltpu.VMEM((2, S // tk, 128, tk), bf16)]
        + [pltpu.VMEM((2, S, 128), f32)] * 2 + [pltpu.VMEM((2, 128, tq), f32)],
        compiler_params=_params(("parallel",)),
    )


def _forget_bwd(order, dF, fa, bpad, dproj):
    nb = S // TQ

    def body(dF_ref, fa_ref, b_ref, _dproj_in, db_ref, dfa_ref):
        rr = lax.broadcasted_iota(jnp.int32, (TQ, TQ), 0)
        cc = lax.broadcasted_iota(jnp.int32, (TQ, TQ), 1)
        upper = (cc >= rr).astype(bf16)
        lane = lax.broadcasted_iota(jnp.int32, (1, 128), 1)
        carry = jnp.zeros((1, 128), f32)
        db = jnp.zeros((1, 128), f32)
        for b in reversed(range(nb)):
            cols = jnp.zeros((TQ, 128), f32)
            for h in range(8):
                c0 = (h // 2) * 128 + h % 2
                cols = jnp.where(lane == h, dF_ref[b * TQ:(b + 1) * TQ, c0:c0 + 1], cols)
            dlf = carry
            for part in _split3(cols):
                dlf = dlf + _dot(upper, part)
            carry = carry + jnp.sum(cols, axis=0, keepdims=True)
            z = fa_ref[b * TQ:(b + 1) * TQ, :] + b_ref[...]
            dz = jnp.where(lane < 8, dlf * jax.nn.sigmoid(-z), 0.0)
            dfa_ref[b * TQ:(b + 1) * TQ, 0:128] = dz.astype(bf16)
            dfa_ref[b * TQ:(b + 1) * TQ, 128:256] = jnp.zeros((TQ, 128), bf16)
            db = db + jnp.sum(dz, axis=0, keepdims=True)
        db_ref[...] = db

    whole = lambda a: pl.BlockSpec(a.shape, lambda i: (0,) * a.ndim)
    return _call(
        order, body, (dF, fa, bpad, dproj), name="forget_bwd", grid=(1,),
        in_specs=[whole(dF), whole(fa), whole(bpad), pl.BlockSpec(memory_space=pl.ANY)],
        out_specs=[pl.BlockSpec((1, 128), lambda i: (0, 0)), pl.BlockSpec((S, 256), lambda i: (0, F_FA // 256))],
        out_shape=[jax.ShapeDtypeStruct((1, 128), f32), jax.ShapeDtypeStruct((S, NP), bf16)],
        input_output_aliases={3: 1},
        compiler_params=_params(("arbitrary",)),
    )


def _dil_bwd(order, qkvb, dob, ob, lseb, rope, dproj):
    c_t, s1_t, s2_t = rope

    def body(*refs):
        q_refs, k_refs, v_refs = refs[0:3], refs[3:6], refs[6:9]
        dob_ref, ob_ref, lse_ref, c_ref, s1_ref, s2_ref, _dproj_in, dqkv_ref = refs[9:17]
        qp, kp, vp, dop, lp, dlp, dln, dqp, dkp, dvp, nat = refs[17:28]
        dq_out, dk_out, dv_out = _dil_views(dqkv_ref)
        _, hm = _head_masks()

        def delta_rows(i, c):
            r0 = pl.multiple_of(i * TQ, TQ)
            prod = dob_ref[pl.ds(r0, TQ), :] * ob_ref[pl.ds(r0, TQ), :].astype(f32)
            d0 = jnp.sum(jnp.where(hm[0], prod, 0.0), axis=1, keepdims=True)
            d1 = jnp.sum(jnp.where(hm[1], prod, 0.0), axis=1, keepdims=True)
            dln[pl.ds(r0, TQ), :] = jnp.where(hm[0], d0, d1)
            return c

        lax.fori_loop(0, S // TQ, delta_rows, 0)

        for g, r in enumerate(DIL):
            nbl = S // r // BAND
            if r == 1:
                srcs = (q_refs[g], k_refs[g], v_refs[g], dob_ref, lse_ref, dln)
            else:
                for dst, src in ((qp, q_refs[g]), (kp, k_refs[g]), (vp, v_refs[g]), (dop, dob_ref),
                                 (lp, lse_ref), (dlp, dln)):
                    _permute_in(dst, src, r)
                srcs = (qp, kp, vp, dop, lp, dlp)
            dkp[...] = jnp.zeros_like(dkp)
            dvp[...] = jnp.zeros_like(dvp)

            def blk(t, c, srcs=srcs, nbl=nbl):
                qs_, ks_, vs_, dos_, ls_, dls_ = srcs
                work = []
                for u in range(DIL_UNROLL):
                    r0, k0, valid = _band_geometry(DIL_UNROLL * t + u, nbl)
                    q = qs_[pl.ds(r0, BAND), :] * 0.125
                    kwf = ks_[pl.ds(k0, 2 * BAND), :]
                    kw = kwf.astype(bf16)
                    vw = vs_[pl.ds(k0, 2 * BAND), :].astype(bf16)
                    do = dos_[pl.ds(r0, BAND), :]
                    lse = ls_[pl.ds(r0, BAND), :]
                    dlt = dls_[pl.ds(r0, BAND), :]
                    for hh in (0, 1):
                        qh = jnp.where(hm[hh], q, 0.0).astype(bf16)
                        doh = jnp.where(hm[hh], do, 0.0).astype(bf16)
                        kh = jnp.where(hm[hh], kwf, 0.0).astype(bf16)
                        work.append((u, hh, r0, k0, valid, qh, doh, kh, lse[:, hh * HD:hh * HD + 1],
                                     dlt[:, hh * HD:hh * HD + 1], _dot_nt(qh, kw), _dot_nt(doh, vw)))
                for u, hh, r0, k0, valid, qh, doh, kh, lse_h, dlt_h, s, dp in work:
                    if hh == 0:
                        dq = jnp.zeros((BAND, 128), f32)
                        dk = jnp.zeros((2 * BAND, 128), f32)
                        dv = jnp.zeros((2 * BAND, 128), f32)
                    pr = jnp.where(valid, jnp.exp(s - lse_h), 0.0)
                    dsb = (pr * (dp - dlt_h)).astype(bf16)
                    dv = dv + _dot_tn(pr.astype(bf16), doh)
                    dk = dk + _dot_tn(dsb, qh)
                    dq = dq + _dot(dsb, kh)
                    if hh == 1:
                        dqp[pl.ds(r0, BAND), :] = dq * 0.125
                        dkp[pl.ds(k0, 2 * BAND), :] += dk
                        dvp[pl.ds(k0, 2 * BAND), :] += dv
                return c

            lax.fori_loop(0, S // BAND // DIL_UNROLL, blk, 0)

            for acc, out, roped in ((dqp, dq_out[g], True), (dkp, dk_out[g], True), (dvp, dv_out[g], False)):
                if r == 1:
                    src = acc
                else:
                    _permute_out(nat, acc, r)
                    src = nat

                def emit(i, c, src=src, out=out, roped=roped):
                    r0 = pl.multiple_of(i * TQ, TQ)
                    d = src[pl.ds(r0, TQ), :]
                    if roped:
                        d = (d * c_ref[pl.ds(r0, TQ), :] + pltpu.roll(d * s1_ref[pl.ds(r0, TQ), :], 8, 1)
                             + pltpu.roll(d * s2_ref[pl.ds(r0, TQ), :], 120, 1))
                    out[pl.ds(r0, TQ), :] = d.astype(bf16)
                    return c

                lax.fori_loop(0, S // TQ, emit, 0)

    pair = pl.BlockSpec((S, 128), lambda p: (0, p))
    tab = pl.BlockSpec((S, 128), lambda p: (0, 0))
    blk_spec = pl.BlockSpec((S, DIL_BLK), lambda p: (0, p))
    return _call(
        order, body, [qkvb] * 9 + [dob, ob, lseb, c_t, s1_t, s2_t, dproj], name="dil_bwd", grid=(2,),
        in_specs=_dil_in_specs() + [pair, pair, pair, tab, tab, tab, pl.BlockSpec(memory_space=pl.ANY)],
        out_specs=blk_spec,
        out_shape=jax.ShapeDtypeStruct((S, NP), bf16),
        input_output_aliases={15: 0},
        scratch_shapes=[pltpu.VMEM((S, 128), f32)] * 11,
        compiler_params=_params(("parallel",)),
    )


def _inproj_bwd(order, dproj, wt, x, dx2, g1):
    tm, tk = 512, 1024
    nk = NP // tk

    def body(d_ref, w_ref, x_ref, dx2_ref, g_ref, dx_ref, dg_ref, acc):
        i, k = pl.program_id(0), pl.program_id(1)

        @pl.when(k == 0)
        def _():
            acc[...] = jnp.zeros_like(acc)

        acc[...] += _dot(d_ref[...], w_ref[...])

        @pl.when((i == 0) & (k == 0))
        def _():
            dg_ref[...] = jnp.zeros_like(dg_ref)

        @pl.when(k == nk - 1)
        def _():
            dh = acc[...]
            xb = x_ref[...]
            r = lax.rsqrt(jnp.mean(xb * xb, axis=-1, keepdims=True) + EPS)
            xh = xb * r
            gdh = dh * g_ref[...]
            dx_ref[...] = dx2_ref[...] + r * (gdh - xh * jnp.mean(gdh * xh, axis=-1, keepdims=True))
            dg_ref[...] += jnp.sum(dh * xh, axis=0, keepdims=True)

    row = pl.BlockSpec((tm, D), lambda i, k: (i, 0))
    vec = pl.BlockSpec((1, D), lambda i, k: (0, 0))
    return _call(
        order, body, (dproj, wt, x, dx2, g1), name="inproj_bwd", grid=(S // tm, nk),
        in_specs=[pl.BlockSpec((tm, tk), lambda i, k: (i, k)), pl.BlockSpec((tk, D), lambda i, k: (k, 0)), row, row, vec],
        out_specs=[row, vec],
        out_shape=[jax.ShapeDtypeStruct((S, D), f32), jax.ShapeDtypeStruct((1, D), f32)],
        scratch_shapes=[pltpu.VMEM((tm, D), f32)],
        compiler_params=_params(("arbitrary", "arbitrary")),
    )


HBM = pl.BlockSpec(memory_space=pltpu.HBM)
SEM = pl.BlockSpec(memory_space=pltpu.SEMAPHORE)
SMALL_ROWS = 8


def _comm_call(name, body, bufs, order, sems_in=(), new_sems=(), behind=()):
    nb, ns, nn = len(bufs), len(sems_in), len(new_sems)
    extra = order.token_for(bufs) + list(behind)

    def kern(*refs):
        off = nb + ns + len(extra)
        body(refs[:nb], refs[nb:nb + ns], refs[off:off + nn])
        refs[-1][...] = jnp.zeros((8, 128), f32)

    res = pl.pallas_call(
        kern, name=name,
        in_specs=[HBM] * nb + [SEM] * ns + [pl.BlockSpec(memory_space=pl.ANY)] * len(extra),
        out_specs=[SEM] * nn + [HBM] * nb + [pl.BlockSpec(memory_space=pltpu.VMEM)],
        out_shape=[pltpu.SemaphoreType.DMA((k,)) for k in new_sems] + [pltpu.HBM(b.shape, b.dtype) for b in bufs]
        + [jax.ShapeDtypeStruct((8, 128), f32)],
        input_output_aliases={i: nn + i for i in range(nb)},
        compiler_params=pltpu.CompilerParams(has_side_effects=pltpu.SideEffectType.DATAFLOW_SIDE_EFFECTING),
    )(*[pltpu.with_memory_space_constraint(b, pltpu.HBM) for b in bufs], *sems_in, *extra)
    order.mark(res[-1])
    return list(res[:nn]), list(res[nn:nn + nb])


def _place():
    x, y, c = lax.axis_index("x"), lax.axis_index("y"), lax.axis_index("c")
    chips = [(1 - x, y), (x, 1 - y), (1 - x, 1 - y)]
    return x, y, c, chips


def _rcopy(src, dst, ssem, rsem, dev):
    return pltpu.make_async_remote_copy(src_ref=src, dst_ref=dst, send_sem=ssem, recv_sem=rsem,
                                        device_id=dev, device_id_type=pl.DeviceIdType.MESH)


def _half(nrows, which):
    return pl.ds(which * (nrows // 2), nrows // 2)


def _ici_copies(stack, group_sizes, ssems, rsems):
    x, y, c, chips = _place()
    me_q = 2 * x + y
    sends, recvs = [], []
    a = 0
    for grp, size in enumerate(group_sizes):
        for k in range(size):
            rows = _half(stack[a].shape[1], c)
            for j, (cx, cy) in enumerate(chips):
                mine = stack[a].at[me_q, rows]
                sends.append(_rcopy(mine, mine, ssems[grp].at[k * 3 + j], rsems[grp].at[k * 3 + j], (cx, cy, c)))
                theirs = stack[a].at[2 * cx + cy, rows]
                recvs.append(_rcopy(theirs, theirs, ssems[grp].at[k * 3 + j], rsems[grp].at[k * 3 + j],
                                    (cx, cy, c)))
            a += 1
    return sends, recvs


def _allgather_start(name, stacks, order):
    n = len(stacks)

    def body(bufs, _, new):
        sends, _r = _ici_copies(bufs, [n], [new[0]], [new[1]])
        for cp in sends:
            cp.start()

    return _comm_call(name, body, stacks, order, new_sems=(3 * n, 3 * n))


def _forward_copies(stack, ssem, rsem):
    x, y, c, chips = _place()
    sib = (x, y, 1 - c)
    sends, recvs = [], []
    for a in range(len(stack)):
        for j, (cx, cy) in enumerate(chips):
            landed = stack[a].at[2 * cx + cy, _half(stack[a].shape[1], c)]
            sends.append(_rcopy(landed, landed, ssem.at[a * 3 + j], rsem.at[a * 3 + j], sib))
            other = stack[a].at[2 * cx + cy, _half(stack[a].shape[1], 1 - c)]
            recvs.append(_rcopy(other, other, ssem.at[a * 3 + j], rsem.at[a * 3 + j], sib))
    return sends, recvs


def _allgather_forward(name, stacks, sems, order, behind=()):
    n = len(stacks)

    def body(bufs, taken, new):
        sends, recvs = _ici_copies(bufs, [n], [taken[0]], [taken[1]])
        fwd, _r = _forward_copies(bufs, new[0], new[1])
        for arrived, onward in zip(recvs, fwd):
            arrived.wait_recv()
            onward.start()
        for cp in sends:
            cp.wait_send()

    return _comm_call(name, body, stacks, order, sems_in=sems, new_sems=(3 * n, 3 * n), behind=behind)


def _allgather_finish(name, stacks, sems, order):
    def body(bufs, taken, _):
        sends, recvs = _forward_copies(bufs, taken[0], taken[1])
        for cp in sends:
            cp.wait_send()
        for cp in recvs:
            cp.wait_recv()

    return _comm_call(name, body, stacks, order, sems_in=sems)[1]


def _window_unit(q, j):
    return C2I[WIN_UNIT0[q] + j]


def _pair_copies(g, t, ssem, rsem, gathered):
    x, y, c, _ = _place()
    sib = (x, y, 1 - c)
    cps, whole = [], []
    for a in range(len(g)):
        if a == 0 and gathered:
            for q in range(NCHIP):
                for j in range(WIN_UNITS // 2):
                    u = jnp.where(c == 0, _window_unit(q, WIN_UNITS // 2 + j), _window_unit(q, j))
                    src = g[0].at[pl.ds(pl.multiple_of(u * UNIT, UNIT), UNIT), :]
                    cps.append(_rcopy(src, t[0].at[q, pl.ds(j * UNIT, UNIT), :], ssem.at[0], rsem.at[0], sib))
            whole.append(_rcopy(t[0], t[0], ssem.at[0], rsem.at[0], sib))
        else:
            cp = _rcopy(g[a].at[:, _half(g[a].shape[1], 1 - c), :], t[a], ssem.at[a], rsem.at[a], sib)
            cps.append(cp)
            whole.append(cp)
    return cps, whole


def _comm_multi(name, parts, order):
    def body(buf_refs, taken, new):
        ib = it = inew = 0
        for pbody, pbufs, psems, pnew, _ in parts:
            pbody(buf_refs[ib:ib + len(pbufs)], taken[it:it + len(psems)], new[inew:inew + len(pnew)])
            ib, it, inew = ib + len(pbufs), it + len(psems), inew + len(pnew)

    sems, bufs = _comm_call(name, body, [b for p in parts for b in p[1]], order,
                            sems_in=[s for p in parts for s in p[2]], new_sems=[k for p in parts for k in p[3]])
    out, ib, inew = [], 0, 0
    for _, pbufs, _, pnew, unpack in parts:
        out.append(unpack(sems[inew:inew + len(pnew)], bufs[ib:ib + len(pbufs)]))
        ib, inew = ib + len(pbufs), inew + len(pnew)
    return out


def _pair_start_part(gs, gathered=False):
    n = len(gs)
    ts = [lax.empty((NCHIP, WIN_ROWS // 2, D) if (a == 0 and gathered) else (NCHIP, g.shape[1] // 2, g.shape[2]), f32)
          for a, g in enumerate(gs)]

    def body(bufs, _, new):
        for cp in _pair_copies(bufs[:n], bufs[n:], new[0], new[1], gathered)[0]:
            cp.start()

    return body, list(gs) + ts, (), (n, n), lambda sems, bufs: (sems, bufs)


def _pair_wait_part(bufs, sems, gathered=False):
    n = len(bufs) // 2

    def body(refs, taken, _):
        for cp in _pair_copies(refs[:n], refs[n:], taken[0], taken[1], gathered)[1]:
            cp.wait_send()
            cp.wait_recv()

    return body, list(bufs), list(sems), (), lambda _, out: (out[:n], out[n:])


def _row_tile(h):
    return min(h, 256)


def _pair_add(order, g, t, c_arr, name):
    _, R, C = g.shape
    h = R // 2
    tr = _row_tile(h)
    nblk = h // tr

    def body(c_ref, g_ref, t_ref, p32_ref, p16_ref):
        s = g_ref[...] + t_ref[...]
        p32_ref[...] = s
        p16_ref[...] = s.astype(bf16)

    blk = pl.BlockSpec((None, tr, C), lambda q, i, c_ref: (q, i, 0))
    return _call_indexed(
        order, body, (c_arr,), (g, t), (NCHIP, nblk),
        [pl.BlockSpec((None, tr, C), lambda q, i, c_ref: (q, c_ref[0] * nblk + i, 0)), blk], [blk, blk],
        name=name,
        out_shape=[jax.ShapeDtypeStruct((NCHIP, h, C), f32), jax.ShapeDtypeStruct((NCHIP, h, C), bf16)],
        compiler_params=_params(("parallel", "parallel")),
    )


def _pair_add_gathered(order, dwt, t, c_arr, name):
    half_units, half_rows = WIN_UNITS // 2, WIN_ROWS // 2
    table = jnp.asarray([_window_unit(q, j) for q in range(NCHIP) for j in range(WIN_UNITS)], jnp.int32)

    def body(tab_ref, c_ref, g_hbm, t_ref, p32_ref, p16_ref, buf, sem):
        q = pl.program_id(0)

        def gather(w, slot):
            cps = []
            for j in range(half_units):
                u = tab_ref[w * WIN_UNITS + c_ref[0] * half_units + j]
                cps.append(pltpu.make_async_copy(g_hbm.at[pl.ds(pl.multiple_of(u * UNIT, UNIT), UNIT), :],
                                                 buf.at[slot, pl.ds(j * UNIT, UNIT), :], sem.at[slot]))
            return cps

        @pl.when(q == 0)
        def _():
            for cp in gather(0, 0):
                cp.start()

        @pl.when(q + 1 < NCHIP)
        def _():
            for cp in gather(q + 1, (q + 1) % 2):
                cp.start()

        slot = q % 2
        pltpu.make_async_copy(buf.at[slot], buf.at[slot], sem.at[slot]).wait()
        s = buf[slot] + t_ref[...]
        p32_ref[...] = s
        p16_ref[...] = s.astype(bf16)

    blk = pl.BlockSpec((None, half_rows, D), lambda q, tab_ref, c_ref: (q, 0, 0))
    return _call_indexed(
        order, body, (table, c_arr), (dwt, t), (NCHIP,),
        [pl.BlockSpec(memory_space=pl.ANY), blk], [blk, blk],
        scratch_shapes=[pltpu.VMEM((2, half_rows, D), f32), pltpu.SemaphoreType.DMA((2,))],
        name=name,
        out_shape=[jax.ShapeDtypeStruct((NCHIP, half_rows, D), f32),
                   jax.ShapeDtypeStruct((NCHIP, half_rows, D), bf16)],
        compiler_params=_params(("arbitrary",)),
    )


def _shard_copies(p, r, sm, ssem, rsem):
    x, y, c, chips = _place()
    n = len(p)
    sends, recvs = [], []
    for a in range(n):
        for j, (cx, cy) in enumerate(chips):
            k = a * 3 + j
            sends.append(_rcopy(p[a].at[2 * cx + cy], r[a].at[j], ssem.at[k], rsem.at[k], (cx, cy, c)))
            recvs.append(_rcopy(r[a].at[j], r[a].at[j], ssem.at[k], rsem.at[k], (cx, cy, c)))
    if sm is not None:
        mine = sm.at[4 * x + 2 * y + c]
        for i in range(1, 8):
            px = (1 - x) if i & 4 else x
            py = (1 - y) if i & 2 else y
            pc = (1 - c) if i & 1 else c
            k = 3 * n + i - 1
            sends.append(_rcopy(mine, mine, ssem.at[k], rsem.at[k], (px, py, pc)))
            slot = sm.at[4 * px + 2 * py + pc]
            recvs.append(_rcopy(slot, slot, ssem.at[k], rsem.at[k], (px, py, pc)))
    return sends, recvs


def _shard_start_part(p16s, sm=None):
    n = len(p16s)
    rs = [lax.empty((3,) + p.shape[1:], bf16) for p in p16s]
    extra = [] if sm is None else [sm]
    nsem = 3 * n + (7 if sm is not None else 0)

    def body(bufs, _, new):
        sends, _r = _shard_copies(bufs[:n], bufs[n:2 * n], bufs[2 * n] if extra else None, new[0], new[1])
        for cp in sends:
            cp.start()

    return body, list(p16s) + rs + extra, (), (nsem, nsem), lambda sems, bufs: (sems, bufs)


def _shard_wait_part(bufs, sems, n):
    has_sm = len(bufs) > 2 * n

    def body(refs, taken, _):
        sends, recvs = _shard_copies(refs[:n], refs[n:2 * n], refs[2 * n] if has_sm else None, taken[0], taken[1])
        for cp in sends:
            cp.wait_send()
        for cp in recvs:
            cp.wait_recv()

    return body, list(bufs), list(sems), (), lambda _, out: (out[n:2 * n], (out[2 * n] if has_sm else None))


def _shard_sum(order, p32, r, q_arr, c_arr, name):
    _, h, C = p32.shape
    tr = _row_tile(h)
    nblk = h // tr

    def body(q_ref, c_ref, p_ref, r_ref, o_ref):
        s = p_ref[...]
        for j in range(3):
            s = s + r_ref[j].astype(f32)
        o_ref[...] = s

    return _call_indexed(
        order, body, (q_arr, c_arr), (p32, r), (nblk,),
        [pl.BlockSpec((None, tr, C), lambda i, q_ref, c_ref: (q_ref[0], i, 0)),
         pl.BlockSpec((3, tr, C), lambda i, q_ref, c_ref: (0, i, 0))],
        pl.BlockSpec((tr, C), lambda i, q_ref, c_ref: (c_ref[0] * nblk + i, 0)),
        name=name, out_shape=jax.ShapeDtypeStruct((2 * h, C), f32),
        compiler_params=_params(("parallel",)),
    )


def _swap_copies(full, ssem, rsem):
    x, y, c, _ = _place()
    sends, recvs = [], []
    for a in range(len(full)):
        mine = full[a].at[_half(full[a].shape[0], c)]
        sends.append(_rcopy(mine, mine, ssem.at[a], rsem.at[a], (x, y, 1 - c)))
        other = full[a].at[_half(full[a].shape[0], 1 - c)]
        recvs.append(_rcopy(other, other, ssem.at[a], rsem.at[a], (x, y, 1 - c)))
    return sends, recvs


def _swap_start_part(fulls):
    n = len(fulls)

    def body(bufs, _, new):
        for cp in _swap_copies(bufs, new[0], new[1])[0]:
            cp.start()

    return body, list(fulls), (), (n, n), lambda sems, bufs: (sems, bufs)


def _swap_wait_part(fulls, sems):
    def body(refs, taken, _):
        sends, recvs = _swap_copies(refs, taken[0], taken[1])
        for cp in sends:
            cp.wait_send()
        for cp in recvs:
            cp.wait_recv()

    return body, list(fulls), list(sems), (), lambda _, out: out


def _small_sum(order, sm):
    def body(sm_ref, o_ref):
        s = sm_ref[0]
        for d in range(1, 8):
            s = s + sm_ref[d]
        o_ref[...] = s

    return _call(order, body, (sm,), name="small_grad_sum", out_shape=jax.ShapeDtypeStruct((SMALL_ROWS, D), f32))


def _adamw_math(w, g, m, v):
    m = ADAM_B1 * m + (1.0 - ADAM_B1) * g
    v = ADAM_B2 * v + (1.0 - ADAM_B2) * (g * g)
    m_hat = m / (1.0 - ADAM_B1 ** ADAM_STEP)
    v_hat = v / (1.0 - ADAM_B2 ** ADAM_STEP)
    return -ADAM_LR * (m_hat / (jnp.sqrt(v_hat) + ADAM_EPS) + ADAM_WD * w), m, v


def _adamw_small(order, ws, gs, ms, vs, name):
    n = len(ws)

    def body(*refs):
        for i in range(n):
            res = _adamw_math(*[refs[k * n + i][...] for k in range(4)])
            for k in range(3):
                refs[4 * n + 3 * i + k][...] = res[k]

    out = _call(order, body, list(ws) + list(gs) + list(ms) + list(vs), name=name,
                out_shape=[jax.ShapeDtypeStruct(w.shape, f32) for w in ws for _ in range(3)])
    return [out[3 * i:3 * i + 3] for i in range(n)]


def _adamw(order, w, g, m, v, name):
    R, C = w.shape
    if R <= 256 or R % 256 == 0:
        tr, tc = min(R, 256), C
    else:
        tr, tc = R, 128

    def body(w_ref, g_ref, m_ref, v_ref, d_ref, nm_ref, nv_ref):
        d_ref[...], nm_ref[...], nv_ref[...] = _adamw_math(w_ref[...], g_ref[...], m_ref[...], v_ref[...])

    blk = pl.BlockSpec((tr, tc), lambda i, j: (i, j))
    return _call(
        order, body, (w, g, m, v), name=name, grid=(R // tr, C // tc), in_specs=[blk] * 4, out_specs=[blk] * 3,
        out_shape=[jax.ShapeDtypeStruct((R, C), f32)] * 3,
        compiler_params=_params(("parallel", "parallel")),
    )


def _feature_major(w):
    return jnp.transpose(w, (2, 0, 1)).reshape(SHARD_IN, D)


def _unfeature_major(a):
    return jnp.transpose(a.reshape(SHARD_IN, 1, D), (1, 2, 0))


def _window_of(wt, q):
    def plain(k):
        return lambda w: jnp.pad(w, ((OWN_ROW0[k], WIN_ROWS - OWN_ROW0[k] - SHARD_IN), (0, 0))).astype(bf16)

    def chip1(w):
        lo = jnp.pad(w[0:62], ((2, WIN_ROWS - 64), (0, 0)))
        hi = jnp.pad(w[70:SHARD_IN], ((64, WIN_ROWS - 64 - (SHARD_IN - 70)), (0, 0)))
        return (lo + hi).astype(bf16)

    win = lax.switch(q, [plain(0), chip1, plain(2), plain(3)], wt)
    fa = jnp.pad(wt[62:70], ((0, FA_ROWS - 8), (0, 0))).astype(bf16)
    return win, fa


def _own_rows(gwin, gfa, q):
    def plain(k):
        return lambda gw, gf: gw[OWN_ROW0[k]:OWN_ROW0[k] + SHARD_IN]

    def chip1(gw, gf):
        return (jnp.pad(gw[2:64], ((0, SHARD_IN - 62), (0, 0))) + jnp.pad(gf[0:8], ((62, SHARD_IN - 70), (0, 0)))
                + jnp.pad(gw[64:64 + SHARD_IN - 70], ((70, 0), (0, 0))))

    return lax.switch(q, [plain(0), chip1, plain(2), plain(3)], gwin, gfa)


def kernel(x, norm_attn_g, w_in, b_forget, w_branch_a, w_branch_b, w_out, norm_mlp_g, w_up, w_down, norm_final_g, loss_target, m_norm_attn_g, m_w_in, m_b_forget, m_w_branch_a, m_w_branch_b, m_w_out, m_norm_mlp_g, m_w_up, m_w_down, m_norm_final_g, v_norm_attn_g, v_w_in, v_b_forget, v_w_branch_a, v_w_branch_b, v_w_out, v_norm_mlp_g, v_w_up, v_w_down, v_norm_final_g):
    xi, yi, ci = lax.axis_index("x"), lax.axis_index("y"), lax.axis_index("c")
    q_me = 2 * xi + yi
    c_arr = jnp.reshape(ci, (1,)).astype(jnp.int32)
    q_arr = jnp.reshape(q_me, (1,)).astype(jnp.int32)
    x_, tgt = x[0], loss_target[0]

    names = ["w_branch_a", "w_branch_b", "w_out", "w_up", "w_down"]
    big = dict(zip(names, [w_branch_a[0], w_branch_b[0], w_out[0], w_up[0], w_down[0]]))
    ms = dict(zip(names, [m_w_branch_a[0], m_w_branch_b[0], m_w_out[0], m_w_up[0], m_w_down[0]]))
    vs = dict(zip(names, [v_w_branch_a[0], v_w_branch_b[0], v_w_out[0], v_w_up[0], v_w_down[0]]))
    grad, upd = {}, {}
    order = _Order()

    def run(fn, *args, **kw):
        return fn(order, *args, **kw)

    def own_slot(a):
        return lax.dynamic_update_slice(lax.empty((NCHIP,) + a.shape, a.dtype), a[None], (q_me, 0, 0))

    wt_own = _feature_major(w_in)
    win, fa_blk = _window_of(wt_own, q_me)
    sem_in, in_s = _allgather_start("allgather_start_in", [own_slot(win), own_slot(fa_blk)], order)
    sem_rest, rest = _allgather_start("allgather_start_rest", [own_slot(w.astype(bf16)) for w in big.values()], order)
    rope = _rope_tables()
    mt_own, vt_own = _feature_major(m_w_in), _feature_major(v_w_in)
    sem_f, in_s = _allgather_forward("allgather_forward_in", in_s, sem_in, order,
                                     behind=[wt_own, mt_own, vt_own, *rope])
    wins, fas = _allgather_finish("allgather_finish_in", in_s, sem_f, order)
    wt = run(_assemble_win, wins, fas)

    bpad = jnp.pad(b_forget, ((0, 0), (0, 120)))
    h1, qkvb, qkva, gates, fa = run(_norm_inproj, x_, norm_attn_g, wt, rope)
    F = run(_forget_cumsum, fa, bpad)
    oa, lsea = run(_fox_fwd, qkva, F)
    sem_f, rest = _allgather_forward("allgather_forward_rest", rest, sem_rest, order)
    ob, lseb = run(_dil_fwd, qkvb)
    was, wbs, wouts, wups, wdowns = _allgather_finish("allgather_finish_rest", rest, sem_f, order)
    wout = wouts.reshape(D, D)
    wdown = wdowns.reshape(DFF, D)
    ya, yb, mixed = run(_branch_mix, oa, ob, was, wbs, gates)
    x2, h2 = run(_outproj_norm, mixed, wout, x_, norm_mlp_g)
    u, a = run(_mlp_up, h2, wups)
    dx3, dx3b, dg3, loss_part = run(_mlp_down_loss, a, wdown, x2, norm_final_g.reshape(1, D), tgt)

    def comm(name, *parts):
        return _comm_multi(name, list(parts), order)

    def pair_adds(group, gs, ts):
        return zip(*[run(_pair_add, gs[i], ts[i], c_arr, "pair_add_" + nm) for i, nm in enumerate(group)])

    def shard_sums(group, p32s, rs):
        return [run(_shard_sum, p32s[i], rs[i], q_arr, c_arr, "shard_sum_" + nm) for i, nm in enumerate(group)]

    def adamw_group(group, fulls):
        for nm, gfull in zip(group, fulls):
            grad[nm] = gfull
            upd[nm] = run(_adamw, big[nm], gfull, ms[nm], vs[nm], "adamw_" + nm)

    grp_a, grp_b, grp_c = ["w_down", "w_up"], ["w_out", "w_branch_a", "w_branch_b"], ["w_in", "w_in_fa"]
    du = run(_mlp_down_bwd, dx3b, wdown, u)
    dwdown = run(_mm, a, dx3b, "tn", f32, 1024, D, "wgrad_down")
    dwup = run(_mm, h2, du, "tn", f32, D, 1024, "wgrad_up", stack_cols=True)
    ((sem_pa, buf_pa),) = comm("pair_start_a", _pair_start_part([dwdown.reshape(NCHIP, DFF // NCHIP, D), dwup]))
    dx2, dx2b, dg2 = run(_mlp_up_bwd, du, wups, x2, dx3, norm_mlp_g)
    ((gs, ts),) = comm("pair_wait_a", _pair_wait_part(buf_pa, sem_pa))
    p32_a, p16_a = pair_adds(grp_a, gs, ts)
    ((sem_sa, buf_sa),) = comm("shard_start_a", _shard_start_part(p16_a))
    dya, dyb, dproj = run(_gate_bwd, dx2b, wout, gates, ya, yb)
    dwout = run(_mm, mixed, dx2b, "tn", f32, D, D, "wgrad_out")
    doa, dob = run(_branch_bwd, dya, dyb, was, wbs)
    dwas, dwbs = run(_branch_wgrad, oa, ob, dya, dyb)
    ((sem_pb, buf_pb),) = comm("pair_start_b", _pair_start_part([dwout.reshape(NCHIP, D // NCHIP, D), dwas, dwbs]))
    dF, dproj = run(_fox_bwd, qkva, doa, oa, lsea, F, dproj)
    (gs, ts), (rs_a, _) = comm("pair_wait_b_shard_wait_a", _pair_wait_part(buf_pb, sem_pb),
                               _shard_wait_part(buf_sa, sem_sa, len(grp_a)))
    p32_b, p16_b = pair_adds(grp_b, gs, ts)
    fulls_a = shard_sums(grp_a, p32_a, rs_a)
    (sem_wa, fulls_a), (sem_sb, buf_sb) = comm("swap_start_a_shard_start_b", _swap_start_part(fulls_a),
                                               _shard_start_part(p16_b))
    dbf, dproj = run(_forget_bwd, dF, fa, bpad, dproj)
    dproj = run(_dil_bwd, qkvb, dob, ob, lseb, rope, dproj)
    (rs_b, _), fulls_a = comm("shard_wait_b_swap_wait_a", _shard_wait_part(buf_sb, sem_sb, len(grp_b)),
                              _swap_wait_part(fulls_a, sem_wa))
    fulls_b = shard_sums(grp_b, p32_b, rs_b)
    ((sem_wb, fulls_b),) = comm("swap_start_b", _swap_start_part(fulls_b))
    dwt = run(_mm, dproj, h1, "tn", f32, 512, D, "wgrad_in")
    dwfa = jnp.broadcast_to(dwt[F_FA:F_FA + FA_ROWS][None], (NCHIP, FA_ROWS, D))
    (sem_pc, buf_pc), fulls_b = comm("pair_start_c_swap_wait_b", _pair_start_part([dwt, dwfa], gathered=True),
                                     _swap_wait_part(fulls_b, sem_wb))
    adamw_group(grp_b, fulls_b)
    (((dwt_c, dwfa_c), (t_in, t_fa)),) = comm("pair_wait_c", _pair_wait_part(buf_pc, sem_pc, gathered=True))
    p32_in, p16_in = run(_pair_add_gathered, dwt_c, t_in, c_arr, "pair_add_w_in")
    p32_fa, p16_fa = run(_pair_add, dwfa_c, t_fa, c_arr, "pair_add_w_in_fa")
    ((sem_sc, buf_sc),) = comm("shard_start_c", _shard_start_part([p16_in, p16_fa]))
    gx, dg1 = run(_inproj_bwd, dproj, wt, x_, dx2, norm_attn_g)
    adamw_group(grp_a, fulls_a)
    small = jnp.concatenate([dg1, dg2, dg3, jnp.pad(dbf[:, 0:8], ((0, 0), (0, D - 8))),
                             jnp.pad(loss_part, ((0, 0), (0, D - 128))),
                             jnp.zeros((SMALL_ROWS - 5, D), f32)], axis=0)
    sm = lax.dynamic_update_slice(lax.empty((8, SMALL_ROWS, D), f32), small[None],
                                  (4 * xi + 2 * yi + ci, 0, 0))
    (sem_sm, buf_sm), (rs_c, _) = comm("small_start_shard_wait_c", _shard_start_part([], sm),
                                       _shard_wait_part(buf_sc, sem_sc, len(grp_c)))
    fulls_c = shard_sums(grp_c, [p32_in, p32_fa], rs_c)
    (sem_wc, fulls_c), (_, sm) = comm("swap_start_c_small_wait", _swap_start_part(fulls_c),
                                      _shard_wait_part(buf_sm, sem_sm, 0))
    gsmall = run(_small_sum, sm)
    loss = gsmall[4, 0]

    grad["norm_attn_g"], grad["norm_mlp_g"] = gsmall[0:1], gsmall[1:2]
    grad["norm_final_g"], grad["b_forget"] = gsmall[2:3], gsmall[3:4, 0:8]
    smalls = ["norm_attn_g", "norm_mlp_g", "norm_final_g", "b_forget"]
    res = run(_adamw_small, [norm_attn_g, norm_mlp_g, norm_final_g.reshape(1, D), b_forget],
              [grad[nm] for nm in smalls],
              [m_norm_attn_g, m_norm_mlp_g, m_norm_final_g.reshape(1, D), m_b_forget],
              [v_norm_attn_g, v_norm_mlp_g, v_norm_final_g.reshape(1, D), v_b_forget], "adamw_small")
    upd.update(zip(smalls, res))

    ((gwin, gfa),) = comm("swap_wait_c", _swap_wait_part(fulls_c, sem_wc))
    g_in = _own_rows(gwin, gfa, q_me)
    upd_in = run(_adamw, wt_own, g_in, mt_own, vt_own, "adamw_w_in")
    grad["w_in"] = _unfeature_major(g_in)
    upd["w_in"] = [_unfeature_major(t) for t in upd_in]

    order_out = ["norm_attn_g", "w_in", "b_forget", "w_branch_a", "w_branch_b", "w_out", "norm_mlp_g", "w_up",
                 "w_down", "norm_final_g"]
    shapes = dict(norm_attn_g=norm_attn_g.shape, w_in=w_in.shape, b_forget=b_forget.shape,
                  w_branch_a=w_branch_a.shape, w_branch_b=w_branch_b.shape, w_out=w_out.shape,
                  norm_mlp_g=norm_mlp_g.shape, w_up=w_up.shape, w_down=w_down.shape, norm_final_g=norm_final_g.shape)
    outs = [loss, gx.reshape(x.shape)]
    outs += [grad[nm].reshape(shapes[nm]) for nm in order_out]
    for k in range(3):
        outs += [upd[nm][k].reshape(shapes[nm]) for nm in order_out]
    return tuple(outs)
```

```python
import jax
import jax.numpy as jnp
from jax import lax
from jax.experimental import pallas as pl
from jax.experimental.pallas import tpu as pltpu

f32 = jnp.float32
bf16 = jnp.bfloat16

S = 2048
D = 1024
DFF = 4096
HD = 64
FOXW = 512
DILOUT = 256
DIL = (1, 4, 16)
BAND = 128
EPS = 1e-6
NEG = -1e30
ROPE_THETA = 500000.0
NCHIP = 4
TQ = 256

ADAM_LR, ADAM_B1, ADAM_B2, ADAM_EPS, ADAM_WD, ADAM_STEP = 0.001, 0.9, 0.999, 1e-08, 0.01, 10
VMEM_LIMIT = 56 * 1024 * 1024

UNIT = 64
NP = 6144
F_DIL, F_FOX, F_FA, F_G = 0, 2304, 3840, 4096
DIL_BLK, FOX_BLK = 1152, 384
WIN_UNITS, WIN_ROWS = 24, 1536
WIN_UNIT0 = (0, 23, 45, 68)
OWN_ROW0 = (0, 2, 60, 62)
SHARD_IN = 1474
FA_ROWS = 32


def _compact_to_internal():
    c2i = {}
    for p in range(2):
        for role in range(3):
            for g in range(3):
                for hh in range(2):
                    c2i[24 + 12 * role + 4 * g + 2 * p + hh] = 18 * p + 6 * role + 2 * g + hh
    for p in range(4):
        for role in range(3):
            for hh in range(2):
                c2i[8 * role + 2 * p + hh] = F_FOX // UNIT + 6 * p + 2 * role + hh
    for j in range(32):
        c2i[60 + j] = F_G // UNIT + j
    return c2i


C2I = _compact_to_internal()
OVERLAP_UNITS = (23, 45, 46, 68)


def _params(sem=None):
    return pltpu.CompilerParams(dimension_semantics=sem, vmem_limit_bytes=VMEM_LIMIT)


class _Order:
    def __init__(self):
        self.tok = None

    def mark(self, v):
        self.tok = v

    def token_for(self, args):
        return [] if self.tok is None or any(self.tok is a for a in args) else [self.tok]


def _call(order, body, args, in_specs=None, **kw):
    args = list(args)
    n_in = len(args)
    if in_specs is None:
        in_specs = [pl.BlockSpec(memory_space=pltpu.VMEM)] * n_in
    kern = body
    extra = order.token_for(args)
    if extra:
        in_specs = list(in_specs) + [pl.BlockSpec(memory_space=pl.ANY)]

        def kern(*refs):
            body(*refs[:n_in], *refs[n_in + 1:])

    out = pl.pallas_call(kern, in_specs=in_specs, **kw)(*args, *extra)
    order.mark(out[0] if isinstance(out, (tuple, list)) else out)
    return out


def _call_indexed(order, body, scalars, args, grid, in_specs, out_specs, scratch_shapes=(), **kw):
    args, in_specs = list(args), list(in_specs)
    n_front = len(scalars) + len(args)
    kern = body
    extra = order.token_for(args)
    if extra:
        in_specs.append(pl.BlockSpec(memory_space=pl.ANY))

        def kern(*refs):
            body(*refs[:n_front], *refs[n_front + 1:])

    out = pl.pallas_call(
        kern, grid_spec=pltpu.PrefetchScalarGridSpec(num_scalar_prefetch=len(scalars), grid=grid, in_specs=in_specs,
                                                     out_specs=out_specs, scratch_shapes=scratch_shapes),
        **kw)(*scalars, *args, *extra)
    order.mark(out[0] if isinstance(out, (tuple, list)) else out)
    return out


def _dot(a, b):
    return jnp.dot(a, b, preferred_element_type=f32)


def _dot_nt(a, b):
    return lax.dot_general(a, b, (((1,), (1,)), ((), ())), preferred_element_type=f32)


def _dot_tn(a, b):
    return lax.dot_general(a, b, (((0,), (0,)), ((), ())), preferred_element_type=f32)


def _split3(x):
    hi = x.astype(bf16)
    r1 = x - hi.astype(f32)
    mid = r1.astype(bf16)
    lo = (r1 - mid.astype(f32)).astype(bf16)
    return hi, mid, lo


def _rope_tables(after):
    half = 8
    inv_freq = jnp.power(jnp.float32(ROPE_THETA), -jnp.arange(half, dtype=f32) * 2.0 / 16)
    ang = (jnp.arange(S).astype(f32) + after)[:, None] * inv_freq[None, :]
    cos, sin = jnp.cos(ang), jnp.sin(ang)
    one = jnp.ones((S, HD - 16), f32)
    zero = jnp.zeros((S, HD - 16), f32)
    z8 = jnp.zeros((S, 8), f32)
    c = jnp.concatenate([cos, cos, one], axis=1)
    s1 = jnp.concatenate([-sin, z8, zero], axis=1)
    s2 = jnp.concatenate([z8, sin, zero], axis=1)
    return tuple(jnp.concatenate([t, t], axis=1) for t in (c, s1, s2))


def _mm(order, a, b, mode, out_dtype, tm, tn, name, stack_cols=False):
    if mode == "nn":
        (M, K), (_, N) = a.shape, b.shape
        a_spec = pl.BlockSpec((tm, K), lambda i, j: (i, 0))
        b_spec = pl.BlockSpec((K, tn), lambda i, j: (0, j))
        dot = _dot
    elif mode == "nt":
        (M, K), (N, _) = a.shape, b.shape
        a_spec = pl.BlockSpec((tm, K), lambda i, j: (i, 0))
        b_spec = pl.BlockSpec((tn, K), lambda i, j: (j, 0))
        dot = _dot_nt
    else:
        (K, M), (_, N) = a.shape, b.shape
        a_spec = pl.BlockSpec((K, tm), lambda i, j: (0, i))
        b_spec = pl.BlockSpec((K, tn), lambda i, j: (0, j))
        dot = _dot_tn

    def body(a_ref, b_ref, o_ref):
        o_ref[...] = dot(a_ref[...], b_ref[...]).astype(out_dtype)

    if stack_cols:
        assert tm == M
        out_spec = pl.BlockSpec((None, tm, tn), lambda i, j: (j, 0, 0))
        out_shape = jax.ShapeDtypeStruct((N // tn, M, tn), out_dtype)
    else:
        out_spec = pl.BlockSpec((tm, tn), lambda i, j: (i, j))
        out_shape = jax.ShapeDtypeStruct((M, N), out_dtype)
    return _call(
        order, body, (a, b), name=name, grid=(M // tm, N // tn), in_specs=[a_spec, b_spec],
        out_specs=out_spec, out_shape=out_shape,
        compiler_params=_params(("parallel", "parallel")),
    )


def _assemble_win(order, wins, fas):
    def body(win_ref, fa_ref, o_ref):
        q = pl.program_id(0)

        @pl.when(q == 0)
        def _():
            o_ref[...] = jnp.zeros_like(o_ref)

        for k in range(NCHIP):
            @pl.when(q == k)
            def _(k=k):
                for j in range(WIN_UNITS):
                    cu = WIN_UNIT0[k] + j
                    dst = pl.ds(C2I[cu] * UNIT, UNIT)
                    if cu in OVERLAP_UNITS:
                        o_ref[dst, :] += win_ref[j * UNIT:(j + 1) * UNIT, :]
                    else:
                        o_ref[dst, :] = win_ref[j * UNIT:(j + 1) * UNIT, :]
                if k == 1:
                    o_ref[F_FA:F_FA + FA_ROWS, :] = fa_ref[...]

    return _call(
        order, body, (wins, fas), name="assemble_w_in", grid=(NCHIP,),
        in_specs=[pl.BlockSpec((None, WIN_ROWS, D), lambda q: (q, 0, 0)),
                  pl.BlockSpec((None, FA_ROWS, D), lambda q: (1, 0, 0))],
        out_specs=pl.BlockSpec((NP, D), lambda q: (0, 0)),
        out_shape=jax.ShapeDtypeStruct((NP, D), bf16),
        compiler_params=_params(("arbitrary",)),
    )


def _norm_inproj(order, x, g1, wt, rope):
    tm = 256
    c_t, s1_t, s2_t = rope

    def body(x_ref, g_ref, w_ref, c_ref, s1_ref, s2_ref, h_ref, qkvb_ref, qkva_ref, gates_ref, fa_ref):
        xb = x_ref[...]
        r = lax.rsqrt(jnp.mean(xb * xb, axis=-1, keepdims=True) + EPS)
        h = ((xb * r) * g_ref[...]).astype(bf16)
        h_ref[...] = h
        c, s1, s2 = c_ref[...], s1_ref[...], s2_ref[...]
        for p in range(2):
            pb = _dot_nt(h, w_ref[F_DIL + p * DIL_BLK:F_DIL + (p + 1) * DIL_BLK, :])
            for ch in range(DIL_BLK // 128):
                pc = pb[:, ch * 128:(ch + 1) * 128]
                if ch < 6:
                    pc = pc * c + pltpu.roll(pc, 120, 1) * s1 + pltpu.roll(pc, 8, 1) * s2
                qkvb_ref[:, p * DIL_BLK + ch * 128:p * DIL_BLK + (ch + 1) * 128] = pc
        qkva_ref[...] = _dot_nt(h, w_ref[F_FOX:F_FA, :]).astype(bf16)
        fa_ref[...] = _dot_nt(h, w_ref[F_FA:F_FA + 128, :])
        gates_ref[...] = _dot_nt(h, w_ref[F_G:NP, :]).astype(bf16)

    row = lambda w: pl.BlockSpec((tm, w), lambda i: (i, 0))
    return _call(
        order, body, (x, g1, wt, c_t, s1_t, s2_t), name="norm_inproj", grid=(S // tm,),
        in_specs=[row(D), pl.BlockSpec((1, D), lambda i: (0, 0)), pl.BlockSpec((NP, D), lambda i: (0, 0)),
                  row(128), row(128), row(128)],
        out_specs=[row(D), row(2 * DIL_BLK), row(4 * FOX_BLK), row(2 * D), row(128)],
        out_shape=[jax.ShapeDtypeStruct((S, D), bf16), jax.ShapeDtypeStruct((S, 2 * DIL_BLK), f32),
                   jax.ShapeDtypeStruct((S, 4 * FOX_BLK), bf16), jax.ShapeDtypeStruct((S, 2 * D), bf16),
                   jax.ShapeDtypeStruct((S, 128), f32)],
        compiler_params=_params(("parallel",)),
    )


def _forget_cumsum(order, fa, bpad):
    nb = S // TQ

    def body(fa_ref, b_ref, F_ref):
        rr = lax.broadcasted_iota(jnp.int32, (TQ, TQ), 0)
        cc = lax.broadcasted_iota(jnp.int32, (TQ, TQ), 1)
        tri = (rr >= cc).astype(bf16)
        lane = lax.broadcasted_iota(jnp.int32, (1, 128), 1)
        carry = jnp.zeros((1, 128), f32)
        for b in range(nb):
            z = fa_ref[b * TQ:(b + 1) * TQ, :] + b_ref[...]
            lf = jnp.minimum(z, 0.0) - jnp.log(1.0 + jnp.exp(-jnp.abs(z)))
            lf = jnp.where(lane < 8, lf, 0.0)
            hi, mid, lo = _split3(lf)
            fb = (_dot(tri, hi) + _dot(tri, mid)) + _dot(tri, lo) + carry
            F_ref[b * TQ:(b + 1) * TQ, :] = fb
            carry = fb[TQ - 1:TQ, :]

    return _call(
        order, body, (fa, bpad), name="forget_cumsum",
        out_shape=jax.ShapeDtypeStruct((S, 128), f32),
        compiler_params=_params(),
    )


def _head_masks():
    lane = lax.broadcasted_iota(jnp.int32, (1, 128), 1)
    return lane, (lane < HD, lane >= HD)


L_FT, L_ONE = 0, 3
FOX_TQ, FOX_TK = 256, 512


def _set_lanes(x, lane, first, cols):
    for n, col in enumerate(cols):
        x = jnp.where(lane == first + n, col, x)
    return x


def _f32_parts(col):
    return [t.astype(f32) for t in _split3(col)]


def _fox_operands(qkv_ref, F_ref, lse_ref, qa, ka, p, rows):
    lane, hm = _head_masks()
    q = qkv_ref[rows, 0:128].astype(f32) * 0.125
    k = qkv_ref[rows, 128:256].astype(f32)
    Fb = F_ref[rows, :]
    for hh in (0, 1):
        free = (1 - hh) * HD
        fcol = jnp.sum(jnp.where(lane == 2 * p + hh, Fb, 0.0), axis=1, keepdims=True)
        qterm = fcol if lse_ref is None else fcol - lse_ref[rows, hh * HD:hh * HD + 1]
        qcols = _f32_parts(qterm) + [1.0] * 3
        kcols = [1.0] * 3 + [-t for t in _f32_parts(fcol)]
        qa[hh, rows, :] = _set_lanes(jnp.where(hm[hh], q, 0.0), lane, free, qcols).astype(bf16)
        ka[hh, rows, :] = _set_lanes(k, lane, free, kcols).astype(bf16)


def _fox_fwd(order, qkva, F):
    tq, tk = FOX_TQ, FOX_TK

    def body(qkv_ref, F_ref, o_ref, lse_ref, qa, ka, vt):
        p = pl.program_id(0)
        keyi = lax.broadcasted_iota(jnp.int32, (tk, 1), 0)
        qryi = lax.broadcasted_iota(jnp.int32, (1, tq), 1)
        sub = lax.broadcasted_iota(jnp.int32, (128, 1), 0)

        def prep(i, c):
            rows = pl.ds(pl.multiple_of(i * tk, tk), tk)
            _fox_operands(qkv_ref, F_ref, None, qa, ka, p, rows)
            vt[i] = qkv_ref[rows, 256:384].astype(f32).T.astype(bf16)
            return c

        lax.fori_loop(0, S // tk, prep, 0)

        def qblock(i, first_half):
            r0 = pl.multiple_of(i * tq, tq)
            qh = [qa[hh, pl.ds(r0, tq), :] for hh in (0, 1)]

            def kv(jb, carry, masked, width):
                keys = pl.ds(pl.multiple_of(jb * tk, tk), width)
                sts = [_dot_nt(ka[hh, keys, :], qh[hh]) for hh in (0, 1)]
                new = []
                for hh in (0, 1):
                    m, l, a = carry[3 * hh:3 * hh + 3]
                    st = sts[hh]
                    if masked:
                        st = jnp.where(jb * tk + keyi[0:width] <= r0 + qryi, st, NEG)
                    mn = jnp.maximum(m, jnp.max(st, axis=0, keepdims=True))
                    al = jnp.exp(m - mn)
                    pt = jnp.exp(st - mn)
                    l = al * l + jnp.sum(pt, axis=0, keepdims=True)
                    a = al * a + _dot(vt[jb, hh * HD:(hh + 1) * HD, 0:width], pt.astype(bf16))
                    new += [mn, l, a]
                return tuple(new)

            init = (jnp.full((1, tq), NEG, f32), jnp.zeros((1, tq), f32), jnp.zeros((HD, tq), f32)) * 2
            last = (r0 + tq - 1) // tk
            carry = lax.fori_loop(0, last, lambda j, cr: kv(j, cr, False, tk), init)
            m0, l0, a0, m1, l1, a1 = kv(last, carry, True, tk // 2 if first_half else tk)
            ot = jnp.concatenate([a0 / l0, a1 / l1], axis=0)
            lt = jnp.where(sub < HD, m0 + jnp.log(l0), m1 + jnp.log(l1))
            o_ref[pl.ds(r0, tq), :] = ot.T.astype(bf16)
            lse_ref[pl.ds(r0, tq), :] = lt.T

        def qpair(t, c):
            qblock(2 * t, True)
            qblock(2 * t + 1, False)
            return c

        assert tk == 2 * tq
        lax.fori_loop(0, S // tk, qpair, 0)

    pair = pl.BlockSpec((S, 128), lambda p: (0, p))
    return _call(
        order, body, (qkva, F), name="fox_fwd", grid=(4,),
        in_specs=[pl.BlockSpec((S, FOX_BLK), lambda p: (0, p)), pl.BlockSpec((S, 128), lambda p: (0, 0))],
        out_specs=[pair, pair],
        out_shape=[jax.ShapeDtypeStruct((S, FOXW), bf16), jax.ShapeDtypeStruct((S, FOXW), f32)],
        scratch_shapes=[pltpu.VMEM((2, S, 128), bf16)] * 2 + [pltpu.VMEM((S // tk, 128, tk), bf16)],
        compiler_params=_params(("parallel",)),
    )


def _permute_in(dst, src, r):
    L = S // r
    for rho in range(r):
        dst[rho * L:(rho + 1) * L, :] = src[pl.ds(rho, L, stride=r), :]


def _permute_out(dst, src, r):
    L = S // r
    for rho in range(r):
        dst[pl.ds(rho, L, stride=r), :] = src[rho * L:(rho + 1) * L, :]


def _band_geometry(bb, nbl):
    r0 = pl.multiple_of(bb * BAND, BAND)
    k0 = pl.multiple_of(jnp.maximum(bb - 1, 0) * BAND, BAND)
    sub0 = (bb - lax.rem(bb, nbl)) * BAND
    qi = r0 + lax.broadcasted_iota(jnp.int32, (BAND, 1), 0)
    ki = k0 + lax.broadcasted_iota(jnp.int32, (1, 2 * BAND), 1)
    diff = qi - ki
    valid = (diff >= 0) & (diff <= BAND) & (ki >= sub0)
    return r0, k0, valid


def _dil_views(ref):
    return [[ref.at[:, pl.ds((3 * role + g) * 128, 128)] for g in range(3)] for role in range(3)]


DIL_UNROLL = 4


def _dil_in_specs():
    return [pl.BlockSpec((S, 128), lambda p, k=k: (0, 9 * p + k)) for k in range(9)]


def _dil_fwd(order, qkvb):
    def body(*refs):
        q_refs, k_refs, v_refs = refs[0:3], refs[3:6], refs[6:9]
        ob_ref, lse_ref, qp, kp, vp, op, lp = refs[9:16]
        on, ln = refs[16:19], refs[19:22]
        _, hm = _head_masks()
        for g, r in enumerate(DIL):
            nbl = S // r // BAND
            if r == 1:
                qs_, ks_, vs_, od, ld = q_refs[g], k_refs[g], v_refs[g], on[g], ln[g]
            else:
                _permute_in(qp, q_refs[g], r)
                _permute_in(kp, k_refs[g], r)
                _permute_in(vp, v_refs[g], r)
                qs_, ks_, vs_, od, ld = qp, kp, vp, op, lp

            def blk(t, c, qs_=qs_, ks_=ks_, vs_=vs_, od=od, ld=ld, nbl=nbl):
                work = []
                for u in range(DIL_UNROLL):
                    r0, k0, valid = _band_geometry(DIL_UNROLL * t + u, nbl)
                    q = qs_[pl.ds(r0, BAND), :] * 0.125
                    kw = ks_[pl.ds(k0, 2 * BAND), :].astype(bf16)
                    vw = vs_[pl.ds(k0, 2 * BAND), :]
                    for hh in (0, 1):
                        qh = jnp.where(hm[hh], q, 0.0).astype(bf16)
                        work.append((u, hh, r0, valid, vw, _dot_nt(qh, kw)))
                o = [jnp.zeros((BAND, 128), f32)] * DIL_UNROLL
                lse = [jnp.zeros((BAND, 128), f32)] * DIL_UNROLL
                for u, hh, r0, valid, vw, s in work:
                    s = jnp.where(valid, s, NEG)
                    m = jnp.max(s, axis=1, keepdims=True)
                    pr = jnp.exp(s - m)
                    l = jnp.sum(pr, axis=1, keepdims=True)
                    vm = jnp.where(hm[hh], vw, 0.0).astype(bf16)
                    o[u] = o[u] + _dot((pr / l).astype(bf16), vm)
                    lse[u] = jnp.where(hm[hh], m + jnp.log(l), lse[u])
                    if hh == 1:
                        od[pl.ds(r0, BAND), :] = o[u]
                        ld[pl.ds(r0, BAND), :] = lse[u]
                return c

            lax.fori_loop(0, S // BAND // DIL_UNROLL, blk, 0)
            if r != 1:
                _permute_out(on[g], op, r)
                _permute_out(ln[g], lp, r)

        def combine(i, c):
            r0 = pl.multiple_of(i * TQ, TQ)
            ls = [ln[g][pl.ds(r0, TQ), :] for g in range(3)]
            mx = jnp.maximum(jnp.maximum(ls[0], ls[1]), ls[2])
            es = [jnp.exp(l - mx) for l in ls]
            tot = (es[0] + es[1]) + es[2]
            acc = (es[0] / tot) * on[0][pl.ds(r0, TQ), :]
            acc = acc + (es[1] / tot) * on[1][pl.ds(r0, TQ), :]
            acc = acc + (es[2] / tot) * on[2][pl.ds(r0, TQ), :]
            ob_ref[pl.ds(r0, TQ), :] = acc.astype(bf16)
            lse_ref[pl.ds(r0, TQ), :] = mx + jnp.log(tot)
            return c

        lax.fori_loop(0, S // TQ, combine, 0)

    out_blk = pl.BlockSpec((S, 128), lambda p: (0, p))
    return _call(
        order, body, [qkvb] * 9, name="dil_fwd", grid=(2,),
        in_specs=_dil_in_specs(), out_specs=[out_blk, out_blk],
        out_shape=[jax.ShapeDtypeStruct((S, DILOUT), bf16), jax.ShapeDtypeStruct((S, DILOUT), f32)],
        scratch_shapes=[pltpu.VMEM((S, 128), f32)] * 11,
        compiler_params=_params(("parallel",)),
    )


def _branch_mix(order, oa, ob, was, wbs, gates):
    tm = 512

    def body(oa_ref, ob_ref, wa_ref, wb_ref, g_ref, ya_ref, yb_ref, mix_ref):
        oa_b, ob_b = oa_ref[...], ob_ref[...]
        for q in range(NCHIP):
            cols = slice(q * 256, (q + 1) * 256)
            ya = _dot(oa_b, wa_ref[q])
            yb = _dot(ob_b, wb_ref[q])
            ya_ref[:, cols] = ya.astype(bf16)
            yb_ref[:, cols] = yb.astype(bf16)
            ga = g_ref[:, q * 256:(q + 1) * 256].astype(f32)
            gb = g_ref[:, D + q * 256:D + (q + 1) * 256].astype(f32)
            mix_ref[:, cols] = (jax.nn.sigmoid(ga) * ya + jax.nn.sigmoid(gb) * yb).astype(bf16)

    row = lambda w: pl.BlockSpec((tm, w), lambda i: (i, 0))
    full3 = lambda a: pl.BlockSpec(a.shape, lambda i: (0, 0, 0))
    return _call(
        order, body, (oa, ob, was, wbs, gates), name="branch_mix", grid=(S // tm,),
        in_specs=[row(FOXW), row(DILOUT), full3(was), full3(wbs), row(2 * D)],
        out_specs=[row(D), row(D), row(D)],
        out_shape=[jax.ShapeDtypeStruct((S, D), bf16), jax.ShapeDtypeStruct((S, D), bf16),
                   jax.ShapeDtypeStruct((S, D), bf16)],
        compiler_params=_params(("parallel",)),
    )


def _outproj_norm(order, mixed, wout, x, g2):
    tm = 512

    def body(m_ref, w_ref, x_ref, g_ref, x2_ref, h2_ref):
        x2 = x_ref[...] + _dot(m_ref[...], w_ref[...])
        x2_ref[...] = x2
        r = lax.rsqrt(jnp.mean(x2 * x2, axis=-1, keepdims=True) + EPS)
        h2_ref[...] = ((x2 * r) * g_ref[...]).astype(bf16)

    row = pl.BlockSpec((tm, D), lambda i: (i, 0))
    return _call(
        order, body, (mixed, wout, x, g2), name="outproj_norm", grid=(S // tm,),
        in_specs=[row, pl.BlockSpec((D, D), lambda i: (0, 0)), row, pl.BlockSpec((1, D), lambda i: (0, 0))],
        out_specs=[row, row],
        out_shape=[jax.ShapeDtypeStruct((S, D), f32), jax.ShapeDtypeStruct((S, D), bf16)],
        compiler_params=_params(("parallel",)),
    )


def _mlp_up(order, h2, wups):
    tm = 1024

    def body(h_ref, w_ref, ru_ref, a_ref):
        ru = jnp.maximum(_dot(h_ref[...], w_ref[...]), 0.0)
        ru_ref[...] = ru.astype(bf16)
        a_ref[...] = (ru * ru).astype(bf16)

    out = pl.BlockSpec((tm, D), lambda q, i: (i, q))
    return _call(
        order, body, (h2, wups), name="mlp_up", grid=(NCHIP, S // tm),
        in_specs=[pl.BlockSpec((tm, D), lambda q, i: (i, 0)), pl.BlockSpec((None, D, D), lambda q, i: (q, 0, 0))],
        out_specs=[out, out],
        out_shape=[jax.ShapeDtypeStruct((S, DFF), bf16), jax.ShapeDtypeStruct((S, DFF), bf16)],
        compiler_params=_params(("parallel", "parallel")),
    )


def _mlp_down_loss(order, a, wdown, x2, g3, tgt):
    tm = 512

    def body(a_ref, w_ref, x2_ref, g_ref, t_ref, dx_ref, dxb_ref, dg_ref, loss_ref):
        i = pl.program_id(0)
        x3 = x2_ref[...] + _dot(a_ref[...], w_ref[...])
        r = lax.rsqrt(jnp.mean(x3 * x3, axis=-1, keepdims=True) + EPS)
        xh = x3 * r
        g = g_ref[...]
        e = xh * g - t_ref[...]
        part = 0.5 * jnp.sum(jnp.mean(e * e, axis=-1, keepdims=True), axis=0, keepdims=True)
        dy = e * (1.0 / D)
        gdy = dy * g
        dx = r * (gdy - xh * jnp.mean(gdy * xh, axis=-1, keepdims=True))
        dx_ref[...] = dx
        dxb_ref[...] = dx.astype(bf16)

        @pl.when(i == 0)
        def _():
            dg_ref[...] = jnp.zeros_like(dg_ref)
            loss_ref[...] = jnp.zeros_like(loss_ref)

        dg_ref[...] += jnp.sum(dy * xh, axis=0, keepdims=True)
        loss_ref[...] += jnp.broadcast_to(part, (1, 128))

    row = pl.BlockSpec((tm, D), lambda i: (i, 0))
    vec = pl.BlockSpec((1, D), lambda i: (0, 0))
    return _call(
        order, body, (a, wdown, x2, g3, tgt), name="mlp_down_loss", grid=(S // tm,),
        in_specs=[pl.BlockSpec((tm, DFF), lambda i: (i, 0)), pl.BlockSpec((DFF, D), lambda i: (0, 0)), row, vec, row],
        out_specs=[row, row, vec, pl.BlockSpec((1, 128), lambda i: (0, 0))],
        out_shape=[jax.ShapeDtypeStruct((S, D), f32), jax.ShapeDtypeStruct((S, D), bf16),
                   jax.ShapeDtypeStruct((1, D), f32), jax.ShapeDtypeStruct((1, 128), f32)],
        compiler_params=_params(("arbitrary",)),
    )


def _mlp_down_bwd(order, dx3b, wdown, u):
    tm = 512

    def body(d_ref, w_ref, u_ref, du_ref):
        da = _dot_nt(d_ref[...], w_ref[...])
        du_ref[...] = (da * (2.0 * u_ref[...].astype(f32))).astype(bf16)

    return _call(
        order, body, (dx3b, wdown, u), name="mlp_down_bwd", grid=(NCHIP, S // tm),
        in_specs=[pl.BlockSpec((tm, D), lambda q, i: (i, 0)), pl.BlockSpec((D, D), lambda q, i: (q, 0)),
                  pl.BlockSpec((tm, D), lambda q, i: (i, q))],
        out_specs=pl.BlockSpec((tm, D), lambda q, i: (i, q)),
        out_shape=jax.ShapeDtypeStruct((S, DFF), bf16),
        compiler_params=_params(("parallel", "parallel")),
    )


def _mlp_up_bwd(order, du, wups, x2, dx3, g2):
    tm = 512

    def body(du_ref, w_ref, x2_ref, dx3_ref, g_ref, dx2_ref, dx2b_ref, dg_ref):
        i = pl.program_id(0)
        dh = jnp.zeros((tm, D), f32)
        for q in range(NCHIP):
            dh = dh + _dot_nt(du_ref[:, q * D:(q + 1) * D], w_ref[q])
        x2 = x2_ref[...]
        r = lax.rsqrt(jnp.mean(x2 * x2, axis=-1, keepdims=True) + EPS)
        xh = x2 * r
        gdh = dh * g_ref[...]
        dx2 = dx3_ref[...] + r * (gdh - xh * jnp.mean(gdh * xh, axis=-1, keepdims=True))
        dx2_ref[...] = dx2
        dx2b_ref[...] = dx2.astype(bf16)

        @pl.when(i == 0)
        def _():
            dg_ref[...] = jnp.zeros_like(dg_ref)

        dg_ref[...] += jnp.sum(dh * xh, axis=0, keepdims=True)

    row = pl.BlockSpec((tm, D), lambda i: (i, 0))
    vec = pl.BlockSpec((1, D), lambda i: (0, 0))
    return _call(
        order, body, (du, wups, x2, dx3, g2), name="mlp_up_bwd", grid=(S // tm,),
        in_specs=[pl.BlockSpec((tm, DFF), lambda i: (i, 0)), pl.BlockSpec((NCHIP, D, D), lambda i: (0, 0, 0)),
                  row, row, vec],
        out_specs=[row, row, vec],
        out_shape=[jax.ShapeDtypeStruct((S, D), f32), jax.ShapeDtypeStruct((S, D), bf16),
                   jax.ShapeDtypeStruct((1, D), f32)],
        compiler_params=_params(("arbitrary",)),
    )


def _gate_bwd(order, dx2b, wout, gates, ya, yb):
    tm = 512

    def body(d_ref, w_ref, g_ref, ya_ref, yb_ref, dya_ref, dyb_ref, dproj_ref):
        dm = _dot_nt(d_ref[...], w_ref[...])
        sa = jax.nn.sigmoid(g_ref[:, 0:D].astype(f32))
        sb = jax.nn.sigmoid(g_ref[:, D:2 * D].astype(f32))
        dya_ref[...] = (dm * sa).astype(bf16)
        dyb_ref[...] = (dm * sb).astype(bf16)
        dproj_ref[:, 0:D] = (dm * ya_ref[...].astype(f32) * (sa * (1.0 - sa))).astype(bf16)
        dproj_ref[:, D:2 * D] = (dm * yb_ref[...].astype(f32) * (sb * (1.0 - sb))).astype(bf16)

    row = lambda w: pl.BlockSpec((tm, w), lambda i: (i, 0))
    return _call(
        order, body, (dx2b, wout, gates, ya, yb), name="gate_bwd", grid=(S // tm,),
        in_specs=[row(D), pl.BlockSpec((D, D), lambda i: (0, 0)), row(2 * D), row(D), row(D)],
        out_specs=[row(D), row(D), pl.BlockSpec((tm, 2 * D), lambda i: (i, F_G // (2 * D)))],
        out_shape=[jax.ShapeDtypeStruct((S, D), bf16), jax.ShapeDtypeStruct((S, D), bf16),
                   jax.ShapeDtypeStruct((S, NP), bf16)],
        compiler_params=_params(("parallel",)),
    )


def _branch_bwd(order, dya, dyb, was, wbs):
    tm = 512

    def body(dya_ref, dyb_ref, wa_ref, wb_ref, doa_ref, dob_ref):
        doa = jnp.zeros((tm, FOXW), f32)
        dob = jnp.zeros((tm, DILOUT), f32)
        for q in range(NCHIP):
            cols = slice(q * 256, (q + 1) * 256)
            doa = doa + _dot_nt(dya_ref[:, cols], wa_ref[q])
            dob = dob + _dot_nt(dyb_ref[:, cols], wb_ref[q])
        doa_ref[...] = doa.astype(bf16)
        dob_ref[...] = dob

    row = lambda w: pl.BlockSpec((tm, w), lambda i: (i, 0))
    full3 = lambda a: pl.BlockSpec(a.shape, lambda i: (0, 0, 0))
    return _call(
        order, body, (dya, dyb, was, wbs), name="branch_bwd", grid=(S // tm,),
        in_specs=[row(D), row(D), full3(was), full3(wbs)],
        out_specs=[row(FOXW), row(DILOUT)],
        out_shape=[jax.ShapeDtypeStruct((S, FOXW), bf16), jax.ShapeDtypeStruct((S, DILOUT), f32)],
        compiler_params=_params(("parallel",)),
    )


def _branch_wgrad(order, oa, ob, dya, dyb):
    def body(oa_ref, ob_ref, dya_ref, dyb_ref, dwa_ref, dwb_ref):
        dwa_ref[...] = _dot_tn(oa_ref[...], dya_ref[...])
        dwb_ref[...] = _dot_tn(ob_ref[...], dyb_ref[...])

    full = lambda w: pl.BlockSpec((S, w), lambda q: (0, 0))
    colq = pl.BlockSpec((S, 256), lambda q: (0, q))
    return _call(
        order, body, (oa, ob, dya, dyb), name="branch_wgrad", grid=(NCHIP,),
        in_specs=[full(FOXW), full(DILOUT), colq, colq],
        out_specs=[pl.BlockSpec((None, FOXW, 256), lambda q: (q, 0, 0)),
                   pl.BlockSpec((None, DILOUT, 256), lambda q: (q, 0, 0))],
        out_shape=[jax.ShapeDtypeStruct((NCHIP, FOXW, 256), f32), jax.ShapeDtypeStruct((NCHIP, DILOUT, 256), f32)],
        compiler_params=_params(("parallel",)),
    )


def _fox_bwd(order, qkva, doa, oa, lse, F, dproj):
    tq, tk = FOX_TQ, FOX_TK

    def body(qkv_ref, do_ref, o_ref, lse_ref, F_ref, _dproj_in, dF_ref, dqkv_ref, qa, ka, da, va, kat,
             dk_scr, dv_scr, dqt_scr):
        p = pl.program_id(0)
        lane, hm = _head_masks()
        keyi = lax.broadcasted_iota(jnp.int32, (tk, 1), 0)
        qryi = lax.broadcasted_iota(jnp.int32, (1, tq), 1)

        def prep(i, c):
            rows = pl.ds(pl.multiple_of(i * tk, tk), tk)
            _fox_operands(qkv_ref, F_ref, lse_ref, qa, ka, p, rows)
            do = do_ref[rows, :].astype(f32)
            prod = do * o_ref[rows, :].astype(f32)
            v = qkv_ref[rows, 256:384].astype(f32)
            for hh in (0, 1):
                free = (1 - hh) * HD
                delta = jnp.sum(jnp.where(hm[hh], prod, 0.0), axis=1, keepdims=True)
                da[hh, rows, :] = _set_lanes(jnp.where(hm[hh], do, 0.0), lane, free,
                                             [-t for t in _f32_parts(delta)]).astype(bf16)
                va[hh, rows, :] = _set_lanes(v, lane, free, [1.0] * 3).astype(bf16)
                kat[hh, i] = ka[hh, rows, :].astype(f32).T.astype(bf16)
                dk_scr[hh, rows, :] = jnp.zeros((tk, 128), f32)
                dv_scr[hh, rows, :] = jnp.zeros((tk, 128), f32)
            return c

        lax.fori_loop(0, S // tk, prep, 0)

        def qblock(i, first_half):
            r0 = pl.multiple_of(i * tq, tq)
            qrows = pl.ds(r0, tq)
            qh = [qa[hh, qrows, :] for hh in (0, 1)]
            dh = [da[hh, qrows, :] for hh in (0, 1)]
            dqt_scr[...] = jnp.zeros_like(dqt_scr)

            def kv(jb, c2, masked, width):
                keys = pl.ds(pl.multiple_of(jb * tk, tk), width)
                sts = [_dot_nt(ka[hh, keys, :], qh[hh]) for hh in (0, 1)]
                dps = [_dot_nt(va[hh, keys, :], dh[hh]) for hh in (0, 1)]
                for hh in (0, 1):
                    pt = jnp.exp(sts[hh])
                    if masked:
                        pt = jnp.where(jb * tk + keyi[0:width] <= r0 + qryi, pt, 0.0)
                    dsb = (pt * dps[hh]).astype(bf16)
                    dv_scr[hh, keys, :] += _dot(pt.astype(bf16), dh[hh])
                    dk_scr[hh, keys, :] += _dot(dsb, qh[hh])
                    dqt_scr[hh] += _dot(kat[hh, jb, :, 0:width], dsb)
                return c2

            last = (r0 + tq - 1) // tk
            lax.fori_loop(0, last, lambda j, c2: kv(j, c2, False, tk), 0)
            kv(last, 0, True, tk // 2 if first_half else tk)
            dq0, dq1 = dqt_scr[0].T, dqt_scr[1].T
            dqkv_ref[qrows, 0:128] = (jnp.where(hm[0], dq0, dq1) * 0.125).astype(bf16)
            dF_ref[qrows, :] = jnp.where(lane == 0, dq0[:, HD:HD + 1], jnp.where(lane == 1, dq1[:, 0:1], 0.0))

        def qpair(t, c):
            qblock(2 * t, True)
            qblock(2 * t + 1, False)
            return c

        assert tk == 2 * tq
        lax.fori_loop(0, S // tk, qpair, 0)

        def finish(i, c):
            rows = pl.ds(pl.multiple_of(i * tq, tq), tq)
            dk0, dk1 = dk_scr[0, rows, :], dk_scr[1, rows, :]
            dqkv_ref[rows, 128:256] = jnp.where(hm[0], dk0, dk1).astype(bf16)
            dqkv_ref[rows, 256:384] = jnp.where(hm[0], dv_scr[0, rows, :], dv_scr[1, rows, :]).astype(bf16)
            cs = jnp.where(lane == 0, dk0[:, HD + L_ONE:HD + L_ONE + 1],
                           jnp.where(lane == 1, dk1[:, L_ONE:L_ONE + 1], 0.0))
            dF_ref[rows, :] = dF_ref[rows, :] - cs
            return c

        lax.fori_loop(0, S // tq, finish, 0)

    pair = pl.BlockSpec((S, 128), lambda p: (0, p))
    return _call(
        order, body, (qkva, doa, oa, lse, F, dproj), name="fox_bwd", grid=(4,),
        in_specs=[pl.BlockSpec((S, FOX_BLK), lambda p: (0, p)), pair, pair, pair,
                  pl.BlockSpec((S, 128), lambda p: (0, 0)), pl.BlockSpec(memory_space=pl.ANY)],
        out_specs=[pair, pl.BlockSpec((S, FOX_BLK), lambda p: (0, F_FOX // FOX_BLK + p))],
        out_shape=[jax.ShapeDtypeStruct((S, FOXW), f32), jax.ShapeDtypeStruct((S, NP), bf16)],
        input_output_aliases={5: 1},
        scratch_shapes=[pltpu.VMEM((2, S, 128), bf16)] * 4 + [pltpu.VMEM((2, S // tk, 128, tk), bf16)]
        + [pltpu.VMEM((2, S, 128), f32)] * 2 + [pltpu.VMEM((2, 128, tq), f32)],
        compiler_params=_params(("parallel",)),
    )


def _forget_bwd(order, dF, fa, bpad, dproj):
    nb = S // TQ

    def body(dF_ref, fa_ref, b_ref, _dproj_in, db_ref, dfa_ref):
        rr = lax.broadcasted_iota(jnp.int32, (TQ, TQ), 0)
        cc = lax.broadcasted_iota(jnp.int32, (TQ, TQ), 1)
        upper = (cc >= rr).astype(bf16)
        lane = lax.broadcasted_iota(jnp.int32, (1, 128), 1)
        carry = jnp.zeros((1, 128), f32)
        db = jnp.zeros((1, 128), f32)
        for b in reversed(range(nb)):
            cols = jnp.zeros((TQ, 128), f32)
            for h in range(8):
                c0 = (h // 2) * 128 + h % 2
                cols = jnp.where(lane == h, dF_ref[b * TQ:(b + 1) * TQ, c0:c0 + 1], cols)
            dlf = carry
            for part in _split3(cols):
                dlf = dlf + _dot(upper, part)
            carry = carry + jnp.sum(cols, axis=0, keepdims=True)
            z = fa_ref[b * TQ:(b + 1) * TQ, :] + b_ref[...]
            dz = jnp.where(lane < 8, dlf * jax.nn.sigmoid(-z), 0.0)
            dfa_ref[b * TQ:(b + 1) * TQ, 0:128] = dz.astype(bf16)
            dfa_ref[b * TQ:(b + 1) * TQ, 128:256] = jnp.zeros((TQ, 128), bf16)
            db = db + jnp.sum(dz, axis=0, keepdims=True)
        db_ref[...] = db

    whole = lambda a: pl.BlockSpec(a.shape, lambda i: (0,) * a.ndim)
    return _call(
        order, body, (dF, fa, bpad, dproj), name="forget_bwd", grid=(1,),
        in_specs=[whole(dF), whole(fa), whole(bpad), pl.BlockSpec(memory_space=pl.ANY)],
        out_specs=[pl.BlockSpec((1, 128), lambda i: (0, 0)), pl.BlockSpec((S, 256), lambda i: (0, F_FA // 256))],
        out_shape=[jax.ShapeDtypeStruct((1, 128), f32), jax.ShapeDtypeStruct((S, NP), bf16)],
        input_output_aliases={3: 1},
        compiler_params=_params(("arbitrary",)),
    )


def _dil_bwd(order, qkvb, dob, ob, lseb, rope, dproj):
    c_t, s1_t, s2_t = rope

    def body(*refs):
        q_refs, k_refs, v_refs = refs[0:3], refs[3:6], refs[6:9]
        dob_ref, ob_ref, lse_ref, c_ref, s1_ref, s2_ref, _dproj_in, dqkv_ref = refs[9:17]
        qp, kp, vp, dop, lp, dlp, dln, dqp, dkp, dvp, nat = refs[17:28]
        dq_out, dk_out, dv_out = _dil_views(dqkv_ref)
        _, hm = _head_masks()

        def delta_rows(i, c):
            r0 = pl.multiple_of(i * TQ, TQ)
            prod = dob_ref[pl.ds(r0, TQ), :] * ob_ref[pl.ds(r0, TQ), :].astype(f32)
            d0 = jnp.sum(jnp.where(hm[0], prod, 0.0), axis=1, keepdims=True)
            d1 = jnp.sum(jnp.where(hm[1], prod, 0.0), axis=1, keepdims=True)
            dln[pl.ds(r0, TQ), :] = jnp.where(hm[0], d0, d1)
            return c

        lax.fori_loop(0, S // TQ, delta_rows, 0)

        for g, r in enumerate(DIL):
            nbl = S // r // BAND
            if r == 1:
                srcs = (q_refs[g], k_refs[g], v_refs[g], dob_ref, lse_ref, dln)
            else:
                for dst, src in ((qp, q_refs[g]), (kp, k_refs[g]), (vp, v_refs[g]), (dop, dob_ref),
                                 (lp, lse_ref), (dlp, dln)):
                    _permute_in(dst, src, r)
                srcs = (qp, kp, vp, dop, lp, dlp)
            dkp[...] = jnp.zeros_like(dkp)
            dvp[...] = jnp.zeros_like(dvp)

            def blk(t, c, srcs=srcs, nbl=nbl):
                qs_, ks_, vs_, dos_, ls_, dls_ = srcs
                work = []
                for u in range(DIL_UNROLL):
                    r0, k0, valid = _band_geometry(DIL_UNROLL * t + u, nbl)
                    q = qs_[pl.ds(r0, BAND), :] * 0.125
                    kwf = ks_[pl.ds(k0, 2 * BAND), :]
                    kw = kwf.astype(bf16)
                    vw = vs_[pl.ds(k0, 2 * BAND), :].astype(bf16)
                    do = dos_[pl.ds(r0, BAND), :]
                    lse = ls_[pl.ds(r0, BAND), :]
                    dlt = dls_[pl.ds(r0, BAND), :]
                    for hh in (0, 1):
                        qh = jnp.where(hm[hh], q, 0.0).astype(bf16)
                        doh = jnp.where(hm[hh], do, 0.0).astype(bf16)
                        kh = jnp.where(hm[hh], kwf, 0.0).astype(bf16)
                        work.append((u, hh, r0, k0, valid, qh, doh, kh, lse[:, hh * HD:hh * HD + 1],
                                     dlt[:, hh * HD:hh * HD + 1], _dot_nt(qh, kw), _dot_nt(doh, vw)))
                for u, hh, r0, k0, valid, qh, doh, kh, lse_h, dlt_h, s, dp in work:
                    if hh == 0:
                        dq = jnp.zeros((BAND, 128), f32)
                        dk = jnp.zeros((2 * BAND, 128), f32)
                        dv = jnp.zeros((2 * BAND, 128), f32)
                    pr = jnp.where(valid, jnp.exp(s - lse_h), 0.0)
                    dsb = (pr * (dp - dlt_h)).astype(bf16)
                    dv = dv + _dot_tn(pr.astype(bf16), doh)
                    dk = dk + _dot_tn(dsb, qh)
                    dq = dq + _dot(dsb, kh)
                    if hh == 1:
                        dqp[pl.ds(r0, BAND), :] = dq * 0.125
                        dkp[pl.ds(k0, 2 * BAND), :] += dk
                        dvp[pl.ds(k0, 2 * BAND), :] += dv
                return c

            lax.fori_loop(0, S // BAND // DIL_UNROLL, blk, 0)

            for acc, out, roped in ((dqp, dq_out[g], True), (dkp, dk_out[g], True), (dvp, dv_out[g], False)):
                if r == 1:
                    src = acc
                else:
                    _permute_out(nat, acc, r)
                    src = nat

                def emit(i, c, src=src, out=out, roped=roped):
                    r0 = pl.multiple_of(i * TQ, TQ)
                    d = src[pl.ds(r0, TQ), :]
                    if roped:
                        d = (d * c_ref[pl.ds(r0, TQ), :] + pltpu.roll(d * s1_ref[pl.ds(r0, TQ), :], 8, 1)
                             + pltpu.roll(d * s2_ref[pl.ds(r0, TQ), :], 120, 1))
                    out[pl.ds(r0, TQ), :] = d.astype(bf16)
                    return c

                lax.fori_loop(0, S // TQ, emit, 0)

    pair = pl.BlockSpec((S, 128), lambda p: (0, p))
    tab = pl.BlockSpec((S, 128), lambda p: (0, 0))
    blk_spec = pl.BlockSpec((S, DIL_BLK), lambda p: (0, p))
    return _call(
        order, body, [qkvb] * 9 + [dob, ob, lseb, c_t, s1_t, s2_t, dproj], name="dil_bwd", grid=(2,),
        in_specs=_dil_in_specs() + [pair, pair, pair, tab, tab, tab, pl.BlockSpec(memory_space=pl.ANY)],
        out_specs=blk_spec,
        out_shape=jax.ShapeDtypeStruct((S, NP), bf16),
        input_output_aliases={15: 0},
        scratch_shapes=[pltpu.VMEM((S, 128), f32)] * 11,
        compiler_params=_params(("parallel",)),
    )


def _inproj_bwd(order, dproj, wt, x, dx2, g1):
    tm = 256

    def body(d_ref, w_ref, x_ref, dx2_ref, g_ref, dx_ref, dg_ref):
        i = pl.program_id(0)
        dh = _dot(d_ref[...], w_ref[...])
        xb = x_ref[...]
        r = lax.rsqrt(jnp.mean(xb * xb, axis=-1, keepdims=True) + EPS)
        xh = xb * r
        gdh = dh * g_ref[...]
        dx_ref[...] = dx2_ref[...] + r * (gdh - xh * jnp.mean(gdh * xh, axis=-1, keepdims=True))

        @pl.when(i == 0)
        def _():
            dg_ref[...] = jnp.zeros_like(dg_ref)

        dg_ref[...] += jnp.sum(dh * xh, axis=0, keepdims=True)

    row = pl.BlockSpec((tm, D), lambda i: (i, 0))
    vec = pl.BlockSpec((1, D), lambda i: (0, 0))
    return _call(
        order, body, (dproj, wt, x, dx2, g1), name="inproj_bwd", grid=(S // tm,),
        in_specs=[pl.BlockSpec((tm, NP), lambda i: (i, 0)), pl.BlockSpec((NP, D), lambda i: (0, 0)), row, row, vec],
        out_specs=[row, vec],
        out_shape=[jax.ShapeDtypeStruct((S, D), f32), jax.ShapeDtypeStruct((1, D), f32)],
        compiler_params=_params(("arbitrary",)),
    )


HBM = pl.BlockSpec(memory_space=pltpu.HBM)
SEM = pl.BlockSpec(memory_space=pltpu.SEMAPHORE)
SMALL_ROWS = 8


def _comm_call(name, body, bufs, order, sems_in=(), new_sems=(), behind=()):
    nb, ns, nn = len(bufs), len(sems_in), len(new_sems)
    extra = order.token_for(bufs) + list(behind)

    def kern(*refs):
        off = nb + ns + len(extra)
        body(refs[:nb], refs[nb:nb + ns], refs[off:off + nn])
        refs[-1][...] = jnp.zeros((8, 128), f32)

    res = pl.pallas_call(
        kern, name=name,
        in_specs=[HBM] * nb + [SEM] * ns + [pl.BlockSpec(memory_space=pl.ANY)] * len(extra),
        out_specs=[SEM] * nn + [HBM] * nb + [pl.BlockSpec(memory_space=pltpu.VMEM)],
        out_shape=[pltpu.SemaphoreType.DMA((k,)) for k in new_sems] + [pltpu.HBM(b.shape, b.dtype) for b in bufs]
        + [jax.ShapeDtypeStruct((8, 128), f32)],
        input_output_aliases={i: nn + i for i in range(nb)},
        compiler_params=pltpu.CompilerParams(has_side_effects=pltpu.SideEffectType.DATAFLOW_SIDE_EFFECTING),
    )(*[pltpu.with_memory_space_constraint(b, pltpu.HBM) for b in bufs], *sems_in, *extra)
    order.mark(res[-1])
    return list(res[:nn]), list(res[nn:nn + nb])


def _place():
    x, y, c = lax.axis_index("x"), lax.axis_index("y"), lax.axis_index("c")
    chips = [(1 - x, y), (x, 1 - y), (1 - x, 1 - y)]
    return x, y, c, chips


def _rcopy(src, dst, ssem, rsem, dev):
    return pltpu.make_async_remote_copy(src_ref=src, dst_ref=dst, send_sem=ssem, recv_sem=rsem,
                                        device_id=dev, device_id_type=pl.DeviceIdType.MESH)


def _half(nrows, which):
    return pl.ds(which * (nrows // 2), nrows // 2)


def _ici_copies(stack, group_sizes, ssems, rsems):
    x, y, c, chips = _place()
    me_q = 2 * x + y
    sends, recvs = [], []
    a = 0
    for grp, size in enumerate(group_sizes):
        for k in range(size):
            rows = _half(stack[a].shape[1], c)
            for j, (cx, cy) in enumerate(chips):
                mine = stack[a].at[me_q, rows]
                sends.append(_rcopy(mine, mine, ssems[grp].at[k * 3 + j], rsems[grp].at[k * 3 + j], (cx, cy, c)))
                theirs = stack[a].at[2 * cx + cy, rows]
                recvs.append(_rcopy(theirs, theirs, ssems[grp].at[k * 3 + j], rsems[grp].at[k * 3 + j],
                                    (cx, cy, c)))
            a += 1
    return sends, recvs


def _allgather_start(name, stacks, order):
    n = len(stacks)

    def body(bufs, _, new):
        sends, _r = _ici_copies(bufs, [n], [new[0]], [new[1]])
        for cp in sends:
            cp.start()

    return _comm_call(name, body, stacks, order, new_sems=(3 * n, 3 * n))


def _forward_copies(stack, ssem, rsem):
    x, y, c, chips = _place()
    sib = (x, y, 1 - c)
    sends, recvs = [], []
    for a in range(len(stack)):
        for j, (cx, cy) in enumerate(chips):
            landed = stack[a].at[2 * cx + cy, _half(stack[a].shape[1], c)]
            sends.append(_rcopy(landed, landed, ssem.at[a * 3 + j], rsem.at[a * 3 + j], sib))
            other = stack[a].at[2 * cx + cy, _half(stack[a].shape[1], 1 - c)]
            recvs.append(_rcopy(other, other, ssem.at[a * 3 + j], rsem.at[a * 3 + j], sib))
    return sends, recvs


def _allgather_forward(name, stacks, sems, order, behind=()):
    n = len(stacks)

    def body(bufs, taken, new):
        sends, recvs = _ici_copies(bufs, [n], [taken[0]], [taken[1]])
        fwd, _r = _forward_copies(bufs, new[0], new[1])
        for arrived, onward in zip(recvs, fwd):
            arrived.wait_recv()
            onward.start()
        for cp in sends:
            cp.wait_send()

    return _comm_call(name, body, stacks, order, sems_in=sems, new_sems=(3 * n, 3 * n), behind=behind)


def _allgather_finish(name, stacks, sems, order):
    def body(bufs, taken, _):
        sends, recvs = _forward_copies(bufs, taken[0], taken[1])
        for cp in sends:
            cp.wait_send()
        for cp in recvs:
            cp.wait_recv()

    return _comm_call(name, body, stacks, order, sems_in=sems)[1]


def _window_unit(q, j):
    return C2I[WIN_UNIT0[q] + j]


def _pair_copies(g, t, ssem, rsem, gathered):
    x, y, c, _ = _place()
    sib = (x, y, 1 - c)
    cps, whole = [], []
    for a in range(len(g)):
        if a == 0 and gathered:
            for q in range(NCHIP):
                for j in range(WIN_UNITS // 2):
                    u = jnp.where(c == 0, _window_unit(q, WIN_UNITS // 2 + j), _window_unit(q, j))
                    src = g[0].at[pl.ds(pl.multiple_of(u * UNIT, UNIT), UNIT), :]
                    cps.append(_rcopy(src, t[0].at[q, pl.ds(j * UNIT, UNIT), :], ssem.at[0], rsem.at[0], sib))
            whole.append(_rcopy(t[0], t[0], ssem.at[0], rsem.at[0], sib))
        else:
            cp = _rcopy(g[a].at[:, _half(g[a].shape[1], 1 - c), :], t[a], ssem.at[a], rsem.at[a], sib)
            cps.append(cp)
            whole.append(cp)
    return cps, whole


def _comm_multi(name, parts, order):
    def body(buf_refs, taken, new):
        ib = it = inew = 0
        for pbody, pbufs, psems, pnew, _ in parts:
            pbody(buf_refs[ib:ib + len(pbufs)], taken[it:it + len(psems)], new[inew:inew + len(pnew)])
            ib, it, inew = ib + len(pbufs), it + len(psems), inew + len(pnew)

    sems, bufs = _comm_call(name, body, [b for p in parts for b in p[1]], order,
                            sems_in=[s for p in parts for s in p[2]], new_sems=[k for p in parts for k in p[3]])
    out, ib, inew = [], 0, 0
    for _, pbufs, _, pnew, unpack in parts:
        out.append(unpack(sems[inew:inew + len(pnew)], bufs[ib:ib + len(pbufs)]))
        ib, inew = ib + len(pbufs), inew + len(pnew)
    return out


def _pair_start_part(gs, gathered=False):
    n = len(gs)
    ts = [lax.empty((NCHIP, WIN_ROWS // 2, D) if (a == 0 and gathered) else (NCHIP, g.shape[1] // 2, g.shape[2]), f32)
          for a, g in enumerate(gs)]

    def body(bufs, _, new):
        for cp in _pair_copies(bufs[:n], bufs[n:], new[0], new[1], gathered)[0]:
            cp.start()

    return body, list(gs) + ts, (), (n, n), lambda sems, bufs: (sems, bufs)


def _pair_wait_part(bufs, sems, gathered=False):
    n = len(bufs) // 2

    def body(refs, taken, _):
        for cp in _pair_copies(refs[:n], refs[n:], taken[0], taken[1], gathered)[1]:
            cp.wait_send()
            cp.wait_recv()

    return body, list(bufs), list(sems), (), lambda _, out: (out[:n], out[n:])


def _row_tile(h):
    return min(h, 256)


def _pair_add(order, g, t, c_arr, name):
    _, R, C = g.shape
    h = R // 2
    tr = _row_tile(h)
    nblk = h // tr

    def body(c_ref, g_ref, t_ref, p32_ref, p16_ref):
        s = g_ref[...] + t_ref[...]
        p32_ref[...] = s
        p16_ref[...] = s.astype(bf16)

    blk = pl.BlockSpec((None, tr, C), lambda q, i, c_ref: (q, i, 0))
    return _call_indexed(
        order, body, (c_arr,), (g, t), (NCHIP, nblk),
        [pl.BlockSpec((None, tr, C), lambda q, i, c_ref: (q, c_ref[0] * nblk + i, 0)), blk], [blk, blk],
        name=name,
        out_shape=[jax.ShapeDtypeStruct((NCHIP, h, C), f32), jax.ShapeDtypeStruct((NCHIP, h, C), bf16)],
        compiler_params=_params(("parallel", "parallel")),
    )


def _pair_add_gathered(order, dwt, t, c_arr, name):
    half_units, half_rows = WIN_UNITS // 2, WIN_ROWS // 2
    table = jnp.asarray([_window_unit(q, j) for q in range(NCHIP) for j in range(WIN_UNITS)], jnp.int32)

    def body(tab_ref, c_ref, g_hbm, t_ref, p32_ref, p16_ref, buf, sem):
        q = pl.program_id(0)

        def gather(w, slot):
            cps = []
            for j in range(half_units):
                u = tab_ref[w * WIN_UNITS + c_ref[0] * half_units + j]
                cps.append(pltpu.make_async_copy(g_hbm.at[pl.ds(pl.multiple_of(u * UNIT, UNIT), UNIT), :],
                                                 buf.at[slot, pl.ds(j * UNIT, UNIT), :], sem.at[slot]))
            return cps

        @pl.when(q == 0)
        def _():
            for cp in gather(0, 0):
                cp.start()

        @pl.when(q + 1 < NCHIP)
        def _():
            for cp in gather(q + 1, (q + 1) % 2):
                cp.start()

        slot = q % 2
        pltpu.make_async_copy(buf.at[slot], buf.at[slot], sem.at[slot]).wait()
        s = buf[slot] + t_ref[...]
        p32_ref[...] = s
        p16_ref[...] = s.astype(bf16)

    blk = pl.BlockSpec((None, half_rows, D), lambda q, tab_ref, c_ref: (q, 0, 0))
    return _call_indexed(
        order, body, (table, c_arr), (dwt, t), (NCHIP,),
        [pl.BlockSpec(memory_space=pl.ANY), blk], [blk, blk],
        scratch_shapes=[pltpu.VMEM((2, half_rows, D), f32), pltpu.SemaphoreType.DMA((2,))],
        name=name,
        out_shape=[jax.ShapeDtypeStruct((NCHIP, half_rows, D), f32),
                   jax.ShapeDtypeStruct((NCHIP, half_rows, D), bf16)],
        compiler_params=_params(("arbitrary",)),
    )


def _shard_copies(p, r, sm, ssem, rsem):
    x, y, c, chips = _place()
    n = len(p)
    sends, recvs = [], []
    for a in range(n):
        for j, (cx, cy) in enumerate(chips):
            k = a * 3 + j
            sends.append(_rcopy(p[a].at[2 * cx + cy], r[a].at[j], ssem.at[k], rsem.at[k], (cx, cy, c)))
            recvs.append(_rcopy(r[a].at[j], r[a].at[j], ssem.at[k], rsem.at[k], (cx, cy, c)))
    if sm is not None:
        mine = sm.at[4 * x + 2 * y + c]
        for i in range(1, 8):
            px = (1 - x) if i & 4 else x
            py = (1 - y) if i & 2 else y
            pc = (1 - c) if i & 1 else c
            k = 3 * n + i - 1
            sends.append(_rcopy(mine, mine, ssem.at[k], rsem.at[k], (px, py, pc)))
            slot = sm.at[4 * px + 2 * py + pc]
            recvs.append(_rcopy(slot, slot, ssem.at[k], rsem.at[k], (px, py, pc)))
    return sends, recvs


def _shard_start_part(p16s, sm=None):
    n = len(p16s)
    rs = [lax.empty((3,) + p.shape[1:], bf16) for p in p16s]
    extra = [] if sm is None else [sm]
    nsem = 3 * n + (7 if sm is not None else 0)

    def body(bufs, _, new):
        sends, _r = _shard_copies(bufs[:n], bufs[n:2 * n], bufs[2 * n] if extra else None, new[0], new[1])
        for cp in sends:
            cp.start()

    return body, list(p16s) + rs + extra, (), (nsem, nsem), lambda sems, bufs: (sems, bufs)


def _shard_wait_part(bufs, sems, n):
    has_sm = len(bufs) > 2 * n

    def body(refs, taken, _):
        sends, recvs = _shard_copies(refs[:n], refs[n:2 * n], refs[2 * n] if has_sm else None, taken[0], taken[1])
        for cp in sends:
            cp.wait_send()
        for cp in recvs:
            cp.wait_recv()

    return body, list(bufs), list(sems), (), lambda _, out: (out[n:2 * n], (out[2 * n] if has_sm else None))


def _shard_sum(order, p32, r, q_arr, c_arr, name):
    _, h, C = p32.shape
    tr = _row_tile(h)
    nblk = h // tr

    def body(q_ref, c_ref, p_ref, r_ref, o_ref):
        s = p_ref[...]
        for j in range(3):
            s = s + r_ref[j].astype(f32)
        o_ref[...] = s

    return _call_indexed(
        order, body, (q_arr, c_arr), (p32, r), (nblk,),
        [pl.BlockSpec((None, tr, C), lambda i, q_ref, c_ref: (q_ref[0], i, 0)),
         pl.BlockSpec((3, tr, C), lambda i, q_ref, c_ref: (0, i, 0))],
        pl.BlockSpec((tr, C), lambda i, q_ref, c_ref: (c_ref[0] * nblk + i, 0)),
        name=name, out_shape=jax.ShapeDtypeStruct((2 * h, C), f32),
        compiler_params=_params(("parallel",)),
    )


def _swap_copies(full, ssem, rsem):
    x, y, c, _ = _place()
    sends, recvs = [], []
    for a in range(len(full)):
        mine = full[a].at[_half(full[a].shape[0], c)]
        sends.append(_rcopy(mine, mine, ssem.at[a], rsem.at[a], (x, y, 1 - c)))
        other = full[a].at[_half(full[a].shape[0], 1 - c)]
        recvs.append(_rcopy(other, other, ssem.at[a], rsem.at[a], (x, y, 1 - c)))
    return sends, recvs


def _swap_start_part(fulls):
    n = len(fulls)

    def body(bufs, _, new):
        for cp in _swap_copies(bufs, new[0], new[1])[0]:
            cp.start()

    return body, list(fulls), (), (n, n), lambda sems, bufs: (sems, bufs)


def _swap_wait_part(fulls, sems):
    def body(refs, taken, _):
        sends, recvs = _swap_copies(refs, taken[0], taken[1])
        for cp in sends:
            cp.wait_send()
        for cp in recvs:
            cp.wait_recv()

    return body, list(fulls), list(sems), (), lambda _, out: out


def _small_sum(order, sm):
    def body(sm_ref, o_ref):
        s = sm_ref[0]
        for d in range(1, 8):
            s = s + sm_ref[d]
        o_ref[...] = s

    return _call(order, body, (sm,), name="small_grad_sum", out_shape=jax.ShapeDtypeStruct((SMALL_ROWS, D), f32))


def _adamw_math(w, g, m, v):
    m = ADAM_B1 * m + (1.0 - ADAM_B1) * g
    v = ADAM_B2 * v + (1.0 - ADAM_B2) * (g * g)
    m_hat = m / (1.0 - ADAM_B1 ** ADAM_STEP)
    v_hat = v / (1.0 - ADAM_B2 ** ADAM_STEP)
    return -ADAM_LR * (m_hat / (jnp.sqrt(v_hat) + ADAM_EPS) + ADAM_WD * w), m, v


def _adamw_small(order, ws, gs, ms, vs, name):
    n = len(ws)

    def body(*refs):
        for i in range(n):
            res = _adamw_math(*[refs[k * n + i][...] for k in range(4)])
            for k in range(3):
                refs[4 * n + 3 * i + k][...] = res[k]

    out = _call(order, body, list(ws) + list(gs) + list(ms) + list(vs), name=name,
                out_shape=[jax.ShapeDtypeStruct(w.shape, f32) for w in ws for _ in range(3)])
    return [out[3 * i:3 * i + 3] for i in range(n)]


def _adamw(order, w, g, m, v, name):
    R, C = w.shape
    if R <= 256 or R % 256 == 0:
        tr, tc = min(R, 256), C
    else:
        tr, tc = R, 128

    def body(w_ref, g_ref, m_ref, v_ref, d_ref, nm_ref, nv_ref):
        d_ref[...], nm_ref[...], nv_ref[...] = _adamw_math(w_ref[...], g_ref[...], m_ref[...], v_ref[...])

    blk = pl.BlockSpec((tr, tc), lambda i, j: (i, j))
    return _call(
        order, body, (w, g, m, v), name=name, grid=(R // tr, C // tc), in_specs=[blk] * 4, out_specs=[blk] * 3,
        out_shape=[jax.ShapeDtypeStruct((R, C), f32)] * 3,
        compiler_params=_params(("parallel", "parallel")),
    )


def _feature_major(w):
    return jnp.transpose(w, (2, 0, 1)).reshape(SHARD_IN, D)


def _unfeature_major(a):
    return jnp.transpose(a.reshape(SHARD_IN, 1, D), (1, 2, 0))


def _window_of(wt, q):
    def plain(k):
        return lambda w: jnp.pad(w, ((OWN_ROW0[k], WIN_ROWS - OWN_ROW0[k] - SHARD_IN), (0, 0))).astype(bf16)

    def chip1(w):
        lo = jnp.pad(w[0:62], ((2, WIN_ROWS - 64), (0, 0)))
        hi = jnp.pad(w[70:SHARD_IN], ((64, WIN_ROWS - 64 - (SHARD_IN - 70)), (0, 0)))
        return (lo + hi).astype(bf16)

    win = lax.switch(q, [plain(0), chip1, plain(2), plain(3)], wt)
    fa = jnp.pad(wt[62:70], ((0, FA_ROWS - 8), (0, 0))).astype(bf16)
    return win, fa


def _own_rows(gwin, gfa, q):
    def plain(k):
        return lambda gw, gf: gw[OWN_ROW0[k]:OWN_ROW0[k] + SHARD_IN]

    def chip1(gw, gf):
        return (jnp.pad(gw[2:64], ((0, SHARD_IN - 62), (0, 0))) + jnp.pad(gf[0:8], ((62, SHARD_IN - 70), (0, 0)))
                + jnp.pad(gw[64:64 + SHARD_IN - 70], ((70, 0), (0, 0))))

    return lax.switch(q, [plain(0), chip1, plain(2), plain(3)], gwin, gfa)


def kernel(x, norm_attn_g, w_in, b_forget, w_branch_a, w_branch_b, w_out, norm_mlp_g, w_up, w_down, norm_final_g, loss_target, m_norm_attn_g, m_w_in, m_b_forget, m_w_branch_a, m_w_branch_b, m_w_out, m_norm_mlp_g, m_w_up, m_w_down, m_norm_final_g, v_norm_attn_g, v_w_in, v_b_forget, v_w_branch_a, v_w_branch_b, v_w_out, v_norm_mlp_g, v_w_up, v_w_down, v_norm_final_g):
    xi, yi, ci = lax.axis_index("x"), lax.axis_index("y"), lax.axis_index("c")
    q_me = 2 * xi + yi
    c_arr = jnp.reshape(ci, (1,)).astype(jnp.int32)
    q_arr = jnp.reshape(q_me, (1,)).astype(jnp.int32)
    x_, tgt = x[0], loss_target[0]

    names = ["w_branch_a", "w_branch_b", "w_out", "w_up", "w_down"]
    big = dict(zip(names, [w_branch_a[0], w_branch_b[0], w_out[0], w_up[0], w_down[0]]))
    ms = dict(zip(names, [m_w_branch_a[0], m_w_branch_b[0], m_w_out[0], m_w_up[0], m_w_down[0]]))
    vs = dict(zip(names, [v_w_branch_a[0], v_w_branch_b[0], v_w_out[0], v_w_up[0], v_w_down[0]]))
    grad, upd = {}, {}
    order = _Order()

    def run(fn, *args, **kw):
        return fn(order, *args, **kw)

    def own_slot(a):
        return lax.dynamic_update_slice(lax.empty((NCHIP,) + a.shape, a.dtype), a[None], (q_me, 0, 0))

    wt_own = _feature_major(w_in)
    win, fa_blk = _window_of(wt_own, q_me)
    sem_in, in_s = _allgather_start("allgather_start_in", [own_slot(win), own_slot(fa_blk)], order)
    sem_rest, rest = _allgather_start("allgather_start_rest", [own_slot(w.astype(bf16)) for w in big.values()], order)
    rope = _rope_tables(order.tok[0, 0])
    mt_own, vt_own = _feature_major(m_w_in), _feature_major(v_w_in)
    sem_f, in_s = _allgather_forward("allgather_forward_in", in_s, sem_in, order,
                                     behind=[wt_own, mt_own, vt_own, *rope])
    wins, fas = _allgather_finish("allgather_finish_in", in_s, sem_f, order)
    wt = run(_assemble_win, wins, fas)

    bpad = jnp.pad(b_forget, ((0, 0), (0, 120)))
    h1, qkvb, qkva, gates, fa = run(_norm_inproj, x_, norm_attn_g, wt, rope)
    F = run(_forget_cumsum, fa, bpad)
    oa, lsea = run(_fox_fwd, qkva, F)
    sem_f, rest = _allgather_forward("allgather_forward_rest", rest, sem_rest, order)
    ob, lseb = run(_dil_fwd, qkvb)
    was, wbs, wouts, wups, wdowns = _allgather_finish("allgather_finish_rest", rest, sem_f, order)
    wout = wouts.reshape(D, D)
    wdown = wdowns.reshape(DFF, D)
    ya, yb, mixed = run(_branch_mix, oa, ob, was, wbs, gates)
    x2, h2 = run(_outproj_norm, mixed, wout, x_, norm_mlp_g)
    u, a = run(_mlp_up, h2, wups)
    dx3, dx3b, dg3, loss_part = run(_mlp_down_loss, a, wdown, x2, norm_final_g.reshape(1, D), tgt)

    def comm(name, *parts):
        return _comm_multi(name, list(parts), order)

    def pair_adds(group, gs, ts):
        return zip(*[run(_pair_add, gs[i], ts[i], c_arr, "pair_add_" + nm) for i, nm in enumerate(group)])

    def shard_sums(group, p32s, rs):
        return [run(_shard_sum, p32s[i], rs[i], q_arr, c_arr, "shard_sum_" + nm) for i, nm in enumerate(group)]

    def adamw_group(group, fulls):
        for nm, gfull in zip(group, fulls):
            grad[nm] = gfull
            upd[nm] = run(_adamw, big[nm], gfull, ms[nm], vs[nm], "adamw_" + nm)

    grp_a, grp_b, grp_c = ["w_down", "w_up"], ["w_out", "w_branch_a", "w_branch_b"], ["w_in", "w_in_fa"]
    du = run(_mlp_down_bwd, dx3b, wdown, u)
    dwdown = run(_mm, a, dx3b, "tn", f32, 1024, D, "wgrad_down")
    dwup = run(_mm, h2, du, "tn", f32, D, 1024, "wgrad_up", stack_cols=True)
    ((sem_pa, buf_pa),) = comm("pair_start_a", _pair_start_part([dwdown.reshape(NCHIP, DFF // NCHIP, D), dwup]))
    dx2, dx2b, dg2 = run(_mlp_up_bwd, du, wups, x2, dx3, norm_mlp_g)
    ((gs, ts),) = comm("pair_wait_a", _pair_wait_part(buf_pa, sem_pa))
    p32_a, p16_a = pair_adds(grp_a, gs, ts)
    ((sem_sa, buf_sa),) = comm("shard_start_a", _shard_start_part(p16_a))
    dya, dyb, dproj = run(_gate_bwd, dx2b, wout, gates, ya, yb)
    dwout = run(_mm, mixed, dx2b, "tn", f32, D, D, "wgrad_out")
    doa, dob = run(_branch_bwd, dya, dyb, was, wbs)
    dwas, dwbs = run(_branch_wgrad, oa, ob, dya, dyb)
    ((sem_pb, buf_pb),) = comm("pair_start_b", _pair_start_part([dwout.reshape(NCHIP, D // NCHIP, D), dwas, dwbs]))
    dF, dproj = run(_fox_bwd, qkva, doa, oa, lsea, F, dproj)
    (gs, ts), (rs_a, _) = comm("pair_wait_b_shard_wait_a", _pair_wait_part(buf_pb, sem_pb),
                               _shard_wait_part(buf_sa, sem_sa, len(grp_a)))
    p32_b, p16_b = pair_adds(grp_b, gs, ts)
    fulls_a = shard_sums(grp_a, p32_a, rs_a)
    (sem_wa, fulls_a), (sem_sb, buf_sb) = comm("swap_start_a_shard_start_b", _swap_start_part(fulls_a),
                                               _shard_start_part(p16_b))
    dbf, dproj = run(_forget_bwd, dF, fa, bpad, dproj)
    dproj = run(_dil_bwd, qkvb, dob, ob, lseb, rope, dproj)
    (rs_b, _), fulls_a = comm("shard_wait_b_swap_wait_a", _shard_wait_part(buf_sb, sem_sb, len(grp_b)),
                              _swap_wait_part(fulls_a, sem_wa))
    fulls_b = shard_sums(grp_b, p32_b, rs_b)
    ((sem_wb, fulls_b),) = comm("swap_start_b", _swap_start_part(fulls_b))
    dwt = run(_mm, dproj, h1, "tn", f32, 512, D, "wgrad_in")
    dwfa = jnp.broadcast_to(dwt[F_FA:F_FA + FA_ROWS][None], (NCHIP, FA_ROWS, D))
    (sem_pc, buf_pc), fulls_b = comm("pair_start_c_swap_wait_b", _pair_start_part([dwt, dwfa], gathered=True),
                                     _swap_wait_part(fulls_b, sem_wb))
    adamw_group(grp_b, fulls_b)
    (((dwt_c, dwfa_c), (t_in, t_fa)),) = comm("pair_wait_c", _pair_wait_part(buf_pc, sem_pc, gathered=True))
    p32_in, p16_in = run(_pair_add_gathered, dwt_c, t_in, c_arr, "pair_add_w_in")
    p32_fa, p16_fa = run(_pair_add, dwfa_c, t_fa, c_arr, "pair_add_w_in_fa")
    ((sem_sc, buf_sc),) = comm("shard_start_c", _shard_start_part([p16_in, p16_fa]))
    gx, dg1 = run(_inproj_bwd, dproj, wt, x_, dx2, norm_attn_g)
    adamw_group(grp_a, fulls_a)
    small = jnp.concatenate([dg1, dg2, dg3, jnp.pad(dbf[:, 0:8], ((0, 0), (0, D - 8))),
                             jnp.pad(loss_part, ((0, 0), (0, D - 128))),
                             jnp.zeros((SMALL_ROWS - 5, D), f32)], axis=0)
    sm = lax.dynamic_update_slice(lax.empty((8, SMALL_ROWS, D), f32), small[None],
                                  (4 * xi + 2 * yi + ci, 0, 0))
    (sem_sm, buf_sm), (rs_c, _) = comm("small_start_shard_wait_c", _shard_start_part([], sm),
                                       _shard_wait_part(buf_sc, sem_sc, len(grp_c)))
    fulls_c = shard_sums(grp_c, [p32_in, p32_fa], rs_c)
    (sem_wc, fulls_c), (_, sm) = comm("swap_start_c_small_wait", _swap_start_part(fulls_c),
                                      _shard_wait_part(buf_sm, sem_sm, 0))
    gsmall = run(_small_sum, sm)
    loss = gsmall[4, 0]

    grad["norm_attn_g"], grad["norm_mlp_g"] = gsmall[0:1], gsmall[1:2]
    grad["norm_final_g"], grad["b_forget"] = gsmall[2:3], gsmall[3:4, 0:8]
    smalls = ["norm_attn_g", "norm_mlp_g", "norm_final_g", "b_forget"]
    res = run(_adamw_small, [norm_attn_g, norm_mlp_g, norm_final_g.reshape(1, D), b_forget],
              [grad[nm] for nm in smalls],
              [m_norm_attn_g, m_norm_mlp_g, m_norm_final_g.reshape(1, D), m_b_forget],
              [v_norm_attn_g, v_norm_mlp_g, v_norm_final_g.reshape(1, D), v_b_forget], "adamw_small")
    upd.update(zip(smalls, res))

    ((gwin, gfa),) = comm("swap_wait_c", _swap_wait_part(fulls_c, sem_wc))
    g_in = _own_rows(gwin, gfa, q_me)
    upd_in = run(_adamw, wt_own, g_in, mt_own, vt_own, "adamw_w_in")
    grad["w_in"] = _unfeature_major(g_in)
    upd["w_in"] = [_unfeature_major(t) for t in upd_in]

    order_out = ["norm_attn_g", "w_in", "b_forget", "w_branch_a", "w_branch_b", "w_out", "norm_mlp_g", "w_up",
                 "w_down", "norm_final_g"]
    shapes = dict(norm_attn_g=norm_attn_g.shape, w_in=w_in.shape, b_forget=b_forget.shape,
                  w_branch_a=w_branch_a.shape, w_branch_b=w_branch_b.shape, w_out=w_out.shape,
                  norm_mlp_g=norm_mlp_g.shape, w_up=w_up.shape, w_down=w_down.shape, norm_final_g=norm_final_g.shape)
    outs = [loss, gx.reshape(x.shape)]
    outs += [grad[nm].reshape(shapes[nm]) for nm in order_out]
    for k in range(3):
        outs += [upd[nm][k].reshape(shapes[nm]) for nm in order_out]
    return tuple(outs)
```

```python
import jax
import jax.numpy as jnp
from jax import lax
from jax.experimental import pallas as pl
from jax.experimental.pallas import tpu as pltpu

f32 = jnp.float32
bf16 = jnp.bfloat16

S = 2048
D = 1024
DFF = 4096
HD = 64
FOXW = 512
DILOUT = 256
DIL = (1, 4, 16)
BAND = 128
EPS = 1e-6
NEG = -1e30
ROPE_THETA = 500000.0
NCHIP = 4
TQ = 256

ADAM_LR, ADAM_B1, ADAM_B2, ADAM_EPS, ADAM_WD, ADAM_STEP = 0.001, 0.9, 0.999, 1e-08, 0.01, 10
VMEM_LIMIT = 56 * 1024 * 1024

UNIT = 64
NP = 6144
F_DIL, F_FOX, F_FA, F_G = 0, 2304, 3840, 4096
DIL_BLK, FOX_BLK = 1152, 384
WIN_UNITS, WIN_ROWS = 24, 1536
WIN_UNIT0 = (0, 23, 45, 68)
OWN_ROW0 = (0, 2, 60, 62)
SHARD_IN = 1474
FA_ROWS = 32


def _compact_to_internal():
    c2i = {}
    for p in range(2):
        for role in range(3):
            for g in range(3):
                for hh in range(2):
                    c2i[24 + 12 * role + 4 * g + 2 * p + hh] = 18 * p + 6 * role + 2 * g + hh
    for p in range(4):
        for role in range(3):
            for hh in range(2):
                c2i[8 * role + 2 * p + hh] = F_FOX // UNIT + 6 * p + 2 * role + hh
    for j in range(32):
        c2i[60 + j] = F_G // UNIT + j
    return c2i


C2I = _compact_to_internal()
OVERLAP_UNITS = (23, 45, 46, 68)


def _params(sem=None):
    return pltpu.CompilerParams(dimension_semantics=sem, vmem_limit_bytes=VMEM_LIMIT)


class _Order:
    def __init__(self):
        self.tok = None

    def mark(self, v):
        self.tok = v

    def token_for(self, args):
        return [] if self.tok is None or any(self.tok is a for a in args) else [self.tok]


def _call(order, body, args, in_specs=None, **kw):
    args = list(args)
    n_in = len(args)
    if in_specs is None:
        in_specs = [pl.BlockSpec(memory_space=pltpu.VMEM)] * n_in
    kern = body
    extra = order.token_for(args)
    if extra:
        in_specs = list(in_specs) + [pl.BlockSpec(memory_space=pl.ANY)]

        def kern(*refs):
            body(*refs[:n_in], *refs[n_in + 1:])

    out = pl.pallas_call(kern, in_specs=in_specs, **kw)(*args, *extra)
    order.mark(out[0] if isinstance(out, (tuple, list)) else out)
    return out


def _call_indexed(order, body, scalars, args, grid, in_specs, out_specs, scratch_shapes=(), **kw):
    args, in_specs = list(args), list(in_specs)
    n_front = len(scalars) + len(args)
    kern = body
    extra = order.token_for(args)
    if extra:
        in_specs.append(pl.BlockSpec(memory_space=pl.ANY))

        def kern(*refs):
            body(*refs[:n_front], *refs[n_front + 1:])

    out = pl.pallas_call(
        kern, grid_spec=pltpu.PrefetchScalarGridSpec(num_scalar_prefetch=len(scalars), grid=grid, in_specs=in_specs,
                                                     out_specs=out_specs, scratch_shapes=scratch_shapes),
        **kw)(*scalars, *args, *extra)
    order.mark(out[0] if isinstance(out, (tuple, list)) else out)
    return out


def _dot(a, b):
    return jnp.dot(a, b, preferred_element_type=f32)


def _dot_nt(a, b):
    return lax.dot_general(a, b, (((1,), (1,)), ((), ())), preferred_element_type=f32)


def _dot_tn(a, b):
    return lax.dot_general(a, b, (((0,), (0,)), ((), ())), preferred_element_type=f32)


def _split3(x):
    hi = x.astype(bf16)
    r1 = x - hi.astype(f32)
    mid = r1.astype(bf16)
    lo = (r1 - mid.astype(f32)).astype(bf16)
    return hi, mid, lo


def _rope_tables(after):
    half = 8
    inv_freq = jnp.power(jnp.float32(ROPE_THETA), -jnp.arange(half, dtype=f32) * 2.0 / 16)
    ang = (jnp.arange(S).astype(f32) + after)[:, None] * inv_freq[None, :]
    cos, sin = jnp.cos(ang), jnp.sin(ang)
    one = jnp.ones((S, HD - 16), f32)
    zero = jnp.zeros((S, HD - 16), f32)
    z8 = jnp.zeros((S, 8), f32)
    c = jnp.concatenate([cos, cos, one], axis=1)
    s1 = jnp.concatenate([-sin, z8, zero], axis=1)
    s2 = jnp.concatenate([z8, sin, zero], axis=1)
    return tuple(jnp.concatenate([t, t], axis=1) for t in (c, s1, s2))


def _mm(order, a, b, mode, out_dtype, tm, tn, name, stack_cols=False):
    if mode == "nn":
        (M, K), (_, N) = a.shape, b.shape
        a_spec = pl.BlockSpec((tm, K), lambda i, j: (i, 0))
        b_spec = pl.BlockSpec((K, tn), lambda i, j: (0, j))
        dot = _dot
    elif mode == "nt":
        (M, K), (N, _) = a.shape, b.shape
        a_spec = pl.BlockSpec((tm, K), lambda i, j: (i, 0))
        b_spec = pl.BlockSpec((tn, K), lambda i, j: (j, 0))
        dot = _dot_nt
    else:
        (K, M), (_, N) = a.shape, b.shape
        a_spec = pl.BlockSpec((K, tm), lambda i, j: (0, i))
        b_spec = pl.BlockSpec((K, tn), lambda i, j: (0, j))
        dot = _dot_tn

    def body(a_ref, b_ref, o_ref):
        o_ref[...] = dot(a_ref[...], b_ref[...]).astype(out_dtype)

    if stack_cols:
        assert tm == M
        out_spec = pl.BlockSpec((None, tm, tn), lambda i, j: (j, 0, 0))
        out_shape = jax.ShapeDtypeStruct((N // tn, M, tn), out_dtype)
    else:
        out_spec = pl.BlockSpec((tm, tn), lambda i, j: (i, j))
        out_shape = jax.ShapeDtypeStruct((M, N), out_dtype)
    return _call(
        order, body, (a, b), name=name, grid=(M // tm, N // tn), in_specs=[a_spec, b_spec],
        out_specs=out_spec, out_shape=out_shape,
        compiler_params=_params(("parallel", "parallel")),
    )


def _assemble_win(order, wins, fas):
    def body(win_ref, fa_ref, o_ref):
        q = pl.program_id(0)

        @pl.when(q == 0)
        def _():
            o_ref[...] = jnp.zeros_like(o_ref)

        for k in range(NCHIP):
            @pl.when(q == k)
            def _(k=k):
                for j in range(WIN_UNITS):
                    cu = WIN_UNIT0[k] + j
                    dst = pl.ds(C2I[cu] * UNIT, UNIT)
                    if cu in OVERLAP_UNITS:
                        o_ref[dst, :] += win_ref[j * UNIT:(j + 1) * UNIT, :]
                    else:
                        o_ref[dst, :] = win_ref[j * UNIT:(j + 1) * UNIT, :]
                if k == 1:
                    o_ref[F_FA:F_FA + FA_ROWS, :] = fa_ref[...]

    return _call(
        order, body, (wins, fas), name="assemble_w_in", grid=(NCHIP,),
        in_specs=[pl.BlockSpec((None, WIN_ROWS, D), lambda q: (q, 0, 0)),
                  pl.BlockSpec((None, FA_ROWS, D), lambda q: (1, 0, 0))],
        out_specs=pl.BlockSpec((NP, D), lambda q: (0, 0)),
        out_shape=jax.ShapeDtypeStruct((NP, D), bf16),
        compiler_params=_params(("arbitrary",)),
    )


def _norm_inproj(order, x, g1, wt, rope):
    tm = 256
    c_t, s1_t, s2_t = rope

    def body(x_ref, g_ref, w_ref, c_ref, s1_ref, s2_ref, h_ref, qkvb_ref, qkva_ref, gates_ref, fa_ref):
        xb = x_ref[...]
        r = lax.rsqrt(jnp.mean(xb * xb, axis=-1, keepdims=True) + EPS)
        h = ((xb * r) * g_ref[...]).astype(bf16)
        h_ref[...] = h
        c, s1, s2 = c_ref[...], s1_ref[...], s2_ref[...]
        for p in range(2):
            pb = _dot_nt(h, w_ref[F_DIL + p * DIL_BLK:F_DIL + (p + 1) * DIL_BLK, :])
            for ch in range(DIL_BLK // 128):
                pc = pb[:, ch * 128:(ch + 1) * 128]
                if ch < 6:
                    pc = pc * c + pltpu.roll(pc, 120, 1) * s1 + pltpu.roll(pc, 8, 1) * s2
                qkvb_ref[:, p * DIL_BLK + ch * 128:p * DIL_BLK + (ch + 1) * 128] = pc
        qkva_ref[...] = _dot_nt(h, w_ref[F_FOX:F_FA, :]).astype(bf16)
        fa_ref[...] = _dot_nt(h, w_ref[F_FA:F_FA + 128, :])
        gates_ref[...] = _dot_nt(h, w_ref[F_G:NP, :]).astype(bf16)

    row = lambda w: pl.BlockSpec((tm, w), lambda i: (i, 0))
    return _call(
        order, body, (x, g1, wt, c_t, s1_t, s2_t), name="norm_inproj", grid=(S // tm,),
        in_specs=[row(D), pl.BlockSpec((1, D), lambda i: (0, 0)), pl.BlockSpec((NP, D), lambda i: (0, 0)),
                  row(128), row(128), row(128)],
        out_specs=[row(D), row(2 * DIL_BLK), row(4 * FOX_BLK), row(2 * D), row(128)],
        out_shape=[jax.ShapeDtypeStruct((S, D), bf16), jax.ShapeDtypeStruct((S, 2 * DIL_BLK), f32),
                   jax.ShapeDtypeStruct((S, 4 * FOX_BLK), bf16), jax.ShapeDtypeStruct((S, 2 * D), bf16),
                   jax.ShapeDtypeStruct((S, 128), f32)],
        compiler_params=_params(("parallel",)),
    )


def _forget_cumsum(order, fa, bpad):
    nb = S // TQ

    def body(fa_ref, b_ref, F_ref):
        rr = lax.broadcasted_iota(jnp.int32, (TQ, TQ), 0)
        cc = lax.broadcasted_iota(jnp.int32, (TQ, TQ), 1)
        tri = (rr >= cc).astype(bf16)
        lane = lax.broadcasted_iota(jnp.int32, (1, 128), 1)
        carry = jnp.zeros((1, 128), f32)
        for b in range(nb):
            z = fa_ref[b * TQ:(b + 1) * TQ, :] + b_ref[...]
            lf = jnp.minimum(z, 0.0) - jnp.log(1.0 + jnp.exp(-jnp.abs(z)))
            lf = jnp.where(lane < 8, lf, 0.0)
            hi, mid, lo = _split3(lf)
            fb = (_dot(tri, hi) + _dot(tri, mid)) + _dot(tri, lo) + carry
            F_ref[b * TQ:(b + 1) * TQ, :] = fb
            carry = fb[TQ - 1:TQ, :]

    return _call(
        order, body, (fa, bpad), name="forget_cumsum",
        out_shape=jax.ShapeDtypeStruct((S, 128), f32),
        compiler_params=_params(),
    )


def _head_masks():
    lane = lax.broadcasted_iota(jnp.int32, (1, 128), 1)
    return lane, (lane < HD, lane >= HD)


L_FT, L_ONE = 0, 3
FOX_TQ, FOX_TK = 256, 512


def _set_lanes(x, lane, first, cols):
    for n, col in enumerate(cols):
        x = jnp.where(lane == first + n, col, x)
    return x


def _f32_parts(col):
    return [t.astype(f32) for t in _split3(col)]


def _fox_operands(qkv_ref, F_ref, lse_ref, qa, ka, p, rows):
    lane, hm = _head_masks()
    q = qkv_ref[rows, 0:128].astype(f32) * 0.125
    k = qkv_ref[rows, 128:256].astype(f32)
    Fb = F_ref[rows, :]
    for hh in (0, 1):
        free = (1 - hh) * HD
        fcol = jnp.sum(jnp.where(lane == 2 * p + hh, Fb, 0.0), axis=1, keepdims=True)
        qterm = fcol if lse_ref is None else fcol - lse_ref[rows, hh * HD:hh * HD + 1]
        qcols = _f32_parts(qterm) + [1.0] * 3
        kcols = [1.0] * 3 + [-t for t in _f32_parts(fcol)]
        qa[hh, rows, :] = _set_lanes(jnp.where(hm[hh], q, 0.0), lane, free, qcols).astype(bf16)
        ka[hh, rows, :] = _set_lanes(k, lane, free, kcols).astype(bf16)


def _fox_fwd(order, qkva, F):
    tq, tk = FOX_TQ, FOX_TK

    def body(qkv_ref, F_ref, o_ref, lse_ref, qa, ka, vt):
        p = pl.program_id(0)
        keyi = lax.broadcasted_iota(jnp.int32, (tk, 1), 0)
        qryi = lax.broadcasted_iota(jnp.int32, (1, tq), 1)
        sub = lax.broadcasted_iota(jnp.int32, (128, 1), 0)

        def prep(i, c):
            rows = pl.ds(pl.multiple_of(i * tk, tk), tk)
            _fox_operands(qkv_ref, F_ref, None, qa, ka, p, rows)
            vt[i] = qkv_ref[rows, 256:384].astype(f32).T.astype(bf16)
            return c

        lax.fori_loop(0, S // tk, prep, 0)

        def qblock(i, first_half):
            r0 = pl.multiple_of(i * tq, tq)
            qh = [qa[hh, pl.ds(r0, tq), :] for hh in (0, 1)]

            def kv(jb, carry, masked, width):
                keys = pl.ds(pl.multiple_of(jb * tk, tk), width)
                sts = [_dot_nt(ka[hh, keys, :], qh[hh]) for hh in (0, 1)]
                new = []
                for hh in (0, 1):
                    m, l, a = carry[3 * hh:3 * hh + 3]
                    st = sts[hh]
                    if masked:
                        st = jnp.where(jb * tk + keyi[0:width] <= r0 + qryi, st, NEG)
                    mn = jnp.maximum(m, jnp.max(st, axis=0, keepdims=True))
                    al = jnp.exp(m - mn)
                    pt = jnp.exp(st - mn)
                    l = al * l + jnp.sum(pt, axis=0, keepdims=True)
                    a = al * a + _dot(vt[jb, hh * HD:(hh + 1) * HD, 0:width], pt.astype(bf16))
                    new += [mn, l, a]
                return tuple(new)

            init = (jnp.full((1, tq), NEG, f32), jnp.zeros((1, tq), f32), jnp.zeros((HD, tq), f32)) * 2
            last = (r0 + tq - 1) // tk
            carry = lax.fori_loop(0, last, lambda j, cr: kv(j, cr, False, tk), init)
            m0, l0, a0, m1, l1, a1 = kv(last, carry, True, tk // 2 if first_half else tk)
            ot = jnp.concatenate([a0 / l0, a1 / l1], axis=0)
            lt = jnp.where(sub < HD, m0 + jnp.log(l0), m1 + jnp.log(l1))
            o_ref[pl.ds(r0, tq), :] = ot.T.astype(bf16)
            lse_ref[pl.ds(r0, tq), :] = lt.T

        def qpair(t, c):
            qblock(2 * t, True)
            qblock(2 * t + 1, False)
            return c

        assert tk == 2 * tq
        lax.fori_loop(0, S // tk, qpair, 0)

    pair = pl.BlockSpec((S, 128), lambda p: (0, p))
    return _call(
        order, body, (qkva, F), name="fox_fwd", grid=(4,),
        in_specs=[pl.BlockSpec((S, FOX_BLK), lambda p: (0, p)), pl.BlockSpec((S, 128), lambda p: (0, 0))],
        out_specs=[pair, pair],
        out_shape=[jax.ShapeDtypeStruct((S, FOXW), bf16), jax.ShapeDtypeStruct((S, FOXW), f32)],
        scratch_shapes=[pltpu.VMEM((2, S, 128), bf16)] * 2 + [pltpu.VMEM((S // tk, 128, tk), bf16)],
        compiler_params=_params(("parallel",)),
    )


def _permute_in(dst, src, r):
    L = S // r
    for rho in range(r):
        dst[rho * L:(rho + 1) * L, :] = src[pl.ds(rho, L, stride=r), :]


def _permute_out(dst, src, r):
    L = S // r
    for rho in range(r):
        dst[pl.ds(rho, L, stride=r), :] = src[rho * L:(rho + 1) * L, :]


def _band_width(nbl):
    return BAND if nbl == 1 else 2 * BAND


def _band_geometry(bb, nbl):
    r0 = pl.multiple_of(bb * BAND, BAND)
    if nbl == 1:
        k0 = r0
    else:
        k0 = pl.multiple_of(jnp.maximum(bb - 1, 0) * BAND, BAND)
    sub0 = (bb - lax.rem(bb, nbl)) * BAND
    qi = r0 + lax.broadcasted_iota(jnp.int32, (BAND, 1), 0)
    ki = k0 + lax.broadcasted_iota(jnp.int32, (1, _band_width(nbl)), 1)
    diff = qi - ki
    valid = (diff >= 0) & (diff <= BAND) & (ki >= sub0)
    return r0, k0, valid


def _dil_views(ref):
    return [[ref.at[:, pl.ds((3 * role + g) * 128, 128)] for g in range(3)] for role in range(3)]


DIL_UNROLL = 4


def _dil_in_specs():
    return [pl.BlockSpec((S, 128), lambda p, k=k: (0, 9 * p + k)) for k in range(9)]


def _dil_fwd(order, qkvb):
    def body(*refs):
        q_refs, k_refs, v_refs = refs[0:3], refs[3:6], refs[6:9]
        ob_ref, lse_ref, qp, kp, vp, op, lp = refs[9:16]
        on, ln = refs[16:19], refs[19:22]
        _, hm = _head_masks()
        for g, r in enumerate(DIL):
            nbl = S // r // BAND
            if r == 1:
                qs_, ks_, vs_, od, ld = q_refs[g], k_refs[g], v_refs[g], on[g], ln[g]
            else:
                _permute_in(qp, q_refs[g], r)
                _permute_in(kp, k_refs[g], r)
                _permute_in(vp, v_refs[g], r)
                qs_, ks_, vs_, od, ld = qp, kp, vp, op, lp

            def blk(t, c, qs_=qs_, ks_=ks_, vs_=vs_, od=od, ld=ld, nbl=nbl):
                work = []
                for u in range(DIL_UNROLL):
                    r0, k0, valid = _band_geometry(DIL_UNROLL * t + u, nbl)
                    q = qs_[pl.ds(r0, BAND), :] * 0.125
                    kw = ks_[pl.ds(k0, _band_width(nbl)), :].astype(bf16)
                    vw = vs_[pl.ds(k0, _band_width(nbl)), :]
                    for hh in (0, 1):
                        qh = jnp.where(hm[hh], q, 0.0).astype(bf16)
                        work.append((u, hh, r0, valid, vw, _dot_nt(qh, kw)))
                o = [jnp.zeros((BAND, 128), f32)] * DIL_UNROLL
                lse = [jnp.zeros((BAND, 128), f32)] * DIL_UNROLL
                for u, hh, r0, valid, vw, s in work:
                    s = jnp.where(valid, s, NEG)
                    m = jnp.max(s, axis=1, keepdims=True)
                    pr = jnp.exp(s - m)
                    l = jnp.sum(pr, axis=1, keepdims=True)
                    vm = jnp.where(hm[hh], vw, 0.0).astype(bf16)
                    o[u] = o[u] + _dot((pr / l).astype(bf16), vm)
                    lse[u] = jnp.where(hm[hh], m + jnp.log(l), lse[u])
                    if hh == 1:
                        od[pl.ds(r0, BAND), :] = o[u]
                        ld[pl.ds(r0, BAND), :] = lse[u]
                return c

            lax.fori_loop(0, S // BAND // DIL_UNROLL, blk, 0)
            if r != 1:
                _permute_out(on[g], op, r)
                _permute_out(ln[g], lp, r)

        def combine(i, c):
            r0 = pl.multiple_of(i * TQ, TQ)
            ls = [ln[g][pl.ds(r0, TQ), :] for g in range(3)]
            mx = jnp.maximum(jnp.maximum(ls[0], ls[1]), ls[2])
            es = [jnp.exp(l - mx) for l in ls]
            tot = (es[0] + es[1]) + es[2]
            acc = (es[0] / tot) * on[0][pl.ds(r0, TQ), :]
            acc = acc + (es[1] / tot) * on[1][pl.ds(r0, TQ), :]
            acc = acc + (es[2] / tot) * on[2][pl.ds(r0, TQ), :]
            ob_ref[pl.ds(r0, TQ), :] = acc.astype(bf16)
            lse_ref[pl.ds(r0, TQ), :] = mx + jnp.log(tot)
            return c

        lax.fori_loop(0, S // TQ, combine, 0)

    out_blk = pl.BlockSpec((S, 128), lambda p: (0, p))
    return _call(
        order, body, [qkvb] * 9, name="dil_fwd", grid=(2,),
        in_specs=_dil_in_specs(), out_specs=[out_blk, out_blk],
        out_shape=[jax.ShapeDtypeStruct((S, DILOUT), bf16), jax.ShapeDtypeStruct((S, DILOUT), f32)],
        scratch_shapes=[pltpu.VMEM((S, 128), f32)] * 11,
        compiler_params=_params(("parallel",)),
    )


def _branch_mix(order, oa, ob, was, wbs, gates):
    tm = 512

    def body(oa_ref, ob_ref, wa_ref, wb_ref, g_ref, ya_ref, yb_ref, mix_ref):
        oa_b, ob_b = oa_ref[...], ob_ref[...]
        for q in range(NCHIP):
            cols = slice(q * 256, (q + 1) * 256)
            ya = _dot(oa_b, wa_ref[q])
            yb = _dot(ob_b, wb_ref[q])
            ya_ref[:, cols] = ya.astype(bf16)
            yb_ref[:, cols] = yb.astype(bf16)
            ga = g_ref[:, q * 256:(q + 1) * 256].astype(f32)
            gb = g_ref[:, D + q * 256:D + (q + 1) * 256].astype(f32)
            mix_ref[:, cols] = (jax.nn.sigmoid(ga) * ya + jax.nn.sigmoid(gb) * yb).astype(bf16)

    row = lambda w: pl.BlockSpec((tm, w), lambda i: (i, 0))
    full3 = lambda a: pl.BlockSpec(a.shape, lambda i: (0, 0, 0))
    return _call(
        order, body, (oa, ob, was, wbs, gates), name="branch_mix", grid=(S // tm,),
        in_specs=[row(FOXW), row(DILOUT), full3(was), full3(wbs), row(2 * D)],
        out_specs=[row(D), row(D), row(D)],
        out_shape=[jax.ShapeDtypeStruct((S, D), bf16), jax.ShapeDtypeStruct((S, D), bf16),
                   jax.ShapeDtypeStruct((S, D), bf16)],
        compiler_params=_params(("parallel",)),
    )


def _outproj_norm(order, mixed, wout, x, g2):
    tm = 512

    def body(m_ref, w_ref, x_ref, g_ref, x2_ref, h2_ref):
        x2 = x_ref[...] + _dot(m_ref[...], w_ref[...])
        x2_ref[...] = x2
        r = lax.rsqrt(jnp.mean(x2 * x2, axis=-1, keepdims=True) + EPS)
        h2_ref[...] = ((x2 * r) * g_ref[...]).astype(bf16)

    row = pl.BlockSpec((tm, D), lambda i: (i, 0))
    return _call(
        order, body, (mixed, wout, x, g2), name="outproj_norm", grid=(S // tm,),
        in_specs=[row, pl.BlockSpec((D, D), lambda i: (0, 0)), row, pl.BlockSpec((1, D), lambda i: (0, 0))],
        out_specs=[row, row],
        out_shape=[jax.ShapeDtypeStruct((S, D), f32), jax.ShapeDtypeStruct((S, D), bf16)],
        compiler_params=_params(("parallel",)),
    )


def _mlp_up(order, h2, wups):
    tm = 1024

    def body(h_ref, w_ref, ru_ref, a_ref):
        ru = jnp.maximum(_dot(h_ref[...], w_ref[...]), 0.0)
        ru_ref[...] = ru.astype(bf16)
        a_ref[...] = (ru * ru).astype(bf16)

    out = pl.BlockSpec((tm, D), lambda q, i: (i, q))
    return _call(
        order, body, (h2, wups), name="mlp_up", grid=(NCHIP, S // tm),
        in_specs=[pl.BlockSpec((tm, D), lambda q, i: (i, 0)), pl.BlockSpec((None, D, D), lambda q, i: (q, 0, 0))],
        out_specs=[out, out],
        out_shape=[jax.ShapeDtypeStruct((S, DFF), bf16), jax.ShapeDtypeStruct((S, DFF), bf16)],
        compiler_params=_params(("parallel", "parallel")),
    )


def _mlp_down_loss(order, a, wdown, x2, g3, tgt):
    tm = 512

    def body(a_ref, w_ref, x2_ref, g_ref, t_ref, dx_ref, dxb_ref, dg_ref, loss_ref):
        i = pl.program_id(0)
        x3 = x2_ref[...] + _dot(a_ref[...], w_ref[...])
        r = lax.rsqrt(jnp.mean(x3 * x3, axis=-1, keepdims=True) + EPS)
        xh = x3 * r
        g = g_ref[...]
        e = xh * g - t_ref[...]
        part = 0.5 * jnp.sum(jnp.mean(e * e, axis=-1, keepdims=True), axis=0, keepdims=True)
        dy = e * (1.0 / D)
        gdy = dy * g
        dx = r * (gdy - xh * jnp.mean(gdy * xh, axis=-1, keepdims=True))
        dx_ref[...] = dx
        dxb_ref[...] = dx.astype(bf16)

        @pl.when(i == 0)
        def _():
            dg_ref[...] = jnp.zeros_like(dg_ref)
            loss_ref[...] = jnp.zeros_like(loss_ref)

        dg_ref[...] += jnp.sum(dy * xh, axis=0, keepdims=True)
        loss_ref[...] += jnp.broadcast_to(part, (1, 128))

    row = pl.BlockSpec((tm, D), lambda i: (i, 0))
    vec = pl.BlockSpec((1, D), lambda i: (0, 0))
    return _call(
        order, body, (a, wdown, x2, g3, tgt), name="mlp_down_loss", grid=(S // tm,),
        in_specs=[pl.BlockSpec((tm, DFF), lambda i: (i, 0)), pl.BlockSpec((DFF, D), lambda i: (0, 0)), row, vec, row],
        out_specs=[row, row, vec, pl.BlockSpec((1, 128), lambda i: (0, 0))],
        out_shape=[jax.ShapeDtypeStruct((S, D), f32), jax.ShapeDtypeStruct((S, D), bf16),
                   jax.ShapeDtypeStruct((1, D), f32), jax.ShapeDtypeStruct((1, 128), f32)],
        compiler_params=_params(("arbitrary",)),
    )


def _mlp_down_bwd(order, dx3b, wdown, u):
    tm = 512

    def body(d_ref, w_ref, u_ref, du_ref):
        d = d_ref[...]
        for q in range(NCHIP):
            cols = slice(q * D, (q + 1) * D)
            da = _dot_nt(d, w_ref[cols, :])
            du_ref[:, cols] = (da * (2.0 * u_ref[:, cols].astype(f32))).astype(bf16)

    return _call(
        order, body, (dx3b, wdown, u), name="mlp_down_bwd", grid=(S // tm,),
        in_specs=[pl.BlockSpec((tm, D), lambda i: (i, 0)), pl.BlockSpec((DFF, D), lambda i: (0, 0)),
                  pl.BlockSpec((tm, DFF), lambda i: (i, 0))],
        out_specs=pl.BlockSpec((tm, DFF), lambda i: (i, 0)),
        out_shape=jax.ShapeDtypeStruct((S, DFF), bf16),
        compiler_params=_params(("parallel",)),
    )


def _mlp_up_bwd(order, du, wups, x2, dx3, g2):
    tm = 512

    def body(du_ref, w_ref, x2_ref, dx3_ref, g_ref, dx2_ref, dx2b_ref, dg_ref):
        i = pl.program_id(0)
        dh = jnp.zeros((tm, D), f32)
        for q in range(NCHIP):
            dh = dh + _dot_nt(du_ref[:, q * D:(q + 1) * D], w_ref[q])
        x2 = x2_ref[...]
        r = lax.rsqrt(jnp.mean(x2 * x2, axis=-1, keepdims=True) + EPS)
        xh = x2 * r
        gdh = dh * g_ref[...]
        dx2 = dx3_ref[...] + r * (gdh - xh * jnp.mean(gdh * xh, axis=-1, keepdims=True))
        dx2_ref[...] = dx2
        dx2b_ref[...] = dx2.astype(bf16)

        @pl.when(i == 0)
        def _():
            dg_ref[...] = jnp.zeros_like(dg_ref)

        dg_ref[...] += jnp.sum(dh * xh, axis=0, keepdims=True)

    row = pl.BlockSpec((tm, D), lambda i: (i, 0))
    vec = pl.BlockSpec((1, D), lambda i: (0, 0))
    return _call(
        order, body, (du, wups, x2, dx3, g2), name="mlp_up_bwd", grid=(S // tm,),
        in_specs=[pl.BlockSpec((tm, DFF), lambda i: (i, 0)), pl.BlockSpec((NCHIP, D, D), lambda i: (0, 0, 0)),
                  row, row, vec],
        out_specs=[row, row, vec],
        out_shape=[jax.ShapeDtypeStruct((S, D), f32), jax.ShapeDtypeStruct((S, D), bf16),
                   jax.ShapeDtypeStruct((1, D), f32)],
        compiler_params=_params(("arbitrary",)),
    )


def _gate_bwd(order, dx2b, wout, gates, ya, yb):
    tm = 512

    def body(d_ref, w_ref, g_ref, ya_ref, yb_ref, dya_ref, dyb_ref, dproj_ref):
        dm = _dot_nt(d_ref[...], w_ref[...])
        sa = jax.nn.sigmoid(g_ref[:, 0:D].astype(f32))
        sb = jax.nn.sigmoid(g_ref[:, D:2 * D].astype(f32))
        dya_ref[...] = (dm * sa).astype(bf16)
        dyb_ref[...] = (dm * sb).astype(bf16)
        dproj_ref[:, 0:D] = (dm * ya_ref[...].astype(f32) * (sa * (1.0 - sa))).astype(bf16)
        dproj_ref[:, D:2 * D] = (dm * yb_ref[...].astype(f32) * (sb * (1.0 - sb))).astype(bf16)

    row = lambda w: pl.BlockSpec((tm, w), lambda i: (i, 0))
    return _call(
        order, body, (dx2b, wout, gates, ya, yb), name="gate_bwd", grid=(S // tm,),
        in_specs=[row(D), pl.BlockSpec((D, D), lambda i: (0, 0)), row(2 * D), row(D), row(D)],
        out_specs=[row(D), row(D), pl.BlockSpec((tm, 2 * D), lambda i: (i, F_G // (2 * D)))],
        out_shape=[jax.ShapeDtypeStruct((S, D), bf16), jax.ShapeDtypeStruct((S, D), bf16),
                   jax.ShapeDtypeStruct((S, NP), bf16)],
        compiler_params=_params(("parallel",)),
    )


def _branch_bwd(order, dya, dyb, was, wbs):
    tm = 512

    def body(dya_ref, dyb_ref, wa_ref, wb_ref, doa_ref, dob_ref):
        doa = jnp.zeros((tm, FOXW), f32)
        dob = jnp.zeros((tm, DILOUT), f32)
        for q in range(NCHIP):
            cols = slice(q * 256, (q + 1) * 256)
            doa = doa + _dot_nt(dya_ref[:, cols], wa_ref[q])
            dob = dob + _dot_nt(dyb_ref[:, cols], wb_ref[q])
        doa_ref[...] = doa.astype(bf16)
        dob_ref[...] = dob

    row = lambda w: pl.BlockSpec((tm, w), lambda i: (i, 0))
    full3 = lambda a: pl.BlockSpec(a.shape, lambda i: (0, 0, 0))
    return _call(
        order, body, (dya, dyb, was, wbs), name="branch_bwd", grid=(S // tm,),
        in_specs=[row(D), row(D), full3(was), full3(wbs)],
        out_specs=[row(FOXW), row(DILOUT)],
        out_shape=[jax.ShapeDtypeStruct((S, FOXW), bf16), jax.ShapeDtypeStruct((S, DILOUT), f32)],
        compiler_params=_params(("parallel",)),
    )


def _branch_wgrad(order, oa, ob, dya, dyb):
    def body(oa_ref, ob_ref, dya_ref, dyb_ref, dwa_ref, dwb_ref):
        dwa_ref[...] = _dot_tn(oa_ref[...], dya_ref[...])
        dwb_ref[...] = _dot_tn(ob_ref[...], dyb_ref[...])

    full = lambda w: pl.BlockSpec((S, w), lambda q: (0, 0))
    colq = pl.BlockSpec((S, 256), lambda q: (0, q))
    return _call(
        order, body, (oa, ob, dya, dyb), name="branch_wgrad", grid=(NCHIP,),
        in_specs=[full(FOXW), full(DILOUT), colq, colq],
        out_specs=[pl.BlockSpec((None, FOXW, 256), lambda q: (q, 0, 0)),
                   pl.BlockSpec((None, DILOUT, 256), lambda q: (q, 0, 0))],
        out_shape=[jax.ShapeDtypeStruct((NCHIP, FOXW, 256), f32), jax.ShapeDtypeStruct((NCHIP, DILOUT, 256), f32)],
        compiler_params=_params(("parallel",)),
    )


def _fox_bwd(order, qkva, doa, oa, lse, F, dproj):
    tq, tk = FOX_TQ, FOX_TK

    def body(qkv_ref, do_ref, o_ref, lse_ref, F_ref, _dproj_in, dF_ref, dqkv_ref, qa, ka, da, va, kat,
             dk_scr, dv_scr, dqt_scr):
        p = pl.program_id(0)
        lane, hm = _head_masks()
        keyi = lax.broadcasted_iota(jnp.int32, (tk, 1), 0)
        qryi = lax.broadcasted_iota(jnp.int32, (1, tq), 1)

        def prep(i, c):
            rows = pl.ds(pl.multiple_of(i * tk, tk), tk)
            _fox_operands(qkv_ref, F_ref, lse_ref, qa, ka, p, rows)
            do = do_ref[rows, :].astype(f32)
            prod = do * o_ref[rows, :].astype(f32)
            v = qkv_ref[rows, 256:384].astype(f32)
            for hh in (0, 1):
                free = (1 - hh) * HD
                delta = jnp.sum(jnp.where(hm[hh], prod, 0.0), axis=1, keepdims=True)
                da[hh, rows, :] = _set_lanes(jnp.where(hm[hh], do, 0.0), lane, free,
                                             [-t for t in _f32_parts(delta)]).astype(bf16)
                va[hh, rows, :] = _set_lanes(v, lane, free, [1.0] * 3).astype(bf16)
                kat[hh, i] = ka[hh, rows, :].astype(f32).T.astype(bf16)
                dk_scr[hh, rows, :] = jnp.zeros((tk, 128), f32)
                dv_scr[hh, rows, :] = jnp.zeros((tk, 128), f32)
            return c

        lax.fori_loop(0, S // tk, prep, 0)

        def qblock(i, first_half):
            r0 = pl.multiple_of(i * tq, tq)
            qrows = pl.ds(r0, tq)
            qh = [qa[hh, qrows, :] for hh in (0, 1)]
            dh = [da[hh, qrows, :] for hh in (0, 1)]
            dqt_scr[...] = jnp.zeros_like(dqt_scr)

            def kv(jb, c2, masked, width):
                keys = pl.ds(pl.multiple_of(jb * tk, tk), width)
                sts = [_dot_nt(ka[hh, keys, :], qh[hh]) for hh in (0, 1)]
                dps = [_dot_nt(va[hh, keys, :], dh[hh]) for hh in (0, 1)]
                for hh in (0, 1):
                    pt = jnp.exp(sts[hh])
                    if masked:
                        pt = jnp.where(jb * tk + keyi[0:width] <= r0 + qryi, pt, 0.0)
                    dsb = (pt * dps[hh]).astype(bf16)
                    dv_scr[hh, keys, :] += _dot(pt.astype(bf16), dh[hh])
                    dk_scr[hh, keys, :] += _dot(dsb, qh[hh])
                    dqt_scr[hh] += _dot(kat[hh, jb, :, 0:width], dsb)
                return c2

            last = (r0 + tq - 1) // tk
            lax.fori_loop(0, last, lambda j, c2: kv(j, c2, False, tk), 0)
            kv(last, 0, True, tk // 2 if first_half else tk)
            dq0, dq1 = dqt_scr[0].T, dqt_scr[1].T
            dqkv_ref[qrows, 0:128] = (jnp.where(hm[0], dq0, dq1) * 0.125).astype(bf16)
            dF_ref[qrows, :] = jnp.where(lane == 0, dq0[:, HD:HD + 1], jnp.where(lane == 1, dq1[:, 0:1], 0.0))

        def qpair(t, c):
            qblock(2 * t, True)
            qblock(2 * t + 1, False)
            return c

        assert tk == 2 * tq
        lax.fori_loop(0, S // tk, qpair, 0)

        def finish(i, c):
            rows = pl.ds(pl.multiple_of(i * tq, tq), tq)
            dk0, dk1 = dk_scr[0, rows, :], dk_scr[1, rows, :]
            dqkv_ref[rows, 128:256] = jnp.where(hm[0], dk0, dk1).astype(bf16)
            dqkv_ref[rows, 256:384] = jnp.where(hm[0], dv_scr[0, rows, :], dv_scr[1, rows, :]).astype(bf16)
            cs = jnp.where(lane == 0, dk0[:, HD + L_ONE:HD + L_ONE + 1],
                           jnp.where(lane == 1, dk1[:, L_ONE:L_ONE + 1], 0.0))
            dF_ref[rows, :] = dF_ref[rows, :] - cs
            return c

        lax.fori_loop(0, S // tq, finish, 0)

    pair = pl.BlockSpec((S, 128), lambda p: (0, p))
    return _call(
        order, body, (qkva, doa, oa, lse, F, dproj), name="fox_bwd", grid=(4,),
        in_specs=[pl.BlockSpec((S, FOX_BLK), lambda p: (0, p)), pair, pair, pair,
                  pl.BlockSpec((S, 128), lambda p: (0, 0)), pl.BlockSpec(memory_space=pl.ANY)],
        out_specs=[pair, pl.BlockSpec((S, FOX_BLK), lambda p: (0, F_FOX // FOX_BLK + p))],
        out_shape=[jax.ShapeDtypeStruct((S, FOXW), f32), jax.ShapeDtypeStruct((S, NP), bf16)],
        input_output_aliases={5: 1},
        scratch_shapes=[pltpu.VMEM((2, S, 128), bf16)] * 4 + [pltpu.VMEM((2, S // tk, 128, tk), bf16)]
        + [pltpu.VMEM((2, S, 128), f32)] * 2 + [pltpu.VMEM((2, 128, tq), f32)],
        compiler_params=_params(("parallel",)),
    )


def _forget_bwd(order, dF, fa, bpad, dproj):
    nb = S // TQ

    def body(dF_ref, fa_ref, b_ref, _dproj_in, db_ref, dfa_ref):
        rr = lax.broadcasted_iota(jnp.int32, (TQ, TQ), 0)
        cc = lax.broadcasted_iota(jnp.int32, (TQ, TQ), 1)
        upper = (cc >= rr).astype(bf16)
        lane = lax.broadcasted_iota(jnp.int32, (1, 128), 1)
        carry = jnp.zeros((1, 128), f32)
        db = jnp.zeros((1, 128), f32)
        for b in reversed(range(nb)):
            cols = jnp.zeros((TQ, 128), f32)
            for h in range(8):
                c0 = (h // 2) * 128 + h % 2
                cols = jnp.where(lane == h, dF_ref[b * TQ:(b + 1) * TQ, c0:c0 + 1], cols)
            dlf = carry
            for part in _split3(cols):
                dlf = dlf + _dot(upper, part)
            carry = carry + jnp.sum(cols, axis=0, keepdims=True)
            z = fa_ref[b * TQ:(b + 1) * TQ, :] + b_ref[...]
            dz = jnp.where(lane < 8, dlf * jax.nn.sigmoid(-z), 0.0)
            dfa_ref[b * TQ:(b + 1) * TQ, 0:128] = dz.astype(bf16)
            dfa_ref[b * TQ:(b + 1) * TQ, 128:256] = jnp.zeros((TQ, 128), bf16)
            db = db + jnp.sum(dz, axis=0, keepdims=True)
        db_ref[...] = db

    whole = lambda a: pl.BlockSpec(a.shape, lambda i: (0,) * a.ndim)
    return _call(
        order, body, (dF, fa, bpad, dproj), name="forget_bwd", grid=(1,),
        in_specs=[whole(dF), whole(fa), whole(bpad), pl.BlockSpec(memory_space=pl.ANY)],
        out_specs=[pl.BlockSpec((1, 128), lambda i: (0, 0)), pl.BlockSpec((S, 256), lambda i: (0, F_FA // 256))],
        out_shape=[jax.ShapeDtypeStruct((1, 128), f32), jax.ShapeDtypeStruct((S, NP), bf16)],
        input_output_aliases={3: 1},
        compiler_params=_params(("arbitrary",)),
    )


def _dil_bwd(order, qkvb, dob, ob, lseb, rope, dproj):
    c_t, s1_t, s2_t = rope

    def body(*refs):
        q_refs, k_refs, v_refs = refs[0:3], refs[3:6], refs[6:9]
        dob_ref, ob_ref, lse_ref, c_ref, s1_ref, s2_ref, _dproj_in, dqkv_ref = refs[9:17]
        qp, kp, vp, dop, lp, dlp, dln, dqp, dkp, dvp, nat = refs[17:28]
        dq_out, dk_out, dv_out = _dil_views(dqkv_ref)
        _, hm = _head_masks()

        def delta_rows(i, c):
            r0 = pl.multiple_of(i * TQ, TQ)
            prod = dob_ref[pl.ds(r0, TQ), :] * ob_ref[pl.ds(r0, TQ), :].astype(f32)
            d0 = jnp.sum(jnp.where(hm[0], prod, 0.0), axis=1, keepdims=True)
            d1 = jnp.sum(jnp.where(hm[1], prod, 0.0), axis=1, keepdims=True)
            dln[pl.ds(r0, TQ), :] = jnp.where(hm[0], d0, d1)
            return c

        lax.fori_loop(0, S // TQ, delta_rows, 0)

        for g, r in enumerate(DIL):
            nbl = S // r // BAND
            if r == 1:
                srcs = (q_refs[g], k_refs[g], v_refs[g], dob_ref, lse_ref, dln)
            else:
                for dst, src in ((qp, q_refs[g]), (kp, k_refs[g]), (vp, v_refs[g]), (dop, dob_ref),
                                 (lp, lse_ref), (dlp, dln)):
                    _permute_in(dst, src, r)
                srcs = (qp, kp, vp, dop, lp, dlp)
            dkp[...] = jnp.zeros_like(dkp)
            dvp[...] = jnp.zeros_like(dvp)

            def blk(t, c, srcs=srcs, nbl=nbl):
                qs_, ks_, vs_, dos_, ls_, dls_ = srcs
                work = []
                for u in range(DIL_UNROLL):
                    r0, k0, valid = _band_geometry(DIL_UNROLL * t + u, nbl)
                    q = qs_[pl.ds(r0, BAND), :] * 0.125
                    kwf = ks_[pl.ds(k0, _band_width(nbl)), :]
                    kw = kwf.astype(bf16)
                    vw = vs_[pl.ds(k0, _band_width(nbl)), :].astype(bf16)
                    do = dos_[pl.ds(r0, BAND), :]
                    lse = ls_[pl.ds(r0, BAND), :]
                    dlt = dls_[pl.ds(r0, BAND), :]
                    for hh in (0, 1):
                        qh = jnp.where(hm[hh], q, 0.0).astype(bf16)
                        doh = jnp.where(hm[hh], do, 0.0).astype(bf16)
                        kh = jnp.where(hm[hh], kwf, 0.0).astype(bf16)
                        work.append((u, hh, r0, k0, valid, qh, doh, kh, lse[:, hh * HD:hh * HD + 1],
                                     dlt[:, hh * HD:hh * HD + 1], _dot_nt(qh, kw), _dot_nt(doh, vw)))
                for u, hh, r0, k0, valid, qh, doh, kh, lse_h, dlt_h, s, dp in work:
                    if hh == 0:
                        dq = jnp.zeros((BAND, 128), f32)
                        dk = jnp.zeros((_band_width(nbl), 128), f32)
                        dv = jnp.zeros((_band_width(nbl), 128), f32)
                    pr = jnp.where(valid, jnp.exp(s - lse_h), 0.0)
                    dsb = (pr * (dp - dlt_h)).astype(bf16)
                    dv = dv + _dot_tn(pr.astype(bf16), doh)
                    dk = dk + _dot_tn(dsb, qh)
                    dq = dq + _dot(dsb, kh)
                    if hh == 1:
                        dqp[pl.ds(r0, BAND), :] = dq * 0.125
                        dkp[pl.ds(k0, _band_width(nbl)), :] += dk
                        dvp[pl.ds(k0, _band_width(nbl)), :] += dv
                return c

            lax.fori_loop(0, S // BAND // DIL_UNROLL, blk, 0)

            for acc, out, roped in ((dqp, dq_out[g], True), (dkp, dk_out[g], True), (dvp, dv_out[g], False)):
                if r == 1:
                    src = acc
                else:
                    _permute_out(nat, acc, r)
                    src = nat

                def emit(i, c, src=src, out=out, roped=roped):
                    r0 = pl.multiple_of(i * TQ, TQ)
                    d = src[pl.ds(r0, TQ), :]
                    if roped:
                        d = (d * c_ref[pl.ds(r0, TQ), :] + pltpu.roll(d * s1_ref[pl.ds(r0, TQ), :], 8, 1)
                             + pltpu.roll(d * s2_ref[pl.ds(r0, TQ), :], 120, 1))
                    out[pl.ds(r0, TQ), :] = d.astype(bf16)
                    return c

                lax.fori_loop(0, S // TQ, emit, 0)

    pair = pl.BlockSpec((S, 128), lambda p: (0, p))
    tab = pl.BlockSpec((S, 128), lambda p: (0, 0))
    blk_spec = pl.BlockSpec((S, DIL_BLK), lambda p: (0, p))
    return _call(
        order, body, [qkvb] * 9 + [dob, ob, lseb, c_t, s1_t, s2_t, dproj], name="dil_bwd", grid=(2,),
        in_specs=_dil_in_specs() + [pair, pair, pair, tab, tab, tab, pl.BlockSpec(memory_space=pl.ANY)],
        out_specs=blk_spec,
        out_shape=jax.ShapeDtypeStruct((S, NP), bf16),
        input_output_aliases={15: 0},
        scratch_shapes=[pltpu.VMEM((S, 128), f32)] * 11,
        compiler_params=_params(("parallel",)),
    )


def _inproj_bwd(order, dproj, wt, x, dx2, g1):
    tm = 256

    def body(d_ref, w_ref, x_ref, dx2_ref, g_ref, dx_ref, dg_ref):
        i = pl.program_id(0)
        dh = _dot(d_ref[...], w_ref[...])
        xb = x_ref[...]
        r = lax.rsqrt(jnp.mean(xb * xb, axis=-1, keepdims=True) + EPS)
        xh = xb * r
        gdh = dh * g_ref[...]
        dx_ref[...] = dx2_ref[...] + r * (gdh - xh * jnp.mean(gdh * xh, axis=-1, keepdims=True))

        @pl.when(i == 0)
        def _():
            dg_ref[...] = jnp.zeros_like(dg_ref)

        dg_ref[...] += jnp.sum(dh * xh, axis=0, keepdims=True)

    row = pl.BlockSpec((tm, D), lambda i: (i, 0))
    vec = pl.BlockSpec((1, D), lambda i: (0, 0))
    return _call(
        order, body, (dproj, wt, x, dx2, g1), name="inproj_bwd", grid=(S // tm,),
        in_specs=[pl.BlockSpec((tm, NP), lambda i: (i, 0)), pl.BlockSpec((NP, D), lambda i: (0, 0)), row, row, vec],
        out_specs=[row, vec],
        out_shape=[jax.ShapeDtypeStruct((S, D), f32), jax.ShapeDtypeStruct((1, D), f32)],
        compiler_params=_params(("arbitrary",)),
    )


HBM = pl.BlockSpec(memory_space=pltpu.HBM)
SEM = pl.BlockSpec(memory_space=pltpu.SEMAPHORE)
SMALL_ROWS = 8


def _comm_call(name, body, bufs, order, sems_in=(), new_sems=(), behind=()):
    nb, ns, nn = len(bufs), len(sems_in), len(new_sems)
    extra = order.token_for(bufs) + list(behind)

    def kern(*refs):
        off = nb + ns + len(extra)
        body(refs[:nb], refs[nb:nb + ns], refs[off:off + nn])
        refs[-1][...] = jnp.zeros((8, 128), f32)

    res = pl.pallas_call(
        kern, name=name,
        in_specs=[HBM] * nb + [SEM] * ns + [pl.BlockSpec(memory_space=pl.ANY)] * len(extra),
        out_specs=[SEM] * nn + [HBM] * nb + [pl.BlockSpec(memory_space=pltpu.VMEM)],
        out_shape=[pltpu.SemaphoreType.DMA((k,)) for k in new_sems] + [pltpu.HBM(b.shape, b.dtype) for b in bufs]
        + [jax.ShapeDtypeStruct((8, 128), f32)],
        input_output_aliases={i: nn + i for i in range(nb)},
        compiler_params=pltpu.CompilerParams(has_side_effects=pltpu.SideEffectType.DATAFLOW_SIDE_EFFECTING),
    )(*[pltpu.with_memory_space_constraint(b, pltpu.HBM) for b in bufs], *sems_in, *extra)
    order.mark(res[-1])
    return list(res[:nn]), list(res[nn:nn + nb])


def _place():
    x, y, c = lax.axis_index("x"), lax.axis_index("y"), lax.axis_index("c")
    chips = [(1 - x, y), (x, 1 - y), (1 - x, 1 - y)]
    return x, y, c, chips


def _rcopy(src, dst, ssem, rsem, dev):
    return pltpu.make_async_remote_copy(src_ref=src, dst_ref=dst, send_sem=ssem, recv_sem=rsem,
                                        device_id=dev, device_id_type=pl.DeviceIdType.MESH)


def _half(nrows, which):
    return pl.ds(which * (nrows // 2), nrows // 2)


def _ici_copies(stack, group_sizes, ssems, rsems):
    x, y, c, chips = _place()
    me_q = 2 * x + y
    sends, recvs = [], []
    a = 0
    for grp, size in enumerate(group_sizes):
        for k in range(size):
            rows = _half(stack[a].shape[1], c)
            for j, (cx, cy) in enumerate(chips):
                mine = stack[a].at[me_q, rows]
                sends.append(_rcopy(mine, mine, ssems[grp].at[k * 3 + j], rsems[grp].at[k * 3 + j], (cx, cy, c)))
                theirs = stack[a].at[2 * cx + cy, rows]
                recvs.append(_rcopy(theirs, theirs, ssems[grp].at[k * 3 + j], rsems[grp].at[k * 3 + j],
                                    (cx, cy, c)))
            a += 1
    return sends, recvs


def _allgather_start(name, stacks, order):
    n = len(stacks)

    def body(bufs, _, new):
        sends, _r = _ici_copies(bufs, [n], [new[0]], [new[1]])
        for cp in sends:
            cp.start()

    return _comm_call(name, body, stacks, order, new_sems=(3 * n, 3 * n))


def _forward_copies(stack, ssem, rsem):
    x, y, c, chips = _place()
    sib = (x, y, 1 - c)
    sends, recvs = [], []
    for a in range(len(stack)):
        for j, (cx, cy) in enumerate(chips):
            landed = stack[a].at[2 * cx + cy, _half(stack[a].shape[1], c)]
            sends.append(_rcopy(landed, landed, ssem.at[a * 3 + j], rsem.at[a * 3 + j], sib))
            other = stack[a].at[2 * cx + cy, _half(stack[a].shape[1], 1 - c)]
            recvs.append(_rcopy(other, other, ssem.at[a * 3 + j], rsem.at[a * 3 + j], sib))
    return sends, recvs


def _allgather_forward(name, stacks, sems, order, behind=()):
    n = len(stacks)

    def body(bufs, taken, new):
        sends, recvs = _ici_copies(bufs, [n], [taken[0]], [taken[1]])
        fwd, _r = _forward_copies(bufs, new[0], new[1])
        for arrived, onward in zip(recvs, fwd):
            arrived.wait_recv()
            onward.start()
        for cp in sends:
            cp.wait_send()

    return _comm_call(name, body, stacks, order, sems_in=sems, new_sems=(3 * n, 3 * n), behind=behind)


def _allgather_finish(name, stacks, sems, order):
    def body(bufs, taken, _):
        sends, recvs = _forward_copies(bufs, taken[0], taken[1])
        for cp in sends:
            cp.wait_send()
        for cp in recvs:
            cp.wait_recv()

    return _comm_call(name, body, stacks, order, sems_in=sems)[1]


def _window_unit(q, j):
    return C2I[WIN_UNIT0[q] + j]


def _pair_copies(g, t, ssem, rsem, gathered):
    x, y, c, _ = _place()
    sib = (x, y, 1 - c)
    cps, whole = [], []
    for a in range(len(g)):
        if a == 0 and gathered:
            for q in range(NCHIP):
                for j in range(WIN_UNITS // 2):
                    u = jnp.where(c == 0, _window_unit(q, WIN_UNITS // 2 + j), _window_unit(q, j))
                    src = g[0].at[pl.ds(pl.multiple_of(u * UNIT, UNIT), UNIT), :]
                    cps.append(_rcopy(src, t[0].at[q, pl.ds(j * UNIT, UNIT), :], ssem.at[0], rsem.at[0], sib))
            whole.append(_rcopy(t[0], t[0], ssem.at[0], rsem.at[0], sib))
        else:
            cp = _rcopy(g[a].at[:, _half(g[a].shape[1], 1 - c), :], t[a], ssem.at[a], rsem.at[a], sib)
            cps.append(cp)
            whole.append(cp)
    return cps, whole


def _comm_multi(name, parts, order):
    def body(buf_refs, taken, new):
        ib = it = inew = 0
        for pbody, pbufs, psems, pnew, _ in parts:
            pbody(buf_refs[ib:ib + len(pbufs)], taken[it:it + len(psems)], new[inew:inew + len(pnew)])
            ib, it, inew = ib + len(pbufs), it + len(psems), inew + len(pnew)

    sems, bufs = _comm_call(name, body, [b for p in parts for b in p[1]], order,
                            sems_in=[s for p in parts for s in p[2]], new_sems=[k for p in parts for k in p[3]])
    out, ib, inew = [], 0, 0
    for _, pbufs, _, pnew, unpack in parts:
        out.append(unpack(sems[inew:inew + len(pnew)], bufs[ib:ib + len(pbufs)]))
        ib, inew = ib + len(pbufs), inew + len(pnew)
    return out


def _pair_start_part(gs, gathered=False):
    n = len(gs)
    ts = [lax.empty((NCHIP, WIN_ROWS // 2, D) if (a == 0 and gathered) else (NCHIP, g.shape[1] // 2, g.shape[2]), f32)
          for a, g in enumerate(gs)]

    def body(bufs, _, new):
        for cp in _pair_copies(bufs[:n], bufs[n:], new[0], new[1], gathered)[0]:
            cp.start()

    return body, list(gs) + ts, (), (n, n), lambda sems, bufs: (sems, bufs)


def _pair_wait_part(bufs, sems, gathered=False):
    n = len(bufs) // 2

    def body(refs, taken, _):
        for cp in _pair_copies(refs[:n], refs[n:], taken[0], taken[1], gathered)[1]:
            cp.wait_send()
            cp.wait_recv()

    return body, list(bufs), list(sems), (), lambda _, out: (out[:n], out[n:])


def _row_tile(h):
    return min(h, 256)


def _pair_add(order, g, t, c_arr, name):
    _, R, C = g.shape
    h = R // 2
    tr = _row_tile(h)
    nblk = h // tr

    def body(c_ref, g_ref, t_ref, p32_ref, p16_ref):
        s = g_ref[...] + t_ref[...]
        p32_ref[...] = s
        p16_ref[...] = s.astype(bf16)

    blk = pl.BlockSpec((None, tr, C), lambda q, i, c_ref: (q, i, 0))
    return _call_indexed(
        order, body, (c_arr,), (g, t), (NCHIP, nblk),
        [pl.BlockSpec((None, tr, C), lambda q, i, c_ref: (q, c_ref[0] * nblk + i, 0)), blk], [blk, blk],
        name=name,
        out_shape=[jax.ShapeDtypeStruct((NCHIP, h, C), f32), jax.ShapeDtypeStruct((NCHIP, h, C), bf16)],
        compiler_params=_params(("parallel", "parallel")),
    )


def _pair_add_gathered(order, dwt, t, c_arr, name):
    half_units, half_rows = WIN_UNITS // 2, WIN_ROWS // 2
    table = jnp.asarray([_window_unit(q, j) for q in range(NCHIP) for j in range(WIN_UNITS)], jnp.int32)

    def body(tab_ref, c_ref, g_hbm, t_ref, p32_ref, p16_ref, buf, sem):
        q = pl.program_id(0)

        def gather(w, slot):
            cps = []
            for j in range(half_units):
                u = tab_ref[w * WIN_UNITS + c_ref[0] * half_units + j]
                cps.append(pltpu.make_async_copy(g_hbm.at[pl.ds(pl.multiple_of(u * UNIT, UNIT), UNIT), :],
                                                 buf.at[slot, pl.ds(j * UNIT, UNIT), :], sem.at[slot]))
            return cps

        @pl.when(q == 0)
        def _():
            for cp in gather(0, 0):
                cp.start()

        @pl.when(q + 1 < NCHIP)
        def _():
            for cp in gather(q + 1, (q + 1) % 2):
                cp.start()

        slot = q % 2
        pltpu.make_async_copy(buf.at[slot], buf.at[slot], sem.at[slot]).wait()
        s = buf[slot] + t_ref[...]
        p32_ref[...] = s
        p16_ref[...] = s.astype(bf16)

    blk = pl.BlockSpec((None, half_rows, D), lambda q, tab_ref, c_ref: (q, 0, 0))
    return _call_indexed(
        order, body, (table, c_arr), (dwt, t), (NCHIP,),
        [pl.BlockSpec(memory_space=pl.ANY), blk], [blk, blk],
        scratch_shapes=[pltpu.VMEM((2, half_rows, D), f32), pltpu.SemaphoreType.DMA((2,))],
        name=name,
        out_shape=[jax.ShapeDtypeStruct((NCHIP, half_rows, D), f32),
                   jax.ShapeDtypeStruct((NCHIP, half_rows, D), bf16)],
        compiler_params=_params(("arbitrary",)),
    )


def _shard_copies(p, r, sm, ssem, rsem):
    x, y, c, chips = _place()
    n = len(p)
    sends, recvs = [], []
    for a in range(n):
        for j, (cx, cy) in enumerate(chips):
            k = a * 3 + j
            sends.append(_rcopy(p[a].at[2 * cx + cy], r[a].at[j], ssem.at[k], rsem.at[k], (cx, cy, c)))
            recvs.append(_rcopy(r[a].at[j], r[a].at[j], ssem.at[k], rsem.at[k], (cx, cy, c)))
    if sm is not None:
        mine = sm.at[4 * x + 2 * y + c]
        for i in range(1, 8):
            px = (1 - x) if i & 4 else x
            py = (1 - y) if i & 2 else y
            pc = (1 - c) if i & 1 else c
            k = 3 * n + i - 1
            sends.append(_rcopy(mine, mine, ssem.at[k], rsem.at[k], (px, py, pc)))
            slot = sm.at[4 * px + 2 * py + pc]
            recvs.append(_rcopy(slot, slot, ssem.at[k], rsem.at[k], (px, py, pc)))
    return sends, recvs


def _shard_start_part(p16s, sm=None):
    n = len(p16s)
    rs = [lax.empty((3,) + p.shape[1:], bf16) for p in p16s]
    extra = [] if sm is None else [sm]
    nsem = 3 * n + (7 if sm is not None else 0)

    def body(bufs, _, new):
        sends, _r = _shard_copies(bufs[:n], bufs[n:2 * n], bufs[2 * n] if extra else None, new[0], new[1])
        for cp in sends:
            cp.start()

    return body, list(p16s) + rs + extra, (), (nsem, nsem), lambda sems, bufs: (sems, bufs)


def _shard_wait_part(bufs, sems, n):
    has_sm = len(bufs) > 2 * n

    def body(refs, taken, _):
        sends, recvs = _shard_copies(refs[:n], refs[n:2 * n], refs[2 * n] if has_sm else None, taken[0], taken[1])
        for cp in sends:
            cp.wait_send()
        for cp in recvs:
            cp.wait_recv()

    return body, list(bufs), list(sems), (), lambda _, out: (out[n:2 * n], (out[2 * n] if has_sm else None))


def _shard_sum(order, p32, r, q_arr, c_arr, name):
    _, h, C = p32.shape
    tr = _row_tile(h)
    nblk = h // tr

    def body(q_ref, c_ref, p_ref, r_ref, o_ref):
        s = p_ref[...]
        for j in range(3):
            s = s + r_ref[j].astype(f32)
        o_ref[...] = s

    return _call_indexed(
        order, body, (q_arr, c_arr), (p32, r), (nblk,),
        [pl.BlockSpec((None, tr, C), lambda i, q_ref, c_ref: (q_ref[0], i, 0)),
         pl.BlockSpec((3, tr, C), lambda i, q_ref, c_ref: (0, i, 0))],
        pl.BlockSpec((tr, C), lambda i, q_ref, c_ref: (c_ref[0] * nblk + i, 0)),
        name=name, out_shape=jax.ShapeDtypeStruct((2 * h, C), f32),
        compiler_params=_params(("parallel",)),
    )


def _swap_copies(full, ssem, rsem):
    x, y, c, _ = _place()
    sends, recvs = [], []
    for a in range(len(full)):
        mine = full[a].at[_half(full[a].shape[0], c)]
        sends.append(_rcopy(mine, mine, ssem.at[a], rsem.at[a], (x, y, 1 - c)))
        other = full[a].at[_half(full[a].shape[0], 1 - c)]
        recvs.append(_rcopy(other, other, ssem.at[a], rsem.at[a], (x, y, 1 - c)))
    return sends, recvs


def _swap_start_part(fulls):
    n = len(fulls)

    def body(bufs, _, new):
        for cp in _swap_copies(bufs, new[0], new[1])[0]:
            cp.start()

    return body, list(fulls), (), (n, n), lambda sems, bufs: (sems, bufs)


def _swap_wait_part(fulls, sems):
    def body(refs, taken, _):
        sends, recvs = _swap_copies(refs, taken[0], taken[1])
        for cp in sends:
            cp.wait_send()
        for cp in recvs:
            cp.wait_recv()

    return body, list(fulls), list(sems), (), lambda _, out: out


def _small_sum(order, sm):
    def body(sm_ref, o_ref):
        s = sm_ref[0]
        for d in range(1, 8):
            s = s + sm_ref[d]
        o_ref[...] = s

    return _call(order, body, (sm,), name="small_grad_sum", out_shape=jax.ShapeDtypeStruct((SMALL_ROWS, D), f32))


def _adamw_math(w, g, m, v):
    m = ADAM_B1 * m + (1.0 - ADAM_B1) * g
    v = ADAM_B2 * v + (1.0 - ADAM_B2) * (g * g)
    m_hat = m / (1.0 - ADAM_B1 ** ADAM_STEP)
    v_hat = v / (1.0 - ADAM_B2 ** ADAM_STEP)
    return -ADAM_LR * (m_hat / (jnp.sqrt(v_hat) + ADAM_EPS) + ADAM_WD * w), m, v


def _adamw_small(order, ws, gs, ms, vs, name):
    n = len(ws)

    def body(*refs):
        for i in range(n):
            res = _adamw_math(*[refs[k * n + i][...] for k in range(4)])
            for k in range(3):
                refs[4 * n + 3 * i + k][...] = res[k]

    out = _call(order, body, list(ws) + list(gs) + list(ms) + list(vs), name=name,
                out_shape=[jax.ShapeDtypeStruct(w.shape, f32) for w in ws for _ in range(3)])
    return [out[3 * i:3 * i + 3] for i in range(n)]


def _adamw(order, w, g, m, v, name):
    R, C = w.shape
    if R <= 256 or R % 256 == 0:
        tr, tc = min(R, 256), C
    else:
        tr, tc = R, 128

    def body(w_ref, g_ref, m_ref, v_ref, d_ref, nm_ref, nv_ref):
        d_ref[...], nm_ref[...], nv_ref[...] = _adamw_math(w_ref[...], g_ref[...], m_ref[...], v_ref[...])

    blk = pl.BlockSpec((tr, tc), lambda i, j: (i, j))
    return _call(
        order, body, (w, g, m, v), name=name, grid=(R // tr, C // tc), in_specs=[blk] * 4, out_specs=[blk] * 3,
        out_shape=[jax.ShapeDtypeStruct((R, C), f32)] * 3,
        compiler_params=_params(("parallel", "parallel")),
    )


def _feature_major(w):
    return jnp.transpose(w, (2, 0, 1)).reshape(SHARD_IN, D)


def _unfeature_major(a):
    return jnp.transpose(a.reshape(SHARD_IN, 1, D), (1, 2, 0))


def _window_of(wt, q):
    def plain(k):
        return lambda w: jnp.pad(w, ((OWN_ROW0[k], WIN_ROWS - OWN_ROW0[k] - SHARD_IN), (0, 0))).astype(bf16)

    def chip1(w):
        lo = jnp.pad(w[0:62], ((2, WIN_ROWS - 64), (0, 0)))
        hi = jnp.pad(w[70:SHARD_IN], ((64, WIN_ROWS - 64 - (SHARD_IN - 70)), (0, 0)))
        return (lo + hi).astype(bf16)

    win = lax.switch(q, [plain(0), chip1, plain(2), plain(3)], wt)
    fa = jnp.pad(wt[62:70], ((0, FA_ROWS - 8), (0, 0))).astype(bf16)
    return win, fa


def _own_rows(gwin, gfa, q):
    def plain(k):
        return lambda gw, gf: gw[OWN_ROW0[k]:OWN_ROW0[k] + SHARD_IN]

    def chip1(gw, gf):
        return (jnp.pad(gw[2:64], ((0, SHARD_IN - 62), (0, 0))) + jnp.pad(gf[0:8], ((62, SHARD_IN - 70), (0, 0)))
                + jnp.pad(gw[64:64 + SHARD_IN - 70], ((70, 0), (0, 0))))

    return lax.switch(q, [plain(0), chip1, plain(2), plain(3)], gwin, gfa)


def kernel(x, norm_attn_g, w_in, b_forget, w_branch_a, w_branch_b, w_out, norm_mlp_g, w_up, w_down, norm_final_g, loss_target, m_norm_attn_g, m_w_in, m_b_forget, m_w_branch_a, m_w_branch_b, m_w_out, m_norm_mlp_g, m_w_up, m_w_down, m_norm_final_g, v_norm_attn_g, v_w_in, v_b_forget, v_w_branch_a, v_w_branch_b, v_w_out, v_norm_mlp_g, v_w_up, v_w_down, v_norm_final_g):
    xi, yi, ci = lax.axis_index("x"), lax.axis_index("y"), lax.axis_index("c")
    q_me = 2 * xi + yi
    c_arr = jnp.reshape(ci, (1,)).astype(jnp.int32)
    q_arr = jnp.reshape(q_me, (1,)).astype(jnp.int32)
    x_, tgt = x[0], loss_target[0]

    names = ["w_branch_a", "w_branch_b", "w_out", "w_up", "w_down"]
    big = dict(zip(names, [w_branch_a[0], w_branch_b[0], w_out[0], w_up[0], w_down[0]]))
    ms = dict(zip(names, [m_w_branch_a[0], m_w_branch_b[0], m_w_out[0], m_w_up[0], m_w_down[0]]))
    vs = dict(zip(names, [v_w_branch_a[0], v_w_branch_b[0], v_w_out[0], v_w_up[0], v_w_down[0]]))
    grad, upd = {}, {}
    order = _Order()

    def run(fn, *args, **kw):
        return fn(order, *args, **kw)

    def own_slot(a):
        return lax.dynamic_update_slice(lax.empty((NCHIP,) + a.shape, a.dtype), a[None], (q_me, 0, 0))

    wt_own = _feature_major(w_in)
    win, fa_blk = _window_of(wt_own, q_me)
    sem_in, in_s = _allgather_start("allgather_start_in", [own_slot(win), own_slot(fa_blk)], order)
    sem_rest, rest = _allgather_start("allgather_start_rest", [own_slot(w.astype(bf16)) for w in big.values()], order)
    rope = _rope_tables(order.tok[0, 0])
    mt_own, vt_own = _feature_major(m_w_in), _feature_major(v_w_in)
    sem_f, in_s = _allgather_forward("allgather_forward_in", in_s, sem_in, order,
                                     behind=[wt_own, mt_own, vt_own, *rope])
    wins, fas = _allgather_finish("allgather_finish_in", in_s, sem_f, order)
    wt = run(_assemble_win, wins, fas)

    bpad = jnp.pad(b_forget, ((0, 0), (0, 120)))
    h1, qkvb, qkva, gates, fa = run(_norm_inproj, x_, norm_attn_g, wt, rope)
    F = run(_forget_cumsum, fa, bpad)
    oa, lsea = run(_fox_fwd, qkva, F)
    sem_f, rest = _allgather_forward("allgather_forward_rest", rest, sem_rest, order)
    ob, lseb = run(_dil_fwd, qkvb)
    was, wbs, wouts, wups, wdowns = _allgather_finish("allgather_finish_rest", rest, sem_f, order)
    wout = wouts.reshape(D, D)
    wdown = wdowns.reshape(DFF, D)
    ya, yb, mixed = run(_branch_mix, oa, ob, was, wbs, gates)
    x2, h2 = run(_outproj_norm, mixed, wout, x_, norm_mlp_g)
    u, a = run(_mlp_up, h2, wups)
    dx3, dx3b, dg3, loss_part = run(_mlp_down_loss, a, wdown, x2, norm_final_g.reshape(1, D), tgt)

    def comm(name, *parts):
        return _comm_multi(name, list(parts), order)

    def pair_adds(group, gs, ts):
        return zip(*[run(_pair_add, gs[i], ts[i], c_arr, "pair_add_" + nm) for i, nm in enumerate(group)])

    def shard_sums(group, p32s, rs):
        return [run(_shard_sum, p32s[i], rs[i], q_arr, c_arr, "shard_sum_" + nm) for i, nm in enumerate(group)]

    def adamw_group(group, fulls):
        for nm, gfull in zip(group, fulls):
            grad[nm] = gfull
            upd[nm] = run(_adamw, big[nm], gfull, ms[nm], vs[nm], "adamw_" + nm)

    grp_a, grp_b, grp_c = ["w_down", "w_up"], ["w_out", "w_branch_a", "w_branch_b"], ["w_in", "w_in_fa"]
    du = run(_mlp_down_bwd, dx3b, wdown, u)
    dwdown = run(_mm, a, dx3b, "tn", f32, 1024, D, "wgrad_down")
    dwup = run(_mm, h2, du, "tn", f32, D, 1024, "wgrad_up", stack_cols=True)
    ((sem_pa, buf_pa),) = comm("pair_start_a", _pair_start_part([dwdown.reshape(NCHIP, DFF // NCHIP, D), dwup]))
    dx2, dx2b, dg2 = run(_mlp_up_bwd, du, wups, x2, dx3, norm_mlp_g)
    ((gs, ts),) = comm("pair_wait_a", _pair_wait_part(buf_pa, sem_pa))
    p32_a, p16_a = pair_adds(grp_a, gs, ts)
    ((sem_sa, buf_sa),) = comm("shard_start_a", _shard_start_part(p16_a))
    dya, dyb, dproj = run(_gate_bwd, dx2b, wout, gates, ya, yb)
    dwout = run(_mm, mixed, dx2b, "tn", f32, D, D, "wgrad_out")
    doa, dob = run(_branch_bwd, dya, dyb, was, wbs)
    dwas, dwbs = run(_branch_wgrad, oa, ob, dya, dyb)
    ((sem_pb, buf_pb),) = comm("pair_start_b", _pair_start_part([dwout.reshape(NCHIP, D // NCHIP, D), dwas, dwbs]))
    dF, dproj = run(_fox_bwd, qkva, doa, oa, lsea, F, dproj)
    (gs, ts), (rs_a, _) = comm("pair_wait_b_shard_wait_a", _pair_wait_part(buf_pb, sem_pb),
                               _shard_wait_part(buf_sa, sem_sa, len(grp_a)))
    p32_b, p16_b = pair_adds(grp_b, gs, ts)
    fulls_a = shard_sums(grp_a, p32_a, rs_a)
    (sem_wa, fulls_a), (sem_sb, buf_sb) = comm("swap_start_a_shard_start_b", _swap_start_part(fulls_a),
                                               _shard_start_part(p16_b))
    dbf, dproj = run(_forget_bwd, dF, fa, bpad, dproj)
    dproj = run(_dil_bwd, qkvb, dob, ob, lseb, rope, dproj)
    (rs_b, _), fulls_a = comm("shard_wait_b_swap_wait_a", _shard_wait_part(buf_sb, sem_sb, len(grp_b)),
                              _swap_wait_part(fulls_a, sem_wa))
    fulls_b = shard_sums(grp_b, p32_b, rs_b)
    ((sem_wb, fulls_b),) = comm("swap_start_b", _swap_start_part(fulls_b))
    dwt = run(_mm, dproj, h1, "tn", f32, 512, D, "wgrad_in")
    dwfa = jnp.broadcast_to(dwt[F_FA:F_FA + FA_ROWS][None], (NCHIP, FA_ROWS, D))
    (sem_pc, buf_pc), fulls_b = comm("pair_start_c_swap_wait_b", _pair_start_part([dwt, dwfa], gathered=True),
                                     _swap_wait_part(fulls_b, sem_wb))
    adamw_group(grp_b, fulls_b)
    (((dwt_c, dwfa_c), (t_in, t_fa)),) = comm("pair_wait_c", _pair_wait_part(buf_pc, sem_pc, gathered=True))
    p32_in, p16_in = run(_pair_add_gathered, dwt_c, t_in, c_arr, "pair_add_w_in")
    p32_fa, p16_fa = run(_pair_add, dwfa_c, t_fa, c_arr, "pair_add_w_in_fa")
    ((sem_sc, buf_sc),) = comm("shard_start_c", _shard_start_part([p16_in, p16_fa]))
    gx, dg1 = run(_inproj_bwd, dproj, wt, x_, dx2, norm_attn_g)
    adamw_group(grp_a, fulls_a)
    small = jnp.concatenate([dg1, dg2, dg3, jnp.pad(dbf[:, 0:8], ((0, 0), (0, D - 8))),
                             jnp.pad(loss_part, ((0, 0), (0, D - 128))),
                             jnp.zeros((SMALL_ROWS - 5, D), f32)], axis=0)
    sm = lax.dynamic_update_slice(lax.empty((8, SMALL_ROWS, D), f32), small[None],
                                  (4 * xi + 2 * yi + ci, 0, 0))
    (sem_sm, buf_sm), (rs_c, _) = comm("small_start_shard_wait_c", _shard_start_part([], sm),
                                       _shard_wait_part(buf_sc, sem_sc, len(grp_c)))
    fulls_c = shard_sums(grp_c, [p32_in, p32_fa], rs_c)
    (sem_wc, fulls_c), (_, sm) = comm("swap_start_c_small_wait", _swap_start_part(fulls_c),
                                      _shard_wait_part(buf_sm, sem_sm, 0))
    gsmall = run(_small_sum, sm)
    loss = gsmall[4, 0]

    grad["norm_attn_g"], grad["norm_mlp_g"] = gsmall[0:1], gsmall[1:2]
    grad["norm_final_g"], grad["b_forget"] = gsmall[2:3], gsmall[3:4, 0:8]
    smalls = ["norm_attn_g", "norm_mlp_g", "norm_final_g", "b_forget"]
    res = run(_adamw_small, [norm_attn_g, norm_mlp_g, norm_final_g.reshape(1, D), b_forget],
              [grad[nm] for nm in smalls],
              [m_norm_attn_g, m_norm_mlp_g, m_norm_final_g.reshape(1, D), m_b_forget],
              [v_norm_attn_g, v_norm_mlp_g, v_norm_final_g.reshape(1, D), v_b_forget], "adamw_small")
    upd.update(zip(smalls, res))

    ((gwin, gfa),) = comm("swap_wait_c", _swap_wait_part(fulls_c, sem_wc))
    g_in = _own_rows(gwin, gfa, q_me)
    upd_in = run(_adamw, wt_own, g_in, mt_own, vt_own, "adamw_w_in")
    grad["w_in"] = _unfeature_major(g_in)
    upd["w_in"] = [_unfeature_major(t) for t in upd_in]

    order_out = ["norm_attn_g", "w_in", "b_forget", "w_branch_a", "w_branch_b", "w_out", "norm_mlp_g", "w_up",
                 "w_down", "norm_final_g"]
    shapes = dict(norm_attn_g=norm_attn_g.shape, w_in=w_in.shape, b_forget=b_forget.shape,
                  w_branch_a=w_branch_a.shape, w_branch_b=w_branch_b.shape, w_out=w_out.shape,
                  norm_mlp_g=norm_mlp_g.shape, w_up=w_up.shape, w_down=w_down.shape, norm_final_g=norm_final_g.shape)
    outs = [loss, gx.reshape(x.shape)]
    outs += [grad[nm].reshape(shapes[nm]) for nm in order_out]
    for k in range(3):
        outs += [upd[nm][k].reshape(shapes[nm]) for nm in order_out]
    return tuple(outs)
```

```python
import jax
import jax.numpy as jnp
from jax import lax
from jax.experimental import pallas as pl
from jax.experimental.pallas import tpu as pltpu

f32 = jnp.float32
bf16 = jnp.bfloat16

S = 2048
D = 1024
DFF = 4096
HD = 64
FOXW = 512
DILOUT = 256
DIL = (1, 4, 16)
BAND = 128
EPS = 1e-6
NEG = -1e30
ROPE_THETA = 500000.0
NCHIP = 4
TQ = 256

ADAM_LR, ADAM_B1, ADAM_B2, ADAM_EPS, ADAM_WD, ADAM_STEP = 0.001, 0.9, 0.999, 1e-08, 0.01, 10
VMEM_LIMIT = 56 * 1024 * 1024

UNIT = 64
NP = 6144
F_DIL, F_FOX, F_FA, F_G = 0, 2304, 3840, 4096
DIL_BLK, FOX_BLK = 1152, 384
WIN_UNITS, WIN_ROWS = 24, 1536
WIN_UNIT0 = (0, 23, 45, 68)
OWN_ROW0 = (0, 2, 60, 62)
SHARD_IN = 1474
N_FA = 8
FA_AT = 1536 - SHARD_IN
FA_ROWS = 32


def _compact_to_internal():
    c2i = {}
    for p in range(2):
        for role in range(3):
            for g in range(3):
                for hh in range(2):
                    c2i[24 + 12 * role + 4 * g + 2 * p + hh] = 18 * p + 6 * role + 2 * g + hh
    for p in range(4):
        for role in range(3):
            for hh in range(2):
                c2i[8 * role + 2 * p + hh] = F_FOX // UNIT + 6 * p + 2 * role + hh
    for j in range(32):
        c2i[60 + j] = F_G // UNIT + j
    return c2i


C2I = _compact_to_internal()
OVERLAP_UNITS = (23, 45, 46, 68)


def _params(sem=None):
    return pltpu.CompilerParams(dimension_semantics=sem, vmem_limit_bytes=VMEM_LIMIT)


class _Order:
    def __init__(self):
        self.tok = None

    def mark(self, v):
        self.tok = v

    def token_for(self, args):
        return [] if self.tok is None or any(self.tok is a for a in args) else [self.tok]


def _call(order, body, args, in_specs=None, **kw):
    args = list(args)
    n_in = len(args)
    if in_specs is None:
        in_specs = [pl.BlockSpec(memory_space=pltpu.VMEM)] * n_in
    kern = body
    extra = order.token_for(args)
    if extra:
        in_specs = list(in_specs) + [pl.BlockSpec(memory_space=pl.ANY)]

        def kern(*refs):
            body(*refs[:n_in], *refs[n_in + 1:])

    out = pl.pallas_call(kern, in_specs=in_specs, **kw)(*args, *extra)
    order.mark(out[0] if isinstance(out, (tuple, list)) else out)
    return out


def _call_indexed(order, body, scalars, args, grid, in_specs, out_specs, scratch_shapes=(), **kw):
    args, in_specs = list(args), list(in_specs)
    n_front = len(scalars) + len(args)
    kern = body
    extra = order.token_for(args)
    if extra:
        in_specs.append(pl.BlockSpec(memory_space=pl.ANY))

        def kern(*refs):
            body(*refs[:n_front], *refs[n_front + 1:])

    out = pl.pallas_call(
        kern, grid_spec=pltpu.PrefetchScalarGridSpec(num_scalar_prefetch=len(scalars), grid=grid, in_specs=in_specs,
                                                     out_specs=out_specs, scratch_shapes=scratch_shapes),
        **kw)(*scalars, *args, *extra)
    order.mark(out[0] if isinstance(out, (tuple, list)) else out)
    return out


def _dot(a, b):
    return jnp.dot(a, b, preferred_element_type=f32)


def _dot_nt(a, b):
    return lax.dot_general(a, b, (((1,), (1,)), ((), ())), preferred_element_type=f32)


def _dot_tn(a, b):
    return lax.dot_general(a, b, (((0,), (0,)), ((), ())), preferred_element_type=f32)


def _split3(x):
    hi = x.astype(bf16)
    r1 = x - hi.astype(f32)
    mid = r1.astype(bf16)
    lo = (r1 - mid.astype(f32)).astype(bf16)
    return hi, mid, lo


def _rope_tables(after):
    half = 8
    inv_freq = jnp.power(jnp.float32(ROPE_THETA), -jnp.arange(half, dtype=f32) * 2.0 / 16)
    ang = (jnp.arange(S).astype(f32) + after)[:, None] * inv_freq[None, :]
    cos, sin = jnp.cos(ang), jnp.sin(ang)
    one = jnp.ones((S, HD - 16), f32)
    zero = jnp.zeros((S, HD - 16), f32)
    z8 = jnp.zeros((S, 8), f32)
    c = jnp.concatenate([cos, cos, one], axis=1)
    s1 = jnp.concatenate([-sin, z8, zero], axis=1)
    s2 = jnp.concatenate([z8, sin, zero], axis=1)
    return tuple(jnp.concatenate([t, t], axis=1) for t in (c, s1, s2))


def _mm(order, a, b, mode, out_dtype, tm, tn, name, stack_cols=False):
    if mode == "nn":
        (M, K), (_, N) = a.shape, b.shape
        a_spec = pl.BlockSpec((tm, K), lambda i, j: (i, 0))
        b_spec = pl.BlockSpec((K, tn), lambda i, j: (0, j))
        dot = _dot
    elif mode == "nt":
        (M, K), (N, _) = a.shape, b.shape
        a_spec = pl.BlockSpec((tm, K), lambda i, j: (i, 0))
        b_spec = pl.BlockSpec((tn, K), lambda i, j: (j, 0))
        dot = _dot_nt
    else:
        (K, M), (_, N) = a.shape, b.shape
        a_spec = pl.BlockSpec((K, tm), lambda i, j: (0, i))
        b_spec = pl.BlockSpec((K, tn), lambda i, j: (0, j))
        dot = _dot_tn

    def body(a_ref, b_ref, o_ref):
        o_ref[...] = dot(a_ref[...], b_ref[...]).astype(out_dtype)

    if stack_cols:
        assert tm == M
        out_spec = pl.BlockSpec((None, tm, tn), lambda i, j: (j, 0, 0))
        out_shape = jax.ShapeDtypeStruct((N // tn, M, tn), out_dtype)
    else:
        out_spec = pl.BlockSpec((tm, tn), lambda i, j: (i, j))
        out_shape = jax.ShapeDtypeStruct((M, N), out_dtype)
    return _call(
        order, body, (a, b), name=name, grid=(M // tm, N // tn), in_specs=[a_spec, b_spec],
        out_specs=out_spec, out_shape=out_shape,
        compiler_params=_params(("parallel", "parallel")),
    )


def _assemble_win(order, wins, fas):
    def body(win_ref, fa_ref, o_ref):
        q = pl.program_id(0)

        @pl.when(q == 0)
        def _():
            o_ref[...] = jnp.zeros_like(o_ref)

        for k in range(NCHIP):
            @pl.when(q == k)
            def _(k=k):
                for j in range(WIN_UNITS):
                    cu = WIN_UNIT0[k] + j
                    dst = pl.ds(C2I[cu] * UNIT, UNIT)
                    if cu in OVERLAP_UNITS:
                        o_ref[dst, :] += win_ref[j * UNIT:(j + 1) * UNIT, :]
                    else:
                        o_ref[dst, :] = win_ref[j * UNIT:(j + 1) * UNIT, :]
                if k == 1:
                    o_ref[F_FA:F_FA + FA_ROWS, :] = fa_ref[...]

    return _call(
        order, body, (wins, fas), name="assemble_w_in", grid=(NCHIP,),
        in_specs=[pl.BlockSpec((None, WIN_ROWS, D), lambda q: (q, 0, 0)),
                  pl.BlockSpec((None, FA_ROWS, D), lambda q: (1, 0, 0))],
        out_specs=pl.BlockSpec((NP, D), lambda q: (0, 0)),
        out_shape=jax.ShapeDtypeStruct((NP, D), bf16),
        compiler_params=_params(("arbitrary",)),
    )


def _norm_inproj(order, x, g1, wt, rope):
    tm = 256
    c_t, s1_t, s2_t = rope

    def body(x_ref, g_ref, w_ref, c_ref, s1_ref, s2_ref, h_ref, qkvb_ref, qkva_ref, gates_ref, fa_ref):
        xb = x_ref[...]
        r = lax.rsqrt(jnp.mean(xb * xb, axis=-1, keepdims=True) + EPS)
        h = ((xb * r) * g_ref[...]).astype(bf16)
        h_ref[...] = h
        c, s1, s2 = c_ref[...], s1_ref[...], s2_ref[...]
        for p in range(2):
            pb = _dot_nt(h, w_ref[F_DIL + p * DIL_BLK:F_DIL + (p + 1) * DIL_BLK, :])
            for ch in range(DIL_BLK // 128):
                pc = pb[:, ch * 128:(ch + 1) * 128]
                if ch < 6:
                    pc = pc * c + pltpu.roll(pc, 120, 1) * s1 + pltpu.roll(pc, 8, 1) * s2
                qkvb_ref[:, p * DIL_BLK + ch * 128:p * DIL_BLK + (ch + 1) * 128] = pc
        qkva_ref[...] = _dot_nt(h, w_ref[F_FOX:F_FA, :]).astype(bf16)
        fa_ref[...] = _dot_nt(h, w_ref[F_FA:F_FA + 128, :])
        gates_ref[...] = _dot_nt(h, w_ref[F_G:NP, :]).astype(bf16)

    row = lambda w: pl.BlockSpec((tm, w), lambda i: (i, 0))
    return _call(
        order, body, (x, g1, wt, c_t, s1_t, s2_t), name="norm_inproj", grid=(S // tm,),
        in_specs=[row(D), pl.BlockSpec((1, D), lambda i: (0, 0)), pl.BlockSpec((NP, D), lambda i: (0, 0)),
                  row(128), row(128), row(128)],
        out_specs=[row(D), row(2 * DIL_BLK), row(4 * FOX_BLK), row(2 * D), row(128)],
        out_shape=[jax.ShapeDtypeStruct((S, D), bf16), jax.ShapeDtypeStruct((S, 2 * DIL_BLK), f32),
                   jax.ShapeDtypeStruct((S, 4 * FOX_BLK), bf16), jax.ShapeDtypeStruct((S, 2 * D), bf16),
                   jax.ShapeDtypeStruct((S, 128), f32)],
        compiler_params=_params(("parallel",)),
    )


def _forget_cumsum(order, fa, bpad):
    nb = S // TQ

    def body(fa_ref, b_ref, F_ref):
        rr = lax.broadcasted_iota(jnp.int32, (TQ, TQ), 0)
        cc = lax.broadcasted_iota(jnp.int32, (TQ, TQ), 1)
        tri = (rr >= cc).astype(bf16)
        lane = lax.broadcasted_iota(jnp.int32, (1, 128), 1)
        carry = jnp.zeros((1, 128), f32)
        for b in range(nb):
            z = fa_ref[b * TQ:(b + 1) * TQ, :] + b_ref[...]
            lf = jnp.minimum(z, 0.0) - jnp.log(1.0 + jnp.exp(-jnp.abs(z)))
            lf = jnp.where(lane < 8, lf, 0.0)
            hi, mid, lo = _split3(lf)
            fb = (_dot(tri, hi) + _dot(tri, mid)) + _dot(tri, lo) + carry
            F_ref[b * TQ:(b + 1) * TQ, :] = fb
            carry = fb[TQ - 1:TQ, :]

    return _call(
        order, body, (fa, bpad), name="forget_cumsum",
        out_shape=jax.ShapeDtypeStruct((S, 128), f32),
        compiler_params=_params(),
    )


def _head_masks():
    lane = lax.broadcasted_iota(jnp.int32, (1, 128), 1)
    return lane, (lane < HD, lane >= HD)


L_ONE = 3
FOX_TQ, FOX_TK = 256, 512


def _set_lanes(x, lane, first, cols):
    for n, col in enumerate(cols):
        x = jnp.where(lane == first + n, col, x)
    return x


def _f32_parts(col):
    return [t.astype(f32) for t in _split3(col)]


def _fox_operands(qkv_ref, F_ref, lse_ref, qa, ka, p, rows):
    lane, hm = _head_masks()
    q = qkv_ref[rows, 0:128].astype(f32) * 0.125
    k = qkv_ref[rows, 128:256].astype(f32)
    Fb = F_ref[rows, :]
    for hh in (0, 1):
        free = (1 - hh) * HD
        fcol = jnp.sum(jnp.where(lane == 2 * p + hh, Fb, 0.0), axis=1, keepdims=True)
        qterm = fcol if lse_ref is None else fcol - lse_ref[rows, hh * HD:hh * HD + 1]
        qcols = _f32_parts(qterm) + [1.0] * 3
        kcols = [1.0] * 3 + [-t for t in _f32_parts(fcol)]
        qa[hh, rows, :] = _set_lanes(jnp.where(hm[hh], q, 0.0), lane, free, qcols).astype(bf16)
        ka[hh, rows, :] = _set_lanes(k, lane, free, kcols).astype(bf16)


def _fox_fwd(order, qkva, F):
    tq, tk = FOX_TQ, FOX_TK

    def body(qkv_ref, F_ref, o_ref, lse_ref, qa, ka, vt):
        p = pl.program_id(0)
        keyi = lax.broadcasted_iota(jnp.int32, (tk, 1), 0)
        qryi = lax.broadcasted_iota(jnp.int32, (1, tq), 1)
        sub = lax.broadcasted_iota(jnp.int32, (128, 1), 0)

        def prep(i, c):
            rows = pl.ds(pl.multiple_of(i * tk, tk), tk)
            _fox_operands(qkv_ref, F_ref, None, qa, ka, p, rows)
            vt[i] = qkv_ref[rows, 256:384].astype(f32).T.astype(bf16)
            return c

        lax.fori_loop(0, S // tk, prep, 0)

        def qblock(i, first_half):
            r0 = pl.multiple_of(i * tq, tq)
            qh = [qa[hh, pl.ds(r0, tq), :] for hh in (0, 1)]

            def kv(jb, carry, masked, width):
                keys = pl.ds(pl.multiple_of(jb * tk, tk), width)
                sts = [_dot_nt(ka[hh, keys, :], qh[hh]) for hh in (0, 1)]
                new = []
                for hh in (0, 1):
                    m, l, a = carry[3 * hh:3 * hh + 3]
                    st = sts[hh]
                    if masked:
                        st = jnp.where(jb * tk + keyi[0:width] <= r0 + qryi, st, NEG)
                    mn = jnp.maximum(m, jnp.max(st, axis=0, keepdims=True))
                    al = jnp.exp(m - mn)
                    pt = jnp.exp(st - mn)
                    l = al * l + jnp.sum(pt, axis=0, keepdims=True)
                    a = al * a + _dot(vt[jb, hh * HD:(hh + 1) * HD, 0:width], pt.astype(bf16))
                    new += [mn, l, a]
                return tuple(new)

            init = (jnp.full((1, tq), NEG, f32), jnp.zeros((1, tq), f32), jnp.zeros((HD, tq), f32)) * 2
            last = (r0 + tq - 1) // tk
            carry = lax.fori_loop(0, last, lambda j, cr: kv(j, cr, False, tk), init)
            m0, l0, a0, m1, l1, a1 = kv(last, carry, True, tk // 2 if first_half else tk)
            ot = jnp.concatenate([a0 / l0, a1 / l1], axis=0)
            lt = jnp.where(sub < HD, m0 + jnp.log(l0), m1 + jnp.log(l1))
            o_ref[pl.ds(r0, tq), :] = ot.T.astype(bf16)
            lse_ref[pl.ds(r0, tq), :] = lt.T

        def qpair(t, c):
            qblock(2 * t, True)
            qblock(2 * t + 1, False)
            return c

        assert tk == 2 * tq
        lax.fori_loop(0, S // tk, qpair, 0)

    pair = pl.BlockSpec((S, 128), lambda p: (0, p))
    return _call(
        order, body, (qkva, F), name="fox_fwd", grid=(4,),
        in_specs=[pl.BlockSpec((S, FOX_BLK), lambda p: (0, p)), pl.BlockSpec((S, 128), lambda p: (0, 0))],
        out_specs=[pair, pair],
        out_shape=[jax.ShapeDtypeStruct((S, FOXW), bf16), jax.ShapeDtypeStruct((S, FOXW), f32)],
        scratch_shapes=[pltpu.VMEM((2, S, 128), bf16)] * 2 + [pltpu.VMEM((S // tk, 128, tk), bf16)],
        compiler_params=_params(("parallel",)),
    )


def _permute_in(dst, src, r):
    L = S // r
    for rho in range(r):
        dst[rho * L:(rho + 1) * L, :] = src[pl.ds(rho, L, stride=r), :]


def _permute_out(dst, src, r):
    L = S // r
    for rho in range(r):
        dst[pl.ds(rho, L, stride=r), :] = src[rho * L:(rho + 1) * L, :]


def _band_width(nbl):
    return BAND if nbl == 1 else 2 * BAND


def _band_geometry(bb, nbl):
    r0 = pl.multiple_of(bb * BAND, BAND)
    if nbl == 1:
        k0 = r0
    else:
        k0 = pl.multiple_of(jnp.maximum(bb - 1, 0) * BAND, BAND)
    sub0 = (bb - lax.rem(bb, nbl)) * BAND
    qi = r0 + lax.broadcasted_iota(jnp.int32, (BAND, 1), 0)
    ki = k0 + lax.broadcasted_iota(jnp.int32, (1, _band_width(nbl)), 1)
    diff = qi - ki
    valid = (diff >= 0) & (diff <= BAND) & (ki >= sub0)
    return r0, k0, valid


def _dil_views(ref):
    return [[ref.at[:, pl.ds((3 * role + g) * 128, 128)] for g in range(3)] for role in range(3)]


DIL_UNROLL = 4


def _dil_in_specs():
    return [pl.BlockSpec((S, 128), lambda p, k=k: (0, 9 * p + k)) for k in range(9)]


def _dil_fwd(order, qkvb):
    def body(*refs):
        q_refs, k_refs, v_refs = refs[0:3], refs[3:6], refs[6:9]
        ob_ref, lse_ref, qp, kp, vp, op, lp = refs[9:16]
        on, ln = refs[16:19], refs[19:22]
        _, hm = _head_masks()
        for g, r in enumerate(DIL):
            nbl = S // r // BAND
            if r == 1:
                qs_, ks_, vs_, od, ld = q_refs[g], k_refs[g], v_refs[g], on[g], ln[g]
            else:
                _permute_in(qp, q_refs[g], r)
                _permute_in(kp, k_refs[g], r)
                _permute_in(vp, v_refs[g], r)
                qs_, ks_, vs_, od, ld = qp, kp, vp, op, lp

            def blk(t, c, qs_=qs_, ks_=ks_, vs_=vs_, od=od, ld=ld, nbl=nbl):
                work = []
                for u in range(DIL_UNROLL):
                    r0, k0, valid = _band_geometry(DIL_UNROLL * t + u, nbl)
                    q = qs_[pl.ds(r0, BAND), :] * 0.125
                    kw = ks_[pl.ds(k0, _band_width(nbl)), :].astype(bf16)
                    vw = vs_[pl.ds(k0, _band_width(nbl)), :]
                    for hh in (0, 1):
                        qh = jnp.where(hm[hh], q, 0.0).astype(bf16)
                        work.append((u, hh, r0, valid, vw, _dot_nt(qh, kw)))
                o = [jnp.zeros((BAND, 128), f32)] * DIL_UNROLL
                lse = [jnp.zeros((BAND, 128), f32)] * DIL_UNROLL
                for u, hh, r0, valid, vw, s in work:
                    s = jnp.where(valid, s, NEG)
                    m = jnp.max(s, axis=1, keepdims=True)
                    pr = jnp.exp(s - m)
                    l = jnp.sum(pr, axis=1, keepdims=True)
                    vm = jnp.where(hm[hh], vw, 0.0).astype(bf16)
                    o[u] = o[u] + _dot((pr / l).astype(bf16), vm)
                    lse[u] = jnp.where(hm[hh], m + jnp.log(l), lse[u])
                    if hh == 1:
                        od[pl.ds(r0, BAND), :] = o[u]
                        ld[pl.ds(r0, BAND), :] = lse[u]
                return c

            lax.fori_loop(0, S // BAND // DIL_UNROLL, blk, 0)
            if r != 1:
                _permute_out(on[g], op, r)
                _permute_out(ln[g], lp, r)

        def combine(i, c):
            r0 = pl.multiple_of(i * TQ, TQ)
            ls = [ln[g][pl.ds(r0, TQ), :] for g in range(3)]
            mx = jnp.maximum(jnp.maximum(ls[0], ls[1]), ls[2])
            es = [jnp.exp(l - mx) for l in ls]
            tot = (es[0] + es[1]) + es[2]
            acc = (es[0] / tot) * on[0][pl.ds(r0, TQ), :]
            acc = acc + (es[1] / tot) * on[1][pl.ds(r0, TQ), :]
            acc = acc + (es[2] / tot) * on[2][pl.ds(r0, TQ), :]
            ob_ref[pl.ds(r0, TQ), :] = acc.astype(bf16)
            lse_ref[pl.ds(r0, TQ), :] = mx + jnp.log(tot)
            return c

        lax.fori_loop(0, S // TQ, combine, 0)

    out_blk = pl.BlockSpec((S, 128), lambda p: (0, p))
    return _call(
        order, body, [qkvb] * 9, name="dil_fwd", grid=(2,),
        in_specs=_dil_in_specs(), out_specs=[out_blk, out_blk],
        out_shape=[jax.ShapeDtypeStruct((S, DILOUT), bf16), jax.ShapeDtypeStruct((S, DILOUT), f32)],
        scratch_shapes=[pltpu.VMEM((S, 128), f32)] * 11,
        compiler_params=_params(("parallel",)),
    )


def _branch_mix(order, oa, ob, was, wbs, gates):
    tm = 512

    def body(oa_ref, ob_ref, wa_ref, wb_ref, g_ref, ya_ref, yb_ref, mix_ref):
        oa_b, ob_b = oa_ref[...], ob_ref[...]
        for q in range(NCHIP):
            cols = slice(q * 256, (q + 1) * 256)
            ya = _dot(oa_b, wa_ref[q])
            yb = _dot(ob_b, wb_ref[q])
            ya_ref[:, cols] = ya.astype(bf16)
            yb_ref[:, cols] = yb.astype(bf16)
            ga = g_ref[:, q * 256:(q + 1) * 256].astype(f32)
            gb = g_ref[:, D + q * 256:D + (q + 1) * 256].astype(f32)
            mix_ref[:, cols] = (jax.nn.sigmoid(ga) * ya + jax.nn.sigmoid(gb) * yb).astype(bf16)

    row = lambda w: pl.BlockSpec((tm, w), lambda i: (i, 0))
    full3 = lambda a: pl.BlockSpec(a.shape, lambda i: (0, 0, 0))
    return _call(
        order, body, (oa, ob, was, wbs, gates), name="branch_mix", grid=(S // tm,),
        in_specs=[row(FOXW), row(DILOUT), full3(was), full3(wbs), row(2 * D)],
        out_specs=[row(D), row(D), row(D)],
        out_shape=[jax.ShapeDtypeStruct((S, D), bf16), jax.ShapeDtypeStruct((S, D), bf16),
                   jax.ShapeDtypeStruct((S, D), bf16)],
        compiler_params=_params(("parallel",)),
    )


def _outproj_norm(order, mixed, wout, x, g2):
    tm = 512

    def body(m_ref, w_ref, x_ref, g_ref, x2_ref, h2_ref):
        x2 = x_ref[...] + _dot(m_ref[...], w_ref[...])
        x2_ref[...] = x2
        r = lax.rsqrt(jnp.mean(x2 * x2, axis=-1, keepdims=True) + EPS)
        h2_ref[...] = ((x2 * r) * g_ref[...]).astype(bf16)

    row = pl.BlockSpec((tm, D), lambda i: (i, 0))
    return _call(
        order, body, (mixed, wout, x, g2), name="outproj_norm", grid=(S // tm,),
        in_specs=[row, pl.BlockSpec((D, D), lambda i: (0, 0)), row, pl.BlockSpec((1, D), lambda i: (0, 0))],
        out_specs=[row, row],
        out_shape=[jax.ShapeDtypeStruct((S, D), f32), jax.ShapeDtypeStruct((S, D), bf16)],
        compiler_params=_params(("parallel",)),
    )


def _mlp_up(order, h2, wups):
    tm = 1024

    def body(h_ref, w_ref, ru_ref, a_ref):
        ru = jnp.maximum(_dot(h_ref[...], w_ref[...]), 0.0)
        ru_ref[...] = ru.astype(bf16)
        a_ref[...] = (ru * ru).astype(bf16)

    out = pl.BlockSpec((tm, D), lambda q, i: (i, q))
    return _call(
        order, body, (h2, wups), name="mlp_up", grid=(NCHIP, S // tm),
        in_specs=[pl.BlockSpec((tm, D), lambda q, i: (i, 0)), pl.BlockSpec((None, D, D), lambda q, i: (q, 0, 0))],
        out_specs=[out, out],
        out_shape=[jax.ShapeDtypeStruct((S, DFF), bf16), jax.ShapeDtypeStruct((S, DFF), bf16)],
        compiler_params=_params(("parallel", "parallel")),
    )


def _mlp_down_loss(order, a, wdown, x2, g3, tgt):
    tm = 512

    def body(a_ref, w_ref, x2_ref, g_ref, t_ref, dx_ref, dxb_ref, dg_ref, loss_ref):
        i = pl.program_id(0)
        x3 = x2_ref[...] + _dot(a_ref[...], w_ref[...])
        r = lax.rsqrt(jnp.mean(x3 * x3, axis=-1, keepdims=True) + EPS)
        xh = x3 * r
        g = g_ref[...]
        e = xh * g - t_ref[...]
        part = 0.5 * jnp.sum(jnp.mean(e * e, axis=-1, keepdims=True), axis=0, keepdims=True)
        dy = e * (1.0 / D)
        gdy = dy * g
        dx = r * (gdy - xh * jnp.mean(gdy * xh, axis=-1, keepdims=True))
        dx_ref[...] = dx
        dxb_ref[...] = dx.astype(bf16)

        @pl.when(i == 0)
        def _():
            dg_ref[...] = jnp.zeros_like(dg_ref)
            loss_ref[...] = jnp.zeros_like(loss_ref)

        dg_ref[...] += jnp.sum(dy * xh, axis=0, keepdims=True)
        loss_ref[...] += jnp.broadcast_to(part, (1, 128))

    row = pl.BlockSpec((tm, D), lambda i: (i, 0))
    vec = pl.BlockSpec((1, D), lambda i: (0, 0))
    return _call(
        order, body, (a, wdown, x2, g3, tgt), name="mlp_down_loss", grid=(S // tm,),
        in_specs=[pl.BlockSpec((tm, DFF), lambda i: (i, 0)), pl.BlockSpec((DFF, D), lambda i: (0, 0)), row, vec, row],
        out_specs=[row, row, vec, pl.BlockSpec((1, 128), lambda i: (0, 0))],
        out_shape=[jax.ShapeDtypeStruct((S, D), f32), jax.ShapeDtypeStruct((S, D), bf16),
                   jax.ShapeDtypeStruct((1, D), f32), jax.ShapeDtypeStruct((1, 128), f32)],
        compiler_params=_params(("arbitrary",)),
    )


def _mlp_down_bwd(order, dx3b, wdown, u):
    tm = 512

    def body(d_ref, w_ref, u_ref, du_ref):
        d = d_ref[...]
        for q in range(NCHIP):
            cols = slice(q * D, (q + 1) * D)
            da = _dot_nt(d, w_ref[cols, :])
            du_ref[:, cols] = (da * (2.0 * u_ref[:, cols].astype(f32))).astype(bf16)

    return _call(
        order, body, (dx3b, wdown, u), name="mlp_down_bwd", grid=(S // tm,),
        in_specs=[pl.BlockSpec((tm, D), lambda i: (i, 0)), pl.BlockSpec((DFF, D), lambda i: (0, 0)),
                  pl.BlockSpec((tm, DFF), lambda i: (i, 0))],
        out_specs=pl.BlockSpec((tm, DFF), lambda i: (i, 0)),
        out_shape=jax.ShapeDtypeStruct((S, DFF), bf16),
        compiler_params=_params(("parallel",)),
    )


def _mlp_up_bwd(order, du, wups, x2, dx3, g2):
    tm = 512

    def body(du_ref, w_ref, x2_ref, dx3_ref, g_ref, dx2_ref, dx2b_ref, dg_ref):
        i = pl.program_id(0)
        dh = jnp.zeros((tm, D), f32)
        for q in range(NCHIP):
            dh = dh + _dot_nt(du_ref[:, q * D:(q + 1) * D], w_ref[q])
        x2 = x2_ref[...]
        r = lax.rsqrt(jnp.mean(x2 * x2, axis=-1, keepdims=True) + EPS)
        xh = x2 * r
        gdh = dh * g_ref[...]
        dx2 = dx3_ref[...] + r * (gdh - xh * jnp.mean(gdh * xh, axis=-1, keepdims=True))
        dx2_ref[...] = dx2
        dx2b_ref[...] = dx2.astype(bf16)

        @pl.when(i == 0)
        def _():
            dg_ref[...] = jnp.zeros_like(dg_ref)

        dg_ref[...] += jnp.sum(dh * xh, axis=0, keepdims=True)

    row = pl.BlockSpec((tm, D), lambda i: (i, 0))
    vec = pl.BlockSpec((1, D), lambda i: (0, 0))
    return _call(
        order, body, (du, wups, x2, dx3, g2), name="mlp_up_bwd", grid=(S // tm,),
        in_specs=[pl.BlockSpec((tm, DFF), lambda i: (i, 0)), pl.BlockSpec((NCHIP, D, D), lambda i: (0, 0, 0)),
                  row, row, vec],
        out_specs=[row, row, vec],
        out_shape=[jax.ShapeDtypeStruct((S, D), f32), jax.ShapeDtypeStruct((S, D), bf16),
                   jax.ShapeDtypeStruct((1, D), f32)],
        compiler_params=_params(("arbitrary",)),
    )


def _gate_bwd(order, dx2b, wout, gates, ya, yb):
    tm = 512

    def body(d_ref, w_ref, g_ref, ya_ref, yb_ref, dya_ref, dyb_ref, dproj_ref):
        dm = _dot_nt(d_ref[...], w_ref[...])
        sa = jax.nn.sigmoid(g_ref[:, 0:D].astype(f32))
        sb = jax.nn.sigmoid(g_ref[:, D:2 * D].astype(f32))
        dya_ref[...] = (dm * sa).astype(bf16)
        dyb_ref[...] = (dm * sb).astype(bf16)
        dproj_ref[:, 0:D] = (dm * ya_ref[...].astype(f32) * (sa * (1.0 - sa))).astype(bf16)
        dproj_ref[:, D:2 * D] = (dm * yb_ref[...].astype(f32) * (sb * (1.0 - sb))).astype(bf16)

    row = lambda w: pl.BlockSpec((tm, w), lambda i: (i, 0))
    return _call(
        order, body, (dx2b, wout, gates, ya, yb), name="gate_bwd", grid=(S // tm,),
        in_specs=[row(D), pl.BlockSpec((D, D), lambda i: (0, 0)), row(2 * D), row(D), row(D)],
        out_specs=[row(D), row(D), pl.BlockSpec((tm, 2 * D), lambda i: (i, F_G // (2 * D)))],
        out_shape=[jax.ShapeDtypeStruct((S, D), bf16), jax.ShapeDtypeStruct((S, D), bf16),
                   jax.ShapeDtypeStruct((S, NP), bf16)],
        compiler_params=_params(("parallel",)),
    )


def _branch_bwd(order, dya, dyb, was, wbs):
    tm = 512

    def body(dya_ref, dyb_ref, wa_ref, wb_ref, doa_ref, dob_ref):
        doa = jnp.zeros((tm, FOXW), f32)
        dob = jnp.zeros((tm, DILOUT), f32)
        for q in range(NCHIP):
            cols = slice(q * 256, (q + 1) * 256)
            doa = doa + _dot_nt(dya_ref[:, cols], wa_ref[q])
            dob = dob + _dot_nt(dyb_ref[:, cols], wb_ref[q])
        doa_ref[...] = doa.astype(bf16)
        dob_ref[...] = dob

    row = lambda w: pl.BlockSpec((tm, w), lambda i: (i, 0))
    full3 = lambda a: pl.BlockSpec(a.shape, lambda i: (0, 0, 0))
    return _call(
        order, body, (dya, dyb, was, wbs), name="branch_bwd", grid=(S // tm,),
        in_specs=[row(D), row(D), full3(was), full3(wbs)],
        out_specs=[row(FOXW), row(DILOUT)],
        out_shape=[jax.ShapeDtypeStruct((S, FOXW), bf16), jax.ShapeDtypeStruct((S, DILOUT), f32)],
        compiler_params=_params(("parallel",)),
    )


def _branch_wgrad(order, oa, ob, dya, dyb):
    def body(oa_ref, ob_ref, dya_ref, dyb_ref, dwa_ref, dwb_ref):
        dwa_ref[...] = _dot_tn(oa_ref[...], dya_ref[...])
        dwb_ref[...] = _dot_tn(ob_ref[...], dyb_ref[...])

    full = lambda w: pl.BlockSpec((S, w), lambda q: (0, 0))
    colq = pl.BlockSpec((S, 256), lambda q: (0, q))
    return _call(
        order, body, (oa, ob, dya, dyb), name="branch_wgrad", grid=(NCHIP,),
        in_specs=[full(FOXW), full(DILOUT), colq, colq],
        out_specs=[pl.BlockSpec((None, FOXW, 256), lambda q: (q, 0, 0)),
                   pl.BlockSpec((None, DILOUT, 256), lambda q: (q, 0, 0))],
        out_shape=[jax.ShapeDtypeStruct((NCHIP, FOXW, 256), f32), jax.ShapeDtypeStruct((NCHIP, DILOUT, 256), f32)],
        compiler_params=_params(("parallel",)),
    )


def _fox_bwd(order, qkva, doa, oa, lse, F, dproj):
    tq, tk = FOX_TQ, FOX_TK

    def body(qkv_ref, do_ref, o_ref, lse_ref, F_ref, _dproj_in, dF_ref, dqkv_ref, qa, ka, da, va, kat,
             dk_scr, dv_scr, dqt_scr):
        p = pl.program_id(0)
        lane, hm = _head_masks()
        keyi = lax.broadcasted_iota(jnp.int32, (tk, 1), 0)
        qryi = lax.broadcasted_iota(jnp.int32, (1, tq), 1)

        def prep(i, c):
            rows = pl.ds(pl.multiple_of(i * tk, tk), tk)
            _fox_operands(qkv_ref, F_ref, lse_ref, qa, ka, p, rows)
            do = do_ref[rows, :].astype(f32)
            prod = do * o_ref[rows, :].astype(f32)
            v = qkv_ref[rows, 256:384].astype(f32)
            for hh in (0, 1):
                free = (1 - hh) * HD
                delta = jnp.sum(jnp.where(hm[hh], prod, 0.0), axis=1, keepdims=True)
                da[hh, rows, :] = _set_lanes(jnp.where(hm[hh], do, 0.0), lane, free,
                                             [-t for t in _f32_parts(delta)]).astype(bf16)
                va[hh, rows, :] = _set_lanes(v, lane, free, [1.0] * 3).astype(bf16)
                kat[hh, i] = ka[hh, rows, :].astype(f32).T.astype(bf16)
                dk_scr[hh, rows, :] = jnp.zeros((tk, 128), f32)
                dv_scr[hh, rows, :] = jnp.zeros((tk, 128), f32)
            return c

        lax.fori_loop(0, S // tk, prep, 0)

        def qblock(i, first_half):
            r0 = pl.multiple_of(i * tq, tq)
            qrows = pl.ds(r0, tq)
            qh = [qa[hh, qrows, :] for hh in (0, 1)]
            dh = [da[hh, qrows, :] for hh in (0, 1)]
            dqt_scr[...] = jnp.zeros_like(dqt_scr)

            def kv(jb, c2, masked, width):
                keys = pl.ds(pl.multiple_of(jb * tk, tk), width)
                sts = [_dot_nt(ka[hh, keys, :], qh[hh]) for hh in (0, 1)]
                dps = [_dot_nt(va[hh, keys, :], dh[hh]) for hh in (0, 1)]
                for hh in (0, 1):
                    pt = jnp.exp(sts[hh])
                    if masked:
                        pt = jnp.where(jb * tk + keyi[0:width] <= r0 + qryi, pt, 0.0)
                    dsb = (pt * dps[hh]).astype(bf16)
                    dv_scr[hh, keys, :] += _dot(pt.astype(bf16), dh[hh])
                    dk_scr[hh, keys, :] += _dot(dsb, qh[hh])
                    dqt_scr[hh] += _dot(kat[hh, jb, :, 0:width], dsb)
                return c2

            last = (r0 + tq - 1) // tk
            lax.fori_loop(0, last, lambda j, c2: kv(j, c2, False, tk), 0)
            kv(last, 0, True, tk // 2 if first_half else tk)
            dq0, dq1 = dqt_scr[0].T, dqt_scr[1].T
            dqkv_ref[qrows, 0:128] = (jnp.where(hm[0], dq0, dq1) * 0.125).astype(bf16)
            dF_ref[qrows, :] = jnp.where(lane == 0, dq0[:, HD:HD + 1], jnp.where(lane == 1, dq1[:, 0:1], 0.0))

        def qpair(t, c):
            qblock(2 * t, True)
            qblock(2 * t + 1, False)
            return c

        assert tk == 2 * tq
        lax.fori_loop(0, S // tk, qpair, 0)

        def finish(i, c):
            rows = pl.ds(pl.multiple_of(i * tq, tq), tq)
            dk0, dk1 = dk_scr[0, rows, :], dk_scr[1, rows, :]
            dqkv_ref[rows, 128:256] = jnp.where(hm[0], dk0, dk1).astype(bf16)
            dqkv_ref[rows, 256:384] = jnp.where(hm[0], dv_scr[0, rows, :], dv_scr[1, rows, :]).astype(bf16)
            cs = jnp.where(lane == 0, dk0[:, HD + L_ONE:HD + L_ONE + 1],
                           jnp.where(lane == 1, dk1[:, L_ONE:L_ONE + 1], 0.0))
            dF_ref[rows, :] = dF_ref[rows, :] - cs
            return c

        lax.fori_loop(0, S // tq, finish, 0)

    pair = pl.BlockSpec((S, 128), lambda p: (0, p))
    return _call(
        order, body, (qkva, doa, oa, lse, F, dproj), name="fox_bwd", grid=(4,),
        in_specs=[pl.BlockSpec((S, FOX_BLK), lambda p: (0, p)), pair, pair, pair,
                  pl.BlockSpec((S, 128), lambda p: (0, 0)), pl.BlockSpec(memory_space=pl.ANY)],
        out_specs=[pair, pl.BlockSpec((S, FOX_BLK), lambda p: (0, F_FOX // FOX_BLK + p))],
        out_shape=[jax.ShapeDtypeStruct((S, FOXW), f32), jax.ShapeDtypeStruct((S, NP), bf16)],
        input_output_aliases={5: 1},
        scratch_shapes=[pltpu.VMEM((2, S, 128), bf16)] * 4 + [pltpu.VMEM((2, S // tk, 128, tk), bf16)]
        + [pltpu.VMEM((2, S, 128), f32)] * 2 + [pltpu.VMEM((2, 128, tq), f32)],
        compiler_params=_params(("parallel",)),
    )


def _forget_bwd(order, dF, fa, bpad, dproj):
    nb = S // TQ

    def body(dF_ref, fa_ref, b_ref, _dproj_in, db_ref, dfa_ref):
        rr = lax.broadcasted_iota(jnp.int32, (TQ, TQ), 0)
        cc = lax.broadcasted_iota(jnp.int32, (TQ, TQ), 1)
        upper = (cc >= rr).astype(bf16)
        lane = lax.broadcasted_iota(jnp.int32, (1, 128), 1)
        carry = jnp.zeros((1, 128), f32)
        db = jnp.zeros((1, 128), f32)
        for b in reversed(range(nb)):
            cols = jnp.zeros((TQ, 128), f32)
            for h in range(8):
                c0 = (h // 2) * 128 + h % 2
                cols = jnp.where(lane == h, dF_ref[b * TQ:(b + 1) * TQ, c0:c0 + 1], cols)
            dlf = carry
            for part in _split3(cols):
                dlf = dlf + _dot(upper, part)
            carry = carry + jnp.sum(cols, axis=0, keepdims=True)
            z = fa_ref[b * TQ:(b + 1) * TQ, :] + b_ref[...]
            dz = jnp.where(lane < 8, dlf * jax.nn.sigmoid(-z), 0.0)
            dfa_ref[b * TQ:(b + 1) * TQ, 0:128] = dz.astype(bf16)
            dfa_ref[b * TQ:(b + 1) * TQ, 128:256] = jnp.zeros((TQ, 128), bf16)
            db = db + jnp.sum(dz, axis=0, keepdims=True)
        db_ref[...] = db

    whole = lambda a: pl.BlockSpec(a.shape, lambda i: (0,) * a.ndim)
    return _call(
        order, body, (dF, fa, bpad, dproj), name="forget_bwd", grid=(1,),
        in_specs=[whole(dF), whole(fa), whole(bpad), pl.BlockSpec(memory_space=pl.ANY)],
        out_specs=[pl.BlockSpec((1, 128), lambda i: (0, 0)), pl.BlockSpec((S, 256), lambda i: (0, F_FA // 256))],
        out_shape=[jax.ShapeDtypeStruct((1, 128), f32), jax.ShapeDtypeStruct((S, NP), bf16)],
        input_output_aliases={3: 1},
        compiler_params=_params(("arbitrary",)),
    )


def _dil_bwd(order, qkvb, dob, ob, lseb, rope, dproj):
    c_t, s1_t, s2_t = rope

    def body(*refs):
        q_refs, k_refs, v_refs = refs[0:3], refs[3:6], refs[6:9]
        dob_ref, ob_ref, lse_ref, c_ref, s1_ref, s2_ref, _dproj_in, dqkv_ref = refs[9:17]
        qp, kp, vp, dop, lp, dlp, dln, dqp, dkp, dvp, nat = refs[17:28]
        dq_out, dk_out, dv_out = _dil_views(dqkv_ref)
        _, hm = _head_masks()

        def delta_rows(i, c):
            r0 = pl.multiple_of(i * TQ, TQ)
            prod = dob_ref[pl.ds(r0, TQ), :] * ob_ref[pl.ds(r0, TQ), :].astype(f32)
            d0 = jnp.sum(jnp.where(hm[0], prod, 0.0), axis=1, keepdims=True)
            d1 = jnp.sum(jnp.where(hm[1], prod, 0.0), axis=1, keepdims=True)
            dln[pl.ds(r0, TQ), :] = jnp.where(hm[0], d0, d1)
            return c

        lax.fori_loop(0, S // TQ, delta_rows, 0)

        for g, r in enumerate(DIL):
            nbl = S // r // BAND
            if r == 1:
                srcs = (q_refs[g], k_refs[g], v_refs[g], dob_ref, lse_ref, dln)
            else:
                for dst, src in ((qp, q_refs[g]), (kp, k_refs[g]), (vp, v_refs[g]), (dop, dob_ref),
                                 (lp, lse_ref), (dlp, dln)):
                    _permute_in(dst, src, r)
                srcs = (qp, kp, vp, dop, lp, dlp)
            dkp[...] = jnp.zeros_like(dkp)
            dvp[...] = jnp.zeros_like(dvp)

            def blk(t, c, srcs=srcs, nbl=nbl):
                qs_, ks_, vs_, dos_, ls_, dls_ = srcs
                work = []
                for u in range(DIL_UNROLL):
                    r0, k0, valid = _band_geometry(DIL_UNROLL * t + u, nbl)
                    q = qs_[pl.ds(r0, BAND), :] * 0.125
                    kwf = ks_[pl.ds(k0, _band_width(nbl)), :]
                    kw = kwf.astype(bf16)
                    vw = vs_[pl.ds(k0, _band_width(nbl)), :].astype(bf16)
                    do = dos_[pl.ds(r0, BAND), :]
                    lse = ls_[pl.ds(r0, BAND), :]
                    dlt = dls_[pl.ds(r0, BAND), :]
                    for hh in (0, 1):
                        qh = jnp.where(hm[hh], q, 0.0).astype(bf16)
                        doh = jnp.where(hm[hh], do, 0.0).astype(bf16)
                        kh = jnp.where(hm[hh], kwf, 0.0).astype(bf16)
                        work.append((u, hh, r0, k0, valid, qh, doh, kh, lse[:, hh * HD:hh * HD + 1],
                                     dlt[:, hh * HD:hh * HD + 1], _dot_nt(qh, kw), _dot_nt(doh, vw)))
                for u, hh, r0, k0, valid, qh, doh, kh, lse_h, dlt_h, s, dp in work:
                    if hh == 0:
                        dq = jnp.zeros((BAND, 128), f32)
                        dk = jnp.zeros((_band_width(nbl), 128), f32)
                        dv = jnp.zeros((_band_width(nbl), 128), f32)
                    pr = jnp.where(valid, jnp.exp(s - lse_h), 0.0)
                    dsb = (pr * (dp - dlt_h)).astype(bf16)
                    dv = dv + _dot_tn(pr.astype(bf16), doh)
                    dk = dk + _dot_tn(dsb, qh)
                    dq = dq + _dot(dsb, kh)
                    if hh == 1:
                        dqp[pl.ds(r0, BAND), :] = dq * 0.125
                        dkp[pl.ds(k0, _band_width(nbl)), :] += dk
                        dvp[pl.ds(k0, _band_width(nbl)), :] += dv
                return c

            lax.fori_loop(0, S // BAND // DIL_UNROLL, blk, 0)

            for acc, out, roped in ((dqp, dq_out[g], True), (dkp, dk_out[g], True), (dvp, dv_out[g], False)):
                if r == 1:
                    src = acc
                else:
                    _permute_out(nat, acc, r)
                    src = nat

                def emit(i, c, src=src, out=out, roped=roped):
                    r0 = pl.multiple_of(i * TQ, TQ)
                    d = src[pl.ds(r0, TQ), :]
                    if roped:
                        d = (d * c_ref[pl.ds(r0, TQ), :] + pltpu.roll(d * s1_ref[pl.ds(r0, TQ), :], 8, 1)
                             + pltpu.roll(d * s2_ref[pl.ds(r0, TQ), :], 120, 1))
                    out[pl.ds(r0, TQ), :] = d.astype(bf16)
                    return c

                lax.fori_loop(0, S // TQ, emit, 0)

    pair = pl.BlockSpec((S, 128), lambda p: (0, p))
    tab = pl.BlockSpec((S, 128), lambda p: (0, 0))
    blk_spec = pl.BlockSpec((S, DIL_BLK), lambda p: (0, p))
    return _call(
        order, body, [qkvb] * 9 + [dob, ob, lseb, c_t, s1_t, s2_t, dproj], name="dil_bwd", grid=(2,),
        in_specs=_dil_in_specs() + [pair, pair, pair, tab, tab, tab, pl.BlockSpec(memory_space=pl.ANY)],
        out_specs=blk_spec,
        out_shape=jax.ShapeDtypeStruct((S, NP), bf16),
        input_output_aliases={15: 0},
        scratch_shapes=[pltpu.VMEM((S, 128), f32)] * 11,
        compiler_params=_params(("parallel",)),
    )


def _inproj_bwd(order, dproj, wt, x, dx2, g1):
    tm = 256

    def body(d_ref, w_ref, x_ref, dx2_ref, g_ref, dx_ref, dg_ref):
        i = pl.program_id(0)
        dh = _dot(d_ref[...], w_ref[...])
        xb = x_ref[...]
        r = lax.rsqrt(jnp.mean(xb * xb, axis=-1, keepdims=True) + EPS)
        xh = xb * r
        gdh = dh * g_ref[...]
        dx_ref[...] = dx2_ref[...] + r * (gdh - xh * jnp.mean(gdh * xh, axis=-1, keepdims=True))

        @pl.when(i == 0)
        def _():
            dg_ref[...] = jnp.zeros_like(dg_ref)

        dg_ref[...] += jnp.sum(dh * xh, axis=0, keepdims=True)

    row = pl.BlockSpec((tm, D), lambda i: (i, 0))
    vec = pl.BlockSpec((1, D), lambda i: (0, 0))
    return _call(
        order, body, (dproj, wt, x, dx2, g1), name="inproj_bwd", grid=(S // tm,),
        in_specs=[pl.BlockSpec((tm, NP), lambda i: (i, 0)), pl.BlockSpec((NP, D), lambda i: (0, 0)), row, row, vec],
        out_specs=[row, vec],
        out_shape=[jax.ShapeDtypeStruct((S, D), f32), jax.ShapeDtypeStruct((1, D), f32)],
        compiler_params=_params(("arbitrary",)),
    )


HBM = pl.BlockSpec(memory_space=pltpu.HBM)
SEM = pl.BlockSpec(memory_space=pltpu.SEMAPHORE)
SMALL_ROWS = 8


def _comm_call(name, body, bufs, order, sems_in=(), new_sems=(), behind=()):
    nb, ns, nn = len(bufs), len(sems_in), len(new_sems)
    extra = order.token_for(bufs) + list(behind)

    def kern(*refs):
        off = nb + ns + len(extra)
        body(refs[:nb], refs[nb:nb + ns], refs[off:off + nn])
        refs[-1][...] = jnp.zeros((8, 128), f32)

    res = pl.pallas_call(
        kern, name=name,
        in_specs=[HBM] * nb + [SEM] * ns + [pl.BlockSpec(memory_space=pl.ANY)] * len(extra),
        out_specs=[SEM] * nn + [HBM] * nb + [pl.BlockSpec(memory_space=pltpu.VMEM)],
        out_shape=[pltpu.SemaphoreType.DMA((k,)) for k in new_sems] + [pltpu.HBM(b.shape, b.dtype) for b in bufs]
        + [jax.ShapeDtypeStruct((8, 128), f32)],
        input_output_aliases={i: nn + i for i in range(nb)},
        compiler_params=pltpu.CompilerParams(has_side_effects=pltpu.SideEffectType.DATAFLOW_SIDE_EFFECTING),
    )(*[pltpu.with_memory_space_constraint(b, pltpu.HBM) for b in bufs], *sems_in, *extra)
    order.mark(res[-1])
    return list(res[:nn]), list(res[nn:nn + nb])


def _place():
    x, y, c = lax.axis_index("x"), lax.axis_index("y"), lax.axis_index("c")
    chips = [(1 - x, y), (x, 1 - y), (1 - x, 1 - y)]
    return x, y, c, chips


def _rcopy(src, dst, ssem, rsem, dev):
    return pltpu.make_async_remote_copy(src_ref=src, dst_ref=dst, send_sem=ssem, recv_sem=rsem,
                                        device_id=dev, device_id_type=pl.DeviceIdType.MESH)


def _half(nrows, which):
    return pl.ds(which * (nrows // 2), nrows // 2)


def _ici_copies(stack, group_sizes, ssems, rsems):
    x, y, c, chips = _place()
    me_q = 2 * x + y
    sends, recvs = [], []
    a = 0
    for grp, size in enumerate(group_sizes):
        for k in range(size):
            rows = _half(stack[a].shape[1], c)
            for j, (cx, cy) in enumerate(chips):
                mine = stack[a].at[me_q, rows]
                sends.append(_rcopy(mine, mine, ssems[grp].at[k * 3 + j], rsems[grp].at[k * 3 + j], (cx, cy, c)))
                theirs = stack[a].at[2 * cx + cy, rows]
                recvs.append(_rcopy(theirs, theirs, ssems[grp].at[k * 3 + j], rsems[grp].at[k * 3 + j],
                                    (cx, cy, c)))
            a += 1
    return sends, recvs


def _allgather_start(name, stacks, order):
    n = len(stacks)

    def body(bufs, _, new):
        sends, _r = _ici_copies(bufs, [n], [new[0]], [new[1]])
        for cp in sends:
            cp.start()

    return _comm_call(name, body, stacks, order, new_sems=(3 * n, 3 * n))


def _forward_copies(stack, ssem, rsem):
    x, y, c, chips = _place()
    sib = (x, y, 1 - c)
    sends, recvs = [], []
    for a in range(len(stack)):
        for j, (cx, cy) in enumerate(chips):
            landed = stack[a].at[2 * cx + cy, _half(stack[a].shape[1], c)]
            sends.append(_rcopy(landed, landed, ssem.at[a * 3 + j], rsem.at[a * 3 + j], sib))
            other = stack[a].at[2 * cx + cy, _half(stack[a].shape[1], 1 - c)]
            recvs.append(_rcopy(other, other, ssem.at[a * 3 + j], rsem.at[a * 3 + j], sib))
    return sends, recvs


def _allgather_forward(name, stacks, sems, order, behind=()):
    n = len(stacks)

    def body(bufs, taken, new):
        sends, recvs = _ici_copies(bufs, [n], [taken[0]], [taken[1]])
        fwd, _r = _forward_copies(bufs, new[0], new[1])
        for arrived, onward in zip(recvs, fwd):
            arrived.wait_recv()
            onward.start()
        for cp in sends:
            cp.wait_send()

    return _comm_call(name, body, stacks, order, sems_in=sems, new_sems=(3 * n, 3 * n), behind=behind)


def _allgather_finish(name, stacks, sems, order):
    def body(bufs, taken, _):
        sends, recvs = _forward_copies(bufs, taken[0], taken[1])
        for cp in sends:
            cp.wait_send()
        for cp in recvs:
            cp.wait_recv()

    return _comm_call(name, body, stacks, order, sems_in=sems)[1]


def _window_unit(q, j):
    return C2I[WIN_UNIT0[q] + j]


def _pair_copies(g, t, ssem, rsem, gathered):
    x, y, c, _ = _place()
    sib = (x, y, 1 - c)
    cps, whole = [], []
    for a in range(len(g)):
        if a == 0 and gathered:
            for q in range(NCHIP):
                for j in range(WIN_UNITS // 2):
                    u = jnp.where(c == 0, _window_unit(q, WIN_UNITS // 2 + j), _window_unit(q, j))
                    src = g[0].at[pl.ds(pl.multiple_of(u * UNIT, UNIT), UNIT), :]
                    cps.append(_rcopy(src, t[0].at[q, pl.ds(j * UNIT, UNIT), :], ssem.at[0], rsem.at[0], sib))
            whole.append(_rcopy(t[0], t[0], ssem.at[0], rsem.at[0], sib))
        else:
            cp = _rcopy(g[a].at[:, _half(g[a].shape[1], 1 - c), :], t[a], ssem.at[a], rsem.at[a], sib)
            cps.append(cp)
            whole.append(cp)
    return cps, whole


def _comm_multi(name, parts, order):
    def body(buf_refs, taken, new):
        ib = it = inew = 0
        for pbody, pbufs, psems, pnew, _ in parts:
            pbody(buf_refs[ib:ib + len(pbufs)], taken[it:it + len(psems)], new[inew:inew + len(pnew)])
            ib, it, inew = ib + len(pbufs), it + len(psems), inew + len(pnew)

    sems, bufs = _comm_call(name, body, [b for p in parts for b in p[1]], order,
                            sems_in=[s for p in parts for s in p[2]], new_sems=[k for p in parts for k in p[3]])
    out, ib, inew = [], 0, 0
    for _, pbufs, _, pnew, unpack in parts:
        out.append(unpack(sems[inew:inew + len(pnew)], bufs[ib:ib + len(pbufs)]))
        ib, inew = ib + len(pbufs), inew + len(pnew)
    return out


def _pair_start_part(gs, gathered=False):
    n = len(gs)
    ts = [lax.empty((NCHIP, WIN_ROWS // 2, D) if (a == 0 and gathered) else (NCHIP, g.shape[1] // 2, g.shape[2]), f32)
          for a, g in enumerate(gs)]

    def body(bufs, _, new):
        for cp in _pair_copies(bufs[:n], bufs[n:], new[0], new[1], gathered)[0]:
            cp.start()

    return body, list(gs) + ts, (), (n, n), lambda sems, bufs: (sems, bufs)


def _pair_wait_part(bufs, sems, gathered=False):
    n = len(bufs) // 2

    def body(refs, taken, _):
        for cp in _pair_copies(refs[:n], refs[n:], taken[0], taken[1], gathered)[1]:
            cp.wait_send()
            cp.wait_recv()

    return body, list(bufs), list(sems), (), lambda _, out: (out[:n], out[n:])


def _row_tile(h):
    return min(h, 256)


def _pair_add(order, g, t, q_arr, c_arr, name):
    _, R, C = g.shape
    h = R // 2
    tr = _row_tile(h)
    nblk = h // tr

    def body(q_ref, c_ref, g_ref, t_ref, own_ref, p16_ref):
        s = g_ref[...] + t_ref[...]
        p16_ref[...] = s.astype(bf16)

        @pl.when(pl.program_id(1) == q_ref[0])
        def _():
            own_ref[...] = s

    blk = pl.BlockSpec((None, tr, C), lambda i, q, q_ref, c_ref: (q, i, 0))
    return _call_indexed(
        order, body, (q_arr, c_arr), (g, t), (nblk, NCHIP),
        [pl.BlockSpec((None, tr, C), lambda i, q, q_ref, c_ref: (q, c_ref[0] * nblk + i, 0)), blk],
        [pl.BlockSpec((tr, C), lambda i, q, q_ref, c_ref: (i, 0)), blk],
        name=name,
        out_shape=[jax.ShapeDtypeStruct((h, C), f32), jax.ShapeDtypeStruct((NCHIP, h, C), bf16)],
        compiler_params=_params(("parallel", "arbitrary")),
    )


def _pair_add_gathered(order, dwt, t, q_arr, c_arr, name):
    half_units, half_rows = WIN_UNITS // 2, WIN_ROWS // 2
    table = jnp.asarray([_window_unit(q, j) for q in range(NCHIP) for j in range(WIN_UNITS)], jnp.int32)

    def body(tab_ref, q_ref, c_ref, g_hbm, t_ref, own_ref, p16_ref, buf, sem):
        q = pl.program_id(0)

        def gather(w, slot):
            cps = []
            for j in range(half_units):
                u = tab_ref[w * WIN_UNITS + c_ref[0] * half_units + j]
                cps.append(pltpu.make_async_copy(g_hbm.at[pl.ds(pl.multiple_of(u * UNIT, UNIT), UNIT), :],
                                                 buf.at[slot, pl.ds(j * UNIT, UNIT), :], sem.at[slot]))
            return cps

        @pl.when(q == 0)
        def _():
            for cp in gather(0, 0):
                cp.start()

        @pl.when(q + 1 < NCHIP)
        def _():
            for cp in gather(q + 1, (q + 1) % 2):
                cp.start()

        slot = q % 2
        pltpu.make_async_copy(buf.at[slot], buf.at[slot], sem.at[slot]).wait()
        s = buf[slot] + t_ref[...]
        p16_ref[...] = s.astype(bf16)

        @pl.when(q == q_ref[0])
        def _():
            own_ref[...] = s

    blk = pl.BlockSpec((None, half_rows, D), lambda q, tab_ref, q_ref, c_ref: (q, 0, 0))
    return _call_indexed(
        order, body, (table, q_arr, c_arr), (dwt, t), (NCHIP,),
        [pl.BlockSpec(memory_space=pl.ANY), blk],
        [pl.BlockSpec((half_rows, D), lambda q, tab_ref, q_ref, c_ref: (0, 0)), blk],
        scratch_shapes=[pltpu.VMEM((2, half_rows, D), f32), pltpu.SemaphoreType.DMA((2,))],
        name=name,
        out_shape=[jax.ShapeDtypeStruct((half_rows, D), f32),
                   jax.ShapeDtypeStruct((NCHIP, half_rows, D), bf16)],
        compiler_params=_params(("arbitrary",)),
    )


def _shard_copies(p, r, sm, ssem, rsem):
    x, y, c, chips = _place()
    n = len(p)
    sends, recvs = [], []
    for a in range(n):
        for j, (cx, cy) in enumerate(chips):
            k = a * 3 + j
            sends.append(_rcopy(p[a].at[2 * cx + cy], r[a].at[j], ssem.at[k], rsem.at[k], (cx, cy, c)))
            recvs.append(_rcopy(r[a].at[j], r[a].at[j], ssem.at[k], rsem.at[k], (cx, cy, c)))
    if sm is not None:
        mine = sm.at[4 * x + 2 * y + c]
        for i in range(1, 8):
            px = (1 - x) if i & 4 else x
            py = (1 - y) if i & 2 else y
            pc = (1 - c) if i & 1 else c
            k = 3 * n + i - 1
            sends.append(_rcopy(mine, mine, ssem.at[k], rsem.at[k], (px, py, pc)))
            slot = sm.at[4 * px + 2 * py + pc]
            recvs.append(_rcopy(slot, slot, ssem.at[k], rsem.at[k], (px, py, pc)))
    return sends, recvs


def _shard_start_part(p16s, sm=None):
    n = len(p16s)
    rs = [lax.empty((3,) + p.shape[1:], bf16) for p in p16s]
    extra = [] if sm is None else [sm]
    nsem = 3 * n + (7 if sm is not None else 0)

    def body(bufs, _, new):
        sends, _r = _shard_copies(bufs[:n], bufs[n:2 * n], bufs[2 * n] if extra else None, new[0], new[1])
        for cp in sends:
            cp.start()

    return body, list(p16s) + rs + extra, (), (nsem, nsem), lambda sems, bufs: (sems, bufs)


def _shard_wait_part(bufs, sems, n):
    has_sm = len(bufs) > 2 * n

    def body(refs, taken, _):
        sends, recvs = _shard_copies(refs[:n], refs[n:2 * n], refs[2 * n] if has_sm else None, taken[0], taken[1])
        for cp in sends:
            cp.wait_send()
        for cp in recvs:
            cp.wait_recv()

    return body, list(bufs), list(sems), (), lambda _, out: (out[n:2 * n], (out[2 * n] if has_sm else None))


def _shard_sum(order, own, r, c_arr, name):
    h, C = own.shape
    tr = _row_tile(h)
    nblk = h // tr

    def body(c_ref, p_ref, r_ref, o_ref):
        s = p_ref[...]
        for j in range(3):
            s = s + r_ref[j].astype(f32)
        o_ref[...] = s

    return _call_indexed(
        order, body, (c_arr,), (own, r), (nblk,),
        [pl.BlockSpec((tr, C), lambda i, c_ref: (i, 0)), pl.BlockSpec((3, tr, C), lambda i, c_ref: (0, i, 0))],
        pl.BlockSpec((tr, C), lambda i, c_ref: (c_ref[0] * nblk + i, 0)),
        name=name, out_shape=jax.ShapeDtypeStruct((2 * h, C), f32),
        compiler_params=_params(("parallel",)),
    )


def _swap_copies(full, ssem, rsem):
    x, y, c, _ = _place()
    sends, recvs = [], []
    for a in range(len(full)):
        mine = full[a].at[_half(full[a].shape[0], c)]
        sends.append(_rcopy(mine, mine, ssem.at[a], rsem.at[a], (x, y, 1 - c)))
        other = full[a].at[_half(full[a].shape[0], 1 - c)]
        recvs.append(_rcopy(other, other, ssem.at[a], rsem.at[a], (x, y, 1 - c)))
    return sends, recvs


def _swap_start_part(fulls):
    n = len(fulls)

    def body(bufs, _, new):
        for cp in _swap_copies(bufs, new[0], new[1])[0]:
            cp.start()

    return body, list(fulls), (), (n, n), lambda sems, bufs: (sems, bufs)


def _swap_wait_part(fulls, sems):
    def body(refs, taken, _):
        sends, recvs = _swap_copies(refs, taken[0], taken[1])
        for cp in sends:
            cp.wait_send()
        for cp in recvs:
            cp.wait_recv()

    return body, list(fulls), list(sems), (), lambda _, out: out


def _small_sum(order, sm):
    def body(sm_ref, o_ref):
        s = sm_ref[0]
        for d in range(1, 8):
            s = s + sm_ref[d]
        o_ref[...] = s

    return _call(order, body, (sm,), name="small_grad_sum", out_shape=jax.ShapeDtypeStruct((SMALL_ROWS, D), f32))


def _adamw_math(w, g, m, v):
    m = ADAM_B1 * m + (1.0 - ADAM_B1) * g
    v = ADAM_B2 * v + (1.0 - ADAM_B2) * (g * g)
    m_hat = m / (1.0 - ADAM_B1 ** ADAM_STEP)
    v_hat = v / (1.0 - ADAM_B2 ** ADAM_STEP)
    return -ADAM_LR * (m_hat / (jnp.sqrt(v_hat) + ADAM_EPS) + ADAM_WD * w), m, v


def _adamw_small(order, ws, gs, ms, vs, name):
    n = len(ws)

    def body(*refs):
        for i in range(n):
            res = _adamw_math(*[refs[k * n + i][...] for k in range(4)])
            for k in range(3):
                refs[4 * n + 3 * i + k][...] = res[k]

    out = _call(order, body, list(ws) + list(gs) + list(ms) + list(vs), name=name,
                out_shape=[jax.ShapeDtypeStruct(w.shape, f32) for w in ws for _ in range(3)])
    return [out[3 * i:3 * i + 3] for i in range(n)]


def _adamw(order, w, g, m, v, name):
    R, C = w.shape
    if R <= 256 or R % 256 == 0:
        tr, tc = min(R, 256), C
    else:
        tr, tc = R, 128

    def body(w_ref, g_ref, m_ref, v_ref, d_ref, nm_ref, nv_ref):
        d_ref[...], nm_ref[...], nv_ref[...] = _adamw_math(w_ref[...], g_ref[...], m_ref[...], v_ref[...])

    blk = pl.BlockSpec((tr, tc), lambda i, j: (i, j))
    return _call(
        order, body, (w, g, m, v), name=name, grid=(R // tr, C // tc), in_specs=[blk] * 4, out_specs=[blk] * 3,
        out_shape=[jax.ShapeDtypeStruct((R, C), f32)] * 3,
        compiler_params=_params(("parallel", "parallel")),
    )


def _feature_major(w):
    return jnp.transpose(w, (2, 0, 1)).reshape(SHARD_IN, D)


def _unfeature_major(a):
    return jnp.transpose(a.reshape(SHARD_IN, 1, D), (1, 2, 0))


def _window_of(wt, q):
    def plain(k):
        return lambda w: jnp.pad(w, ((OWN_ROW0[k], WIN_ROWS - OWN_ROW0[k] - SHARD_IN), (0, 0))).astype(bf16)

    def chip1(w):
        lo = jnp.pad(w[0:FA_AT], ((OWN_ROW0[1], WIN_ROWS - OWN_ROW0[1] - FA_AT), (0, 0)))
        hi = jnp.pad(w[FA_AT + N_FA:SHARD_IN], ((OWN_ROW0[1] + FA_AT, WIN_ROWS - OWN_ROW0[1] - (SHARD_IN - N_FA)), (0, 0)))
        return (lo + hi).astype(bf16)

    win = lax.switch(q, [plain(0), chip1, plain(2), plain(3)], wt)
    fa = jnp.pad(wt[FA_AT:FA_AT + N_FA], ((0, FA_ROWS - N_FA), (0, 0))).astype(bf16)
    return win, fa


def _own_rows(gwin, gfa, q):
    def plain(k):
        return lambda gw, gf: gw[OWN_ROW0[k]:OWN_ROW0[k] + SHARD_IN]

    def chip1(gw, gf):
        o, rest = OWN_ROW0[1], SHARD_IN - FA_AT - N_FA
        return (jnp.pad(gw[o:o + FA_AT], ((0, SHARD_IN - FA_AT), (0, 0)))
                + jnp.pad(gf[0:N_FA], ((FA_AT, rest), (0, 0)))
                + jnp.pad(gw[o + FA_AT:o + FA_AT + rest], ((FA_AT + N_FA, 0), (0, 0))))

    return lax.switch(q, [plain(0), chip1, plain(2), plain(3)], gwin, gfa)


def kernel(x, norm_attn_g, w_in, b_forget, w_branch_a, w_branch_b, w_out, norm_mlp_g, w_up, w_down, norm_final_g, loss_target, m_norm_attn_g, m_w_in, m_b_forget, m_w_branch_a, m_w_branch_b, m_w_out, m_norm_mlp_g, m_w_up, m_w_down, m_norm_final_g, v_norm_attn_g, v_w_in, v_b_forget, v_w_branch_a, v_w_branch_b, v_w_out, v_norm_mlp_g, v_w_up, v_w_down, v_norm_final_g):
    xi, yi, ci = lax.axis_index("x"), lax.axis_index("y"), lax.axis_index("c")
    q_me = 2 * xi + yi
    c_arr = jnp.reshape(ci, (1,)).astype(jnp.int32)
    q_arr = jnp.reshape(q_me, (1,)).astype(jnp.int32)
    x_, tgt = x[0], loss_target[0]

    names = ["w_branch_a", "w_branch_b", "w_out", "w_up", "w_down"]
    big = dict(zip(names, [w_branch_a[0], w_branch_b[0], w_out[0], w_up[0], w_down[0]]))
    ms = dict(zip(names, [m_w_branch_a[0], m_w_branch_b[0], m_w_out[0], m_w_up[0], m_w_down[0]]))
    vs = dict(zip(names, [v_w_branch_a[0], v_w_branch_b[0], v_w_out[0], v_w_up[0], v_w_down[0]]))
    grad, upd = {}, {}
    order = _Order()

    def run(fn, *args, **kw):
        return fn(order, *args, **kw)

    def own_slot(a):
        return lax.dynamic_update_slice(lax.empty((NCHIP,) + a.shape, a.dtype), a[None], (q_me, 0, 0))

    wt_own = _feature_major(w_in)
    win, fa_blk = _window_of(wt_own, q_me)
    sem_in, in_s = _allgather_start("allgather_start_in", [own_slot(win), own_slot(fa_blk)], order)
    sem_rest, rest = _allgather_start("allgather_start_rest", [own_slot(w.astype(bf16)) for w in big.values()], order)
    rope = _rope_tables(order.tok[0, 0])
    mt_own, vt_own = _feature_major(m_w_in), _feature_major(v_w_in)
    sem_f, in_s = _allgather_forward("allgather_forward_in", in_s, sem_in, order,
                                     behind=[wt_own, mt_own, vt_own, *rope])
    wins, fas = _allgather_finish("allgather_finish_in", in_s, sem_f, order)
    wt = run(_assemble_win, wins, fas)

    bpad = jnp.pad(b_forget, ((0, 0), (0, 120)))
    h1, qkvb, qkva, gates, fa = run(_norm_inproj, x_, norm_attn_g, wt, rope)
    F = run(_forget_cumsum, fa, bpad)
    oa, lsea = run(_fox_fwd, qkva, F)
    sem_f, rest = _allgather_forward("allgather_forward_rest", rest, sem_rest, order)
    ob, lseb = run(_dil_fwd, qkvb)
    was, wbs, wouts, wups, wdowns = _allgather_finish("allgather_finish_rest", rest, sem_f, order)
    wout = wouts.reshape(D, D)
    wdown = wdowns.reshape(DFF, D)
    ya, yb, mixed = run(_branch_mix, oa, ob, was, wbs, gates)
    x2, h2 = run(_outproj_norm, mixed, wout, x_, norm_mlp_g)
    u, a = run(_mlp_up, h2, wups)
    dx3, dx3b, dg3, loss_part = run(_mlp_down_loss, a, wdown, x2, norm_final_g.reshape(1, D), tgt)

    def comm(name, *parts):
        return _comm_multi(name, list(parts), order)

    def pair_adds(group, gs, ts):
        return zip(*[run(_pair_add, gs[i], ts[i], q_arr, c_arr, "pair_add_" + nm) for i, nm in enumerate(group)])

    def shard_sums(group, p32s, rs):
        return [run(_shard_sum, p32s[i], rs[i], c_arr, "shard_sum_" + nm) for i, nm in enumerate(group)]

    def adamw_group(group, fulls):
        for nm, gfull in zip(group, fulls):
            grad[nm] = gfull
            upd[nm] = run(_adamw, big[nm], gfull, ms[nm], vs[nm], "adamw_" + nm)

    grp_a, grp_b, grp_c = ["w_down", "w_up"], ["w_out", "w_branch_a", "w_branch_b"], ["w_in", "w_in_fa"]
    du = run(_mlp_down_bwd, dx3b, wdown, u)
    dwdown = run(_mm, a, dx3b, "tn", f32, 1024, D, "wgrad_down")
    dwup = run(_mm, h2, du, "tn", f32, D, 1024, "wgrad_up", stack_cols=True)
    ((sem_pa, buf_pa),) = comm("pair_start_a", _pair_start_part([dwdown.reshape(NCHIP, DFF // NCHIP, D), dwup]))
    dx2, dx2b, dg2 = run(_mlp_up_bwd, du, wups, x2, dx3, norm_mlp_g)
    ((gs, ts),) = comm("pair_wait_a", _pair_wait_part(buf_pa, sem_pa))
    p32_a, p16_a = pair_adds(grp_a, gs, ts)
    ((sem_sa, buf_sa),) = comm("shard_start_a", _shard_start_part(p16_a))
    dya, dyb, dproj = run(_gate_bwd, dx2b, wout, gates, ya, yb)
    dwout = run(_mm, mixed, dx2b, "tn", f32, D, D, "wgrad_out")
    doa, dob = run(_branch_bwd, dya, dyb, was, wbs)
    dwas, dwbs = run(_branch_wgrad, oa, ob, dya, dyb)
    ((sem_pb, buf_pb),) = comm("pair_start_b", _pair_start_part([dwout.reshape(NCHIP, D // NCHIP, D), dwas, dwbs]))
    dF, dproj = run(_fox_bwd, qkva, doa, oa, lsea, F, dproj)
    (gs, ts), (rs_a, _) = comm("pair_wait_b_shard_wait_a", _pair_wait_part(buf_pb, sem_pb),
                               _shard_wait_part(buf_sa, sem_sa, len(grp_a)))
    p32_b, p16_b = pair_adds(grp_b, gs, ts)
    fulls_a = shard_sums(grp_a, p32_a, rs_a)
    (sem_wa, fulls_a), (sem_sb, buf_sb) = comm("swap_start_a_shard_start_b", _swap_start_part(fulls_a),
                                               _shard_start_part(p16_b))
    dbf, dproj = run(_forget_bwd, dF, fa, bpad, dproj)
    dproj = run(_dil_bwd, qkvb, dob, ob, lseb, rope, dproj)
    (rs_b, _), fulls_a = comm("shard_wait_b_swap_wait_a", _shard_wait_part(buf_sb, sem_sb, len(grp_b)),
                              _swap_wait_part(fulls_a, sem_wa))
    fulls_b = shard_sums(grp_b, p32_b, rs_b)
    ((sem_wb, fulls_b),) = comm("swap_start_b", _swap_start_part(fulls_b))
    dwt = run(_mm, dproj, h1, "tn", f32, 512, D, "wgrad_in")
    dwfa = jnp.broadcast_to(dwt[F_FA:F_FA + FA_ROWS][None], (NCHIP, FA_ROWS, D))
    (sem_pc, buf_pc), fulls_b = comm("pair_start_c_swap_wait_b", _pair_start_part([dwt, dwfa], gathered=True),
                                     _swap_wait_part(fulls_b, sem_wb))
    adamw_group(grp_b, fulls_b)
    (((dwt_c, dwfa_c), (t_in, t_fa)),) = comm("pair_wait_c", _pair_wait_part(buf_pc, sem_pc, gathered=True))
    p32_in, p16_in = run(_pair_add_gathered, dwt_c, t_in, q_arr, c_arr, "pair_add_w_in")
    p32_fa, p16_fa = run(_pair_add, dwfa_c, t_fa, q_arr, c_arr, "pair_add_w_in_fa")
    ((sem_sc, buf_sc),) = comm("shard_start_c", _shard_start_part([p16_in, p16_fa]))
    gx, dg1 = run(_inproj_bwd, dproj, wt, x_, dx2, norm_attn_g)
    adamw_group(grp_a, fulls_a)
    small = jnp.concatenate([dg1, dg2, dg3, jnp.pad(dbf[:, 0:8], ((0, 0), (0, D - 8))),
                             jnp.pad(loss_part, ((0, 0), (0, D - 128))),
                             jnp.zeros((SMALL_ROWS - 5, D), f32)], axis=0)
    sm = lax.dynamic_update_slice(lax.empty((8, SMALL_ROWS, D), f32), small[None],
                                  (4 * xi + 2 * yi + ci, 0, 0))
    (sem_sm, buf_sm), (rs_c, _) = comm("small_start_shard_wait_c", _shard_start_part([], sm),
                                       _shard_wait_part(buf_sc, sem_sc, len(grp_c)))
    fulls_c = shard_sums(grp_c, [p32_in, p32_fa], rs_c)
    (sem_wc, fulls_c), (_, sm) = comm("swap_start_c_small_wait", _swap_start_part(fulls_c),
                                      _shard_wait_part(buf_sm, sem_sm, 0))
    gsmall = run(_small_sum, sm)
    loss = gsmall[4, 0]

    grad["norm_attn_g"], grad["norm_mlp_g"] = gsmall[0:1], gsmall[1:2]
    grad["norm_final_g"], grad["b_forget"] = gsmall[2:3], gsmall[3:4, 0:8]
    smalls = ["norm_attn_g", "norm_mlp_g", "norm_final_g", "b_forget"]
    res = run(_adamw_small, [norm_attn_g, norm_mlp_g, norm_final_g.reshape(1, D), b_forget],
              [grad[nm] for nm in smalls],
              [m_norm_attn_g, m_norm_mlp_g, m_norm_final_g.reshape(1, D), m_b_forget],
              [v_norm_attn_g, v_norm_mlp_g, v_norm_final_g.reshape(1, D), v_b_forget], "adamw_small")
    upd.update(zip(smalls, res))

    ((gwin, gfa),) = comm("swap_wait_c", _swap_wait_part(fulls_c, sem_wc))
    g_in = _own_rows(gwin, gfa, q_me)
    upd_in = run(_adamw, wt_own, g_in, mt_own, vt_own, "adamw_w_in")
    grad["w_in"] = _unfeature_major(g_in)
    upd["w_in"] = [_unfeature_major(t) for t in upd_in]

    order_out = ["norm_attn_g", "w_in", "b_forget", "w_branch_a", "w_branch_b", "w_out", "norm_mlp_g", "w_up",
                 "w_down", "norm_final_g"]
    shapes = dict(norm_attn_g=norm_attn_g.shape, w_in=w_in.shape, b_forget=b_forget.shape,
                  w_branch_a=w_branch_a.shape, w_branch_b=w_branch_b.shape, w_out=w_out.shape,
                  norm_mlp_g=norm_mlp_g.shape, w_up=w_up.shape, w_down=w_down.shape, norm_final_g=norm_final_g.shape)
    outs = [loss, gx.reshape(x.shape)]
    outs += [grad[nm].reshape(shapes[nm]) for nm in order_out]
    for k in range(3):
        outs += [upd[nm][k].reshape(shapes[nm]) for nm in order_out]
    return tuple(outs)
```

```python
import jax
import jax.numpy as jnp
from jax import lax
from jax.experimental import pallas as pl
from jax.experimental.pallas import tpu as pltpu

f32 = jnp.float32
bf16 = jnp.bfloat16

S = 2048
D = 1024
DFF = 4096
HD = 64
FOXW = 512
DILOUT = 256
DIL = (1, 4, 16)
BAND = 128
EPS = 1e-6
NEG = -1e30
ROPE_THETA = 500000.0
NCHIP = 4
TQ = 256

ADAM_LR, ADAM_B1, ADAM_B2, ADAM_EPS, ADAM_WD, ADAM_STEP = 0.001, 0.9, 0.999, 1e-08, 0.01, 10
VMEM_LIMIT = 56 * 1024 * 1024

UNIT = 64
NP = 6144
F_DIL, F_FOX, F_FA, F_G = 0, 2304, 3840, 4096
DIL_BLK, FOX_BLK = 1152, 384
WIN_UNITS, WIN_ROWS = 24, 1536
WIN_UNIT0 = (0, 23, 45, 68)
OWN_ROW0 = (0, 2, 60, 62)
SHARD_IN = 1474
N_FA = 8
FA_AT = 1536 - SHARD_IN
FA_ROWS = 32


def _compact_to_internal():
    c2i = {}
    for p in range(2):
        for role in range(3):
            for g in range(3):
                for hh in range(2):
                    c2i[24 + 12 * role + 4 * g + 2 * p + hh] = 18 * p + 6 * role + 2 * g + hh
    for p in range(4):
        for role in range(3):
            for hh in range(2):
                c2i[8 * role + 2 * p + hh] = F_FOX // UNIT + 6 * p + 2 * role + hh
    for j in range(32):
        c2i[60 + j] = F_G // UNIT + j
    return c2i


C2I = _compact_to_internal()
OVERLAP_UNITS = (23, 45, 46, 68)


def _params(sem=None):
    return pltpu.CompilerParams(dimension_semantics=sem, vmem_limit_bytes=VMEM_LIMIT)


class _Order:
    def __init__(self):
        self.tok = None

    def mark(self, v):
        self.tok = v

    def token_for(self, args):
        return [] if self.tok is None or any(self.tok is a for a in args) else [self.tok]


def _call(order, body, args, in_specs=None, **kw):
    args = list(args)
    n_in = len(args)
    if in_specs is None:
        in_specs = [pl.BlockSpec(memory_space=pltpu.VMEM)] * n_in
    kern = body
    extra = order.token_for(args)
    if extra:
        in_specs = list(in_specs) + [pl.BlockSpec(memory_space=pl.ANY)]

        def kern(*refs):
            body(*refs[:n_in], *refs[n_in + 1:])

    out = pl.pallas_call(kern, in_specs=in_specs, **kw)(*args, *extra)
    order.mark(out[0] if isinstance(out, (tuple, list)) else out)
    return out


def _call_indexed(order, body, scalars, args, grid, in_specs, out_specs, scratch_shapes=(), **kw):
    args, in_specs = list(args), list(in_specs)
    n_front = len(scalars) + len(args)
    kern = body
    extra = order.token_for(args)
    if extra:
        in_specs.append(pl.BlockSpec(memory_space=pl.ANY))

        def kern(*refs):
            body(*refs[:n_front], *refs[n_front + 1:])

    out = pl.pallas_call(
        kern, grid_spec=pltpu.PrefetchScalarGridSpec(num_scalar_prefetch=len(scalars), grid=grid, in_specs=in_specs,
                                                     out_specs=out_specs, scratch_shapes=scratch_shapes),
        **kw)(*scalars, *args, *extra)
    order.mark(out[0] if isinstance(out, (tuple, list)) else out)
    return out


def _dot(a, b):
    return jnp.dot(a, b, preferred_element_type=f32)


def _dot_nt(a, b):
    return lax.dot_general(a, b, (((1,), (1,)), ((), ())), preferred_element_type=f32)


def _dot_tn(a, b):
    return lax.dot_general(a, b, (((0,), (0,)), ((), ())), preferred_element_type=f32)


def _split3(x):
    hi = x.astype(bf16)
    r1 = x - hi.astype(f32)
    mid = r1.astype(bf16)
    lo = (r1 - mid.astype(f32)).astype(bf16)
    return hi, mid, lo


def _rope_tables(after):
    half = 8
    inv_freq = jnp.power(jnp.float32(ROPE_THETA), -jnp.arange(half, dtype=f32) * 2.0 / 16)
    ang = (jnp.arange(S).astype(f32) + after)[:, None] * inv_freq[None, :]
    cos, sin = jnp.cos(ang), jnp.sin(ang)
    one = jnp.ones((S, HD - 16), f32)
    zero = jnp.zeros((S, HD - 16), f32)
    z8 = jnp.zeros((S, 8), f32)
    c = jnp.concatenate([cos, cos, one], axis=1)
    s1 = jnp.concatenate([-sin, z8, zero], axis=1)
    s2 = jnp.concatenate([z8, sin, zero], axis=1)
    return tuple(jnp.concatenate([t, t], axis=1) for t in (c, s1, s2))


def _mm(order, a, b, mode, out_dtype, tm, tn, name, stack_cols=False):
    if mode == "nn":
        (M, K), (_, N) = a.shape, b.shape
        a_spec = pl.BlockSpec((tm, K), lambda i, j: (i, 0))
        b_spec = pl.BlockSpec((K, tn), lambda i, j: (0, j))
        dot = _dot
    elif mode == "nt":
        (M, K), (N, _) = a.shape, b.shape
        a_spec = pl.BlockSpec((tm, K), lambda i, j: (i, 0))
        b_spec = pl.BlockSpec((tn, K), lambda i, j: (j, 0))
        dot = _dot_nt
    else:
        (K, M), (_, N) = a.shape, b.shape
        a_spec = pl.BlockSpec((K, tm), lambda i, j: (0, i))
        b_spec = pl.BlockSpec((K, tn), lambda i, j: (0, j))
        dot = _dot_tn

    def body(a_ref, b_ref, o_ref):
        o_ref[...] = dot(a_ref[...], b_ref[...]).astype(out_dtype)

    if stack_cols:
        assert tm == M
        out_spec = pl.BlockSpec((None, tm, tn), lambda i, j: (j, 0, 0))
        out_shape = jax.ShapeDtypeStruct((N // tn, M, tn), out_dtype)
    else:
        out_spec = pl.BlockSpec((tm, tn), lambda i, j: (i, j))
        out_shape = jax.ShapeDtypeStruct((M, N), out_dtype)
    return _call(
        order, body, (a, b), name=name, grid=(M // tm, N // tn), in_specs=[a_spec, b_spec],
        out_specs=out_spec, out_shape=out_shape,
        compiler_params=_params(("parallel", "parallel")),
    )


def _assemble_win(order, wins, fas):
    def body(win_ref, fa_ref, o_ref):
        q = pl.program_id(0)

        @pl.when(q == 0)
        def _():
            o_ref[...] = jnp.zeros_like(o_ref)

        for k in range(NCHIP):
            @pl.when(q == k)
            def _(k=k):
                for j in range(WIN_UNITS):
                    cu = WIN_UNIT0[k] + j
                    dst = pl.ds(C2I[cu] * UNIT, UNIT)
                    if cu in OVERLAP_UNITS:
                        o_ref[dst, :] += win_ref[j * UNIT:(j + 1) * UNIT, :]
                    else:
                        o_ref[dst, :] = win_ref[j * UNIT:(j + 1) * UNIT, :]
                if k == 1:
                    o_ref[F_FA:F_FA + FA_ROWS, :] = fa_ref[...]

    return _call(
        order, body, (wins, fas), name="assemble_w_in", grid=(NCHIP,),
        in_specs=[pl.BlockSpec((None, WIN_ROWS, D), lambda q: (q, 0, 0)),
                  pl.BlockSpec((None, FA_ROWS, D), lambda q: (1, 0, 0))],
        out_specs=pl.BlockSpec((NP, D), lambda q: (0, 0)),
        out_shape=jax.ShapeDtypeStruct((NP, D), bf16),
        compiler_params=_params(("arbitrary",)),
    )


def _norm_inproj(order, x, g1, wt, rope):
    tm = 256
    c_t, s1_t, s2_t = rope

    def body(x_ref, g_ref, w_ref, c_ref, s1_ref, s2_ref, h_ref, qkvb_ref, qkva_ref, gates_ref, fa_ref):
        xb = x_ref[...]
        r = lax.rsqrt(jnp.mean(xb * xb, axis=-1, keepdims=True) + EPS)
        h = ((xb * r) * g_ref[...]).astype(bf16)
        h_ref[...] = h
        c, s1, s2 = c_ref[...], s1_ref[...], s2_ref[...]
        for p in range(2):
            pb = _dot_nt(h, w_ref[F_DIL + p * DIL_BLK:F_DIL + (p + 1) * DIL_BLK, :])
            for ch in range(DIL_BLK // 128):
                pc = pb[:, ch * 128:(ch + 1) * 128]
                if ch < 6:
                    pc = pc * c + pltpu.roll(pc, 120, 1) * s1 + pltpu.roll(pc, 8, 1) * s2
                qkvb_ref[:, p * DIL_BLK + ch * 128:p * DIL_BLK + (ch + 1) * 128] = pc
        qkva_ref[...] = _dot_nt(h, w_ref[F_FOX:F_FA, :]).astype(bf16)
        fa_ref[...] = _dot_nt(h, w_ref[F_FA:F_FA + 128, :])
        gates_ref[...] = _dot_nt(h, w_ref[F_G:NP, :]).astype(bf16)

    row = lambda w: pl.BlockSpec((tm, w), lambda i: (i, 0))
    return _call(
        order, body, (x, g1, wt, c_t, s1_t, s2_t), name="norm_inproj", grid=(S // tm,),
        in_specs=[row(D), pl.BlockSpec((1, D), lambda i: (0, 0)), pl.BlockSpec((NP, D), lambda i: (0, 0)),
                  row(128), row(128), row(128)],
        out_specs=[row(D), row(2 * DIL_BLK), row(4 * FOX_BLK), row(2 * D), row(128)],
        out_shape=[jax.ShapeDtypeStruct((S, D), bf16), jax.ShapeDtypeStruct((S, 2 * DIL_BLK), f32),
                   jax.ShapeDtypeStruct((S, 4 * FOX_BLK), bf16), jax.ShapeDtypeStruct((S, 2 * D), bf16),
                   jax.ShapeDtypeStruct((S, 128), f32)],
        compiler_params=_params(("parallel",)),
    )


def _forget_cumsum(order, fa, bpad):
    nb = S // TQ

    def body(fa_ref, b_ref, F_ref):
        rr = lax.broadcasted_iota(jnp.int32, (TQ, TQ), 0)
        cc = lax.broadcasted_iota(jnp.int32, (TQ, TQ), 1)
        tri = (rr >= cc).astype(bf16)
        lane = lax.broadcasted_iota(jnp.int32, (1, 128), 1)
        carry = jnp.zeros((1, 128), f32)
        for b in range(nb):
            z = fa_ref[b * TQ:(b + 1) * TQ, :] + b_ref[...]
            lf = jnp.minimum(z, 0.0) - jnp.log(1.0 + jnp.exp(-jnp.abs(z)))
            lf = jnp.where(lane < 8, lf, 0.0)
            hi, mid, lo = _split3(lf)
            fb = (_dot(tri, hi) + _dot(tri, mid)) + _dot(tri, lo) + carry
            F_ref[b * TQ:(b + 1) * TQ, :] = fb
            carry = fb[TQ - 1:TQ, :]

    return _call(
        order, body, (fa, bpad), name="forget_cumsum",
        out_shape=jax.ShapeDtypeStruct((S, 128), f32),
        compiler_params=_params(),
    )


def _head_masks():
    lane = lax.broadcasted_iota(jnp.int32, (1, 128), 1)
    return lane, (lane < HD, lane >= HD)


L_ONE = 3
FOX_TQ, FOX_TK = 256, 512


def _set_lanes(x, lane, first, cols):
    for n, col in enumerate(cols):
        x = jnp.where(lane == first + n, col, x)
    return x


def _f32_parts(col):
    return [t.astype(f32) for t in _split3(col)]


def _fox_operands(qkv_ref, F_ref, lse_ref, qa, ka, p, rows):
    lane, hm = _head_masks()
    q = qkv_ref[rows, 0:128].astype(f32) * 0.125
    k = qkv_ref[rows, 128:256].astype(f32)
    Fb = F_ref[rows, :]
    for hh in (0, 1):
        free = (1 - hh) * HD
        fcol = jnp.sum(jnp.where(lane == 2 * p + hh, Fb, 0.0), axis=1, keepdims=True)
        qterm = fcol if lse_ref is None else fcol - lse_ref[rows, hh * HD:hh * HD + 1]
        qcols = _f32_parts(qterm) + [1.0] * 3
        kcols = [1.0] * 3 + [-t for t in _f32_parts(fcol)]
        qa[hh, rows, :] = _set_lanes(jnp.where(hm[hh], q, 0.0), lane, free, qcols).astype(bf16)
        ka[hh, rows, :] = _set_lanes(k, lane, free, kcols).astype(bf16)


def _fox_fwd(order, qkva, F):
    tq, tk = FOX_TQ, FOX_TK

    def body(qkv_ref, F_ref, o_ref, lse_ref, qa, ka, vt):
        p = pl.program_id(0)
        keyi = lax.broadcasted_iota(jnp.int32, (tk, 1), 0)
        qryi = lax.broadcasted_iota(jnp.int32, (1, tq), 1)
        sub = lax.broadcasted_iota(jnp.int32, (128, 1), 0)

        def prep(i, c):
            rows = pl.ds(pl.multiple_of(i * tk, tk), tk)
            _fox_operands(qkv_ref, F_ref, None, qa, ka, p, rows)
            vt[i] = qkv_ref[rows, 256:384].astype(f32).T.astype(bf16)
            return c

        lax.fori_loop(0, S // tk, prep, 0)

        def qblock(i, first_half):
            r0 = pl.multiple_of(i * tq, tq)
            qh = [qa[hh, pl.ds(r0, tq), :] for hh in (0, 1)]

            def kv(jb, carry, masked, width):
                keys = pl.ds(pl.multiple_of(jb * tk, tk), width)
                sts = [_dot_nt(ka[hh, keys, :], qh[hh]) for hh in (0, 1)]
                new = []
                for hh in (0, 1):
                    m, l, a = carry[3 * hh:3 * hh + 3]
                    st = sts[hh]
                    if masked:
                        st = jnp.where(jb * tk + keyi[0:width] <= r0 + qryi, st, NEG)
                    mn = jnp.maximum(m, jnp.max(st, axis=0, keepdims=True))
                    al = jnp.exp(m - mn)
                    pt = jnp.exp(st - mn)
                    l = al * l + jnp.sum(pt, axis=0, keepdims=True)
                    a = al * a + _dot(vt[jb, hh * HD:(hh + 1) * HD, 0:width], pt.astype(bf16))
                    new += [mn, l, a]
                return tuple(new)

            init = (jnp.full((1, tq), NEG, f32), jnp.zeros((1, tq), f32), jnp.zeros((HD, tq), f32)) * 2
            last = (r0 + tq - 1) // tk
            carry = lax.fori_loop(0, last, lambda j, cr: kv(j, cr, False, tk), init)
            m0, l0, a0, m1, l1, a1 = kv(last, carry, True, tk // 2 if first_half else tk)
            ot = jnp.concatenate([a0 / l0, a1 / l1], axis=0)
            lt = jnp.where(sub < HD, m0 + jnp.log(l0), m1 + jnp.log(l1))
            o_ref[pl.ds(r0, tq), :] = ot.T.astype(bf16)
            lse_ref[pl.ds(r0, tq), :] = lt.T

        def qpair(t, c):
            qblock(2 * t, True)
            qblock(2 * t + 1, False)
            return c

        assert tk == 2 * tq
        lax.fori_loop(0, S // tk, qpair, 0)

    pair = pl.BlockSpec((S, 128), lambda p: (0, p))
    return _call(
        order, body, (qkva, F), name="fox_fwd", grid=(4,),
        in_specs=[pl.BlockSpec((S, FOX_BLK), lambda p: (0, p)), pl.BlockSpec((S, 128), lambda p: (0, 0))],
        out_specs=[pair, pair],
        out_shape=[jax.ShapeDtypeStruct((S, FOXW), bf16), jax.ShapeDtypeStruct((S, FOXW), f32)],
        scratch_shapes=[pltpu.VMEM((2, S, 128), bf16)] * 2 + [pltpu.VMEM((S // tk, 128, tk), bf16)],
        compiler_params=_params(("parallel",)),
    )


def _permute_in(dst, src, r):
    L = S // r
    for rho in range(r):
        dst[rho * L:(rho + 1) * L, :] = src[pl.ds(rho, L, stride=r), :]


def _permute_out(dst, src, r):
    L = S // r
    for rho in range(r):
        dst[pl.ds(rho, L, stride=r), :] = src[rho * L:(rho + 1) * L, :]


def _band_width(nbl):
    return BAND if nbl == 1 else 2 * BAND


def _band_geometry(bb, nbl):
    r0 = pl.multiple_of(bb * BAND, BAND)
    if nbl == 1:
        k0 = r0
    else:
        k0 = pl.multiple_of(jnp.maximum(bb - 1, 0) * BAND, BAND)
    sub0 = (bb - lax.rem(bb, nbl)) * BAND
    qi = r0 + lax.broadcasted_iota(jnp.int32, (BAND, 1), 0)
    ki = k0 + lax.broadcasted_iota(jnp.int32, (1, _band_width(nbl)), 1)
    diff = qi - ki
    valid = (diff >= 0) & (diff <= BAND) & (ki >= sub0)
    return r0, k0, valid


def _dil_views(ref):
    return [[ref.at[:, pl.ds((3 * role + g) * 128, 128)] for g in range(3)] for role in range(3)]


DIL_UNROLL = 4


def _dil_in_specs():
    return [pl.BlockSpec((S, 128), lambda p, k=k: (0, 9 * p + k)) for k in range(9)]


def _dil_fwd(order, qkvb):
    def body(*refs):
        q_refs, k_refs, v_refs = refs[0:3], refs[3:6], refs[6:9]
        ob_ref, lse_ref, qp, kp, vp, op, lp = refs[9:16]
        on, ln = refs[16:19], refs[19:22]
        _, hm = _head_masks()
        for g, r in enumerate(DIL):
            nbl = S // r // BAND
            if r == 1:
                qs_, ks_, vs_, od, ld = q_refs[g], k_refs[g], v_refs[g], on[g], ln[g]
            else:
                _permute_in(qp, q_refs[g], r)
                _permute_in(kp, k_refs[g], r)
                _permute_in(vp, v_refs[g], r)
                qs_, ks_, vs_, od, ld = qp, kp, vp, op, lp

            def blk(t, c, qs_=qs_, ks_=ks_, vs_=vs_, od=od, ld=ld, nbl=nbl):
                work = []
                for u in range(DIL_UNROLL):
                    r0, k0, valid = _band_geometry(DIL_UNROLL * t + u, nbl)
                    q = qs_[pl.ds(r0, BAND), :] * 0.125
                    kw = ks_[pl.ds(k0, _band_width(nbl)), :].astype(bf16)
                    vw = vs_[pl.ds(k0, _band_width(nbl)), :]
                    for hh in (0, 1):
                        qh = jnp.where(hm[hh], q, 0.0).astype(bf16)
                        work.append((u, hh, r0, valid, vw, _dot_nt(qh, kw)))
                o = [jnp.zeros((BAND, 128), f32)] * DIL_UNROLL
                lse = [jnp.zeros((BAND, 128), f32)] * DIL_UNROLL
                for u, hh, r0, valid, vw, s in work:
                    s = jnp.where(valid, s, NEG)
                    m = jnp.max(s, axis=1, keepdims=True)
                    pr = jnp.exp(s - m)
                    l = jnp.sum(pr, axis=1, keepdims=True)
                    vm = jnp.where(hm[hh], vw, 0.0).astype(bf16)
                    o[u] = o[u] + _dot((pr / l).astype(bf16), vm)
                    lse[u] = jnp.where(hm[hh], m + jnp.log(l), lse[u])
                    if hh == 1:
                        od[pl.ds(r0, BAND), :] = o[u]
                        ld[pl.ds(r0, BAND), :] = lse[u]
                return c

            lax.fori_loop(0, S // BAND // DIL_UNROLL, blk, 0)
            if r != 1:
                _permute_out(on[g], op, r)
                _permute_out(ln[g], lp, r)

        def combine(i, c):
            r0 = pl.multiple_of(i * TQ, TQ)
            ls = [ln[g][pl.ds(r0, TQ), :] for g in range(3)]
            mx = jnp.maximum(jnp.maximum(ls[0], ls[1]), ls[2])
            es = [jnp.exp(l - mx) for l in ls]
            tot = (es[0] + es[1]) + es[2]
            acc = (es[0] / tot) * on[0][pl.ds(r0, TQ), :]
            acc = acc + (es[1] / tot) * on[1][pl.ds(r0, TQ), :]
            acc = acc + (es[2] / tot) * on[2][pl.ds(r0, TQ), :]
            ob_ref[pl.ds(r0, TQ), :] = acc.astype(bf16)
            lse_ref[pl.ds(r0, TQ), :] = mx + jnp.log(tot)
            return c

        lax.fori_loop(0, S // TQ, combine, 0)

    out_blk = pl.BlockSpec((S, 128), lambda p: (0, p))
    return _call(
        order, body, [qkvb] * 9, name="dil_fwd", grid=(2,),
        in_specs=_dil_in_specs(), out_specs=[out_blk, out_blk],
        out_shape=[jax.ShapeDtypeStruct((S, DILOUT), bf16), jax.ShapeDtypeStruct((S, DILOUT), f32)],
        scratch_shapes=[pltpu.VMEM((S, 128), f32)] * 11,
        compiler_params=_params(("parallel",)),
    )


def _branch_mix(order, oa, ob, was, wbs, gates):
    tm = 512

    def body(oa_ref, ob_ref, wa_ref, wb_ref, g_ref, ya_ref, yb_ref, mix_ref):
        oa_b, ob_b = oa_ref[...], ob_ref[...]
        for q in range(NCHIP):
            cols = slice(q * 256, (q + 1) * 256)
            ya = _dot(oa_b, wa_ref[q])
            yb = _dot(ob_b, wb_ref[q])
            ya_ref[:, cols] = ya.astype(bf16)
            yb_ref[:, cols] = yb.astype(bf16)
            ga = g_ref[:, q * 256:(q + 1) * 256].astype(f32)
            gb = g_ref[:, D + q * 256:D + (q + 1) * 256].astype(f32)
            mix_ref[:, cols] = (jax.nn.sigmoid(ga) * ya + jax.nn.sigmoid(gb) * yb).astype(bf16)

    row = lambda w: pl.BlockSpec((tm, w), lambda i: (i, 0))
    full3 = lambda a: pl.BlockSpec(a.shape, lambda i: (0, 0, 0))
    return _call(
        order, body, (oa, ob, was, wbs, gates), name="branch_mix", grid=(S // tm,),
        in_specs=[row(FOXW), row(DILOUT), full3(was), full3(wbs), row(2 * D)],
        out_specs=[row(D), row(D), row(D)],
        out_shape=[jax.ShapeDtypeStruct((S, D), bf16), jax.ShapeDtypeStruct((S, D), bf16),
                   jax.ShapeDtypeStruct((S, D), bf16)],
        compiler_params=_params(("parallel",)),
    )


def _outproj_norm(order, mixed, wout, x, g2):
    tm = 512

    def body(m_ref, w_ref, x_ref, g_ref, x2_ref, h2_ref):
        x2 = x_ref[...] + _dot(m_ref[...], w_ref[...])
        x2_ref[...] = x2
        r = lax.rsqrt(jnp.mean(x2 * x2, axis=-1, keepdims=True) + EPS)
        h2_ref[...] = ((x2 * r) * g_ref[...]).astype(bf16)

    row = pl.BlockSpec((tm, D), lambda i: (i, 0))
    return _call(
        order, body, (mixed, wout, x, g2), name="outproj_norm", grid=(S // tm,),
        in_specs=[row, pl.BlockSpec((D, D), lambda i: (0, 0)), row, pl.BlockSpec((1, D), lambda i: (0, 0))],
        out_specs=[row, row],
        out_shape=[jax.ShapeDtypeStruct((S, D), f32), jax.ShapeDtypeStruct((S, D), bf16)],
        compiler_params=_params(("parallel",)),
    )


def _mlp_up(order, h2, wups):
    tm = 1024

    def body(h_ref, w_ref, ru_ref, a_ref):
        ru = jnp.maximum(_dot(h_ref[...], w_ref[...]), 0.0)
        ru_ref[...] = ru.astype(bf16)
        a_ref[...] = (ru * ru).astype(bf16)

    out = pl.BlockSpec((tm, D), lambda q, i: (i, q))
    return _call(
        order, body, (h2, wups), name="mlp_up", grid=(NCHIP, S // tm),
        in_specs=[pl.BlockSpec((tm, D), lambda q, i: (i, 0)), pl.BlockSpec((None, D, D), lambda q, i: (q, 0, 0))],
        out_specs=[out, out],
        out_shape=[jax.ShapeDtypeStruct((S, DFF), bf16), jax.ShapeDtypeStruct((S, DFF), bf16)],
        compiler_params=_params(("parallel", "parallel")),
    )


def _mlp_down_loss(order, a, wdown, x2, g3, tgt):
    tm = 512

    def body(a_ref, w_ref, x2_ref, g_ref, t_ref, dx_ref, dxb_ref, dg_ref, loss_ref):
        i = pl.program_id(0)
        x3 = x2_ref[...] + _dot(a_ref[...], w_ref[...])
        r = lax.rsqrt(jnp.mean(x3 * x3, axis=-1, keepdims=True) + EPS)
        xh = x3 * r
        g = g_ref[...]
        e = xh * g - t_ref[...]
        part = 0.5 * jnp.sum(jnp.mean(e * e, axis=-1, keepdims=True), axis=0, keepdims=True)
        dy = e * (1.0 / D)
        gdy = dy * g
        dx = r * (gdy - xh * jnp.mean(gdy * xh, axis=-1, keepdims=True))
        dx_ref[...] = dx
        dxb_ref[...] = dx.astype(bf16)

        @pl.when(i == 0)
        def _():
            dg_ref[...] = jnp.zeros_like(dg_ref)
            loss_ref[...] = jnp.zeros_like(loss_ref)

        dg_ref[...] += jnp.sum(dy * xh, axis=0, keepdims=True)
        loss_ref[...] += jnp.broadcast_to(part, (1, 128))

    row = pl.BlockSpec((tm, D), lambda i: (i, 0))
    vec = pl.BlockSpec((1, D), lambda i: (0, 0))
    return _call(
        order, body, (a, wdown, x2, g3, tgt), name="mlp_down_loss", grid=(S // tm,),
        in_specs=[pl.BlockSpec((tm, DFF), lambda i: (i, 0)), pl.BlockSpec((DFF, D), lambda i: (0, 0)), row, vec, row],
        out_specs=[row, row, vec, pl.BlockSpec((1, 128), lambda i: (0, 0))],
        out_shape=[jax.ShapeDtypeStruct((S, D), f32), jax.ShapeDtypeStruct((S, D), bf16),
                   jax.ShapeDtypeStruct((1, D), f32), jax.ShapeDtypeStruct((1, 128), f32)],
        compiler_params=_params(("arbitrary",)),
    )


def _mlp_down_bwd(order, dx3b, wdown, u):
    tm = 512

    def body(d_ref, w_ref, u_ref, du_ref):
        d = d_ref[...]
        for q in range(NCHIP):
            cols = slice(q * D, (q + 1) * D)
            da = _dot_nt(d, w_ref[cols, :])
            du_ref[:, cols] = (da * (2.0 * u_ref[:, cols].astype(f32))).astype(bf16)

    return _call(
        order, body, (dx3b, wdown, u), name="mlp_down_bwd", grid=(S // tm,),
        in_specs=[pl.BlockSpec((tm, D), lambda i: (i, 0)), pl.BlockSpec((DFF, D), lambda i: (0, 0)),
                  pl.BlockSpec((tm, DFF), lambda i: (i, 0))],
        out_specs=pl.BlockSpec((tm, DFF), lambda i: (i, 0)),
        out_shape=jax.ShapeDtypeStruct((S, DFF), bf16),
        compiler_params=_params(("parallel",)),
    )


def _mlp_up_bwd(order, du, wups, x2, dx3, g2):
    tm = 512

    def body(du_ref, w_ref, x2_ref, dx3_ref, g_ref, dx2_ref, dx2b_ref, dg_ref):
        i = pl.program_id(0)
        dh = jnp.zeros((tm, D), f32)
        for q in range(NCHIP):
            dh = dh + _dot_nt(du_ref[:, q * D:(q + 1) * D], w_ref[q])
        x2 = x2_ref[...]
        r = lax.rsqrt(jnp.mean(x2 * x2, axis=-1, keepdims=True) + EPS)
        xh = x2 * r
        gdh = dh * g_ref[...]
        dx2 = dx3_ref[...] + r * (gdh - xh * jnp.mean(gdh * xh, axis=-1, keepdims=True))
        dx2_ref[...] = dx2
        dx2b_ref[...] = dx2.astype(bf16)

        @pl.when(i == 0)
        def _():
            dg_ref[...] = jnp.zeros_like(dg_ref)

        dg_ref[...] += jnp.sum(dh * xh, axis=0, keepdims=True)

    row = pl.BlockSpec((tm, D), lambda i: (i, 0))
    vec = pl.BlockSpec((1, D), lambda i: (0, 0))
    return _call(
        order, body, (du, wups, x2, dx3, g2), name="mlp_up_bwd", grid=(S // tm,),
        in_specs=[pl.BlockSpec((tm, DFF), lambda i: (i, 0)), pl.BlockSpec((NCHIP, D, D), lambda i: (0, 0, 0)),
                  row, row, vec],
        out_specs=[row, row, vec],
        out_shape=[jax.ShapeDtypeStruct((S, D), f32), jax.ShapeDtypeStruct((S, D), bf16),
                   jax.ShapeDtypeStruct((1, D), f32)],
        compiler_params=_params(("arbitrary",)),
    )


def _gate_bwd(order, dx2b, wout, gates, ya, yb):
    tm = 512

    def body(d_ref, w_ref, g_ref, ya_ref, yb_ref, dya_ref, dyb_ref, dproj_ref):
        dm = _dot_nt(d_ref[...], w_ref[...])
        sa = jax.nn.sigmoid(g_ref[:, 0:D].astype(f32))
        sb = jax.nn.sigmoid(g_ref[:, D:2 * D].astype(f32))
        dya_ref[...] = (dm * sa).astype(bf16)
        dyb_ref[...] = (dm * sb).astype(bf16)
        dproj_ref[:, 0:D] = (dm * ya_ref[...].astype(f32) * (sa * (1.0 - sa))).astype(bf16)
        dproj_ref[:, D:2 * D] = (dm * yb_ref[...].astype(f32) * (sb * (1.0 - sb))).astype(bf16)

    row = lambda w: pl.BlockSpec((tm, w), lambda i: (i, 0))
    return _call(
        order, body, (dx2b, wout, gates, ya, yb), name="gate_bwd", grid=(S // tm,),
        in_specs=[row(D), pl.BlockSpec((D, D), lambda i: (0, 0)), row(2 * D), row(D), row(D)],
        out_specs=[row(D), row(D), pl.BlockSpec((tm, 2 * D), lambda i: (i, F_G // (2 * D)))],
        out_shape=[jax.ShapeDtypeStruct((S, D), bf16), jax.ShapeDtypeStruct((S, D), bf16),
                   jax.ShapeDtypeStruct((S, NP), bf16)],
        compiler_params=_params(("parallel",)),
    )


def _branch_bwd(order, dya, dyb, was, wbs):
    tm = 512

    def body(dya_ref, dyb_ref, wa_ref, wb_ref, doa_ref, dob_ref):
        doa = jnp.zeros((tm, FOXW), f32)
        dob = jnp.zeros((tm, DILOUT), f32)
        for q in range(NCHIP):
            cols = slice(q * 256, (q + 1) * 256)
            doa = doa + _dot_nt(dya_ref[:, cols], wa_ref[q])
            dob = dob + _dot_nt(dyb_ref[:, cols], wb_ref[q])
        doa_ref[...] = doa.astype(bf16)
        dob_ref[...] = dob

    row = lambda w: pl.BlockSpec((tm, w), lambda i: (i, 0))
    full3 = lambda a: pl.BlockSpec(a.shape, lambda i: (0, 0, 0))
    return _call(
        order, body, (dya, dyb, was, wbs), name="branch_bwd", grid=(S // tm,),
        in_specs=[row(D), row(D), full3(was), full3(wbs)],
        out_specs=[row(FOXW), row(DILOUT)],
        out_shape=[jax.ShapeDtypeStruct((S, FOXW), bf16), jax.ShapeDtypeStruct((S, DILOUT), f32)],
        compiler_params=_params(("parallel",)),
    )


def _branch_wgrad(order, oa, ob, dya, dyb):
    def body(oa_ref, ob_ref, dya_ref, dyb_ref, dwa_ref, dwb_ref):
        dwa_ref[...] = _dot_tn(oa_ref[...], dya_ref[...])
        dwb_ref[...] = _dot_tn(ob_ref[...], dyb_ref[...])

    full = lambda w: pl.BlockSpec((S, w), lambda q: (0, 0))
    colq = pl.BlockSpec((S, 256), lambda q: (0, q))
    return _call(
        order, body, (oa, ob, dya, dyb), name="branch_wgrad", grid=(NCHIP,),
        in_specs=[full(FOXW), full(DILOUT), colq, colq],
        out_specs=[pl.BlockSpec((None, FOXW, 256), lambda q: (q, 0, 0)),
                   pl.BlockSpec((None, DILOUT, 256), lambda q: (q, 0, 0))],
        out_shape=[jax.ShapeDtypeStruct((NCHIP, FOXW, 256), f32), jax.ShapeDtypeStruct((NCHIP, DILOUT, 256), f32)],
        compiler_params=_params(("parallel",)),
    )


def _fox_bwd(order, qkva, doa, oa, lse, F, dproj):
    tq, tk = FOX_TQ, FOX_TK

    def body(qkv_ref, do_ref, o_ref, lse_ref, F_ref, _dproj_in, dF_ref, dqkv_ref, qa, ka, da, va, kat,
             dk_scr, dv_scr, dqt_scr):
        p = pl.program_id(0)
        lane, hm = _head_masks()
        keyi = lax.broadcasted_iota(jnp.int32, (tk, 1), 0)
        qryi = lax.broadcasted_iota(jnp.int32, (1, tq), 1)

        def prep(i, c):
            rows = pl.ds(pl.multiple_of(i * tk, tk), tk)
            _fox_operands(qkv_ref, F_ref, lse_ref, qa, ka, p, rows)
            do = do_ref[rows, :].astype(f32)
            prod = do * o_ref[rows, :].astype(f32)
            v = qkv_ref[rows, 256:384].astype(f32)
            for hh in (0, 1):
                free = (1 - hh) * HD
                delta = jnp.sum(jnp.where(hm[hh], prod, 0.0), axis=1, keepdims=True)
                da[hh, rows, :] = _set_lanes(jnp.where(hm[hh], do, 0.0), lane, free,
                                             [-t for t in _f32_parts(delta)]).astype(bf16)
                va[hh, rows, :] = _set_lanes(v, lane, free, [1.0] * 3).astype(bf16)
                kat[hh, i] = ka[hh, rows, :].astype(f32).T.astype(bf16)
                dk_scr[hh, rows, :] = jnp.zeros((tk, 128), f32)
                dv_scr[hh, rows, :] = jnp.zeros((tk, 128), f32)
            return c

        lax.fori_loop(0, S // tk, prep, 0)

        def qblock(i, first_half):
            r0 = pl.multiple_of(i * tq, tq)
            qrows = pl.ds(r0, tq)
            qh = [qa[hh, qrows, :] for hh in (0, 1)]
            dh = [da[hh, qrows, :] for hh in (0, 1)]
            dqt_scr[...] = jnp.zeros_like(dqt_scr)

            def kv(jb, c2, masked, width):
                keys = pl.ds(pl.multiple_of(jb * tk, tk), width)
                sts = [_dot_nt(ka[hh, keys, :], qh[hh]) for hh in (0, 1)]
                dps = [_dot_nt(va[hh, keys, :], dh[hh]) for hh in (0, 1)]
                for hh in (0, 1):
                    pt = jnp.exp(sts[hh])
                    if masked:
                        pt = jnp.where(jb * tk + keyi[0:width] <= r0 + qryi, pt, 0.0)
                    dsb = (pt * dps[hh]).astype(bf16)
                    dv_scr[hh, keys, :] += _dot(pt.astype(bf16), dh[hh])
                    dk_scr[hh, keys, :] += _dot(dsb, qh[hh])
                    dqt_scr[hh] += _dot(kat[hh, jb, :, 0:width], dsb)
                return c2

            last = (r0 + tq - 1) // tk
            lax.fori_loop(0, last, lambda j, c2: kv(j, c2, False, tk), 0)
            kv(last, 0, True, tk // 2 if first_half else tk)
            dq0, dq1 = dqt_scr[0].T, dqt_scr[1].T
            dqkv_ref[qrows, 0:128] = (jnp.where(hm[0], dq0, dq1) * 0.125).astype(bf16)
            dF_ref[qrows, :] = jnp.where(lane == 0, dq0[:, HD:HD + 1], jnp.where(lane == 1, dq1[:, 0:1], 0.0))

        def qpair(t, c):
            qblock(2 * t, True)
            qblock(2 * t + 1, False)
            return c

        assert tk == 2 * tq
        lax.fori_loop(0, S // tk, qpair, 0)

        def finish(i, c):
            rows = pl.ds(pl.multiple_of(i * tq, tq), tq)
            dk0, dk1 = dk_scr[0, rows, :], dk_scr[1, rows, :]
            dqkv_ref[rows, 128:256] = jnp.where(hm[0], dk0, dk1).astype(bf16)
            dqkv_ref[rows, 256:384] = jnp.where(hm[0], dv_scr[0, rows, :], dv_scr[1, rows, :]).astype(bf16)
            cs = jnp.where(lane == 0, dk0[:, HD + L_ONE:HD + L_ONE + 1],
                           jnp.where(lane == 1, dk1[:, L_ONE:L_ONE + 1], 0.0))
            dF_ref[rows, :] = dF_ref[rows, :] - cs
            return c

        lax.fori_loop(0, S // tq, finish, 0)

    pair = pl.BlockSpec((S, 128), lambda p: (0, p))
    return _call(
        order, body, (qkva, doa, oa, lse, F, dproj), name="fox_bwd", grid=(4,),
        in_specs=[pl.BlockSpec((S, FOX_BLK), lambda p: (0, p)), pair, pair, pair,
                  pl.BlockSpec((S, 128), lambda p: (0, 0)), pl.BlockSpec(memory_space=pl.ANY)],
        out_specs=[pair, pl.BlockSpec((S, FOX_BLK), lambda p: (0, F_FOX // FOX_BLK + p))],
        out_shape=[jax.ShapeDtypeStruct((S, FOXW), f32), jax.ShapeDtypeStruct((S, NP), bf16)],
        input_output_aliases={5: 1},
        scratch_shapes=[pltpu.VMEM((2, S, 128), bf16)] * 4 + [pltpu.VMEM((2, S // tk, 128, tk), bf16)]
        + [pltpu.VMEM((2, S, 128), f32)] * 2 + [pltpu.VMEM((2, 128, tq), f32)],
        compiler_params=_params(("parallel",)),
    )


def _forget_bwd(order, dF, fa, bpad, dproj):
    nb = S // TQ

    def body(dF_ref, fa_ref, b_ref, _dproj_in, db_ref, dfa_ref):
        rr = lax.broadcasted_iota(jnp.int32, (TQ, TQ), 0)
        cc = lax.broadcasted_iota(jnp.int32, (TQ, TQ), 1)
        upper = (cc >= rr).astype(bf16)
        lane = lax.broadcasted_iota(jnp.int32, (1, 128), 1)
        carry = jnp.zeros((1, 128), f32)
        db = jnp.zeros((1, 128), f32)
        for b in reversed(range(nb)):
            cols = jnp.zeros((TQ, 128), f32)
            for h in range(8):
                c0 = (h // 2) * 128 + h % 2
                cols = jnp.where(lane == h, dF_ref[b * TQ:(b + 1) * TQ, c0:c0 + 1], cols)
            dlf = carry
            for part in _split3(cols):
                dlf = dlf + _dot(upper, part)
            carry = carry + jnp.sum(cols, axis=0, keepdims=True)
            z = fa_ref[b * TQ:(b + 1) * TQ, :] + b_ref[...]
            dz = jnp.where(lane < 8, dlf * jax.nn.sigmoid(-z), 0.0)
            dfa_ref[b * TQ:(b + 1) * TQ, 0:128] = dz.astype(bf16)
            dfa_ref[b * TQ:(b + 1) * TQ, 128:256] = jnp.zeros((TQ, 128), bf16)
            db = db + jnp.sum(dz, axis=0, keepdims=True)
        db_ref[...] = db

    whole = lambda a: pl.BlockSpec(a.shape, lambda i: (0,) * a.ndim)
    return _call(
        order, body, (dF, fa, bpad, dproj), name="forget_bwd", grid=(1,),
        in_specs=[whole(dF), whole(fa), whole(bpad), pl.BlockSpec(memory_space=pl.ANY)],
        out_specs=[pl.BlockSpec((1, 128), lambda i: (0, 0)), pl.BlockSpec((S, 256), lambda i: (0, F_FA // 256))],
        out_shape=[jax.ShapeDtypeStruct((1, 128), f32), jax.ShapeDtypeStruct((S, NP), bf16)],
        input_output_aliases={3: 1},
        compiler_params=_params(("arbitrary",)),
    )


def _dil_bwd(order, qkvb, dob, ob, lseb, rope, dproj):
    c_t, s1_t, s2_t = rope

    def body(*refs):
        q_refs, k_refs, v_refs = refs[0:3], refs[3:6], refs[6:9]
        dob_ref, ob_ref, lse_ref, c_ref, s1_ref, s2_ref, _dproj_in, dqkv_ref = refs[9:17]
        qp, kp, vp, dop, lp, dlp, dln, dqp, dkp, dvp, nat = refs[17:28]
        dq_out, dk_out, dv_out = _dil_views(dqkv_ref)
        _, hm = _head_masks()

        def delta_rows(i, c):
            r0 = pl.multiple_of(i * TQ, TQ)
            prod = dob_ref[pl.ds(r0, TQ), :] * ob_ref[pl.ds(r0, TQ), :].astype(f32)
            d0 = jnp.sum(jnp.where(hm[0], prod, 0.0), axis=1, keepdims=True)
            d1 = jnp.sum(jnp.where(hm[1], prod, 0.0), axis=1, keepdims=True)
            dln[pl.ds(r0, TQ), :] = jnp.where(hm[0], d0, d1)
            return c

        lax.fori_loop(0, S // TQ, delta_rows, 0)

        for g, r in enumerate(DIL):
            nbl = S // r // BAND
            if r == 1:
                srcs = (q_refs[g], k_refs[g], v_refs[g], dob_ref, lse_ref, dln)
            else:
                for dst, src in ((qp, q_refs[g]), (kp, k_refs[g]), (vp, v_refs[g]), (dop, dob_ref),
                                 (lp, lse_ref), (dlp, dln)):
                    _permute_in(dst, src, r)
                srcs = (qp, kp, vp, dop, lp, dlp)
            dkp[...] = jnp.zeros_like(dkp)
            dvp[...] = jnp.zeros_like(dvp)

            def blk(t, c, srcs=srcs, nbl=nbl):
                qs_, ks_, vs_, dos_, ls_, dls_ = srcs
                work = []
                for u in range(DIL_UNROLL):
                    r0, k0, valid = _band_geometry(DIL_UNROLL * t + u, nbl)
                    q = qs_[pl.ds(r0, BAND), :] * 0.125
                    kwf = ks_[pl.ds(k0, _band_width(nbl)), :]
                    kw = kwf.astype(bf16)
                    vw = vs_[pl.ds(k0, _band_width(nbl)), :].astype(bf16)
                    do = dos_[pl.ds(r0, BAND), :]
                    lse = ls_[pl.ds(r0, BAND), :]
                    dlt = dls_[pl.ds(r0, BAND), :]
                    for hh in (0, 1):
                        qh = jnp.where(hm[hh], q, 0.0).astype(bf16)
                        doh = jnp.where(hm[hh], do, 0.0).astype(bf16)
                        kh = jnp.where(hm[hh], kwf, 0.0).astype(bf16)
                        work.append((u, hh, r0, k0, valid, qh, doh, kh, lse[:, hh * HD:hh * HD + 1],
                                     dlt[:, hh * HD:hh * HD + 1], _dot_nt(qh, kw), _dot_nt(doh, vw)))
                for u, hh, r0, k0, valid, qh, doh, kh, lse_h, dlt_h, s, dp in work:
                    if hh == 0:
                        dq = jnp.zeros((BAND, 128), f32)
                        dk = jnp.zeros((_band_width(nbl), 128), f32)
                        dv = jnp.zeros((_band_width(nbl), 128), f32)
                    pr = jnp.where(valid, jnp.exp(s - lse_h), 0.0)
                    dsb = (pr * (dp - dlt_h)).astype(bf16)
                    dv = dv + _dot_tn(pr.astype(bf16), doh)
                    dk = dk + _dot_tn(dsb, qh)
                    dq = dq + _dot(dsb, kh)
                    if hh == 1:
                        dqp[pl.ds(r0, BAND), :] = dq * 0.125
                        dkp[pl.ds(k0, _band_width(nbl)), :] += dk
                        dvp[pl.ds(k0, _band_width(nbl)), :] += dv
                return c

            lax.fori_loop(0, S // BAND // DIL_UNROLL, blk, 0)

            for acc, out, roped in ((dqp, dq_out[g], True), (dkp, dk_out[g], True), (dvp, dv_out[g], False)):
                if r == 1:
                    src = acc
                else:
                    _permute_out(nat, acc, r)
                    src = nat

                def emit(i, c, src=src, out=out, roped=roped):
                    r0 = pl.multiple_of(i * TQ, TQ)
                    d = src[pl.ds(r0, TQ), :]
                    if roped:
                        d = (d * c_ref[pl.ds(r0, TQ), :] + pltpu.roll(d * s1_ref[pl.ds(r0, TQ), :], 8, 1)
                             + pltpu.roll(d * s2_ref[pl.ds(r0, TQ), :], 120, 1))
                    out[pl.ds(r0, TQ), :] = d.astype(bf16)
                    return c

                lax.fori_loop(0, S // TQ, emit, 0)

    pair = pl.BlockSpec((S, 128), lambda p: (0, p))
    tab = pl.BlockSpec((S, 128), lambda p: (0, 0))
    blk_spec = pl.BlockSpec((S, DIL_BLK), lambda p: (0, p))
    return _call(
        order, body, [qkvb] * 9 + [dob, ob, lseb, c_t, s1_t, s2_t, dproj], name="dil_bwd", grid=(2,),
        in_specs=_dil_in_specs() + [pair, pair, pair, tab, tab, tab, pl.BlockSpec(memory_space=pl.ANY)],
        out_specs=blk_spec,
        out_shape=jax.ShapeDtypeStruct((S, NP), bf16),
        input_output_aliases={15: 0},
        scratch_shapes=[pltpu.VMEM((S, 128), f32)] * 11,
        compiler_params=_params(("parallel",)),
    )


def _inproj_bwd(order, dproj, wt, x, dx2, g1):
    tm = 256

    def body(d_ref, w_ref, x_ref, dx2_ref, g_ref, dx_ref, dg_ref):
        i = pl.program_id(0)
        dh = _dot(d_ref[...], w_ref[...])
        xb = x_ref[...]
        r = lax.rsqrt(jnp.mean(xb * xb, axis=-1, keepdims=True) + EPS)
        xh = xb * r
        gdh = dh * g_ref[...]
        dx_ref[...] = dx2_ref[...] + r * (gdh - xh * jnp.mean(gdh * xh, axis=-1, keepdims=True))

        @pl.when(i == 0)
        def _():
            dg_ref[...] = jnp.zeros_like(dg_ref)

        dg_ref[...] += jnp.sum(dh * xh, axis=0, keepdims=True)

    row = pl.BlockSpec((tm, D), lambda i: (i, 0))
    vec = pl.BlockSpec((1, D), lambda i: (0, 0))
    return _call(
        order, body, (dproj, wt, x, dx2, g1), name="inproj_bwd", grid=(S // tm,),
        in_specs=[pl.BlockSpec((tm, NP), lambda i: (i, 0)), pl.BlockSpec((NP, D), lambda i: (0, 0)), row, row, vec],
        out_specs=[row, vec],
        out_shape=[jax.ShapeDtypeStruct((S, D), f32), jax.ShapeDtypeStruct((1, D), f32)],
        compiler_params=_params(("arbitrary",)),
    )


HBM = pl.BlockSpec(memory_space=pltpu.HBM)
SEM = pl.BlockSpec(memory_space=pltpu.SEMAPHORE)
SMALL_ROWS = 8


def _comm_call(name, body, bufs, order, sems_in=(), new_sems=(), behind=()):
    nb, ns, nn = len(bufs), len(sems_in), len(new_sems)
    extra = order.token_for(bufs) + list(behind)

    def kern(*refs):
        off = nb + ns + len(extra)
        body(refs[:nb], refs[nb:nb + ns], refs[off:off + nn])
        refs[-1][...] = jnp.zeros((8, 128), f32)

    res = pl.pallas_call(
        kern, name=name,
        in_specs=[HBM] * nb + [SEM] * ns + [pl.BlockSpec(memory_space=pl.ANY)] * len(extra),
        out_specs=[SEM] * nn + [HBM] * nb + [pl.BlockSpec(memory_space=pltpu.VMEM)],
        out_shape=[pltpu.SemaphoreType.DMA((k,)) for k in new_sems] + [pltpu.HBM(b.shape, b.dtype) for b in bufs]
        + [jax.ShapeDtypeStruct((8, 128), f32)],
        input_output_aliases={i: nn + i for i in range(nb)},
        compiler_params=pltpu.CompilerParams(has_side_effects=pltpu.SideEffectType.DATAFLOW_SIDE_EFFECTING),
    )(*[pltpu.with_memory_space_constraint(b, pltpu.HBM) for b in bufs], *sems_in, *extra)
    order.mark(res[-1])
    return list(res[:nn]), list(res[nn:nn + nb])


def _place():
    x, y, c = lax.axis_index("x"), lax.axis_index("y"), lax.axis_index("c")
    chips = [(1 - x, y), (x, 1 - y), (1 - x, 1 - y)]
    return x, y, c, chips


def _rcopy(src, dst, ssem, rsem, dev):
    return pltpu.make_async_remote_copy(src_ref=src, dst_ref=dst, send_sem=ssem, recv_sem=rsem,
                                        device_id=dev, device_id_type=pl.DeviceIdType.MESH)


def _half(nrows, which):
    return pl.ds(which * (nrows // 2), nrows // 2)


def _ici_copies(stack, group_sizes, ssems, rsems):
    x, y, c, chips = _place()
    me_q = 2 * x + y
    sends, recvs = [], []
    a = 0
    for grp, size in enumerate(group_sizes):
        for k in range(size):
            rows = _half(stack[a].shape[1], c)
            for j, (cx, cy) in enumerate(chips):
                mine = stack[a].at[me_q, rows]
                sends.append(_rcopy(mine, mine, ssems[grp].at[k * 3 + j], rsems[grp].at[k * 3 + j], (cx, cy, c)))
                theirs = stack[a].at[2 * cx + cy, rows]
                recvs.append(_rcopy(theirs, theirs, ssems[grp].at[k * 3 + j], rsems[grp].at[k * 3 + j],
                                    (cx, cy, c)))
            a += 1
    return sends, recvs


def _allgather_start(name, stacks, order):
    n = len(stacks)

    def body(bufs, _, new):
        sends, _r = _ici_copies(bufs, [n], [new[0]], [new[1]])
        for cp in sends:
            cp.start()

    return _comm_call(name, body, stacks, order, new_sems=(3 * n, 3 * n))


def _forward_copies(stack, ssem, rsem):
    x, y, c, chips = _place()
    sib = (x, y, 1 - c)
    sends, recvs = [], []
    for a in range(len(stack)):
        for j, (cx, cy) in enumerate(chips):
            landed = stack[a].at[2 * cx + cy, _half(stack[a].shape[1], c)]
            sends.append(_rcopy(landed, landed, ssem.at[a * 3 + j], rsem.at[a * 3 + j], sib))
            other = stack[a].at[2 * cx + cy, _half(stack[a].shape[1], 1 - c)]
            recvs.append(_rcopy(other, other, ssem.at[a * 3 + j], rsem.at[a * 3 + j], sib))
    return sends, recvs


def _allgather_forward(name, stacks, sems, order, behind=()):
    n = len(stacks)

    def body(bufs, taken, new):
        sends, recvs = _ici_copies(bufs, [n], [taken[0]], [taken[1]])
        fwd, _r = _forward_copies(bufs, new[0], new[1])
        for arrived, onward in zip(recvs, fwd):
            arrived.wait_recv()
            onward.start()
        for cp in sends:
            cp.wait_send()

    return _comm_call(name, body, stacks, order, sems_in=sems, new_sems=(3 * n, 3 * n), behind=behind)


def _allgather_finish(name, stacks, sems, order):
    def body(bufs, taken, _):
        sends, recvs = _forward_copies(bufs, taken[0], taken[1])
        for cp in sends:
            cp.wait_send()
        for cp in recvs:
            cp.wait_recv()

    return _comm_call(name, body, stacks, order, sems_in=sems)[1]


def _window_unit(q, j):
    return C2I[WIN_UNIT0[q] + j]


def _pair_copies(g, t, ssem, rsem, gathered):
    x, y, c, _ = _place()
    sib = (x, y, 1 - c)
    cps, whole = [], []
    for a in range(len(g)):
        if a == 0 and gathered:
            for q in range(NCHIP):
                for j in range(WIN_UNITS // 2):
                    u = jnp.where(c == 0, _window_unit(q, WIN_UNITS // 2 + j), _window_unit(q, j))
                    src = g[0].at[pl.ds(pl.multiple_of(u * UNIT, UNIT), UNIT), :]
                    cps.append(_rcopy(src, t[0].at[q, pl.ds(j * UNIT, UNIT), :], ssem.at[0], rsem.at[0], sib))
            whole.append(_rcopy(t[0], t[0], ssem.at[0], rsem.at[0], sib))
        else:
            cp = _rcopy(g[a].at[:, _half(g[a].shape[1], 1 - c), :], t[a], ssem.at[a], rsem.at[a], sib)
            cps.append(cp)
            whole.append(cp)
    return cps, whole


def _comm_multi(name, parts, order):
    def body(buf_refs, taken, new):
        ib = it = inew = 0
        for pbody, pbufs, psems, pnew, _ in parts:
            pbody(buf_refs[ib:ib + len(pbufs)], taken[it:it + len(psems)], new[inew:inew + len(pnew)])
            ib, it, inew = ib + len(pbufs), it + len(psems), inew + len(pnew)

    sems, bufs = _comm_call(name, body, [b for p in parts for b in p[1]], order,
                            sems_in=[s for p in parts for s in p[2]], new_sems=[k for p in parts for k in p[3]])
    out, ib, inew = [], 0, 0
    for _, pbufs, _, pnew, unpack in parts:
        out.append(unpack(sems[inew:inew + len(pnew)], bufs[ib:ib + len(pbufs)]))
        ib, inew = ib + len(pbufs), inew + len(pnew)
    return out


def _pair_start_part(gs, gathered=False):
    n = len(gs)
    ts = [lax.empty((NCHIP, WIN_ROWS // 2, D) if (a == 0 and gathered) else (NCHIP, g.shape[1] // 2, g.shape[2]), f32)
          for a, g in enumerate(gs)]

    def body(bufs, _, new):
        for cp in _pair_copies(bufs[:n], bufs[n:], new[0], new[1], gathered)[0]:
            cp.start()

    return body, list(gs) + ts, (), (n, n), lambda sems, bufs: (sems, bufs)


def _pair_wait_part(bufs, sems, gathered=False):
    n = len(bufs) // 2

    def body(refs, taken, _):
        for cp in _pair_copies(refs[:n], refs[n:], taken[0], taken[1], gathered)[1]:
            cp.wait_send()
            cp.wait_recv()

    return body, list(bufs), list(sems), (), lambda _, out: (out[:n], out[n:])


def _row_tile(h):
    return min(h, 256)


def _pair_add(order, g, t, q_arr, c_arr, name):
    _, R, C = g.shape
    h = R // 2
    tr = _row_tile(h)
    nblk = h // tr

    def body(q_ref, c_ref, g_ref, t_ref, own_ref, p16_ref):
        s = g_ref[...] + t_ref[...]
        p16_ref[...] = s.astype(bf16)

        @pl.when(pl.program_id(1) == q_ref[0])
        def _():
            own_ref[...] = s

    blk = pl.BlockSpec((None, tr, C), lambda i, q, q_ref, c_ref: (q, i, 0))
    return _call_indexed(
        order, body, (q_arr, c_arr), (g, t), (nblk, NCHIP),
        [pl.BlockSpec((None, tr, C), lambda i, q, q_ref, c_ref: (q, c_ref[0] * nblk + i, 0)), blk],
        [pl.BlockSpec((tr, C), lambda i, q, q_ref, c_ref: (i, 0)), blk],
        name=name,
        out_shape=[jax.ShapeDtypeStruct((h, C), f32), jax.ShapeDtypeStruct((NCHIP, h, C), bf16)],
        compiler_params=_params(("parallel", "arbitrary")),
    )


def _pair_add_gathered(order, dwt, t, q_arr, c_arr, name):
    half_units, half_rows = WIN_UNITS // 2, WIN_ROWS // 2
    table = jnp.asarray([_window_unit(q, j) for q in range(NCHIP) for j in range(WIN_UNITS)], jnp.int32)

    def body(tab_ref, q_ref, c_ref, g_hbm, t_ref, own_ref, p16_ref, buf, sem):
        q = pl.program_id(0)

        def gather(w, slot):
            cps = []
            for j in range(half_units):
                u = tab_ref[w * WIN_UNITS + c_ref[0] * half_units + j]
                cps.append(pltpu.make_async_copy(g_hbm.at[pl.ds(pl.multiple_of(u * UNIT, UNIT), UNIT), :],
                                                 buf.at[slot, pl.ds(j * UNIT, UNIT), :], sem.at[slot]))
            return cps

        @pl.when(q == 0)
        def _():
            for cp in gather(0, 0):
                cp.start()

        @pl.when(q + 1 < NCHIP)
        def _():
            for cp in gather(q + 1, (q + 1) % 2):
                cp.start()

        slot = q % 2
        pltpu.make_async_copy(buf.at[slot], buf.at[slot], sem.at[slot]).wait()
        s = buf[slot] + t_ref[...]
        p16_ref[...] = s.astype(bf16)

        @pl.when(q == q_ref[0])
        def _():
            own_ref[...] = s

    blk = pl.BlockSpec((None, half_rows, D), lambda q, tab_ref, q_ref, c_ref: (q, 0, 0))
    return _call_indexed(
        order, body, (table, q_arr, c_arr), (dwt, t), (NCHIP,),
        [pl.BlockSpec(memory_space=pl.ANY), blk],
        [pl.BlockSpec((half_rows, D), lambda q, tab_ref, q_ref, c_ref: (0, 0)), blk],
        scratch_shapes=[pltpu.VMEM((2, half_rows, D), f32), pltpu.SemaphoreType.DMA((2,))],
        name=name,
        out_shape=[jax.ShapeDtypeStruct((half_rows, D), f32),
                   jax.ShapeDtypeStruct((NCHIP, half_rows, D), bf16)],
        compiler_params=_params(("arbitrary",)),
    )


def _shard_copies(p, r, sm, ssem, rsem):
    x, y, c, chips = _place()
    n = len(p)
    sends, recvs = [], []
    for a in range(n):
        for j, (cx, cy) in enumerate(chips):
            k = a * 3 + j
            sends.append(_rcopy(p[a].at[2 * cx + cy], r[a].at[j], ssem.at[k], rsem.at[k], (cx, cy, c)))
            recvs.append(_rcopy(r[a].at[j], r[a].at[j], ssem.at[k], rsem.at[k], (cx, cy, c)))
    if sm is not None:
        mine = sm.at[4 * x + 2 * y + c]
        for i in range(1, 8):
            px = (1 - x) if i & 4 else x
            py = (1 - y) if i & 2 else y
            pc = (1 - c) if i & 1 else c
            k = 3 * n + i - 1
            sends.append(_rcopy(mine, mine, ssem.at[k], rsem.at[k], (px, py, pc)))
            slot = sm.at[4 * px + 2 * py + pc]
            recvs.append(_rcopy(slot, slot, ssem.at[k], rsem.at[k], (px, py, pc)))
    return sends, recvs


def _shard_start_part(p16s, sm=None):
    n = len(p16s)
    rs = [lax.empty((3,) + p.shape[1:], bf16) for p in p16s]
    extra = [] if sm is None else [sm]
    nsem = 3 * n + (7 if sm is not None else 0)

    def body(bufs, _, new):
        sends, _r = _shard_copies(bufs[:n], bufs[n:2 * n], bufs[2 * n] if extra else None, new[0], new[1])
        for cp in sends:
            cp.start()

    return body, list(p16s) + rs + extra, (), (nsem, nsem), lambda sems, bufs: (sems, bufs)


def _shard_wait_part(bufs, sems, n):
    has_sm = len(bufs) > 2 * n

    def body(refs, taken, _):
        sends, recvs = _shard_copies(refs[:n], refs[n:2 * n], refs[2 * n] if has_sm else None, taken[0], taken[1])
        for cp in sends:
            cp.wait_send()
        for cp in recvs:
            cp.wait_recv()

    return body, list(bufs), list(sems), (), lambda _, out: (out[n:2 * n], (out[2 * n] if has_sm else None))


def _shard_sum(order, own, r, c_arr, name):
    h, C = own.shape
    tr = _row_tile(h)
    nblk = h // tr

    def body(c_ref, p_ref, r_ref, o_ref):
        s = p_ref[...]
        for j in range(3):
            s = s + r_ref[j].astype(f32)
        o_ref[...] = s

    return _call_indexed(
        order, body, (c_arr,), (own, r), (nblk,),
        [pl.BlockSpec((tr, C), lambda i, c_ref: (i, 0)), pl.BlockSpec((3, tr, C), lambda i, c_ref: (0, i, 0))],
        pl.BlockSpec((tr, C), lambda i, c_ref: (c_ref[0] * nblk + i, 0)),
        name=name, out_shape=jax.ShapeDtypeStruct((2 * h, C), f32),
        compiler_params=_params(("parallel",)),
    )


def _swap_copies(full, ssem, rsem):
    x, y, c, _ = _place()
    sends, recvs = [], []
    for a in range(len(full)):
        mine = full[a].at[_half(full[a].shape[0], c)]
        sends.append(_rcopy(mine, mine, ssem.at[a], rsem.at[a], (x, y, 1 - c)))
        other = full[a].at[_half(full[a].shape[0], 1 - c)]
        recvs.append(_rcopy(other, other, ssem.at[a], rsem.at[a], (x, y, 1 - c)))
    return sends, recvs


def _swap_start_part(fulls):
    n = len(fulls)

    def body(bufs, _, new):
        for cp in _swap_copies(bufs, new[0], new[1])[0]:
            cp.start()

    return body, list(fulls), (), (n, n), lambda sems, bufs: (sems, bufs)


def _swap_wait_part(fulls, sems):
    def body(refs, taken, _):
        sends, recvs = _swap_copies(refs, taken[0], taken[1])
        for cp in sends:
            cp.wait_send()
        for cp in recvs:
            cp.wait_recv()

    return body, list(fulls), list(sems), (), lambda _, out: out


def _small_sum(order, sm):
    def body(sm_ref, o_ref):
        s = sm_ref[0]
        for d in range(1, 8):
            s = s + sm_ref[d]
        o_ref[...] = s

    return _call(order, body, (sm,), name="small_grad_sum", out_shape=jax.ShapeDtypeStruct((SMALL_ROWS, D), f32))


def _adamw_math(w, g, m, v):
    m = ADAM_B1 * m + (1.0 - ADAM_B1) * g
    v = ADAM_B2 * v + (1.0 - ADAM_B2) * (g * g)
    m_hat = m / (1.0 - ADAM_B1 ** ADAM_STEP)
    v_hat = v / (1.0 - ADAM_B2 ** ADAM_STEP)
    return -ADAM_LR * (m_hat / (jnp.sqrt(v_hat) + ADAM_EPS) + ADAM_WD * w), m, v


def _adamw_small(order, ws, gs, ms, vs, name):
    n = len(ws)

    def body(*refs):
        for i in range(n):
            res = _adamw_math(*[refs[k * n + i][...] for k in range(4)])
            for k in range(3):
                refs[4 * n + 3 * i + k][...] = res[k]

    out = _call(order, body, list(ws) + list(gs) + list(ms) + list(vs), name=name,
                out_shape=[jax.ShapeDtypeStruct(w.shape, f32) for w in ws for _ in range(3)])
    return [out[3 * i:3 * i + 3] for i in range(n)]


def _adamw(order, w, g, m, v, name):
    R, C = w.shape
    if R <= 256 or R % 256 == 0:
        tr, tc = min(R, 256), C
    else:
        tr, tc = R, 128

    def body(w_ref, g_ref, m_ref, v_ref, d_ref, nm_ref, nv_ref):
        d_ref[...], nm_ref[...], nv_ref[...] = _adamw_math(w_ref[...], g_ref[...], m_ref[...], v_ref[...])

    blk = pl.BlockSpec((tr, tc), lambda i, j: (i, j))
    return _call(
        order, body, (w, g, m, v), name=name, grid=(R // tr, C // tc), in_specs=[blk] * 4, out_specs=[blk] * 3,
        out_shape=[jax.ShapeDtypeStruct((R, C), f32)] * 3,
        compiler_params=_params(("parallel", "parallel")),
    )


def _feature_rows(w):
    return jnp.transpose(w, (2, 0, 1))


def _feature_major(w):
    return _feature_rows(w).reshape(SHARD_IN, D)


def _unfeature_rows(a):
    return jnp.transpose(a, (1, 2, 0))


ADAM_IN_ROWS = 134


def _adamw_w_in(order, w, g, m, v):
    def body(w_ref, g_ref, m_ref, v_ref, go_ref, d_ref, nm_ref, nv_ref):
        g = g_ref[...]
        go_ref[...] = g
        d_ref[...], nm_ref[...], nv_ref[...] = _adamw_math(w_ref[...], g, m_ref[...], v_ref[...])

    blk = pl.BlockSpec((ADAM_IN_ROWS, 1, D), lambda i: (i, 0, 0))
    return _call(
        order, body, (w, g, m, v), name="adamw_w_in", grid=(SHARD_IN // ADAM_IN_ROWS,), in_specs=[blk] * 4,
        out_specs=[blk] * 4, out_shape=[jax.ShapeDtypeStruct((SHARD_IN, 1, D), f32)] * 4,
        compiler_params=_params(("parallel",)),
    )


def _window_of(wt, q):
    def plain(k):
        return lambda w: jnp.pad(w, ((OWN_ROW0[k], WIN_ROWS - OWN_ROW0[k] - SHARD_IN), (0, 0))).astype(bf16)

    def chip1(w):
        lo = jnp.pad(w[0:FA_AT], ((OWN_ROW0[1], WIN_ROWS - OWN_ROW0[1] - FA_AT), (0, 0)))
        hi = jnp.pad(w[FA_AT + N_FA:SHARD_IN], ((OWN_ROW0[1] + FA_AT, WIN_ROWS - OWN_ROW0[1] - (SHARD_IN - N_FA)), (0, 0)))
        return (lo + hi).astype(bf16)

    win = lax.switch(q, [plain(0), chip1, plain(2), plain(3)], wt)
    fa = jnp.pad(wt[FA_AT:FA_AT + N_FA], ((0, FA_ROWS - N_FA), (0, 0))).astype(bf16)
    return win, fa


def _own_rows(gwin, gfa, q):
    def plain(k):
        return lambda gw, gf: gw[OWN_ROW0[k]:OWN_ROW0[k] + SHARD_IN]

    def chip1(gw, gf):
        o, rest = OWN_ROW0[1], SHARD_IN - FA_AT - N_FA
        return (jnp.pad(gw[o:o + FA_AT], ((0, SHARD_IN - FA_AT), (0, 0)))
                + jnp.pad(gf[0:N_FA], ((FA_AT, rest), (0, 0)))
                + jnp.pad(gw[o + FA_AT:o + FA_AT + rest], ((FA_AT + N_FA, 0), (0, 0))))

    return lax.switch(q, [plain(0), chip1, plain(2), plain(3)], gwin, gfa)


def kernel(x, norm_attn_g, w_in, b_forget, w_branch_a, w_branch_b, w_out, norm_mlp_g, w_up, w_down, norm_final_g, loss_target, m_norm_attn_g, m_w_in, m_b_forget, m_w_branch_a, m_w_branch_b, m_w_out, m_norm_mlp_g, m_w_up, m_w_down, m_norm_final_g, v_norm_attn_g, v_w_in, v_b_forget, v_w_branch_a, v_w_branch_b, v_w_out, v_norm_mlp_g, v_w_up, v_w_down, v_norm_final_g):
    xi, yi, ci = lax.axis_index("x"), lax.axis_index("y"), lax.axis_index("c")
    q_me = 2 * xi + yi
    c_arr = jnp.reshape(ci, (1,)).astype(jnp.int32)
    q_arr = jnp.reshape(q_me, (1,)).astype(jnp.int32)
    x_, tgt = x[0], loss_target[0]

    names = ["w_branch_a", "w_branch_b", "w_out", "w_up", "w_down"]
    big = dict(zip(names, [w_branch_a[0], w_branch_b[0], w_out[0], w_up[0], w_down[0]]))
    ms = dict(zip(names, [m_w_branch_a[0], m_w_branch_b[0], m_w_out[0], m_w_up[0], m_w_down[0]]))
    vs = dict(zip(names, [v_w_branch_a[0], v_w_branch_b[0], v_w_out[0], v_w_up[0], v_w_down[0]]))
    grad, upd = {}, {}
    order = _Order()

    def run(fn, *args, **kw):
        return fn(order, *args, **kw)

    def own_slot(a):
        return lax.dynamic_update_slice(lax.empty((NCHIP,) + a.shape, a.dtype), a[None], (q_me, 0, 0))

    wt_own = _feature_major(w_in)
    win, fa_blk = _window_of(wt_own, q_me)
    sem_in, in_s = _allgather_start("allgather_start_in", [own_slot(win), own_slot(fa_blk)], order)
    sem_rest, rest = _allgather_start("allgather_start_rest", [own_slot(w.astype(bf16)) for w in big.values()], order)
    rope = _rope_tables(order.tok[0, 0])
    sem_f, in_s = _allgather_forward("allgather_forward_in", in_s, sem_in, order, behind=[wt_own, *rope])
    wins, fas = _allgather_finish("allgather_finish_in", in_s, sem_f, order)
    wt = run(_assemble_win, wins, fas)

    bpad = jnp.pad(b_forget, ((0, 0), (0, 120)))
    h1, qkvb, qkva, gates, fa = run(_norm_inproj, x_, norm_attn_g, wt, rope)
    F = run(_forget_cumsum, fa, bpad)
    oa, lsea = run(_fox_fwd, qkva, F)
    sem_f, rest = _allgather_forward("allgather_forward_rest", rest, sem_rest, order)
    ob, lseb = run(_dil_fwd, qkvb)
    was, wbs, wouts, wups, wdowns = _allgather_finish("allgather_finish_rest", rest, sem_f, order)
    wout = wouts.reshape(D, D)
    wdown = wdowns.reshape(DFF, D)
    ya, yb, mixed = run(_branch_mix, oa, ob, was, wbs, gates)
    x2, h2 = run(_outproj_norm, mixed, wout, x_, norm_mlp_g)
    u, a = run(_mlp_up, h2, wups)
    dx3, dx3b, dg3, loss_part = run(_mlp_down_loss, a, wdown, x2, norm_final_g.reshape(1, D), tgt)

    def comm(name, *parts):
        return _comm_multi(name, list(parts), order)

    def pair_adds(group, gs, ts):
        return zip(*[run(_pair_add, gs[i], ts[i], q_arr, c_arr, "pair_add_" + nm) for i, nm in enumerate(group)])

    def shard_sums(group, p32s, rs):
        return [run(_shard_sum, p32s[i], rs[i], c_arr, "shard_sum_" + nm) for i, nm in enumerate(group)]

    def adamw_group(group, fulls):
        for nm, gfull in zip(group, fulls):
            grad[nm] = gfull
            upd[nm] = run(_adamw, big[nm], gfull, ms[nm], vs[nm], "adamw_" + nm)

    grp_a, grp_b, grp_c = ["w_down", "w_up"], ["w_out", "w_branch_a", "w_branch_b"], ["w_in", "w_in_fa"]
    du = run(_mlp_down_bwd, dx3b, wdown, u)
    dwdown = run(_mm, a, dx3b, "tn", f32, 1024, D, "wgrad_down")
    dwup = run(_mm, h2, du, "tn", f32, D, 1024, "wgrad_up", stack_cols=True)
    ((sem_pa, buf_pa),) = comm("pair_start_a", _pair_start_part([dwdown.reshape(NCHIP, DFF // NCHIP, D), dwup]))
    dx2, dx2b, dg2 = run(_mlp_up_bwd, du, wups, x2, dx3, norm_mlp_g)
    ((gs, ts),) = comm("pair_wait_a", _pair_wait_part(buf_pa, sem_pa))
    p32_a, p16_a = pair_adds(grp_a, gs, ts)
    ((sem_sa, buf_sa),) = comm("shard_start_a", _shard_start_part(p16_a))
    dya, dyb, dproj = run(_gate_bwd, dx2b, wout, gates, ya, yb)
    dwout = run(_mm, mixed, dx2b, "tn", f32, D, D, "wgrad_out")
    doa, dob = run(_branch_bwd, dya, dyb, was, wbs)
    dwas, dwbs = run(_branch_wgrad, oa, ob, dya, dyb)
    ((sem_pb, buf_pb),) = comm("pair_start_b", _pair_start_part([dwout.reshape(NCHIP, D // NCHIP, D), dwas, dwbs]))
    dF, dproj = run(_fox_bwd, qkva, doa, oa, lsea, F, dproj)
    (gs, ts), (rs_a, _) = comm("pair_wait_b_shard_wait_a", _pair_wait_part(buf_pb, sem_pb),
                               _shard_wait_part(buf_sa, sem_sa, len(grp_a)))
    p32_b, p16_b = pair_adds(grp_b, gs, ts)
    fulls_a = shard_sums(grp_a, p32_a, rs_a)
    (sem_wa, fulls_a), (sem_sb, buf_sb) = comm("swap_start_a_shard_start_b", _swap_start_part(fulls_a),
                                               _shard_start_part(p16_b))
    dbf, dproj = run(_forget_bwd, dF, fa, bpad, dproj)
    dproj = run(_dil_bwd, qkvb, dob, ob, lseb, rope, dproj)
    (rs_b, _), fulls_a = comm("shard_wait_b_swap_wait_a", _shard_wait_part(buf_sb, sem_sb, len(grp_b)),
                              _swap_wait_part(fulls_a, sem_wa))
    fulls_b = shard_sums(grp_b, p32_b, rs_b)
    ((sem_wb, fulls_b),) = comm("swap_start_b", _swap_start_part(fulls_b))
    dwt = run(_mm, dproj, h1, "tn", f32, 512, D, "wgrad_in")
    dwfa = jnp.broadcast_to(dwt[F_FA:F_FA + FA_ROWS][None], (NCHIP, FA_ROWS, D))
    (sem_pc, buf_pc), fulls_b = comm("pair_start_c_swap_wait_b", _pair_start_part([dwt, dwfa], gathered=True),
                                     _swap_wait_part(fulls_b, sem_wb))
    adamw_group(grp_b, fulls_b)
    (((dwt_c, dwfa_c), (t_in, t_fa)),) = comm("pair_wait_c", _pair_wait_part(buf_pc, sem_pc, gathered=True))
    p32_in, p16_in = run(_pair_add_gathered, dwt_c, t_in, q_arr, c_arr, "pair_add_w_in")
    p32_fa, p16_fa = run(_pair_add, dwfa_c, t_fa, q_arr, c_arr, "pair_add_w_in_fa")
    ((sem_sc, buf_sc),) = comm("shard_start_c", _shard_start_part([p16_in, p16_fa]))
    gx, dg1 = run(_inproj_bwd, dproj, wt, x_, dx2, norm_attn_g)
    adamw_group(grp_a, fulls_a)
    small = jnp.concatenate([dg1, dg2, dg3, jnp.pad(dbf[:, 0:8], ((0, 0), (0, D - 8))),
                             jnp.pad(loss_part, ((0, 0), (0, D - 128))),
                             jnp.zeros((SMALL_ROWS - 5, D), f32)], axis=0)
    sm = lax.dynamic_update_slice(lax.empty((8, SMALL_ROWS, D), f32), small[None],
                                  (4 * xi + 2 * yi + ci, 0, 0))
    (sem_sm, buf_sm), (rs_c, _) = comm("small_start_shard_wait_c", _shard_start_part([], sm),
                                       _shard_wait_part(buf_sc, sem_sc, len(grp_c)))
    fulls_c = shard_sums(grp_c, [p32_in, p32_fa], rs_c)
    (sem_wc, fulls_c), (_, sm) = comm("swap_start_c_small_wait", _swap_start_part(fulls_c),
                                      _shard_wait_part(buf_sm, sem_sm, 0))
    gsmall = run(_small_sum, sm)
    loss = gsmall[4, 0]

    grad["norm_attn_g"], grad["norm_mlp_g"] = gsmall[0:1], gsmall[1:2]
    grad["norm_final_g"], grad["b_forget"] = gsmall[2:3], gsmall[3:4, 0:8]
    smalls = ["norm_attn_g", "norm_mlp_g", "norm_final_g", "b_forget"]
    res = run(_adamw_small, [norm_attn_g, norm_mlp_g, norm_final_g.reshape(1, D), b_forget],
              [grad[nm] for nm in smalls],
              [m_norm_attn_g, m_norm_mlp_g, m_norm_final_g.reshape(1, D), m_b_forget],
              [v_norm_attn_g, v_norm_mlp_g, v_norm_final_g.reshape(1, D), v_b_forget], "adamw_small")
    upd.update(zip(smalls, res))

    ((gwin, gfa),) = comm("swap_wait_c", _swap_wait_part(fulls_c, sem_wc))
    g_in = _own_rows(gwin, gfa, q_me)
    res_in = run(_adamw_w_in, _feature_rows(w_in), g_in.reshape(SHARD_IN, 1, D), _feature_rows(m_w_in),
                 _feature_rows(v_w_in))
    grad["w_in"] = _unfeature_rows(res_in[0])
    upd["w_in"] = [_unfeature_rows(t) for t in res_in[1:]]

    order_out = ["norm_attn_g", "w_in", "b_forget", "w_branch_a", "w_branch_b", "w_out", "norm_mlp_g", "w_up",
                 "w_down", "norm_final_g"]
    shapes = dict(norm_attn_g=norm_attn_g.shape, w_in=w_in.shape, b_forget=b_forget.shape,
                  w_branch_a=w_branch_a.shape, w_branch_b=w_branch_b.shape, w_out=w_out.shape,
                  norm_mlp_g=norm_mlp_g.shape, w_up=w_up.shape, w_down=w_down.shape, norm_final_g=norm_final_g.shape)
    outs = [loss, gx.reshape(x.shape)]
    outs += [grad[nm].reshape(shapes[nm]) for nm in order_out]
    for k in range(3):
        outs += [upd[nm][k].reshape(shapes[nm]) for nm in order_out]
    return tuple(outs)
```

```python
import jax
import jax.numpy as jnp
from jax import lax
from jax.experimental import pallas as pl
from jax.experimental.pallas import tpu as pltpu

f32 = jnp.float32
bf16 = jnp.bfloat16

S = 2048
D = 1024
DFF = 4096
HD = 64
FOXW = 512
DILOUT = 256
DIL = (1, 4, 16)
BAND = 128
EPS = 1e-6
NEG = -1e30
ROPE_THETA = 500000.0
NCHIP = 4
TQ = 256

ADAM_LR, ADAM_B1, ADAM_B2, ADAM_EPS, ADAM_WD, ADAM_STEP = 0.001, 0.9, 0.999, 1e-08, 0.01, 10
VMEM_LIMIT = 56 * 1024 * 1024

UNIT = 64
NP = 6144
F_DIL, F_FOX, F_FA, F_G = 0, 2304, 3840, 4096
DIL_BLK, FOX_BLK = 1152, 384
WIN_UNITS, WIN_ROWS = 24, 1536
WIN_UNIT0 = (0, 23, 45, 68)
OWN_ROW0 = (0, 2, 60, 62)
SHARD_IN = 1474
N_FA = 8
FA_AT = 1536 - SHARD_IN
FA_ROWS = 32


def _compact_to_internal():
    c2i = {}
    for p in range(2):
        for role in range(3):
            for g in range(3):
                for hh in range(2):
                    c2i[24 + 12 * role + 4 * g + 2 * p + hh] = 18 * p + 6 * role + 2 * g + hh
    for p in range(4):
        for role in range(3):
            for hh in range(2):
                c2i[8 * role + 2 * p + hh] = F_FOX // UNIT + 6 * p + 2 * role + hh
    for j in range(32):
        c2i[60 + j] = F_G // UNIT + j
    return c2i


C2I = _compact_to_internal()
OVERLAP_UNITS = (23, 45, 46, 68)


def _params(sem=None):
    return pltpu.CompilerParams(dimension_semantics=sem, vmem_limit_bytes=VMEM_LIMIT)


class _Order:
    def __init__(self):
        self.tok = None

    def mark(self, v):
        self.tok = v

    def token_for(self, args):
        return [] if self.tok is None or any(self.tok is a for a in args) else [self.tok]


def _call(order, body, args, in_specs=None, **kw):
    args = list(args)
    n_in = len(args)
    if in_specs is None:
        in_specs = [pl.BlockSpec(memory_space=pltpu.VMEM)] * n_in
    kern = body
    extra = order.token_for(args)
    if extra:
        in_specs = list(in_specs) + [pl.BlockSpec(memory_space=pl.ANY)]

        def kern(*refs):
            body(*refs[:n_in], *refs[n_in + 1:])

    out = pl.pallas_call(kern, in_specs=in_specs, **kw)(*args, *extra)
    order.mark(out[0] if isinstance(out, (tuple, list)) else out)
    return out


def _call_indexed(order, body, scalars, args, grid, in_specs, out_specs, scratch_shapes=(), **kw):
    args, in_specs = list(args), list(in_specs)
    n_front = len(scalars) + len(args)
    kern = body
    extra = order.token_for(args)
    if extra:
        in_specs.append(pl.BlockSpec(memory_space=pl.ANY))

        def kern(*refs):
            body(*refs[:n_front], *refs[n_front + 1:])

    out = pl.pallas_call(
        kern, grid_spec=pltpu.PrefetchScalarGridSpec(num_scalar_prefetch=len(scalars), grid=grid, in_specs=in_specs,
                                                     out_specs=out_specs, scratch_shapes=scratch_shapes),
        **kw)(*scalars, *args, *extra)
    order.mark(out[0] if isinstance(out, (tuple, list)) else out)
    return out


def _dot(a, b):
    return jnp.dot(a, b, preferred_element_type=f32)


def _dot_nt(a, b):
    return lax.dot_general(a, b, (((1,), (1,)), ((), ())), preferred_element_type=f32)


def _dot_tn(a, b):
    return lax.dot_general(a, b, (((0,), (0,)), ((), ())), preferred_element_type=f32)


def _split3(x):
    hi = x.astype(bf16)
    r1 = x - hi.astype(f32)
    mid = r1.astype(bf16)
    lo = (r1 - mid.astype(f32)).astype(bf16)
    return hi, mid, lo


def _rope_tables(after):
    half = 8
    inv_freq = jnp.power(jnp.float32(ROPE_THETA), -jnp.arange(half, dtype=f32) * 2.0 / 16)
    ang = (jnp.arange(S).astype(f32) + after)[:, None] * inv_freq[None, :]
    cos, sin = jnp.cos(ang), jnp.sin(ang)
    one = jnp.ones((S, HD - 16), f32)
    zero = jnp.zeros((S, HD - 16), f32)
    z8 = jnp.zeros((S, 8), f32)
    c = jnp.concatenate([cos, cos, one], axis=1)
    s1 = jnp.concatenate([-sin, z8, zero], axis=1)
    s2 = jnp.concatenate([z8, sin, zero], axis=1)
    return tuple(jnp.concatenate([t, t], axis=1) for t in (c, s1, s2))


def _mm(order, a, b, mode, out_dtype, tm, tn, name, stack_cols=False):
    if mode == "nn":
        (M, K), (_, N) = a.shape, b.shape
        a_spec = pl.BlockSpec((tm, K), lambda i, j: (i, 0))
        b_spec = pl.BlockSpec((K, tn), lambda i, j: (0, j))
        dot = _dot
    elif mode == "nt":
        (M, K), (N, _) = a.shape, b.shape
        a_spec = pl.BlockSpec((tm, K), lambda i, j: (i, 0))
        b_spec = pl.BlockSpec((tn, K), lambda i, j: (j, 0))
        dot = _dot_nt
    else:
        (K, M), (_, N) = a.shape, b.shape
        a_spec = pl.BlockSpec((K, tm), lambda i, j: (0, i))
        b_spec = pl.BlockSpec((K, tn), lambda i, j: (0, j))
        dot = _dot_tn

    def body(a_ref, b_ref, o_ref):
        o_ref[...] = dot(a_ref[...], b_ref[...]).astype(out_dtype)

    if stack_cols:
        assert tm == M
        out_spec = pl.BlockSpec((None, tm, tn), lambda i, j: (j, 0, 0))
        out_shape = jax.ShapeDtypeStruct((N // tn, M, tn), out_dtype)
    else:
        out_spec = pl.BlockSpec((tm, tn), lambda i, j: (i, j))
        out_shape = jax.ShapeDtypeStruct((M, N), out_dtype)
    return _call(
        order, body, (a, b), name=name, grid=(M // tm, N // tn), in_specs=[a_spec, b_spec],
        out_specs=out_spec, out_shape=out_shape,
        compiler_params=_params(("parallel", "parallel")),
    )


def _assemble_win(order, wins, fas):
    def body(win_ref, fa_ref, o_ref):
        q = pl.program_id(0)

        @pl.when(q == 0)
        def _():
            o_ref[...] = jnp.zeros_like(o_ref)

        for k in range(NCHIP):
            @pl.when(q == k)
            def _(k=k):
                for j in range(WIN_UNITS):
                    cu = WIN_UNIT0[k] + j
                    dst = pl.ds(C2I[cu] * UNIT, UNIT)
                    if cu in OVERLAP_UNITS:
                        o_ref[dst, :] += win_ref[j * UNIT:(j + 1) * UNIT, :]
                    else:
                        o_ref[dst, :] = win_ref[j * UNIT:(j + 1) * UNIT, :]
                if k == 1:
                    o_ref[F_FA:F_FA + FA_ROWS, :] = fa_ref[...]

    return _call(
        order, body, (wins, fas), name="assemble_w_in", grid=(NCHIP,),
        in_specs=[pl.BlockSpec((None, WIN_ROWS, D), lambda q: (q, 0, 0)),
                  pl.BlockSpec((None, FA_ROWS, D), lambda q: (1, 0, 0))],
        out_specs=pl.BlockSpec((NP, D), lambda q: (0, 0)),
        out_shape=jax.ShapeDtypeStruct((NP, D), bf16),
        compiler_params=_params(("arbitrary",)),
    )


def _norm_inproj(order, x, g1, wt, rope):
    tm = 256
    c_t, s1_t, s2_t = rope

    def body(x_ref, g_ref, w_ref, c_ref, s1_ref, s2_ref, h_ref, qkvb_ref, qkva_ref, gates_ref, fa_ref):
        xb = x_ref[...]
        r = lax.rsqrt(jnp.mean(xb * xb, axis=-1, keepdims=True) + EPS)
        h = ((xb * r) * g_ref[...]).astype(bf16)
        h_ref[...] = h
        c, s1, s2 = c_ref[...], s1_ref[...], s2_ref[...]
        for p in range(2):
            pb = _dot_nt(h, w_ref[F_DIL + p * DIL_BLK:F_DIL + (p + 1) * DIL_BLK, :])
            for ch in range(DIL_BLK // 128):
                pc = pb[:, ch * 128:(ch + 1) * 128]
                if ch < 6:
                    pc = pc * c + pltpu.roll(pc, 120, 1) * s1 + pltpu.roll(pc, 8, 1) * s2
                qkvb_ref[:, p * DIL_BLK + ch * 128:p * DIL_BLK + (ch + 1) * 128] = pc
        qkva_ref[...] = _dot_nt(h, w_ref[F_FOX:F_FA, :]).astype(bf16)
        fa_ref[...] = _dot_nt(h, w_ref[F_FA:F_FA + 128, :])
        gates_ref[...] = _dot_nt(h, w_ref[F_G:NP, :]).astype(bf16)

    row = lambda w: pl.BlockSpec((tm, w), lambda i: (i, 0))
    return _call(
        order, body, (x, g1, wt, c_t, s1_t, s2_t), name="norm_inproj", grid=(S // tm,),
        in_specs=[row(D), pl.BlockSpec((1, D), lambda i: (0, 0)), pl.BlockSpec((NP, D), lambda i: (0, 0)),
                  row(128), row(128), row(128)],
        out_specs=[row(D), row(2 * DIL_BLK), row(4 * FOX_BLK), row(2 * D), row(128)],
        out_shape=[jax.ShapeDtypeStruct((S, D), bf16), jax.ShapeDtypeStruct((S, 2 * DIL_BLK), f32),
                   jax.ShapeDtypeStruct((S, 4 * FOX_BLK), bf16), jax.ShapeDtypeStruct((S, 2 * D), bf16),
                   jax.ShapeDtypeStruct((S, 128), f32)],
        compiler_params=_params(("parallel",)),
    )


def _forget_cumsum(order, fa, bpad):
    nb = S // TQ

    def body(fa_ref, b_ref, F_ref):
        rr = lax.broadcasted_iota(jnp.int32, (TQ, TQ), 0)
        cc = lax.broadcasted_iota(jnp.int32, (TQ, TQ), 1)
        tri = (rr >= cc).astype(bf16)
        lane = lax.broadcasted_iota(jnp.int32, (1, 128), 1)
        carry = jnp.zeros((1, 128), f32)
        for b in range(nb):
            z = fa_ref[b * TQ:(b + 1) * TQ, :] + b_ref[...]
            lf = jnp.minimum(z, 0.0) - jnp.log(1.0 + jnp.exp(-jnp.abs(z)))
            lf = jnp.where(lane < 8, lf, 0.0)
            hi, mid, lo = _split3(lf)
            fb = (_dot(tri, hi) + _dot(tri, mid)) + _dot(tri, lo) + carry
            F_ref[b * TQ:(b + 1) * TQ, :] = fb
            carry = fb[TQ - 1:TQ, :]

    return _call(
        order, body, (fa, bpad), name="forget_cumsum",
        out_shape=jax.ShapeDtypeStruct((S, 128), f32),
        compiler_params=_params(),
    )


def _head_masks():
    lane = lax.broadcasted_iota(jnp.int32, (1, 128), 1)
    return lane, (lane < HD, lane >= HD)


L_ONE = 3
FOX_TQ, FOX_TK = 256, 512


def _set_lanes(x, lane, first, cols):
    for n, col in enumerate(cols):
        x = jnp.where(lane == first + n, col, x)
    return x


def _f32_parts(col):
    return [t.astype(f32) for t in _split3(col)]


def _fox_operands(qkv_ref, F_ref, lse_ref, qa, ka, p, rows):
    lane, hm = _head_masks()
    q = qkv_ref[rows, 0:128].astype(f32) * 0.125
    k = qkv_ref[rows, 128:256].astype(f32)
    Fb = F_ref[rows, :]
    for hh in (0, 1):
        free = (1 - hh) * HD
        fcol = jnp.sum(jnp.where(lane == 2 * p + hh, Fb, 0.0), axis=1, keepdims=True)
        qterm = fcol if lse_ref is None else fcol - lse_ref[rows, hh * HD:hh * HD + 1]
        qcols = _f32_parts(qterm) + [1.0] * 3
        kcols = [1.0] * 3 + [-t for t in _f32_parts(fcol)]
        qa[hh, rows, :] = _set_lanes(jnp.where(hm[hh], q, 0.0), lane, free, qcols).astype(bf16)
        ka[hh, rows, :] = _set_lanes(k, lane, free, kcols).astype(bf16)


def _fox_fwd(order, qkva, F):
    tq, tk = FOX_TQ, FOX_TK

    def body(qkv_ref, F_ref, o_ref, lse_ref, qa, ka, vt):
        p = pl.program_id(0)
        keyi = lax.broadcasted_iota(jnp.int32, (tk, 1), 0)
        qryi = lax.broadcasted_iota(jnp.int32, (1, tq), 1)
        sub = lax.broadcasted_iota(jnp.int32, (128, 1), 0)

        def prep(i, c):
            rows = pl.ds(pl.multiple_of(i * tk, tk), tk)
            _fox_operands(qkv_ref, F_ref, None, qa, ka, p, rows)
            vt[i] = qkv_ref[rows, 256:384].astype(f32).T.astype(bf16)
            return c

        lax.fori_loop(0, S // tk, prep, 0)

        def qblock(i, first_half):
            r0 = pl.multiple_of(i * tq, tq)
            qh = [qa[hh, pl.ds(r0, tq), :] for hh in (0, 1)]

            def kv(jb, carry, masked, width):
                keys = pl.ds(pl.multiple_of(jb * tk, tk), width)
                sts = [_dot_nt(ka[hh, keys, :], qh[hh]) for hh in (0, 1)]
                new = []
                for hh in (0, 1):
                    m, l, a = carry[3 * hh:3 * hh + 3]
                    st = sts[hh]
                    if masked:
                        st = jnp.where(jb * tk + keyi[0:width] <= r0 + qryi, st, NEG)
                    mn = jnp.maximum(m, jnp.max(st, axis=0, keepdims=True))
                    al = jnp.exp(m - mn)
                    pt = jnp.exp(st - mn)
                    l = al * l + jnp.sum(pt, axis=0, keepdims=True)
                    a = al * a + _dot(vt[jb, hh * HD:(hh + 1) * HD, 0:width], pt.astype(bf16))
                    new += [mn, l, a]
                return tuple(new)

            init = (jnp.full((1, tq), NEG, f32), jnp.zeros((1, tq), f32), jnp.zeros((HD, tq), f32)) * 2
            last = (r0 + tq - 1) // tk
            carry = lax.fori_loop(0, last, lambda j, cr: kv(j, cr, False, tk), init)
            m0, l0, a0, m1, l1, a1 = kv(last, carry, True, tk // 2 if first_half else tk)
            ot = jnp.concatenate([a0 / l0, a1 / l1], axis=0)
            lt = jnp.where(sub < HD, m0 + jnp.log(l0), m1 + jnp.log(l1))
            o_ref[pl.ds(r0, tq), :] = ot.T.astype(bf16)
            lse_ref[pl.ds(r0, tq), :] = lt.T

        def qpair(t, c):
            qblock(2 * t, True)
            qblock(2 * t + 1, False)
            return c

        assert tk == 2 * tq
        lax.fori_loop(0, S // tk, qpair, 0)

    pair = pl.BlockSpec((S, 128), lambda p: (0, p))
    return _call(
        order, body, (qkva, F), name="fox_fwd", grid=(4,),
        in_specs=[pl.BlockSpec((S, FOX_BLK), lambda p: (0, p)), pl.BlockSpec((S, 128), lambda p: (0, 0))],
        out_specs=[pair, pair],
        out_shape=[jax.ShapeDtypeStruct((S, FOXW), bf16), jax.ShapeDtypeStruct((S, FOXW), f32)],
        scratch_shapes=[pltpu.VMEM((2, S, 128), bf16)] * 2 + [pltpu.VMEM((S // tk, 128, tk), bf16)],
        compiler_params=_params(("parallel",)),
    )


def _permute_in(dst, src, r):
    L = S // r
    for rho in range(r):
        dst[rho * L:(rho + 1) * L, :] = src[pl.ds(rho, L, stride=r), :]


def _permute_out(dst, src, r):
    L = S // r
    for rho in range(r):
        dst[pl.ds(rho, L, stride=r), :] = src[rho * L:(rho + 1) * L, :]


def _band_width(nbl):
    return BAND if nbl == 1 else 2 * BAND


def _band_geometry(bb, nbl):
    r0 = pl.multiple_of(bb * BAND, BAND)
    if nbl == 1:
        k0 = r0
    else:
        k0 = pl.multiple_of(jnp.maximum(bb - 1, 0) * BAND, BAND)
    sub0 = (bb - lax.rem(bb, nbl)) * BAND
    qi = r0 + lax.broadcasted_iota(jnp.int32, (BAND, 1), 0)
    ki = k0 + lax.broadcasted_iota(jnp.int32, (1, _band_width(nbl)), 1)
    diff = qi - ki
    valid = (diff >= 0) & (diff <= BAND) & (ki >= sub0)
    return r0, k0, valid


def _dil_views(ref):
    return [[ref.at[:, pl.ds((3 * role + g) * 128, 128)] for g in range(3)] for role in range(3)]


DIL_UNROLL = 4


def _dil_in_specs():
    return [pl.BlockSpec((S, 128), lambda p, k=k: (0, 9 * p + k)) for k in range(9)]


def _dil_fwd(order, qkvb):
    def body(*refs):
        q_refs, k_refs, v_refs = refs[0:3], refs[3:6], refs[6:9]
        ob_ref, lse_ref, qp, kp, vp, op, lp = refs[9:16]
        on, ln = refs[16:19], refs[19:22]
        _, hm = _head_masks()
        for g, r in enumerate(DIL):
            nbl = S // r // BAND
            if r == 1:
                qs_, ks_, vs_, od, ld = q_refs[g], k_refs[g], v_refs[g], on[g], ln[g]
            else:
                _permute_in(qp, q_refs[g], r)
                _permute_in(kp, k_refs[g], r)
                _permute_in(vp, v_refs[g], r)
                qs_, ks_, vs_, od, ld = qp, kp, vp, op, lp

            def blk(t, c, qs_=qs_, ks_=ks_, vs_=vs_, od=od, ld=ld, nbl=nbl):
                work = []
                for u in range(DIL_UNROLL):
                    r0, k0, valid = _band_geometry(DIL_UNROLL * t + u, nbl)
                    q = qs_[pl.ds(r0, BAND), :] * 0.125
                    kw = ks_[pl.ds(k0, _band_width(nbl)), :].astype(bf16)
                    vw = vs_[pl.ds(k0, _band_width(nbl)), :]
                    for hh in (0, 1):
                        qh = jnp.where(hm[hh], q, 0.0).astype(bf16)
                        work.append((u, hh, r0, valid, vw, _dot_nt(qh, kw)))
                o = [jnp.zeros((BAND, 128), f32)] * DIL_UNROLL
                lse = [jnp.zeros((BAND, 128), f32)] * DIL_UNROLL
                for u, hh, r0, valid, vw, s in work:
                    s = jnp.where(valid, s, NEG)
                    m = jnp.max(s, axis=1, keepdims=True)
                    pr = jnp.exp(s - m)
                    l = jnp.sum(pr, axis=1, keepdims=True)
                    vm = jnp.where(hm[hh], vw, 0.0).astype(bf16)
                    o[u] = o[u] + _dot((pr / l).astype(bf16), vm)
                    lse[u] = jnp.where(hm[hh], m + jnp.log(l), lse[u])
                    if hh == 1:
                        od[pl.ds(r0, BAND), :] = o[u]
                        ld[pl.ds(r0, BAND), :] = lse[u]
                return c

            lax.fori_loop(0, S // BAND // DIL_UNROLL, blk, 0)
            if r != 1:
                _permute_out(on[g], op, r)
                _permute_out(ln[g], lp, r)

        def combine(i, c):
            r0 = pl.multiple_of(i * TQ, TQ)
            ls = [ln[g][pl.ds(r0, TQ), :] for g in range(3)]
            mx = jnp.maximum(jnp.maximum(ls[0], ls[1]), ls[2])
            es = [jnp.exp(l - mx) for l in ls]
            tot = (es[0] + es[1]) + es[2]
            acc = (es[0] / tot) * on[0][pl.ds(r0, TQ), :]
            acc = acc + (es[1] / tot) * on[1][pl.ds(r0, TQ), :]
            acc = acc + (es[2] / tot) * on[2][pl.ds(r0, TQ), :]
            ob_ref[pl.ds(r0, TQ), :] = acc.astype(bf16)
            lse_ref[pl.ds(r0, TQ), :] = mx + jnp.log(tot)
            return c

        lax.fori_loop(0, S // TQ, combine, 0)

    out_blk = pl.BlockSpec((S, 128), lambda p: (0, p))
    return _call(
        order, body, [qkvb] * 9, name="dil_fwd", grid=(2,),
        in_specs=_dil_in_specs(), out_specs=[out_blk, out_blk],
        out_shape=[jax.ShapeDtypeStruct((S, DILOUT), bf16), jax.ShapeDtypeStruct((S, DILOUT), f32)],
        scratch_shapes=[pltpu.VMEM((S, 128), f32)] * 11,
        compiler_params=_params(("parallel",)),
    )


def _branch_mix(order, oa, ob, was, wbs, gates):
    tm = 512

    def body(oa_ref, ob_ref, wa_ref, wb_ref, g_ref, ya_ref, yb_ref, mix_ref):
        oa_b, ob_b = oa_ref[...], ob_ref[...]
        for q in range(NCHIP):
            cols = slice(q * 256, (q + 1) * 256)
            ya = _dot(oa_b, wa_ref[q])
            yb = _dot(ob_b, wb_ref[q])
            ya_ref[:, cols] = ya.astype(bf16)
            yb_ref[:, cols] = yb.astype(bf16)
            ga = g_ref[:, q * 256:(q + 1) * 256].astype(f32)
            gb = g_ref[:, D + q * 256:D + (q + 1) * 256].astype(f32)
            mix_ref[:, cols] = (jax.nn.sigmoid(ga) * ya + jax.nn.sigmoid(gb) * yb).astype(bf16)

    row = lambda w: pl.BlockSpec((tm, w), lambda i: (i, 0))
    full3 = lambda a: pl.BlockSpec(a.shape, lambda i: (0, 0, 0))
    return _call(
        order, body, (oa, ob, was, wbs, gates), name="branch_mix", grid=(S // tm,),
        in_specs=[row(FOXW), row(DILOUT), full3(was), full3(wbs), row(2 * D)],
        out_specs=[row(D), row(D), row(D)],
        out_shape=[jax.ShapeDtypeStruct((S, D), bf16), jax.ShapeDtypeStruct((S, D), bf16),
                   jax.ShapeDtypeStruct((S, D), bf16)],
        compiler_params=_params(("parallel",)),
    )


def _outproj_norm(order, mixed, wout, x, g2):
    tm = 512

    def body(m_ref, w_ref, x_ref, g_ref, x2_ref, h2_ref):
        x2 = x_ref[...] + _dot(m_ref[...], w_ref[...])
        x2_ref[...] = x2
        r = lax.rsqrt(jnp.mean(x2 * x2, axis=-1, keepdims=True) + EPS)
        h2_ref[...] = ((x2 * r) * g_ref[...]).astype(bf16)

    row = pl.BlockSpec((tm, D), lambda i: (i, 0))
    return _call(
        order, body, (mixed, wout, x, g2), name="outproj_norm", grid=(S // tm,),
        in_specs=[row, pl.BlockSpec((D, D), lambda i: (0, 0)), row, pl.BlockSpec((1, D), lambda i: (0, 0))],
        out_specs=[row, row],
        out_shape=[jax.ShapeDtypeStruct((S, D), f32), jax.ShapeDtypeStruct((S, D), bf16)],
        compiler_params=_params(("parallel",)),
    )


def _mlp_up(order, h2, wups):
    tm = 1024

    def body(h_ref, w_ref, ru_ref, a_ref):
        ru = jnp.maximum(_dot(h_ref[...], w_ref[...]), 0.0)
        ru_ref[...] = ru.astype(bf16)
        a_ref[...] = (ru * ru).astype(bf16)

    out = pl.BlockSpec((tm, D), lambda q, i: (i, q))
    return _call(
        order, body, (h2, wups), name="mlp_up", grid=(NCHIP, S // tm),
        in_specs=[pl.BlockSpec((tm, D), lambda q, i: (i, 0)), pl.BlockSpec((None, D, D), lambda q, i: (q, 0, 0))],
        out_specs=[out, out],
        out_shape=[jax.ShapeDtypeStruct((S, DFF), bf16), jax.ShapeDtypeStruct((S, DFF), bf16)],
        compiler_params=_params(("parallel", "parallel")),
    )


def _mlp_down_loss(order, a, wdown, x2, g3, tgt):
    tm = 512

    def body(a_ref, w_ref, x2_ref, g_ref, t_ref, dx_ref, dxb_ref, dg_ref, loss_ref):
        i = pl.program_id(0)
        x3 = x2_ref[...] + _dot(a_ref[...], w_ref[...])
        r = lax.rsqrt(jnp.mean(x3 * x3, axis=-1, keepdims=True) + EPS)
        xh = x3 * r
        g = g_ref[...]
        e = xh * g - t_ref[...]
        part = 0.5 * jnp.sum(jnp.mean(e * e, axis=-1, keepdims=True), axis=0, keepdims=True)
        dy = e * (1.0 / D)
        gdy = dy * g
        dx = r * (gdy - xh * jnp.mean(gdy * xh, axis=-1, keepdims=True))
        dx_ref[...] = dx
        dxb_ref[...] = dx.astype(bf16)

        @pl.when(i == 0)
        def _():
            dg_ref[...] = jnp.zeros_like(dg_ref)
            loss_ref[...] = jnp.zeros_like(loss_ref)

        dg_ref[...] += jnp.sum(dy * xh, axis=0, keepdims=True)
        loss_ref[...] += jnp.broadcast_to(part, (1, 128))

    row = pl.BlockSpec((tm, D), lambda i: (i, 0))
    vec = pl.BlockSpec((1, D), lambda i: (0, 0))
    return _call(
        order, body, (a, wdown, x2, g3, tgt), name="mlp_down_loss", grid=(S // tm,),
        in_specs=[pl.BlockSpec((tm, DFF), lambda i: (i, 0)), pl.BlockSpec((DFF, D), lambda i: (0, 0)), row, vec, row],
        out_specs=[row, row, vec, pl.BlockSpec((1, 128), lambda i: (0, 0))],
        out_shape=[jax.ShapeDtypeStruct((S, D), f32), jax.ShapeDtypeStruct((S, D), bf16),
                   jax.ShapeDtypeStruct((1, D), f32), jax.ShapeDtypeStruct((1, 128), f32)],
        compiler_params=_params(("arbitrary",)),
    )


def _mlp_down_bwd(order, dx3b, wdown, u):
    tm = 512

    def body(d_ref, w_ref, u_ref, du_ref):
        d = d_ref[...]
        for q in range(NCHIP):
            cols = slice(q * D, (q + 1) * D)
            da = _dot_nt(d, w_ref[cols, :])
            du_ref[:, cols] = (da * (2.0 * u_ref[:, cols].astype(f32))).astype(bf16)

    return _call(
        order, body, (dx3b, wdown, u), name="mlp_down_bwd", grid=(S // tm,),
        in_specs=[pl.BlockSpec((tm, D), lambda i: (i, 0)), pl.BlockSpec((DFF, D), lambda i: (0, 0)),
                  pl.BlockSpec((tm, DFF), lambda i: (i, 0))],
        out_specs=pl.BlockSpec((tm, DFF), lambda i: (i, 0)),
        out_shape=jax.ShapeDtypeStruct((S, DFF), bf16),
        compiler_params=_params(("parallel",)),
    )


def _mlp_up_bwd(order, du, wups, x2, dx3, g2):
    tm = 512

    def body(du_ref, w_ref, x2_ref, dx3_ref, g_ref, dx2_ref, dx2b_ref, dg_ref):
        i = pl.program_id(0)
        dh = jnp.zeros((tm, D), f32)
        for q in range(NCHIP):
            dh = dh + _dot_nt(du_ref[:, q * D:(q + 1) * D], w_ref[q])
        x2 = x2_ref[...]
        r = lax.rsqrt(jnp.mean(x2 * x2, axis=-1, keepdims=True) + EPS)
        xh = x2 * r
        gdh = dh * g_ref[...]
        dx2 = dx3_ref[...] + r * (gdh - xh * jnp.mean(gdh * xh, axis=-1, keepdims=True))
        dx2_ref[...] = dx2
        dx2b_ref[...] = dx2.astype(bf16)

        @pl.when(i == 0)
        def _():
            dg_ref[...] = jnp.zeros_like(dg_ref)

        dg_ref[...] += jnp.sum(dh * xh, axis=0, keepdims=True)

    row = pl.BlockSpec((tm, D), lambda i: (i, 0))
    vec = pl.BlockSpec((1, D), lambda i: (0, 0))
    return _call(
        order, body, (du, wups, x2, dx3, g2), name="mlp_up_bwd", grid=(S // tm,),
        in_specs=[pl.BlockSpec((tm, DFF), lambda i: (i, 0)), pl.BlockSpec((NCHIP, D, D), lambda i: (0, 0, 0)),
                  row, row, vec],
        out_specs=[row, row, vec],
        out_shape=[jax.ShapeDtypeStruct((S, D), f32), jax.ShapeDtypeStruct((S, D), bf16),
                   jax.ShapeDtypeStruct((1, D), f32)],
        compiler_params=_params(("arbitrary",)),
    )


def _gate_bwd(order, dx2b, wout, gates, ya, yb):
    tm = 512

    def body(d_ref, w_ref, g_ref, ya_ref, yb_ref, dya_ref, dyb_ref, dproj_ref):
        dm = _dot_nt(d_ref[...], w_ref[...])
        sa = jax.nn.sigmoid(g_ref[:, 0:D].astype(f32))
        sb = jax.nn.sigmoid(g_ref[:, D:2 * D].astype(f32))
        dya_ref[...] = (dm * sa).astype(bf16)
        dyb_ref[...] = (dm * sb).astype(bf16)
        dproj_ref[:, 0:D] = (dm * ya_ref[...].astype(f32) * (sa * (1.0 - sa))).astype(bf16)
        dproj_ref[:, D:2 * D] = (dm * yb_ref[...].astype(f32) * (sb * (1.0 - sb))).astype(bf16)

    row = lambda w: pl.BlockSpec((tm, w), lambda i: (i, 0))
    return _call(
        order, body, (dx2b, wout, gates, ya, yb), name="gate_bwd", grid=(S // tm,),
        in_specs=[row(D), pl.BlockSpec((D, D), lambda i: (0, 0)), row(2 * D), row(D), row(D)],
        out_specs=[row(D), row(D), pl.BlockSpec((tm, 2 * D), lambda i: (i, F_G // (2 * D)))],
        out_shape=[jax.ShapeDtypeStruct((S, D), bf16), jax.ShapeDtypeStruct((S, D), bf16),
                   jax.ShapeDtypeStruct((S, NP), bf16)],
        compiler_params=_params(("parallel",)),
    )


def _branch_bwd(order, dya, dyb, was, wbs):
    tm = 512

    def body(dya_ref, dyb_ref, wa_ref, wb_ref, doa_ref, dob_ref):
        doa = jnp.zeros((tm, FOXW), f32)
        dob = jnp.zeros((tm, DILOUT), f32)
        for q in range(NCHIP):
            cols = slice(q * 256, (q + 1) * 256)
            doa = doa + _dot_nt(dya_ref[:, cols], wa_ref[q])
            dob = dob + _dot_nt(dyb_ref[:, cols], wb_ref[q])
        doa_ref[...] = doa.astype(bf16)
        dob_ref[...] = dob

    row = lambda w: pl.BlockSpec((tm, w), lambda i: (i, 0))
    full3 = lambda a: pl.BlockSpec(a.shape, lambda i: (0, 0, 0))
    return _call(
        order, body, (dya, dyb, was, wbs), name="branch_bwd", grid=(S // tm,),
        in_specs=[row(D), row(D), full3(was), full3(wbs)],
        out_specs=[row(FOXW), row(DILOUT)],
        out_shape=[jax.ShapeDtypeStruct((S, FOXW), bf16), jax.ShapeDtypeStruct((S, DILOUT), f32)],
        compiler_params=_params(("parallel",)),
    )


def _branch_wgrad(order, oa, ob, dya, dyb):
    def body(oa_ref, ob_ref, dya_ref, dyb_ref, dwa_ref, dwb_ref):
        dwa_ref[...] = _dot_tn(oa_ref[...], dya_ref[...])
        dwb_ref[...] = _dot_tn(ob_ref[...], dyb_ref[...])

    full = lambda w: pl.BlockSpec((S, w), lambda q: (0, 0))
    colq = pl.BlockSpec((S, 256), lambda q: (0, q))
    return _call(
        order, body, (oa, ob, dya, dyb), name="branch_wgrad", grid=(NCHIP,),
        in_specs=[full(FOXW), full(DILOUT), colq, colq],
        out_specs=[pl.BlockSpec((None, FOXW, 256), lambda q: (q, 0, 0)),
                   pl.BlockSpec((None, DILOUT, 256), lambda q: (q, 0, 0))],
        out_shape=[jax.ShapeDtypeStruct((NCHIP, FOXW, 256), f32), jax.ShapeDtypeStruct((NCHIP, DILOUT, 256), f32)],
        compiler_params=_params(("parallel",)),
    )


def _fox_bwd(order, qkva, doa, oa, lse, F, dproj):
    tq, tk = FOX_TQ, FOX_TK

    def body(qkv_ref, do_ref, o_ref, lse_ref, F_ref, _dproj_in, dF_ref, dqkv_ref, qa, ka, da, va, kat,
             dk_scr, dv_scr, dqt_scr):
        p = pl.program_id(0)
        lane, hm = _head_masks()
        keyi = lax.broadcasted_iota(jnp.int32, (tk, 1), 0)
        qryi = lax.broadcasted_iota(jnp.int32, (1, tq), 1)

        def prep(i, c):
            rows = pl.ds(pl.multiple_of(i * tk, tk), tk)
            _fox_operands(qkv_ref, F_ref, lse_ref, qa, ka, p, rows)
            do = do_ref[rows, :].astype(f32)
            prod = do * o_ref[rows, :].astype(f32)
            v = qkv_ref[rows, 256:384].astype(f32)
            for hh in (0, 1):
                free = (1 - hh) * HD
                delta = jnp.sum(jnp.where(hm[hh], prod, 0.0), axis=1, keepdims=True)
                da[hh, rows, :] = _set_lanes(jnp.where(hm[hh], do, 0.0), lane, free,
                                             [-t for t in _f32_parts(delta)]).astype(bf16)
                va[hh, rows, :] = _set_lanes(v, lane, free, [1.0] * 3).astype(bf16)
                kat[hh, i] = ka[hh, rows, :].astype(f32).T.astype(bf16)
                dk_scr[hh, rows, :] = jnp.zeros((tk, 128), f32)
                dv_scr[hh, rows, :] = jnp.zeros((tk, 128), f32)
            return c

        lax.fori_loop(0, S // tk, prep, 0)

        def qblock(i, first_half):
            r0 = pl.multiple_of(i * tq, tq)
            qrows = pl.ds(r0, tq)
            qh = [qa[hh, qrows, :] for hh in (0, 1)]
            dh = [da[hh, qrows, :] for hh in (0, 1)]
            dqt_scr[...] = jnp.zeros_like(dqt_scr)

            def kv(jb, c2, masked, width):
                keys = pl.ds(pl.multiple_of(jb * tk, tk), width)
                sts = [_dot_nt(ka[hh, keys, :], qh[hh]) for hh in (0, 1)]
                dps = [_dot_nt(va[hh, keys, :], dh[hh]) for hh in (0, 1)]
                for hh in (0, 1):
                    pt = jnp.exp(sts[hh])
                    if masked:
                        pt = jnp.where(jb * tk + keyi[0:width] <= r0 + qryi, pt, 0.0)
                    dsb = (pt * dps[hh]).astype(bf16)
                    dv_scr[hh, keys, :] += _dot(pt.astype(bf16), dh[hh])
                    dk_scr[hh, keys, :] += _dot(dsb, qh[hh])
                    dqt_scr[hh] += _dot(kat[hh, jb, :, 0:width], dsb)
                return c2

            last = (r0 + tq - 1) // tk
            lax.fori_loop(0, last, lambda j, c2: kv(j, c2, False, tk), 0)
            kv(last, 0, True, tk // 2 if first_half else tk)
            dq0, dq1 = dqt_scr[0].T, dqt_scr[1].T
            dqkv_ref[qrows, 0:128] = (jnp.where(hm[0], dq0, dq1) * 0.125).astype(bf16)
            dF_ref[qrows, :] = jnp.where(lane == 0, dq0[:, HD:HD + 1], jnp.where(lane == 1, dq1[:, 0:1], 0.0))

        def qpair(t, c):
            qblock(2 * t, True)
            qblock(2 * t + 1, False)
            return c

        assert tk == 2 * tq
        lax.fori_loop(0, S // tk, qpair, 0)

        def finish(i, c):
            rows = pl.ds(pl.multiple_of(i * tq, tq), tq)
            dk0, dk1 = dk_scr[0, rows, :], dk_scr[1, rows, :]
            dqkv_ref[rows, 128:256] = jnp.where(hm[0], dk0, dk1).astype(bf16)
            dqkv_ref[rows, 256:384] = jnp.where(hm[0], dv_scr[0, rows, :], dv_scr[1, rows, :]).astype(bf16)
            cs = jnp.where(lane == 0, dk0[:, HD + L_ONE:HD + L_ONE + 1],
                           jnp.where(lane == 1, dk1[:, L_ONE:L_ONE + 1], 0.0))
            dF_ref[rows, :] = dF_ref[rows, :] - cs
            return c

        lax.fori_loop(0, S // tq, finish, 0)

    pair = pl.BlockSpec((S, 128), lambda p: (0, p))
    return _call(
        order, body, (qkva, doa, oa, lse, F, dproj), name="fox_bwd", grid=(4,),
        in_specs=[pl.BlockSpec((S, FOX_BLK), lambda p: (0, p)), pair, pair, pair,
                  pl.BlockSpec((S, 128), lambda p: (0, 0)), pl.BlockSpec(memory_space=pl.ANY)],
        out_specs=[pair, pl.BlockSpec((S, FOX_BLK), lambda p: (0, F_FOX // FOX_BLK + p))],
        out_shape=[jax.ShapeDtypeStruct((S, FOXW), f32), jax.ShapeDtypeStruct((S, NP), bf16)],
        input_output_aliases={5: 1},
        scratch_shapes=[pltpu.VMEM((2, S, 128), bf16)] * 4 + [pltpu.VMEM((2, S // tk, 128, tk), bf16)]
        + [pltpu.VMEM((2, S, 128), f32)] * 2 + [pltpu.VMEM((2, 128, tq), f32)],
        compiler_params=_params(("parallel",)),
    )


def _forget_bwd(order, dF, fa, bpad, dproj):
    nb = S // TQ

    def body(dF_ref, fa_ref, b_ref, _dproj_in, db_ref, dfa_ref):
        rr = lax.broadcasted_iota(jnp.int32, (TQ, TQ), 0)
        cc = lax.broadcasted_iota(jnp.int32, (TQ, TQ), 1)
        upper = (cc >= rr).astype(bf16)
        lane = lax.broadcasted_iota(jnp.int32, (1, 128), 1)
        carry = jnp.zeros((1, 128), f32)
        db = jnp.zeros((1, 128), f32)
        for b in reversed(range(nb)):
            cols = jnp.zeros((TQ, 128), f32)
            for h in range(8):
                c0 = (h // 2) * 128 + h % 2
                cols = jnp.where(lane == h, dF_ref[b * TQ:(b + 1) * TQ, c0:c0 + 1], cols)
            dlf = carry
            for part in _split3(cols):
                dlf = dlf + _dot(upper, part)
            carry = carry + jnp.sum(cols, axis=0, keepdims=True)
            z = fa_ref[b * TQ:(b + 1) * TQ, :] + b_ref[...]
            dz = jnp.where(lane < 8, dlf * jax.nn.sigmoid(-z), 0.0)
            dfa_ref[b * TQ:(b + 1) * TQ, 0:128] = dz.astype(bf16)
            dfa_ref[b * TQ:(b + 1) * TQ, 128:256] = jnp.zeros((TQ, 128), bf16)
            db = db + jnp.sum(dz, axis=0, keepdims=True)
        db_ref[...] = db

    whole = lambda a: pl.BlockSpec(a.shape, lambda i: (0,) * a.ndim)
    return _call(
        order, body, (dF, fa, bpad, dproj), name="forget_bwd", grid=(1,),
        in_specs=[whole(dF), whole(fa), whole(bpad), pl.BlockSpec(memory_space=pl.ANY)],
        out_specs=[pl.BlockSpec((1, 128), lambda i: (0, 0)), pl.BlockSpec((S, 256), lambda i: (0, F_FA // 256))],
        out_shape=[jax.ShapeDtypeStruct((1, 128), f32), jax.ShapeDtypeStruct((S, NP), bf16)],
        input_output_aliases={3: 1},
        compiler_params=_params(("arbitrary",)),
    )


def _dil_bwd(order, qkvb, dob, ob, lseb, rope, dproj):
    c_t, s1_t, s2_t = rope

    def body(*refs):
        q_refs, k_refs, v_refs = refs[0:3], refs[3:6], refs[6:9]
        dob_ref, ob_ref, lse_ref, c_ref, s1_ref, s2_ref, _dproj_in, dqkv_ref = refs[9:17]
        qp, kp, vp, dop, lp, dlp, dln, dqp, dkp, dvp, nat = refs[17:28]
        dq_out, dk_out, dv_out = _dil_views(dqkv_ref)
        _, hm = _head_masks()

        def delta_rows(i, c):
            r0 = pl.multiple_of(i * TQ, TQ)
            prod = dob_ref[pl.ds(r0, TQ), :] * ob_ref[pl.ds(r0, TQ), :].astype(f32)
            d0 = jnp.sum(jnp.where(hm[0], prod, 0.0), axis=1, keepdims=True)
            d1 = jnp.sum(jnp.where(hm[1], prod, 0.0), axis=1, keepdims=True)
            dln[pl.ds(r0, TQ), :] = jnp.where(hm[0], d0, d1)
            return c

        lax.fori_loop(0, S // TQ, delta_rows, 0)

        for g, r in enumerate(DIL):
            nbl = S // r // BAND
            if r == 1:
                srcs = (q_refs[g], k_refs[g], v_refs[g], dob_ref, lse_ref, dln)
            else:
                for dst, src in ((qp, q_refs[g]), (kp, k_refs[g]), (vp, v_refs[g]), (dop, dob_ref),
                                 (lp, lse_ref), (dlp, dln)):
                    _permute_in(dst, src, r)
                srcs = (qp, kp, vp, dop, lp, dlp)
            dkp[...] = jnp.zeros_like(dkp)
            dvp[...] = jnp.zeros_like(dvp)

            def blk(t, c, srcs=srcs, nbl=nbl):
                qs_, ks_, vs_, dos_, ls_, dls_ = srcs
                work = []
                for u in range(DIL_UNROLL):
                    r0, k0, valid = _band_geometry(DIL_UNROLL * t + u, nbl)
                    q = qs_[pl.ds(r0, BAND), :] * 0.125
                    kwf = ks_[pl.ds(k0, _band_width(nbl)), :]
                    kw = kwf.astype(bf16)
                    vw = vs_[pl.ds(k0, _band_width(nbl)), :].astype(bf16)
                    do = dos_[pl.ds(r0, BAND), :]
                    lse = ls_[pl.ds(r0, BAND), :]
                    dlt = dls_[pl.ds(r0, BAND), :]
                    for hh in (0, 1):
                        qh = jnp.where(hm[hh], q, 0.0).astype(bf16)
                        doh = jnp.where(hm[hh], do, 0.0).astype(bf16)
                        kh = jnp.where(hm[hh], kwf, 0.0).astype(bf16)
                        work.append((u, hh, r0, k0, valid, qh, doh, kh, lse[:, hh * HD:hh * HD + 1],
                                     dlt[:, hh * HD:hh * HD + 1], _dot_nt(qh, kw), _dot_nt(doh, vw)))
                for u, hh, r0, k0, valid, qh, doh, kh, lse_h, dlt_h, s, dp in work:
                    if hh == 0:
                        dq = jnp.zeros((BAND, 128), f32)
                        dk = jnp.zeros((_band_width(nbl), 128), f32)
                        dv = jnp.zeros((_band_width(nbl), 128), f32)
                    pr = jnp.where(valid, jnp.exp(s - lse_h), 0.0)
                    dsb = (pr * (dp - dlt_h)).astype(bf16)
                    dv = dv + _dot_tn(pr.astype(bf16), doh)
                    dk = dk + _dot_tn(dsb, qh)
                    dq = dq + _dot(dsb, kh)
                    if hh == 1:
                        dqp[pl.ds(r0, BAND), :] = dq * 0.125
                        dkp[pl.ds(k0, _band_width(nbl)), :] += dk
                        dvp[pl.ds(k0, _band_width(nbl)), :] += dv
                return c

            lax.fori_loop(0, S // BAND // DIL_UNROLL, blk, 0)

            for acc, out, roped in ((dqp, dq_out[g], True), (dkp, dk_out[g], True), (dvp, dv_out[g], False)):
                if r == 1:
                    src = acc
                else:
                    _permute_out(nat, acc, r)
                    src = nat

                def emit(i, c, src=src, out=out, roped=roped):
                    r0 = pl.multiple_of(i * TQ, TQ)
                    d = src[pl.ds(r0, TQ), :]
                    if roped:
                        d = (d * c_ref[pl.ds(r0, TQ), :] + pltpu.roll(d * s1_ref[pl.ds(r0, TQ), :], 8, 1)
                             + pltpu.roll(d * s2_ref[pl.ds(r0, TQ), :], 120, 1))
                    out[pl.ds(r0, TQ), :] = d.astype(bf16)
                    return c

                lax.fori_loop(0, S // TQ, emit, 0)

    pair = pl.BlockSpec((S, 128), lambda p: (0, p))
    tab = pl.BlockSpec((S, 128), lambda p: (0, 0))
    blk_spec = pl.BlockSpec((S, DIL_BLK), lambda p: (0, p))
    return _call(
        order, body, [qkvb] * 9 + [dob, ob, lseb, c_t, s1_t, s2_t, dproj], name="dil_bwd", grid=(2,),
        in_specs=_dil_in_specs() + [pair, pair, pair, tab, tab, tab, pl.BlockSpec(memory_space=pl.ANY)],
        out_specs=blk_spec,
        out_shape=jax.ShapeDtypeStruct((S, NP), bf16),
        input_output_aliases={15: 0},
        scratch_shapes=[pltpu.VMEM((S, 128), f32)] * 11,
        compiler_params=_params(("parallel",)),
    )


def _inproj_bwd(order, dproj, wt, x, dx2, g1):
    tm = 256

    def body(d_ref, w_ref, x_ref, dx2_ref, g_ref, dx_ref, dg_ref):
        i = pl.program_id(0)
        dh = _dot(d_ref[...], w_ref[...])
        xb = x_ref[...]
        r = lax.rsqrt(jnp.mean(xb * xb, axis=-1, keepdims=True) + EPS)
        xh = xb * r
        gdh = dh * g_ref[...]
        dx_ref[...] = dx2_ref[...] + r * (gdh - xh * jnp.mean(gdh * xh, axis=-1, keepdims=True))

        @pl.when(i == 0)
        def _():
            dg_ref[...] = jnp.zeros_like(dg_ref)

        dg_ref[...] += jnp.sum(dh * xh, axis=0, keepdims=True)

    row = pl.BlockSpec((tm, D), lambda i: (i, 0))
    vec = pl.BlockSpec((1, D), lambda i: (0, 0))
    return _call(
        order, body, (dproj, wt, x, dx2, g1), name="inproj_bwd", grid=(S // tm,),
        in_specs=[pl.BlockSpec((tm, NP), lambda i: (i, 0)), pl.BlockSpec((NP, D), lambda i: (0, 0)), row, row, vec],
        out_specs=[row, vec],
        out_shape=[jax.ShapeDtypeStruct((S, D), f32), jax.ShapeDtypeStruct((1, D), f32)],
        compiler_params=_params(("arbitrary",)),
    )


HBM = pl.BlockSpec(memory_space=pltpu.HBM)
SEM = pl.BlockSpec(memory_space=pltpu.SEMAPHORE)
SMALL_ROWS = 8


def _comm_call(name, body, bufs, order, sems_in=(), new_sems=(), behind=()):
    nb, ns, nn = len(bufs), len(sems_in), len(new_sems)
    extra = order.token_for(bufs) + list(behind)

    def kern(*refs):
        off = nb + ns + len(extra)
        body(refs[:nb], refs[nb:nb + ns], refs[off:off + nn])
        refs[-1][...] = jnp.zeros((8, 128), f32)

    res = pl.pallas_call(
        kern, name=name,
        in_specs=[HBM] * nb + [SEM] * ns + [pl.BlockSpec(memory_space=pl.ANY)] * len(extra),
        out_specs=[SEM] * nn + [HBM] * nb + [pl.BlockSpec(memory_space=pltpu.VMEM)],
        out_shape=[pltpu.SemaphoreType.DMA((k,)) for k in new_sems] + [pltpu.HBM(b.shape, b.dtype) for b in bufs]
        + [jax.ShapeDtypeStruct((8, 128), f32)],
        input_output_aliases={i: nn + i for i in range(nb)},
        compiler_params=pltpu.CompilerParams(has_side_effects=pltpu.SideEffectType.DATAFLOW_SIDE_EFFECTING),
    )(*[pltpu.with_memory_space_constraint(b, pltpu.HBM) for b in bufs], *sems_in, *extra)
    order.mark(res[-1])
    return list(res[:nn]), list(res[nn:nn + nb])


def _place():
    x, y, c = lax.axis_index("x"), lax.axis_index("y"), lax.axis_index("c")
    chips = [(1 - x, y), (x, 1 - y), (1 - x, 1 - y)]
    return x, y, c, chips


def _rcopy(src, dst, ssem, rsem, dev):
    return pltpu.make_async_remote_copy(src_ref=src, dst_ref=dst, send_sem=ssem, recv_sem=rsem,
                                        device_id=dev, device_id_type=pl.DeviceIdType.MESH)


def _half(nrows, which):
    return pl.ds(which * (nrows // 2), nrows // 2)


def _ici_copies(stack, group_sizes, ssems, rsems):
    x, y, c, chips = _place()
    me_q = 2 * x + y
    sends, recvs = [], []
    a = 0
    for grp, size in enumerate(group_sizes):
        for k in range(size):
            rows = _half(stack[a].shape[1], c)
            for j, (cx, cy) in enumerate(chips):
                mine = stack[a].at[me_q, rows]
                sends.append(_rcopy(mine, mine, ssems[grp].at[k * 3 + j], rsems[grp].at[k * 3 + j], (cx, cy, c)))
                theirs = stack[a].at[2 * cx + cy, rows]
                recvs.append(_rcopy(theirs, theirs, ssems[grp].at[k * 3 + j], rsems[grp].at[k * 3 + j],
                                    (cx, cy, c)))
            a += 1
    return sends, recvs


def _allgather_start(name, stacks, order):
    n = len(stacks)

    def body(bufs, _, new):
        sends, _r = _ici_copies(bufs, [n], [new[0]], [new[1]])
        for cp in sends:
            cp.start()

    return _comm_call(name, body, stacks, order, new_sems=(3 * n, 3 * n))


def _forward_copies(stack, ssem, rsem):
    x, y, c, chips = _place()
    sib = (x, y, 1 - c)
    sends, recvs = [], []
    for a in range(len(stack)):
        for j, (cx, cy) in enumerate(chips):
            landed = stack[a].at[2 * cx + cy, _half(stack[a].shape[1], c)]
            sends.append(_rcopy(landed, landed, ssem.at[a * 3 + j], rsem.at[a * 3 + j], sib))
            other = stack[a].at[2 * cx + cy, _half(stack[a].shape[1], 1 - c)]
            recvs.append(_rcopy(other, other, ssem.at[a * 3 + j], rsem.at[a * 3 + j], sib))
    return sends, recvs


def _allgather_forward(name, stacks, sems, order, behind=()):
    n = len(stacks)

    def body(bufs, taken, new):
        sends, recvs = _ici_copies(bufs, [n], [taken[0]], [taken[1]])
        fwd, _r = _forward_copies(bufs, new[0], new[1])
        for arrived, onward in zip(recvs, fwd):
            arrived.wait_recv()
            onward.start()
        for cp in sends:
            cp.wait_send()

    return _comm_call(name, body, stacks, order, sems_in=sems, new_sems=(3 * n, 3 * n), behind=behind)


def _allgather_finish(name, stacks, sems, order):
    def body(bufs, taken, _):
        sends, recvs = _forward_copies(bufs, taken[0], taken[1])
        for cp in sends:
            cp.wait_send()
        for cp in recvs:
            cp.wait_recv()

    return _comm_call(name, body, stacks, order, sems_in=sems)[1]


def _window_unit(q, j):
    return C2I[WIN_UNIT0[q] + j]


def _pair_copies(g, t, ssem, rsem, gathered):
    x, y, c, _ = _place()
    sib = (x, y, 1 - c)
    cps, whole = [], []
    for a in range(len(g)):
        if a == 0 and gathered:
            for q in range(NCHIP):
                for j in range(WIN_UNITS // 2):
                    u = jnp.where(c == 0, _window_unit(q, WIN_UNITS // 2 + j), _window_unit(q, j))
                    src = g[0].at[pl.ds(pl.multiple_of(u * UNIT, UNIT), UNIT), :]
                    cps.append(_rcopy(src, t[0].at[q, pl.ds(j * UNIT, UNIT), :], ssem.at[0], rsem.at[0], sib))
            whole.append(_rcopy(t[0], t[0], ssem.at[0], rsem.at[0], sib))
        else:
            cp = _rcopy(g[a].at[:, _half(g[a].shape[1], 1 - c), :], t[a], ssem.at[a], rsem.at[a], sib)
            cps.append(cp)
            whole.append(cp)
    return cps, whole


def _comm_multi(name, parts, order):
    def body(buf_refs, taken, new):
        ib = it = inew = 0
        for pbody, pbufs, psems, pnew, _ in parts:
            pbody(buf_refs[ib:ib + len(pbufs)], taken[it:it + len(psems)], new[inew:inew + len(pnew)])
            ib, it, inew = ib + len(pbufs), it + len(psems), inew + len(pnew)

    sems, bufs = _comm_call(name, body, [b for p in parts for b in p[1]], order,
                            sems_in=[s for p in parts for s in p[2]], new_sems=[k for p in parts for k in p[3]])
    out, ib, inew = [], 0, 0
    for _, pbufs, _, pnew, unpack in parts:
        out.append(unpack(sems[inew:inew + len(pnew)], bufs[ib:ib + len(pbufs)]))
        ib, inew = ib + len(pbufs), inew + len(pnew)
    return out


def _pair_start_part(gs, gathered=False):
    n = len(gs)
    ts = [lax.empty((NCHIP, WIN_ROWS // 2, D) if (a == 0 and gathered) else (NCHIP, g.shape[1] // 2, g.shape[2]), f32)
          for a, g in enumerate(gs)]

    def body(bufs, _, new):
        for cp in _pair_copies(bufs[:n], bufs[n:], new[0], new[1], gathered)[0]:
            cp.start()

    return body, list(gs) + ts, (), (n, n), lambda sems, bufs: (sems, bufs)


def _pair_wait_part(bufs, sems, gathered=False):
    n = len(bufs) // 2

    def body(refs, taken, _):
        for cp in _pair_copies(refs[:n], refs[n:], taken[0], taken[1], gathered)[1]:
            cp.wait_send()
            cp.wait_recv()

    return body, list(bufs), list(sems), (), lambda _, out: (out[:n], out[n:])


def _row_tile(h):
    return min(h, 256)


def _pair_add(order, g, t, q_arr, c_arr, name):
    _, R, C = g.shape
    h = R // 2
    tr = _row_tile(h)
    nblk = h // tr

    def body(q_ref, c_ref, g_ref, t_ref, own_ref, p16_ref):
        s = g_ref[...] + t_ref[...]
        p16_ref[...] = s.astype(bf16)

        @pl.when(pl.program_id(1) == q_ref[0])
        def _():
            own_ref[...] = s

    blk = pl.BlockSpec((None, tr, C), lambda i, q, q_ref, c_ref: (q, i, 0))
    return _call_indexed(
        order, body, (q_arr, c_arr), (g, t), (nblk, NCHIP),
        [pl.BlockSpec((None, tr, C), lambda i, q, q_ref, c_ref: (q, c_ref[0] * nblk + i, 0)), blk],
        [pl.BlockSpec((tr, C), lambda i, q, q_ref, c_ref: (i, 0)), blk],
        name=name,
        out_shape=[jax.ShapeDtypeStruct((h, C), f32), jax.ShapeDtypeStruct((NCHIP, h, C), bf16)],
        compiler_params=_params(("parallel", "arbitrary")),
    )


def _pair_add_gathered(order, dwt, t, q_arr, c_arr, name):
    half_units, half_rows = WIN_UNITS // 2, WIN_ROWS // 2
    table = jnp.asarray([_window_unit(q, j) for q in range(NCHIP) for j in range(WIN_UNITS)], jnp.int32)

    def body(tab_ref, q_ref, c_ref, g_hbm, t_ref, own_ref, p16_ref, buf, sem):
        q = pl.program_id(0)

        def gather(w, slot):
            cps = []
            for j in range(half_units):
                u = tab_ref[w * WIN_UNITS + c_ref[0] * half_units + j]
                cps.append(pltpu.make_async_copy(g_hbm.at[pl.ds(pl.multiple_of(u * UNIT, UNIT), UNIT), :],
                                                 buf.at[slot, pl.ds(j * UNIT, UNIT), :], sem.at[slot]))
            return cps

        @pl.when(q == 0)
        def _():
            for cp in gather(0, 0):
                cp.start()

        @pl.when(q + 1 < NCHIP)
        def _():
            for cp in gather(q + 1, (q + 1) % 2):
                cp.start()

        slot = q % 2
        pltpu.make_async_copy(buf.at[slot], buf.at[slot], sem.at[slot]).wait()
        s = buf[slot] + t_ref[...]
        p16_ref[...] = s.astype(bf16)

        @pl.when(q == q_ref[0])
        def _():
            own_ref[...] = s

    blk = pl.BlockSpec((None, half_rows, D), lambda q, tab_ref, q_ref, c_ref: (q, 0, 0))
    return _call_indexed(
        order, body, (table, q_arr, c_arr), (dwt, t), (NCHIP,),
        [pl.BlockSpec(memory_space=pl.ANY), blk],
        [pl.BlockSpec((half_rows, D), lambda q, tab_ref, q_ref, c_ref: (0, 0)), blk],
        scratch_shapes=[pltpu.VMEM((2, half_rows, D), f32), pltpu.SemaphoreType.DMA((2,))],
        name=name,
        out_shape=[jax.ShapeDtypeStruct((half_rows, D), f32),
                   jax.ShapeDtypeStruct((NCHIP, half_rows, D), bf16)],
        compiler_params=_params(("arbitrary",)),
    )


def _shard_copies(p, r, sm, ssem, rsem):
    x, y, c, chips = _place()
    n = len(p)
    sends, recvs = [], []
    for a in range(n):
        for j, (cx, cy) in enumerate(chips):
            k = a * 3 + j
            sends.append(_rcopy(p[a].at[2 * cx + cy], r[a].at[j], ssem.at[k], rsem.at[k], (cx, cy, c)))
            recvs.append(_rcopy(r[a].at[j], r[a].at[j], ssem.at[k], rsem.at[k], (cx, cy, c)))
    if sm is not None:
        mine = sm.at[4 * x + 2 * y + c]
        for i in range(1, 8):
            px = (1 - x) if i & 4 else x
            py = (1 - y) if i & 2 else y
            pc = (1 - c) if i & 1 else c
            k = 3 * n + i - 1
            sends.append(_rcopy(mine, mine, ssem.at[k], rsem.at[k], (px, py, pc)))
            slot = sm.at[4 * px + 2 * py + pc]
            recvs.append(_rcopy(slot, slot, ssem.at[k], rsem.at[k], (px, py, pc)))
    return sends, recvs


def _shard_start_part(p16s, sm=None):
    n = len(p16s)
    rs = [lax.empty((3,) + p.shape[1:], bf16) for p in p16s]
    extra = [] if sm is None else [sm]
    nsem = 3 * n + (7 if sm is not None else 0)

    def body(bufs, _, new):
        sends, _r = _shard_copies(bufs[:n], bufs[n:2 * n], bufs[2 * n] if extra else None, new[0], new[1])
        for cp in sends:
            cp.start()

    return body, list(p16s) + rs + extra, (), (nsem, nsem), lambda sems, bufs: (sems, bufs)


def _shard_wait_part(bufs, sems, n):
    has_sm = len(bufs) > 2 * n

    def body(refs, taken, _):
        sends, recvs = _shard_copies(refs[:n], refs[n:2 * n], refs[2 * n] if has_sm else None, taken[0], taken[1])
        for cp in sends:
            cp.wait_send()
        for cp in recvs:
            cp.wait_recv()

    return body, list(bufs), list(sems), (), lambda _, out: (out[n:2 * n], (out[2 * n] if has_sm else None))


def _shard_sum(order, own, r, c_arr, name):
    h, C = own.shape
    tr = _row_tile(h)
    nblk = h // tr

    def body(c_ref, p_ref, r_ref, o_ref):
        s = p_ref[...]
        for j in range(3):
            s = s + r_ref[j].astype(f32)
        o_ref[...] = s

    return _call_indexed(
        order, body, (c_arr,), (own, r), (nblk,),
        [pl.BlockSpec((tr, C), lambda i, c_ref: (i, 0)), pl.BlockSpec((3, tr, C), lambda i, c_ref: (0, i, 0))],
        pl.BlockSpec((tr, C), lambda i, c_ref: (c_ref[0] * nblk + i, 0)),
        name=name, out_shape=jax.ShapeDtypeStruct((2 * h, C), f32),
        compiler_params=_params(("parallel",)),
    )


def _swap_copies(full, ssem, rsem):
    x, y, c, _ = _place()
    sends, recvs = [], []
    for a in range(len(full)):
        mine = full[a].at[_half(full[a].shape[0], c)]
        sends.append(_rcopy(mine, mine, ssem.at[a], rsem.at[a], (x, y, 1 - c)))
        other = full[a].at[_half(full[a].shape[0], 1 - c)]
        recvs.append(_rcopy(other, other, ssem.at[a], rsem.at[a], (x, y, 1 - c)))
    return sends, recvs


def _swap_start_part(fulls):
    n = len(fulls)

    def body(bufs, _, new):
        for cp in _swap_copies(bufs, new[0], new[1])[0]:
            cp.start()

    return body, list(fulls), (), (n, n), lambda sems, bufs: (sems, bufs)


def _swap_wait_part(fulls, sems):
    def body(refs, taken, _):
        sends, recvs = _swap_copies(refs, taken[0], taken[1])
        for cp in sends:
            cp.wait_send()
        for cp in recvs:
            cp.wait_recv()

    return body, list(fulls), list(sems), (), lambda _, out: out


def _small_sum(order, sm):
    def body(sm_ref, o_ref):
        s = sm_ref[0]
        for d in range(1, 8):
            s = s + sm_ref[d]
        o_ref[...] = s

    return _call(order, body, (sm,), name="small_grad_sum", out_shape=jax.ShapeDtypeStruct((SMALL_ROWS, D), f32))


def _adamw_math(w, g, m, v):
    m = ADAM_B1 * m + (1.0 - ADAM_B1) * g
    v = ADAM_B2 * v + (1.0 - ADAM_B2) * (g * g)
    m_hat = m / (1.0 - ADAM_B1 ** ADAM_STEP)
    v_hat = v / (1.0 - ADAM_B2 ** ADAM_STEP)
    return -ADAM_LR * (m_hat / (jnp.sqrt(v_hat) + ADAM_EPS) + ADAM_WD * w), m, v


def _adamw_small(order, ws, gs, ms, vs, name):
    n = len(ws)

    def body(*refs):
        for i in range(n):
            res = _adamw_math(*[refs[k * n + i][...] for k in range(4)])
            for k in range(3):
                refs[4 * n + 3 * i + k][...] = res[k]

    out = _call(order, body, list(ws) + list(gs) + list(ms) + list(vs), name=name,
                out_shape=[jax.ShapeDtypeStruct(w.shape, f32) for w in ws for _ in range(3)])
    return [out[3 * i:3 * i + 3] for i in range(n)]


def _adamw(order, w, g, m, v, name):
    R, C = w.shape
    if R <= 256 or R % 256 == 0:
        tr, tc = min(R, 256), C
    else:
        tr, tc = R, 128

    def body(w_ref, g_ref, m_ref, v_ref, d_ref, nm_ref, nv_ref):
        d_ref[...], nm_ref[...], nv_ref[...] = _adamw_math(w_ref[...], g_ref[...], m_ref[...], v_ref[...])

    blk = pl.BlockSpec((tr, tc), lambda i, j: (i, j))
    return _call(
        order, body, (w, g, m, v), name=name, grid=(R // tr, C // tc), in_specs=[blk] * 4, out_specs=[blk] * 3,
        out_shape=[jax.ShapeDtypeStruct((R, C), f32)] * 3,
        compiler_params=_params(("parallel", "parallel")),
    )


def _feature_rows(w):
    return jnp.transpose(w, (2, 0, 1))


def _feature_major(w):
    return _feature_rows(w).reshape(SHARD_IN, D)


def _unfeature_rows(a):
    return jnp.transpose(a, (1, 2, 0))


ADAM_IN_ROWS = 134
ADAM_IN_STEPS = SHARD_IN // ADAM_IN_ROWS
ADAM_IN_CHUNK = 136
ADAM_IN_CHUNKS = ADAM_IN_STEPS + 1
ADAM_IN_BUF = WIN_ROWS + N_FA


def _adamw_w_in(order, w, gwin, gfa, m, v, q_arr):
    assert ADAM_IN_CHUNK * ADAM_IN_STEPS < WIN_ROWS <= ADAM_IN_CHUNK * ADAM_IN_CHUNKS
    assert OWN_ROW0[NCHIP - 1] + ADAM_IN_ROWS <= 2 * ADAM_IN_CHUNK and ADAM_IN_CHUNK >= ADAM_IN_ROWS
    last0 = ADAM_IN_CHUNK * ADAM_IN_STEPS
    cut = OWN_ROW0[1] + FA_AT

    def body(q_ref, w_ref, gwin_ref, gfa_ref, m_ref, v_ref, go_ref, d_ref, nm_ref, nv_ref, buf, sem):
        i = pl.program_id(0)
        q = q_ref[0]
        chip1 = q == 1
        shift = jnp.where(chip1, N_FA, 0)

        def copy(src_ref, src0, dst0, n, slot):
            return pltpu.make_async_copy(src_ref.at[pl.ds(src0, n)], buf.at[pl.ds(dst0, n), 0], sem.at[slot])

        def first(on_chip1):
            if on_chip1:
                return [copy(gwin_ref, 0, 0, cut, 0), copy(gfa_ref, 0, cut, N_FA, ADAM_IN_CHUNKS),
                        copy(gwin_ref, cut, cut + N_FA, ADAM_IN_CHUNK - cut - N_FA, ADAM_IN_CHUNKS + 1)]
            return [copy(gwin_ref, 0, 0, ADAM_IN_CHUNK, 0)]

        def middle(k):
            return [copy(gwin_ref, pl.multiple_of(k * ADAM_IN_CHUNK - shift, 8), k * ADAM_IN_CHUNK, ADAM_IN_CHUNK, k)]

        def last(on_chip1):
            n = WIN_ROWS - last0 + (N_FA if on_chip1 else 0)
            return [copy(gwin_ref, WIN_ROWS - n, last0, n, ADAM_IN_STEPS)]

        def both(make, act):
            for on_chip1 in (False, True):
                @pl.when(chip1 if on_chip1 else jnp.logical_not(chip1))
                def _():
                    for c in make(on_chip1):
                        act(c)

        @pl.when(i == 0)
        def _():
            both(first, lambda c: c.start())
            for k in range(1, ADAM_IN_STEPS):
                middle(k)[0].start()
            both(last, lambda c: c.start())
            both(first, lambda c: c.wait())

        @pl.when(i < ADAM_IN_STEPS - 1)
        def _():
            middle(i + 1)[0].wait()

        @pl.when(i == ADAM_IN_STEPS - 1)
        def _():
            both(last, lambda c: c.wait())

        row0 = jnp.where(q == 0, OWN_ROW0[0], jnp.where(chip1, OWN_ROW0[1], jnp.where(q == 2, OWN_ROW0[2], OWN_ROW0[3])))
        g = buf[pl.ds(row0 + i * ADAM_IN_ROWS, ADAM_IN_ROWS)]
        go_ref[...] = g
        d_ref[...], nm_ref[...], nv_ref[...] = _adamw_math(w_ref[...], g, m_ref[...], v_ref[...])

    blk = pl.BlockSpec((ADAM_IN_ROWS, 1, D), lambda i, q: (i, 0, 0))
    hbm = pl.BlockSpec(memory_space=pl.ANY)
    return _call_indexed(
        order, body, (q_arr,), (w, gwin, gfa, m, v), (ADAM_IN_STEPS,), [blk, hbm, hbm, blk, blk], [blk] * 4,
        scratch_shapes=[pltpu.VMEM((ADAM_IN_BUF, 1, D), f32), pltpu.SemaphoreType.DMA((ADAM_IN_CHUNKS + 2,))],
        name="adamw_w_in", out_shape=[jax.ShapeDtypeStruct((SHARD_IN, 1, D), f32)] * 4,
        compiler_params=_params(("arbitrary",)),
    )


def _window_of(wt, q):
    def plain(k):
        return lambda w: jnp.pad(w, ((OWN_ROW0[k], WIN_ROWS - OWN_ROW0[k] - SHARD_IN), (0, 0))).astype(bf16)

    def chip1(w):
        lo = jnp.pad(w[0:FA_AT], ((OWN_ROW0[1], WIN_ROWS - OWN_ROW0[1] - FA_AT), (0, 0)))
        hi = jnp.pad(w[FA_AT + N_FA:SHARD_IN], ((OWN_ROW0[1] + FA_AT, WIN_ROWS - OWN_ROW0[1] - (SHARD_IN - N_FA)), (0, 0)))
        return (lo + hi).astype(bf16)

    win = lax.switch(q, [plain(0), chip1, plain(2), plain(3)], wt)
    fa = jnp.pad(wt[FA_AT:FA_AT + N_FA], ((0, FA_ROWS - N_FA), (0, 0))).astype(bf16)
    return win, fa


def kernel(x, norm_attn_g, w_in, b_forget, w_branch_a, w_branch_b, w_out, norm_mlp_g, w_up, w_down, norm_final_g, loss_target, m_norm_attn_g, m_w_in, m_b_forget, m_w_branch_a, m_w_branch_b, m_w_out, m_norm_mlp_g, m_w_up, m_w_down, m_norm_final_g, v_norm_attn_g, v_w_in, v_b_forget, v_w_branch_a, v_w_branch_b, v_w_out, v_norm_mlp_g, v_w_up, v_w_down, v_norm_final_g):
    xi, yi, ci = lax.axis_index("x"), lax.axis_index("y"), lax.axis_index("c")
    q_me = 2 * xi + yi
    c_arr = jnp.reshape(ci, (1,)).astype(jnp.int32)
    q_arr = jnp.reshape(q_me, (1,)).astype(jnp.int32)
    x_, tgt = x[0], loss_target[0]

    names = ["w_branch_a", "w_branch_b", "w_out", "w_up", "w_down"]
    big = dict(zip(names, [w_branch_a[0], w_branch_b[0], w_out[0], w_up[0], w_down[0]]))
    ms = dict(zip(names, [m_w_branch_a[0], m_w_branch_b[0], m_w_out[0], m_w_up[0], m_w_down[0]]))
    vs = dict(zip(names, [v_w_branch_a[0], v_w_branch_b[0], v_w_out[0], v_w_up[0], v_w_down[0]]))
    grad, upd = {}, {}
    order = _Order()

    def run(fn, *args, **kw):
        return fn(order, *args, **kw)

    def own_slot(a):
        return lax.dynamic_update_slice(lax.empty((NCHIP,) + a.shape, a.dtype), a[None], (q_me, 0, 0))

    wt_own = _feature_major(w_in)
    win, fa_blk = _window_of(wt_own, q_me)
    sem_in, in_s = _allgather_start("allgather_start_in", [own_slot(win), own_slot(fa_blk)], order)
    sem_rest, rest = _allgather_start("allgather_start_rest", [own_slot(w.astype(bf16)) for w in big.values()], order)
    rope = _rope_tables(order.tok[0, 0])
    sem_f, in_s = _allgather_forward("allgather_forward_in", in_s, sem_in, order, behind=[wt_own, *rope])
    wins, fas = _allgather_finish("allgather_finish_in", in_s, sem_f, order)
    wt = run(_assemble_win, wins, fas)

    bpad = jnp.pad(b_forget, ((0, 0), (0, 120)))
    h1, qkvb, qkva, gates, fa = run(_norm_inproj, x_, norm_attn_g, wt, rope)
    F = run(_forget_cumsum, fa, bpad)
    oa, lsea = run(_fox_fwd, qkva, F)
    sem_f, rest = _allgather_forward("allgather_forward_rest", rest, sem_rest, order)
    ob, lseb = run(_dil_fwd, qkvb)
    was, wbs, wouts, wups, wdowns = _allgather_finish("allgather_finish_rest", rest, sem_f, order)
    wout = wouts.reshape(D, D)
    wdown = wdowns.reshape(DFF, D)
    ya, yb, mixed = run(_branch_mix, oa, ob, was, wbs, gates)
    x2, h2 = run(_outproj_norm, mixed, wout, x_, norm_mlp_g)
    u, a = run(_mlp_up, h2, wups)
    dx3, dx3b, dg3, loss_part = run(_mlp_down_loss, a, wdown, x2, norm_final_g.reshape(1, D), tgt)

    def comm(name, *parts):
        return _comm_multi(name, list(parts), order)

    def pair_adds(group, gs, ts):
        return zip(*[run(_pair_add, gs[i], ts[i], q_arr, c_arr, "pair_add_" + nm) for i, nm in enumerate(group)])

    def shard_sums(group, p32s, rs):
        return [run(_shard_sum, p32s[i], rs[i], c_arr, "shard_sum_" + nm) for i, nm in enumerate(group)]

    def adamw_group(group, fulls):
        for nm, gfull in zip(group, fulls):
            grad[nm] = gfull
            upd[nm] = run(_adamw, big[nm], gfull, ms[nm], vs[nm], "adamw_" + nm)

    grp_a, grp_b, grp_c = ["w_down", "w_up"], ["w_out", "w_branch_a", "w_branch_b"], ["w_in", "w_in_fa"]
    du = run(_mlp_down_bwd, dx3b, wdown, u)
    dwdown = run(_mm, a, dx3b, "tn", f32, 1024, D, "wgrad_down")
    dwup = run(_mm, h2, du, "tn", f32, D, 1024, "wgrad_up", stack_cols=True)
    ((sem_pa, buf_pa),) = comm("pair_start_a", _pair_start_part([dwdown.reshape(NCHIP, DFF // NCHIP, D), dwup]))
    dx2, dx2b, dg2 = run(_mlp_up_bwd, du, wups, x2, dx3, norm_mlp_g)
    ((gs, ts),) = comm("pair_wait_a", _pair_wait_part(buf_pa, sem_pa))
    p32_a, p16_a = pair_adds(grp_a, gs, ts)
    ((sem_sa, buf_sa),) = comm("shard_start_a", _shard_start_part(p16_a))
    dya, dyb, dproj = run(_gate_bwd, dx2b, wout, gates, ya, yb)
    dwout = run(_mm, mixed, dx2b, "tn", f32, D, D, "wgrad_out")
    doa, dob = run(_branch_bwd, dya, dyb, was, wbs)
    dwas, dwbs = run(_branch_wgrad, oa, ob, dya, dyb)
    ((sem_pb, buf_pb),) = comm("pair_start_b", _pair_start_part([dwout.reshape(NCHIP, D // NCHIP, D), dwas, dwbs]))
    dF, dproj = run(_fox_bwd, qkva, doa, oa, lsea, F, dproj)
    (gs, ts), (rs_a, _) = comm("pair_wait_b_shard_wait_a", _pair_wait_part(buf_pb, sem_pb),
                               _shard_wait_part(buf_sa, sem_sa, len(grp_a)))
    p32_b, p16_b = pair_adds(grp_b, gs, ts)
    fulls_a = shard_sums(grp_a, p32_a, rs_a)
    (sem_wa, fulls_a), (sem_sb, buf_sb) = comm("swap_start_a_shard_start_b", _swap_start_part(fulls_a),
                                               _shard_start_part(p16_b))
    dbf, dproj = run(_forget_bwd, dF, fa, bpad, dproj)
    dproj = run(_dil_bwd, qkvb, dob, ob, lseb, rope, dproj)
    (rs_b, _), fulls_a = comm("shard_wait_b_swap_wait_a", _shard_wait_part(buf_sb, sem_sb, len(grp_b)),
                              _swap_wait_part(fulls_a, sem_wa))
    fulls_b = shard_sums(grp_b, p32_b, rs_b)
    ((sem_wb, fulls_b),) = comm("swap_start_b", _swap_start_part(fulls_b))
    dwt = run(_mm, dproj, h1, "tn", f32, 512, D, "wgrad_in")
    dwfa = jnp.broadcast_to(dwt[F_FA:F_FA + FA_ROWS][None], (NCHIP, FA_ROWS, D))
    (sem_pc, buf_pc), fulls_b = comm("pair_start_c_swap_wait_b", _pair_start_part([dwt, dwfa], gathered=True),
                                     _swap_wait_part(fulls_b, sem_wb))
    adamw_group(grp_b, fulls_b)
    (((dwt_c, dwfa_c), (t_in, t_fa)),) = comm("pair_wait_c", _pair_wait_part(buf_pc, sem_pc, gathered=True))
    p32_in, p16_in = run(_pair_add_gathered, dwt_c, t_in, q_arr, c_arr, "pair_add_w_in")
    p32_fa, p16_fa = run(_pair_add, dwfa_c, t_fa, q_arr, c_arr, "pair_add_w_in_fa")
    ((sem_sc, buf_sc),) = comm("shard_start_c", _shard_start_part([p16_in, p16_fa]))
    gx, dg1 = run(_inproj_bwd, dproj, wt, x_, dx2, norm_attn_g)
    adamw_group(grp_a, fulls_a)
    small = jnp.concatenate([dg1, dg2, dg3, jnp.pad(dbf[:, 0:8], ((0, 0), (0, D - 8))),
                             jnp.pad(loss_part, ((0, 0), (0, D - 128))),
                             jnp.zeros((SMALL_ROWS - 5, D), f32)], axis=0)
    sm = lax.dynamic_update_slice(lax.empty((8, SMALL_ROWS, D), f32), small[None],
                                  (4 * xi + 2 * yi + ci, 0, 0))
    (sem_sm, buf_sm), (rs_c, _) = comm("small_start_shard_wait_c", _shard_start_part([], sm),
                                       _shard_wait_part(buf_sc, sem_sc, len(grp_c)))
    fulls_c = shard_sums(grp_c, [p32_in, p32_fa], rs_c)
    (sem_wc, fulls_c), (_, sm) = comm("swap_start_c_small_wait", _swap_start_part(fulls_c),
                                      _shard_wait_part(buf_sm, sem_sm, 0))
    gsmall = run(_small_sum, sm)
    loss = gsmall[4, 0]

    grad["norm_attn_g"], grad["norm_mlp_g"] = gsmall[0:1], gsmall[1:2]
    grad["norm_final_g"], grad["b_forget"] = gsmall[2:3], gsmall[3:4, 0:8]
    smalls = ["norm_attn_g", "norm_mlp_g", "norm_final_g", "b_forget"]
    res = run(_adamw_small, [norm_attn_g, norm_mlp_g, norm_final_g.reshape(1, D), b_forget],
              [grad[nm] for nm in smalls],
              [m_norm_attn_g, m_norm_mlp_g, m_norm_final_g.reshape(1, D), m_b_forget],
              [v_norm_attn_g, v_norm_mlp_g, v_norm_final_g.reshape(1, D), v_b_forget], "adamw_small")
    upd.update(zip(smalls, res))

    ((gwin, gfa),) = comm("swap_wait_c", _swap_wait_part(fulls_c, sem_wc))
    res_in = run(_adamw_w_in, _feature_rows(w_in), gwin, gfa, _feature_rows(m_w_in), _feature_rows(v_w_in), q_arr)
    grad["w_in"] = _unfeature_rows(res_in[0])
    upd["w_in"] = [_unfeature_rows(t) for t in res_in[1:]]

    order_out = ["norm_attn_g", "w_in", "b_forget", "w_branch_a", "w_branch_b", "w_out", "norm_mlp_g", "w_up",
                 "w_down", "norm_final_g"]
    shapes = dict(norm_attn_g=norm_attn_g.shape, w_in=w_in.shape, b_forget=b_forget.shape,
                  w_branch_a=w_branch_a.shape, w_branch_b=w_branch_b.shape, w_out=w_out.shape,
                  norm_mlp_g=norm_mlp_g.shape, w_up=w_up.shape, w_down=w_down.shape, norm_final_g=norm_final_g.shape)
    outs = [loss, gx.reshape(x.shape)]
    outs += [grad[nm].reshape(shapes[nm]) for nm in order_out]
    for k in range(3):
        outs += [upd[nm][k].reshape(shapes[nm]) for nm in order_out]
    return tuple(outs)
```

```python
import jax
import jax.numpy as jnp
from jax import lax
from jax.experimental import pallas as pl
from jax.experimental.pallas import tpu as pltpu

f32 = jnp.float32
bf16 = jnp.bfloat16

S = 2048
D = 1024
DFF = 4096
HD = 64
FOXW = 512
DILOUT = 256
DIL = (1, 4, 16)
BAND = 128
EPS = 1e-6
NEG = -1e30
ROPE_THETA = 500000.0
NCHIP = 4
TQ = 256

ADAM_LR, ADAM_B1, ADAM_B2, ADAM_EPS, ADAM_WD, ADAM_STEP = 0.001, 0.9, 0.999, 1e-08, 0.01, 10
VMEM_LIMIT = 56 * 1024 * 1024

UNIT = 64
NP = 6144
F_DIL, F_FOX, F_FA, F_G = 0, 2304, 3840, 4096
DIL_BLK, FOX_BLK = 1152, 384
WIN_UNITS, WIN_ROWS = 24, 1536
WIN_UNIT0 = (0, 23, 45, 68)
OWN_ROW0 = (0, 2, 60, 62)
SHARD_IN = 1474
N_FA = 8
FA_AT = 1536 - SHARD_IN
FA_ROWS = 32


def _compact_to_internal():
    c2i = {}
    for p in range(2):
        for role in range(3):
            for g in range(3):
                for hh in range(2):
                    c2i[24 + 12 * role + 4 * g + 2 * p + hh] = 18 * p + 6 * role + 2 * g + hh
    for p in range(4):
        for role in range(3):
            for hh in range(2):
                c2i[8 * role + 2 * p + hh] = F_FOX // UNIT + 6 * p + 2 * role + hh
    for j in range(32):
        c2i[60 + j] = F_G // UNIT + j
    return c2i


C2I = _compact_to_internal()
OVERLAP_UNITS = (23, 45, 46, 68)


def _params(sem=None):
    return pltpu.CompilerParams(dimension_semantics=sem, vmem_limit_bytes=VMEM_LIMIT)


class _Order:
    def __init__(self):
        self.tok = None

    def mark(self, v):
        self.tok = v

    def token_for(self, args):
        return [] if self.tok is None or any(self.tok is a for a in args) else [self.tok]


def _call(order, body, args, in_specs=None, **kw):
    args = list(args)
    n_in = len(args)
    if in_specs is None:
        in_specs = [pl.BlockSpec(memory_space=pltpu.VMEM)] * n_in
    kern = body
    extra = order.token_for(args)
    if extra:
        in_specs = list(in_specs) + [pl.BlockSpec(memory_space=pl.ANY)]

        def kern(*refs):
            body(*refs[:n_in], *refs[n_in + 1:])

    out = pl.pallas_call(kern, in_specs=in_specs, **kw)(*args, *extra)
    order.mark(out[0] if isinstance(out, (tuple, list)) else out)
    return out


def _call_indexed(order, body, scalars, args, grid, in_specs, out_specs, scratch_shapes=(), **kw):
    args, in_specs = list(args), list(in_specs)
    n_front = len(scalars) + len(args)
    kern = body
    extra = order.token_for(args)
    if extra:
        in_specs.append(pl.BlockSpec(memory_space=pl.ANY))

        def kern(*refs):
            body(*refs[:n_front], *refs[n_front + 1:])

    out = pl.pallas_call(
        kern, grid_spec=pltpu.PrefetchScalarGridSpec(num_scalar_prefetch=len(scalars), grid=grid, in_specs=in_specs,
                                                     out_specs=out_specs, scratch_shapes=scratch_shapes),
        **kw)(*scalars, *args, *extra)
    order.mark(out[0] if isinstance(out, (tuple, list)) else out)
    return out


def _dot(a, b):
    return jnp.dot(a, b, preferred_element_type=f32)


def _dot_nt(a, b):
    return lax.dot_general(a, b, (((1,), (1,)), ((), ())), preferred_element_type=f32)


def _dot_tn(a, b):
    return lax.dot_general(a, b, (((0,), (0,)), ((), ())), preferred_element_type=f32)


def _split3(x):
    hi = x.astype(bf16)
    r1 = x - hi.astype(f32)
    mid = r1.astype(bf16)
    lo = (r1 - mid.astype(f32)).astype(bf16)
    return hi, mid, lo


def _rope_tables(after):
    half = 8
    inv_freq = jnp.power(jnp.float32(ROPE_THETA), -jnp.arange(half, dtype=f32) * 2.0 / 16)
    ang = (jnp.arange(S).astype(f32) + after)[:, None] * inv_freq[None, :]
    cos, sin = jnp.cos(ang), jnp.sin(ang)
    one = jnp.ones((S, HD - 16), f32)
    zero = jnp.zeros((S, HD - 16), f32)
    z8 = jnp.zeros((S, 8), f32)
    c = jnp.concatenate([cos, cos, one], axis=1)
    s1 = jnp.concatenate([-sin, z8, zero], axis=1)
    s2 = jnp.concatenate([z8, sin, zero], axis=1)
    return tuple(jnp.concatenate([t, t], axis=1) for t in (c, s1, s2))


def _mm(order, a, b, mode, out_dtype, tm, tn, name, stack_cols=False):
    if mode == "nn":
        (M, K), (_, N) = a.shape, b.shape
        a_spec = pl.BlockSpec((tm, K), lambda i, j: (i, 0))
        b_spec = pl.BlockSpec((K, tn), lambda i, j: (0, j))
        dot = _dot
    elif mode == "nt":
        (M, K), (N, _) = a.shape, b.shape
        a_spec = pl.BlockSpec((tm, K), lambda i, j: (i, 0))
        b_spec = pl.BlockSpec((tn, K), lambda i, j: (j, 0))
        dot = _dot_nt
    else:
        (K, M), (_, N) = a.shape, b.shape
        a_spec = pl.BlockSpec((K, tm), lambda i, j: (0, i))
        b_spec = pl.BlockSpec((K, tn), lambda i, j: (0, j))
        dot = _dot_tn

    def body(a_ref, b_ref, o_ref):
        o_ref[...] = dot(a_ref[...], b_ref[...]).astype(out_dtype)

    if stack_cols:
        assert tm == M
        out_spec = pl.BlockSpec((None, tm, tn), lambda i, j: (j, 0, 0))
        out_shape = jax.ShapeDtypeStruct((N // tn, M, tn), out_dtype)
    else:
        out_spec = pl.BlockSpec((tm, tn), lambda i, j: (i, j))
        out_shape = jax.ShapeDtypeStruct((M, N), out_dtype)
    return _call(
        order, body, (a, b), name=name, grid=(M // tm, N // tn), in_specs=[a_spec, b_spec],
        out_specs=out_spec, out_shape=out_shape,
        compiler_params=_params(("parallel", "parallel")),
    )


def _assemble_win(order, wins, fas):
    def body(win_ref, fa_ref, o_ref):
        q = pl.program_id(0)

        @pl.when(q == 0)
        def _():
            o_ref[...] = jnp.zeros_like(o_ref)

        for k in range(NCHIP):
            @pl.when(q == k)
            def _(k=k):
                for j in range(WIN_UNITS):
                    cu = WIN_UNIT0[k] + j
                    dst = pl.ds(C2I[cu] * UNIT, UNIT)
                    if cu in OVERLAP_UNITS:
                        o_ref[dst, :] += win_ref[j * UNIT:(j + 1) * UNIT, :]
                    else:
                        o_ref[dst, :] = win_ref[j * UNIT:(j + 1) * UNIT, :]
                if k == 1:
                    o_ref[F_FA:F_FA + FA_ROWS, :] = fa_ref[...]

    return _call(
        order, body, (wins, fas), name="assemble_w_in", grid=(NCHIP,),
        in_specs=[pl.BlockSpec((None, WIN_ROWS, D), lambda q: (q, 0, 0)),
                  pl.BlockSpec((None, FA_ROWS, D), lambda q: (1, 0, 0))],
        out_specs=pl.BlockSpec((NP, D), lambda q: (0, 0)),
        out_shape=jax.ShapeDtypeStruct((NP, D), bf16),
        compiler_params=_params(("arbitrary",)),
    )


def _norm_inproj(order, x, g1, wt, rope):
    tm = 256
    c_t, s1_t, s2_t = rope

    def body(x_ref, g_ref, w_ref, c_ref, s1_ref, s2_ref, h_ref, qkvb_ref, qkva_ref, gates_ref, fa_ref):
        xb = x_ref[...]
        r = lax.rsqrt(jnp.mean(xb * xb, axis=-1, keepdims=True) + EPS)
        h = ((xb * r) * g_ref[...]).astype(bf16)
        h_ref[...] = h
        c, s1, s2 = c_ref[...], s1_ref[...], s2_ref[...]
        for p in range(2):
            pb = _dot_nt(h, w_ref[F_DIL + p * DIL_BLK:F_DIL + (p + 1) * DIL_BLK, :])
            for ch in range(DIL_BLK // 128):
                pc = pb[:, ch * 128:(ch + 1) * 128]
                if ch < 6:
                    pc = pc * c + pltpu.roll(pc, 120, 1) * s1 + pltpu.roll(pc, 8, 1) * s2
                qkvb_ref[:, p * DIL_BLK + ch * 128:p * DIL_BLK + (ch + 1) * 128] = pc
        qkva_ref[...] = _dot_nt(h, w_ref[F_FOX:F_FA, :]).astype(bf16)
        fa_ref[...] = _dot_nt(h, w_ref[F_FA:F_FA + 128, :])
        gates_ref[...] = _dot_nt(h, w_ref[F_G:NP, :]).astype(bf16)

    row = lambda w: pl.BlockSpec((tm, w), lambda i: (i, 0))
    return _call(
        order, body, (x, g1, wt, c_t, s1_t, s2_t), name="norm_inproj", grid=(S // tm,),
        in_specs=[row(D), pl.BlockSpec((1, D), lambda i: (0, 0)), pl.BlockSpec((NP, D), lambda i: (0, 0)),
                  row(128), row(128), row(128)],
        out_specs=[row(D), row(2 * DIL_BLK), row(4 * FOX_BLK), row(2 * D), row(128)],
        out_shape=[jax.ShapeDtypeStruct((S, D), bf16), jax.ShapeDtypeStruct((S, 2 * DIL_BLK), f32),
                   jax.ShapeDtypeStruct((S, 4 * FOX_BLK), bf16), jax.ShapeDtypeStruct((S, 2 * D), bf16),
                   jax.ShapeDtypeStruct((S, 128), f32)],
        compiler_params=_params(("parallel",)),
    )


def _forget_cumsum(order, fa, bpad):
    nb = S // TQ

    def body(fa_ref, b_ref, F_ref):
        rr = lax.broadcasted_iota(jnp.int32, (TQ, TQ), 0)
        cc = lax.broadcasted_iota(jnp.int32, (TQ, TQ), 1)
        tri = (rr >= cc).astype(bf16)
        lane = lax.broadcasted_iota(jnp.int32, (1, 128), 1)
        carry = jnp.zeros((1, 128), f32)
        for b in range(nb):
            z = fa_ref[b * TQ:(b + 1) * TQ, :] + b_ref[...]
            lf = jnp.minimum(z, 0.0) - jnp.log(1.0 + jnp.exp(-jnp.abs(z)))
            lf = jnp.where(lane < 8, lf, 0.0)
            hi, mid, lo = _split3(lf)
            fb = (_dot(tri, hi) + _dot(tri, mid)) + _dot(tri, lo) + carry
            F_ref[b * TQ:(b + 1) * TQ, :] = fb
            carry = fb[TQ - 1:TQ, :]

    return _call(
        order, body, (fa, bpad), name="forget_cumsum",
        out_shape=jax.ShapeDtypeStruct((S, 128), f32),
        compiler_params=_params(),
    )


def _head_masks():
    lane = lax.broadcasted_iota(jnp.int32, (1, 128), 1)
    return lane, (lane < HD, lane >= HD)


L_ONE = 3
FOX_TQ, FOX_TK = 256, 512


def _set_lanes(x, lane, first, cols):
    for n, col in enumerate(cols):
        x = jnp.where(lane == first + n, col, x)
    return x


def _f32_parts(col):
    return [t.astype(f32) for t in _split3(col)]


def _fox_operands(qkv_ref, F_ref, lse_ref, qa, ka, p, rows):
    lane, hm = _head_masks()
    q = qkv_ref[rows, 0:128].astype(f32) * 0.125
    k = qkv_ref[rows, 128:256].astype(f32)
    Fb = F_ref[rows, :]
    for hh in (0, 1):
        free = (1 - hh) * HD
        fcol = jnp.sum(jnp.where(lane == 2 * p + hh, Fb, 0.0), axis=1, keepdims=True)
        qterm = fcol if lse_ref is None else fcol - lse_ref[rows, hh * HD:hh * HD + 1]
        qcols = _f32_parts(qterm) + [1.0] * 3
        kcols = [1.0] * 3 + [-t for t in _f32_parts(fcol)]
        qa[hh, rows, :] = _set_lanes(jnp.where(hm[hh], q, 0.0), lane, free, qcols).astype(bf16)
        ka[hh, rows, :] = _set_lanes(k, lane, free, kcols).astype(bf16)


def _fox_fwd(order, qkva, F):
    tq, tk = FOX_TQ, FOX_TK

    def body(qkv_ref, F_ref, o_ref, lse_ref, qa, ka, vt):
        p = pl.program_id(0)
        keyi = lax.broadcasted_iota(jnp.int32, (tk, 1), 0)
        qryi = lax.broadcasted_iota(jnp.int32, (1, tq), 1)
        sub = lax.broadcasted_iota(jnp.int32, (128, 1), 0)

        def prep(i, c):
            rows = pl.ds(pl.multiple_of(i * tk, tk), tk)
            _fox_operands(qkv_ref, F_ref, None, qa, ka, p, rows)
            vt[i] = qkv_ref[rows, 256:384].astype(f32).T.astype(bf16)
            return c

        lax.fori_loop(0, S // tk, prep, 0)

        def qblock(i, first_half):
            r0 = pl.multiple_of(i * tq, tq)
            qh = [qa[hh, pl.ds(r0, tq), :] for hh in (0, 1)]

            def kv(jb, carry, masked, width):
                keys = pl.ds(pl.multiple_of(jb * tk, tk), width)
                sts = [_dot_nt(ka[hh, keys, :], qh[hh]) for hh in (0, 1)]
                new = []
                for hh in (0, 1):
                    m, l, a = carry[3 * hh:3 * hh + 3]
                    st = sts[hh]
                    if masked:
                        st = jnp.where(jb * tk + keyi[0:width] <= r0 + qryi, st, NEG)
                    mn = jnp.maximum(m, jnp.max(st, axis=0, keepdims=True))
                    al = jnp.exp(m - mn)
                    pt = jnp.exp(st - mn)
                    l = al * l + jnp.sum(pt, axis=0, keepdims=True)
                    a = al * a + _dot(vt[jb, hh * HD:(hh + 1) * HD, 0:width], pt.astype(bf16))
                    new += [mn, l, a]
                return tuple(new)

            init = (jnp.full((1, tq), NEG, f32), jnp.zeros((1, tq), f32), jnp.zeros((HD, tq), f32)) * 2
            last = (r0 + tq - 1) // tk
            carry = lax.fori_loop(0, last, lambda j, cr: kv(j, cr, False, tk), init)
            m0, l0, a0, m1, l1, a1 = kv(last, carry, True, tk // 2 if first_half else tk)
            ot = jnp.concatenate([a0 / l0, a1 / l1], axis=0)
            lt = jnp.where(sub < HD, m0 + jnp.log(l0), m1 + jnp.log(l1))
            o_ref[pl.ds(r0, tq), :] = ot.T.astype(bf16)
            lse_ref[pl.ds(r0, tq), :] = lt.T

        def qpair(t, c):
            qblock(2 * t, True)
            qblock(2 * t + 1, False)
            return c

        assert tk == 2 * tq
        lax.fori_loop(0, S // tk, qpair, 0)

    pair = pl.BlockSpec((S, 128), lambda p: (0, p))
    return _call(
        order, body, (qkva, F), name="fox_fwd", grid=(4,),
        in_specs=[pl.BlockSpec((S, FOX_BLK), lambda p: (0, p)), pl.BlockSpec((S, 128), lambda p: (0, 0))],
        out_specs=[pair, pair],
        out_shape=[jax.ShapeDtypeStruct((S, FOXW), bf16), jax.ShapeDtypeStruct((S, FOXW), f32)],
        scratch_shapes=[pltpu.VMEM((2, S, 128), bf16)] * 2 + [pltpu.VMEM((S // tk, 128, tk), bf16)],
        compiler_params=_params(("parallel",)),
    )


def _permute_in(dst, src, r):
    L = S // r
    for rho in range(r):
        dst[rho * L:(rho + 1) * L, :] = src[pl.ds(rho, L, stride=r), :]


def _permute_out(dst, src, r):
    L = S // r
    for rho in range(r):
        dst[pl.ds(rho, L, stride=r), :] = src[rho * L:(rho + 1) * L, :]


def _band_width(nbl):
    return BAND if nbl == 1 else 2 * BAND


def _band_geometry(bb, nbl):
    r0 = pl.multiple_of(bb * BAND, BAND)
    if nbl == 1:
        k0 = r0
    else:
        k0 = pl.multiple_of(jnp.maximum(bb - 1, 0) * BAND, BAND)
    sub0 = (bb - lax.rem(bb, nbl)) * BAND
    qi = r0 + lax.broadcasted_iota(jnp.int32, (BAND, 1), 0)
    ki = k0 + lax.broadcasted_iota(jnp.int32, (1, _band_width(nbl)), 1)
    diff = qi - ki
    valid = (diff >= 0) & (diff <= BAND) & (ki >= sub0)
    return r0, k0, valid


def _dil_views(ref):
    return [[ref.at[:, pl.ds((3 * role + g) * 128, 128)] for g in range(3)] for role in range(3)]


DIL_UNROLL = 4


def _dil_in_specs():
    return [pl.BlockSpec((S, 128), lambda p, k=k: (0, 9 * p + k)) for k in range(9)]


def _dil_fwd(order, qkvb):
    def body(*refs):
        q_refs, k_refs, v_refs = refs[0:3], refs[3:6], refs[6:9]
        ob_ref, lse_ref, qp, kp, vp, op, lp = refs[9:16]
        on, ln = refs[16:19], refs[19:22]
        _, hm = _head_masks()
        for g, r in enumerate(DIL):
            nbl = S // r // BAND
            if r == 1:
                qs_, ks_, vs_, od, ld = q_refs[g], k_refs[g], v_refs[g], on[g], ln[g]
            else:
                _permute_in(qp, q_refs[g], r)
                _permute_in(kp, k_refs[g], r)
                _permute_in(vp, v_refs[g], r)
                qs_, ks_, vs_, od, ld = qp, kp, vp, op, lp

            def blk(t, c, qs_=qs_, ks_=ks_, vs_=vs_, od=od, ld=ld, nbl=nbl):
                work = []
                for u in range(DIL_UNROLL):
                    r0, k0, valid = _band_geometry(DIL_UNROLL * t + u, nbl)
                    q = qs_[pl.ds(r0, BAND), :] * 0.125
                    kw = ks_[pl.ds(k0, _band_width(nbl)), :].astype(bf16)
                    vw = vs_[pl.ds(k0, _band_width(nbl)), :]
                    for hh in (0, 1):
                        qh = jnp.where(hm[hh], q, 0.0).astype(bf16)
                        work.append((u, hh, r0, valid, vw, _dot_nt(qh, kw)))
                o = [jnp.zeros((BAND, 128), f32)] * DIL_UNROLL
                lse = [jnp.zeros((BAND, 128), f32)] * DIL_UNROLL
                for u, hh, r0, valid, vw, s in work:
                    s = jnp.where(valid, s, NEG)
                    m = jnp.max(s, axis=1, keepdims=True)
                    pr = jnp.exp(s - m)
                    l = jnp.sum(pr, axis=1, keepdims=True)
                    vm = jnp.where(hm[hh], vw, 0.0).astype(bf16)
                    o[u] = o[u] + _dot((pr / l).astype(bf16), vm)
                    lse[u] = jnp.where(hm[hh], m + jnp.log(l), lse[u])
                    if hh == 1:
                        od[pl.ds(r0, BAND), :] = o[u]
                        ld[pl.ds(r0, BAND), :] = lse[u]
                return c

            lax.fori_loop(0, S // BAND // DIL_UNROLL, blk, 0)
            if r != 1:
                _permute_out(on[g], op, r)
                _permute_out(ln[g], lp, r)

        def combine(i, c):
            r0 = pl.multiple_of(i * TQ, TQ)
            ls = [ln[g][pl.ds(r0, TQ), :] for g in range(3)]
            mx = jnp.maximum(jnp.maximum(ls[0], ls[1]), ls[2])
            es = [jnp.exp(l - mx) for l in ls]
            tot = (es[0] + es[1]) + es[2]
            acc = (es[0] / tot) * on[0][pl.ds(r0, TQ), :]
            acc = acc + (es[1] / tot) * on[1][pl.ds(r0, TQ), :]
            acc = acc + (es[2] / tot) * on[2][pl.ds(r0, TQ), :]
            ob_ref[pl.ds(r0, TQ), :] = acc.astype(bf16)
            lse_ref[pl.ds(r0, TQ), :] = mx + jnp.log(tot)
            return c

        lax.fori_loop(0, S // TQ, combine, 0)

    out_blk = pl.BlockSpec((S, 128), lambda p: (0, p))
    return _call(
        order, body, [qkvb] * 9, name="dil_fwd", grid=(2,),
        in_specs=_dil_in_specs(), out_specs=[out_blk, out_blk],
        out_shape=[jax.ShapeDtypeStruct((S, DILOUT), bf16), jax.ShapeDtypeStruct((S, DILOUT), f32)],
        scratch_shapes=[pltpu.VMEM((S, 128), f32)] * 11,
        compiler_params=_params(("parallel",)),
    )


def _branch_mix(order, oa, ob, was, wbs, gates):
    tm = 512

    def body(oa_ref, ob_ref, wa_ref, wb_ref, g_ref, ya_ref, yb_ref, mix_ref):
        oa_b, ob_b = oa_ref[...], ob_ref[...]
        for q in range(NCHIP):
            cols = slice(q * 256, (q + 1) * 256)
            ya = _dot(oa_b, wa_ref[q])
            yb = _dot(ob_b, wb_ref[q])
            ya_ref[:, cols] = ya.astype(bf16)
            yb_ref[:, cols] = yb.astype(bf16)
            ga = g_ref[:, q * 256:(q + 1) * 256].astype(f32)
            gb = g_ref[:, D + q * 256:D + (q + 1) * 256].astype(f32)
            mix_ref[:, cols] = (jax.nn.sigmoid(ga) * ya + jax.nn.sigmoid(gb) * yb).astype(bf16)

    row = lambda w: pl.BlockSpec((tm, w), lambda i: (i, 0))
    full3 = lambda a: pl.BlockSpec(a.shape, lambda i: (0, 0, 0))
    return _call(
        order, body, (oa, ob, was, wbs, gates), name="branch_mix", grid=(S // tm,),
        in_specs=[row(FOXW), row(DILOUT), full3(was), full3(wbs), row(2 * D)],
        out_specs=[row(D), row(D), row(D)],
        out_shape=[jax.ShapeDtypeStruct((S, D), bf16), jax.ShapeDtypeStruct((S, D), bf16),
                   jax.ShapeDtypeStruct((S, D), bf16)],
        compiler_params=_params(("parallel",)),
    )


def _outproj_norm(order, mixed, wout, x, g2):
    tm = 512

    def body(m_ref, w_ref, x_ref, g_ref, x2_ref, h2_ref):
        x2 = x_ref[...] + _dot(m_ref[...], w_ref[...])
        x2_ref[...] = x2
        r = lax.rsqrt(jnp.mean(x2 * x2, axis=-1, keepdims=True) + EPS)
        h2_ref[...] = ((x2 * r) * g_ref[...]).astype(bf16)

    row = pl.BlockSpec((tm, D), lambda i: (i, 0))
    return _call(
        order, body, (mixed, wout, x, g2), name="outproj_norm", grid=(S // tm,),
        in_specs=[row, pl.BlockSpec((D, D), lambda i: (0, 0)), row, pl.BlockSpec((1, D), lambda i: (0, 0))],
        out_specs=[row, row],
        out_shape=[jax.ShapeDtypeStruct((S, D), f32), jax.ShapeDtypeStruct((S, D), bf16)],
        compiler_params=_params(("parallel",)),
    )


def _mlp_up(order, h2, wups):
    tm = 1024

    def body(h_ref, w_ref, ru_ref, a_ref):
        ru = jnp.maximum(_dot(h_ref[...], w_ref[...]), 0.0)
        ru_ref[...] = ru.astype(bf16)
        a_ref[...] = (ru * ru).astype(bf16)

    out = pl.BlockSpec((tm, D), lambda q, i: (i, q))
    return _call(
        order, body, (h2, wups), name="mlp_up", grid=(NCHIP, S // tm),
        in_specs=[pl.BlockSpec((tm, D), lambda q, i: (i, 0)), pl.BlockSpec((None, D, D), lambda q, i: (q, 0, 0))],
        out_specs=[out, out],
        out_shape=[jax.ShapeDtypeStruct((S, DFF), bf16), jax.ShapeDtypeStruct((S, DFF), bf16)],
        compiler_params=_params(("parallel", "parallel")),
    )


def _mlp_down_loss(order, a, wdown, x2, g3, tgt):
    tm = 512

    def body(a_ref, w_ref, x2_ref, g_ref, t_ref, dx_ref, dxb_ref, dg_ref, loss_ref):
        i = pl.program_id(0)
        x3 = x2_ref[...] + _dot(a_ref[...], w_ref[...])
        r = lax.rsqrt(jnp.mean(x3 * x3, axis=-1, keepdims=True) + EPS)
        xh = x3 * r
        g = g_ref[...]
        e = xh * g - t_ref[...]
        part = 0.5 * jnp.sum(jnp.mean(e * e, axis=-1, keepdims=True), axis=0, keepdims=True)
        dy = e * (1.0 / D)
        gdy = dy * g
        dx = r * (gdy - xh * jnp.mean(gdy * xh, axis=-1, keepdims=True))
        dx_ref[...] = dx
        dxb_ref[...] = dx.astype(bf16)

        @pl.when(i == 0)
        def _():
            dg_ref[...] = jnp.zeros_like(dg_ref)
            loss_ref[...] = jnp.zeros_like(loss_ref)

        dg_ref[...] += jnp.sum(dy * xh, axis=0, keepdims=True)
        loss_ref[...] += jnp.broadcast_to(part, (1, 128))

    row = pl.BlockSpec((tm, D), lambda i: (i, 0))
    vec = pl.BlockSpec((1, D), lambda i: (0, 0))
    return _call(
        order, body, (a, wdown, x2, g3, tgt), name="mlp_down_loss", grid=(S // tm,),
        in_specs=[pl.BlockSpec((tm, DFF), lambda i: (i, 0)), pl.BlockSpec((DFF, D), lambda i: (0, 0)), row, vec, row],
        out_specs=[row, row, vec, pl.BlockSpec((1, 128), lambda i: (0, 0))],
        out_shape=[jax.ShapeDtypeStruct((S, D), f32), jax.ShapeDtypeStruct((S, D), bf16),
                   jax.ShapeDtypeStruct((1, D), f32), jax.ShapeDtypeStruct((1, 128), f32)],
        compiler_params=_params(("arbitrary",)),
    )


def _mlp_down_bwd(order, dx3b, wdown, u):
    tm = 512

    def body(d_ref, w_ref, u_ref, du_ref):
        d = d_ref[...]
        for q in range(NCHIP):
            cols = slice(q * D, (q + 1) * D)
            da = _dot_nt(d, w_ref[cols, :])
            du_ref[:, cols] = (da * (2.0 * u_ref[:, cols].astype(f32))).astype(bf16)

    return _call(
        order, body, (dx3b, wdown, u), name="mlp_down_bwd", grid=(S // tm,),
        in_specs=[pl.BlockSpec((tm, D), lambda i: (i, 0)), pl.BlockSpec((DFF, D), lambda i: (0, 0)),
                  pl.BlockSpec((tm, DFF), lambda i: (i, 0))],
        out_specs=pl.BlockSpec((tm, DFF), lambda i: (i, 0)),
        out_shape=jax.ShapeDtypeStruct((S, DFF), bf16),
        compiler_params=_params(("parallel",)),
    )


def _mlp_up_bwd(order, du, wups, x2, dx3, g2):
    tm = 512

    def body(du_ref, w_ref, x2_ref, dx3_ref, g_ref, dx2_ref, dx2b_ref, dg_ref):
        i = pl.program_id(0)
        dh = jnp.zeros((tm, D), f32)
        for q in range(NCHIP):
            dh = dh + _dot_nt(du_ref[:, q * D:(q + 1) * D], w_ref[q])
        x2 = x2_ref[...]
        r = lax.rsqrt(jnp.mean(x2 * x2, axis=-1, keepdims=True) + EPS)
        xh = x2 * r
        gdh = dh * g_ref[...]
        dx2 = dx3_ref[...] + r * (gdh - xh * jnp.mean(gdh * xh, axis=-1, keepdims=True))
        dx2_ref[...] = dx2
        dx2b_ref[...] = dx2.astype(bf16)

        @pl.when(i == 0)
        def _():
            dg_ref[...] = jnp.zeros_like(dg_ref)

        dg_ref[...] += jnp.sum(dh * xh, axis=0, keepdims=True)

    row = pl.BlockSpec((tm, D), lambda i: (i, 0))
    vec = pl.BlockSpec((1, D), lambda i: (0, 0))
    return _call(
        order, body, (du, wups, x2, dx3, g2), name="mlp_up_bwd", grid=(S // tm,),
        in_specs=[pl.BlockSpec((tm, DFF), lambda i: (i, 0)), pl.BlockSpec((NCHIP, D, D), lambda i: (0, 0, 0)),
                  row, row, vec],
        out_specs=[row, row, vec],
        out_shape=[jax.ShapeDtypeStruct((S, D), f32), jax.ShapeDtypeStruct((S, D), bf16),
                   jax.ShapeDtypeStruct((1, D), f32)],
        compiler_params=_params(("arbitrary",)),
    )


def _gate_bwd(order, dx2b, wout, gates, ya, yb):
    tm = 512

    def body(d_ref, w_ref, g_ref, ya_ref, yb_ref, dya_ref, dyb_ref, dproj_ref):
        dm = _dot_nt(d_ref[...], w_ref[...])
        sa = jax.nn.sigmoid(g_ref[:, 0:D].astype(f32))
        sb = jax.nn.sigmoid(g_ref[:, D:2 * D].astype(f32))
        dya_ref[...] = (dm * sa).astype(bf16)
        dyb_ref[...] = (dm * sb).astype(bf16)
        dproj_ref[:, 0:D] = (dm * ya_ref[...].astype(f32) * (sa * (1.0 - sa))).astype(bf16)
        dproj_ref[:, D:2 * D] = (dm * yb_ref[...].astype(f32) * (sb * (1.0 - sb))).astype(bf16)

    row = lambda w: pl.BlockSpec((tm, w), lambda i: (i, 0))
    return _call(
        order, body, (dx2b, wout, gates, ya, yb), name="gate_bwd", grid=(S // tm,),
        in_specs=[row(D), pl.BlockSpec((D, D), lambda i: (0, 0)), row(2 * D), row(D), row(D)],
        out_specs=[row(D), row(D), pl.BlockSpec((tm, 2 * D), lambda i: (i, F_G // (2 * D)))],
        out_shape=[jax.ShapeDtypeStruct((S, D), bf16), jax.ShapeDtypeStruct((S, D), bf16),
                   jax.ShapeDtypeStruct((S, NP), bf16)],
        compiler_params=_params(("parallel",)),
    )


def _branch_bwd(order, dya, dyb, was, wbs):
    tm = 512

    def body(dya_ref, dyb_ref, wa_ref, wb_ref, doa_ref, dob_ref):
        doa = jnp.zeros((tm, FOXW), f32)
        dob = jnp.zeros((tm, DILOUT), f32)
        for q in range(NCHIP):
            cols = slice(q * 256, (q + 1) * 256)
            doa = doa + _dot_nt(dya_ref[:, cols], wa_ref[q])
            dob = dob + _dot_nt(dyb_ref[:, cols], wb_ref[q])
        doa_ref[...] = doa.astype(bf16)
        dob_ref[...] = dob

    row = lambda w: pl.BlockSpec((tm, w), lambda i: (i, 0))
    full3 = lambda a: pl.BlockSpec(a.shape, lambda i: (0, 0, 0))
    return _call(
        order, body, (dya, dyb, was, wbs), name="branch_bwd", grid=(S // tm,),
        in_specs=[row(D), row(D), full3(was), full3(wbs)],
        out_specs=[row(FOXW), row(DILOUT)],
        out_shape=[jax.ShapeDtypeStruct((S, FOXW), bf16), jax.ShapeDtypeStruct((S, DILOUT), f32)],
        compiler_params=_params(("parallel",)),
    )


def _branch_wgrad(order, oa, ob, dya, dyb):
    def body(oa_ref, ob_ref, dya_ref, dyb_ref, dwa_ref, dwb_ref):
        dwa_ref[...] = _dot_tn(oa_ref[...], dya_ref[...])
        dwb_ref[...] = _dot_tn(ob_ref[...], dyb_ref[...])

    full = lambda w: pl.BlockSpec((S, w), lambda q: (0, 0))
    colq = pl.BlockSpec((S, 256), lambda q: (0, q))
    return _call(
        order, body, (oa, ob, dya, dyb), name="branch_wgrad", grid=(NCHIP,),
        in_specs=[full(FOXW), full(DILOUT), colq, colq],
        out_specs=[pl.BlockSpec((None, FOXW, 256), lambda q: (q, 0, 0)),
                   pl.BlockSpec((None, DILOUT, 256), lambda q: (q, 0, 0))],
        out_shape=[jax.ShapeDtypeStruct((NCHIP, FOXW, 256), f32), jax.ShapeDtypeStruct((NCHIP, DILOUT, 256), f32)],
        compiler_params=_params(("parallel",)),
    )


def _fox_bwd(order, qkva, doa, oa, lse, F, dproj):
    tq, tk = FOX_TQ, FOX_TK

    def body(qkv_ref, do_ref, o_ref, lse_ref, F_ref, _dproj_in, dF_ref, dqkv_ref, qa, ka, da, va, kat,
             dk_scr, dv_scr, dqt_scr):
        p = pl.program_id(0)
        lane, hm = _head_masks()
        keyi = lax.broadcasted_iota(jnp.int32, (tk, 1), 0)
        qryi = lax.broadcasted_iota(jnp.int32, (1, tq), 1)

        def prep(i, c):
            rows = pl.ds(pl.multiple_of(i * tk, tk), tk)
            _fox_operands(qkv_ref, F_ref, lse_ref, qa, ka, p, rows)
            do = do_ref[rows, :].astype(f32)
            prod = do * o_ref[rows, :].astype(f32)
            v = qkv_ref[rows, 256:384].astype(f32)
            for hh in (0, 1):
                free = (1 - hh) * HD
                delta = jnp.sum(jnp.where(hm[hh], prod, 0.0), axis=1, keepdims=True)
                da[hh, rows, :] = _set_lanes(jnp.where(hm[hh], do, 0.0), lane, free,
                                             [-t for t in _f32_parts(delta)]).astype(bf16)
                va[hh, rows, :] = _set_lanes(v, lane, free, [1.0] * 3).astype(bf16)
                kat[hh, i] = ka[hh, rows, :].astype(f32).T.astype(bf16)
                dk_scr[hh, rows, :] = jnp.zeros((tk, 128), f32)
                dv_scr[hh, rows, :] = jnp.zeros((tk, 128), f32)
            return c

        lax.fori_loop(0, S // tk, prep, 0)

        def qblock(i, first_half):
            r0 = pl.multiple_of(i * tq, tq)
            qrows = pl.ds(r0, tq)
            qh = [qa[hh, qrows, :] for hh in (0, 1)]
            dh = [da[hh, qrows, :] for hh in (0, 1)]
            dqt_scr[...] = jnp.zeros_like(dqt_scr)

            def kv(jb, c2, masked, width):
                keys = pl.ds(pl.multiple_of(jb * tk, tk), width)
                sts = [_dot_nt(ka[hh, keys, :], qh[hh]) for hh in (0, 1)]
                dps = [_dot_nt(va[hh, keys, :], dh[hh]) for hh in (0, 1)]
                for hh in (0, 1):
                    pt = jnp.exp(sts[hh])
                    if masked:
                        pt = jnp.where(jb * tk + keyi[0:width] <= r0 + qryi, pt, 0.0)
                    dsb = (pt * dps[hh]).astype(bf16)
                    dv_scr[hh, keys, :] += _dot(pt.astype(bf16), dh[hh])
                    dk_scr[hh, keys, :] += _dot(dsb, qh[hh])
                    dqt_scr[hh] += _dot(kat[hh, jb, :, 0:width], dsb)
                return c2

            last = (r0 + tq - 1) // tk
            lax.fori_loop(0, last, lambda j, c2: kv(j, c2, False, tk), 0)
            kv(last, 0, True, tk // 2 if first_half else tk)
            dq0, dq1 = dqt_scr[0].T, dqt_scr[1].T
            dqkv_ref[qrows, 0:128] = (jnp.where(hm[0], dq0, dq1) * 0.125).astype(bf16)
            dF_ref[qrows, :] = jnp.where(lane == 0, dq0[:, HD:HD + 1], jnp.where(lane == 1, dq1[:, 0:1], 0.0))

        def qpair(t, c):
            qblock(2 * t, True)
            qblock(2 * t + 1, False)
            return c

        assert tk == 2 * tq
        lax.fori_loop(0, S // tk, qpair, 0)

        def finish(i, c):
            rows = pl.ds(pl.multiple_of(i * tq, tq), tq)
            dk0, dk1 = dk_scr[0, rows, :], dk_scr[1, rows, :]
            dqkv_ref[rows, 128:256] = jnp.where(hm[0], dk0, dk1).astype(bf16)
            dqkv_ref[rows, 256:384] = jnp.where(hm[0], dv_scr[0, rows, :], dv_scr[1, rows, :]).astype(bf16)
            cs = jnp.where(lane == 0, dk0[:, HD + L_ONE:HD + L_ONE + 1],
                           jnp.where(lane == 1, dk1[:, L_ONE:L_ONE + 1], 0.0))
            dF_ref[rows, :] = dF_ref[rows, :] - cs
            return c

        lax.fori_loop(0, S // tq, finish, 0)

    pair = pl.BlockSpec((S, 128), lambda p: (0, p))
    return _call(
        order, body, (qkva, doa, oa, lse, F, dproj), name="fox_bwd", grid=(4,),
        in_specs=[pl.BlockSpec((S, FOX_BLK), lambda p: (0, p)), pair, pair, pair,
                  pl.BlockSpec((S, 128), lambda p: (0, 0)), pl.BlockSpec(memory_space=pl.ANY)],
        out_specs=[pair, pl.BlockSpec((S, FOX_BLK), lambda p: (0, F_FOX // FOX_BLK + p))],
        out_shape=[jax.ShapeDtypeStruct((S, FOXW), f32), jax.ShapeDtypeStruct((S, NP), bf16)],
        input_output_aliases={5: 1},
        scratch_shapes=[pltpu.VMEM((2, S, 128), bf16)] * 4 + [pltpu.VMEM((2, S // tk, 128, tk), bf16)]
        + [pltpu.VMEM((2, S, 128), f32)] * 2 + [pltpu.VMEM((2, 128, tq), f32)],
        compiler_params=_params(("parallel",)),
    )


def _forget_bwd(order, dF, fa, bpad, dproj):
    nb = S // TQ

    def body(dF_ref, fa_ref, b_ref, _dproj_in, db_ref, dfa_ref):
        rr = lax.broadcasted_iota(jnp.int32, (TQ, TQ), 0)
        cc = lax.broadcasted_iota(jnp.int32, (TQ, TQ), 1)
        upper = (cc >= rr).astype(bf16)
        lane = lax.broadcasted_iota(jnp.int32, (1, 128), 1)
        carry = jnp.zeros((1, 128), f32)
        db = jnp.zeros((1, 128), f32)
        for b in reversed(range(nb)):
            cols = jnp.zeros((TQ, 128), f32)
            for h in range(8):
                c0 = (h // 2) * 128 + h % 2
                cols = jnp.where(lane == h, dF_ref[b * TQ:(b + 1) * TQ, c0:c0 + 1], cols)
            dlf = carry
            for part in _split3(cols):
                dlf = dlf + _dot(upper, part)
            carry = carry + jnp.sum(cols, axis=0, keepdims=True)
            z = fa_ref[b * TQ:(b + 1) * TQ, :] + b_ref[...]
            dz = jnp.where(lane < 8, dlf * jax.nn.sigmoid(-z), 0.0)
            dfa_ref[b * TQ:(b + 1) * TQ, 0:128] = dz.astype(bf16)
            dfa_ref[b * TQ:(b + 1) * TQ, 128:256] = jnp.zeros((TQ, 128), bf16)
            db = db + jnp.sum(dz, axis=0, keepdims=True)
        db_ref[...] = db

    whole = lambda a: pl.BlockSpec(a.shape, lambda i: (0,) * a.ndim)
    return _call(
        order, body, (dF, fa, bpad, dproj), name="forget_bwd", grid=(1,),
        in_specs=[whole(dF), whole(fa), whole(bpad), pl.BlockSpec(memory_space=pl.ANY)],
        out_specs=[pl.BlockSpec((1, 128), lambda i: (0, 0)), pl.BlockSpec((S, 256), lambda i: (0, F_FA // 256))],
        out_shape=[jax.ShapeDtypeStruct((1, 128), f32), jax.ShapeDtypeStruct((S, NP), bf16)],
        input_output_aliases={3: 1},
        compiler_params=_params(("arbitrary",)),
    )


def _dil_bwd(order, qkvb, dob, ob, lseb, rope, dproj):
    c_t, s1_t, s2_t = rope

    def body(*refs):
        q_refs, k_refs, v_refs = refs[0:3], refs[3:6], refs[6:9]
        dob_ref, ob_ref, lse_ref, c_ref, s1_ref, s2_ref, _dproj_in, dqkv_ref = refs[9:17]
        qp, kp, vp, dop, lp, dlp, dln, dqp, dkp, dvp, nat = refs[17:28]
        dq_out, dk_out, dv_out = _dil_views(dqkv_ref)
        _, hm = _head_masks()

        def delta_rows(i, c):
            r0 = pl.multiple_of(i * TQ, TQ)
            prod = dob_ref[pl.ds(r0, TQ), :] * ob_ref[pl.ds(r0, TQ), :].astype(f32)
            d0 = jnp.sum(jnp.where(hm[0], prod, 0.0), axis=1, keepdims=True)
            d1 = jnp.sum(jnp.where(hm[1], prod, 0.0), axis=1, keepdims=True)
            dln[pl.ds(r0, TQ), :] = jnp.where(hm[0], d0, d1)
            return c

        lax.fori_loop(0, S // TQ, delta_rows, 0)

        for g, r in enumerate(DIL):
            nbl = S // r // BAND
            if r == 1:
                srcs = (q_refs[g], k_refs[g], v_refs[g], dob_ref, lse_ref, dln)
            else:
                for dst, src in ((qp, q_refs[g]), (kp, k_refs[g]), (vp, v_refs[g]), (dop, dob_ref),
                                 (lp, lse_ref), (dlp, dln)):
                    _permute_in(dst, src, r)
                srcs = (qp, kp, vp, dop, lp, dlp)
            dkp[...] = jnp.zeros_like(dkp)
            dvp[...] = jnp.zeros_like(dvp)

            def blk(t, c, srcs=srcs, nbl=nbl):
                qs_, ks_, vs_, dos_, ls_, dls_ = srcs
                work = []
                for u in range(DIL_UNROLL):
                    r0, k0, valid = _band_geometry(DIL_UNROLL * t + u, nbl)
                    q = qs_[pl.ds(r0, BAND), :] * 0.125
                    kwf = ks_[pl.ds(k0, _band_width(nbl)), :]
                    kw = kwf.astype(bf16)
                    vw = vs_[pl.ds(k0, _band_width(nbl)), :].astype(bf16)
                    do = dos_[pl.ds(r0, BAND), :]
                    lse = ls_[pl.ds(r0, BAND), :]
                    dlt = dls_[pl.ds(r0, BAND), :]
                    for hh in (0, 1):
                        qh = jnp.where(hm[hh], q, 0.0).astype(bf16)
                        doh = jnp.where(hm[hh], do, 0.0).astype(bf16)
                        kh = jnp.where(hm[hh], kwf, 0.0).astype(bf16)
                        work.append((u, hh, r0, k0, valid, qh, doh, kh, lse[:, hh * HD:hh * HD + 1],
                                     dlt[:, hh * HD:hh * HD + 1], _dot_nt(qh, kw), _dot_nt(doh, vw)))
                for u, hh, r0, k0, valid, qh, doh, kh, lse_h, dlt_h, s, dp in work:
                    if hh == 0:
                        dq = jnp.zeros((BAND, 128), f32)
                        dk = jnp.zeros((_band_width(nbl), 128), f32)
                        dv = jnp.zeros((_band_width(nbl), 128), f32)
                    pr = jnp.where(valid, jnp.exp(s - lse_h), 0.0)
                    dsb = (pr * (dp - dlt_h)).astype(bf16)
                    dv = dv + _dot_tn(pr.astype(bf16), doh)
                    dk = dk + _dot_tn(dsb, qh)
                    dq = dq + _dot(dsb, kh)
                    if hh == 1:
                        dqp[pl.ds(r0, BAND), :] = dq * 0.125
                        dkp[pl.ds(k0, _band_width(nbl)), :] += dk
                        dvp[pl.ds(k0, _band_width(nbl)), :] += dv
                return c

            lax.fori_loop(0, S // BAND // DIL_UNROLL, blk, 0)

            for acc, out, roped in ((dqp, dq_out[g], True), (dkp, dk_out[g], True), (dvp, dv_out[g], False)):
                if r == 1:
                    src = acc
                else:
                    _permute_out(nat, acc, r)
                    src = nat

                def emit(i, c, src=src, out=out, roped=roped):
                    r0 = pl.multiple_of(i * TQ, TQ)
                    d = src[pl.ds(r0, TQ), :]
                    if roped:
                        d = (d * c_ref[pl.ds(r0, TQ), :] + pltpu.roll(d * s1_ref[pl.ds(r0, TQ), :], 8, 1)
                             + pltpu.roll(d * s2_ref[pl.ds(r0, TQ), :], 120, 1))
                    out[pl.ds(r0, TQ), :] = d.astype(bf16)
                    return c

                lax.fori_loop(0, S // TQ, emit, 0)

    pair = pl.BlockSpec((S, 128), lambda p: (0, p))
    tab = pl.BlockSpec((S, 128), lambda p: (0, 0))
    blk_spec = pl.BlockSpec((S, DIL_BLK), lambda p: (0, p))
    return _call(
        order, body, [qkvb] * 9 + [dob, ob, lseb, c_t, s1_t, s2_t, dproj], name="dil_bwd", grid=(2,),
        in_specs=_dil_in_specs() + [pair, pair, pair, tab, tab, tab, pl.BlockSpec(memory_space=pl.ANY)],
        out_specs=blk_spec,
        out_shape=jax.ShapeDtypeStruct((S, NP), bf16),
        input_output_aliases={15: 0},
        scratch_shapes=[pltpu.VMEM((S, 128), f32)] * 11,
        compiler_params=_params(("parallel",)),
    )


def _inproj_bwd(order, dproj, wt, x, dx2, g1):
    tm = 256

    def body(d_ref, w_ref, x_ref, dx2_ref, g_ref, dx_ref, dg_ref):
        i = pl.program_id(0)
        dh = _dot(d_ref[...], w_ref[...])
        xb = x_ref[...]
        r = lax.rsqrt(jnp.mean(xb * xb, axis=-1, keepdims=True) + EPS)
        xh = xb * r
        gdh = dh * g_ref[...]
        dx_ref[...] = dx2_ref[...] + r * (gdh - xh * jnp.mean(gdh * xh, axis=-1, keepdims=True))

        @pl.when(i == 0)
        def _():
            dg_ref[...] = jnp.zeros_like(dg_ref)

        dg_ref[...] += jnp.sum(dh * xh, axis=0, keepdims=True)

    row = pl.BlockSpec((tm, D), lambda i: (i, 0))
    vec = pl.BlockSpec((1, D), lambda i: (0, 0))
    return _call(
        order, body, (dproj, wt, x, dx2, g1), name="inproj_bwd", grid=(S // tm,),
        in_specs=[pl.BlockSpec((tm, NP), lambda i: (i, 0)), pl.BlockSpec((NP, D), lambda i: (0, 0)), row, row, vec],
        out_specs=[row, vec],
        out_shape=[jax.ShapeDtypeStruct((S, D), f32), jax.ShapeDtypeStruct((1, D), f32)],
        compiler_params=_params(("arbitrary",)),
    )


HBM = pl.BlockSpec(memory_space=pltpu.HBM)
SEM = pl.BlockSpec(memory_space=pltpu.SEMAPHORE)
SMALL_ROWS = 8


def _comm_call(name, body, bufs, order, sems_in=(), new_sems=(), behind=()):
    nb, ns, nn = len(bufs), len(sems_in), len(new_sems)
    extra = order.token_for(bufs) + list(behind)

    def kern(*refs):
        off = nb + ns + len(extra)
        body(refs[:nb], refs[nb:nb + ns], refs[off:off + nn])
        refs[-1][...] = jnp.zeros((8, 128), f32)

    res = pl.pallas_call(
        kern, name=name,
        in_specs=[HBM] * nb + [SEM] * ns + [pl.BlockSpec(memory_space=pl.ANY)] * len(extra),
        out_specs=[SEM] * nn + [HBM] * nb + [pl.BlockSpec(memory_space=pltpu.VMEM)],
        out_shape=[pltpu.SemaphoreType.DMA((k,)) for k in new_sems] + [pltpu.HBM(b.shape, b.dtype) for b in bufs]
        + [jax.ShapeDtypeStruct((8, 128), f32)],
        input_output_aliases={i: nn + i for i in range(nb)},
        compiler_params=pltpu.CompilerParams(has_side_effects=pltpu.SideEffectType.DATAFLOW_SIDE_EFFECTING),
    )(*[pltpu.with_memory_space_constraint(b, pltpu.HBM) for b in bufs], *sems_in, *extra)
    order.mark(res[-1])
    return list(res[:nn]), list(res[nn:nn + nb])


def _place():
    x, y, c = lax.axis_index("x"), lax.axis_index("y"), lax.axis_index("c")
    chips = [(1 - x, y), (x, 1 - y), (1 - x, 1 - y)]
    return x, y, c, chips


def _rcopy(src, dst, ssem, rsem, dev):
    return pltpu.make_async_remote_copy(src_ref=src, dst_ref=dst, send_sem=ssem, recv_sem=rsem,
                                        device_id=dev, device_id_type=pl.DeviceIdType.MESH)


def _half(nrows, which):
    return pl.ds(which * (nrows // 2), nrows // 2)


def _ici_copies(stack, group_sizes, ssems, rsems):
    x, y, c, chips = _place()
    me_q = 2 * x + y
    sends, recvs = [], []
    a = 0
    for grp, size in enumerate(group_sizes):
        for k in range(size):
            rows = _half(stack[a].shape[1], c)
            for j, (cx, cy) in enumerate(chips):
                mine = stack[a].at[me_q, rows]
                sends.append(_rcopy(mine, mine, ssems[grp].at[k * 3 + j], rsems[grp].at[k * 3 + j], (cx, cy, c)))
                theirs = stack[a].at[2 * cx + cy, rows]
                recvs.append(_rcopy(theirs, theirs, ssems[grp].at[k * 3 + j], rsems[grp].at[k * 3 + j],
                                    (cx, cy, c)))
            a += 1
    return sends, recvs


def _allgather_start(name, stacks, order):
    n = len(stacks)

    def body(bufs, _, new):
        sends, _r = _ici_copies(bufs, [n], [new[0]], [new[1]])
        for cp in sends:
            cp.start()

    return _comm_call(name, body, stacks, order, new_sems=(3 * n, 3 * n))


def _forward_copies(stack, ssem, rsem):
    x, y, c, chips = _place()
    sib = (x, y, 1 - c)
    sends, recvs = [], []
    for a in range(len(stack)):
        for j, (cx, cy) in enumerate(chips):
            landed = stack[a].at[2 * cx + cy, _half(stack[a].shape[1], c)]
            sends.append(_rcopy(landed, landed, ssem.at[a * 3 + j], rsem.at[a * 3 + j], sib))
            other = stack[a].at[2 * cx + cy, _half(stack[a].shape[1], 1 - c)]
            recvs.append(_rcopy(other, other, ssem.at[a * 3 + j], rsem.at[a * 3 + j], sib))
    return sends, recvs


def _allgather_forward(name, stacks, sems, order, behind=()):
    n = len(stacks)

    def body(bufs, taken, new):
        sends, recvs = _ici_copies(bufs, [n], [taken[0]], [taken[1]])
        fwd, _r = _forward_copies(bufs, new[0], new[1])
        for arrived, onward in zip(recvs, fwd):
            arrived.wait_recv()
            onward.start()
        for cp in sends:
            cp.wait_send()

    return _comm_call(name, body, stacks, order, sems_in=sems, new_sems=(3 * n, 3 * n), behind=behind)


def _allgather_finish(name, stacks, sems, order):
    def body(bufs, taken, _):
        sends, recvs = _forward_copies(bufs, taken[0], taken[1])
        for cp in sends:
            cp.wait_send()
        for cp in recvs:
            cp.wait_recv()

    return _comm_call(name, body, stacks, order, sems_in=sems)[1]


def _window_unit(q, j):
    return C2I[WIN_UNIT0[q] + j]


def _pair_copies(g, t, ssem, rsem, gathered):
    x, y, c, _ = _place()
    sib = (x, y, 1 - c)
    cps, whole = [], []
    for a in range(len(g)):
        if a == 0 and gathered:
            for q in range(NCHIP):
                for j in range(WIN_UNITS // 2):
                    u = jnp.where(c == 0, _window_unit(q, WIN_UNITS // 2 + j), _window_unit(q, j))
                    src = g[0].at[pl.ds(pl.multiple_of(u * UNIT, UNIT), UNIT), :]
                    cps.append(_rcopy(src, t[0].at[q, pl.ds(j * UNIT, UNIT), :], ssem.at[0], rsem.at[0], sib))
            whole.append(_rcopy(t[0], t[0], ssem.at[0], rsem.at[0], sib))
        else:
            cp = _rcopy(g[a].at[:, _half(g[a].shape[1], 1 - c), :], t[a], ssem.at[a], rsem.at[a], sib)
            cps.append(cp)
            whole.append(cp)
    return cps, whole


def _comm_multi(name, parts, order):
    def body(buf_refs, taken, new):
        ib = it = inew = 0
        for pbody, pbufs, psems, pnew, _ in parts:
            pbody(buf_refs[ib:ib + len(pbufs)], taken[it:it + len(psems)], new[inew:inew + len(pnew)])
            ib, it, inew = ib + len(pbufs), it + len(psems), inew + len(pnew)

    sems, bufs = _comm_call(name, body, [b for p in parts for b in p[1]], order,
                            sems_in=[s for p in parts for s in p[2]], new_sems=[k for p in parts for k in p[3]])
    out, ib, inew = [], 0, 0
    for _, pbufs, _, pnew, unpack in parts:
        out.append(unpack(sems[inew:inew + len(pnew)], bufs[ib:ib + len(pbufs)]))
        ib, inew = ib + len(pbufs), inew + len(pnew)
    return out


def _pair_start_part(gs, gathered=False):
    n = len(gs)
    ts = [lax.empty((NCHIP, WIN_ROWS // 2, D) if (a == 0 and gathered) else (NCHIP, g.shape[1] // 2, g.shape[2]), f32)
          for a, g in enumerate(gs)]

    def body(bufs, _, new):
        for cp in _pair_copies(bufs[:n], bufs[n:], new[0], new[1], gathered)[0]:
            cp.start()

    return body, list(gs) + ts, (), (n, n), lambda sems, bufs: (sems, bufs)


def _pair_wait_part(bufs, sems, gathered=False):
    n = len(bufs) // 2

    def body(refs, taken, _):
        for cp in _pair_copies(refs[:n], refs[n:], taken[0], taken[1], gathered)[1]:
            cp.wait_send()
            cp.wait_recv()

    return body, list(bufs), list(sems), (), lambda _, out: (out[:n], out[n:])


def _row_tile(h):
    return min(h, 256)


def _pair_add(order, g, t, q_arr, c_arr, name):
    _, R, C = g.shape
    h = R // 2
    tr = _row_tile(h)
    nblk = h // tr

    def body(q_ref, c_ref, g_ref, t_ref, own_ref, p16_ref):
        s = g_ref[...] + t_ref[...]
        p16_ref[...] = s.astype(bf16)

        @pl.when(pl.program_id(1) == q_ref[0])
        def _():
            own_ref[...] = s

    blk = pl.BlockSpec((None, tr, C), lambda i, q, q_ref, c_ref: (q, i, 0))
    return _call_indexed(
        order, body, (q_arr, c_arr), (g, t), (nblk, NCHIP),
        [pl.BlockSpec((None, tr, C), lambda i, q, q_ref, c_ref: (q, c_ref[0] * nblk + i, 0)), blk],
        [pl.BlockSpec((tr, C), lambda i, q, q_ref, c_ref: (i, 0)), blk],
        name=name,
        out_shape=[jax.ShapeDtypeStruct((h, C), f32), jax.ShapeDtypeStruct((NCHIP, h, C), bf16)],
        compiler_params=_params(("parallel", "arbitrary")),
    )


def _pair_add_gathered(order, dwt, t, q_arr, c_arr, name):
    half_units, half_rows = WIN_UNITS // 2, WIN_ROWS // 2
    table = jnp.asarray([_window_unit(q, j) for q in range(NCHIP) for j in range(WIN_UNITS)], jnp.int32)

    def body(tab_ref, q_ref, c_ref, g_hbm, t_ref, own_ref, p16_ref, buf, sem):
        q = pl.program_id(0)

        def gather(w, slot):
            cps = []
            for j in range(half_units):
                u = tab_ref[w * WIN_UNITS + c_ref[0] * half_units + j]
                cps.append(pltpu.make_async_copy(g_hbm.at[pl.ds(pl.multiple_of(u * UNIT, UNIT), UNIT), :],
                                                 buf.at[slot, pl.ds(j * UNIT, UNIT), :], sem.at[slot]))
            return cps

        @pl.when(q == 0)
        def _():
            for cp in gather(0, 0):
                cp.start()

        @pl.when(q + 1 < NCHIP)
        def _():
            for cp in gather(q + 1, (q + 1) % 2):
                cp.start()

        slot = q % 2
        pltpu.make_async_copy(buf.at[slot], buf.at[slot], sem.at[slot]).wait()
        s = buf[slot] + t_ref[...]
        p16_ref[...] = s.astype(bf16)

        @pl.when(q == q_ref[0])
        def _():
            own_ref[...] = s

    blk = pl.BlockSpec((None, half_rows, D), lambda q, tab_ref, q_ref, c_ref: (q, 0, 0))
    return _call_indexed(
        order, body, (table, q_arr, c_arr), (dwt, t), (NCHIP,),
        [pl.BlockSpec(memory_space=pl.ANY), blk],
        [pl.BlockSpec((half_rows, D), lambda q, tab_ref, q_ref, c_ref: (0, 0)), blk],
        scratch_shapes=[pltpu.VMEM((2, half_rows, D), f32), pltpu.SemaphoreType.DMA((2,))],
        name=name,
        out_shape=[jax.ShapeDtypeStruct((half_rows, D), f32),
                   jax.ShapeDtypeStruct((NCHIP, half_rows, D), bf16)],
        compiler_params=_params(("arbitrary",)),
    )


def _shard_copies(p, r, sm, ssem, rsem):
    x, y, c, chips = _place()
    n = len(p)
    sends, recvs = [], []
    for a in range(n):
        for j, (cx, cy) in enumerate(chips):
            k = a * 3 + j
            sends.append(_rcopy(p[a].at[2 * cx + cy], r[a].at[j], ssem.at[k], rsem.at[k], (cx, cy, c)))
            recvs.append(_rcopy(r[a].at[j], r[a].at[j], ssem.at[k], rsem.at[k], (cx, cy, c)))
    if sm is not None:
        mine = sm.at[4 * x + 2 * y + c]
        for i in range(1, 8):
            px = (1 - x) if i & 4 else x
            py = (1 - y) if i & 2 else y
            pc = (1 - c) if i & 1 else c
            k = 3 * n + i - 1
            sends.append(_rcopy(mine, mine, ssem.at[k], rsem.at[k], (px, py, pc)))
            slot = sm.at[4 * px + 2 * py + pc]
            recvs.append(_rcopy(slot, slot, ssem.at[k], rsem.at[k], (px, py, pc)))
    return sends, recvs


def _shard_start_part(p16s, sm=None):
    n = len(p16s)
    rs = [lax.empty((3,) + p.shape[1:], bf16) for p in p16s]
    extra = [] if sm is None else [sm]
    nsem = 3 * n + (7 if sm is not None else 0)

    def body(bufs, _, new):
        sends, _r = _shard_copies(bufs[:n], bufs[n:2 * n], bufs[2 * n] if extra else None, new[0], new[1])
        for cp in sends:
            cp.start()

    return body, list(p16s) + rs + extra, (), (nsem, nsem), lambda sems, bufs: (sems, bufs)


def _shard_wait_part(bufs, sems, n):
    has_sm = len(bufs) > 2 * n

    def body(refs, taken, _):
        sends, recvs = _shard_copies(refs[:n], refs[n:2 * n], refs[2 * n] if has_sm else None, taken[0], taken[1])
        for cp in sends:
            cp.wait_send()
        for cp in recvs:
            cp.wait_recv()

    return body, list(bufs), list(sems), (), lambda _, out: (out[n:2 * n], (out[2 * n] if has_sm else None))


def _shard_sum(order, own, r, c_arr, name):
    h, C = own.shape
    tr = _row_tile(h)
    nblk = h // tr

    def body(c_ref, p_ref, r_ref, o_ref):
        s = p_ref[...]
        for j in range(3):
            s = s + r_ref[j].astype(f32)
        o_ref[...] = s

    return _call_indexed(
        order, body, (c_arr,), (own, r), (nblk,),
        [pl.BlockSpec((tr, C), lambda i, c_ref: (i, 0)), pl.BlockSpec((3, tr, C), lambda i, c_ref: (0, i, 0))],
        pl.BlockSpec((tr, C), lambda i, c_ref: (c_ref[0] * nblk + i, 0)),
        name=name, out_shape=jax.ShapeDtypeStruct((2 * h, C), f32),
        compiler_params=_params(("parallel",)),
    )


def _swap_copies(full, ssem, rsem):
    x, y, c, _ = _place()
    sends, recvs = [], []
    for a in range(len(full)):
        mine = full[a].at[_half(full[a].shape[0], c)]
        sends.append(_rcopy(mine, mine, ssem.at[a], rsem.at[a], (x, y, 1 - c)))
        other = full[a].at[_half(full[a].shape[0], 1 - c)]
        recvs.append(_rcopy(other, other, ssem.at[a], rsem.at[a], (x, y, 1 - c)))
    return sends, recvs


def _swap_start_part(fulls):
    n = len(fulls)

    def body(bufs, _, new):
        for cp in _swap_copies(bufs, new[0], new[1])[0]:
            cp.start()

    return body, list(fulls), (), (n, n), lambda sems, bufs: (sems, bufs)


def _swap_wait_part(fulls, sems):
    def body(refs, taken, _):
        sends, recvs = _swap_copies(refs, taken[0], taken[1])
        for cp in sends:
            cp.wait_send()
        for cp in recvs:
            cp.wait_recv()

    return body, list(fulls), list(sems), (), lambda _, out: out


def _small_sum(order, sm):
    def body(sm_ref, o_ref):
        s = sm_ref[0]
        for d in range(1, 8):
            s = s + sm_ref[d]
        o_ref[...] = s

    return _call(order, body, (sm,), name="small_grad_sum", out_shape=jax.ShapeDtypeStruct((SMALL_ROWS, D), f32))


def _adamw_math(w, g, m, v):
    m = ADAM_B1 * m + (1.0 - ADAM_B1) * g
    v = ADAM_B2 * v + (1.0 - ADAM_B2) * (g * g)
    m_hat = m / (1.0 - ADAM_B1 ** ADAM_STEP)
    v_hat = v / (1.0 - ADAM_B2 ** ADAM_STEP)
    return -ADAM_LR * (m_hat / (jnp.sqrt(v_hat) + ADAM_EPS) + ADAM_WD * w), m, v


def _adamw_small(order, ws, gs, ms, vs, name):
    n = len(ws)

    def body(*refs):
        for i in range(n):
            res = _adamw_math(*[refs[k * n + i][...] for k in range(4)])
            for k in range(3):
                refs[4 * n + 3 * i + k][...] = res[k]

    out = _call(order, body, list(ws) + list(gs) + list(ms) + list(vs), name=name,
                out_shape=[jax.ShapeDtypeStruct(w.shape, f32) for w in ws for _ in range(3)])
    return [out[3 * i:3 * i + 3] for i in range(n)]


def _adamw(order, w, g, m, v, name):
    R, C = w.shape
    if R <= 256 or R % 256 == 0:
        tr, tc = min(R, 256), C
    else:
        tr, tc = R, 128

    def body(w_ref, g_ref, m_ref, v_ref, d_ref, nm_ref, nv_ref):
        d_ref[...], nm_ref[...], nv_ref[...] = _adamw_math(w_ref[...], g_ref[...], m_ref[...], v_ref[...])

    blk = pl.BlockSpec((tr, tc), lambda i, j: (i, j))
    return _call(
        order, body, (w, g, m, v), name=name, grid=(R // tr, C // tc), in_specs=[blk] * 4, out_specs=[blk] * 3,
        out_shape=[jax.ShapeDtypeStruct((R, C), f32)] * 3,
        compiler_params=_params(("parallel", "parallel")),
    )


def _feature_rows(w):
    return jnp.transpose(w, (2, 0, 1))


WIN_STEP = 128
WIN_PIECE = 3 * WIN_STEP


def _window_stacks(order, w, q_arr):
    n_piece = -(-SHARD_IN // WIN_PIECE)
    steps = WIN_ROWS // WIN_STEP
    head, tail = UNIT, WIN_STEP
    assert max(OWN_ROW0) < head and OWN_ROW0[1] + FA_AT == UNIT and WIN_ROWS - tail <= SHARD_IN - N_FA
    rest = SHARD_IN - (n_piece - 1) * WIN_PIECE

    def body(q_ref, w_ref, win_ref, fa_ref, buf, fabuf, sem):
        i = pl.program_id(0)
        q = q_ref[0]
        chip1 = q == 1
        row0 = jnp.where(q == 0, OWN_ROW0[0], jnp.where(chip1, OWN_ROW0[1], jnp.where(q == 2, OWN_ROW0[2], OWN_ROW0[3])))
        skip = jnp.where(chip1, N_FA, 0)

        def copy(src0, dst0, n, slot):
            return pltpu.make_async_copy(w_ref.at[pl.ds(src0, n)], buf.at[pl.ds(dst0, n)], sem.at[slot])

        def piece(j, on_chip1):
            if j == 0 and on_chip1:
                return [copy(0, OWN_ROW0[1], FA_AT, 0),
                        copy(FA_AT + N_FA, UNIT, WIN_PIECE - FA_AT, n_piece)]
            if j == 0:
                return [copy(0, row0, WIN_PIECE, 0)]
            if j == n_piece - 1:
                n = rest - (N_FA if on_chip1 else 0)
                return [copy(SHARD_IN - n, row0 + SHARD_IN - skip - n, n, j)]
            return [copy(j * WIN_PIECE + skip, row0 + j * WIN_PIECE, WIN_PIECE, j)]

        def both(j, act):
            if 0 < j < n_piece - 1:
                for c in piece(j, False):
                    act(c)
                return
            for on_chip1 in (False, True):
                @pl.when(chip1 if on_chip1 else jnp.logical_not(chip1))
                def _():
                    for c in piece(j, on_chip1):
                        act(c)

        fa_copy = pltpu.make_async_copy(w_ref.at[pl.ds(FA_AT, N_FA)], fabuf.at[pl.ds(0, N_FA)], sem.at[n_piece + 1])

        @pl.when(i == 0)
        def _():
            buf[pl.ds(0, head)] = jnp.zeros((head, 1, D), f32)
            buf[pl.ds(WIN_ROWS - tail, tail)] = jnp.zeros((tail, 1, D), f32)
            fabuf[pl.ds(N_FA, FA_ROWS - N_FA)] = jnp.zeros((FA_ROWS - N_FA, 1, D), f32)
            fa_copy.start()
            for j in range(n_piece):
                both(j, lambda c: c.start())
            fa_copy.wait()
            fa_ref[...] = fabuf[...].reshape(FA_ROWS, D).astype(bf16)

        for j in range(n_piece):
            @pl.when(i == j * (WIN_PIECE // WIN_STEP))
            def _():
                both(j, lambda c: c.wait())

        win_ref[...] = buf[pl.ds(pl.multiple_of(i * WIN_STEP, WIN_STEP), WIN_STEP)].reshape(WIN_STEP, D).astype(bf16)

    return _call_indexed(
        order, body, (q_arr,), (w,), (steps,), [pl.BlockSpec(memory_space=pl.ANY)],
        [pl.BlockSpec((None, WIN_STEP, D), lambda i, q: (q[0], i, 0)),
         pl.BlockSpec((None, FA_ROWS, D), lambda i, q: (q[0], 0, 0))],
        scratch_shapes=[pltpu.VMEM((WIN_ROWS, 1, D), f32), pltpu.VMEM((FA_ROWS, 1, D), f32),
                        pltpu.SemaphoreType.DMA((n_piece + 2,))],
        name="window_w_in", out_shape=[jax.ShapeDtypeStruct((NCHIP, WIN_ROWS, D), bf16),
                                       jax.ShapeDtypeStruct((NCHIP, FA_ROWS, D), bf16)],
        compiler_params=_params(("arbitrary",)),
    )


def _unfeature_rows(a):
    return jnp.transpose(a, (1, 2, 0))


ADAM_IN_ROWS = 134
ADAM_IN_STEPS = SHARD_IN // ADAM_IN_ROWS
ADAM_IN_CHUNK = 136
ADAM_IN_CHUNKS = ADAM_IN_STEPS + 1
ADAM_IN_BUF = WIN_ROWS + N_FA


def _adamw_w_in(order, w, gwin, gfa, m, v, q_arr):
    assert ADAM_IN_CHUNK * ADAM_IN_STEPS < WIN_ROWS <= ADAM_IN_CHUNK * ADAM_IN_CHUNKS
    assert OWN_ROW0[NCHIP - 1] + ADAM_IN_ROWS <= 2 * ADAM_IN_CHUNK and ADAM_IN_CHUNK >= ADAM_IN_ROWS
    last0 = ADAM_IN_CHUNK * ADAM_IN_STEPS
    cut = OWN_ROW0[1] + FA_AT

    def body(q_ref, w_ref, gwin_ref, gfa_ref, m_ref, v_ref, go_ref, d_ref, nm_ref, nv_ref, buf, sem):
        i = pl.program_id(0)
        q = q_ref[0]
        chip1 = q == 1
        shift = jnp.where(chip1, N_FA, 0)

        def copy(src_ref, src0, dst0, n, slot):
            return pltpu.make_async_copy(src_ref.at[pl.ds(src0, n)], buf.at[pl.ds(dst0, n), 0], sem.at[slot])

        def first(on_chip1):
            if on_chip1:
                return [copy(gwin_ref, 0, 0, cut, 0), copy(gfa_ref, 0, cut, N_FA, ADAM_IN_CHUNKS),
                        copy(gwin_ref, cut, cut + N_FA, ADAM_IN_CHUNK - cut - N_FA, ADAM_IN_CHUNKS + 1)]
            return [copy(gwin_ref, 0, 0, ADAM_IN_CHUNK, 0)]

        def middle(k):
            return [copy(gwin_ref, pl.multiple_of(k * ADAM_IN_CHUNK - shift, 8), k * ADAM_IN_CHUNK, ADAM_IN_CHUNK, k)]

        def last(on_chip1):
            n = WIN_ROWS - last0 + (N_FA if on_chip1 else 0)
            return [copy(gwin_ref, WIN_ROWS - n, last0, n, ADAM_IN_STEPS)]

        def both(make, act):
            for on_chip1 in (False, True):
                @pl.when(chip1 if on_chip1 else jnp.logical_not(chip1))
                def _():
                    for c in make(on_chip1):
                        act(c)

        @pl.when(i == 0)
        def _():
            both(first, lambda c: c.start())
            for k in range(1, ADAM_IN_STEPS):
                middle(k)[0].start()
            both(last, lambda c: c.start())
            both(first, lambda c: c.wait())

        @pl.when(i < ADAM_IN_STEPS - 1)
        def _():
            middle(i + 1)[0].wait()

        @pl.when(i == ADAM_IN_STEPS - 1)
        def _():
            both(last, lambda c: c.wait())

        row0 = jnp.where(q == 0, OWN_ROW0[0], jnp.where(chip1, OWN_ROW0[1], jnp.where(q == 2, OWN_ROW0[2], OWN_ROW0[3])))
        g = buf[pl.ds(row0 + i * ADAM_IN_ROWS, ADAM_IN_ROWS)]
        go_ref[...] = g
        d_ref[...], nm_ref[...], nv_ref[...] = _adamw_math(w_ref[...], g, m_ref[...], v_ref[...])

    blk = pl.BlockSpec((ADAM_IN_ROWS, 1, D), lambda i, q: (i, 0, 0))
    hbm = pl.BlockSpec(memory_space=pl.ANY)
    return _call_indexed(
        order, body, (q_arr,), (w, gwin, gfa, m, v), (ADAM_IN_STEPS,), [blk, hbm, hbm, blk, blk], [blk] * 4,
        scratch_shapes=[pltpu.VMEM((ADAM_IN_BUF, 1, D), f32), pltpu.SemaphoreType.DMA((ADAM_IN_CHUNKS + 2,))],
        name="adamw_w_in", out_shape=[jax.ShapeDtypeStruct((SHARD_IN, 1, D), f32)] * 4,
        compiler_params=_params(("arbitrary",)),
    )


def kernel(x, norm_attn_g, w_in, b_forget, w_branch_a, w_branch_b, w_out, norm_mlp_g, w_up, w_down, norm_final_g, loss_target, m_norm_attn_g, m_w_in, m_b_forget, m_w_branch_a, m_w_branch_b, m_w_out, m_norm_mlp_g, m_w_up, m_w_down, m_norm_final_g, v_norm_attn_g, v_w_in, v_b_forget, v_w_branch_a, v_w_branch_b, v_w_out, v_norm_mlp_g, v_w_up, v_w_down, v_norm_final_g):
    xi, yi, ci = lax.axis_index("x"), lax.axis_index("y"), lax.axis_index("c")
    q_me = 2 * xi + yi
    c_arr = jnp.reshape(ci, (1,)).astype(jnp.int32)
    q_arr = jnp.reshape(q_me, (1,)).astype(jnp.int32)
    x_, tgt = x[0], loss_target[0]

    names = ["w_branch_a", "w_branch_b", "w_out", "w_up", "w_down"]
    big = dict(zip(names, [w_branch_a[0], w_branch_b[0], w_out[0], w_up[0], w_down[0]]))
    ms = dict(zip(names, [m_w_branch_a[0], m_w_branch_b[0], m_w_out[0], m_w_up[0], m_w_down[0]]))
    vs = dict(zip(names, [v_w_branch_a[0], v_w_branch_b[0], v_w_out[0], v_w_up[0], v_w_down[0]]))
    grad, upd = {}, {}
    order = _Order()

    def run(fn, *args, **kw):
        return fn(order, *args, **kw)

    def own_slot(a):
        return lax.dynamic_update_slice(lax.empty((NCHIP,) + a.shape, a.dtype), a[None], (q_me, 0, 0))

    sem_in, in_s = _allgather_start("allgather_start_in", run(_window_stacks, _feature_rows(w_in), q_arr), order)
    sem_rest, rest = _allgather_start("allgather_start_rest", [own_slot(w.astype(bf16)) for w in big.values()], order)
    rope = _rope_tables(order.tok[0, 0])
    sem_f, in_s = _allgather_forward("allgather_forward_in", in_s, sem_in, order, behind=rope)
    wins, fas = _allgather_finish("allgather_finish_in", in_s, sem_f, order)
    wt = run(_assemble_win, wins, fas)

    bpad = jnp.pad(b_forget, ((0, 0), (0, 120)))
    h1, qkvb, qkva, gates, fa = run(_norm_inproj, x_, norm_attn_g, wt, rope)
    F = run(_forget_cumsum, fa, bpad)
    oa, lsea = run(_fox_fwd, qkva, F)
    sem_f, rest = _allgather_forward("allgather_forward_rest", rest, sem_rest, order)
    ob, lseb = run(_dil_fwd, qkvb)
    was, wbs, wouts, wups, wdowns = _allgather_finish("allgather_finish_rest", rest, sem_f, order)
    wout = wouts.reshape(D, D)
    wdown = wdowns.reshape(DFF, D)
    ya, yb, mixed = run(_branch_mix, oa, ob, was, wbs, gates)
    x2, h2 = run(_outproj_norm, mixed, wout, x_, norm_mlp_g)
    u, a = run(_mlp_up, h2, wups)
    dx3, dx3b, dg3, loss_part = run(_mlp_down_loss, a, wdown, x2, norm_final_g.reshape(1, D), tgt)

    def comm(name, *parts):
        return _comm_multi(name, list(parts), order)

    def pair_adds(group, gs, ts):
        return zip(*[run(_pair_add, gs[i], ts[i], q_arr, c_arr, "pair_add_" + nm) for i, nm in enumerate(group)])

    def shard_sums(group, p32s, rs):
        return [run(_shard_sum, p32s[i], rs[i], c_arr, "shard_sum_" + nm) for i, nm in enumerate(group)]

    def adamw_group(group, fulls):
        for nm, gfull in zip(group, fulls):
            grad[nm] = gfull
            upd[nm] = run(_adamw, big[nm], gfull, ms[nm], vs[nm], "adamw_" + nm)

    grp_a, grp_b, grp_c = ["w_down", "w_up"], ["w_out", "w_branch_a", "w_branch_b"], ["w_in", "w_in_fa"]
    du = run(_mlp_down_bwd, dx3b, wdown, u)
    dwdown = run(_mm, a, dx3b, "tn", f32, 1024, D, "wgrad_down")
    dwup = run(_mm, h2, du, "tn", f32, D, 1024, "wgrad_up", stack_cols=True)
    ((sem_pa, buf_pa),) = comm("pair_start_a", _pair_start_part([dwdown.reshape(NCHIP, DFF // NCHIP, D), dwup]))
    dx2, dx2b, dg2 = run(_mlp_up_bwd, du, wups, x2, dx3, norm_mlp_g)
    ((gs, ts),) = comm("pair_wait_a", _pair_wait_part(buf_pa, sem_pa))
    p32_a, p16_a = pair_adds(grp_a, gs, ts)
    ((sem_sa, buf_sa),) = comm("shard_start_a", _shard_start_part(p16_a))
    dya, dyb, dproj = run(_gate_bwd, dx2b, wout, gates, ya, yb)
    dwout = run(_mm, mixed, dx2b, "tn", f32, D, D, "wgrad_out")
    doa, dob = run(_branch_bwd, dya, dyb, was, wbs)
    dwas, dwbs = run(_branch_wgrad, oa, ob, dya, dyb)
    ((sem_pb, buf_pb),) = comm("pair_start_b", _pair_start_part([dwout.reshape(NCHIP, D // NCHIP, D), dwas, dwbs]))
    dF, dproj = run(_fox_bwd, qkva, doa, oa, lsea, F, dproj)
    (gs, ts), (rs_a, _) = comm("pair_wait_b_shard_wait_a", _pair_wait_part(buf_pb, sem_pb),
                               _shard_wait_part(buf_sa, sem_sa, len(grp_a)))
    p32_b, p16_b = pair_adds(grp_b, gs, ts)
    fulls_a = shard_sums(grp_a, p32_a, rs_a)
    (sem_wa, fulls_a), (sem_sb, buf_sb) = comm("swap_start_a_shard_start_b", _swap_start_part(fulls_a),
                                               _shard_start_part(p16_b))
    dbf, dproj = run(_forget_bwd, dF, fa, bpad, dproj)
    dproj = run(_dil_bwd, qkvb, dob, ob, lseb, rope, dproj)
    (rs_b, _), fulls_a = comm("shard_wait_b_swap_wait_a", _shard_wait_part(buf_sb, sem_sb, len(grp_b)),
                              _swap_wait_part(fulls_a, sem_wa))
    fulls_b = shard_sums(grp_b, p32_b, rs_b)
    ((sem_wb, fulls_b),) = comm("swap_start_b", _swap_start_part(fulls_b))
    dwt = run(_mm, dproj, h1, "tn", f32, 512, D, "wgrad_in")
    dwfa = jnp.broadcast_to(dwt[F_FA:F_FA + FA_ROWS][None], (NCHIP, FA_ROWS, D))
    (sem_pc, buf_pc), fulls_b = comm("pair_start_c_swap_wait_b", _pair_start_part([dwt, dwfa], gathered=True),
                                     _swap_wait_part(fulls_b, sem_wb))
    adamw_group(grp_b, fulls_b)
    (((dwt_c, dwfa_c), (t_in, t_fa)),) = comm("pair_wait_c", _pair_wait_part(buf_pc, sem_pc, gathered=True))
    p32_in, p16_in = run(_pair_add_gathered, dwt_c, t_in, q_arr, c_arr, "pair_add_w_in")
    p32_fa, p16_fa = run(_pair_add, dwfa_c, t_fa, q_arr, c_arr, "pair_add_w_in_fa")
    ((sem_sc, buf_sc),) = comm("shard_start_c", _shard_start_part([p16_in, p16_fa]))
    gx, dg1 = run(_inproj_bwd, dproj, wt, x_, dx2, norm_attn_g)
    adamw_group(grp_a, fulls_a)
    small = jnp.concatenate([dg1, dg2, dg3, jnp.pad(dbf[:, 0:8], ((0, 0), (0, D - 8))),
                             jnp.pad(loss_part, ((0, 0), (0, D - 128))),
                             jnp.zeros((SMALL_ROWS - 5, D), f32)], axis=0)
    sm = lax.dynamic_update_slice(lax.empty((8, SMALL_ROWS, D), f32), small[None],
                                  (4 * xi + 2 * yi + ci, 0, 0))
    (sem_sm, buf_sm), (rs_c, _) = comm("small_start_shard_wait_c", _shard_start_part([], sm),
                                       _shard_wait_part(buf_sc, sem_sc, len(grp_c)))
    fulls_c = shard_sums(grp_c, [p32_in, p32_fa], rs_c)
    (sem_wc, fulls_c), (_, sm) = comm("swap_start_c_small_wait", _swap_start_part(fulls_c),
                                      _shard_wait_part(buf_sm, sem_sm, 0))
    gsmall = run(_small_sum, sm)
    loss = gsmall[4, 0]

    grad["norm_attn_g"], grad["norm_mlp_g"] = gsmall[0:1], gsmall[1:2]
    grad["norm_final_g"], grad["b_forget"] = gsmall[2:3], gsmall[3:4, 0:8]
    smalls = ["norm_attn_g", "norm_mlp_g", "norm_final_g", "b_forget"]
    res = run(_adamw_small, [norm_attn_g, norm_mlp_g, norm_final_g.reshape(1, D), b_forget],
              [grad[nm] for nm in smalls],
              [m_norm_attn_g, m_norm_mlp_g, m_norm_final_g.reshape(1, D), m_b_forget],
              [v_norm_attn_g, v_norm_mlp_g, v_norm_final_g.reshape(1, D), v_b_forget], "adamw_small")
    upd.update(zip(smalls, res))

    ((gwin, gfa),) = comm("swap_wait_c", _swap_wait_part(fulls_c, sem_wc))
    res_in = run(_adamw_w_in, _feature_rows(w_in), gwin, gfa, _feature_rows(m_w_in), _feature_rows(v_w_in), q_arr)
    grad["w_in"] = _unfeature_rows(res_in[0])
    upd["w_in"] = [_unfeature_rows(t) for t in res_in[1:]]

    order_out = ["norm_attn_g", "w_in", "b_forget", "w_branch_a", "w_branch_b", "w_out", "norm_mlp_g", "w_up",
                 "w_down", "norm_final_g"]
    shapes = dict(norm_attn_g=norm_attn_g.shape, w_in=w_in.shape, b_forget=b_forget.shape,
                  w_branch_a=w_branch_a.shape, w_branch_b=w_branch_b.shape, w_out=w_out.shape,
                  norm_mlp_g=norm_mlp_g.shape, w_up=w_up.shape, w_down=w_down.shape, norm_final_g=norm_final_g.shape)
    outs = [loss, gx.reshape(x.shape)]
    outs += [grad[nm].reshape(shapes[nm]) for nm in order_out]
    for k in range(3):
        outs += [upd[nm][k].reshape(shapes[nm]) for nm in order_out]
    return tuple(outs)
```

```python
import jax
import jax.numpy as jnp
from jax import lax
from jax.experimental import pallas as pl
from jax.experimental.pallas import tpu as pltpu

f32 = jnp.float32
bf16 = jnp.bfloat16

S = 2048
D = 1024
DFF = 4096
HD = 64
FOXW = 512
DILOUT = 256
DIL = (1, 4, 16)
BAND = 128
EPS = 1e-6
NEG = -1e30
ROPE_THETA = 500000.0
NCHIP = 4
TQ = 256

ADAM_LR, ADAM_B1, ADAM_B2, ADAM_EPS, ADAM_WD, ADAM_STEP = 0.001, 0.9, 0.999, 1e-08, 0.01, 10
VMEM_LIMIT = 56 * 1024 * 1024

UNIT = 64
NP = 6144
F_DIL, F_FOX, F_FA, F_G = 0, 2304, 3840, 4096
DIL_BLK, FOX_BLK = 1152, 384
WIN_UNITS, WIN_ROWS = 24, 1536
WIN_UNIT0 = (0, 23, 45, 68)
OWN_ROW0 = (0, 2, 60, 62)
SHARD_IN = 1474
N_FA = 8
FA_AT = 1536 - SHARD_IN
FA_ROWS = 32


def _compact_to_internal():
    c2i = {}
    for p in range(2):
        for role in range(3):
            for g in range(3):
                for hh in range(2):
                    c2i[24 + 12 * role + 4 * g + 2 * p + hh] = 18 * p + 6 * role + 2 * g + hh
    for p in range(4):
        for role in range(3):
            for hh in range(2):
                c2i[8 * role + 2 * p + hh] = F_FOX // UNIT + 6 * p + 2 * role + hh
    for j in range(32):
        c2i[60 + j] = F_G // UNIT + j
    return c2i


C2I = _compact_to_internal()
OVERLAP_UNITS = (23, 45, 46, 68)


def _params(sem=None):
    return pltpu.CompilerParams(dimension_semantics=sem, vmem_limit_bytes=VMEM_LIMIT)


class _Order:
    def __init__(self):
        self.tok = None

    def mark(self, v):
        self.tok = v

    def token_for(self, args):
        return [] if self.tok is None or any(self.tok is a for a in args) else [self.tok]


def _call(order, body, args, in_specs=None, **kw):
    args = list(args)
    n_in = len(args)
    if in_specs is None:
        in_specs = [pl.BlockSpec(memory_space=pltpu.VMEM)] * n_in
    kern = body
    extra = order.token_for(args)
    if extra:
        in_specs = list(in_specs) + [pl.BlockSpec(memory_space=pl.ANY)]

        def kern(*refs):
            body(*refs[:n_in], *refs[n_in + 1:])

    out = pl.pallas_call(kern, in_specs=in_specs, **kw)(*args, *extra)
    order.mark(out[0] if isinstance(out, (tuple, list)) else out)
    return out


def _call_indexed(order, body, scalars, args, grid, in_specs, out_specs, scratch_shapes=(), **kw):
    args, in_specs = list(args), list(in_specs)
    n_front = len(scalars) + len(args)
    kern = body
    extra = order.token_for(args)
    if extra:
        in_specs.append(pl.BlockSpec(memory_space=pl.ANY))

        def kern(*refs):
            body(*refs[:n_front], *refs[n_front + 1:])

    out = pl.pallas_call(
        kern, grid_spec=pltpu.PrefetchScalarGridSpec(num_scalar_prefetch=len(scalars), grid=grid, in_specs=in_specs,
                                                     out_specs=out_specs, scratch_shapes=scratch_shapes),
        **kw)(*scalars, *args, *extra)
    order.mark(out[0] if isinstance(out, (tuple, list)) else out)
    return out


def _dot(a, b):
    return jnp.dot(a, b, preferred_element_type=f32)


def _dot_nt(a, b):
    return lax.dot_general(a, b, (((1,), (1,)), ((), ())), preferred_element_type=f32)


def _dot_tn(a, b):
    return lax.dot_general(a, b, (((0,), (0,)), ((), ())), preferred_element_type=f32)


def _split3(x):
    hi = x.astype(bf16)
    r1 = x - hi.astype(f32)
    mid = r1.astype(bf16)
    lo = (r1 - mid.astype(f32)).astype(bf16)
    return hi, mid, lo


def _rope_tables(after):
    half = 8
    inv_freq = jnp.power(jnp.float32(ROPE_THETA), -jnp.arange(half, dtype=f32) * 2.0 / 16)
    ang = (jnp.arange(S).astype(f32) + after)[:, None] * inv_freq[None, :]
    cos, sin = jnp.cos(ang), jnp.sin(ang)
    one = jnp.ones((S, HD - 16), f32)
    zero = jnp.zeros((S, HD - 16), f32)
    z8 = jnp.zeros((S, 8), f32)
    c = jnp.concatenate([cos, cos, one], axis=1)
    s1 = jnp.concatenate([-sin, z8, zero], axis=1)
    s2 = jnp.concatenate([z8, sin, zero], axis=1)
    return tuple(jnp.concatenate([t, t], axis=1) for t in (c, s1, s2))


def _mm(order, a, b, mode, out_dtype, tm, tn, name, stack_cols=False):
    if mode == "nn":
        (M, K), (_, N) = a.shape, b.shape
        a_spec = pl.BlockSpec((tm, K), lambda i, j: (i, 0))
        b_spec = pl.BlockSpec((K, tn), lambda i, j: (0, j))
        dot = _dot
    elif mode == "nt":
        (M, K), (N, _) = a.shape, b.shape
        a_spec = pl.BlockSpec((tm, K), lambda i, j: (i, 0))
        b_spec = pl.BlockSpec((tn, K), lambda i, j: (j, 0))
        dot = _dot_nt
    else:
        (K, M), (_, N) = a.shape, b.shape
        a_spec = pl.BlockSpec((K, tm), lambda i, j: (0, i))
        b_spec = pl.BlockSpec((K, tn), lambda i, j: (0, j))
        dot = _dot_tn

    def body(a_ref, b_ref, o_ref):
        o_ref[...] = dot(a_ref[...], b_ref[...]).astype(out_dtype)

    if stack_cols:
        assert tm == M
        out_spec = pl.BlockSpec((None, tm, tn), lambda i, j: (j, 0, 0))
        out_shape = jax.ShapeDtypeStruct((N // tn, M, tn), out_dtype)
    else:
        out_spec = pl.BlockSpec((tm, tn), lambda i, j: (i, j))
        out_shape = jax.ShapeDtypeStruct((M, N), out_dtype)
    return _call(
        order, body, (a, b), name=name, grid=(M // tm, N // tn), in_specs=[a_spec, b_spec],
        out_specs=out_spec, out_shape=out_shape,
        compiler_params=_params(("parallel", "parallel")),
    )


def _assemble_win(order, wins, fas):
    def body(win_ref, fa_ref, o_ref):
        q = pl.program_id(0)

        @pl.when(q == 0)
        def _():
            o_ref[...] = jnp.zeros_like(o_ref)

        for k in range(NCHIP):
            @pl.when(q == k)
            def _(k=k):
                for j in range(WIN_UNITS):
                    cu = WIN_UNIT0[k] + j
                    dst = pl.ds(C2I[cu] * UNIT, UNIT)
                    if cu in OVERLAP_UNITS:
                        o_ref[dst, :] += win_ref[j * UNIT:(j + 1) * UNIT, :]
                    else:
                        o_ref[dst, :] = win_ref[j * UNIT:(j + 1) * UNIT, :]
                if k == 1:
                    o_ref[F_FA:F_FA + FA_ROWS, :] = fa_ref[...]

    return _call(
        order, body, (wins, fas), name="assemble_w_in", grid=(NCHIP,),
        in_specs=[pl.BlockSpec((None, WIN_ROWS, D), lambda q: (q, 0, 0)),
                  pl.BlockSpec((None, FA_ROWS, D), lambda q: (1, 0, 0))],
        out_specs=pl.BlockSpec((NP, D), lambda q: (0, 0)),
        out_shape=jax.ShapeDtypeStruct((NP, D), bf16),
        compiler_params=_params(("arbitrary",)),
    )


def _norm_inproj(order, x, g1, wt, rope):
    tm = 256
    c_t, s1_t, s2_t = rope

    def body(x_ref, g_ref, w_ref, c_ref, s1_ref, s2_ref, h_ref, qkvb_ref, qkva_ref, gates_ref, fa_ref):
        xb = x_ref[...]
        r = lax.rsqrt(jnp.mean(xb * xb, axis=-1, keepdims=True) + EPS)
        h = ((xb * r) * g_ref[...]).astype(bf16)
        h_ref[...] = h
        c, s1, s2 = c_ref[...], s1_ref[...], s2_ref[...]
        for p in range(2):
            pb = _dot_nt(h, w_ref[F_DIL + p * DIL_BLK:F_DIL + (p + 1) * DIL_BLK, :])
            for ch in range(DIL_BLK // 128):
                pc = pb[:, ch * 128:(ch + 1) * 128]
                if ch < 6:
                    pc = pc * c + pltpu.roll(pc, 120, 1) * s1 + pltpu.roll(pc, 8, 1) * s2
                qkvb_ref[:, p * DIL_BLK + ch * 128:p * DIL_BLK + (ch + 1) * 128] = pc
        qkva_ref[...] = _dot_nt(h, w_ref[F_FOX:F_FA, :]).astype(bf16)
        fa_ref[...] = _dot_nt(h, w_ref[F_FA:F_FA + 128, :])
        gates_ref[...] = _dot_nt(h, w_ref[F_G:NP, :]).astype(bf16)

    row = lambda w: pl.BlockSpec((tm, w), lambda i: (i, 0))
    return _call(
        order, body, (x, g1, wt, c_t, s1_t, s2_t), name="norm_inproj", grid=(S // tm,),
        in_specs=[row(D), pl.BlockSpec((1, D), lambda i: (0, 0)), pl.BlockSpec((NP, D), lambda i: (0, 0)),
                  row(128), row(128), row(128)],
        out_specs=[row(D), row(2 * DIL_BLK), row(4 * FOX_BLK), row(2 * D), row(128)],
        out_shape=[jax.ShapeDtypeStruct((S, D), bf16), jax.ShapeDtypeStruct((S, 2 * DIL_BLK), f32),
                   jax.ShapeDtypeStruct((S, 4 * FOX_BLK), bf16), jax.ShapeDtypeStruct((S, 2 * D), bf16),
                   jax.ShapeDtypeStruct((S, 128), f32)],
        compiler_params=_params(("parallel",)),
    )


def _forget_cumsum(order, fa, bpad):
    nb = S // TQ

    def body(fa_ref, b_ref, F_ref):
        rr = lax.broadcasted_iota(jnp.int32, (TQ, TQ), 0)
        cc = lax.broadcasted_iota(jnp.int32, (TQ, TQ), 1)
        tri = (rr >= cc).astype(bf16)
        lane = lax.broadcasted_iota(jnp.int32, (1, 128), 1)
        carry = jnp.zeros((1, 128), f32)
        for b in range(nb):
            z = fa_ref[b * TQ:(b + 1) * TQ, :] + b_ref[...]
            lf = jnp.minimum(z, 0.0) - jnp.log(1.0 + jnp.exp(-jnp.abs(z)))
            lf = jnp.where(lane < 8, lf, 0.0)
            hi, mid, lo = _split3(lf)
            fb = (_dot(tri, hi) + _dot(tri, mid)) + _dot(tri, lo) + carry
            F_ref[b * TQ:(b + 1) * TQ, :] = fb
            carry = fb[TQ - 1:TQ, :]

    return _call(
        order, body, (fa, bpad), name="forget_cumsum",
        out_shape=jax.ShapeDtypeStruct((S, 128), f32),
        compiler_params=_params(),
    )


def _head_masks():
    lane = lax.broadcasted_iota(jnp.int32, (1, 128), 1)
    return lane, (lane < HD, lane >= HD)


L_ONE = 3
FOX_TQ, FOX_TK = 256, 512


def _set_lanes(x, lane, first, cols):
    for n, col in enumerate(cols):
        x = jnp.where(lane == first + n, col, x)
    return x


def _f32_parts(col):
    return [t.astype(f32) for t in _split3(col)]


def _fox_operands(qkv_ref, F_ref, lse_ref, qa, ka, p, rows):
    lane, hm = _head_masks()
    q = qkv_ref[rows, 0:128].astype(f32) * 0.125
    k = qkv_ref[rows, 128:256].astype(f32)
    Fb = F_ref[rows, :]
    for hh in (0, 1):
        free = (1 - hh) * HD
        fcol = jnp.sum(jnp.where(lane == 2 * p + hh, Fb, 0.0), axis=1, keepdims=True)
        qterm = fcol if lse_ref is None else fcol - lse_ref[rows, hh * HD:hh * HD + 1]
        qcols = _f32_parts(qterm) + [1.0] * 3
        kcols = [1.0] * 3 + [-t for t in _f32_parts(fcol)]
        qa[hh, rows, :] = _set_lanes(jnp.where(hm[hh], q, 0.0), lane, free, qcols).astype(bf16)
        ka[hh, rows, :] = _set_lanes(k, lane, free, kcols).astype(bf16)


def _fox_fwd(order, qkva, F):
    tq, tk = FOX_TQ, FOX_TK

    def body(qkv_ref, F_ref, o_ref, lse_ref, qa, ka, vt):
        p = pl.program_id(0)
        keyi = lax.broadcasted_iota(jnp.int32, (tk, 1), 0)
        qryi = lax.broadcasted_iota(jnp.int32, (1, tq), 1)
        sub = lax.broadcasted_iota(jnp.int32, (128, 1), 0)

        def prep(i, c):
            rows = pl.ds(pl.multiple_of(i * tk, tk), tk)
            _fox_operands(qkv_ref, F_ref, None, qa, ka, p, rows)
            vt[i] = qkv_ref[rows, 256:384].astype(f32).T.astype(bf16)
            return c

        lax.fori_loop(0, S // tk, prep, 0)

        def qblock(i, first_half):
            r0 = pl.multiple_of(i * tq, tq)
            qh = [qa[hh, pl.ds(r0, tq), :] for hh in (0, 1)]

            def kv(jb, carry, masked, width):
                keys = pl.ds(pl.multiple_of(jb * tk, tk), width)
                sts = [_dot_nt(ka[hh, keys, :], qh[hh]) for hh in (0, 1)]
                new = []
                for hh in (0, 1):
                    m, l, a = carry[3 * hh:3 * hh + 3]
                    st = sts[hh]
                    if masked:
                        st = jnp.where(jb * tk + keyi[0:width] <= r0 + qryi, st, NEG)
                    mn = jnp.maximum(m, jnp.max(st, axis=0, keepdims=True))
                    al = jnp.exp(m - mn)
                    pt = jnp.exp(st - mn)
                    l = al * l + jnp.sum(pt, axis=0, keepdims=True)
                    a = al * a + _dot(vt[jb, hh * HD:(hh + 1) * HD, 0:width], pt.astype(bf16))
                    new += [mn, l, a]
                return tuple(new)

            init = (jnp.full((1, tq), NEG, f32), jnp.zeros((1, tq), f32), jnp.zeros((HD, tq), f32)) * 2
            last = (r0 + tq - 1) // tk
            carry = lax.fori_loop(0, last, lambda j, cr: kv(j, cr, False, tk), init)
            m0, l0, a0, m1, l1, a1 = kv(last, carry, True, tk // 2 if first_half else tk)
            ot = jnp.concatenate([a0 / l0, a1 / l1], axis=0)
            lt = jnp.where(sub < HD, m0 + jnp.log(l0), m1 + jnp.log(l1))
            o_ref[pl.ds(r0, tq), :] = ot.T.astype(bf16)
            lse_ref[pl.ds(r0, tq), :] = lt.T

        def qpair(t, c):
            qblock(2 * t, True)
            qblock(2 * t + 1, False)
            return c

        assert tk == 2 * tq
        lax.fori_loop(0, S // tk, qpair, 0)

    pair = pl.BlockSpec((S, 128), lambda p: (0, p))
    return _call(
        order, body, (qkva, F), name="fox_fwd", grid=(4,),
        in_specs=[pl.BlockSpec((S, FOX_BLK), lambda p: (0, p)), pl.BlockSpec((S, 128), lambda p: (0, 0))],
        out_specs=[pair, pair],
        out_shape=[jax.ShapeDtypeStruct((S, FOXW), bf16), jax.ShapeDtypeStruct((S, FOXW), f32)],
        scratch_shapes=[pltpu.VMEM((2, S, 128), bf16)] * 2 + [pltpu.VMEM((S // tk, 128, tk), bf16)],
        compiler_params=_params(("parallel",)),
    )


def _permute_in(dst, src, r):
    L = S // r
    for rho in range(r):
        dst[rho * L:(rho + 1) * L, :] = src[pl.ds(rho, L, stride=r), :]


def _permute_out(dst, src, r):
    L = S // r
    for rho in range(r):
        dst[pl.ds(rho, L, stride=r), :] = src[rho * L:(rho + 1) * L, :]


def _band_width(nbl):
    return BAND if nbl == 1 else 2 * BAND


def _band_geometry(bb, nbl):
    r0 = pl.multiple_of(bb * BAND, BAND)
    if nbl == 1:
        k0 = r0
    else:
        k0 = pl.multiple_of(jnp.maximum(bb - 1, 0) * BAND, BAND)
    sub0 = (bb - lax.rem(bb, nbl)) * BAND
    qi = r0 + lax.broadcasted_iota(jnp.int32, (BAND, 1), 0)
    ki = k0 + lax.broadcasted_iota(jnp.int32, (1, _band_width(nbl)), 1)
    diff = qi - ki
    valid = (diff >= 0) & (diff <= BAND) & (ki >= sub0)
    return r0, k0, valid


def _dil_views(ref):
    return [[ref.at[:, pl.ds((3 * role + g) * 128, 128)] for g in range(3)] for role in range(3)]


DIL_UNROLL = 4


def _dil_in_specs():
    return [pl.BlockSpec((S, 128), lambda p, k=k: (0, 9 * p + k)) for k in range(9)]


def _dil_fwd(order, qkvb):
    def body(*refs):
        q_refs, k_refs, v_refs = refs[0:3], refs[3:6], refs[6:9]
        ob_ref, lse_ref, qp, kp, vp, op, lp = refs[9:16]
        on, ln = refs[16:19], refs[19:22]
        _, hm = _head_masks()
        for g, r in enumerate(DIL):
            nbl = S // r // BAND
            if r == 1:
                qs_, ks_, vs_, od, ld = q_refs[g], k_refs[g], v_refs[g], on[g], ln[g]
            else:
                _permute_in(qp, q_refs[g], r)
                _permute_in(kp, k_refs[g], r)
                _permute_in(vp, v_refs[g], r)
                qs_, ks_, vs_, od, ld = qp, kp, vp, op, lp

            def blk(t, c, qs_=qs_, ks_=ks_, vs_=vs_, od=od, ld=ld, nbl=nbl):
                work = []
                for u in range(DIL_UNROLL):
                    r0, k0, valid = _band_geometry(DIL_UNROLL * t + u, nbl)
                    q = qs_[pl.ds(r0, BAND), :] * 0.125
                    kw = ks_[pl.ds(k0, _band_width(nbl)), :].astype(bf16)
                    vw = vs_[pl.ds(k0, _band_width(nbl)), :]
                    for hh in (0, 1):
                        qh = jnp.where(hm[hh], q, 0.0).astype(bf16)
                        work.append((u, hh, r0, valid, vw, _dot_nt(qh, kw)))
                o = [jnp.zeros((BAND, 128), f32)] * DIL_UNROLL
                lse = [jnp.zeros((BAND, 128), f32)] * DIL_UNROLL
                for u, hh, r0, valid, vw, s in work:
                    s = jnp.where(valid, s, NEG)
                    m = jnp.max(s, axis=1, keepdims=True)
                    pr = jnp.exp(s - m)
                    l = jnp.sum(pr, axis=1, keepdims=True)
                    vm = jnp.where(hm[hh], vw, 0.0).astype(bf16)
                    o[u] = o[u] + _dot((pr / l).astype(bf16), vm)
                    lse[u] = jnp.where(hm[hh], m + jnp.log(l), lse[u])
                    if hh == 1:
                        od[pl.ds(r0, BAND), :] = o[u]
                        ld[pl.ds(r0, BAND), :] = lse[u]
                return c

            lax.fori_loop(0, S // BAND // DIL_UNROLL, blk, 0)
            if r != 1:
                _permute_out(on[g], op, r)
                _permute_out(ln[g], lp, r)

        def combine(i, c):
            r0 = pl.multiple_of(i * TQ, TQ)
            ls = [ln[g][pl.ds(r0, TQ), :] for g in range(3)]
            mx = jnp.maximum(jnp.maximum(ls[0], ls[1]), ls[2])
            es = [jnp.exp(l - mx) for l in ls]
            tot = (es[0] + es[1]) + es[2]
            acc = (es[0] / tot) * on[0][pl.ds(r0, TQ), :]
            acc = acc + (es[1] / tot) * on[1][pl.ds(r0, TQ), :]
            acc = acc + (es[2] / tot) * on[2][pl.ds(r0, TQ), :]
            ob_ref[pl.ds(r0, TQ), :] = acc.astype(bf16)
            lse_ref[pl.ds(r0, TQ), :] = mx + jnp.log(tot)
            return c

        lax.fori_loop(0, S // TQ, combine, 0)

    out_blk = pl.BlockSpec((S, 128), lambda p: (0, p))
    return _call(
        order, body, [qkvb] * 9, name="dil_fwd", grid=(2,),
        in_specs=_dil_in_specs(), out_specs=[out_blk, out_blk],
        out_shape=[jax.ShapeDtypeStruct((S, DILOUT), bf16), jax.ShapeDtypeStruct((S, DILOUT), f32)],
        scratch_shapes=[pltpu.VMEM((S, 128), f32)] * 11,
        compiler_params=_params(("parallel",)),
    )


def _branch_mix(order, oa, ob, was, wbs, gates):
    tm = 512

    def body(oa_ref, ob_ref, wa_ref, wb_ref, g_ref, ya_ref, yb_ref, mix_ref):
        oa_b, ob_b = oa_ref[...], ob_ref[...]
        for q in range(NCHIP):
            cols = slice(q * 256, (q + 1) * 256)
            ya = _dot(oa_b, wa_ref[q])
            yb = _dot(ob_b, wb_ref[q])
            ya_ref[:, cols] = ya.astype(bf16)
            yb_ref[:, cols] = yb.astype(bf16)
            ga = g_ref[:, q * 256:(q + 1) * 256].astype(f32)
            gb = g_ref[:, D + q * 256:D + (q + 1) * 256].astype(f32)
            mix_ref[:, cols] = (jax.nn.sigmoid(ga) * ya + jax.nn.sigmoid(gb) * yb).astype(bf16)

    row = lambda w: pl.BlockSpec((tm, w), lambda i: (i, 0))
    full3 = lambda a: pl.BlockSpec(a.shape, lambda i: (0, 0, 0))
    return _call(
        order, body, (oa, ob, was, wbs, gates), name="branch_mix", grid=(S // tm,),
        in_specs=[row(FOXW), row(DILOUT), full3(was), full3(wbs), row(2 * D)],
        out_specs=[row(D), row(D), row(D)],
        out_shape=[jax.ShapeDtypeStruct((S, D), bf16), jax.ShapeDtypeStruct((S, D), bf16),
                   jax.ShapeDtypeStruct((S, D), bf16)],
        compiler_params=_params(("parallel",)),
    )


def _outproj_norm(order, mixed, wout, x, g2):
    tm = 512

    def body(m_ref, w_ref, x_ref, g_ref, x2_ref, h2_ref):
        x2 = x_ref[...] + _dot(m_ref[...], w_ref[...])
        x2_ref[...] = x2
        r = lax.rsqrt(jnp.mean(x2 * x2, axis=-1, keepdims=True) + EPS)
        h2_ref[...] = ((x2 * r) * g_ref[...]).astype(bf16)

    row = pl.BlockSpec((tm, D), lambda i: (i, 0))
    return _call(
        order, body, (mixed, wout, x, g2), name="outproj_norm", grid=(S // tm,),
        in_specs=[row, pl.BlockSpec((D, D), lambda i: (0, 0)), row, pl.BlockSpec((1, D), lambda i: (0, 0))],
        out_specs=[row, row],
        out_shape=[jax.ShapeDtypeStruct((S, D), f32), jax.ShapeDtypeStruct((S, D), bf16)],
        compiler_params=_params(("parallel",)),
    )


def _mlp_up(order, h2, wups):
    tm = 1024

    def body(h_ref, w_ref, ru_ref, a_ref):
        ru = jnp.maximum(_dot(h_ref[...], w_ref[...]), 0.0)
        ru_ref[...] = ru.astype(bf16)
        a_ref[...] = (ru * ru).astype(bf16)

    out = pl.BlockSpec((tm, D), lambda q, i: (i, q))
    return _call(
        order, body, (h2, wups), name="mlp_up", grid=(NCHIP, S // tm),
        in_specs=[pl.BlockSpec((tm, D), lambda q, i: (i, 0)), pl.BlockSpec((None, D, D), lambda q, i: (q, 0, 0))],
        out_specs=[out, out],
        out_shape=[jax.ShapeDtypeStruct((S, DFF), bf16), jax.ShapeDtypeStruct((S, DFF), bf16)],
        compiler_params=_params(("parallel", "parallel")),
    )


def _mlp_down_loss(order, a, wdown, x2, g3, tgt):
    tm = 512

    def body(a_ref, w_ref, x2_ref, g_ref, t_ref, dx_ref, dxb_ref, dg_ref, loss_ref):
        i = pl.program_id(0)
        x3 = x2_ref[...] + _dot(a_ref[...], w_ref[...])
        r = lax.rsqrt(jnp.mean(x3 * x3, axis=-1, keepdims=True) + EPS)
        xh = x3 * r
        g = g_ref[...]
        e = xh * g - t_ref[...]
        part = 0.5 * jnp.sum(jnp.mean(e * e, axis=-1, keepdims=True), axis=0, keepdims=True)
        dy = e * (1.0 / D)
        gdy = dy * g
        dx = r * (gdy - xh * jnp.mean(gdy * xh, axis=-1, keepdims=True))
        dx_ref[...] = dx
        dxb_ref[...] = dx.astype(bf16)

        @pl.when(i == 0)
        def _():
            dg_ref[...] = jnp.zeros_like(dg_ref)
            loss_ref[...] = jnp.zeros_like(loss_ref)

        dg_ref[...] += jnp.sum(dy * xh, axis=0, keepdims=True)
        loss_ref[...] += jnp.broadcast_to(part, (1, 128))

    row = pl.BlockSpec((tm, D), lambda i: (i, 0))
    vec = pl.BlockSpec((1, D), lambda i: (0, 0))
    return _call(
        order, body, (a, wdown, x2, g3, tgt), name="mlp_down_loss", grid=(S // tm,),
        in_specs=[pl.BlockSpec((tm, DFF), lambda i: (i, 0)), pl.BlockSpec((DFF, D), lambda i: (0, 0)), row, vec, row],
        out_specs=[row, row, vec, pl.BlockSpec((1, 128), lambda i: (0, 0))],
        out_shape=[jax.ShapeDtypeStruct((S, D), f32), jax.ShapeDtypeStruct((S, D), bf16),
                   jax.ShapeDtypeStruct((1, D), f32), jax.ShapeDtypeStruct((1, 128), f32)],
        compiler_params=_params(("arbitrary",)),
    )


def _mlp_down_bwd(order, dx3b, wdown, u):
    tm = 512

    def body(d_ref, w_ref, u_ref, du_ref):
        d = d_ref[...]
        for q in range(NCHIP):
            cols = slice(q * D, (q + 1) * D)
            da = _dot_nt(d, w_ref[cols, :])
            du_ref[:, cols] = (da * (2.0 * u_ref[:, cols].astype(f32))).astype(bf16)

    return _call(
        order, body, (dx3b, wdown, u), name="mlp_down_bwd", grid=(S // tm,),
        in_specs=[pl.BlockSpec((tm, D), lambda i: (i, 0)), pl.BlockSpec((DFF, D), lambda i: (0, 0)),
                  pl.BlockSpec((tm, DFF), lambda i: (i, 0))],
        out_specs=pl.BlockSpec((tm, DFF), lambda i: (i, 0)),
        out_shape=jax.ShapeDtypeStruct((S, DFF), bf16),
        compiler_params=_params(("parallel",)),
    )


def _mlp_up_bwd(order, du, wups, x2, dx3, g2):
    tm = 512

    def body(du_ref, w_ref, x2_ref, dx3_ref, g_ref, dx2_ref, dx2b_ref, dg_ref):
        i = pl.program_id(0)
        dh = jnp.zeros((tm, D), f32)
        for q in range(NCHIP):
            dh = dh + _dot_nt(du_ref[:, q * D:(q + 1) * D], w_ref[q])
        x2 = x2_ref[...]
        r = lax.rsqrt(jnp.mean(x2 * x2, axis=-1, keepdims=True) + EPS)
        xh = x2 * r
        gdh = dh * g_ref[...]
        dx2 = dx3_ref[...] + r * (gdh - xh * jnp.mean(gdh * xh, axis=-1, keepdims=True))
        dx2_ref[...] = dx2
        dx2b_ref[...] = dx2.astype(bf16)

        @pl.when(i == 0)
        def _():
            dg_ref[...] = jnp.zeros_like(dg_ref)

        dg_ref[...] += jnp.sum(dh * xh, axis=0, keepdims=True)

    row = pl.BlockSpec((tm, D), lambda i: (i, 0))
    vec = pl.BlockSpec((1, D), lambda i: (0, 0))
    return _call(
        order, body, (du, wups, x2, dx3, g2), name="mlp_up_bwd", grid=(S // tm,),
        in_specs=[pl.BlockSpec((tm, DFF), lambda i: (i, 0)), pl.BlockSpec((NCHIP, D, D), lambda i: (0, 0, 0)),
                  row, row, vec],
        out_specs=[row, row, vec],
        out_shape=[jax.ShapeDtypeStruct((S, D), f32), jax.ShapeDtypeStruct((S, D), bf16),
                   jax.ShapeDtypeStruct((1, D), f32)],
        compiler_params=_params(("arbitrary",)),
    )


def _gate_bwd(order, dx2b, wout, gates, ya, yb):
    tm = 512

    def body(d_ref, w_ref, g_ref, ya_ref, yb_ref, dya_ref, dyb_ref, dproj_ref):
        dm = _dot_nt(d_ref[...], w_ref[...])
        sa = jax.nn.sigmoid(g_ref[:, 0:D].astype(f32))
        sb = jax.nn.sigmoid(g_ref[:, D:2 * D].astype(f32))
        dya_ref[...] = (dm * sa).astype(bf16)
        dyb_ref[...] = (dm * sb).astype(bf16)
        dproj_ref[:, 0:D] = (dm * ya_ref[...].astype(f32) * (sa * (1.0 - sa))).astype(bf16)
        dproj_ref[:, D:2 * D] = (dm * yb_ref[...].astype(f32) * (sb * (1.0 - sb))).astype(bf16)

    row = lambda w: pl.BlockSpec((tm, w), lambda i: (i, 0))
    return _call(
        order, body, (dx2b, wout, gates, ya, yb), name="gate_bwd", grid=(S // tm,),
        in_specs=[row(D), pl.BlockSpec((D, D), lambda i: (0, 0)), row(2 * D), row(D), row(D)],
        out_specs=[row(D), row(D), pl.BlockSpec((tm, 2 * D), lambda i: (i, F_G // (2 * D)))],
        out_shape=[jax.ShapeDtypeStruct((S, D), bf16), jax.ShapeDtypeStruct((S, D), bf16),
                   jax.ShapeDtypeStruct((S, NP), bf16)],
        compiler_params=_params(("parallel",)),
    )


def _branch_bwd(order, dya, dyb, was, wbs):
    tm = 512

    def body(dya_ref, dyb_ref, wa_ref, wb_ref, doa_ref, dob_ref):
        doa = jnp.zeros((tm, FOXW), f32)
        dob = jnp.zeros((tm, DILOUT), f32)
        for q in range(NCHIP):
            cols = slice(q * 256, (q + 1) * 256)
            doa = doa + _dot_nt(dya_ref[:, cols], wa_ref[q])
            dob = dob + _dot_nt(dyb_ref[:, cols], wb_ref[q])
        doa_ref[...] = doa.astype(bf16)
        dob_ref[...] = dob

    row = lambda w: pl.BlockSpec((tm, w), lambda i: (i, 0))
    full3 = lambda a: pl.BlockSpec(a.shape, lambda i: (0, 0, 0))
    return _call(
        order, body, (dya, dyb, was, wbs), name="branch_bwd", grid=(S // tm,),
        in_specs=[row(D), row(D), full3(was), full3(wbs)],
        out_specs=[row(FOXW), row(DILOUT)],
        out_shape=[jax.ShapeDtypeStruct((S, FOXW), bf16), jax.ShapeDtypeStruct((S, DILOUT), f32)],
        compiler_params=_params(("parallel",)),
    )


def _branch_wgrad(order, oa, ob, dya, dyb):
    def body(oa_ref, ob_ref, dya_ref, dyb_ref, dwa_ref, dwb_ref):
        dwa_ref[...] = _dot_tn(oa_ref[...], dya_ref[...])
        dwb_ref[...] = _dot_tn(ob_ref[...], dyb_ref[...])

    full = lambda w: pl.BlockSpec((S, w), lambda q: (0, 0))
    colq = pl.BlockSpec((S, 256), lambda q: (0, q))
    return _call(
        order, body, (oa, ob, dya, dyb), name="branch_wgrad", grid=(NCHIP,),
        in_specs=[full(FOXW), full(DILOUT), colq, colq],
        out_specs=[pl.BlockSpec((None, FOXW, 256), lambda q: (q, 0, 0)),
                   pl.BlockSpec((None, DILOUT, 256), lambda q: (q, 0, 0))],
        out_shape=[jax.ShapeDtypeStruct((NCHIP, FOXW, 256), f32), jax.ShapeDtypeStruct((NCHIP, DILOUT, 256), f32)],
        compiler_params=_params(("parallel",)),
    )


def _fox_bwd(order, qkva, doa, oa, lse, F, dproj):
    tq, tk = FOX_TQ, FOX_TK

    def body(qkv_ref, do_ref, o_ref, lse_ref, F_ref, _dproj_in, dF_ref, dqkv_ref, qa, ka, da, va, kat,
             dk_scr, dv_scr, dqt_scr):
        p = pl.program_id(0)
        lane, hm = _head_masks()
        keyi = lax.broadcasted_iota(jnp.int32, (tk, 1), 0)
        qryi = lax.broadcasted_iota(jnp.int32, (1, tq), 1)

        def prep(i, c):
            rows = pl.ds(pl.multiple_of(i * tk, tk), tk)
            _fox_operands(qkv_ref, F_ref, lse_ref, qa, ka, p, rows)
            do = do_ref[rows, :].astype(f32)
            prod = do * o_ref[rows, :].astype(f32)
            v = qkv_ref[rows, 256:384].astype(f32)
            for hh in (0, 1):
                free = (1 - hh) * HD
                delta = jnp.sum(jnp.where(hm[hh], prod, 0.0), axis=1, keepdims=True)
                da[hh, rows, :] = _set_lanes(jnp.where(hm[hh], do, 0.0), lane, free,
                                             [-t for t in _f32_parts(delta)]).astype(bf16)
                va[hh, rows, :] = _set_lanes(v, lane, free, [1.0] * 3).astype(bf16)
                kat[hh, i] = ka[hh, rows, :].astype(f32).T.astype(bf16)
                dk_scr[hh, rows, :] = jnp.zeros((tk, 128), f32)
                dv_scr[hh, rows, :] = jnp.zeros((tk, 128), f32)
            return c

        lax.fori_loop(0, S // tk, prep, 0)

        def qblock(i, first_half):
            r0 = pl.multiple_of(i * tq, tq)
            qrows = pl.ds(r0, tq)
            qh = [qa[hh, qrows, :] for hh in (0, 1)]
            dh = [da[hh, qrows, :] for hh in (0, 1)]
            dqt_scr[...] = jnp.zeros_like(dqt_scr)

            def kv(jb, c2, masked, width):
                keys = pl.ds(pl.multiple_of(jb * tk, tk), width)
                sts = [_dot_nt(ka[hh, keys, :], qh[hh]) for hh in (0, 1)]
                dps = [_dot_nt(va[hh, keys, :], dh[hh]) for hh in (0, 1)]
                for hh in (0, 1):
                    pt = jnp.exp(sts[hh])
                    if masked:
                        pt = jnp.where(jb * tk + keyi[0:width] <= r0 + qryi, pt, 0.0)
                    dsb = (pt * dps[hh]).astype(bf16)
                    dv_scr[hh, keys, :] += _dot(pt.astype(bf16), dh[hh])
                    dk_scr[hh, keys, :] += _dot(dsb, qh[hh])
                    dqt_scr[hh] += _dot(kat[hh, jb, :, 0:width], dsb)
                return c2

            last = (r0 + tq - 1) // tk
            lax.fori_loop(0, last, lambda j, c2: kv(j, c2, False, tk), 0)
            kv(last, 0, True, tk // 2 if first_half else tk)
            dq0, dq1 = dqt_scr[0].T, dqt_scr[1].T
            dqkv_ref[qrows, 0:128] = (jnp.where(hm[0], dq0, dq1) * 0.125).astype(bf16)
            dF_ref[qrows, :] = jnp.where(lane == 0, dq0[:, HD:HD + 1], jnp.where(lane == 1, dq1[:, 0:1], 0.0))

        def qpair(t, c):
            qblock(2 * t, True)
            qblock(2 * t + 1, False)
            return c

        assert tk == 2 * tq
        lax.fori_loop(0, S // tk, qpair, 0)

        def finish(i, c):
            rows = pl.ds(pl.multiple_of(i * tq, tq), tq)
            dk0, dk1 = dk_scr[0, rows, :], dk_scr[1, rows, :]
            dqkv_ref[rows, 128:256] = jnp.where(hm[0], dk0, dk1).astype(bf16)
            dqkv_ref[rows, 256:384] = jnp.where(hm[0], dv_scr[0, rows, :], dv_scr[1, rows, :]).astype(bf16)
            cs = jnp.where(lane == 0, dk0[:, HD + L_ONE:HD + L_ONE + 1],
                           jnp.where(lane == 1, dk1[:, L_ONE:L_ONE + 1], 0.0))
            dF_ref[rows, :] = dF_ref[rows, :] - cs
            return c

        lax.fori_loop(0, S // tq, finish, 0)

    pair = pl.BlockSpec((S, 128), lambda p: (0, p))
    return _call(
        order, body, (qkva, doa, oa, lse, F, dproj), name="fox_bwd", grid=(4,),
        in_specs=[pl.BlockSpec((S, FOX_BLK), lambda p: (0, p)), pair, pair, pair,
                  pl.BlockSpec((S, 128), lambda p: (0, 0)), pl.BlockSpec(memory_space=pl.ANY)],
        out_specs=[pair, pl.BlockSpec((S, FOX_BLK), lambda p: (0, F_FOX // FOX_BLK + p))],
        out_shape=[jax.ShapeDtypeStruct((S, FOXW), f32), jax.ShapeDtypeStruct((S, NP), bf16)],
        input_output_aliases={5: 1},
        scratch_shapes=[pltpu.VMEM((2, S, 128), bf16)] * 4 + [pltpu.VMEM((2, S // tk, 128, tk), bf16)]
        + [pltpu.VMEM((2, S, 128), f32)] * 2 + [pltpu.VMEM((2, 128, tq), f32)],
        compiler_params=_params(("parallel",)),
    )


def _forget_bwd(order, dF, fa, bpad, dproj):
    nb = S // TQ

    def body(dF_ref, fa_ref, b_ref, _dproj_in, db_ref, dfa_ref):
        rr = lax.broadcasted_iota(jnp.int32, (TQ, TQ), 0)
        cc = lax.broadcasted_iota(jnp.int32, (TQ, TQ), 1)
        upper = (cc >= rr).astype(bf16)
        lane = lax.broadcasted_iota(jnp.int32, (1, 128), 1)
        carry = jnp.zeros((1, 128), f32)
        db = jnp.zeros((1, 128), f32)
        for b in reversed(range(nb)):
            cols = jnp.zeros((TQ, 128), f32)
            for h in range(8):
                c0 = (h // 2) * 128 + h % 2
                cols = jnp.where(lane == h, dF_ref[b * TQ:(b + 1) * TQ, c0:c0 + 1], cols)
            dlf = carry
            for part in _split3(cols):
                dlf = dlf + _dot(upper, part)
            carry = carry + jnp.sum(cols, axis=0, keepdims=True)
            z = fa_ref[b * TQ:(b + 1) * TQ, :] + b_ref[...]
            dz = jnp.where(lane < 8, dlf * jax.nn.sigmoid(-z), 0.0)
            dfa_ref[b * TQ:(b + 1) * TQ, 0:128] = dz.astype(bf16)
            dfa_ref[b * TQ:(b + 1) * TQ, 128:256] = jnp.zeros((TQ, 128), bf16)
            db = db + jnp.sum(dz, axis=0, keepdims=True)
        db_ref[...] = db

    whole = lambda a: pl.BlockSpec(a.shape, lambda i: (0,) * a.ndim)
    return _call(
        order, body, (dF, fa, bpad, dproj), name="forget_bwd", grid=(1,),
        in_specs=[whole(dF), whole(fa), whole(bpad), pl.BlockSpec(memory_space=pl.ANY)],
        out_specs=[pl.BlockSpec((1, 128), lambda i: (0, 0)), pl.BlockSpec((S, 256), lambda i: (0, F_FA // 256))],
        out_shape=[jax.ShapeDtypeStruct((1, 128), f32), jax.ShapeDtypeStruct((S, NP), bf16)],
        input_output_aliases={3: 1},
        compiler_params=_params(("arbitrary",)),
    )


def _dil_bwd(order, qkvb, dob, ob, lseb, rope, dproj):
    c_t, s1_t, s2_t = rope

    def body(*refs):
        q_refs, k_refs, v_refs = refs[0:3], refs[3:6], refs[6:9]
        dob_ref, ob_ref, lse_ref, c_ref, s1_ref, s2_ref, _dproj_in, dqkv_ref = refs[9:17]
        qp, kp, vp, dop, lp, dlp, dln, dqp, dkp, dvp, nat = refs[17:28]
        dq_out, dk_out, dv_out = _dil_views(dqkv_ref)
        _, hm = _head_masks()

        def delta_rows(i, c):
            r0 = pl.multiple_of(i * TQ, TQ)
            prod = dob_ref[pl.ds(r0, TQ), :] * ob_ref[pl.ds(r0, TQ), :].astype(f32)
            d0 = jnp.sum(jnp.where(hm[0], prod, 0.0), axis=1, keepdims=True)
            d1 = jnp.sum(jnp.where(hm[1], prod, 0.0), axis=1, keepdims=True)
            dln[pl.ds(r0, TQ), :] = jnp.where(hm[0], d0, d1)
            return c

        lax.fori_loop(0, S // TQ, delta_rows, 0)

        for g, r in enumerate(DIL):
            nbl = S // r // BAND
            if r == 1:
                srcs = (q_refs[g], k_refs[g], v_refs[g], dob_ref, lse_ref, dln)
            else:
                for dst, src in ((qp, q_refs[g]), (kp, k_refs[g]), (vp, v_refs[g]), (dop, dob_ref),
                                 (lp, lse_ref), (dlp, dln)):
                    _permute_in(dst, src, r)
                srcs = (qp, kp, vp, dop, lp, dlp)
            dkp[...] = jnp.zeros_like(dkp)
            dvp[...] = jnp.zeros_like(dvp)

            def blk(t, c, srcs=srcs, nbl=nbl):
                qs_, ks_, vs_, dos_, ls_, dls_ = srcs
                work = []
                for u in range(DIL_UNROLL):
                    r0, k0, valid = _band_geometry(DIL_UNROLL * t + u, nbl)
                    q = qs_[pl.ds(r0, BAND), :] * 0.125
                    kwf = ks_[pl.ds(k0, _band_width(nbl)), :]
                    kw = kwf.astype(bf16)
                    vw = vs_[pl.ds(k0, _band_width(nbl)), :].astype(bf16)
                    do = dos_[pl.ds(r0, BAND), :]
                    lse = ls_[pl.ds(r0, BAND), :]
                    dlt = dls_[pl.ds(r0, BAND), :]
                    for hh in (0, 1):
                        qh = jnp.where(hm[hh], q, 0.0).astype(bf16)
                        doh = jnp.where(hm[hh], do, 0.0).astype(bf16)
                        kh = jnp.where(hm[hh], kwf, 0.0).astype(bf16)
                        work.append((u, hh, r0, k0, valid, qh, doh, kh, lse[:, hh * HD:hh * HD + 1],
                                     dlt[:, hh * HD:hh * HD + 1], _dot_nt(qh, kw), _dot_nt(doh, vw)))
                for u, hh, r0, k0, valid, qh, doh, kh, lse_h, dlt_h, s, dp in work:
                    if hh == 0:
                        dq = jnp.zeros((BAND, 128), f32)
                        dk = jnp.zeros((_band_width(nbl), 128), f32)
                        dv = jnp.zeros((_band_width(nbl), 128), f32)
                    pr = jnp.where(valid, jnp.exp(s - lse_h), 0.0)
                    dsb = (pr * (dp - dlt_h)).astype(bf16)
                    dv = dv + _dot_tn(pr.astype(bf16), doh)
                    dk = dk + _dot_tn(dsb, qh)
                    dq = dq + _dot(dsb, kh)
                    if hh == 1:
                        dqp[pl.ds(r0, BAND), :] = dq * 0.125
                        dkp[pl.ds(k0, _band_width(nbl)), :] += dk
                        dvp[pl.ds(k0, _band_width(nbl)), :] += dv
                return c

            lax.fori_loop(0, S // BAND // DIL_UNROLL, blk, 0)

            for acc, out, roped in ((dqp, dq_out[g], True), (dkp, dk_out[g], True), (dvp, dv_out[g], False)):
                if r == 1:
                    src = acc
                else:
                    _permute_out(nat, acc, r)
                    src = nat

                def emit(i, c, src=src, out=out, roped=roped):
                    r0 = pl.multiple_of(i * TQ, TQ)
                    d = src[pl.ds(r0, TQ), :]
                    if roped:
                        d = (d * c_ref[pl.ds(r0, TQ), :] + pltpu.roll(d * s1_ref[pl.ds(r0, TQ), :], 8, 1)
                             + pltpu.roll(d * s2_ref[pl.ds(r0, TQ), :], 120, 1))
                    out[pl.ds(r0, TQ), :] = d.astype(bf16)
                    return c

                lax.fori_loop(0, S // TQ, emit, 0)

    pair = pl.BlockSpec((S, 128), lambda p: (0, p))
    tab = pl.BlockSpec((S, 128), lambda p: (0, 0))
    blk_spec = pl.BlockSpec((S, DIL_BLK), lambda p: (0, p))
    return _call(
        order, body, [qkvb] * 9 + [dob, ob, lseb, c_t, s1_t, s2_t, dproj], name="dil_bwd", grid=(2,),
        in_specs=_dil_in_specs() + [pair, pair, pair, tab, tab, tab, pl.BlockSpec(memory_space=pl.ANY)],
        out_specs=blk_spec,
        out_shape=jax.ShapeDtypeStruct((S, NP), bf16),
        input_output_aliases={15: 0},
        scratch_shapes=[pltpu.VMEM((S, 128), f32)] * 11,
        compiler_params=_params(("parallel",)),
    )


def _inproj_bwd(order, dproj, wt, x, dx2, g1):
    tm = 256

    def body(d_ref, w_ref, x_ref, dx2_ref, g_ref, dx_ref, dg_ref):
        i = pl.program_id(0)
        dh = _dot(d_ref[...], w_ref[...])
        xb = x_ref[...]
        r = lax.rsqrt(jnp.mean(xb * xb, axis=-1, keepdims=True) + EPS)
        xh = xb * r
        gdh = dh * g_ref[...]
        dx_ref[...] = dx2_ref[...] + r * (gdh - xh * jnp.mean(gdh * xh, axis=-1, keepdims=True))

        @pl.when(i == 0)
        def _():
            dg_ref[...] = jnp.zeros_like(dg_ref)

        dg_ref[...] += jnp.sum(dh * xh, axis=0, keepdims=True)

    row = pl.BlockSpec((tm, D), lambda i: (i, 0))
    vec = pl.BlockSpec((1, D), lambda i: (0, 0))
    return _call(
        order, body, (dproj, wt, x, dx2, g1), name="inproj_bwd", grid=(S // tm,),
        in_specs=[pl.BlockSpec((tm, NP), lambda i: (i, 0)), pl.BlockSpec((NP, D), lambda i: (0, 0)), row, row, vec],
        out_specs=[row, vec],
        out_shape=[jax.ShapeDtypeStruct((S, D), f32), jax.ShapeDtypeStruct((1, D), f32)],
        compiler_params=_params(("arbitrary",)),
    )


HBM = pl.BlockSpec(memory_space=pltpu.HBM)
SEM = pl.BlockSpec(memory_space=pltpu.SEMAPHORE)
SMALL_ROWS = 8


def _comm_call(name, body, bufs, order, sems_in=(), new_sems=(), behind=()):
    nb, ns, nn = len(bufs), len(sems_in), len(new_sems)
    extra = order.token_for(bufs) + list(behind)

    def kern(*refs):
        off = nb + ns + len(extra)
        body(refs[:nb], refs[nb:nb + ns], refs[off:off + nn])
        refs[-1][...] = jnp.zeros((8, 128), f32)

    res = pl.pallas_call(
        kern, name=name,
        in_specs=[HBM] * nb + [SEM] * ns + [pl.BlockSpec(memory_space=pl.ANY)] * len(extra),
        out_specs=[SEM] * nn + [HBM] * nb + [pl.BlockSpec(memory_space=pltpu.VMEM)],
        out_shape=[pltpu.SemaphoreType.DMA((k,)) for k in new_sems] + [pltpu.HBM(b.shape, b.dtype) for b in bufs]
        + [jax.ShapeDtypeStruct((8, 128), f32)],
        input_output_aliases={i: nn + i for i in range(nb)},
        compiler_params=pltpu.CompilerParams(has_side_effects=pltpu.SideEffectType.DATAFLOW_SIDE_EFFECTING),
    )(*[pltpu.with_memory_space_constraint(b, pltpu.HBM) for b in bufs], *sems_in, *extra)
    order.mark(res[-1])
    return list(res[:nn]), list(res[nn:nn + nb])


def _place():
    x, y, c = lax.axis_index("x"), lax.axis_index("y"), lax.axis_index("c")
    chips = [(1 - x, y), (x, 1 - y), (1 - x, 1 - y)]
    return x, y, c, chips


def _rcopy(src, dst, ssem, rsem, dev):
    return pltpu.make_async_remote_copy(src_ref=src, dst_ref=dst, send_sem=ssem, recv_sem=rsem,
                                        device_id=dev, device_id_type=pl.DeviceIdType.MESH)


def _half(nrows, which):
    return pl.ds(which * (nrows // 2), nrows // 2)


def _ici_copies(stack, group_sizes, ssems, rsems):
    x, y, c, chips = _place()
    me_q = 2 * x + y
    sends, recvs = [], []
    a = 0
    for grp, size in enumerate(group_sizes):
        for k in range(size):
            rows = _half(stack[a].shape[1], c)
            for j, (cx, cy) in enumerate(chips):
                mine = stack[a].at[me_q, rows]
                sends.append(_rcopy(mine, mine, ssems[grp].at[k * 3 + j], rsems[grp].at[k * 3 + j], (cx, cy, c)))
                theirs = stack[a].at[2 * cx + cy, rows]
                recvs.append(_rcopy(theirs, theirs, ssems[grp].at[k * 3 + j], rsems[grp].at[k * 3 + j],
                                    (cx, cy, c)))
            a += 1
    return sends, recvs


def _allgather_start(name, stacks, order):
    n = len(stacks)

    def body(bufs, _, new):
        sends, _r = _ici_copies(bufs, [n], [new[0]], [new[1]])
        for cp in sends:
            cp.start()

    return _comm_call(name, body, stacks, order, new_sems=(3 * n, 3 * n))


def _forward_copies(stack, ssem, rsem):
    x, y, c, chips = _place()
    sib = (x, y, 1 - c)
    sends, recvs = [], []
    for a in range(len(stack)):
        for j, (cx, cy) in enumerate(chips):
            landed = stack[a].at[2 * cx + cy, _half(stack[a].shape[1], c)]
            sends.append(_rcopy(landed, landed, ssem.at[a * 3 + j], rsem.at[a * 3 + j], sib))
            other = stack[a].at[2 * cx + cy, _half(stack[a].shape[1], 1 - c)]
            recvs.append(_rcopy(other, other, ssem.at[a * 3 + j], rsem.at[a * 3 + j], sib))
    return sends, recvs


def _allgather_forward(name, stacks, sems, order, behind=()):
    n = len(stacks)

    def body(bufs, taken, new):
        sends, recvs = _ici_copies(bufs, [n], [taken[0]], [taken[1]])
        fwd, _r = _forward_copies(bufs, new[0], new[1])
        for arrived, onward in zip(recvs, fwd):
            arrived.wait_recv()
            onward.start()
        for cp in sends:
            cp.wait_send()

    return _comm_call(name, body, stacks, order, sems_in=sems, new_sems=(3 * n, 3 * n), behind=behind)


def _allgather_finish(name, stacks, sems, order):
    def body(bufs, taken, _):
        sends, recvs = _forward_copies(bufs, taken[0], taken[1])
        for cp in sends:
            cp.wait_send()
        for cp in recvs:
            cp.wait_recv()

    return _comm_call(name, body, stacks, order, sems_in=sems)[1]


def _window_unit(q, j):
    return C2I[WIN_UNIT0[q] + j]


def _pair_copies(g, t, ssem, rsem, gathered):
    x, y, c, _ = _place()
    sib = (x, y, 1 - c)
    cps, whole = [], []
    for a in range(len(g)):
        if a == 0 and gathered:
            for q in range(NCHIP):
                for j in range(WIN_UNITS // 2):
                    u = jnp.where(c == 0, _window_unit(q, WIN_UNITS // 2 + j), _window_unit(q, j))
                    src = g[0].at[pl.ds(pl.multiple_of(u * UNIT, UNIT), UNIT), :]
                    cps.append(_rcopy(src, t[0].at[q, pl.ds(j * UNIT, UNIT), :], ssem.at[0], rsem.at[0], sib))
            whole.append(_rcopy(t[0], t[0], ssem.at[0], rsem.at[0], sib))
        else:
            cp = _rcopy(g[a].at[:, _half(g[a].shape[1], 1 - c), :], t[a], ssem.at[a], rsem.at[a], sib)
            cps.append(cp)
            whole.append(cp)
    return cps, whole


def _comm_multi(name, parts, order):
    def body(buf_refs, taken, new):
        ib = it = inew = 0
        for pbody, pbufs, psems, pnew, _ in parts:
            pbody(buf_refs[ib:ib + len(pbufs)], taken[it:it + len(psems)], new[inew:inew + len(pnew)])
            ib, it, inew = ib + len(pbufs), it + len(psems), inew + len(pnew)

    sems, bufs = _comm_call(name, body, [b for p in parts for b in p[1]], order,
                            sems_in=[s for p in parts for s in p[2]], new_sems=[k for p in parts for k in p[3]])
    out, ib, inew = [], 0, 0
    for _, pbufs, _, pnew, unpack in parts:
        out.append(unpack(sems[inew:inew + len(pnew)], bufs[ib:ib + len(pbufs)]))
        ib, inew = ib + len(pbufs), inew + len(pnew)
    return out


def _pair_start_part(gs, gathered=False):
    n = len(gs)
    ts = [lax.empty((NCHIP, WIN_ROWS // 2, D) if (a == 0 and gathered) else (NCHIP, g.shape[1] // 2, g.shape[2]), f32)
          for a, g in enumerate(gs)]

    def body(bufs, _, new):
        for cp in _pair_copies(bufs[:n], bufs[n:], new[0], new[1], gathered)[0]:
            cp.start()

    return body, list(gs) + ts, (), (n, n), lambda sems, bufs: (sems, bufs)


def _pair_wait_part(bufs, sems, gathered=False):
    n = len(bufs) // 2

    def body(refs, taken, _):
        for cp in _pair_copies(refs[:n], refs[n:], taken[0], taken[1], gathered)[1]:
            cp.wait_send()
            cp.wait_recv()

    return body, list(bufs), list(sems), (), lambda _, out: (out[:n], out[n:])


def _row_tile(h):
    return min(h, 256)


def _pair_add(order, g, t, q_arr, c_arr, name):
    _, R, C = g.shape
    h = R // 2
    tr = _row_tile(h)
    nblk = h // tr

    def body(q_ref, c_ref, g_ref, t_ref, own_ref, p16_ref):
        s = g_ref[...] + t_ref[...]
        p16_ref[...] = s.astype(bf16)

        @pl.when(pl.program_id(1) == q_ref[0])
        def _():
            own_ref[...] = s

    blk = pl.BlockSpec((None, tr, C), lambda i, q, q_ref, c_ref: (q, i, 0))
    return _call_indexed(
        order, body, (q_arr, c_arr), (g, t), (nblk, NCHIP),
        [pl.BlockSpec((None, tr, C), lambda i, q, q_ref, c_ref: (q, c_ref[0] * nblk + i, 0)), blk],
        [pl.BlockSpec((tr, C), lambda i, q, q_ref, c_ref: (i, 0)), blk],
        name=name,
        out_shape=[jax.ShapeDtypeStruct((h, C), f32), jax.ShapeDtypeStruct((NCHIP, h, C), bf16)],
        compiler_params=_params(("parallel", "arbitrary")),
    )


def _pair_add_gathered(order, dwt, t, q_arr, c_arr, name):
    half_units, half_rows = WIN_UNITS // 2, WIN_ROWS // 2
    table = jnp.asarray([_window_unit(q, j) for q in range(NCHIP) for j in range(WIN_UNITS)], jnp.int32)

    def body(tab_ref, q_ref, c_ref, g_hbm, t_ref, own_ref, p16_ref, buf, sem):
        q = pl.program_id(0)

        def gather(w, slot):
            cps = []
            for j in range(half_units):
                u = tab_ref[w * WIN_UNITS + c_ref[0] * half_units + j]
                cps.append(pltpu.make_async_copy(g_hbm.at[pl.ds(pl.multiple_of(u * UNIT, UNIT), UNIT), :],
                                                 buf.at[slot, pl.ds(j * UNIT, UNIT), :], sem.at[slot]))
            return cps

        @pl.when(q == 0)
        def _():
            for cp in gather(0, 0):
                cp.start()

        @pl.when(q + 1 < NCHIP)
        def _():
            for cp in gather(q + 1, (q + 1) % 2):
                cp.start()

        slot = q % 2
        pltpu.make_async_copy(buf.at[slot], buf.at[slot], sem.at[slot]).wait()
        s = buf[slot] + t_ref[...]
        p16_ref[...] = s.astype(bf16)

        @pl.when(q == q_ref[0])
        def _():
            own_ref[...] = s

    blk = pl.BlockSpec((None, half_rows, D), lambda q, tab_ref, q_ref, c_ref: (q, 0, 0))
    return _call_indexed(
        order, body, (table, q_arr, c_arr), (dwt, t), (NCHIP,),
        [pl.BlockSpec(memory_space=pl.ANY), blk],
        [pl.BlockSpec((half_rows, D), lambda q, tab_ref, q_ref, c_ref: (0, 0)), blk],
        scratch_shapes=[pltpu.VMEM((2, half_rows, D), f32), pltpu.SemaphoreType.DMA((2,))],
        name=name,
        out_shape=[jax.ShapeDtypeStruct((half_rows, D), f32),
                   jax.ShapeDtypeStruct((NCHIP, half_rows, D), bf16)],
        compiler_params=_params(("arbitrary",)),
    )


def _shard_copies(p, r, sm, ssem, rsem):
    x, y, c, chips = _place()
    n = len(p)
    sends, recvs = [], []
    for a in range(n):
        for j, (cx, cy) in enumerate(chips):
            k = a * 3 + j
            sends.append(_rcopy(p[a].at[2 * cx + cy], r[a].at[j], ssem.at[k], rsem.at[k], (cx, cy, c)))
            recvs.append(_rcopy(r[a].at[j], r[a].at[j], ssem.at[k], rsem.at[k], (cx, cy, c)))
    if sm is not None:
        mine = sm.at[4 * x + 2 * y + c]
        for i in range(1, 8):
            px = (1 - x) if i & 4 else x
            py = (1 - y) if i & 2 else y
            pc = (1 - c) if i & 1 else c
            k = 3 * n + i - 1
            sends.append(_rcopy(mine, mine, ssem.at[k], rsem.at[k], (px, py, pc)))
            slot = sm.at[4 * px + 2 * py + pc]
            recvs.append(_rcopy(slot, slot, ssem.at[k], rsem.at[k], (px, py, pc)))
    return sends, recvs


def _shard_start_part(p16s, sm=None):
    n = len(p16s)
    rs = [lax.empty((3,) + p.shape[1:], bf16) for p in p16s]
    extra = [] if sm is None else [sm]
    nsem = 3 * n + (7 if sm is not None else 0)

    def body(bufs, _, new):
        sends, _r = _shard_copies(bufs[:n], bufs[n:2 * n], bufs[2 * n] if extra else None, new[0], new[1])
        for cp in sends:
            cp.start()

    return body, list(p16s) + rs + extra, (), (nsem, nsem), lambda sems, bufs: (sems, bufs)


def _shard_wait_part(bufs, sems, n):
    has_sm = len(bufs) > 2 * n

    def body(refs, taken, _):
        sends, recvs = _shard_copies(refs[:n], refs[n:2 * n], refs[2 * n] if has_sm else None, taken[0], taken[1])
        for cp in sends:
            cp.wait_send()
        for cp in recvs:
            cp.wait_recv()

    return body, list(bufs), list(sems), (), lambda _, out: (out[n:2 * n], (out[2 * n] if has_sm else None))


def _shard_sum(order, own, r, c_arr, name):
    h, C = own.shape
    tr = _row_tile(h)
    nblk = h // tr

    def body(c_ref, p_ref, r_ref, o_ref):
        s = p_ref[...]
        for j in range(3):
            s = s + r_ref[j].astype(f32)
        o_ref[...] = s

    return _call_indexed(
        order, body, (c_arr,), (own, r), (nblk,),
        [pl.BlockSpec((tr, C), lambda i, c_ref: (i, 0)), pl.BlockSpec((3, tr, C), lambda i, c_ref: (0, i, 0))],
        pl.BlockSpec((tr, C), lambda i, c_ref: (c_ref[0] * nblk + i, 0)),
        name=name, out_shape=jax.ShapeDtypeStruct((2 * h, C), f32),
        compiler_params=_params(("parallel",)),
    )


def _swap_copies(full, ssem, rsem):
    x, y, c, _ = _place()
    sends, recvs = [], []
    for a in range(len(full)):
        mine = full[a].at[_half(full[a].shape[0], c)]
        sends.append(_rcopy(mine, mine, ssem.at[a], rsem.at[a], (x, y, 1 - c)))
        other = full[a].at[_half(full[a].shape[0], 1 - c)]
        recvs.append(_rcopy(other, other, ssem.at[a], rsem.at[a], (x, y, 1 - c)))
    return sends, recvs


def _swap_start_part(fulls):
    n = len(fulls)

    def body(bufs, _, new):
        for cp in _swap_copies(bufs, new[0], new[1])[0]:
            cp.start()

    return body, list(fulls), (), (n, n), lambda sems, bufs: (sems, bufs)


def _swap_wait_part(fulls, sems):
    def body(refs, taken, _):
        sends, recvs = _swap_copies(refs, taken[0], taken[1])
        for cp in sends:
            cp.wait_send()
        for cp in recvs:
            cp.wait_recv()

    return body, list(fulls), list(sems), (), lambda _, out: out


def _small_finish(order, sm, ws, ms, vs):
    n = len(ws)

    def body(sm_ref, *refs):
        s = sm_ref[0]
        for d in range(1, 8):
            s = s + sm_ref[d]
        loss_ref, g_refs, upd_refs = refs[3 * n], refs[3 * n + 1:4 * n + 1], refs[4 * n + 1:]
        loss_ref[...] = s[n:n + 1, 0:1]
        for i in range(n):
            g = s[i:i + 1, 0:ws[i].shape[1]]
            g_refs[i][...] = g
            res = _adamw_math(refs[i][...], g, refs[n + i][...], refs[2 * n + i][...])
            for k in range(3):
                upd_refs[3 * i + k][...] = res[k]

    out = _call(order, body, [sm] + list(ws) + list(ms) + list(vs), name="small_sum_adamw",
                out_shape=[jax.ShapeDtypeStruct((1, 1), f32)] + [jax.ShapeDtypeStruct(w.shape, f32) for w in ws]
                + [jax.ShapeDtypeStruct(w.shape, f32) for w in ws for _ in range(3)])
    return out[0], out[1:n + 1], [out[n + 1 + 3 * i:n + 4 + 3 * i] for i in range(n)]


def _adamw_math(w, g, m, v):
    m = ADAM_B1 * m + (1.0 - ADAM_B1) * g
    v = ADAM_B2 * v + (1.0 - ADAM_B2) * (g * g)
    m_hat = m / (1.0 - ADAM_B1 ** ADAM_STEP)
    v_hat = v / (1.0 - ADAM_B2 ** ADAM_STEP)
    return -ADAM_LR * (m_hat / (jnp.sqrt(v_hat) + ADAM_EPS) + ADAM_WD * w), m, v


def _adamw(order, w, g, m, v, name):
    R, C = w.shape
    if R <= 256 or R % 256 == 0:
        tr, tc = min(R, 256), C
    else:
        tr, tc = R, 128

    def body(w_ref, g_ref, m_ref, v_ref, d_ref, nm_ref, nv_ref, g_out):
        g = g_ref[...]
        g_out[...] = g
        d_ref[...], nm_ref[...], nv_ref[...] = _adamw_math(w_ref[...], g, m_ref[...], v_ref[...])

    blk = pl.BlockSpec((tr, tc), lambda i, j: (i, j))
    return _call(
        order, body, (w, g, m, v), name=name, grid=(R // tr, C // tc), in_specs=[blk] * 4, out_specs=[blk] * 4,
        out_shape=[jax.ShapeDtypeStruct((R, C), f32)] * 4, compiler_params=_params(("parallel", "parallel")),
    )


def _feature_rows(w):
    return jnp.transpose(w, (2, 0, 1))


WIN_STEP = 128
WIN_PIECE = 3 * WIN_STEP


def _window_stacks(order, w, q_arr):
    n_piece = -(-SHARD_IN // WIN_PIECE)
    steps = WIN_ROWS // WIN_STEP
    head, tail = UNIT, WIN_STEP
    assert max(OWN_ROW0) < head and OWN_ROW0[1] + FA_AT == UNIT and WIN_ROWS - tail <= SHARD_IN - N_FA
    rest = SHARD_IN - (n_piece - 1) * WIN_PIECE

    def body(q_ref, w_ref, win_ref, fa_ref, buf, fabuf, sem):
        i = pl.program_id(0)
        q = q_ref[0]
        chip1 = q == 1
        row0 = jnp.where(q == 0, OWN_ROW0[0], jnp.where(chip1, OWN_ROW0[1], jnp.where(q == 2, OWN_ROW0[2], OWN_ROW0[3])))
        skip = jnp.where(chip1, N_FA, 0)

        def copy(src0, dst0, n, slot):
            return pltpu.make_async_copy(w_ref.at[pl.ds(src0, n)], buf.at[pl.ds(dst0, n)], sem.at[slot])

        def piece(j, on_chip1):
            if j == 0 and on_chip1:
                return [copy(0, OWN_ROW0[1], FA_AT, 0),
                        copy(FA_AT + N_FA, UNIT, WIN_PIECE - FA_AT, n_piece)]
            if j == 0:
                return [copy(0, row0, WIN_PIECE, 0)]
            if j == n_piece - 1:
                n = rest - (N_FA if on_chip1 else 0)
                return [copy(SHARD_IN - n, row0 + SHARD_IN - skip - n, n, j)]
            return [copy(j * WIN_PIECE + skip, row0 + j * WIN_PIECE, WIN_PIECE, j)]

        def both(j, act):
            if 0 < j < n_piece - 1:
                for c in piece(j, False):
                    act(c)
                return
            for on_chip1 in (False, True):
                @pl.when(chip1 if on_chip1 else jnp.logical_not(chip1))
                def _():
                    for c in piece(j, on_chip1):
                        act(c)

        fa_copy = pltpu.make_async_copy(w_ref.at[pl.ds(FA_AT, N_FA)], fabuf.at[pl.ds(0, N_FA)], sem.at[n_piece + 1])

        @pl.when(i == 0)
        def _():
            buf[pl.ds(0, head)] = jnp.zeros((head, 1, D), f32)
            buf[pl.ds(WIN_ROWS - tail, tail)] = jnp.zeros((tail, 1, D), f32)
            fabuf[pl.ds(N_FA, FA_ROWS - N_FA)] = jnp.zeros((FA_ROWS - N_FA, 1, D), f32)
            fa_copy.start()
            for j in range(n_piece):
                both(j, lambda c: c.start())
            fa_copy.wait()
            fa_ref[...] = fabuf[...].reshape(FA_ROWS, D).astype(bf16)

        for j in range(n_piece):
            @pl.when(i == j * (WIN_PIECE // WIN_STEP))
            def _():
                both(j, lambda c: c.wait())

        win_ref[...] = buf[pl.ds(pl.multiple_of(i * WIN_STEP, WIN_STEP), WIN_STEP)].reshape(WIN_STEP, D).astype(bf16)

    return _call_indexed(
        order, body, (q_arr,), (w,), (steps,), [pl.BlockSpec(memory_space=pl.ANY)],
        [pl.BlockSpec((None, WIN_STEP, D), lambda i, q: (q[0], i, 0)),
         pl.BlockSpec((None, FA_ROWS, D), lambda i, q: (q[0], 0, 0))],
        scratch_shapes=[pltpu.VMEM((WIN_ROWS, 1, D), f32), pltpu.VMEM((FA_ROWS, 1, D), f32),
                        pltpu.SemaphoreType.DMA((n_piece + 2,))],
        name="window_w_in", out_shape=[jax.ShapeDtypeStruct((NCHIP, WIN_ROWS, D), bf16),
                                       jax.ShapeDtypeStruct((NCHIP, FA_ROWS, D), bf16)],
        compiler_params=_params(("arbitrary",)),
    )


def _unfeature_rows(a):
    return jnp.transpose(a, (1, 2, 0))


ADAM_IN_ROWS = 134
ADAM_IN_STEPS = SHARD_IN // ADAM_IN_ROWS
ADAM_IN_CHUNK = 136
ADAM_IN_CHUNKS = ADAM_IN_STEPS + 1
ADAM_IN_BUF = WIN_ROWS + N_FA


def _adamw_w_in(order, w, gwin, gfa, m, v, q_arr):
    assert ADAM_IN_CHUNK * ADAM_IN_STEPS < WIN_ROWS <= ADAM_IN_CHUNK * ADAM_IN_CHUNKS
    assert OWN_ROW0[NCHIP - 1] + ADAM_IN_ROWS <= 2 * ADAM_IN_CHUNK and ADAM_IN_CHUNK >= ADAM_IN_ROWS
    last0 = ADAM_IN_CHUNK * ADAM_IN_STEPS
    cut = OWN_ROW0[1] + FA_AT

    def body(q_ref, w_ref, gwin_ref, gfa_ref, m_ref, v_ref, go_ref, d_ref, nm_ref, nv_ref, buf, sem):
        i = pl.program_id(0)
        q = q_ref[0]
        chip1 = q == 1
        shift = jnp.where(chip1, N_FA, 0)

        def copy(src_ref, src0, dst0, n, slot):
            return pltpu.make_async_copy(src_ref.at[pl.ds(src0, n)], buf.at[pl.ds(dst0, n), 0], sem.at[slot])

        def first(on_chip1):
            if on_chip1:
                return [copy(gwin_ref, 0, 0, cut, 0), copy(gfa_ref, 0, cut, N_FA, ADAM_IN_CHUNKS),
                        copy(gwin_ref, cut, cut + N_FA, ADAM_IN_CHUNK - cut - N_FA, ADAM_IN_CHUNKS + 1)]
            return [copy(gwin_ref, 0, 0, ADAM_IN_CHUNK, 0)]

        def middle(k):
            return [copy(gwin_ref, pl.multiple_of(k * ADAM_IN_CHUNK - shift, 8), k * ADAM_IN_CHUNK, ADAM_IN_CHUNK, k)]

        def last(on_chip1):
            n = WIN_ROWS - last0 + (N_FA if on_chip1 else 0)
            return [copy(gwin_ref, WIN_ROWS - n, last0, n, ADAM_IN_STEPS)]

        def both(make, act):
            for on_chip1 in (False, True):
                @pl.when(chip1 if on_chip1 else jnp.logical_not(chip1))
                def _():
                    for c in make(on_chip1):
                        act(c)

        @pl.when(i == 0)
        def _():
            both(first, lambda c: c.start())
            for k in range(1, ADAM_IN_STEPS):
                middle(k)[0].start()
            both(last, lambda c: c.start())
            both(first, lambda c: c.wait())

        @pl.when(i < ADAM_IN_STEPS - 1)
        def _():
            middle(i + 1)[0].wait()

        @pl.when(i == ADAM_IN_STEPS - 1)
        def _():
            both(last, lambda c: c.wait())

        row0 = jnp.where(q == 0, OWN_ROW0[0], jnp.where(chip1, OWN_ROW0[1], jnp.where(q == 2, OWN_ROW0[2], OWN_ROW0[3])))
        g = buf[pl.ds(row0 + i * ADAM_IN_ROWS, ADAM_IN_ROWS)]
        go_ref[...] = g
        d_ref[...], nm_ref[...], nv_ref[...] = _adamw_math(w_ref[...], g, m_ref[...], v_ref[...])

    blk = pl.BlockSpec((ADAM_IN_ROWS, 1, D), lambda i, q: (i, 0, 0))
    hbm = pl.BlockSpec(memory_space=pl.ANY)
    return _call_indexed(
        order, body, (q_arr,), (w, gwin, gfa, m, v), (ADAM_IN_STEPS,), [blk, hbm, hbm, blk, blk], [blk] * 4,
        scratch_shapes=[pltpu.VMEM((ADAM_IN_BUF, 1, D), f32), pltpu.SemaphoreType.DMA((ADAM_IN_CHUNKS + 2,))],
        name="adamw_w_in", out_shape=[jax.ShapeDtypeStruct((SHARD_IN, 1, D), f32)] * 4,
        compiler_params=_params(("arbitrary",)),
    )


def kernel(x, norm_attn_g, w_in, b_forget, w_branch_a, w_branch_b, w_out, norm_mlp_g, w_up, w_down, norm_final_g, loss_target, m_norm_attn_g, m_w_in, m_b_forget, m_w_branch_a, m_w_branch_b, m_w_out, m_norm_mlp_g, m_w_up, m_w_down, m_norm_final_g, v_norm_attn_g, v_w_in, v_b_forget, v_w_branch_a, v_w_branch_b, v_w_out, v_norm_mlp_g, v_w_up, v_w_down, v_norm_final_g):
    xi, yi, ci = lax.axis_index("x"), lax.axis_index("y"), lax.axis_index("c")
    q_me = 2 * xi + yi
    c_arr = jnp.reshape(ci, (1,)).astype(jnp.int32)
    q_arr = jnp.reshape(q_me, (1,)).astype(jnp.int32)
    x_, tgt = x[0], loss_target[0]

    names = ["w_branch_a", "w_branch_b", "w_out", "w_up", "w_down"]
    big = dict(zip(names, [w_branch_a[0], w_branch_b[0], w_out[0], w_up[0], w_down[0]]))
    ms = dict(zip(names, [m_w_branch_a[0], m_w_branch_b[0], m_w_out[0], m_w_up[0], m_w_down[0]]))
    vs = dict(zip(names, [v_w_branch_a[0], v_w_branch_b[0], v_w_out[0], v_w_up[0], v_w_down[0]]))
    grad, upd = {}, {}
    order = _Order()

    def run(fn, *args, **kw):
        return fn(order, *args, **kw)

    def own_slot(a):
        return lax.dynamic_update_slice(lax.empty((NCHIP,) + a.shape, a.dtype), a[None], (q_me, 0, 0))

    sem_in, in_s = _allgather_start("allgather_start_in", run(_window_stacks, _feature_rows(w_in), q_arr), order)
    sem_rest, rest = _allgather_start("allgather_start_rest", [own_slot(w.astype(bf16)) for w in big.values()], order)
    rope = _rope_tables(order.tok[0, 0])
    sem_f, in_s = _allgather_forward("allgather_forward_in", in_s, sem_in, order, behind=rope)
    wins, fas = _allgather_finish("allgather_finish_in", in_s, sem_f, order)
    wt = run(_assemble_win, wins, fas)

    bpad = jnp.pad(b_forget, ((0, 0), (0, 120)))
    h1, qkvb, qkva, gates, fa = run(_norm_inproj, x_, norm_attn_g, wt, rope)
    F = run(_forget_cumsum, fa, bpad)
    oa, lsea = run(_fox_fwd, qkva, F)
    sem_f, rest = _allgather_forward("allgather_forward_rest", rest, sem_rest, order)
    ob, lseb = run(_dil_fwd, qkvb)
    was, wbs, wouts, wups, wdowns = _allgather_finish("allgather_finish_rest", rest, sem_f, order)
    wout = wouts.reshape(D, D)
    wdown = wdowns.reshape(DFF, D)
    ya, yb, mixed = run(_branch_mix, oa, ob, was, wbs, gates)
    x2, h2 = run(_outproj_norm, mixed, wout, x_, norm_mlp_g)
    u, a = run(_mlp_up, h2, wups)
    dx3, dx3b, dg3, loss_part = run(_mlp_down_loss, a, wdown, x2, norm_final_g.reshape(1, D), tgt)

    def comm(name, *parts):
        return _comm_multi(name, list(parts), order)

    def pair_adds(group, gs, ts):
        return zip(*[run(_pair_add, gs[i], ts[i], q_arr, c_arr, "pair_add_" + nm) for i, nm in enumerate(group)])

    def shard_sums(group, p32s, rs):
        return [run(_shard_sum, p32s[i], rs[i], c_arr, "shard_sum_" + nm) for i, nm in enumerate(group)]

    def adamw_group(group, fulls):
        for nm, gfull in zip(group, fulls):
            *upd[nm], grad[nm] = run(_adamw, big[nm], gfull, ms[nm], vs[nm], "adamw_" + nm)

    grp_a, grp_b, grp_c = ["w_down", "w_up"], ["w_out", "w_branch_a", "w_branch_b"], ["w_in", "w_in_fa"]
    du = run(_mlp_down_bwd, dx3b, wdown, u)
    dwdown = run(_mm, a, dx3b, "tn", f32, 1024, D, "wgrad_down")
    dwup = run(_mm, h2, du, "tn", f32, D, 1024, "wgrad_up", stack_cols=True)
    ((sem_pa, buf_pa),) = comm("pair_start_a", _pair_start_part([dwdown.reshape(NCHIP, DFF // NCHIP, D), dwup]))
    dx2, dx2b, dg2 = run(_mlp_up_bwd, du, wups, x2, dx3, norm_mlp_g)
    ((gs, ts),) = comm("pair_wait_a", _pair_wait_part(buf_pa, sem_pa))
    p32_a, p16_a = pair_adds(grp_a, gs, ts)
    ((sem_sa, buf_sa),) = comm("shard_start_a", _shard_start_part(p16_a))
    dya, dyb, dproj = run(_gate_bwd, dx2b, wout, gates, ya, yb)
    dwout = run(_mm, mixed, dx2b, "tn", f32, D, D, "wgrad_out")
    doa, dob = run(_branch_bwd, dya, dyb, was, wbs)
    dwas, dwbs = run(_branch_wgrad, oa, ob, dya, dyb)
    ((sem_pb, buf_pb),) = comm("pair_start_b", _pair_start_part([dwout.reshape(NCHIP, D // NCHIP, D), dwas, dwbs]))
    dF, dproj = run(_fox_bwd, qkva, doa, oa, lsea, F, dproj)
    (gs, ts), (rs_a, _) = comm("pair_wait_b_shard_wait_a", _pair_wait_part(buf_pb, sem_pb),
                               _shard_wait_part(buf_sa, sem_sa, len(grp_a)))
    p32_b, p16_b = pair_adds(grp_b, gs, ts)
    fulls_a = shard_sums(grp_a, p32_a, rs_a)
    (sem_wa, fulls_a), (sem_sb, buf_sb) = comm("swap_start_a_shard_start_b", _swap_start_part(fulls_a),
                                               _shard_start_part(p16_b))
    dbf, dproj = run(_forget_bwd, dF, fa, bpad, dproj)
    dproj = run(_dil_bwd, qkvb, dob, ob, lseb, rope, dproj)
    (rs_b, _), fulls_a = comm("shard_wait_b_swap_wait_a", _shard_wait_part(buf_sb, sem_sb, len(grp_b)),
                              _swap_wait_part(fulls_a, sem_wa))
    fulls_b = shard_sums(grp_b, p32_b, rs_b)
    ((sem_wb, fulls_b),) = comm("swap_start_b", _swap_start_part(fulls_b))
    dwt = run(_mm, dproj, h1, "tn", f32, 512, D, "wgrad_in")
    dwfa = jnp.broadcast_to(dwt[F_FA:F_FA + FA_ROWS][None], (NCHIP, FA_ROWS, D))
    (sem_pc, buf_pc), fulls_b = comm("pair_start_c_swap_wait_b", _pair_start_part([dwt, dwfa], gathered=True),
                                     _swap_wait_part(fulls_b, sem_wb))
    adamw_group(grp_b, fulls_b)
    (((dwt_c, dwfa_c), (t_in, t_fa)),) = comm("pair_wait_c", _pair_wait_part(buf_pc, sem_pc, gathered=True))
    p32_in, p16_in = run(_pair_add_gathered, dwt_c, t_in, q_arr, c_arr, "pair_add_w_in")
    p32_fa, p16_fa = run(_pair_add, dwfa_c, t_fa, q_arr, c_arr, "pair_add_w_in_fa")
    ((sem_sc, buf_sc),) = comm("shard_start_c", _shard_start_part([p16_in, p16_fa]))
    gx, dg1 = run(_inproj_bwd, dproj, wt, x_, dx2, norm_attn_g)
    adamw_group(grp_a, fulls_a)
    small = jnp.concatenate([dg1, dg2, dg3, jnp.pad(dbf[:, 0:8], ((0, 0), (0, D - 8))),
                             jnp.pad(loss_part, ((0, 0), (0, D - 128))),
                             jnp.zeros((SMALL_ROWS - 5, D), f32)], axis=0)
    sm = lax.dynamic_update_slice(lax.empty((8, SMALL_ROWS, D), f32), small[None],
                                  (4 * xi + 2 * yi + ci, 0, 0))
    (sem_sm, buf_sm), (rs_c, _) = comm("small_start_shard_wait_c", _shard_start_part([], sm),
                                       _shard_wait_part(buf_sc, sem_sc, len(grp_c)))
    fulls_c = shard_sums(grp_c, [p32_in, p32_fa], rs_c)
    (sem_wc, fulls_c), (_, sm) = comm("swap_start_c_small_wait", _swap_start_part(fulls_c),
                                      _shard_wait_part(buf_sm, sem_sm, 0))
    smalls = ["norm_attn_g", "norm_mlp_g", "norm_final_g", "b_forget"]
    loss, gs, res = run(_small_finish, sm, [norm_attn_g, norm_mlp_g, norm_final_g.reshape(1, D), b_forget],
                        [m_norm_attn_g, m_norm_mlp_g, m_norm_final_g.reshape(1, D), m_b_forget],
                        [v_norm_attn_g, v_norm_mlp_g, v_norm_final_g.reshape(1, D), v_b_forget])
    loss = loss.reshape(())
    grad.update(zip(smalls, gs))
    upd.update(zip(smalls, res))

    ((gwin, gfa),) = comm("swap_wait_c", _swap_wait_part(fulls_c, sem_wc))
    res_in = run(_adamw_w_in, _feature_rows(w_in), gwin, gfa, _feature_rows(m_w_in), _feature_rows(v_w_in), q_arr)
    grad["w_in"] = _unfeature_rows(res_in[0])
    upd["w_in"] = [_unfeature_rows(t) for t in res_in[1:]]

    order_out = ["norm_attn_g", "w_in", "b_forget", "w_branch_a", "w_branch_b", "w_out", "norm_mlp_g", "w_up",
                 "w_down", "norm_final_g"]
    shapes = dict(norm_attn_g=norm_attn_g.shape, w_in=w_in.shape, b_forget=b_forget.shape,
                  w_branch_a=w_branch_a.shape, w_branch_b=w_branch_b.shape, w_out=w_out.shape,
                  norm_mlp_g=norm_mlp_g.shape, w_up=w_up.shape, w_down=w_down.shape, norm_final_g=norm_final_g.shape)
    outs = [loss, gx.reshape(x.shape)]
    outs += [grad[nm].reshape(shapes[nm]) for nm in order_out]
    for k in range(3):
        outs += [upd[nm][k].reshape(shapes[nm]) for nm in order_out]
    return tuple(outs)
```

```python
import jax
import jax.numpy as jnp
from jax import lax
from jax.experimental import pallas as pl
from jax.experimental.pallas import tpu as pltpu

f32 = jnp.float32
bf16 = jnp.bfloat16

S = 2048
D = 1024
DFF = 4096
HD = 64
FOXW = 512
DILOUT = 256
DIL = (1, 4, 16)
BAND = 128
EPS = 1e-6
NEG = -1e30
ROPE_THETA = 500000.0
NCHIP = 4
TQ = 256

ADAM_LR, ADAM_B1, ADAM_B2, ADAM_EPS, ADAM_WD, ADAM_STEP = 0.001, 0.9, 0.999, 1e-08, 0.01, 10
VMEM_LIMIT = 56 * 1024 * 1024

UNIT = 64
NP = 6144
F_DIL, F_FOX, F_FA, F_G = 0, 2304, 3840, 4096
DIL_BLK, FOX_BLK = 1152, 384
WIN_UNITS, WIN_ROWS = 24, 1536
WIN_UNIT0 = (0, 23, 45, 68)
OWN_ROW0 = (0, 2, 60, 62)
SHARD_IN = 1474
N_FA = 8
FA_AT = 1536 - SHARD_IN
FA_ROWS = 32


def _compact_to_internal():
    c2i = {}
    for p in range(2):
        for role in range(3):
            for g in range(3):
                for hh in range(2):
                    c2i[24 + 12 * role + 4 * g + 2 * p + hh] = 18 * p + 6 * role + 2 * g + hh
    for p in range(4):
        for role in range(3):
            for hh in range(2):
                c2i[8 * role + 2 * p + hh] = F_FOX // UNIT + 6 * p + 2 * role + hh
    for j in range(32):
        c2i[60 + j] = F_G // UNIT + j
    return c2i


C2I = _compact_to_internal()
OVERLAP_UNITS = (23, 45, 46, 68)


def _params(sem=None):
    return pltpu.CompilerParams(dimension_semantics=sem, vmem_limit_bytes=VMEM_LIMIT)


class _Order:
    def __init__(self):
        self.tok = None

    def mark(self, v):
        self.tok = v

    def token_for(self, args):
        return [] if self.tok is None or any(self.tok is a for a in args) else [self.tok]


def _call(order, body, args, in_specs=None, **kw):
    args = list(args)
    n_in = len(args)
    if in_specs is None:
        in_specs = [pl.BlockSpec(memory_space=pltpu.VMEM)] * n_in
    kern = body
    extra = order.token_for(args)
    if extra:
        in_specs = list(in_specs) + [pl.BlockSpec(memory_space=pl.ANY)]

        def kern(*refs):
            body(*refs[:n_in], *refs[n_in + 1:])

    out = pl.pallas_call(kern, in_specs=in_specs, **kw)(*args, *extra)
    order.mark(out[0] if isinstance(out, (tuple, list)) else out)
    return out


def _call_indexed(order, body, scalars, args, grid, in_specs, out_specs, scratch_shapes=(), **kw):
    args, in_specs = list(args), list(in_specs)
    n_front = len(scalars) + len(args)
    kern = body
    extra = order.token_for(args)
    if extra:
        in_specs.append(pl.BlockSpec(memory_space=pl.ANY))

        def kern(*refs):
            body(*refs[:n_front], *refs[n_front + 1:])

    out = pl.pallas_call(
        kern, grid_spec=pltpu.PrefetchScalarGridSpec(num_scalar_prefetch=len(scalars), grid=grid, in_specs=in_specs,
                                                     out_specs=out_specs, scratch_shapes=scratch_shapes),
        **kw)(*scalars, *args, *extra)
    order.mark(out[0] if isinstance(out, (tuple, list)) else out)
    return out


def _dot(a, b):
    return jnp.dot(a, b, preferred_element_type=f32)


def _dot_nt(a, b):
    return lax.dot_general(a, b, (((1,), (1,)), ((), ())), preferred_element_type=f32)


def _dot_tn(a, b):
    return lax.dot_general(a, b, (((0,), (0,)), ((), ())), preferred_element_type=f32)


def _split3(x):
    hi = x.astype(bf16)
    r1 = x - hi.astype(f32)
    mid = r1.astype(bf16)
    lo = (r1 - mid.astype(f32)).astype(bf16)
    return hi, mid, lo


def _rope_tables(after):
    half = 8
    inv_freq = jnp.power(jnp.float32(ROPE_THETA), -jnp.arange(half, dtype=f32) * 2.0 / 16)
    ang = (jnp.arange(S).astype(f32) + after)[:, None] * inv_freq[None, :]
    cos, sin = jnp.cos(ang), jnp.sin(ang)
    one = jnp.ones((S, HD - 16), f32)
    zero = jnp.zeros((S, HD - 16), f32)
    z8 = jnp.zeros((S, 8), f32)
    c = jnp.concatenate([cos, cos, one], axis=1)
    s1 = jnp.concatenate([-sin, z8, zero], axis=1)
    s2 = jnp.concatenate([z8, sin, zero], axis=1)
    return tuple(jnp.concatenate([t, t], axis=1) for t in (c, s1, s2))


def _mm(order, a, b, mode, out_dtype, tm, tn, name, stack_cols=False):
    if mode == "nn":
        (M, K), (_, N) = a.shape, b.shape
        a_spec = pl.BlockSpec((tm, K), lambda i, j: (i, 0))
        b_spec = pl.BlockSpec((K, tn), lambda i, j: (0, j))
        dot = _dot
    elif mode == "nt":
        (M, K), (N, _) = a.shape, b.shape
        a_spec = pl.BlockSpec((tm, K), lambda i, j: (i, 0))
        b_spec = pl.BlockSpec((tn, K), lambda i, j: (j, 0))
        dot = _dot_nt
    else:
        (K, M), (_, N) = a.shape, b.shape
        a_spec = pl.BlockSpec((K, tm), lambda i, j: (0, i))
        b_spec = pl.BlockSpec((K, tn), lambda i, j: (0, j))
        dot = _dot_tn

    def body(a_ref, b_ref, o_ref):
        o_ref[...] = dot(a_ref[...], b_ref[...]).astype(out_dtype)

    if stack_cols:
        assert tm == M
        out_spec = pl.BlockSpec((None, tm, tn), lambda i, j: (j, 0, 0))
        out_shape = jax.ShapeDtypeStruct((N // tn, M, tn), out_dtype)
    else:
        out_spec = pl.BlockSpec((tm, tn), lambda i, j: (i, j))
        out_shape = jax.ShapeDtypeStruct((M, N), out_dtype)
    return _call(
        order, body, (a, b), name=name, grid=(M // tm, N // tn), in_specs=[a_spec, b_spec],
        out_specs=out_spec, out_shape=out_shape,
        compiler_params=_params(("parallel", "parallel")),
    )


def _assemble_win(order, wins, fas):
    def body(win_ref, fa_ref, o_ref):
        q = pl.program_id(0)

        @pl.when(q == 0)
        def _():
            o_ref[...] = jnp.zeros_like(o_ref)

        for k in range(NCHIP):
            @pl.when(q == k)
            def _(k=k):
                for j in range(WIN_UNITS):
                    cu = WIN_UNIT0[k] + j
                    dst = pl.ds(C2I[cu] * UNIT, UNIT)
                    if cu in OVERLAP_UNITS:
                        o_ref[dst, :] += win_ref[j * UNIT:(j + 1) * UNIT, :]
                    else:
                        o_ref[dst, :] = win_ref[j * UNIT:(j + 1) * UNIT, :]
                if k == 1:
                    o_ref[F_FA:F_FA + FA_ROWS, :] = fa_ref[...]

    return _call(
        order, body, (wins, fas), name="assemble_w_in", grid=(NCHIP,),
        in_specs=[pl.BlockSpec((None, WIN_ROWS, D), lambda q: (q, 0, 0)),
                  pl.BlockSpec((None, FA_ROWS, D), lambda q: (1, 0, 0))],
        out_specs=pl.BlockSpec((NP, D), lambda q: (0, 0)),
        out_shape=jax.ShapeDtypeStruct((NP, D), bf16),
        compiler_params=_params(("arbitrary",)),
    )


def _norm_inproj(order, x, g1, wt, rope):
    tm = 256
    c_t, s1_t, s2_t = rope

    def body(x_ref, g_ref, w_ref, c_ref, s1_ref, s2_ref, h_ref, qkvb_ref, qkva_ref, gates_ref, fa_ref):
        xb = x_ref[...]
        r = lax.rsqrt(jnp.mean(xb * xb, axis=-1, keepdims=True) + EPS)
        h = ((xb * r) * g_ref[...]).astype(bf16)
        h_ref[...] = h
        c, s1, s2 = c_ref[...], s1_ref[...], s2_ref[...]
        for p in range(2):
            pb = _dot_nt(h, w_ref[F_DIL + p * DIL_BLK:F_DIL + (p + 1) * DIL_BLK, :])
            for ch in range(DIL_BLK // 128):
                pc = pb[:, ch * 128:(ch + 1) * 128]
                if ch < 6:
                    pc = pc * c + pltpu.roll(pc, 120, 1) * s1 + pltpu.roll(pc, 8, 1) * s2
                qkvb_ref[:, p * DIL_BLK + ch * 128:p * DIL_BLK + (ch + 1) * 128] = pc
        qkva_ref[...] = _dot_nt(h, w_ref[F_FOX:F_FA, :]).astype(bf16)
        fa_ref[...] = _dot_nt(h, w_ref[F_FA:F_FA + 128, :])
        gates_ref[...] = _dot_nt(h, w_ref[F_G:NP, :]).astype(bf16)

    row = lambda w: pl.BlockSpec((tm, w), lambda i: (i, 0))
    return _call(
        order, body, (x, g1, wt, c_t, s1_t, s2_t), name="norm_inproj", grid=(S // tm,),
        in_specs=[row(D), pl.BlockSpec((1, D), lambda i: (0, 0)), pl.BlockSpec((NP, D), lambda i: (0, 0)),
                  row(128), row(128), row(128)],
        out_specs=[row(D), row(2 * DIL_BLK), row(4 * FOX_BLK), row(2 * D), row(128)],
        out_shape=[jax.ShapeDtypeStruct((S, D), bf16), jax.ShapeDtypeStruct((S, 2 * DIL_BLK), f32),
                   jax.ShapeDtypeStruct((S, 4 * FOX_BLK), bf16), jax.ShapeDtypeStruct((S, 2 * D), bf16),
                   jax.ShapeDtypeStruct((S, 128), f32)],
        compiler_params=_params(("parallel",)),
    )


def _forget_cumsum(order, fa, bpad):
    nb = S // TQ

    def body(fa_ref, b_ref, F_ref):
        rr = lax.broadcasted_iota(jnp.int32, (TQ, TQ), 0)
        cc = lax.broadcasted_iota(jnp.int32, (TQ, TQ), 1)
        tri = (rr >= cc).astype(bf16)
        lane = lax.broadcasted_iota(jnp.int32, (1, 128), 1)
        carry = jnp.zeros((1, 128), f32)
        for b in range(nb):
            z = fa_ref[b * TQ:(b + 1) * TQ, :] + b_ref[...]
            lf = jnp.minimum(z, 0.0) - jnp.log(1.0 + jnp.exp(-jnp.abs(z)))
            lf = jnp.where(lane < 8, lf, 0.0)
            hi, mid, lo = _split3(lf)
            fb = (_dot(tri, hi) + _dot(tri, mid)) + _dot(tri, lo) + carry
            F_ref[b * TQ:(b + 1) * TQ, :] = fb
            carry = fb[TQ - 1:TQ, :]

    return _call(
        order, body, (fa, bpad), name="forget_cumsum",
        out_shape=jax.ShapeDtypeStruct((S, 128), f32),
        compiler_params=_params(),
    )


def _head_masks():
    lane = lax.broadcasted_iota(jnp.int32, (1, 128), 1)
    return lane, (lane < HD, lane >= HD)


L_ONE = 3
FOX_TQ, FOX_TK = 256, 512


def _set_lanes(x, lane, first, cols):
    for n, col in enumerate(cols):
        x = jnp.where(lane == first + n, col, x)
    return x


def _f32_parts(col):
    return [t.astype(f32) for t in _split3(col)]


def _fox_operands(qkv_ref, F_ref, lse_ref, qa, ka, p, rows):
    lane, hm = _head_masks()
    q = qkv_ref[rows, 0:128].astype(f32) * 0.125
    k = qkv_ref[rows, 128:256].astype(f32)
    Fb = F_ref[rows, :]
    for hh in (0, 1):
        free = (1 - hh) * HD
        fcol = jnp.sum(jnp.where(lane == 2 * p + hh, Fb, 0.0), axis=1, keepdims=True)
        qterm = fcol if lse_ref is None else fcol - lse_ref[rows, hh * HD:hh * HD + 1]
        qcols = _f32_parts(qterm) + [1.0] * 3
        kcols = [1.0] * 3 + [-t for t in _f32_parts(fcol)]
        qa[hh, rows, :] = _set_lanes(jnp.where(hm[hh], q, 0.0), lane, free, qcols).astype(bf16)
        ka[hh, rows, :] = _set_lanes(k, lane, free, kcols).astype(bf16)


def _fox_fwd(order, qkva, F):
    tq, tk = FOX_TQ, FOX_TK

    def body(qkv_ref, F_ref, o_ref, lse_ref, qa, ka, vt):
        p = pl.program_id(0)
        keyi = lax.broadcasted_iota(jnp.int32, (tk, 1), 0)
        qryi = lax.broadcasted_iota(jnp.int32, (1, tq), 1)
        sub = lax.broadcasted_iota(jnp.int32, (128, 1), 0)

        def prep(i, c):
            rows = pl.ds(pl.multiple_of(i * tk, tk), tk)
            _fox_operands(qkv_ref, F_ref, None, qa, ka, p, rows)
            vt[i] = qkv_ref[rows, 256:384].astype(f32).T.astype(bf16)
            return c

        lax.fori_loop(0, S // tk, prep, 0)

        def qblock(i, first_half):
            r0 = pl.multiple_of(i * tq, tq)
            qh = [qa[hh, pl.ds(r0, tq), :] for hh in (0, 1)]

            def kv(jb, carry, masked, width):
                keys = pl.ds(pl.multiple_of(jb * tk, tk), width)
                sts = [_dot_nt(ka[hh, keys, :], qh[hh]) for hh in (0, 1)]
                new = []
                for hh in (0, 1):
                    m, l, a = carry[3 * hh:3 * hh + 3]
                    st = sts[hh]
                    if masked:
                        st = jnp.where(jb * tk + keyi[0:width] <= r0 + qryi, st, NEG)
                    mn = jnp.maximum(m, jnp.max(st, axis=0, keepdims=True))
                    al = jnp.exp(m - mn)
                    pt = jnp.exp(st - mn)
                    l = al * l + jnp.sum(pt, axis=0, keepdims=True)
                    a = al * a + _dot(vt[jb, hh * HD:(hh + 1) * HD, 0:width], pt.astype(bf16))
                    new += [mn, l, a]
                return tuple(new)

            init = (jnp.full((1, tq), NEG, f32), jnp.zeros((1, tq), f32), jnp.zeros((HD, tq), f32)) * 2
            last = (r0 + tq - 1) // tk
            carry = lax.fori_loop(0, last, lambda j, cr: kv(j, cr, False, tk), init)
            m0, l0, a0, m1, l1, a1 = kv(last, carry, True, tk // 2 if first_half else tk)
            ot = jnp.concatenate([a0 / l0, a1 / l1], axis=0)
            lt = jnp.where(sub < HD, m0 + jnp.log(l0), m1 + jnp.log(l1))
            o_ref[pl.ds(r0, tq), :] = ot.T.astype(bf16)
            lse_ref[pl.ds(r0, tq), :] = lt.T

        def qpair(t, c):
            qblock(2 * t, True)
            qblock(2 * t + 1, False)
            return c

        assert tk == 2 * tq
        lax.fori_loop(0, S // tk, qpair, 0)

    pair = pl.BlockSpec((S, 128), lambda p: (0, p))
    return _call(
        order, body, (qkva, F), name="fox_fwd", grid=(4,),
        in_specs=[pl.BlockSpec((S, FOX_BLK), lambda p: (0, p)), pl.BlockSpec((S, 128), lambda p: (0, 0))],
        out_specs=[pair, pair],
        out_shape=[jax.ShapeDtypeStruct((S, FOXW), bf16), jax.ShapeDtypeStruct((S, FOXW), f32)],
        scratch_shapes=[pltpu.VMEM((2, S, 128), bf16)] * 2 + [pltpu.VMEM((S // tk, 128, tk), bf16)],
        compiler_params=_params(("parallel",)),
    )


def _permute_in(dst, src, r):
    L = S // r
    for rho in range(r):
        dst[rho * L:(rho + 1) * L, :] = src[pl.ds(rho, L, stride=r), :]


def _permute_out(dst, src, r):
    L = S // r
    for rho in range(r):
        dst[pl.ds(rho, L, stride=r), :] = src[rho * L:(rho + 1) * L, :]


def _band_width(nbl):
    return BAND if nbl == 1 else 2 * BAND


def _band_geometry(bb, nbl):
    r0 = pl.multiple_of(bb * BAND, BAND)
    if nbl == 1:
        k0 = r0
    else:
        k0 = pl.multiple_of(jnp.maximum(bb - 1, 0) * BAND, BAND)
    sub0 = (bb - lax.rem(bb, nbl)) * BAND
    qi = r0 + lax.broadcasted_iota(jnp.int32, (BAND, 1), 0)
    ki = k0 + lax.broadcasted_iota(jnp.int32, (1, _band_width(nbl)), 1)
    diff = qi - ki
    valid = (diff >= 0) & (diff <= BAND) & (ki >= sub0)
    return r0, k0, valid


def _dil_views(ref):
    return [[ref.at[:, pl.ds((3 * role + g) * 128, 128)] for g in range(3)] for role in range(3)]


DIL_UNROLL = 4


def _dil_in_specs():
    return [pl.BlockSpec((S, 128), lambda p, k=k: (0, 9 * p + k)) for k in range(9)]


def _dil_fwd(order, qkvb):
    def body(*refs):
        q_refs, k_refs, v_refs = refs[0:3], refs[3:6], refs[6:9]
        ob_ref, lse_ref, qp, kp, vp, op, lp = refs[9:16]
        on, ln = refs[16:19], refs[19:22]
        _, hm = _head_masks()
        for g, r in enumerate(DIL):
            nbl = S // r // BAND
            if r == 1:
                qs_, ks_, vs_, od, ld = q_refs[g], k_refs[g], v_refs[g], on[g], ln[g]
            else:
                _permute_in(qp, q_refs[g], r)
                _permute_in(kp, k_refs[g], r)
                _permute_in(vp, v_refs[g], r)
                qs_, ks_, vs_, od, ld = qp, kp, vp, op, lp

            def blk(t, c, qs_=qs_, ks_=ks_, vs_=vs_, od=od, ld=ld, nbl=nbl):
                work = []
                for u in range(DIL_UNROLL):
                    r0, k0, valid = _band_geometry(DIL_UNROLL * t + u, nbl)
                    q = qs_[pl.ds(r0, BAND), :] * 0.125
                    kw = ks_[pl.ds(k0, _band_width(nbl)), :].astype(bf16)
                    vw = vs_[pl.ds(k0, _band_width(nbl)), :]
                    for hh in (0, 1):
                        qh = jnp.where(hm[hh], q, 0.0).astype(bf16)
                        work.append((u, hh, r0, valid, vw, _dot_nt(qh, kw)))
                o = [jnp.zeros((BAND, 128), f32)] * DIL_UNROLL
                lse = [jnp.zeros((BAND, 128), f32)] * DIL_UNROLL
                for u, hh, r0, valid, vw, s in work:
                    s = jnp.where(valid, s, NEG)
                    m = jnp.max(s, axis=1, keepdims=True)
                    pr = jnp.exp(s - m)
                    l = jnp.sum(pr, axis=1, keepdims=True)
                    vm = jnp.where(hm[hh], vw, 0.0).astype(bf16)
                    o[u] = o[u] + _dot((pr / l).astype(bf16), vm)
                    lse[u] = jnp.where(hm[hh], m + jnp.log(l), lse[u])
                    if hh == 1:
                        od[pl.ds(r0, BAND), :] = o[u]
                        ld[pl.ds(r0, BAND), :] = lse[u]
                return c

            lax.fori_loop(0, S // BAND // DIL_UNROLL, blk, 0)
            if r != 1:
                _permute_out(on[g], op, r)
                _permute_out(ln[g], lp, r)

        def combine(i, c):
            r0 = pl.multiple_of(i * TQ, TQ)
            ls = [ln[g][pl.ds(r0, TQ), :] for g in range(3)]
            mx = jnp.maximum(jnp.maximum(ls[0], ls[1]), ls[2])
            es = [jnp.exp(l - mx) for l in ls]
            tot = (es[0] + es[1]) + es[2]
            acc = (es[0] / tot) * on[0][pl.ds(r0, TQ), :]
            acc = acc + (es[1] / tot) * on[1][pl.ds(r0, TQ), :]
            acc = acc + (es[2] / tot) * on[2][pl.ds(r0, TQ), :]
            ob_ref[pl.ds(r0, TQ), :] = acc.astype(bf16)
            lse_ref[pl.ds(r0, TQ), :] = mx + jnp.log(tot)
            return c

        lax.fori_loop(0, S // TQ, combine, 0)

    out_blk = pl.BlockSpec((S, 128), lambda p: (0, p))
    return _call(
        order, body, [qkvb] * 9, name="dil_fwd", grid=(2,),
        in_specs=_dil_in_specs(), out_specs=[out_blk, out_blk],
        out_shape=[jax.ShapeDtypeStruct((S, DILOUT), bf16), jax.ShapeDtypeStruct((S, DILOUT), f32)],
        scratch_shapes=[pltpu.VMEM((S, 128), f32)] * 11,
        compiler_params=_params(("parallel",)),
    )


def _branch_mix(order, oa, ob, was, wbs, gates):
    tm = 512

    def body(oa_ref, ob_ref, wa_ref, wb_ref, g_ref, ya_ref, yb_ref, mix_ref):
        oa_b, ob_b = oa_ref[...], ob_ref[...]
        for q in range(NCHIP):
            cols = slice(q * 256, (q + 1) * 256)
            ya = _dot(oa_b, wa_ref[q])
            yb = _dot(ob_b, wb_ref[q])
            ya_ref[:, cols] = ya.astype(bf16)
            yb_ref[:, cols] = yb.astype(bf16)
            ga = g_ref[:, q * 256:(q + 1) * 256].astype(f32)
            gb = g_ref[:, D + q * 256:D + (q + 1) * 256].astype(f32)
            mix_ref[:, cols] = (jax.nn.sigmoid(ga) * ya + jax.nn.sigmoid(gb) * yb).astype(bf16)

    row = lambda w: pl.BlockSpec((tm, w), lambda i: (i, 0))
    full3 = lambda a: pl.BlockSpec(a.shape, lambda i: (0, 0, 0))
    return _call(
        order, body, (oa, ob, was, wbs, gates), name="branch_mix", grid=(S // tm,),
        in_specs=[row(FOXW), row(DILOUT), full3(was), full3(wbs), row(2 * D)],
        out_specs=[row(D), row(D), row(D)],
        out_shape=[jax.ShapeDtypeStruct((S, D), bf16), jax.ShapeDtypeStruct((S, D), bf16),
                   jax.ShapeDtypeStruct((S, D), bf16)],
        compiler_params=_params(("parallel",)),
    )


def _outproj_norm(order, mixed, wout, x, g2):
    tm = 512

    def body(m_ref, w_ref, x_ref, g_ref, x2_ref, h2_ref):
        x2 = x_ref[...] + _dot(m_ref[...], w_ref[...])
        x2_ref[...] = x2
        r = lax.rsqrt(jnp.mean(x2 * x2, axis=-1, keepdims=True) + EPS)
        h2_ref[...] = ((x2 * r) * g_ref[...]).astype(bf16)

    row = pl.BlockSpec((tm, D), lambda i: (i, 0))
    return _call(
        order, body, (mixed, wout, x, g2), name="outproj_norm", grid=(S // tm,),
        in_specs=[row, pl.BlockSpec((D, D), lambda i: (0, 0)), row, pl.BlockSpec((1, D), lambda i: (0, 0))],
        out_specs=[row, row],
        out_shape=[jax.ShapeDtypeStruct((S, D), f32), jax.ShapeDtypeStruct((S, D), bf16)],
        compiler_params=_params(("parallel",)),
    )


def _mlp_up(order, h2, wups):
    tm = 1024

    def body(h_ref, w_ref, ru_ref, a_ref):
        ru = jnp.maximum(_dot(h_ref[...], w_ref[...]), 0.0)
        ru_ref[...] = ru.astype(bf16)
        a_ref[...] = (ru * ru).astype(bf16)

    out = pl.BlockSpec((tm, D), lambda q, i: (i, q))
    return _call(
        order, body, (h2, wups), name="mlp_up", grid=(NCHIP, S // tm),
        in_specs=[pl.BlockSpec((tm, D), lambda q, i: (i, 0)), pl.BlockSpec((None, D, D), lambda q, i: (q, 0, 0))],
        out_specs=[out, out],
        out_shape=[jax.ShapeDtypeStruct((S, DFF), bf16), jax.ShapeDtypeStruct((S, DFF), bf16)],
        compiler_params=_params(("parallel", "parallel")),
    )


def _mlp_down_loss(order, a, wdown, x2, g3, tgt):
    tm = 512

    def body(a_ref, w_ref, x2_ref, g_ref, t_ref, dx_ref, dxb_ref, dg_ref, loss_ref):
        i = pl.program_id(0)
        x3 = x2_ref[...] + _dot(a_ref[...], w_ref[...])
        r = lax.rsqrt(jnp.mean(x3 * x3, axis=-1, keepdims=True) + EPS)
        xh = x3 * r
        g = g_ref[...]
        e = xh * g - t_ref[...]
        part = 0.5 * jnp.sum(jnp.mean(e * e, axis=-1, keepdims=True), axis=0, keepdims=True)
        dy = e * (1.0 / D)
        gdy = dy * g
        dx = r * (gdy - xh * jnp.mean(gdy * xh, axis=-1, keepdims=True))
        dx_ref[...] = dx
        dxb_ref[...] = dx.astype(bf16)

        @pl.when(i == 0)
        def _():
            dg_ref[...] = jnp.zeros_like(dg_ref)
            loss_ref[...] = jnp.zeros_like(loss_ref)

        dg_ref[...] += jnp.sum(dy * xh, axis=0, keepdims=True)
        loss_ref[...] += jnp.broadcast_to(part, (1, 128))

    row = pl.BlockSpec((tm, D), lambda i: (i, 0))
    vec = pl.BlockSpec((1, D), lambda i: (0, 0))
    return _call(
        order, body, (a, wdown, x2, g3, tgt), name="mlp_down_loss", grid=(S // tm,),
        in_specs=[pl.BlockSpec((tm, DFF), lambda i: (i, 0)), pl.BlockSpec((DFF, D), lambda i: (0, 0)), row, vec, row],
        out_specs=[row, row, vec, pl.BlockSpec((1, 128), lambda i: (0, 0))],
        out_shape=[jax.ShapeDtypeStruct((S, D), f32), jax.ShapeDtypeStruct((S, D), bf16),
                   jax.ShapeDtypeStruct((1, D), f32), jax.ShapeDtypeStruct((1, 128), f32)],
        compiler_params=_params(("arbitrary",)),
    )


def _mlp_down_bwd(order, dx3b, wdown, u):
    tm = 512

    def body(d_ref, w_ref, u_ref, du_ref):
        d = d_ref[...]
        for q in range(NCHIP):
            cols = slice(q * D, (q + 1) * D)
            da = _dot_nt(d, w_ref[cols, :])
            du_ref[:, cols] = (da * (2.0 * u_ref[:, cols].astype(f32))).astype(bf16)

    return _call(
        order, body, (dx3b, wdown, u), name="mlp_down_bwd", grid=(S // tm,),
        in_specs=[pl.BlockSpec((tm, D), lambda i: (i, 0)), pl.BlockSpec((DFF, D), lambda i: (0, 0)),
                  pl.BlockSpec((tm, DFF), lambda i: (i, 0))],
        out_specs=pl.BlockSpec((tm, DFF), lambda i: (i, 0)),
        out_shape=jax.ShapeDtypeStruct((S, DFF), bf16),
        compiler_params=_params(("parallel",)),
    )


def _mlp_up_bwd(order, du, wups, x2, dx3, g2):
    tm = 512

    def body(du_ref, w_ref, x2_ref, dx3_ref, g_ref, dx2_ref, dx2b_ref, dg_ref):
        i = pl.program_id(0)
        dh = jnp.zeros((tm, D), f32)
        for q in range(NCHIP):
            dh = dh + _dot_nt(du_ref[:, q * D:(q + 1) * D], w_ref[q])
        x2 = x2_ref[...]
        r = lax.rsqrt(jnp.mean(x2 * x2, axis=-1, keepdims=True) + EPS)
        xh = x2 * r
        gdh = dh * g_ref[...]
        dx2 = dx3_ref[...] + r * (gdh - xh * jnp.mean(gdh * xh, axis=-1, keepdims=True))
        dx2_ref[...] = dx2
        dx2b_ref[...] = dx2.astype(bf16)

        @pl.when(i == 0)
        def _():
            dg_ref[...] = jnp.zeros_like(dg_ref)

        dg_ref[...] += jnp.sum(dh * xh, axis=0, keepdims=True)

    row = pl.BlockSpec((tm, D), lambda i: (i, 0))
    vec = pl.BlockSpec((1, D), lambda i: (0, 0))
    return _call(
        order, body, (du, wups, x2, dx3, g2), name="mlp_up_bwd", grid=(S // tm,),
        in_specs=[pl.BlockSpec((tm, DFF), lambda i: (i, 0)), pl.BlockSpec((NCHIP, D, D), lambda i: (0, 0, 0)),
                  row, row, vec],
        out_specs=[row, row, vec],
        out_shape=[jax.ShapeDtypeStruct((S, D), f32), jax.ShapeDtypeStruct((S, D), bf16),
                   jax.ShapeDtypeStruct((1, D), f32)],
        compiler_params=_params(("arbitrary",)),
    )


def _gate_bwd(order, dx2b, wout, gates, ya, yb):
    tm = 512

    def body(d_ref, w_ref, g_ref, ya_ref, yb_ref, dya_ref, dyb_ref, dproj_ref):
        dm = _dot_nt(d_ref[...], w_ref[...])
        sa = jax.nn.sigmoid(g_ref[:, 0:D].astype(f32))
        sb = jax.nn.sigmoid(g_ref[:, D:2 * D].astype(f32))
        dya_ref[...] = (dm * sa).astype(bf16)
        dyb_ref[...] = (dm * sb).astype(bf16)
        dproj_ref[:, 0:D] = (dm * ya_ref[...].astype(f32) * (sa * (1.0 - sa))).astype(bf16)
        dproj_ref[:, D:2 * D] = (dm * yb_ref[...].astype(f32) * (sb * (1.0 - sb))).astype(bf16)

    row = lambda w: pl.BlockSpec((tm, w), lambda i: (i, 0))
    return _call(
        order, body, (dx2b, wout, gates, ya, yb), name="gate_bwd", grid=(S // tm,),
        in_specs=[row(D), pl.BlockSpec((D, D), lambda i: (0, 0)), row(2 * D), row(D), row(D)],
        out_specs=[row(D), row(D), pl.BlockSpec((tm, 2 * D), lambda i: (i, F_G // (2 * D)))],
        out_shape=[jax.ShapeDtypeStruct((S, D), bf16), jax.ShapeDtypeStruct((S, D), bf16),
                   jax.ShapeDtypeStruct((S, NP), bf16)],
        compiler_params=_params(("parallel",)),
    )


def _branch_bwd(order, dya, dyb, was, wbs):
    tm = 512

    def body(dya_ref, dyb_ref, wa_ref, wb_ref, doa_ref, dob_ref):
        doa = jnp.zeros((tm, FOXW), f32)
        dob = jnp.zeros((tm, DILOUT), f32)
        for q in range(NCHIP):
            cols = slice(q * 256, (q + 1) * 256)
            doa = doa + _dot_nt(dya_ref[:, cols], wa_ref[q])
            dob = dob + _dot_nt(dyb_ref[:, cols], wb_ref[q])
        doa_ref[...] = doa.astype(bf16)
        dob_ref[...] = dob

    row = lambda w: pl.BlockSpec((tm, w), lambda i: (i, 0))
    full3 = lambda a: pl.BlockSpec(a.shape, lambda i: (0, 0, 0))
    return _call(
        order, body, (dya, dyb, was, wbs), name="branch_bwd", grid=(S // tm,),
        in_specs=[row(D), row(D), full3(was), full3(wbs)],
        out_specs=[row(FOXW), row(DILOUT)],
        out_shape=[jax.ShapeDtypeStruct((S, FOXW), bf16), jax.ShapeDtypeStruct((S, DILOUT), f32)],
        compiler_params=_params(("parallel",)),
    )


def _branch_wgrad(order, oa, ob, dya, dyb):
    def body(oa_ref, ob_ref, dya_ref, dyb_ref, dwa_ref, dwb_ref):
        dwa_ref[...] = _dot_tn(oa_ref[...], dya_ref[...])
        dwb_ref[...] = _dot_tn(ob_ref[...], dyb_ref[...])

    full = lambda w: pl.BlockSpec((S, w), lambda q: (0, 0))
    colq = pl.BlockSpec((S, 256), lambda q: (0, q))
    return _call(
        order, body, (oa, ob, dya, dyb), name="branch_wgrad", grid=(NCHIP,),
        in_specs=[full(FOXW), full(DILOUT), colq, colq],
        out_specs=[pl.BlockSpec((None, FOXW, 256), lambda q: (q, 0, 0)),
                   pl.BlockSpec((None, DILOUT, 256), lambda q: (q, 0, 0))],
        out_shape=[jax.ShapeDtypeStruct((NCHIP, FOXW, 256), f32), jax.ShapeDtypeStruct((NCHIP, DILOUT, 256), f32)],
        compiler_params=_params(("parallel",)),
    )


def _fox_bwd(order, qkva, doa, oa, lse, F, dproj):
    tq, tk = FOX_TQ, FOX_TK

    def body(qkv_ref, do_ref, o_ref, lse_ref, F_ref, _dproj_in, dF_ref, dqkv_ref, qa, ka, da, va, kat,
             dk_scr, dv_scr, dqt_scr):
        p = pl.program_id(0)
        lane, hm = _head_masks()
        keyi = lax.broadcasted_iota(jnp.int32, (tk, 1), 0)
        qryi = lax.broadcasted_iota(jnp.int32, (1, tq), 1)

        def prep(i, c):
            rows = pl.ds(pl.multiple_of(i * tk, tk), tk)
            _fox_operands(qkv_ref, F_ref, lse_ref, qa, ka, p, rows)
            do = do_ref[rows, :].astype(f32)
            prod = do * o_ref[rows, :].astype(f32)
            v = qkv_ref[rows, 256:384].astype(f32)
            for hh in (0, 1):
                free = (1 - hh) * HD
                delta = jnp.sum(jnp.where(hm[hh], prod, 0.0), axis=1, keepdims=True)
                da[hh, rows, :] = _set_lanes(jnp.where(hm[hh], do, 0.0), lane, free,
                                             [-t for t in _f32_parts(delta)]).astype(bf16)
                va[hh, rows, :] = _set_lanes(v, lane, free, [1.0] * 3).astype(bf16)
                kat[hh, i] = ka[hh, rows, :].astype(f32).T.astype(bf16)
                dk_scr[hh, rows, :] = jnp.zeros((tk, 128), f32)
                dv_scr[hh, rows, :] = jnp.zeros((tk, 128), f32)
            return c

        lax.fori_loop(0, S // tk, prep, 0)

        def qblock(i, first_half):
            r0 = pl.multiple_of(i * tq, tq)
            qrows = pl.ds(r0, tq)
            qh = [qa[hh, qrows, :] for hh in (0, 1)]
            dh = [da[hh, qrows, :] for hh in (0, 1)]
            dqt_scr[...] = jnp.zeros_like(dqt_scr)

            def kv(jb, c2, masked, width):
                keys = pl.ds(pl.multiple_of(jb * tk, tk), width)
                sts = [_dot_nt(ka[hh, keys, :], qh[hh]) for hh in (0, 1)]
                dps = [_dot_nt(va[hh, keys, :], dh[hh]) for hh in (0, 1)]
                for hh in (0, 1):
                    pt = jnp.exp(sts[hh])
                    if masked:
                        pt = jnp.where(jb * tk + keyi[0:width] <= r0 + qryi, pt, 0.0)
                    dsb = (pt * dps[hh]).astype(bf16)
                    dv_scr[hh, keys, :] += _dot(pt.astype(bf16), dh[hh])
                    dk_scr[hh, keys, :] += _dot(dsb, qh[hh])
                    dqt_scr[hh] += _dot(kat[hh, jb, :, 0:width], dsb)
                return c2

            last = (r0 + tq - 1) // tk
            lax.fori_loop(0, last, lambda j, c2: kv(j, c2, False, tk), 0)
            kv(last, 0, True, tk // 2 if first_half else tk)
            dq0, dq1 = dqt_scr[0].T, dqt_scr[1].T
            dqkv_ref[qrows, 0:128] = (jnp.where(hm[0], dq0, dq1) * 0.125).astype(bf16)
            dF_ref[qrows, :] = jnp.where(lane == 0, dq0[:, HD:HD + 1], jnp.where(lane == 1, dq1[:, 0:1], 0.0))

        def qpair(t, c):
            qblock(2 * t, True)
            qblock(2 * t + 1, False)
            return c

        assert tk == 2 * tq
        lax.fori_loop(0, S // tk, qpair, 0)

        def finish(i, c):
            rows = pl.ds(pl.multiple_of(i * tq, tq), tq)
            dk0, dk1 = dk_scr[0, rows, :], dk_scr[1, rows, :]
            dqkv_ref[rows, 128:256] = jnp.where(hm[0], dk0, dk1).astype(bf16)
            dqkv_ref[rows, 256:384] = jnp.where(hm[0], dv_scr[0, rows, :], dv_scr[1, rows, :]).astype(bf16)
            cs = jnp.where(lane == 0, dk0[:, HD + L_ONE:HD + L_ONE + 1],
                           jnp.where(lane == 1, dk1[:, L_ONE:L_ONE + 1], 0.0))
            dF_ref[rows, :] = dF_ref[rows, :] - cs
            return c

        lax.fori_loop(0, S // tq, finish, 0)

    pair = pl.BlockSpec((S, 128), lambda p: (0, p))
    return _call(
        order, body, (qkva, doa, oa, lse, F, dproj), name="fox_bwd", grid=(4,),
        in_specs=[pl.BlockSpec((S, FOX_BLK), lambda p: (0, p)), pair, pair, pair,
                  pl.BlockSpec((S, 128), lambda p: (0, 0)), pl.BlockSpec(memory_space=pl.ANY)],
        out_specs=[pair, pl.BlockSpec((S, FOX_BLK), lambda p: (0, F_FOX // FOX_BLK + p))],
        out_shape=[jax.ShapeDtypeStruct((S, FOXW), f32), jax.ShapeDtypeStruct((S, NP), bf16)],
        input_output_aliases={5: 1},
        scratch_shapes=[pltpu.VMEM((2, S, 128), bf16)] * 4 + [pltpu.VMEM((2, S // tk, 128, tk), bf16)]
        + [pltpu.VMEM((2, S, 128), f32)] * 2 + [pltpu.VMEM((2, 128, tq), f32)],
        compiler_params=_params(("parallel",)),
    )


def _forget_bwd(order, dF, fa, bpad, dproj):
    nb = S // TQ

    def body(dF_ref, fa_ref, b_ref, _dproj_in, db_ref, dfa_ref):
        rr = lax.broadcasted_iota(jnp.int32, (TQ, TQ), 0)
        cc = lax.broadcasted_iota(jnp.int32, (TQ, TQ), 1)
        upper = (cc >= rr).astype(bf16)
        lane = lax.broadcasted_iota(jnp.int32, (1, 128), 1)
        carry = jnp.zeros((1, 128), f32)
        db = jnp.zeros((1, 128), f32)
        for b in reversed(range(nb)):
            cols = jnp.zeros((TQ, 128), f32)
            for h in range(8):
                c0 = (h // 2) * 128 + h % 2
                cols = jnp.where(lane == h, dF_ref[b * TQ:(b + 1) * TQ, c0:c0 + 1], cols)
            dlf = carry
            for part in _split3(cols):
                dlf = dlf + _dot(upper, part)
            carry = carry + jnp.sum(cols, axis=0, keepdims=True)
            z = fa_ref[b * TQ:(b + 1) * TQ, :] + b_ref[...]
            dz = jnp.where(lane < 8, dlf * jax.nn.sigmoid(-z), 0.0)
            dfa_ref[b * TQ:(b + 1) * TQ, 0:128] = dz.astype(bf16)
            dfa_ref[b * TQ:(b + 1) * TQ, 128:256] = jnp.zeros((TQ, 128), bf16)
            db = db + jnp.sum(dz, axis=0, keepdims=True)
        db_ref[...] = db

    whole = lambda a: pl.BlockSpec(a.shape, lambda i: (0,) * a.ndim)
    return _call(
        order, body, (dF, fa, bpad, dproj), name="forget_bwd", grid=(1,),
        in_specs=[whole(dF), whole(fa), whole(bpad), pl.BlockSpec(memory_space=pl.ANY)],
        out_specs=[pl.BlockSpec((1, 128), lambda i: (0, 0)), pl.BlockSpec((S, 256), lambda i: (0, F_FA // 256))],
        out_shape=[jax.ShapeDtypeStruct((1, 128), f32), jax.ShapeDtypeStruct((S, NP), bf16)],
        input_output_aliases={3: 1},
        compiler_params=_params(("arbitrary",)),
    )


def _dil_bwd(order, qkvb, dob, ob, lseb, rope, dproj):
    c_t, s1_t, s2_t = rope

    def body(*refs):
        q_refs, k_refs, v_refs = refs[0:3], refs[3:6], refs[6:9]
        dob_ref, ob_ref, lse_ref, c_ref, s1_ref, s2_ref, _dproj_in, dqkv_ref = refs[9:17]
        qp, kp, vp, dop, lp, dlp, dln, dqp, dkp, dvp, nat = refs[17:28]
        dq_out, dk_out, dv_out = _dil_views(dqkv_ref)
        _, hm = _head_masks()

        def delta_rows(i, c):
            r0 = pl.multiple_of(i * TQ, TQ)
            prod = dob_ref[pl.ds(r0, TQ), :] * ob_ref[pl.ds(r0, TQ), :].astype(f32)
            d0 = jnp.sum(jnp.where(hm[0], prod, 0.0), axis=1, keepdims=True)
            d1 = jnp.sum(jnp.where(hm[1], prod, 0.0), axis=1, keepdims=True)
            dln[pl.ds(r0, TQ), :] = jnp.where(hm[0], d0, d1)
            return c

        lax.fori_loop(0, S // TQ, delta_rows, 0)

        for g, r in enumerate(DIL):
            nbl = S // r // BAND
            if r == 1:
                srcs = (q_refs[g], k_refs[g], v_refs[g], dob_ref, lse_ref, dln)
            else:
                for dst, src in ((qp, q_refs[g]), (kp, k_refs[g]), (vp, v_refs[g]), (dop, dob_ref),
                                 (lp, lse_ref), (dlp, dln)):
                    _permute_in(dst, src, r)
                srcs = (qp, kp, vp, dop, lp, dlp)
            dkp[...] = jnp.zeros_like(dkp)
            dvp[...] = jnp.zeros_like(dvp)

            def blk(t, c, srcs=srcs, nbl=nbl):
                qs_, ks_, vs_, dos_, ls_, dls_ = srcs
                work = []
                for u in range(DIL_UNROLL):
                    r0, k0, valid = _band_geometry(DIL_UNROLL * t + u, nbl)
                    q = qs_[pl.ds(r0, BAND), :] * 0.125
                    kwf = ks_[pl.ds(k0, _band_width(nbl)), :]
                    kw = kwf.astype(bf16)
                    vw = vs_[pl.ds(k0, _band_width(nbl)), :].astype(bf16)
                    do = dos_[pl.ds(r0, BAND), :]
                    lse = ls_[pl.ds(r0, BAND), :]
                    dlt = dls_[pl.ds(r0, BAND), :]
                    for hh in (0, 1):
                        qh = jnp.where(hm[hh], q, 0.0).astype(bf16)
                        doh = jnp.where(hm[hh], do, 0.0).astype(bf16)
                        kh = jnp.where(hm[hh], kwf, 0.0).astype(bf16)
                        work.append((u, hh, r0, k0, valid, qh, doh, kh, lse[:, hh * HD:hh * HD + 1],
                                     dlt[:, hh * HD:hh * HD + 1], _dot_nt(qh, kw), _dot_nt(doh, vw)))
                for u, hh, r0, k0, valid, qh, doh, kh, lse_h, dlt_h, s, dp in work:
                    if hh == 0:
                        dq = jnp.zeros((BAND, 128), f32)
                        dk = jnp.zeros((_band_width(nbl), 128), f32)
                        dv = jnp.zeros((_band_width(nbl), 128), f32)
                    pr = jnp.where(valid, jnp.exp(s - lse_h), 0.0)
                    dsb = (pr * (dp - dlt_h)).astype(bf16)
                    dv = dv + _dot_tn(pr.astype(bf16), doh)
                    dk = dk + _dot_tn(dsb, qh)
                    dq = dq + _dot(dsb, kh)
                    if hh == 1:
                        dqp[pl.ds(r0, BAND), :] = dq * 0.125
                        dkp[pl.ds(k0, _band_width(nbl)), :] += dk
                        dvp[pl.ds(k0, _band_width(nbl)), :] += dv
                return c

            lax.fori_loop(0, S // BAND // DIL_UNROLL, blk, 0)

            for acc, out, roped in ((dqp, dq_out[g], True), (dkp, dk_out[g], True), (dvp, dv_out[g], False)):
                if r == 1:
                    src = acc
                else:
                    _permute_out(nat, acc, r)
                    src = nat

                def emit(i, c, src=src, out=out, roped=roped):
                    r0 = pl.multiple_of(i * TQ, TQ)
                    d = src[pl.ds(r0, TQ), :]
                    if roped:
                        d = (d * c_ref[pl.ds(r0, TQ), :] + pltpu.roll(d * s1_ref[pl.ds(r0, TQ), :], 8, 1)
                             + pltpu.roll(d * s2_ref[pl.ds(r0, TQ), :], 120, 1))
                    out[pl.ds(r0, TQ), :] = d.astype(bf16)
                    return c

                lax.fori_loop(0, S // TQ, emit, 0)

    pair = pl.BlockSpec((S, 128), lambda p: (0, p))
    tab = pl.BlockSpec((S, 128), lambda p: (0, 0))
    blk_spec = pl.BlockSpec((S, DIL_BLK), lambda p: (0, p))
    return _call(
        order, body, [qkvb] * 9 + [dob, ob, lseb, c_t, s1_t, s2_t, dproj], name="dil_bwd", grid=(2,),
        in_specs=_dil_in_specs() + [pair, pair, pair, tab, tab, tab, pl.BlockSpec(memory_space=pl.ANY)],
        out_specs=blk_spec,
        out_shape=jax.ShapeDtypeStruct((S, NP), bf16),
        input_output_aliases={15: 0},
        scratch_shapes=[pltpu.VMEM((S, 128), f32)] * 11,
        compiler_params=_params(("parallel",)),
    )


def _inproj_bwd(order, dproj, wt, x, dx2, g1):
    tm = 256

    def body(d_ref, w_ref, x_ref, dx2_ref, g_ref, dx_ref, dg_ref):
        i = pl.program_id(0)
        dh = _dot(d_ref[...], w_ref[...])
        xb = x_ref[...]
        r = lax.rsqrt(jnp.mean(xb * xb, axis=-1, keepdims=True) + EPS)
        xh = xb * r
        gdh = dh * g_ref[...]
        dx_ref[...] = dx2_ref[...] + r * (gdh - xh * jnp.mean(gdh * xh, axis=-1, keepdims=True))

        @pl.when(i == 0)
        def _():
            dg_ref[...] = jnp.zeros_like(dg_ref)

        dg_ref[...] += jnp.sum(dh * xh, axis=0, keepdims=True)

    row = pl.BlockSpec((tm, D), lambda i: (i, 0))
    vec = pl.BlockSpec((1, D), lambda i: (0, 0))
    return _call(
        order, body, (dproj, wt, x, dx2, g1), name="inproj_bwd", grid=(S // tm,),
        in_specs=[pl.BlockSpec((tm, NP), lambda i: (i, 0)), pl.BlockSpec((NP, D), lambda i: (0, 0)), row, row, vec],
        out_specs=[row, vec],
        out_shape=[jax.ShapeDtypeStruct((S, D), f32), jax.ShapeDtypeStruct((1, D), f32)],
        compiler_params=_params(("arbitrary",)),
    )


HBM = pl.BlockSpec(memory_space=pltpu.HBM)
SEM = pl.BlockSpec(memory_space=pltpu.SEMAPHORE)
SMALL_ROWS = 8


def _comm_call(name, body, bufs, order, sems_in=(), new_sems=(), behind=()):
    nb, ns, nn = len(bufs), len(sems_in), len(new_sems)
    extra = order.token_for(bufs) + list(behind)

    def kern(*refs):
        off = nb + ns + len(extra)
        body(refs[:nb], refs[nb:nb + ns], refs[off:off + nn])
        refs[-1][...] = jnp.zeros((8, 128), f32)

    res = pl.pallas_call(
        kern, name=name,
        in_specs=[HBM] * nb + [SEM] * ns + [pl.BlockSpec(memory_space=pl.ANY)] * len(extra),
        out_specs=[SEM] * nn + [HBM] * nb + [pl.BlockSpec(memory_space=pltpu.VMEM)],
        out_shape=[pltpu.SemaphoreType.DMA((k,)) for k in new_sems] + [pltpu.HBM(b.shape, b.dtype) for b in bufs]
        + [jax.ShapeDtypeStruct((8, 128), f32)],
        input_output_aliases={i: nn + i for i in range(nb)},
        compiler_params=pltpu.CompilerParams(has_side_effects=pltpu.SideEffectType.DATAFLOW_SIDE_EFFECTING),
    )(*[pltpu.with_memory_space_constraint(b, pltpu.HBM) for b in bufs], *sems_in, *extra)
    order.mark(res[-1])
    return list(res[:nn]), list(res[nn:nn + nb])


def _place():
    x, y, c = lax.axis_index("x"), lax.axis_index("y"), lax.axis_index("c")
    chips = [(1 - x, y), (x, 1 - y), (1 - x, 1 - y)]
    return x, y, c, chips


def _rcopy(src, dst, ssem, rsem, dev):
    return pltpu.make_async_remote_copy(src_ref=src, dst_ref=dst, send_sem=ssem, recv_sem=rsem,
                                        device_id=dev, device_id_type=pl.DeviceIdType.MESH)


def _half(nrows, which):
    return pl.ds(which * (nrows // 2), nrows // 2)


def _ici_copies(stack, group_sizes, ssems, rsems):
    x, y, c, chips = _place()
    me_q = 2 * x + y
    sends, recvs = [], []
    a = 0
    for grp, size in enumerate(group_sizes):
        for k in range(size):
            rows = _half(stack[a].shape[1], c)
            for j, (cx, cy) in enumerate(chips):
                mine = stack[a].at[me_q, rows]
                sends.append(_rcopy(mine, mine, ssems[grp].at[k * 3 + j], rsems[grp].at[k * 3 + j], (cx, cy, c)))
                theirs = stack[a].at[2 * cx + cy, rows]
                recvs.append(_rcopy(theirs, theirs, ssems[grp].at[k * 3 + j], rsems[grp].at[k * 3 + j],
                                    (cx, cy, c)))
            a += 1
    return sends, recvs


def _allgather_start(name, stacks, order):
    n = len(stacks)

    def body(bufs, _, new):
        sends, _r = _ici_copies(bufs, [n], [new[0]], [new[1]])
        for cp in sends:
            cp.start()

    return _comm_call(name, body, stacks, order, new_sems=(3 * n, 3 * n))


def _forward_copies(stack, ssem, rsem):
    x, y, c, chips = _place()
    sib = (x, y, 1 - c)
    sends, recvs = [], []
    for a in range(len(stack)):
        for j, (cx, cy) in enumerate(chips):
            landed = stack[a].at[2 * cx + cy, _half(stack[a].shape[1], c)]
            sends.append(_rcopy(landed, landed, ssem.at[a * 3 + j], rsem.at[a * 3 + j], sib))
            other = stack[a].at[2 * cx + cy, _half(stack[a].shape[1], 1 - c)]
            recvs.append(_rcopy(other, other, ssem.at[a * 3 + j], rsem.at[a * 3 + j], sib))
    return sends, recvs


def _allgather_forward(name, stacks, sems, order, behind=()):
    n = len(stacks)

    def body(bufs, taken, new):
        sends, recvs = _ici_copies(bufs, [n], [taken[0]], [taken[1]])
        fwd, _r = _forward_copies(bufs, new[0], new[1])
        for arrived, onward in zip(recvs, fwd):
            arrived.wait_recv()
            onward.start()
        for cp in sends:
            cp.wait_send()

    return _comm_call(name, body, stacks, order, sems_in=sems, new_sems=(3 * n, 3 * n), behind=behind)


def _allgather_finish(name, stacks, sems, order):
    def body(bufs, taken, _):
        sends, recvs = _forward_copies(bufs, taken[0], taken[1])
        for cp in sends:
            cp.wait_send()
        for cp in recvs:
            cp.wait_recv()

    return _comm_call(name, body, stacks, order, sems_in=sems)[1]


def _window_unit(q, j):
    return C2I[WIN_UNIT0[q] + j]


def _pair_copies(g, t, ssem, rsem, gathered):
    x, y, c, _ = _place()
    sib = (x, y, 1 - c)
    cps, whole = [], []
    for a in range(len(g)):
        if a == 0 and gathered:
            for q in range(NCHIP):
                for j in range(WIN_UNITS // 2):
                    u = jnp.where(c == 0, _window_unit(q, WIN_UNITS // 2 + j), _window_unit(q, j))
                    src = g[0].at[pl.ds(pl.multiple_of(u * UNIT, UNIT), UNIT), :]
                    cps.append(_rcopy(src, t[0].at[q, pl.ds(j * UNIT, UNIT), :], ssem.at[0], rsem.at[0], sib))
            whole.append(_rcopy(t[0], t[0], ssem.at[0], rsem.at[0], sib))
        else:
            cp = _rcopy(g[a].at[:, _half(g[a].shape[1], 1 - c), :], t[a], ssem.at[a], rsem.at[a], sib)
            cps.append(cp)
            whole.append(cp)
    return cps, whole


def _comm_multi(name, parts, order):
    def body(buf_refs, taken, new):
        ib = it = inew = 0
        for pbody, pbufs, psems, pnew, _ in parts:
            pbody(buf_refs[ib:ib + len(pbufs)], taken[it:it + len(psems)], new[inew:inew + len(pnew)])
            ib, it, inew = ib + len(pbufs), it + len(psems), inew + len(pnew)

    sems, bufs = _comm_call(name, body, [b for p in parts for b in p[1]], order,
                            sems_in=[s for p in parts for s in p[2]], new_sems=[k for p in parts for k in p[3]])
    out, ib, inew = [], 0, 0
    for _, pbufs, _, pnew, unpack in parts:
        out.append(unpack(sems[inew:inew + len(pnew)], bufs[ib:ib + len(pbufs)]))
        ib, inew = ib + len(pbufs), inew + len(pnew)
    return out


def _pair_start_part(gs, gathered=False):
    n = len(gs)
    ts = [lax.empty((NCHIP, WIN_ROWS // 2, D) if (a == 0 and gathered) else (NCHIP, g.shape[1] // 2, g.shape[2]), f32)
          for a, g in enumerate(gs)]

    def body(bufs, _, new):
        for cp in _pair_copies(bufs[:n], bufs[n:], new[0], new[1], gathered)[0]:
            cp.start()

    return body, list(gs) + ts, (), (n, n), lambda sems, bufs: (sems, bufs)


def _pair_wait_part(bufs, sems, gathered=False):
    n = len(bufs) // 2

    def body(refs, taken, _):
        for cp in _pair_copies(refs[:n], refs[n:], taken[0], taken[1], gathered)[1]:
            cp.wait_send()
            cp.wait_recv()

    return body, list(bufs), list(sems), (), lambda _, out: (out[:n], out[n:])


ROW_TILE = 256


def _pair_add(order, gs, ts, q_arr, c_arr, name):
    n = len(gs)
    hs = [g.shape[1] // 2 for g in gs]
    nblk = max(1, max(hs) // ROW_TILE)
    assert all(h % (16 * nblk) == 0 for h in hs)

    def body(q_ref, c_ref, *refs):
        for a in range(n):
            s = refs[a][...] + refs[n + a][...]
            refs[3 * n + a][...] = s.astype(bf16)

            @pl.when(pl.program_id(1) == q_ref[0])
            def _():
                refs[2 * n + a][...] = s

    def blk(a, half):
        return pl.BlockSpec((None, hs[a] // nblk, gs[a].shape[2]),
                            lambda i, q, q_ref, c_ref: (q, (c_ref[0] * nblk if half else 0) + i, 0))

    out = _call_indexed(
        order, body, (q_arr, c_arr), list(gs) + list(ts), (nblk, NCHIP),
        [blk(a, True) for a in range(n)] + [blk(a, False) for a in range(n)],
        [pl.BlockSpec((hs[a] // nblk, gs[a].shape[2]), lambda i, q, q_ref, c_ref: (i, 0)) for a in range(n)]
        + [blk(a, False) for a in range(n)],
        name=name,
        out_shape=[jax.ShapeDtypeStruct((hs[a], gs[a].shape[2]), f32) for a in range(n)]
        + [jax.ShapeDtypeStruct((NCHIP, hs[a], gs[a].shape[2]), bf16) for a in range(n)],
        compiler_params=_params(("parallel", "arbitrary")),
    )
    return out[:n], out[n:]


def _pair_add_gathered(order, dwt, t, q_arr, c_arr, name):
    half_units, half_rows = WIN_UNITS // 2, WIN_ROWS // 2
    table = jnp.asarray([_window_unit(q, j) for q in range(NCHIP) for j in range(WIN_UNITS)], jnp.int32)

    def body(tab_ref, q_ref, c_ref, g_hbm, t_ref, own_ref, p16_ref, buf, sem):
        q = pl.program_id(0)

        def gather(w, slot):
            cps = []
            for j in range(half_units):
                u = tab_ref[w * WIN_UNITS + c_ref[0] * half_units + j]
                cps.append(pltpu.make_async_copy(g_hbm.at[pl.ds(pl.multiple_of(u * UNIT, UNIT), UNIT), :],
                                                 buf.at[slot, pl.ds(j * UNIT, UNIT), :], sem.at[slot]))
            return cps

        @pl.when(q == 0)
        def _():
            for cp in gather(0, 0):
                cp.start()

        @pl.when(q + 1 < NCHIP)
        def _():
            for cp in gather(q + 1, (q + 1) % 2):
                cp.start()

        slot = q % 2
        pltpu.make_async_copy(buf.at[slot], buf.at[slot], sem.at[slot]).wait()
        s = buf[slot] + t_ref[...]
        p16_ref[...] = s.astype(bf16)

        @pl.when(q == q_ref[0])
        def _():
            own_ref[...] = s

    blk = pl.BlockSpec((None, half_rows, D), lambda q, tab_ref, q_ref, c_ref: (q, 0, 0))
    return _call_indexed(
        order, body, (table, q_arr, c_arr), (dwt, t), (NCHIP,),
        [pl.BlockSpec(memory_space=pl.ANY), blk],
        [pl.BlockSpec((half_rows, D), lambda q, tab_ref, q_ref, c_ref: (0, 0)), blk],
        scratch_shapes=[pltpu.VMEM((2, half_rows, D), f32), pltpu.SemaphoreType.DMA((2,))],
        name=name,
        out_shape=[jax.ShapeDtypeStruct((half_rows, D), f32),
                   jax.ShapeDtypeStruct((NCHIP, half_rows, D), bf16)],
        compiler_params=_params(("arbitrary",)),
    )


def _shard_copies(p, r, sm, ssem, rsem):
    x, y, c, chips = _place()
    n = len(p)
    sends, recvs = [], []
    for a in range(n):
        for j, (cx, cy) in enumerate(chips):
            k = a * 3 + j
            sends.append(_rcopy(p[a].at[2 * cx + cy], r[a].at[j], ssem.at[k], rsem.at[k], (cx, cy, c)))
            recvs.append(_rcopy(r[a].at[j], r[a].at[j], ssem.at[k], rsem.at[k], (cx, cy, c)))
    if sm is not None:
        mine = sm.at[4 * x + 2 * y + c]
        for i in range(1, 8):
            px = (1 - x) if i & 4 else x
            py = (1 - y) if i & 2 else y
            pc = (1 - c) if i & 1 else c
            k = 3 * n + i - 1
            sends.append(_rcopy(mine, mine, ssem.at[k], rsem.at[k], (px, py, pc)))
            slot = sm.at[4 * px + 2 * py + pc]
            recvs.append(_rcopy(slot, slot, ssem.at[k], rsem.at[k], (px, py, pc)))
    return sends, recvs


def _shard_start_part(p16s, sm=None):
    n = len(p16s)
    rs = [lax.empty((3,) + p.shape[1:], bf16) for p in p16s]
    extra = [] if sm is None else [sm]
    nsem = 3 * n + (7 if sm is not None else 0)

    def body(bufs, _, new):
        sends, _r = _shard_copies(bufs[:n], bufs[n:2 * n], bufs[2 * n] if extra else None, new[0], new[1])
        for cp in sends:
            cp.start()

    return body, list(p16s) + rs + extra, (), (nsem, nsem), lambda sems, bufs: (sems, bufs)


def _shard_wait_part(bufs, sems, n):
    has_sm = len(bufs) > 2 * n

    def body(refs, taken, _):
        sends, recvs = _shard_copies(refs[:n], refs[n:2 * n], refs[2 * n] if has_sm else None, taken[0], taken[1])
        for cp in sends:
            cp.wait_send()
        for cp in recvs:
            cp.wait_recv()

    return body, list(bufs), list(sems), (), lambda _, out: (out[n:2 * n], (out[2 * n] if has_sm else None))


def _shard_sum(order, owns, rs, c_arr, name):
    n = len(owns)
    hs = [o.shape[0] for o in owns]
    nblk = max(1, max(hs) // ROW_TILE)
    assert all(h % (16 * nblk) == 0 for h in hs)
    trs = [h // nblk for h in hs]

    def body(c_ref, *refs):
        for a in range(n):
            s = refs[a][...]
            for j in range(3):
                s = s + refs[n + a][j].astype(f32)
            refs[2 * n + a][...] = s

    out = _call_indexed(
        order, body, (c_arr,), list(owns) + list(rs), (nblk,),
        [pl.BlockSpec((trs[a], owns[a].shape[1]), lambda i, c_ref: (i, 0)) for a in range(n)]
        + [pl.BlockSpec((3, trs[a], owns[a].shape[1]), lambda i, c_ref: (0, i, 0)) for a in range(n)],
        [pl.BlockSpec((trs[a], owns[a].shape[1]), lambda i, c_ref: (c_ref[0] * nblk + i, 0)) for a in range(n)],
        name=name, out_shape=[jax.ShapeDtypeStruct((2 * hs[a], owns[a].shape[1]), f32) for a in range(n)],
        compiler_params=_params(("parallel",)),
    )
    return list(out)


def _swap_copies(full, ssem, rsem):
    x, y, c, _ = _place()
    sends, recvs = [], []
    for a in range(len(full)):
        mine = full[a].at[_half(full[a].shape[0], c)]
        sends.append(_rcopy(mine, mine, ssem.at[a], rsem.at[a], (x, y, 1 - c)))
        other = full[a].at[_half(full[a].shape[0], 1 - c)]
        recvs.append(_rcopy(other, other, ssem.at[a], rsem.at[a], (x, y, 1 - c)))
    return sends, recvs


def _swap_start_part(fulls):
    n = len(fulls)

    def body(bufs, _, new):
        for cp in _swap_copies(bufs, new[0], new[1])[0]:
            cp.start()

    return body, list(fulls), (), (n, n), lambda sems, bufs: (sems, bufs)


def _swap_wait_part(fulls, sems):
    def body(refs, taken, _):
        sends, recvs = _swap_copies(refs, taken[0], taken[1])
        for cp in sends:
            cp.wait_send()
        for cp in recvs:
            cp.wait_recv()

    return body, list(fulls), list(sems), (), lambda _, out: out


def _small_finish(order, sm, ws, ms, vs):
    n = len(ws)

    def body(sm_ref, *refs):
        s = sm_ref[0]
        for d in range(1, 8):
            s = s + sm_ref[d]
        loss_ref, g_refs, upd_refs = refs[3 * n], refs[3 * n + 1:4 * n + 1], refs[4 * n + 1:]
        loss_ref[...] = s[n:n + 1, 0:1]
        for i in range(n):
            g = s[i:i + 1, 0:ws[i].shape[1]]
            g_refs[i][...] = g
            res = _adamw_math(refs[i][...], g, refs[n + i][...], refs[2 * n + i][...])
            for k in range(3):
                upd_refs[3 * i + k][...] = res[k]

    out = _call(order, body, [sm] + list(ws) + list(ms) + list(vs), name="small_sum_adamw",
                out_shape=[jax.ShapeDtypeStruct((1, 1), f32)] + [jax.ShapeDtypeStruct(w.shape, f32) for w in ws]
                + [jax.ShapeDtypeStruct(w.shape, f32) for w in ws for _ in range(3)])
    return out[0], out[1:n + 1], [out[n + 1 + 3 * i:n + 4 + 3 * i] for i in range(n)]


def _adamw_math(w, g, m, v):
    m = ADAM_B1 * m + (1.0 - ADAM_B1) * g
    v = ADAM_B2 * v + (1.0 - ADAM_B2) * (g * g)
    m_hat = m / (1.0 - ADAM_B1 ** ADAM_STEP)
    v_hat = v / (1.0 - ADAM_B2 ** ADAM_STEP)
    return -ADAM_LR * (m_hat / (jnp.sqrt(v_hat) + ADAM_EPS) + ADAM_WD * w), m, v


def _adamw(order, ws, gs, ms, vs, name):
    n = len(ws)
    nblk = max(1, max(w.shape[0] for w in ws) // ROW_TILE)
    assert all(w.shape[0] % (8 * nblk) == 0 for w in ws)

    def body(*refs):
        for a in range(n):
            w_ref, g_ref, m_ref, v_ref = (refs[k * n + a] for k in range(4))
            d_ref, nm_ref, nv_ref, g_out = refs[4 * n + 4 * a:4 * n + 4 * a + 4]
            g = g_ref[...]
            g_out[...] = g
            d_ref[...], nm_ref[...], nv_ref[...] = _adamw_math(w_ref[...], g, m_ref[...], v_ref[...])

    blks = [pl.BlockSpec((w.shape[0] // nblk, w.shape[1]), lambda i: (i, 0)) for w in ws]
    out = _call(
        order, body, list(ws) + list(gs) + list(ms) + list(vs), name=name, grid=(nblk,), in_specs=blks * 4,
        out_specs=[b for b in blks for _ in range(4)],
        out_shape=[jax.ShapeDtypeStruct(w.shape, f32) for w in ws for _ in range(4)],
        compiler_params=_params(("parallel",)),
    )
    return [out[4 * a:4 * a + 4] for a in range(n)]


def _feature_rows(w):
    return jnp.transpose(w, (2, 0, 1))


WIN_STEP = 128
WIN_PIECE = 3 * WIN_STEP


def _window_stacks(order, w, q_arr):
    n_piece = -(-SHARD_IN // WIN_PIECE)
    steps = WIN_ROWS // WIN_STEP
    head, tail = UNIT, WIN_STEP
    assert max(OWN_ROW0) < head and OWN_ROW0[1] + FA_AT == UNIT and WIN_ROWS - tail <= SHARD_IN - N_FA
    rest = SHARD_IN - (n_piece - 1) * WIN_PIECE

    def body(q_ref, w_ref, win_ref, fa_ref, buf, fabuf, sem):
        i = pl.program_id(0)
        q = q_ref[0]
        chip1 = q == 1
        row0 = jnp.where(q == 0, OWN_ROW0[0], jnp.where(chip1, OWN_ROW0[1], jnp.where(q == 2, OWN_ROW0[2], OWN_ROW0[3])))
        skip = jnp.where(chip1, N_FA, 0)

        def copy(src0, dst0, n, slot):
            return pltpu.make_async_copy(w_ref.at[pl.ds(src0, n)], buf.at[pl.ds(dst0, n)], sem.at[slot])

        def piece(j, on_chip1):
            if j == 0 and on_chip1:
                return [copy(0, OWN_ROW0[1], FA_AT, 0),
                        copy(FA_AT + N_FA, UNIT, WIN_PIECE - FA_AT, n_piece)]
            if j == 0:
                return [copy(0, row0, WIN_PIECE, 0)]
            if j == n_piece - 1:
                n = rest - (N_FA if on_chip1 else 0)
                return [copy(SHARD_IN - n, row0 + SHARD_IN - skip - n, n, j)]
            return [copy(j * WIN_PIECE + skip, row0 + j * WIN_PIECE, WIN_PIECE, j)]

        def both(j, act):
            if 0 < j < n_piece - 1:
                for c in piece(j, False):
                    act(c)
                return
            for on_chip1 in (False, True):
                @pl.when(chip1 if on_chip1 else jnp.logical_not(chip1))
                def _():
                    for c in piece(j, on_chip1):
                        act(c)

        fa_copy = pltpu.make_async_copy(w_ref.at[pl.ds(FA_AT, N_FA)], fabuf.at[pl.ds(0, N_FA)], sem.at[n_piece + 1])

        @pl.when(i == 0)
        def _():
            buf[pl.ds(0, head)] = jnp.zeros((head, 1, D), f32)
            buf[pl.ds(WIN_ROWS - tail, tail)] = jnp.zeros((tail, 1, D), f32)
            fabuf[pl.ds(N_FA, FA_ROWS - N_FA)] = jnp.zeros((FA_ROWS - N_FA, 1, D), f32)
            fa_copy.start()
            for j in range(n_piece):
                both(j, lambda c: c.start())
            fa_copy.wait()
            fa_ref[...] = fabuf[...].reshape(FA_ROWS, D).astype(bf16)

        for j in range(n_piece):
            @pl.when(i == j * (WIN_PIECE // WIN_STEP))
            def _():
                both(j, lambda c: c.wait())

        win_ref[...] = buf[pl.ds(pl.multiple_of(i * WIN_STEP, WIN_STEP), WIN_STEP)].reshape(WIN_STEP, D).astype(bf16)

    return _call_indexed(
        order, body, (q_arr,), (w,), (steps,), [pl.BlockSpec(memory_space=pl.ANY)],
        [pl.BlockSpec((None, WIN_STEP, D), lambda i, q: (q[0], i, 0)),
         pl.BlockSpec((None, FA_ROWS, D), lambda i, q: (q[0], 0, 0))],
        scratch_shapes=[pltpu.VMEM((WIN_ROWS, 1, D), f32), pltpu.VMEM((FA_ROWS, 1, D), f32),
                        pltpu.SemaphoreType.DMA((n_piece + 2,))],
        name="window_w_in", out_shape=[jax.ShapeDtypeStruct((NCHIP, WIN_ROWS, D), bf16),
                                       jax.ShapeDtypeStruct((NCHIP, FA_ROWS, D), bf16)],
        compiler_params=_params(("arbitrary",)),
    )


def _unfeature_rows(a):
    return jnp.transpose(a, (1, 2, 0))


ADAM_IN_ROWS = 134
ADAM_IN_STEPS = SHARD_IN // ADAM_IN_ROWS
ADAM_IN_CHUNK = 136
ADAM_IN_CHUNKS = ADAM_IN_STEPS + 1
ADAM_IN_BUF = WIN_ROWS + N_FA


def _adamw_w_in(order, w, gwin, gfa, m, v, q_arr):
    assert ADAM_IN_CHUNK * ADAM_IN_STEPS < WIN_ROWS <= ADAM_IN_CHUNK * ADAM_IN_CHUNKS
    assert OWN_ROW0[NCHIP - 1] + ADAM_IN_ROWS <= 2 * ADAM_IN_CHUNK and ADAM_IN_CHUNK >= ADAM_IN_ROWS
    last0 = ADAM_IN_CHUNK * ADAM_IN_STEPS
    cut = OWN_ROW0[1] + FA_AT

    def body(q_ref, w_ref, gwin_ref, gfa_ref, m_ref, v_ref, go_ref, d_ref, nm_ref, nv_ref, buf, sem):
        i = pl.program_id(0)
        q = q_ref[0]
        chip1 = q == 1
        shift = jnp.where(chip1, N_FA, 0)

        def copy(src_ref, src0, dst0, n, slot):
            return pltpu.make_async_copy(src_ref.at[pl.ds(src0, n)], buf.at[pl.ds(dst0, n), 0], sem.at[slot])

        def first(on_chip1):
            if on_chip1:
                return [copy(gwin_ref, 0, 0, cut, 0), copy(gfa_ref, 0, cut, N_FA, ADAM_IN_CHUNKS),
                        copy(gwin_ref, cut, cut + N_FA, ADAM_IN_CHUNK - cut - N_FA, ADAM_IN_CHUNKS + 1)]
            return [copy(gwin_ref, 0, 0, ADAM_IN_CHUNK, 0)]

        def middle(k):
            return [copy(gwin_ref, pl.multiple_of(k * ADAM_IN_CHUNK - shift, 8), k * ADAM_IN_CHUNK, ADAM_IN_CHUNK, k)]

        def last(on_chip1):
            n = WIN_ROWS - last0 + (N_FA if on_chip1 else 0)
            return [copy(gwin_ref, WIN_ROWS - n, last0, n, ADAM_IN_STEPS)]

        def both(make, act):
            for on_chip1 in (False, True):
                @pl.when(chip1 if on_chip1 else jnp.logical_not(chip1))
                def _():
                    for c in make(on_chip1):
                        act(c)

        @pl.when(i == 0)
        def _():
            both(first, lambda c: c.start())
            for k in range(1, ADAM_IN_STEPS):
                middle(k)[0].start()
            both(last, lambda c: c.start())
            both(first, lambda c: c.wait())

        @pl.when(i < ADAM_IN_STEPS - 1)
        def _():
            middle(i + 1)[0].wait()

        @pl.when(i == ADAM_IN_STEPS - 1)
        def _():
            both(last, lambda c: c.wait())

        row0 = jnp.where(q == 0, OWN_ROW0[0], jnp.where(chip1, OWN_ROW0[1], jnp.where(q == 2, OWN_ROW0[2], OWN_ROW0[3])))
        g = buf[pl.ds(row0 + i * ADAM_IN_ROWS, ADAM_IN_ROWS)]
        go_ref[...] = g
        d_ref[...], nm_ref[...], nv_ref[...] = _adamw_math(w_ref[...], g, m_ref[...], v_ref[...])

    blk = pl.BlockSpec((ADAM_IN_ROWS, 1, D), lambda i, q: (i, 0, 0))
    hbm = pl.BlockSpec(memory_space=pl.ANY)
    return _call_indexed(
        order, body, (q_arr,), (w, gwin, gfa, m, v), (ADAM_IN_STEPS,), [blk, hbm, hbm, blk, blk], [blk] * 4,
        scratch_shapes=[pltpu.VMEM((ADAM_IN_BUF, 1, D), f32), pltpu.SemaphoreType.DMA((ADAM_IN_CHUNKS + 2,))],
        name="adamw_w_in", out_shape=[jax.ShapeDtypeStruct((SHARD_IN, 1, D), f32)] * 4,
        compiler_params=_params(("arbitrary",)),
    )


def kernel(x, norm_attn_g, w_in, b_forget, w_branch_a, w_branch_b, w_out, norm_mlp_g, w_up, w_down, norm_final_g, loss_target, m_norm_attn_g, m_w_in, m_b_forget, m_w_branch_a, m_w_branch_b, m_w_out, m_norm_mlp_g, m_w_up, m_w_down, m_norm_final_g, v_norm_attn_g, v_w_in, v_b_forget, v_w_branch_a, v_w_branch_b, v_w_out, v_norm_mlp_g, v_w_up, v_w_down, v_norm_final_g):
    xi, yi, ci = lax.axis_index("x"), lax.axis_index("y"), lax.axis_index("c")
    q_me = 2 * xi + yi
    c_arr = jnp.reshape(ci, (1,)).astype(jnp.int32)
    q_arr = jnp.reshape(q_me, (1,)).astype(jnp.int32)
    x_, tgt = x[0], loss_target[0]

    names = ["w_branch_a", "w_branch_b", "w_out", "w_up", "w_down"]
    big = dict(zip(names, [w_branch_a[0], w_branch_b[0], w_out[0], w_up[0], w_down[0]]))
    ms = dict(zip(names, [m_w_branch_a[0], m_w_branch_b[0], m_w_out[0], m_w_up[0], m_w_down[0]]))
    vs = dict(zip(names, [v_w_branch_a[0], v_w_branch_b[0], v_w_out[0], v_w_up[0], v_w_down[0]]))
    grad, upd = {}, {}
    order = _Order()

    def run(fn, *args, **kw):
        return fn(order, *args, **kw)

    def own_slot(a):
        return lax.dynamic_update_slice(lax.empty((NCHIP,) + a.shape, a.dtype), a[None], (q_me, 0, 0))

    sem_in, in_s = _allgather_start("allgather_start_in", run(_window_stacks, _feature_rows(w_in), q_arr), order)
    sem_rest, rest = _allgather_start("allgather_start_rest", [own_slot(w.astype(bf16)) for w in big.values()], order)
    rope = _rope_tables(order.tok[0, 0])
    sem_f, in_s = _allgather_forward("allgather_forward_in", in_s, sem_in, order, behind=rope)
    wins, fas = _allgather_finish("allgather_finish_in", in_s, sem_f, order)
    wt = run(_assemble_win, wins, fas)

    bpad = jnp.pad(b_forget, ((0, 0), (0, 120)))
    h1, qkvb, qkva, gates, fa = run(_norm_inproj, x_, norm_attn_g, wt, rope)
    F = run(_forget_cumsum, fa, bpad)
    oa, lsea = run(_fox_fwd, qkva, F)
    sem_f, rest = _allgather_forward("allgather_forward_rest", rest, sem_rest, order)
    ob, lseb = run(_dil_fwd, qkvb)
    was, wbs, wouts, wups, wdowns = _allgather_finish("allgather_finish_rest", rest, sem_f, order)
    wout = wouts.reshape(D, D)
    wdown = wdowns.reshape(DFF, D)
    ya, yb, mixed = run(_branch_mix, oa, ob, was, wbs, gates)
    x2, h2 = run(_outproj_norm, mixed, wout, x_, norm_mlp_g)
    u, a = run(_mlp_up, h2, wups)
    dx3, dx3b, dg3, loss_part = run(_mlp_down_loss, a, wdown, x2, norm_final_g.reshape(1, D), tgt)

    def comm(name, *parts):
        return _comm_multi(name, list(parts), order)

    def adamw_group(group, fulls, name):
        res = run(_adamw, [big[nm] for nm in group], fulls, [ms[nm] for nm in group], [vs[nm] for nm in group], name)
        for nm, r in zip(group, res):
            *upd[nm], grad[nm] = r

    grp_a, grp_b, grp_c = ["w_down", "w_up"], ["w_out", "w_branch_a", "w_branch_b"], ["w_in", "w_in_fa"]
    du = run(_mlp_down_bwd, dx3b, wdown, u)
    dwdown = run(_mm, a, dx3b, "tn", f32, 1024, D, "wgrad_down")
    dwup = run(_mm, h2, du, "tn", f32, D, 1024, "wgrad_up", stack_cols=True)
    ((sem_pa, buf_pa),) = comm("pair_start_a", _pair_start_part([dwdown.reshape(NCHIP, DFF // NCHIP, D), dwup]))
    dx2, dx2b, dg2 = run(_mlp_up_bwd, du, wups, x2, dx3, norm_mlp_g)
    ((gs, ts),) = comm("pair_wait_a", _pair_wait_part(buf_pa, sem_pa))
    p32_a, p16_a = run(_pair_add, gs, ts, q_arr, c_arr, "pair_add_a")
    ((sem_sa, buf_sa),) = comm("shard_start_a", _shard_start_part(p16_a))
    dya, dyb, dproj = run(_gate_bwd, dx2b, wout, gates, ya, yb)
    dwout = run(_mm, mixed, dx2b, "tn", f32, D, D, "wgrad_out")
    doa, dob = run(_branch_bwd, dya, dyb, was, wbs)
    dwas, dwbs = run(_branch_wgrad, oa, ob, dya, dyb)
    ((sem_pb, buf_pb),) = comm("pair_start_b", _pair_start_part([dwout.reshape(NCHIP, D // NCHIP, D), dwas, dwbs]))
    dF, dproj = run(_fox_bwd, qkva, doa, oa, lsea, F, dproj)
    (gs, ts), (rs_a, _) = comm("pair_wait_b_shard_wait_a", _pair_wait_part(buf_pb, sem_pb),
                               _shard_wait_part(buf_sa, sem_sa, len(grp_a)))
    p32_b, p16_b = run(_pair_add, gs, ts, q_arr, c_arr, "pair_add_b")
    fulls_a = run(_shard_sum, p32_a, rs_a, c_arr, "shard_sum_a")
    (sem_wa, fulls_a), (sem_sb, buf_sb) = comm("swap_start_a_shard_start_b", _swap_start_part(fulls_a),
                                               _shard_start_part(p16_b))
    dbf, dproj = run(_forget_bwd, dF, fa, bpad, dproj)
    dproj = run(_dil_bwd, qkvb, dob, ob, lseb, rope, dproj)
    (rs_b, _), fulls_a = comm("shard_wait_b_swap_wait_a", _shard_wait_part(buf_sb, sem_sb, len(grp_b)),
                              _swap_wait_part(fulls_a, sem_wa))
    fulls_b = run(_shard_sum, p32_b, rs_b, c_arr, "shard_sum_b")
    ((sem_wb, fulls_b),) = comm("swap_start_b", _swap_start_part(fulls_b))
    dwt = run(_mm, dproj, h1, "tn", f32, 512, D, "wgrad_in")
    dwfa = jnp.broadcast_to(dwt[F_FA:F_FA + FA_ROWS][None], (NCHIP, FA_ROWS, D))
    (sem_pc, buf_pc), fulls_b = comm("pair_start_c_swap_wait_b", _pair_start_part([dwt, dwfa], gathered=True),
                                     _swap_wait_part(fulls_b, sem_wb))
    adamw_group(grp_b, fulls_b, "adamw_b")
    (((dwt_c, dwfa_c), (t_in, t_fa)),) = comm("pair_wait_c", _pair_wait_part(buf_pc, sem_pc, gathered=True))
    p32_in, p16_in = run(_pair_add_gathered, dwt_c, t_in, q_arr, c_arr, "pair_add_w_in")
    p32_fa, p16_fa = run(_pair_add, [dwfa_c], [t_fa], q_arr, c_arr, "pair_add_w_in_fa")
    ((sem_sc, buf_sc),) = comm("shard_start_c", _shard_start_part([p16_in, *p16_fa]))
    gx, dg1 = run(_inproj_bwd, dproj, wt, x_, dx2, norm_attn_g)
    adamw_group(grp_a, fulls_a, "adamw_a")
    small = jnp.concatenate([dg1, dg2, dg3, jnp.pad(dbf[:, 0:8], ((0, 0), (0, D - 8))),
                             jnp.pad(loss_part, ((0, 0), (0, D - 128))),
                             jnp.zeros((SMALL_ROWS - 5, D), f32)], axis=0)
    sm = lax.dynamic_update_slice(lax.empty((8, SMALL_ROWS, D), f32), small[None],
                                  (4 * xi + 2 * yi + ci, 0, 0))
    (sem_sm, buf_sm), (rs_c, _) = comm("small_start_shard_wait_c", _shard_start_part([], sm),
                                       _shard_wait_part(buf_sc, sem_sc, len(grp_c)))
    fulls_c = (run(_shard_sum, [p32_in], rs_c[0:1], c_arr, "shard_sum_w_in")
               + run(_shard_sum, p32_fa, rs_c[1:2], c_arr, "shard_sum_w_in_fa"))
    (sem_wc, fulls_c), (_, sm) = comm("swap_start_c_small_wait", _swap_start_part(fulls_c),
                                      _shard_wait_part(buf_sm, sem_sm, 0))
    smalls = ["norm_attn_g", "norm_mlp_g", "norm_final_g", "b_forget"]
    loss, gs, res = run(_small_finish, sm, [norm_attn_g, norm_mlp_g, norm_final_g.reshape(1, D), b_forget],
                        [m_norm_attn_g, m_norm_mlp_g, m_norm_final_g.reshape(1, D), m_b_forget],
                        [v_norm_attn_g, v_norm_mlp_g, v_norm_final_g.reshape(1, D), v_b_forget])
    loss = loss.reshape(())
    grad.update(zip(smalls, gs))
    upd.update(zip(smalls, res))

    ((gwin, gfa),) = comm("swap_wait_c", _swap_wait_part(fulls_c, sem_wc))
    res_in = run(_adamw_w_in, _feature_rows(w_in), gwin, gfa, _feature_rows(m_w_in), _feature_rows(v_w_in), q_arr)
    grad["w_in"] = _unfeature_rows(res_in[0])
    upd["w_in"] = [_unfeature_rows(t) for t in res_in[1:]]

    order_out = ["norm_attn_g", "w_in", "b_forget", "w_branch_a", "w_branch_b", "w_out", "norm_mlp_g", "w_up",
                 "w_down", "norm_final_g"]
    shapes = dict(norm_attn_g=norm_attn_g.shape, w_in=w_in.shape, b_forget=b_forget.shape,
                  w_branch_a=w_branch_a.shape, w_branch_b=w_branch_b.shape, w_out=w_out.shape,
                  norm_mlp_g=norm_mlp_g.shape, w_up=w_up.shape, w_down=w_down.shape, norm_final_g=norm_final_g.shape)
    outs = [loss, gx.reshape(x.shape)]
    outs += [grad[nm].reshape(shapes[nm]) for nm in order_out]
    for k in range(3):
        outs += [upd[nm][k].reshape(shapes[nm]) for nm in order_out]
    return tuple(outs)
```

```python
import jax
import jax.numpy as jnp
from jax import lax
from jax.experimental import pallas as pl
from jax.experimental.pallas import tpu as pltpu

f32 = jnp.float32
bf16 = jnp.bfloat16

S = 2048
D = 1024
DFF = 4096
HD = 64
FOXW = 512
DILOUT = 256
DIL = (1, 4, 16)
BAND = 128
EPS = 1e-6
NEG = -1e30
ROPE_THETA = 500000.0
NCHIP = 4
TQ = 256

ADAM_LR, ADAM_B1, ADAM_B2, ADAM_EPS, ADAM_WD, ADAM_STEP = 0.001, 0.9, 0.999, 1e-08, 0.01, 10
VMEM_LIMIT = 56 * 1024 * 1024

UNIT = 64
NP = 6144
F_DIL, F_FOX, F_FA, F_G = 0, 2304, 3840, 4096
DIL_BLK, FOX_BLK = 1152, 384
WIN_UNITS, WIN_ROWS = 24, 1536
WIN_UNIT0 = (0, 23, 45, 68)
OWN_ROW0 = (0, 2, 60, 62)
SHARD_IN = 1474
N_FA = 8
FA_AT = 1536 - SHARD_IN
FA_ROWS = 32


def _compact_to_internal():
    c2i = {}
    for p in range(2):
        for role in range(3):
            for g in range(3):
                for hh in range(2):
                    c2i[24 + 12 * role + 4 * g + 2 * p + hh] = 18 * p + 6 * role + 2 * g + hh
    for p in range(4):
        for role in range(3):
            for hh in range(2):
                c2i[8 * role + 2 * p + hh] = F_FOX // UNIT + 6 * p + 2 * role + hh
    for j in range(32):
        c2i[60 + j] = F_G // UNIT + j
    return c2i


C2I = _compact_to_internal()
OVERLAP_UNITS = (23, 45, 46, 68)


def _params(sem=None):
    return pltpu.CompilerParams(dimension_semantics=sem, vmem_limit_bytes=VMEM_LIMIT)


class _Order:
    def __init__(self):
        self.tok = None

    def mark(self, v):
        self.tok = v

    def token_for(self, args):
        return [] if self.tok is None or any(self.tok is a for a in args) else [self.tok]


def _call(order, body, args, in_specs=None, **kw):
    args = list(args)
    n_in = len(args)
    if in_specs is None:
        in_specs = [pl.BlockSpec(memory_space=pltpu.VMEM)] * n_in
    kern = body
    extra = order.token_for(args)
    if extra:
        in_specs = list(in_specs) + [pl.BlockSpec(memory_space=pl.ANY)]

        def kern(*refs):
            body(*refs[:n_in], *refs[n_in + 1:])

    out = pl.pallas_call(kern, in_specs=in_specs, **kw)(*args, *extra)
    order.mark(out[0] if isinstance(out, (tuple, list)) else out)
    return out


def _call_indexed(order, body, scalars, args, grid, in_specs, out_specs, scratch_shapes=(), **kw):
    args, in_specs = list(args), list(in_specs)
    n_front = len(scalars) + len(args)
    kern = body
    extra = order.token_for(args)
    if extra:
        in_specs.append(pl.BlockSpec(memory_space=pl.ANY))

        def kern(*refs):
            body(*refs[:n_front], *refs[n_front + 1:])

    out = pl.pallas_call(
        kern, grid_spec=pltpu.PrefetchScalarGridSpec(num_scalar_prefetch=len(scalars), grid=grid, in_specs=in_specs,
                                                     out_specs=out_specs, scratch_shapes=scratch_shapes),
        **kw)(*scalars, *args, *extra)
    order.mark(out[0] if isinstance(out, (tuple, list)) else out)
    return out


def _dot(a, b):
    return jnp.dot(a, b, preferred_element_type=f32)


def _dot_nt(a, b):
    return lax.dot_general(a, b, (((1,), (1,)), ((), ())), preferred_element_type=f32)


def _dot_tn(a, b):
    return lax.dot_general(a, b, (((0,), (0,)), ((), ())), preferred_element_type=f32)


def _split3(x):
    hi = x.astype(bf16)
    r1 = x - hi.astype(f32)
    mid = r1.astype(bf16)
    lo = (r1 - mid.astype(f32)).astype(bf16)
    return hi, mid, lo


def _rope_tables(after):
    half = 8
    inv_freq = jnp.power(jnp.float32(ROPE_THETA), -jnp.arange(half, dtype=f32) * 2.0 / 16)
    ang = (jnp.arange(S).astype(f32) + after)[:, None] * inv_freq[None, :]
    cos, sin = jnp.cos(ang), jnp.sin(ang)
    one = jnp.ones((S, HD - 16), f32)
    zero = jnp.zeros((S, HD - 16), f32)
    z8 = jnp.zeros((S, 8), f32)
    c = jnp.concatenate([cos, cos, one], axis=1)
    s1 = jnp.concatenate([-sin, z8, zero], axis=1)
    s2 = jnp.concatenate([z8, sin, zero], axis=1)
    return tuple(jnp.concatenate([t, t], axis=1) for t in (c, s1, s2))


def _mm(order, a, b, mode, out_dtype, tm, tn, name, stack_cols=False):
    if mode == "nn":
        (M, K), (_, N) = a.shape, b.shape
        a_spec = pl.BlockSpec((tm, K), lambda i, j: (i, 0))
        b_spec = pl.BlockSpec((K, tn), lambda i, j: (0, j))
        dot = _dot
    elif mode == "nt":
        (M, K), (N, _) = a.shape, b.shape
        a_spec = pl.BlockSpec((tm, K), lambda i, j: (i, 0))
        b_spec = pl.BlockSpec((tn, K), lambda i, j: (j, 0))
        dot = _dot_nt
    else:
        (K, M), (_, N) = a.shape, b.shape
        a_spec = pl.BlockSpec((K, tm), lambda i, j: (0, i))
        b_spec = pl.BlockSpec((K, tn), lambda i, j: (0, j))
        dot = _dot_tn

    def body(a_ref, b_ref, o_ref):
        o_ref[...] = dot(a_ref[...], b_ref[...]).astype(out_dtype)

    if stack_cols:
        assert tm == M
        out_spec = pl.BlockSpec((None, tm, tn), lambda i, j: (j, 0, 0))
        out_shape = jax.ShapeDtypeStruct((N // tn, M, tn), out_dtype)
    else:
        out_spec = pl.BlockSpec((tm, tn), lambda i, j: (i, j))
        out_shape = jax.ShapeDtypeStruct((M, N), out_dtype)
    return _call(
        order, body, (a, b), name=name, grid=(M // tm, N // tn), in_specs=[a_spec, b_spec],
        out_specs=out_spec, out_shape=out_shape,
        compiler_params=_params(("parallel", "parallel")),
    )


def _assemble_win(order, wins, fas):
    def body(win_ref, fa_ref, o_ref):
        q = pl.program_id(0)

        @pl.when(q == 0)
        def _():
            o_ref[...] = jnp.zeros_like(o_ref)

        for k in range(NCHIP):
            @pl.when(q == k)
            def _(k=k):
                for j in range(WIN_UNITS):
                    cu = WIN_UNIT0[k] + j
                    dst = pl.ds(C2I[cu] * UNIT, UNIT)
                    if cu in OVERLAP_UNITS:
                        o_ref[dst, :] += win_ref[j * UNIT:(j + 1) * UNIT, :]
                    else:
                        o_ref[dst, :] = win_ref[j * UNIT:(j + 1) * UNIT, :]
                if k == 1:
                    o_ref[F_FA:F_FA + FA_ROWS, :] = fa_ref[...]

    return _call(
        order, body, (wins, fas), name="assemble_w_in", grid=(NCHIP,),
        in_specs=[pl.BlockSpec((None, WIN_ROWS, D), lambda q: (q, 0, 0)),
                  pl.BlockSpec((None, FA_ROWS, D), lambda q: (1, 0, 0))],
        out_specs=pl.BlockSpec((NP, D), lambda q: (0, 0)),
        out_shape=jax.ShapeDtypeStruct((NP, D), bf16),
        compiler_params=_params(("arbitrary",)),
    )


def _norm_inproj(order, x, g1, wt, rope):
    tm = 256
    c_t, s1_t, s2_t = rope

    def body(x_ref, g_ref, w_ref, c_ref, s1_ref, s2_ref, h_ref, qkvb_ref, qkva_ref, gates_ref, fa_ref):
        xb = x_ref[...]
        r = lax.rsqrt(jnp.mean(xb * xb, axis=-1, keepdims=True) + EPS)
        h = ((xb * r) * g_ref[...]).astype(bf16)
        h_ref[...] = h
        c, s1, s2 = c_ref[...], s1_ref[...], s2_ref[...]
        for p in range(2):
            pb = _dot_nt(h, w_ref[F_DIL + p * DIL_BLK:F_DIL + (p + 1) * DIL_BLK, :])
            for ch in range(DIL_BLK // 128):
                pc = pb[:, ch * 128:(ch + 1) * 128]
                if ch < 6:
                    pc = pc * c + pltpu.roll(pc, 120, 1) * s1 + pltpu.roll(pc, 8, 1) * s2
                qkvb_ref[:, p * DIL_BLK + ch * 128:p * DIL_BLK + (ch + 1) * 128] = pc
        qkva_ref[...] = _dot_nt(h, w_ref[F_FOX:F_FA, :]).astype(bf16)
        fa_ref[...] = _dot_nt(h, w_ref[F_FA:F_FA + 128, :])
        gates_ref[...] = _dot_nt(h, w_ref[F_G:NP, :]).astype(bf16)

    row = lambda w: pl.BlockSpec((tm, w), lambda i: (i, 0))
    return _call(
        order, body, (x, g1, wt, c_t, s1_t, s2_t), name="norm_inproj", grid=(S // tm,),
        in_specs=[row(D), pl.BlockSpec((1, D), lambda i: (0, 0)), pl.BlockSpec((NP, D), lambda i: (0, 0)),
                  row(128), row(128), row(128)],
        out_specs=[row(D), row(2 * DIL_BLK), row(4 * FOX_BLK), row(2 * D), row(128)],
        out_shape=[jax.ShapeDtypeStruct((S, D), bf16), jax.ShapeDtypeStruct((S, 2 * DIL_BLK), f32),
                   jax.ShapeDtypeStruct((S, 4 * FOX_BLK), bf16), jax.ShapeDtypeStruct((S, 2 * D), bf16),
                   jax.ShapeDtypeStruct((S, 128), f32)],
        compiler_params=_params(("parallel",)),
    )


def _forget_cumsum(order, fa, bpad):
    nb = S // TQ

    def body(fa_ref, b_ref, F_ref):
        rr = lax.broadcasted_iota(jnp.int32, (TQ, TQ), 0)
        cc = lax.broadcasted_iota(jnp.int32, (TQ, TQ), 1)
        tri = (rr >= cc).astype(bf16)
        lane = lax.broadcasted_iota(jnp.int32, (1, 128), 1)
        carry = jnp.zeros((1, 128), f32)
        for b in range(nb):
            z = fa_ref[b * TQ:(b + 1) * TQ, :] + b_ref[...]
            lf = jnp.minimum(z, 0.0) - jnp.log(1.0 + jnp.exp(-jnp.abs(z)))
            lf = jnp.where(lane < 8, lf, 0.0)
            hi, mid, lo = _split3(lf)
            fb = (_dot(tri, hi) + _dot(tri, mid)) + _dot(tri, lo) + carry
            F_ref[b * TQ:(b + 1) * TQ, :] = fb
            carry = fb[TQ - 1:TQ, :]

    return _call(
        order, body, (fa, bpad), name="forget_cumsum",
        out_shape=jax.ShapeDtypeStruct((S, 128), f32),
        compiler_params=_params(),
    )


def _head_masks():
    lane = lax.broadcasted_iota(jnp.int32, (1, 128), 1)
    return lane, (lane < HD, lane >= HD)


L_ONE = 3
FOX_TQ, FOX_TK = 256, 512


def _set_lanes(x, lane, first, cols):
    for n, col in enumerate(cols):
        x = jnp.where(lane == first + n, col, x)
    return x


def _f32_parts(col):
    return [t.astype(f32) for t in _split3(col)]


def _fox_operands(qkv_ref, F_ref, lse_ref, qa, ka, p, rows):
    lane, hm = _head_masks()
    q = qkv_ref[rows, 0:128].astype(f32) * 0.125
    k = qkv_ref[rows, 128:256].astype(f32)
    Fb = F_ref[rows, :]
    for hh in (0, 1):
        free = (1 - hh) * HD
        fcol = jnp.sum(jnp.where(lane == 2 * p + hh, Fb, 0.0), axis=1, keepdims=True)
        qterm = fcol if lse_ref is None else fcol - lse_ref[rows, hh * HD:hh * HD + 1]
        qcols = _f32_parts(qterm) + [1.0] * 3
        kcols = [1.0] * 3 + [-t for t in _f32_parts(fcol)]
        qa[hh, rows, :] = _set_lanes(jnp.where(hm[hh], q, 0.0), lane, free, qcols).astype(bf16)
        ka[hh, rows, :] = _set_lanes(k, lane, free, kcols).astype(bf16)


def _fox_fwd(order, qkva, F):
    tq, tk = FOX_TQ, FOX_TK

    def body(qkv_ref, F_ref, o_ref, lse_ref, qa, ka, vt):
        p = pl.program_id(0)
        keyi = lax.broadcasted_iota(jnp.int32, (tk, 1), 0)
        qryi = lax.broadcasted_iota(jnp.int32, (1, tq), 1)
        sub = lax.broadcasted_iota(jnp.int32, (128, 1), 0)

        def prep(i, c):
            rows = pl.ds(pl.multiple_of(i * tk, tk), tk)
            _fox_operands(qkv_ref, F_ref, None, qa, ka, p, rows)
            vt[i] = qkv_ref[rows, 256:384].astype(f32).T.astype(bf16)
            return c

        lax.fori_loop(0, S // tk, prep, 0)

        def qblock(i, first_half):
            r0 = pl.multiple_of(i * tq, tq)
            qh = [qa[hh, pl.ds(r0, tq), :] for hh in (0, 1)]

            def kv(jb, carry, masked, width):
                keys = pl.ds(pl.multiple_of(jb * tk, tk), width)
                sts = [_dot_nt(ka[hh, keys, :], qh[hh]) for hh in (0, 1)]
                new = []
                for hh in (0, 1):
                    m, l, a = carry[3 * hh:3 * hh + 3]
                    st = sts[hh]
                    if masked:
                        st = jnp.where(jb * tk + keyi[0:width] <= r0 + qryi, st, NEG)
                    mn = jnp.maximum(m, jnp.max(st, axis=0, keepdims=True))
                    al = jnp.exp(m - mn)
                    pt = jnp.exp(st - mn)
                    l = al * l + jnp.sum(pt, axis=0, keepdims=True)
                    a = al * a + _dot(vt[jb, hh * HD:(hh + 1) * HD, 0:width], pt.astype(bf16))
                    new += [mn, l, a]
                return tuple(new)

            init = (jnp.full((1, tq), NEG, f32), jnp.zeros((1, tq), f32), jnp.zeros((HD, tq), f32)) * 2
            last = (r0 + tq - 1) // tk
            carry = lax.fori_loop(0, last, lambda j, cr: kv(j, cr, False, tk), init)
            m0, l0, a0, m1, l1, a1 = kv(last, carry, True, tk // 2 if first_half else tk)
            ot = jnp.concatenate([a0 / l0, a1 / l1], axis=0)
            lt = jnp.where(sub < HD, m0 + jnp.log(l0), m1 + jnp.log(l1))
            o_ref[pl.ds(r0, tq), :] = ot.T.astype(bf16)
            lse_ref[pl.ds(r0, tq), :] = lt.T

        def qpair(t, c):
            qblock(2 * t, True)
            qblock(2 * t + 1, False)
            return c

        assert tk == 2 * tq
        lax.fori_loop(0, S // tk, qpair, 0)

    pair = pl.BlockSpec((S, 128), lambda p: (0, p))
    return _call(
        order, body, (qkva, F), name="fox_fwd", grid=(4,),
        in_specs=[pl.BlockSpec((S, FOX_BLK), lambda p: (0, p)), pl.BlockSpec((S, 128), lambda p: (0, 0))],
        out_specs=[pair, pair],
        out_shape=[jax.ShapeDtypeStruct((S, FOXW), bf16), jax.ShapeDtypeStruct((S, FOXW), f32)],
        scratch_shapes=[pltpu.VMEM((2, S, 128), bf16)] * 2 + [pltpu.VMEM((S // tk, 128, tk), bf16)],
        compiler_params=_params(("parallel",)),
    )


def _permute_in(dst, src, r):
    L = S // r
    for rho in range(r):
        dst[rho * L:(rho + 1) * L, :] = src[pl.ds(rho, L, stride=r), :]


def _permute_out(dst, src, r):
    L = S // r
    for rho in range(r):
        dst[pl.ds(rho, L, stride=r), :] = src[rho * L:(rho + 1) * L, :]


def _band_width(nbl):
    return BAND if nbl == 1 else 2 * BAND


def _band_geometry(bb, nbl):
    r0 = pl.multiple_of(bb * BAND, BAND)
    if nbl == 1:
        k0 = r0
    else:
        k0 = pl.multiple_of(jnp.maximum(bb - 1, 0) * BAND, BAND)
    sub0 = (bb - lax.rem(bb, nbl)) * BAND
    qi = r0 + lax.broadcasted_iota(jnp.int32, (BAND, 1), 0)
    ki = k0 + lax.broadcasted_iota(jnp.int32, (1, _band_width(nbl)), 1)
    diff = qi - ki
    valid = (diff >= 0) & (diff <= BAND) & (ki >= sub0)
    return r0, k0, valid


def _dil_views(ref):
    return [[ref.at[:, pl.ds((3 * role + g) * 128, 128)] for g in range(3)] for role in range(3)]


DIL_UNROLL = 4


def _dil_in_specs():
    return [pl.BlockSpec((S, 128), lambda p, k=k: (0, 9 * p + k)) for k in range(9)]


def _dil_fwd(order, qkvb):
    def body(*refs):
        q_refs, k_refs, v_refs = refs[0:3], refs[3:6], refs[6:9]
        ob_ref, lse_ref, qp, kp, vp, op, lp = refs[9:16]
        on, ln = refs[16:19], refs[19:22]
        _, hm = _head_masks()
        for g, r in enumerate(DIL):
            nbl = S // r // BAND
            if r == 1:
                qs_, ks_, vs_, od, ld = q_refs[g], k_refs[g], v_refs[g], on[g], ln[g]
            else:
                _permute_in(qp, q_refs[g], r)
                _permute_in(kp, k_refs[g], r)
                _permute_in(vp, v_refs[g], r)
                qs_, ks_, vs_, od, ld = qp, kp, vp, op, lp

            def blk(t, c, qs_=qs_, ks_=ks_, vs_=vs_, od=od, ld=ld, nbl=nbl):
                work = []
                for u in range(DIL_UNROLL):
                    r0, k0, valid = _band_geometry(DIL_UNROLL * t + u, nbl)
                    q = qs_[pl.ds(r0, BAND), :] * 0.125
                    kw = ks_[pl.ds(k0, _band_width(nbl)), :].astype(bf16)
                    vw = vs_[pl.ds(k0, _band_width(nbl)), :]
                    for hh in (0, 1):
                        qh = jnp.where(hm[hh], q, 0.0).astype(bf16)
                        work.append((u, hh, r0, valid, vw, _dot_nt(qh, kw)))
                o = [jnp.zeros((BAND, 128), f32)] * DIL_UNROLL
                lse = [jnp.zeros((BAND, 128), f32)] * DIL_UNROLL
                for u, hh, r0, valid, vw, s in work:
                    s = jnp.where(valid, s, NEG)
                    m = jnp.max(s, axis=1, keepdims=True)
                    pr = jnp.exp(s - m)
                    l = jnp.sum(pr, axis=1, keepdims=True)
                    vm = jnp.where(hm[hh], vw, 0.0).astype(bf16)
                    o[u] = o[u] + _dot((pr / l).astype(bf16), vm)
                    lse[u] = jnp.where(hm[hh], m + jnp.log(l), lse[u])
                    if hh == 1:
                        od[pl.ds(r0, BAND), :] = o[u]
                        ld[pl.ds(r0, BAND), :] = lse[u]
                return c

            lax.fori_loop(0, S // BAND // DIL_UNROLL, blk, 0)
            if r != 1:
                _permute_out(on[g], op, r)
                _permute_out(ln[g], lp, r)

        def combine(i, c):
            r0 = pl.multiple_of(i * TQ, TQ)
            ls = [ln[g][pl.ds(r0, TQ), :] for g in range(3)]
            mx = jnp.maximum(jnp.maximum(ls[0], ls[1]), ls[2])
            es = [jnp.exp(l - mx) for l in ls]
            tot = (es[0] + es[1]) + es[2]
            acc = (es[0] / tot) * on[0][pl.ds(r0, TQ), :]
            acc = acc + (es[1] / tot) * on[1][pl.ds(r0, TQ), :]
            acc = acc + (es[2] / tot) * on[2][pl.ds(r0, TQ), :]
            ob_ref[pl.ds(r0, TQ), :] = acc.astype(bf16)
            lse_ref[pl.ds(r0, TQ), :] = mx + jnp.log(tot)
            return c

        lax.fori_loop(0, S // TQ, combine, 0)

    out_blk = pl.BlockSpec((S, 128), lambda p: (0, p))
    return _call(
        order, body, [qkvb] * 9, name="dil_fwd", grid=(2,),
        in_specs=_dil_in_specs(), out_specs=[out_blk, out_blk],
        out_shape=[jax.ShapeDtypeStruct((S, DILOUT), bf16), jax.ShapeDtypeStruct((S, DILOUT), f32)],
        scratch_shapes=[pltpu.VMEM((S, 128), f32)] * 11,
        compiler_params=_params(("parallel",)),
    )


def _branch_mix(order, oa, ob, was, wbs, gates):
    tm = 512

    def body(oa_ref, ob_ref, wa_ref, wb_ref, g_ref, ya_ref, yb_ref, mix_ref):
        oa_b, ob_b = oa_ref[...], ob_ref[...]
        for q in range(NCHIP):
            cols = slice(q * 256, (q + 1) * 256)
            ya = _dot(oa_b, wa_ref[q])
            yb = _dot(ob_b, wb_ref[q])
            ya_ref[:, cols] = ya.astype(bf16)
            yb_ref[:, cols] = yb.astype(bf16)
            ga = g_ref[:, q * 256:(q + 1) * 256].astype(f32)
            gb = g_ref[:, D + q * 256:D + (q + 1) * 256].astype(f32)
            mix_ref[:, cols] = (jax.nn.sigmoid(ga) * ya + jax.nn.sigmoid(gb) * yb).astype(bf16)

    row = lambda w: pl.BlockSpec((tm, w), lambda i: (i, 0))
    full3 = lambda a: pl.BlockSpec(a.shape, lambda i: (0, 0, 0))
    return _call(
        order, body, (oa, ob, was, wbs, gates), name="branch_mix", grid=(S // tm,),
        in_specs=[row(FOXW), row(DILOUT), full3(was), full3(wbs), row(2 * D)],
        out_specs=[row(D), row(D), row(D)],
        out_shape=[jax.ShapeDtypeStruct((S, D), bf16), jax.ShapeDtypeStruct((S, D), bf16),
                   jax.ShapeDtypeStruct((S, D), bf16)],
        compiler_params=_params(("parallel",)),
    )


def _outproj_norm(order, mixed, wout, x, g2):
    tm = 512

    def body(m_ref, w_ref, x_ref, g_ref, x2_ref, h2_ref):
        x2 = x_ref[...] + _dot(m_ref[...], w_ref[...])
        x2_ref[...] = x2
        r = lax.rsqrt(jnp.mean(x2 * x2, axis=-1, keepdims=True) + EPS)
        h2_ref[...] = ((x2 * r) * g_ref[...]).astype(bf16)

    row = pl.BlockSpec((tm, D), lambda i: (i, 0))
    return _call(
        order, body, (mixed, wout, x, g2), name="outproj_norm", grid=(S // tm,),
        in_specs=[row, pl.BlockSpec((D, D), lambda i: (0, 0)), row, pl.BlockSpec((1, D), lambda i: (0, 0))],
        out_specs=[row, row],
        out_shape=[jax.ShapeDtypeStruct((S, D), f32), jax.ShapeDtypeStruct((S, D), bf16)],
        compiler_params=_params(("parallel",)),
    )


def _mlp_up(order, h2, wups):
    tm = 1024

    def body(h_ref, w_ref, ru_ref, a_ref):
        ru = jnp.maximum(_dot(h_ref[...], w_ref[...]), 0.0)
        ru_ref[...] = ru.astype(bf16)
        a_ref[...] = (ru * ru).astype(bf16)

    out = pl.BlockSpec((tm, D), lambda q, i: (i, q))
    return _call(
        order, body, (h2, wups), name="mlp_up", grid=(NCHIP, S // tm),
        in_specs=[pl.BlockSpec((tm, D), lambda q, i: (i, 0)), pl.BlockSpec((None, D, D), lambda q, i: (q, 0, 0))],
        out_specs=[out, out],
        out_shape=[jax.ShapeDtypeStruct((S, DFF), bf16), jax.ShapeDtypeStruct((S, DFF), bf16)],
        compiler_params=_params(("parallel", "parallel")),
    )


def _mlp_down_loss(order, a, wdown, x2, g3, tgt):
    tm = 512

    def body(a_ref, w_ref, x2_ref, g_ref, t_ref, dx_ref, dxb_ref, dg_ref, loss_ref):
        i = pl.program_id(0)
        x3 = x2_ref[...] + _dot(a_ref[...], w_ref[...])
        r = lax.rsqrt(jnp.mean(x3 * x3, axis=-1, keepdims=True) + EPS)
        xh = x3 * r
        g = g_ref[...]
        e = xh * g - t_ref[...]
        part = 0.5 * jnp.sum(jnp.mean(e * e, axis=-1, keepdims=True), axis=0, keepdims=True)
        dy = e * (1.0 / D)
        gdy = dy * g
        dx = r * (gdy - xh * jnp.mean(gdy * xh, axis=-1, keepdims=True))
        dx_ref[...] = dx
        dxb_ref[...] = dx.astype(bf16)

        @pl.when(i == 0)
        def _():
            dg_ref[...] = jnp.zeros_like(dg_ref)
            loss_ref[...] = jnp.zeros_like(loss_ref)

        dg_ref[...] += jnp.sum(dy * xh, axis=0, keepdims=True)
        loss_ref[...] += jnp.broadcast_to(part, (1, 128))

    row = pl.BlockSpec((tm, D), lambda i: (i, 0))
    vec = pl.BlockSpec((1, D), lambda i: (0, 0))
    return _call(
        order, body, (a, wdown, x2, g3, tgt), name="mlp_down_loss", grid=(S // tm,),
        in_specs=[pl.BlockSpec((tm, DFF), lambda i: (i, 0)), pl.BlockSpec((DFF, D), lambda i: (0, 0)), row, vec, row],
        out_specs=[row, row, vec, pl.BlockSpec((1, 128), lambda i: (0, 0))],
        out_shape=[jax.ShapeDtypeStruct((S, D), f32), jax.ShapeDtypeStruct((S, D), bf16),
                   jax.ShapeDtypeStruct((1, D), f32), jax.ShapeDtypeStruct((1, 128), f32)],
        compiler_params=_params(("arbitrary",)),
    )


def _mlp_down_bwd(order, dx3b, wdown, u):
    tm = 512

    def body(d_ref, w_ref, u_ref, du_ref):
        d = d_ref[...]
        for q in range(NCHIP):
            cols = slice(q * D, (q + 1) * D)
            da = _dot_nt(d, w_ref[cols, :])
            du_ref[:, cols] = (da * (2.0 * u_ref[:, cols].astype(f32))).astype(bf16)

    return _call(
        order, body, (dx3b, wdown, u), name="mlp_down_bwd", grid=(S // tm,),
        in_specs=[pl.BlockSpec((tm, D), lambda i: (i, 0)), pl.BlockSpec((DFF, D), lambda i: (0, 0)),
                  pl.BlockSpec((tm, DFF), lambda i: (i, 0))],
        out_specs=pl.BlockSpec((tm, DFF), lambda i: (i, 0)),
        out_shape=jax.ShapeDtypeStruct((S, DFF), bf16),
        compiler_params=_params(("parallel",)),
    )


def _mlp_up_bwd(order, du, wups, x2, dx3, g2):
    tm = 512

    def body(du_ref, w_ref, x2_ref, dx3_ref, g_ref, dx2_ref, dx2b_ref, dg_ref):
        i = pl.program_id(0)
        dh = jnp.zeros((tm, D), f32)
        for q in range(NCHIP):
            dh = dh + _dot_nt(du_ref[:, q * D:(q + 1) * D], w_ref[q])
        x2 = x2_ref[...]
        r = lax.rsqrt(jnp.mean(x2 * x2, axis=-1, keepdims=True) + EPS)
        xh = x2 * r
        gdh = dh * g_ref[...]
        dx2 = dx3_ref[...] + r * (gdh - xh * jnp.mean(gdh * xh, axis=-1, keepdims=True))
        dx2_ref[...] = dx2
        dx2b_ref[...] = dx2.astype(bf16)

        @pl.when(i == 0)
        def _():
            dg_ref[...] = jnp.zeros_like(dg_ref)

        dg_ref[...] += jnp.sum(dh * xh, axis=0, keepdims=True)

    row = pl.BlockSpec((tm, D), lambda i: (i, 0))
    vec = pl.BlockSpec((1, D), lambda i: (0, 0))
    return _call(
        order, body, (du, wups, x2, dx3, g2), name="mlp_up_bwd", grid=(S // tm,),
        in_specs=[pl.BlockSpec((tm, DFF), lambda i: (i, 0)), pl.BlockSpec((NCHIP, D, D), lambda i: (0, 0, 0)),
                  row, row, vec],
        out_specs=[row, row, vec],
        out_shape=[jax.ShapeDtypeStruct((S, D), f32), jax.ShapeDtypeStruct((S, D), bf16),
                   jax.ShapeDtypeStruct((1, D), f32)],
        compiler_params=_params(("arbitrary",)),
    )


def _gate_bwd(order, dx2b, wout, gates, ya, yb):
    tm = 512

    def body(d_ref, w_ref, g_ref, ya_ref, yb_ref, dya_ref, dyb_ref, dproj_ref):
        dm = _dot_nt(d_ref[...], w_ref[...])
        sa = jax.nn.sigmoid(g_ref[:, 0:D].astype(f32))
        sb = jax.nn.sigmoid(g_ref[:, D:2 * D].astype(f32))
        dya_ref[...] = (dm * sa).astype(bf16)
        dyb_ref[...] = (dm * sb).astype(bf16)
        dproj_ref[:, 0:D] = (dm * ya_ref[...].astype(f32) * (sa * (1.0 - sa))).astype(bf16)
        dproj_ref[:, D:2 * D] = (dm * yb_ref[...].astype(f32) * (sb * (1.0 - sb))).astype(bf16)

    row = lambda w: pl.BlockSpec((tm, w), lambda i: (i, 0))
    return _call(
        order, body, (dx2b, wout, gates, ya, yb), name="gate_bwd", grid=(S // tm,),
        in_specs=[row(D), pl.BlockSpec((D, D), lambda i: (0, 0)), row(2 * D), row(D), row(D)],
        out_specs=[row(D), row(D), pl.BlockSpec((tm, 2 * D), lambda i: (i, F_G // (2 * D)))],
        out_shape=[jax.ShapeDtypeStruct((S, D), bf16), jax.ShapeDtypeStruct((S, D), bf16),
                   jax.ShapeDtypeStruct((S, NP), bf16)],
        compiler_params=_params(("parallel",)),
    )


def _branch_bwd(order, dya, dyb, was, wbs):
    tm = 512

    def body(dya_ref, dyb_ref, wa_ref, wb_ref, doa_ref, dob_ref):
        doa = jnp.zeros((tm, FOXW), f32)
        dob = jnp.zeros((tm, DILOUT), f32)
        for q in range(NCHIP):
            cols = slice(q * 256, (q + 1) * 256)
            doa = doa + _dot_nt(dya_ref[:, cols], wa_ref[q])
            dob = dob + _dot_nt(dyb_ref[:, cols], wb_ref[q])
        doa_ref[...] = doa.astype(bf16)
        dob_ref[...] = dob

    row = lambda w: pl.BlockSpec((tm, w), lambda i: (i, 0))
    full3 = lambda a: pl.BlockSpec(a.shape, lambda i: (0, 0, 0))
    return _call(
        order, body, (dya, dyb, was, wbs), name="branch_bwd", grid=(S // tm,),
        in_specs=[row(D), row(D), full3(was), full3(wbs)],
        out_specs=[row(FOXW), row(DILOUT)],
        out_shape=[jax.ShapeDtypeStruct((S, FOXW), bf16), jax.ShapeDtypeStruct((S, DILOUT), f32)],
        compiler_params=_params(("parallel",)),
    )


def _branch_wgrad(order, oa, ob, dya, dyb):
    def body(oa_ref, ob_ref, dya_ref, dyb_ref, dwa_ref, dwb_ref):
        dwa_ref[...] = _dot_tn(oa_ref[...], dya_ref[...])
        dwb_ref[...] = _dot_tn(ob_ref[...], dyb_ref[...])

    full = lambda w: pl.BlockSpec((S, w), lambda q: (0, 0))
    colq = pl.BlockSpec((S, 256), lambda q: (0, q))
    return _call(
        order, body, (oa, ob, dya, dyb), name="branch_wgrad", grid=(NCHIP,),
        in_specs=[full(FOXW), full(DILOUT), colq, colq],
        out_specs=[pl.BlockSpec((None, FOXW, 256), lambda q: (q, 0, 0)),
                   pl.BlockSpec((None, DILOUT, 256), lambda q: (q, 0, 0))],
        out_shape=[jax.ShapeDtypeStruct((NCHIP, FOXW, 256), f32), jax.ShapeDtypeStruct((NCHIP, DILOUT, 256), f32)],
        compiler_params=_params(("parallel",)),
    )


def _fox_bwd(order, qkva, doa, oa, lse, F, dproj):
    tq, tk = FOX_TQ, FOX_TK

    def body(qkv_ref, do_ref, o_ref, lse_ref, F_ref, _dproj_in, dF_ref, dqkv_ref, qa, ka, da, va, kat,
             dk_scr, dv_scr, dqt_scr):
        p = pl.program_id(0)
        lane, hm = _head_masks()
        keyi = lax.broadcasted_iota(jnp.int32, (tk, 1), 0)
        qryi = lax.broadcasted_iota(jnp.int32, (1, tq), 1)

        def prep(i, c):
            rows = pl.ds(pl.multiple_of(i * tk, tk), tk)
            _fox_operands(qkv_ref, F_ref, lse_ref, qa, ka, p, rows)
            do = do_ref[rows, :].astype(f32)
            prod = do * o_ref[rows, :].astype(f32)
            v = qkv_ref[rows, 256:384].astype(f32)
            for hh in (0, 1):
                free = (1 - hh) * HD
                delta = jnp.sum(jnp.where(hm[hh], prod, 0.0), axis=1, keepdims=True)
                da[hh, rows, :] = _set_lanes(jnp.where(hm[hh], do, 0.0), lane, free,
                                             [-t for t in _f32_parts(delta)]).astype(bf16)
                va[hh, rows, :] = _set_lanes(v, lane, free, [1.0] * 3).astype(bf16)
                kat[hh, i] = ka[hh, rows, :].astype(f32).T.astype(bf16)
                dk_scr[hh, rows, :] = jnp.zeros((tk, 128), f32)
                dv_scr[hh, rows, :] = jnp.zeros((tk, 128), f32)
            return c

        lax.fori_loop(0, S // tk, prep, 0)

        def qblock(i, first_half):
            r0 = pl.multiple_of(i * tq, tq)
            qrows = pl.ds(r0, tq)
            qh = [qa[hh, qrows, :] for hh in (0, 1)]
            dh = [da[hh, qrows, :] for hh in (0, 1)]
            dqt_scr[...] = jnp.zeros_like(dqt_scr)

            def kv(jb, c2, masked, width):
                keys = pl.ds(pl.multiple_of(jb * tk, tk), width)
                sts = [_dot_nt(ka[hh, keys, :], qh[hh]) for hh in (0, 1)]
                dps = [_dot_nt(va[hh, keys, :], dh[hh]) for hh in (0, 1)]
                for hh in (0, 1):
                    pt = jnp.exp(sts[hh])
                    if masked:
                        pt = jnp.where(jb * tk + keyi[0:width] <= r0 + qryi, pt, 0.0)
                    dsb = (pt * dps[hh]).astype(bf16)
                    dv_scr[hh, keys, :] += _dot(pt.astype(bf16), dh[hh])
                    dk_scr[hh, keys, :] += _dot(dsb, qh[hh])
                    dqt_scr[hh] += _dot(kat[hh, jb, :, 0:width], dsb)
                return c2

            last = (r0 + tq - 1) // tk
            lax.fori_loop(0, last, lambda j, c2: kv(j, c2, False, tk), 0)
            kv(last, 0, True, tk // 2 if first_half else tk)
            dq0, dq1 = dqt_scr[0].T, dqt_scr[1].T
            dqkv_ref[qrows, 0:128] = (jnp.where(hm[0], dq0, dq1) * 0.125).astype(bf16)
            dF_ref[qrows, :] = jnp.where(lane == 0, dq0[:, HD:HD + 1], jnp.where(lane == 1, dq1[:, 0:1], 0.0))

        def qpair(t, c):
            qblock(2 * t, True)
            qblock(2 * t + 1, False)
            return c

        assert tk == 2 * tq
        lax.fori_loop(0, S // tk, qpair, 0)

        def finish(i, c):
            rows = pl.ds(pl.multiple_of(i * tq, tq), tq)
            dk0, dk1 = dk_scr[0, rows, :], dk_scr[1, rows, :]
            dqkv_ref[rows, 128:256] = jnp.where(hm[0], dk0, dk1).astype(bf16)
            dqkv_ref[rows, 256:384] = jnp.where(hm[0], dv_scr[0, rows, :], dv_scr[1, rows, :]).astype(bf16)
            cs = jnp.where(lane == 0, dk0[:, HD + L_ONE:HD + L_ONE + 1],
                           jnp.where(lane == 1, dk1[:, L_ONE:L_ONE + 1], 0.0))
            dF_ref[rows, :] = dF_ref[rows, :] - cs
            return c

        lax.fori_loop(0, S // tq, finish, 0)

    pair = pl.BlockSpec((S, 128), lambda p: (0, p))
    return _call(
        order, body, (qkva, doa, oa, lse, F, dproj), name="fox_bwd", grid=(4,),
        in_specs=[pl.BlockSpec((S, FOX_BLK), lambda p: (0, p)), pair, pair, pair,
                  pl.BlockSpec((S, 128), lambda p: (0, 0)), pl.BlockSpec(memory_space=pl.ANY)],
        out_specs=[pair, pl.BlockSpec((S, FOX_BLK), lambda p: (0, F_FOX // FOX_BLK + p))],
        out_shape=[jax.ShapeDtypeStruct((S, FOXW), f32), jax.ShapeDtypeStruct((S, NP), bf16)],
        input_output_aliases={5: 1},
        scratch_shapes=[pltpu.VMEM((2, S, 128), bf16)] * 4 + [pltpu.VMEM((2, S // tk, 128, tk), bf16)]
        + [pltpu.VMEM((2, S, 128), f32)] * 2 + [pltpu.VMEM((2, 128, tq), f32)],
        compiler_params=_params(("parallel",)),
    )


def _forget_bwd(order, dF, fa, bpad, dproj):
    nb = S // TQ

    def body(dF_ref, fa_ref, b_ref, _dproj_in, db_ref, dfa_ref):
        rr = lax.broadcasted_iota(jnp.int32, (TQ, TQ), 0)
        cc = lax.broadcasted_iota(jnp.int32, (TQ, TQ), 1)
        upper = (cc >= rr).astype(bf16)
        lane = lax.broadcasted_iota(jnp.int32, (1, 128), 1)
        carry = jnp.zeros((1, 128), f32)
        db = jnp.zeros((1, 128), f32)
        for b in reversed(range(nb)):
            cols = jnp.zeros((TQ, 128), f32)
            for h in range(8):
                c0 = (h // 2) * 128 + h % 2
                cols = jnp.where(lane == h, dF_ref[b * TQ:(b + 1) * TQ, c0:c0 + 1], cols)
            dlf = carry
            for part in _split3(cols):
                dlf = dlf + _dot(upper, part)
            carry = carry + jnp.sum(cols, axis=0, keepdims=True)
            z = fa_ref[b * TQ:(b + 1) * TQ, :] + b_ref[...]
            dz = jnp.where(lane < 8, dlf * jax.nn.sigmoid(-z), 0.0)
            dfa_ref[b * TQ:(b + 1) * TQ, 0:128] = dz.astype(bf16)
            dfa_ref[b * TQ:(b + 1) * TQ, 128:256] = jnp.zeros((TQ, 128), bf16)
            db = db + jnp.sum(dz, axis=0, keepdims=True)
        db_ref[...] = db

    whole = lambda a: pl.BlockSpec(a.shape, lambda i: (0,) * a.ndim)
    return _call(
        order, body, (dF, fa, bpad, dproj), name="forget_bwd", grid=(1,),
        in_specs=[whole(dF), whole(fa), whole(bpad), pl.BlockSpec(memory_space=pl.ANY)],
        out_specs=[pl.BlockSpec((1, 128), lambda i: (0, 0)), pl.BlockSpec((S, 256), lambda i: (0, F_FA // 256))],
        out_shape=[jax.ShapeDtypeStruct((1, 128), f32), jax.ShapeDtypeStruct((S, NP), bf16)],
        input_output_aliases={3: 1},
        compiler_params=_params(("arbitrary",)),
    )


def _dil_bwd(order, qkvb, dob, ob, lseb, rope, dproj):
    c_t, s1_t, s2_t = rope

    def body(*refs):
        q_refs, k_refs, v_refs = refs[0:3], refs[3:6], refs[6:9]
        dob_ref, ob_ref, lse_ref, c_ref, s1_ref, s2_ref, _dproj_in, dqkv_ref = refs[9:17]
        qp, kp, vp, dop, lp, dlp, dln, dqp, dkp, dvp, nat = refs[17:28]
        dq_out, dk_out, dv_out = _dil_views(dqkv_ref)
        _, hm = _head_masks()

        def delta_rows(i, c):
            r0 = pl.multiple_of(i * TQ, TQ)
            prod = dob_ref[pl.ds(r0, TQ), :] * ob_ref[pl.ds(r0, TQ), :].astype(f32)
            d0 = jnp.sum(jnp.where(hm[0], prod, 0.0), axis=1, keepdims=True)
            d1 = jnp.sum(jnp.where(hm[1], prod, 0.0), axis=1, keepdims=True)
            dln[pl.ds(r0, TQ), :] = jnp.where(hm[0], d0, d1)
            return c

        lax.fori_loop(0, S // TQ, delta_rows, 0)

        for g, r in enumerate(DIL):
            nbl = S // r // BAND
            if r == 1:
                srcs = (q_refs[g], k_refs[g], v_refs[g], dob_ref, lse_ref, dln)
            else:
                for dst, src in ((qp, q_refs[g]), (kp, k_refs[g]), (vp, v_refs[g]), (dop, dob_ref),
                                 (lp, lse_ref), (dlp, dln)):
                    _permute_in(dst, src, r)
                srcs = (qp, kp, vp, dop, lp, dlp)
            dkp[...] = jnp.zeros_like(dkp)
            dvp[...] = jnp.zeros_like(dvp)

            def blk(t, c, srcs=srcs, nbl=nbl):
                qs_, ks_, vs_, dos_, ls_, dls_ = srcs
                work = []
                for u in range(DIL_UNROLL):
                    r0, k0, valid = _band_geometry(DIL_UNROLL * t + u, nbl)
                    q = qs_[pl.ds(r0, BAND), :] * 0.125
                    kwf = ks_[pl.ds(k0, _band_width(nbl)), :]
                    kw = kwf.astype(bf16)
                    vw = vs_[pl.ds(k0, _band_width(nbl)), :].astype(bf16)
                    do = dos_[pl.ds(r0, BAND), :]
                    lse = ls_[pl.ds(r0, BAND), :]
                    dlt = dls_[pl.ds(r0, BAND), :]
                    for hh in (0, 1):
                        qh = jnp.where(hm[hh], q, 0.0).astype(bf16)
                        doh = jnp.where(hm[hh], do, 0.0).astype(bf16)
                        kh = jnp.where(hm[hh], kwf, 0.0).astype(bf16)
                        work.append((u, hh, r0, k0, valid, qh, doh, kh, lse[:, hh * HD:hh * HD + 1],
                                     dlt[:, hh * HD:hh * HD + 1], _dot_nt(qh, kw), _dot_nt(doh, vw)))
                for u, hh, r0, k0, valid, qh, doh, kh, lse_h, dlt_h, s, dp in work:
                    if hh == 0:
                        dq = jnp.zeros((BAND, 128), f32)
                        dk = jnp.zeros((_band_width(nbl), 128), f32)
                        dv = jnp.zeros((_band_width(nbl), 128), f32)
                    pr = jnp.where(valid, jnp.exp(s - lse_h), 0.0)
                    dsb = (pr * (dp - dlt_h)).astype(bf16)
                    dv = dv + _dot_tn(pr.astype(bf16), doh)
                    dk = dk + _dot_tn(dsb, qh)
                    dq = dq + _dot(dsb, kh)
                    if hh == 1:
                        dqp[pl.ds(r0, BAND), :] = dq * 0.125
                        dkp[pl.ds(k0, _band_width(nbl)), :] += dk
                        dvp[pl.ds(k0, _band_width(nbl)), :] += dv
                return c

            lax.fori_loop(0, S // BAND // DIL_UNROLL, blk, 0)

            for acc, out, roped in ((dqp, dq_out[g], True), (dkp, dk_out[g], True), (dvp, dv_out[g], False)):
                if r == 1:
                    src = acc
                else:
                    _permute_out(nat, acc, r)
                    src = nat

                def emit(i, c, src=src, out=out, roped=roped):
                    r0 = pl.multiple_of(i * TQ, TQ)
                    d = src[pl.ds(r0, TQ), :]
                    if roped:
                        d = (d * c_ref[pl.ds(r0, TQ), :] + pltpu.roll(d * s1_ref[pl.ds(r0, TQ), :], 8, 1)
                             + pltpu.roll(d * s2_ref[pl.ds(r0, TQ), :], 120, 1))
                    out[pl.ds(r0, TQ), :] = d.astype(bf16)
                    return c

                lax.fori_loop(0, S // TQ, emit, 0)

    pair = pl.BlockSpec((S, 128), lambda p: (0, p))
    tab = pl.BlockSpec((S, 128), lambda p: (0, 0))
    blk_spec = pl.BlockSpec((S, DIL_BLK), lambda p: (0, p))
    return _call(
        order, body, [qkvb] * 9 + [dob, ob, lseb, c_t, s1_t, s2_t, dproj], name="dil_bwd", grid=(2,),
        in_specs=_dil_in_specs() + [pair, pair, pair, tab, tab, tab, pl.BlockSpec(memory_space=pl.ANY)],
        out_specs=blk_spec,
        out_shape=jax.ShapeDtypeStruct((S, NP), bf16),
        input_output_aliases={15: 0},
        scratch_shapes=[pltpu.VMEM((S, 128), f32)] * 11,
        compiler_params=_params(("parallel",)),
    )


def _inproj_bwd(order, dproj, wt, x, dx2, g1):
    tm = 256

    def body(d_ref, w_ref, x_ref, dx2_ref, g_ref, dx_ref, dg_ref):
        i = pl.program_id(0)
        dh = _dot(d_ref[...], w_ref[...])
        xb = x_ref[...]
        r = lax.rsqrt(jnp.mean(xb * xb, axis=-1, keepdims=True) + EPS)
        xh = xb * r
        gdh = dh * g_ref[...]
        dx_ref[...] = dx2_ref[...] + r * (gdh - xh * jnp.mean(gdh * xh, axis=-1, keepdims=True))

        @pl.when(i == 0)
        def _():
            dg_ref[...] = jnp.zeros_like(dg_ref)

        dg_ref[...] += jnp.sum(dh * xh, axis=0, keepdims=True)

    row = pl.BlockSpec((tm, D), lambda i: (i, 0))
    vec = pl.BlockSpec((1, D), lambda i: (0, 0))
    return _call(
        order, body, (dproj, wt, x, dx2, g1), name="inproj_bwd", grid=(S // tm,),
        in_specs=[pl.BlockSpec((tm, NP), lambda i: (i, 0)), pl.BlockSpec((NP, D), lambda i: (0, 0)), row, row, vec],
        out_specs=[row, vec],
        out_shape=[jax.ShapeDtypeStruct((S, D), f32), jax.ShapeDtypeStruct((1, D), f32)],
        compiler_params=_params(("arbitrary",)),
    )


HBM = pl.BlockSpec(memory_space=pltpu.HBM)
SEM = pl.BlockSpec(memory_space=pltpu.SEMAPHORE)
SMALL_ROWS = 8


def _comm_call(name, body, bufs, order, sems_in=(), new_sems=(), behind=()):
    nb, ns, nn = len(bufs), len(sems_in), len(new_sems)
    extra = order.token_for(bufs) + list(behind)

    def kern(*refs):
        off = nb + ns + len(extra)
        body(refs[:nb], refs[nb:nb + ns], refs[off:off + nn])
        refs[-1][...] = jnp.zeros((8, 128), f32)

    res = pl.pallas_call(
        kern, name=name,
        in_specs=[HBM] * nb + [SEM] * ns + [pl.BlockSpec(memory_space=pl.ANY)] * len(extra),
        out_specs=[SEM] * nn + [HBM] * nb + [pl.BlockSpec(memory_space=pltpu.VMEM)],
        out_shape=[pltpu.SemaphoreType.DMA((k,)) for k in new_sems] + [pltpu.HBM(b.shape, b.dtype) for b in bufs]
        + [jax.ShapeDtypeStruct((8, 128), f32)],
        input_output_aliases={i: nn + i for i in range(nb)},
        compiler_params=pltpu.CompilerParams(has_side_effects=pltpu.SideEffectType.DATAFLOW_SIDE_EFFECTING),
    )(*[pltpu.with_memory_space_constraint(b, pltpu.HBM) for b in bufs], *sems_in, *extra)
    order.mark(res[-1])
    return list(res[:nn]), list(res[nn:nn + nb])


def _place():
    x, y, c = lax.axis_index("x"), lax.axis_index("y"), lax.axis_index("c")
    chips = [(1 - x, y), (x, 1 - y), (1 - x, 1 - y)]
    return x, y, c, chips


def _rcopy(src, dst, ssem, rsem, dev):
    return pltpu.make_async_remote_copy(src_ref=src, dst_ref=dst, send_sem=ssem, recv_sem=rsem,
                                        device_id=dev, device_id_type=pl.DeviceIdType.MESH)


def _half(nrows, which):
    return pl.ds(which * (nrows // 2), nrows // 2)


def _ici_copies(stack, ssem, rsem, relay):
    x, y, c, chips = _place()
    me_q = 2 * x + y
    sends, recvs = {}, {}
    for a in range(len(stack)):
        rows = _half(stack[a].shape[1], c)
        for j, (cx, cy) in enumerate(chips):
            if relay and a == 0 and j == 2:
                continue
            mine = stack[a].at[me_q, rows]
            sends[a, j] = _rcopy(mine, mine, ssem.at[a * 3 + j], rsem.at[a * 3 + j], (cx, cy, c))
            theirs = stack[a].at[2 * cx + cy, rows]
            recvs[a, j] = _rcopy(theirs, theirs, ssem.at[a * 3 + j], rsem.at[a * 3 + j], (cx, cy, c))
    return sends, recvs


def _relay_copies(win, ssem, rsem):
    x, y, c, chips = _place()
    quarter = win.shape[1] // 4
    sends, recvs = [], []
    for k in range(2):
        rows = pl.ds(c * 2 * quarter + k * quarter, quarter)
        (fx, fy), (tx, ty) = chips[k], chips[1 - k]
        landed = win.at[2 * fx + fy, rows]
        sends.append(_rcopy(landed, landed, ssem.at[k], rsem.at[k], (tx, ty, c)))
        far = win.at[2 * chips[2][0] + chips[2][1], rows]
        recvs.append(_rcopy(far, far, ssem.at[k], rsem.at[k], (tx, ty, c)))
    return sends, recvs


def _allgather_start(name, stacks, order, relay=False):
    n = len(stacks)

    def body(bufs, _, new):
        sends, _r = _ici_copies(bufs, new[0], new[1], relay)
        for cp in sends.values():
            cp.start()

    return _comm_call(name, body, stacks, order, new_sems=(3 * n, 3 * n))


def _forward_copies(stack, ssem, rsem, relay=False):
    x, y, c, chips = _place()
    sib = (x, y, 1 - c)
    sends, recvs = {}, {}
    for a in range(len(stack)):
        for j, (cx, cy) in enumerate(chips):
            if relay and a == 0 and j == 2:
                continue
            landed = stack[a].at[2 * cx + cy, _half(stack[a].shape[1], c)]
            sends[a, j] = _rcopy(landed, landed, ssem.at[a * 3 + j], rsem.at[a * 3 + j], sib)
            other = stack[a].at[2 * cx + cy, _half(stack[a].shape[1], 1 - c)]
            recvs[a, j] = _rcopy(other, other, ssem.at[a * 3 + j], rsem.at[a * 3 + j], sib)
    return sends, recvs


def _far_forward(win, ssem, rsem):
    x, y, c, chips = _place()
    sib, far_q = (x, y, 1 - c), 2 * chips[2][0] + chips[2][1]
    landed, other = win.at[far_q, _half(win.shape[1], c)], win.at[far_q, _half(win.shape[1], 1 - c)]
    return _rcopy(landed, landed, ssem.at[0], rsem.at[0], sib), _rcopy(other, other, ssem.at[0], rsem.at[0], sib)


def _allgather_forward(name, stacks, sems, order, behind=(), relay=False):
    n = len(stacks)

    def body(bufs, taken, new):
        sends, recvs = _ici_copies(bufs, taken[0], taken[1], relay)
        fwd, _r = _forward_copies(bufs, new[0], new[1], relay)
        relay_sends = _relay_copies(bufs[0], new[2], new[3])[0] if relay else []
        for (a, j), arrived in recvs.items():
            arrived.wait_recv()
            fwd[a, j].start()
            if relay and a == 0:
                relay_sends[j].start()
        for cp in sends.values():
            cp.wait_send()

    return _comm_call(name, body, stacks, order, sems_in=sems, behind=behind,
                      new_sems=(3 * n, 3 * n) + ((2, 2) if relay else ()))


def _allgather_finish(name, stacks, sems, order, relay=False):
    def body(bufs, taken, new):
        sends, recvs = _forward_copies(bufs, taken[0], taken[1], relay)
        if relay:
            relay_sends, relay_recvs = _relay_copies(bufs[0], taken[2], taken[3])
            for cp in relay_recvs:
                cp.wait_recv()
            _far_forward(bufs[0], new[0], new[1])[0].start()
            for cp in relay_sends:
                cp.wait_send()
        for cp in sends.values():
            cp.wait_send()
        for cp in recvs.values():
            cp.wait_recv()

    if relay:
        return _comm_call(name, body, stacks, order, sems_in=sems, new_sems=(1, 1))
    return _comm_call(name, body, stacks, order, sems_in=sems)[1]


def _allgather_finish_far(name, stacks, sems, order):
    def body(bufs, taken, _):
        send, recv = _far_forward(bufs[0], taken[0], taken[1])
        send.wait_send()
        recv.wait_recv()

    return _comm_call(name, body, stacks, order, sems_in=sems)[1]


def _window_unit(q, j):
    return C2I[WIN_UNIT0[q] + j]


def _pair_copies(g, t, ssem, rsem, gathered):
    x, y, c, _ = _place()
    sib = (x, y, 1 - c)
    cps, whole = [], []
    for a in range(len(g)):
        if a == 0 and gathered:
            for q in range(NCHIP):
                for j in range(WIN_UNITS // 2):
                    u = jnp.where(c == 0, _window_unit(q, WIN_UNITS // 2 + j), _window_unit(q, j))
                    src = g[0].at[pl.ds(pl.multiple_of(u * UNIT, UNIT), UNIT), :]
                    cps.append(_rcopy(src, t[0].at[q, pl.ds(j * UNIT, UNIT), :], ssem.at[0], rsem.at[0], sib))
            whole.append(_rcopy(t[0], t[0], ssem.at[0], rsem.at[0], sib))
        else:
            cp = _rcopy(g[a].at[:, _half(g[a].shape[1], 1 - c), :], t[a], ssem.at[a], rsem.at[a], sib)
            cps.append(cp)
            whole.append(cp)
    return cps, whole


def _comm_multi(name, parts, order):
    def body(buf_refs, taken, new):
        ib = it = inew = 0
        for pbody, pbufs, psems, pnew, _ in parts:
            pbody(buf_refs[ib:ib + len(pbufs)], taken[it:it + len(psems)], new[inew:inew + len(pnew)])
            ib, it, inew = ib + len(pbufs), it + len(psems), inew + len(pnew)

    sems, bufs = _comm_call(name, body, [b for p in parts for b in p[1]], order,
                            sems_in=[s for p in parts for s in p[2]], new_sems=[k for p in parts for k in p[3]])
    out, ib, inew = [], 0, 0
    for _, pbufs, _, pnew, unpack in parts:
        out.append(unpack(sems[inew:inew + len(pnew)], bufs[ib:ib + len(pbufs)]))
        ib, inew = ib + len(pbufs), inew + len(pnew)
    return out


def _pair_start_part(gs, gathered=False):
    n = len(gs)
    ts = [lax.empty((NCHIP, WIN_ROWS // 2, D) if (a == 0 and gathered) else (NCHIP, g.shape[1] // 2, g.shape[2]), f32)
          for a, g in enumerate(gs)]

    def body(bufs, _, new):
        for cp in _pair_copies(bufs[:n], bufs[n:], new[0], new[1], gathered)[0]:
            cp.start()

    return body, list(gs) + ts, (), (n, n), lambda sems, bufs: (sems, bufs)


def _pair_wait_part(bufs, sems, gathered=False):
    n = len(bufs) // 2

    def body(refs, taken, _):
        for cp in _pair_copies(refs[:n], refs[n:], taken[0], taken[1], gathered)[1]:
            cp.wait_send()
            cp.wait_recv()

    return body, list(bufs), list(sems), (), lambda _, out: (out[:n], out[n:])


ROW_TILE = 256


def _pair_add(order, gs, ts, q_arr, c_arr, name):
    n = len(gs)
    hs = [g.shape[1] // 2 for g in gs]
    nblk = max(1, max(hs) // ROW_TILE)
    assert all(h % (16 * nblk) == 0 for h in hs)

    def body(q_ref, c_ref, *refs):
        for a in range(n):
            s = refs[a][...] + refs[n + a][...]
            refs[3 * n + a][...] = s.astype(bf16)

            @pl.when(pl.program_id(1) == q_ref[0])
            def _():
                refs[2 * n + a][...] = s

    def blk(a, half):
        return pl.BlockSpec((None, hs[a] // nblk, gs[a].shape[2]),
                            lambda i, q, q_ref, c_ref: (q, (c_ref[0] * nblk if half else 0) + i, 0))

    out = _call_indexed(
        order, body, (q_arr, c_arr), list(gs) + list(ts), (nblk, NCHIP),
        [blk(a, True) for a in range(n)] + [blk(a, False) for a in range(n)],
        [pl.BlockSpec((hs[a] // nblk, gs[a].shape[2]), lambda i, q, q_ref, c_ref: (i, 0)) for a in range(n)]
        + [blk(a, False) for a in range(n)],
        name=name,
        out_shape=[jax.ShapeDtypeStruct((hs[a], gs[a].shape[2]), f32) for a in range(n)]
        + [jax.ShapeDtypeStruct((NCHIP, hs[a], gs[a].shape[2]), bf16) for a in range(n)],
        compiler_params=_params(("parallel", "arbitrary")),
    )
    return out[:n], out[n:]


def _pair_add_gathered(order, dwt, t, q_arr, c_arr, name):
    half_units, half_rows = WIN_UNITS // 2, WIN_ROWS // 2
    table = jnp.asarray([_window_unit(q, j) for q in range(NCHIP) for j in range(WIN_UNITS)], jnp.int32)

    def body(tab_ref, q_ref, c_ref, g_hbm, t_ref, own_ref, p16_ref, buf, sem):
        q = pl.program_id(0)

        def gather(w, slot):
            cps = []
            for j in range(half_units):
                u = tab_ref[w * WIN_UNITS + c_ref[0] * half_units + j]
                cps.append(pltpu.make_async_copy(g_hbm.at[pl.ds(pl.multiple_of(u * UNIT, UNIT), UNIT), :],
                                                 buf.at[slot, pl.ds(j * UNIT, UNIT), :], sem.at[slot]))
            return cps

        @pl.when(q == 0)
        def _():
            for cp in gather(0, 0):
                cp.start()

        @pl.when(q + 1 < NCHIP)
        def _():
            for cp in gather(q + 1, (q + 1) % 2):
                cp.start()

        slot = q % 2
        pltpu.make_async_copy(buf.at[slot], buf.at[slot], sem.at[slot]).wait()
        s = buf[slot] + t_ref[...]
        p16_ref[...] = s.astype(bf16)

        @pl.when(q == q_ref[0])
        def _():
            own_ref[...] = s

    blk = pl.BlockSpec((None, half_rows, D), lambda q, tab_ref, q_ref, c_ref: (q, 0, 0))
    return _call_indexed(
        order, body, (table, q_arr, c_arr), (dwt, t), (NCHIP,),
        [pl.BlockSpec(memory_space=pl.ANY), blk],
        [pl.BlockSpec((half_rows, D), lambda q, tab_ref, q_ref, c_ref: (0, 0)), blk],
        scratch_shapes=[pltpu.VMEM((2, half_rows, D), f32), pltpu.SemaphoreType.DMA((2,))],
        name=name,
        out_shape=[jax.ShapeDtypeStruct((half_rows, D), f32),
                   jax.ShapeDtypeStruct((NCHIP, half_rows, D), bf16)],
        compiler_params=_params(("arbitrary",)),
    )


def _shard_copies(p, r, sm, ssem, rsem):
    x, y, c, chips = _place()
    n = len(p)
    sends, recvs = [], []
    for a in range(n):
        for j, (cx, cy) in enumerate(chips):
            k = a * 3 + j
            sends.append(_rcopy(p[a].at[2 * cx + cy], r[a].at[j], ssem.at[k], rsem.at[k], (cx, cy, c)))
            recvs.append(_rcopy(r[a].at[j], r[a].at[j], ssem.at[k], rsem.at[k], (cx, cy, c)))
    if sm is not None:
        mine = sm.at[4 * x + 2 * y + c]
        for i in range(1, 8):
            px = (1 - x) if i & 4 else x
            py = (1 - y) if i & 2 else y
            pc = (1 - c) if i & 1 else c
            k = 3 * n + i - 1
            sends.append(_rcopy(mine, mine, ssem.at[k], rsem.at[k], (px, py, pc)))
            slot = sm.at[4 * px + 2 * py + pc]
            recvs.append(_rcopy(slot, slot, ssem.at[k], rsem.at[k], (px, py, pc)))
    return sends, recvs


def _shard_start_part(p16s, sm=None):
    n = len(p16s)
    rs = [lax.empty((3,) + p.shape[1:], bf16) for p in p16s]
    extra = [] if sm is None else [sm]
    nsem = 3 * n + (7 if sm is not None else 0)

    def body(bufs, _, new):
        sends, _r = _shard_copies(bufs[:n], bufs[n:2 * n], bufs[2 * n] if extra else None, new[0], new[1])
        for cp in sends:
            cp.start()

    return body, list(p16s) + rs + extra, (), (nsem, nsem), lambda sems, bufs: (sems, bufs)


def _shard_wait_part(bufs, sems, n):
    has_sm = len(bufs) > 2 * n

    def body(refs, taken, _):
        sends, recvs = _shard_copies(refs[:n], refs[n:2 * n], refs[2 * n] if has_sm else None, taken[0], taken[1])
        for cp in sends:
            cp.wait_send()
        for cp in recvs:
            cp.wait_recv()

    return body, list(bufs), list(sems), (), lambda _, out: (out[n:2 * n], (out[2 * n] if has_sm else None))


def _shard_sum(order, owns, rs, c_arr, name):
    n = len(owns)
    hs = [o.shape[0] for o in owns]
    nblk = max(1, max(hs) // ROW_TILE)
    assert all(h % (16 * nblk) == 0 for h in hs)
    trs = [h // nblk for h in hs]

    def body(c_ref, *refs):
        for a in range(n):
            s = refs[a][...]
            for j in range(3):
                s = s + refs[n + a][j].astype(f32)
            refs[2 * n + a][...] = s

    out = _call_indexed(
        order, body, (c_arr,), list(owns) + list(rs), (nblk,),
        [pl.BlockSpec((trs[a], owns[a].shape[1]), lambda i, c_ref: (i, 0)) for a in range(n)]
        + [pl.BlockSpec((3, trs[a], owns[a].shape[1]), lambda i, c_ref: (0, i, 0)) for a in range(n)],
        [pl.BlockSpec((trs[a], owns[a].shape[1]), lambda i, c_ref: (c_ref[0] * nblk + i, 0)) for a in range(n)],
        name=name, out_shape=[jax.ShapeDtypeStruct((2 * hs[a], owns[a].shape[1]), f32) for a in range(n)],
        compiler_params=_params(("parallel",)),
    )
    return list(out)


def _swap_copies(full, ssem, rsem):
    x, y, c, _ = _place()
    sends, recvs = [], []
    for a in range(len(full)):
        mine = full[a].at[_half(full[a].shape[0], c)]
        sends.append(_rcopy(mine, mine, ssem.at[a], rsem.at[a], (x, y, 1 - c)))
        other = full[a].at[_half(full[a].shape[0], 1 - c)]
        recvs.append(_rcopy(other, other, ssem.at[a], rsem.at[a], (x, y, 1 - c)))
    return sends, recvs


def _swap_start_part(fulls):
    n = len(fulls)

    def body(bufs, _, new):
        for cp in _swap_copies(bufs, new[0], new[1])[0]:
            cp.start()

    return body, list(fulls), (), (n, n), lambda sems, bufs: (sems, bufs)


def _swap_wait_part(fulls, sems):
    def body(refs, taken, _):
        sends, recvs = _swap_copies(refs, taken[0], taken[1])
        for cp in sends:
            cp.wait_send()
        for cp in recvs:
            cp.wait_recv()

    return body, list(fulls), list(sems), (), lambda _, out: out


def _small_finish(order, sm, ws, ms, vs):
    n = len(ws)

    def body(sm_ref, *refs):
        s = sm_ref[0]
        for d in range(1, 8):
            s = s + sm_ref[d]
        loss_ref, g_refs, upd_refs = refs[3 * n], refs[3 * n + 1:4 * n + 1], refs[4 * n + 1:]
        loss_ref[...] = s[n:n + 1, 0:1]
        for i in range(n):
            g = s[i:i + 1, 0:ws[i].shape[1]]
            g_refs[i][...] = g
            res = _adamw_math(refs[i][...], g, refs[n + i][...], refs[2 * n + i][...])
            for k in range(3):
                upd_refs[3 * i + k][...] = res[k]

    out = _call(order, body, [sm] + list(ws) + list(ms) + list(vs), name="small_sum_adamw",
                out_shape=[jax.ShapeDtypeStruct((1, 1), f32)] + [jax.ShapeDtypeStruct(w.shape, f32) for w in ws]
                + [jax.ShapeDtypeStruct(w.shape, f32) for w in ws for _ in range(3)])
    return out[0], out[1:n + 1], [out[n + 1 + 3 * i:n + 4 + 3 * i] for i in range(n)]


def _adamw_math(w, g, m, v):
    m = ADAM_B1 * m + (1.0 - ADAM_B1) * g
    v = ADAM_B2 * v + (1.0 - ADAM_B2) * (g * g)
    m_hat = m / (1.0 - ADAM_B1 ** ADAM_STEP)
    v_hat = v / (1.0 - ADAM_B2 ** ADAM_STEP)
    return -ADAM_LR * (m_hat / (jnp.sqrt(v_hat) + ADAM_EPS) + ADAM_WD * w), m, v


def _adamw(order, ws, gs, ms, vs, name):
    n = len(ws)
    nblk = max(1, max(w.shape[0] for w in ws) // ROW_TILE)
    assert all(w.shape[0] % (8 * nblk) == 0 for w in ws)

    def body(*refs):
        for a in range(n):
            w_ref, g_ref, m_ref, v_ref = (refs[k * n + a] for k in range(4))
            d_ref, nm_ref, nv_ref, g_out = refs[4 * n + 4 * a:4 * n + 4 * a + 4]
            g = g_ref[...]
            g_out[...] = g
            d_ref[...], nm_ref[...], nv_ref[...] = _adamw_math(w_ref[...], g, m_ref[...], v_ref[...])

    blks = [pl.BlockSpec((w.shape[0] // nblk, w.shape[1]), lambda i: (i, 0)) for w in ws]
    out = _call(
        order, body, list(ws) + list(gs) + list(ms) + list(vs), name=name, grid=(nblk,), in_specs=blks * 4,
        out_specs=[b for b in blks for _ in range(4)],
        out_shape=[jax.ShapeDtypeStruct(w.shape, f32) for w in ws for _ in range(4)],
        compiler_params=_params(("parallel",)),
    )
    return [out[4 * a:4 * a + 4] for a in range(n)]


def _feature_rows(w):
    return jnp.transpose(w, (2, 0, 1))


WIN_STEP = 128
WIN_PIECE = 3 * WIN_STEP


def _window_stacks(order, w, q_arr):
    n_piece = -(-SHARD_IN // WIN_PIECE)
    steps = WIN_ROWS // WIN_STEP
    head, tail = UNIT, WIN_STEP
    assert max(OWN_ROW0) < head and OWN_ROW0[1] + FA_AT == UNIT and WIN_ROWS - tail <= SHARD_IN - N_FA
    rest = SHARD_IN - (n_piece - 1) * WIN_PIECE

    def body(q_ref, w_ref, win_ref, fa_ref, buf, fabuf, sem):
        i = pl.program_id(0)
        q = q_ref[0]
        chip1 = q == 1
        row0 = jnp.where(q == 0, OWN_ROW0[0], jnp.where(chip1, OWN_ROW0[1], jnp.where(q == 2, OWN_ROW0[2], OWN_ROW0[3])))
        skip = jnp.where(chip1, N_FA, 0)

        def copy(src0, dst0, n, slot):
            return pltpu.make_async_copy(w_ref.at[pl.ds(src0, n)], buf.at[pl.ds(dst0, n)], sem.at[slot])

        def piece(j, on_chip1):
            if j == 0 and on_chip1:
                return [copy(0, OWN_ROW0[1], FA_AT, 0),
                        copy(FA_AT + N_FA, UNIT, WIN_PIECE - FA_AT, n_piece)]
            if j == 0:
                return [copy(0, row0, WIN_PIECE, 0)]
            if j == n_piece - 1:
                n = rest - (N_FA if on_chip1 else 0)
                return [copy(SHARD_IN - n, row0 + SHARD_IN - skip - n, n, j)]
            return [copy(j * WIN_PIECE + skip, row0 + j * WIN_PIECE, WIN_PIECE, j)]

        def both(j, act):
            if 0 < j < n_piece - 1:
                for c in piece(j, False):
                    act(c)
                return
            for on_chip1 in (False, True):
                @pl.when(chip1 if on_chip1 else jnp.logical_not(chip1))
                def _():
                    for c in piece(j, on_chip1):
                        act(c)

        fa_copy = pltpu.make_async_copy(w_ref.at[pl.ds(FA_AT, N_FA)], fabuf.at[pl.ds(0, N_FA)], sem.at[n_piece + 1])

        @pl.when(i == 0)
        def _():
            buf[pl.ds(0, head)] = jnp.zeros((head, 1, D), f32)
            buf[pl.ds(WIN_ROWS - tail, tail)] = jnp.zeros((tail, 1, D), f32)
            fabuf[pl.ds(N_FA, FA_ROWS - N_FA)] = jnp.zeros((FA_ROWS - N_FA, 1, D), f32)
            fa_copy.start()
            for j in range(n_piece):
                both(j, lambda c: c.start())
            fa_copy.wait()
            fa_ref[...] = fabuf[...].reshape(FA_ROWS, D).astype(bf16)

        for j in range(n_piece):
            @pl.when(i == j * (WIN_PIECE // WIN_STEP))
            def _():
                both(j, lambda c: c.wait())

        win_ref[...] = buf[pl.ds(pl.multiple_of(i * WIN_STEP, WIN_STEP), WIN_STEP)].reshape(WIN_STEP, D).astype(bf16)

    return _call_indexed(
        order, body, (q_arr,), (w,), (steps,), [pl.BlockSpec(memory_space=pl.ANY)],
        [pl.BlockSpec((None, WIN_STEP, D), lambda i, q: (q[0], i, 0)),
         pl.BlockSpec((None, FA_ROWS, D), lambda i, q: (q[0], 0, 0))],
        scratch_shapes=[pltpu.VMEM((WIN_ROWS, 1, D), f32), pltpu.VMEM((FA_ROWS, 1, D), f32),
                        pltpu.SemaphoreType.DMA((n_piece + 2,))],
        name="window_w_in", out_shape=[jax.ShapeDtypeStruct((NCHIP, WIN_ROWS, D), bf16),
                                       jax.ShapeDtypeStruct((NCHIP, FA_ROWS, D), bf16)],
        compiler_params=_params(("arbitrary",)),
    )


def _unfeature_rows(a):
    return jnp.transpose(a, (1, 2, 0))


ADAM_IN_ROWS = 134
ADAM_IN_STEPS = SHARD_IN // ADAM_IN_ROWS
ADAM_IN_CHUNK = 136
ADAM_IN_CHUNKS = ADAM_IN_STEPS + 1
ADAM_IN_BUF = WIN_ROWS + N_FA


def _adamw_w_in(order, w, gwin, gfa, m, v, q_arr):
    assert ADAM_IN_CHUNK * ADAM_IN_STEPS < WIN_ROWS <= ADAM_IN_CHUNK * ADAM_IN_CHUNKS
    assert OWN_ROW0[NCHIP - 1] + ADAM_IN_ROWS <= 2 * ADAM_IN_CHUNK and ADAM_IN_CHUNK >= ADAM_IN_ROWS
    last0 = ADAM_IN_CHUNK * ADAM_IN_STEPS
    cut = OWN_ROW0[1] + FA_AT

    def body(q_ref, w_ref, gwin_ref, gfa_ref, m_ref, v_ref, go_ref, d_ref, nm_ref, nv_ref, buf, sem):
        i = pl.program_id(0)
        q = q_ref[0]
        chip1 = q == 1
        shift = jnp.where(chip1, N_FA, 0)

        def copy(src_ref, src0, dst0, n, slot):
            return pltpu.make_async_copy(src_ref.at[pl.ds(src0, n)], buf.at[pl.ds(dst0, n), 0], sem.at[slot])

        def first(on_chip1):
            if on_chip1:
                return [copy(gwin_ref, 0, 0, cut, 0), copy(gfa_ref, 0, cut, N_FA, ADAM_IN_CHUNKS),
                        copy(gwin_ref, cut, cut + N_FA, ADAM_IN_CHUNK - cut - N_FA, ADAM_IN_CHUNKS + 1)]
            return [copy(gwin_ref, 0, 0, ADAM_IN_CHUNK, 0)]

        def middle(k):
            return [copy(gwin_ref, pl.multiple_of(k * ADAM_IN_CHUNK - shift, 8), k * ADAM_IN_CHUNK, ADAM_IN_CHUNK, k)]

        def last(on_chip1):
            n = WIN_ROWS - last0 + (N_FA if on_chip1 else 0)
            return [copy(gwin_ref, WIN_ROWS - n, last0, n, ADAM_IN_STEPS)]

        def both(make, act):
            for on_chip1 in (False, True):
                @pl.when(chip1 if on_chip1 else jnp.logical_not(chip1))
                def _():
                    for c in make(on_chip1):
                        act(c)

        @pl.when(i == 0)
        def _():
            both(first, lambda c: c.start())
            for k in range(1, ADAM_IN_STEPS):
                middle(k)[0].start()
            both(last, lambda c: c.start())
            both(first, lambda c: c.wait())

        @pl.when(i < ADAM_IN_STEPS - 1)
        def _():
            middle(i + 1)[0].wait()

        @pl.when(i == ADAM_IN_STEPS - 1)
        def _():
            both(last, lambda c: c.wait())

        row0 = jnp.where(q == 0, OWN_ROW0[0], jnp.where(chip1, OWN_ROW0[1], jnp.where(q == 2, OWN_ROW0[2], OWN_ROW0[3])))
        g = buf[pl.ds(row0 + i * ADAM_IN_ROWS, ADAM_IN_ROWS)]
        go_ref[...] = g
        d_ref[...], nm_ref[...], nv_ref[...] = _adamw_math(w_ref[...], g, m_ref[...], v_ref[...])

    blk = pl.BlockSpec((ADAM_IN_ROWS, 1, D), lambda i, q: (i, 0, 0))
    hbm = pl.BlockSpec(memory_space=pl.ANY)
    return _call_indexed(
        order, body, (q_arr,), (w, gwin, gfa, m, v), (ADAM_IN_STEPS,), [blk, hbm, hbm, blk, blk], [blk] * 4,
        scratch_shapes=[pltpu.VMEM((ADAM_IN_BUF, 1, D), f32), pltpu.SemaphoreType.DMA((ADAM_IN_CHUNKS + 2,))],
        name="adamw_w_in", out_shape=[jax.ShapeDtypeStruct((SHARD_IN, 1, D), f32)] * 4,
        compiler_params=_params(("arbitrary",)),
    )


def kernel(x, norm_attn_g, w_in, b_forget, w_branch_a, w_branch_b, w_out, norm_mlp_g, w_up, w_down, norm_final_g, loss_target, m_norm_attn_g, m_w_in, m_b_forget, m_w_branch_a, m_w_branch_b, m_w_out, m_norm_mlp_g, m_w_up, m_w_down, m_norm_final_g, v_norm_attn_g, v_w_in, v_b_forget, v_w_branch_a, v_w_branch_b, v_w_out, v_norm_mlp_g, v_w_up, v_w_down, v_norm_final_g):
    xi, yi, ci = lax.axis_index("x"), lax.axis_index("y"), lax.axis_index("c")
    q_me = 2 * xi + yi
    c_arr = jnp.reshape(ci, (1,)).astype(jnp.int32)
    q_arr = jnp.reshape(q_me, (1,)).astype(jnp.int32)
    x_, tgt = x[0], loss_target[0]

    names = ["w_branch_a", "w_branch_b", "w_out", "w_up", "w_down"]
    big = dict(zip(names, [w_branch_a[0], w_branch_b[0], w_out[0], w_up[0], w_down[0]]))
    ms = dict(zip(names, [m_w_branch_a[0], m_w_branch_b[0], m_w_out[0], m_w_up[0], m_w_down[0]]))
    vs = dict(zip(names, [v_w_branch_a[0], v_w_branch_b[0], v_w_out[0], v_w_up[0], v_w_down[0]]))
    grad, upd = {}, {}
    order = _Order()

    def run(fn, *args, **kw):
        return fn(order, *args, **kw)

    def own_slot(a):
        return lax.dynamic_update_slice(lax.empty((NCHIP,) + a.shape, a.dtype), a[None], (q_me, 0, 0))

    sem_in, in_s = _allgather_start("allgather_start_in", run(_window_stacks, _feature_rows(w_in), q_arr), order,
                                    relay=True)
    sem_rest, rest = _allgather_start("allgather_start_rest", [own_slot(w.astype(bf16)) for w in big.values()], order)
    rope = _rope_tables(order.tok[0, 0])
    sem_f, in_s = _allgather_forward("allgather_forward_in", in_s, sem_in, order, behind=rope, relay=True)
    sem_f, in_s = _allgather_finish("allgather_finish_in", in_s, sem_f, order, relay=True)
    wins, fas = _allgather_finish_far("allgather_finish_far_in", in_s, sem_f, order)
    wt = run(_assemble_win, wins, fas)

    bpad = jnp.pad(b_forget, ((0, 0), (0, 120)))
    h1, qkvb, qkva, gates, fa = run(_norm_inproj, x_, norm_attn_g, wt, rope)
    F = run(_forget_cumsum, fa, bpad)
    oa, lsea = run(_fox_fwd, qkva, F)
    sem_f, rest = _allgather_forward("allgather_forward_rest", rest, sem_rest, order)
    ob, lseb = run(_dil_fwd, qkvb)
    was, wbs, wouts, wups, wdowns = _allgather_finish("allgather_finish_rest", rest, sem_f, order)
    wout = wouts.reshape(D, D)
    wdown = wdowns.reshape(DFF, D)
    ya, yb, mixed = run(_branch_mix, oa, ob, was, wbs, gates)
    x2, h2 = run(_outproj_norm, mixed, wout, x_, norm_mlp_g)
    u, a = run(_mlp_up, h2, wups)
    dx3, dx3b, dg3, loss_part = run(_mlp_down_loss, a, wdown, x2, norm_final_g.reshape(1, D), tgt)

    def comm(name, *parts):
        return _comm_multi(name, list(parts), order)

    def adamw_group(group, fulls, name):
        res = run(_adamw, [big[nm] for nm in group], fulls, [ms[nm] for nm in group], [vs[nm] for nm in group], name)
        for nm, r in zip(group, res):
            *upd[nm], grad[nm] = r

    grp_a, grp_b, grp_c = ["w_down", "w_up"], ["w_out", "w_branch_a", "w_branch_b"], ["w_in", "w_in_fa"]
    du = run(_mlp_down_bwd, dx3b, wdown, u)
    dwdown = run(_mm, a, dx3b, "tn", f32, 1024, D, "wgrad_down")
    dwup = run(_mm, h2, du, "tn", f32, D, 1024, "wgrad_up", stack_cols=True)
    ((sem_pa, buf_pa),) = comm("pair_start_a", _pair_start_part([dwdown.reshape(NCHIP, DFF // NCHIP, D), dwup]))
    dx2, dx2b, dg2 = run(_mlp_up_bwd, du, wups, x2, dx3, norm_mlp_g)
    ((gs, ts),) = comm("pair_wait_a", _pair_wait_part(buf_pa, sem_pa))
    p32_a, p16_a = run(_pair_add, gs, ts, q_arr, c_arr, "pair_add_a")
    ((sem_sa, buf_sa),) = comm("shard_start_a", _shard_start_part(p16_a))
    dya, dyb, dproj = run(_gate_bwd, dx2b, wout, gates, ya, yb)
    dwout = run(_mm, mixed, dx2b, "tn", f32, D, D, "wgrad_out")
    doa, dob = run(_branch_bwd, dya, dyb, was, wbs)
    dwas, dwbs = run(_branch_wgrad, oa, ob, dya, dyb)
    ((sem_pb, buf_pb),) = comm("pair_start_b", _pair_start_part([dwout.reshape(NCHIP, D // NCHIP, D), dwas, dwbs]))
    dF, dproj = run(_fox_bwd, qkva, doa, oa, lsea, F, dproj)
    (gs, ts), (rs_a, _) = comm("pair_wait_b_shard_wait_a", _pair_wait_part(buf_pb, sem_pb),
                               _shard_wait_part(buf_sa, sem_sa, len(grp_a)))
    p32_b, p16_b = run(_pair_add, gs, ts, q_arr, c_arr, "pair_add_b")
    fulls_a = run(_shard_sum, p32_a, rs_a, c_arr, "shard_sum_a")
    (sem_wa, fulls_a), (sem_sb, buf_sb) = comm("swap_start_a_shard_start_b", _swap_start_part(fulls_a),
                                               _shard_start_part(p16_b))
    dbf, dproj = run(_forget_bwd, dF, fa, bpad, dproj)
    dproj = run(_dil_bwd, qkvb, dob, ob, lseb, rope, dproj)
    (rs_b, _), fulls_a = comm("shard_wait_b_swap_wait_a", _shard_wait_part(buf_sb, sem_sb, len(grp_b)),
                              _swap_wait_part(fulls_a, sem_wa))
    fulls_b = run(_shard_sum, p32_b, rs_b, c_arr, "shard_sum_b")
    ((sem_wb, fulls_b),) = comm("swap_start_b", _swap_start_part(fulls_b))
    dwt = run(_mm, dproj, h1, "tn", f32, 512, D, "wgrad_in")
    dwfa = jnp.broadcast_to(dwt[F_FA:F_FA + FA_ROWS][None], (NCHIP, FA_ROWS, D))
    (sem_pc, buf_pc), fulls_b = comm("pair_start_c_swap_wait_b", _pair_start_part([dwt, dwfa], gathered=True),
                                     _swap_wait_part(fulls_b, sem_wb))
    adamw_group(grp_b, fulls_b, "adamw_b")
    (((dwt_c, dwfa_c), (t_in, t_fa)),) = comm("pair_wait_c", _pair_wait_part(buf_pc, sem_pc, gathered=True))
    p32_in, p16_in = run(_pair_add_gathered, dwt_c, t_in, q_arr, c_arr, "pair_add_w_in")
    p32_fa, p16_fa = run(_pair_add, [dwfa_c], [t_fa], q_arr, c_arr, "pair_add_w_in_fa")
    ((sem_sc, buf_sc),) = comm("shard_start_c", _shard_start_part([p16_in, *p16_fa]))
    gx, dg1 = run(_inproj_bwd, dproj, wt, x_, dx2, norm_attn_g)
    adamw_group(grp_a, fulls_a, "adamw_a")
    small = jnp.concatenate([dg1, dg2, dg3, jnp.pad(dbf[:, 0:8], ((0, 0), (0, D - 8))),
                             jnp.pad(loss_part, ((0, 0), (0, D - 128))),
                             jnp.zeros((SMALL_ROWS - 5, D), f32)], axis=0)
    sm = lax.dynamic_update_slice(lax.empty((8, SMALL_ROWS, D), f32), small[None],
                                  (4 * xi + 2 * yi + ci, 0, 0))
    (sem_sm, buf_sm), (rs_c, _) = comm("small_start_shard_wait_c", _shard_start_part([], sm),
                                       _shard_wait_part(buf_sc, sem_sc, len(grp_c)))
    fulls_c = (run(_shard_sum, [p32_in], rs_c[0:1], c_arr, "shard_sum_w_in")
               + run(_shard_sum, p32_fa, rs_c[1:2], c_arr, "shard_sum_w_in_fa"))
    (sem_wc, fulls_c), (_, sm) = comm("swap_start_c_small_wait", _swap_start_part(fulls_c),
                                      _shard_wait_part(buf_sm, sem_sm, 0))
    smalls = ["norm_attn_g", "norm_mlp_g", "norm_final_g", "b_forget"]
    loss, gs, res = run(_small_finish, sm, [norm_attn_g, norm_mlp_g, norm_final_g.reshape(1, D), b_forget],
                        [m_norm_attn_g, m_norm_mlp_g, m_norm_final_g.reshape(1, D), m_b_forget],
                        [v_norm_attn_g, v_norm_mlp_g, v_norm_final_g.reshape(1, D), v_b_forget])
    loss = loss.reshape(())
    grad.update(zip(smalls, gs))
    upd.update(zip(smalls, res))

    ((gwin, gfa),) = comm("swap_wait_c", _swap_wait_part(fulls_c, sem_wc))
    res_in = run(_adamw_w_in, _feature_rows(w_in), gwin, gfa, _feature_rows(m_w_in), _feature_rows(v_w_in), q_arr)
    grad["w_in"] = _unfeature_rows(res_in[0])
    upd["w_in"] = [_unfeature_rows(t) for t in res_in[1:]]

    order_out = ["norm_attn_g", "w_in", "b_forget", "w_branch_a", "w_branch_b", "w_out", "norm_mlp_g", "w_up",
                 "w_down", "norm_final_g"]
    shapes = dict(norm_attn_g=norm_attn_g.shape, w_in=w_in.shape, b_forget=b_forget.shape,
                  w_branch_a=w_branch_a.shape, w_branch_b=w_branch_b.shape, w_out=w_out.shape,
                  norm_mlp_g=norm_mlp_g.shape, w_up=w_up.shape, w_down=w_down.shape, norm_final_g=norm_final_g.shape)
    outs = [loss, gx.reshape(x.shape)]
    outs += [grad[nm].reshape(shapes[nm]) for nm in order_out]
    for k in range(3):
        outs += [upd[nm][k].reshape(shapes[nm]) for nm in order_out]
    return tuple(outs)
```

```python
import jax
import jax.numpy as jnp
from jax import lax
from jax.experimental import pallas as pl
from jax.experimental.pallas import tpu as pltpu

f32 = jnp.float32
bf16 = jnp.bfloat16

S = 2048
D = 1024
DFF = 4096
HD = 64
FOXW = 512
DILOUT = 256
DIL = (1, 4, 16)
BAND = 128
EPS = 1e-6
NEG = -1e30
ROPE_THETA = 500000.0
NCHIP = 4
TQ = 256

ADAM_LR, ADAM_B1, ADAM_B2, ADAM_EPS, ADAM_WD, ADAM_STEP = 0.001, 0.9, 0.999, 1e-08, 0.01, 10
VMEM_LIMIT = 56 * 1024 * 1024

UNIT = 64
NP = 6144
F_DIL, F_FOX, F_FA, F_G = 0, 2304, 3840, 4096
DIL_BLK, FOX_BLK = 1152, 384
WIN_UNITS, WIN_ROWS = 24, 1536
WIN_UNIT0 = (0, 23, 45, 68)
OWN_ROW0 = (0, 2, 60, 62)
SHARD_IN = 1474
N_FA = 8
FA_AT = 1536 - SHARD_IN
FA_ROWS = 32


def _compact_to_internal():
    c2i = {}
    for p in range(2):
        for role in range(3):
            for g in range(3):
                for hh in range(2):
                    c2i[24 + 12 * role + 4 * g + 2 * p + hh] = 18 * p + 6 * role + 2 * g + hh
    for p in range(4):
        for role in range(3):
            for hh in range(2):
                c2i[8 * role + 2 * p + hh] = F_FOX // UNIT + 6 * p + 2 * role + hh
    for j in range(32):
        c2i[60 + j] = F_G // UNIT + j
    return c2i


C2I = _compact_to_internal()
OVERLAP_UNITS = (23, 45, 46, 68)


def _params(sem=None):
    return pltpu.CompilerParams(dimension_semantics=sem, vmem_limit_bytes=VMEM_LIMIT)


class _Order:
    def __init__(self):
        self.tok = None

    def mark(self, v):
        self.tok = v

    def token_for(self, args):
        return [] if self.tok is None or any(self.tok is a for a in args) else [self.tok]


def _call(order, body, args, in_specs=None, **kw):
    args = list(args)
    n_in = len(args)
    if in_specs is None:
        in_specs = [pl.BlockSpec(memory_space=pltpu.VMEM)] * n_in
    kern = body
    extra = order.token_for(args)
    if extra:
        in_specs = list(in_specs) + [pl.BlockSpec(memory_space=pl.ANY)]

        def kern(*refs):
            body(*refs[:n_in], *refs[n_in + 1:])

    out = pl.pallas_call(kern, in_specs=in_specs, **kw)(*args, *extra)
    order.mark(out[0] if isinstance(out, (tuple, list)) else out)
    return out


def _call_indexed(order, body, scalars, args, grid, in_specs, out_specs, scratch_shapes=(), **kw):
    args, in_specs = list(args), list(in_specs)
    n_front = len(scalars) + len(args)
    kern = body
    extra = order.token_for(args)
    if extra:
        in_specs.append(pl.BlockSpec(memory_space=pl.ANY))

        def kern(*refs):
            body(*refs[:n_front], *refs[n_front + 1:])

    out = pl.pallas_call(
        kern, grid_spec=pltpu.PrefetchScalarGridSpec(num_scalar_prefetch=len(scalars), grid=grid, in_specs=in_specs,
                                                     out_specs=out_specs, scratch_shapes=scratch_shapes),
        **kw)(*scalars, *args, *extra)
    order.mark(out[0] if isinstance(out, (tuple, list)) else out)
    return out


def _dot(a, b):
    return jnp.dot(a, b, preferred_element_type=f32)


def _dot_nt(a, b):
    return lax.dot_general(a, b, (((1,), (1,)), ((), ())), preferred_element_type=f32)


def _dot_tn(a, b):
    return lax.dot_general(a, b, (((0,), (0,)), ((), ())), preferred_element_type=f32)


def _split3(x):
    hi = x.astype(bf16)
    r1 = x - hi.astype(f32)
    mid = r1.astype(bf16)
    lo = (r1 - mid.astype(f32)).astype(bf16)
    return hi, mid, lo


def _rope_tables(after):
    half = 8
    inv_freq = jnp.power(jnp.float32(ROPE_THETA), -jnp.arange(half, dtype=f32) * 2.0 / 16)
    ang = (jnp.arange(S).astype(f32) + after)[:, None] * inv_freq[None, :]
    cos, sin = jnp.cos(ang), jnp.sin(ang)
    one = jnp.ones((S, HD - 16), f32)
    zero = jnp.zeros((S, HD - 16), f32)
    z8 = jnp.zeros((S, 8), f32)
    c = jnp.concatenate([cos, cos, one], axis=1)
    s1 = jnp.concatenate([-sin, z8, zero], axis=1)
    s2 = jnp.concatenate([z8, sin, zero], axis=1)
    return tuple(jnp.concatenate([t, t], axis=1) for t in (c, s1, s2))


def _mm(order, a, b, mode, out_dtype, tm, tn, name, stack_cols=False):
    if mode == "nn":
        (M, K), (_, N) = a.shape, b.shape
        a_spec = pl.BlockSpec((tm, K), lambda i, j: (i, 0))
        b_spec = pl.BlockSpec((K, tn), lambda i, j: (0, j))
        dot = _dot
    elif mode == "nt":
        (M, K), (N, _) = a.shape, b.shape
        a_spec = pl.BlockSpec((tm, K), lambda i, j: (i, 0))
        b_spec = pl.BlockSpec((tn, K), lambda i, j: (j, 0))
        dot = _dot_nt
    else:
        (K, M), (_, N) = a.shape, b.shape
        a_spec = pl.BlockSpec((K, tm), lambda i, j: (0, i))
        b_spec = pl.BlockSpec((K, tn), lambda i, j: (0, j))
        dot = _dot_tn

    def body(a_ref, b_ref, o_ref):
        o_ref[...] = dot(a_ref[...], b_ref[...]).astype(out_dtype)

    if stack_cols:
        assert tm == M
        out_spec = pl.BlockSpec((None, tm, tn), lambda i, j: (j, 0, 0))
        out_shape = jax.ShapeDtypeStruct((N // tn, M, tn), out_dtype)
    else:
        out_spec = pl.BlockSpec((tm, tn), lambda i, j: (i, j))
        out_shape = jax.ShapeDtypeStruct((M, N), out_dtype)
    return _call(
        order, body, (a, b), name=name, grid=(M // tm, N // tn), in_specs=[a_spec, b_spec],
        out_specs=out_spec, out_shape=out_shape,
        compiler_params=_params(("parallel", "parallel")),
    )


def _assemble_win(order, wins, fas):
    def body(win_ref, fa_ref, o_ref):
        q = pl.program_id(0)

        @pl.when(q == 0)
        def _():
            o_ref[...] = jnp.zeros_like(o_ref)

        for k in range(NCHIP):
            @pl.when(q == k)
            def _(k=k):
                for j in range(WIN_UNITS):
                    cu = WIN_UNIT0[k] + j
                    dst = pl.ds(C2I[cu] * UNIT, UNIT)
                    if cu in OVERLAP_UNITS:
                        o_ref[dst, :] += win_ref[j * UNIT:(j + 1) * UNIT, :]
                    else:
                        o_ref[dst, :] = win_ref[j * UNIT:(j + 1) * UNIT, :]
                if k == 1:
                    o_ref[F_FA:F_FA + FA_ROWS, :] = fa_ref[...]

    return _call(
        order, body, (wins, fas), name="assemble_w_in", grid=(NCHIP,),
        in_specs=[pl.BlockSpec((None, WIN_ROWS, D), lambda q: (q, 0, 0)),
                  pl.BlockSpec((None, FA_ROWS, D), lambda q: (1, 0, 0))],
        out_specs=pl.BlockSpec((NP, D), lambda q: (0, 0)),
        out_shape=jax.ShapeDtypeStruct((NP, D), bf16),
        compiler_params=_params(("arbitrary",)),
    )


def _norm_inproj(order, x, g1, wt, rope):
    tm = 256
    c_t, s1_t, s2_t = rope

    def body(x_ref, g_ref, w_ref, c_ref, s1_ref, s2_ref, h_ref, qkvb_ref, qkva_ref, gates_ref, fa_ref):
        xb = x_ref[...]
        r = lax.rsqrt(jnp.mean(xb * xb, axis=-1, keepdims=True) + EPS)
        h = ((xb * r) * g_ref[...]).astype(bf16)
        h_ref[...] = h
        c, s1, s2 = c_ref[...], s1_ref[...], s2_ref[...]
        for p in range(2):
            pb = _dot_nt(h, w_ref[F_DIL + p * DIL_BLK:F_DIL + (p + 1) * DIL_BLK, :])
            for ch in range(DIL_BLK // 128):
                pc = pb[:, ch * 128:(ch + 1) * 128]
                if ch < 6:
                    pc = pc * c + pltpu.roll(pc, 120, 1) * s1 + pltpu.roll(pc, 8, 1) * s2
                qkvb_ref[:, p * DIL_BLK + ch * 128:p * DIL_BLK + (ch + 1) * 128] = pc
        qkva_ref[...] = _dot_nt(h, w_ref[F_FOX:F_FA, :]).astype(bf16)
        fa_ref[...] = _dot_nt(h, w_ref[F_FA:F_FA + 128, :])
        gates_ref[...] = _dot_nt(h, w_ref[F_G:NP, :]).astype(bf16)

    row = lambda w: pl.BlockSpec((tm, w), lambda i: (i, 0))
    return _call(
        order, body, (x, g1, wt, c_t, s1_t, s2_t), name="norm_inproj", grid=(S // tm,),
        in_specs=[row(D), pl.BlockSpec((1, D), lambda i: (0, 0)), pl.BlockSpec((NP, D), lambda i: (0, 0)),
                  row(128), row(128), row(128)],
        out_specs=[row(D), row(2 * DIL_BLK), row(4 * FOX_BLK), row(2 * D), row(128)],
        out_shape=[jax.ShapeDtypeStruct((S, D), bf16), jax.ShapeDtypeStruct((S, 2 * DIL_BLK), f32),
                   jax.ShapeDtypeStruct((S, 4 * FOX_BLK), bf16), jax.ShapeDtypeStruct((S, 2 * D), bf16),
                   jax.ShapeDtypeStruct((S, 128), f32)],
        compiler_params=_params(("parallel",)),
    )


def _forget_cumsum(order, fa, bpad):
    nb = S // TQ

    def body(fa_ref, b_ref, F_ref):
        rr = lax.broadcasted_iota(jnp.int32, (TQ, TQ), 0)
        cc = lax.broadcasted_iota(jnp.int32, (TQ, TQ), 1)
        tri = (rr >= cc).astype(bf16)
        lane = lax.broadcasted_iota(jnp.int32, (1, 128), 1)
        carry = jnp.zeros((1, 128), f32)
        for b in range(nb):
            z = fa_ref[b * TQ:(b + 1) * TQ, :] + b_ref[...]
            lf = jnp.minimum(z, 0.0) - jnp.log(1.0 + jnp.exp(-jnp.abs(z)))
            lf = jnp.where(lane < 8, lf, 0.0)
            hi, mid, lo = _split3(lf)
            fb = (_dot(tri, hi) + _dot(tri, mid)) + _dot(tri, lo) + carry
            F_ref[b * TQ:(b + 1) * TQ, :] = fb
            carry = fb[TQ - 1:TQ, :]

    return _call(
        order, body, (fa, bpad), name="forget_cumsum",
        out_shape=jax.ShapeDtypeStruct((S, 128), f32),
        compiler_params=_params(),
    )


def _head_masks():
    lane = lax.broadcasted_iota(jnp.int32, (1, 128), 1)
    return lane, (lane < HD, lane >= HD)


L_ONE = 3
FOX_TQ, FOX_TK = 256, 512


def _set_lanes(x, lane, first, cols):
    for n, col in enumerate(cols):
        x = jnp.where(lane == first + n, col, x)
    return x


def _f32_parts(col):
    return [t.astype(f32) for t in _split3(col)]


def _fox_operands(qkv_ref, F_ref, lse_ref, qa, ka, p, rows):
    lane, hm = _head_masks()
    q = qkv_ref[rows, 0:128].astype(f32) * 0.125
    k = qkv_ref[rows, 128:256].astype(f32)
    Fb = F_ref[rows, :]
    for hh in (0, 1):
        free = (1 - hh) * HD
        fcol = jnp.sum(jnp.where(lane == 2 * p + hh, Fb, 0.0), axis=1, keepdims=True)
        qterm = fcol if lse_ref is None else fcol - lse_ref[rows, hh * HD:hh * HD + 1]
        qcols = _f32_parts(qterm) + [1.0] * 3
        kcols = [1.0] * 3 + [-t for t in _f32_parts(fcol)]
        qa[hh, rows, :] = _set_lanes(jnp.where(hm[hh], q, 0.0), lane, free, qcols).astype(bf16)
        ka[hh, rows, :] = _set_lanes(k, lane, free, kcols).astype(bf16)


def _fox_fwd(order, qkva, F):
    tq, tk = FOX_TQ, FOX_TK

    def body(qkv_ref, F_ref, o_ref, lse_ref, qa, ka, vt):
        p = pl.program_id(0)
        keyi = lax.broadcasted_iota(jnp.int32, (tk, 1), 0)
        qryi = lax.broadcasted_iota(jnp.int32, (1, tq), 1)
        sub = lax.broadcasted_iota(jnp.int32, (128, 1), 0)

        def prep(i, c):
            rows = pl.ds(pl.multiple_of(i * tk, tk), tk)
            _fox_operands(qkv_ref, F_ref, None, qa, ka, p, rows)
            vt[i] = qkv_ref[rows, 256:384].astype(f32).T.astype(bf16)
            return c

        lax.fori_loop(0, S // tk, prep, 0)

        def qblock(i, first_half):
            r0 = pl.multiple_of(i * tq, tq)
            qh = [qa[hh, pl.ds(r0, tq), :] for hh in (0, 1)]

            def kv(jb, carry, masked, width):
                keys = pl.ds(pl.multiple_of(jb * tk, tk), width)
                sts = [_dot_nt(ka[hh, keys, :], qh[hh]) for hh in (0, 1)]
                new = []
                for hh in (0, 1):
                    m, l, a = carry[3 * hh:3 * hh + 3]
                    st = sts[hh]
                    if masked:
                        st = jnp.where(jb * tk + keyi[0:width] <= r0 + qryi, st, NEG)
                    mn = jnp.maximum(m, jnp.max(st, axis=0, keepdims=True))
                    al = jnp.exp(m - mn)
                    pt = jnp.exp(st - mn)
                    l = al * l + jnp.sum(pt, axis=0, keepdims=True)
                    a = al * a + _dot(vt[jb, hh * HD:(hh + 1) * HD, 0:width], pt.astype(bf16))
                    new += [mn, l, a]
                return tuple(new)

            init = (jnp.full((1, tq), NEG, f32), jnp.zeros((1, tq), f32), jnp.zeros((HD, tq), f32)) * 2
            last = (r0 + tq - 1) // tk
            carry = lax.fori_loop(0, last, lambda j, cr: kv(j, cr, False, tk), init)
            m0, l0, a0, m1, l1, a1 = kv(last, carry, True, tk // 2 if first_half else tk)
            ot = jnp.concatenate([a0 / l0, a1 / l1], axis=0)
            lt = jnp.where(sub < HD, m0 + jnp.log(l0), m1 + jnp.log(l1))
            o_ref[pl.ds(r0, tq), :] = ot.T.astype(bf16)
            lse_ref[pl.ds(r0, tq), :] = lt.T

        def qpair(t, c):
            qblock(2 * t, True)
            qblock(2 * t + 1, False)
            return c

        assert tk == 2 * tq
        lax.fori_loop(0, S // tk, qpair, 0)

    pair = pl.BlockSpec((S, 128), lambda p: (0, p))
    return _call(
        order, body, (qkva, F), name="fox_fwd", grid=(4,),
        in_specs=[pl.BlockSpec((S, FOX_BLK), lambda p: (0, p)), pl.BlockSpec((S, 128), lambda p: (0, 0))],
        out_specs=[pair, pair],
        out_shape=[jax.ShapeDtypeStruct((S, FOXW), bf16), jax.ShapeDtypeStruct((S, FOXW), f32)],
        scratch_shapes=[pltpu.VMEM((2, S, 128), bf16)] * 2 + [pltpu.VMEM((S // tk, 128, tk), bf16)],
        compiler_params=_params(("parallel",)),
    )


def _permute_in(dst, src, r):
    L = S // r
    for rho in range(r):
        dst[rho * L:(rho + 1) * L, :] = src[pl.ds(rho, L, stride=r), :]


def _permute_out(dst, src, r):
    L = S // r
    for rho in range(r):
        dst[pl.ds(rho, L, stride=r), :] = src[rho * L:(rho + 1) * L, :]


def _band_width(nbl):
    return BAND if nbl == 1 else 2 * BAND


def _band_geometry(bb, nbl):
    r0 = pl.multiple_of(bb * BAND, BAND)
    if nbl == 1:
        k0 = r0
    else:
        k0 = pl.multiple_of(jnp.maximum(bb - 1, 0) * BAND, BAND)
    sub0 = (bb - lax.rem(bb, nbl)) * BAND
    qi = r0 + lax.broadcasted_iota(jnp.int32, (BAND, 1), 0)
    ki = k0 + lax.broadcasted_iota(jnp.int32, (1, _band_width(nbl)), 1)
    diff = qi - ki
    valid = (diff >= 0) & (diff <= BAND) & (ki >= sub0)
    return r0, k0, valid


def _dil_views(ref):
    return [[ref.at[:, pl.ds((3 * role + g) * 128, 128)] for g in range(3)] for role in range(3)]


DIL_UNROLL = 4


def _dil_in_specs():
    return [pl.BlockSpec((S, 128), lambda p, k=k: (0, 9 * p + k)) for k in range(9)]


def _dil_fwd(order, qkvb):
    def body(*refs):
        q_refs, k_refs, v_refs = refs[0:3], refs[3:6], refs[6:9]
        ob_ref, lse_ref, qp, kp, vp, op, lp = refs[9:16]
        on, ln = refs[16:19], refs[19:22]
        _, hm = _head_masks()
        for g, r in enumerate(DIL):
            nbl = S // r // BAND
            if r == 1:
                qs_, ks_, vs_, od, ld = q_refs[g], k_refs[g], v_refs[g], on[g], ln[g]
            else:
                _permute_in(qp, q_refs[g], r)
                _permute_in(kp, k_refs[g], r)
                _permute_in(vp, v_refs[g], r)
                qs_, ks_, vs_, od, ld = qp, kp, vp, op, lp

            def blk(t, c, qs_=qs_, ks_=ks_, vs_=vs_, od=od, ld=ld, nbl=nbl):
                work = []
                for u in range(DIL_UNROLL):
                    r0, k0, valid = _band_geometry(DIL_UNROLL * t + u, nbl)
                    q = qs_[pl.ds(r0, BAND), :] * 0.125
                    kw = ks_[pl.ds(k0, _band_width(nbl)), :].astype(bf16)
                    vw = vs_[pl.ds(k0, _band_width(nbl)), :]
                    for hh in (0, 1):
                        qh = jnp.where(hm[hh], q, 0.0).astype(bf16)
                        work.append((u, hh, r0, valid, vw, _dot_nt(qh, kw)))
                o = [jnp.zeros((BAND, 128), f32)] * DIL_UNROLL
                lse = [jnp.zeros((BAND, 128), f32)] * DIL_UNROLL
                for u, hh, r0, valid, vw, s in work:
                    s = jnp.where(valid, s, NEG)
                    m = jnp.max(s, axis=1, keepdims=True)
                    pr = jnp.exp(s - m)
                    l = jnp.sum(pr, axis=1, keepdims=True)
                    vm = jnp.where(hm[hh], vw, 0.0).astype(bf16)
                    o[u] = o[u] + _dot((pr / l).astype(bf16), vm)
                    lse[u] = jnp.where(hm[hh], m + jnp.log(l), lse[u])
                    if hh == 1:
                        od[pl.ds(r0, BAND), :] = o[u]
                        ld[pl.ds(r0, BAND), :] = lse[u]
                return c

            lax.fori_loop(0, S // BAND // DIL_UNROLL, blk, 0)
            if r != 1:
                _permute_out(on[g], op, r)
                _permute_out(ln[g], lp, r)

        def combine(i, c):
            r0 = pl.multiple_of(i * TQ, TQ)
            ls = [ln[g][pl.ds(r0, TQ), :] for g in range(3)]
            mx = jnp.maximum(jnp.maximum(ls[0], ls[1]), ls[2])
            es = [jnp.exp(l - mx) for l in ls]
            tot = (es[0] + es[1]) + es[2]
            acc = (es[0] / tot) * on[0][pl.ds(r0, TQ), :]
            acc = acc + (es[1] / tot) * on[1][pl.ds(r0, TQ), :]
            acc = acc + (es[2] / tot) * on[2][pl.ds(r0, TQ), :]
            ob_ref[pl.ds(r0, TQ), :] = acc.astype(bf16)
            lse_ref[pl.ds(r0, TQ), :] = mx + jnp.log(tot)
            return c

        lax.fori_loop(0, S // TQ, combine, 0)

    out_blk = pl.BlockSpec((S, 128), lambda p: (0, p))
    return _call(
        order, body, [qkvb] * 9, name="dil_fwd", grid=(2,),
        in_specs=_dil_in_specs(), out_specs=[out_blk, out_blk],
        out_shape=[jax.ShapeDtypeStruct((S, DILOUT), bf16), jax.ShapeDtypeStruct((S, DILOUT), f32)],
        scratch_shapes=[pltpu.VMEM((S, 128), f32)] * 11,
        compiler_params=_params(("parallel",)),
    )


def _branch_mix(order, oa, ob, was, wbs, gates):
    tm = 512

    def body(oa_ref, ob_ref, wa_ref, wb_ref, g_ref, ya_ref, yb_ref, mix_ref):
        oa_b, ob_b = oa_ref[...], ob_ref[...]
        for q in range(NCHIP):
            cols = slice(q * 256, (q + 1) * 256)
            ya = _dot(oa_b, wa_ref[q])
            yb = _dot(ob_b, wb_ref[q])
            ya_ref[:, cols] = ya.astype(bf16)
            yb_ref[:, cols] = yb.astype(bf16)
            ga = g_ref[:, q * 256:(q + 1) * 256].astype(f32)
            gb = g_ref[:, D + q * 256:D + (q + 1) * 256].astype(f32)
            mix_ref[:, cols] = (jax.nn.sigmoid(ga) * ya + jax.nn.sigmoid(gb) * yb).astype(bf16)

    row = lambda w: pl.BlockSpec((tm, w), lambda i: (i, 0))
    full3 = lambda a: pl.BlockSpec(a.shape, lambda i: (0, 0, 0))
    return _call(
        order, body, (oa, ob, was, wbs, gates), name="branch_mix", grid=(S // tm,),
        in_specs=[row(FOXW), row(DILOUT), full3(was), full3(wbs), row(2 * D)],
        out_specs=[row(D), row(D), row(D)],
        out_shape=[jax.ShapeDtypeStruct((S, D), bf16), jax.ShapeDtypeStruct((S, D), bf16),
                   jax.ShapeDtypeStruct((S, D), bf16)],
        compiler_params=_params(("parallel",)),
    )


def _outproj_norm(order, mixed, wout, x, g2):
    tm = 512

    def body(m_ref, w_ref, x_ref, g_ref, x2_ref, h2_ref):
        x2 = x_ref[...] + _dot(m_ref[...], w_ref[...])
        x2_ref[...] = x2
        r = lax.rsqrt(jnp.mean(x2 * x2, axis=-1, keepdims=True) + EPS)
        h2_ref[...] = ((x2 * r) * g_ref[...]).astype(bf16)

    row = pl.BlockSpec((tm, D), lambda i: (i, 0))
    return _call(
        order, body, (mixed, wout, x, g2), name="outproj_norm", grid=(S // tm,),
        in_specs=[row, pl.BlockSpec((D, D), lambda i: (0, 0)), row, pl.BlockSpec((1, D), lambda i: (0, 0))],
        out_specs=[row, row],
        out_shape=[jax.ShapeDtypeStruct((S, D), f32), jax.ShapeDtypeStruct((S, D), bf16)],
        compiler_params=_params(("parallel",)),
    )


def _mlp_up(order, h2, wups):
    tm = 1024

    def body(h_ref, w_ref, ru_ref, a_ref):
        ru = jnp.maximum(_dot(h_ref[...], w_ref[...]), 0.0)
        ru_ref[...] = ru.astype(bf16)
        a_ref[...] = (ru * ru).astype(bf16)

    out = pl.BlockSpec((tm, D), lambda q, i: (i, q))
    return _call(
        order, body, (h2, wups), name="mlp_up", grid=(NCHIP, S // tm),
        in_specs=[pl.BlockSpec((tm, D), lambda q, i: (i, 0)), pl.BlockSpec((None, D, D), lambda q, i: (q, 0, 0))],
        out_specs=[out, out],
        out_shape=[jax.ShapeDtypeStruct((S, DFF), bf16), jax.ShapeDtypeStruct((S, DFF), bf16)],
        compiler_params=_params(("parallel", "parallel")),
    )


def _mlp_down_loss(order, a, wdown, x2, g3, tgt):
    tm = 512

    def body(a_ref, w_ref, x2_ref, g_ref, t_ref, dx_ref, dxb_ref, dg_ref, loss_ref):
        i = pl.program_id(0)
        x3 = x2_ref[...] + _dot(a_ref[...], w_ref[...])
        r = lax.rsqrt(jnp.mean(x3 * x3, axis=-1, keepdims=True) + EPS)
        xh = x3 * r
        g = g_ref[...]
        e = xh * g - t_ref[...]
        part = 0.5 * jnp.sum(jnp.mean(e * e, axis=-1, keepdims=True), axis=0, keepdims=True)
        dy = e * (1.0 / D)
        gdy = dy * g
        dx = r * (gdy - xh * jnp.mean(gdy * xh, axis=-1, keepdims=True))
        dx_ref[...] = dx
        dxb_ref[...] = dx.astype(bf16)

        @pl.when(i == 0)
        def _():
            dg_ref[...] = jnp.zeros_like(dg_ref)
            loss_ref[...] = jnp.zeros_like(loss_ref)

        dg_ref[...] += jnp.sum(dy * xh, axis=0, keepdims=True)
        loss_ref[...] += jnp.broadcast_to(part, (1, 128))

    row = pl.BlockSpec((tm, D), lambda i: (i, 0))
    vec = pl.BlockSpec((1, D), lambda i: (0, 0))
    return _call(
        order, body, (a, wdown, x2, g3, tgt), name="mlp_down_loss", grid=(S // tm,),
        in_specs=[pl.BlockSpec((tm, DFF), lambda i: (i, 0)), pl.BlockSpec((DFF, D), lambda i: (0, 0)), row, vec, row],
        out_specs=[row, row, vec, pl.BlockSpec((1, 128), lambda i: (0, 0))],
        out_shape=[jax.ShapeDtypeStruct((S, D), f32), jax.ShapeDtypeStruct((S, D), bf16),
                   jax.ShapeDtypeStruct((1, D), f32), jax.ShapeDtypeStruct((1, 128), f32)],
        compiler_params=_params(("arbitrary",)),
    )


def _mlp_down_bwd(order, dx3b, wdown, u):
    tm = 512

    def body(d_ref, w_ref, u_ref, du_ref):
        d = d_ref[...]
        for q in range(NCHIP):
            cols = slice(q * D, (q + 1) * D)
            da = _dot_nt(d, w_ref[cols, :])
            du_ref[:, cols] = (da * (2.0 * u_ref[:, cols].astype(f32))).astype(bf16)

    return _call(
        order, body, (dx3b, wdown, u), name="mlp_down_bwd", grid=(S // tm,),
        in_specs=[pl.BlockSpec((tm, D), lambda i: (i, 0)), pl.BlockSpec((DFF, D), lambda i: (0, 0)),
                  pl.BlockSpec((tm, DFF), lambda i: (i, 0))],
        out_specs=pl.BlockSpec((tm, DFF), lambda i: (i, 0)),
        out_shape=jax.ShapeDtypeStruct((S, DFF), bf16),
        compiler_params=_params(("parallel",)),
    )


def _mlp_up_bwd(order, du, wups, x2, dx3, g2):
    tm = 512

    def body(du_ref, w_ref, x2_ref, dx3_ref, g_ref, dx2_ref, dx2b_ref, dg_ref):
        i = pl.program_id(0)
        dh = jnp.zeros((tm, D), f32)
        for q in range(NCHIP):
            dh = dh + _dot_nt(du_ref[:, q * D:(q + 1) * D], w_ref[q])
        x2 = x2_ref[...]
        r = lax.rsqrt(jnp.mean(x2 * x2, axis=-1, keepdims=True) + EPS)
        xh = x2 * r
        gdh = dh * g_ref[...]
        dx2 = dx3_ref[...] + r * (gdh - xh * jnp.mean(gdh * xh, axis=-1, keepdims=True))
        dx2_ref[...] = dx2
        dx2b_ref[...] = dx2.astype(bf16)

        @pl.when(i == 0)
        def _():
            dg_ref[...] = jnp.zeros_like(dg_ref)

        dg_ref[...] += jnp.sum(dh * xh, axis=0, keepdims=True)

    row = pl.BlockSpec((tm, D), lambda i: (i, 0))
    vec = pl.BlockSpec((1, D), lambda i: (0, 0))
    return _call(
        order, body, (du, wups, x2, dx3, g2), name="mlp_up_bwd", grid=(S // tm,),
        in_specs=[pl.BlockSpec((tm, DFF), lambda i: (i, 0)), pl.BlockSpec((NCHIP, D, D), lambda i: (0, 0, 0)),
                  row, row, vec],
        out_specs=[row, row, vec],
        out_shape=[jax.ShapeDtypeStruct((S, D), f32), jax.ShapeDtypeStruct((S, D), bf16),
                   jax.ShapeDtypeStruct((1, D), f32)],
        compiler_params=_params(("arbitrary",)),
    )


def _gate_bwd(order, dx2b, wout, gates, ya, yb):
    tm = 512

    def body(d_ref, w_ref, g_ref, ya_ref, yb_ref, dya_ref, dyb_ref, dproj_ref):
        dm = _dot_nt(d_ref[...], w_ref[...])
        sa = jax.nn.sigmoid(g_ref[:, 0:D].astype(f32))
        sb = jax.nn.sigmoid(g_ref[:, D:2 * D].astype(f32))
        dya_ref[...] = (dm * sa).astype(bf16)
        dyb_ref[...] = (dm * sb).astype(bf16)
        dproj_ref[:, 0:D] = (dm * ya_ref[...].astype(f32) * (sa * (1.0 - sa))).astype(bf16)
        dproj_ref[:, D:2 * D] = (dm * yb_ref[...].astype(f32) * (sb * (1.0 - sb))).astype(bf16)

    row = lambda w: pl.BlockSpec((tm, w), lambda i: (i, 0))
    return _call(
        order, body, (dx2b, wout, gates, ya, yb), name="gate_bwd", grid=(S // tm,),
        in_specs=[row(D), pl.BlockSpec((D, D), lambda i: (0, 0)), row(2 * D), row(D), row(D)],
        out_specs=[row(D), row(D), pl.BlockSpec((tm, 2 * D), lambda i: (i, F_G // (2 * D)))],
        out_shape=[jax.ShapeDtypeStruct((S, D), bf16), jax.ShapeDtypeStruct((S, D), bf16),
                   jax.ShapeDtypeStruct((S, NP), bf16)],
        compiler_params=_params(("parallel",)),
    )


def _branch_bwd(order, dya, dyb, was, wbs):
    tm = 512

    def body(dya_ref, dyb_ref, wa_ref, wb_ref, doa_ref, dob_ref):
        doa = jnp.zeros((tm, FOXW), f32)
        dob = jnp.zeros((tm, DILOUT), f32)
        for q in range(NCHIP):
            cols = slice(q * 256, (q + 1) * 256)
            doa = doa + _dot_nt(dya_ref[:, cols], wa_ref[q])
            dob = dob + _dot_nt(dyb_ref[:, cols], wb_ref[q])
        doa_ref[...] = doa.astype(bf16)
        dob_ref[...] = dob

    row = lambda w: pl.BlockSpec((tm, w), lambda i: (i, 0))
    full3 = lambda a: pl.BlockSpec(a.shape, lambda i: (0, 0, 0))
    return _call(
        order, body, (dya, dyb, was, wbs), name="branch_bwd", grid=(S // tm,),
        in_specs=[row(D), row(D), full3(was), full3(wbs)],
        out_specs=[row(FOXW), row(DILOUT)],
        out_shape=[jax.ShapeDtypeStruct((S, FOXW), bf16), jax.ShapeDtypeStruct((S, DILOUT), f32)],
        compiler_params=_params(("parallel",)),
    )


def _branch_wgrad(order, oa, ob, dya, dyb):
    def body(oa_ref, ob_ref, dya_ref, dyb_ref, dwa_ref, dwb_ref):
        dwa_ref[...] = _dot_tn(oa_ref[...], dya_ref[...])
        dwb_ref[...] = _dot_tn(ob_ref[...], dyb_ref[...])

    full = lambda w: pl.BlockSpec((S, w), lambda q: (0, 0))
    colq = pl.BlockSpec((S, 256), lambda q: (0, q))
    return _call(
        order, body, (oa, ob, dya, dyb), name="branch_wgrad", grid=(NCHIP,),
        in_specs=[full(FOXW), full(DILOUT), colq, colq],
        out_specs=[pl.BlockSpec((None, FOXW, 256), lambda q: (q, 0, 0)),
                   pl.BlockSpec((None, DILOUT, 256), lambda q: (q, 0, 0))],
        out_shape=[jax.ShapeDtypeStruct((NCHIP, FOXW, 256), f32), jax.ShapeDtypeStruct((NCHIP, DILOUT, 256), f32)],
        compiler_params=_params(("parallel",)),
    )


def _fox_bwd(order, qkva, doa, oa, lse, F, dproj):
    tq, tk = FOX_TQ, FOX_TK

    def body(qkv_ref, do_ref, o_ref, lse_ref, F_ref, _dproj_in, dF_ref, dqkv_ref, qa, ka, da, va, kat,
             dk_scr, dv_scr, dqt_scr):
        p = pl.program_id(0)
        lane, hm = _head_masks()
        keyi = lax.broadcasted_iota(jnp.int32, (tk, 1), 0)
        qryi = lax.broadcasted_iota(jnp.int32, (1, tq), 1)

        def prep(i, c):
            rows = pl.ds(pl.multiple_of(i * tk, tk), tk)
            _fox_operands(qkv_ref, F_ref, lse_ref, qa, ka, p, rows)
            do = do_ref[rows, :].astype(f32)
            prod = do * o_ref[rows, :].astype(f32)
            v = qkv_ref[rows, 256:384].astype(f32)
            for hh in (0, 1):
                free = (1 - hh) * HD
                delta = jnp.sum(jnp.where(hm[hh], prod, 0.0), axis=1, keepdims=True)
                da[hh, rows, :] = _set_lanes(jnp.where(hm[hh], do, 0.0), lane, free,
                                             [-t for t in _f32_parts(delta)]).astype(bf16)
                va[hh, rows, :] = _set_lanes(v, lane, free, [1.0] * 3).astype(bf16)
                kat[hh, i] = ka[hh, rows, :].astype(f32).T.astype(bf16)
                dk_scr[hh, rows, :] = jnp.zeros((tk, 128), f32)
                dv_scr[hh, rows, :] = jnp.zeros((tk, 128), f32)
            return c

        lax.fori_loop(0, S // tk, prep, 0)

        def qblock(i, first_half):
            r0 = pl.multiple_of(i * tq, tq)
            qrows = pl.ds(r0, tq)
            qh = [qa[hh, qrows, :] for hh in (0, 1)]
            dh = [da[hh, qrows, :] for hh in (0, 1)]
            dqt_scr[...] = jnp.zeros_like(dqt_scr)

            def kv(jb, c2, masked, width):
                keys = pl.ds(pl.multiple_of(jb * tk, tk), width)
                sts = [_dot_nt(ka[hh, keys, :], qh[hh]) for hh in (0, 1)]
                dps = [_dot_nt(va[hh, keys, :], dh[hh]) for hh in (0, 1)]
                for hh in (0, 1):
                    pt = jnp.exp(sts[hh])
                    if masked:
                        pt = jnp.where(jb * tk + keyi[0:width] <= r0 + qryi, pt, 0.0)
                    dsb = (pt * dps[hh]).astype(bf16)
                    dv_scr[hh, keys, :] += _dot(pt.astype(bf16), dh[hh])
                    dk_scr[hh, keys, :] += _dot(dsb, qh[hh])
                    dqt_scr[hh] += _dot(kat[hh, jb, :, 0:width], dsb)
                return c2

            last = (r0 + tq - 1) // tk
            lax.fori_loop(0, last, lambda j, c2: kv(j, c2, False, tk), 0)
            kv(last, 0, True, tk // 2 if first_half else tk)
            dq0, dq1 = dqt_scr[0].T, dqt_scr[1].T
            dqkv_ref[qrows, 0:128] = (jnp.where(hm[0], dq0, dq1) * 0.125).astype(bf16)
            dF_ref[qrows, :] = jnp.where(lane == 0, dq0[:, HD:HD + 1], jnp.where(lane == 1, dq1[:, 0:1], 0.0))

        def qpair(t, c):
            qblock(2 * t, True)
            qblock(2 * t + 1, False)
            return c

        assert tk == 2 * tq
        lax.fori_loop(0, S // tk, qpair, 0)

        def finish(i, c):
            rows = pl.ds(pl.multiple_of(i * tq, tq), tq)
            dk0, dk1 = dk_scr[0, rows, :], dk_scr[1, rows, :]
            dqkv_ref[rows, 128:256] = jnp.where(hm[0], dk0, dk1).astype(bf16)
            dqkv_ref[rows, 256:384] = jnp.where(hm[0], dv_scr[0, rows, :], dv_scr[1, rows, :]).astype(bf16)
            cs = jnp.where(lane == 0, dk0[:, HD + L_ONE:HD + L_ONE + 1],
                           jnp.where(lane == 1, dk1[:, L_ONE:L_ONE + 1], 0.0))
            dF_ref[rows, :] = dF_ref[rows, :] - cs
            return c

        lax.fori_loop(0, S // tq, finish, 0)

    pair = pl.BlockSpec((S, 128), lambda p: (0, p))
    return _call(
        order, body, (qkva, doa, oa, lse, F, dproj), name="fox_bwd", grid=(4,),
        in_specs=[pl.BlockSpec((S, FOX_BLK), lambda p: (0, p)), pair, pair, pair,
                  pl.BlockSpec((S, 128), lambda p: (0, 0)), pl.BlockSpec(memory_space=pl.ANY)],
        out_specs=[pair, pl.BlockSpec((S, FOX_BLK), lambda p: (0, F_FOX // FOX_BLK + p))],
        out_shape=[jax.ShapeDtypeStruct((S, FOXW), f32), jax.ShapeDtypeStruct((S, NP), bf16)],
        input_output_aliases={5: 1},
        scratch_shapes=[pltpu.VMEM((2, S, 128), bf16)] * 4 + [pltpu.VMEM((2, S // tk, 128, tk), bf16)]
        + [pltpu.VMEM((2, S, 128), f32)] * 2 + [pltpu.VMEM((2, 128, tq), f32)],
        compiler_params=_params(("parallel",)),
    )


def _forget_bwd(order, dF, fa, bpad, dproj):
    nb = S // TQ

    def body(dF_ref, fa_ref, b_ref, _dproj_in, db_ref, dfa_ref):
        rr = lax.broadcasted_iota(jnp.int32, (TQ, TQ), 0)
        cc = lax.broadcasted_iota(jnp.int32, (TQ, TQ), 1)
        upper = (cc >= rr).astype(bf16)
        lane = lax.broadcasted_iota(jnp.int32, (1, 128), 1)
        carry = jnp.zeros((1, 128), f32)
        db = jnp.zeros((1, 128), f32)
        for b in reversed(range(nb)):
            cols = jnp.zeros((TQ, 128), f32)
            for h in range(8):
                c0 = (h // 2) * 128 + h % 2
                cols = jnp.where(lane == h, dF_ref[b * TQ:(b + 1) * TQ, c0:c0 + 1], cols)
            dlf = carry
            for part in _split3(cols):
                dlf = dlf + _dot(upper, part)
            carry = carry + jnp.sum(cols, axis=0, keepdims=True)
            z = fa_ref[b * TQ:(b + 1) * TQ, :] + b_ref[...]
            dz = jnp.where(lane < 8, dlf * jax.nn.sigmoid(-z), 0.0)
            dfa_ref[b * TQ:(b + 1) * TQ, 0:128] = dz.astype(bf16)
            dfa_ref[b * TQ:(b + 1) * TQ, 128:256] = jnp.zeros((TQ, 128), bf16)
            db = db + jnp.sum(dz, axis=0, keepdims=True)
        db_ref[...] = db

    whole = lambda a: pl.BlockSpec(a.shape, lambda i: (0,) * a.ndim)
    return _call(
        order, body, (dF, fa, bpad, dproj), name="forget_bwd", grid=(1,),
        in_specs=[whole(dF), whole(fa), whole(bpad), pl.BlockSpec(memory_space=pl.ANY)],
        out_specs=[pl.BlockSpec((1, 128), lambda i: (0, 0)), pl.BlockSpec((S, 256), lambda i: (0, F_FA // 256))],
        out_shape=[jax.ShapeDtypeStruct((1, 128), f32), jax.ShapeDtypeStruct((S, NP), bf16)],
        input_output_aliases={3: 1},
        compiler_params=_params(("arbitrary",)),
    )


def _dil_bwd(order, qkvb, dob, ob, lseb, rope, dproj):
    c_t, s1_t, s2_t = rope

    def body(*refs):
        q_refs, k_refs, v_refs = refs[0:3], refs[3:6], refs[6:9]
        dob_ref, ob_ref, lse_ref, c_ref, s1_ref, s2_ref, _dproj_in, dqkv_ref = refs[9:17]
        qp, kp, vp, dop, lp, dlp, dln, dqp, dkp, dvp, nat = refs[17:28]
        dq_out, dk_out, dv_out = _dil_views(dqkv_ref)
        _, hm = _head_masks()

        def delta_rows(i, c):
            r0 = pl.multiple_of(i * TQ, TQ)
            prod = dob_ref[pl.ds(r0, TQ), :] * ob_ref[pl.ds(r0, TQ), :].astype(f32)
            d0 = jnp.sum(jnp.where(hm[0], prod, 0.0), axis=1, keepdims=True)
            d1 = jnp.sum(jnp.where(hm[1], prod, 0.0), axis=1, keepdims=True)
            dln[pl.ds(r0, TQ), :] = jnp.where(hm[0], d0, d1)
            return c

        lax.fori_loop(0, S // TQ, delta_rows, 0)

        for g, r in enumerate(DIL):
            nbl = S // r // BAND
            if r == 1:
                srcs = (q_refs[g], k_refs[g], v_refs[g], dob_ref, lse_ref, dln)
            else:
                for dst, src in ((qp, q_refs[g]), (kp, k_refs[g]), (vp, v_refs[g]), (dop, dob_ref),
                                 (lp, lse_ref), (dlp, dln)):
                    _permute_in(dst, src, r)
                srcs = (qp, kp, vp, dop, lp, dlp)
            dkp[...] = jnp.zeros_like(dkp)
            dvp[...] = jnp.zeros_like(dvp)

            def blk(t, c, srcs=srcs, nbl=nbl):
                qs_, ks_, vs_, dos_, ls_, dls_ = srcs
                work = []
                for u in range(DIL_UNROLL):
                    r0, k0, valid = _band_geometry(DIL_UNROLL * t + u, nbl)
                    q = qs_[pl.ds(r0, BAND), :] * 0.125
                    kwf = ks_[pl.ds(k0, _band_width(nbl)), :]
                    kw = kwf.astype(bf16)
                    vw = vs_[pl.ds(k0, _band_width(nbl)), :].astype(bf16)
                    do = dos_[pl.ds(r0, BAND), :]
                    lse = ls_[pl.ds(r0, BAND), :]
                    dlt = dls_[pl.ds(r0, BAND), :]
                    for hh in (0, 1):
                        qh = jnp.where(hm[hh], q, 0.0).astype(bf16)
                        doh = jnp.where(hm[hh], do, 0.0).astype(bf16)
                        kh = jnp.where(hm[hh], kwf, 0.0).astype(bf16)
                        work.append((u, hh, r0, k0, valid, qh, doh, kh, lse[:, hh * HD:hh * HD + 1],
                                     dlt[:, hh * HD:hh * HD + 1], _dot_nt(qh, kw), _dot_nt(doh, vw)))
                for u, hh, r0, k0, valid, qh, doh, kh, lse_h, dlt_h, s, dp in work:
                    if hh == 0:
                        dq = jnp.zeros((BAND, 128), f32)
                        dk = jnp.zeros((_band_width(nbl), 128), f32)
                        dv = jnp.zeros((_band_width(nbl), 128), f32)
                    pr = jnp.where(valid, jnp.exp(s - lse_h), 0.0)
                    dsb = (pr * (dp - dlt_h)).astype(bf16)
                    dv = dv + _dot_tn(pr.astype(bf16), doh)
                    dk = dk + _dot_tn(dsb, qh)
                    dq = dq + _dot(dsb, kh)
                    if hh == 1:
                        dqp[pl.ds(r0, BAND), :] = dq * 0.125
                        dkp[pl.ds(k0, _band_width(nbl)), :] += dk
                        dvp[pl.ds(k0, _band_width(nbl)), :] += dv
                return c

            lax.fori_loop(0, S // BAND // DIL_UNROLL, blk, 0)

            for acc, out, roped in ((dqp, dq_out[g], True), (dkp, dk_out[g], True), (dvp, dv_out[g], False)):
                if r == 1:
                    src = acc
                else:
                    _permute_out(nat, acc, r)
                    src = nat

                def emit(i, c, src=src, out=out, roped=roped):
                    r0 = pl.multiple_of(i * TQ, TQ)
                    d = src[pl.ds(r0, TQ), :]
                    if roped:
                        d = (d * c_ref[pl.ds(r0, TQ), :] + pltpu.roll(d * s1_ref[pl.ds(r0, TQ), :], 8, 1)
                             + pltpu.roll(d * s2_ref[pl.ds(r0, TQ), :], 120, 1))
                    out[pl.ds(r0, TQ), :] = d.astype(bf16)
                    return c

                lax.fori_loop(0, S // TQ, emit, 0)

    pair = pl.BlockSpec((S, 128), lambda p: (0, p))
    tab = pl.BlockSpec((S, 128), lambda p: (0, 0))
    blk_spec = pl.BlockSpec((S, DIL_BLK), lambda p: (0, p))
    return _call(
        order, body, [qkvb] * 9 + [dob, ob, lseb, c_t, s1_t, s2_t, dproj], name="dil_bwd", grid=(2,),
        in_specs=_dil_in_specs() + [pair, pair, pair, tab, tab, tab, pl.BlockSpec(memory_space=pl.ANY)],
        out_specs=blk_spec,
        out_shape=jax.ShapeDtypeStruct((S, NP), bf16),
        input_output_aliases={15: 0},
        scratch_shapes=[pltpu.VMEM((S, 128), f32)] * 11,
        compiler_params=_params(("parallel",)),
    )


def _inproj_bwd(order, dproj, wt, x, dx2, g1):
    tm = 256

    def body(d_ref, w_ref, x_ref, dx2_ref, g_ref, dx_ref, dg_ref):
        i = pl.program_id(0)
        dh = _dot(d_ref[...], w_ref[...])
        xb = x_ref[...]
        r = lax.rsqrt(jnp.mean(xb * xb, axis=-1, keepdims=True) + EPS)
        xh = xb * r
        gdh = dh * g_ref[...]
        dx_ref[...] = dx2_ref[...] + r * (gdh - xh * jnp.mean(gdh * xh, axis=-1, keepdims=True))

        @pl.when(i == 0)
        def _():
            dg_ref[...] = jnp.zeros_like(dg_ref)

        dg_ref[...] += jnp.sum(dh * xh, axis=0, keepdims=True)

    row = pl.BlockSpec((tm, D), lambda i: (i, 0))
    vec = pl.BlockSpec((1, D), lambda i: (0, 0))
    return _call(
        order, body, (dproj, wt, x, dx2, g1), name="inproj_bwd", grid=(S // tm,),
        in_specs=[pl.BlockSpec((tm, NP), lambda i: (i, 0)), pl.BlockSpec((NP, D), lambda i: (0, 0)), row, row, vec],
        out_specs=[row, vec],
        out_shape=[jax.ShapeDtypeStruct((S, D), f32), jax.ShapeDtypeStruct((1, D), f32)],
        compiler_params=_params(("arbitrary",)),
    )


HBM = pl.BlockSpec(memory_space=pltpu.HBM)
SEM = pl.BlockSpec(memory_space=pltpu.SEMAPHORE)
SMALL_ROWS = 8


def _comm_call(name, body, bufs, order, sems_in=(), new_sems=(), behind=()):
    nb, ns, nn = len(bufs), len(sems_in), len(new_sems)
    extra = order.token_for(bufs) + list(behind)

    def kern(*refs):
        off = nb + ns + len(extra)
        body(refs[:nb], refs[nb:nb + ns], refs[off:off + nn])
        refs[-1][...] = jnp.zeros((8, 128), f32)

    res = pl.pallas_call(
        kern, name=name,
        in_specs=[HBM] * nb + [SEM] * ns + [pl.BlockSpec(memory_space=pl.ANY)] * len(extra),
        out_specs=[SEM] * nn + [HBM] * nb + [pl.BlockSpec(memory_space=pltpu.VMEM)],
        out_shape=[pltpu.SemaphoreType.DMA((k,)) for k in new_sems] + [pltpu.HBM(b.shape, b.dtype) for b in bufs]
        + [jax.ShapeDtypeStruct((8, 128), f32)],
        input_output_aliases={i: nn + i for i in range(nb)},
        compiler_params=pltpu.CompilerParams(has_side_effects=pltpu.SideEffectType.DATAFLOW_SIDE_EFFECTING),
    )(*[pltpu.with_memory_space_constraint(b, pltpu.HBM) for b in bufs], *sems_in, *extra)
    order.mark(res[-1])
    return list(res[:nn]), list(res[nn:nn + nb])


def _place():
    x, y, c = lax.axis_index("x"), lax.axis_index("y"), lax.axis_index("c")
    chips = [(1 - x, y), (x, 1 - y), (1 - x, 1 - y)]
    return x, y, c, chips


def _rcopy(src, dst, ssem, rsem, dev):
    return pltpu.make_async_remote_copy(src_ref=src, dst_ref=dst, send_sem=ssem, recv_sem=rsem,
                                        device_id=dev, device_id_type=pl.DeviceIdType.MESH)


def _half(nrows, which):
    return pl.ds(which * (nrows // 2), nrows // 2)


def _ici_copies(stack, ssem, rsem, relay):
    x, y, c, chips = _place()
    me_q = 2 * x + y
    sends, recvs = {}, {}
    for a in range(len(stack)):
        rows = _half(stack[a].shape[1], c)
        for j, (cx, cy) in enumerate(chips):
            if relay and a == 0 and j == 2:
                continue
            mine = stack[a].at[me_q, rows]
            sends[a, j] = _rcopy(mine, mine, ssem.at[a * 3 + j], rsem.at[a * 3 + j], (cx, cy, c))
            theirs = stack[a].at[2 * cx + cy, rows]
            recvs[a, j] = _rcopy(theirs, theirs, ssem.at[a * 3 + j], rsem.at[a * 3 + j], (cx, cy, c))
    return sends, recvs


def _relay_copies(win, ssem, rsem):
    x, y, c, chips = _place()
    quarter = win.shape[1] // 4
    sends, recvs = [], []
    for k in range(2):
        rows = pl.ds(c * 2 * quarter + k * quarter, quarter)
        (fx, fy), (tx, ty) = chips[k], chips[1 - k]
        landed = win.at[2 * fx + fy, rows]
        sends.append(_rcopy(landed, landed, ssem.at[k], rsem.at[k], (tx, ty, c)))
        far = win.at[2 * chips[2][0] + chips[2][1], rows]
        recvs.append(_rcopy(far, far, ssem.at[k], rsem.at[k], (tx, ty, c)))
    return sends, recvs


def _allgather_start(name, stacks, order, relay=False):
    n = len(stacks)

    def body(bufs, _, new):
        sends, _r = _ici_copies(bufs, new[0], new[1], relay)
        for cp in sends.values():
            cp.start()

    return _comm_call(name, body, stacks, order, new_sems=(3 * n, 3 * n))


def _forward_copies(stack, ssem, rsem, relay=False):
    x, y, c, chips = _place()
    sib = (x, y, 1 - c)
    sends, recvs = {}, {}
    for a in range(len(stack)):
        for j, (cx, cy) in enumerate(chips):
            if relay and a == 0 and j == 2:
                continue
            landed = stack[a].at[2 * cx + cy, _half(stack[a].shape[1], c)]
            sends[a, j] = _rcopy(landed, landed, ssem.at[a * 3 + j], rsem.at[a * 3 + j], sib)
            other = stack[a].at[2 * cx + cy, _half(stack[a].shape[1], 1 - c)]
            recvs[a, j] = _rcopy(other, other, ssem.at[a * 3 + j], rsem.at[a * 3 + j], sib)
    return sends, recvs


def _far_forward(win, ssem, rsem):
    x, y, c, chips = _place()
    sib, far_q = (x, y, 1 - c), 2 * chips[2][0] + chips[2][1]
    landed, other = win.at[far_q, _half(win.shape[1], c)], win.at[far_q, _half(win.shape[1], 1 - c)]
    return _rcopy(landed, landed, ssem.at[0], rsem.at[0], sib), _rcopy(other, other, ssem.at[0], rsem.at[0], sib)


def _allgather_forward(name, stacks, sems, order, behind=(), relay=False):
    n = len(stacks)

    def body(bufs, taken, new):
        sends, recvs = _ici_copies(bufs, taken[0], taken[1], relay)
        fwd, _r = _forward_copies(bufs, new[0], new[1], relay)
        relay_sends = _relay_copies(bufs[0], new[2], new[3])[0] if relay else []
        for (a, j), arrived in recvs.items():
            arrived.wait_recv()
            fwd[a, j].start()
            if relay and a == 0:
                relay_sends[j].start()
        for cp in sends.values():
            cp.wait_send()

    return _comm_call(name, body, stacks, order, sems_in=sems, behind=behind,
                      new_sems=(3 * n, 3 * n) + ((2, 2) if relay else ()))


def _allgather_finish(name, stacks, sems, order, relay=False):
    def body(bufs, taken, new):
        sends, recvs = _forward_copies(bufs, taken[0], taken[1], relay)
        if relay:
            relay_sends, relay_recvs = _relay_copies(bufs[0], taken[2], taken[3])
            for cp in relay_recvs:
                cp.wait_recv()
            _far_forward(bufs[0], new[0], new[1])[0].start()
            for cp in relay_sends:
                cp.wait_send()
        for cp in sends.values():
            cp.wait_send()
        for cp in recvs.values():
            cp.wait_recv()

    if relay:
        return _comm_call(name, body, stacks, order, sems_in=sems, new_sems=(1, 1))
    return _comm_call(name, body, stacks, order, sems_in=sems)[1]


def _allgather_finish_far(name, stacks, sems, order):
    def body(bufs, taken, _):
        send, recv = _far_forward(bufs[0], taken[0], taken[1])
        send.wait_send()
        recv.wait_recv()

    return _comm_call(name, body, stacks, order, sems_in=sems)[1]


def _window_unit(q, j):
    return C2I[WIN_UNIT0[q] + j]


def _pair_copies(g, t, ssem, rsem, gathered):
    x, y, c, _ = _place()
    sib = (x, y, 1 - c)
    cps, whole = [], []
    for a in range(len(g)):
        if a == 0 and gathered:
            for q in range(NCHIP):
                for j in range(WIN_UNITS // 2):
                    u = jnp.where(c == 0, _window_unit(q, WIN_UNITS // 2 + j), _window_unit(q, j))
                    src = g[0].at[pl.ds(pl.multiple_of(u * UNIT, UNIT), UNIT), :]
                    cps.append(_rcopy(src, t[0].at[q, pl.ds(j * UNIT, UNIT), :], ssem.at[0], rsem.at[0], sib))
            whole.append(_rcopy(t[0], t[0], ssem.at[0], rsem.at[0], sib))
        else:
            cp = _rcopy(g[a].at[:, _half(g[a].shape[1], 1 - c), :], t[a], ssem.at[a], rsem.at[a], sib)
            cps.append(cp)
            whole.append(cp)
    return cps, whole


def _comm_multi(name, parts, order):
    def body(buf_refs, taken, new):
        ib = it = inew = 0
        for pbody, pbufs, psems, pnew, _ in parts:
            pbody(buf_refs[ib:ib + len(pbufs)], taken[it:it + len(psems)], new[inew:inew + len(pnew)])
            ib, it, inew = ib + len(pbufs), it + len(psems), inew + len(pnew)

    sems, bufs = _comm_call(name, body, [b for p in parts for b in p[1]], order,
                            sems_in=[s for p in parts for s in p[2]], new_sems=[k for p in parts for k in p[3]])
    out, ib, inew = [], 0, 0
    for _, pbufs, _, pnew, unpack in parts:
        out.append(unpack(sems[inew:inew + len(pnew)], bufs[ib:ib + len(pbufs)]))
        ib, inew = ib + len(pbufs), inew + len(pnew)
    return out


def _pair_start_part(gs, gathered=False):
    n = len(gs)
    ts = [lax.empty((NCHIP, WIN_ROWS // 2, D) if (a == 0 and gathered) else (NCHIP, g.shape[1] // 2, g.shape[2]), f32)
          for a, g in enumerate(gs)]

    def body(bufs, _, new):
        for cp in _pair_copies(bufs[:n], bufs[n:], new[0], new[1], gathered)[0]:
            cp.start()

    return body, list(gs) + ts, (), (n, n), lambda sems, bufs: (sems, bufs)


def _pair_wait_part(bufs, sems, gathered=False):
    n = len(bufs) // 2

    def body(refs, taken, _):
        for cp in _pair_copies(refs[:n], refs[n:], taken[0], taken[1], gathered)[1]:
            cp.wait_send()
            cp.wait_recv()

    return body, list(bufs), list(sems), (), lambda _, out: (out[:n], out[n:])


ROW_TILE = 256


def _pair_add(order, gs, ts, q_arr, c_arr, name):
    n = len(gs)
    hs = [g.shape[1] // 2 for g in gs]
    nblk = max(1, max(hs) // ROW_TILE)
    assert all(h % (16 * nblk) == 0 for h in hs)

    def body(q_ref, c_ref, *refs):
        for a in range(n):
            s = refs[a][...] + refs[n + a][...]
            refs[3 * n + a][...] = s.astype(bf16)

            @pl.when(pl.program_id(1) == q_ref[0])
            def _():
                refs[2 * n + a][...] = s

    def blk(a, half):
        return pl.BlockSpec((None, hs[a] // nblk, gs[a].shape[2]),
                            lambda i, q, q_ref, c_ref: (q, (c_ref[0] * nblk if half else 0) + i, 0))

    out = _call_indexed(
        order, body, (q_arr, c_arr), list(gs) + list(ts), (nblk, NCHIP),
        [blk(a, True) for a in range(n)] + [blk(a, False) for a in range(n)],
        [pl.BlockSpec((hs[a] // nblk, gs[a].shape[2]), lambda i, q, q_ref, c_ref: (i, 0)) for a in range(n)]
        + [blk(a, False) for a in range(n)],
        name=name,
        out_shape=[jax.ShapeDtypeStruct((hs[a], gs[a].shape[2]), f32) for a in range(n)]
        + [jax.ShapeDtypeStruct((NCHIP, hs[a], gs[a].shape[2]), bf16) for a in range(n)],
        compiler_params=_params(("parallel", "arbitrary")),
    )
    return out[:n], out[n:]


def _pair_add_gathered(order, dwt, t, q_arr, c_arr, name):
    half_units, half_rows = WIN_UNITS // 2, WIN_ROWS // 2
    table = jnp.asarray([_window_unit(q, j) for q in range(NCHIP) for j in range(WIN_UNITS)], jnp.int32)

    def body(tab_ref, q_ref, c_ref, g_hbm, t_ref, own_ref, p16_ref, buf, sem):
        q = pl.program_id(0)

        def gather(w, slot):
            cps = []
            for j in range(half_units):
                u = tab_ref[w * WIN_UNITS + c_ref[0] * half_units + j]
                cps.append(pltpu.make_async_copy(g_hbm.at[pl.ds(pl.multiple_of(u * UNIT, UNIT), UNIT), :],
                                                 buf.at[slot, pl.ds(j * UNIT, UNIT), :], sem.at[slot]))
            return cps

        @pl.when(q == 0)
        def _():
            for cp in gather(0, 0):
                cp.start()

        @pl.when(q + 1 < NCHIP)
        def _():
            for cp in gather(q + 1, (q + 1) % 2):
                cp.start()

        slot = q % 2
        pltpu.make_async_copy(buf.at[slot], buf.at[slot], sem.at[slot]).wait()
        s = buf[slot] + t_ref[...]
        p16_ref[...] = s.astype(bf16)

        @pl.when(q == q_ref[0])
        def _():
            own_ref[...] = s

    blk = pl.BlockSpec((None, half_rows, D), lambda q, tab_ref, q_ref, c_ref: (q, 0, 0))
    return _call_indexed(
        order, body, (table, q_arr, c_arr), (dwt, t), (NCHIP,),
        [pl.BlockSpec(memory_space=pl.ANY), blk],
        [pl.BlockSpec((half_rows, D), lambda q, tab_ref, q_ref, c_ref: (0, 0)), blk],
        scratch_shapes=[pltpu.VMEM((2, half_rows, D), f32), pltpu.SemaphoreType.DMA((2,))],
        name=name,
        out_shape=[jax.ShapeDtypeStruct((half_rows, D), f32),
                   jax.ShapeDtypeStruct((NCHIP, half_rows, D), bf16)],
        compiler_params=_params(("arbitrary",)),
    )


def _shard_copies(p, r, sm, ssem, rsem):
    x, y, c, chips = _place()
    n = len(p)
    sends, recvs = [], []
    for a in range(n):
        for j, (cx, cy) in enumerate(chips):
            k = a * 3 + j
            sends.append(_rcopy(p[a].at[2 * cx + cy], r[a].at[j], ssem.at[k], rsem.at[k], (cx, cy, c)))
            recvs.append(_rcopy(r[a].at[j], r[a].at[j], ssem.at[k], rsem.at[k], (cx, cy, c)))
    if sm is not None:
        mine = sm.at[4 * x + 2 * y + c]
        for i in range(1, 8):
            px = (1 - x) if i & 4 else x
            py = (1 - y) if i & 2 else y
            pc = (1 - c) if i & 1 else c
            k = 3 * n + i - 1
            sends.append(_rcopy(mine, mine, ssem.at[k], rsem.at[k], (px, py, pc)))
            slot = sm.at[4 * px + 2 * py + pc]
            recvs.append(_rcopy(slot, slot, ssem.at[k], rsem.at[k], (px, py, pc)))
    return sends, recvs


def _shard_start_part(p16s, sm=None):
    n = len(p16s)
    rs = [lax.empty((3,) + p.shape[1:], bf16) for p in p16s]
    extra = [] if sm is None else [sm]
    nsem = 3 * n + (7 if sm is not None else 0)

    def body(bufs, _, new):
        sends, _r = _shard_copies(bufs[:n], bufs[n:2 * n], bufs[2 * n] if extra else None, new[0], new[1])
        for cp in sends:
            cp.start()

    return body, list(p16s) + rs + extra, (), (nsem, nsem), lambda sems, bufs: (sems, bufs)


def _shard_wait_part(bufs, sems, n):
    has_sm = len(bufs) > 2 * n

    def body(refs, taken, _):
        sends, recvs = _shard_copies(refs[:n], refs[n:2 * n], refs[2 * n] if has_sm else None, taken[0], taken[1])
        for cp in sends:
            cp.wait_send()
        for cp in recvs:
            cp.wait_recv()

    return body, list(bufs), list(sems), (), lambda _, out: (out[n:2 * n], (out[2 * n] if has_sm else None))


def _shard_sum(order, owns, rs, c_arr, name):
    n = len(owns)
    hs = [o.shape[0] for o in owns]
    nblk = max(1, max(hs) // ROW_TILE)
    assert all(h % (16 * nblk) == 0 for h in hs)
    trs = [h // nblk for h in hs]

    def body(c_ref, *refs):
        for a in range(n):
            s = refs[a][...]
            for j in range(3):
                s = s + refs[n + a][j].astype(f32)
            refs[2 * n + a][...] = s

    out = _call_indexed(
        order, body, (c_arr,), list(owns) + list(rs), (nblk,),
        [pl.BlockSpec((trs[a], owns[a].shape[1]), lambda i, c_ref: (i, 0)) for a in range(n)]
        + [pl.BlockSpec((3, trs[a], owns[a].shape[1]), lambda i, c_ref: (0, i, 0)) for a in range(n)],
        [pl.BlockSpec((trs[a], owns[a].shape[1]), lambda i, c_ref: (c_ref[0] * nblk + i, 0)) for a in range(n)],
        name=name, out_shape=[jax.ShapeDtypeStruct((2 * hs[a], owns[a].shape[1]), f32) for a in range(n)],
        compiler_params=_params(("parallel",)),
    )
    return list(out)


def _swap_copies(full, ssem, rsem):
    x, y, c, _ = _place()
    sends, recvs = [], []
    for a in range(len(full)):
        mine = full[a].at[_half(full[a].shape[0], c)]
        sends.append(_rcopy(mine, mine, ssem.at[a], rsem.at[a], (x, y, 1 - c)))
        other = full[a].at[_half(full[a].shape[0], 1 - c)]
        recvs.append(_rcopy(other, other, ssem.at[a], rsem.at[a], (x, y, 1 - c)))
    return sends, recvs


def _swap_start_part(fulls):
    n = len(fulls)

    def body(bufs, _, new):
        for cp in _swap_copies(bufs, new[0], new[1])[0]:
            cp.start()

    return body, list(fulls), (), (n, n), lambda sems, bufs: (sems, bufs)


def _swap_wait_part(fulls, sems):
    def body(refs, taken, _):
        sends, recvs = _swap_copies(refs, taken[0], taken[1])
        for cp in sends:
            cp.wait_send()
        for cp in recvs:
            cp.wait_recv()

    return body, list(fulls), list(sems), (), lambda _, out: out


def _small_finish(order, sm, ws, ms, vs):
    n = len(ws)

    def body(sm_ref, *refs):
        s = sm_ref[0]
        for d in range(1, 8):
            s = s + sm_ref[d]
        loss_ref, g_refs, upd_refs = refs[3 * n], refs[3 * n + 1:4 * n + 1], refs[4 * n + 1:]
        loss_ref[...] = s[n:n + 1, 0:1]
        for i in range(n):
            g = s[i:i + 1, 0:ws[i].shape[1]]
            g_refs[i][...] = g
            res = _adamw_math(refs[i][...], g, refs[n + i][...], refs[2 * n + i][...])
            for k in range(3):
                upd_refs[3 * i + k][...] = res[k]

    out = _call(order, body, [sm] + list(ws) + list(ms) + list(vs), name="small_sum_adamw",
                out_shape=[jax.ShapeDtypeStruct((1, 1), f32)] + [jax.ShapeDtypeStruct(w.shape, f32) for w in ws]
                + [jax.ShapeDtypeStruct(w.shape, f32) for w in ws for _ in range(3)])
    return out[0], out[1:n + 1], [out[n + 1 + 3 * i:n + 4 + 3 * i] for i in range(n)]


def _adamw_math(w, g, m, v):
    m = ADAM_B1 * m + (1.0 - ADAM_B1) * g
    v = ADAM_B2 * v + (1.0 - ADAM_B2) * (g * g)
    m_hat = m / (1.0 - ADAM_B1 ** ADAM_STEP)
    v_hat = v / (1.0 - ADAM_B2 ** ADAM_STEP)
    return -ADAM_LR * (m_hat / (jnp.sqrt(v_hat) + ADAM_EPS) + ADAM_WD * w), m, v


def _adamw(order, ws, gs, ms, vs, name):
    n = len(ws)
    nblk = max(1, max(w.shape[0] for w in ws) // ROW_TILE)
    assert all(w.shape[0] % (8 * nblk) == 0 for w in ws)

    def body(*refs):
        for a in range(n):
            w_ref, g_ref, m_ref, v_ref = (refs[k * n + a] for k in range(4))
            d_ref, nm_ref, nv_ref, g_out = refs[4 * n + 4 * a:4 * n + 4 * a + 4]
            g = g_ref[...]
            g_out[...] = g
            d_ref[...], nm_ref[...], nv_ref[...] = _adamw_math(w_ref[...], g, m_ref[...], v_ref[...])

    blks = [pl.BlockSpec((w.shape[0] // nblk, w.shape[1]), lambda i: (i, 0)) for w in ws]
    out = _call(
        order, body, list(ws) + list(gs) + list(ms) + list(vs), name=name, grid=(nblk,), in_specs=blks * 4,
        out_specs=[b for b in blks for _ in range(4)],
        out_shape=[jax.ShapeDtypeStruct(w.shape, f32) for w in ws for _ in range(4)],
        compiler_params=_params(("parallel",)),
    )
    return [out[4 * a:4 * a + 4] for a in range(n)]


def _feature_rows(w):
    return jnp.transpose(w, (2, 0, 1))


WIN_STEP = 128
WIN_PIECE = 2 * WIN_STEP


def _window_stacks(order, w, q_arr):
    steps = WIN_ROWS // WIN_STEP
    n_piece = (WIN_ROWS - 2 * WIN_STEP) // WIN_PIECE
    assert n_piece * WIN_PIECE == WIN_ROWS - 2 * WIN_STEP and WIN_STEP % 16 == 0
    assert max(OWN_ROW0) < UNIT <= WIN_STEP and OWN_ROW0[1] + FA_AT == UNIT and FA_AT + N_FA + UNIT <= SHARD_IN
    pad = -(-(WIN_ROWS - SHARD_IN + N_FA) // 8) * 8
    lead = pad - (WIN_ROWS - SHARD_IN)
    assert lead + OWN_ROW0[1] - N_FA >= 0 and lead + max(OWN_ROW0) <= pad and max(OWN_ROW0) <= WIN_ROWS - SHARD_IN

    def body(q_ref, w_ref, win_ref, fa_ref, first, last, til, fabuf, sem):
        i = pl.program_id(0)
        q = q_ref[0]
        chip1 = q == 1
        row0 = jnp.where(q == 0, OWN_ROW0[0], jnp.where(chip1, OWN_ROW0[1], jnp.where(q == 2, OWN_ROW0[2], OWN_ROW0[3])))
        skip = jnp.where(chip1, N_FA, 0)

        def rows(dst, src0, dst0, n, slot):
            return pltpu.make_async_copy(w_ref.at[pl.ds(src0, n)], dst.at[pl.ds(dst0, n)], sem.at[slot])

        def first_copies(on_chip1):
            if on_chip1:
                return [rows(first, 0, OWN_ROW0[1], FA_AT, 0), rows(first, FA_AT + N_FA, UNIT, UNIT, 1)]
            return [rows(first, 0, row0, WIN_STEP, 0)]

        def first_do(act):
            for on_chip1 in (False, True):
                @pl.when(chip1 if on_chip1 else jnp.logical_not(chip1))
                def _():
                    for c in first_copies(on_chip1):
                        act(c)

        def piece(j):
            dst0 = WIN_STEP + j * WIN_PIECE
            return pltpu.make_async_copy(w_ref.at[pl.ds(dst0 - row0 + skip, WIN_PIECE), 0],
                                         til.at[pl.ds(dst0, WIN_PIECE)], sem.at[2 + j])

        last_copy = rows(last, SHARD_IN - WIN_STEP, lead + row0 - skip, WIN_STEP, 2 + n_piece)
        fa_copy = rows(fabuf, FA_AT, 0, N_FA, 3 + n_piece)

        @pl.when(i == 0)
        def _():
            first[pl.ds(0, UNIT)] = jnp.zeros((UNIT, 1, D), f32)
            last[...] = jnp.zeros(last.shape, f32)
            fabuf[pl.ds(N_FA, FA_ROWS - N_FA)] = jnp.zeros((FA_ROWS - N_FA, 1, D), f32)
            fa_copy.start()
            first_do(lambda c: c.start())
            for j in range(n_piece):
                piece(j).start()
            last_copy.start()
            fa_copy.wait()
            fa_ref[...] = fabuf[...].reshape(FA_ROWS, D).astype(bf16)
            first_do(lambda c: c.wait())
            win_ref[...] = first[pl.ds(0, WIN_STEP)].reshape(WIN_STEP, D).astype(bf16)

        for j in range(n_piece):
            @pl.when(i == 1 + j * (WIN_PIECE // WIN_STEP))
            def _():
                piece(j).wait()

        @pl.when(jnp.logical_and(i > 0, i < steps - 1))
        def _():
            win_ref[...] = til[pl.ds(pl.multiple_of(i * WIN_STEP, WIN_STEP), WIN_STEP)].astype(bf16)

        @pl.when(i == steps - 1)
        def _():
            last_copy.wait()
            win_ref[...] = last[pl.ds(pad, WIN_STEP)].reshape(WIN_STEP, D).astype(bf16)

    return _call_indexed(
        order, body, (q_arr,), (w,), (steps,), [pl.BlockSpec(memory_space=pl.ANY)],
        [pl.BlockSpec((None, WIN_STEP, D), lambda i, q: (q[0], i, 0)),
         pl.BlockSpec((None, FA_ROWS, D), lambda i, q: (q[0], 0, 0))],
        scratch_shapes=[pltpu.VMEM((WIN_STEP + UNIT, 1, D), f32), pltpu.VMEM((pad + WIN_STEP, 1, D), f32),
                        pltpu.VMEM((WIN_ROWS, D), f32), pltpu.VMEM((FA_ROWS, 1, D), f32),
                        pltpu.SemaphoreType.DMA((4 + n_piece,))],
        name="window_w_in", out_shape=[jax.ShapeDtypeStruct((NCHIP, WIN_ROWS, D), bf16),
                                       jax.ShapeDtypeStruct((NCHIP, FA_ROWS, D), bf16)],
        compiler_params=_params(("arbitrary",)),
    )


def _unfeature_rows(a):
    return jnp.transpose(a, (1, 2, 0))


ADAM_IN_ROWS = 134
ADAM_IN_STEPS = SHARD_IN // ADAM_IN_ROWS
ADAM_IN_CHUNK = 136
ADAM_IN_CHUNKS = ADAM_IN_STEPS + 1
ADAM_IN_BUF = WIN_ROWS + N_FA


def _adamw_w_in(order, w, gwin, gfa, m, v, q_arr):
    assert ADAM_IN_CHUNK * ADAM_IN_STEPS < WIN_ROWS <= ADAM_IN_CHUNK * ADAM_IN_CHUNKS
    assert OWN_ROW0[NCHIP - 1] + ADAM_IN_ROWS <= 2 * ADAM_IN_CHUNK and ADAM_IN_CHUNK >= ADAM_IN_ROWS
    last0 = ADAM_IN_CHUNK * ADAM_IN_STEPS
    cut = OWN_ROW0[1] + FA_AT

    def body(q_ref, w_ref, gwin_ref, gfa_ref, m_ref, v_ref, go_ref, d_ref, nm_ref, nv_ref, buf, sem):
        i = pl.program_id(0)
        q = q_ref[0]
        chip1 = q == 1
        shift = jnp.where(chip1, N_FA, 0)

        def copy(src_ref, src0, dst0, n, slot):
            return pltpu.make_async_copy(src_ref.at[pl.ds(src0, n)], buf.at[pl.ds(dst0, n), 0], sem.at[slot])

        def first(on_chip1):
            if on_chip1:
                return [copy(gwin_ref, 0, 0, cut, 0), copy(gfa_ref, 0, cut, N_FA, ADAM_IN_CHUNKS),
                        copy(gwin_ref, cut, cut + N_FA, ADAM_IN_CHUNK - cut - N_FA, ADAM_IN_CHUNKS + 1)]
            return [copy(gwin_ref, 0, 0, ADAM_IN_CHUNK, 0)]

        def middle(k):
            return [copy(gwin_ref, pl.multiple_of(k * ADAM_IN_CHUNK - shift, 8), k * ADAM_IN_CHUNK, ADAM_IN_CHUNK, k)]

        def last(on_chip1):
            n = WIN_ROWS - last0 + (N_FA if on_chip1 else 0)
            return [copy(gwin_ref, WIN_ROWS - n, last0, n, ADAM_IN_STEPS)]

        def both(make, act):
            for on_chip1 in (False, True):
                @pl.when(chip1 if on_chip1 else jnp.logical_not(chip1))
                def _():
                    for c in make(on_chip1):
                        act(c)

        @pl.when(i == 0)
        def _():
            both(first, lambda c: c.start())
            for k in range(1, ADAM_IN_STEPS):
                middle(k)[0].start()
            both(last, lambda c: c.start())
            both(first, lambda c: c.wait())

        @pl.when(i < ADAM_IN_STEPS - 1)
        def _():
            middle(i + 1)[0].wait()

        @pl.when(i == ADAM_IN_STEPS - 1)
        def _():
            both(last, lambda c: c.wait())

        row0 = jnp.where(q == 0, OWN_ROW0[0], jnp.where(chip1, OWN_ROW0[1], jnp.where(q == 2, OWN_ROW0[2], OWN_ROW0[3])))
        g = buf[pl.ds(row0 + i * ADAM_IN_ROWS, ADAM_IN_ROWS)]
        go_ref[...] = g
        d_ref[...], nm_ref[...], nv_ref[...] = _adamw_math(w_ref[...], g, m_ref[...], v_ref[...])

    blk = pl.BlockSpec((ADAM_IN_ROWS, 1, D), lambda i, q: (i, 0, 0))
    hbm = pl.BlockSpec(memory_space=pl.ANY)
    return _call_indexed(
        order, body, (q_arr,), (w, gwin, gfa, m, v), (ADAM_IN_STEPS,), [blk, hbm, hbm, blk, blk], [blk] * 4,
        scratch_shapes=[pltpu.VMEM((ADAM_IN_BUF, 1, D), f32), pltpu.SemaphoreType.DMA((ADAM_IN_CHUNKS + 2,))],
        name="adamw_w_in", out_shape=[jax.ShapeDtypeStruct((SHARD_IN, 1, D), f32)] * 4,
        compiler_params=_params(("arbitrary",)),
    )


def kernel(x, norm_attn_g, w_in, b_forget, w_branch_a, w_branch_b, w_out, norm_mlp_g, w_up, w_down, norm_final_g, loss_target, m_norm_attn_g, m_w_in, m_b_forget, m_w_branch_a, m_w_branch_b, m_w_out, m_norm_mlp_g, m_w_up, m_w_down, m_norm_final_g, v_norm_attn_g, v_w_in, v_b_forget, v_w_branch_a, v_w_branch_b, v_w_out, v_norm_mlp_g, v_w_up, v_w_down, v_norm_final_g):
    xi, yi, ci = lax.axis_index("x"), lax.axis_index("y"), lax.axis_index("c")
    q_me = 2 * xi + yi
    c_arr = jnp.reshape(ci, (1,)).astype(jnp.int32)
    q_arr = jnp.reshape(q_me, (1,)).astype(jnp.int32)
    x_, tgt = x[0], loss_target[0]

    names = ["w_branch_a", "w_branch_b", "w_out", "w_up", "w_down"]
    big = dict(zip(names, [w_branch_a[0], w_branch_b[0], w_out[0], w_up[0], w_down[0]]))
    ms = dict(zip(names, [m_w_branch_a[0], m_w_branch_b[0], m_w_out[0], m_w_up[0], m_w_down[0]]))
    vs = dict(zip(names, [v_w_branch_a[0], v_w_branch_b[0], v_w_out[0], v_w_up[0], v_w_down[0]]))
    grad, upd = {}, {}
    order = _Order()

    def run(fn, *args, **kw):
        return fn(order, *args, **kw)

    def own_slot(a):
        return lax.dynamic_update_slice(lax.empty((NCHIP,) + a.shape, a.dtype), a[None], (q_me, 0, 0))

    sem_in, in_s = _allgather_start("allgather_start_in", run(_window_stacks, _feature_rows(w_in), q_arr), order,
                                    relay=True)
    rope = _rope_tables(order.tok[0, 0])
    sem_f, in_s = _allgather_forward("allgather_forward_in", in_s, sem_in, order, behind=rope, relay=True)
    sem_rest, rest = _allgather_start("allgather_start_rest", [own_slot(w.astype(bf16)) for w in big.values()], order)
    sem_f, in_s = _allgather_finish("allgather_finish_in", in_s, sem_f, order, relay=True)
    wins, fas = _allgather_finish_far("allgather_finish_far_in", in_s, sem_f, order)
    wt = run(_assemble_win, wins, fas)

    bpad = jnp.pad(b_forget, ((0, 0), (0, 120)))
    h1, qkvb, qkva, gates, fa = run(_norm_inproj, x_, norm_attn_g, wt, rope)
    F = run(_forget_cumsum, fa, bpad)
    oa, lsea = run(_fox_fwd, qkva, F)
    sem_f, rest = _allgather_forward("allgather_forward_rest", rest, sem_rest, order)
    ob, lseb = run(_dil_fwd, qkvb)
    was, wbs, wouts, wups, wdowns = _allgather_finish("allgather_finish_rest", rest, sem_f, order)
    wout = wouts.reshape(D, D)
    wdown = wdowns.reshape(DFF, D)
    ya, yb, mixed = run(_branch_mix, oa, ob, was, wbs, gates)
    x2, h2 = run(_outproj_norm, mixed, wout, x_, norm_mlp_g)
    u, a = run(_mlp_up, h2, wups)
    dx3, dx3b, dg3, loss_part = run(_mlp_down_loss, a, wdown, x2, norm_final_g.reshape(1, D), tgt)

    def comm(name, *parts):
        return _comm_multi(name, list(parts), order)

    def adamw_group(group, fulls, name):
        res = run(_adamw, [big[nm] for nm in group], fulls, [ms[nm] for nm in group], [vs[nm] for nm in group], name)
        for nm, r in zip(group, res):
            *upd[nm], grad[nm] = r

    grp_a, grp_b, grp_c = ["w_down", "w_up"], ["w_out", "w_branch_a", "w_branch_b"], ["w_in", "w_in_fa"]
    du = run(_mlp_down_bwd, dx3b, wdown, u)
    dwdown = run(_mm, a, dx3b, "tn", f32, 1024, D, "wgrad_down")
    dwup = run(_mm, h2, du, "tn", f32, D, 1024, "wgrad_up", stack_cols=True)
    ((sem_pa, buf_pa),) = comm("pair_start_a", _pair_start_part([dwdown.reshape(NCHIP, DFF // NCHIP, D), dwup]))
    dx2, dx2b, dg2 = run(_mlp_up_bwd, du, wups, x2, dx3, norm_mlp_g)
    ((gs, ts),) = comm("pair_wait_a", _pair_wait_part(buf_pa, sem_pa))
    p32_a, p16_a = run(_pair_add, gs, ts, q_arr, c_arr, "pair_add_a")
    ((sem_sa, buf_sa),) = comm("shard_start_a", _shard_start_part(p16_a))
    dya, dyb, dproj = run(_gate_bwd, dx2b, wout, gates, ya, yb)
    dwout = run(_mm, mixed, dx2b, "tn", f32, D, D, "wgrad_out")
    doa, dob = run(_branch_bwd, dya, dyb, was, wbs)
    dwas, dwbs = run(_branch_wgrad, oa, ob, dya, dyb)
    ((sem_pb, buf_pb),) = comm("pair_start_b", _pair_start_part([dwout.reshape(NCHIP, D // NCHIP, D), dwas, dwbs]))
    dF, dproj = run(_fox_bwd, qkva, doa, oa, lsea, F, dproj)
    (gs, ts), (rs_a, _) = comm("pair_wait_b_shard_wait_a", _pair_wait_part(buf_pb, sem_pb),
                               _shard_wait_part(buf_sa, sem_sa, len(grp_a)))
    p32_b, p16_b = run(_pair_add, gs, ts, q_arr, c_arr, "pair_add_b")
    fulls_a = run(_shard_sum, p32_a, rs_a, c_arr, "shard_sum_a")
    (sem_wa, fulls_a), (sem_sb, buf_sb) = comm("swap_start_a_shard_start_b", _swap_start_part(fulls_a),
                                               _shard_start_part(p16_b))
    dbf, dproj = run(_forget_bwd, dF, fa, bpad, dproj)
    dproj = run(_dil_bwd, qkvb, dob, ob, lseb, rope, dproj)
    (rs_b, _), fulls_a = comm("shard_wait_b_swap_wait_a", _shard_wait_part(buf_sb, sem_sb, len(grp_b)),
                              _swap_wait_part(fulls_a, sem_wa))
    fulls_b = run(_shard_sum, p32_b, rs_b, c_arr, "shard_sum_b")
    ((sem_wb, fulls_b),) = comm("swap_start_b", _swap_start_part(fulls_b))
    dwt = run(_mm, dproj, h1, "tn", f32, 512, D, "wgrad_in")
    dwfa = jnp.broadcast_to(dwt[F_FA:F_FA + FA_ROWS][None], (NCHIP, FA_ROWS, D))
    (sem_pc, buf_pc), fulls_b = comm("pair_start_c_swap_wait_b", _pair_start_part([dwt, dwfa], gathered=True),
                                     _swap_wait_part(fulls_b, sem_wb))
    adamw_group(grp_b, fulls_b, "adamw_b")
    (((dwt_c, dwfa_c), (t_in, t_fa)),) = comm("pair_wait_c", _pair_wait_part(buf_pc, sem_pc, gathered=True))
    p32_in, p16_in = run(_pair_add_gathered, dwt_c, t_in, q_arr, c_arr, "pair_add_w_in")
    p32_fa, p16_fa = run(_pair_add, [dwfa_c], [t_fa], q_arr, c_arr, "pair_add_w_in_fa")
    ((sem_sc, buf_sc),) = comm("shard_start_c", _shard_start_part([p16_in, *p16_fa]))
    gx, dg1 = run(_inproj_bwd, dproj, wt, x_, dx2, norm_attn_g)
    adamw_group(grp_a, fulls_a, "adamw_a")
    small = jnp.concatenate([dg1, dg2, dg3, jnp.pad(dbf[:, 0:8], ((0, 0), (0, D - 8))),
                             jnp.pad(loss_part, ((0, 0), (0, D - 128))),
                             jnp.zeros((SMALL_ROWS - 5, D), f32)], axis=0)
    sm = lax.dynamic_update_slice(lax.empty((8, SMALL_ROWS, D), f32), small[None],
                                  (4 * xi + 2 * yi + ci, 0, 0))
    (sem_sm, buf_sm), (rs_c, _) = comm("small_start_shard_wait_c", _shard_start_part([], sm),
                                       _shard_wait_part(buf_sc, sem_sc, len(grp_c)))
    fulls_c = (run(_shard_sum, [p32_in], rs_c[0:1], c_arr, "shard_sum_w_in")
               + run(_shard_sum, p32_fa, rs_c[1:2], c_arr, "shard_sum_w_in_fa"))
    (sem_wc, fulls_c), (_, sm) = comm("swap_start_c_small_wait", _swap_start_part(fulls_c),
                                      _shard_wait_part(buf_sm, sem_sm, 0))
    smalls = ["norm_attn_g", "norm_mlp_g", "norm_final_g", "b_forget"]
    loss, gs, res = run(_small_finish, sm, [norm_attn_g, norm_mlp_g, norm_final_g.reshape(1, D), b_forget],
                        [m_norm_attn_g, m_norm_mlp_g, m_norm_final_g.reshape(1, D), m_b_forget],
                        [v_norm_attn_g, v_norm_mlp_g, v_norm_final_g.reshape(1, D), v_b_forget])
    loss = loss.reshape(())
    grad.update(zip(smalls, gs))
    upd.update(zip(smalls, res))

    ((gwin, gfa),) = comm("swap_wait_c", _swap_wait_part(fulls_c, sem_wc))
    res_in = run(_adamw_w_in, _feature_rows(w_in), gwin, gfa, _feature_rows(m_w_in), _feature_rows(v_w_in), q_arr)
    grad["w_in"] = _unfeature_rows(res_in[0])
    upd["w_in"] = [_unfeature_rows(t) for t in res_in[1:]]

    order_out = ["norm_attn_g", "w_in", "b_forget", "w_branch_a", "w_branch_b", "w_out", "norm_mlp_g", "w_up",
                 "w_down", "norm_final_g"]
    shapes = dict(norm_attn_g=norm_attn_g.shape, w_in=w_in.shape, b_forget=b_forget.shape,
                  w_branch_a=w_branch_a.shape, w_branch_b=w_branch_b.shape, w_out=w_out.shape,
                  norm_mlp_g=norm_mlp_g.shape, w_up=w_up.shape, w_down=w_down.shape, norm_final_g=norm_final_g.shape)
    outs = [loss, gx.reshape(x.shape)]
    outs += [grad[nm].reshape(shapes[nm]) for nm in order_out]
    for k in range(3):
        outs += [upd[nm][k].reshape(shapes[nm]) for nm in order_out]
    return tuple(outs)
```

```python
import jax
import jax.numpy as jnp
from jax import lax
from jax.experimental import pallas as pl
from jax.experimental.pallas import tpu as pltpu

f32 = jnp.float32
bf16 = jnp.bfloat16

S = 2048
D = 1024
DFF = 4096
HD = 64
FOXW = 512
DILOUT = 256
DIL = (1, 4, 16)
BAND = 128
EPS = 1e-6
NEG = -1e30
ROPE_THETA = 500000.0
NCHIP = 4
TQ = 256

ADAM_LR, ADAM_B1, ADAM_B2, ADAM_EPS, ADAM_WD, ADAM_STEP = 0.001, 0.9, 0.999, 1e-08, 0.01, 10
VMEM_LIMIT = 56 * 1024 * 1024

UNIT = 64
NP = 6144
F_DIL, F_FOX, F_FA, F_G = 0, 2304, 3840, 4096
DIL_BLK, FOX_BLK = 1152, 384
WIN_UNITS, WIN_ROWS = 24, 1536
WIN_UNIT0 = (0, 23, 45, 68)
OWN_ROW0 = (0, 2, 60, 62)
SHARD_IN = 1474
N_FA = 8
FA_AT = 1536 - SHARD_IN
FA_ROWS = 32


def _compact_to_internal():
    c2i = {}
    for p in range(2):
        for role in range(3):
            for g in range(3):
                for hh in range(2):
                    c2i[24 + 12 * role + 4 * g + 2 * p + hh] = 18 * p + 6 * role + 2 * g + hh
    for p in range(4):
        for role in range(3):
            for hh in range(2):
                c2i[8 * role + 2 * p + hh] = F_FOX // UNIT + 6 * p + 2 * role + hh
    for j in range(32):
        c2i[60 + j] = F_G // UNIT + j
    return c2i


C2I = _compact_to_internal()
OVERLAP_UNITS = (23, 45, 46, 68)


def _params(sem=None):
    return pltpu.CompilerParams(dimension_semantics=sem, vmem_limit_bytes=VMEM_LIMIT)


class _Order:
    def __init__(self):
        self.tok = None

    def mark(self, v):
        self.tok = v

    def token_for(self, args):
        return [] if self.tok is None or any(self.tok is a for a in args) else [self.tok]


def _call(order, body, args, in_specs=None, **kw):
    args = list(args)
    n_in = len(args)
    if in_specs is None:
        in_specs = [pl.BlockSpec(memory_space=pltpu.VMEM)] * n_in
    kern = body
    extra = order.token_for(args)
    if extra:
        in_specs = list(in_specs) + [pl.BlockSpec(memory_space=pl.ANY)]

        def kern(*refs):
            body(*refs[:n_in], *refs[n_in + 1:])

    out = pl.pallas_call(kern, in_specs=in_specs, **kw)(*args, *extra)
    order.mark(out[0] if isinstance(out, (tuple, list)) else out)
    return out


def _call_indexed(order, body, scalars, args, grid, in_specs, out_specs, scratch_shapes=(), **kw):
    args, in_specs = list(args), list(in_specs)
    n_front = len(scalars) + len(args)
    kern = body
    extra = order.token_for(args)
    if extra:
        in_specs.append(pl.BlockSpec(memory_space=pl.ANY))

        def kern(*refs):
            body(*refs[:n_front], *refs[n_front + 1:])

    out = pl.pallas_call(
        kern, grid_spec=pltpu.PrefetchScalarGridSpec(num_scalar_prefetch=len(scalars), grid=grid, in_specs=in_specs,
                                                     out_specs=out_specs, scratch_shapes=scratch_shapes),
        **kw)(*scalars, *args, *extra)
    order.mark(out[0] if isinstance(out, (tuple, list)) else out)
    return out


def _dot(a, b):
    return jnp.dot(a, b, preferred_element_type=f32)


def _dot_nt(a, b):
    return lax.dot_general(a, b, (((1,), (1,)), ((), ())), preferred_element_type=f32)


def _dot_tn(a, b):
    return lax.dot_general(a, b, (((0,), (0,)), ((), ())), preferred_element_type=f32)


def _split3(x):
    hi = x.astype(bf16)
    r1 = x - hi.astype(f32)
    mid = r1.astype(bf16)
    lo = (r1 - mid.astype(f32)).astype(bf16)
    return hi, mid, lo


def _rope_tables(after):
    half = 8
    inv_freq = jnp.power(jnp.float32(ROPE_THETA), -jnp.arange(half, dtype=f32) * 2.0 / 16)
    ang = (jnp.arange(S).astype(f32) + after)[:, None] * inv_freq[None, :]
    cos, sin = jnp.cos(ang), jnp.sin(ang)
    one = jnp.ones((S, HD - 16), f32)
    zero = jnp.zeros((S, HD - 16), f32)
    z8 = jnp.zeros((S, 8), f32)
    c = jnp.concatenate([cos, cos, one], axis=1)
    s1 = jnp.concatenate([-sin, z8, zero], axis=1)
    s2 = jnp.concatenate([z8, sin, zero], axis=1)
    return tuple(jnp.concatenate([t, t], axis=1) for t in (c, s1, s2))


def _mm(order, a, b, mode, out_dtype, tm, tn, name, stack_cols=False):
    if mode == "nn":
        (M, K), (_, N) = a.shape, b.shape
        a_spec = pl.BlockSpec((tm, K), lambda i, j: (i, 0))
        b_spec = pl.BlockSpec((K, tn), lambda i, j: (0, j))
        dot = _dot
    elif mode == "nt":
        (M, K), (N, _) = a.shape, b.shape
        a_spec = pl.BlockSpec((tm, K), lambda i, j: (i, 0))
        b_spec = pl.BlockSpec((tn, K), lambda i, j: (j, 0))
        dot = _dot_nt
    else:
        (K, M), (_, N) = a.shape, b.shape
        a_spec = pl.BlockSpec((K, tm), lambda i, j: (0, i))
        b_spec = pl.BlockSpec((K, tn), lambda i, j: (0, j))
        dot = _dot_tn

    def body(a_ref, b_ref, o_ref):
        o_ref[...] = dot(a_ref[...], b_ref[...]).astype(out_dtype)

    if stack_cols:
        assert tm == M
        out_spec = pl.BlockSpec((None, tm, tn), lambda i, j: (j, 0, 0))
        out_shape = jax.ShapeDtypeStruct((N // tn, M, tn), out_dtype)
    else:
        out_spec = pl.BlockSpec((tm, tn), lambda i, j: (i, j))
        out_shape = jax.ShapeDtypeStruct((M, N), out_dtype)
    return _call(
        order, body, (a, b), name=name, grid=(M // tm, N // tn), in_specs=[a_spec, b_spec],
        out_specs=out_spec, out_shape=out_shape,
        compiler_params=_params(("parallel", "parallel")),
    )


def _assemble_win(order, wins, fas):
    def body(win_ref, fa_ref, o_ref):
        q = pl.program_id(0)

        @pl.when(q == 0)
        def _():
            o_ref[...] = jnp.zeros_like(o_ref)

        for k in range(NCHIP):
            @pl.when(q == k)
            def _(k=k):
                for j in range(WIN_UNITS):
                    cu = WIN_UNIT0[k] + j
                    dst = pl.ds(C2I[cu] * UNIT, UNIT)
                    if cu in OVERLAP_UNITS:
                        o_ref[dst, :] += win_ref[j * UNIT:(j + 1) * UNIT, :]
                    else:
                        o_ref[dst, :] = win_ref[j * UNIT:(j + 1) * UNIT, :]
                if k == 1:
                    o_ref[F_FA:F_FA + FA_ROWS, :] = fa_ref[...]

    return _call(
        order, body, (wins, fas), name="assemble_w_in", grid=(NCHIP,),
        in_specs=[pl.BlockSpec((None, WIN_ROWS, D), lambda q: (q, 0, 0)),
                  pl.BlockSpec((None, FA_ROWS, D), lambda q: (1, 0, 0))],
        out_specs=pl.BlockSpec((NP, D), lambda q: (0, 0)),
        out_shape=jax.ShapeDtypeStruct((NP, D), bf16),
        compiler_params=_params(("arbitrary",)),
    )


def _weight_chunks(w_hbm, w_ref, sem, bounds):
    cps = [pltpu.make_async_copy(w_hbm.at[pl.ds(a, b - a)], w_ref.at[pl.ds(a, b - a)], sem.at[k])
           for k, (a, b) in enumerate(bounds)]

    @pl.when(pl.program_id(0) == 0)
    def _():
        for cp in cps:
            cp.start()

    return cps


def _norm_inproj(order, x, g1, wt, rope):
    tm = 256
    c_t, s1_t, s2_t = rope
    bounds = [(F_DIL, F_DIL + DIL_BLK), (F_DIL + DIL_BLK, F_FOX), (F_FOX, F_FA), (F_FA, F_FA + 128), (F_G, NP)]

    def body(x_ref, g_ref, w_hbm, c_ref, s1_ref, s2_ref, h_ref, qkvb_ref, qkva_ref, gates_ref, fa_ref, w_ref, sem):
        cps = _weight_chunks(w_hbm, w_ref, sem, bounds)

        def part(k):
            @pl.when(pl.program_id(0) == 0)
            def _():
                cps[k].wait()
            return _dot_nt(h, w_ref[bounds[k][0]:bounds[k][1], :])

        xb = x_ref[...]
        r = lax.rsqrt(jnp.mean(xb * xb, axis=-1, keepdims=True) + EPS)
        h = ((xb * r) * g_ref[...]).astype(bf16)
        h_ref[...] = h
        c, s1, s2 = c_ref[...], s1_ref[...], s2_ref[...]
        for p in range(2):
            pb = part(p)
            for ch in range(DIL_BLK // 128):
                pc = pb[:, ch * 128:(ch + 1) * 128]
                if ch < 6:
                    pc = pc * c + pltpu.roll(pc, 120, 1) * s1 + pltpu.roll(pc, 8, 1) * s2
                qkvb_ref[:, p * DIL_BLK + ch * 128:p * DIL_BLK + (ch + 1) * 128] = pc
        qkva_ref[...] = part(2).astype(bf16)
        fa_ref[...] = part(3)
        gates_ref[...] = part(4).astype(bf16)

    row = lambda w: pl.BlockSpec((tm, w), lambda i: (i, 0))
    return _call(
        order, body, (x, g1, wt, c_t, s1_t, s2_t), name="norm_inproj", grid=(S // tm,),
        in_specs=[row(D), pl.BlockSpec((1, D), lambda i: (0, 0)), pl.BlockSpec(memory_space=pl.ANY),
                  row(128), row(128), row(128)],
        out_specs=[row(D), row(2 * DIL_BLK), row(4 * FOX_BLK), row(2 * D), row(128)],
        out_shape=[jax.ShapeDtypeStruct((S, D), bf16), jax.ShapeDtypeStruct((S, 2 * DIL_BLK), f32),
                   jax.ShapeDtypeStruct((S, 4 * FOX_BLK), bf16), jax.ShapeDtypeStruct((S, 2 * D), bf16),
                   jax.ShapeDtypeStruct((S, 128), f32)],
        scratch_shapes=[pltpu.VMEM((NP, D), bf16), pltpu.SemaphoreType.DMA((len(bounds),))],
        compiler_params=_params(("arbitrary",)),
    )


def _forget_cumsum(order, fa, bpad):
    nb = S // TQ

    def body(fa_ref, b_ref, F_ref):
        rr = lax.broadcasted_iota(jnp.int32, (TQ, TQ), 0)
        cc = lax.broadcasted_iota(jnp.int32, (TQ, TQ), 1)
        tri = (rr >= cc).astype(bf16)
        lane = lax.broadcasted_iota(jnp.int32, (1, 128), 1)
        carry = jnp.zeros((1, 128), f32)
        for b in range(nb):
            z = fa_ref[b * TQ:(b + 1) * TQ, :] + b_ref[...]
            lf = jnp.minimum(z, 0.0) - jnp.log(1.0 + jnp.exp(-jnp.abs(z)))
            lf = jnp.where(lane < 8, lf, 0.0)
            hi, mid, lo = _split3(lf)
            fb = (_dot(tri, hi) + _dot(tri, mid)) + _dot(tri, lo) + carry
            F_ref[b * TQ:(b + 1) * TQ, :] = fb
            carry = fb[TQ - 1:TQ, :]

    return _call(
        order, body, (fa, bpad), name="forget_cumsum",
        out_shape=jax.ShapeDtypeStruct((S, 128), f32),
        compiler_params=_params(),
    )


def _head_masks():
    lane = lax.broadcasted_iota(jnp.int32, (1, 128), 1)
    return lane, (lane < HD, lane >= HD)


L_ONE = 3
FOX_TQ, FOX_TK = 256, 512


def _set_lanes(x, lane, first, cols):
    for n, col in enumerate(cols):
        x = jnp.where(lane == first + n, col, x)
    return x


def _f32_parts(col):
    return [t.astype(f32) for t in _split3(col)]


def _fox_operands(qkv_ref, F_ref, lse_ref, qa, ka, p, rows):
    lane, hm = _head_masks()
    q = qkv_ref[rows, 0:128].astype(f32) * 0.125
    k = qkv_ref[rows, 128:256].astype(f32)
    Fb = F_ref[rows, :]
    for hh in (0, 1):
        free = (1 - hh) * HD
        fcol = jnp.sum(jnp.where(lane == 2 * p + hh, Fb, 0.0), axis=1, keepdims=True)
        qterm = fcol if lse_ref is None else fcol - lse_ref[rows, hh * HD:hh * HD + 1]
        qcols = _f32_parts(qterm) + [1.0] * 3
        kcols = [1.0] * 3 + [-t for t in _f32_parts(fcol)]
        qa[hh, rows, :] = _set_lanes(jnp.where(hm[hh], q, 0.0), lane, free, qcols).astype(bf16)
        ka[hh, rows, :] = _set_lanes(k, lane, free, kcols).astype(bf16)


def _fox_fwd(order, qkva, F):
    tq, tk = FOX_TQ, FOX_TK

    def body(qkv_ref, F_ref, o_ref, lse_ref, qa, ka, vt):
        p = pl.program_id(0)
        keyi = lax.broadcasted_iota(jnp.int32, (tk, 1), 0)
        qryi = lax.broadcasted_iota(jnp.int32, (1, tq), 1)
        sub = lax.broadcasted_iota(jnp.int32, (128, 1), 0)

        def prep(i, c):
            rows = pl.ds(pl.multiple_of(i * tk, tk), tk)
            _fox_operands(qkv_ref, F_ref, None, qa, ka, p, rows)
            vt[i] = qkv_ref[rows, 256:384].astype(f32).T.astype(bf16)
            return c

        lax.fori_loop(0, S // tk, prep, 0)

        def qblock(i, first_half):
            r0 = pl.multiple_of(i * tq, tq)
            qh = [qa[hh, pl.ds(r0, tq), :] for hh in (0, 1)]

            def kv(jb, carry, masked, width):
                keys = pl.ds(pl.multiple_of(jb * tk, tk), width)
                sts = [_dot_nt(ka[hh, keys, :], qh[hh]) for hh in (0, 1)]
                new = []
                for hh in (0, 1):
                    m, l, a = carry[3 * hh:3 * hh + 3]
                    st = sts[hh]
                    if masked:
                        st = jnp.where(jb * tk + keyi[0:width] <= r0 + qryi, st, NEG)
                    mn = jnp.maximum(m, jnp.max(st, axis=0, keepdims=True))
                    al = jnp.exp(m - mn)
                    pt = jnp.exp(st - mn)
                    l = al * l + jnp.sum(pt, axis=0, keepdims=True)
                    a = al * a + _dot(vt[jb, hh * HD:(hh + 1) * HD, 0:width], pt.astype(bf16))
                    new += [mn, l, a]
                return tuple(new)

            init = (jnp.full((1, tq), NEG, f32), jnp.zeros((1, tq), f32), jnp.zeros((HD, tq), f32)) * 2
            last = (r0 + tq - 1) // tk
            carry = lax.fori_loop(0, last, lambda j, cr: kv(j, cr, False, tk), init)
            m0, l0, a0, m1, l1, a1 = kv(last, carry, True, tk // 2 if first_half else tk)
            ot = jnp.concatenate([a0 / l0, a1 / l1], axis=0)
            lt = jnp.where(sub < HD, m0 + jnp.log(l0), m1 + jnp.log(l1))
            o_ref[pl.ds(r0, tq), :] = ot.T.astype(bf16)
            lse_ref[pl.ds(r0, tq), :] = lt.T

        def qpair(t, c):
            qblock(2 * t, True)
            qblock(2 * t + 1, False)
            return c

        assert tk == 2 * tq
        lax.fori_loop(0, S // tk, qpair, 0)

    pair = pl.BlockSpec((S, 128), lambda p: (0, p))
    return _call(
        order, body, (qkva, F), name="fox_fwd", grid=(4,),
        in_specs=[pl.BlockSpec((S, FOX_BLK), lambda p: (0, p)), pl.BlockSpec((S, 128), lambda p: (0, 0))],
        out_specs=[pair, pair],
        out_shape=[jax.ShapeDtypeStruct((S, FOXW), bf16), jax.ShapeDtypeStruct((S, FOXW), f32)],
        scratch_shapes=[pltpu.VMEM((2, S, 128), bf16)] * 2 + [pltpu.VMEM((S // tk, 128, tk), bf16)],
        compiler_params=_params(("parallel",)),
    )


def _permute_in(dst, src, r):
    L = S // r
    for rho in range(r):
        dst[rho * L:(rho + 1) * L, :] = src[pl.ds(rho, L, stride=r), :]


def _permute_out(dst, src, r):
    L = S // r
    for rho in range(r):
        dst[pl.ds(rho, L, stride=r), :] = src[rho * L:(rho + 1) * L, :]


def _band_width(nbl):
    return BAND if nbl == 1 else 2 * BAND


def _band_geometry(bb, nbl):
    r0 = pl.multiple_of(bb * BAND, BAND)
    if nbl == 1:
        k0 = r0
    else:
        k0 = pl.multiple_of(jnp.maximum(bb - 1, 0) * BAND, BAND)
    sub0 = (bb - lax.rem(bb, nbl)) * BAND
    qi = r0 + lax.broadcasted_iota(jnp.int32, (BAND, 1), 0)
    ki = k0 + lax.broadcasted_iota(jnp.int32, (1, _band_width(nbl)), 1)
    diff = qi - ki
    valid = (diff >= 0) & (diff <= BAND) & (ki >= sub0)
    return r0, k0, valid


def _dil_views(ref):
    return [[ref.at[:, pl.ds((3 * role + g) * 128, 128)] for g in range(3)] for role in range(3)]


DIL_UNROLL = 4


def _dil_in_specs():
    return [pl.BlockSpec((S, 128), lambda p, k=k: (0, 9 * p + k)) for k in range(9)]


def _dil_fwd(order, qkvb):
    def body(*refs):
        q_refs, k_refs, v_refs = refs[0:3], refs[3:6], refs[6:9]
        ob_ref, lse_ref, qp, kp, vp, op, lp = refs[9:16]
        on, ln = refs[16:19], refs[19:22]
        _, hm = _head_masks()
        for g, r in enumerate(DIL):
            nbl = S // r // BAND
            if r == 1:
                qs_, ks_, vs_, od, ld = q_refs[g], k_refs[g], v_refs[g], on[g], ln[g]
            else:
                _permute_in(qp, q_refs[g], r)
                _permute_in(kp, k_refs[g], r)
                _permute_in(vp, v_refs[g], r)
                qs_, ks_, vs_, od, ld = qp, kp, vp, op, lp

            def blk(t, c, qs_=qs_, ks_=ks_, vs_=vs_, od=od, ld=ld, nbl=nbl):
                work = []
                for u in range(DIL_UNROLL):
                    r0, k0, valid = _band_geometry(DIL_UNROLL * t + u, nbl)
                    q = qs_[pl.ds(r0, BAND), :] * 0.125
                    kw = ks_[pl.ds(k0, _band_width(nbl)), :].astype(bf16)
                    vw = vs_[pl.ds(k0, _band_width(nbl)), :]
                    for hh in (0, 1):
                        qh = jnp.where(hm[hh], q, 0.0).astype(bf16)
                        work.append((u, hh, r0, valid, vw, _dot_nt(qh, kw)))
                o = [jnp.zeros((BAND, 128), f32)] * DIL_UNROLL
                lse = [jnp.zeros((BAND, 128), f32)] * DIL_UNROLL
                for u, hh, r0, valid, vw, s in work:
                    s = jnp.where(valid, s, NEG)
                    m = jnp.max(s, axis=1, keepdims=True)
                    pr = jnp.exp(s - m)
                    l = jnp.sum(pr, axis=1, keepdims=True)
                    vm = jnp.where(hm[hh], vw, 0.0).astype(bf16)
                    o[u] = o[u] + _dot((pr / l).astype(bf16), vm)
                    lse[u] = jnp.where(hm[hh], m + jnp.log(l), lse[u])
                    if hh == 1:
                        od[pl.ds(r0, BAND), :] = o[u]
                        ld[pl.ds(r0, BAND), :] = lse[u]
                return c

            lax.fori_loop(0, S // BAND // DIL_UNROLL, blk, 0)
            if r != 1:
                _permute_out(on[g], op, r)
                _permute_out(ln[g], lp, r)

        def combine(i, c):
            r0 = pl.multiple_of(i * TQ, TQ)
            ls = [ln[g][pl.ds(r0, TQ), :] for g in range(3)]
            mx = jnp.maximum(jnp.maximum(ls[0], ls[1]), ls[2])
            es = [jnp.exp(l - mx) for l in ls]
            tot = (es[0] + es[1]) + es[2]
            acc = (es[0] / tot) * on[0][pl.ds(r0, TQ), :]
            acc = acc + (es[1] / tot) * on[1][pl.ds(r0, TQ), :]
            acc = acc + (es[2] / tot) * on[2][pl.ds(r0, TQ), :]
            ob_ref[pl.ds(r0, TQ), :] = acc.astype(bf16)
            lse_ref[pl.ds(r0, TQ), :] = mx + jnp.log(tot)
            return c

        lax.fori_loop(0, S // TQ, combine, 0)

    out_blk = pl.BlockSpec((S, 128), lambda p: (0, p))
    return _call(
        order, body, [qkvb] * 9, name="dil_fwd", grid=(2,),
        in_specs=_dil_in_specs(), out_specs=[out_blk, out_blk],
        out_shape=[jax.ShapeDtypeStruct((S, DILOUT), bf16), jax.ShapeDtypeStruct((S, DILOUT), f32)],
        scratch_shapes=[pltpu.VMEM((S, 128), f32)] * 11,
        compiler_params=_params(("parallel",)),
    )


def _branch_mix(order, oa, ob, was, wbs, gates):
    tm = 512

    def body(oa_ref, ob_ref, wa_ref, wb_ref, g_ref, ya_ref, yb_ref, mix_ref):
        oa_b, ob_b = oa_ref[...], ob_ref[...]
        for q in range(NCHIP):
            cols = slice(q * 256, (q + 1) * 256)
            ya = _dot(oa_b, wa_ref[q])
            yb = _dot(ob_b, wb_ref[q])
            ya_ref[:, cols] = ya.astype(bf16)
            yb_ref[:, cols] = yb.astype(bf16)
            ga = g_ref[:, q * 256:(q + 1) * 256].astype(f32)
            gb = g_ref[:, D + q * 256:D + (q + 1) * 256].astype(f32)
            mix_ref[:, cols] = (jax.nn.sigmoid(ga) * ya + jax.nn.sigmoid(gb) * yb).astype(bf16)

    row = lambda w: pl.BlockSpec((tm, w), lambda i: (i, 0))
    full3 = lambda a: pl.BlockSpec(a.shape, lambda i: (0, 0, 0))
    return _call(
        order, body, (oa, ob, was, wbs, gates), name="branch_mix", grid=(S // tm,),
        in_specs=[row(FOXW), row(DILOUT), full3(was), full3(wbs), row(2 * D)],
        out_specs=[row(D), row(D), row(D)],
        out_shape=[jax.ShapeDtypeStruct((S, D), bf16), jax.ShapeDtypeStruct((S, D), bf16),
                   jax.ShapeDtypeStruct((S, D), bf16)],
        compiler_params=_params(("parallel",)),
    )


def _outproj_norm(order, mixed, wout, x, g2):
    tm = 512

    def body(m_ref, w_ref, x_ref, g_ref, x2_ref, h2_ref):
        x2 = x_ref[...] + _dot(m_ref[...], w_ref[...])
        x2_ref[...] = x2
        r = lax.rsqrt(jnp.mean(x2 * x2, axis=-1, keepdims=True) + EPS)
        h2_ref[...] = ((x2 * r) * g_ref[...]).astype(bf16)

    row = pl.BlockSpec((tm, D), lambda i: (i, 0))
    return _call(
        order, body, (mixed, wout, x, g2), name="outproj_norm", grid=(S // tm,),
        in_specs=[row, pl.BlockSpec((D, D), lambda i: (0, 0)), row, pl.BlockSpec((1, D), lambda i: (0, 0))],
        out_specs=[row, row],
        out_shape=[jax.ShapeDtypeStruct((S, D), f32), jax.ShapeDtypeStruct((S, D), bf16)],
        compiler_params=_params(("parallel",)),
    )


def _mlp_up(order, h2, wups):
    tm = 1024

    def body(h_ref, w_ref, ru_ref, a_ref):
        ru = jnp.maximum(_dot(h_ref[...], w_ref[...]), 0.0)
        ru_ref[...] = ru.astype(bf16)
        a_ref[...] = (ru * ru).astype(bf16)

    out = pl.BlockSpec((tm, D), lambda q, i: (i, q))
    return _call(
        order, body, (h2, wups), name="mlp_up", grid=(NCHIP, S // tm),
        in_specs=[pl.BlockSpec((tm, D), lambda q, i: (i, 0)), pl.BlockSpec((None, D, D), lambda q, i: (q, 0, 0))],
        out_specs=[out, out],
        out_shape=[jax.ShapeDtypeStruct((S, DFF), bf16), jax.ShapeDtypeStruct((S, DFF), bf16)],
        compiler_params=_params(("parallel", "parallel")),
    )


def _mlp_down_loss(order, a, wdown, x2, g3, tgt):
    tm = 512

    def body(a_ref, w_ref, x2_ref, g_ref, t_ref, dx_ref, dxb_ref, dg_ref, loss_ref):
        i = pl.program_id(0)
        x3 = x2_ref[...] + _dot(a_ref[...], w_ref[...])
        r = lax.rsqrt(jnp.mean(x3 * x3, axis=-1, keepdims=True) + EPS)
        xh = x3 * r
        g = g_ref[...]
        e = xh * g - t_ref[...]
        part = 0.5 * jnp.sum(jnp.mean(e * e, axis=-1, keepdims=True), axis=0, keepdims=True)
        dy = e * (1.0 / D)
        gdy = dy * g
        dx = r * (gdy - xh * jnp.mean(gdy * xh, axis=-1, keepdims=True))
        dx_ref[...] = dx
        dxb_ref[...] = dx.astype(bf16)

        @pl.when(i == 0)
        def _():
            dg_ref[...] = jnp.zeros_like(dg_ref)
            loss_ref[...] = jnp.zeros_like(loss_ref)

        dg_ref[...] += jnp.sum(dy * xh, axis=0, keepdims=True)
        loss_ref[...] += jnp.broadcast_to(part, (1, 128))

    row = pl.BlockSpec((tm, D), lambda i: (i, 0))
    vec = pl.BlockSpec((1, D), lambda i: (0, 0))
    return _call(
        order, body, (a, wdown, x2, g3, tgt), name="mlp_down_loss", grid=(S // tm,),
        in_specs=[pl.BlockSpec((tm, DFF), lambda i: (i, 0)), pl.BlockSpec((DFF, D), lambda i: (0, 0)), row, vec, row],
        out_specs=[row, row, vec, pl.BlockSpec((1, 128), lambda i: (0, 0))],
        out_shape=[jax.ShapeDtypeStruct((S, D), f32), jax.ShapeDtypeStruct((S, D), bf16),
                   jax.ShapeDtypeStruct((1, D), f32), jax.ShapeDtypeStruct((1, 128), f32)],
        compiler_params=_params(("arbitrary",)),
    )


def _mlp_down_bwd(order, dx3b, wdown, u):
    tm = 512

    def body(d_ref, w_ref, u_ref, du_ref):
        d = d_ref[...]
        for q in range(NCHIP):
            cols = slice(q * D, (q + 1) * D)
            da = _dot_nt(d, w_ref[cols, :])
            du_ref[:, cols] = (da * (2.0 * u_ref[:, cols].astype(f32))).astype(bf16)

    return _call(
        order, body, (dx3b, wdown, u), name="mlp_down_bwd", grid=(S // tm,),
        in_specs=[pl.BlockSpec((tm, D), lambda i: (i, 0)), pl.BlockSpec((DFF, D), lambda i: (0, 0)),
                  pl.BlockSpec((tm, DFF), lambda i: (i, 0))],
        out_specs=pl.BlockSpec((tm, DFF), lambda i: (i, 0)),
        out_shape=jax.ShapeDtypeStruct((S, DFF), bf16),
        compiler_params=_params(("parallel",)),
    )


def _mlp_up_bwd(order, du, wups, x2, dx3, g2):
    tm = 512

    def body(du_ref, w_ref, x2_ref, dx3_ref, g_ref, dx2_ref, dx2b_ref, dg_ref):
        i = pl.program_id(0)
        dh = jnp.zeros((tm, D), f32)
        for q in range(NCHIP):
            dh = dh + _dot_nt(du_ref[:, q * D:(q + 1) * D], w_ref[q])
        x2 = x2_ref[...]
        r = lax.rsqrt(jnp.mean(x2 * x2, axis=-1, keepdims=True) + EPS)
        xh = x2 * r
        gdh = dh * g_ref[...]
        dx2 = dx3_ref[...] + r * (gdh - xh * jnp.mean(gdh * xh, axis=-1, keepdims=True))
        dx2_ref[...] = dx2
        dx2b_ref[...] = dx2.astype(bf16)

        @pl.when(i == 0)
        def _():
            dg_ref[...] = jnp.zeros_like(dg_ref)

        dg_ref[...] += jnp.sum(dh * xh, axis=0, keepdims=True)

    row = pl.BlockSpec((tm, D), lambda i: (i, 0))
    vec = pl.BlockSpec((1, D), lambda i: (0, 0))
    return _call(
        order, body, (du, wups, x2, dx3, g2), name="mlp_up_bwd", grid=(S // tm,),
        in_specs=[pl.BlockSpec((tm, DFF), lambda i: (i, 0)), pl.BlockSpec((NCHIP, D, D), lambda i: (0, 0, 0)),
                  row, row, vec],
        out_specs=[row, row, vec],
        out_shape=[jax.ShapeDtypeStruct((S, D), f32), jax.ShapeDtypeStruct((S, D), bf16),
                   jax.ShapeDtypeStruct((1, D), f32)],
        compiler_params=_params(("arbitrary",)),
    )


def _gate_bwd(order, dx2b, wout, gates, ya, yb):
    tm = 512

    def body(d_ref, w_ref, g_ref, ya_ref, yb_ref, dya_ref, dyb_ref, dproj_ref):
        dm = _dot_nt(d_ref[...], w_ref[...])
        sa = jax.nn.sigmoid(g_ref[:, 0:D].astype(f32))
        sb = jax.nn.sigmoid(g_ref[:, D:2 * D].astype(f32))
        dya_ref[...] = (dm * sa).astype(bf16)
        dyb_ref[...] = (dm * sb).astype(bf16)
        dproj_ref[:, 0:D] = (dm * ya_ref[...].astype(f32) * (sa * (1.0 - sa))).astype(bf16)
        dproj_ref[:, D:2 * D] = (dm * yb_ref[...].astype(f32) * (sb * (1.0 - sb))).astype(bf16)

    row = lambda w: pl.BlockSpec((tm, w), lambda i: (i, 0))
    return _call(
        order, body, (dx2b, wout, gates, ya, yb), name="gate_bwd", grid=(S // tm,),
        in_specs=[row(D), pl.BlockSpec((D, D), lambda i: (0, 0)), row(2 * D), row(D), row(D)],
        out_specs=[row(D), row(D), pl.BlockSpec((tm, 2 * D), lambda i: (i, F_G // (2 * D)))],
        out_shape=[jax.ShapeDtypeStruct((S, D), bf16), jax.ShapeDtypeStruct((S, D), bf16),
                   jax.ShapeDtypeStruct((S, NP), bf16)],
        compiler_params=_params(("parallel",)),
    )


def _branch_bwd(order, dya, dyb, was, wbs):
    tm = 512

    def body(dya_ref, dyb_ref, wa_ref, wb_ref, doa_ref, dob_ref):
        doa = jnp.zeros((tm, FOXW), f32)
        dob = jnp.zeros((tm, DILOUT), f32)
        for q in range(NCHIP):
            cols = slice(q * 256, (q + 1) * 256)
            doa = doa + _dot_nt(dya_ref[:, cols], wa_ref[q])
            dob = dob + _dot_nt(dyb_ref[:, cols], wb_ref[q])
        doa_ref[...] = doa.astype(bf16)
        dob_ref[...] = dob

    row = lambda w: pl.BlockSpec((tm, w), lambda i: (i, 0))
    full3 = lambda a: pl.BlockSpec(a.shape, lambda i: (0, 0, 0))
    return _call(
        order, body, (dya, dyb, was, wbs), name="branch_bwd", grid=(S // tm,),
        in_specs=[row(D), row(D), full3(was), full3(wbs)],
        out_specs=[row(FOXW), row(DILOUT)],
        out_shape=[jax.ShapeDtypeStruct((S, FOXW), bf16), jax.ShapeDtypeStruct((S, DILOUT), f32)],
        compiler_params=_params(("parallel",)),
    )


def _branch_wgrad(order, oa, ob, dya, dyb):
    def body(oa_ref, ob_ref, dya_ref, dyb_ref, dwa_ref, dwb_ref):
        dwa_ref[...] = _dot_tn(oa_ref[...], dya_ref[...])
        dwb_ref[...] = _dot_tn(ob_ref[...], dyb_ref[...])

    full = lambda w: pl.BlockSpec((S, w), lambda q: (0, 0))
    colq = pl.BlockSpec((S, 256), lambda q: (0, q))
    return _call(
        order, body, (oa, ob, dya, dyb), name="branch_wgrad", grid=(NCHIP,),
        in_specs=[full(FOXW), full(DILOUT), colq, colq],
        out_specs=[pl.BlockSpec((None, FOXW, 256), lambda q: (q, 0, 0)),
                   pl.BlockSpec((None, DILOUT, 256), lambda q: (q, 0, 0))],
        out_shape=[jax.ShapeDtypeStruct((NCHIP, FOXW, 256), f32), jax.ShapeDtypeStruct((NCHIP, DILOUT, 256), f32)],
        compiler_params=_params(("parallel",)),
    )


def _fox_bwd(order, qkva, doa, oa, lse, F, dproj):
    tq, tk = FOX_TQ, FOX_TK

    def body(qkv_ref, do_ref, o_ref, lse_ref, F_ref, _dproj_in, dF_ref, dqkv_ref, qa, ka, da, va, kat,
             dk_scr, dv_scr, dqt_scr):
        p = pl.program_id(0)
        lane, hm = _head_masks()
        keyi = lax.broadcasted_iota(jnp.int32, (tk, 1), 0)
        qryi = lax.broadcasted_iota(jnp.int32, (1, tq), 1)

        def prep(i, c):
            rows = pl.ds(pl.multiple_of(i * tk, tk), tk)
            _fox_operands(qkv_ref, F_ref, lse_ref, qa, ka, p, rows)
            do = do_ref[rows, :].astype(f32)
            prod = do * o_ref[rows, :].astype(f32)
            v = qkv_ref[rows, 256:384].astype(f32)
            for hh in (0, 1):
                free = (1 - hh) * HD
                delta = jnp.sum(jnp.where(hm[hh], prod, 0.0), axis=1, keepdims=True)
                da[hh, rows, :] = _set_lanes(jnp.where(hm[hh], do, 0.0), lane, free,
                                             [-t for t in _f32_parts(delta)]).astype(bf16)
                va[hh, rows, :] = _set_lanes(v, lane, free, [1.0] * 3).astype(bf16)
                kat[hh, i] = ka[hh, rows, :].astype(f32).T.astype(bf16)
                dk_scr[hh, rows, :] = jnp.zeros((tk, 128), f32)
                dv_scr[hh, rows, :] = jnp.zeros((tk, 128), f32)
            return c

        lax.fori_loop(0, S // tk, prep, 0)

        def qblock(i, first_half):
            r0 = pl.multiple_of(i * tq, tq)
            qrows = pl.ds(r0, tq)
            qh = [qa[hh, qrows, :] for hh in (0, 1)]
            dh = [da[hh, qrows, :] for hh in (0, 1)]
            dqt_scr[...] = jnp.zeros_like(dqt_scr)

            def kv(jb, c2, masked, width):
                keys = pl.ds(pl.multiple_of(jb * tk, tk), width)
                sts = [_dot_nt(ka[hh, keys, :], qh[hh]) for hh in (0, 1)]
                dps = [_dot_nt(va[hh, keys, :], dh[hh]) for hh in (0, 1)]
                for hh in (0, 1):
                    pt = jnp.exp(sts[hh])
                    if masked:
                        pt = jnp.where(jb * tk + keyi[0:width] <= r0 + qryi, pt, 0.0)
                    dsb = (pt * dps[hh]).astype(bf16)
                    dv_scr[hh, keys, :] += _dot(pt.astype(bf16), dh[hh])
                    dk_scr[hh, keys, :] += _dot(dsb, qh[hh])
                    dqt_scr[hh] += _dot(kat[hh, jb, :, 0:width], dsb)
                return c2

            last = (r0 + tq - 1) // tk
            lax.fori_loop(0, last, lambda j, c2: kv(j, c2, False, tk), 0)
            kv(last, 0, True, tk // 2 if first_half else tk)
            dq0, dq1 = dqt_scr[0].T, dqt_scr[1].T
            dqkv_ref[qrows, 0:128] = (jnp.where(hm[0], dq0, dq1) * 0.125).astype(bf16)
            dF_ref[qrows, :] = jnp.where(lane == 0, dq0[:, HD:HD + 1], jnp.where(lane == 1, dq1[:, 0:1], 0.0))

        def qpair(t, c):
            qblock(2 * t, True)
            qblock(2 * t + 1, False)
            return c

        assert tk == 2 * tq
        lax.fori_loop(0, S // tk, qpair, 0)

        def finish(i, c):
            rows = pl.ds(pl.multiple_of(i * tq, tq), tq)
            dk0, dk1 = dk_scr[0, rows, :], dk_scr[1, rows, :]
            dqkv_ref[rows, 128:256] = jnp.where(hm[0], dk0, dk1).astype(bf16)
            dqkv_ref[rows, 256:384] = jnp.where(hm[0], dv_scr[0, rows, :], dv_scr[1, rows, :]).astype(bf16)
            cs = jnp.where(lane == 0, dk0[:, HD + L_ONE:HD + L_ONE + 1],
                           jnp.where(lane == 1, dk1[:, L_ONE:L_ONE + 1], 0.0))
            dF_ref[rows, :] = dF_ref[rows, :] - cs
            return c

        lax.fori_loop(0, S // tq, finish, 0)

    pair = pl.BlockSpec((S, 128), lambda p: (0, p))
    return _call(
        order, body, (qkva, doa, oa, lse, F, dproj), name="fox_bwd", grid=(4,),
        in_specs=[pl.BlockSpec((S, FOX_BLK), lambda p: (0, p)), pair, pair, pair,
                  pl.BlockSpec((S, 128), lambda p: (0, 0)), pl.BlockSpec(memory_space=pl.ANY)],
        out_specs=[pair, pl.BlockSpec((S, FOX_BLK), lambda p: (0, F_FOX // FOX_BLK + p))],
        out_shape=[jax.ShapeDtypeStruct((S, FOXW), f32), jax.ShapeDtypeStruct((S, NP), bf16)],
        input_output_aliases={5: 1},
        scratch_shapes=[pltpu.VMEM((2, S, 128), bf16)] * 4 + [pltpu.VMEM((2, S // tk, 128, tk), bf16)]
        + [pltpu.VMEM((2, S, 128), f32)] * 2 + [pltpu.VMEM((2, 128, tq), f32)],
        compiler_params=_params(("parallel",)),
    )


def _forget_bwd(order, dF, fa, bpad, dproj):
    nb = S // TQ

    def body(dF_ref, fa_ref, b_ref, _dproj_in, db_ref, dfa_ref):
        rr = lax.broadcasted_iota(jnp.int32, (TQ, TQ), 0)
        cc = lax.broadcasted_iota(jnp.int32, (TQ, TQ), 1)
        upper = (cc >= rr).astype(bf16)
        lane = lax.broadcasted_iota(jnp.int32, (1, 128), 1)
        carry = jnp.zeros((1, 128), f32)
        db = jnp.zeros((1, 128), f32)
        for b in reversed(range(nb)):
            cols = jnp.zeros((TQ, 128), f32)
            for h in range(8):
                c0 = (h // 2) * 128 + h % 2
                cols = jnp.where(lane == h, dF_ref[b * TQ:(b + 1) * TQ, c0:c0 + 1], cols)
            dlf = carry
            for part in _split3(cols):
                dlf = dlf + _dot(upper, part)
            carry = carry + jnp.sum(cols, axis=0, keepdims=True)
            z = fa_ref[b * TQ:(b + 1) * TQ, :] + b_ref[...]
            dz = jnp.where(lane < 8, dlf * jax.nn.sigmoid(-z), 0.0)
            dfa_ref[b * TQ:(b + 1) * TQ, 0:128] = dz.astype(bf16)
            dfa_ref[b * TQ:(b + 1) * TQ, 128:256] = jnp.zeros((TQ, 128), bf16)
            db = db + jnp.sum(dz, axis=0, keepdims=True)
        db_ref[...] = db

    whole = lambda a: pl.BlockSpec(a.shape, lambda i: (0,) * a.ndim)
    return _call(
        order, body, (dF, fa, bpad, dproj), name="forget_bwd", grid=(1,),
        in_specs=[whole(dF), whole(fa), whole(bpad), pl.BlockSpec(memory_space=pl.ANY)],
        out_specs=[pl.BlockSpec((1, 128), lambda i: (0, 0)), pl.BlockSpec((S, 256), lambda i: (0, F_FA // 256))],
        out_shape=[jax.ShapeDtypeStruct((1, 128), f32), jax.ShapeDtypeStruct((S, NP), bf16)],
        input_output_aliases={3: 1},
        compiler_params=_params(("arbitrary",)),
    )


def _dil_bwd(order, qkvb, dob, ob, lseb, rope, dproj):
    c_t, s1_t, s2_t = rope

    def body(*refs):
        q_refs, k_refs, v_refs = refs[0:3], refs[3:6], refs[6:9]
        dob_ref, ob_ref, lse_ref, c_ref, s1_ref, s2_ref, _dproj_in, dqkv_ref = refs[9:17]
        qp, kp, vp, dop, lp, dlp, dln, dqp, dkp, dvp, nat = refs[17:28]
        dq_out, dk_out, dv_out = _dil_views(dqkv_ref)
        _, hm = _head_masks()

        def delta_rows(i, c):
            r0 = pl.multiple_of(i * TQ, TQ)
            prod = dob_ref[pl.ds(r0, TQ), :] * ob_ref[pl.ds(r0, TQ), :].astype(f32)
            d0 = jnp.sum(jnp.where(hm[0], prod, 0.0), axis=1, keepdims=True)
            d1 = jnp.sum(jnp.where(hm[1], prod, 0.0), axis=1, keepdims=True)
            dln[pl.ds(r0, TQ), :] = jnp.where(hm[0], d0, d1)
            return c

        lax.fori_loop(0, S // TQ, delta_rows, 0)

        for g, r in enumerate(DIL):
            nbl = S // r // BAND
            if r == 1:
                srcs = (q_refs[g], k_refs[g], v_refs[g], dob_ref, lse_ref, dln)
            else:
                for dst, src in ((qp, q_refs[g]), (kp, k_refs[g]), (vp, v_refs[g]), (dop, dob_ref),
                                 (lp, lse_ref), (dlp, dln)):
                    _permute_in(dst, src, r)
                srcs = (qp, kp, vp, dop, lp, dlp)
            dkp[...] = jnp.zeros_like(dkp)
            dvp[...] = jnp.zeros_like(dvp)

            def blk(t, c, srcs=srcs, nbl=nbl):
                qs_, ks_, vs_, dos_, ls_, dls_ = srcs
                work = []
                for u in range(DIL_UNROLL):
                    r0, k0, valid = _band_geometry(DIL_UNROLL * t + u, nbl)
                    q = qs_[pl.ds(r0, BAND), :] * 0.125
                    kwf = ks_[pl.ds(k0, _band_width(nbl)), :]
                    kw = kwf.astype(bf16)
                    vw = vs_[pl.ds(k0, _band_width(nbl)), :].astype(bf16)
                    do = dos_[pl.ds(r0, BAND), :]
                    lse = ls_[pl.ds(r0, BAND), :]
                    dlt = dls_[pl.ds(r0, BAND), :]
                    for hh in (0, 1):
                        qh = jnp.where(hm[hh], q, 0.0).astype(bf16)
                        doh = jnp.where(hm[hh], do, 0.0).astype(bf16)
                        kh = jnp.where(hm[hh], kwf, 0.0).astype(bf16)
                        work.append((u, hh, r0, k0, valid, qh, doh, kh, lse[:, hh * HD:hh * HD + 1],
                                     dlt[:, hh * HD:hh * HD + 1], _dot_nt(qh, kw), _dot_nt(doh, vw)))
                for u, hh, r0, k0, valid, qh, doh, kh, lse_h, dlt_h, s, dp in work:
                    if hh == 0:
                        dq = jnp.zeros((BAND, 128), f32)
                        dk = jnp.zeros((_band_width(nbl), 128), f32)
                        dv = jnp.zeros((_band_width(nbl), 128), f32)
                    pr = jnp.where(valid, jnp.exp(s - lse_h), 0.0)
                    dsb = (pr * (dp - dlt_h)).astype(bf16)
                    dv = dv + _dot_tn(pr.astype(bf16), doh)
                    dk = dk + _dot_tn(dsb, qh)
                    dq = dq + _dot(dsb, kh)
                    if hh == 1:
                        dqp[pl.ds(r0, BAND), :] = dq * 0.125
                        dkp[pl.ds(k0, _band_width(nbl)), :] += dk
                        dvp[pl.ds(k0, _band_width(nbl)), :] += dv
                return c

            lax.fori_loop(0, S // BAND // DIL_UNROLL, blk, 0)

            for acc, out, roped in ((dqp, dq_out[g], True), (dkp, dk_out[g], True), (dvp, dv_out[g], False)):
                if r == 1:
                    src = acc
                else:
                    _permute_out(nat, acc, r)
                    src = nat

                def emit(i, c, src=src, out=out, roped=roped):
                    r0 = pl.multiple_of(i * TQ, TQ)
                    d = src[pl.ds(r0, TQ), :]
                    if roped:
                        d = (d * c_ref[pl.ds(r0, TQ), :] + pltpu.roll(d * s1_ref[pl.ds(r0, TQ), :], 8, 1)
                             + pltpu.roll(d * s2_ref[pl.ds(r0, TQ), :], 120, 1))
                    out[pl.ds(r0, TQ), :] = d.astype(bf16)
                    return c

                lax.fori_loop(0, S // TQ, emit, 0)

    pair = pl.BlockSpec((S, 128), lambda p: (0, p))
    tab = pl.BlockSpec((S, 128), lambda p: (0, 0))
    blk_spec = pl.BlockSpec((S, DIL_BLK), lambda p: (0, p))
    return _call(
        order, body, [qkvb] * 9 + [dob, ob, lseb, c_t, s1_t, s2_t, dproj], name="dil_bwd", grid=(2,),
        in_specs=_dil_in_specs() + [pair, pair, pair, tab, tab, tab, pl.BlockSpec(memory_space=pl.ANY)],
        out_specs=blk_spec,
        out_shape=jax.ShapeDtypeStruct((S, NP), bf16),
        input_output_aliases={15: 0},
        scratch_shapes=[pltpu.VMEM((S, 128), f32)] * 11,
        compiler_params=_params(("parallel",)),
    )


def _inproj_bwd(order, dproj, wt, x, dx2, g1):
    tm = 256

    bounds = [(k * D, (k + 1) * D) for k in range(NP // D)]

    def body(d_ref, w_hbm, x_ref, dx2_ref, g_ref, dx_ref, dg_ref, w_ref, sem):
        i = pl.program_id(0)
        cps = _weight_chunks(w_hbm, w_ref, sem, bounds)

        def by_range():
            acc = None
            for k, (a, b) in enumerate(bounds):
                cps[k].wait()
                term = _dot(d_ref[:, a:b], w_ref[a:b, :])
                acc = term if acc is None else acc + term
            return acc

        dh = lax.cond(i == 0, by_range, lambda: _dot(d_ref[...], w_ref[...]))
        xb = x_ref[...]
        r = lax.rsqrt(jnp.mean(xb * xb, axis=-1, keepdims=True) + EPS)
        xh = xb * r
        gdh = dh * g_ref[...]
        dx_ref[...] = dx2_ref[...] + r * (gdh - xh * jnp.mean(gdh * xh, axis=-1, keepdims=True))

        @pl.when(i == 0)
        def _():
            dg_ref[...] = jnp.zeros_like(dg_ref)

        dg_ref[...] += jnp.sum(dh * xh, axis=0, keepdims=True)

    row = pl.BlockSpec((tm, D), lambda i: (i, 0))
    vec = pl.BlockSpec((1, D), lambda i: (0, 0))
    return _call(
        order, body, (dproj, wt, x, dx2, g1), name="inproj_bwd", grid=(S // tm,),
        in_specs=[pl.BlockSpec((tm, NP), lambda i: (i, 0)), pl.BlockSpec(memory_space=pl.ANY), row, row, vec],
        out_specs=[row, vec],
        out_shape=[jax.ShapeDtypeStruct((S, D), f32), jax.ShapeDtypeStruct((1, D), f32)],
        scratch_shapes=[pltpu.VMEM((NP, D), bf16), pltpu.SemaphoreType.DMA((len(bounds),))],
        compiler_params=_params(("arbitrary",)),
    )


HBM = pl.BlockSpec(memory_space=pltpu.HBM)
SEM = pl.BlockSpec(memory_space=pltpu.SEMAPHORE)
SMALL_ROWS = 8


def _comm_call(name, body, bufs, order, sems_in=(), new_sems=(), behind=()):
    nb, ns, nn = len(bufs), len(sems_in), len(new_sems)
    extra = order.token_for(bufs) + list(behind)

    def kern(*refs):
        off = nb + ns + len(extra)
        body(refs[:nb], refs[nb:nb + ns], refs[off:off + nn])
        refs[-1][...] = jnp.zeros((8, 128), f32)

    res = pl.pallas_call(
        kern, name=name,
        in_specs=[HBM] * nb + [SEM] * ns + [pl.BlockSpec(memory_space=pl.ANY)] * len(extra),
        out_specs=[SEM] * nn + [HBM] * nb + [pl.BlockSpec(memory_space=pltpu.VMEM)],
        out_shape=[pltpu.SemaphoreType.DMA((k,)) for k in new_sems] + [pltpu.HBM(b.shape, b.dtype) for b in bufs]
        + [jax.ShapeDtypeStruct((8, 128), f32)],
        input_output_aliases={i: nn + i for i in range(nb)},
        compiler_params=pltpu.CompilerParams(has_side_effects=pltpu.SideEffectType.DATAFLOW_SIDE_EFFECTING),
    )(*[pltpu.with_memory_space_constraint(b, pltpu.HBM) for b in bufs], *sems_in, *extra)
    order.mark(res[-1])
    return list(res[:nn]), list(res[nn:nn + nb])


def _place():
    x, y, c = lax.axis_index("x"), lax.axis_index("y"), lax.axis_index("c")
    chips = [(1 - x, y), (x, 1 - y), (1 - x, 1 - y)]
    return x, y, c, chips


def _rcopy(src, dst, ssem, rsem, dev):
    return pltpu.make_async_remote_copy(src_ref=src, dst_ref=dst, send_sem=ssem, recv_sem=rsem,
                                        device_id=dev, device_id_type=pl.DeviceIdType.MESH)


def _half(nrows, which):
    return pl.ds(which * (nrows // 2), nrows // 2)


def _ici_copies(stack, ssem, rsem, relay):
    x, y, c, chips = _place()
    me_q = 2 * x + y
    sends, recvs = {}, {}
    for a in range(len(stack)):
        rows = _half(stack[a].shape[1], c)
        for j, (cx, cy) in enumerate(chips):
            if relay and a == 0 and j == 2:
                continue
            mine = stack[a].at[me_q, rows]
            sends[a, j] = _rcopy(mine, mine, ssem.at[a * 3 + j], rsem.at[a * 3 + j], (cx, cy, c))
            theirs = stack[a].at[2 * cx + cy, rows]
            recvs[a, j] = _rcopy(theirs, theirs, ssem.at[a * 3 + j], rsem.at[a * 3 + j], (cx, cy, c))
    return sends, recvs


def _relay_copies(win, ssem, rsem):
    x, y, c, chips = _place()
    quarter = win.shape[1] // 4
    sends, recvs = [], []
    for k in range(2):
        rows = pl.ds(c * 2 * quarter + k * quarter, quarter)
        (fx, fy), (tx, ty) = chips[k], chips[1 - k]
        landed = win.at[2 * fx + fy, rows]
        sends.append(_rcopy(landed, landed, ssem.at[k], rsem.at[k], (tx, ty, c)))
        far = win.at[2 * chips[2][0] + chips[2][1], rows]
        recvs.append(_rcopy(far, far, ssem.at[k], rsem.at[k], (tx, ty, c)))
    return sends, recvs


def _allgather_start(name, stacks, order, relay=False):
    n = len(stacks)

    def body(bufs, _, new):
        sends, _r = _ici_copies(bufs, new[0], new[1], relay)
        for cp in sends.values():
            cp.start()

    return _comm_call(name, body, stacks, order, new_sems=(3 * n, 3 * n))


def _forward_copies(stack, ssem, rsem, relay=False):
    x, y, c, chips = _place()
    sib = (x, y, 1 - c)
    sends, recvs = {}, {}
    for a in range(len(stack)):
        for j, (cx, cy) in enumerate(chips):
            if relay and a == 0 and j == 2:
                continue
            landed = stack[a].at[2 * cx + cy, _half(stack[a].shape[1], c)]
            sends[a, j] = _rcopy(landed, landed, ssem.at[a * 3 + j], rsem.at[a * 3 + j], sib)
            other = stack[a].at[2 * cx + cy, _half(stack[a].shape[1], 1 - c)]
            recvs[a, j] = _rcopy(other, other, ssem.at[a * 3 + j], rsem.at[a * 3 + j], sib)
    return sends, recvs


def _far_forward(win, ssem, rsem):
    x, y, c, chips = _place()
    sib, far_q = (x, y, 1 - c), 2 * chips[2][0] + chips[2][1]
    landed, other = win.at[far_q, _half(win.shape[1], c)], win.at[far_q, _half(win.shape[1], 1 - c)]
    return _rcopy(landed, landed, ssem.at[0], rsem.at[0], sib), _rcopy(other, other, ssem.at[0], rsem.at[0], sib)


def _allgather_forward(name, stacks, sems, order, behind=(), relay=False):
    n = len(stacks)

    def body(bufs, taken, new):
        sends, recvs = _ici_copies(bufs, taken[0], taken[1], relay)
        fwd, _r = _forward_copies(bufs, new[0], new[1], relay)
        relay_sends = _relay_copies(bufs[0], new[2], new[3])[0] if relay else []
        for (a, j), arrived in recvs.items():
            arrived.wait_recv()
            fwd[a, j].start()
            if relay and a == 0:
                relay_sends[j].start()
        for cp in sends.values():
            cp.wait_send()

    return _comm_call(name, body, stacks, order, sems_in=sems, behind=behind,
                      new_sems=(3 * n, 3 * n) + ((2, 2) if relay else ()))


def _allgather_finish(name, stacks, sems, order, relay=False):
    def body(bufs, taken, new):
        sends, recvs = _forward_copies(bufs, taken[0], taken[1], relay)
        if relay:
            relay_sends, relay_recvs = _relay_copies(bufs[0], taken[2], taken[3])
            for cp in relay_recvs:
                cp.wait_recv()
            _far_forward(bufs[0], new[0], new[1])[0].start()
            for cp in relay_sends:
                cp.wait_send()
        for cp in sends.values():
            cp.wait_send()
        for cp in recvs.values():
            cp.wait_recv()

    if relay:
        return _comm_call(name, body, stacks, order, sems_in=sems, new_sems=(1, 1))
    return _comm_call(name, body, stacks, order, sems_in=sems)[1]


def _allgather_finish_far(name, stacks, sems, order):
    def body(bufs, taken, _):
        send, recv = _far_forward(bufs[0], taken[0], taken[1])
        send.wait_send()
        recv.wait_recv()

    return _comm_call(name, body, stacks, order, sems_in=sems)[1]


def _window_unit(q, j):
    return C2I[WIN_UNIT0[q] + j]


def _pair_copies(g, t, ssem, rsem, gathered):
    x, y, c, _ = _place()
    sib = (x, y, 1 - c)
    cps, whole = [], []
    for a in range(len(g)):
        if a == 0 and gathered:
            for q in range(NCHIP):
                for j in range(WIN_UNITS // 2):
                    u = jnp.where(c == 0, _window_unit(q, WIN_UNITS // 2 + j), _window_unit(q, j))
                    src = g[0].at[pl.ds(pl.multiple_of(u * UNIT, UNIT), UNIT), :]
                    cps.append(_rcopy(src, t[0].at[q, pl.ds(j * UNIT, UNIT), :], ssem.at[0], rsem.at[0], sib))
            whole.append(_rcopy(t[0], t[0], ssem.at[0], rsem.at[0], sib))
        else:
            cp = _rcopy(g[a].at[:, _half(g[a].shape[1], 1 - c), :], t[a], ssem.at[a], rsem.at[a], sib)
            cps.append(cp)
            whole.append(cp)
    return cps, whole


def _comm_multi(name, parts, order):
    def body(buf_refs, taken, new):
        ib = it = inew = 0
        for pbody, pbufs, psems, pnew, _ in parts:
            pbody(buf_refs[ib:ib + len(pbufs)], taken[it:it + len(psems)], new[inew:inew + len(pnew)])
            ib, it, inew = ib + len(pbufs), it + len(psems), inew + len(pnew)

    sems, bufs = _comm_call(name, body, [b for p in parts for b in p[1]], order,
                            sems_in=[s for p in parts for s in p[2]], new_sems=[k for p in parts for k in p[3]])
    out, ib, inew = [], 0, 0
    for _, pbufs, _, pnew, unpack in parts:
        out.append(unpack(sems[inew:inew + len(pnew)], bufs[ib:ib + len(pbufs)]))
        ib, inew = ib + len(pbufs), inew + len(pnew)
    return out


def _pair_start_part(gs, gathered=False):
    n = len(gs)
    ts = [lax.empty((NCHIP, WIN_ROWS // 2, D) if (a == 0 and gathered) else (NCHIP, g.shape[1] // 2, g.shape[2]), f32)
          for a, g in enumerate(gs)]

    def body(bufs, _, new):
        for cp in _pair_copies(bufs[:n], bufs[n:], new[0], new[1], gathered)[0]:
            cp.start()

    return body, list(gs) + ts, (), (n, n), lambda sems, bufs: (sems, bufs)


def _pair_wait_part(bufs, sems, gathered=False):
    n = len(bufs) // 2

    def body(refs, taken, _):
        for cp in _pair_copies(refs[:n], refs[n:], taken[0], taken[1], gathered)[1]:
            cp.wait_send()
            cp.wait_recv()

    return body, list(bufs), list(sems), (), lambda _, out: (out[:n], out[n:])


ROW_TILE = 256


def _pair_add(order, gs, ts, q_arr, c_arr, name):
    n = len(gs)
    hs = [g.shape[1] // 2 for g in gs]
    nblk = max(1, max(hs) // ROW_TILE)
    assert all(h % (16 * nblk) == 0 for h in hs)

    def body(q_ref, c_ref, *refs):
        for a in range(n):
            s = refs[a][...] + refs[n + a][...]
            refs[3 * n + a][...] = s.astype(bf16)

            @pl.when(pl.program_id(1) == q_ref[0])
            def _():
                refs[2 * n + a][...] = s

    def blk(a, half):
        return pl.BlockSpec((None, hs[a] // nblk, gs[a].shape[2]),
                            lambda i, q, q_ref, c_ref: (q, (c_ref[0] * nblk if half else 0) + i, 0))

    out = _call_indexed(
        order, body, (q_arr, c_arr), list(gs) + list(ts), (nblk, NCHIP),
        [blk(a, True) for a in range(n)] + [blk(a, False) for a in range(n)],
        [pl.BlockSpec((hs[a] // nblk, gs[a].shape[2]), lambda i, q, q_ref, c_ref: (i, 0)) for a in range(n)]
        + [blk(a, False) for a in range(n)],
        name=name,
        out_shape=[jax.ShapeDtypeStruct((hs[a], gs[a].shape[2]), f32) for a in range(n)]
        + [jax.ShapeDtypeStruct((NCHIP, hs[a], gs[a].shape[2]), bf16) for a in range(n)],
        compiler_params=_params(("parallel", "arbitrary")),
    )
    return out[:n], out[n:]


def _pair_add_gathered(order, dwt, t, q_arr, c_arr, name):
    half_units, half_rows = WIN_UNITS // 2, WIN_ROWS // 2
    table = jnp.asarray([_window_unit(q, j) for q in range(NCHIP) for j in range(WIN_UNITS)], jnp.int32)

    def body(tab_ref, q_ref, c_ref, g_hbm, t_ref, own_ref, p16_ref, buf, sem):
        q = pl.program_id(0)

        def gather(w, slot):
            cps = []
            for j in range(half_units):
                u = tab_ref[w * WIN_UNITS + c_ref[0] * half_units + j]
                cps.append(pltpu.make_async_copy(g_hbm.at[pl.ds(pl.multiple_of(u * UNIT, UNIT), UNIT), :],
                                                 buf.at[slot, pl.ds(j * UNIT, UNIT), :], sem.at[slot]))
            return cps

        @pl.when(q == 0)
        def _():
            for cp in gather(0, 0):
                cp.start()

        @pl.when(q + 1 < NCHIP)
        def _():
            for cp in gather(q + 1, (q + 1) % 2):
                cp.start()

        slot = q % 2
        pltpu.make_async_copy(buf.at[slot], buf.at[slot], sem.at[slot]).wait()
        s = buf[slot] + t_ref[...]
        p16_ref[...] = s.astype(bf16)

        @pl.when(q == q_ref[0])
        def _():
            own_ref[...] = s

    blk = pl.BlockSpec((None, half_rows, D), lambda q, tab_ref, q_ref, c_ref: (q, 0, 0))
    return _call_indexed(
        order, body, (table, q_arr, c_arr), (dwt, t), (NCHIP,),
        [pl.BlockSpec(memory_space=pl.ANY), blk],
        [pl.BlockSpec((half_rows, D), lambda q, tab_ref, q_ref, c_ref: (0, 0)), blk],
        scratch_shapes=[pltpu.VMEM((2, half_rows, D), f32), pltpu.SemaphoreType.DMA((2,))],
        name=name,
        out_shape=[jax.ShapeDtypeStruct((half_rows, D), f32),
                   jax.ShapeDtypeStruct((NCHIP, half_rows, D), bf16)],
        compiler_params=_params(("arbitrary",)),
    )


def _shard_copies(p, r, sm, ssem, rsem):
    x, y, c, chips = _place()
    n = len(p)
    sends, recvs = [], []
    for a in range(n):
        for j, (cx, cy) in enumerate(chips):
            k = a * 3 + j
            sends.append(_rcopy(p[a].at[2 * cx + cy], r[a].at[j], ssem.at[k], rsem.at[k], (cx, cy, c)))
            recvs.append(_rcopy(r[a].at[j], r[a].at[j], ssem.at[k], rsem.at[k], (cx, cy, c)))
    if sm is not None:
        mine = sm.at[4 * x + 2 * y + c]
        for i in range(1, 8):
            px = (1 - x) if i & 4 else x
            py = (1 - y) if i & 2 else y
            pc = (1 - c) if i & 1 else c
            k = 3 * n + i - 1
            sends.append(_rcopy(mine, mine, ssem.at[k], rsem.at[k], (px, py, pc)))
            slot = sm.at[4 * px + 2 * py + pc]
            recvs.append(_rcopy(slot, slot, ssem.at[k], rsem.at[k], (px, py, pc)))
    return sends, recvs


def _shard_start_part(p16s, sm=None):
    n = len(p16s)
    rs = [lax.empty((3,) + p.shape[1:], bf16) for p in p16s]
    extra = [] if sm is None else [sm]
    nsem = 3 * n + (7 if sm is not None else 0)

    def body(bufs, _, new):
        sends, _r = _shard_copies(bufs[:n], bufs[n:2 * n], bufs[2 * n] if extra else None, new[0], new[1])
        for cp in sends:
            cp.start()

    return body, list(p16s) + rs + extra, (), (nsem, nsem), lambda sems, bufs: (sems, bufs)


def _shard_wait_part(bufs, sems, n):
    has_sm = len(bufs) > 2 * n

    def body(refs, taken, _):
        sends, recvs = _shard_copies(refs[:n], refs[n:2 * n], refs[2 * n] if has_sm else None, taken[0], taken[1])
        for cp in sends:
            cp.wait_send()
        for cp in recvs:
            cp.wait_recv()

    return body, list(bufs), list(sems), (), lambda _, out: (out[n:2 * n], (out[2 * n] if has_sm else None))


def _shard_sum(order, owns, rs, c_arr, name):
    n = len(owns)
    hs = [o.shape[0] for o in owns]
    nblk = max(1, max(hs) // ROW_TILE)
    assert all(h % (16 * nblk) == 0 for h in hs)
    trs = [h // nblk for h in hs]

    def body(c_ref, *refs):
        for a in range(n):
            s = refs[a][...]
            for j in range(3):
                s = s + refs[n + a][j].astype(f32)
            refs[2 * n + a][...] = s

    out = _call_indexed(
        order, body, (c_arr,), list(owns) + list(rs), (nblk,),
        [pl.BlockSpec((trs[a], owns[a].shape[1]), lambda i, c_ref: (i, 0)) for a in range(n)]
        + [pl.BlockSpec((3, trs[a], owns[a].shape[1]), lambda i, c_ref: (0, i, 0)) for a in range(n)],
        [pl.BlockSpec((trs[a], owns[a].shape[1]), lambda i, c_ref: (c_ref[0] * nblk + i, 0)) for a in range(n)],
        name=name, out_shape=[jax.ShapeDtypeStruct((2 * hs[a], owns[a].shape[1]), f32) for a in range(n)],
        compiler_params=_params(("parallel",)),
    )
    return list(out)


def _swap_copies(full, ssem, rsem):
    x, y, c, _ = _place()
    sends, recvs = [], []
    for a in range(len(full)):
        mine = full[a].at[_half(full[a].shape[0], c)]
        sends.append(_rcopy(mine, mine, ssem.at[a], rsem.at[a], (x, y, 1 - c)))
        other = full[a].at[_half(full[a].shape[0], 1 - c)]
        recvs.append(_rcopy(other, other, ssem.at[a], rsem.at[a], (x, y, 1 - c)))
    return sends, recvs


def _swap_start_part(fulls):
    n = len(fulls)

    def body(bufs, _, new):
        for cp in _swap_copies(bufs, new[0], new[1])[0]:
            cp.start()

    return body, list(fulls), (), (n, n), lambda sems, bufs: (sems, bufs)


def _swap_wait_part(fulls, sems):
    def body(refs, taken, _):
        sends, recvs = _swap_copies(refs, taken[0], taken[1])
        for cp in sends:
            cp.wait_send()
        for cp in recvs:
            cp.wait_recv()

    return body, list(fulls), list(sems), (), lambda _, out: out


def _small_finish(order, sm, ws, ms, vs):
    n = len(ws)

    def body(sm_ref, *refs):
        s = sm_ref[0]
        for d in range(1, 8):
            s = s + sm_ref[d]
        loss_ref, g_refs, upd_refs = refs[3 * n], refs[3 * n + 1:4 * n + 1], refs[4 * n + 1:]
        loss_ref[...] = s[n:n + 1, 0:1]
        for i in range(n):
            g = s[i:i + 1, 0:ws[i].shape[1]]
            g_refs[i][...] = g
            res = _adamw_math(refs[i][...], g, refs[n + i][...], refs[2 * n + i][...])
            for k in range(3):
                upd_refs[3 * i + k][...] = res[k]

    out = _call(order, body, [sm] + list(ws) + list(ms) + list(vs), name="small_sum_adamw",
                out_shape=[jax.ShapeDtypeStruct((1, 1), f32)] + [jax.ShapeDtypeStruct(w.shape, f32) for w in ws]
                + [jax.ShapeDtypeStruct(w.shape, f32) for w in ws for _ in range(3)])
    return out[0], out[1:n + 1], [out[n + 1 + 3 * i:n + 4 + 3 * i] for i in range(n)]


def _adamw_math(w, g, m, v):
    m = ADAM_B1 * m + (1.0 - ADAM_B1) * g
    v = ADAM_B2 * v + (1.0 - ADAM_B2) * (g * g)
    m_hat = m / (1.0 - ADAM_B1 ** ADAM_STEP)
    v_hat = v / (1.0 - ADAM_B2 ** ADAM_STEP)
    return -ADAM_LR * (m_hat / (jnp.sqrt(v_hat) + ADAM_EPS) + ADAM_WD * w), m, v


def _adamw(order, ws, gs, ms, vs, name):
    n = len(ws)
    nblk = max(1, max(w.shape[0] for w in ws) // ROW_TILE)
    assert all(w.shape[0] % (8 * nblk) == 0 for w in ws)

    def body(*refs):
        for a in range(n):
            w_ref, g_ref, m_ref, v_ref = (refs[k * n + a] for k in range(4))
            d_ref, nm_ref, nv_ref, g_out = refs[4 * n + 4 * a:4 * n + 4 * a + 4]
            g = g_ref[...]
            g_out[...] = g
            d_ref[...], nm_ref[...], nv_ref[...] = _adamw_math(w_ref[...], g, m_ref[...], v_ref[...])

    blks = [pl.BlockSpec((w.shape[0] // nblk, w.shape[1]), lambda i: (i, 0)) for w in ws]
    out = _call(
        order, body, list(ws) + list(gs) + list(ms) + list(vs), name=name, grid=(nblk,), in_specs=blks * 4,
        out_specs=[b for b in blks for _ in range(4)],
        out_shape=[jax.ShapeDtypeStruct(w.shape, f32) for w in ws for _ in range(4)],
        compiler_params=_params(("parallel",)),
    )
    return [out[4 * a:4 * a + 4] for a in range(n)]


def _feature_rows(w):
    return jnp.transpose(w, (2, 0, 1))


WIN_STEP = 128
WIN_PIECE = 2 * WIN_STEP


def _window_stacks(order, w, q_arr):
    steps = WIN_ROWS // WIN_STEP
    n_piece = (WIN_ROWS - 2 * WIN_STEP) // WIN_PIECE
    assert n_piece * WIN_PIECE == WIN_ROWS - 2 * WIN_STEP and WIN_STEP % 16 == 0
    assert max(OWN_ROW0) < UNIT <= WIN_STEP and OWN_ROW0[1] + FA_AT == UNIT and FA_AT + N_FA + UNIT <= SHARD_IN
    pad = -(-(WIN_ROWS - SHARD_IN + N_FA) // 8) * 8
    lead = pad - (WIN_ROWS - SHARD_IN)
    assert lead + OWN_ROW0[1] - N_FA >= 0 and lead + max(OWN_ROW0) <= pad and max(OWN_ROW0) <= WIN_ROWS - SHARD_IN

    def body(q_ref, w_ref, win_ref, fa_ref, first, last, til, fabuf, sem):
        i = pl.program_id(0)
        q = q_ref[0]
        chip1 = q == 1
        row0 = jnp.where(q == 0, OWN_ROW0[0], jnp.where(chip1, OWN_ROW0[1], jnp.where(q == 2, OWN_ROW0[2], OWN_ROW0[3])))
        skip = jnp.where(chip1, N_FA, 0)

        def rows(dst, src0, dst0, n, slot):
            return pltpu.make_async_copy(w_ref.at[pl.ds(src0, n)], dst.at[pl.ds(dst0, n)], sem.at[slot])

        def first_copies(on_chip1):
            if on_chip1:
                return [rows(first, 0, OWN_ROW0[1], FA_AT, 0), rows(first, FA_AT + N_FA, UNIT, UNIT, 1)]
            return [rows(first, 0, row0, WIN_STEP, 0)]

        def first_do(act):
            for on_chip1 in (False, True):
                @pl.when(chip1 if on_chip1 else jnp.logical_not(chip1))
                def _():
                    for c in first_copies(on_chip1):
                        act(c)

        def piece(j):
            dst0 = WIN_STEP + j * WIN_PIECE
            return pltpu.make_async_copy(w_ref.at[pl.ds(dst0 - row0 + skip, WIN_PIECE), 0],
                                         til.at[pl.ds(dst0, WIN_PIECE)], sem.at[2 + j])

        last_copy = rows(last, SHARD_IN - WIN_STEP, lead + row0 - skip, WIN_STEP, 2 + n_piece)
        fa_copy = rows(fabuf, FA_AT, 0, N_FA, 3 + n_piece)

        @pl.when(i == 0)
        def _():
            first[pl.ds(0, UNIT)] = jnp.zeros((UNIT, 1, D), f32)
            last[...] = jnp.zeros(last.shape, f32)
            fabuf[pl.ds(N_FA, FA_ROWS - N_FA)] = jnp.zeros((FA_ROWS - N_FA, 1, D), f32)
            fa_copy.start()
            first_do(lambda c: c.start())
            for j in range(n_piece):
                piece(j).start()
            last_copy.start()
            fa_copy.wait()
            fa_ref[...] = fabuf[...].reshape(FA_ROWS, D).astype(bf16)
            first_do(lambda c: c.wait())
            win_ref[...] = first[pl.ds(0, WIN_STEP)].reshape(WIN_STEP, D).astype(bf16)

        for j in range(n_piece):
            @pl.when(i == 1 + j * (WIN_PIECE // WIN_STEP))
            def _():
                piece(j).wait()

        @pl.when(jnp.logical_and(i > 0, i < steps - 1))
        def _():
            win_ref[...] = til[pl.ds(pl.multiple_of(i * WIN_STEP, WIN_STEP), WIN_STEP)].astype(bf16)

        @pl.when(i == steps - 1)
        def _():
            last_copy.wait()
            win_ref[...] = last[pl.ds(pad, WIN_STEP)].reshape(WIN_STEP, D).astype(bf16)

    return _call_indexed(
        order, body, (q_arr,), (w,), (steps,), [pl.BlockSpec(memory_space=pl.ANY)],
        [pl.BlockSpec((None, WIN_STEP, D), lambda i, q: (q[0], i, 0)),
         pl.BlockSpec((None, FA_ROWS, D), lambda i, q: (q[0], 0, 0))],
        scratch_shapes=[pltpu.VMEM((WIN_STEP + UNIT, 1, D), f32), pltpu.VMEM((pad + WIN_STEP, 1, D), f32),
                        pltpu.VMEM((WIN_ROWS, D), f32), pltpu.VMEM((FA_ROWS, 1, D), f32),
                        pltpu.SemaphoreType.DMA((4 + n_piece,))],
        name="window_w_in", out_shape=[jax.ShapeDtypeStruct((NCHIP, WIN_ROWS, D), bf16),
                                       jax.ShapeDtypeStruct((NCHIP, FA_ROWS, D), bf16)],
        compiler_params=_params(("arbitrary",)),
    )


def _unfeature_rows(a):
    return jnp.transpose(a, (1, 2, 0))


ADAM_IN_ROWS = 134
ADAM_IN_STEPS = SHARD_IN // ADAM_IN_ROWS
ADAM_IN_CHUNK = 136
ADAM_IN_CHUNKS = ADAM_IN_STEPS + 1
ADAM_IN_BUF = WIN_ROWS + N_FA


def _adamw_w_in(order, w, gwin, gfa, m, v, q_arr):
    assert ADAM_IN_CHUNK * ADAM_IN_STEPS < WIN_ROWS <= ADAM_IN_CHUNK * ADAM_IN_CHUNKS
    assert OWN_ROW0[NCHIP - 1] + ADAM_IN_ROWS <= 2 * ADAM_IN_CHUNK and ADAM_IN_CHUNK >= ADAM_IN_ROWS
    last0 = ADAM_IN_CHUNK * ADAM_IN_STEPS
    cut = OWN_ROW0[1] + FA_AT

    def body(q_ref, w_ref, gwin_ref, gfa_ref, m_ref, v_ref, go_ref, d_ref, nm_ref, nv_ref, buf, sem):
        i = pl.program_id(0)
        q = q_ref[0]
        chip1 = q == 1
        shift = jnp.where(chip1, N_FA, 0)

        def copy(src_ref, src0, dst0, n, slot):
            return pltpu.make_async_copy(src_ref.at[pl.ds(src0, n)], buf.at[pl.ds(dst0, n), 0], sem.at[slot])

        def first(on_chip1):
            if on_chip1:
                return [copy(gwin_ref, 0, 0, cut, 0), copy(gfa_ref, 0, cut, N_FA, ADAM_IN_CHUNKS),
                        copy(gwin_ref, cut, cut + N_FA, ADAM_IN_CHUNK - cut - N_FA, ADAM_IN_CHUNKS + 1)]
            return [copy(gwin_ref, 0, 0, ADAM_IN_CHUNK, 0)]

        def middle(k):
            return [copy(gwin_ref, pl.multiple_of(k * ADAM_IN_CHUNK - shift, 8), k * ADAM_IN_CHUNK, ADAM_IN_CHUNK, k)]

        def last(on_chip1):
            n = WIN_ROWS - last0 + (N_FA if on_chip1 else 0)
            return [copy(gwin_ref, WIN_ROWS - n, last0, n, ADAM_IN_STEPS)]

        def both(make, act):
            for on_chip1 in (False, True):
                @pl.when(chip1 if on_chip1 else jnp.logical_not(chip1))
                def _():
                    for c in make(on_chip1):
                        act(c)

        @pl.when(i == 0)
        def _():
            both(first, lambda c: c.start())
            for k in range(1, ADAM_IN_STEPS):
                middle(k)[0].start()
            both(last, lambda c: c.start())
            both(first, lambda c: c.wait())

        @pl.when(i < ADAM_IN_STEPS - 1)
        def _():
            middle(i + 1)[0].wait()

        @pl.when(i == ADAM_IN_STEPS - 1)
        def _():
            both(last, lambda c: c.wait())

        row0 = jnp.where(q == 0, OWN_ROW0[0], jnp.where(chip1, OWN_ROW0[1], jnp.where(q == 2, OWN_ROW0[2], OWN_ROW0[3])))
        g = buf[pl.ds(row0 + i * ADAM_IN_ROWS, ADAM_IN_ROWS)]
        go_ref[...] = g
        d_ref[...], nm_ref[...], nv_ref[...] = _adamw_math(w_ref[...], g, m_ref[...], v_ref[...])

    blk = pl.BlockSpec((ADAM_IN_ROWS, 1, D), lambda i, q: (i, 0, 0))
    hbm = pl.BlockSpec(memory_space=pl.ANY)
    return _call_indexed(
        order, body, (q_arr,), (w, gwin, gfa, m, v), (ADAM_IN_STEPS,), [blk, hbm, hbm, blk, blk], [blk] * 4,
        scratch_shapes=[pltpu.VMEM((ADAM_IN_BUF, 1, D), f32), pltpu.SemaphoreType.DMA((ADAM_IN_CHUNKS + 2,))],
        name="adamw_w_in", out_shape=[jax.ShapeDtypeStruct((SHARD_IN, 1, D), f32)] * 4,
        compiler_params=_params(("arbitrary",)),
    )


def kernel(x, norm_attn_g, w_in, b_forget, w_branch_a, w_branch_b, w_out, norm_mlp_g, w_up, w_down, norm_final_g, loss_target, m_norm_attn_g, m_w_in, m_b_forget, m_w_branch_a, m_w_branch_b, m_w_out, m_norm_mlp_g, m_w_up, m_w_down, m_norm_final_g, v_norm_attn_g, v_w_in, v_b_forget, v_w_branch_a, v_w_branch_b, v_w_out, v_norm_mlp_g, v_w_up, v_w_down, v_norm_final_g):
    xi, yi, ci = lax.axis_index("x"), lax.axis_index("y"), lax.axis_index("c")
    q_me = 2 * xi + yi
    c_arr = jnp.reshape(ci, (1,)).astype(jnp.int32)
    q_arr = jnp.reshape(q_me, (1,)).astype(jnp.int32)
    x_, tgt = x[0], loss_target[0]

    names = ["w_branch_a", "w_branch_b", "w_out", "w_up", "w_down"]
    big = dict(zip(names, [w_branch_a[0], w_branch_b[0], w_out[0], w_up[0], w_down[0]]))
    ms = dict(zip(names, [m_w_branch_a[0], m_w_branch_b[0], m_w_out[0], m_w_up[0], m_w_down[0]]))
    vs = dict(zip(names, [v_w_branch_a[0], v_w_branch_b[0], v_w_out[0], v_w_up[0], v_w_down[0]]))
    grad, upd = {}, {}
    order = _Order()

    def run(fn, *args, **kw):
        return fn(order, *args, **kw)

    def own_slot(a):
        return lax.dynamic_update_slice(lax.empty((NCHIP,) + a.shape, a.dtype), a[None], (q_me, 0, 0))

    sem_in, in_s = _allgather_start("allgather_start_in", run(_window_stacks, _feature_rows(w_in), q_arr), order,
                                    relay=True)
    rope = _rope_tables(order.tok[0, 0])
    sem_f, in_s = _allgather_forward("allgather_forward_in", in_s, sem_in, order, behind=rope, relay=True)
    sem_rest, rest = _allgather_start("allgather_start_rest", [own_slot(w.astype(bf16)) for w in big.values()], order)
    sem_f, in_s = _allgather_finish("allgather_finish_in", in_s, sem_f, order, relay=True)
    wins, fas = _allgather_finish_far("allgather_finish_far_in", in_s, sem_f, order)
    wt = run(_assemble_win, wins, fas)

    bpad = jnp.pad(b_forget, ((0, 0), (0, 120)))
    h1, qkvb, qkva, gates, fa = run(_norm_inproj, x_, norm_attn_g, wt, rope)
    F = run(_forget_cumsum, fa, bpad)
    oa, lsea = run(_fox_fwd, qkva, F)
    sem_f, rest = _allgather_forward("allgather_forward_rest", rest, sem_rest, order)
    ob, lseb = run(_dil_fwd, qkvb)
    was, wbs, wouts, wups, wdowns = _allgather_finish("allgather_finish_rest", rest, sem_f, order)
    wout = wouts.reshape(D, D)
    wdown = wdowns.reshape(DFF, D)
    ya, yb, mixed = run(_branch_mix, oa, ob, was, wbs, gates)
    x2, h2 = run(_outproj_norm, mixed, wout, x_, norm_mlp_g)
    u, a = run(_mlp_up, h2, wups)
    dx3, dx3b, dg3, loss_part = run(_mlp_down_loss, a, wdown, x2, norm_final_g.reshape(1, D), tgt)

    def comm(name, *parts):
        return _comm_multi(name, list(parts), order)

    def adamw_group(group, fulls, name):
        res = run(_adamw, [big[nm] for nm in group], fulls, [ms[nm] for nm in group], [vs[nm] for nm in group], name)
        for nm, r in zip(group, res):
            *upd[nm], grad[nm] = r

    grp_a, grp_b, grp_c = ["w_down", "w_up"], ["w_out", "w_branch_a", "w_branch_b"], ["w_in", "w_in_fa"]
    du = run(_mlp_down_bwd, dx3b, wdown, u)
    dwdown = run(_mm, a, dx3b, "tn", f32, 1024, D, "wgrad_down")
    dwup = run(_mm, h2, du, "tn", f32, D, 1024, "wgrad_up", stack_cols=True)
    ((sem_pa, buf_pa),) = comm("pair_start_a", _pair_start_part([dwdown.reshape(NCHIP, DFF // NCHIP, D), dwup]))
    dx2, dx2b, dg2 = run(_mlp_up_bwd, du, wups, x2, dx3, norm_mlp_g)
    ((gs, ts),) = comm("pair_wait_a", _pair_wait_part(buf_pa, sem_pa))
    p32_a, p16_a = run(_pair_add, gs, ts, q_arr, c_arr, "pair_add_a")
    ((sem_sa, buf_sa),) = comm("shard_start_a", _shard_start_part(p16_a))
    dya, dyb, dproj = run(_gate_bwd, dx2b, wout, gates, ya, yb)
    dwout = run(_mm, mixed, dx2b, "tn", f32, D, D, "wgrad_out")
    doa, dob = run(_branch_bwd, dya, dyb, was, wbs)
    dwas, dwbs = run(_branch_wgrad, oa, ob, dya, dyb)
    ((sem_pb, buf_pb),) = comm("pair_start_b", _pair_start_part([dwout.reshape(NCHIP, D // NCHIP, D), dwas, dwbs]))
    dF, dproj = run(_fox_bwd, qkva, doa, oa, lsea, F, dproj)
    (gs, ts), (rs_a, _) = comm("pair_wait_b_shard_wait_a", _pair_wait_part(buf_pb, sem_pb),
                               _shard_wait_part(buf_sa, sem_sa, len(grp_a)))
    p32_b, p16_b = run(_pair_add, gs, ts, q_arr, c_arr, "pair_add_b")
    fulls_a = run(_shard_sum, p32_a, rs_a, c_arr, "shard_sum_a")
    (sem_wa, fulls_a), (sem_sb, buf_sb) = comm("swap_start_a_shard_start_b", _swap_start_part(fulls_a),
                                               _shard_start_part(p16_b))
    dbf, dproj = run(_forget_bwd, dF, fa, bpad, dproj)
    dproj = run(_dil_bwd, qkvb, dob, ob, lseb, rope, dproj)
    (rs_b, _), fulls_a = comm("shard_wait_b_swap_wait_a", _shard_wait_part(buf_sb, sem_sb, len(grp_b)),
                              _swap_wait_part(fulls_a, sem_wa))
    fulls_b = run(_shard_sum, p32_b, rs_b, c_arr, "shard_sum_b")
    ((sem_wb, fulls_b),) = comm("swap_start_b", _swap_start_part(fulls_b))
    dwt = run(_mm, dproj, h1, "tn", f32, 512, D, "wgrad_in")
    dwfa = jnp.broadcast_to(dwt[F_FA:F_FA + FA_ROWS][None], (NCHIP, FA_ROWS, D))
    (sem_pc, buf_pc), fulls_b = comm("pair_start_c_swap_wait_b", _pair_start_part([dwt, dwfa], gathered=True),
                                     _swap_wait_part(fulls_b, sem_wb))
    adamw_group(grp_b, fulls_b, "adamw_b")
    (((dwt_c, dwfa_c), (t_in, t_fa)),) = comm("pair_wait_c", _pair_wait_part(buf_pc, sem_pc, gathered=True))
    p32_in, p16_in = run(_pair_add_gathered, dwt_c, t_in, q_arr, c_arr, "pair_add_w_in")
    p32_fa, p16_fa = run(_pair_add, [dwfa_c], [t_fa], q_arr, c_arr, "pair_add_w_in_fa")
    ((sem_sc, buf_sc),) = comm("shard_start_c", _shard_start_part([p16_in, *p16_fa]))
    gx, dg1 = run(_inproj_bwd, dproj, wt, x_, dx2, norm_attn_g)
    adamw_group(grp_a, fulls_a, "adamw_a")
    small = jnp.concatenate([dg1, dg2, dg3, jnp.pad(dbf[:, 0:8], ((0, 0), (0, D - 8))),
                             jnp.pad(loss_part, ((0, 0), (0, D - 128))),
                             jnp.zeros((SMALL_ROWS - 5, D), f32)], axis=0)
    sm = lax.dynamic_update_slice(lax.empty((8, SMALL_ROWS, D), f32), small[None],
                                  (4 * xi + 2 * yi + ci, 0, 0))
    (sem_sm, buf_sm), (rs_c, _) = comm("small_start_shard_wait_c", _shard_start_part([], sm),
                                       _shard_wait_part(buf_sc, sem_sc, len(grp_c)))
    fulls_c = (run(_shard_sum, [p32_in], rs_c[0:1], c_arr, "shard_sum_w_in")
               + run(_shard_sum, p32_fa, rs_c[1:2], c_arr, "shard_sum_w_in_fa"))
    (sem_wc, fulls_c), (_, sm) = comm("swap_start_c_small_wait", _swap_start_part(fulls_c),
                                      _shard_wait_part(buf_sm, sem_sm, 0))
    smalls = ["norm_attn_g", "norm_mlp_g", "norm_final_g", "b_forget"]
    loss, gs, res = run(_small_finish, sm, [norm_attn_g, norm_mlp_g, norm_final_g.reshape(1, D), b_forget],
                        [m_norm_attn_g, m_norm_mlp_g, m_norm_final_g.reshape(1, D), m_b_forget],
                        [v_norm_attn_g, v_norm_mlp_g, v_norm_final_g.reshape(1, D), v_b_forget])
    loss = loss.reshape(())
    grad.update(zip(smalls, gs))
    upd.update(zip(smalls, res))

    ((gwin, gfa),) = comm("swap_wait_c", _swap_wait_part(fulls_c, sem_wc))
    res_in = run(_adamw_w_in, _feature_rows(w_in), gwin, gfa, _feature_rows(m_w_in), _feature_rows(v_w_in), q_arr)
    grad["w_in"] = _unfeature_rows(res_in[0])
    upd["w_in"] = [_unfeature_rows(t) for t in res_in[1:]]

    order_out = ["norm_attn_g", "w_in", "b_forget", "w_branch_a", "w_branch_b", "w_out", "norm_mlp_g", "w_up",
                 "w_down", "norm_final_g"]
    shapes = dict(norm_attn_g=norm_attn_g.shape, w_in=w_in.shape, b_forget=b_forget.shape,
                  w_branch_a=w_branch_a.shape, w_branch_b=w_branch_b.shape, w_out=w_out.shape,
                  norm_mlp_g=norm_mlp_g.shape, w_up=w_up.shape, w_down=w_down.shape, norm_final_g=norm_final_g.shape)
    outs = [loss, gx.reshape(x.shape)]
    outs += [grad[nm].reshape(shapes[nm]) for nm in order_out]
    for k in range(3):
        outs += [upd[nm][k].reshape(shapes[nm]) for nm in order_out]
    return tuple(outs)
```

```python
import jax
import jax.numpy as jnp
from jax import lax
from jax.experimental import pallas as pl
from jax.experimental.pallas import tpu as pltpu

f32 = jnp.float32
bf16 = jnp.bfloat16

S = 2048
D = 1024
DFF = 4096
HD = 64
FOXW = 512
DILOUT = 256
DIL = (1, 4, 16)
BAND = 128
EPS = 1e-6
NEG = -1e30
ROPE_THETA = 500000.0
NCHIP = 4
TQ = 256

ADAM_LR, ADAM_B1, ADAM_B2, ADAM_EPS, ADAM_WD, ADAM_STEP = 0.001, 0.9, 0.999, 1e-08, 0.01, 10
VMEM_LIMIT = 56 * 1024 * 1024

UNIT = 64
NP = 6144
F_DIL, F_FOX, F_FA, F_G = 0, 2304, 3840, 4096
DIL_BLK, FOX_BLK = 1152, 384
WIN_UNITS, WIN_ROWS = 24, 1536
WIN_UNIT0 = (0, 23, 45, 68)
OWN_ROW0 = (0, 2, 60, 62)
SHARD_IN = 1474
N_FA = 8
FA_AT = 1536 - SHARD_IN
FA_ROWS = 32


def _compact_to_internal():
    c2i = {}
    for p in range(2):
        for role in range(3):
            for g in range(3):
                for hh in range(2):
                    c2i[24 + 12 * role + 4 * g + 2 * p + hh] = 18 * p + 6 * role + 2 * g + hh
    for p in range(4):
        for role in range(3):
            for hh in range(2):
                c2i[8 * role + 2 * p + hh] = F_FOX // UNIT + 6 * p + 2 * role + hh
    for j in range(32):
        c2i[60 + j] = F_G // UNIT + j
    return c2i


C2I = _compact_to_internal()
OVERLAP_UNITS = (23, 45, 46, 68)


def _params(sem=None):
    return pltpu.CompilerParams(dimension_semantics=sem, vmem_limit_bytes=VMEM_LIMIT)


class _Order:
    def __init__(self):
        self.tok = None

    def mark(self, v):
        self.tok = v

    def token_for(self, args):
        return [] if self.tok is None or any(self.tok is a for a in args) else [self.tok]


def _call(order, body, args, in_specs=None, **kw):
    args = list(args)
    n_in = len(args)
    if in_specs is None:
        in_specs = [pl.BlockSpec(memory_space=pltpu.VMEM)] * n_in
    kern = body
    extra = order.token_for(args)
    if extra:
        in_specs = list(in_specs) + [pl.BlockSpec(memory_space=pl.ANY)]

        def kern(*refs):
            body(*refs[:n_in], *refs[n_in + 1:])

    out = pl.pallas_call(kern, in_specs=in_specs, **kw)(*args, *extra)
    order.mark(out[0] if isinstance(out, (tuple, list)) else out)
    return out


def _call_indexed(order, body, scalars, args, grid, in_specs, out_specs, scratch_shapes=(), **kw):
    args, in_specs = list(args), list(in_specs)
    n_front = len(scalars) + len(args)
    kern = body
    extra = order.token_for(args)
    if extra:
        in_specs.append(pl.BlockSpec(memory_space=pl.ANY))

        def kern(*refs):
            body(*refs[:n_front], *refs[n_front + 1:])

    out = pl.pallas_call(
        kern, grid_spec=pltpu.PrefetchScalarGridSpec(num_scalar_prefetch=len(scalars), grid=grid, in_specs=in_specs,
                                                     out_specs=out_specs, scratch_shapes=scratch_shapes),
        **kw)(*scalars, *args, *extra)
    order.mark(out[0] if isinstance(out, (tuple, list)) else out)
    return out


def _dot(a, b):
    return jnp.dot(a, b, preferred_element_type=f32)


def _dot_nt(a, b):
    return lax.dot_general(a, b, (((1,), (1,)), ((), ())), preferred_element_type=f32)


def _dot_tn(a, b):
    return lax.dot_general(a, b, (((0,), (0,)), ((), ())), preferred_element_type=f32)


def _split3(x):
    hi = x.astype(bf16)
    r1 = x - hi.astype(f32)
    mid = r1.astype(bf16)
    lo = (r1 - mid.astype(f32)).astype(bf16)
    return hi, mid, lo


def _rope_tables(after):
    half = 8
    inv_freq = jnp.power(jnp.float32(ROPE_THETA), -jnp.arange(half, dtype=f32) * 2.0 / 16)
    ang = (jnp.arange(S).astype(f32) + after)[:, None] * inv_freq[None, :]
    cos, sin = jnp.cos(ang), jnp.sin(ang)
    one = jnp.ones((S, HD - 16), f32)
    zero = jnp.zeros((S, HD - 16), f32)
    z8 = jnp.zeros((S, 8), f32)
    c = jnp.concatenate([cos, cos, one], axis=1)
    s1 = jnp.concatenate([-sin, z8, zero], axis=1)
    s2 = jnp.concatenate([z8, sin, zero], axis=1)
    return tuple(jnp.concatenate([t, t], axis=1) for t in (c, s1, s2))


def _mm(order, a, b, mode, out_dtype, tm, tn, name, stack_cols=False):
    if mode == "nn":
        (M, K), (_, N) = a.shape, b.shape
        a_spec = pl.BlockSpec((tm, K), lambda i, j: (i, 0))
        b_spec = pl.BlockSpec((K, tn), lambda i, j: (0, j))
        dot = _dot
    elif mode == "nt":
        (M, K), (N, _) = a.shape, b.shape
        a_spec = pl.BlockSpec((tm, K), lambda i, j: (i, 0))
        b_spec = pl.BlockSpec((tn, K), lambda i, j: (j, 0))
        dot = _dot_nt
    else:
        (K, M), (_, N) = a.shape, b.shape
        a_spec = pl.BlockSpec((K, tm), lambda i, j: (0, i))
        b_spec = pl.BlockSpec((K, tn), lambda i, j: (0, j))
        dot = _dot_tn

    def body(a_ref, b_ref, o_ref):
        o_ref[...] = dot(a_ref[...], b_ref[...]).astype(out_dtype)

    if stack_cols:
        assert tm == M
        out_spec = pl.BlockSpec((None, tm, tn), lambda i, j: (j, 0, 0))
        out_shape = jax.ShapeDtypeStruct((N // tn, M, tn), out_dtype)
    else:
        out_spec = pl.BlockSpec((tm, tn), lambda i, j: (i, j))
        out_shape = jax.ShapeDtypeStruct((M, N), out_dtype)
    return _call(
        order, body, (a, b), name=name, grid=(M // tm, N // tn), in_specs=[a_spec, b_spec],
        out_specs=out_spec, out_shape=out_shape,
        compiler_params=_params(("parallel", "parallel")),
    )


def _assemble_win(order, wins, fas):
    def body(win_ref, fa_ref, o_ref):
        q = pl.program_id(0)

        @pl.when(q == 0)
        def _():
            o_ref[...] = jnp.zeros_like(o_ref)

        for k in range(NCHIP):
            @pl.when(q == k)
            def _(k=k):
                for j in range(WIN_UNITS):
                    cu = WIN_UNIT0[k] + j
                    dst = pl.ds(C2I[cu] * UNIT, UNIT)
                    if cu in OVERLAP_UNITS:
                        o_ref[dst, :] += win_ref[j * UNIT:(j + 1) * UNIT, :]
                    else:
                        o_ref[dst, :] = win_ref[j * UNIT:(j + 1) * UNIT, :]
                if k == 1:
                    o_ref[F_FA:F_FA + FA_ROWS, :] = fa_ref[...]

    return _call(
        order, body, (wins, fas), name="assemble_w_in", grid=(NCHIP,),
        in_specs=[pl.BlockSpec((None, WIN_ROWS, D), lambda q: (q, 0, 0)),
                  pl.BlockSpec((None, FA_ROWS, D), lambda q: (1, 0, 0))],
        out_specs=pl.BlockSpec((NP, D), lambda q: (0, 0)),
        out_shape=jax.ShapeDtypeStruct((NP, D), bf16),
        compiler_params=_params(("arbitrary",)),
    )


def _norm_inproj(order, x, g1, wt, rope):
    tm = 256
    c_t, s1_t, s2_t = rope

    def body(x_ref, g_ref, w_ref, c_ref, s1_ref, s2_ref, h_ref, qkvb_ref, qkva_ref, gates_ref, fa_ref):
        xb = x_ref[...]
        r = lax.rsqrt(jnp.mean(xb * xb, axis=-1, keepdims=True) + EPS)
        h = ((xb * r) * g_ref[...]).astype(bf16)
        h_ref[...] = h
        c, s1, s2 = c_ref[...], s1_ref[...], s2_ref[...]
        for p in range(2):
            pb = _dot_nt(h, w_ref[F_DIL + p * DIL_BLK:F_DIL + (p + 1) * DIL_BLK, :])
            for ch in range(DIL_BLK // 128):
                pc = pb[:, ch * 128:(ch + 1) * 128]
                if ch < 6:
                    pc = pc * c + pltpu.roll(pc, 120, 1) * s1 + pltpu.roll(pc, 8, 1) * s2
                qkvb_ref[:, p * DIL_BLK + ch * 128:p * DIL_BLK + (ch + 1) * 128] = pc
        qkva_ref[...] = _dot_nt(h, w_ref[F_FOX:F_FA, :]).astype(bf16)
        fa_ref[...] = _dot_nt(h, w_ref[F_FA:F_FA + 128, :])
        gates_ref[...] = _dot_nt(h, w_ref[F_G:NP, :]).astype(bf16)

    row = lambda w: pl.BlockSpec((tm, w), lambda i: (i, 0))
    return _call(
        order, body, (x, g1, wt, c_t, s1_t, s2_t), name="norm_inproj", grid=(S // tm,),
        in_specs=[row(D), pl.BlockSpec((1, D), lambda i: (0, 0)), pl.BlockSpec((NP, D), lambda i: (0, 0)),
                  row(128), row(128), row(128)],
        out_specs=[row(D), row(2 * DIL_BLK), row(4 * FOX_BLK), row(2 * D), row(128)],
        out_shape=[jax.ShapeDtypeStruct((S, D), bf16), jax.ShapeDtypeStruct((S, 2 * DIL_BLK), f32),
                   jax.ShapeDtypeStruct((S, 4 * FOX_BLK), bf16), jax.ShapeDtypeStruct((S, 2 * D), bf16),
                   jax.ShapeDtypeStruct((S, 128), f32)],
        compiler_params=_params(("parallel",)),
    )


def _forget_cumsum(order, fa, bpad):
    nb = S // TQ

    def body(fa_ref, b_ref, F_ref):
        rr = lax.broadcasted_iota(jnp.int32, (TQ, TQ), 0)
        cc = lax.broadcasted_iota(jnp.int32, (TQ, TQ), 1)
        tri = (rr >= cc).astype(bf16)
        lane = lax.broadcasted_iota(jnp.int32, (1, 128), 1)
        carry = jnp.zeros((1, 128), f32)
        for b in range(nb):
            z = fa_ref[b * TQ:(b + 1) * TQ, :] + b_ref[...]
            lf = jnp.minimum(z, 0.0) - jnp.log(1.0 + jnp.exp(-jnp.abs(z)))
            lf = jnp.where(lane < 8, lf, 0.0)
            hi, mid, lo = _split3(lf)
            fb = (_dot(tri, hi) + _dot(tri, mid)) + _dot(tri, lo) + carry
            F_ref[b * TQ:(b + 1) * TQ, :] = fb
            carry = fb[TQ - 1:TQ, :]

    return _call(
        order, body, (fa, bpad), name="forget_cumsum",
        out_shape=jax.ShapeDtypeStruct((S, 128), f32),
        compiler_params=_params(),
    )


def _head_masks():
    lane = lax.broadcasted_iota(jnp.int32, (1, 128), 1)
    return lane, (lane < HD, lane >= HD)


L_ONE = 3
FOX_TQ, FOX_TK = 256, 512


def _set_lanes(x, lane, first, cols):
    for n, col in enumerate(cols):
        x = jnp.where(lane == first + n, col, x)
    return x


def _f32_parts(col):
    return [t.astype(f32) for t in _split3(col)]


def _fox_operands(qkv_ref, F_ref, lse_ref, qa, ka, p, rows):
    lane, hm = _head_masks()
    q = qkv_ref[rows, 0:128].astype(f32) * 0.125
    k = qkv_ref[rows, 128:256].astype(f32)
    Fb = F_ref[rows, :]
    for hh in (0, 1):
        free = (1 - hh) * HD
        fcol = jnp.sum(jnp.where(lane == 2 * p + hh, Fb, 0.0), axis=1, keepdims=True)
        qterm = fcol if lse_ref is None else fcol - lse_ref[rows, hh * HD:hh * HD + 1]
        qcols = _f32_parts(qterm) + [1.0] * 3
        kcols = [1.0] * 3 + [-t for t in _f32_parts(fcol)]
        qa[hh, rows, :] = _set_lanes(jnp.where(hm[hh], q, 0.0), lane, free, qcols).astype(bf16)
        ka[hh, rows, :] = _set_lanes(k, lane, free, kcols).astype(bf16)


def _fox_fwd(order, qkva, F):
    tq, tk = FOX_TQ, FOX_TK

    def body(qkv_ref, F_ref, o_ref, lse_ref, qa, ka, vt):
        p = pl.program_id(0)
        keyi = lax.broadcasted_iota(jnp.int32, (tk, 1), 0)
        qryi = lax.broadcasted_iota(jnp.int32, (1, tq), 1)
        sub = lax.broadcasted_iota(jnp.int32, (128, 1), 0)

        def prep(i, c):
            rows = pl.ds(pl.multiple_of(i * tk, tk), tk)
            _fox_operands(qkv_ref, F_ref, None, qa, ka, p, rows)
            vt[i] = qkv_ref[rows, 256:384].astype(f32).T.astype(bf16)
            return c

        lax.fori_loop(0, S // tk, prep, 0)

        def qblock(i, first_half):
            r0 = pl.multiple_of(i * tq, tq)
            qh = [qa[hh, pl.ds(r0, tq), :] for hh in (0, 1)]

            def kv(jb, carry, masked, width):
                keys = pl.ds(pl.multiple_of(jb * tk, tk), width)
                sts = [_dot_nt(ka[hh, keys, :], qh[hh]) for hh in (0, 1)]
                new = []
                for hh in (0, 1):
                    m, l, a = carry[3 * hh:3 * hh + 3]
                    st = sts[hh]
                    if masked:
                        st = jnp.where(jb * tk + keyi[0:width] <= r0 + qryi, st, NEG)
                    mn = jnp.maximum(m, jnp.max(st, axis=0, keepdims=True))
                    al = jnp.exp(m - mn)
                    pt = jnp.exp(st - mn)
                    l = al * l + jnp.sum(pt, axis=0, keepdims=True)
                    a = al * a + _dot(vt[jb, hh * HD:(hh + 1) * HD, 0:width], pt.astype(bf16))
                    new += [mn, l, a]
                return tuple(new)

            init = (jnp.full((1, tq), NEG, f32), jnp.zeros((1, tq), f32), jnp.zeros((HD, tq), f32)) * 2
            last = (r0 + tq - 1) // tk
            carry = lax.fori_loop(0, last, lambda j, cr: kv(j, cr, False, tk), init)
            m0, l0, a0, m1, l1, a1 = kv(last, carry, True, tk // 2 if first_half else tk)
            ot = jnp.concatenate([a0 / l0, a1 / l1], axis=0)
            lt = jnp.where(sub < HD, m0 + jnp.log(l0), m1 + jnp.log(l1))
            o_ref[pl.ds(r0, tq), :] = ot.T.astype(bf16)
            lse_ref[pl.ds(r0, tq), :] = lt.T

        def qpair(t, c):
            qblock(2 * t, True)
            qblock(2 * t + 1, False)
            return c

        assert tk == 2 * tq
        lax.fori_loop(0, S // tk, qpair, 0)

    pair = pl.BlockSpec((S, 128), lambda p: (0, p))
    return _call(
        order, body, (qkva, F), name="fox_fwd", grid=(4,),
        in_specs=[pl.BlockSpec((S, FOX_BLK), lambda p: (0, p)), pl.BlockSpec((S, 128), lambda p: (0, 0))],
        out_specs=[pair, pair],
        out_shape=[jax.ShapeDtypeStruct((S, FOXW), bf16), jax.ShapeDtypeStruct((S, FOXW), f32)],
        scratch_shapes=[pltpu.VMEM((2, S, 128), bf16)] * 2 + [pltpu.VMEM((S // tk, 128, tk), bf16)],
        compiler_params=_params(("parallel",)),
    )


def _permute_in(dst, src, r):
    L = S // r
    for rho in range(r):
        dst[rho * L:(rho + 1) * L, :] = src[pl.ds(rho, L, stride=r), :]


def _permute_out(dst, src, r):
    L = S // r
    for rho in range(r):
        dst[pl.ds(rho, L, stride=r), :] = src[rho * L:(rho + 1) * L, :]


def _band_width(nbl):
    return BAND if nbl == 1 else 2 * BAND


def _band_geometry(bb, nbl):
    r0 = pl.multiple_of(bb * BAND, BAND)
    if nbl == 1:
        k0 = r0
    else:
        k0 = pl.multiple_of(jnp.maximum(bb - 1, 0) * BAND, BAND)
    sub0 = (bb - lax.rem(bb, nbl)) * BAND
    qi = r0 + lax.broadcasted_iota(jnp.int32, (BAND, 1), 0)
    ki = k0 + lax.broadcasted_iota(jnp.int32, (1, _band_width(nbl)), 1)
    diff = qi - ki
    valid = (diff >= 0) & (diff <= BAND) & (ki >= sub0)
    return r0, k0, valid


def _dil_views(ref):
    return [[ref.at[:, pl.ds((3 * role + g) * 128, 128)] for g in range(3)] for role in range(3)]


DIL_UNROLL = 4


def _dil_in_specs():
    return [pl.BlockSpec((S, 128), lambda p, k=k: (0, 9 * p + k)) for k in range(9)]


def _dil_fwd(order, qkvb):
    def body(*refs):
        q_refs, k_refs, v_refs = refs[0:3], refs[3:6], refs[6:9]
        ob_ref, lse_ref, qp, kp, vp, op, lp = refs[9:16]
        on, ln = refs[16:19], refs[19:22]
        _, hm = _head_masks()
        for g, r in enumerate(DIL):
            nbl = S // r // BAND
            if r == 1:
                qs_, ks_, vs_, od, ld = q_refs[g], k_refs[g], v_refs[g], on[g], ln[g]
            else:
                _permute_in(qp, q_refs[g], r)
                _permute_in(kp, k_refs[g], r)
                _permute_in(vp, v_refs[g], r)
                qs_, ks_, vs_, od, ld = qp, kp, vp, op, lp

            def blk(t, c, qs_=qs_, ks_=ks_, vs_=vs_, od=od, ld=ld, nbl=nbl):
                work = []
                for u in range(DIL_UNROLL):
                    r0, k0, valid = _band_geometry(DIL_UNROLL * t + u, nbl)
                    q = qs_[pl.ds(r0, BAND), :] * 0.125
                    kw = ks_[pl.ds(k0, _band_width(nbl)), :].astype(bf16)
                    vw = vs_[pl.ds(k0, _band_width(nbl)), :]
                    for hh in (0, 1):
                        qh = jnp.where(hm[hh], q, 0.0).astype(bf16)
                        work.append((u, hh, r0, valid, vw, _dot_nt(qh, kw)))
                o = [jnp.zeros((BAND, 128), f32)] * DIL_UNROLL
                lse = [jnp.zeros((BAND, 128), f32)] * DIL_UNROLL
                for u, hh, r0, valid, vw, s in work:
                    s = jnp.where(valid, s, NEG)
                    m = jnp.max(s, axis=1, keepdims=True)
                    pr = jnp.exp(s - m)
                    l = jnp.sum(pr, axis=1, keepdims=True)
                    vm = jnp.where(hm[hh], vw, 0.0).astype(bf16)
                    o[u] = o[u] + _dot((pr / l).astype(bf16), vm)
                    lse[u] = jnp.where(hm[hh], m + jnp.log(l), lse[u])
                    if hh == 1:
                        od[pl.ds(r0, BAND), :] = o[u]
                        ld[pl.ds(r0, BAND), :] = lse[u]
                return c

            lax.fori_loop(0, S // BAND // DIL_UNROLL, blk, 0)
            if r != 1:
                _permute_out(on[g], op, r)
                _permute_out(ln[g], lp, r)

        def combine(i, c):
            r0 = pl.multiple_of(i * TQ, TQ)
            ls = [ln[g][pl.ds(r0, TQ), :] for g in range(3)]
            mx = jnp.maximum(jnp.maximum(ls[0], ls[1]), ls[2])
            es = [jnp.exp(l - mx) for l in ls]
            tot = (es[0] + es[1]) + es[2]
            acc = (es[0] / tot) * on[0][pl.ds(r0, TQ), :]
            acc = acc + (es[1] / tot) * on[1][pl.ds(r0, TQ), :]
            acc = acc + (es[2] / tot) * on[2][pl.ds(r0, TQ), :]
            ob_ref[pl.ds(r0, TQ), :] = acc.astype(bf16)
            lse_ref[pl.ds(r0, TQ), :] = mx + jnp.log(tot)
            return c

        lax.fori_loop(0, S // TQ, combine, 0)

    out_blk = pl.BlockSpec((S, 128), lambda p: (0, p))
    return _call(
        order, body, [qkvb] * 9, name="dil_fwd", grid=(2,),
        in_specs=_dil_in_specs(), out_specs=[out_blk, out_blk],
        out_shape=[jax.ShapeDtypeStruct((S, DILOUT), bf16), jax.ShapeDtypeStruct((S, DILOUT), f32)],
        scratch_shapes=[pltpu.VMEM((S, 128), f32)] * 11,
        compiler_params=_params(("parallel",)),
    )


def _branch_mix(order, oa, ob, was, wbs, gates):
    tm = 512

    def body(oa_ref, ob_ref, wa_ref, wb_ref, g_ref, ya_ref, yb_ref, mix_ref):
        oa_b, ob_b = oa_ref[...], ob_ref[...]
        for q in range(NCHIP):
            cols = slice(q * 256, (q + 1) * 256)
            ya = _dot(oa_b, wa_ref[q])
            yb = _dot(ob_b, wb_ref[q])
            ya_ref[:, cols] = ya.astype(bf16)
            yb_ref[:, cols] = yb.astype(bf16)
            ga = g_ref[:, q * 256:(q + 1) * 256].astype(f32)
            gb = g_ref[:, D + q * 256:D + (q + 1) * 256].astype(f32)
            mix_ref[:, cols] = (jax.nn.sigmoid(ga) * ya + jax.nn.sigmoid(gb) * yb).astype(bf16)

    row = lambda w: pl.BlockSpec((tm, w), lambda i: (i, 0))
    full3 = lambda a: pl.BlockSpec(a.shape, lambda i: (0, 0, 0))
    return _call(
        order, body, (oa, ob, was, wbs, gates), name="branch_mix", grid=(S // tm,),
        in_specs=[row(FOXW), row(DILOUT), full3(was), full3(wbs), row(2 * D)],
        out_specs=[row(D), row(D), row(D)],
        out_shape=[jax.ShapeDtypeStruct((S, D), bf16), jax.ShapeDtypeStruct((S, D), bf16),
                   jax.ShapeDtypeStruct((S, D), bf16)],
        compiler_params=_params(("parallel",)),
    )


def _outproj_norm(order, mixed, wout, x, g2):
    tm = 512

    def body(m_ref, w_ref, x_ref, g_ref, x2_ref, h2_ref):
        x2 = x_ref[...] + _dot(m_ref[...], w_ref[...])
        x2_ref[...] = x2
        r = lax.rsqrt(jnp.mean(x2 * x2, axis=-1, keepdims=True) + EPS)
        h2_ref[...] = ((x2 * r) * g_ref[...]).astype(bf16)

    row = pl.BlockSpec((tm, D), lambda i: (i, 0))
    return _call(
        order, body, (mixed, wout, x, g2), name="outproj_norm", grid=(S // tm,),
        in_specs=[row, pl.BlockSpec((D, D), lambda i: (0, 0)), row, pl.BlockSpec((1, D), lambda i: (0, 0))],
        out_specs=[row, row],
        out_shape=[jax.ShapeDtypeStruct((S, D), f32), jax.ShapeDtypeStruct((S, D), bf16)],
        compiler_params=_params(("parallel",)),
    )


def _mlp_up(order, h2, wups):
    tm = 1024

    def body(h_ref, w_ref, ru_ref, a_ref):
        ru = jnp.maximum(_dot(h_ref[...], w_ref[...]), 0.0)
        ru_ref[...] = ru.astype(bf16)
        a_ref[...] = (ru * ru).astype(bf16)

    out = pl.BlockSpec((tm, D), lambda q, i: (i, q))
    return _call(
        order, body, (h2, wups), name="mlp_up", grid=(NCHIP, S // tm),
        in_specs=[pl.BlockSpec((tm, D), lambda q, i: (i, 0)), pl.BlockSpec((None, D, D), lambda q, i: (q, 0, 0))],
        out_specs=[out, out],
        out_shape=[jax.ShapeDtypeStruct((S, DFF), bf16), jax.ShapeDtypeStruct((S, DFF), bf16)],
        compiler_params=_params(("parallel", "parallel")),
    )


def _mlp_down_loss(order, a, wdown, x2, g3, tgt):
    tm = 512

    def body(a_ref, w_ref, x2_ref, g_ref, t_ref, dx_ref, dxb_ref, dg_ref, loss_ref):
        i = pl.program_id(0)
        x3 = x2_ref[...] + _dot(a_ref[...], w_ref[...])
        r = lax.rsqrt(jnp.mean(x3 * x3, axis=-1, keepdims=True) + EPS)
        xh = x3 * r
        g = g_ref[...]
        e = xh * g - t_ref[...]
        part = 0.5 * jnp.sum(jnp.mean(e * e, axis=-1, keepdims=True), axis=0, keepdims=True)
        dy = e * (1.0 / D)
        gdy = dy * g
        dx = r * (gdy - xh * jnp.mean(gdy * xh, axis=-1, keepdims=True))
        dx_ref[...] = dx
        dxb_ref[...] = dx.astype(bf16)

        @pl.when(i == 0)
        def _():
            dg_ref[...] = jnp.zeros_like(dg_ref)
            loss_ref[...] = jnp.zeros_like(loss_ref)

        dg_ref[...] += jnp.sum(dy * xh, axis=0, keepdims=True)
        loss_ref[...] += jnp.broadcast_to(part, (1, 128))

    row = pl.BlockSpec((tm, D), lambda i: (i, 0))
    vec = pl.BlockSpec((1, D), lambda i: (0, 0))
    return _call(
        order, body, (a, wdown, x2, g3, tgt), name="mlp_down_loss", grid=(S // tm,),
        in_specs=[pl.BlockSpec((tm, DFF), lambda i: (i, 0)), pl.BlockSpec((DFF, D), lambda i: (0, 0)), row, vec, row],
        out_specs=[row, row, vec, pl.BlockSpec((1, 128), lambda i: (0, 0))],
        out_shape=[jax.ShapeDtypeStruct((S, D), f32), jax.ShapeDtypeStruct((S, D), bf16),
                   jax.ShapeDtypeStruct((1, D), f32), jax.ShapeDtypeStruct((1, 128), f32)],
        compiler_params=_params(("arbitrary",)),
    )


def _mlp_down_bwd(order, dx3b, wdown, u):
    tm = 512

    def body(d_ref, w_ref, u_ref, du_ref):
        d = d_ref[...]
        for q in range(NCHIP):
            cols = slice(q * D, (q + 1) * D)
            da = _dot_nt(d, w_ref[cols, :])
            du_ref[:, cols] = (da * (2.0 * u_ref[:, cols].astype(f32))).astype(bf16)

    return _call(
        order, body, (dx3b, wdown, u), name="mlp_down_bwd", grid=(S // tm,),
        in_specs=[pl.BlockSpec((tm, D), lambda i: (i, 0)), pl.BlockSpec((DFF, D), lambda i: (0, 0)),
                  pl.BlockSpec((tm, DFF), lambda i: (i, 0))],
        out_specs=pl.BlockSpec((tm, DFF), lambda i: (i, 0)),
        out_shape=jax.ShapeDtypeStruct((S, DFF), bf16),
        compiler_params=_params(("parallel",)),
    )


def _mlp_up_bwd(order, du, wups, x2, dx3, g2):
    tm = 512

    def body(du_ref, w_ref, x2_ref, dx3_ref, g_ref, dx2_ref, dx2b_ref, dg_ref):
        i = pl.program_id(0)
        dh = jnp.zeros((tm, D), f32)
        for q in range(NCHIP):
            dh = dh + _dot_nt(du_ref[:, q * D:(q + 1) * D], w_ref[q])
        x2 = x2_ref[...]
        r = lax.rsqrt(jnp.mean(x2 * x2, axis=-1, keepdims=True) + EPS)
        xh = x2 * r
        gdh = dh * g_ref[...]
        dx2 = dx3_ref[...] + r * (gdh - xh * jnp.mean(gdh * xh, axis=-1, keepdims=True))
        dx2_ref[...] = dx2
        dx2b_ref[...] = dx2.astype(bf16)

        @pl.when(i == 0)
        def _():
            dg_ref[...] = jnp.zeros_like(dg_ref)

        dg_ref[...] += jnp.sum(dh * xh, axis=0, keepdims=True)

    row = pl.BlockSpec((tm, D), lambda i: (i, 0))
    vec = pl.BlockSpec((1, D), lambda i: (0, 0))
    return _call(
        order, body, (du, wups, x2, dx3, g2), name="mlp_up_bwd", grid=(S // tm,),
        in_specs=[pl.BlockSpec((tm, DFF), lambda i: (i, 0)), pl.BlockSpec((NCHIP, D, D), lambda i: (0, 0, 0)),
                  row, row, vec],
        out_specs=[row, row, vec],
        out_shape=[jax.ShapeDtypeStruct((S, D), f32), jax.ShapeDtypeStruct((S, D), bf16),
                   jax.ShapeDtypeStruct((1, D), f32)],
        compiler_params=_params(("arbitrary",)),
    )


def _gate_bwd(order, dx2b, wout, gates, ya, yb):
    tm = 512

    def body(d_ref, w_ref, g_ref, ya_ref, yb_ref, dya_ref, dyb_ref, dproj_ref):
        dm = _dot_nt(d_ref[...], w_ref[...])
        sa = jax.nn.sigmoid(g_ref[:, 0:D].astype(f32))
        sb = jax.nn.sigmoid(g_ref[:, D:2 * D].astype(f32))
        dya_ref[...] = (dm * sa).astype(bf16)
        dyb_ref[...] = (dm * sb).astype(bf16)
        dproj_ref[:, 0:D] = (dm * ya_ref[...].astype(f32) * (sa * (1.0 - sa))).astype(bf16)
        dproj_ref[:, D:2 * D] = (dm * yb_ref[...].astype(f32) * (sb * (1.0 - sb))).astype(bf16)

    row = lambda w: pl.BlockSpec((tm, w), lambda i: (i, 0))
    return _call(
        order, body, (dx2b, wout, gates, ya, yb), name="gate_bwd", grid=(S // tm,),
        in_specs=[row(D), pl.BlockSpec((D, D), lambda i: (0, 0)), row(2 * D), row(D), row(D)],
        out_specs=[row(D), row(D), pl.BlockSpec((tm, 2 * D), lambda i: (i, F_G // (2 * D)))],
        out_shape=[jax.ShapeDtypeStruct((S, D), bf16), jax.ShapeDtypeStruct((S, D), bf16),
                   jax.ShapeDtypeStruct((S, NP), bf16)],
        compiler_params=_params(("parallel",)),
    )


def _branch_bwd(order, dya, dyb, was, wbs):
    tm = 512

    def body(dya_ref, dyb_ref, wa_ref, wb_ref, doa_ref, dob_ref):
        doa = jnp.zeros((tm, FOXW), f32)
        dob = jnp.zeros((tm, DILOUT), f32)
        for q in range(NCHIP):
            cols = slice(q * 256, (q + 1) * 256)
            doa = doa + _dot_nt(dya_ref[:, cols], wa_ref[q])
            dob = dob + _dot_nt(dyb_ref[:, cols], wb_ref[q])
        doa_ref[...] = doa.astype(bf16)
        dob_ref[...] = dob

    row = lambda w: pl.BlockSpec((tm, w), lambda i: (i, 0))
    full3 = lambda a: pl.BlockSpec(a.shape, lambda i: (0, 0, 0))
    return _call(
        order, body, (dya, dyb, was, wbs), name="branch_bwd", grid=(S // tm,),
        in_specs=[row(D), row(D), full3(was), full3(wbs)],
        out_specs=[row(FOXW), row(DILOUT)],
        out_shape=[jax.ShapeDtypeStruct((S, FOXW), bf16), jax.ShapeDtypeStruct((S, DILOUT), f32)],
        compiler_params=_params(("parallel",)),
    )


def _branch_wgrad(order, oa, ob, dya, dyb):
    def body(oa_ref, ob_ref, dya_ref, dyb_ref, dwa_ref, dwb_ref):
        dwa_ref[...] = _dot_tn(oa_ref[...], dya_ref[...])
        dwb_ref[...] = _dot_tn(ob_ref[...], dyb_ref[...])

    full = lambda w: pl.BlockSpec((S, w), lambda q: (0, 0))
    colq = pl.BlockSpec((S, 256), lambda q: (0, q))
    return _call(
        order, body, (oa, ob, dya, dyb), name="branch_wgrad", grid=(NCHIP,),
        in_specs=[full(FOXW), full(DILOUT), colq, colq],
        out_specs=[pl.BlockSpec((None, FOXW, 256), lambda q: (q, 0, 0)),
                   pl.BlockSpec((None, DILOUT, 256), lambda q: (q, 0, 0))],
        out_shape=[jax.ShapeDtypeStruct((NCHIP, FOXW, 256), f32), jax.ShapeDtypeStruct((NCHIP, DILOUT, 256), f32)],
        compiler_params=_params(("parallel",)),
    )


def _fox_bwd(order, qkva, doa, oa, lse, F, dproj):
    tq, tk = FOX_TQ, FOX_TK

    def body(qkv_ref, do_ref, o_ref, lse_ref, F_ref, _dproj_in, dF_ref, dqkv_ref, qa, ka, da, va, kat,
             dk_scr, dv_scr, dqt_scr):
        p = pl.program_id(0)
        lane, hm = _head_masks()
        keyi = lax.broadcasted_iota(jnp.int32, (tk, 1), 0)
        qryi = lax.broadcasted_iota(jnp.int32, (1, tq), 1)

        def prep(i, c):
            rows = pl.ds(pl.multiple_of(i * tk, tk), tk)
            _fox_operands(qkv_ref, F_ref, lse_ref, qa, ka, p, rows)
            do = do_ref[rows, :].astype(f32)
            prod = do * o_ref[rows, :].astype(f32)
            v = qkv_ref[rows, 256:384].astype(f32)
            for hh in (0, 1):
                free = (1 - hh) * HD
                delta = jnp.sum(jnp.where(hm[hh], prod, 0.0), axis=1, keepdims=True)
                da[hh, rows, :] = _set_lanes(jnp.where(hm[hh], do, 0.0), lane, free,
                                             [-t for t in _f32_parts(delta)]).astype(bf16)
                va[hh, rows, :] = _set_lanes(v, lane, free, [1.0] * 3).astype(bf16)
                kat[hh, i] = ka[hh, rows, :].astype(f32).T.astype(bf16)
                dk_scr[hh, rows, :] = jnp.zeros((tk, 128), f32)
                dv_scr[hh, rows, :] = jnp.zeros((tk, 128), f32)
            return c

        lax.fori_loop(0, S // tk, prep, 0)

        def qblock(i, first_half):
            r0 = pl.multiple_of(i * tq, tq)
            qrows = pl.ds(r0, tq)
            qh = [qa[hh, qrows, :] for hh in (0, 1)]
            dh = [da[hh, qrows, :] for hh in (0, 1)]
            dqt_scr[...] = jnp.zeros_like(dqt_scr)

            def kv(jb, c2, masked, width):
                keys = pl.ds(pl.multiple_of(jb * tk, tk), width)
                sts = [_dot_nt(ka[hh, keys, :], qh[hh]) for hh in (0, 1)]
                dps = [_dot_nt(va[hh, keys, :], dh[hh]) for hh in (0, 1)]
                for hh in (0, 1):
                    pt = jnp.exp(sts[hh])
                    if masked:
                        pt = jnp.where(jb * tk + keyi[0:width] <= r0 + qryi, pt, 0.0)
                    dsb = (pt * dps[hh]).astype(bf16)
                    dv_scr[hh, keys, :] += _dot(pt.astype(bf16), dh[hh])
                    dk_scr[hh, keys, :] += _dot(dsb, qh[hh])
                    dqt_scr[hh] += _dot(kat[hh, jb, :, 0:width], dsb)
                return c2

            last = (r0 + tq - 1) // tk
            lax.fori_loop(0, last, lambda j, c2: kv(j, c2, False, tk), 0)
            kv(last, 0, True, tk // 2 if first_half else tk)
            dq0, dq1 = dqt_scr[0].T, dqt_scr[1].T
            dqkv_ref[qrows, 0:128] = (jnp.where(hm[0], dq0, dq1) * 0.125).astype(bf16)
            dF_ref[qrows, :] = jnp.where(lane == 0, dq0[:, HD:HD + 1], jnp.where(lane == 1, dq1[:, 0:1], 0.0))

        def qpair(t, c):
            qblock(2 * t, True)
            qblock(2 * t + 1, False)
            return c

        assert tk == 2 * tq
        lax.fori_loop(0, S // tk, qpair, 0)

        def finish(i, c):
            rows = pl.ds(pl.multiple_of(i * tq, tq), tq)
            dk0, dk1 = dk_scr[0, rows, :], dk_scr[1, rows, :]
            dqkv_ref[rows, 128:256] = jnp.where(hm[0], dk0, dk1).astype(bf16)
            dqkv_ref[rows, 256:384] = jnp.where(hm[0], dv_scr[0, rows, :], dv_scr[1, rows, :]).astype(bf16)
            cs = jnp.where(lane == 0, dk0[:, HD + L_ONE:HD + L_ONE + 1],
                           jnp.where(lane == 1, dk1[:, L_ONE:L_ONE + 1], 0.0))
            dF_ref[rows, :] = dF_ref[rows, :] - cs
            return c

        lax.fori_loop(0, S // tq, finish, 0)

    pair = pl.BlockSpec((S, 128), lambda p: (0, p))
    return _call(
        order, body, (qkva, doa, oa, lse, F, dproj), name="fox_bwd", grid=(4,),
        in_specs=[pl.BlockSpec((S, FOX_BLK), lambda p: (0, p)), pair, pair, pair,
                  pl.BlockSpec((S, 128), lambda p: (0, 0)), pl.BlockSpec(memory_space=pl.ANY)],
        out_specs=[pair, pl.BlockSpec((S, FOX_BLK), lambda p: (0, F_FOX // FOX_BLK + p))],
        out_shape=[jax.ShapeDtypeStruct((S, FOXW), f32), jax.ShapeDtypeStruct((S, NP), bf16)],
        input_output_aliases={5: 1},
        scratch_shapes=[pltpu.VMEM((2, S, 128), bf16)] * 4 + [pltpu.VMEM((2, S // tk, 128, tk), bf16)]
        + [pltpu.VMEM((2, S, 128), f32)] * 2 + [pltpu.VMEM((2, 128, tq), f32)],
        compiler_params=_params(("parallel",)),
    )


def _forget_bwd(order, dF, fa, bpad, dproj):
    nb = S // TQ

    def body(dF_ref, fa_ref, b_ref, _dproj_in, db_ref, dfa_ref):
        rr = lax.broadcasted_iota(jnp.int32, (TQ, TQ), 0)
        cc = lax.broadcasted_iota(jnp.int32, (TQ, TQ), 1)
        upper = (cc >= rr).astype(bf16)
        lane = lax.broadcasted_iota(jnp.int32, (1, 128), 1)
        carry = jnp.zeros((1, 128), f32)
        db = jnp.zeros((1, 128), f32)
        for b in reversed(range(nb)):
            cols = jnp.zeros((TQ, 128), f32)
            for h in range(8):
                c0 = (h // 2) * 128 + h % 2
                cols = jnp.where(lane == h, dF_ref[b * TQ:(b + 1) * TQ, c0:c0 + 1], cols)
            dlf = carry
            for part in _split3(cols):
                dlf = dlf + _dot(upper, part)
            carry = carry + jnp.sum(cols, axis=0, keepdims=True)
            z = fa_ref[b * TQ:(b + 1) * TQ, :] + b_ref[...]
            dz = jnp.where(lane < 8, dlf * jax.nn.sigmoid(-z), 0.0)
            dfa_ref[b * TQ:(b + 1) * TQ, 0:128] = dz.astype(bf16)
            dfa_ref[b * TQ:(b + 1) * TQ, 128:256] = jnp.zeros((TQ, 128), bf16)
            db = db + jnp.sum(dz, axis=0, keepdims=True)
        db_ref[...] = db

    whole = lambda a: pl.BlockSpec(a.shape, lambda i: (0,) * a.ndim)
    return _call(
        order, body, (dF, fa, bpad, dproj), name="forget_bwd", grid=(1,),
        in_specs=[whole(dF), whole(fa), whole(bpad), pl.BlockSpec(memory_space=pl.ANY)],
        out_specs=[pl.BlockSpec((1, 128), lambda i: (0, 0)), pl.BlockSpec((S, 256), lambda i: (0, F_FA // 256))],
        out_shape=[jax.ShapeDtypeStruct((1, 128), f32), jax.ShapeDtypeStruct((S, NP), bf16)],
        input_output_aliases={3: 1},
        compiler_params=_params(("arbitrary",)),
    )


def _dil_bwd(order, qkvb, dob, ob, lseb, rope, dproj):
    c_t, s1_t, s2_t = rope

    def body(*refs):
        q_refs, k_refs, v_refs = refs[0:3], refs[3:6], refs[6:9]
        dob_ref, ob_ref, lse_ref, c_ref, s1_ref, s2_ref, _dproj_in, dqkv_ref = refs[9:17]
        qp, kp, vp, dop, lp, dlp, dln, dqp, dkp, dvp, nat = refs[17:28]
        dq_out, dk_out, dv_out = _dil_views(dqkv_ref)
        _, hm = _head_masks()

        def delta_rows(i, c):
            r0 = pl.multiple_of(i * TQ, TQ)
            prod = dob_ref[pl.ds(r0, TQ), :] * ob_ref[pl.ds(r0, TQ), :].astype(f32)
            d0 = jnp.sum(jnp.where(hm[0], prod, 0.0), axis=1, keepdims=True)
            d1 = jnp.sum(jnp.where(hm[1], prod, 0.0), axis=1, keepdims=True)
            dln[pl.ds(r0, TQ), :] = jnp.where(hm[0], d0, d1)
            return c

        lax.fori_loop(0, S // TQ, delta_rows, 0)

        for g, r in enumerate(DIL):
            nbl = S // r // BAND
            if r == 1:
                srcs = (q_refs[g], k_refs[g], v_refs[g], dob_ref, lse_ref, dln)
            else:
                for dst, src in ((qp, q_refs[g]), (kp, k_refs[g]), (vp, v_refs[g]), (dop, dob_ref),
                                 (lp, lse_ref), (dlp, dln)):
                    _permute_in(dst, src, r)
                srcs = (qp, kp, vp, dop, lp, dlp)
            dkp[...] = jnp.zeros_like(dkp)
            dvp[...] = jnp.zeros_like(dvp)

            def blk(t, c, srcs=srcs, nbl=nbl):
                qs_, ks_, vs_, dos_, ls_, dls_ = srcs
                work = []
                for u in range(DIL_UNROLL):
                    r0, k0, valid = _band_geometry(DIL_UNROLL * t + u, nbl)
                    q = qs_[pl.ds(r0, BAND), :] * 0.125
                    kwf = ks_[pl.ds(k0, _band_width(nbl)), :]
                    kw = kwf.astype(bf16)
                    vw = vs_[pl.ds(k0, _band_width(nbl)), :].astype(bf16)
                    do = dos_[pl.ds(r0, BAND), :]
                    lse = ls_[pl.ds(r0, BAND), :]
                    dlt = dls_[pl.ds(r0, BAND), :]
                    for hh in (0, 1):
                        qh = jnp.where(hm[hh], q, 0.0).astype(bf16)
                        doh = jnp.where(hm[hh], do, 0.0).astype(bf16)
                        kh = jnp.where(hm[hh], kwf, 0.0).astype(bf16)
                        work.append((u, hh, r0, k0, valid, qh, doh, kh, lse[:, hh * HD:hh * HD + 1],
                                     dlt[:, hh * HD:hh * HD + 1], _dot_nt(qh, kw), _dot_nt(doh, vw)))
                for u, hh, r0, k0, valid, qh, doh, kh, lse_h, dlt_h, s, dp in work:
                    if hh == 0:
                        dq = jnp.zeros((BAND, 128), f32)
                        dk = jnp.zeros((_band_width(nbl), 128), f32)
                        dv = jnp.zeros((_band_width(nbl), 128), f32)
                    pr = jnp.where(valid, jnp.exp(s - lse_h), 0.0)
                    dsb = (pr * (dp - dlt_h)).astype(bf16)
                    dv = dv + _dot_tn(pr.astype(bf16), doh)
                    dk = dk + _dot_tn(dsb, qh)
                    dq = dq + _dot(dsb, kh)
                    if hh == 1:
                        dqp[pl.ds(r0, BAND), :] = dq * 0.125
                        dkp[pl.ds(k0, _band_width(nbl)), :] += dk
                        dvp[pl.ds(k0, _band_width(nbl)), :] += dv
                return c

            lax.fori_loop(0, S // BAND // DIL_UNROLL, blk, 0)

            for acc, out, roped in ((dqp, dq_out[g], True), (dkp, dk_out[g], True), (dvp, dv_out[g], False)):
                if r == 1:
                    src = acc
                else:
                    _permute_out(nat, acc, r)
                    src = nat

                def emit(i, c, src=src, out=out, roped=roped):
                    r0 = pl.multiple_of(i * TQ, TQ)
                    d = src[pl.ds(r0, TQ), :]
                    if roped:
                        d = (d * c_ref[pl.ds(r0, TQ), :] + pltpu.roll(d * s1_ref[pl.ds(r0, TQ), :], 8, 1)
                             + pltpu.roll(d * s2_ref[pl.ds(r0, TQ), :], 120, 1))
                    out[pl.ds(r0, TQ), :] = d.astype(bf16)
                    return c

                lax.fori_loop(0, S // TQ, emit, 0)

    pair = pl.BlockSpec((S, 128), lambda p: (0, p))
    tab = pl.BlockSpec((S, 128), lambda p: (0, 0))
    blk_spec = pl.BlockSpec((S, DIL_BLK), lambda p: (0, p))
    return _call(
        order, body, [qkvb] * 9 + [dob, ob, lseb, c_t, s1_t, s2_t, dproj], name="dil_bwd", grid=(2,),
        in_specs=_dil_in_specs() + [pair, pair, pair, tab, tab, tab, pl.BlockSpec(memory_space=pl.ANY)],
        out_specs=blk_spec,
        out_shape=jax.ShapeDtypeStruct((S, NP), bf16),
        input_output_aliases={15: 0},
        scratch_shapes=[pltpu.VMEM((S, 128), f32)] * 11,
        compiler_params=_params(("parallel",)),
    )


def _inproj_bwd(order, dproj, wt, x, dx2, g1):
    tm = 256

    def body(d_ref, w_ref, x_ref, dx2_ref, g_ref, dx_ref, dg_ref):
        dh = _dot(d_ref[...], w_ref[...])
        xb = x_ref[...]
        r = lax.rsqrt(jnp.mean(xb * xb, axis=-1, keepdims=True) + EPS)
        xh = xb * r
        gdh = dh * g_ref[...]
        dx_ref[...] = dx2_ref[...] + r * (gdh - xh * jnp.mean(gdh * xh, axis=-1, keepdims=True))
        dg_ref[...] = jnp.sum(dh * xh, axis=0, keepdims=True)

    row = pl.BlockSpec((tm, D), lambda i: (i, 0))
    vec = pl.BlockSpec((1, D), lambda i: (0, 0))
    return _call(
        order, body, (dproj, wt, x, dx2, g1), name="inproj_bwd", grid=(S // tm,),
        in_specs=[pl.BlockSpec((tm, NP), lambda i: (i, 0)), pl.BlockSpec((NP, D), lambda i: (0, 0)), row, row, vec],
        out_specs=[row, pl.BlockSpec((None, 1, D), lambda i: (i, 0, 0))],
        out_shape=[jax.ShapeDtypeStruct((S, D), f32), jax.ShapeDtypeStruct((S // tm, 1, D), f32)],
        compiler_params=_params(("parallel",)),
    )


HBM = pl.BlockSpec(memory_space=pltpu.HBM)
SEM = pl.BlockSpec(memory_space=pltpu.SEMAPHORE)
SMALL_ROWS = 8


def _comm_call(name, body, bufs, order, sems_in=(), new_sems=(), behind=()):
    nb, ns, nn = len(bufs), len(sems_in), len(new_sems)
    extra = order.token_for(bufs) + list(behind)

    def kern(*refs):
        off = nb + ns + len(extra)
        body(refs[:nb], refs[nb:nb + ns], refs[off:off + nn])
        refs[-1][...] = jnp.zeros((8, 128), f32)

    res = pl.pallas_call(
        kern, name=name,
        in_specs=[HBM] * nb + [SEM] * ns + [pl.BlockSpec(memory_space=pl.ANY)] * len(extra),
        out_specs=[SEM] * nn + [HBM] * nb + [pl.BlockSpec(memory_space=pltpu.VMEM)],
        out_shape=[pltpu.SemaphoreType.DMA((k,)) for k in new_sems] + [pltpu.HBM(b.shape, b.dtype) for b in bufs]
        + [jax.ShapeDtypeStruct((8, 128), f32)],
        input_output_aliases={i: nn + i for i in range(nb)},
        compiler_params=pltpu.CompilerParams(has_side_effects=pltpu.SideEffectType.DATAFLOW_SIDE_EFFECTING),
    )(*[pltpu.with_memory_space_constraint(b, pltpu.HBM) for b in bufs], *sems_in, *extra)
    order.mark(res[-1])
    return list(res[:nn]), list(res[nn:nn + nb])


def _place():
    x, y, c = lax.axis_index("x"), lax.axis_index("y"), lax.axis_index("c")
    chips = [(1 - x, y), (x, 1 - y), (1 - x, 1 - y)]
    return x, y, c, chips


def _rcopy(src, dst, ssem, rsem, dev):
    return pltpu.make_async_remote_copy(src_ref=src, dst_ref=dst, send_sem=ssem, recv_sem=rsem,
                                        device_id=dev, device_id_type=pl.DeviceIdType.MESH)


def _half(nrows, which):
    return pl.ds(which * (nrows // 2), nrows // 2)


def _ici_copies(stack, ssem, rsem, relay):
    x, y, c, chips = _place()
    me_q = 2 * x + y
    sends, recvs = {}, {}
    for a in range(len(stack)):
        rows = _half(stack[a].shape[1], c)
        for j, (cx, cy) in enumerate(chips):
            if relay and a == 0 and j == 2:
                continue
            mine = stack[a].at[me_q, rows]
            sends[a, j] = _rcopy(mine, mine, ssem.at[a * 3 + j], rsem.at[a * 3 + j], (cx, cy, c))
            theirs = stack[a].at[2 * cx + cy, rows]
            recvs[a, j] = _rcopy(theirs, theirs, ssem.at[a * 3 + j], rsem.at[a * 3 + j], (cx, cy, c))
    return sends, recvs


def _relay_copies(win, ssem, rsem):
    x, y, c, chips = _place()
    quarter = win.shape[1] // 4
    sends, recvs = [], []
    for k in range(2):
        rows = pl.ds(c * 2 * quarter + k * quarter, quarter)
        (fx, fy), (tx, ty) = chips[k], chips[1 - k]
        landed = win.at[2 * fx + fy, rows]
        sends.append(_rcopy(landed, landed, ssem.at[k], rsem.at[k], (tx, ty, c)))
        far = win.at[2 * chips[2][0] + chips[2][1], rows]
        recvs.append(_rcopy(far, far, ssem.at[k], rsem.at[k], (tx, ty, c)))
    return sends, recvs


def _allgather_start(name, stacks, order, relay=False):
    n = len(stacks)

    def body(bufs, _, new):
        sends, _r = _ici_copies(bufs, new[0], new[1], relay)
        for cp in sends.values():
            cp.start()

    return _comm_call(name, body, stacks, order, new_sems=(3 * n, 3 * n))


def _forward_copies(stack, ssem, rsem, relay=False):
    x, y, c, chips = _place()
    sib = (x, y, 1 - c)
    sends, recvs = {}, {}
    for a in range(len(stack)):
        for j, (cx, cy) in enumerate(chips):
            if relay and a == 0 and j == 2:
                continue
            landed = stack[a].at[2 * cx + cy, _half(stack[a].shape[1], c)]
            sends[a, j] = _rcopy(landed, landed, ssem.at[a * 3 + j], rsem.at[a * 3 + j], sib)
            other = stack[a].at[2 * cx + cy, _half(stack[a].shape[1], 1 - c)]
            recvs[a, j] = _rcopy(other, other, ssem.at[a * 3 + j], rsem.at[a * 3 + j], sib)
    return sends, recvs


def _far_forward(win, ssem, rsem):
    x, y, c, chips = _place()
    sib, far_q = (x, y, 1 - c), 2 * chips[2][0] + chips[2][1]
    landed, other = win.at[far_q, _half(win.shape[1], c)], win.at[far_q, _half(win.shape[1], 1 - c)]
    return _rcopy(landed, landed, ssem.at[0], rsem.at[0], sib), _rcopy(other, other, ssem.at[0], rsem.at[0], sib)


def _allgather_forward(name, stacks, sems, order, behind=(), relay=False):
    n = len(stacks)

    def body(bufs, taken, new):
        sends, recvs = _ici_copies(bufs, taken[0], taken[1], relay)
        fwd, _r = _forward_copies(bufs, new[0], new[1], relay)
        relay_sends = _relay_copies(bufs[0], new[2], new[3])[0] if relay else []
        for (a, j), arrived in recvs.items():
            arrived.wait_recv()
            fwd[a, j].start()
            if relay and a == 0:
                relay_sends[j].start()
        for cp in sends.values():
            cp.wait_send()

    return _comm_call(name, body, stacks, order, sems_in=sems, behind=behind,
                      new_sems=(3 * n, 3 * n) + ((2, 2) if relay else ()))


def _allgather_finish(name, stacks, sems, order, relay=False):
    def body(bufs, taken, new):
        sends, recvs = _forward_copies(bufs, taken[0], taken[1], relay)
        if relay:
            relay_sends, relay_recvs = _relay_copies(bufs[0], taken[2], taken[3])
            for cp in relay_recvs:
                cp.wait_recv()
            _far_forward(bufs[0], new[0], new[1])[0].start()
            for cp in relay_sends:
                cp.wait_send()
        for cp in sends.values():
            cp.wait_send()
        for cp in recvs.values():
            cp.wait_recv()

    if relay:
        return _comm_call(name, body, stacks, order, sems_in=sems, new_sems=(1, 1))
    return _comm_call(name, body, stacks, order, sems_in=sems)[1]


def _allgather_finish_far(name, stacks, sems, order):
    def body(bufs, taken, _):
        send, recv = _far_forward(bufs[0], taken[0], taken[1])
        send.wait_send()
        recv.wait_recv()

    return _comm_call(name, body, stacks, order, sems_in=sems)[1]


def _window_unit(q, j):
    return C2I[WIN_UNIT0[q] + j]


def _pair_copies(g, t, ssem, rsem, gathered):
    x, y, c, _ = _place()
    sib = (x, y, 1 - c)
    cps, whole = [], []
    for a in range(len(g)):
        if a == 0 and gathered:
            for q in range(NCHIP):
                for j in range(WIN_UNITS // 2):
                    u = jnp.where(c == 0, _window_unit(q, WIN_UNITS // 2 + j), _window_unit(q, j))
                    src = g[0].at[pl.ds(pl.multiple_of(u * UNIT, UNIT), UNIT), :]
                    cps.append(_rcopy(src, t[0].at[q, pl.ds(j * UNIT, UNIT), :], ssem.at[0], rsem.at[0], sib))
            whole.append(_rcopy(t[0], t[0], ssem.at[0], rsem.at[0], sib))
        else:
            cp = _rcopy(g[a].at[:, _half(g[a].shape[1], 1 - c), :], t[a], ssem.at[a], rsem.at[a], sib)
            cps.append(cp)
            whole.append(cp)
    return cps, whole


def _comm_multi(name, parts, order):
    def body(buf_refs, taken, new):
        ib = it = inew = 0
        for pbody, pbufs, psems, pnew, _ in parts:
            pbody(buf_refs[ib:ib + len(pbufs)], taken[it:it + len(psems)], new[inew:inew + len(pnew)])
            ib, it, inew = ib + len(pbufs), it + len(psems), inew + len(pnew)

    sems, bufs = _comm_call(name, body, [b for p in parts for b in p[1]], order,
                            sems_in=[s for p in parts for s in p[2]], new_sems=[k for p in parts for k in p[3]])
    out, ib, inew = [], 0, 0
    for _, pbufs, _, pnew, unpack in parts:
        out.append(unpack(sems[inew:inew + len(pnew)], bufs[ib:ib + len(pbufs)]))
        ib, inew = ib + len(pbufs), inew + len(pnew)
    return out


def _pair_start_part(gs, gathered=False):
    n = len(gs)
    ts = [lax.empty((NCHIP, WIN_ROWS // 2, D) if (a == 0 and gathered) else (NCHIP, g.shape[1] // 2, g.shape[2]), f32)
          for a, g in enumerate(gs)]

    def body(bufs, _, new):
        for cp in _pair_copies(bufs[:n], bufs[n:], new[0], new[1], gathered)[0]:
            cp.start()

    return body, list(gs) + ts, (), (n, n), lambda sems, bufs: (sems, bufs)


def _pair_wait_part(bufs, sems, gathered=False):
    n = len(bufs) // 2

    def body(refs, taken, _):
        for cp in _pair_copies(refs[:n], refs[n:], taken[0], taken[1], gathered)[1]:
            cp.wait_send()
            cp.wait_recv()

    return body, list(bufs), list(sems), (), lambda _, out: (out[:n], out[n:])


ROW_TILE = 256


def _pair_add(order, gs, ts, q_arr, c_arr, name):
    n = len(gs)
    hs = [g.shape[1] // 2 for g in gs]
    nblk = max(1, max(hs) // ROW_TILE)
    assert all(h % (16 * nblk) == 0 for h in hs)

    def body(q_ref, c_ref, *refs):
        for a in range(n):
            s = refs[a][...] + refs[n + a][...]
            refs[3 * n + a][...] = s.astype(bf16)

            @pl.when(pl.program_id(1) == q_ref[0])
            def _():
                refs[2 * n + a][...] = s

    def blk(a, half):
        return pl.BlockSpec((None, hs[a] // nblk, gs[a].shape[2]),
                            lambda i, q, q_ref, c_ref: (q, (c_ref[0] * nblk if half else 0) + i, 0))

    out = _call_indexed(
        order, body, (q_arr, c_arr), list(gs) + list(ts), (nblk, NCHIP),
        [blk(a, True) for a in range(n)] + [blk(a, False) for a in range(n)],
        [pl.BlockSpec((hs[a] // nblk, gs[a].shape[2]), lambda i, q, q_ref, c_ref: (i, 0)) for a in range(n)]
        + [blk(a, False) for a in range(n)],
        name=name,
        out_shape=[jax.ShapeDtypeStruct((hs[a], gs[a].shape[2]), f32) for a in range(n)]
        + [jax.ShapeDtypeStruct((NCHIP, hs[a], gs[a].shape[2]), bf16) for a in range(n)],
        compiler_params=_params(("parallel", "arbitrary")),
    )
    return out[:n], out[n:]


def _pair_add_gathered(order, dwt, t, q_arr, c_arr, name):
    half_units, half_rows = WIN_UNITS // 2, WIN_ROWS // 2
    table = jnp.asarray([_window_unit(q, j) for q in range(NCHIP) for j in range(WIN_UNITS)], jnp.int32)

    def body(tab_ref, q_ref, c_ref, g_hbm, t_ref, own_ref, p16_ref, buf, sem):
        q = pl.program_id(0)

        def gather(w, slot):
            cps = []
            for j in range(half_units):
                u = tab_ref[w * WIN_UNITS + c_ref[0] * half_units + j]
                cps.append(pltpu.make_async_copy(g_hbm.at[pl.ds(pl.multiple_of(u * UNIT, UNIT), UNIT), :],
                                                 buf.at[slot, pl.ds(j * UNIT, UNIT), :], sem.at[slot]))
            return cps

        @pl.when(q == 0)
        def _():
            for cp in gather(0, 0):
                cp.start()

        @pl.when(q + 1 < NCHIP)
        def _():
            for cp in gather(q + 1, (q + 1) % 2):
                cp.start()

        slot = q % 2
        pltpu.make_async_copy(buf.at[slot], buf.at[slot], sem.at[slot]).wait()
        s = buf[slot] + t_ref[...]
        p16_ref[...] = s.astype(bf16)

        @pl.when(q == q_ref[0])
        def _():
            own_ref[...] = s

    blk = pl.BlockSpec((None, half_rows, D), lambda q, tab_ref, q_ref, c_ref: (q, 0, 0))
    return _call_indexed(
        order, body, (table, q_arr, c_arr), (dwt, t), (NCHIP,),
        [pl.BlockSpec(memory_space=pl.ANY), blk],
        [pl.BlockSpec((half_rows, D), lambda q, tab_ref, q_ref, c_ref: (0, 0)), blk],
        scratch_shapes=[pltpu.VMEM((2, half_rows, D), f32), pltpu.SemaphoreType.DMA((2,))],
        name=name,
        out_shape=[jax.ShapeDtypeStruct((half_rows, D), f32),
                   jax.ShapeDtypeStruct((NCHIP, half_rows, D), bf16)],
        compiler_params=_params(("arbitrary",)),
    )


def _shard_copies(p, r, sm, ssem, rsem):
    x, y, c, chips = _place()
    n = len(p)
    sends, recvs = [], []
    for a in range(n):
        for j, (cx, cy) in enumerate(chips):
            k = a * 3 + j
            sends.append(_rcopy(p[a].at[2 * cx + cy], r[a].at[j], ssem.at[k], rsem.at[k], (cx, cy, c)))
            recvs.append(_rcopy(r[a].at[j], r[a].at[j], ssem.at[k], rsem.at[k], (cx, cy, c)))
    if sm is not None:
        mine = sm.at[4 * x + 2 * y + c]
        for i in range(1, 8):
            px = (1 - x) if i & 4 else x
            py = (1 - y) if i & 2 else y
            pc = (1 - c) if i & 1 else c
            k = 3 * n + i - 1
            sends.append(_rcopy(mine, mine, ssem.at[k], rsem.at[k], (px, py, pc)))
            slot = sm.at[4 * px + 2 * py + pc]
            recvs.append(_rcopy(slot, slot, ssem.at[k], rsem.at[k], (px, py, pc)))
    return sends, recvs


def _shard_start_part(p16s, sm=None):
    n = len(p16s)
    rs = [lax.empty((3,) + p.shape[1:], bf16) for p in p16s]
    extra = [] if sm is None else [sm]
    nsem = 3 * n + (7 if sm is not None else 0)

    def body(bufs, _, new):
        sends, _r = _shard_copies(bufs[:n], bufs[n:2 * n], bufs[2 * n] if extra else None, new[0], new[1])
        for cp in sends:
            cp.start()

    return body, list(p16s) + rs + extra, (), (nsem, nsem), lambda sems, bufs: (sems, bufs)


def _shard_wait_part(bufs, sems, n):
    has_sm = len(bufs) > 2 * n

    def body(refs, taken, _):
        sends, recvs = _shard_copies(refs[:n], refs[n:2 * n], refs[2 * n] if has_sm else None, taken[0], taken[1])
        for cp in sends:
            cp.wait_send()
        for cp in recvs:
            cp.wait_recv()

    return body, list(bufs), list(sems), (), lambda _, out: (out[n:2 * n], (out[2 * n] if has_sm else None))


def _shard_sum(order, owns, rs, c_arr, name):
    n = len(owns)
    hs = [o.shape[0] for o in owns]
    nblk = max(1, max(hs) // ROW_TILE)
    assert all(h % (16 * nblk) == 0 for h in hs)
    trs = [h // nblk for h in hs]

    def body(c_ref, *refs):
        for a in range(n):
            s = refs[a][...]
            for j in range(3):
                s = s + refs[n + a][j].astype(f32)
            refs[2 * n + a][...] = s

    out = _call_indexed(
        order, body, (c_arr,), list(owns) + list(rs), (nblk,),
        [pl.BlockSpec((trs[a], owns[a].shape[1]), lambda i, c_ref: (i, 0)) for a in range(n)]
        + [pl.BlockSpec((3, trs[a], owns[a].shape[1]), lambda i, c_ref: (0, i, 0)) for a in range(n)],
        [pl.BlockSpec((trs[a], owns[a].shape[1]), lambda i, c_ref: (c_ref[0] * nblk + i, 0)) for a in range(n)],
        name=name, out_shape=[jax.ShapeDtypeStruct((2 * hs[a], owns[a].shape[1]), f32) for a in range(n)],
        compiler_params=_params(("parallel",)),
    )
    return list(out)


def _swap_copies(full, ssem, rsem):
    x, y, c, _ = _place()
    sends, recvs = [], []
    for a in range(len(full)):
        mine = full[a].at[_half(full[a].shape[0], c)]
        sends.append(_rcopy(mine, mine, ssem.at[a], rsem.at[a], (x, y, 1 - c)))
        other = full[a].at[_half(full[a].shape[0], 1 - c)]
        recvs.append(_rcopy(other, other, ssem.at[a], rsem.at[a], (x, y, 1 - c)))
    return sends, recvs


def _swap_start_part(fulls):
    n = len(fulls)

    def body(bufs, _, new):
        for cp in _swap_copies(bufs, new[0], new[1])[0]:
            cp.start()

    return body, list(fulls), (), (n, n), lambda sems, bufs: (sems, bufs)


def _swap_wait_part(fulls, sems):
    def body(refs, taken, _):
        sends, recvs = _swap_copies(refs, taken[0], taken[1])
        for cp in sends:
            cp.wait_send()
        for cp in recvs:
            cp.wait_recv()

    return body, list(fulls), list(sems), (), lambda _, out: out


def _small_finish(order, sm, ws, ms, vs):
    n = len(ws)

    def body(sm_ref, *refs):
        s = sm_ref[0]
        for d in range(1, 8):
            s = s + sm_ref[d]
        loss_ref, g_refs, upd_refs = refs[3 * n], refs[3 * n + 1:4 * n + 1], refs[4 * n + 1:]
        loss_ref[...] = s[n:n + 1, 0:1]
        for i in range(n):
            g = s[i:i + 1, 0:ws[i].shape[1]]
            g_refs[i][...] = g
            res = _adamw_math(refs[i][...], g, refs[n + i][...], refs[2 * n + i][...])
            for k in range(3):
                upd_refs[3 * i + k][...] = res[k]

    out = _call(order, body, [sm] + list(ws) + list(ms) + list(vs), name="small_sum_adamw",
                out_shape=[jax.ShapeDtypeStruct((1, 1), f32)] + [jax.ShapeDtypeStruct(w.shape, f32) for w in ws]
                + [jax.ShapeDtypeStruct(w.shape, f32) for w in ws for _ in range(3)])
    return out[0], out[1:n + 1], [out[n + 1 + 3 * i:n + 4 + 3 * i] for i in range(n)]


def _adamw_math(w, g, m, v):
    m = ADAM_B1 * m + (1.0 - ADAM_B1) * g
    v = ADAM_B2 * v + (1.0 - ADAM_B2) * (g * g)
    m_hat = m / (1.0 - ADAM_B1 ** ADAM_STEP)
    v_hat = v / (1.0 - ADAM_B2 ** ADAM_STEP)
    return -ADAM_LR * (m_hat / (jnp.sqrt(v_hat) + ADAM_EPS) + ADAM_WD * w), m, v


def _adamw(order, ws, gs, ms, vs, name):
    n = len(ws)
    nblk = max(1, max(w.shape[0] for w in ws) // ROW_TILE)
    assert all(w.shape[0] % (8 * nblk) == 0 for w in ws)

    def body(*refs):
        for a in range(n):
            w_ref, g_ref, m_ref, v_ref = (refs[k * n + a] for k in range(4))
            d_ref, nm_ref, nv_ref, g_out = refs[4 * n + 4 * a:4 * n + 4 * a + 4]
            g = g_ref[...]
            g_out[...] = g
            d_ref[...], nm_ref[...], nv_ref[...] = _adamw_math(w_ref[...], g, m_ref[...], v_ref[...])

    blks = [pl.BlockSpec((w.shape[0] // nblk, w.shape[1]), lambda i: (i, 0)) for w in ws]
    out = _call(
        order, body, list(ws) + list(gs) + list(ms) + list(vs), name=name, grid=(nblk,), in_specs=blks * 4,
        out_specs=[b for b in blks for _ in range(4)],
        out_shape=[jax.ShapeDtypeStruct(w.shape, f32) for w in ws for _ in range(4)],
        compiler_params=_params(("parallel",)),
    )
    return [out[4 * a:4 * a + 4] for a in range(n)]


def _feature_rows(w):
    return jnp.transpose(w, (2, 0, 1))


WIN_STEP = 128
WIN_PIECE = 2 * WIN_STEP


def _window_stacks(order, w, q_arr):
    steps = WIN_ROWS // WIN_STEP
    n_piece = (WIN_ROWS - 2 * WIN_STEP) // WIN_PIECE
    assert n_piece * WIN_PIECE == WIN_ROWS - 2 * WIN_STEP and WIN_STEP % 16 == 0
    assert max(OWN_ROW0) < UNIT <= WIN_STEP and OWN_ROW0[1] + FA_AT == UNIT and FA_AT + N_FA + UNIT <= SHARD_IN
    pad = -(-(WIN_ROWS - SHARD_IN + N_FA) // 8) * 8
    lead = pad - (WIN_ROWS - SHARD_IN)
    assert lead + OWN_ROW0[1] - N_FA >= 0 and lead + max(OWN_ROW0) <= pad and max(OWN_ROW0) <= WIN_ROWS - SHARD_IN

    def body(q_ref, w_ref, win_ref, fa_ref, first, last, til, fabuf, sem):
        i = pl.program_id(0)
        q = q_ref[0]
        chip1 = q == 1
        row0 = jnp.where(q == 0, OWN_ROW0[0], jnp.where(chip1, OWN_ROW0[1], jnp.where(q == 2, OWN_ROW0[2], OWN_ROW0[3])))
        skip = jnp.where(chip1, N_FA, 0)

        def rows(dst, src0, dst0, n, slot):
            return pltpu.make_async_copy(w_ref.at[pl.ds(src0, n)], dst.at[pl.ds(dst0, n)], sem.at[slot])

        def first_copies(on_chip1):
            if on_chip1:
                return [rows(first, 0, OWN_ROW0[1], FA_AT, 0), rows(first, FA_AT + N_FA, UNIT, UNIT, 1)]
            return [rows(first, 0, row0, WIN_STEP, 0)]

        def first_do(act):
            for on_chip1 in (False, True):
                @pl.when(chip1 if on_chip1 else jnp.logical_not(chip1))
                def _():
                    for c in first_copies(on_chip1):
                        act(c)

        def piece(j):
            dst0 = WIN_STEP + j * WIN_PIECE
            return pltpu.make_async_copy(w_ref.at[pl.ds(dst0 - row0 + skip, WIN_PIECE), 0],
                                         til.at[pl.ds(dst0, WIN_PIECE)], sem.at[2 + j])

        last_copy = rows(last, SHARD_IN - WIN_STEP, lead + row0 - skip, WIN_STEP, 2 + n_piece)
        fa_copy = rows(fabuf, FA_AT, 0, N_FA, 3 + n_piece)

        @pl.when(i == 0)
        def _():
            first[pl.ds(0, UNIT)] = jnp.zeros((UNIT, 1, D), f32)
            last[...] = jnp.zeros(last.shape, f32)
            fabuf[pl.ds(N_FA, FA_ROWS - N_FA)] = jnp.zeros((FA_ROWS - N_FA, 1, D), f32)
            fa_copy.start()
            first_do(lambda c: c.start())
            for j in range(n_piece):
                piece(j).start()
            last_copy.start()
            fa_copy.wait()
            fa_ref[...] = fabuf[...].reshape(FA_ROWS, D).astype(bf16)
            first_do(lambda c: c.wait())
            win_ref[...] = first[pl.ds(0, WIN_STEP)].reshape(WIN_STEP, D).astype(bf16)

        for j in range(n_piece):
            @pl.when(i == 1 + j * (WIN_PIECE // WIN_STEP))
            def _():
                piece(j).wait()

        @pl.when(jnp.logical_and(i > 0, i < steps - 1))
        def _():
            win_ref[...] = til[pl.ds(pl.multiple_of(i * WIN_STEP, WIN_STEP), WIN_STEP)].astype(bf16)

        @pl.when(i == steps - 1)
        def _():
            last_copy.wait()
            win_ref[...] = last[pl.ds(pad, WIN_STEP)].reshape(WIN_STEP, D).astype(bf16)

    return _call_indexed(
        order, body, (q_arr,), (w,), (steps,), [pl.BlockSpec(memory_space=pl.ANY)],
        [pl.BlockSpec((None, WIN_STEP, D), lambda i, q: (q[0], i, 0)),
         pl.BlockSpec((None, FA_ROWS, D), lambda i, q: (q[0], 0, 0))],
        scratch_shapes=[pltpu.VMEM((WIN_STEP + UNIT, 1, D), f32), pltpu.VMEM((pad + WIN_STEP, 1, D), f32),
                        pltpu.VMEM((WIN_ROWS, D), f32), pltpu.VMEM((FA_ROWS, 1, D), f32),
                        pltpu.SemaphoreType.DMA((4 + n_piece,))],
        name="window_w_in", out_shape=[jax.ShapeDtypeStruct((NCHIP, WIN_ROWS, D), bf16),
                                       jax.ShapeDtypeStruct((NCHIP, FA_ROWS, D), bf16)],
        compiler_params=_params(("arbitrary",)),
    )


def _unfeature_rows(a):
    return jnp.transpose(a, (1, 2, 0))


ADAM_IN_ROWS = 134
ADAM_IN_STEPS = SHARD_IN // ADAM_IN_ROWS
ADAM_IN_CHUNK = 136
ADAM_IN_CHUNKS = ADAM_IN_STEPS + 1
ADAM_IN_BUF = WIN_ROWS + N_FA


def _adamw_w_in(order, w, gwin, gfa, m, v, q_arr):
    assert ADAM_IN_CHUNK * ADAM_IN_STEPS < WIN_ROWS <= ADAM_IN_CHUNK * ADAM_IN_CHUNKS
    assert OWN_ROW0[NCHIP - 1] + ADAM_IN_ROWS <= 2 * ADAM_IN_CHUNK and ADAM_IN_CHUNK >= ADAM_IN_ROWS
    last0 = ADAM_IN_CHUNK * ADAM_IN_STEPS
    cut = OWN_ROW0[1] + FA_AT

    def body(q_ref, w_ref, gwin_ref, gfa_ref, m_ref, v_ref, go_ref, d_ref, nm_ref, nv_ref, buf, sem):
        i = pl.program_id(0)
        q = q_ref[0]
        chip1 = q == 1
        shift = jnp.where(chip1, N_FA, 0)

        def copy(src_ref, src0, dst0, n, slot):
            return pltpu.make_async_copy(src_ref.at[pl.ds(src0, n)], buf.at[pl.ds(dst0, n), 0], sem.at[slot])

        def first(on_chip1):
            if on_chip1:
                return [copy(gwin_ref, 0, 0, cut, 0), copy(gfa_ref, 0, cut, N_FA, ADAM_IN_CHUNKS),
                        copy(gwin_ref, cut, cut + N_FA, ADAM_IN_CHUNK - cut - N_FA, ADAM_IN_CHUNKS + 1)]
            return [copy(gwin_ref, 0, 0, ADAM_IN_CHUNK, 0)]

        def middle(k):
            return [copy(gwin_ref, pl.multiple_of(k * ADAM_IN_CHUNK - shift, 8), k * ADAM_IN_CHUNK, ADAM_IN_CHUNK, k)]

        def last(on_chip1):
            n = WIN_ROWS - last0 + (N_FA if on_chip1 else 0)
            return [copy(gwin_ref, WIN_ROWS - n, last0, n, ADAM_IN_STEPS)]

        def both(make, act):
            for on_chip1 in (False, True):
                @pl.when(chip1 if on_chip1 else jnp.logical_not(chip1))
                def _():
                    for c in make(on_chip1):
                        act(c)

        @pl.when(i == 0)
        def _():
            both(first, lambda c: c.start())
            for k in range(1, ADAM_IN_STEPS):
                middle(k)[0].start()
            both(last, lambda c: c.start())
            both(first, lambda c: c.wait())

        @pl.when(i < ADAM_IN_STEPS - 1)
        def _():
            middle(i + 1)[0].wait()

        @pl.when(i == ADAM_IN_STEPS - 1)
        def _():
            both(last, lambda c: c.wait())

        row0 = jnp.where(q == 0, OWN_ROW0[0], jnp.where(chip1, OWN_ROW0[1], jnp.where(q == 2, OWN_ROW0[2], OWN_ROW0[3])))
        g = buf[pl.ds(row0 + i * ADAM_IN_ROWS, ADAM_IN_ROWS)]
        go_ref[...] = g
        d_ref[...], nm_ref[...], nv_ref[...] = _adamw_math(w_ref[...], g, m_ref[...], v_ref[...])

    blk = pl.BlockSpec((ADAM_IN_ROWS, 1, D), lambda i, q: (i, 0, 0))
    hbm = pl.BlockSpec(memory_space=pl.ANY)
    return _call_indexed(
        order, body, (q_arr,), (w, gwin, gfa, m, v), (ADAM_IN_STEPS,), [blk, hbm, hbm, blk, blk], [blk] * 4,
        scratch_shapes=[pltpu.VMEM((ADAM_IN_BUF, 1, D), f32), pltpu.SemaphoreType.DMA((ADAM_IN_CHUNKS + 2,))],
        name="adamw_w_in", out_shape=[jax.ShapeDtypeStruct((SHARD_IN, 1, D), f32)] * 4,
        compiler_params=_params(("arbitrary",)),
    )


def kernel(x, norm_attn_g, w_in, b_forget, w_branch_a, w_branch_b, w_out, norm_mlp_g, w_up, w_down, norm_final_g, loss_target, m_norm_attn_g, m_w_in, m_b_forget, m_w_branch_a, m_w_branch_b, m_w_out, m_norm_mlp_g, m_w_up, m_w_down, m_norm_final_g, v_norm_attn_g, v_w_in, v_b_forget, v_w_branch_a, v_w_branch_b, v_w_out, v_norm_mlp_g, v_w_up, v_w_down, v_norm_final_g):
    xi, yi, ci = lax.axis_index("x"), lax.axis_index("y"), lax.axis_index("c")
    q_me = 2 * xi + yi
    c_arr = jnp.reshape(ci, (1,)).astype(jnp.int32)
    q_arr = jnp.reshape(q_me, (1,)).astype(jnp.int32)
    x_, tgt = x[0], loss_target[0]

    names = ["w_branch_a", "w_branch_b", "w_out", "w_up", "w_down"]
    big = dict(zip(names, [w_branch_a[0], w_branch_b[0], w_out[0], w_up[0], w_down[0]]))
    ms = dict(zip(names, [m_w_branch_a[0], m_w_branch_b[0], m_w_out[0], m_w_up[0], m_w_down[0]]))
    vs = dict(zip(names, [v_w_branch_a[0], v_w_branch_b[0], v_w_out[0], v_w_up[0], v_w_down[0]]))
    grad, upd = {}, {}
    order = _Order()

    def run(fn, *args, **kw):
        return fn(order, *args, **kw)

    def own_slot(a):
        return lax.dynamic_update_slice(lax.empty((NCHIP,) + a.shape, a.dtype), a[None], (q_me, 0, 0))

    sem_in, in_s = _allgather_start("allgather_start_in", run(_window_stacks, _feature_rows(w_in), q_arr), order,
                                    relay=True)
    rope = _rope_tables(order.tok[0, 0])
    sem_f, in_s = _allgather_forward("allgather_forward_in", in_s, sem_in, order, behind=rope, relay=True)
    sem_rest, rest = _allgather_start("allgather_start_rest", [own_slot(w.astype(bf16)) for w in big.values()], order)
    sem_f, in_s = _allgather_finish("allgather_finish_in", in_s, sem_f, order, relay=True)
    wins, fas = _allgather_finish_far("allgather_finish_far_in", in_s, sem_f, order)
    wt = run(_assemble_win, wins, fas)

    bpad = jnp.pad(b_forget, ((0, 0), (0, 120)))
    h1, qkvb, qkva, gates, fa = run(_norm_inproj, x_, norm_attn_g, wt, rope)
    F = run(_forget_cumsum, fa, bpad)
    oa, lsea = run(_fox_fwd, qkva, F)
    sem_f, rest = _allgather_forward("allgather_forward_rest", rest, sem_rest, order)
    ob, lseb = run(_dil_fwd, qkvb)
    was, wbs, wouts, wups, wdowns = _allgather_finish("allgather_finish_rest", rest, sem_f, order)
    wout = wouts.reshape(D, D)
    wdown = wdowns.reshape(DFF, D)
    ya, yb, mixed = run(_branch_mix, oa, ob, was, wbs, gates)
    x2, h2 = run(_outproj_norm, mixed, wout, x_, norm_mlp_g)
    u, a = run(_mlp_up, h2, wups)
    dx3, dx3b, dg3, loss_part = run(_mlp_down_loss, a, wdown, x2, norm_final_g.reshape(1, D), tgt)

    def comm(name, *parts):
        return _comm_multi(name, list(parts), order)

    def adamw_group(group, fulls, name):
        res = run(_adamw, [big[nm] for nm in group], fulls, [ms[nm] for nm in group], [vs[nm] for nm in group], name)
        for nm, r in zip(group, res):
            *upd[nm], grad[nm] = r

    grp_a, grp_b, grp_c = ["w_down", "w_up"], ["w_out", "w_branch_a", "w_branch_b"], ["w_in", "w_in_fa"]
    du = run(_mlp_down_bwd, dx3b, wdown, u)
    dwdown = run(_mm, a, dx3b, "tn", f32, 1024, D, "wgrad_down")
    dwup = run(_mm, h2, du, "tn", f32, D, 1024, "wgrad_up", stack_cols=True)
    ((sem_pa, buf_pa),) = comm("pair_start_a", _pair_start_part([dwdown.reshape(NCHIP, DFF // NCHIP, D), dwup]))
    dx2, dx2b, dg2 = run(_mlp_up_bwd, du, wups, x2, dx3, norm_mlp_g)
    ((gs, ts),) = comm("pair_wait_a", _pair_wait_part(buf_pa, sem_pa))
    p32_a, p16_a = run(_pair_add, gs, ts, q_arr, c_arr, "pair_add_a")
    ((sem_sa, buf_sa),) = comm("shard_start_a", _shard_start_part(p16_a))
    dya, dyb, dproj = run(_gate_bwd, dx2b, wout, gates, ya, yb)
    dwout = run(_mm, mixed, dx2b, "tn", f32, D, D, "wgrad_out")
    doa, dob = run(_branch_bwd, dya, dyb, was, wbs)
    dwas, dwbs = run(_branch_wgrad, oa, ob, dya, dyb)
    ((sem_pb, buf_pb),) = comm("pair_start_b", _pair_start_part([dwout.reshape(NCHIP, D // NCHIP, D), dwas, dwbs]))
    dF, dproj = run(_fox_bwd, qkva, doa, oa, lsea, F, dproj)
    (gs, ts), (rs_a, _) = comm("pair_wait_b_shard_wait_a", _pair_wait_part(buf_pb, sem_pb),
                               _shard_wait_part(buf_sa, sem_sa, len(grp_a)))
    p32_b, p16_b = run(_pair_add, gs, ts, q_arr, c_arr, "pair_add_b")
    fulls_a = run(_shard_sum, p32_a, rs_a, c_arr, "shard_sum_a")
    (sem_wa, fulls_a), (sem_sb, buf_sb) = comm("swap_start_a_shard_start_b", _swap_start_part(fulls_a),
                                               _shard_start_part(p16_b))
    dbf, dproj = run(_forget_bwd, dF, fa, bpad, dproj)
    dproj = run(_dil_bwd, qkvb, dob, ob, lseb, rope, dproj)
    (rs_b, _), fulls_a = comm("shard_wait_b_swap_wait_a", _shard_wait_part(buf_sb, sem_sb, len(grp_b)),
                              _swap_wait_part(fulls_a, sem_wa))
    fulls_b = run(_shard_sum, p32_b, rs_b, c_arr, "shard_sum_b")
    ((sem_wb, fulls_b),) = comm("swap_start_b", _swap_start_part(fulls_b))
    dwt = run(_mm, dproj, h1, "tn", f32, 512, D, "wgrad_in")
    dwfa = jnp.broadcast_to(dwt[F_FA:F_FA + FA_ROWS][None], (NCHIP, FA_ROWS, D))
    (sem_pc, buf_pc), fulls_b = comm("pair_start_c_swap_wait_b", _pair_start_part([dwt, dwfa], gathered=True),
                                     _swap_wait_part(fulls_b, sem_wb))
    adamw_group(grp_b, fulls_b, "adamw_b")
    (((dwt_c, dwfa_c), (t_in, t_fa)),) = comm("pair_wait_c", _pair_wait_part(buf_pc, sem_pc, gathered=True))
    p32_in, p16_in = run(_pair_add_gathered, dwt_c, t_in, q_arr, c_arr, "pair_add_w_in")
    p32_fa, p16_fa = run(_pair_add, [dwfa_c], [t_fa], q_arr, c_arr, "pair_add_w_in_fa")
    ((sem_sc, buf_sc),) = comm("shard_start_c", _shard_start_part([p16_in, *p16_fa]))
    gx, dg1 = run(_inproj_bwd, dproj, wt, x_, dx2, norm_attn_g)
    dg1 = jnp.sum(dg1, axis=0)
    adamw_group(grp_a, fulls_a, "adamw_a")
    small = jnp.concatenate([dg1, dg2, dg3, jnp.pad(dbf[:, 0:8], ((0, 0), (0, D - 8))),
                             jnp.pad(loss_part, ((0, 0), (0, D - 128))),
                             jnp.zeros((SMALL_ROWS - 5, D), f32)], axis=0)
    sm = lax.dynamic_update_slice(lax.empty((8, SMALL_ROWS, D), f32), small[None],
                                  (4 * xi + 2 * yi + ci, 0, 0))
    (sem_sm, buf_sm), (rs_c, _) = comm("small_start_shard_wait_c", _shard_start_part([], sm),
                                       _shard_wait_part(buf_sc, sem_sc, len(grp_c)))
    fulls_c = (run(_shard_sum, [p32_in], rs_c[0:1], c_arr, "shard_sum_w_in")
               + run(_shard_sum, p32_fa, rs_c[1:2], c_arr, "shard_sum_w_in_fa"))
    (sem_wc, fulls_c), (_, sm) = comm("swap_start_c_small_wait", _swap_start_part(fulls_c),
                                      _shard_wait_part(buf_sm, sem_sm, 0))
    smalls = ["norm_attn_g", "norm_mlp_g", "norm_final_g", "b_forget"]
    loss, gs, res = run(_small_finish, sm, [norm_attn_g, norm_mlp_g, norm_final_g.reshape(1, D), b_forget],
                        [m_norm_attn_g, m_norm_mlp_g, m_norm_final_g.reshape(1, D), m_b_forget],
                        [v_norm_attn_g, v_norm_mlp_g, v_norm_final_g.reshape(1, D), v_b_forget])
    loss = loss.reshape(())
    grad.update(zip(smalls, gs))
    upd.update(zip(smalls, res))

    ((gwin, gfa),) = comm("swap_wait_c", _swap_wait_part(fulls_c, sem_wc))
    res_in = run(_adamw_w_in, _feature_rows(w_in), gwin, gfa, _feature_rows(m_w_in), _feature_rows(v_w_in), q_arr)
    grad["w_in"] = _unfeature_rows(res_in[0])
    upd["w_in"] = [_unfeature_rows(t) for t in res_in[1:]]

    order_out = ["norm_attn_g", "w_in", "b_forget", "w_branch_a", "w_branch_b", "w_out", "norm_mlp_g", "w_up",
                 "w_down", "norm_final_g"]
    shapes = dict(norm_attn_g=norm_attn_g.shape, w_in=w_in.shape, b_forget=b_forget.shape,
                  w_branch_a=w_branch_a.shape, w_branch_b=w_branch_b.shape, w_out=w_out.shape,
                  norm_mlp_g=norm_mlp_g.shape, w_up=w_up.shape, w_down=w_down.shape, norm_final_g=norm_final_g.shape)
    outs = [loss, gx.reshape(x.shape)]
    outs += [grad[nm].reshape(shapes[nm]) for nm in order_out]
    for k in range(3):
        outs += [upd[nm][k].reshape(shapes[nm]) for nm in order_out]
    return tuple(outs)
```

```python
import jax
import jax.numpy as jnp
from jax import lax
from jax.experimental import pallas as pl
from jax.experimental.pallas import tpu as pltpu

f32 = jnp.float32
bf16 = jnp.bfloat16

S = 2048
D = 1024
DFF = 4096
HD = 64
FOXW = 512
DILOUT = 256
DIL = (1, 4, 16)
BAND = 128
EPS = 1e-6
NEG = -1e30
ROPE_THETA = 500000.0
NCHIP = 4
TQ = 256

ADAM_LR, ADAM_B1, ADAM_B2, ADAM_EPS, ADAM_WD, ADAM_STEP = 0.001, 0.9, 0.999, 1e-08, 0.01, 10
VMEM_LIMIT = 56 * 1024 * 1024

UNIT = 64
NP = 6144
F_DIL, F_FOX, F_FA, F_G = 0, 2304, 3840, 4096
DIL_BLK, FOX_BLK = 1152, 384
WIN_UNITS, WIN_ROWS = 24, 1536
WIN_UNIT0 = (0, 23, 45, 68)
OWN_ROW0 = (0, 2, 60, 62)
SHARD_IN = 1474
N_FA = 8
FA_AT = 1536 - SHARD_IN
FA_ROWS = 32


def _compact_to_internal():
    c2i = {}
    for p in range(2):
        for role in range(3):
            for g in range(3):
                for hh in range(2):
                    c2i[24 + 12 * role + 4 * g + 2 * p + hh] = 18 * p + 6 * role + 2 * g + hh
    for p in range(4):
        for role in range(3):
            for hh in range(2):
                c2i[8 * role + 2 * p + hh] = F_FOX // UNIT + 6 * p + 2 * role + hh
    for j in range(32):
        c2i[60 + j] = F_G // UNIT + j
    return c2i


C2I = _compact_to_internal()
OVERLAP_UNITS = (23, 45, 46, 68)


def _params(sem=None):
    return pltpu.CompilerParams(dimension_semantics=sem, vmem_limit_bytes=VMEM_LIMIT)


class _Order:
    def __init__(self):
        self.tok = None

    def mark(self, v):
        self.tok = v

    def token_for(self, args):
        return [] if self.tok is None or any(self.tok is a for a in args) else [self.tok]


def _call(order, body, args, in_specs=None, **kw):
    args = list(args)
    n_in = len(args)
    if in_specs is None:
        in_specs = [pl.BlockSpec(memory_space=pltpu.VMEM)] * n_in
    kern = body
    extra = order.token_for(args)
    if extra:
        in_specs = list(in_specs) + [pl.BlockSpec(memory_space=pl.ANY)]

        def kern(*refs):
            body(*refs[:n_in], *refs[n_in + 1:])

    out = pl.pallas_call(kern, in_specs=in_specs, **kw)(*args, *extra)
    order.mark(out[0] if isinstance(out, (tuple, list)) else out)
    return out


def _call_indexed(order, body, scalars, args, grid, in_specs, out_specs, scratch_shapes=(), **kw):
    args, in_specs = list(args), list(in_specs)
    n_front = len(scalars) + len(args)
    kern = body
    extra = order.token_for(args)
    if extra:
        in_specs.append(pl.BlockSpec(memory_space=pl.ANY))

        def kern(*refs):
            body(*refs[:n_front], *refs[n_front + 1:])

    out = pl.pallas_call(
        kern, grid_spec=pltpu.PrefetchScalarGridSpec(num_scalar_prefetch=len(scalars), grid=grid, in_specs=in_specs,
                                                     out_specs=out_specs, scratch_shapes=scratch_shapes),
        **kw)(*scalars, *args, *extra)
    order.mark(out[0] if isinstance(out, (tuple, list)) else out)
    return out


def _dot(a, b):
    return jnp.dot(a, b, preferred_element_type=f32)


def _dot_nt(a, b):
    return lax.dot_general(a, b, (((1,), (1,)), ((), ())), preferred_element_type=f32)


def _dot_tn(a, b):
    return lax.dot_general(a, b, (((0,), (0,)), ((), ())), preferred_element_type=f32)


def _split3(x):
    hi = x.astype(bf16)
    r1 = x - hi.astype(f32)
    mid = r1.astype(bf16)
    lo = (r1 - mid.astype(f32)).astype(bf16)
    return hi, mid, lo


def _rope_tables(after):
    half = 8
    inv_freq = jnp.power(jnp.float32(ROPE_THETA), -jnp.arange(half, dtype=f32) * 2.0 / 16)
    ang = (jnp.arange(S).astype(f32) + after)[:, None] * inv_freq[None, :]
    cos, sin = jnp.cos(ang), jnp.sin(ang)
    one = jnp.ones((S, HD - 16), f32)
    zero = jnp.zeros((S, HD - 16), f32)
    z8 = jnp.zeros((S, 8), f32)
    c = jnp.concatenate([cos, cos, one], axis=1)
    s1 = jnp.concatenate([-sin, z8, zero], axis=1)
    s2 = jnp.concatenate([z8, sin, zero], axis=1)
    return tuple(jnp.concatenate([t, t], axis=1) for t in (c, s1, s2))


def _mm(order, a, b, mode, out_dtype, tm, tn, name, stack_cols=False):
    if mode == "nn":
        (M, K), (_, N) = a.shape, b.shape
        a_spec = pl.BlockSpec((tm, K), lambda i, j: (i, 0))
        b_spec = pl.BlockSpec((K, tn), lambda i, j: (0, j))
        dot = _dot
    elif mode == "nt":
        (M, K), (N, _) = a.shape, b.shape
        a_spec = pl.BlockSpec((tm, K), lambda i, j: (i, 0))
        b_spec = pl.BlockSpec((tn, K), lambda i, j: (j, 0))
        dot = _dot_nt
    else:
        (K, M), (_, N) = a.shape, b.shape
        a_spec = pl.BlockSpec((K, tm), lambda i, j: (0, i))
        b_spec = pl.BlockSpec((K, tn), lambda i, j: (0, j))
        dot = _dot_tn

    def body(a_ref, b_ref, o_ref):
        o_ref[...] = dot(a_ref[...], b_ref[...]).astype(out_dtype)

    if stack_cols:
        assert tm == M
        out_spec = pl.BlockSpec((None, tm, tn), lambda i, j: (j, 0, 0))
        out_shape = jax.ShapeDtypeStruct((N // tn, M, tn), out_dtype)
    else:
        out_spec = pl.BlockSpec((tm, tn), lambda i, j: (i, j))
        out_shape = jax.ShapeDtypeStruct((M, N), out_dtype)
    return _call(
        order, body, (a, b), name=name, grid=(M // tm, N // tn), in_specs=[a_spec, b_spec],
        out_specs=out_spec, out_shape=out_shape,
        compiler_params=_params(("parallel", "parallel")),
    )


def _assemble_win(order, wins, fas):
    def body(win_ref, fa_ref, o_ref):
        q = pl.program_id(0)

        @pl.when(q == 0)
        def _():
            o_ref[...] = jnp.zeros_like(o_ref)

        for k in range(NCHIP):
            @pl.when(q == k)
            def _(k=k):
                for j in range(WIN_UNITS):
                    cu = WIN_UNIT0[k] + j
                    dst = pl.ds(C2I[cu] * UNIT, UNIT)
                    if cu in OVERLAP_UNITS:
                        o_ref[dst, :] += win_ref[j * UNIT:(j + 1) * UNIT, :]
                    else:
                        o_ref[dst, :] = win_ref[j * UNIT:(j + 1) * UNIT, :]
                if k == 1:
                    o_ref[F_FA:F_FA + FA_ROWS, :] = fa_ref[...]

    return _call(
        order, body, (wins, fas), name="assemble_w_in", grid=(NCHIP,),
        in_specs=[pl.BlockSpec((None, WIN_ROWS, D), lambda q: (q, 0, 0)),
                  pl.BlockSpec((None, FA_ROWS, D), lambda q: (1, 0, 0))],
        out_specs=pl.BlockSpec((NP, D), lambda q: (0, 0)),
        out_shape=jax.ShapeDtypeStruct((NP, D), bf16),
        compiler_params=_params(("arbitrary",)),
    )


def _norm_inproj(order, x, g1, wt, rope):
    tm = 256
    c_t, s1_t, s2_t = rope

    def body(x_ref, g_ref, w_ref, c_ref, s1_ref, s2_ref, h_ref, qkvb_ref, qkva_ref, gates_ref, fa_ref):
        xb = x_ref[...]
        r = lax.rsqrt(jnp.mean(xb * xb, axis=-1, keepdims=True) + EPS)
        h = ((xb * r) * g_ref[...]).astype(bf16)
        h_ref[...] = h
        c, s1, s2 = c_ref[...], s1_ref[...], s2_ref[...]
        for p in range(2):
            pb = _dot_nt(h, w_ref[F_DIL + p * DIL_BLK:F_DIL + (p + 1) * DIL_BLK, :])
            for ch in range(DIL_BLK // 128):
                pc = pb[:, ch * 128:(ch + 1) * 128]
                if ch < 6:
                    pc = pc * c + pltpu.roll(pc, 120, 1) * s1 + pltpu.roll(pc, 8, 1) * s2
                qkvb_ref[:, p * DIL_BLK + ch * 128:p * DIL_BLK + (ch + 1) * 128] = pc
        qkva_ref[...] = _dot_nt(h, w_ref[F_FOX:F_FA, :]).astype(bf16)
        fa_ref[...] = _dot_nt(h, w_ref[F_FA:F_FA + 128, :])
        gates_ref[...] = _dot_nt(h, w_ref[F_G:NP, :]).astype(bf16)

    row = lambda w: pl.BlockSpec((tm, w), lambda i: (i, 0))
    return _call(
        order, body, (x, g1, wt, c_t, s1_t, s2_t), name="norm_inproj", grid=(S // tm,),
        in_specs=[row(D), pl.BlockSpec((1, D), lambda i: (0, 0)), pl.BlockSpec((NP, D), lambda i: (0, 0)),
                  row(128), row(128), row(128)],
        out_specs=[row(D), row(2 * DIL_BLK), row(4 * FOX_BLK), row(2 * D), row(128)],
        out_shape=[jax.ShapeDtypeStruct((S, D), bf16), jax.ShapeDtypeStruct((S, 2 * DIL_BLK), f32),
                   jax.ShapeDtypeStruct((S, 4 * FOX_BLK), bf16), jax.ShapeDtypeStruct((S, 2 * D), bf16),
                   jax.ShapeDtypeStruct((S, 128), f32)],
        compiler_params=_params(("parallel",)),
    )


def _forget_cumsum(order, fa, bpad):
    nb = S // TQ

    def body(fa_ref, b_ref, F_ref):
        rr = lax.broadcasted_iota(jnp.int32, (TQ, TQ), 0)
        cc = lax.broadcasted_iota(jnp.int32, (TQ, TQ), 1)
        tri = (rr >= cc).astype(bf16)
        lane = lax.broadcasted_iota(jnp.int32, (1, 128), 1)
        carry = jnp.zeros((1, 128), f32)
        for b in range(nb):
            z = fa_ref[b * TQ:(b + 1) * TQ, :] + b_ref[...]
            lf = jnp.minimum(z, 0.0) - jnp.log(1.0 + jnp.exp(-jnp.abs(z)))
            lf = jnp.where(lane < 8, lf, 0.0)
            hi, mid, lo = _split3(lf)
            fb = (_dot(tri, hi) + _dot(tri, mid)) + _dot(tri, lo) + carry
            F_ref[b * TQ:(b + 1) * TQ, :] = fb
            carry = fb[TQ - 1:TQ, :]

    return _call(
        order, body, (fa, bpad), name="forget_cumsum",
        out_shape=jax.ShapeDtypeStruct((S, 128), f32),
        compiler_params=_params(),
    )


def _head_masks():
    lane = lax.broadcasted_iota(jnp.int32, (1, 128), 1)
    return lane, (lane < HD, lane >= HD)


L_ONE = 3
FOX_TQ, FOX_TK = 256, 512


def _set_lanes(x, lane, first, cols):
    for n, col in enumerate(cols):
        x = jnp.where(lane == first + n, col, x)
    return x


def _f32_parts(col):
    return [t.astype(f32) for t in _split3(col)]


def _fox_operands(qkv_ref, F_ref, lse_ref, qa, ka, p, rows):
    lane, hm = _head_masks()
    q = qkv_ref[rows, 0:128].astype(f32) * 0.125
    k = qkv_ref[rows, 128:256].astype(f32)
    Fb = F_ref[rows, :]
    for hh in (0, 1):
        free = (1 - hh) * HD
        fcol = jnp.sum(jnp.where(lane == 2 * p + hh, Fb, 0.0), axis=1, keepdims=True)
        qterm = fcol if lse_ref is None else fcol - lse_ref[rows, hh * HD:hh * HD + 1]
        qcols = _f32_parts(qterm) + [1.0] * 3
        kcols = [1.0] * 3 + [-t for t in _f32_parts(fcol)]
        qa[hh, rows, :] = _set_lanes(jnp.where(hm[hh], q, 0.0), lane, free, qcols).astype(bf16)
        ka[hh, rows, :] = _set_lanes(k, lane, free, kcols).astype(bf16)


def _fox_fwd(order, qkva, F):
    tq, tk = FOX_TQ, FOX_TK

    def body(qkv_ref, F_ref, o_ref, lse_ref, qa, ka, vt):
        p = pl.program_id(0)
        keyi = lax.broadcasted_iota(jnp.int32, (tk, 1), 0)
        qryi = lax.broadcasted_iota(jnp.int32, (1, tq), 1)
        sub = lax.broadcasted_iota(jnp.int32, (128, 1), 0)

        def prep(i, c):
            rows = pl.ds(pl.multiple_of(i * tk, tk), tk)
            _fox_operands(qkv_ref, F_ref, None, qa, ka, p, rows)
            vt[i] = qkv_ref[rows, 256:384].astype(f32).T.astype(bf16)
            return c

        lax.fori_loop(0, S // tk, prep, 0)

        def qblock(i, first_half):
            r0 = pl.multiple_of(i * tq, tq)
            qh = [qa[hh, pl.ds(r0, tq), :] for hh in (0, 1)]

            def kv(jb, carry, masked, width):
                keys = pl.ds(pl.multiple_of(jb * tk, tk), width)
                sts = [_dot_nt(ka[hh, keys, :], qh[hh]) for hh in (0, 1)]
                new = []
                for hh in (0, 1):
                    m, l, a = carry[3 * hh:3 * hh + 3]
                    st = sts[hh]
                    if masked:
                        st = jnp.where(jb * tk + keyi[0:width] <= r0 + qryi, st, NEG)
                    mn = jnp.maximum(m, jnp.max(st, axis=0, keepdims=True))
                    al = jnp.exp(m - mn)
                    pt = jnp.exp(st - mn)
                    l = al * l + jnp.sum(pt, axis=0, keepdims=True)
                    a = al * a + _dot(vt[jb, hh * HD:(hh + 1) * HD, 0:width], pt.astype(bf16))
                    new += [mn, l, a]
                return tuple(new)

            init = (jnp.full((1, tq), NEG, f32), jnp.zeros((1, tq), f32), jnp.zeros((HD, tq), f32)) * 2
            last = (r0 + tq - 1) // tk
            carry = lax.fori_loop(0, last, lambda j, cr: kv(j, cr, False, tk), init)
            m0, l0, a0, m1, l1, a1 = kv(last, carry, True, tk // 2 if first_half else tk)
            ot = jnp.concatenate([a0 / l0, a1 / l1], axis=0)
            lt = jnp.where(sub < HD, m0 + jnp.log(l0), m1 + jnp.log(l1))
            o_ref[pl.ds(r0, tq), :] = ot.T.astype(bf16)
            lse_ref[pl.ds(r0, tq), :] = lt.T

        def qpair(t, c):
            qblock(2 * t, True)
            qblock(2 * t + 1, False)
            return c

        assert tk == 2 * tq
        lax.fori_loop(0, S // tk, qpair, 0)

    pair = pl.BlockSpec((S, 128), lambda p: (0, p))
    return _call(
        order, body, (qkva, F), name="fox_fwd", grid=(4,),
        in_specs=[pl.BlockSpec((S, FOX_BLK), lambda p: (0, p)), pl.BlockSpec((S, 128), lambda p: (0, 0))],
        out_specs=[pair, pair],
        out_shape=[jax.ShapeDtypeStruct((S, FOXW), bf16), jax.ShapeDtypeStruct((S, FOXW), f32)],
        scratch_shapes=[pltpu.VMEM((2, S, 128), bf16)] * 2 + [pltpu.VMEM((S // tk, 128, tk), bf16)],
        compiler_params=_params(("parallel",)),
    )


def _permute_in(dst, src, r):
    L = S // r
    for rho in range(r):
        dst[rho * L:(rho + 1) * L, :] = src[pl.ds(rho, L, stride=r), :]


def _permute_out(dst, src, r):
    L = S // r
    for rho in range(r):
        dst[pl.ds(rho, L, stride=r), :] = src[rho * L:(rho + 1) * L, :]


def _band_width(nbl):
    return BAND if nbl == 1 else 2 * BAND


def _band_geometry(bb, nbl):
    r0 = pl.multiple_of(bb * BAND, BAND)
    if nbl == 1:
        k0 = r0
    else:
        k0 = pl.multiple_of(jnp.maximum(bb - 1, 0) * BAND, BAND)
    sub0 = (bb - lax.rem(bb, nbl)) * BAND
    qi = r0 + lax.broadcasted_iota(jnp.int32, (BAND, 1), 0)
    ki = k0 + lax.broadcasted_iota(jnp.int32, (1, _band_width(nbl)), 1)
    diff = qi - ki
    valid = (diff >= 0) & (diff <= BAND) & (ki >= sub0)
    return r0, k0, valid


def _dil_views(ref):
    return [[ref.at[:, pl.ds((3 * role + g) * 128, 128)] for g in range(3)] for role in range(3)]


DIL_UNROLL = 4


def _dil_in_specs():
    return [pl.BlockSpec((S, 128), lambda p, k=k: (0, 9 * p + k)) for k in range(9)]


def _dil_fwd(order, qkvb):
    def body(*refs):
        q_refs, k_refs, v_refs = refs[0:3], refs[3:6], refs[6:9]
        ob_ref, lse_ref, qp, kp, vp, op, lp = refs[9:16]
        on, ln = refs[16:19], refs[19:22]
        _, hm = _head_masks()
        for g, r in enumerate(DIL):
            nbl = S // r // BAND
            if r == 1:
                qs_, ks_, vs_, od, ld = q_refs[g], k_refs[g], v_refs[g], on[g], ln[g]
            else:
                _permute_in(qp, q_refs[g], r)
                _permute_in(kp, k_refs[g], r)
                _permute_in(vp, v_refs[g], r)
                qs_, ks_, vs_, od, ld = qp, kp, vp, op, lp

            def blk(t, c, qs_=qs_, ks_=ks_, vs_=vs_, od=od, ld=ld, nbl=nbl):
                work = []
                for u in range(DIL_UNROLL):
                    r0, k0, valid = _band_geometry(DIL_UNROLL * t + u, nbl)
                    q = qs_[pl.ds(r0, BAND), :] * 0.125
                    kw = ks_[pl.ds(k0, _band_width(nbl)), :].astype(bf16)
                    vw = vs_[pl.ds(k0, _band_width(nbl)), :]
                    for hh in (0, 1):
                        qh = jnp.where(hm[hh], q, 0.0).astype(bf16)
                        work.append((u, hh, r0, valid, vw, _dot_nt(qh, kw)))
                o = [jnp.zeros((BAND, 128), f32)] * DIL_UNROLL
                lse = [jnp.zeros((BAND, 128), f32)] * DIL_UNROLL
                for u, hh, r0, valid, vw, s in work:
                    s = jnp.where(valid, s, NEG)
                    m = jnp.max(s, axis=1, keepdims=True)
                    pr = jnp.exp(s - m)
                    l = jnp.sum(pr, axis=1, keepdims=True)
                    vm = jnp.where(hm[hh], vw, 0.0).astype(bf16)
                    o[u] = o[u] + _dot((pr / l).astype(bf16), vm)
                    lse[u] = jnp.where(hm[hh], m + jnp.log(l), lse[u])
                    if hh == 1:
                        od[pl.ds(r0, BAND), :] = o[u]
                        ld[pl.ds(r0, BAND), :] = lse[u]
                return c

            lax.fori_loop(0, S // BAND // DIL_UNROLL, blk, 0)
            if r != 1:
                _permute_out(on[g], op, r)
                _permute_out(ln[g], lp, r)

        def combine(i, c):
            r0 = pl.multiple_of(i * TQ, TQ)
            ls = [ln[g][pl.ds(r0, TQ), :] for g in range(3)]
            mx = jnp.maximum(jnp.maximum(ls[0], ls[1]), ls[2])
            es = [jnp.exp(l - mx) for l in ls]
            tot = (es[0] + es[1]) + es[2]
            acc = (es[0] / tot) * on[0][pl.ds(r0, TQ), :]
            acc = acc + (es[1] / tot) * on[1][pl.ds(r0, TQ), :]
            acc = acc + (es[2] / tot) * on[2][pl.ds(r0, TQ), :]
            ob_ref[pl.ds(r0, TQ), :] = acc.astype(bf16)
            lse_ref[pl.ds(r0, TQ), :] = mx + jnp.log(tot)
            return c

        lax.fori_loop(0, S // TQ, combine, 0)

    out_blk = pl.BlockSpec((S, 128), lambda p: (0, p))
    return _call(
        order, body, [qkvb] * 9, name="dil_fwd", grid=(2,),
        in_specs=_dil_in_specs(), out_specs=[out_blk, out_blk],
        out_shape=[jax.ShapeDtypeStruct((S, DILOUT), bf16), jax.ShapeDtypeStruct((S, DILOUT), f32)],
        scratch_shapes=[pltpu.VMEM((S, 128), f32)] * 11,
        compiler_params=_params(("parallel",)),
    )


def _branch_mix(order, oa, ob, was, wbs, gates):
    tm = 512

    def body(oa_ref, ob_ref, wa_ref, wb_ref, g_ref, ya_ref, yb_ref, mix_ref):
        oa_b, ob_b = oa_ref[...], ob_ref[...]
        for q in range(NCHIP):
            cols = slice(q * 256, (q + 1) * 256)
            ya = _dot(oa_b, wa_ref[q])
            yb = _dot(ob_b, wb_ref[q])
            ya_ref[:, cols] = ya.astype(bf16)
            yb_ref[:, cols] = yb.astype(bf16)
            ga = g_ref[:, q * 256:(q + 1) * 256].astype(f32)
            gb = g_ref[:, D + q * 256:D + (q + 1) * 256].astype(f32)
            mix_ref[:, cols] = (jax.nn.sigmoid(ga) * ya + jax.nn.sigmoid(gb) * yb).astype(bf16)

    row = lambda w: pl.BlockSpec((tm, w), lambda i: (i, 0))
    full3 = lambda a: pl.BlockSpec(a.shape, lambda i: (0, 0, 0))
    return _call(
        order, body, (oa, ob, was, wbs, gates), name="branch_mix", grid=(S // tm,),
        in_specs=[row(FOXW), row(DILOUT), full3(was), full3(wbs), row(2 * D)],
        out_specs=[row(D), row(D), row(D)],
        out_shape=[jax.ShapeDtypeStruct((S, D), bf16), jax.ShapeDtypeStruct((S, D), bf16),
                   jax.ShapeDtypeStruct((S, D), bf16)],
        compiler_params=_params(("parallel",)),
    )


def _outproj_norm(order, mixed, wout, x, g2):
    tm = 512

    def body(m_ref, w_ref, x_ref, g_ref, x2_ref, h2_ref):
        x2 = x_ref[...] + _dot(m_ref[...], w_ref[...])
        x2_ref[...] = x2
        r = lax.rsqrt(jnp.mean(x2 * x2, axis=-1, keepdims=True) + EPS)
        h2_ref[...] = ((x2 * r) * g_ref[...]).astype(bf16)

    row = pl.BlockSpec((tm, D), lambda i: (i, 0))
    return _call(
        order, body, (mixed, wout, x, g2), name="outproj_norm", grid=(S // tm,),
        in_specs=[row, pl.BlockSpec((D, D), lambda i: (0, 0)), row, pl.BlockSpec((1, D), lambda i: (0, 0))],
        out_specs=[row, row],
        out_shape=[jax.ShapeDtypeStruct((S, D), f32), jax.ShapeDtypeStruct((S, D), bf16)],
        compiler_params=_params(("parallel",)),
    )


def _mlp_up(order, h2, wups):
    tm = 1024

    def body(h_ref, w_ref, ru_ref, a_ref):
        ru = jnp.maximum(_dot(h_ref[...], w_ref[...]), 0.0)
        ru_ref[...] = ru.astype(bf16)
        a_ref[...] = (ru * ru).astype(bf16)

    out = pl.BlockSpec((tm, D), lambda q, i: (i, q))
    return _call(
        order, body, (h2, wups), name="mlp_up", grid=(NCHIP, S // tm),
        in_specs=[pl.BlockSpec((tm, D), lambda q, i: (i, 0)), pl.BlockSpec((None, D, D), lambda q, i: (q, 0, 0))],
        out_specs=[out, out],
        out_shape=[jax.ShapeDtypeStruct((S, DFF), bf16), jax.ShapeDtypeStruct((S, DFF), bf16)],
        compiler_params=_params(("parallel", "parallel")),
    )


def _mlp_down_loss(order, a, wdown, x2, g3, tgt):
    tm = 512

    def body(a_ref, w_ref, x2_ref, g_ref, t_ref, dx_ref, dxb_ref, dg_ref, loss_ref):
        i = pl.program_id(0)
        x3 = x2_ref[...] + _dot(a_ref[...], w_ref[...])
        r = lax.rsqrt(jnp.mean(x3 * x3, axis=-1, keepdims=True) + EPS)
        xh = x3 * r
        g = g_ref[...]
        e = xh * g - t_ref[...]
        part = 0.5 * jnp.sum(jnp.mean(e * e, axis=-1, keepdims=True), axis=0, keepdims=True)
        dy = e * (1.0 / D)
        gdy = dy * g
        dx = r * (gdy - xh * jnp.mean(gdy * xh, axis=-1, keepdims=True))
        dx_ref[...] = dx
        dxb_ref[...] = dx.astype(bf16)

        @pl.when(i == 0)
        def _():
            dg_ref[...] = jnp.zeros_like(dg_ref)
            loss_ref[...] = jnp.zeros_like(loss_ref)

        dg_ref[...] += jnp.sum(dy * xh, axis=0, keepdims=True)
        loss_ref[...] += jnp.broadcast_to(part, (1, 128))

    row = pl.BlockSpec((tm, D), lambda i: (i, 0))
    vec = pl.BlockSpec((1, D), lambda i: (0, 0))
    return _call(
        order, body, (a, wdown, x2, g3, tgt), name="mlp_down_loss", grid=(S // tm,),
        in_specs=[pl.BlockSpec((tm, DFF), lambda i: (i, 0)), pl.BlockSpec((DFF, D), lambda i: (0, 0)), row, vec, row],
        out_specs=[row, row, vec, pl.BlockSpec((1, 128), lambda i: (0, 0))],
        out_shape=[jax.ShapeDtypeStruct((S, D), f32), jax.ShapeDtypeStruct((S, D), bf16),
                   jax.ShapeDtypeStruct((1, D), f32), jax.ShapeDtypeStruct((1, 128), f32)],
        compiler_params=_params(("arbitrary",)),
    )


def _mlp_down_bwd(order, dx3b, wdown, u):
    tm = 512

    def body(d_ref, w_ref, u_ref, du_ref):
        d = d_ref[...]
        for q in range(NCHIP):
            cols = slice(q * D, (q + 1) * D)
            da = _dot_nt(d, w_ref[cols, :])
            du_ref[:, cols] = (da * (2.0 * u_ref[:, cols].astype(f32))).astype(bf16)

    return _call(
        order, body, (dx3b, wdown, u), name="mlp_down_bwd", grid=(S // tm,),
        in_specs=[pl.BlockSpec((tm, D), lambda i: (i, 0)), pl.BlockSpec((DFF, D), lambda i: (0, 0)),
                  pl.BlockSpec((tm, DFF), lambda i: (i, 0))],
        out_specs=pl.BlockSpec((tm, DFF), lambda i: (i, 0)),
        out_shape=jax.ShapeDtypeStruct((S, DFF), bf16),
        compiler_params=_params(("parallel",)),
    )


def _mlp_up_bwd(order, du, wups, x2, dx3, g2):
    tm = 512

    def body(du_ref, w_ref, x2_ref, dx3_ref, g_ref, dx2_ref, dx2b_ref, dg_ref):
        i = pl.program_id(0)
        dh = jnp.zeros((tm, D), f32)
        for q in range(NCHIP):
            dh = dh + _dot_nt(du_ref[:, q * D:(q + 1) * D], w_ref[q])
        x2 = x2_ref[...]
        r = lax.rsqrt(jnp.mean(x2 * x2, axis=-1, keepdims=True) + EPS)
        xh = x2 * r
        gdh = dh * g_ref[...]
        dx2 = dx3_ref[...] + r * (gdh - xh * jnp.mean(gdh * xh, axis=-1, keepdims=True))
        dx2_ref[...] = dx2
        dx2b_ref[...] = dx2.astype(bf16)

        @pl.when(i == 0)
        def _():
            dg_ref[...] = jnp.zeros_like(dg_ref)

        dg_ref[...] += jnp.sum(dh * xh, axis=0, keepdims=True)

    row = pl.BlockSpec((tm, D), lambda i: (i, 0))
    vec = pl.BlockSpec((1, D), lambda i: (0, 0))
    return _call(
        order, body, (du, wups, x2, dx3, g2), name="mlp_up_bwd", grid=(S // tm,),
        in_specs=[pl.BlockSpec((tm, DFF), lambda i: (i, 0)), pl.BlockSpec((NCHIP, D, D), lambda i: (0, 0, 0)),
                  row, row, vec],
        out_specs=[row, row, vec],
        out_shape=[jax.ShapeDtypeStruct((S, D), f32), jax.ShapeDtypeStruct((S, D), bf16),
                   jax.ShapeDtypeStruct((1, D), f32)],
        compiler_params=_params(("arbitrary",)),
    )


def _gate_bwd(order, dx2b, wout, gates, ya, yb):
    tm = 512

    def body(d_ref, w_ref, g_ref, ya_ref, yb_ref, dya_ref, dyb_ref, dproj_ref):
        dm = _dot_nt(d_ref[...], w_ref[...])
        sa = jax.nn.sigmoid(g_ref[:, 0:D].astype(f32))
        sb = jax.nn.sigmoid(g_ref[:, D:2 * D].astype(f32))
        dya_ref[...] = (dm * sa).astype(bf16)
        dyb_ref[...] = (dm * sb).astype(bf16)
        dproj_ref[:, 0:D] = (dm * ya_ref[...].astype(f32) * (sa * (1.0 - sa))).astype(bf16)
        dproj_ref[:, D:2 * D] = (dm * yb_ref[...].astype(f32) * (sb * (1.0 - sb))).astype(bf16)

    row = lambda w: pl.BlockSpec((tm, w), lambda i: (i, 0))
    return _call(
        order, body, (dx2b, wout, gates, ya, yb), name="gate_bwd", grid=(S // tm,),
        in_specs=[row(D), pl.BlockSpec((D, D), lambda i: (0, 0)), row(2 * D), row(D), row(D)],
        out_specs=[row(D), row(D), pl.BlockSpec((tm, 2 * D), lambda i: (i, F_G // (2 * D)))],
        out_shape=[jax.ShapeDtypeStruct((S, D), bf16), jax.ShapeDtypeStruct((S, D), bf16),
                   jax.ShapeDtypeStruct((S, NP), bf16)],
        compiler_params=_params(("parallel",)),
    )


def _branch_bwd(order, dya, dyb, was, wbs):
    tm = 512

    def body(dya_ref, dyb_ref, wa_ref, wb_ref, doa_ref, dob_ref):
        doa = jnp.zeros((tm, FOXW), f32)
        dob = jnp.zeros((tm, DILOUT), f32)
        for q in range(NCHIP):
            cols = slice(q * 256, (q + 1) * 256)
            doa = doa + _dot_nt(dya_ref[:, cols], wa_ref[q])
            dob = dob + _dot_nt(dyb_ref[:, cols], wb_ref[q])
        doa_ref[...] = doa.astype(bf16)
        dob_ref[...] = dob

    row = lambda w: pl.BlockSpec((tm, w), lambda i: (i, 0))
    full3 = lambda a: pl.BlockSpec(a.shape, lambda i: (0, 0, 0))
    return _call(
        order, body, (dya, dyb, was, wbs), name="branch_bwd", grid=(S // tm,),
        in_specs=[row(D), row(D), full3(was), full3(wbs)],
        out_specs=[row(FOXW), row(DILOUT)],
        out_shape=[jax.ShapeDtypeStruct((S, FOXW), bf16), jax.ShapeDtypeStruct((S, DILOUT), f32)],
        compiler_params=_params(("parallel",)),
    )


def _branch_wgrad(order, oa, ob, dya, dyb):
    def body(oa_ref, ob_ref, dya_ref, dyb_ref, dwa_ref, dwb_ref):
        dwa_ref[...] = _dot_tn(oa_ref[...], dya_ref[...])
        dwb_ref[...] = _dot_tn(ob_ref[...], dyb_ref[...])

    full = lambda w: pl.BlockSpec((S, w), lambda q: (0, 0))
    colq = pl.BlockSpec((S, 256), lambda q: (0, q))
    return _call(
        order, body, (oa, ob, dya, dyb), name="branch_wgrad", grid=(NCHIP,),
        in_specs=[full(FOXW), full(DILOUT), colq, colq],
        out_specs=[pl.BlockSpec((None, FOXW, 256), lambda q: (q, 0, 0)),
                   pl.BlockSpec((None, DILOUT, 256), lambda q: (q, 0, 0))],
        out_shape=[jax.ShapeDtypeStruct((NCHIP, FOXW, 256), f32), jax.ShapeDtypeStruct((NCHIP, DILOUT, 256), f32)],
        compiler_params=_params(("parallel",)),
    )


def _fox_bwd(order, qkva, doa, oa, lse, F, dproj):
    tq, tk = FOX_TQ, FOX_TK

    def body(qkv_ref, do_ref, o_ref, lse_ref, F_ref, _dproj_in, dF_ref, dqkv_ref, qa, ka, da, va, kat,
             dk_scr, dv_scr, dqt_scr):
        p = pl.program_id(0)
        lane, hm = _head_masks()
        keyi = lax.broadcasted_iota(jnp.int32, (tk, 1), 0)
        qryi = lax.broadcasted_iota(jnp.int32, (1, tq), 1)

        def prep(i, c):
            rows = pl.ds(pl.multiple_of(i * tk, tk), tk)
            _fox_operands(qkv_ref, F_ref, lse_ref, qa, ka, p, rows)
            do = do_ref[rows, :].astype(f32)
            prod = do * o_ref[rows, :].astype(f32)
            v = qkv_ref[rows, 256:384].astype(f32)
            for hh in (0, 1):
                free = (1 - hh) * HD
                delta = jnp.sum(jnp.where(hm[hh], prod, 0.0), axis=1, keepdims=True)
                da[hh, rows, :] = _set_lanes(jnp.where(hm[hh], do, 0.0), lane, free,
                                             [-t for t in _f32_parts(delta)]).astype(bf16)
                va[hh, rows, :] = _set_lanes(v, lane, free, [1.0] * 3).astype(bf16)
                kat[hh, i] = ka[hh, rows, :].astype(f32).T.astype(bf16)
                dk_scr[hh, rows, :] = jnp.zeros((tk, 128), f32)
                dv_scr[hh, rows, :] = jnp.zeros((tk, 128), f32)
            return c

        lax.fori_loop(0, S // tk, prep, 0)

        def qblock(i, first_half):
            r0 = pl.multiple_of(i * tq, tq)
            qrows = pl.ds(r0, tq)
            qh = [qa[hh, qrows, :] for hh in (0, 1)]
            dh = [da[hh, qrows, :] for hh in (0, 1)]
            dqt_scr[...] = jnp.zeros_like(dqt_scr)

            def kv(jb, c2, masked, width):
                keys = pl.ds(pl.multiple_of(jb * tk, tk), width)
                sts = [_dot_nt(ka[hh, keys, :], qh[hh]) for hh in (0, 1)]
                dps = [_dot_nt(va[hh, keys, :], dh[hh]) for hh in (0, 1)]
                for hh in (0, 1):
                    pt = jnp.exp(sts[hh])
                    if masked:
                        pt = jnp.where(jb * tk + keyi[0:width] <= r0 + qryi, pt, 0.0)
                    dsb = (pt * dps[hh]).astype(bf16)
                    dv_scr[hh, keys, :] += _dot(pt.astype(bf16), dh[hh])
                    dk_scr[hh, keys, :] += _dot(dsb, qh[hh])
                    dqt_scr[hh] += _dot(kat[hh, jb, :, 0:width], dsb)
                return c2

            last = (r0 + tq - 1) // tk
            lax.fori_loop(0, last, lambda j, c2: kv(j, c2, False, tk), 0)
            kv(last, 0, True, tk // 2 if first_half else tk)
            dq0, dq1 = dqt_scr[0].T, dqt_scr[1].T
            dqkv_ref[qrows, 0:128] = (jnp.where(hm[0], dq0, dq1) * 0.125).astype(bf16)
            dF_ref[qrows, :] = jnp.where(lane == 0, dq0[:, HD:HD + 1], jnp.where(lane == 1, dq1[:, 0:1], 0.0))

        def qpair(t, c):
            qblock(2 * t, True)
            qblock(2 * t + 1, False)
            return c

        assert tk == 2 * tq
        lax.fori_loop(0, S // tk, qpair, 0)

        def finish(i, c):
            rows = pl.ds(pl.multiple_of(i * tq, tq), tq)
            dk0, dk1 = dk_scr[0, rows, :], dk_scr[1, rows, :]
            dqkv_ref[rows, 128:256] = jnp.where(hm[0], dk0, dk1).astype(bf16)
            dqkv_ref[rows, 256:384] = jnp.where(hm[0], dv_scr[0, rows, :], dv_scr[1, rows, :]).astype(bf16)
            cs = jnp.where(lane == 0, dk0[:, HD + L_ONE:HD + L_ONE + 1],
                           jnp.where(lane == 1, dk1[:, L_ONE:L_ONE + 1], 0.0))
            dF_ref[rows, :] = dF_ref[rows, :] - cs
            return c

        lax.fori_loop(0, S // tq, finish, 0)

    pair = pl.BlockSpec((S, 128), lambda p: (0, p))
    return _call(
        order, body, (qkva, doa, oa, lse, F, dproj), name="fox_bwd", grid=(4,),
        in_specs=[pl.BlockSpec((S, FOX_BLK), lambda p: (0, p)), pair, pair, pair,
                  pl.BlockSpec((S, 128), lambda p: (0, 0)), pl.BlockSpec(memory_space=pl.ANY)],
        out_specs=[pair, pl.BlockSpec((S, FOX_BLK), lambda p: (0, F_FOX // FOX_BLK + p))],
        out_shape=[jax.ShapeDtypeStruct((S, FOXW), f32), jax.ShapeDtypeStruct((S, NP), bf16)],
        input_output_aliases={5: 1},
        scratch_shapes=[pltpu.VMEM((2, S, 128), bf16)] * 4 + [pltpu.VMEM((2, S // tk, 128, tk), bf16)]
        + [pltpu.VMEM((2, S, 128), f32)] * 2 + [pltpu.VMEM((2, 128, tq), f32)],
        compiler_params=_params(("parallel",)),
    )


def _forget_bwd(order, dF, fa, bpad, dproj):
    nb = S // TQ

    def body(dF_ref, fa_ref, b_ref, _dproj_in, db_ref, dfa_ref):
        rr = lax.broadcasted_iota(jnp.int32, (TQ, TQ), 0)
        cc = lax.broadcasted_iota(jnp.int32, (TQ, TQ), 1)
        upper = (cc >= rr).astype(bf16)
        lane = lax.broadcasted_iota(jnp.int32, (1, 128), 1)
        carry = jnp.zeros((1, 128), f32)
        db = jnp.zeros((1, 128), f32)
        for b in reversed(range(nb)):
            cols = jnp.zeros((TQ, 128), f32)
            for h in range(8):
                c0 = (h // 2) * 128 + h % 2
                cols = jnp.where(lane == h, dF_ref[b * TQ:(b + 1) * TQ, c0:c0 + 1], cols)
            dlf = carry
            for part in _split3(cols):
                dlf = dlf + _dot(upper, part)
            carry = carry + jnp.sum(cols, axis=0, keepdims=True)
            z = fa_ref[b * TQ:(b + 1) * TQ, :] + b_ref[...]
            dz = jnp.where(lane < 8, dlf * jax.nn.sigmoid(-z), 0.0)
            dfa_ref[b * TQ:(b + 1) * TQ, 0:128] = dz.astype(bf16)
            dfa_ref[b * TQ:(b + 1) * TQ, 128:256] = jnp.zeros((TQ, 128), bf16)
            db = db + jnp.sum(dz, axis=0, keepdims=True)
        db_ref[...] = db

    whole = lambda a: pl.BlockSpec(a.shape, lambda i: (0,) * a.ndim)
    return _call(
        order, body, (dF, fa, bpad, dproj), name="forget_bwd", grid=(1,),
        in_specs=[whole(dF), whole(fa), whole(bpad), pl.BlockSpec(memory_space=pl.ANY)],
        out_specs=[pl.BlockSpec((1, 128), lambda i: (0, 0)), pl.BlockSpec((S, 256), lambda i: (0, F_FA // 256))],
        out_shape=[jax.ShapeDtypeStruct((1, 128), f32), jax.ShapeDtypeStruct((S, NP), bf16)],
        input_output_aliases={3: 1},
        compiler_params=_params(("arbitrary",)),
    )


def _dil_bwd(order, qkvb, dob, ob, lseb, rope, dproj):
    c_t, s1_t, s2_t = rope

    def body(*refs):
        q_refs, k_refs, v_refs = refs[0:3], refs[3:6], refs[6:9]
        dob_ref, ob_ref, lse_ref, c_ref, s1_ref, s2_ref, _dproj_in, dqkv_ref = refs[9:17]
        qp, kp, vp, dop, lp, dlp, dln, dqp, dkp, dvp, nat = refs[17:28]
        dq_out, dk_out, dv_out = _dil_views(dqkv_ref)
        _, hm = _head_masks()

        def delta_rows(i, c):
            r0 = pl.multiple_of(i * TQ, TQ)
            prod = dob_ref[pl.ds(r0, TQ), :] * ob_ref[pl.ds(r0, TQ), :].astype(f32)
            d0 = jnp.sum(jnp.where(hm[0], prod, 0.0), axis=1, keepdims=True)
            d1 = jnp.sum(jnp.where(hm[1], prod, 0.0), axis=1, keepdims=True)
            dln[pl.ds(r0, TQ), :] = jnp.where(hm[0], d0, d1)
            return c

        lax.fori_loop(0, S // TQ, delta_rows, 0)

        for g, r in enumerate(DIL):
            nbl = S // r // BAND
            if r == 1:
                srcs = (q_refs[g], k_refs[g], v_refs[g], dob_ref, lse_ref, dln)
            else:
                for dst, src in ((qp, q_refs[g]), (kp, k_refs[g]), (vp, v_refs[g]), (dop, dob_ref),
                                 (lp, lse_ref), (dlp, dln)):
                    _permute_in(dst, src, r)
                srcs = (qp, kp, vp, dop, lp, dlp)
            dkp[...] = jnp.zeros_like(dkp)
            dvp[...] = jnp.zeros_like(dvp)

            def blk(t, c, srcs=srcs, nbl=nbl):
                qs_, ks_, vs_, dos_, ls_, dls_ = srcs
                work = []
                for u in range(DIL_UNROLL):
                    r0, k0, valid = _band_geometry(DIL_UNROLL * t + u, nbl)
                    q = qs_[pl.ds(r0, BAND), :] * 0.125
                    kwf = ks_[pl.ds(k0, _band_width(nbl)), :]
                    kw = kwf.astype(bf16)
                    vw = vs_[pl.ds(k0, _band_width(nbl)), :].astype(bf16)
                    do = dos_[pl.ds(r0, BAND), :]
                    lse = ls_[pl.ds(r0, BAND), :]
                    dlt = dls_[pl.ds(r0, BAND), :]
                    for hh in (0, 1):
                        qh = jnp.where(hm[hh], q, 0.0).astype(bf16)
                        doh = jnp.where(hm[hh], do, 0.0).astype(bf16)
                        kh = jnp.where(hm[hh], kwf, 0.0).astype(bf16)
                        work.append((u, hh, r0, k0, valid, qh, doh, kh, lse[:, hh * HD:hh * HD + 1],
                                     dlt[:, hh * HD:hh * HD + 1], _dot_nt(qh, kw), _dot_nt(doh, vw)))
                for u, hh, r0, k0, valid, qh, doh, kh, lse_h, dlt_h, s, dp in work:
                    if hh == 0:
                        dq = jnp.zeros((BAND, 128), f32)
                        dk = jnp.zeros((_band_width(nbl), 128), f32)
                        dv = jnp.zeros((_band_width(nbl), 128), f32)
                    pr = jnp.where(valid, jnp.exp(s - lse_h), 0.0)
                    dsb = (pr * (dp - dlt_h)).astype(bf16)
                    dv = dv + _dot_tn(pr.astype(bf16), doh)
                    dk = dk + _dot_tn(dsb, qh)
                    dq = dq + _dot(dsb, kh)
                    if hh == 1:
                        dqp[pl.ds(r0, BAND), :] = dq * 0.125
                        dkp[pl.ds(k0, _band_width(nbl)), :] += dk
                        dvp[pl.ds(k0, _band_width(nbl)), :] += dv
                return c

            lax.fori_loop(0, S // BAND // DIL_UNROLL, blk, 0)

            for acc, out, roped in ((dqp, dq_out[g], True), (dkp, dk_out[g], True), (dvp, dv_out[g], False)):
                if r == 1:
                    src = acc
                else:
                    _permute_out(nat, acc, r)
                    src = nat

                def emit(i, c, src=src, out=out, roped=roped):
                    r0 = pl.multiple_of(i * TQ, TQ)
                    d = src[pl.ds(r0, TQ), :]
                    if roped:
                        d = (d * c_ref[pl.ds(r0, TQ), :] + pltpu.roll(d * s1_ref[pl.ds(r0, TQ), :], 8, 1)
                             + pltpu.roll(d * s2_ref[pl.ds(r0, TQ), :], 120, 1))
                    out[pl.ds(r0, TQ), :] = d.astype(bf16)
                    return c

                lax.fori_loop(0, S // TQ, emit, 0)

    pair = pl.BlockSpec((S, 128), lambda p: (0, p))
    tab = pl.BlockSpec((S, 128), lambda p: (0, 0))
    blk_spec = pl.BlockSpec((S, DIL_BLK), lambda p: (0, p))
    return _call(
        order, body, [qkvb] * 9 + [dob, ob, lseb, c_t, s1_t, s2_t, dproj], name="dil_bwd", grid=(2,),
        in_specs=_dil_in_specs() + [pair, pair, pair, tab, tab, tab, pl.BlockSpec(memory_space=pl.ANY)],
        out_specs=blk_spec,
        out_shape=jax.ShapeDtypeStruct((S, NP), bf16),
        input_output_aliases={15: 0},
        scratch_shapes=[pltpu.VMEM((S, 128), f32)] * 11,
        compiler_params=_params(("parallel",)),
    )


def _inproj_bwd(order, dproj, wt, x, dx2, g1):
    tm = 256

    def body(d_ref, w_ref, x_ref, dx2_ref, g_ref, dx_ref, dg_ref):
        i = pl.program_id(0)
        dh = _dot(d_ref[...], w_ref[...])
        xb = x_ref[...]
        r = lax.rsqrt(jnp.mean(xb * xb, axis=-1, keepdims=True) + EPS)
        xh = xb * r
        gdh = dh * g_ref[...]
        dx_ref[...] = dx2_ref[...] + r * (gdh - xh * jnp.mean(gdh * xh, axis=-1, keepdims=True))

        @pl.when(i == 0)
        def _():
            dg_ref[...] = jnp.zeros_like(dg_ref)

        dg_ref[...] += jnp.sum(dh * xh, axis=0, keepdims=True)

    row = pl.BlockSpec((tm, D), lambda i: (i, 0))
    vec = pl.BlockSpec((1, D), lambda i: (0, 0))
    return _call(
        order, body, (dproj, wt, x, dx2, g1), name="inproj_bwd", grid=(S // tm,),
        in_specs=[pl.BlockSpec((tm, NP), lambda i: (i, 0)), pl.BlockSpec((NP, D), lambda i: (0, 0)), row, row, vec],
        out_specs=[row, vec],
        out_shape=[jax.ShapeDtypeStruct((S, D), f32), jax.ShapeDtypeStruct((1, D), f32)],
        compiler_params=_params(("arbitrary",)),
    )


HBM = pl.BlockSpec(memory_space=pltpu.HBM)
SEM = pl.BlockSpec(memory_space=pltpu.SEMAPHORE)
SMALL_ROWS = 8


def _comm_call(name, body, bufs, order, sems_in=(), new_sems=(), behind=()):
    nb, ns, nn = len(bufs), len(sems_in), len(new_sems)
    extra = order.token_for(bufs) + list(behind)

    def kern(*refs):
        off = nb + ns + len(extra)
        body(refs[:nb], refs[nb:nb + ns], refs[off:off + nn])
        refs[-1][...] = jnp.zeros((8, 128), f32)

    res = pl.pallas_call(
        kern, name=name,
        in_specs=[HBM] * nb + [SEM] * ns + [pl.BlockSpec(memory_space=pl.ANY)] * len(extra),
        out_specs=[SEM] * nn + [HBM] * nb + [pl.BlockSpec(memory_space=pltpu.VMEM)],
        out_shape=[pltpu.SemaphoreType.DMA((k,)) for k in new_sems] + [pltpu.HBM(b.shape, b.dtype) for b in bufs]
        + [jax.ShapeDtypeStruct((8, 128), f32)],
        input_output_aliases={i: nn + i for i in range(nb)},
        compiler_params=pltpu.CompilerParams(has_side_effects=pltpu.SideEffectType.DATAFLOW_SIDE_EFFECTING),
    )(*[pltpu.with_memory_space_constraint(b, pltpu.HBM) for b in bufs], *sems_in, *extra)
    order.mark(res[-1])
    return list(res[:nn]), list(res[nn:nn + nb])


def _place():
    x, y, c = lax.axis_index("x"), lax.axis_index("y"), lax.axis_index("c")
    chips = [(1 - x, y), (x, 1 - y), (1 - x, 1 - y)]
    return x, y, c, chips


def _rcopy(src, dst, ssem, rsem, dev):
    return pltpu.make_async_remote_copy(src_ref=src, dst_ref=dst, send_sem=ssem, recv_sem=rsem,
                                        device_id=dev, device_id_type=pl.DeviceIdType.MESH)


def _half(nrows, which):
    return pl.ds(which * (nrows // 2), nrows // 2)


def _ici_copies(stack, ssem, rsem, relay):
    x, y, c, chips = _place()
    me_q = 2 * x + y
    sends, recvs = {}, {}
    for a in range(len(stack)):
        rows = _half(stack[a].shape[1], c)
        for j, (cx, cy) in enumerate(chips):
            if relay and a == 0 and j == 2:
                continue
            mine = stack[a].at[me_q, rows]
            sends[a, j] = _rcopy(mine, mine, ssem.at[a * 3 + j], rsem.at[a * 3 + j], (cx, cy, c))
            theirs = stack[a].at[2 * cx + cy, rows]
            recvs[a, j] = _rcopy(theirs, theirs, ssem.at[a * 3 + j], rsem.at[a * 3 + j], (cx, cy, c))
    return sends, recvs


def _relay_copies(win, ssem, rsem):
    x, y, c, chips = _place()
    quarter = win.shape[1] // 4
    sends, recvs = [], []
    for k in range(2):
        rows = pl.ds(c * 2 * quarter + k * quarter, quarter)
        (fx, fy), (tx, ty) = chips[k], chips[1 - k]
        landed = win.at[2 * fx + fy, rows]
        sends.append(_rcopy(landed, landed, ssem.at[k], rsem.at[k], (tx, ty, c)))
        far = win.at[2 * chips[2][0] + chips[2][1], rows]
        recvs.append(_rcopy(far, far, ssem.at[k], rsem.at[k], (tx, ty, c)))
    return sends, recvs


def _allgather_start(name, stacks, order, relay=False):
    n = len(stacks)

    def body(bufs, _, new):
        sends, _r = _ici_copies(bufs, new[0], new[1], relay)
        for cp in sends.values():
            cp.start()

    return _comm_call(name, body, stacks, order, new_sems=(3 * n, 3 * n))


def _forward_copies(stack, ssem, rsem, relay=False):
    x, y, c, chips = _place()
    sib = (x, y, 1 - c)
    sends, recvs = {}, {}
    for a in range(len(stack)):
        for j, (cx, cy) in enumerate(chips):
            if relay and a == 0 and j == 2:
                continue
            landed = stack[a].at[2 * cx + cy, _half(stack[a].shape[1], c)]
            sends[a, j] = _rcopy(landed, landed, ssem.at[a * 3 + j], rsem.at[a * 3 + j], sib)
            other = stack[a].at[2 * cx + cy, _half(stack[a].shape[1], 1 - c)]
            recvs[a, j] = _rcopy(other, other, ssem.at[a * 3 + j], rsem.at[a * 3 + j], sib)
    return sends, recvs


def _far_forward(win, ssem, rsem):
    x, y, c, chips = _place()
    sib, far_q = (x, y, 1 - c), 2 * chips[2][0] + chips[2][1]
    landed, other = win.at[far_q, _half(win.shape[1], c)], win.at[far_q, _half(win.shape[1], 1 - c)]
    return _rcopy(landed, landed, ssem.at[0], rsem.at[0], sib), _rcopy(other, other, ssem.at[0], rsem.at[0], sib)


def _allgather_forward(name, stacks, sems, order, behind=(), relay=False):
    n = len(stacks)

    def body(bufs, taken, new):
        sends, recvs = _ici_copies(bufs, taken[0], taken[1], relay)
        fwd, _r = _forward_copies(bufs, new[0], new[1], relay)
        relay_sends = _relay_copies(bufs[0], new[2], new[3])[0] if relay else []
        for (a, j), arrived in recvs.items():
            arrived.wait_recv()
            fwd[a, j].start()
            if relay and a == 0:
                relay_sends[j].start()
        for cp in sends.values():
            cp.wait_send()

    return _comm_call(name, body, stacks, order, sems_in=sems, behind=behind,
                      new_sems=(3 * n, 3 * n) + ((2, 2) if relay else ()))


def _allgather_finish(name, stacks, sems, order, relay=False):
    def body(bufs, taken, new):
        sends, recvs = _forward_copies(bufs, taken[0], taken[1], relay)
        if relay:
            relay_sends, relay_recvs = _relay_copies(bufs[0], taken[2], taken[3])
            for cp in relay_recvs:
                cp.wait_recv()
            _far_forward(bufs[0], new[0], new[1])[0].start()
            for cp in relay_sends:
                cp.wait_send()
        for cp in sends.values():
            cp.wait_send()
        for cp in recvs.values():
            cp.wait_recv()

    if relay:
        return _comm_call(name, body, stacks, order, sems_in=sems, new_sems=(1, 1))
    return _comm_call(name, body, stacks, order, sems_in=sems)[1]


def _allgather_finish_far(name, stacks, sems, order):
    def body(bufs, taken, _):
        send, recv = _far_forward(bufs[0], taken[0], taken[1])
        send.wait_send()
        recv.wait_recv()

    return _comm_call(name, body, stacks, order, sems_in=sems)[1]


def _window_unit(q, j):
    return C2I[WIN_UNIT0[q] + j]


def _pair_copies(g, t, ssem, rsem, gathered):
    x, y, c, _ = _place()
    sib = (x, y, 1 - c)
    cps, whole = [], []
    for a in range(len(g)):
        if a == 0 and gathered:
            for q in range(NCHIP):
                for j in range(WIN_UNITS // 2):
                    u = jnp.where(c == 0, _window_unit(q, WIN_UNITS // 2 + j), _window_unit(q, j))
                    src = g[0].at[pl.ds(pl.multiple_of(u * UNIT, UNIT), UNIT), :]
                    cps.append(_rcopy(src, t[0].at[q, pl.ds(j * UNIT, UNIT), :], ssem.at[0], rsem.at[0], sib))
            whole.append(_rcopy(t[0], t[0], ssem.at[0], rsem.at[0], sib))
        else:
            cp = _rcopy(g[a].at[:, _half(g[a].shape[1], 1 - c), :], t[a], ssem.at[a], rsem.at[a], sib)
            cps.append(cp)
            whole.append(cp)
    return cps, whole


def _comm_multi(name, parts, order):
    def body(buf_refs, taken, new):
        ib = it = inew = 0
        for pbody, pbufs, psems, pnew, _ in parts:
            pbody(buf_refs[ib:ib + len(pbufs)], taken[it:it + len(psems)], new[inew:inew + len(pnew)])
            ib, it, inew = ib + len(pbufs), it + len(psems), inew + len(pnew)

    sems, bufs = _comm_call(name, body, [b for p in parts for b in p[1]], order,
                            sems_in=[s for p in parts for s in p[2]], new_sems=[k for p in parts for k in p[3]])
    out, ib, inew = [], 0, 0
    for _, pbufs, _, pnew, unpack in parts:
        out.append(unpack(sems[inew:inew + len(pnew)], bufs[ib:ib + len(pbufs)]))
        ib, inew = ib + len(pbufs), inew + len(pnew)
    return out


def _pair_start_part(gs, gathered=False):
    n = len(gs)
    ts = [lax.empty((NCHIP, WIN_ROWS // 2, D) if (a == 0 and gathered) else (NCHIP, g.shape[1] // 2, g.shape[2]), f32)
          for a, g in enumerate(gs)]

    def body(bufs, _, new):
        for cp in _pair_copies(bufs[:n], bufs[n:], new[0], new[1], gathered)[0]:
            cp.start()

    return body, list(gs) + ts, (), (n, n), lambda sems, bufs: (sems, bufs)


def _pair_wait_part(bufs, sems, gathered=False):
    n = len(bufs) // 2

    def body(refs, taken, _):
        for cp in _pair_copies(refs[:n], refs[n:], taken[0], taken[1], gathered)[1]:
            cp.wait_send()
            cp.wait_recv()

    return body, list(bufs), list(sems), (), lambda _, out: (out[:n], out[n:])


ROW_TILE = 256
ADAM_RING = 3


def _pair_add(order, gs, ts, q_arr, c_arr, name):
    n = len(gs)
    hs = [g.shape[1] // 2 for g in gs]
    nblk = max(1, max(hs) // ROW_TILE)
    assert all(h % (16 * nblk) == 0 for h in hs)

    def body(q_ref, c_ref, *refs):
        for a in range(n):
            s = refs[a][...] + refs[n + a][...]
            refs[3 * n + a][...] = s.astype(bf16)

            @pl.when(pl.program_id(1) == q_ref[0])
            def _():
                refs[2 * n + a][...] = s

    def blk(a, half):
        return pl.BlockSpec((None, hs[a] // nblk, gs[a].shape[2]),
                            lambda i, q, q_ref, c_ref: (q, (c_ref[0] * nblk if half else 0) + i, 0))

    out = _call_indexed(
        order, body, (q_arr, c_arr), list(gs) + list(ts), (nblk, NCHIP),
        [blk(a, True) for a in range(n)] + [blk(a, False) for a in range(n)],
        [pl.BlockSpec((hs[a] // nblk, gs[a].shape[2]), lambda i, q, q_ref, c_ref: (i, 0)) for a in range(n)]
        + [blk(a, False) for a in range(n)],
        name=name,
        out_shape=[jax.ShapeDtypeStruct((hs[a], gs[a].shape[2]), f32) for a in range(n)]
        + [jax.ShapeDtypeStruct((NCHIP, hs[a], gs[a].shape[2]), bf16) for a in range(n)],
        compiler_params=_params(("parallel", "arbitrary")),
    )
    return out[:n], out[n:]


def _pair_add_gathered(order, dwt, t, q_arr, c_arr, name):
    half_units, half_rows = WIN_UNITS // 2, WIN_ROWS // 2
    table = jnp.asarray([_window_unit(q, j) for q in range(NCHIP) for j in range(WIN_UNITS)], jnp.int32)

    def body(tab_ref, q_ref, c_ref, g_hbm, t_ref, own_ref, p16_ref, buf, sem):
        q = pl.program_id(0)

        def gather(w, slot):
            cps = []
            for j in range(half_units):
                u = tab_ref[w * WIN_UNITS + c_ref[0] * half_units + j]
                cps.append(pltpu.make_async_copy(g_hbm.at[pl.ds(pl.multiple_of(u * UNIT, UNIT), UNIT), :],
                                                 buf.at[slot, pl.ds(j * UNIT, UNIT), :], sem.at[slot]))
            return cps

        @pl.when(q == 0)
        def _():
            for cp in gather(0, 0):
                cp.start()

        @pl.when(q + 1 < NCHIP)
        def _():
            for cp in gather(q + 1, (q + 1) % 2):
                cp.start()

        slot = q % 2
        pltpu.make_async_copy(buf.at[slot], buf.at[slot], sem.at[slot]).wait()
        s = buf[slot] + t_ref[...]
        p16_ref[...] = s.astype(bf16)

        @pl.when(q == q_ref[0])
        def _():
            own_ref[...] = s

    blk = pl.BlockSpec((None, half_rows, D), lambda q, tab_ref, q_ref, c_ref: (q, 0, 0))
    return _call_indexed(
        order, body, (table, q_arr, c_arr), (dwt, t), (NCHIP,),
        [pl.BlockSpec(memory_space=pl.ANY), blk],
        [pl.BlockSpec((half_rows, D), lambda q, tab_ref, q_ref, c_ref: (0, 0)), blk],
        scratch_shapes=[pltpu.VMEM((2, half_rows, D), f32), pltpu.SemaphoreType.DMA((2,))],
        name=name,
        out_shape=[jax.ShapeDtypeStruct((half_rows, D), f32),
                   jax.ShapeDtypeStruct((NCHIP, half_rows, D), bf16)],
        compiler_params=_params(("arbitrary",)),
    )


def _shard_copies(p, r, sm, ssem, rsem):
    x, y, c, chips = _place()
    n = len(p)
    sends, recvs = [], []
    for a in range(n):
        for j, (cx, cy) in enumerate(chips):
            k = a * 3 + j
            sends.append(_rcopy(p[a].at[2 * cx + cy], r[a].at[j], ssem.at[k], rsem.at[k], (cx, cy, c)))
            recvs.append(_rcopy(r[a].at[j], r[a].at[j], ssem.at[k], rsem.at[k], (cx, cy, c)))
    if sm is not None:
        mine = sm.at[4 * x + 2 * y + c]
        for i in range(1, 8):
            px = (1 - x) if i & 4 else x
            py = (1 - y) if i & 2 else y
            pc = (1 - c) if i & 1 else c
            k = 3 * n + i - 1
            sends.append(_rcopy(mine, mine, ssem.at[k], rsem.at[k], (px, py, pc)))
            slot = sm.at[4 * px + 2 * py + pc]
            recvs.append(_rcopy(slot, slot, ssem.at[k], rsem.at[k], (px, py, pc)))
    return sends, recvs


def _shard_start_part(p16s, sm=None):
    n = len(p16s)
    rs = [lax.empty((3,) + p.shape[1:], bf16) for p in p16s]
    extra = [] if sm is None else [sm]
    nsem = 3 * n + (7 if sm is not None else 0)

    def body(bufs, _, new):
        sends, _r = _shard_copies(bufs[:n], bufs[n:2 * n], bufs[2 * n] if extra else None, new[0], new[1])
        for cp in sends:
            cp.start()

    return body, list(p16s) + rs + extra, (), (nsem, nsem), lambda sems, bufs: (sems, bufs)


def _shard_wait_part(bufs, sems, n):
    has_sm = len(bufs) > 2 * n

    def body(refs, taken, _):
        sends, recvs = _shard_copies(refs[:n], refs[n:2 * n], refs[2 * n] if has_sm else None, taken[0], taken[1])
        for cp in sends:
            cp.wait_send()
        for cp in recvs:
            cp.wait_recv()

    return body, list(bufs), list(sems), (), lambda _, out: (out[n:2 * n], (out[2 * n] if has_sm else None))


def _shard_sum(order, owns, rs, c_arr, name):
    n = len(owns)
    hs = [o.shape[0] for o in owns]
    nblk = max(1, max(hs) // ROW_TILE)
    assert all(h % (16 * nblk) == 0 for h in hs)
    trs = [h // nblk for h in hs]

    def body(c_ref, *refs):
        for a in range(n):
            s = refs[a][...]
            for j in range(3):
                s = s + refs[n + a][j].astype(f32)
            refs[2 * n + a][...] = s

    out = _call_indexed(
        order, body, (c_arr,), list(owns) + list(rs), (nblk,),
        [pl.BlockSpec((trs[a], owns[a].shape[1]), lambda i, c_ref: (i, 0)) for a in range(n)]
        + [pl.BlockSpec((3, trs[a], owns[a].shape[1]), lambda i, c_ref: (0, i, 0)) for a in range(n)],
        [pl.BlockSpec((trs[a], owns[a].shape[1]), lambda i, c_ref: (c_ref[0] * nblk + i, 0)) for a in range(n)],
        name=name, out_shape=[jax.ShapeDtypeStruct((2 * hs[a], owns[a].shape[1]), f32) for a in range(n)],
        compiler_params=_params(("parallel",)),
    )
    return list(out)


def _swap_copies(full, ssem, rsem):
    x, y, c, _ = _place()
    sends, recvs = [], []
    for a in range(len(full)):
        mine = full[a].at[_half(full[a].shape[0], c)]
        sends.append(_rcopy(mine, mine, ssem.at[a], rsem.at[a], (x, y, 1 - c)))
        other = full[a].at[_half(full[a].shape[0], 1 - c)]
        recvs.append(_rcopy(other, other, ssem.at[a], rsem.at[a], (x, y, 1 - c)))
    return sends, recvs


def _swap_start_part(fulls):
    n = len(fulls)

    def body(bufs, _, new):
        for cp in _swap_copies(bufs, new[0], new[1])[0]:
            cp.start()

    return body, list(fulls), (), (n, n), lambda sems, bufs: (sems, bufs)


def _swap_wait_part(fulls, sems):
    def body(refs, taken, _):
        sends, recvs = _swap_copies(refs, taken[0], taken[1])
        for cp in sends:
            cp.wait_send()
        for cp in recvs:
            cp.wait_recv()

    return body, list(fulls), list(sems), (), lambda _, out: out


def _small_finish(order, sm, ws, ms, vs):
    n = len(ws)

    def body(sm_ref, *refs):
        s = sm_ref[0]
        for d in range(1, 8):
            s = s + sm_ref[d]
        loss_ref, g_refs, upd_refs = refs[3 * n], refs[3 * n + 1:4 * n + 1], refs[4 * n + 1:]
        loss_ref[...] = s[n:n + 1, 0:1]
        for i in range(n):
            g = s[i:i + 1, 0:ws[i].shape[1]]
            g_refs[i][...] = g
            res = _adamw_math(refs[i][...], g, refs[n + i][...], refs[2 * n + i][...])
            for k in range(3):
                upd_refs[3 * i + k][...] = res[k]

    out = _call(order, body, [sm] + list(ws) + list(ms) + list(vs), name="small_sum_adamw",
                out_shape=[jax.ShapeDtypeStruct((1, 1), f32)] + [jax.ShapeDtypeStruct(w.shape, f32) for w in ws]
                + [jax.ShapeDtypeStruct(w.shape, f32) for w in ws for _ in range(3)])
    return out[0], out[1:n + 1], [out[n + 1 + 3 * i:n + 4 + 3 * i] for i in range(n)]


def _adamw_math(w, g, m, v):
    m = ADAM_B1 * m + (1.0 - ADAM_B1) * g
    v = ADAM_B2 * v + (1.0 - ADAM_B2) * (g * g)
    m_hat = m / (1.0 - ADAM_B1 ** ADAM_STEP)
    v_hat = v / (1.0 - ADAM_B2 ** ADAM_STEP)
    return -ADAM_LR * (m_hat / (jnp.sqrt(v_hat) + ADAM_EPS) + ADAM_WD * w), m, v


def _adamw(order, ws, gs, ms, vs, name):
    n = len(ws)
    nblk = max(1, max(w.shape[0] for w in ws) // ROW_TILE)
    assert all(w.shape[0] % (8 * nblk) == 0 for w in ws)

    trs = [w.shape[0] // nblk for w in ws]

    def body(*refs):
        ins, outs, bufs, sem = refs[:4 * n], refs[4 * n:8 * n], refs[8 * n:12 * n], refs[12 * n]
        i = pl.program_id(0)

        def fetch(s):
            slot = s % ADAM_RING
            row = lambda k: s * trs[k % n] if isinstance(s, int) else pl.multiple_of(s * trs[k % n], 8)
            return [pltpu.make_async_copy(ins[k].at[pl.ds(row(k), trs[k % n])], bufs[k].at[slot],
                                          sem.at[k, slot]) for k in range(4 * n)]

        @pl.when(i == 0)
        def _():
            for s in range(min(ADAM_RING - 1, nblk)):
                for cp in fetch(s):
                    cp.start()

        @pl.when(i + ADAM_RING - 1 < nblk)
        def _():
            for cp in fetch(i + ADAM_RING - 1):
                cp.start()

        for cp in fetch(i):
            cp.wait()
        slot = i % ADAM_RING
        for a in range(n):
            w, g, m, v = (bufs[k * n + a][slot] for k in range(4))
            d_ref, nm_ref, nv_ref, g_out = outs[4 * a:4 * a + 4]
            g_out[...] = g
            d_ref[...], nm_ref[...], nv_ref[...] = _adamw_math(w, g, m, v)

    blks = [pl.BlockSpec((trs[a], ws[a].shape[1]), lambda i: (i, 0)) for a in range(n)]
    out = _call(
        order, body, list(ws) + list(gs) + list(ms) + list(vs), name=name, grid=(nblk,),
        in_specs=[pl.BlockSpec(memory_space=pl.ANY)] * (4 * n), out_specs=[b for b in blks for _ in range(4)],
        out_shape=[jax.ShapeDtypeStruct(w.shape, f32) for w in ws for _ in range(4)],
        scratch_shapes=[pltpu.VMEM((ADAM_RING, trs[k % n], ws[k % n].shape[1]), f32) for k in range(4 * n)]
        + [pltpu.SemaphoreType.DMA((4 * n, ADAM_RING))],
        compiler_params=_params(("arbitrary",)),
    )
    return [out[4 * a:4 * a + 4] for a in range(n)]


def _feature_rows(w):
    return jnp.transpose(w, (2, 0, 1))


WIN_STEP = 128
WIN_PIECE = 2 * WIN_STEP


def _window_stacks(order, w, q_arr):
    steps = WIN_ROWS // WIN_STEP
    n_piece = (WIN_ROWS - 2 * WIN_STEP) // WIN_PIECE
    assert n_piece * WIN_PIECE == WIN_ROWS - 2 * WIN_STEP and WIN_STEP % 16 == 0
    assert max(OWN_ROW0) < UNIT <= WIN_STEP and OWN_ROW0[1] + FA_AT == UNIT and FA_AT + N_FA + UNIT <= SHARD_IN
    pad = -(-(WIN_ROWS - SHARD_IN + N_FA) // 8) * 8
    lead = pad - (WIN_ROWS - SHARD_IN)
    assert lead + OWN_ROW0[1] - N_FA >= 0 and lead + max(OWN_ROW0) <= pad and max(OWN_ROW0) <= WIN_ROWS - SHARD_IN

    def body(q_ref, w_ref, win_ref, fa_ref, first, last, til, fabuf, sem):
        i = pl.program_id(0)
        q = q_ref[0]
        chip1 = q == 1
        row0 = jnp.where(q == 0, OWN_ROW0[0], jnp.where(chip1, OWN_ROW0[1], jnp.where(q == 2, OWN_ROW0[2], OWN_ROW0[3])))
        skip = jnp.where(chip1, N_FA, 0)

        def rows(dst, src0, dst0, n, slot):
            return pltpu.make_async_copy(w_ref.at[pl.ds(src0, n)], dst.at[pl.ds(dst0, n)], sem.at[slot])

        def first_copies(on_chip1):
            if on_chip1:
                return [rows(first, 0, OWN_ROW0[1], FA_AT, 0), rows(first, FA_AT + N_FA, UNIT, UNIT, 1)]
            return [rows(first, 0, row0, WIN_STEP, 0)]

        def first_do(act):
            for on_chip1 in (False, True):
                @pl.when(chip1 if on_chip1 else jnp.logical_not(chip1))
                def _():
                    for c in first_copies(on_chip1):
                        act(c)

        def piece(j):
            dst0 = WIN_STEP + j * WIN_PIECE
            return pltpu.make_async_copy(w_ref.at[pl.ds(dst0 - row0 + skip, WIN_PIECE), 0],
                                         til.at[pl.ds(dst0, WIN_PIECE)], sem.at[2 + j])

        last_copy = rows(last, SHARD_IN - WIN_STEP, lead + row0 - skip, WIN_STEP, 2 + n_piece)
        fa_copy = rows(fabuf, FA_AT, 0, N_FA, 3 + n_piece)

        @pl.when(i == 0)
        def _():
            first[pl.ds(0, UNIT)] = jnp.zeros((UNIT, 1, D), f32)
            last[...] = jnp.zeros(last.shape, f32)
            fabuf[pl.ds(N_FA, FA_ROWS - N_FA)] = jnp.zeros((FA_ROWS - N_FA, 1, D), f32)
            fa_copy.start()
            first_do(lambda c: c.start())
            for j in range(n_piece):
                piece(j).start()
            last_copy.start()
            fa_copy.wait()
            fa_ref[...] = fabuf[...].reshape(FA_ROWS, D).astype(bf16)
            first_do(lambda c: c.wait())
            win_ref[...] = first[pl.ds(0, WIN_STEP)].reshape(WIN_STEP, D).astype(bf16)

        for j in range(n_piece):
            @pl.when(i == 1 + j * (WIN_PIECE // WIN_STEP))
            def _():
                piece(j).wait()

        @pl.when(jnp.logical_and(i > 0, i < steps - 1))
        def _():
            win_ref[...] = til[pl.ds(pl.multiple_of(i * WIN_STEP, WIN_STEP), WIN_STEP)].astype(bf16)

        @pl.when(i == steps - 1)
        def _():
            last_copy.wait()
            win_ref[...] = last[pl.ds(pad, WIN_STEP)].reshape(WIN_STEP, D).astype(bf16)

    return _call_indexed(
        order, body, (q_arr,), (w,), (steps,), [pl.BlockSpec(memory_space=pl.ANY)],
        [pl.BlockSpec((None, WIN_STEP, D), lambda i, q: (q[0], i, 0)),
         pl.BlockSpec((None, FA_ROWS, D), lambda i, q: (q[0], 0, 0))],
        scratch_shapes=[pltpu.VMEM((WIN_STEP + UNIT, 1, D), f32), pltpu.VMEM((pad + WIN_STEP, 1, D), f32),
                        pltpu.VMEM((WIN_ROWS, D), f32), pltpu.VMEM((FA_ROWS, 1, D), f32),
                        pltpu.SemaphoreType.DMA((4 + n_piece,))],
        name="window_w_in", out_shape=[jax.ShapeDtypeStruct((NCHIP, WIN_ROWS, D), bf16),
                                       jax.ShapeDtypeStruct((NCHIP, FA_ROWS, D), bf16)],
        compiler_params=_params(("arbitrary",)),
    )


def _unfeature_rows(a):
    return jnp.transpose(a, (1, 2, 0))


ADAM_IN_ROWS = 134
ADAM_IN_STEPS = SHARD_IN // ADAM_IN_ROWS
ADAM_IN_CHUNK = 136
ADAM_IN_CHUNKS = ADAM_IN_STEPS + 1
ADAM_IN_BUF = WIN_ROWS + N_FA


def _adamw_w_in(order, w, gwin, gfa, m, v, q_arr):
    assert ADAM_IN_CHUNK * ADAM_IN_STEPS < WIN_ROWS <= ADAM_IN_CHUNK * ADAM_IN_CHUNKS
    assert OWN_ROW0[NCHIP - 1] + ADAM_IN_ROWS <= 2 * ADAM_IN_CHUNK and ADAM_IN_CHUNK >= ADAM_IN_ROWS
    last0 = ADAM_IN_CHUNK * ADAM_IN_STEPS
    cut = OWN_ROW0[1] + FA_AT

    def body(q_ref, w_ref, gwin_ref, gfa_ref, m_ref, v_ref, go_ref, d_ref, nm_ref, nv_ref, buf, sem):
        i = pl.program_id(0)
        q = q_ref[0]
        chip1 = q == 1
        shift = jnp.where(chip1, N_FA, 0)

        def copy(src_ref, src0, dst0, n, slot):
            return pltpu.make_async_copy(src_ref.at[pl.ds(src0, n)], buf.at[pl.ds(dst0, n), 0], sem.at[slot])

        def first(on_chip1):
            if on_chip1:
                return [copy(gwin_ref, 0, 0, cut, 0), copy(gfa_ref, 0, cut, N_FA, ADAM_IN_CHUNKS),
                        copy(gwin_ref, cut, cut + N_FA, ADAM_IN_CHUNK - cut - N_FA, ADAM_IN_CHUNKS + 1)]
            return [copy(gwin_ref, 0, 0, ADAM_IN_CHUNK, 0)]

        def middle(k):
            return [copy(gwin_ref, pl.multiple_of(k * ADAM_IN_CHUNK - shift, 8), k * ADAM_IN_CHUNK, ADAM_IN_CHUNK, k)]

        def last(on_chip1):
            n = WIN_ROWS - last0 + (N_FA if on_chip1 else 0)
            return [copy(gwin_ref, WIN_ROWS - n, last0, n, ADAM_IN_STEPS)]

        def both(make, act):
            for on_chip1 in (False, True):
                @pl.when(chip1 if on_chip1 else jnp.logical_not(chip1))
                def _():
                    for c in make(on_chip1):
                        act(c)

        @pl.when(i == 0)
        def _():
            both(first, lambda c: c.start())
            for k in range(1, ADAM_IN_STEPS):
                middle(k)[0].start()
            both(last, lambda c: c.start())
            both(first, lambda c: c.wait())

        @pl.when(i < ADAM_IN_STEPS - 1)
        def _():
            middle(i + 1)[0].wait()

        @pl.when(i == ADAM_IN_STEPS - 1)
        def _():
            both(last, lambda c: c.wait())

        row0 = jnp.where(q == 0, OWN_ROW0[0], jnp.where(chip1, OWN_ROW0[1], jnp.where(q == 2, OWN_ROW0[2], OWN_ROW0[3])))
        g = buf[pl.ds(row0 + i * ADAM_IN_ROWS, ADAM_IN_ROWS)]
        go_ref[...] = g
        d_ref[...], nm_ref[...], nv_ref[...] = _adamw_math(w_ref[...], g, m_ref[...], v_ref[...])

    blk = pl.BlockSpec((ADAM_IN_ROWS, 1, D), lambda i, q: (i, 0, 0))
    hbm = pl.BlockSpec(memory_space=pl.ANY)
    return _call_indexed(
        order, body, (q_arr,), (w, gwin, gfa, m, v), (ADAM_IN_STEPS,), [blk, hbm, hbm, blk, blk], [blk] * 4,
        scratch_shapes=[pltpu.VMEM((ADAM_IN_BUF, 1, D), f32), pltpu.SemaphoreType.DMA((ADAM_IN_CHUNKS + 2,))],
        name="adamw_w_in", out_shape=[jax.ShapeDtypeStruct((SHARD_IN, 1, D), f32)] * 4,
        compiler_params=_params(("arbitrary",)),
    )


def kernel(x, norm_attn_g, w_in, b_forget, w_branch_a, w_branch_b, w_out, norm_mlp_g, w_up, w_down, norm_final_g, loss_target, m_norm_attn_g, m_w_in, m_b_forget, m_w_branch_a, m_w_branch_b, m_w_out, m_norm_mlp_g, m_w_up, m_w_down, m_norm_final_g, v_norm_attn_g, v_w_in, v_b_forget, v_w_branch_a, v_w_branch_b, v_w_out, v_norm_mlp_g, v_w_up, v_w_down, v_norm_final_g):
    xi, yi, ci = lax.axis_index("x"), lax.axis_index("y"), lax.axis_index("c")
    q_me = 2 * xi + yi
    c_arr = jnp.reshape(ci, (1,)).astype(jnp.int32)
    q_arr = jnp.reshape(q_me, (1,)).astype(jnp.int32)
    x_, tgt = x[0], loss_target[0]

    names = ["w_branch_a", "w_branch_b", "w_out", "w_up", "w_down"]
    big = dict(zip(names, [w_branch_a[0], w_branch_b[0], w_out[0], w_up[0], w_down[0]]))
    ms = dict(zip(names, [m_w_branch_a[0], m_w_branch_b[0], m_w_out[0], m_w_up[0], m_w_down[0]]))
    vs = dict(zip(names, [v_w_branch_a[0], v_w_branch_b[0], v_w_out[0], v_w_up[0], v_w_down[0]]))
    grad, upd = {}, {}
    order = _Order()

    def run(fn, *args, **kw):
        return fn(order, *args, **kw)

    def own_slot(a):
        return lax.dynamic_update_slice(lax.empty((NCHIP,) + a.shape, a.dtype), a[None], (q_me, 0, 0))

    sem_in, in_s = _allgather_start("allgather_start_in", run(_window_stacks, _feature_rows(w_in), q_arr), order,
                                    relay=True)
    rope = _rope_tables(order.tok[0, 0])
    sem_f, in_s = _allgather_forward("allgather_forward_in", in_s, sem_in, order, behind=rope, relay=True)
    sem_rest, rest = _allgather_start("allgather_start_rest", [own_slot(w.astype(bf16)) for w in big.values()], order)
    sem_f, in_s = _allgather_finish("allgather_finish_in", in_s, sem_f, order, relay=True)
    wins, fas = _allgather_finish_far("allgather_finish_far_in", in_s, sem_f, order)
    wt = run(_assemble_win, wins, fas)

    bpad = jnp.pad(b_forget, ((0, 0), (0, 120)))
    h1, qkvb, qkva, gates, fa = run(_norm_inproj, x_, norm_attn_g, wt, rope)
    F = run(_forget_cumsum, fa, bpad)
    oa, lsea = run(_fox_fwd, qkva, F)
    sem_f, rest = _allgather_forward("allgather_forward_rest", rest, sem_rest, order)
    ob, lseb = run(_dil_fwd, qkvb)
    was, wbs, wouts, wups, wdowns = _allgather_finish("allgather_finish_rest", rest, sem_f, order)
    wout = wouts.reshape(D, D)
    wdown = wdowns.reshape(DFF, D)
    ya, yb, mixed = run(_branch_mix, oa, ob, was, wbs, gates)
    x2, h2 = run(_outproj_norm, mixed, wout, x_, norm_mlp_g)
    u, a = run(_mlp_up, h2, wups)
    dx3, dx3b, dg3, loss_part = run(_mlp_down_loss, a, wdown, x2, norm_final_g.reshape(1, D), tgt)

    def comm(name, *parts):
        return _comm_multi(name, list(parts), order)

    def adamw_group(group, fulls, name):
        res = run(_adamw, [big[nm] for nm in group], fulls, [ms[nm] for nm in group], [vs[nm] for nm in group], name)
        for nm, r in zip(group, res):
            *upd[nm], grad[nm] = r

    grp_a, grp_b, grp_c = ["w_down", "w_up"], ["w_out", "w_branch_a", "w_branch_b"], ["w_in", "w_in_fa"]
    du = run(_mlp_down_bwd, dx3b, wdown, u)
    dwdown = run(_mm, a, dx3b, "tn", f32, 1024, D, "wgrad_down")
    dwup = run(_mm, h2, du, "tn", f32, D, 1024, "wgrad_up", stack_cols=True)
    ((sem_pa, buf_pa),) = comm("pair_start_a", _pair_start_part([dwdown.reshape(NCHIP, DFF // NCHIP, D), dwup]))
    dx2, dx2b, dg2 = run(_mlp_up_bwd, du, wups, x2, dx3, norm_mlp_g)
    ((gs, ts),) = comm("pair_wait_a", _pair_wait_part(buf_pa, sem_pa))
    p32_a, p16_a = run(_pair_add, gs, ts, q_arr, c_arr, "pair_add_a")
    ((sem_sa, buf_sa),) = comm("shard_start_a", _shard_start_part(p16_a))
    dya, dyb, dproj = run(_gate_bwd, dx2b, wout, gates, ya, yb)
    dwout = run(_mm, mixed, dx2b, "tn", f32, D, D, "wgrad_out")
    doa, dob = run(_branch_bwd, dya, dyb, was, wbs)
    dwas, dwbs = run(_branch_wgrad, oa, ob, dya, dyb)
    ((sem_pb, buf_pb),) = comm("pair_start_b", _pair_start_part([dwout.reshape(NCHIP, D // NCHIP, D), dwas, dwbs]))
    dF, dproj = run(_fox_bwd, qkva, doa, oa, lsea, F, dproj)
    (gs, ts), (rs_a, _) = comm("pair_wait_b_shard_wait_a", _pair_wait_part(buf_pb, sem_pb),
                               _shard_wait_part(buf_sa, sem_sa, len(grp_a)))
    p32_b, p16_b = run(_pair_add, gs, ts, q_arr, c_arr, "pair_add_b")
    fulls_a = run(_shard_sum, p32_a, rs_a, c_arr, "shard_sum_a")
    (sem_wa, fulls_a), (sem_sb, buf_sb) = comm("swap_start_a_shard_start_b", _swap_start_part(fulls_a),
                                               _shard_start_part(p16_b))
    dbf, dproj = run(_forget_bwd, dF, fa, bpad, dproj)
    dproj = run(_dil_bwd, qkvb, dob, ob, lseb, rope, dproj)
    (rs_b, _), fulls_a = comm("shard_wait_b_swap_wait_a", _shard_wait_part(buf_sb, sem_sb, len(grp_b)),
                              _swap_wait_part(fulls_a, sem_wa))
    fulls_b = run(_shard_sum, p32_b, rs_b, c_arr, "shard_sum_b")
    ((sem_wb, fulls_b),) = comm("swap_start_b", _swap_start_part(fulls_b))
    dwt = run(_mm, dproj, h1, "tn", f32, 512, D, "wgrad_in")
    dwfa = jnp.broadcast_to(dwt[F_FA:F_FA + FA_ROWS][None], (NCHIP, FA_ROWS, D))
    (sem_pc, buf_pc), fulls_b = comm("pair_start_c_swap_wait_b", _pair_start_part([dwt, dwfa], gathered=True),
                                     _swap_wait_part(fulls_b, sem_wb))
    adamw_group(grp_b, fulls_b, "adamw_b")
    (((dwt_c, dwfa_c), (t_in, t_fa)),) = comm("pair_wait_c", _pair_wait_part(buf_pc, sem_pc, gathered=True))
    p32_in, p16_in = run(_pair_add_gathered, dwt_c, t_in, q_arr, c_arr, "pair_add_w_in")
    p32_fa, p16_fa = run(_pair_add, [dwfa_c], [t_fa], q_arr, c_arr, "pair_add_w_in_fa")
    ((sem_sc, buf_sc),) = comm("shard_start_c", _shard_start_part([p16_in, *p16_fa]))
    gx, dg1 = run(_inproj_bwd, dproj, wt, x_, dx2, norm_attn_g)
    adamw_group(grp_a, fulls_a, "adamw_a")
    small = jnp.concatenate([dg1, dg2, dg3, jnp.pad(dbf[:, 0:8], ((0, 0), (0, D - 8))),
                             jnp.pad(loss_part, ((0, 0), (0, D - 128))),
                             jnp.zeros((SMALL_ROWS - 5, D), f32)], axis=0)
    sm = lax.dynamic_update_slice(lax.empty((8, SMALL_ROWS, D), f32), small[None],
                                  (4 * xi + 2 * yi + ci, 0, 0))
    (sem_sm, buf_sm), (rs_c, _) = comm("small_start_shard_wait_c", _shard_start_part([], sm),
                                       _shard_wait_part(buf_sc, sem_sc, len(grp_c)))
    fulls_c = (run(_shard_sum, [p32_in], rs_c[0:1], c_arr, "shard_sum_w_in")
               + run(_shard_sum, p32_fa, rs_c[1:2], c_arr, "shard_sum_w_in_fa"))
    (sem_wc, fulls_c), (_, sm) = comm("swap_start_c_small_wait", _swap_start_part(fulls_c),
                                      _shard_wait_part(buf_sm, sem_sm, 0))
    smalls = ["norm_attn_g", "norm_mlp_g", "norm_final_g", "b_forget"]
    loss, gs, res = run(_small_finish, sm, [norm_attn_g, norm_mlp_g, norm_final_g.reshape(1, D), b_forget],
                        [m_norm_attn_g, m_norm_mlp_g, m_norm_final_g.reshape(1, D), m_b_forget],
                        [v_norm_attn_g, v_norm_mlp_g, v_norm_final_g.reshape(1, D), v_b_forget])
    loss = loss.reshape(())
    grad.update(zip(smalls, gs))
    upd.update(zip(smalls, res))

    ((gwin, gfa),) = comm("swap_wait_c", _swap_wait_part(fulls_c, sem_wc))
    res_in = run(_adamw_w_in, _feature_rows(w_in), gwin, gfa, _feature_rows(m_w_in), _feature_rows(v_w_in), q_arr)
    grad["w_in"] = _unfeature_rows(res_in[0])
    upd["w_in"] = [_unfeature_rows(t) for t in res_in[1:]]

    order_out = ["norm_attn_g", "w_in", "b_forget", "w_branch_a", "w_branch_b", "w_out", "norm_mlp_g", "w_up",
                 "w_down", "norm_final_g"]
    shapes = dict(norm_attn_g=norm_attn_g.shape, w_in=w_in.shape, b_forget=b_forget.shape,
                  w_branch_a=w_branch_a.shape, w_branch_b=w_branch_b.shape, w_out=w_out.shape,
                  norm_mlp_g=norm_mlp_g.shape, w_up=w_up.shape, w_down=w_down.shape, norm_final_g=norm_final_g.shape)
    outs = [loss, gx.reshape(x.shape)]
    outs += [grad[nm].reshape(shapes[nm]) for nm in order_out]
    for k in range(3):
        outs += [upd[nm][k].reshape(shapes[nm]) for nm in order_out]
    return tuple(outs)
```

```python
import jax
import jax.numpy as jnp
from jax import lax
from jax.experimental import pallas as pl
from jax.experimental.pallas import tpu as pltpu

f32 = jnp.float32
bf16 = jnp.bfloat16

S = 2048
D = 1024
DFF = 4096
HD = 64
FOXW = 512
DILOUT = 256
DIL = (1, 4, 16)
BAND = 128
EPS = 1e-6
NEG = -1e30
ROPE_THETA = 500000.0
NCHIP = 4
TQ = 256

ADAM_LR, ADAM_B1, ADAM_B2, ADAM_EPS, ADAM_WD, ADAM_STEP = 0.001, 0.9, 0.999, 1e-08, 0.01, 10
VMEM_LIMIT = 56 * 1024 * 1024

UNIT = 64
NP = 6144
F_DIL, F_FOX, F_FA, F_G = 0, 2304, 3840, 4096
DIL_BLK, FOX_BLK = 1152, 384
WIN_UNITS, WIN_ROWS = 24, 1536
WIN_UNIT0 = (0, 23, 45, 68)
OWN_ROW0 = (0, 2, 60, 62)
SHARD_IN = 1474
N_FA = 8
FA_AT = 1536 - SHARD_IN
FA_ROWS = 32


def _compact_to_internal():
    c2i = {}
    for p in range(2):
        for role in range(3):
            for g in range(3):
                for hh in range(2):
                    c2i[24 + 12 * role + 4 * g + 2 * p + hh] = 18 * p + 6 * role + 2 * g + hh
    for p in range(4):
        for role in range(3):
            for hh in range(2):
                c2i[8 * role + 2 * p + hh] = F_FOX // UNIT + 6 * p + 2 * role + hh
    for j in range(32):
        c2i[60 + j] = F_G // UNIT + j
    return c2i


C2I = _compact_to_internal()
OVERLAP_UNITS = (23, 45, 46, 68)


def _params(sem=None):
    return pltpu.CompilerParams(dimension_semantics=sem, vmem_limit_bytes=VMEM_LIMIT)


class _Order:
    def __init__(self):
        self.tok = None

    def mark(self, v):
        self.tok = v

    def token_for(self, args):
        return [] if self.tok is None or any(self.tok is a for a in args) else [self.tok]


def _call(order, body, args, in_specs=None, **kw):
    args = list(args)
    n_in = len(args)
    if in_specs is None:
        in_specs = [pl.BlockSpec(memory_space=pltpu.VMEM)] * n_in
    kern = body
    extra = order.token_for(args)
    if extra:
        in_specs = list(in_specs) + [pl.BlockSpec(memory_space=pl.ANY)]

        def kern(*refs):
            body(*refs[:n_in], *refs[n_in + 1:])

    out = pl.pallas_call(kern, in_specs=in_specs, **kw)(*args, *extra)
    order.mark(out[0] if isinstance(out, (tuple, list)) else out)
    return out


def _call_indexed(order, body, scalars, args, grid, in_specs, out_specs, scratch_shapes=(), **kw):
    args, in_specs = list(args), list(in_specs)
    n_front = len(scalars) + len(args)
    kern = body
    extra = order.token_for(args)
    if extra:
        in_specs.append(pl.BlockSpec(memory_space=pl.ANY))

        def kern(*refs):
            body(*refs[:n_front], *refs[n_front + 1:])

    out = pl.pallas_call(
        kern, grid_spec=pltpu.PrefetchScalarGridSpec(num_scalar_prefetch=len(scalars), grid=grid, in_specs=in_specs,
                                                     out_specs=out_specs, scratch_shapes=scratch_shapes),
        **kw)(*scalars, *args, *extra)
    order.mark(out[0] if isinstance(out, (tuple, list)) else out)
    return out


def _dot(a, b):
    return jnp.dot(a, b, preferred_element_type=f32)


def _dot_nt(a, b):
    return lax.dot_general(a, b, (((1,), (1,)), ((), ())), preferred_element_type=f32)


def _dot_tn(a, b):
    return lax.dot_general(a, b, (((0,), (0,)), ((), ())), preferred_element_type=f32)


def _split3(x):
    hi = x.astype(bf16)
    r1 = x - hi.astype(f32)
    mid = r1.astype(bf16)
    lo = (r1 - mid.astype(f32)).astype(bf16)
    return hi, mid, lo


def _rope_tables(after):
    half = 8
    inv_freq = jnp.power(jnp.float32(ROPE_THETA), -jnp.arange(half, dtype=f32) * 2.0 / 16)
    ang = (jnp.arange(S).astype(f32) + after)[:, None] * inv_freq[None, :]
    cos, sin = jnp.cos(ang), jnp.sin(ang)
    one = jnp.ones((S, HD - 16), f32)
    zero = jnp.zeros((S, HD - 16), f32)
    z8 = jnp.zeros((S, 8), f32)
    c = jnp.concatenate([cos, cos, one], axis=1)
    s1 = jnp.concatenate([-sin, z8, zero], axis=1)
    s2 = jnp.concatenate([z8, sin, zero], axis=1)
    return tuple(jnp.concatenate([t, t], axis=1) for t in (c, s1, s2))


def _mm(order, a, b, mode, out_dtype, tm, tn, name, stack_cols=False):
    if mode == "nn":
        (M, K), (_, N) = a.shape, b.shape
        a_spec = pl.BlockSpec((tm, K), lambda i, j: (i, 0))
        b_spec = pl.BlockSpec((K, tn), lambda i, j: (0, j))
        dot = _dot
    elif mode == "nt":
        (M, K), (N, _) = a.shape, b.shape
        a_spec = pl.BlockSpec((tm, K), lambda i, j: (i, 0))
        b_spec = pl.BlockSpec((tn, K), lambda i, j: (j, 0))
        dot = _dot_nt
    else:
        (K, M), (_, N) = a.shape, b.shape
        a_spec = pl.BlockSpec((K, tm), lambda i, j: (0, i))
        b_spec = pl.BlockSpec((K, tn), lambda i, j: (0, j))
        dot = _dot_tn

    def body(a_ref, b_ref, o_ref):
        o_ref[...] = dot(a_ref[...], b_ref[...]).astype(out_dtype)

    if stack_cols:
        assert tm == M
        out_spec = pl.BlockSpec((None, tm, tn), lambda i, j: (j, 0, 0))
        out_shape = jax.ShapeDtypeStruct((N // tn, M, tn), out_dtype)
    else:
        out_spec = pl.BlockSpec((tm, tn), lambda i, j: (i, j))
        out_shape = jax.ShapeDtypeStruct((M, N), out_dtype)
    return _call(
        order, body, (a, b), name=name, grid=(M // tm, N // tn), in_specs=[a_spec, b_spec],
        out_specs=out_spec, out_shape=out_shape,
        compiler_params=_params(("parallel", "parallel")),
    )


def _assemble_win(order, wins, fas):
    def body(win_ref, fa_ref, o_ref):
        q = pl.program_id(0)

        @pl.when(q == 0)
        def _():
            o_ref[...] = jnp.zeros_like(o_ref)

        for k in range(NCHIP):
            @pl.when(q == k)
            def _(k=k):
                for j in range(WIN_UNITS):
                    cu = WIN_UNIT0[k] + j
                    dst = pl.ds(C2I[cu] * UNIT, UNIT)
                    if cu in OVERLAP_UNITS:
                        o_ref[dst, :] += win_ref[j * UNIT:(j + 1) * UNIT, :]
                    else:
                        o_ref[dst, :] = win_ref[j * UNIT:(j + 1) * UNIT, :]
                if k == 1:
                    o_ref[F_FA:F_FA + FA_ROWS, :] = fa_ref[...]

    return _call(
        order, body, (wins, fas), name="assemble_w_in", grid=(NCHIP,),
        in_specs=[pl.BlockSpec((None, WIN_ROWS, D), lambda q: (q, 0, 0)),
                  pl.BlockSpec((None, FA_ROWS, D), lambda q: (1, 0, 0))],
        out_specs=pl.BlockSpec((NP, D), lambda q: (0, 0)),
        out_shape=jax.ShapeDtypeStruct((NP, D), bf16),
        compiler_params=_params(("arbitrary",)),
    )


def _norm_inproj(order, x, g1, wt, rope):
    tm = 256
    c_t, s1_t, s2_t = rope

    def body(x_ref, g_ref, w_ref, c_ref, s1_ref, s2_ref, h_ref, qkvb_ref, qkva_ref, gates_ref, fa_ref):
        xb = x_ref[...]
        r = lax.rsqrt(jnp.mean(xb * xb, axis=-1, keepdims=True) + EPS)
        h = ((xb * r) * g_ref[...]).astype(bf16)
        h_ref[...] = h
        c, s1, s2 = c_ref[...], s1_ref[...], s2_ref[...]
        for p in range(2):
            pb = _dot_nt(h, w_ref[F_DIL + p * DIL_BLK:F_DIL + (p + 1) * DIL_BLK, :])
            for ch in range(DIL_BLK // 128):
                pc = pb[:, ch * 128:(ch + 1) * 128]
                if ch < 6:
                    pc = pc * c + pltpu.roll(pc, 120, 1) * s1 + pltpu.roll(pc, 8, 1) * s2
                qkvb_ref[:, p * DIL_BLK + ch * 128:p * DIL_BLK + (ch + 1) * 128] = pc
        qkva_ref[...] = _dot_nt(h, w_ref[F_FOX:F_FA, :]).astype(bf16)
        fa_ref[...] = _dot_nt(h, w_ref[F_FA:F_FA + 128, :])
        gates_ref[...] = _dot_nt(h, w_ref[F_G:NP, :]).astype(bf16)

    row = lambda w: pl.BlockSpec((tm, w), lambda i: (i, 0))
    return _call(
        order, body, (x, g1, wt, c_t, s1_t, s2_t), name="norm_inproj", grid=(S // tm,),
        in_specs=[row(D), pl.BlockSpec((1, D), lambda i: (0, 0)), pl.BlockSpec((NP, D), lambda i: (0, 0)),
                  row(128), row(128), row(128)],
        out_specs=[row(D), row(2 * DIL_BLK), row(4 * FOX_BLK), row(2 * D), row(128)],
        out_shape=[jax.ShapeDtypeStruct((S, D), bf16), jax.ShapeDtypeStruct((S, 2 * DIL_BLK), f32),
                   jax.ShapeDtypeStruct((S, 4 * FOX_BLK), bf16), jax.ShapeDtypeStruct((S, 2 * D), bf16),
                   jax.ShapeDtypeStruct((S, 128), f32)],
        compiler_params=_params(("parallel",)),
    )


def _forget_cumsum(order, fa, bpad):
    nb = S // TQ

    def body(fa_ref, b_ref, F_ref):
        rr = lax.broadcasted_iota(jnp.int32, (TQ, TQ), 0)
        cc = lax.broadcasted_iota(jnp.int32, (TQ, TQ), 1)
        tri = (rr >= cc).astype(bf16)
        lane = lax.broadcasted_iota(jnp.int32, (1, 128), 1)
        carry = jnp.zeros((1, 128), f32)
        for b in range(nb):
            z = fa_ref[b * TQ:(b + 1) * TQ, :] + b_ref[...]
            lf = jnp.minimum(z, 0.0) - jnp.log(1.0 + jnp.exp(-jnp.abs(z)))
            lf = jnp.where(lane < 8, lf, 0.0)
            hi, mid, lo = _split3(lf)
            fb = (_dot(tri, hi) + _dot(tri, mid)) + _dot(tri, lo) + carry
            F_ref[b * TQ:(b + 1) * TQ, :] = fb
            carry = fb[TQ - 1:TQ, :]

    return _call(
        order, body, (fa, bpad), name="forget_cumsum",
        out_shape=jax.ShapeDtypeStruct((S, 128), f32),
        compiler_params=_params(),
    )


def _head_masks():
    lane = lax.broadcasted_iota(jnp.int32, (1, 128), 1)
    return lane, (lane < HD, lane >= HD)


L_ONE = 3
FOX_TQ, FOX_TK = 256, 512


def _set_lanes(x, lane, first, cols):
    for n, col in enumerate(cols):
        x = jnp.where(lane == first + n, col, x)
    return x


def _f32_parts(col):
    return [t.astype(f32) for t in _split3(col)]


def _fox_operands(qkv_ref, F_ref, lse_ref, qa, ka, p, rows):
    lane, hm = _head_masks()
    q = qkv_ref[rows, 0:128].astype(f32) * 0.125
    k = qkv_ref[rows, 128:256].astype(f32)
    Fb = F_ref[rows, :]
    for hh in (0, 1):
        free = (1 - hh) * HD
        fcol = jnp.sum(jnp.where(lane == 2 * p + hh, Fb, 0.0), axis=1, keepdims=True)
        qterm = fcol if lse_ref is None else fcol - lse_ref[rows, hh * HD:hh * HD + 1]
        qcols = _f32_parts(qterm) + [1.0] * 3
        kcols = [1.0] * 3 + [-t for t in _f32_parts(fcol)]
        qa[hh, rows, :] = _set_lanes(jnp.where(hm[hh], q, 0.0), lane, free, qcols).astype(bf16)
        ka[hh, rows, :] = _set_lanes(k, lane, free, kcols).astype(bf16)


def _fox_fwd(order, qkva, F):
    tq, tk = FOX_TQ, FOX_TK

    def body(qkv_ref, F_ref, o_ref, lse_ref, qa, ka, vt):
        p = pl.program_id(0)
        keyi = lax.broadcasted_iota(jnp.int32, (tk, 1), 0)
        qryi = lax.broadcasted_iota(jnp.int32, (1, tq), 1)
        sub = lax.broadcasted_iota(jnp.int32, (128, 1), 0)

        def prep(i, c):
            rows = pl.ds(pl.multiple_of(i * tk, tk), tk)
            _fox_operands(qkv_ref, F_ref, None, qa, ka, p, rows)
            vt[i] = qkv_ref[rows, 256:384].astype(f32).T.astype(bf16)
            return c

        lax.fori_loop(0, S // tk, prep, 0)

        def qblock(i, first_half):
            r0 = pl.multiple_of(i * tq, tq)
            qh = [qa[hh, pl.ds(r0, tq), :] for hh in (0, 1)]

            def kv(jb, carry, masked, width):
                keys = pl.ds(pl.multiple_of(jb * tk, tk), width)
                sts = [_dot_nt(ka[hh, keys, :], qh[hh]) for hh in (0, 1)]
                new = []
                for hh in (0, 1):
                    m, l, a = carry[3 * hh:3 * hh + 3]
                    st = sts[hh]
                    if masked:
                        st = jnp.where(jb * tk + keyi[0:width] <= r0 + qryi, st, NEG)
                    mn = jnp.maximum(m, jnp.max(st, axis=0, keepdims=True))
                    al = jnp.exp(m - mn)
                    pt = jnp.exp(st - mn)
                    l = al * l + jnp.sum(pt, axis=0, keepdims=True)
                    a = al * a + _dot(vt[jb, hh * HD:(hh + 1) * HD, 0:width], pt.astype(bf16))
                    new += [mn, l, a]
                return tuple(new)

            init = (jnp.full((1, tq), NEG, f32), jnp.zeros((1, tq), f32), jnp.zeros((HD, tq), f32)) * 2
            last = (r0 + tq - 1) // tk
            carry = lax.fori_loop(0, last, lambda j, cr: kv(j, cr, False, tk), init)
            m0, l0, a0, m1, l1, a1 = kv(last, carry, True, tk // 2 if first_half else tk)
            ot = jnp.concatenate([a0 / l0, a1 / l1], axis=0)
            lt = jnp.where(sub < HD, m0 + jnp.log(l0), m1 + jnp.log(l1))
            o_ref[pl.ds(r0, tq), :] = ot.T.astype(bf16)
            lse_ref[pl.ds(r0, tq), :] = lt.T

        def qpair(t, c):
            qblock(2 * t, True)
            qblock(2 * t + 1, False)
            return c

        assert tk == 2 * tq
        lax.fori_loop(0, S // tk, qpair, 0)

    pair = pl.BlockSpec((S, 128), lambda p: (0, p))
    return _call(
        order, body, (qkva, F), name="fox_fwd", grid=(4,),
        in_specs=[pl.BlockSpec((S, FOX_BLK), lambda p: (0, p)), pl.BlockSpec((S, 128), lambda p: (0, 0))],
        out_specs=[pair, pair],
        out_shape=[jax.ShapeDtypeStruct((S, FOXW), bf16), jax.ShapeDtypeStruct((S, FOXW), f32)],
        scratch_shapes=[pltpu.VMEM((2, S, 128), bf16)] * 2 + [pltpu.VMEM((S // tk, 128, tk), bf16)],
        compiler_params=_params(("parallel",)),
    )


def _permute_in(dst, src, r):
    L = S // r
    for rho in range(r):
        dst[rho * L:(rho + 1) * L, :] = src[pl.ds(rho, L, stride=r), :]


def _permute_out(dst, src, r):
    L = S // r
    for rho in range(r):
        dst[pl.ds(rho, L, stride=r), :] = src[rho * L:(rho + 1) * L, :]


def _band_width(nbl):
    return BAND if nbl == 1 else 2 * BAND


def _band_geometry(bb, nbl):
    r0 = pl.multiple_of(bb * BAND, BAND)
    if nbl == 1:
        k0 = r0
    else:
        k0 = pl.multiple_of(jnp.maximum(bb - 1, 0) * BAND, BAND)
    sub0 = (bb - lax.rem(bb, nbl)) * BAND
    qi = r0 + lax.broadcasted_iota(jnp.int32, (BAND, 1), 0)
    ki = k0 + lax.broadcasted_iota(jnp.int32, (1, _band_width(nbl)), 1)
    diff = qi - ki
    valid = (diff >= 0) & (diff <= BAND) & (ki >= sub0)
    return r0, k0, valid


def _dil_views(ref):
    return [[ref.at[:, pl.ds((3 * role + g) * 128, 128)] for g in range(3)] for role in range(3)]


DIL_UNROLL = 4


def _dil_in_specs():
    return [pl.BlockSpec((S, 128), lambda p, k=k: (0, 9 * p + k)) for k in range(9)]


def _dil_fwd(order, qkvb):
    def body(*refs):
        q_refs, k_refs, v_refs = refs[0:3], refs[3:6], refs[6:9]
        ob_ref, lse_ref, qp, kp, vp, op, lp = refs[9:16]
        on, ln = refs[16:19], refs[19:22]
        _, hm = _head_masks()
        for g, r in enumerate(DIL):
            nbl = S // r // BAND
            if r == 1:
                qs_, ks_, vs_, od, ld = q_refs[g], k_refs[g], v_refs[g], on[g], ln[g]
            else:
                _permute_in(qp, q_refs[g], r)
                _permute_in(kp, k_refs[g], r)
                _permute_in(vp, v_refs[g], r)
                qs_, ks_, vs_, od, ld = qp, kp, vp, op, lp

            def blk(t, c, qs_=qs_, ks_=ks_, vs_=vs_, od=od, ld=ld, nbl=nbl):
                work = []
                for u in range(DIL_UNROLL):
                    r0, k0, valid = _band_geometry(DIL_UNROLL * t + u, nbl)
                    q = qs_[pl.ds(r0, BAND), :] * 0.125
                    kw = ks_[pl.ds(k0, _band_width(nbl)), :].astype(bf16)
                    vw = vs_[pl.ds(k0, _band_width(nbl)), :]
                    for hh in (0, 1):
                        qh = jnp.where(hm[hh], q, 0.0).astype(bf16)
                        work.append((u, hh, r0, valid, vw, _dot_nt(qh, kw)))
                o = [jnp.zeros((BAND, 128), f32)] * DIL_UNROLL
                lse = [jnp.zeros((BAND, 128), f32)] * DIL_UNROLL
                for u, hh, r0, valid, vw, s in work:
                    s = jnp.where(valid, s, NEG)
                    m = jnp.max(s, axis=1, keepdims=True)
                    pr = jnp.exp(s - m)
                    l = jnp.sum(pr, axis=1, keepdims=True)
                    vm = jnp.where(hm[hh], vw, 0.0).astype(bf16)
                    o[u] = o[u] + _dot((pr / l).astype(bf16), vm)
                    lse[u] = jnp.where(hm[hh], m + jnp.log(l), lse[u])
                    if hh == 1:
                        od[pl.ds(r0, BAND), :] = o[u]
                        ld[pl.ds(r0, BAND), :] = lse[u]
                return c

            lax.fori_loop(0, S // BAND // DIL_UNROLL, blk, 0)
            if r != 1:
                _permute_out(on[g], op, r)
                _permute_out(ln[g], lp, r)

        def combine(i, c):
            r0 = pl.multiple_of(i * TQ, TQ)
            ls = [ln[g][pl.ds(r0, TQ), :] for g in range(3)]
            mx = jnp.maximum(jnp.maximum(ls[0], ls[1]), ls[2])
            es = [jnp.exp(l - mx) for l in ls]
            tot = (es[0] + es[1]) + es[2]
            acc = (es[0] / tot) * on[0][pl.ds(r0, TQ), :]
            acc = acc + (es[1] / tot) * on[1][pl.ds(r0, TQ), :]
            acc = acc + (es[2] / tot) * on[2][pl.ds(r0, TQ), :]
            ob_ref[pl.ds(r0, TQ), :] = acc.astype(bf16)
            lse_ref[pl.ds(r0, TQ), :] = mx + jnp.log(tot)
            return c

        lax.fori_loop(0, S // TQ, combine, 0)

    out_blk = pl.BlockSpec((S, 128), lambda p: (0, p))
    return _call(
        order, body, [qkvb] * 9, name="dil_fwd", grid=(2,),
        in_specs=_dil_in_specs(), out_specs=[out_blk, out_blk],
        out_shape=[jax.ShapeDtypeStruct((S, DILOUT), bf16), jax.ShapeDtypeStruct((S, DILOUT), f32)],
        scratch_shapes=[pltpu.VMEM((S, 128), f32)] * 11,
        compiler_params=_params(("parallel",)),
    )


def _branch_mix(order, oa, ob, was, wbs, gates):
    tm = 512

    def body(oa_ref, ob_ref, wa_ref, wb_ref, g_ref, ya_ref, yb_ref, mix_ref):
        oa_b, ob_b = oa_ref[...], ob_ref[...]
        for q in range(NCHIP):
            cols = slice(q * 256, (q + 1) * 256)
            ya = _dot(oa_b, wa_ref[q])
            yb = _dot(ob_b, wb_ref[q])
            ya_ref[:, cols] = ya.astype(bf16)
            yb_ref[:, cols] = yb.astype(bf16)
            ga = g_ref[:, q * 256:(q + 1) * 256].astype(f32)
            gb = g_ref[:, D + q * 256:D + (q + 1) * 256].astype(f32)
            mix_ref[:, cols] = (jax.nn.sigmoid(ga) * ya + jax.nn.sigmoid(gb) * yb).astype(bf16)

    row = lambda w: pl.BlockSpec((tm, w), lambda i: (i, 0))
    full3 = lambda a: pl.BlockSpec(a.shape, lambda i: (0, 0, 0))
    return _call(
        order, body, (oa, ob, was, wbs, gates), name="branch_mix", grid=(S // tm,),
        in_specs=[row(FOXW), row(DILOUT), full3(was), full3(wbs), row(2 * D)],
        out_specs=[row(D), row(D), row(D)],
        out_shape=[jax.ShapeDtypeStruct((S, D), bf16), jax.ShapeDtypeStruct((S, D), bf16),
                   jax.ShapeDtypeStruct((S, D), bf16)],
        compiler_params=_params(("parallel",)),
    )


def _outproj_norm(order, mixed, wout, x, g2):
    tm = 512

    def body(m_ref, w_ref, x_ref, g_ref, x2_ref, h2_ref):
        x2 = x_ref[...] + _dot(m_ref[...], w_ref[...])
        x2_ref[...] = x2
        r = lax.rsqrt(jnp.mean(x2 * x2, axis=-1, keepdims=True) + EPS)
        h2_ref[...] = ((x2 * r) * g_ref[...]).astype(bf16)

    row = pl.BlockSpec((tm, D), lambda i: (i, 0))
    return _call(
        order, body, (mixed, wout, x, g2), name="outproj_norm", grid=(S // tm,),
        in_specs=[row, pl.BlockSpec((D, D), lambda i: (0, 0)), row, pl.BlockSpec((1, D), lambda i: (0, 0))],
        out_specs=[row, row],
        out_shape=[jax.ShapeDtypeStruct((S, D), f32), jax.ShapeDtypeStruct((S, D), bf16)],
        compiler_params=_params(("parallel",)),
    )


def _mlp_up(order, h2, wups):
    tm = 1024

    def body(h_ref, w_ref, ru_ref, a_ref):
        ru = jnp.maximum(_dot(h_ref[...], w_ref[...]), 0.0)
        ru_ref[...] = ru.astype(bf16)
        a_ref[...] = (ru * ru).astype(bf16)

    out = pl.BlockSpec((tm, D), lambda q, i: (i, q))
    return _call(
        order, body, (h2, wups), name="mlp_up", grid=(NCHIP, S // tm),
        in_specs=[pl.BlockSpec((tm, D), lambda q, i: (i, 0)), pl.BlockSpec((None, D, D), lambda q, i: (q, 0, 0))],
        out_specs=[out, out],
        out_shape=[jax.ShapeDtypeStruct((S, DFF), bf16), jax.ShapeDtypeStruct((S, DFF), bf16)],
        compiler_params=_params(("parallel", "parallel")),
    )


def _mlp_down_loss(order, a, wdown, x2, g3, tgt):
    tm = 512

    def body(a_ref, w_ref, x2_ref, g_ref, t_ref, dx_ref, dxb_ref, dg_ref, loss_ref):
        i = pl.program_id(0)
        x3 = x2_ref[...] + _dot(a_ref[...], w_ref[...])
        r = lax.rsqrt(jnp.mean(x3 * x3, axis=-1, keepdims=True) + EPS)
        xh = x3 * r
        g = g_ref[...]
        e = xh * g - t_ref[...]
        part = 0.5 * jnp.sum(jnp.mean(e * e, axis=-1, keepdims=True), axis=0, keepdims=True)
        dy = e * (1.0 / D)
        gdy = dy * g
        dx = r * (gdy - xh * jnp.mean(gdy * xh, axis=-1, keepdims=True))
        dx_ref[...] = dx
        dxb_ref[...] = dx.astype(bf16)

        @pl.when(i == 0)
        def _():
            dg_ref[...] = jnp.zeros_like(dg_ref)
            loss_ref[...] = jnp.zeros_like(loss_ref)

        dg_ref[...] += jnp.sum(dy * xh, axis=0, keepdims=True)
        loss_ref[...] += jnp.broadcast_to(part, (1, 128))

    row = pl.BlockSpec((tm, D), lambda i: (i, 0))
    vec = pl.BlockSpec((1, D), lambda i: (0, 0))
    return _call(
        order, body, (a, wdown, x2, g3, tgt), name="mlp_down_loss", grid=(S // tm,),
        in_specs=[pl.BlockSpec((tm, DFF), lambda i: (i, 0)), pl.BlockSpec((DFF, D), lambda i: (0, 0)), row, vec, row],
        out_specs=[row, row, vec, pl.BlockSpec((1, 128), lambda i: (0, 0))],
        out_shape=[jax.ShapeDtypeStruct((S, D), f32), jax.ShapeDtypeStruct((S, D), bf16),
                   jax.ShapeDtypeStruct((1, D), f32), jax.ShapeDtypeStruct((1, 128), f32)],
        compiler_params=_params(("arbitrary",)),
    )


def _mlp_down_bwd(order, dx3b, wdown, u):
    tm = 512

    def body(d_ref, w_ref, u_ref, du_ref):
        d = d_ref[...]
        for q in range(NCHIP):
            cols = slice(q * D, (q + 1) * D)
            da = _dot_nt(d, w_ref[cols, :])
            du_ref[:, cols] = (da * (2.0 * u_ref[:, cols].astype(f32))).astype(bf16)

    return _call(
        order, body, (dx3b, wdown, u), name="mlp_down_bwd", grid=(S // tm,),
        in_specs=[pl.BlockSpec((tm, D), lambda i: (i, 0)), pl.BlockSpec((DFF, D), lambda i: (0, 0)),
                  pl.BlockSpec((tm, DFF), lambda i: (i, 0))],
        out_specs=pl.BlockSpec((tm, DFF), lambda i: (i, 0)),
        out_shape=jax.ShapeDtypeStruct((S, DFF), bf16),
        compiler_params=_params(("parallel",)),
    )


def _mlp_up_bwd(order, du, wups, x2, dx3, g2):
    tm = 512

    def body(du_ref, w_ref, x2_ref, dx3_ref, g_ref, dx2_ref, dx2b_ref, dg_ref):
        i = pl.program_id(0)
        dh = jnp.zeros((tm, D), f32)
        for q in range(NCHIP):
            dh = dh + _dot_nt(du_ref[:, q * D:(q + 1) * D], w_ref[q])
        x2 = x2_ref[...]
        r = lax.rsqrt(jnp.mean(x2 * x2, axis=-1, keepdims=True) + EPS)
        xh = x2 * r
        gdh = dh * g_ref[...]
        dx2 = dx3_ref[...] + r * (gdh - xh * jnp.mean(gdh * xh, axis=-1, keepdims=True))
        dx2_ref[...] = dx2
        dx2b_ref[...] = dx2.astype(bf16)

        @pl.when(i == 0)
        def _():
            dg_ref[...] = jnp.zeros_like(dg_ref)

        dg_ref[...] += jnp.sum(dh * xh, axis=0, keepdims=True)

    row = pl.BlockSpec((tm, D), lambda i: (i, 0))
    vec = pl.BlockSpec((1, D), lambda i: (0, 0))
    return _call(
        order, body, (du, wups, x2, dx3, g2), name="mlp_up_bwd", grid=(S // tm,),
        in_specs=[pl.BlockSpec((tm, DFF), lambda i: (i, 0)), pl.BlockSpec((NCHIP, D, D), lambda i: (0, 0, 0)),
                  row, row, vec],
        out_specs=[row, row, vec],
        out_shape=[jax.ShapeDtypeStruct((S, D), f32), jax.ShapeDtypeStruct((S, D), bf16),
                   jax.ShapeDtypeStruct((1, D), f32)],
        compiler_params=_params(("arbitrary",)),
    )


def _gate_bwd(order, dx2b, wout, gates, ya, yb):
    tm = 512

    def body(d_ref, w_ref, g_ref, ya_ref, yb_ref, dya_ref, dyb_ref, dproj_ref):
        dm = _dot_nt(d_ref[...], w_ref[...])
        sa = jax.nn.sigmoid(g_ref[:, 0:D].astype(f32))
        sb = jax.nn.sigmoid(g_ref[:, D:2 * D].astype(f32))
        dya_ref[...] = (dm * sa).astype(bf16)
        dyb_ref[...] = (dm * sb).astype(bf16)
        dproj_ref[:, 0:D] = (dm * ya_ref[...].astype(f32) * (sa * (1.0 - sa))).astype(bf16)
        dproj_ref[:, D:2 * D] = (dm * yb_ref[...].astype(f32) * (sb * (1.0 - sb))).astype(bf16)

    row = lambda w: pl.BlockSpec((tm, w), lambda i: (i, 0))
    return _call(
        order, body, (dx2b, wout, gates, ya, yb), name="gate_bwd", grid=(S // tm,),
        in_specs=[row(D), pl.BlockSpec((D, D), lambda i: (0, 0)), row(2 * D), row(D), row(D)],
        out_specs=[row(D), row(D), pl.BlockSpec((tm, 2 * D), lambda i: (i, F_G // (2 * D)))],
        out_shape=[jax.ShapeDtypeStruct((S, D), bf16), jax.ShapeDtypeStruct((S, D), bf16),
                   jax.ShapeDtypeStruct((S, NP), bf16)],
        compiler_params=_params(("parallel",)),
    )


def _branch_bwd(order, dya, dyb, was, wbs):
    tm = 512

    def body(dya_ref, dyb_ref, wa_ref, wb_ref, doa_ref, dob_ref):
        doa = jnp.zeros((tm, FOXW), f32)
        dob = jnp.zeros((tm, DILOUT), f32)
        for q in range(NCHIP):
            cols = slice(q * 256, (q + 1) * 256)
            doa = doa + _dot_nt(dya_ref[:, cols], wa_ref[q])
            dob = dob + _dot_nt(dyb_ref[:, cols], wb_ref[q])
        doa_ref[...] = doa.astype(bf16)
        dob_ref[...] = dob

    row = lambda w: pl.BlockSpec((tm, w), lambda i: (i, 0))
    full3 = lambda a: pl.BlockSpec(a.shape, lambda i: (0, 0, 0))
    return _call(
        order, body, (dya, dyb, was, wbs), name="branch_bwd", grid=(S // tm,),
        in_specs=[row(D), row(D), full3(was), full3(wbs)],
        out_specs=[row(FOXW), row(DILOUT)],
        out_shape=[jax.ShapeDtypeStruct((S, FOXW), bf16), jax.ShapeDtypeStruct((S, DILOUT), f32)],
        compiler_params=_params(("parallel",)),
    )


def _branch_wgrad(order, oa, ob, dya, dyb):
    def body(oa_ref, ob_ref, dya_ref, dyb_ref, dwa_ref, dwb_ref):
        dwa_ref[...] = _dot_tn(oa_ref[...], dya_ref[...])
        dwb_ref[...] = _dot_tn(ob_ref[...], dyb_ref[...])

    full = lambda w: pl.BlockSpec((S, w), lambda q: (0, 0))
    colq = pl.BlockSpec((S, 256), lambda q: (0, q))
    return _call(
        order, body, (oa, ob, dya, dyb), name="branch_wgrad", grid=(NCHIP,),
        in_specs=[full(FOXW), full(DILOUT), colq, colq],
        out_specs=[pl.BlockSpec((None, FOXW, 256), lambda q: (q, 0, 0)),
                   pl.BlockSpec((None, DILOUT, 256), lambda q: (q, 0, 0))],
        out_shape=[jax.ShapeDtypeStruct((NCHIP, FOXW, 256), f32), jax.ShapeDtypeStruct((NCHIP, DILOUT, 256), f32)],
        compiler_params=_params(("parallel",)),
    )


def _fox_bwd(order, qkva, doa, oa, lse, F, dproj):
    tq, tk = FOX_TQ, FOX_TK

    def body(qkv_ref, do_ref, o_ref, lse_ref, F_ref, _dproj_in, dF_ref, dqkv_ref, qa, ka, da, va, kat,
             dk_scr, dv_scr, dqt_scr):
        p = pl.program_id(0)
        lane, hm = _head_masks()
        keyi = lax.broadcasted_iota(jnp.int32, (tk, 1), 0)
        qryi = lax.broadcasted_iota(jnp.int32, (1, tq), 1)

        def prep(i, c):
            rows = pl.ds(pl.multiple_of(i * tk, tk), tk)
            _fox_operands(qkv_ref, F_ref, lse_ref, qa, ka, p, rows)
            do = do_ref[rows, :].astype(f32)
            prod = do * o_ref[rows, :].astype(f32)
            v = qkv_ref[rows, 256:384].astype(f32)
            for hh in (0, 1):
                free = (1 - hh) * HD
                delta = jnp.sum(jnp.where(hm[hh], prod, 0.0), axis=1, keepdims=True)
                da[hh, rows, :] = _set_lanes(jnp.where(hm[hh], do, 0.0), lane, free,
                                             [-t for t in _f32_parts(delta)]).astype(bf16)
                va[hh, rows, :] = _set_lanes(v, lane, free, [1.0] * 3).astype(bf16)
                kat[hh, i] = ka[hh, rows, :].astype(f32).T.astype(bf16)
                dk_scr[hh, rows, :] = jnp.zeros((tk, 128), f32)
                dv_scr[hh, rows, :] = jnp.zeros((tk, 128), f32)
            return c

        lax.fori_loop(0, S // tk, prep, 0)

        def qblock(i, first_half):
            r0 = pl.multiple_of(i * tq, tq)
            qrows = pl.ds(r0, tq)
            qh = [qa[hh, qrows, :] for hh in (0, 1)]
            dh = [da[hh, qrows, :] for hh in (0, 1)]
            dqt_scr[...] = jnp.zeros_like(dqt_scr)

            def kv(jb, c2, masked, width):
                keys = pl.ds(pl.multiple_of(jb * tk, tk), width)
                sts = [_dot_nt(ka[hh, keys, :], qh[hh]) for hh in (0, 1)]
                dps = [_dot_nt(va[hh, keys, :], dh[hh]) for hh in (0, 1)]
                for hh in (0, 1):
                    pt = jnp.exp(sts[hh])
                    if masked:
                        pt = jnp.where(jb * tk + keyi[0:width] <= r0 + qryi, pt, 0.0)
                    dsb = (pt * dps[hh]).astype(bf16)
                    dv_scr[hh, keys, :] += _dot(pt.astype(bf16), dh[hh])
                    dk_scr[hh, keys, :] += _dot(dsb, qh[hh])
                    dqt_scr[hh] += _dot(kat[hh, jb, :, 0:width], dsb)
                return c2

            last = (r0 + tq - 1) // tk
            lax.fori_loop(0, last, lambda j, c2: kv(j, c2, False, tk), 0)
            kv(last, 0, True, tk // 2 if first_half else tk)
            dq0, dq1 = dqt_scr[0].T, dqt_scr[1].T
            dqkv_ref[qrows, 0:128] = (jnp.where(hm[0], dq0, dq1) * 0.125).astype(bf16)
            dF_ref[qrows, :] = jnp.where(lane == 0, dq0[:, HD:HD + 1], jnp.where(lane == 1, dq1[:, 0:1], 0.0))

        def qpair(t, c):
            qblock(2 * t, True)
            qblock(2 * t + 1, False)
            return c

        assert tk == 2 * tq
        lax.fori_loop(0, S // tk, qpair, 0)

        def finish(i, c):
            rows = pl.ds(pl.multiple_of(i * tq, tq), tq)
            dk0, dk1 = dk_scr[0, rows, :], dk_scr[1, rows, :]
            dqkv_ref[rows, 128:256] = jnp.where(hm[0], dk0, dk1).astype(bf16)
            dqkv_ref[rows, 256:384] = jnp.where(hm[0], dv_scr[0, rows, :], dv_scr[1, rows, :]).astype(bf16)
            cs = jnp.where(lane == 0, dk0[:, HD + L_ONE:HD + L_ONE + 1],
                           jnp.where(lane == 1, dk1[:, L_ONE:L_ONE + 1], 0.0))
            dF_ref[rows, :] = dF_ref[rows, :] - cs
            return c

        lax.fori_loop(0, S // tq, finish, 0)

    pair = pl.BlockSpec((S, 128), lambda p: (0, p))
    return _call(
        order, body, (qkva, doa, oa, lse, F, dproj), name="fox_bwd", grid=(4,),
        in_specs=[pl.BlockSpec((S, FOX_BLK), lambda p: (0, p)), pair, pair, pair,
                  pl.BlockSpec((S, 128), lambda p: (0, 0)), pl.BlockSpec(memory_space=pl.ANY)],
        out_specs=[pair, pl.BlockSpec((S, FOX_BLK), lambda p: (0, F_FOX // FOX_BLK + p))],
        out_shape=[jax.ShapeDtypeStruct((S, FOXW), f32), jax.ShapeDtypeStruct((S, NP), bf16)],
        input_output_aliases={5: 1},
        scratch_shapes=[pltpu.VMEM((2, S, 128), bf16)] * 4 + [pltpu.VMEM((2, S // tk, 128, tk), bf16)]
        + [pltpu.VMEM((2, S, 128), f32)] * 2 + [pltpu.VMEM((2, 128, tq), f32)],
        compiler_params=_params(("parallel",)),
    )


def _forget_bwd(order, dF, fa, bpad, dproj):
    nb = S // TQ

    def body(dF_ref, fa_ref, b_ref, _dproj_in, db_ref, dfa_ref):
        rr = lax.broadcasted_iota(jnp.int32, (TQ, TQ), 0)
        cc = lax.broadcasted_iota(jnp.int32, (TQ, TQ), 1)
        upper = (cc >= rr).astype(bf16)
        lane = lax.broadcasted_iota(jnp.int32, (1, 128), 1)
        carry = jnp.zeros((1, 128), f32)
        db = jnp.zeros((1, 128), f32)
        for b in reversed(range(nb)):
            cols = jnp.zeros((TQ, 128), f32)
            for h in range(8):
                c0 = (h // 2) * 128 + h % 2
                cols = jnp.where(lane == h, dF_ref[b * TQ:(b + 1) * TQ, c0:c0 + 1], cols)
            dlf = carry
            for part in _split3(cols):
                dlf = dlf + _dot(upper, part)
            carry = carry + jnp.sum(cols, axis=0, keepdims=True)
            z = fa_ref[b * TQ:(b + 1) * TQ, :] + b_ref[...]
            dz = jnp.where(lane < 8, dlf * jax.nn.sigmoid(-z), 0.0)
            dfa_ref[b * TQ:(b + 1) * TQ, 0:128] = dz.astype(bf16)
            dfa_ref[b * TQ:(b + 1) * TQ, 128:256] = jnp.zeros((TQ, 128), bf16)
            db = db + jnp.sum(dz, axis=0, keepdims=True)
        db_ref[...] = db

    whole = lambda a: pl.BlockSpec(a.shape, lambda i: (0,) * a.ndim)
    return _call(
        order, body, (dF, fa, bpad, dproj), name="forget_bwd", grid=(1,),
        in_specs=[whole(dF), whole(fa), whole(bpad), pl.BlockSpec(memory_space=pl.ANY)],
        out_specs=[pl.BlockSpec((1, 128), lambda i: (0, 0)), pl.BlockSpec((S, 256), lambda i: (0, F_FA // 256))],
        out_shape=[jax.ShapeDtypeStruct((1, 128), f32), jax.ShapeDtypeStruct((S, NP), bf16)],
        input_output_aliases={3: 1},
        compiler_params=_params(("arbitrary",)),
    )


def _dil_bwd(order, qkvb, dob, ob, lseb, rope, dproj):
    c_t, s1_t, s2_t = rope

    def body(*refs):
        q_refs, k_refs, v_refs = refs[0:3], refs[3:6], refs[6:9]
        dob_ref, ob_ref, lse_ref, c_ref, s1_ref, s2_ref, _dproj_in, dqkv_ref = refs[9:17]
        qp, kp, vp, dop, lp, dlp, dln, dqp, dkp, dvp, nat = refs[17:28]
        dq_out, dk_out, dv_out = _dil_views(dqkv_ref)
        _, hm = _head_masks()

        def delta_rows(i, c):
            r0 = pl.multiple_of(i * TQ, TQ)
            prod = dob_ref[pl.ds(r0, TQ), :] * ob_ref[pl.ds(r0, TQ), :].astype(f32)
            d0 = jnp.sum(jnp.where(hm[0], prod, 0.0), axis=1, keepdims=True)
            d1 = jnp.sum(jnp.where(hm[1], prod, 0.0), axis=1, keepdims=True)
            dln[pl.ds(r0, TQ), :] = jnp.where(hm[0], d0, d1)
            return c

        lax.fori_loop(0, S // TQ, delta_rows, 0)

        for g, r in enumerate(DIL):
            nbl = S // r // BAND
            if r == 1:
                srcs = (q_refs[g], k_refs[g], v_refs[g], dob_ref, lse_ref, dln)
            else:
                for dst, src in ((qp, q_refs[g]), (kp, k_refs[g]), (vp, v_refs[g]), (dop, dob_ref),
                                 (lp, lse_ref), (dlp, dln)):
                    _permute_in(dst, src, r)
                srcs = (qp, kp, vp, dop, lp, dlp)
            dkp[...] = jnp.zeros_like(dkp)
            dvp[...] = jnp.zeros_like(dvp)

            def blk(t, c, srcs=srcs, nbl=nbl):
                qs_, ks_, vs_, dos_, ls_, dls_ = srcs
                work = []
                for u in range(DIL_UNROLL):
                    r0, k0, valid = _band_geometry(DIL_UNROLL * t + u, nbl)
                    q = qs_[pl.ds(r0, BAND), :] * 0.125
                    kwf = ks_[pl.ds(k0, _band_width(nbl)), :]
                    kw = kwf.astype(bf16)
                    vw = vs_[pl.ds(k0, _band_width(nbl)), :].astype(bf16)
                    do = dos_[pl.ds(r0, BAND), :]
                    lse = ls_[pl.ds(r0, BAND), :]
                    dlt = dls_[pl.ds(r0, BAND), :]
                    for hh in (0, 1):
                        qh = jnp.where(hm[hh], q, 0.0).astype(bf16)
                        doh = jnp.where(hm[hh], do, 0.0).astype(bf16)
                        kh = jnp.where(hm[hh], kwf, 0.0).astype(bf16)
                        work.append((u, hh, r0, k0, valid, qh, doh, kh, lse[:, hh * HD:hh * HD + 1],
                                     dlt[:, hh * HD:hh * HD + 1], _dot_nt(qh, kw), _dot_nt(doh, vw)))
                for u, hh, r0, k0, valid, qh, doh, kh, lse_h, dlt_h, s, dp in work:
                    if hh == 0:
                        dq = jnp.zeros((BAND, 128), f32)
                        dk = jnp.zeros((_band_width(nbl), 128), f32)
                        dv = jnp.zeros((_band_width(nbl), 128), f32)
                    pr = jnp.where(valid, jnp.exp(s - lse_h), 0.0)
                    dsb = (pr * (dp - dlt_h)).astype(bf16)
                    dv = dv + _dot_tn(pr.astype(bf16), doh)
                    dk = dk + _dot_tn(dsb, qh)
                    dq = dq + _dot(dsb, kh)
                    if hh == 1:
                        dqp[pl.ds(r0, BAND), :] = dq * 0.125
                        dkp[pl.ds(k0, _band_width(nbl)), :] += dk
                        dvp[pl.ds(k0, _band_width(nbl)), :] += dv
                return c

            lax.fori_loop(0, S // BAND // DIL_UNROLL, blk, 0)

            for acc, out, roped in ((dqp, dq_out[g], True), (dkp, dk_out[g], True), (dvp, dv_out[g], False)):
                if r == 1:
                    src = acc
                else:
                    _permute_out(nat, acc, r)
                    src = nat

                def emit(i, c, src=src, out=out, roped=roped):
                    r0 = pl.multiple_of(i * TQ, TQ)
                    d = src[pl.ds(r0, TQ), :]
                    if roped:
                        d = (d * c_ref[pl.ds(r0, TQ), :] + pltpu.roll(d * s1_ref[pl.ds(r0, TQ), :], 8, 1)
                             + pltpu.roll(d * s2_ref[pl.ds(r0, TQ), :], 120, 1))
                    out[pl.ds(r0, TQ), :] = d.astype(bf16)
                    return c

                lax.fori_loop(0, S // TQ, emit, 0)

    pair = pl.BlockSpec((S, 128), lambda p: (0, p))
    tab = pl.BlockSpec((S, 128), lambda p: (0, 0))
    blk_spec = pl.BlockSpec((S, DIL_BLK), lambda p: (0, p))
    return _call(
        order, body, [qkvb] * 9 + [dob, ob, lseb, c_t, s1_t, s2_t, dproj], name="dil_bwd", grid=(2,),
        in_specs=_dil_in_specs() + [pair, pair, pair, tab, tab, tab, pl.BlockSpec(memory_space=pl.ANY)],
        out_specs=blk_spec,
        out_shape=jax.ShapeDtypeStruct((S, NP), bf16),
        input_output_aliases={15: 0},
        scratch_shapes=[pltpu.VMEM((S, 128), f32)] * 11,
        compiler_params=_params(("parallel",)),
    )


def _inproj_bwd(order, dproj, wt, x, dx2, g1):
    tm = 256

    def body(d_ref, w_ref, x_ref, dx2_ref, g_ref, dx_ref, dg_ref):
        i = pl.program_id(0)
        dh = _dot(d_ref[...], w_ref[...])
        xb = x_ref[...]
        r = lax.rsqrt(jnp.mean(xb * xb, axis=-1, keepdims=True) + EPS)
        xh = xb * r
        gdh = dh * g_ref[...]
        dx_ref[...] = dx2_ref[...] + r * (gdh - xh * jnp.mean(gdh * xh, axis=-1, keepdims=True))

        @pl.when(i == 0)
        def _():
            dg_ref[...] = jnp.zeros_like(dg_ref)

        dg_ref[...] += jnp.sum(dh * xh, axis=0, keepdims=True)

    row = pl.BlockSpec((tm, D), lambda i: (i, 0))
    vec = pl.BlockSpec((1, D), lambda i: (0, 0))
    return _call(
        order, body, (dproj, wt, x, dx2, g1), name="inproj_bwd", grid=(S // tm,),
        in_specs=[pl.BlockSpec((tm, NP), lambda i: (i, 0)), pl.BlockSpec((NP, D), lambda i: (0, 0)), row, row, vec],
        out_specs=[row, vec],
        out_shape=[jax.ShapeDtypeStruct((S, D), f32), jax.ShapeDtypeStruct((1, D), f32)],
        compiler_params=_params(("arbitrary",)),
    )


HBM = pl.BlockSpec(memory_space=pltpu.HBM)
SEM = pl.BlockSpec(memory_space=pltpu.SEMAPHORE)
SMALL_ROWS = 8


def _comm_call(name, body, bufs, order, sems_in=(), new_sems=(), behind=()):
    nb, ns, nn = len(bufs), len(sems_in), len(new_sems)
    extra = order.token_for(bufs) + list(behind)

    def kern(*refs):
        off = nb + ns + len(extra)
        body(refs[:nb], refs[nb:nb + ns], refs[off:off + nn])
        refs[-1][...] = jnp.zeros((8, 128), f32)

    res = pl.pallas_call(
        kern, name=name,
        in_specs=[HBM] * nb + [SEM] * ns + [pl.BlockSpec(memory_space=pl.ANY)] * len(extra),
        out_specs=[SEM] * nn + [HBM] * nb + [pl.BlockSpec(memory_space=pltpu.VMEM)],
        out_shape=[pltpu.SemaphoreType.DMA((k,)) for k in new_sems] + [pltpu.HBM(b.shape, b.dtype) for b in bufs]
        + [jax.ShapeDtypeStruct((8, 128), f32)],
        input_output_aliases={i: nn + i for i in range(nb)},
        compiler_params=pltpu.CompilerParams(has_side_effects=pltpu.SideEffectType.DATAFLOW_SIDE_EFFECTING),
    )(*[pltpu.with_memory_space_constraint(b, pltpu.HBM) for b in bufs], *sems_in, *extra)
    order.mark(res[-1])
    return list(res[:nn]), list(res[nn:nn + nb])


def _place():
    x, y, c = lax.axis_index("x"), lax.axis_index("y"), lax.axis_index("c")
    chips = [(1 - x, y), (x, 1 - y), (1 - x, 1 - y)]
    return x, y, c, chips


def _rcopy(src, dst, ssem, rsem, dev):
    return pltpu.make_async_remote_copy(src_ref=src, dst_ref=dst, send_sem=ssem, recv_sem=rsem,
                                        device_id=dev, device_id_type=pl.DeviceIdType.MESH)


def _half(nrows, which):
    return pl.ds(which * (nrows // 2), nrows // 2)


def _ici_copies(stack, ssem, rsem, relay):
    x, y, c, chips = _place()
    me_q = 2 * x + y
    sends, recvs = {}, {}
    for a in range(len(stack)):
        rows = _half(stack[a].shape[1], c)
        for j, (cx, cy) in enumerate(chips):
            if relay and a == 0 and j == 2:
                continue
            mine = stack[a].at[me_q, rows]
            sends[a, j] = _rcopy(mine, mine, ssem.at[a * 3 + j], rsem.at[a * 3 + j], (cx, cy, c))
            theirs = stack[a].at[2 * cx + cy, rows]
            recvs[a, j] = _rcopy(theirs, theirs, ssem.at[a * 3 + j], rsem.at[a * 3 + j], (cx, cy, c))
    return sends, recvs


def _relay_copies(win, ssem, rsem):
    x, y, c, chips = _place()
    quarter = win.shape[1] // 4
    sends, recvs = [], []
    for k in range(2):
        rows = pl.ds(c * 2 * quarter + k * quarter, quarter)
        (fx, fy), (tx, ty) = chips[k], chips[1 - k]
        landed = win.at[2 * fx + fy, rows]
        sends.append(_rcopy(landed, landed, ssem.at[k], rsem.at[k], (tx, ty, c)))
        far = win.at[2 * chips[2][0] + chips[2][1], rows]
        recvs.append(_rcopy(far, far, ssem.at[k], rsem.at[k], (tx, ty, c)))
    return sends, recvs


def _allgather_start(name, stacks, order, relay=False):
    n = len(stacks)

    def body(bufs, _, new):
        sends, _r = _ici_copies(bufs, new[0], new[1], relay)
        for cp in sends.values():
            cp.start()

    return _comm_call(name, body, stacks, order, new_sems=(3 * n, 3 * n))


def _forward_copies(stack, ssem, rsem, relay=False):
    x, y, c, chips = _place()
    sib = (x, y, 1 - c)
    sends, recvs = {}, {}
    for a in range(len(stack)):
        for j, (cx, cy) in enumerate(chips):
            if relay and a == 0 and j == 2:
                continue
            landed = stack[a].at[2 * cx + cy, _half(stack[a].shape[1], c)]
            sends[a, j] = _rcopy(landed, landed, ssem.at[a * 3 + j], rsem.at[a * 3 + j], sib)
            other = stack[a].at[2 * cx + cy, _half(stack[a].shape[1], 1 - c)]
            recvs[a, j] = _rcopy(other, other, ssem.at[a * 3 + j], rsem.at[a * 3 + j], sib)
    return sends, recvs


def _far_forward(win, ssem, rsem):
    x, y, c, chips = _place()
    sib, far_q = (x, y, 1 - c), 2 * chips[2][0] + chips[2][1]
    landed, other = win.at[far_q, _half(win.shape[1], c)], win.at[far_q, _half(win.shape[1], 1 - c)]
    return _rcopy(landed, landed, ssem.at[0], rsem.at[0], sib), _rcopy(other, other, ssem.at[0], rsem.at[0], sib)


def _allgather_forward(name, stacks, sems, order, behind=(), relay=False):
    n = len(stacks)

    def body(bufs, taken, new):
        sends, recvs = _ici_copies(bufs, taken[0], taken[1], relay)
        fwd, _r = _forward_copies(bufs, new[0], new[1], relay)
        relay_sends = _relay_copies(bufs[0], new[2], new[3])[0] if relay else []
        for (a, j), arrived in recvs.items():
            arrived.wait_recv()
            fwd[a, j].start()
            if relay and a == 0:
                relay_sends[j].start()
        for cp in sends.values():
            cp.wait_send()

    return _comm_call(name, body, stacks, order, sems_in=sems, behind=behind,
                      new_sems=(3 * n, 3 * n) + ((2, 2) if relay else ()))


def _allgather_finish(name, stacks, sems, order, relay=False):
    def body(bufs, taken, new):
        sends, recvs = _forward_copies(bufs, taken[0], taken[1], relay)
        if relay:
            relay_sends, relay_recvs = _relay_copies(bufs[0], taken[2], taken[3])
            for cp in relay_recvs:
                cp.wait_recv()
            _far_forward(bufs[0], new[0], new[1])[0].start()
            for cp in relay_sends:
                cp.wait_send()
        for cp in sends.values():
            cp.wait_send()
        for cp in recvs.values():
            cp.wait_recv()

    if relay:
        return _comm_call(name, body, stacks, order, sems_in=sems, new_sems=(1, 1))
    return _comm_call(name, body, stacks, order, sems_in=sems)[1]


def _allgather_finish_far(name, stacks, sems, order):
    def body(bufs, taken, _):
        send, recv = _far_forward(bufs[0], taken[0], taken[1])
        send.wait_send()
        recv.wait_recv()

    return _comm_call(name, body, stacks, order, sems_in=sems)[1]


def _window_unit(q, j):
    return C2I[WIN_UNIT0[q] + j]


def _pair_copies(g, t, ssem, rsem, gathered):
    x, y, c, _ = _place()
    sib = (x, y, 1 - c)
    cps, whole = [], []
    for a in range(len(g)):
        if a == 0 and gathered:
            for q in range(NCHIP):
                for j in range(WIN_UNITS // 2):
                    u = jnp.where(c == 0, _window_unit(q, WIN_UNITS // 2 + j), _window_unit(q, j))
                    src = g[0].at[pl.ds(pl.multiple_of(u * UNIT, UNIT), UNIT), :]
                    cps.append(_rcopy(src, t[0].at[q, pl.ds(j * UNIT, UNIT), :], ssem.at[0], rsem.at[0], sib))
            whole.append(_rcopy(t[0], t[0], ssem.at[0], rsem.at[0], sib))
        else:
            cp = _rcopy(g[a].at[:, _half(g[a].shape[1], 1 - c), :], t[a], ssem.at[a], rsem.at[a], sib)
            cps.append(cp)
            whole.append(cp)
    return cps, whole


def _comm_multi(name, parts, order):
    def body(buf_refs, taken, new):
        ib = it = inew = 0
        for pbody, pbufs, psems, pnew, _ in parts:
            pbody(buf_refs[ib:ib + len(pbufs)], taken[it:it + len(psems)], new[inew:inew + len(pnew)])
            ib, it, inew = ib + len(pbufs), it + len(psems), inew + len(pnew)

    sems, bufs = _comm_call(name, body, [b for p in parts for b in p[1]], order,
                            sems_in=[s for p in parts for s in p[2]], new_sems=[k for p in parts for k in p[3]])
    out, ib, inew = [], 0, 0
    for _, pbufs, _, pnew, unpack in parts:
        out.append(unpack(sems[inew:inew + len(pnew)], bufs[ib:ib + len(pbufs)]))
        ib, inew = ib + len(pbufs), inew + len(pnew)
    return out


def _pair_start_part(gs, gathered=False):
    n = len(gs)
    ts = [lax.empty((NCHIP, WIN_ROWS // 2, D) if (a == 0 and gathered) else (NCHIP, g.shape[1] // 2, g.shape[2]), f32)
          for a, g in enumerate(gs)]

    def body(bufs, _, new):
        for cp in _pair_copies(bufs[:n], bufs[n:], new[0], new[1], gathered)[0]:
            cp.start()

    return body, list(gs) + ts, (), (n, n), lambda sems, bufs: (sems, bufs)


def _pair_wait_part(bufs, sems, gathered=False):
    n = len(bufs) // 2

    def body(refs, taken, _):
        for cp in _pair_copies(refs[:n], refs[n:], taken[0], taken[1], gathered)[1]:
            cp.wait_send()
            cp.wait_recv()

    return body, list(bufs), list(sems), (), lambda _, out: (out[:n], out[n:])


ROW_TILE = 256


def _pair_add(order, gs, ts, q_arr, c_arr, name):
    n = len(gs)
    hs = [g.shape[1] // 2 for g in gs]
    nblk = max(1, max(hs) // ROW_TILE)
    assert all(h % (16 * nblk) == 0 for h in hs)

    def body(q_ref, c_ref, *refs):
        for a in range(n):
            s = refs[a][...] + refs[n + a][...]
            refs[3 * n + a][...] = s.astype(bf16)

            @pl.when(pl.program_id(1) == q_ref[0])
            def _():
                refs[2 * n + a][...] = s

    def blk(a, half):
        return pl.BlockSpec((None, hs[a] // nblk, gs[a].shape[2]),
                            lambda i, q, q_ref, c_ref: (q, (c_ref[0] * nblk if half else 0) + i, 0))

    out = _call_indexed(
        order, body, (q_arr, c_arr), list(gs) + list(ts), (nblk, NCHIP),
        [blk(a, True) for a in range(n)] + [blk(a, False) for a in range(n)],
        [pl.BlockSpec((hs[a] // nblk, gs[a].shape[2]), lambda i, q, q_ref, c_ref: (i, 0)) for a in range(n)]
        + [blk(a, False) for a in range(n)],
        name=name,
        out_shape=[jax.ShapeDtypeStruct((hs[a], gs[a].shape[2]), f32) for a in range(n)]
        + [jax.ShapeDtypeStruct((NCHIP, hs[a], gs[a].shape[2]), bf16) for a in range(n)],
        compiler_params=_params(("parallel", "arbitrary")),
    )
    return out[:n], out[n:]


def _pair_add_gathered(order, dwt, t, gfa, tfa, q_arr, c_arr, name):
    half_units, half_rows, half_fa = WIN_UNITS // 2, WIN_ROWS // 2, FA_ROWS // 2
    table = jnp.asarray([_window_unit(q, j) for q in range(NCHIP) for j in range(WIN_UNITS)], jnp.int32)

    def body(tab_ref, q_ref, c_ref, g_hbm, t_ref, gfa_ref, tfa_ref, own_ref, p16_ref, ownfa_ref, p16fa_ref, buf, sem):
        q = pl.program_id(0)

        def gather(w, slot):
            cps = []
            for j in range(half_units):
                u = tab_ref[w * WIN_UNITS + c_ref[0] * half_units + j]
                cps.append(pltpu.make_async_copy(g_hbm.at[pl.ds(pl.multiple_of(u * UNIT, UNIT), UNIT), :],
                                                 buf.at[slot, pl.ds(j * UNIT, UNIT), :], sem.at[slot]))
            return cps

        @pl.when(q == 0)
        def _():
            for cp in gather(0, 0):
                cp.start()

        @pl.when(q + 1 < NCHIP)
        def _():
            for cp in gather(q + 1, (q + 1) % 2):
                cp.start()

        slot = q % 2
        pltpu.make_async_copy(buf.at[slot], buf.at[slot], sem.at[slot]).wait()
        s = buf[slot] + t_ref[...]
        p16_ref[...] = s.astype(bf16)
        sfa = gfa_ref[...] + tfa_ref[...]
        p16fa_ref[...] = sfa.astype(bf16)

        @pl.when(q == q_ref[0])
        def _():
            own_ref[...] = s
            ownfa_ref[...] = sfa

    def per_chip(rows):
        return pl.BlockSpec((None, rows, D), lambda q, tab_ref, q_ref, c_ref: (q, 0, 0))

    def own(rows):
        return pl.BlockSpec((rows, D), lambda q, tab_ref, q_ref, c_ref: (0, 0))

    return _call_indexed(
        order, body, (table, q_arr, c_arr), (dwt, t, gfa, tfa), (NCHIP,),
        [pl.BlockSpec(memory_space=pl.ANY), per_chip(half_rows),
         pl.BlockSpec((None, half_fa, D), lambda q, tab_ref, q_ref, c_ref: (q, c_ref[0], 0)), per_chip(half_fa)],
        [own(half_rows), per_chip(half_rows), own(half_fa), per_chip(half_fa)],
        scratch_shapes=[pltpu.VMEM((2, half_rows, D), f32), pltpu.SemaphoreType.DMA((2,))],
        name=name,
        out_shape=[jax.ShapeDtypeStruct((half_rows, D), f32), jax.ShapeDtypeStruct((NCHIP, half_rows, D), bf16),
                   jax.ShapeDtypeStruct((half_fa, D), f32), jax.ShapeDtypeStruct((NCHIP, half_fa, D), bf16)],
        compiler_params=_params(("arbitrary",)),
    )


def _shard_copies(p, r, sm, ssem, rsem):
    x, y, c, chips = _place()
    n = len(p)
    sends, recvs = [], []
    for a in range(n):
        for j, (cx, cy) in enumerate(chips):
            k = a * 3 + j
            sends.append(_rcopy(p[a].at[2 * cx + cy], r[a].at[j], ssem.at[k], rsem.at[k], (cx, cy, c)))
            recvs.append(_rcopy(r[a].at[j], r[a].at[j], ssem.at[k], rsem.at[k], (cx, cy, c)))
    if sm is not None:
        mine = sm.at[4 * x + 2 * y + c]
        for i in range(1, 8):
            px = (1 - x) if i & 4 else x
            py = (1 - y) if i & 2 else y
            pc = (1 - c) if i & 1 else c
            k = 3 * n + i - 1
            sends.append(_rcopy(mine, mine, ssem.at[k], rsem.at[k], (px, py, pc)))
            slot = sm.at[4 * px + 2 * py + pc]
            recvs.append(_rcopy(slot, slot, ssem.at[k], rsem.at[k], (px, py, pc)))
    return sends, recvs


def _shard_start_part(p16s, sm=None):
    n = len(p16s)
    rs = [lax.empty((3,) + p.shape[1:], bf16) for p in p16s]
    extra = [] if sm is None else [sm]
    nsem = 3 * n + (7 if sm is not None else 0)

    def body(bufs, _, new):
        sends, _r = _shard_copies(bufs[:n], bufs[n:2 * n], bufs[2 * n] if extra else None, new[0], new[1])
        for cp in sends:
            cp.start()

    return body, list(p16s) + rs + extra, (), (nsem, nsem), lambda sems, bufs: (sems, bufs)


def _shard_wait_part(bufs, sems, n):
    has_sm = len(bufs) > 2 * n

    def body(refs, taken, _):
        sends, recvs = _shard_copies(refs[:n], refs[n:2 * n], refs[2 * n] if has_sm else None, taken[0], taken[1])
        for cp in sends:
            cp.wait_send()
        for cp in recvs:
            cp.wait_recv()

    return body, list(bufs), list(sems), (), lambda _, out: (out[n:2 * n], (out[2 * n] if has_sm else None))


def _shard_sum(order, owns, rs, c_arr, name):
    n = len(owns)
    hs = [o.shape[0] for o in owns]
    nblk = max(1, max(hs) // ROW_TILE)
    assert all(h % (16 * nblk) == 0 for h in hs)
    trs = [h // nblk for h in hs]

    def body(c_ref, *refs):
        for a in range(n):
            s = refs[a][...]
            for j in range(3):
                s = s + refs[n + a][j].astype(f32)
            refs[2 * n + a][...] = s

    out = _call_indexed(
        order, body, (c_arr,), list(owns) + list(rs), (nblk,),
        [pl.BlockSpec((trs[a], owns[a].shape[1]), lambda i, c_ref: (i, 0)) for a in range(n)]
        + [pl.BlockSpec((3, trs[a], owns[a].shape[1]), lambda i, c_ref: (0, i, 0)) for a in range(n)],
        [pl.BlockSpec((trs[a], owns[a].shape[1]), lambda i, c_ref: (c_ref[0] * nblk + i, 0)) for a in range(n)],
        name=name, out_shape=[jax.ShapeDtypeStruct((2 * hs[a], owns[a].shape[1]), f32) for a in range(n)],
        compiler_params=_params(("parallel",)),
    )
    return list(out)


def _swap_copies(full, ssem, rsem):
    x, y, c, _ = _place()
    sends, recvs = [], []
    for a in range(len(full)):
        mine = full[a].at[_half(full[a].shape[0], c)]
        sends.append(_rcopy(mine, mine, ssem.at[a], rsem.at[a], (x, y, 1 - c)))
        other = full[a].at[_half(full[a].shape[0], 1 - c)]
        recvs.append(_rcopy(other, other, ssem.at[a], rsem.at[a], (x, y, 1 - c)))
    return sends, recvs


def _swap_start_part(fulls):
    n = len(fulls)

    def body(bufs, _, new):
        for cp in _swap_copies(bufs, new[0], new[1])[0]:
            cp.start()

    return body, list(fulls), (), (n, n), lambda sems, bufs: (sems, bufs)


def _swap_wait_part(fulls, sems):
    def body(refs, taken, _):
        sends, recvs = _swap_copies(refs, taken[0], taken[1])
        for cp in sends:
            cp.wait_send()
        for cp in recvs:
            cp.wait_recv()

    return body, list(fulls), list(sems), (), lambda _, out: out


def _small_finish(order, sm, ws, ms, vs):
    n = len(ws)

    def body(sm_ref, *refs):
        s = sm_ref[0]
        for d in range(1, 8):
            s = s + sm_ref[d]
        loss_ref, g_refs, upd_refs = refs[3 * n], refs[3 * n + 1:4 * n + 1], refs[4 * n + 1:]
        loss_ref[...] = s[n:n + 1, 0:1]
        for i in range(n):
            g = s[i:i + 1, 0:ws[i].shape[1]]
            g_refs[i][...] = g
            res = _adamw_math(refs[i][...], g, refs[n + i][...], refs[2 * n + i][...])
            for k in range(3):
                upd_refs[3 * i + k][...] = res[k]

    out = _call(order, body, [sm] + list(ws) + list(ms) + list(vs), name="small_sum_adamw",
                out_shape=[jax.ShapeDtypeStruct((1, 1), f32)] + [jax.ShapeDtypeStruct(w.shape, f32) for w in ws]
                + [jax.ShapeDtypeStruct(w.shape, f32) for w in ws for _ in range(3)])
    return out[0], out[1:n + 1], [out[n + 1 + 3 * i:n + 4 + 3 * i] for i in range(n)]


def _adamw_math(w, g, m, v):
    m = ADAM_B1 * m + (1.0 - ADAM_B1) * g
    v = ADAM_B2 * v + (1.0 - ADAM_B2) * (g * g)
    m_hat = m / (1.0 - ADAM_B1 ** ADAM_STEP)
    v_hat = v / (1.0 - ADAM_B2 ** ADAM_STEP)
    return -ADAM_LR * (m_hat / (jnp.sqrt(v_hat) + ADAM_EPS) + ADAM_WD * w), m, v


def _adamw(order, ws, gs, ms, vs, name):
    n = len(ws)
    nblk = max(1, max(w.shape[0] for w in ws) // ROW_TILE)
    assert all(w.shape[0] % (8 * nblk) == 0 for w in ws)

    def body(*refs):
        for a in range(n):
            w_ref, g_ref, m_ref, v_ref = (refs[k * n + a] for k in range(4))
            d_ref, nm_ref, nv_ref, g_out = refs[4 * n + 4 * a:4 * n + 4 * a + 4]
            g = g_ref[...]
            g_out[...] = g
            d_ref[...], nm_ref[...], nv_ref[...] = _adamw_math(w_ref[...], g, m_ref[...], v_ref[...])

    blks = [pl.BlockSpec((w.shape[0] // nblk, w.shape[1]), lambda i: (i, 0)) for w in ws]
    out = _call(
        order, body, list(ws) + list(gs) + list(ms) + list(vs), name=name, grid=(nblk,), in_specs=blks * 4,
        out_specs=[b for b in blks for _ in range(4)],
        out_shape=[jax.ShapeDtypeStruct(w.shape, f32) for w in ws for _ in range(4)],
        compiler_params=_params(("parallel",)),
    )
    return [out[4 * a:4 * a + 4] for a in range(n)]


def _feature_rows(w):
    return jnp.transpose(w, (2, 0, 1))


WIN_STEP = 128
WIN_PIECE = 2 * WIN_STEP


def _window_stacks(order, w, q_arr):
    steps = WIN_ROWS // WIN_STEP
    n_piece = (WIN_ROWS - 2 * WIN_STEP) // WIN_PIECE
    assert n_piece * WIN_PIECE == WIN_ROWS - 2 * WIN_STEP and WIN_STEP % 16 == 0
    assert max(OWN_ROW0) < UNIT <= WIN_STEP and OWN_ROW0[1] + FA_AT == UNIT and FA_AT + N_FA + UNIT <= SHARD_IN
    pad = -(-(WIN_ROWS - SHARD_IN + N_FA) // 8) * 8
    lead = pad - (WIN_ROWS - SHARD_IN)
    assert lead + OWN_ROW0[1] - N_FA >= 0 and lead + max(OWN_ROW0) <= pad and max(OWN_ROW0) <= WIN_ROWS - SHARD_IN

    def body(q_ref, w_ref, win_ref, fa_ref, first, last, til, fabuf, sem):
        i = pl.program_id(0)
        q = q_ref[0]
        chip1 = q == 1
        row0 = jnp.where(q == 0, OWN_ROW0[0], jnp.where(chip1, OWN_ROW0[1], jnp.where(q == 2, OWN_ROW0[2], OWN_ROW0[3])))
        skip = jnp.where(chip1, N_FA, 0)

        def rows(dst, src0, dst0, n, slot):
            return pltpu.make_async_copy(w_ref.at[pl.ds(src0, n)], dst.at[pl.ds(dst0, n)], sem.at[slot])

        def first_copies(on_chip1):
            if on_chip1:
                return [rows(first, 0, OWN_ROW0[1], FA_AT, 0), rows(first, FA_AT + N_FA, UNIT, UNIT, 1)]
            return [rows(first, 0, row0, WIN_STEP, 0)]

        def first_do(act):
            for on_chip1 in (False, True):
                @pl.when(chip1 if on_chip1 else jnp.logical_not(chip1))
                def _():
                    for c in first_copies(on_chip1):
                        act(c)

        def piece(j):
            dst0 = WIN_STEP + j * WIN_PIECE
            return pltpu.make_async_copy(w_ref.at[pl.ds(dst0 - row0 + skip, WIN_PIECE), 0],
                                         til.at[pl.ds(dst0, WIN_PIECE)], sem.at[2 + j])

        last_copy = rows(last, SHARD_IN - WIN_STEP, lead + row0 - skip, WIN_STEP, 2 + n_piece)
        fa_copy = rows(fabuf, FA_AT, 0, N_FA, 3 + n_piece)

        @pl.when(i == 0)
        def _():
            first[pl.ds(0, UNIT)] = jnp.zeros((UNIT, 1, D), f32)
            last[...] = jnp.zeros(last.shape, f32)
            fabuf[pl.ds(N_FA, FA_ROWS - N_FA)] = jnp.zeros((FA_ROWS - N_FA, 1, D), f32)
            fa_copy.start()
            first_do(lambda c: c.start())
            for j in range(n_piece):
                piece(j).start()
            last_copy.start()
            fa_copy.wait()
            fa_ref[...] = fabuf[...].reshape(FA_ROWS, D).astype(bf16)
            first_do(lambda c: c.wait())
            win_ref[...] = first[pl.ds(0, WIN_STEP)].reshape(WIN_STEP, D).astype(bf16)

        for j in range(n_piece):
            @pl.when(i == 1 + j * (WIN_PIECE // WIN_STEP))
            def _():
                piece(j).wait()

        @pl.when(jnp.logical_and(i > 0, i < steps - 1))
        def _():
            win_ref[...] = til[pl.ds(pl.multiple_of(i * WIN_STEP, WIN_STEP), WIN_STEP)].astype(bf16)

        @pl.when(i == steps - 1)
        def _():
            last_copy.wait()
            win_ref[...] = last[pl.ds(pad, WIN_STEP)].reshape(WIN_STEP, D).astype(bf16)

    return _call_indexed(
        order, body, (q_arr,), (w,), (steps,), [pl.BlockSpec(memory_space=pl.ANY)],
        [pl.BlockSpec((None, WIN_STEP, D), lambda i, q: (q[0], i, 0)),
         pl.BlockSpec((None, FA_ROWS, D), lambda i, q: (q[0], 0, 0))],
        scratch_shapes=[pltpu.VMEM((WIN_STEP + UNIT, 1, D), f32), pltpu.VMEM((pad + WIN_STEP, 1, D), f32),
                        pltpu.VMEM((WIN_ROWS, D), f32), pltpu.VMEM((FA_ROWS, 1, D), f32),
                        pltpu.SemaphoreType.DMA((4 + n_piece,))],
        name="window_w_in", out_shape=[jax.ShapeDtypeStruct((NCHIP, WIN_ROWS, D), bf16),
                                       jax.ShapeDtypeStruct((NCHIP, FA_ROWS, D), bf16)],
        compiler_params=_params(("arbitrary",)),
    )


def _unfeature_rows(a):
    return jnp.transpose(a, (1, 2, 0))


ADAM_IN_ROWS = 134
ADAM_IN_STEPS = SHARD_IN // ADAM_IN_ROWS
ADAM_IN_CHUNK = 136
ADAM_IN_CHUNKS = ADAM_IN_STEPS + 1
ADAM_IN_BUF = WIN_ROWS + N_FA


def _adamw_w_in(order, w, gwin, gfa, m, v, q_arr):
    assert ADAM_IN_CHUNK * ADAM_IN_STEPS < WIN_ROWS <= ADAM_IN_CHUNK * ADAM_IN_CHUNKS
    assert OWN_ROW0[NCHIP - 1] + ADAM_IN_ROWS <= 2 * ADAM_IN_CHUNK and ADAM_IN_CHUNK >= ADAM_IN_ROWS
    last0 = ADAM_IN_CHUNK * ADAM_IN_STEPS
    cut = OWN_ROW0[1] + FA_AT

    def body(q_ref, w_ref, gwin_ref, gfa_ref, m_ref, v_ref, go_ref, d_ref, nm_ref, nv_ref, buf, sem):
        i = pl.program_id(0)
        q = q_ref[0]
        chip1 = q == 1
        shift = jnp.where(chip1, N_FA, 0)

        def copy(src_ref, src0, dst0, n, slot):
            return pltpu.make_async_copy(src_ref.at[pl.ds(src0, n)], buf.at[pl.ds(dst0, n), 0], sem.at[slot])

        def first(on_chip1):
            if on_chip1:
                return [copy(gwin_ref, 0, 0, cut, 0), copy(gfa_ref, 0, cut, N_FA, ADAM_IN_CHUNKS),
                        copy(gwin_ref, cut, cut + N_FA, ADAM_IN_CHUNK - cut - N_FA, ADAM_IN_CHUNKS + 1)]
            return [copy(gwin_ref, 0, 0, ADAM_IN_CHUNK, 0)]

        def middle(k):
            return [copy(gwin_ref, pl.multiple_of(k * ADAM_IN_CHUNK - shift, 8), k * ADAM_IN_CHUNK, ADAM_IN_CHUNK, k)]

        def last(on_chip1):
            n = WIN_ROWS - last0 + (N_FA if on_chip1 else 0)
            return [copy(gwin_ref, WIN_ROWS - n, last0, n, ADAM_IN_STEPS)]

        def both(make, act):
            for on_chip1 in (False, True):
                @pl.when(chip1 if on_chip1 else jnp.logical_not(chip1))
                def _():
                    for c in make(on_chip1):
                        act(c)

        @pl.when(i == 0)
        def _():
            both(first, lambda c: c.start())
            for k in range(1, ADAM_IN_STEPS):
                middle(k)[0].start()
            both(last, lambda c: c.start())
            both(first, lambda c: c.wait())

        @pl.when(i < ADAM_IN_STEPS - 1)
        def _():
            middle(i + 1)[0].wait()

        @pl.when(i == ADAM_IN_STEPS - 1)
        def _():
            both(last, lambda c: c.wait())

        row0 = jnp.where(q == 0, OWN_ROW0[0], jnp.where(chip1, OWN_ROW0[1], jnp.where(q == 2, OWN_ROW0[2], OWN_ROW0[3])))
        g = buf[pl.ds(row0 + i * ADAM_IN_ROWS, ADAM_IN_ROWS)]
        go_ref[...] = g
        d_ref[...], nm_ref[...], nv_ref[...] = _adamw_math(w_ref[...], g, m_ref[...], v_ref[...])

    blk = pl.BlockSpec((ADAM_IN_ROWS, 1, D), lambda i, q: (i, 0, 0))
    hbm = pl.BlockSpec(memory_space=pl.ANY)
    return _call_indexed(
        order, body, (q_arr,), (w, gwin, gfa, m, v), (ADAM_IN_STEPS,), [blk, hbm, hbm, blk, blk], [blk] * 4,
        scratch_shapes=[pltpu.VMEM((ADAM_IN_BUF, 1, D), f32), pltpu.SemaphoreType.DMA((ADAM_IN_CHUNKS + 2,))],
        name="adamw_w_in", out_shape=[jax.ShapeDtypeStruct((SHARD_IN, 1, D), f32)] * 4,
        compiler_params=_params(("arbitrary",)),
    )


def kernel(x, norm_attn_g, w_in, b_forget, w_branch_a, w_branch_b, w_out, norm_mlp_g, w_up, w_down, norm_final_g, loss_target, m_norm_attn_g, m_w_in, m_b_forget, m_w_branch_a, m_w_branch_b, m_w_out, m_norm_mlp_g, m_w_up, m_w_down, m_norm_final_g, v_norm_attn_g, v_w_in, v_b_forget, v_w_branch_a, v_w_branch_b, v_w_out, v_norm_mlp_g, v_w_up, v_w_down, v_norm_final_g):
    xi, yi, ci = lax.axis_index("x"), lax.axis_index("y"), lax.axis_index("c")
    q_me = 2 * xi + yi
    c_arr = jnp.reshape(ci, (1,)).astype(jnp.int32)
    q_arr = jnp.reshape(q_me, (1,)).astype(jnp.int32)
    x_, tgt = x[0], loss_target[0]

    names = ["w_branch_a", "w_branch_b", "w_out", "w_up", "w_down"]
    big = dict(zip(names, [w_branch_a[0], w_branch_b[0], w_out[0], w_up[0], w_down[0]]))
    ms = dict(zip(names, [m_w_branch_a[0], m_w_branch_b[0], m_w_out[0], m_w_up[0], m_w_down[0]]))
    vs = dict(zip(names, [v_w_branch_a[0], v_w_branch_b[0], v_w_out[0], v_w_up[0], v_w_down[0]]))
    grad, upd = {}, {}
    order = _Order()

    def run(fn, *args, **kw):
        return fn(order, *args, **kw)

    def own_slot(a):
        return lax.dynamic_update_slice(lax.empty((NCHIP,) + a.shape, a.dtype), a[None], (q_me, 0, 0))

    sem_in, in_s = _allgather_start("allgather_start_in", run(_window_stacks, _feature_rows(w_in), q_arr), order,
                                    relay=True)
    rope = _rope_tables(order.tok[0, 0])
    sem_f, in_s = _allgather_forward("allgather_forward_in", in_s, sem_in, order, behind=rope, relay=True)
    sem_rest, rest = _allgather_start("allgather_start_rest", [own_slot(w.astype(bf16)) for w in big.values()], order)
    sem_f, in_s = _allgather_finish("allgather_finish_in", in_s, sem_f, order, relay=True)
    wins, fas = _allgather_finish_far("allgather_finish_far_in", in_s, sem_f, order)
    wt = run(_assemble_win, wins, fas)

    bpad = jnp.pad(b_forget, ((0, 0), (0, 120)))
    h1, qkvb, qkva, gates, fa = run(_norm_inproj, x_, norm_attn_g, wt, rope)
    F = run(_forget_cumsum, fa, bpad)
    oa, lsea = run(_fox_fwd, qkva, F)
    sem_f, rest = _allgather_forward("allgather_forward_rest", rest, sem_rest, order)
    ob, lseb = run(_dil_fwd, qkvb)
    was, wbs, wouts, wups, wdowns = _allgather_finish("allgather_finish_rest", rest, sem_f, order)
    wout = wouts.reshape(D, D)
    wdown = wdowns.reshape(DFF, D)
    ya, yb, mixed = run(_branch_mix, oa, ob, was, wbs, gates)
    x2, h2 = run(_outproj_norm, mixed, wout, x_, norm_mlp_g)
    u, a = run(_mlp_up, h2, wups)
    dx3, dx3b, dg3, loss_part = run(_mlp_down_loss, a, wdown, x2, norm_final_g.reshape(1, D), tgt)

    def comm(name, *parts):
        return _comm_multi(name, list(parts), order)

    def adamw_group(group, fulls, name):
        res = run(_adamw, [big[nm] for nm in group], fulls, [ms[nm] for nm in group], [vs[nm] for nm in group], name)
        for nm, r in zip(group, res):
            *upd[nm], grad[nm] = r

    grp_a, grp_b, grp_c = ["w_down", "w_up"], ["w_out", "w_branch_a", "w_branch_b"], ["w_in", "w_in_fa"]
    du = run(_mlp_down_bwd, dx3b, wdown, u)
    dwdown = run(_mm, a, dx3b, "tn", f32, 1024, D, "wgrad_down")
    dwup = run(_mm, h2, du, "tn", f32, D, 1024, "wgrad_up", stack_cols=True)
    ((sem_pa, buf_pa),) = comm("pair_start_a", _pair_start_part([dwdown.reshape(NCHIP, DFF // NCHIP, D), dwup]))
    dx2, dx2b, dg2 = run(_mlp_up_bwd, du, wups, x2, dx3, norm_mlp_g)
    ((gs, ts),) = comm("pair_wait_a", _pair_wait_part(buf_pa, sem_pa))
    p32_a, p16_a = run(_pair_add, gs, ts, q_arr, c_arr, "pair_add_a")
    ((sem_sa, buf_sa),) = comm("shard_start_a", _shard_start_part(p16_a))
    dya, dyb, dproj = run(_gate_bwd, dx2b, wout, gates, ya, yb)
    dwout = run(_mm, mixed, dx2b, "tn", f32, D, D, "wgrad_out")
    doa, dob = run(_branch_bwd, dya, dyb, was, wbs)
    dwas, dwbs = run(_branch_wgrad, oa, ob, dya, dyb)
    ((sem_pb, buf_pb),) = comm("pair_start_b", _pair_start_part([dwout.reshape(NCHIP, D // NCHIP, D), dwas, dwbs]))
    dF, dproj = run(_fox_bwd, qkva, doa, oa, lsea, F, dproj)
    (gs, ts), (rs_a, _) = comm("pair_wait_b_shard_wait_a", _pair_wait_part(buf_pb, sem_pb),
                               _shard_wait_part(buf_sa, sem_sa, len(grp_a)))
    p32_b, p16_b = run(_pair_add, gs, ts, q_arr, c_arr, "pair_add_b")
    fulls_a = run(_shard_sum, p32_a, rs_a, c_arr, "shard_sum_a")
    (sem_wa, fulls_a), (sem_sb, buf_sb) = comm("swap_start_a_shard_start_b", _swap_start_part(fulls_a),
                                               _shard_start_part(p16_b))
    dbf, dproj = run(_forget_bwd, dF, fa, bpad, dproj)
    dproj = run(_dil_bwd, qkvb, dob, ob, lseb, rope, dproj)
    (rs_b, _), fulls_a = comm("shard_wait_b_swap_wait_a", _shard_wait_part(buf_sb, sem_sb, len(grp_b)),
                              _swap_wait_part(fulls_a, sem_wa))
    fulls_b = run(_shard_sum, p32_b, rs_b, c_arr, "shard_sum_b")
    ((sem_wb, fulls_b),) = comm("swap_start_b", _swap_start_part(fulls_b))
    dwt = run(_mm, dproj, h1, "tn", f32, 512, D, "wgrad_in")
    dwfa = jnp.broadcast_to(dwt[F_FA:F_FA + FA_ROWS][None], (NCHIP, FA_ROWS, D))
    (sem_pc, buf_pc), fulls_b = comm("pair_start_c_swap_wait_b", _pair_start_part([dwt, dwfa], gathered=True),
                                     _swap_wait_part(fulls_b, sem_wb))
    adamw_group(grp_b, fulls_b, "adamw_b")
    (((dwt_c, dwfa_c), (t_in, t_fa)),) = comm("pair_wait_c", _pair_wait_part(buf_pc, sem_pc, gathered=True))
    p32_in, p16_in, p32_fa, p16_fa = run(_pair_add_gathered, dwt_c, t_in, dwfa_c, t_fa, q_arr, c_arr, "pair_add_w_in")
    ((sem_sc, buf_sc),) = comm("shard_start_c", _shard_start_part([p16_in, p16_fa]))
    gx, dg1 = run(_inproj_bwd, dproj, wt, x_, dx2, norm_attn_g)
    adamw_group(grp_a, fulls_a, "adamw_a")
    small = jnp.concatenate([dg1, dg2, dg3, jnp.pad(dbf[:, 0:8], ((0, 0), (0, D - 8))),
                             jnp.pad(loss_part, ((0, 0), (0, D - 128))),
                             jnp.zeros((SMALL_ROWS - 5, D), f32)], axis=0)
    sm = lax.dynamic_update_slice(lax.empty((8, SMALL_ROWS, D), f32), small[None],
                                  (4 * xi + 2 * yi + ci, 0, 0))
    (sem_sm, buf_sm), (rs_c, _) = comm("small_start_shard_wait_c", _shard_start_part([], sm),
                                       _shard_wait_part(buf_sc, sem_sc, len(grp_c)))
    fulls_c = (run(_shard_sum, [p32_in], rs_c[0:1], c_arr, "shard_sum_w_in")
               + run(_shard_sum, [p32_fa], rs_c[1:2], c_arr, "shard_sum_w_in_fa"))
    (sem_wc, fulls_c), (_, sm) = comm("swap_start_c_small_wait", _swap_start_part(fulls_c),
                                      _shard_wait_part(buf_sm, sem_sm, 0))
    smalls = ["norm_attn_g", "norm_mlp_g", "norm_final_g", "b_forget"]
    loss, gs, res = run(_small_finish, sm, [norm_attn_g, norm_mlp_g, norm_final_g.reshape(1, D), b_forget],
                        [m_norm_attn_g, m_norm_mlp_g, m_norm_final_g.reshape(1, D), m_b_forget],
                        [v_norm_attn_g, v_norm_mlp_g, v_norm_final_g.reshape(1, D), v_b_forget])
    loss = loss.reshape(())
    grad.update(zip(smalls, gs))
    upd.update(zip(smalls, res))

    ((gwin, gfa),) = comm("swap_wait_c", _swap_wait_part(fulls_c, sem_wc))
    res_in = run(_adamw_w_in, _feature_rows(w_in), gwin, gfa, _feature_rows(m_w_in), _feature_rows(v_w_in), q_arr)
    grad["w_in"] = _unfeature_rows(res_in[0])
    upd["w_in"] = [_unfeature_rows(t) for t in res_in[1:]]

    order_out = ["norm_attn_g", "w_in", "b_forget", "w_branch_a", "w_branch_b", "w_out", "norm_mlp_g", "w_up",
                 "w_down", "norm_final_g"]
    shapes = dict(norm_attn_g=norm_attn_g.shape, w_in=w_in.shape, b_forget=b_forget.shape,
                  w_branch_a=w_branch_a.shape, w_branch_b=w_branch_b.shape, w_out=w_out.shape,
                  norm_mlp_g=norm_mlp_g.shape, w_up=w_up.shape, w_down=w_down.shape, norm_final_g=norm_final_g.shape)
    outs = [loss, gx.reshape(x.shape)]
    outs += [grad[nm].reshape(shapes[nm]) for nm in order_out]
    for k in range(3):
        outs += [upd[nm][k].reshape(shapes[nm]) for nm in order_out]
    return tuple(outs)
```

```python
import jax
import jax.numpy as jnp
from jax import lax
from jax.experimental import pallas as pl
from jax.experimental.pallas import tpu as pltpu

f32 = jnp.float32
bf16 = jnp.bfloat16

S = 2048
D = 1024
DFF = 4096
HD = 64
FOXW = 512
DILOUT = 256
DIL = (1, 4, 16)
BAND = 128
EPS = 1e-6
NEG = -1e30
ROPE_THETA = 500000.0
NCHIP = 4
TQ = 256

ADAM_LR, ADAM_B1, ADAM_B2, ADAM_EPS, ADAM_WD, ADAM_STEP = 0.001, 0.9, 0.999, 1e-08, 0.01, 10
VMEM_LIMIT = 56 * 1024 * 1024

UNIT = 64
NP = 6144
F_DIL, F_FOX, F_FA, F_G = 0, 2304, 3840, 4096
DIL_BLK, FOX_BLK = 1152, 384
WIN_UNITS, WIN_ROWS = 24, 1536
WIN_UNIT0 = (0, 23, 45, 68)
OWN_ROW0 = (0, 2, 60, 62)
SHARD_IN = 1474
N_FA = 8
FA_AT = 1536 - SHARD_IN
FA_ROWS = 32


def _compact_to_internal():
    c2i = {}
    for p in range(2):
        for role in range(3):
            for g in range(3):
                for hh in range(2):
                    c2i[24 + 12 * role + 4 * g + 2 * p + hh] = 18 * p + 6 * role + 2 * g + hh
    for p in range(4):
        for role in range(3):
            for hh in range(2):
                c2i[8 * role + 2 * p + hh] = F_FOX // UNIT + 6 * p + 2 * role + hh
    for j in range(32):
        c2i[60 + j] = F_G // UNIT + j
    return c2i


C2I = _compact_to_internal()
OVERLAP_UNITS = (23, 45, 46, 68)


def _params(sem=None):
    return pltpu.CompilerParams(dimension_semantics=sem, vmem_limit_bytes=VMEM_LIMIT)


class _Order:
    def __init__(self):
        self.tok = None

    def mark(self, v):
        self.tok = v

    def token_for(self, args):
        return [] if self.tok is None or any(self.tok is a for a in args) else [self.tok]


def _call(order, body, args, in_specs=None, **kw):
    args = list(args)
    n_in = len(args)
    if in_specs is None:
        in_specs = [pl.BlockSpec(memory_space=pltpu.VMEM)] * n_in
    kern = body
    extra = order.token_for(args)
    if extra:
        in_specs = list(in_specs) + [pl.BlockSpec(memory_space=pl.ANY)]

        def kern(*refs):
            body(*refs[:n_in], *refs[n_in + 1:])

    out = pl.pallas_call(kern, in_specs=in_specs, **kw)(*args, *extra)
    order.mark(out[0] if isinstance(out, (tuple, list)) else out)
    return out


def _call_indexed(order, body, scalars, args, grid, in_specs, out_specs, scratch_shapes=(), **kw):
    args, in_specs = list(args), list(in_specs)
    n_front = len(scalars) + len(args)
    kern = body
    extra = order.token_for(args)
    if extra:
        in_specs.append(pl.BlockSpec(memory_space=pl.ANY))

        def kern(*refs):
            body(*refs[:n_front], *refs[n_front + 1:])

    out = pl.pallas_call(
        kern, grid_spec=pltpu.PrefetchScalarGridSpec(num_scalar_prefetch=len(scalars), grid=grid, in_specs=in_specs,
                                                     out_specs=out_specs, scratch_shapes=scratch_shapes),
        **kw)(*scalars, *args, *extra)
    order.mark(out[0] if isinstance(out, (tuple, list)) else out)
    return out


def _dot(a, b):
    return jnp.dot(a, b, preferred_element_type=f32)


def _dot_nt(a, b):
    return lax.dot_general(a, b, (((1,), (1,)), ((), ())), preferred_element_type=f32)


def _dot_tn(a, b):
    return lax.dot_general(a, b, (((0,), (0,)), ((), ())), preferred_element_type=f32)


def _split3(x):
    hi = x.astype(bf16)
    r1 = x - hi.astype(f32)
    mid = r1.astype(bf16)
    lo = (r1 - mid.astype(f32)).astype(bf16)
    return hi, mid, lo


def _rope_tables(after):
    half = 8
    inv_freq = jnp.power(jnp.float32(ROPE_THETA), -jnp.arange(half, dtype=f32) * 2.0 / 16)
    ang = (jnp.arange(S).astype(f32) + after)[:, None] * inv_freq[None, :]
    cos, sin = jnp.cos(ang), jnp.sin(ang)
    one = jnp.ones((S, HD - 16), f32)
    zero = jnp.zeros((S, HD - 16), f32)
    z8 = jnp.zeros((S, 8), f32)
    c = jnp.concatenate([cos, cos, one], axis=1)
    s1 = jnp.concatenate([-sin, z8, zero], axis=1)
    s2 = jnp.concatenate([z8, sin, zero], axis=1)
    return tuple(jnp.concatenate([t, t], axis=1) for t in (c, s1, s2))


def _mm(order, a, b, mode, out_dtype, tm, tn, name, stack_cols=False):
    if mode == "nn":
        (M, K), (_, N) = a.shape, b.shape
        a_spec = pl.BlockSpec((tm, K), lambda i, j: (i, 0))
        b_spec = pl.BlockSpec((K, tn), lambda i, j: (0, j))
        dot = _dot
    elif mode == "nt":
        (M, K), (N, _) = a.shape, b.shape
        a_spec = pl.BlockSpec((tm, K), lambda i, j: (i, 0))
        b_spec = pl.BlockSpec((tn, K), lambda i, j: (j, 0))
        dot = _dot_nt
    else:
        (K, M), (_, N) = a.shape, b.shape
        a_spec = pl.BlockSpec((K, tm), lambda i, j: (0, i))
        b_spec = pl.BlockSpec((K, tn), lambda i, j: (0, j))
        dot = _dot_tn

    def body(a_ref, b_ref, o_ref):
        o_ref[...] = dot(a_ref[...], b_ref[...]).astype(out_dtype)

    if stack_cols:
        assert tm == M
        out_spec = pl.BlockSpec((None, tm, tn), lambda i, j: (j, 0, 0))
        out_shape = jax.ShapeDtypeStruct((N // tn, M, tn), out_dtype)
    else:
        out_spec = pl.BlockSpec((tm, tn), lambda i, j: (i, j))
        out_shape = jax.ShapeDtypeStruct((M, N), out_dtype)
    return _call(
        order, body, (a, b), name=name, grid=(M // tm, N // tn), in_specs=[a_spec, b_spec],
        out_specs=out_spec, out_shape=out_shape,
        compiler_params=_params(("parallel", "parallel")),
    )


def _assemble_win(order, wins, fas):
    def body(win_ref, fa_ref, o_ref):
        q = pl.program_id(0)

        @pl.when(q == 0)
        def _():
            o_ref[...] = jnp.zeros_like(o_ref)

        for k in range(NCHIP):
            @pl.when(q == k)
            def _(k=k):
                for j in range(WIN_UNITS):
                    cu = WIN_UNIT0[k] + j
                    dst = pl.ds(C2I[cu] * UNIT, UNIT)
                    if cu in OVERLAP_UNITS:
                        o_ref[dst, :] += win_ref[j * UNIT:(j + 1) * UNIT, :]
                    else:
                        o_ref[dst, :] = win_ref[j * UNIT:(j + 1) * UNIT, :]
                if k == 1:
                    o_ref[F_FA:F_FA + FA_ROWS, :] = fa_ref[...]

    return _call(
        order, body, (wins, fas), name="assemble_w_in", grid=(NCHIP,),
        in_specs=[pl.BlockSpec((None, WIN_ROWS, D), lambda q: (q, 0, 0)),
                  pl.BlockSpec((None, FA_ROWS, D), lambda q: (1, 0, 0))],
        out_specs=pl.BlockSpec((NP, D), lambda q: (0, 0)),
        out_shape=jax.ShapeDtypeStruct((NP, D), bf16),
        compiler_params=_params(("arbitrary",)),
    )


def _norm_inproj(order, x, g1, wt, rope):
    tm = 256
    c_t, s1_t, s2_t = rope

    def body(x_ref, g_ref, w_ref, c_ref, s1_ref, s2_ref, h_ref, qkvb_ref, qkva_ref, gates_ref, fa_ref):
        xb = x_ref[...]
        r = lax.rsqrt(jnp.mean(xb * xb, axis=-1, keepdims=True) + EPS)
        h = ((xb * r) * g_ref[...]).astype(bf16)
        h_ref[...] = h
        c, s1, s2 = c_ref[...], s1_ref[...], s2_ref[...]
        for p in range(2):
            pb = _dot_nt(h, w_ref[F_DIL + p * DIL_BLK:F_DIL + (p + 1) * DIL_BLK, :])
            for ch in range(DIL_BLK // 128):
                pc = pb[:, ch * 128:(ch + 1) * 128]
                if ch < 6:
                    pc = pc * c + pltpu.roll(pc, 120, 1) * s1 + pltpu.roll(pc, 8, 1) * s2
                qkvb_ref[:, p * DIL_BLK + ch * 128:p * DIL_BLK + (ch + 1) * 128] = pc
        qkva_ref[...] = _dot_nt(h, w_ref[F_FOX:F_FA, :]).astype(bf16)
        fa_ref[...] = _dot_nt(h, w_ref[F_FA:F_FA + 128, :])
        gates_ref[...] = _dot_nt(h, w_ref[F_G:NP, :]).astype(bf16)

    row = lambda w: pl.BlockSpec((tm, w), lambda i: (i, 0))
    return _call(
        order, body, (x, g1, wt, c_t, s1_t, s2_t), name="norm_inproj", grid=(S // tm,),
        in_specs=[row(D), pl.BlockSpec((1, D), lambda i: (0, 0)), pl.BlockSpec((NP, D), lambda i: (0, 0)),
                  row(128), row(128), row(128)],
        out_specs=[row(D), row(2 * DIL_BLK), row(4 * FOX_BLK), row(2 * D), row(128)],
        out_shape=[jax.ShapeDtypeStruct((S, D), bf16), jax.ShapeDtypeStruct((S, 2 * DIL_BLK), f32),
                   jax.ShapeDtypeStruct((S, 4 * FOX_BLK), bf16), jax.ShapeDtypeStruct((S, 2 * D), bf16),
                   jax.ShapeDtypeStruct((S, 128), f32)],
        compiler_params=_params(("parallel",)),
    )


def _forget_cumsum(order, fa, bpad):
    nb = S // TQ

    def body(fa_ref, b_ref, F_ref):
        rr = lax.broadcasted_iota(jnp.int32, (TQ, TQ), 0)
        cc = lax.broadcasted_iota(jnp.int32, (TQ, TQ), 1)
        tri = (rr >= cc).astype(bf16)
        lane = lax.broadcasted_iota(jnp.int32, (1, 128), 1)
        carry = jnp.zeros((1, 128), f32)
        for b in range(nb):
            z = fa_ref[b * TQ:(b + 1) * TQ, :] + b_ref[...]
            lf = jnp.minimum(z, 0.0) - jnp.log(1.0 + jnp.exp(-jnp.abs(z)))
            lf = jnp.where(lane < 8, lf, 0.0)
            hi, mid, lo = _split3(lf)
            fb = (_dot(tri, hi) + _dot(tri, mid)) + _dot(tri, lo) + carry
            F_ref[b * TQ:(b + 1) * TQ, :] = fb
            carry = fb[TQ - 1:TQ, :]

    return _call(
        order, body, (fa, bpad), name="forget_cumsum",
        out_shape=jax.ShapeDtypeStruct((S, 128), f32),
        compiler_params=_params(),
    )


def _head_masks():
    lane = lax.broadcasted_iota(jnp.int32, (1, 128), 1)
    return lane, (lane < HD, lane >= HD)


L_ONE = 3
FOX_TQ, FOX_TK = 256, 512


def _set_lanes(x, lane, first, cols):
    for n, col in enumerate(cols):
        x = jnp.where(lane == first + n, col, x)
    return x


def _f32_parts(col):
    return [t.astype(f32) for t in _split3(col)]


def _fox_operands(qkv_ref, F_ref, lse_ref, qa, ka, p, rows):
    lane, hm = _head_masks()
    q = qkv_ref[rows, 0:128].astype(f32) * 0.125
    k = qkv_ref[rows, 128:256].astype(f32)
    Fb = F_ref[rows, :]
    for hh in (0, 1):
        free = (1 - hh) * HD
        fcol = jnp.sum(jnp.where(lane == 2 * p + hh, Fb, 0.0), axis=1, keepdims=True)
        qterm = fcol if lse_ref is None else fcol - lse_ref[rows, hh * HD:hh * HD + 1]
        qcols = _f32_parts(qterm) + [1.0] * 3
        kcols = [1.0] * 3 + [-t for t in _f32_parts(fcol)]
        qa[hh, rows, :] = _set_lanes(jnp.where(hm[hh], q, 0.0), lane, free, qcols).astype(bf16)
        ka[hh, rows, :] = _set_lanes(k, lane, free, kcols).astype(bf16)


def _fox_fwd(order, qkva, F):
    tq, tk = FOX_TQ, FOX_TK

    def body(qkv_ref, F_ref, o_ref, lse_ref, qa, ka, vt):
        p = pl.program_id(0)
        keyi = lax.broadcasted_iota(jnp.int32, (tk, 1), 0)
        qryi = lax.broadcasted_iota(jnp.int32, (1, tq), 1)
        sub = lax.broadcasted_iota(jnp.int32, (128, 1), 0)

        def prep(i, c):
            rows = pl.ds(pl.multiple_of(i * tk, tk), tk)
            _fox_operands(qkv_ref, F_ref, None, qa, ka, p, rows)
            vt[i] = qkv_ref[rows, 256:384].astype(f32).T.astype(bf16)
            return c

        lax.fori_loop(0, S // tk, prep, 0)

        def qblock(i, first_half):
            r0 = pl.multiple_of(i * tq, tq)
            qh = [qa[hh, pl.ds(r0, tq), :] for hh in (0, 1)]

            def kv(jb, carry, masked, width):
                keys = pl.ds(pl.multiple_of(jb * tk, tk), width)
                sts = [_dot_nt(ka[hh, keys, :], qh[hh]) for hh in (0, 1)]
                new = []
                for hh in (0, 1):
                    m, l, a = carry[3 * hh:3 * hh + 3]
                    st = sts[hh]
                    if masked:
                        st = jnp.where(jb * tk + keyi[0:width] <= r0 + qryi, st, NEG)
                    mn = jnp.maximum(m, jnp.max(st, axis=0, keepdims=True))
                    al = jnp.exp(m - mn)
                    pt = jnp.exp(st - mn)
                    l = al * l + jnp.sum(pt, axis=0, keepdims=True)
                    a = al * a + _dot(vt[jb, hh * HD:(hh + 1) * HD, 0:width], pt.astype(bf16))
                    new += [mn, l, a]
                return tuple(new)

            init = (jnp.full((1, tq), NEG, f32), jnp.zeros((1, tq), f32), jnp.zeros((HD, tq), f32)) * 2
            last = (r0 + tq - 1) // tk
            carry = lax.fori_loop(0, last, lambda j, cr: kv(j, cr, False, tk), init)
            m0, l0, a0, m1, l1, a1 = kv(last, carry, True, tk // 2 if first_half else tk)
            ot = jnp.concatenate([a0 / l0, a1 / l1], axis=0)
            lt = jnp.where(sub < HD, m0 + jnp.log(l0), m1 + jnp.log(l1))
            o_ref[pl.ds(r0, tq), :] = ot.T.astype(bf16)
            lse_ref[pl.ds(r0, tq), :] = lt.T

        def qpair(t, c):
            qblock(2 * t, True)
            qblock(2 * t + 1, False)
            return c

        assert tk == 2 * tq
        lax.fori_loop(0, S // tk, qpair, 0)

    pair = pl.BlockSpec((S, 128), lambda p: (0, p))
    return _call(
        order, body, (qkva, F), name="fox_fwd", grid=(4,),
        in_specs=[pl.BlockSpec((S, FOX_BLK), lambda p: (0, p)), pl.BlockSpec((S, 128), lambda p: (0, 0))],
        out_specs=[pair, pair],
        out_shape=[jax.ShapeDtypeStruct((S, FOXW), bf16), jax.ShapeDtypeStruct((S, FOXW), f32)],
        scratch_shapes=[pltpu.VMEM((2, S, 128), bf16)] * 2 + [pltpu.VMEM((S // tk, 128, tk), bf16)],
        compiler_params=_params(("parallel",)),
    )


def _permute_in(dst, src, r):
    L = S // r
    for rho in range(r):
        dst[rho * L:(rho + 1) * L, :] = src[pl.ds(rho, L, stride=r), :]


def _permute_out(dst, src, r):
    L = S // r
    for rho in range(r):
        dst[pl.ds(rho, L, stride=r), :] = src[rho * L:(rho + 1) * L, :]


def _band_width(nbl):
    return BAND if nbl == 1 else 2 * BAND


def _band_geometry(bb, nbl):
    r0 = pl.multiple_of(bb * BAND, BAND)
    if nbl == 1:
        k0 = r0
    else:
        k0 = pl.multiple_of(jnp.maximum(bb - 1, 0) * BAND, BAND)
    sub0 = (bb - lax.rem(bb, nbl)) * BAND
    qi = r0 + lax.broadcasted_iota(jnp.int32, (BAND, 1), 0)
    ki = k0 + lax.broadcasted_iota(jnp.int32, (1, _band_width(nbl)), 1)
    diff = qi - ki
    valid = (diff >= 0) & (diff <= BAND) & (ki >= sub0)
    return r0, k0, valid


def _dil_views(ref):
    return [[ref.at[:, pl.ds((3 * role + g) * 128, 128)] for g in range(3)] for role in range(3)]


DIL_UNROLL = 4


def _dil_in_specs():
    return [pl.BlockSpec((S, 128), lambda p, k=k: (0, 9 * p + k)) for k in range(9)]


def _dil_fwd(order, qkvb):
    def body(*refs):
        q_refs, k_refs, v_refs = refs[0:3], refs[3:6], refs[6:9]
        ob_ref, lse_ref, qp, kp, vp, op, lp = refs[9:16]
        on, ln = refs[16:19], refs[19:22]
        _, hm = _head_masks()
        for g, r in enumerate(DIL):
            nbl = S // r // BAND
            if r == 1:
                qs_, ks_, vs_, od, ld = q_refs[g], k_refs[g], v_refs[g], on[g], ln[g]
            else:
                _permute_in(qp, q_refs[g], r)
                _permute_in(kp, k_refs[g], r)
                _permute_in(vp, v_refs[g], r)
                qs_, ks_, vs_, od, ld = qp, kp, vp, op, lp

            def blk(t, c, qs_=qs_, ks_=ks_, vs_=vs_, od=od, ld=ld, nbl=nbl):
                work = []
                for u in range(DIL_UNROLL):
                    r0, k0, valid = _band_geometry(DIL_UNROLL * t + u, nbl)
                    q = qs_[pl.ds(r0, BAND), :] * 0.125
                    kw = ks_[pl.ds(k0, _band_width(nbl)), :].astype(bf16)
                    vw = vs_[pl.ds(k0, _band_width(nbl)), :]
                    for hh in (0, 1):
                        qh = jnp.where(hm[hh], q, 0.0).astype(bf16)
                        work.append((u, hh, r0, valid, vw, _dot_nt(qh, kw)))
                o = [jnp.zeros((BAND, 128), f32)] * DIL_UNROLL
                lse = [jnp.zeros((BAND, 128), f32)] * DIL_UNROLL
                for u, hh, r0, valid, vw, s in work:
                    s = jnp.where(valid, s, NEG)
                    m = jnp.max(s, axis=1, keepdims=True)
                    pr = jnp.exp(s - m)
                    l = jnp.sum(pr, axis=1, keepdims=True)
                    vm = jnp.where(hm[hh], vw, 0.0).astype(bf16)
                    o[u] = o[u] + _dot((pr / l).astype(bf16), vm)
                    lse[u] = jnp.where(hm[hh], m + jnp.log(l), lse[u])
                    if hh == 1:
                        od[pl.ds(r0, BAND), :] = o[u]
                        ld[pl.ds(r0, BAND), :] = lse[u]
                return c

            lax.fori_loop(0, S // BAND // DIL_UNROLL, blk, 0)
            if r != 1:
                _permute_out(on[g], op, r)
                _permute_out(ln[g], lp, r)

        def combine(i, c):
            r0 = pl.multiple_of(i * TQ, TQ)
            ls = [ln[g][pl.ds(r0, TQ), :] for g in range(3)]
            mx = jnp.maximum(jnp.maximum(ls[0], ls[1]), ls[2])
            es = [jnp.exp(l - mx) for l in ls]
            tot = (es[0] + es[1]) + es[2]
            acc = (es[0] / tot) * on[0][pl.ds(r0, TQ), :]
            acc = acc + (es[1] / tot) * on[1][pl.ds(r0, TQ), :]
            acc = acc + (es[2] / tot) * on[2][pl.ds(r0, TQ), :]
            ob_ref[pl.ds(r0, TQ), :] = acc.astype(bf16)
            lse_ref[pl.ds(r0, TQ), :] = mx + jnp.log(tot)
            return c

        lax.fori_loop(0, S // TQ, combine, 0)

    out_blk = pl.BlockSpec((S, 128), lambda p: (0, p))
    return _call(
        order, body, [qkvb] * 9, name="dil_fwd", grid=(2,),
        in_specs=_dil_in_specs(), out_specs=[out_blk, out_blk],
        out_shape=[jax.ShapeDtypeStruct((S, DILOUT), bf16), jax.ShapeDtypeStruct((S, DILOUT), f32)],
        scratch_shapes=[pltpu.VMEM((S, 128), f32)] * 11,
        compiler_params=_params(("parallel",)),
    )


def _branch_mix(order, oa, ob, was, wbs, gates):
    tm = 512

    def body(oa_ref, ob_ref, wa_ref, wb_ref, g_ref, ya_ref, yb_ref, mix_ref):
        oa_b, ob_b = oa_ref[...], ob_ref[...]
        for q in range(NCHIP):
            cols = slice(q * 256, (q + 1) * 256)
            ya = _dot(oa_b, wa_ref[q])
            yb = _dot(ob_b, wb_ref[q])
            ya_ref[:, cols] = ya.astype(bf16)
            yb_ref[:, cols] = yb.astype(bf16)
            ga = g_ref[:, q * 256:(q + 1) * 256].astype(f32)
            gb = g_ref[:, D + q * 256:D + (q + 1) * 256].astype(f32)
            mix_ref[:, cols] = (jax.nn.sigmoid(ga) * ya + jax.nn.sigmoid(gb) * yb).astype(bf16)

    row = lambda w: pl.BlockSpec((tm, w), lambda i: (i, 0))
    full3 = lambda a: pl.BlockSpec(a.shape, lambda i: (0, 0, 0))
    return _call(
        order, body, (oa, ob, was, wbs, gates), name="branch_mix", grid=(S // tm,),
        in_specs=[row(FOXW), row(DILOUT), full3(was), full3(wbs), row(2 * D)],
        out_specs=[row(D), row(D), row(D)],
        out_shape=[jax.ShapeDtypeStruct((S, D), bf16), jax.ShapeDtypeStruct((S, D), bf16),
                   jax.ShapeDtypeStruct((S, D), bf16)],
        compiler_params=_params(("parallel",)),
    )


def _outproj_norm(order, mixed, wout, x, g2):
    tm = 512

    def body(m_ref, w_ref, x_ref, g_ref, x2_ref, h2_ref):
        x2 = x_ref[...] + _dot(m_ref[...], w_ref[...])
        x2_ref[...] = x2
        r = lax.rsqrt(jnp.mean(x2 * x2, axis=-1, keepdims=True) + EPS)
        h2_ref[...] = ((x2 * r) * g_ref[...]).astype(bf16)

    row = pl.BlockSpec((tm, D), lambda i: (i, 0))
    return _call(
        order, body, (mixed, wout, x, g2), name="outproj_norm", grid=(S // tm,),
        in_specs=[row, pl.BlockSpec((D, D), lambda i: (0, 0)), row, pl.BlockSpec((1, D), lambda i: (0, 0))],
        out_specs=[row, row],
        out_shape=[jax.ShapeDtypeStruct((S, D), f32), jax.ShapeDtypeStruct((S, D), bf16)],
        compiler_params=_params(("parallel",)),
    )


def _mlp_up(order, h2, wups):
    tm = 1024

    def body(h_ref, w_ref, ru_ref, a_ref):
        ru = jnp.maximum(_dot(h_ref[...], w_ref[...]), 0.0)
        ru_ref[...] = ru.astype(bf16)
        a_ref[...] = (ru * ru).astype(bf16)

    out = pl.BlockSpec((tm, D), lambda q, i: (i, q))
    return _call(
        order, body, (h2, wups), name="mlp_up", grid=(NCHIP, S // tm),
        in_specs=[pl.BlockSpec((tm, D), lambda q, i: (i, 0)), pl.BlockSpec((None, D, D), lambda q, i: (q, 0, 0))],
        out_specs=[out, out],
        out_shape=[jax.ShapeDtypeStruct((S, DFF), bf16), jax.ShapeDtypeStruct((S, DFF), bf16)],
        compiler_params=_params(("parallel", "parallel")),
    )


def _mlp_down_loss(order, a, wdown, x2, g3, tgt):
    tm = 512

    def body(a_ref, w_ref, x2_ref, g_ref, t_ref, dx_ref, dxb_ref, dg_ref, loss_ref):
        i = pl.program_id(0)
        x3 = x2_ref[...] + _dot(a_ref[...], w_ref[...])
        r = lax.rsqrt(jnp.mean(x3 * x3, axis=-1, keepdims=True) + EPS)
        xh = x3 * r
        g = g_ref[...]
        e = xh * g - t_ref[...]
        part = 0.5 * jnp.sum(jnp.mean(e * e, axis=-1, keepdims=True), axis=0, keepdims=True)
        dy = e * (1.0 / D)
        gdy = dy * g
        dx = r * (gdy - xh * jnp.mean(gdy * xh, axis=-1, keepdims=True))
        dx_ref[...] = dx
        dxb_ref[...] = dx.astype(bf16)

        @pl.when(i == 0)
        def _():
            dg_ref[...] = jnp.zeros_like(dg_ref)
            loss_ref[...] = jnp.zeros_like(loss_ref)

        dg_ref[...] += jnp.sum(dy * xh, axis=0, keepdims=True)
        loss_ref[...] += jnp.broadcast_to(part, (1, 128))

    row = pl.BlockSpec((tm, D), lambda i: (i, 0))
    vec = pl.BlockSpec((1, D), lambda i: (0, 0))
    return _call(
        order, body, (a, wdown, x2, g3, tgt), name="mlp_down_loss", grid=(S // tm,),
        in_specs=[pl.BlockSpec((tm, DFF), lambda i: (i, 0)), pl.BlockSpec((DFF, D), lambda i: (0, 0)), row, vec, row],
        out_specs=[row, row, vec, pl.BlockSpec((1, 128), lambda i: (0, 0))],
        out_shape=[jax.ShapeDtypeStruct((S, D), f32), jax.ShapeDtypeStruct((S, D), bf16),
                   jax.ShapeDtypeStruct((1, D), f32), jax.ShapeDtypeStruct((1, 128), f32)],
        compiler_params=_params(("arbitrary",)),
    )


def _mlp_down_bwd(order, dx3b, wdown, u):
    tm = 512

    def body(d_ref, w_ref, u_ref, du_ref):
        d = d_ref[...]
        for q in range(NCHIP):
            cols = slice(q * D, (q + 1) * D)
            da = _dot_nt(d, w_ref[cols, :])
            du_ref[:, cols] = (da * (2.0 * u_ref[:, cols].astype(f32))).astype(bf16)

    return _call(
        order, body, (dx3b, wdown, u), name="mlp_down_bwd", grid=(S // tm,),
        in_specs=[pl.BlockSpec((tm, D), lambda i: (i, 0)), pl.BlockSpec((DFF, D), lambda i: (0, 0)),
                  pl.BlockSpec((tm, DFF), lambda i: (i, 0))],
        out_specs=pl.BlockSpec((tm, DFF), lambda i: (i, 0)),
        out_shape=jax.ShapeDtypeStruct((S, DFF), bf16),
        compiler_params=_params(("parallel",)),
    )


def _mlp_up_bwd(order, du, wups, x2, dx3, g2):
    tm = 512

    def body(du_ref, w_ref, x2_ref, dx3_ref, g_ref, dx2_ref, dx2b_ref, dg_ref):
        i = pl.program_id(0)
        dh = jnp.zeros((tm, D), f32)
        for q in range(NCHIP):
            dh = dh + _dot_nt(du_ref[:, q * D:(q + 1) * D], w_ref[q])
        x2 = x2_ref[...]
        r = lax.rsqrt(jnp.mean(x2 * x2, axis=-1, keepdims=True) + EPS)
        xh = x2 * r
        gdh = dh * g_ref[...]
        dx2 = dx3_ref[...] + r * (gdh - xh * jnp.mean(gdh * xh, axis=-1, keepdims=True))
        dx2_ref[...] = dx2
        dx2b_ref[...] = dx2.astype(bf16)

        @pl.when(i == 0)
        def _():
            dg_ref[...] = jnp.zeros_like(dg_ref)

        dg_ref[...] += jnp.sum(dh * xh, axis=0, keepdims=True)

    row = pl.BlockSpec((tm, D), lambda i: (i, 0))
    vec = pl.BlockSpec((1, D), lambda i: (0, 0))
    return _call(
        order, body, (du, wups, x2, dx3, g2), name="mlp_up_bwd", grid=(S // tm,),
        in_specs=[pl.BlockSpec((tm, DFF), lambda i: (i, 0)), pl.BlockSpec((NCHIP, D, D), lambda i: (0, 0, 0)),
                  row, row, vec],
        out_specs=[row, row, vec],
        out_shape=[jax.ShapeDtypeStruct((S, D), f32), jax.ShapeDtypeStruct((S, D), bf16),
                   jax.ShapeDtypeStruct((1, D), f32)],
        compiler_params=_params(("arbitrary",)),
    )


def _gate_bwd(order, dx2b, wout, gates, ya, yb):
    tm = 512

    def body(d_ref, w_ref, g_ref, ya_ref, yb_ref, dya_ref, dyb_ref, dproj_ref):
        dm = _dot_nt(d_ref[...], w_ref[...])
        sa = jax.nn.sigmoid(g_ref[:, 0:D].astype(f32))
        sb = jax.nn.sigmoid(g_ref[:, D:2 * D].astype(f32))
        dya_ref[...] = (dm * sa).astype(bf16)
        dyb_ref[...] = (dm * sb).astype(bf16)
        dproj_ref[:, 0:D] = (dm * ya_ref[...].astype(f32) * (sa * (1.0 - sa))).astype(bf16)
        dproj_ref[:, D:2 * D] = (dm * yb_ref[...].astype(f32) * (sb * (1.0 - sb))).astype(bf16)

    row = lambda w: pl.BlockSpec((tm, w), lambda i: (i, 0))
    return _call(
        order, body, (dx2b, wout, gates, ya, yb), name="gate_bwd", grid=(S // tm,),
        in_specs=[row(D), pl.BlockSpec((D, D), lambda i: (0, 0)), row(2 * D), row(D), row(D)],
        out_specs=[row(D), row(D), pl.BlockSpec((tm, 2 * D), lambda i: (i, F_G // (2 * D)))],
        out_shape=[jax.ShapeDtypeStruct((S, D), bf16), jax.ShapeDtypeStruct((S, D), bf16),
                   jax.ShapeDtypeStruct((S, NP), bf16)],
        compiler_params=_params(("parallel",)),
    )


def _branch_bwd(order, dya, dyb, was, wbs):
    tm = 512

    def body(dya_ref, dyb_ref, wa_ref, wb_ref, doa_ref, dob_ref):
        doa = jnp.zeros((tm, FOXW), f32)
        dob = jnp.zeros((tm, DILOUT), f32)
        for q in range(NCHIP):
            cols = slice(q * 256, (q + 1) * 256)
            doa = doa + _dot_nt(dya_ref[:, cols], wa_ref[q])
            dob = dob + _dot_nt(dyb_ref[:, cols], wb_ref[q])
        doa_ref[...] = doa.astype(bf16)
        dob_ref[...] = dob

    row = lambda w: pl.BlockSpec((tm, w), lambda i: (i, 0))
    full3 = lambda a: pl.BlockSpec(a.shape, lambda i: (0, 0, 0))
    return _call(
        order, body, (dya, dyb, was, wbs), name="branch_bwd", grid=(S // tm,),
        in_specs=[row(D), row(D), full3(was), full3(wbs)],
        out_specs=[row(FOXW), row(DILOUT)],
        out_shape=[jax.ShapeDtypeStruct((S, FOXW), bf16), jax.ShapeDtypeStruct((S, DILOUT), f32)],
        compiler_params=_params(("parallel",)),
    )


def _branch_wgrad(order, oa, ob, dya, dyb):
    def body(oa_ref, ob_ref, dya_ref, dyb_ref, dwa_ref, dwb_ref):
        dwa_ref[...] = _dot_tn(oa_ref[...], dya_ref[...])
        dwb_ref[...] = _dot_tn(ob_ref[...], dyb_ref[...])

    full = lambda w: pl.BlockSpec((S, w), lambda q: (0, 0))
    colq = pl.BlockSpec((S, 256), lambda q: (0, q))
    return _call(
        order, body, (oa, ob, dya, dyb), name="branch_wgrad", grid=(NCHIP,),
        in_specs=[full(FOXW), full(DILOUT), colq, colq],
        out_specs=[pl.BlockSpec((None, FOXW, 256), lambda q: (q, 0, 0)),
                   pl.BlockSpec((None, DILOUT, 256), lambda q: (q, 0, 0))],
        out_shape=[jax.ShapeDtypeStruct((NCHIP, FOXW, 256), f32), jax.ShapeDtypeStruct((NCHIP, DILOUT, 256), f32)],
        compiler_params=_params(("parallel",)),
    )


def _fox_bwd(order, qkva, doa, oa, lse, F, dproj):
    tq, tk = FOX_TQ, FOX_TK

    def body(qkv_ref, do_ref, o_ref, lse_ref, F_ref, _dproj_in, dF_ref, dqkv_ref, qa, ka, da, va, kat,
             dk_scr, dv_scr, dqt_scr):
        p = pl.program_id(0)
        lane, hm = _head_masks()
        keyi = lax.broadcasted_iota(jnp.int32, (tk, 1), 0)
        qryi = lax.broadcasted_iota(jnp.int32, (1, tq), 1)

        def prep(i, c):
            rows = pl.ds(pl.multiple_of(i * tk, tk), tk)
            _fox_operands(qkv_ref, F_ref, lse_ref, qa, ka, p, rows)
            do = do_ref[rows, :].astype(f32)
            prod = do * o_ref[rows, :].astype(f32)
            v = qkv_ref[rows, 256:384].astype(f32)
            for hh in (0, 1):
                free = (1 - hh) * HD
                delta = jnp.sum(jnp.where(hm[hh], prod, 0.0), axis=1, keepdims=True)
                da[hh, rows, :] = _set_lanes(jnp.where(hm[hh], do, 0.0), lane, free,
                                             [-t for t in _f32_parts(delta)]).astype(bf16)
                va[hh, rows, :] = _set_lanes(v, lane, free, [1.0] * 3).astype(bf16)
                kat[hh, i] = ka[hh, rows, :].astype(f32).T.astype(bf16)
                dk_scr[hh, rows, :] = jnp.zeros((tk, 128), f32)
                dv_scr[hh, rows, :] = jnp.zeros((tk, 128), f32)
            return c

        lax.fori_loop(0, S // tk, prep, 0)

        def qblock(i, first_half):
            r0 = pl.multiple_of(i * tq, tq)
            qrows = pl.ds(r0, tq)
            qh = [qa[hh, qrows, :] for hh in (0, 1)]
            dh = [da[hh, qrows, :] for hh in (0, 1)]
            dqt_scr[...] = jnp.zeros_like(dqt_scr)

            def kv(jb, c2, masked, width):
                keys = pl.ds(pl.multiple_of(jb * tk, tk), width)
                sts = [_dot_nt(ka[hh, keys, :], qh[hh]) for hh in (0, 1)]
                dps = [_dot_nt(va[hh, keys, :], dh[hh]) for hh in (0, 1)]
                for hh in (0, 1):
                    pt = jnp.exp(sts[hh])
                    if masked:
                        pt = jnp.where(jb * tk + keyi[0:width] <= r0 + qryi, pt, 0.0)
                    dsb = (pt * dps[hh]).astype(bf16)
                    dv_scr[hh, keys, :] += _dot(pt.astype(bf16), dh[hh])
                    dk_scr[hh, keys, :] += _dot(dsb, qh[hh])
                    dqt_scr[hh] += _dot(kat[hh, jb, :, 0:width], dsb)
                return c2

            last = (r0 + tq - 1) // tk
            lax.fori_loop(0, last, lambda j, c2: kv(j, c2, False, tk), 0)
            kv(last, 0, True, tk // 2 if first_half else tk)
            dq0, dq1 = dqt_scr[0].T, dqt_scr[1].T
            dqkv_ref[qrows, 0:128] = (jnp.where(hm[0], dq0, dq1) * 0.125).astype(bf16)
            dF_ref[qrows, :] = jnp.where(lane == 0, dq0[:, HD:HD + 1], jnp.where(lane == 1, dq1[:, 0:1], 0.0))

        def qpair(t, c):
            qblock(2 * t, True)
            qblock(2 * t + 1, False)
            return c

        assert tk == 2 * tq
        lax.fori_loop(0, S // tk, qpair, 0)

        def finish(i, c):
            rows = pl.ds(pl.multiple_of(i * tq, tq), tq)
            dk0, dk1 = dk_scr[0, rows, :], dk_scr[1, rows, :]
            dqkv_ref[rows, 128:256] = jnp.where(hm[0], dk0, dk1).astype(bf16)
            dqkv_ref[rows, 256:384] = jnp.where(hm[0], dv_scr[0, rows, :], dv_scr[1, rows, :]).astype(bf16)
            cs = jnp.where(lane == 0, dk0[:, HD + L_ONE:HD + L_ONE + 1],
                           jnp.where(lane == 1, dk1[:, L_ONE:L_ONE + 1], 0.0))
            dF_ref[rows, :] = dF_ref[rows, :] - cs
            return c

        lax.fori_loop(0, S // tq, finish, 0)

    pair = pl.BlockSpec((S, 128), lambda p: (0, p))
    return _call(
        order, body, (qkva, doa, oa, lse, F, dproj), name="fox_bwd", grid=(4,),
        in_specs=[pl.BlockSpec((S, FOX_BLK), lambda p: (0, p)), pair, pair, pair,
                  pl.BlockSpec((S, 128), lambda p: (0, 0)), pl.BlockSpec(memory_space=pl.ANY)],
        out_specs=[pair, pl.BlockSpec((S, FOX_BLK), lambda p: (0, F_FOX // FOX_BLK + p))],
        out_shape=[jax.ShapeDtypeStruct((S, FOXW), f32), jax.ShapeDtypeStruct((S, NP), bf16)],
        input_output_aliases={5: 1},
        scratch_shapes=[pltpu.VMEM((2, S, 128), bf16)] * 4 + [pltpu.VMEM((2, S // tk, 128, tk), bf16)]
        + [pltpu.VMEM((2, S, 128), f32)] * 2 + [pltpu.VMEM((2, 128, tq), f32)],
        compiler_params=_params(("parallel",)),
    )


def _forget_bwd(order, dF, fa, bpad, dproj):
    nb = S // TQ

    def body(dF_ref, fa_ref, b_ref, _dproj_in, db_ref, dfa_ref):
        rr = lax.broadcasted_iota(jnp.int32, (TQ, TQ), 0)
        cc = lax.broadcasted_iota(jnp.int32, (TQ, TQ), 1)
        upper = (cc >= rr).astype(bf16)
        lane = lax.broadcasted_iota(jnp.int32, (1, 128), 1)
        carry = jnp.zeros((1, 128), f32)
        db = jnp.zeros((1, 128), f32)
        for b in reversed(range(nb)):
            cols = jnp.zeros((TQ, 128), f32)
            for h in range(8):
                c0 = (h // 2) * 128 + h % 2
                cols = jnp.where(lane == h, dF_ref[b * TQ:(b + 1) * TQ, c0:c0 + 1], cols)
            dlf = carry
            for part in _split3(cols):
                dlf = dlf + _dot(upper, part)
            carry = carry + jnp.sum(cols, axis=0, keepdims=True)
            z = fa_ref[b * TQ:(b + 1) * TQ, :] + b_ref[...]
            dz = jnp.where(lane < 8, dlf * jax.nn.sigmoid(-z), 0.0)
            dfa_ref[b * TQ:(b + 1) * TQ, 0:128] = dz.astype(bf16)
            dfa_ref[b * TQ:(b + 1) * TQ, 128:256] = jnp.zeros((TQ, 128), bf16)
            db = db + jnp.sum(dz, axis=0, keepdims=True)
        db_ref[...] = db

    whole = lambda a: pl.BlockSpec(a.shape, lambda i: (0,) * a.ndim)
    return _call(
        order, body, (dF, fa, bpad, dproj), name="forget_bwd", grid=(1,),
        in_specs=[whole(dF), whole(fa), whole(bpad), pl.BlockSpec(memory_space=pl.ANY)],
        out_specs=[pl.BlockSpec((1, 128), lambda i: (0, 0)), pl.BlockSpec((S, 256), lambda i: (0, F_FA // 256))],
        out_shape=[jax.ShapeDtypeStruct((1, 128), f32), jax.ShapeDtypeStruct((S, NP), bf16)],
        input_output_aliases={3: 1},
        compiler_params=_params(("arbitrary",)),
    )


def _dil_bwd(order, qkvb, dob, ob, lseb, rope, dproj):
    c_t, s1_t, s2_t = rope

    def body(*refs):
        q_refs, k_refs, v_refs = refs[0:3], refs[3:6], refs[6:9]
        dob_ref, ob_ref, lse_ref, c_ref, s1_ref, s2_ref, _dproj_in, dqkv_ref = refs[9:17]
        qp, kp, vp, dop, lp, dlp, dln, dqp, dkp, dvp, nat = refs[17:28]
        dq_out, dk_out, dv_out = _dil_views(dqkv_ref)
        _, hm = _head_masks()

        def delta_rows(i, c):
            r0 = pl.multiple_of(i * TQ, TQ)
            prod = dob_ref[pl.ds(r0, TQ), :] * ob_ref[pl.ds(r0, TQ), :].astype(f32)
            d0 = jnp.sum(jnp.where(hm[0], prod, 0.0), axis=1, keepdims=True)
            d1 = jnp.sum(jnp.where(hm[1], prod, 0.0), axis=1, keepdims=True)
            dln[pl.ds(r0, TQ), :] = jnp.where(hm[0], d0, d1)
            return c

        lax.fori_loop(0, S // TQ, delta_rows, 0)

        for g, r in enumerate(DIL):
            nbl = S // r // BAND
            if r == 1:
                srcs = (q_refs[g], k_refs[g], v_refs[g], dob_ref, lse_ref, dln)
            else:
                for dst, src in ((qp, q_refs[g]), (kp, k_refs[g]), (vp, v_refs[g]), (dop, dob_ref),
                                 (lp, lse_ref), (dlp, dln)):
                    _permute_in(dst, src, r)
                srcs = (qp, kp, vp, dop, lp, dlp)
            dkp[...] = jnp.zeros_like(dkp)
            dvp[...] = jnp.zeros_like(dvp)

            def blk(t, c, srcs=srcs, nbl=nbl):
                qs_, ks_, vs_, dos_, ls_, dls_ = srcs
                work = []
                for u in range(DIL_UNROLL):
                    r0, k0, valid = _band_geometry(DIL_UNROLL * t + u, nbl)
                    q = qs_[pl.ds(r0, BAND), :] * 0.125
                    kwf = ks_[pl.ds(k0, _band_width(nbl)), :]
                    kw = kwf.astype(bf16)
                    vw = vs_[pl.ds(k0, _band_width(nbl)), :].astype(bf16)
                    do = dos_[pl.ds(r0, BAND), :]
                    lse = ls_[pl.ds(r0, BAND), :]
                    dlt = dls_[pl.ds(r0, BAND), :]
                    for hh in (0, 1):
                        qh = jnp.where(hm[hh], q, 0.0).astype(bf16)
                        doh = jnp.where(hm[hh], do, 0.0).astype(bf16)
                        kh = jnp.where(hm[hh], kwf, 0.0).astype(bf16)
                        work.append((u, hh, r0, k0, valid, qh, doh, kh, lse[:, hh * HD:hh * HD + 1],
                                     dlt[:, hh * HD:hh * HD + 1], _dot_nt(qh, kw), _dot_nt(doh, vw)))
                for u, hh, r0, k0, valid, qh, doh, kh, lse_h, dlt_h, s, dp in work:
                    if hh == 0:
                        dq = jnp.zeros((BAND, 128), f32)
                        dk = jnp.zeros((_band_width(nbl), 128), f32)
                        dv = jnp.zeros((_band_width(nbl), 128), f32)
                    pr = jnp.where(valid, jnp.exp(s - lse_h), 0.0)
                    dsb = (pr * (dp - dlt_h)).astype(bf16)
                    dv = dv + _dot_tn(pr.astype(bf16), doh)
                    dk = dk + _dot_tn(dsb, qh)
                    dq = dq + _dot(dsb, kh)
                    if hh == 1:
                        dqp[pl.ds(r0, BAND), :] = dq * 0.125
                        dkp[pl.ds(k0, _band_width(nbl)), :] += dk
                        dvp[pl.ds(k0, _band_width(nbl)), :] += dv
                return c

            lax.fori_loop(0, S // BAND // DIL_UNROLL, blk, 0)

            for acc, out, roped in ((dqp, dq_out[g], True), (dkp, dk_out[g], True), (dvp, dv_out[g], False)):
                if r == 1:
                    src = acc
                else:
                    _permute_out(nat, acc, r)
                    src = nat

                def emit(i, c, src=src, out=out, roped=roped):
                    r0 = pl.multiple_of(i * TQ, TQ)
                    d = src[pl.ds(r0, TQ), :]
                    if roped:
                        d = (d * c_ref[pl.ds(r0, TQ), :] + pltpu.roll(d * s1_ref[pl.ds(r0, TQ), :], 8, 1)
                             + pltpu.roll(d * s2_ref[pl.ds(r0, TQ), :], 120, 1))
                    out[pl.ds(r0, TQ), :] = d.astype(bf16)
                    return c

                lax.fori_loop(0, S // TQ, emit, 0)

    pair = pl.BlockSpec((S, 128), lambda p: (0, p))
    tab = pl.BlockSpec((S, 128), lambda p: (0, 0))
    blk_spec = pl.BlockSpec((S, DIL_BLK), lambda p: (0, p))
    return _call(
        order, body, [qkvb] * 9 + [dob, ob, lseb, c_t, s1_t, s2_t, dproj], name="dil_bwd", grid=(2,),
        in_specs=_dil_in_specs() + [pair, pair, pair, tab, tab, tab, pl.BlockSpec(memory_space=pl.ANY)],
        out_specs=blk_spec,
        out_shape=jax.ShapeDtypeStruct((S, NP), bf16),
        input_output_aliases={15: 0},
        scratch_shapes=[pltpu.VMEM((S, 128), f32)] * 11,
        compiler_params=_params(("parallel",)),
    )


def _inproj_bwd(order, dproj, wt, x, dx2, g1):
    tm = 256

    def body(d_ref, w_ref, x_ref, dx2_ref, g_ref, dx_ref, dg_ref):
        i = pl.program_id(0)
        dh = _dot(d_ref[...], w_ref[...])
        xb = x_ref[...]
        r = lax.rsqrt(jnp.mean(xb * xb, axis=-1, keepdims=True) + EPS)
        xh = xb * r
        gdh = dh * g_ref[...]
        dx_ref[...] = dx2_ref[...] + r * (gdh - xh * jnp.mean(gdh * xh, axis=-1, keepdims=True))

        @pl.when(i == 0)
        def _():
            dg_ref[...] = jnp.zeros_like(dg_ref)

        dg_ref[...] += jnp.sum(dh * xh, axis=0, keepdims=True)

    row = pl.BlockSpec((tm, D), lambda i: (i, 0))
    vec = pl.BlockSpec((1, D), lambda i: (0, 0))
    return _call(
        order, body, (dproj, wt, x, dx2, g1), name="inproj_bwd", grid=(S // tm,),
        in_specs=[pl.BlockSpec((tm, NP), lambda i: (i, 0)), pl.BlockSpec((NP, D), lambda i: (0, 0)), row, row, vec],
        out_specs=[row, vec],
        out_shape=[jax.ShapeDtypeStruct((S, D), f32), jax.ShapeDtypeStruct((1, D), f32)],
        compiler_params=_params(("arbitrary",)),
    )


HBM = pl.BlockSpec(memory_space=pltpu.HBM)
SEM = pl.BlockSpec(memory_space=pltpu.SEMAPHORE)
SMALL_ROWS = 8


def _comm_call(name, body, bufs, order, sems_in=(), new_sems=(), behind=()):
    nb, ns, nn = len(bufs), len(sems_in), len(new_sems)
    extra = order.token_for(bufs) + list(behind)

    def kern(*refs):
        off = nb + ns + len(extra)
        body(refs[:nb], refs[nb:nb + ns], refs[off:off + nn])
        refs[-1][...] = jnp.zeros((8, 128), f32)

    res = pl.pallas_call(
        kern, name=name,
        in_specs=[HBM] * nb + [SEM] * ns + [pl.BlockSpec(memory_space=pl.ANY)] * len(extra),
        out_specs=[SEM] * nn + [HBM] * nb + [pl.BlockSpec(memory_space=pltpu.VMEM)],
        out_shape=[pltpu.SemaphoreType.DMA((k,)) for k in new_sems] + [pltpu.HBM(b.shape, b.dtype) for b in bufs]
        + [jax.ShapeDtypeStruct((8, 128), f32)],
        input_output_aliases={i: nn + i for i in range(nb)},
        compiler_params=pltpu.CompilerParams(has_side_effects=pltpu.SideEffectType.DATAFLOW_SIDE_EFFECTING),
    )(*[pltpu.with_memory_space_constraint(b, pltpu.HBM) for b in bufs], *sems_in, *extra)
    order.mark(res[-1])
    return list(res[:nn]), list(res[nn:nn + nb])


def _place():
    x, y, c = lax.axis_index("x"), lax.axis_index("y"), lax.axis_index("c")
    chips = [(1 - x, y), (x, 1 - y), (1 - x, 1 - y)]
    return x, y, c, chips


def _rcopy(src, dst, ssem, rsem, dev):
    return pltpu.make_async_remote_copy(src_ref=src, dst_ref=dst, send_sem=ssem, recv_sem=rsem,
                                        device_id=dev, device_id_type=pl.DeviceIdType.MESH)


def _half(nrows, which):
    return pl.ds(which * (nrows // 2), nrows // 2)


def _ici_copies(stack, ssem, rsem, relay):
    x, y, c, chips = _place()
    me_q = 2 * x + y
    sends, recvs = {}, {}
    for a in range(len(stack)):
        rows = _half(stack[a].shape[1], c)
        for j, (cx, cy) in enumerate(chips):
            if relay and a == 0 and j == 2:
                continue
            mine = stack[a].at[me_q, rows]
            sends[a, j] = _rcopy(mine, mine, ssem.at[a * 3 + j], rsem.at[a * 3 + j], (cx, cy, c))
            theirs = stack[a].at[2 * cx + cy, rows]
            recvs[a, j] = _rcopy(theirs, theirs, ssem.at[a * 3 + j], rsem.at[a * 3 + j], (cx, cy, c))
    return sends, recvs


def _relay_copies(win, ssem, rsem):
    x, y, c, chips = _place()
    quarter = win.shape[1] // 4
    sends, recvs = [], []
    for k in range(2):
        rows = pl.ds(c * 2 * quarter + k * quarter, quarter)
        (fx, fy), (tx, ty) = chips[k], chips[1 - k]
        landed = win.at[2 * fx + fy, rows]
        sends.append(_rcopy(landed, landed, ssem.at[k], rsem.at[k], (tx, ty, c)))
        far = win.at[2 * chips[2][0] + chips[2][1], rows]
        recvs.append(_rcopy(far, far, ssem.at[k], rsem.at[k], (tx, ty, c)))
    return sends, recvs


def _allgather_start(name, stacks, order, relay=False):
    n = len(stacks)

    def body(bufs, _, new):
        sends, _r = _ici_copies(bufs, new[0], new[1], relay)
        for cp in sends.values():
            cp.start()

    return _comm_call(name, body, stacks, order, new_sems=(3 * n, 3 * n))


def _forward_copies(stack, ssem, rsem, relay=False):
    x, y, c, chips = _place()
    sib = (x, y, 1 - c)
    sends, recvs = {}, {}
    for a in range(len(stack)):
        for j, (cx, cy) in enumerate(chips):
            if relay and a == 0 and j == 2:
                continue
            landed = stack[a].at[2 * cx + cy, _half(stack[a].shape[1], c)]
            sends[a, j] = _rcopy(landed, landed, ssem.at[a * 3 + j], rsem.at[a * 3 + j], sib)
            other = stack[a].at[2 * cx + cy, _half(stack[a].shape[1], 1 - c)]
            recvs[a, j] = _rcopy(other, other, ssem.at[a * 3 + j], rsem.at[a * 3 + j], sib)
    return sends, recvs


def _far_forward(win, ssem, rsem):
    x, y, c, chips = _place()
    sib, far_q = (x, y, 1 - c), 2 * chips[2][0] + chips[2][1]
    landed, other = win.at[far_q, _half(win.shape[1], c)], win.at[far_q, _half(win.shape[1], 1 - c)]
    return _rcopy(landed, landed, ssem.at[0], rsem.at[0], sib), _rcopy(other, other, ssem.at[0], rsem.at[0], sib)


def _allgather_forward(name, stacks, sems, order, behind=(), relay=False):
    n = len(stacks)

    def body(bufs, taken, new):
        sends, recvs = _ici_copies(bufs, taken[0], taken[1], relay)
        fwd, _r = _forward_copies(bufs, new[0], new[1], relay)
        relay_sends = _relay_copies(bufs[0], new[2], new[3])[0] if relay else []
        for (a, j), arrived in recvs.items():
            arrived.wait_recv()
            fwd[a, j].start()
            if relay and a == 0:
                relay_sends[j].start()
        for cp in sends.values():
            cp.wait_send()

    return _comm_call(name, body, stacks, order, sems_in=sems, behind=behind,
                      new_sems=(3 * n, 3 * n) + ((2, 2) if relay else ()))


def _allgather_finish(name, stacks, sems, order, relay=False):
    def body(bufs, taken, new):
        sends, recvs = _forward_copies(bufs, taken[0], taken[1], relay)
        if relay:
            relay_sends, relay_recvs = _relay_copies(bufs[0], taken[2], taken[3])
            for cp in relay_recvs:
                cp.wait_recv()
            _far_forward(bufs[0], new[0], new[1])[0].start()
            for cp in relay_sends:
                cp.wait_send()
        for cp in sends.values():
            cp.wait_send()
        for cp in recvs.values():
            cp.wait_recv()

    if relay:
        return _comm_call(name, body, stacks, order, sems_in=sems, new_sems=(1, 1))
    return _comm_call(name, body, stacks, order, sems_in=sems)[1]


def _allgather_finish_far(name, stacks, sems, order):
    def body(bufs, taken, _):
        send, recv = _far_forward(bufs[0], taken[0], taken[1])
        send.wait_send()
        recv.wait_recv()

    return _comm_call(name, body, stacks, order, sems_in=sems)[1]


def _window_unit(q, j):
    return C2I[WIN_UNIT0[q] + j]


def _pair_copies(g, t, ssem, rsem, gathered):
    x, y, c, _ = _place()
    sib = (x, y, 1 - c)
    cps, whole = [], []
    for a in range(len(g)):
        if a == 0 and gathered:
            for q in range(NCHIP):
                for j in range(WIN_UNITS // 2):
                    u = jnp.where(c == 0, _window_unit(q, WIN_UNITS // 2 + j), _window_unit(q, j))
                    src = g[0].at[pl.ds(pl.multiple_of(u * UNIT, UNIT), UNIT), :]
                    cps.append(_rcopy(src, t[0].at[q, pl.ds(j * UNIT, UNIT), :], ssem.at[0], rsem.at[0], sib))
            whole.append(_rcopy(t[0], t[0], ssem.at[0], rsem.at[0], sib))
        else:
            cp = _rcopy(g[a].at[:, _half(g[a].shape[1], 1 - c), :], t[a], ssem.at[a], rsem.at[a], sib)
            cps.append(cp)
            whole.append(cp)
    return cps, whole


def _comm_multi(name, parts, order):
    def body(buf_refs, taken, new):
        ib = it = inew = 0
        for pbody, pbufs, psems, pnew, _ in parts:
            pbody(buf_refs[ib:ib + len(pbufs)], taken[it:it + len(psems)], new[inew:inew + len(pnew)])
            ib, it, inew = ib + len(pbufs), it + len(psems), inew + len(pnew)

    sems, bufs = _comm_call(name, body, [b for p in parts for b in p[1]], order,
                            sems_in=[s for p in parts for s in p[2]], new_sems=[k for p in parts for k in p[3]])
    out, ib, inew = [], 0, 0
    for _, pbufs, _, pnew, unpack in parts:
        out.append(unpack(sems[inew:inew + len(pnew)], bufs[ib:ib + len(pbufs)]))
        ib, inew = ib + len(pbufs), inew + len(pnew)
    return out


def _pair_start_part(gs, gathered=False):
    n = len(gs)
    ts = [lax.empty((NCHIP, WIN_ROWS // 2, D) if (a == 0 and gathered) else (NCHIP, g.shape[1] // 2, g.shape[2]), f32)
          for a, g in enumerate(gs)]

    def body(bufs, _, new):
        for cp in _pair_copies(bufs[:n], bufs[n:], new[0], new[1], gathered)[0]:
            cp.start()

    return body, list(gs) + ts, (), (n, n), lambda sems, bufs: (sems, bufs)


def _pair_wait_part(bufs, sems, gathered=False):
    n = len(bufs) // 2

    def body(refs, taken, _):
        for cp in _pair_copies(refs[:n], refs[n:], taken[0], taken[1], gathered)[1]:
            cp.wait_send()
            cp.wait_recv()

    return body, list(bufs), list(sems), (), lambda _, out: (out[:n], out[n:])


ROW_TILE = 256


def _pair_add(order, gs, ts, q_arr, c_arr, name):
    n = len(gs)
    hs = [g.shape[1] // 2 for g in gs]
    nblk = max(1, max(hs) // ROW_TILE)
    assert all(h % (16 * nblk) == 0 for h in hs)

    def body(q_ref, c_ref, *refs):
        for a in range(n):
            s = refs[a][...] + refs[n + a][...]
            refs[3 * n + a][...] = s.astype(bf16)

            @pl.when(pl.program_id(1) == q_ref[0])
            def _():
                refs[2 * n + a][...] = s

    def blk(a, half):
        return pl.BlockSpec((None, hs[a] // nblk, gs[a].shape[2]),
                            lambda i, q, q_ref, c_ref: (q, (c_ref[0] * nblk if half else 0) + i, 0))

    out = _call_indexed(
        order, body, (q_arr, c_arr), list(gs) + list(ts), (nblk, NCHIP),
        [blk(a, True) for a in range(n)] + [blk(a, False) for a in range(n)],
        [pl.BlockSpec((hs[a] // nblk, gs[a].shape[2]), lambda i, q, q_ref, c_ref: (i, 0)) for a in range(n)]
        + [blk(a, False) for a in range(n)],
        name=name,
        out_shape=[jax.ShapeDtypeStruct((hs[a], gs[a].shape[2]), f32) for a in range(n)]
        + [jax.ShapeDtypeStruct((NCHIP, hs[a], gs[a].shape[2]), bf16) for a in range(n)],
        compiler_params=_params(("parallel", "arbitrary")),
    )
    return out[:n], out[n:]


def _pair_add_gathered(order, dwt, t, q_arr, c_arr, name):
    half_units, half_rows = WIN_UNITS // 2, WIN_ROWS // 2
    table = jnp.asarray([_window_unit(q, j) for q in range(NCHIP) for j in range(WIN_UNITS)], jnp.int32)

    def body(tab_ref, q_ref, c_ref, g_hbm, t_ref, own_ref, p16_ref, buf, sem):
        q = pl.program_id(0)

        def gather(w, slot):
            cps = []
            for j in range(half_units):
                u = tab_ref[w * WIN_UNITS + c_ref[0] * half_units + j]
                cps.append(pltpu.make_async_copy(g_hbm.at[pl.ds(pl.multiple_of(u * UNIT, UNIT), UNIT), :],
                                                 buf.at[slot, pl.ds(j * UNIT, UNIT), :], sem.at[slot]))
            return cps

        @pl.when(q == 0)
        def _():
            for j, cp in enumerate(gather(0, 0)):
                cp.start(priority=j % 2)

        @pl.when(q + 1 < NCHIP)
        def _():
            for j, cp in enumerate(gather(q + 1, (q + 1) % 2)):
                cp.start(priority=j % 2)

        slot = q % 2
        pltpu.make_async_copy(buf.at[slot], buf.at[slot], sem.at[slot]).wait()
        s = buf[slot] + t_ref[...]
        p16_ref[...] = s.astype(bf16)

        @pl.when(q == q_ref[0])
        def _():
            own_ref[...] = s

    blk = pl.BlockSpec((None, half_rows, D), lambda q, tab_ref, q_ref, c_ref: (q, 0, 0))
    return _call_indexed(
        order, body, (table, q_arr, c_arr), (dwt, t), (NCHIP,),
        [pl.BlockSpec(memory_space=pl.ANY), blk],
        [pl.BlockSpec((half_rows, D), lambda q, tab_ref, q_ref, c_ref: (0, 0)), blk],
        scratch_shapes=[pltpu.VMEM((2, half_rows, D), f32), pltpu.SemaphoreType.DMA((2,))],
        name=name,
        out_shape=[jax.ShapeDtypeStruct((half_rows, D), f32),
                   jax.ShapeDtypeStruct((NCHIP, half_rows, D), bf16)],
        compiler_params=_params(("arbitrary",)),
    )


def _shard_copies(p, r, sm, ssem, rsem):
    x, y, c, chips = _place()
    n = len(p)
    sends, recvs = [], []
    for a in range(n):
        for j, (cx, cy) in enumerate(chips):
            k = a * 3 + j
            sends.append(_rcopy(p[a].at[2 * cx + cy], r[a].at[j], ssem.at[k], rsem.at[k], (cx, cy, c)))
            recvs.append(_rcopy(r[a].at[j], r[a].at[j], ssem.at[k], rsem.at[k], (cx, cy, c)))
    if sm is not None:
        mine = sm.at[4 * x + 2 * y + c]
        for i in range(1, 8):
            px = (1 - x) if i & 4 else x
            py = (1 - y) if i & 2 else y
            pc = (1 - c) if i & 1 else c
            k = 3 * n + i - 1
            sends.append(_rcopy(mine, mine, ssem.at[k], rsem.at[k], (px, py, pc)))
            slot = sm.at[4 * px + 2 * py + pc]
            recvs.append(_rcopy(slot, slot, ssem.at[k], rsem.at[k], (px, py, pc)))
    return sends, recvs


def _shard_start_part(p16s, sm=None):
    n = len(p16s)
    rs = [lax.empty((3,) + p.shape[1:], bf16) for p in p16s]
    extra = [] if sm is None else [sm]
    nsem = 3 * n + (7 if sm is not None else 0)

    def body(bufs, _, new):
        sends, _r = _shard_copies(bufs[:n], bufs[n:2 * n], bufs[2 * n] if extra else None, new[0], new[1])
        for cp in sends:
            cp.start()

    return body, list(p16s) + rs + extra, (), (nsem, nsem), lambda sems, bufs: (sems, bufs)


def _shard_wait_part(bufs, sems, n):
    has_sm = len(bufs) > 2 * n

    def body(refs, taken, _):
        sends, recvs = _shard_copies(refs[:n], refs[n:2 * n], refs[2 * n] if has_sm else None, taken[0], taken[1])
        for cp in sends:
            cp.wait_send()
        for cp in recvs:
            cp.wait_recv()

    return body, list(bufs), list(sems), (), lambda _, out: (out[n:2 * n], (out[2 * n] if has_sm else None))


def _shard_sum(order, owns, rs, c_arr, name):
    n = len(owns)
    hs = [o.shape[0] for o in owns]
    nblk = max(1, max(hs) // ROW_TILE)
    assert all(h % (16 * nblk) == 0 for h in hs)
    trs = [h // nblk for h in hs]

    def body(c_ref, *refs):
        for a in range(n):
            s = refs[a][...]
            for j in range(3):
                s = s + refs[n + a][j].astype(f32)
            refs[2 * n + a][...] = s

    out = _call_indexed(
        order, body, (c_arr,), list(owns) + list(rs), (nblk,),
        [pl.BlockSpec((trs[a], owns[a].shape[1]), lambda i, c_ref: (i, 0)) for a in range(n)]
        + [pl.BlockSpec((3, trs[a], owns[a].shape[1]), lambda i, c_ref: (0, i, 0)) for a in range(n)],
        [pl.BlockSpec((trs[a], owns[a].shape[1]), lambda i, c_ref: (c_ref[0] * nblk + i, 0)) for a in range(n)],
        name=name, out_shape=[jax.ShapeDtypeStruct((2 * hs[a], owns[a].shape[1]), f32) for a in range(n)],
        compiler_params=_params(("parallel",)),
    )
    return list(out)


def _swap_copies(full, ssem, rsem):
    x, y, c, _ = _place()
    sends, recvs = [], []
    for a in range(len(full)):
        mine = full[a].at[_half(full[a].shape[0], c)]
        sends.append(_rcopy(mine, mine, ssem.at[a], rsem.at[a], (x, y, 1 - c)))
        other = full[a].at[_half(full[a].shape[0], 1 - c)]
        recvs.append(_rcopy(other, other, ssem.at[a], rsem.at[a], (x, y, 1 - c)))
    return sends, recvs


def _swap_start_part(fulls):
    n = len(fulls)

    def body(bufs, _, new):
        for cp in _swap_copies(bufs, new[0], new[1])[0]:
            cp.start()

    return body, list(fulls), (), (n, n), lambda sems, bufs: (sems, bufs)


def _swap_wait_part(fulls, sems):
    def body(refs, taken, _):
        sends, recvs = _swap_copies(refs, taken[0], taken[1])
        for cp in sends:
            cp.wait_send()
        for cp in recvs:
            cp.wait_recv()

    return body, list(fulls), list(sems), (), lambda _, out: out


def _small_finish(order, sm, ws, ms, vs):
    n = len(ws)

    def body(sm_ref, *refs):
        s = sm_ref[0]
        for d in range(1, 8):
            s = s + sm_ref[d]
        loss_ref, g_refs, upd_refs = refs[3 * n], refs[3 * n + 1:4 * n + 1], refs[4 * n + 1:]
        loss_ref[...] = s[n:n + 1, 0:1]
        for i in range(n):
            g = s[i:i + 1, 0:ws[i].shape[1]]
            g_refs[i][...] = g
            res = _adamw_math(refs[i][...], g, refs[n + i][...], refs[2 * n + i][...])
            for k in range(3):
                upd_refs[3 * i + k][...] = res[k]

    out = _call(order, body, [sm] + list(ws) + list(ms) + list(vs), name="small_sum_adamw",
                out_shape=[jax.ShapeDtypeStruct((1, 1), f32)] + [jax.ShapeDtypeStruct(w.shape, f32) for w in ws]
                + [jax.ShapeDtypeStruct(w.shape, f32) for w in ws for _ in range(3)])
    return out[0], out[1:n + 1], [out[n + 1 + 3 * i:n + 4 + 3 * i] for i in range(n)]


def _adamw_math(w, g, m, v):
    m = ADAM_B1 * m + (1.0 - ADAM_B1) * g
    v = ADAM_B2 * v + (1.0 - ADAM_B2) * (g * g)
    m_hat = m / (1.0 - ADAM_B1 ** ADAM_STEP)
    v_hat = v / (1.0 - ADAM_B2 ** ADAM_STEP)
    return -ADAM_LR * (m_hat / (jnp.sqrt(v_hat) + ADAM_EPS) + ADAM_WD * w), m, v


def _adamw(order, ws, gs, ms, vs, name):
    n = len(ws)
    nblk = max(1, max(w.shape[0] for w in ws) // ROW_TILE)
    assert all(w.shape[0] % (8 * nblk) == 0 for w in ws)

    def body(*refs):
        for a in range(n):
            w_ref, g_ref, m_ref, v_ref = (refs[k * n + a] for k in range(4))
            d_ref, nm_ref, nv_ref, g_out = refs[4 * n + 4 * a:4 * n + 4 * a + 4]
            g = g_ref[...]
            g_out[...] = g
            d_ref[...], nm_ref[...], nv_ref[...] = _adamw_math(w_ref[...], g, m_ref[...], v_ref[...])

    blks = [pl.BlockSpec((w.shape[0] // nblk, w.shape[1]), lambda i: (i, 0)) for w in ws]
    out = _call(
        order, body, list(ws) + list(gs) + list(ms) + list(vs), name=name, grid=(nblk,), in_specs=blks * 4,
        out_specs=[b for b in blks for _ in range(4)],
        out_shape=[jax.ShapeDtypeStruct(w.shape, f32) for w in ws for _ in range(4)],
        compiler_params=_params(("parallel",)),
    )
    return [out[4 * a:4 * a + 4] for a in range(n)]


def _feature_rows(w):
    return jnp.transpose(w, (2, 0, 1))


WIN_STEP = 128
WIN_PIECE = 2 * WIN_STEP


def _window_stacks(order, w, q_arr):
    steps = WIN_ROWS // WIN_STEP
    n_piece = (WIN_ROWS - 2 * WIN_STEP) // WIN_PIECE
    assert n_piece * WIN_PIECE == WIN_ROWS - 2 * WIN_STEP and WIN_STEP % 16 == 0
    assert max(OWN_ROW0) < UNIT <= WIN_STEP and OWN_ROW0[1] + FA_AT == UNIT and FA_AT + N_FA + UNIT <= SHARD_IN
    pad = -(-(WIN_ROWS - SHARD_IN + N_FA) // 8) * 8
    lead = pad - (WIN_ROWS - SHARD_IN)
    assert lead + OWN_ROW0[1] - N_FA >= 0 and lead + max(OWN_ROW0) <= pad and max(OWN_ROW0) <= WIN_ROWS - SHARD_IN

    def body(q_ref, w_ref, win_ref, fa_ref, first, last, til, fabuf, sem):
        i = pl.program_id(0)
        q = q_ref[0]
        chip1 = q == 1
        row0 = jnp.where(q == 0, OWN_ROW0[0], jnp.where(chip1, OWN_ROW0[1], jnp.where(q == 2, OWN_ROW0[2], OWN_ROW0[3])))
        skip = jnp.where(chip1, N_FA, 0)

        def rows(dst, src0, dst0, n, slot):
            return pltpu.make_async_copy(w_ref.at[pl.ds(src0, n)], dst.at[pl.ds(dst0, n)], sem.at[slot])

        def first_copies(on_chip1):
            if on_chip1:
                return [rows(first, 0, OWN_ROW0[1], FA_AT, 0), rows(first, FA_AT + N_FA, UNIT, UNIT, 1)]
            return [rows(first, 0, row0, WIN_STEP, 0)]

        def first_do(act):
            for on_chip1 in (False, True):
                @pl.when(chip1 if on_chip1 else jnp.logical_not(chip1))
                def _():
                    for c in first_copies(on_chip1):
                        act(c)

        def piece(j):
            dst0 = WIN_STEP + j * WIN_PIECE
            return pltpu.make_async_copy(w_ref.at[pl.ds(dst0 - row0 + skip, WIN_PIECE), 0],
                                         til.at[pl.ds(dst0, WIN_PIECE)], sem.at[2 + j])

        last_copy = rows(last, SHARD_IN - WIN_STEP, lead + row0 - skip, WIN_STEP, 2 + n_piece)
        fa_copy = rows(fabuf, FA_AT, 0, N_FA, 3 + n_piece)

        @pl.when(i == 0)
        def _():
            first[pl.ds(0, UNIT)] = jnp.zeros((UNIT, 1, D), f32)
            last[...] = jnp.zeros(last.shape, f32)
            fabuf[pl.ds(N_FA, FA_ROWS - N_FA)] = jnp.zeros((FA_ROWS - N_FA, 1, D), f32)
            fa_copy.start()
            first_do(lambda c: c.start())
            for j in range(n_piece):
                piece(j).start()
            last_copy.start()
            fa_copy.wait()
            fa_ref[...] = fabuf[...].reshape(FA_ROWS, D).astype(bf16)
            first_do(lambda c: c.wait())
            win_ref[...] = first[pl.ds(0, WIN_STEP)].reshape(WIN_STEP, D).astype(bf16)

        for j in range(n_piece):
            @pl.when(i == 1 + j * (WIN_PIECE // WIN_STEP))
            def _():
                piece(j).wait()

        @pl.when(jnp.logical_and(i > 0, i < steps - 1))
        def _():
            win_ref[...] = til[pl.ds(pl.multiple_of(i * WIN_STEP, WIN_STEP), WIN_STEP)].astype(bf16)

        @pl.when(i == steps - 1)
        def _():
            last_copy.wait()
            win_ref[...] = last[pl.ds(pad, WIN_STEP)].reshape(WIN_STEP, D).astype(bf16)

    return _call_indexed(
        order, body, (q_arr,), (w,), (steps,), [pl.BlockSpec(memory_space=pl.ANY)],
        [pl.BlockSpec((None, WIN_STEP, D), lambda i, q: (q[0], i, 0)),
         pl.BlockSpec((None, FA_ROWS, D), lambda i, q: (q[0], 0, 0))],
        scratch_shapes=[pltpu.VMEM((WIN_STEP + UNIT, 1, D), f32), pltpu.VMEM((pad + WIN_STEP, 1, D), f32),
                        pltpu.VMEM((WIN_ROWS, D), f32), pltpu.VMEM((FA_ROWS, 1, D), f32),
                        pltpu.SemaphoreType.DMA((4 + n_piece,))],
        name="window_w_in", out_shape=[jax.ShapeDtypeStruct((NCHIP, WIN_ROWS, D), bf16),
                                       jax.ShapeDtypeStruct((NCHIP, FA_ROWS, D), bf16)],
        compiler_params=_params(("arbitrary",)),
    )


def _unfeature_rows(a):
    return jnp.transpose(a, (1, 2, 0))


ADAM_IN_ROWS = 134
ADAM_IN_STEPS = SHARD_IN // ADAM_IN_ROWS
ADAM_IN_CHUNK = 136
ADAM_IN_CHUNKS = ADAM_IN_STEPS + 1
ADAM_IN_BUF = WIN_ROWS + N_FA


def _adamw_w_in(order, w, gwin, gfa, m, v, q_arr):
    assert ADAM_IN_CHUNK * ADAM_IN_STEPS < WIN_ROWS <= ADAM_IN_CHUNK * ADAM_IN_CHUNKS
    assert OWN_ROW0[NCHIP - 1] + ADAM_IN_ROWS <= 2 * ADAM_IN_CHUNK and ADAM_IN_CHUNK >= ADAM_IN_ROWS
    last0 = ADAM_IN_CHUNK * ADAM_IN_STEPS
    cut = OWN_ROW0[1] + FA_AT

    def body(q_ref, w_ref, gwin_ref, gfa_ref, m_ref, v_ref, go_ref, d_ref, nm_ref, nv_ref, buf, sem):
        i = pl.program_id(0)
        q = q_ref[0]
        chip1 = q == 1
        shift = jnp.where(chip1, N_FA, 0)

        def copy(src_ref, src0, dst0, n, slot):
            return pltpu.make_async_copy(src_ref.at[pl.ds(src0, n)], buf.at[pl.ds(dst0, n), 0], sem.at[slot])

        def first(on_chip1):
            if on_chip1:
                return [copy(gwin_ref, 0, 0, cut, 0), copy(gfa_ref, 0, cut, N_FA, ADAM_IN_CHUNKS),
                        copy(gwin_ref, cut, cut + N_FA, ADAM_IN_CHUNK - cut - N_FA, ADAM_IN_CHUNKS + 1)]
            return [copy(gwin_ref, 0, 0, ADAM_IN_CHUNK, 0)]

        def middle(k):
            return [copy(gwin_ref, pl.multiple_of(k * ADAM_IN_CHUNK - shift, 8), k * ADAM_IN_CHUNK, ADAM_IN_CHUNK, k)]

        def last(on_chip1):
            n = WIN_ROWS - last0 + (N_FA if on_chip1 else 0)
            return [copy(gwin_ref, WIN_ROWS - n, last0, n, ADAM_IN_STEPS)]

        def both(make, act):
            for on_chip1 in (False, True):
                @pl.when(chip1 if on_chip1 else jnp.logical_not(chip1))
                def _():
                    for c in make(on_chip1):
                        act(c)

        @pl.when(i == 0)
        def _():
            both(first, lambda c: c.start())
            for k in range(1, ADAM_IN_STEPS):
                middle(k)[0].start()
            both(last, lambda c: c.start())
            both(first, lambda c: c.wait())

        @pl.when(i < ADAM_IN_STEPS - 1)
        def _():
            middle(i + 1)[0].wait()

        @pl.when(i == ADAM_IN_STEPS - 1)
        def _():
            both(last, lambda c: c.wait())

        row0 = jnp.where(q == 0, OWN_ROW0[0], jnp.where(chip1, OWN_ROW0[1], jnp.where(q == 2, OWN_ROW0[2], OWN_ROW0[3])))
        g = buf[pl.ds(row0 + i * ADAM_IN_ROWS, ADAM_IN_ROWS)]
        go_ref[...] = g
        d_ref[...], nm_ref[...], nv_ref[...] = _adamw_math(w_ref[...], g, m_ref[...], v_ref[...])

    blk = pl.BlockSpec((ADAM_IN_ROWS, 1, D), lambda i, q: (i, 0, 0))
    hbm = pl.BlockSpec(memory_space=pl.ANY)
    return _call_indexed(
        order, body, (q_arr,), (w, gwin, gfa, m, v), (ADAM_IN_STEPS,), [blk, hbm, hbm, blk, blk], [blk] * 4,
        scratch_shapes=[pltpu.VMEM((ADAM_IN_BUF, 1, D), f32), pltpu.SemaphoreType.DMA((ADAM_IN_CHUNKS + 2,))],
        name="adamw_w_in", out_shape=[jax.ShapeDtypeStruct((SHARD_IN, 1, D), f32)] * 4,
        compiler_params=_params(("arbitrary",)),
    )


def kernel(x, norm_attn_g, w_in, b_forget, w_branch_a, w_branch_b, w_out, norm_mlp_g, w_up, w_down, norm_final_g, loss_target, m_norm_attn_g, m_w_in, m_b_forget, m_w_branch_a, m_w_branch_b, m_w_out, m_norm_mlp_g, m_w_up, m_w_down, m_norm_final_g, v_norm_attn_g, v_w_in, v_b_forget, v_w_branch_a, v_w_branch_b, v_w_out, v_norm_mlp_g, v_w_up, v_w_down, v_norm_final_g):
    xi, yi, ci = lax.axis_index("x"), lax.axis_index("y"), lax.axis_index("c")
    q_me = 2 * xi + yi
    c_arr = jnp.reshape(ci, (1,)).astype(jnp.int32)
    q_arr = jnp.reshape(q_me, (1,)).astype(jnp.int32)
    x_, tgt = x[0], loss_target[0]

    names = ["w_branch_a", "w_branch_b", "w_out", "w_up", "w_down"]
    big = dict(zip(names, [w_branch_a[0], w_branch_b[0], w_out[0], w_up[0], w_down[0]]))
    ms = dict(zip(names, [m_w_branch_a[0], m_w_branch_b[0], m_w_out[0], m_w_up[0], m_w_down[0]]))
    vs = dict(zip(names, [v_w_branch_a[0], v_w_branch_b[0], v_w_out[0], v_w_up[0], v_w_down[0]]))
    grad, upd = {}, {}
    order = _Order()

    def run(fn, *args, **kw):
        return fn(order, *args, **kw)

    def own_slot(a):
        return lax.dynamic_update_slice(lax.empty((NCHIP,) + a.shape, a.dtype), a[None], (q_me, 0, 0))

    sem_in, in_s = _allgather_start("allgather_start_in", run(_window_stacks, _feature_rows(w_in), q_arr), order,
                                    relay=True)
    rope = _rope_tables(order.tok[0, 0])
    sem_f, in_s = _allgather_forward("allgather_forward_in", in_s, sem_in, order, behind=rope, relay=True)
    sem_rest, rest = _allgather_start("allgather_start_rest", [own_slot(w.astype(bf16)) for w in big.values()], order)
    sem_f, in_s = _allgather_finish("allgather_finish_in", in_s, sem_f, order, relay=True)
    wins, fas = _allgather_finish_far("allgather_finish_far_in", in_s, sem_f, order)
    wt = run(_assemble_win, wins, fas)

    bpad = jnp.pad(b_forget, ((0, 0), (0, 120)))
    h1, qkvb, qkva, gates, fa = run(_norm_inproj, x_, norm_attn_g, wt, rope)
    F = run(_forget_cumsum, fa, bpad)
    oa, lsea = run(_fox_fwd, qkva, F)
    sem_f, rest = _allgather_forward("allgather_forward_rest", rest, sem_rest, order)
    ob, lseb = run(_dil_fwd, qkvb)
    was, wbs, wouts, wups, wdowns = _allgather_finish("allgather_finish_rest", rest, sem_f, order)
    wout = wouts.reshape(D, D)
    wdown = wdowns.reshape(DFF, D)
    ya, yb, mixed = run(_branch_mix, oa, ob, was, wbs, gates)
    x2, h2 = run(_outproj_norm, mixed, wout, x_, norm_mlp_g)
    u, a = run(_mlp_up, h2, wups)
    dx3, dx3b, dg3, loss_part = run(_mlp_down_loss, a, wdown, x2, norm_final_g.reshape(1, D), tgt)

    def comm(name, *parts):
        return _comm_multi(name, list(parts), order)

    def adamw_group(group, fulls, name):
        res = run(_adamw, [big[nm] for nm in group], fulls, [ms[nm] for nm in group], [vs[nm] for nm in group], name)
        for nm, r in zip(group, res):
            *upd[nm], grad[nm] = r

    grp_a, grp_b, grp_c = ["w_down", "w_up"], ["w_out", "w_branch_a", "w_branch_b"], ["w_in", "w_in_fa"]
    du = run(_mlp_down_bwd, dx3b, wdown, u)
    dwdown = run(_mm, a, dx3b, "tn", f32, 1024, D, "wgrad_down")
    dwup = run(_mm, h2, du, "tn", f32, D, 1024, "wgrad_up", stack_cols=True)
    ((sem_pa, buf_pa),) = comm("pair_start_a", _pair_start_part([dwdown.reshape(NCHIP, DFF // NCHIP, D), dwup]))
    dx2, dx2b, dg2 = run(_mlp_up_bwd, du, wups, x2, dx3, norm_mlp_g)
    ((gs, ts),) = comm("pair_wait_a", _pair_wait_part(buf_pa, sem_pa))
    p32_a, p16_a = run(_pair_add, gs, ts, q_arr, c_arr, "pair_add_a")
    ((sem_sa, buf_sa),) = comm("shard_start_a", _shard_start_part(p16_a))
    dya, dyb, dproj = run(_gate_bwd, dx2b, wout, gates, ya, yb)
    dwout = run(_mm, mixed, dx2b, "tn", f32, D, D, "wgrad_out")
    doa, dob = run(_branch_bwd, dya, dyb, was, wbs)
    dwas, dwbs = run(_branch_wgrad, oa, ob, dya, dyb)
    ((sem_pb, buf_pb),) = comm("pair_start_b", _pair_start_part([dwout.reshape(NCHIP, D // NCHIP, D), dwas, dwbs]))
    dF, dproj = run(_fox_bwd, qkva, doa, oa, lsea, F, dproj)
    (gs, ts), (rs_a, _) = comm("pair_wait_b_shard_wait_a", _pair_wait_part(buf_pb, sem_pb),
                               _shard_wait_part(buf_sa, sem_sa, len(grp_a)))
    p32_b, p16_b = run(_pair_add, gs, ts, q_arr, c_arr, "pair_add_b")
    fulls_a = run(_shard_sum, p32_a, rs_a, c_arr, "shard_sum_a")
    (sem_wa, fulls_a), (sem_sb, buf_sb) = comm("swap_start_a_shard_start_b", _swap_start_part(fulls_a),
                                               _shard_start_part(p16_b))
    dbf, dproj = run(_forget_bwd, dF, fa, bpad, dproj)
    dproj = run(_dil_bwd, qkvb, dob, ob, lseb, rope, dproj)
    (rs_b, _), fulls_a = comm("shard_wait_b_swap_wait_a", _shard_wait_part(buf_sb, sem_sb, len(grp_b)),
                              _swap_wait_part(fulls_a, sem_wa))
    fulls_b = run(_shard_sum, p32_b, rs_b, c_arr, "shard_sum_b")
    ((sem_wb, fulls_b),) = comm("swap_start_b", _swap_start_part(fulls_b))
    dwt = run(_mm, dproj, h1, "tn", f32, 512, D, "wgrad_in")
    dwfa = jnp.broadcast_to(dwt[F_FA:F_FA + FA_ROWS][None], (NCHIP, FA_ROWS, D))
    (sem_pc, buf_pc), fulls_b = comm("pair_start_c_swap_wait_b", _pair_start_part([dwt, dwfa], gathered=True),
                                     _swap_wait_part(fulls_b, sem_wb))
    adamw_group(grp_b, fulls_b, "adamw_b")
    (((dwt_c, dwfa_c), (t_in, t_fa)),) = comm("pair_wait_c", _pair_wait_part(buf_pc, sem_pc, gathered=True))
    p32_in, p16_in = run(_pair_add_gathered, dwt_c, t_in, q_arr, c_arr, "pair_add_w_in")
    p32_fa, p16_fa = run(_pair_add, [dwfa_c], [t_fa], q_arr, c_arr, "pair_add_w_in_fa")
    ((sem_sc, buf_sc),) = comm("shard_start_c", _shard_start_part([p16_in, *p16_fa]))
    gx, dg1 = run(_inproj_bwd, dproj, wt, x_, dx2, norm_attn_g)
    adamw_group(grp_a, fulls_a, "adamw_a")
    small = jnp.concatenate([dg1, dg2, dg3, jnp.pad(dbf[:, 0:8], ((0, 0), (0, D - 8))),
                             jnp.pad(loss_part, ((0, 0), (0, D - 128))),
                             jnp.zeros((SMALL_ROWS - 5, D), f32)], axis=0)
    sm = lax.dynamic_update_slice(lax.empty((8, SMALL_ROWS, D), f32), small[None],
                                  (4 * xi + 2 * yi + ci, 0, 0))
    (sem_sm, buf_sm), (rs_c, _) = comm("small_start_shard_wait_c", _shard_start_part([], sm),
                                       _shard_wait_part(buf_sc, sem_sc, len(grp_c)))
    fulls_c = (run(_shard_sum, [p32_in], rs_c[0:1], c_arr, "shard_sum_w_in")
               + run(_shard_sum, p32_fa, rs_c[1:2], c_arr, "shard_sum_w_in_fa"))
    (sem_wc, fulls_c), (_, sm) = comm("swap_start_c_small_wait", _swap_start_part(fulls_c),
                                      _shard_wait_part(buf_sm, sem_sm, 0))
    smalls = ["norm_attn_g", "norm_mlp_g", "norm_final_g", "b_forget"]
    loss, gs, res = run(_small_finish, sm, [norm_attn_g, norm_mlp_g, norm_final_g.reshape(1, D), b_forget],
                        [m_norm_attn_g, m_norm_mlp_g, m_norm_final_g.reshape(1, D), m_b_forget],
                        [v_norm_attn_g, v_norm_mlp_g, v_norm_final_g.reshape(1, D), v_b_forget])
    loss = loss.reshape(())
    grad.update(zip(smalls, gs))
    upd.update(zip(smalls, res))

    ((gwin, gfa),) = comm("swap_wait_c", _swap_wait_part(fulls_c, sem_wc))
    res_in = run(_adamw_w_in, _feature_rows(w_in), gwin, gfa, _feature_rows(m_w_in), _feature_rows(v_w_in), q_arr)
    grad["w_in"] = _unfeature_rows(res_in[0])
    upd["w_in"] = [_unfeature_rows(t) for t in res_in[1:]]

    order_out = ["norm_attn_g", "w_in", "b_forget", "w_branch_a", "w_branch_b", "w_out", "norm_mlp_g", "w_up",
                 "w_down", "norm_final_g"]
    shapes = dict(norm_attn_g=norm_attn_g.shape, w_in=w_in.shape, b_forget=b_forget.shape,
                  w_branch_a=w_branch_a.shape, w_branch_b=w_branch_b.shape, w_out=w_out.shape,
                  norm_mlp_g=norm_mlp_g.shape, w_up=w_up.shape, w_down=w_down.shape, norm_final_g=norm_final_g.shape)
    outs = [loss, gx.reshape(x.shape)]
    outs += [grad[nm].reshape(shapes[nm]) for nm in order_out]
    for k in range(3):
        outs += [upd[nm][k].reshape(shapes[nm]) for nm in order_out]
    return tuple(outs)
```
